```python
import jax, jax.numpy as jnp
from jax import lax
import numpy as np

D_MODEL = 2048
BATCH = 8
SEQ = 2048
DEPTH = 1

HEAD_DIM = 128
N_HEADS = D_MODEL // HEAD_DIM
N_HEADS_B = N_HEADS // 4
N_HEADS_A = N_HEADS - N_HEADS_B
DILATED_PAIRS = ((128, 1), (512, 4), (2048, 16))
N_GROUPS_A = len(DILATED_PAIRS)
HEADS_PER_GROUP_A = N_HEADS_A // N_GROUPS_A
GRID_W = 64
WIN_R = 8
WIN_C = 16
QKV_W = N_HEADS * HEAD_DIM
D_IN = 3 * QKV_W + 2 * D_MODEL
D_A_OUT = HEADS_PER_GROUP_A * HEAD_DIM
D_B_OUT = N_HEADS_B * HEAD_DIM
D_FF = 4 * D_MODEL
ROPE_THETA = 10000.0
EPS = 1e-6
NEG_INF = -1e30

kernel_name = "hybrid_dilated_neighbourhood_gated_encoder"


def rms_norm(x, g):
    x32 = x.astype(jnp.float32)
    y = x32 * lax.rsqrt(jnp.mean(x32 * x32, axis=-1, keepdims=True) + EPS)
    return (y * g.astype(jnp.float32)).astype(x.dtype)


def rope(x, seq_len):
    pos = jnp.arange(seq_len, dtype=jnp.float32)
    inv = ROPE_THETA ** (-jnp.arange(0, HEAD_DIM, 2, dtype=jnp.float32) / HEAD_DIM)
    ang = pos[:, None] * inv[None, :]
    cos = jnp.cos(ang)[None, :, None, :]
    sin = jnp.sin(ang)[None, :, None, :]
    x32 = x.astype(jnp.float32)
    x1, x2 = jnp.split(x32, 2, axis=-1)
    return jnp.concatenate([x1 * cos - x2 * sin, x2 * cos + x1 * sin], axis=-1).astype(x.dtype)


def dilated_window_attention(q, k, v, window, dil):
    B, S, H, E = q.shape
    half = window // (2 * dil)
    blk = half
    M = S // dil
    nb = -(-M // blk)
    Mp = nb * blk

    def to_sub(t):
        return t.reshape(B, M, dil, H, E).transpose(0, 2, 3, 1, 4)

    qs = jnp.pad(to_sub(q), ((0, 0), (0, 0), (0, 0), (0, Mp - M), (0, 0)))
    qs = qs.reshape(B, dil, H, nb, blk, E)
    pad_kv = ((0, 0), (0, 0), (0, 0), (half, Mp - M + half), (0, 0))
    ks = jnp.pad(to_sub(k), pad_kv)
    vs = jnp.pad(to_sub(v), pad_kv)
    kb_len = blk + 2 * half
    idx = (jnp.arange(nb) * blk)[:, None] + jnp.arange(kb_len)[None, :]
    kb = ks[:, :, :, idx]
    vb = vs[:, :, :, idx]
    kpos = idx - half
    qpos = (jnp.arange(nb) * blk)[:, None] + jnp.arange(blk)[None, :]
    valid = (kpos >= 0) & (kpos < M)
    mask = (jnp.abs(kpos[:, None, :] - qpos[:, :, None]) <= half) & valid[:, None, :]
    s = jnp.einsum('bdhnqe,bdhnke->bdhnqk', qs, kb,
                   preferred_element_type=jnp.float32) * (E ** -0.5)
    s = jnp.where(mask, s, NEG_INF)
    lse = jax.nn.logsumexp(s, axis=-1)
    p = jnp.exp(s - lse[..., None]).astype(v.dtype)
    o = jnp.einsum('bdhnqk,bdhnke->bdhnqe', p, vb)
    o = o.reshape(B, dil, H, Mp, E)[:, :, :, :M].transpose(0, 3, 1, 2, 4).reshape(B, S, H, E)
    lse = lse.reshape(B, dil, H, Mp)[..., :M].transpose(0, 3, 1, 2).reshape(B, S, H)
    return o, lse


def neighbourhood_attention(q, k, v, rpb):
    B, S, H, E = q.shape
    rows = S // GRID_W
    kr = min(WIN_R, rows)
    kc = WIN_C

    def to_grid(t):
        return t.reshape(B, rows, GRID_W, H, E).transpose(0, 3, 1, 2, 4)

    qg, kg, vg = to_grid(q), to_grid(k), to_grid(v)
    r = jnp.arange(rows)
    row_idx = jnp.clip(r - kr // 2, 0, rows - kr)[:, None] + jnp.arange(kr)[None, :]
    kb = kg[:, :, row_idx]
    vb = vg[:, :, row_idx]
    c = jnp.arange(GRID_W)
    col_start = jnp.clip(c - kc // 2, 0, GRID_W - kc)
    col_mask = (c[None, :] >= col_start[:, None]) & (c[None, :] < col_start[:, None] + kc)
    dr = row_idx - r[:, None] + (WIN_R - 1)
    dc = jnp.clip(c[None, :] - c[:, None], -(kc - 1), kc - 1) + (WIN_C - 1)
    bias = rpb[:, dr[:, None, :, None], dc[None, :, None, :]]
    s = jnp.einsum('bhrqe,bhrjke->bhrqjk', qg, kb,
                   preferred_element_type=jnp.float32) * (E ** -0.5)
    s = s + bias.astype(jnp.float32)[None]
    s = jnp.where(col_mask[:, None, :], s, NEG_INF)
    p = jax.nn.softmax(s.reshape(B, H, rows, GRID_W, kr * GRID_W), axis=-1)
    p = p.reshape(s.shape).astype(v.dtype)
    o = jnp.einsum('bhrqjk,bhrjke->bhrqe', p, vb)
    return o.transpose(0, 2, 3, 1, 4).reshape(B, S, H, E)


def _fwd_setup_inputs(seed: int = 0) -> dict:
    key = jax.random.key(seed)
    ks = jax.random.split(key, 16)
    f32 = jnp.float32

    def nrm(k, shape, scale):
        return jax.random.normal(k, shape, f32) * scale

    return {
        "x": nrm(ks[0], (BATCH, SEQ, D_MODEL), 1.0),
        "norm_mix": 1.0 + nrm(ks[1], (DEPTH, D_MODEL), 0.05),
        "w_in": nrm(ks[2], (DEPTH, D_MODEL, D_IN), D_MODEL ** -0.5),
        "b_gate": nrm(ks[3], (DEPTH, 2 * D_MODEL), 0.1),
        "q_norm_a": 1.0 + nrm(ks[4], (DEPTH, HEAD_DIM), 0.05),
        "k_norm_a": 1.0 + nrm(ks[5], (DEPTH, HEAD_DIM), 0.05),
        "q_norm_b": 1.0 + nrm(ks[6], (DEPTH, HEAD_DIM), 0.05),
        "k_norm_b": 1.0 + nrm(ks[7], (DEPTH, HEAD_DIM), 0.05),
        "rpb_b": nrm(ks[8], (DEPTH, N_HEADS_B, 2 * WIN_R - 1, 2 * WIN_C - 1), 0.1),
        "w_proj_a": nrm(ks[9], (DEPTH, D_A_OUT, D_MODEL), D_A_OUT ** -0.5),
        "w_proj_b": nrm(ks[10], (DEPTH, D_B_OUT, D_MODEL), D_B_OUT ** -0.5),
        "w_out": nrm(ks[11], (DEPTH, D_MODEL, D_MODEL), D_MODEL ** -0.5),
        "norm_ffn": 1.0 + nrm(ks[12], (DEPTH, D_MODEL), 0.05),
        "w_up": nrm(ks[13], (DEPTH, D_MODEL, D_FF), D_MODEL ** -0.5),
        "w_down": nrm(ks[14], (DEPTH, D_FF, D_MODEL), D_FF ** -0.5),
    }


def _fwd_reference(x, norm_mix, w_in, b_gate, q_norm_a, k_norm_a, q_norm_b, k_norm_b, rpb_b,
              w_proj_a, w_proj_b, w_out, norm_ffn, w_up, w_down):
    B, S, _ = x.shape
    h = x
    for l in range(DEPTH):
        xn = rms_norm(h, norm_mix[l])
        proj = xn @ w_in[l]
        q, k, v, gate = jnp.split(proj, [QKV_W, 2 * QKV_W, 3 * QKV_W], axis=-1)
        q = q.reshape(B, S, N_HEADS, HEAD_DIM)
        k = k.reshape(B, S, N_HEADS, HEAD_DIM)
        v = v.reshape(B, S, N_HEADS, HEAD_DIM)

        qa = rope(rms_norm(q[:, :, :N_HEADS_A], q_norm_a[l]), S)
        ka = rope(rms_norm(k[:, :, :N_HEADS_A], k_norm_a[l]), S)
        va = v[:, :, :N_HEADS_A]
        outs, lses = [], []
        for g, (win, dil) in enumerate(DILATED_PAIRS):
            sl = slice(g * HEADS_PER_GROUP_A, (g + 1) * HEADS_PER_GROUP_A)
            o_g, lse_g = dilated_window_attention(qa[:, :, sl], ka[:, :, sl], va[:, :, sl], win, dil)
            outs.append(o_g)
            lses.append(lse_g)
        wts = jax.nn.softmax(jnp.stack(lses, axis=0), axis=0)
        oa = jnp.einsum('gbsh,gbshe->bshe', wts,
                        jnp.stack(outs, axis=0).astype(jnp.float32)).astype(x.dtype)

        qb = rms_norm(q[:, :, N_HEADS_A:], q_norm_b[l])
        kb = rms_norm(k[:, :, N_HEADS_A:], k_norm_b[l])
        ob = neighbourhood_attention(qb, kb, v[:, :, N_HEADS_A:], rpb_b[l])

        ya = oa.reshape(B, S, D_A_OUT) @ w_proj_a[l]
        yb = ob.reshape(B, S, D_B_OUT) @ w_proj_b[l]
        ga, gb = jnp.split(jax.nn.sigmoid((gate + b_gate[l]).astype(jnp.float32)), 2, axis=-1)
        mixed = (ga * ya + gb * yb).astype(x.dtype)
        h = h + mixed @ w_out[l]

        hn = rms_norm(h, norm_ffn[l])
        u = jax.nn.relu(hn @ w_up[l])
        h = h + (u * u) @ w_down[l]
    return h


import jax as _jax
import jax.numpy as _jnp

TWIN_FORMAT = 'train_step'
FWD_PARAMS = ['x', 'norm_mix', 'w_in', 'b_gate', 'q_norm_a', 'k_norm_a', 'q_norm_b', 'k_norm_b', 'rpb_b', 'w_proj_a', 'w_proj_b', 'w_out', 'norm_ffn', 'w_up', 'w_down']
TWIN_WEIGHTS = ['norm_mix', 'w_in', 'b_gate', 'q_norm_a', 'k_norm_a', 'q_norm_b', 'k_norm_b', 'rpb_b', 'w_proj_a', 'w_proj_b', 'w_out', 'norm_ffn', 'w_up', 'w_down']
TWIN_DIFF_INPUT = 'x'
TWIN_INPUTS = ['x', 'norm_mix', 'w_in', 'b_gate', 'q_norm_a', 'k_norm_a', 'q_norm_b', 'k_norm_b', 'rpb_b', 'w_proj_a', 'w_proj_b', 'w_out', 'norm_ffn', 'w_up', 'w_down', 'loss_target', 'm_norm_mix', 'm_w_in', 'm_b_gate', 'm_q_norm_a', 'm_k_norm_a', 'm_q_norm_b', 'm_k_norm_b', 'm_rpb_b', 'm_w_proj_a', 'm_w_proj_b', 'm_w_out', 'm_norm_ffn', 'm_w_up', 'm_w_down', 'v_norm_mix', 'v_w_in', 'v_b_gate', 'v_q_norm_a', 'v_k_norm_a', 'v_q_norm_b', 'v_k_norm_b', 'v_rpb_b', 'v_w_proj_a', 'v_w_proj_b', 'v_w_out', 'v_norm_ffn', 'v_w_up', 'v_w_down']
TWIN_OUTPUTS = ['loss', 'grad_x', 'grad_norm_mix', 'grad_w_in', 'grad_b_gate', 'grad_q_norm_a', 'grad_k_norm_a', 'grad_q_norm_b', 'grad_k_norm_b', 'grad_rpb_b', 'grad_w_proj_a', 'grad_w_proj_b', 'grad_w_out', 'grad_norm_ffn', 'grad_w_up', 'grad_w_down', 'delta_norm_mix', 'delta_w_in', 'delta_b_gate', 'delta_q_norm_a', 'delta_k_norm_a', 'delta_q_norm_b', 'delta_k_norm_b', 'delta_rpb_b', 'delta_w_proj_a', 'delta_w_proj_b', 'delta_w_out', 'delta_norm_ffn', 'delta_w_up', 'delta_w_down', 'new_m_norm_mix', 'new_m_w_in', 'new_m_b_gate', 'new_m_q_norm_a', 'new_m_k_norm_a', 'new_m_q_norm_b', 'new_m_k_norm_b', 'new_m_rpb_b', 'new_m_w_proj_a', 'new_m_w_proj_b', 'new_m_w_out', 'new_m_norm_ffn', 'new_m_w_up', 'new_m_w_down', 'new_v_norm_mix', 'new_v_w_in', 'new_v_b_gate', 'new_v_q_norm_a', 'new_v_k_norm_a', 'new_v_q_norm_b', 'new_v_k_norm_b', 'new_v_rpb_b', 'new_v_w_proj_a', 'new_v_w_proj_b', 'new_v_w_out', 'new_v_norm_ffn', 'new_v_w_up', 'new_v_w_down']
TWIN_LEAF_KINDS = {'loss': 'loss', 'grad_x': 'grad_x', 'grad_norm_mix': 'grad_w', 'grad_w_in': 'grad_w', 'grad_b_gate': 'grad_w', 'grad_q_norm_a': 'grad_w', 'grad_k_norm_a': 'grad_w', 'grad_q_norm_b': 'grad_w', 'grad_k_norm_b': 'grad_w', 'grad_rpb_b': 'grad_w', 'grad_w_proj_a': 'grad_w', 'grad_w_proj_b': 'grad_w', 'grad_w_out': 'grad_w', 'grad_norm_ffn': 'grad_w', 'grad_w_up': 'grad_w', 'grad_w_down': 'grad_w', 'delta_norm_mix': 'delta_w', 'delta_w_in': 'delta_w', 'delta_b_gate': 'delta_w', 'delta_q_norm_a': 'delta_w', 'delta_k_norm_a': 'delta_w', 'delta_q_norm_b': 'delta_w', 'delta_k_norm_b': 'delta_w', 'delta_rpb_b': 'delta_w', 'delta_w_proj_a': 'delta_w', 'delta_w_proj_b': 'delta_w', 'delta_w_out': 'delta_w', 'delta_norm_ffn': 'delta_w', 'delta_w_up': 'delta_w', 'delta_w_down': 'delta_w', 'new_m_norm_mix': 'new_m', 'new_m_w_in': 'new_m', 'new_m_b_gate': 'new_m', 'new_m_q_norm_a': 'new_m', 'new_m_k_norm_a': 'new_m', 'new_m_q_norm_b': 'new_m', 'new_m_k_norm_b': 'new_m', 'new_m_rpb_b': 'new_m', 'new_m_w_proj_a': 'new_m', 'new_m_w_proj_b': 'new_m', 'new_m_w_out': 'new_m', 'new_m_norm_ffn': 'new_m', 'new_m_w_up': 'new_m', 'new_m_w_down': 'new_m', 'new_v_norm_mix': 'new_v', 'new_v_w_in': 'new_v', 'new_v_b_gate': 'new_v', 'new_v_q_norm_a': 'new_v', 'new_v_k_norm_a': 'new_v', 'new_v_q_norm_b': 'new_v', 'new_v_k_norm_b': 'new_v', 'new_v_rpb_b': 'new_v', 'new_v_w_proj_a': 'new_v', 'new_v_w_proj_b': 'new_v', 'new_v_w_out': 'new_v', 'new_v_norm_ffn': 'new_v', 'new_v_w_up': 'new_v', 'new_v_w_down': 'new_v'}


def _forward(args):
    return _fwd_reference(*[args[k] for k in FWD_PARAMS])


def _output_shape():
    out = _jax.eval_shape(lambda: _forward(_fwd_setup_inputs(0)))
    return out.shape, out.dtype

N_MICROBATCH = 1
ADAM_LR = 0.001
ADAM_B1 = 0.9
ADAM_B2 = 0.999
ADAM_EPS = 1e-08
ADAM_WD = 0.01
ADAM_STEP = 10
PER_EXAMPLE_BATCH_AXIS = {'x': 0, 'loss_target': 0}
SHARED_INPUTS = []
_WEIGHT_DTYPES = {'norm_mix': _jnp.float32, 'w_in': _jnp.float32, 'b_gate': _jnp.float32, 'q_norm_a': _jnp.float32, 'k_norm_a': _jnp.float32, 'q_norm_b': _jnp.float32, 'k_norm_b': _jnp.float32, 'rpb_b': _jnp.float32, 'w_proj_a': _jnp.float32, 'w_proj_b': _jnp.float32, 'w_out': _jnp.float32, 'norm_ffn': _jnp.float32, 'w_up': _jnp.float32, 'w_down': _jnp.float32}
MOMENT_SCALE = {'norm_mix': 7.706288e-02, 'w_in': 3.114870e-02, 'b_gate': 1.596984e-02, 'q_norm_a': 2.364007e-01, 'k_norm_a': 2.371987e-01, 'q_norm_b': 3.911628e-01, 'k_norm_b': 3.934270e-01, 'rpb_b': 2.960772e-02, 'w_proj_a': 2.668050e-02, 'w_proj_b': 4.356080e-02, 'w_out': 5.176039e-02, 'norm_ffn': 2.418147e+01, 'w_up': 1.702760e-01, 'w_down': 1.995323e+00}


def _to_microbatches(a, axis):
    t = _jnp.moveaxis(a, axis, 0)
    t = t.reshape((N_MICROBATCH, t.shape[0] // N_MICROBATCH) + t.shape[1:])
    return _jnp.moveaxis(t, 1, axis + 1)


def setup_inputs(seed: int = 0) -> dict:
    inp = _fwd_setup_inputs(seed)
    key = _jax.random.fold_in(_jax.random.key(seed), 7919)
    shape, _ = _output_shape()
    out = dict(inp)
    out["loss_target"] = _jax.random.normal(_jax.random.fold_in(key, 0), shape, _jnp.float32)
    for i, name in enumerate(TWIN_WEIGHTS):
        w = inp[name].astype(_jnp.float32)
        if MOMENT_SCALE is None:
            s = _jnp.sqrt(_jnp.mean(_jnp.square(w)) + 1e-30)
        else:
            s = MOMENT_SCALE[name]
        km, kv = _jax.random.split(_jax.random.fold_in(key, i + 1))
        out[name] = w
        out["m_" + name] = s * _jax.random.normal(km, w.shape, _jnp.float32)
        out["v_" + name] = (s * s) * _jax.random.uniform(kv, w.shape, _jnp.float32, 0.5, 1.5)
    if N_MICROBATCH > 1:
        for name, axis in PER_EXAMPLE_BATCH_AXIS.items():
            out[name] = _to_microbatches(out[name], axis)
    return {'x': out['x'], 'norm_mix': out['norm_mix'], 'w_in': out['w_in'], 'b_gate': out['b_gate'], 'q_norm_a': out['q_norm_a'], 'k_norm_a': out['k_norm_a'], 'q_norm_b': out['q_norm_b'], 'k_norm_b': out['k_norm_b'], 'rpb_b': out['rpb_b'], 'w_proj_a': out['w_proj_a'], 'w_proj_b': out['w_proj_b'], 'w_out': out['w_out'], 'norm_ffn': out['norm_ffn'], 'w_up': out['w_up'], 'w_down': out['w_down'], 'loss_target': out['loss_target'], 'm_norm_mix': out['m_norm_mix'], 'm_w_in': out['m_w_in'], 'm_b_gate': out['m_b_gate'], 'm_q_norm_a': out['m_q_norm_a'], 'm_k_norm_a': out['m_k_norm_a'], 'm_q_norm_b': out['m_q_norm_b'], 'm_k_norm_b': out['m_k_norm_b'], 'm_rpb_b': out['m_rpb_b'], 'm_w_proj_a': out['m_w_proj_a'], 'm_w_proj_b': out['m_w_proj_b'], 'm_w_out': out['m_w_out'], 'm_norm_ffn': out['m_norm_ffn'], 'm_w_up': out['m_w_up'], 'm_w_down': out['m_w_down'], 'v_norm_mix': out['v_norm_mix'], 'v_w_in': out['v_w_in'], 'v_b_gate': out['v_b_gate'], 'v_q_norm_a': out['v_q_norm_a'], 'v_k_norm_a': out['v_k_norm_a'], 'v_q_norm_b': out['v_q_norm_b'], 'v_k_norm_b': out['v_k_norm_b'], 'v_rpb_b': out['v_rpb_b'], 'v_w_proj_a': out['v_w_proj_a'], 'v_w_proj_b': out['v_w_proj_b'], 'v_w_out': out['v_w_out'], 'v_norm_ffn': out['v_norm_ffn'], 'v_w_up': out['v_w_up'], 'v_w_down': out['v_w_down']}


def _loss(weights, diff, rest, loss_target):
    with _jax.named_scope("forward"):
        args = {**rest, TWIN_DIFF_INPUT: diff, **{k: w.astype(_WEIGHT_DTYPES[k]) for k, w in weights.items()}}
        y = _forward(args)
    with _jax.named_scope("loss_head"):
        err = _jnp.square(y.astype(_jnp.float32) - loss_target)
        return 0.5 * _jnp.sum(_jnp.mean(err, axis=-1)) if err.ndim else 0.5 * err


def _adamw(w, g, m, v):
    m = ADAM_B1 * m + (1.0 - ADAM_B1) * g
    v = ADAM_B2 * v + (1.0 - ADAM_B2) * _jnp.square(g)
    m_hat = m / (1.0 - ADAM_B1 ** ADAM_STEP)
    v_hat = v / (1.0 - ADAM_B2 ** ADAM_STEP)
    delta = -ADAM_LR * (m_hat / (_jnp.sqrt(v_hat) + ADAM_EPS) + ADAM_WD * w)
    return delta, m, v


def reference(x, norm_mix, w_in, b_gate, q_norm_a, k_norm_a, q_norm_b, k_norm_b, rpb_b, w_proj_a, w_proj_b, w_out, norm_ffn, w_up, w_down, loss_target, m_norm_mix, m_w_in, m_b_gate, m_q_norm_a, m_k_norm_a, m_q_norm_b, m_k_norm_b, m_rpb_b, m_w_proj_a, m_w_proj_b, m_w_out, m_norm_ffn, m_w_up, m_w_down, v_norm_mix, v_w_in, v_b_gate, v_q_norm_a, v_k_norm_a, v_q_norm_b, v_k_norm_b, v_rpb_b, v_w_proj_a, v_w_proj_b, v_w_out, v_norm_ffn, v_w_up, v_w_down):
    given = dict(x=x, norm_mix=norm_mix, w_in=w_in, b_gate=b_gate, q_norm_a=q_norm_a, k_norm_a=k_norm_a, q_norm_b=q_norm_b, k_norm_b=k_norm_b, rpb_b=rpb_b, w_proj_a=w_proj_a, w_proj_b=w_proj_b, w_out=w_out, norm_ffn=norm_ffn, w_up=w_up, w_down=w_down, loss_target=loss_target, m_norm_mix=m_norm_mix, m_w_in=m_w_in, m_b_gate=m_b_gate, m_q_norm_a=m_q_norm_a, m_k_norm_a=m_k_norm_a, m_q_norm_b=m_q_norm_b, m_k_norm_b=m_k_norm_b, m_rpb_b=m_rpb_b, m_w_proj_a=m_w_proj_a, m_w_proj_b=m_w_proj_b, m_w_out=m_w_out, m_norm_ffn=m_norm_ffn, m_w_up=m_w_up, m_w_down=m_w_down, v_norm_mix=v_norm_mix, v_w_in=v_w_in, v_b_gate=v_b_gate, v_q_norm_a=v_q_norm_a, v_k_norm_a=v_k_norm_a, v_q_norm_b=v_q_norm_b, v_k_norm_b=v_k_norm_b, v_rpb_b=v_rpb_b, v_w_proj_a=v_w_proj_a, v_w_proj_b=v_w_proj_b, v_w_out=v_w_out, v_norm_ffn=v_norm_ffn, v_w_up=v_w_up, v_w_down=v_w_down)
    weights = {n: given[n] for n in TWIN_WEIGHTS}
    shared = {n: given[n] for n in SHARED_INPUTS}
    per_example = {n: given[n] for n in ['x']}
    grad_fn = _jax.value_and_grad(_loss, argnums=(0, 1))

    def one_microbatch(ex, loss_target):
        ex = dict(ex)
        diff = ex.pop(TWIN_DIFF_INPUT)
        return grad_fn(weights, diff, {**shared, **ex}, loss_target)

    if N_MICROBATCH == 1:
        loss, (grad_w, grad_x) = one_microbatch(per_example, given["loss_target"])
    else:
        def body(carry, xs):
            loss_sum, grad_sum = carry
            l_k, (gw_k, gx_k) = one_microbatch(xs[0], xs[1])
            with _jax.named_scope("update"):
                return (loss_sum + l_k, _jax.tree.map(_jnp.add, grad_sum, gw_k)), gx_k

        init = (_jnp.zeros((), _jnp.float32), _jax.tree.map(_jnp.zeros_like, weights))
        (loss, grad_w), grad_x = _jax.lax.scan(body, init, (per_example, given["loss_target"]))
    with _jax.named_scope("update"):
        delta_w, new_m, new_v = {}, {}, {}
        for n in TWIN_WEIGHTS:
            delta_w[n], new_m[n], new_v[n] = _adamw(weights[n], grad_w[n], given["m_" + n], given["v_" + n])
    return (loss, grad_x, *[grad_w[n] for n in TWIN_WEIGHTS], *[delta_w[n] for n in TWIN_WEIGHTS],
            *[new_m[n] for n in TWIN_WEIGHTS], *[new_v[n] for n in TWIN_WEIGHTS])
```

```python
import functools

import numpy as np
import jax
import jax.numpy as jnp
from jax import lax
from jax.experimental import pallas as pl
from jax.experimental.pallas import tpu as pltpu

F32, BF16 = jnp.float32, jnp.bfloat16
SDS = jax.ShapeDtypeStruct
MESH = pl.DeviceIdType.MESH

T = 2048
D = 2048
HD = 128
NH, NHA = 16, 12
DIN = 10240
DFF = 8192
NSH = 4
DILS = (1, 4, 16)
EPS = 1e-6
NEG = -1e30
SCALE = HD ** -0.5
GRID_W, WIN_R, WIN_C = 64, 8, 16
NRPB = 15 * 31
VMEM_LIMIT = 56 * 1024 * 1024
B1, B2, LR, AEPS, WD, STEP = 0.9, 0.999, 0.001, 1e-08, 0.01, 10
SMALL_ROWS = 88


def _dot(a, b):
    return jnp.dot(a, b, preferred_element_type=F32)


def _dot_nt(a, b):
    return lax.dot_general(a, b, (((1,), (1,)), ((), ())), preferred_element_type=F32)


def _dot_tn(a, b):
    return lax.dot_general(a, b, (((0,), (0,)), ((), ())), preferred_element_type=F32)


def _params(n):
    return pltpu.CompilerParams(dimension_semantics=("arbitrary",) * n, vmem_limit_bytes=VMEM_LIMIT)


def _resident(shape, index_map):
    return pl.BlockSpec(shape, index_map, pipeline_mode=pl.Buffered(1))


def _sigmoid(z):
    return 1.0 / (1.0 + jnp.exp(-z))


def _wide(v, n):
    return jnp.concatenate([v] * n, axis=1)


def _row_tile(rows, cols, elems):
    tr = 16
    while tr * 2 <= rows and tr * 2 * cols <= elems:
        tr *= 2
    return tr


def _place():
    x, y, c = lax.axis_index("x"), lax.axis_index("y"), lax.axis_index("c")
    peers = [(1 - x, y), (x, 1 - y), (1 - x, 1 - y)]
    return x, y, c, peers


def _cast_bf16(w, name):
    rows, cols = w.shape
    tr = min(rows, 256)

    def body(w_ref, o_ref):
        o_ref[...] = w_ref[...].astype(BF16)

    return pl.pallas_call(
        body, name=name, out_shape=SDS((rows, cols), BF16), grid=(rows // tr,),
        in_specs=[pl.BlockSpec((tr, cols), lambda i: (i, 0))],
        out_specs=pl.BlockSpec((tr, cols), lambda i: (i, 0)), compiler_params=_params(1))(w)


def _allgather_weights(shards):
    n = len(shards)

    def body(*refs):
        ins, outs = refs[:n], refs[n:2 * n]
        send_sems, recv_sems, local_sems = refs[2 * n:]
        x, y, c, peers = _place()
        me = 2 * x + y
        local = [pltpu.make_async_copy(ins[i], outs[i].at[me], local_sems.at[i]) for i in range(n)]
        for cp in local:
            cp.start()
        sends = []
        for i in range(n):
            for k, (px, py) in enumerate(peers):
                cp = pltpu.make_async_remote_copy(
                    src_ref=ins[i], dst_ref=outs[i].at[me], send_sem=send_sems.at[3 * i + k],
                    recv_sem=recv_sems.at[3 * i + k], device_id=(px, py, c), device_id_type=MESH)
                cp.start()
                sends.append(cp)
        for i in range(n):
            for k, (px, py) in enumerate(peers):
                pltpu.make_async_remote_copy(
                    src_ref=ins[i], dst_ref=outs[i].at[2 * px + py], send_sem=send_sems.at[3 * i + k],
                    recv_sem=recv_sems.at[3 * i + k], device_id=(px, py, c), device_id_type=MESH).wait_recv()
        for cp in sends:
            cp.wait_send()
        for cp in local:
            cp.wait()

    any_spec = pl.BlockSpec(memory_space=pl.ANY)
    return pl.pallas_call(
        body, name="allgather_weights",
        out_shape=[SDS((NSH,) + s.shape, s.dtype) for s in shards],
        in_specs=[any_spec] * n, out_specs=[any_spec] * n,
        scratch_shapes=[pltpu.SemaphoreType.DMA((3 * n,)), pltpu.SemaphoreType.DMA((3 * n,)),
                        pltpu.SemaphoreType.DMA((n,))])(*shards)


def _scatter_grads(grads):
    n = len(grads)

    def body(*refs):
        ins, outs = refs[:n], refs[n:2 * n]
        send_sems, recv_sems, local_sems = refs[2 * n:]
        x, y, c, peers = _place()
        me = 2 * x + y
        local = [pltpu.make_async_copy(ins[i].at[me], outs[i].at[3], local_sems.at[i]) for i in range(n)]
        for cp in local:
            cp.start()
        sends = []
        for i in range(n):
            for k, (px, py) in enumerate(peers):
                cp = pltpu.make_async_remote_copy(
                    src_ref=ins[i].at[2 * px + py], dst_ref=outs[i].at[k], send_sem=send_sems.at[3 * i + k],
                    recv_sem=recv_sems.at[3 * i + k], device_id=(px, py, c), device_id_type=MESH)
                cp.start()
                sends.append(cp)
        for cp in sends:
            cp.wait_recv()
        for cp in sends:
            cp.wait_send()
        for cp in local:
            cp.wait()

    any_spec = pl.BlockSpec(memory_space=pl.ANY)
    return pl.pallas_call(
        body, name="scatter_grads",
        out_shape=[SDS(g.shape, g.dtype) for g in grads],
        in_specs=[any_spec] * n, out_specs=[any_spec] * n,
        scratch_shapes=[pltpu.SemaphoreType.DMA((3 * n,)), pltpu.SemaphoreType.DMA((3 * n,)),
                        pltpu.SemaphoreType.DMA((n,))])(*grads)


def _swap_with_sibling(parts):
    n = len(parts)

    def body(*refs):
        ins, outs = refs[:n], refs[n:2 * n]
        send_sems, recv_sems = refs[2 * n:]
        x, y, c, _ = _place()
        copies = []
        for i in range(n):
            cp = pltpu.make_async_remote_copy(
                src_ref=ins[i], dst_ref=outs[i], send_sem=send_sems.at[i], recv_sem=recv_sems.at[i],
                device_id=(x, y, 1 - c), device_id_type=MESH)
            cp.start()
            copies.append(cp)
        for cp in copies:
            cp.wait_recv()
        for cp in copies:
            cp.wait_send()

    any_spec = pl.BlockSpec(memory_space=pl.ANY)
    return pl.pallas_call(
        body, name="swap_with_sibling",
        out_shape=[SDS(p.shape, p.dtype) for p in parts],
        in_specs=[any_spec] * n, out_specs=[any_spec] * n,
        scratch_shapes=[pltpu.SemaphoreType.DMA((n,)), pltpu.SemaphoreType.DMA((n,))])(*parts)


def _allgather_small(v):
    m_per, n = v.shape

    def body(x_ref, out_ref, send_sems, recv_sems, local_sem):
        x, y, c = lax.axis_index("x"), lax.axis_index("y"), lax.axis_index("c")
        me, sibling = (x, y, c), (x, y, 1 - c)
        chips = [(1 - x, y), (x, 1 - y), (1 - x, 1 - y)]

        def rows(px, py, pc):
            return out_ref.at[pl.ds((4 * px + 2 * py + pc) * m_per, m_per), :]

        def copy(k, block, to, src=None):
            return pltpu.make_async_remote_copy(
                src_ref=rows(*block) if src is None else src, dst_ref=rows(*block),
                send_sem=send_sems.at[k], recv_sem=recv_sems.at[k], device_id=to, device_id_type=MESH)

        mine = pltpu.make_async_copy(x_ref, rows(*me), local_sem)
        mine.start()
        first = [copy(0, me, sibling, src=x_ref)]
        first += [copy(1 + j, me, (*chip, c), src=x_ref) for j, chip in enumerate(chips)]
        for cp in first:
            cp.start()
        passed = [copy(4 + j, (*chip, c), sibling) for j, chip in enumerate(chips)]
        for j, chip in enumerate(chips):
            copy(1 + j, (*chip, c), me).wait_recv()
            passed[j].start()
        copy(0, sibling, me).wait_recv()
        for j, chip in enumerate(chips):
            copy(4 + j, (*chip, 1 - c), me).wait_recv()
        for cp in first + passed:
            cp.wait_send()
        mine.wait()

    return pl.pallas_call(
        body, name="allgather_small", out_shape=SDS((8 * m_per, n), v.dtype),
        in_specs=[pl.BlockSpec(memory_space=pltpu.VMEM)], out_specs=pl.BlockSpec(memory_space=pltpu.VMEM),
        scratch_shapes=[pltpu.SemaphoreType.DMA((7,)), pltpu.SemaphoreType.DMA((7,)), pltpu.SemaphoreType.DMA])(v)


def _norm_in_proj(x, g, w_full):
    tm, tn = 512, 512
    per = (DIN // NSH) // tn

    def body(x_ref, g_ref, w_ref, proj_ref, xn_ref):
        @pl.when(pl.program_id(1) == 0)
        def _():
            xv = x_ref[...]
            r = lax.rsqrt(jnp.mean(xv * xv, axis=-1, keepdims=True) + EPS)
            xn_ref[...] = (xv * r * g_ref[...]).astype(BF16)

        proj_ref[...] = _dot(xn_ref[...], w_ref[...])

    return pl.pallas_call(
        body, name="norm_in_proj", out_shape=[SDS((T, DIN), F32), SDS((T, D), BF16)],
        grid=(T // tm, DIN // tn),
        in_specs=[pl.BlockSpec((tm, D), lambda i, j: (i, 0)),
                  pl.BlockSpec((1, D), lambda i, j: (0, 0)),
                  pl.BlockSpec((None, D, tn), lambda i, j: (j // per, 0, j % per))],
        out_specs=[pl.BlockSpec((tm, tn), lambda i, j: (i, j)),
                   pl.BlockSpec((tm, D), lambda i, j: (i, 0))],
        compiler_params=_params(2))(x, g, w_full)


def _rope_tables():
    pos = np.arange(T, dtype=np.float32)
    inv = (10000.0 ** (-np.arange(0, HD, 2, dtype=np.float32) / HD)).astype(np.float32)
    ang = (pos[:, None] * inv[None, :]).astype(np.float32)
    cos, sin = np.cos(ang).astype(np.float32), np.sin(ang).astype(np.float32)
    return (jnp.asarray(np.concatenate([cos, cos], axis=1)), jnp.asarray(np.concatenate([-sin, sin], axis=1)))


def _qk_prep(proj, nw, cos, sin):
    tm = 256

    def body(p_ref, w_ref, cos_ref, sin_ref, o_ref):
        cv, sv = cos_ref[...], sin_ref[...]
        for h in range(NH):
            sl = slice(h * HD, (h + 1) * HD)
            xv = p_ref[:, sl]
            r = lax.rsqrt(jnp.mean(xv * xv, axis=-1, keepdims=True) + EPS)
            z = xv * r * w_ref[:, sl]
            if h < NHA:
                z = z * cv + pltpu.roll(z, 64, 1) * sv
            o_ref[:, sl] = z.astype(BF16)

    return pl.pallas_call(
        body, name="qk_prep", out_shape=SDS((T, 2 * D), BF16), grid=(T // tm, 2),
        in_specs=[pl.BlockSpec((tm, D), lambda i, j: (i, j)),
                  pl.BlockSpec((None, 1, D), lambda i, j: (j, 0, 0)),
                  pl.BlockSpec((tm, HD), lambda i, j: (i, 0)),
                  pl.BlockSpec((tm, HD), lambda i, j: (i, 0))],
        out_specs=pl.BlockSpec((tm, D), lambda i, j: (i, j)),
        compiler_params=_params(2))(proj, nw, cos, sin)


def _band_mask(q0, m):
    ii = lax.broadcasted_iota(jnp.int32, (128, 256), 0)
    jj = lax.broadcasted_iota(jnp.int32, (128, 256), 1)
    rel = jj - ii
    kpos = jj + (q0 - 64)
    return (rel >= 0) & (rel <= 128) & (kpos >= 0) & (kpos < m)


def _fill_padded(dst, src, m):
    zeros = jnp.zeros((64, HD), dst.dtype)
    dst[0:64, :] = zeros
    dst[64 + m:128 + m, :] = zeros
    dst[64:64 + m, :] = src.astype(dst.dtype)


def _attn_a_fwd(qkn, proj, g):
    dil = DILS[g]
    m = T // dil
    nb = m // 128

    def body(q_ref, k_ref, v_ref, o_ref, l_ref, kp, vp):
        _fill_padded(kp, k_ref[...], m)
        _fill_padded(vp, v_ref[...], m)

        def block(b, carry):
            q0 = pl.multiple_of(b * 128, 128)
            kw, vw = kp[pl.ds(q0, 256), :], vp[pl.ds(q0, 256), :]
            s = _dot_nt(q_ref[pl.ds(q0, 128), :], kw) * SCALE
            s = jnp.where(_band_mask(q0, m), s, NEG)
            mx = jnp.max(s, axis=-1, keepdims=True)
            e = jnp.exp(s - mx)
            den = jnp.sum(e, axis=-1, keepdims=True)
            o_ref[pl.ds(q0, 128), :] = _dot((e / den).astype(BF16), vw)
            l_ref[pl.ds(q0, 128), :] = jnp.broadcast_to(mx + jnp.log(den), (128, HD))
            return carry

        lax.fori_loop(0, nb, block, 0)

    o, lse = pl.pallas_call(
        body, name=f"attn_a_fwd_{g}", out_shape=[SDS((m, dil * 512), F32)] * 2, grid=(4, dil),
        in_specs=[pl.BlockSpec((m, HD), lambda h, r: (0, r * 32 + g * 4 + h)),
                  pl.BlockSpec((m, HD), lambda h, r: (0, r * 32 + 16 + g * 4 + h)),
                  pl.BlockSpec((m, HD), lambda h, r: (0, r * 80 + 32 + g * 4 + h))],
        out_specs=[pl.BlockSpec((m, HD), lambda h, r: (0, r * 4 + h))] * 2,
        scratch_shapes=[pltpu.VMEM((m + 128, HD), BF16), pltpu.VMEM((m + 128, HD), BF16)],
        compiler_params=_params(2))(qkn.reshape(m, dil * 2 * D), qkn.reshape(m, dil * 2 * D), proj.reshape(m, dil * DIN))
    return o.reshape(T, 512), lse.reshape(T, 512)


def _nbr_window(r):
    start = jnp.clip(r - WIN_R // 2, 0, T // GRID_W - WIN_R)
    return start, start - r + (WIN_R - 1)


def _attn_b_fwd(qkn, proj, rpb_flat):
    def body(rpb_ref, q_ref, k_ref, v_ref, o_ref, l_ref, bias_ref, vb):
        h = pl.program_id(0)
        qc = lax.broadcasted_iota(jnp.int32, (GRID_W, 512), 0)
        lane = lax.broadcasted_iota(jnp.int32, (GRID_W, 512), 1)
        kc = lane & (GRID_W - 1)
        dc = jnp.clip(kc - qc, -(WIN_C - 1), WIN_C - 1) + (WIN_C - 1)
        cs = jnp.clip(qc - WIN_C // 2, 0, GRID_W - WIN_C)
        colmask = (kc >= cs) & (kc < cs + WIN_C)
        jrow = lax.broadcasted_iota(jnp.int32, (1, 512), 1) >> 6
        for off in range(8):
            bias_ref[off] = jnp.zeros((GRID_W, 512), F32)
        for e in range(31):
            sel = dc == e
            for off in range(8):
                v = jnp.zeros((1, 512), F32)
                for j in range(8):
                    v = jnp.where(jrow == j, rpb_ref[h * NRPB + (off + j) * 31 + e], v)
                bias_ref[off] = jnp.where(sel, v, bias_ref[off])
        for off in range(8):
            bias_ref[off] = jnp.where(colmask, bias_ref[off], NEG)
        vb[...] = v_ref[...].astype(BF16)

        def row(r, carry):
            start, off = _nbr_window(r)
            q0 = pl.multiple_of(r * GRID_W, GRID_W)
            k0 = pl.multiple_of(start * GRID_W, GRID_W)
            s = _dot_nt(q_ref[pl.ds(q0, GRID_W), :], k_ref[pl.ds(k0, 512), :]) * SCALE + bias_ref[off]
            mx = jnp.max(s, axis=-1, keepdims=True)
            e = jnp.exp(s - mx)
            den = jnp.sum(e, axis=-1, keepdims=True)
            o_ref[pl.ds(q0, GRID_W), :] = _dot((e / den).astype(BF16), vb[pl.ds(k0, 512), :])
            l_ref[pl.ds(q0, GRID_W), :] = jnp.broadcast_to(mx + jnp.log(den), (GRID_W, HD))
            return carry

        lax.fori_loop(0, T // GRID_W, row, 0)

    return pl.pallas_call(
        body, name="attn_b_fwd",
        out_shape=[SDS((T, 512), F32), SDS((T, 512), F32), SDS((4, 8, GRID_W, 512), F32)], grid=(4,),
        in_specs=[pl.BlockSpec(memory_space=pltpu.SMEM),
                  pl.BlockSpec((T, HD), lambda h: (0, NHA + h)),
                  pl.BlockSpec((T, HD), lambda h: (0, NH + NHA + h)),
                  pl.BlockSpec((T, HD), lambda h: (0, 2 * NH + NHA + h))],
        out_specs=[pl.BlockSpec((T, HD), lambda h: (0, h)), pl.BlockSpec((T, HD), lambda h: (0, h)),
                   pl.BlockSpec((None, 8, GRID_W, 512), lambda h: (h, 0, 0, 0))],
        scratch_shapes=[pltpu.VMEM((T, HD), BF16)],
        compiler_params=_params(1))(rpb_flat, qkn, qkn, proj)


def _comb_fwd(os, ls):
    tm = 512

    def body(o0, o1, o2, l0, l1, l2, oa_ref, w0, w1, w2):
        lv = [l0[...], l1[...], l2[...]]
        mx = jnp.maximum(jnp.maximum(lv[0], lv[1]), lv[2])
        ev = [jnp.exp(l - mx) for l in lv]
        den = ev[0] + ev[1] + ev[2]
        wv = [e / den for e in ev]
        oa_ref[...] = (wv[0] * o0[...] + wv[1] * o1[...] + wv[2] * o2[...]).astype(BF16)
        w0[...], w1[...], w2[...] = wv

    spec = pl.BlockSpec((tm, 512), lambda i: (i, 0))
    return pl.pallas_call(
        body, name="comb_fwd", out_shape=[SDS((T, 512), BF16)] + [SDS((T, 512), F32)] * 3, grid=(T // tm,),
        in_specs=[spec] * 6, out_specs=[spec] * 4, compiler_params=_params(1))(*os, *ls)


def _mix_fwd(oa, ob, proj, b_gate, wpa, wpb):
    tm = 256

    def body(oa_ref, ob_ref, ga_ref, gb_ref, ba_ref, bb_ref, wpa_ref, wpb_ref, mixed_ref, ob16_ref):
        oav = oa_ref[...]
        obv = ob_ref[...].astype(BF16)
        ob16_ref[...] = obv
        for s in range(NSH):
            sl = slice(s * 512, (s + 1) * 512)
            ga = _sigmoid(ga_ref[:, sl] + ba_ref[:, sl])
            gb = _sigmoid(gb_ref[:, sl] + bb_ref[:, sl])
            mixed_ref[:, sl] = (ga * _dot(oav, wpa_ref[s]) + gb * _dot(obv, wpb_ref[s])).astype(BF16)

    row = lambda w: pl.BlockSpec((tm, w), lambda i: (i, 0))
    return pl.pallas_call(
        body, name="mix_fwd", out_shape=[SDS((T, D), BF16), SDS((T, 512), BF16)], grid=(T // tm,),
        in_specs=[row(512), row(512),
                  pl.BlockSpec((tm, D), lambda i: (i, 3)), pl.BlockSpec((tm, D), lambda i: (i, 4)),
                  pl.BlockSpec((1, D), lambda i: (0, 0)), pl.BlockSpec((1, D), lambda i: (0, 1)),
                  _resident((NSH, 512, 512), lambda i: (0, 0, 0)), _resident((NSH, 512, 512), lambda i: (0, 0, 0))],
        out_specs=[row(D), row(512)], compiler_params=_params(1))(oa, ob, proj, proj, b_gate, b_gate, wpa, wpb)


def _out_proj_fwd(mixed, w_out, x, g):
    tm = 256

    def body(m_ref, w_ref, x_ref, g_ref, h1_ref, hn_ref):
        h1 = x_ref[...] + _dot(m_ref[...], w_ref[...])
        h1_ref[...] = h1
        r = lax.rsqrt(jnp.mean(h1 * h1, axis=-1, keepdims=True) + EPS)
        hn_ref[...] = (h1 * r * g_ref[...]).astype(BF16)

    row = pl.BlockSpec((tm, D), lambda i: (i, 0))
    return pl.pallas_call(
        body, name="out_proj_fwd", out_shape=[SDS((T, D), F32), SDS((T, D), BF16)], grid=(T // tm,),
        in_specs=[row, _resident((D, D), lambda i: (0, 0)), row, pl.BlockSpec((1, D), lambda i: (0, 0))],
        out_specs=[row, row], compiler_params=_params(1))(mixed, w_out, x, g)


def _ffn_up(hn, w_up):
    tm, tn = 1024, 512
    per = (DFF // NSH) // tn

    def body(h_ref, w_ref, u_ref):
        u_ref[...] = jnp.maximum(_dot(h_ref[...], w_ref[...]), 0.0)

    return pl.pallas_call(
        body, name="ffn_up", out_shape=SDS((T, DFF), F32), grid=(T // tm, DFF // tn),
        in_specs=[pl.BlockSpec((tm, D), lambda i, j: (i, 0)),
                  pl.BlockSpec((None, D, tn), lambda i, j: (j // per, 0, j % per))],
        out_specs=pl.BlockSpec((tm, tn), lambda i, j: (i, j)), compiler_params=_params(2))(hn, w_up)


def _ffn_down_loss(u, w_down, h1, target):
    tm, tk = 512, 512
    nk = DFF // tk

    def body(u_ref, w_ref, h1_ref, t_ref, dy_ref, dy16_ref, loss_ref, acc):
        k = pl.program_id(1)

        @pl.when(k == 0)
        def _():
            acc[...] = jnp.zeros_like(acc)

        uv = u_ref[...]
        acc[...] += _dot((uv * uv).astype(BF16), w_ref[...])

        @pl.when(k == nk - 1)
        def _():
            err = acc[...] + h1_ref[...] - t_ref[...]
            dy = err * (1.0 / D)
            dy_ref[...] = dy
            dy16_ref[...] = dy.astype(BF16)
            part = 0.5 * jnp.sum(jnp.mean(err * err, axis=-1, keepdims=True), axis=0, keepdims=True)
            loss_ref[...] = jnp.broadcast_to(part, (8, 128))

    row = pl.BlockSpec((tm, D), lambda i, k: (i, 0))
    return pl.pallas_call(
        body, name="ffn_down_loss",
        out_shape=[SDS((T, D), F32), SDS((T, D), BF16), SDS((T // tm, 8, 128), F32)], grid=(T // tm, nk),
        in_specs=[pl.BlockSpec((tm, tk), lambda i, k: (i, k)), pl.BlockSpec((tk, D), lambda i, k: (k, 0)), row, row],
        out_specs=[row, row, pl.BlockSpec((None, 8, 128), lambda i, k: (i, 0, 0))],
        scratch_shapes=[pltpu.VMEM((tm, D), F32)], compiler_params=_params(2))(u, w_down, h1, target)


def _ffn_down_bwd(dy16, w_down, u):
    tm, tn = 1024, 512

    def body(dy_ref, w_ref, u_ref, du_ref):
        uv = u_ref[...]
        du_ref[...] = jnp.where(uv > 0.0, 2.0 * uv * _dot_nt(dy_ref[...], w_ref[...]), 0.0).astype(BF16)

    return pl.pallas_call(
        body, name="ffn_down_bwd", out_shape=SDS((T, DFF), BF16), grid=(T // tm, DFF // tn),
        in_specs=[pl.BlockSpec((tm, D), lambda i, j: (i, 0)), pl.BlockSpec((tn, D), lambda i, j: (j, 0)),
                  pl.BlockSpec((tm, tn), lambda i, j: (i, j))],
        out_specs=pl.BlockSpec((tm, tn), lambda i, j: (i, j)), compiler_params=_params(2))(dy16, w_down, u)


def _norm_bwd(xv, dz_in, g):
    r = lax.rsqrt(jnp.mean(xv * xv, axis=-1, keepdims=True) + EPS)
    dg = jnp.sum(xv * r * dz_in, axis=0, keepdims=True)
    dz = dz_in * g
    dx = r * dz - xv * (r * r * r) * jnp.mean(xv * dz, axis=-1, keepdims=True)
    return dx, dg


def _ffn_up_bwd(du, w_up, h1, dy, g):
    tm, tk = 512, 1024
    per = (DFF // NSH) // tk
    nk = DFF // tk

    def body(du_ref, w_ref, h1_ref, dy_ref, g_ref, dh1_ref, dh16_ref, dg_ref, acc):
        i, k = pl.program_id(0), pl.program_id(1)

        @pl.when(k == 0)
        def _():
            acc[...] = jnp.zeros_like(acc)

        @pl.when((k == 0) & (i == 0))
        def _():
            dg_ref[...] = jnp.zeros_like(dg_ref)

        acc[...] += _dot_nt(du_ref[...], w_ref[...])

        @pl.when(k == nk - 1)
        def _():
            dx, dg = _norm_bwd(h1_ref[...], acc[...], g_ref[...])
            dh1 = dy_ref[...] + dx
            dh1_ref[...] = dh1
            dh16_ref[...] = dh1.astype(BF16)
            dg_ref[...] += dg

    row = pl.BlockSpec((tm, D), lambda i, k: (i, 0))
    vec = pl.BlockSpec((1, D), lambda i, k: (0, 0))
    return pl.pallas_call(
        body, name="ffn_up_bwd", out_shape=[SDS((T, D), F32), SDS((T, D), BF16), SDS((1, D), F32)],
        grid=(T // tm, nk),
        in_specs=[pl.BlockSpec((tm, tk), lambda i, k: (i, k)),
                  pl.BlockSpec((None, D, tk), lambda i, k: (k // per, 0, k % per)), row, row, vec],
        out_specs=[row, row, vec], scratch_shapes=[pltpu.VMEM((tm, D), F32)],
        compiler_params=_params(2))(du, w_up, h1, dy, g)


def _mix_bwd(dh16, w_out, oa, ob16, proj, b_gate, wpa, wpb):
    tm = 128

    def body(dh_ref, wo_ref, oa_ref, ob_ref, ga_ref, gb_ref, ba_ref, bb_ref, wpa_ref, wpb_ref,
             dya_ref, dyb_ref, dga_ref, dgb_ref, doa_ref, dob_ref, dba_ref, dbb_ref):
        @pl.when(pl.program_id(0) == 0)
        def _():
            dba_ref[...] = jnp.zeros_like(dba_ref)
            dbb_ref[...] = jnp.zeros_like(dbb_ref)

        oav, obv = oa_ref[...], ob_ref[...]
        doa = jnp.zeros((tm, 512), F32)
        dob = jnp.zeros((tm, 512), F32)
        for s in range(NSH):
            sl = slice(s * 512, (s + 1) * 512)
            dm = _dot_nt(dh_ref[...], wo_ref[sl, :])
            ga = _sigmoid(ga_ref[:, sl] + ba_ref[:, sl])
            gb = _sigmoid(gb_ref[:, sl] + bb_ref[:, sl])
            dya = (dm * ga).astype(BF16)
            dyb = (dm * gb).astype(BF16)
            dza = dm * _dot(oav, wpa_ref[s]) * ga * (1.0 - ga)
            dzb = dm * _dot(obv, wpb_ref[s]) * gb * (1.0 - gb)
            dya_ref[:, sl], dyb_ref[:, sl] = dya, dyb
            dga_ref[:, sl], dgb_ref[:, sl] = dza.astype(BF16), dzb.astype(BF16)
            dba_ref[:, sl] += jnp.sum(dza, axis=0, keepdims=True)
            dbb_ref[:, sl] += jnp.sum(dzb, axis=0, keepdims=True)
            doa += _dot_nt(dya, wpa_ref[s])
            dob += _dot_nt(dyb, wpb_ref[s])
        doa_ref[...], dob_ref[...] = doa, dob

    row = lambda w: pl.BlockSpec((tm, w), lambda i: (i, 0))
    vec = pl.BlockSpec((1, D), lambda i: (0, 0))
    wp = _resident((NSH, 512, 512), lambda i: (0, 0, 0))
    return pl.pallas_call(
        body, name="mix_bwd",
        out_shape=[SDS((T, D), BF16)] * 4 + [SDS((T, 512), F32)] * 2 + [SDS((1, D), F32)] * 2, grid=(T // tm,),
        in_specs=[row(D), _resident((D, D), lambda i: (0, 0)), row(512), row(512),
                  pl.BlockSpec((tm, D), lambda i: (i, 3)), pl.BlockSpec((tm, D), lambda i: (i, 4)),
                  pl.BlockSpec((1, D), lambda i: (0, 0)), pl.BlockSpec((1, D), lambda i: (0, 1)), wp, wp],
        out_specs=[row(D)] * 4 + [row(512)] * 2 + [vec] * 2,
        compiler_params=_params(1))(dh16, w_out, oa, ob16, proj, proj, b_gate, b_gate, wpa, wpb)


def _comb_bwd(doa, os, ws):
    tm = 512

    def body(d_ref, o0, o1, o2, w0, w1, w2, cc_ref):
        prod = d_ref[...] * (w0[...] * o0[...] + w1[...] * o1[...] + w2[...] * o2[...])
        for h in range(4):
            sl = slice(h * HD, (h + 1) * HD)
            cc_ref[:, sl] = jnp.broadcast_to(jnp.sum(prod[:, sl], axis=-1, keepdims=True), (tm, HD))

    spec = pl.BlockSpec((tm, 512), lambda i: (i, 0))
    return pl.pallas_call(
        body, name="comb_bwd", out_shape=SDS((T, 512), F32), grid=(T // tm,),
        in_specs=[spec] * 7, out_specs=spec, compiler_params=_params(1))(doa, *os, *ws)


def _attn_a_bwd(qkn, proj, doa, lse, w, cc, g):
    dil = DILS[g]
    m = T // dil
    nb = m // 128

    def body(q_ref, k_ref, v_ref, d_ref, l_ref, w_ref, c_ref, dqk_ref, dv_ref, kp, vp, dkp, dvp):
        _fill_padded(kp, k_ref[...], m)
        _fill_padded(vp, v_ref[...], m)
        dkp[...] = jnp.zeros_like(dkp)
        dvp[...] = jnp.zeros_like(dvp)

        def block(b, carry):
            q0 = pl.multiple_of(b * 128, 128)
            rows = pl.ds(q0, 128)
            win = pl.ds(q0, 256)
            qb, kw, vw = q_ref[rows, :], kp[win, :], vp[win, :]
            s = _dot_nt(qb, kw) * SCALE
            s = jnp.where(_band_mask(q0, m), s, NEG)
            wp = _wide(w_ref[rows, :], 2) * jnp.exp(s - _wide(l_ref[rows, :], 2))
            dob = d_ref[rows, :].astype(BF16)
            ds = (wp * (_dot_nt(dob, vw) - _wide(c_ref[rows, :], 2))).astype(BF16)
            dqk_ref[0, rows, :] = _dot(ds, kw) * SCALE
            dkp[win, :] += _dot_tn(ds, qb) * SCALE
            dvp[win, :] += _dot_tn(wp.astype(BF16), dob)
            return carry

        lax.fori_loop(0, nb, block, 0)
        dqk_ref[1] = dkp[64:64 + m, :]
        dv_ref[...] = dvp[64:64 + m, :]

    blk = pl.BlockSpec((m, HD), lambda h, r: (0, r * 4 + h))
    view = lambda a: a.reshape(m, dil * 512)
    dqk, dv = pl.pallas_call(
        body, name=f"attn_a_bwd_{g}", out_shape=[SDS((2, m, dil * 512), F32), SDS((m, dil * 512), F32)], grid=(4, dil),
        in_specs=[pl.BlockSpec((m, HD), lambda h, r: (0, r * 32 + g * 4 + h)),
                  pl.BlockSpec((m, HD), lambda h, r: (0, r * 32 + 16 + g * 4 + h)),
                  pl.BlockSpec((m, HD), lambda h, r: (0, r * 80 + 32 + g * 4 + h)), blk, blk, blk, blk],
        out_specs=[pl.BlockSpec((2, m, HD), lambda h, r: (0, 0, r * 4 + h)), blk],
        scratch_shapes=[pltpu.VMEM((m + 128, HD), BF16), pltpu.VMEM((m + 128, HD), BF16),
                        pltpu.VMEM((m + 128, HD), F32), pltpu.VMEM((m + 128, HD), F32)],
        compiler_params=_params(2))(qkn.reshape(m, dil * 2 * D), qkn.reshape(m, dil * 2 * D), proj.reshape(m, dil * DIN),
                                    view(doa), view(lse), view(w), view(cc))
    return dqk.reshape(2, T, 512), dv.reshape(T, 512)


def _attn_b_bwd(qkn, proj, dob, ob, lse, bias):
    def body(q_ref, k_ref, v_ref, d_ref, o_ref, l_ref, bias_ref, dqk_ref, dv_ref, drpb_ref, vb, dk_acc, dv_acc, a_acc):
        vb[...] = v_ref[...].astype(BF16)
        dk_acc[...] = jnp.zeros_like(dk_acc)
        dv_acc[...] = jnp.zeros_like(dv_acc)
        a_acc[...] = jnp.zeros_like(a_acc)

        def row(r, carry):
            start, off = _nbr_window(r)
            rows = pl.ds(pl.multiple_of(r * GRID_W, GRID_W), GRID_W)
            win = pl.ds(pl.multiple_of(start * GRID_W, GRID_W), 512)
            qr, kw, vw = q_ref[rows, :], k_ref[win, :], vb[win, :]
            s = _dot_nt(qr, kw) * SCALE + bias_ref[off]
            p = jnp.exp(s - _wide(l_ref[rows, :], 4))
            dov = d_ref[rows, :]
            delta = jnp.sum(dov * o_ref[rows, :], axis=-1, keepdims=True)
            do16 = dov.astype(BF16)
            ds = p * (_dot_nt(do16, vw) - delta)
            a_acc[off] += ds
            ds16 = ds.astype(BF16)
            dqk_ref[0, rows, :] = _dot(ds16, kw) * SCALE
            dk_acc[win, :] += _dot_tn(ds16, qr) * SCALE
            dv_acc[win, :] += _dot_tn(p.astype(BF16), do16)
            return carry

        lax.fori_loop(0, T // GRID_W, row, 0)
        dqk_ref[1] = dk_acc[...]
        dv_ref[...] = dv_acc[...]

        qc = lax.broadcasted_iota(jnp.int32, (GRID_W, 512), 0)
        lane = lax.broadcasted_iota(jnp.int32, (GRID_W, 512), 1)
        dc = jnp.clip((lane & (GRID_W - 1)) - qc, -(WIN_C - 1), WIN_C - 1) + (WIN_C - 1)
        jrow = lax.broadcasted_iota(jnp.int32, (1, 512), 1) >> 6
        dlane = lax.broadcasted_iota(jnp.int32, (1, HD), 1)
        drpb_ref[...] = jnp.zeros_like(drpb_ref)

        def per_dc(e, carry):
            sel = dc == e
            out = jnp.zeros((1, HD), F32)
            for off in range(8):
                col = jnp.sum(jnp.where(sel, a_acc[off], 0.0), axis=0, keepdims=True)
                for j in range(8):
                    part = jnp.sum(jnp.where(jrow == j, col, 0.0), axis=-1, keepdims=True)
                    out = out + jnp.where(dlane == off + j, part, 0.0)
            drpb_ref[pl.ds(e, 1), :] = out
            return carry

        lax.fori_loop(0, 31, per_dc, 0)

    blk = pl.BlockSpec((T, HD), lambda h: (0, h))
    return pl.pallas_call(
        body, name="attn_b_bwd",
        out_shape=[SDS((2, T, 512), F32), SDS((T, 512), F32), SDS((4, 32, HD), F32)], grid=(4,),
        in_specs=[pl.BlockSpec((T, HD), lambda h: (0, NHA + h)),
                  pl.BlockSpec((T, HD), lambda h: (0, NH + NHA + h)),
                  pl.BlockSpec((T, HD), lambda h: (0, 2 * NH + NHA + h)), blk, blk, blk,
                  pl.BlockSpec((None, 8, GRID_W, 512), lambda h: (h, 0, 0, 0))],
        out_specs=[pl.BlockSpec((2, T, HD), lambda h: (0, 0, h)), blk,
                   pl.BlockSpec((None, 32, HD), lambda h: (h, 0, 0))],
        scratch_shapes=[pltpu.VMEM((T, HD), BF16), pltpu.VMEM((T, HD), F32), pltpu.VMEM((T, HD), F32),
                        pltpu.VMEM((8, GRID_W, 512), F32)],
        compiler_params=_params(1))(qkn, qkn, proj, dob, ob, lse, bias)


def _qk_bwd(proj, nw, cos, sin, dqk_groups, dqk_b):
    tm = 256

    def body(p_ref, w_ref, cos_ref, sin_ref, d0, d1, d2, d3, o_ref, dn_ref):
        @pl.when(pl.program_id(1) == 0)
        def _():
            dn_ref[...] = jnp.zeros_like(dn_ref)

        cv, sv = cos_ref[...], sin_ref[...]
        srcs = (d0, d1, d2, d3)
        dna = jnp.zeros((1, HD), F32)
        dnb = jnp.zeros((1, HD), F32)
        for h in range(NH):
            sl = slice(h * HD, (h + 1) * HD)
            dz = srcs[h // 4][:, (h % 4) * HD:(h % 4 + 1) * HD]
            if h < NHA:
                dz = dz * cv + pltpu.roll(dz * sv, 64, 1)
            dx, dg = _norm_bwd(p_ref[:, sl], dz, w_ref[:, sl])
            o_ref[:, sl] = dx.astype(BF16)
            if h < NHA:
                dna += dg
            else:
                dnb += dg
        dn_ref[0:1, :] += dna
        dn_ref[1:2, :] += dnb

    dspec = pl.BlockSpec((None, tm, 512), lambda j, i: (j, i, 0))
    return pl.pallas_call(
        body, name="qk_bwd", out_shape=[SDS((T, 2 * D), BF16), SDS((2, 8, HD), F32)], grid=(2, T // tm),
        in_specs=[pl.BlockSpec((tm, D), lambda j, i: (i, j)),
                  pl.BlockSpec((None, 1, D), lambda j, i: (j, 0, 0)),
                  pl.BlockSpec((tm, HD), lambda j, i: (i, 0)),
                  pl.BlockSpec((tm, HD), lambda j, i: (i, 0)), dspec, dspec, dspec, dspec],
        out_specs=[pl.BlockSpec((tm, D), lambda j, i: (i, j)), pl.BlockSpec((None, 8, HD), lambda j, i: (j, 0, 0))],
        compiler_params=_params(2))(proj, nw, cos, sin, *dqk_groups, dqk_b)


def _in_proj_bwd(dproj, w_in, x, dh1, g):
    tm, tk = 512, 1280
    per = (DIN // NSH) // tk
    nk = DIN // tk

    def body(dp_ref, w_ref, x_ref, dh_ref, g_ref, dx_ref, dg_ref, acc):
        i, k = pl.program_id(0), pl.program_id(1)

        @pl.when(k == 0)
        def _():
            acc[...] = jnp.zeros_like(acc)

        @pl.when((k == 0) & (i == 0))
        def _():
            dg_ref[...] = jnp.zeros_like(dg_ref)

        acc[...] += _dot_nt(dp_ref[...], w_ref[...])

        @pl.when(k == nk - 1)
        def _():
            dx, dg = _norm_bwd(x_ref[...], acc[...], g_ref[...])
            dx_ref[...] = dh_ref[...] + dx
            dg_ref[...] += dg

    row = pl.BlockSpec((tm, D), lambda i, k: (i, 0))
    vec = pl.BlockSpec((1, D), lambda i, k: (0, 0))
    return pl.pallas_call(
        body, name="in_proj_bwd", out_shape=[SDS((T, D), F32), SDS((1, D), F32)], grid=(T // tm, nk),
        in_specs=[pl.BlockSpec((tm, tk), lambda i, k: (i, k)),
                  pl.BlockSpec((None, D, tk), lambda i, k: (k // per, 0, k % per)), row, row, vec],
        out_specs=[row, vec], scratch_shapes=[pltpu.VMEM((tm, D), F32)],
        compiler_params=_params(2))(dproj, w_in, x, dh1, g)


def _grad_w(name, a, g, shard_rows, rows, cols, tr, tc, square=False):
    ni, nj = rows // tr, cols // tc
    if shard_rows:
        a_map, g_map = (lambda s, i, j: (0, s * ni + i)), (lambda s, i, j: (0, j))
    else:
        a_map, g_map = (lambda s, i, j: (0, i)), (lambda s, i, j: (0, s * nj + j))

    def body(a_ref, g_ref, o_ref):
        av = a_ref[...]
        if square:
            av = (av * av).astype(BF16)
        o_ref[...] = _dot_tn(av, g_ref[...]).astype(BF16)

    return pl.pallas_call(
        body, name=name, out_shape=SDS((NSH, rows, cols), BF16), grid=(NSH, ni, nj),
        in_specs=[pl.BlockSpec((T, tr), a_map), pl.BlockSpec((T, tc), g_map)],
        out_specs=pl.BlockSpec((None, tr, tc), lambda s, i, j: (s, i, j)), compiler_params=_params(3))(a, g)


def _adamw(w, g, m, v):
    m = B1 * m + (1.0 - B1) * g
    v = B2 * v + (1.0 - B2) * (g * g)
    m_hat = m / (1.0 - B1 ** STEP)
    v_hat = v / (1.0 - B2 ** STEP)
    delta = -LR * (m_hat / (jnp.sqrt(v_hat) + AEPS) + WD * w)
    return delta, m, v


def _sum_partials(name, parts):
    _, rows, cols = parts.shape
    tr = _row_tile(rows, cols, 1 << 18)

    def body(p_ref, o_ref):
        o_ref[...] = ((p_ref[3].astype(F32) + p_ref[0].astype(F32)) + p_ref[1].astype(F32)) + p_ref[2].astype(F32)

    return pl.pallas_call(
        body, name=name, out_shape=SDS((rows, cols), F32), grid=(rows // tr,),
        in_specs=[pl.BlockSpec((4, tr, cols), lambda i: (0, i, 0))],
        out_specs=pl.BlockSpec((tr, cols), lambda i: (i, 0)), compiler_params=_params(1))(parts)


def _adam_shard(name, mine, theirs, w, m, v):
    rows, cols = w.shape
    tr = _row_tile(rows, cols, 1 << 18)

    def body(a_ref, b_ref, w_ref, m_ref, v_ref, g_ref, d_ref, nm_ref, nv_ref):
        g = a_ref[...] + b_ref[...]
        g_ref[...] = g
        d_ref[...], nm_ref[...], nv_ref[...] = _adamw(w_ref[...], g, m_ref[...], v_ref[...])

    spec = pl.BlockSpec((tr, cols), lambda i: (i, 0))
    return pl.pallas_call(
        body, name=name, out_shape=[SDS((rows, cols), F32)] * 4, grid=(rows // tr,),
        in_specs=[spec] * 5, out_specs=[spec] * 4, compiler_params=_params(1))(mine, theirs, w, m, v)


def _adam_small(gathered, w, m, v):
    def body(g_ref, w_ref, m_ref, v_ref, go_ref, d_ref, nm_ref, nv_ref):
        g = g_ref[0:SMALL_ROWS, :]
        for dev in range(1, 8):
            g = g + g_ref[dev * SMALL_ROWS:(dev + 1) * SMALL_ROWS, :]
        go_ref[...] = g
        d_ref[...], nm_ref[...], nv_ref[...] = _adamw(w_ref[...], g, m_ref[...], v_ref[...])

    return pl.pallas_call(body, name="adam_small", out_shape=[SDS((SMALL_ROWS, HD), F32)] * 4)(gathered, w, m, v)


SMALL = (("norm_mix", (1, D)), ("b_gate", (1, 2 * D)), ("q_norm_a", (1, HD)), ("k_norm_a", (1, HD)),
         ("q_norm_b", (1, HD)), ("k_norm_b", (1, HD)), ("rpb_b", (1, 4, 15, 31)), ("norm_ffn", (1, D)))


def _pack_small(vals):
    pieces = []
    for (name, shape), val in zip(SMALL, vals):
        flat = val.reshape(-1)
        pad = (-flat.shape[0]) % HD
        pieces.append(jnp.pad(flat, (0, pad)).reshape(-1, HD))
    packed = jnp.concatenate(pieces, axis=0)
    return jnp.pad(packed, ((0, SMALL_ROWS - packed.shape[0]), (0, 0)))


def _unpack_small(packed):
    out, row = [], 0
    for name, shape in SMALL:
        size = int(np.prod(shape))
        nrows = -(-size // HD)
        out.append(packed[row:row + nrows].reshape(-1)[:size].reshape(shape))
        row += nrows
    return out


def kernel(x, norm_mix, w_in, b_gate, q_norm_a, k_norm_a, q_norm_b, k_norm_b, rpb_b, w_proj_a, w_proj_b, w_out, norm_ffn, w_up, w_down, loss_target, m_norm_mix, m_w_in, m_b_gate, m_q_norm_a, m_k_norm_a, m_q_norm_b, m_k_norm_b, m_rpb_b, m_w_proj_a, m_w_proj_b, m_w_out, m_norm_ffn, m_w_up, m_w_down, v_norm_mix, v_w_in, v_b_gate, v_q_norm_a, v_k_norm_a, v_q_norm_b, v_k_norm_b, v_rpb_b, v_w_proj_a, v_w_proj_b, v_w_out, v_norm_ffn, v_w_up, v_w_down):
    big_names = ("w_in", "w_proj_a", "w_proj_b", "w_out", "w_up", "w_down")
    big_w = [a[0] for a in (w_in, w_proj_a, w_proj_b, w_out, w_up, w_down)]
    big_m = [a[0] for a in (m_w_in, m_w_proj_a, m_w_proj_b, m_w_out, m_w_up, m_w_down)]
    big_v = [a[0] for a in (v_w_in, v_w_proj_a, v_w_proj_b, v_w_out, v_w_up, v_w_down)]
    x2, target = x[0], loss_target[0]

    shards = [_cast_bf16(w, "cast_" + n) for n, w in zip(big_names, big_w)]
    win_f, wpa_f, wpb_f, wout_f, wup_f, wdown_f = _allgather_weights(shards)
    wout_f = wout_f.reshape(D, D)
    wdown_f = wdown_f.reshape(DFF, D)

    proj, xn = _norm_in_proj(x2, norm_mix, win_f)
    cos, sin = _rope_tables()
    nw = jnp.stack([jnp.concatenate([jnp.tile(q_norm_a, (1, NHA)), jnp.tile(q_norm_b, (1, NH - NHA))], axis=1),
                    jnp.concatenate([jnp.tile(k_norm_a, (1, NHA)), jnp.tile(k_norm_b, (1, NH - NHA))], axis=1)])
    qkn = _qk_prep(proj, nw, cos, sin)
    fwd_a = [_attn_a_fwd(qkn, proj, g) for g in range(3)]
    os, ls = [f[0] for f in fwd_a], [f[1] for f in fwd_a]
    ob, lse_b, bias = _attn_b_fwd(qkn, proj, rpb_b.reshape(-1))
    oa, w0, w1, w2 = _comb_fwd(os, ls)
    ws = [w0, w1, w2]
    mixed, ob16 = _mix_fwd(oa, ob, proj, b_gate, wpa_f, wpb_f)
    h1, hn = _out_proj_fwd(mixed, wout_f, x2, norm_ffn)
    u = _ffn_up(hn, wup_f)
    dy, dy16, loss_parts = _ffn_down_loss(u, wdown_f, h1, target)
    loss = lax.psum(jnp.sum(loss_parts[:, 0, 0]), ("x", "y", "c"))

    du = _ffn_down_bwd(dy16, wdown_f, u)
    g_down = _grad_w("grad_w_down", u, dy16, True, DFF // NSH, D, 512, 1024, square=True)
    dh1, dh16, d_norm_ffn = _ffn_up_bwd(du, wup_f, h1, dy, norm_ffn)
    g_up = _grad_w("grad_w_up", hn, du, False, D, DFF // NSH, 1024, 1024)
    dya, dyb, dga, dgb, doa, dob, dba, dbb = _mix_bwd(dh16, wout_f, oa, ob16, proj, b_gate, wpa_f, wpb_f)
    g_out = _grad_w("grad_w_out", mixed, dh16, True, D // NSH, D, 512, 1024)
    g_pa = _grad_w("grad_w_proj_a", oa, dya, False, 512, 512, 512, 512)
    g_pb = _grad_w("grad_w_proj_b", ob16, dyb, False, 512, 512, 512, 512)
    cc = _comb_bwd(doa, os, ws)
    bwd_a = [_attn_a_bwd(qkn, proj, doa, ls[g], ws[g], cc, g) for g in range(3)]
    dqk_b, dv_b, drpb_t = _attn_b_bwd(qkn, proj, dob, ob, lse_b, bias)
    dqk_pre, dn = _qk_bwd(proj, nw, cos, sin, [b[0] for b in bwd_a], dqk_b)
    dv16 = jnp.concatenate([b[1] for b in bwd_a] + [dv_b], axis=1).astype(BF16)
    dproj = jnp.concatenate([dqk_pre, dv16, dga, dgb], axis=1)
    grad_x, d_norm_mix = _in_proj_bwd(dproj, win_f, x2, dh1, norm_mix)
    g_in = _grad_w("grad_w_in", xn, dproj, False, D, DIN // NSH, 1024, 1280)

    landed = _scatter_grads([g_in, g_pa, g_pb, g_out, g_up, g_down])
    sums = [_sum_partials("sum_" + n, p) for n, p in zip(big_names, landed)]
    theirs = _swap_with_sibling(sums)
    big_out = [_adam_shard("adam_" + n, a, b, w, m, v)
               for n, a, b, w, m, v in zip(big_names, sums, theirs, big_w, big_m, big_v)]

    d_rpb = jnp.transpose(drpb_t[:, :31, :15], (0, 2, 1))
    small_g = [d_norm_mix, jnp.concatenate([dba, dbb], axis=1), dn[0, 0], dn[1, 0], dn[0, 1], dn[1, 1], d_rpb, d_norm_ffn]
    gathered = _allgather_small(_pack_small(small_g))
    small_w = (norm_mix, b_gate, q_norm_a, k_norm_a, q_norm_b, k_norm_b, rpb_b, norm_ffn)
    small_m = (m_norm_mix, m_b_gate, m_q_norm_a, m_k_norm_a, m_q_norm_b, m_k_norm_b, m_rpb_b, m_norm_ffn)
    small_v = (v_norm_mix, v_b_gate, v_q_norm_a, v_k_norm_a, v_q_norm_b, v_k_norm_b, v_rpb_b, v_norm_ffn)
    small_out = [_unpack_small(p) for p in
                 _adam_small(gathered, _pack_small(small_w), _pack_small(small_m), _pack_small(small_v))]

    order = ("norm_mix", "w_in", "b_gate", "q_norm_a", "k_norm_a", "q_norm_b", "k_norm_b", "rpb_b",
             "w_proj_a", "w_proj_b", "w_out", "norm_ffn", "w_up", "w_down")
    small_idx = {name: i for i, (name, _) in enumerate(SMALL)}
    outs = []
    for kind in range(4):
        for name in order:
            if name in small_idx:
                outs.append(small_out[kind][small_idx[name]])
            else:
                outs.append(big_out[big_names.index(name)][kind][None])
    return (loss, grad_x[None], *outs)
```

```python
import functools

import numpy as np
import jax
import jax.numpy as jnp
from jax import lax
from jax.experimental import pallas as pl
from jax.experimental.pallas import tpu as pltpu

F32, BF16 = jnp.float32, jnp.bfloat16
SDS = jax.ShapeDtypeStruct
MESH = pl.DeviceIdType.MESH

T = 2048
D = 2048
HD = 128
NH, NHA = 16, 12
DIN = 10240
DFF = 8192
NSH = 4
DILS = (1, 4, 16)
EPS = 1e-6
NEG = -1e30
SCALE = HD ** -0.5
GRID_W, WIN_R, WIN_C = 64, 8, 16
NRPB = 15 * 31
VMEM_LIMIT = 56 * 1024 * 1024
B1, B2, LR, AEPS, WD, STEP = 0.9, 0.999, 0.001, 1e-08, 0.01, 10
SMALL_ROWS = 88


def _dot(a, b):
    return jnp.dot(a, b, preferred_element_type=F32)


def _dot_nt(a, b):
    return lax.dot_general(a, b, (((1,), (1,)), ((), ())), preferred_element_type=F32)


def _dot_tn(a, b):
    return lax.dot_general(a, b, (((0,), (0,)), ((), ())), preferred_element_type=F32)


def _params(n):
    return pltpu.CompilerParams(dimension_semantics=("arbitrary",) * n, vmem_limit_bytes=VMEM_LIMIT)


def _resident(shape, index_map):
    return pl.BlockSpec(shape, index_map, pipeline_mode=pl.Buffered(1))


def _sigmoid(z):
    return 1.0 / (1.0 + jnp.exp(-z))


def _wide(v, n):
    return jnp.concatenate([v] * n, axis=1)


def _row_tile(rows, cols, elems):
    tr = 16
    while tr * 2 <= rows and tr * 2 * cols <= elems:
        tr *= 2
    return tr


def _place():
    x, y, c = lax.axis_index("x"), lax.axis_index("y"), lax.axis_index("c")
    peers = [(1 - x, y), (x, 1 - y), (1 - x, 1 - y)]
    return x, y, c, peers


def _cast_bf16(w, name):
    rows, cols = w.shape
    tr = min(rows, 256)

    def body(w_ref, o_ref):
        o_ref[...] = w_ref[...].astype(BF16)

    return pl.pallas_call(
        body, name=name, out_shape=SDS((rows, cols), BF16), grid=(rows // tr,),
        in_specs=[pl.BlockSpec((tr, cols), lambda i: (i, 0))],
        out_specs=pl.BlockSpec((tr, cols), lambda i: (i, 0)), compiler_params=_params(1))(w)


ANY_SPEC = pl.BlockSpec(memory_space=pl.ANY)
HBM_SPEC = pl.BlockSpec(memory_space=pltpu.HBM)
SEM_SPEC = pl.BlockSpec(memory_space=pltpu.SEMAPHORE)
DEP_SPEC = pl.BlockSpec((8, 128), lambda *_: (0, 0))
EFFECT = pltpu.SideEffectType.DATAFLOW_SIDE_EFFECTING


def _after(body, deps):
    n = len(deps)
    return (lambda *refs: body(*refs[n:])) if n else body


def _half(c, rows):
    return pl.ds(pl.multiple_of(c * (rows // 2), 16), rows // 2)


def _split_start(name, srcs, lands, n_copies, issue):
    n, m = len(srcs), len(lands)

    def body(*refs):
        issue(refs[:n], refs[n:n + m], refs[n + m], refs[n + m + 1])
        refs[-1][...] = jnp.zeros((8, 128), F32)

    arrays = list(srcs) + list(lands)
    outs = pl.pallas_call(
        body, name=name,
        out_shape=(pltpu.SemaphoreType.DMA((n_copies,)), pltpu.SemaphoreType.DMA((n_copies,)),
                   *[pltpu.HBM(a.shape, a.dtype) for a in arrays], SDS((8, 128), F32)),
        in_specs=[HBM_SPEC] * (n + m),
        out_specs=(SEM_SPEC, SEM_SPEC, *[HBM_SPEC] * (n + m), pl.BlockSpec(memory_space=pltpu.VMEM)),
        input_output_aliases={i: 2 + i for i in range(n + m)},
        compiler_params=pltpu.CompilerParams(has_side_effects=EFFECT),
    )(*[pltpu.with_memory_space_constraint(a, pltpu.HBM) for a in arrays])
    return outs[0], outs[1], list(outs[2:2 + n]), list(outs[2 + n:2 + n + m]), outs[-1]


def _split_wait(name, send_sems, recv_sems, srcs, lands, after, wait):
    n, m = len(srcs), len(lands)

    def body(*refs):
        wait(refs[:n], refs[n:n + m], refs[n + m], refs[n + m + 1])

    arrays = list(srcs) + list(lands)
    outs = pl.pallas_call(
        body, name=name, out_shape=[pltpu.HBM(a.shape, a.dtype) for a in arrays],
        in_specs=[HBM_SPEC] * (n + m) + [SEM_SPEC, SEM_SPEC] + [ANY_SPEC] * len(after),
        out_specs=[HBM_SPEC] * (n + m), input_output_aliases={i: i for i in range(n + m)},
        compiler_params=pltpu.CompilerParams(has_side_effects=EFFECT),
    )(*arrays, send_sems, recv_sems, *after)
    return list(outs[:n]), list(outs[n:])


def _gather_start(name, shards):
    lands = [lax.empty((NSH,) + s.shape, s.dtype) for s in shards]

    def issue(srcs, dsts, send_sems, recv_sems):
        x, y, c, peers = _place()
        for i, s in enumerate(shards):
            half = _half(c, s.shape[0])
            for k, (px, py) in enumerate(peers):
                pltpu.make_async_remote_copy(
                    src_ref=srcs[i].at[half], dst_ref=dsts[i].at[2 * x + y, half], send_sem=send_sems.at[3 * i + k],
                    recv_sem=recv_sems.at[3 * i + k], device_id=(px, py, c), device_id_type=MESH).start()

    return _split_start(name, shards, lands, 3 * len(shards), issue)


def _gather_wait(name, send_sems, recv_sems, shards, lands, after):
    def wait(srcs, dsts, send_sems, recv_sems):
        x, y, c, peers = _place()
        for i, s in enumerate(shards):
            half = _half(c, s.shape[0])
            for k, (px, py) in enumerate(peers):
                cp = pltpu.make_async_remote_copy(
                    src_ref=srcs[i].at[half], dst_ref=dsts[i].at[2 * px + py, half], send_sem=send_sems.at[3 * i + k],
                    recv_sem=recv_sems.at[3 * i + k], device_id=(px, py, c), device_id_type=MESH)
                cp.wait_send()
                cp.wait_recv()

    return _split_wait(name, send_sems, recv_sems, shards, lands, after, wait)


def _gather_finish(name, shards, fulls):
    n = len(shards)

    def body(*refs):
        srcs, fin, fout = refs[:n], refs[n:2 * n], refs[2 * n:3 * n]
        send_sems, recv_sems, local_sems = refs[3 * n:]
        x, y, c, peers = _place()
        local = [pltpu.make_async_copy(srcs[i], fout[i].at[2 * x + y], local_sems.at[i]) for i in range(n)]
        for cp in local:
            cp.start()

        def copy(i, k, half):
            px, py = peers[k]
            return pltpu.make_async_remote_copy(
                src_ref=fin[i].at[2 * px + py, half], dst_ref=fout[i].at[2 * px + py, half],
                send_sem=send_sems.at[3 * i + k], recv_sem=recv_sems.at[3 * i + k],
                device_id=(x, y, 1 - c), device_id_type=MESH)

        sends = [copy(i, k, _half(c, shards[i].shape[0])) for i in range(n) for k in range(3)]
        for cp in sends:
            cp.start()
        for i in range(n):
            for k in range(3):
                copy(i, k, _half(1 - c, shards[i].shape[0])).wait_recv()
        for cp in sends:
            cp.wait_send()
        for cp in local:
            cp.wait()

    return pl.pallas_call(
        body, name=name, out_shape=[SDS(f.shape, f.dtype) for f in fulls],
        in_specs=[ANY_SPEC] * (2 * n), out_specs=[ANY_SPEC] * n, input_output_aliases={n + i: i for i in range(n)},
        scratch_shapes=[pltpu.SemaphoreType.DMA((3 * n,)), pltpu.SemaphoreType.DMA((3 * n,)),
                        pltpu.SemaphoreType.DMA((n,))])(*shards, *fulls)


def _reduce_exchange(name, grads):
    n = len(grads)

    def body(*refs):
        ins, mine, theirs = refs[:n], refs[n:2 * n], refs[2 * n:3 * n]
        send_sems, recv_sems, local_sems = refs[3 * n:]
        x, y, c, _ = _place()
        copies = []
        for i, g in enumerate(grads):
            local = pltpu.make_async_copy(ins[i].at[:, _half(c, g.shape[1])], mine[i], local_sems.at[i])
            local.start()
            cp = pltpu.make_async_remote_copy(
                src_ref=ins[i].at[:, _half(1 - c, g.shape[1])], dst_ref=theirs[i], send_sem=send_sems.at[i],
                recv_sem=recv_sems.at[i], device_id=(x, y, 1 - c), device_id_type=MESH)
            cp.start()
            copies.append((local, cp))
        for local, cp in copies:
            cp.wait_recv()
            cp.wait_send()
            local.wait()

    halves = [SDS((NSH, g.shape[1] // 2, g.shape[2]), g.dtype) for g in grads]
    outs = pl.pallas_call(
        body, name=name, out_shape=halves + halves, in_specs=[ANY_SPEC] * n, out_specs=[ANY_SPEC] * (2 * n),
        scratch_shapes=[pltpu.SemaphoreType.DMA((n,)), pltpu.SemaphoreType.DMA((n,)),
                        pltpu.SemaphoreType.DMA((n,))])(*grads)
    return list(outs[:n]), list(outs[n:])


def _reduce_start(name, parts):
    lands = [lax.empty((3,) + p.shape[1:], p.dtype) for p in parts]

    def issue(srcs, dsts, send_sems, recv_sems):
        x, y, c, peers = _place()
        for i in range(len(parts)):
            for k, (px, py) in enumerate(peers):
                pltpu.make_async_remote_copy(
                    src_ref=srcs[i].at[2 * px + py], dst_ref=dsts[i].at[k], send_sem=send_sems.at[3 * i + k],
                    recv_sem=recv_sems.at[3 * i + k], device_id=(px, py, c), device_id_type=MESH).start()

    return _split_start(name, parts, lands, 3 * len(parts), issue)


def _reduce_wait(name, send_sems, recv_sems, parts, lands, after):
    def wait(srcs, dsts, send_sems, recv_sems):
        x, y, c, peers = _place()
        for i in range(len(parts)):
            for k, (px, py) in enumerate(peers):
                cp = pltpu.make_async_remote_copy(
                    src_ref=srcs[i].at[2 * px + py], dst_ref=dsts[i].at[k], send_sem=send_sems.at[3 * i + k],
                    recv_sem=recv_sems.at[3 * i + k], device_id=(px, py, c), device_id_type=MESH)
                cp.wait_send()
                cp.wait_recv()

    return _split_wait(name, send_sems, recv_sems, parts, lands, after, wait)


def _reduce_share(name, sums):
    n = len(sums)

    def body(*refs):
        ins, outs = refs[:n], refs[n:2 * n]
        send_sems, recv_sems, local_sems = refs[2 * n:]
        x, y, c, _ = _place()
        copies = []
        for i, s in enumerate(sums):
            mine = _half(c, 2 * s.shape[0])
            local = pltpu.make_async_copy(ins[i], outs[i].at[mine], local_sems.at[i])
            local.start()
            cp = pltpu.make_async_remote_copy(
                src_ref=ins[i], dst_ref=outs[i].at[mine], send_sem=send_sems.at[i], recv_sem=recv_sems.at[i],
                device_id=(x, y, 1 - c), device_id_type=MESH)
            cp.start()
            copies.append((local, cp))
        for i, (local, cp) in enumerate(copies):
            pltpu.make_async_remote_copy(
                src_ref=ins[i], dst_ref=outs[i].at[_half(1 - c, 2 * sums[i].shape[0])], send_sem=send_sems.at[i],
                recv_sem=recv_sems.at[i], device_id=(x, y, 1 - c), device_id_type=MESH).wait_recv()
            cp.wait_send()
            local.wait()

    return pl.pallas_call(
        body, name=name, out_shape=[SDS((2 * s.shape[0], s.shape[1]), s.dtype) for s in sums],
        in_specs=[ANY_SPEC] * n, out_specs=[ANY_SPEC] * n,
        scratch_shapes=[pltpu.SemaphoreType.DMA((n,)), pltpu.SemaphoreType.DMA((n,)),
                        pltpu.SemaphoreType.DMA((n,))])(*sums)


def _allgather_small(v):
    m_per, n = v.shape

    def body(x_ref, out_ref, send_sems, recv_sems, local_sem):
        x, y, c = lax.axis_index("x"), lax.axis_index("y"), lax.axis_index("c")
        me, sibling = (x, y, c), (x, y, 1 - c)
        chips = [(1 - x, y), (x, 1 - y), (1 - x, 1 - y)]

        def rows(px, py, pc):
            return out_ref.at[pl.ds((4 * px + 2 * py + pc) * m_per, m_per), :]

        def copy(k, block, to, src=None):
            return pltpu.make_async_remote_copy(
                src_ref=rows(*block) if src is None else src, dst_ref=rows(*block),
                send_sem=send_sems.at[k], recv_sem=recv_sems.at[k], device_id=to, device_id_type=MESH)

        mine = pltpu.make_async_copy(x_ref, rows(*me), local_sem)
        mine.start()
        first = [copy(0, me, sibling, src=x_ref)]
        first += [copy(1 + j, me, (*chip, c), src=x_ref) for j, chip in enumerate(chips)]
        for cp in first:
            cp.start()
        passed = [copy(4 + j, (*chip, c), sibling) for j, chip in enumerate(chips)]
        for j, chip in enumerate(chips):
            copy(1 + j, (*chip, c), me).wait_recv()
            passed[j].start()
        copy(0, sibling, me).wait_recv()
        for j, chip in enumerate(chips):
            copy(4 + j, (*chip, 1 - c), me).wait_recv()
        for cp in first + passed:
            cp.wait_send()
        mine.wait()

    return pl.pallas_call(
        body, name="allgather_small", out_shape=SDS((8 * m_per, n), v.dtype),
        in_specs=[pl.BlockSpec(memory_space=pltpu.VMEM)], out_specs=pl.BlockSpec(memory_space=pltpu.VMEM),
        scratch_shapes=[pltpu.SemaphoreType.DMA((7,)), pltpu.SemaphoreType.DMA((7,)), pltpu.SemaphoreType.DMA])(v)


def _norm_in_proj(x, g, w_full):
    tm, tn = 512, 512
    per = (DIN // NSH) // tn

    def body(x_ref, g_ref, w_ref, proj_ref, xn_ref):
        @pl.when(pl.program_id(1) == 0)
        def _():
            xv = x_ref[...]
            r = lax.rsqrt(jnp.mean(xv * xv, axis=-1, keepdims=True) + EPS)
            xn_ref[...] = (xv * r * g_ref[...]).astype(BF16)

        proj_ref[...] = _dot(xn_ref[...], w_ref[...])

    return pl.pallas_call(
        body, name="norm_in_proj", out_shape=[SDS((T, DIN), F32), SDS((T, D), BF16)],
        grid=(T // tm, DIN // tn),
        in_specs=[pl.BlockSpec((tm, D), lambda i, j: (i, 0)),
                  pl.BlockSpec((1, D), lambda i, j: (0, 0)),
                  pl.BlockSpec((None, D, tn), lambda i, j: (j // per, 0, j % per))],
        out_specs=[pl.BlockSpec((tm, tn), lambda i, j: (i, j)),
                   pl.BlockSpec((tm, D), lambda i, j: (i, 0))],
        compiler_params=_params(2))(x, g, w_full)


def _rope_tables():
    pos = np.arange(T, dtype=np.float32)
    inv = (10000.0 ** (-np.arange(0, HD, 2, dtype=np.float32) / HD)).astype(np.float32)
    ang = (pos[:, None] * inv[None, :]).astype(np.float32)
    cos, sin = np.cos(ang).astype(np.float32), np.sin(ang).astype(np.float32)
    return (jnp.asarray(np.concatenate([cos, cos], axis=1)), jnp.asarray(np.concatenate([-sin, sin], axis=1)))


def _qk_prep(proj, nw, cos, sin):
    tm = 256

    def body(p_ref, w_ref, cos_ref, sin_ref, o_ref):
        cv, sv = cos_ref[...], sin_ref[...]
        for h in range(NH):
            sl = slice(h * HD, (h + 1) * HD)
            xv = p_ref[:, sl]
            r = lax.rsqrt(jnp.mean(xv * xv, axis=-1, keepdims=True) + EPS)
            z = xv * r * w_ref[:, sl]
            if h < NHA:
                z = z * cv + pltpu.roll(z, 64, 1) * sv
            o_ref[:, sl] = z.astype(BF16)

    return pl.pallas_call(
        body, name="qk_prep", out_shape=SDS((T, 2 * D), BF16), grid=(T // tm, 2),
        in_specs=[pl.BlockSpec((tm, D), lambda i, j: (i, j)),
                  pl.BlockSpec((None, 1, D), lambda i, j: (j, 0, 0)),
                  pl.BlockSpec((tm, HD), lambda i, j: (i, 0)),
                  pl.BlockSpec((tm, HD), lambda i, j: (i, 0))],
        out_specs=pl.BlockSpec((tm, D), lambda i, j: (i, j)),
        compiler_params=_params(2))(proj, nw, cos, sin)


def _band_mask(q0, m):
    ii = lax.broadcasted_iota(jnp.int32, (128, 256), 0)
    jj = lax.broadcasted_iota(jnp.int32, (128, 256), 1)
    rel = jj - ii
    kpos = jj + (q0 - 64)
    return (rel >= 0) & (rel <= 128) & (kpos >= 0) & (kpos < m)


def _fill_padded(dst, src, m):
    zeros = jnp.zeros((64, HD), dst.dtype)
    dst[0:64, :] = zeros
    dst[64 + m:128 + m, :] = zeros
    dst[64:64 + m, :] = src.astype(dst.dtype)


def _group_views(qkn, proj, g):
    m = T // DILS[g]
    cols = (qkn[:, g * 512:(g + 1) * 512], qkn[:, D + g * 512:D + (g + 1) * 512],
            proj[:, 2 * D + g * 512:2 * D + (g + 1) * 512])
    return [a.reshape(m, DILS[g] * 512) for a in cols]


def _attn_a_fwd(qkn, proj, g):
    dil = DILS[g]
    m = T // dil
    nb = m // 128

    def body(q_ref, k_ref, v_ref, o_ref, l_ref, kp, vp):
        _fill_padded(kp, k_ref[...], m)
        _fill_padded(vp, v_ref[...], m)

        def block(b, carry):
            q0 = pl.multiple_of(b * 128, 128)
            kw, vw = kp[pl.ds(q0, 256), :], vp[pl.ds(q0, 256), :]
            s = _dot_nt(q_ref[pl.ds(q0, 128), :], kw) * SCALE
            s = jnp.where(_band_mask(q0, m), s, NEG)
            mx = jnp.max(s, axis=-1, keepdims=True)
            e = jnp.exp(s - mx)
            den = jnp.sum(e, axis=-1, keepdims=True)
            o_ref[pl.ds(q0, 128), :] = _dot((e / den).astype(BF16), vw)
            l_ref[pl.ds(q0, 128), :] = jnp.broadcast_to(mx + jnp.log(den), (128, HD))
            return carry

        lax.fori_loop(0, nb, block, 0)

    blk = pl.BlockSpec((m, HD), lambda h, r: (0, r * 4 + h))
    o, lse = pl.pallas_call(
        body, name=f"attn_a_fwd_{g}", out_shape=[SDS((m, dil * 512), F32)] * 2, grid=(4, dil),
        in_specs=[blk] * 3, out_specs=[blk] * 2,
        scratch_shapes=[pltpu.VMEM((m + 128, HD), BF16), pltpu.VMEM((m + 128, HD), BF16)],
        compiler_params=_params(2))(*_group_views(qkn, proj, g))
    return o.reshape(T, 512), lse.reshape(T, 512)


def _nbr_window(r):
    start = jnp.clip(r - WIN_R // 2, 0, T // GRID_W - WIN_R)
    return start, start - r + (WIN_R - 1)


def _attn_b_fwd(qkn, proj, rpb_flat):
    def body(rpb_ref, q_ref, k_ref, v_ref, o_ref, l_ref, bias_ref, vb):
        h = pl.program_id(0)
        qc = lax.broadcasted_iota(jnp.int32, (GRID_W, 512), 0)
        lane = lax.broadcasted_iota(jnp.int32, (GRID_W, 512), 1)
        kc = lane & (GRID_W - 1)
        dc = jnp.clip(kc - qc, -(WIN_C - 1), WIN_C - 1) + (WIN_C - 1)
        cs = jnp.clip(qc - WIN_C // 2, 0, GRID_W - WIN_C)
        colmask = (kc >= cs) & (kc < cs + WIN_C)
        jrow = lax.broadcasted_iota(jnp.int32, (1, 512), 1) >> 6
        for off in range(8):
            bias_ref[off] = jnp.zeros((GRID_W, 512), F32)
        for e in range(31):
            sel = dc == e
            for off in range(8):
                v = jnp.zeros((1, 512), F32)
                for j in range(8):
                    v = jnp.where(jrow == j, rpb_ref[h * NRPB + (off + j) * 31 + e], v)
                bias_ref[off] = jnp.where(sel, v, bias_ref[off])
        for off in range(8):
            bias_ref[off] = jnp.where(colmask, bias_ref[off], NEG)
        vb[...] = v_ref[...].astype(BF16)

        def row(r, carry):
            start, off = _nbr_window(r)
            q0 = pl.multiple_of(r * GRID_W, GRID_W)
            k0 = pl.multiple_of(start * GRID_W, GRID_W)
            s = _dot_nt(q_ref[pl.ds(q0, GRID_W), :], k_ref[pl.ds(k0, 512), :]) * SCALE + bias_ref[off]
            mx = jnp.max(s, axis=-1, keepdims=True)
            e = jnp.exp(s - mx)
            den = jnp.sum(e, axis=-1, keepdims=True)
            o_ref[pl.ds(q0, GRID_W), :] = _dot((e / den).astype(BF16), vb[pl.ds(k0, 512), :])
            l_ref[pl.ds(q0, GRID_W), :] = jnp.broadcast_to(mx + jnp.log(den), (GRID_W, HD))
            return carry

        lax.fori_loop(0, T // GRID_W, row, 0)

    return pl.pallas_call(
        body, name="attn_b_fwd",
        out_shape=[SDS((T, 512), F32), SDS((T, 512), F32), SDS((4, 8, GRID_W, 512), F32)], grid=(4,),
        in_specs=[pl.BlockSpec(memory_space=pltpu.SMEM),
                  pl.BlockSpec((T, HD), lambda h: (0, NHA + h)),
                  pl.BlockSpec((T, HD), lambda h: (0, NH + NHA + h)),
                  pl.BlockSpec((T, HD), lambda h: (0, 2 * NH + NHA + h))],
        out_specs=[pl.BlockSpec((T, HD), lambda h: (0, h)), pl.BlockSpec((T, HD), lambda h: (0, h)),
                   pl.BlockSpec((None, 8, GRID_W, 512), lambda h: (h, 0, 0, 0))],
        scratch_shapes=[pltpu.VMEM((T, HD), BF16)],
        compiler_params=_params(1))(rpb_flat, qkn, qkn, proj)


def _comb_fwd(os, ls):
    tm = 512

    def body(o0, o1, o2, l0, l1, l2, oa_ref, w0, w1, w2):
        lv = [l0[...], l1[...], l2[...]]
        mx = jnp.maximum(jnp.maximum(lv[0], lv[1]), lv[2])
        ev = [jnp.exp(l - mx) for l in lv]
        den = ev[0] + ev[1] + ev[2]
        wv = [e / den for e in ev]
        oa_ref[...] = (wv[0] * o0[...] + wv[1] * o1[...] + wv[2] * o2[...]).astype(BF16)
        w0[...], w1[...], w2[...] = wv

    spec = pl.BlockSpec((tm, 512), lambda i: (i, 0))
    return pl.pallas_call(
        body, name="comb_fwd", out_shape=[SDS((T, 512), BF16)] + [SDS((T, 512), F32)] * 3, grid=(T // tm,),
        in_specs=[spec] * 6, out_specs=[spec] * 4, compiler_params=_params(1))(*os, *ls)


def _mix_fwd(oa, ob, proj, b_gate, wpa, wpb):
    tm = 256

    def body(oa_ref, ob_ref, ga_ref, gb_ref, ba_ref, bb_ref, wpa_ref, wpb_ref, mixed_ref, ob16_ref):
        oav = oa_ref[...]
        obv = ob_ref[...].astype(BF16)
        ob16_ref[...] = obv
        for s in range(NSH):
            sl = slice(s * 512, (s + 1) * 512)
            ga = _sigmoid(ga_ref[:, sl] + ba_ref[:, sl])
            gb = _sigmoid(gb_ref[:, sl] + bb_ref[:, sl])
            mixed_ref[:, sl] = (ga * _dot(oav, wpa_ref[s]) + gb * _dot(obv, wpb_ref[s])).astype(BF16)

    row = lambda w: pl.BlockSpec((tm, w), lambda i: (i, 0))
    return pl.pallas_call(
        body, name="mix_fwd", out_shape=[SDS((T, D), BF16), SDS((T, 512), BF16)], grid=(T // tm,),
        in_specs=[row(512), row(512),
                  pl.BlockSpec((tm, D), lambda i: (i, 3)), pl.BlockSpec((tm, D), lambda i: (i, 4)),
                  pl.BlockSpec((1, D), lambda i: (0, 0)), pl.BlockSpec((1, D), lambda i: (0, 1)),
                  _resident((NSH, 512, 512), lambda i: (0, 0, 0)), _resident((NSH, 512, 512), lambda i: (0, 0, 0))],
        out_specs=[row(D), row(512)], compiler_params=_params(1))(oa, ob, proj, proj, b_gate, b_gate, wpa, wpb)


def _out_proj_fwd(mixed, w_out, x, g):
    tm = 256

    def body(m_ref, w_ref, x_ref, g_ref, h1_ref, hn_ref):
        h1 = x_ref[...] + _dot(m_ref[...], w_ref[...])
        h1_ref[...] = h1
        r = lax.rsqrt(jnp.mean(h1 * h1, axis=-1, keepdims=True) + EPS)
        hn_ref[...] = (h1 * r * g_ref[...]).astype(BF16)

    row = pl.BlockSpec((tm, D), lambda i: (i, 0))
    return pl.pallas_call(
        body, name="out_proj_fwd", out_shape=[SDS((T, D), F32), SDS((T, D), BF16)], grid=(T // tm,),
        in_specs=[row, _resident((D, D), lambda i: (0, 0)), row, pl.BlockSpec((1, D), lambda i: (0, 0))],
        out_specs=[row, row], compiler_params=_params(1))(mixed, w_out, x, g)


def _ffn_up(hn, w_up):
    tm, tn = 1024, 512
    per = (DFF // NSH) // tn

    def body(h_ref, w_ref, u_ref):
        u_ref[...] = jnp.maximum(_dot(h_ref[...], w_ref[...]), 0.0)

    return pl.pallas_call(
        body, name="ffn_up", out_shape=SDS((T, DFF), F32), grid=(T // tm, DFF // tn),
        in_specs=[pl.BlockSpec((tm, D), lambda i, j: (i, 0)),
                  pl.BlockSpec((None, D, tn), lambda i, j: (j // per, 0, j % per))],
        out_specs=pl.BlockSpec((tm, tn), lambda i, j: (i, j)), compiler_params=_params(2))(hn, w_up)


def _ffn_down_loss(u, w_down, h1, target):
    tm, tk = 512, 512
    nk = DFF // tk

    def body(u_ref, w_ref, h1_ref, t_ref, dy_ref, dy16_ref, loss_ref, acc):
        k = pl.program_id(1)

        @pl.when(k == 0)
        def _():
            acc[...] = jnp.zeros_like(acc)

        uv = u_ref[...]
        acc[...] += _dot((uv * uv).astype(BF16), w_ref[...])

        @pl.when(k == nk - 1)
        def _():
            err = acc[...] + h1_ref[...] - t_ref[...]
            dy = err * (1.0 / D)
            dy_ref[...] = dy
            dy16_ref[...] = dy.astype(BF16)
            part = 0.5 * jnp.sum(jnp.mean(err * err, axis=-1, keepdims=True), axis=0, keepdims=True)
            loss_ref[...] = jnp.broadcast_to(part, (8, 128))

    row = pl.BlockSpec((tm, D), lambda i, k: (i, 0))
    return pl.pallas_call(
        body, name="ffn_down_loss",
        out_shape=[SDS((T, D), F32), SDS((T, D), BF16), SDS((T // tm, 8, 128), F32)], grid=(T // tm, nk),
        in_specs=[pl.BlockSpec((tm, tk), lambda i, k: (i, k)), pl.BlockSpec((tk, D), lambda i, k: (k, 0)), row, row],
        out_specs=[row, row, pl.BlockSpec((None, 8, 128), lambda i, k: (i, 0, 0))],
        scratch_shapes=[pltpu.VMEM((tm, D), F32)], compiler_params=_params(2))(u, w_down, h1, target)


def _ffn_down_bwd(dy16, w_down, u, deps=()):
    tm, tn = 1024, 512

    def body(dy_ref, w_ref, u_ref, du_ref):
        uv = u_ref[...]
        du_ref[...] = jnp.where(uv > 0.0, 2.0 * uv * _dot_nt(dy_ref[...], w_ref[...]), 0.0).astype(BF16)

    return pl.pallas_call(
        _after(body, deps), name="ffn_down_bwd", out_shape=SDS((T, DFF), BF16), grid=(T // tm, DFF // tn),
        in_specs=[DEP_SPEC] * len(deps) + [
            pl.BlockSpec((tm, D), lambda i, j: (i, 0)), pl.BlockSpec((tn, D), lambda i, j: (j, 0)),
            pl.BlockSpec((tm, tn), lambda i, j: (i, j))],
        out_specs=pl.BlockSpec((tm, tn), lambda i, j: (i, j)), compiler_params=_params(2))(*deps, dy16, w_down, u)


def _norm_bwd(xv, dz_in, g):
    r = lax.rsqrt(jnp.mean(xv * xv, axis=-1, keepdims=True) + EPS)
    dg = jnp.sum(xv * r * dz_in, axis=0, keepdims=True)
    dz = dz_in * g
    dx = r * dz - xv * (r * r * r) * jnp.mean(xv * dz, axis=-1, keepdims=True)
    return dx, dg


def _ffn_up_bwd(du, w_up, h1, dy, g, deps=()):
    tm, tk = 512, 1024
    per = (DFF // NSH) // tk
    nk = DFF // tk

    def body(du_ref, w_ref, h1_ref, dy_ref, g_ref, dh1_ref, dh16_ref, dg_ref, acc):
        i, k = pl.program_id(0), pl.program_id(1)

        @pl.when(k == 0)
        def _():
            acc[...] = jnp.zeros_like(acc)

        @pl.when((k == 0) & (i == 0))
        def _():
            dg_ref[...] = jnp.zeros_like(dg_ref)

        acc[...] += _dot_nt(du_ref[...], w_ref[...])

        @pl.when(k == nk - 1)
        def _():
            dx, dg = _norm_bwd(h1_ref[...], acc[...], g_ref[...])
            dh1 = dy_ref[...] + dx
            dh1_ref[...] = dh1
            dh16_ref[...] = dh1.astype(BF16)
            dg_ref[...] += dg

    row = pl.BlockSpec((tm, D), lambda i, k: (i, 0))
    vec = pl.BlockSpec((1, D), lambda i, k: (0, 0))
    return pl.pallas_call(
        _after(body, deps), name="ffn_up_bwd", out_shape=[SDS((T, D), F32), SDS((T, D), BF16), SDS((1, D), F32)],
        grid=(T // tm, nk),
        in_specs=[DEP_SPEC] * len(deps) + [
            pl.BlockSpec((tm, tk), lambda i, k: (i, k)),
            pl.BlockSpec((None, D, tk), lambda i, k: (k // per, 0, k % per)), row, row, vec],
        out_specs=[row, row, vec], scratch_shapes=[pltpu.VMEM((tm, D), F32)],
        compiler_params=_params(2))(*deps, du, w_up, h1, dy, g)


def _mix_bwd(dh16, w_out, oa, ob16, proj, b_gate, wpa, wpb):
    tm = 128

    def body(dh_ref, wo_ref, oa_ref, ob_ref, ga_ref, gb_ref, ba_ref, bb_ref, wpa_ref, wpb_ref,
             dya_ref, dyb_ref, dga_ref, dgb_ref, doa_ref, dob_ref, dba_ref, dbb_ref):
        @pl.when(pl.program_id(0) == 0)
        def _():
            dba_ref[...] = jnp.zeros_like(dba_ref)
            dbb_ref[...] = jnp.zeros_like(dbb_ref)

        oav, obv = oa_ref[...], ob_ref[...]
        doa = jnp.zeros((tm, 512), F32)
        dob = jnp.zeros((tm, 512), F32)
        for s in range(NSH):
            sl = slice(s * 512, (s + 1) * 512)
            dm = _dot_nt(dh_ref[...], wo_ref[sl, :])
            ga = _sigmoid(ga_ref[:, sl] + ba_ref[:, sl])
            gb = _sigmoid(gb_ref[:, sl] + bb_ref[:, sl])
            dya = (dm * ga).astype(BF16)
            dyb = (dm * gb).astype(BF16)
            dza = dm * _dot(oav, wpa_ref[s]) * ga * (1.0 - ga)
            dzb = dm * _dot(obv, wpb_ref[s]) * gb * (1.0 - gb)
            dya_ref[:, sl], dyb_ref[:, sl] = dya, dyb
            dga_ref[:, sl], dgb_ref[:, sl] = dza.astype(BF16), dzb.astype(BF16)
            dba_ref[:, sl] += jnp.sum(dza, axis=0, keepdims=True)
            dbb_ref[:, sl] += jnp.sum(dzb, axis=0, keepdims=True)
            doa += _dot_nt(dya, wpa_ref[s])
            dob += _dot_nt(dyb, wpb_ref[s])
        doa_ref[...], dob_ref[...] = doa, dob

    row = lambda w: pl.BlockSpec((tm, w), lambda i: (i, 0))
    vec = pl.BlockSpec((1, D), lambda i: (0, 0))
    wp = _resident((NSH, 512, 512), lambda i: (0, 0, 0))
    return pl.pallas_call(
        body, name="mix_bwd",
        out_shape=[SDS((T, D), BF16)] * 4 + [SDS((T, 512), F32)] * 2 + [SDS((1, D), F32)] * 2, grid=(T // tm,),
        in_specs=[row(D), _resident((D, D), lambda i: (0, 0)), row(512), row(512),
                  pl.BlockSpec((tm, D), lambda i: (i, 3)), pl.BlockSpec((tm, D), lambda i: (i, 4)),
                  pl.BlockSpec((1, D), lambda i: (0, 0)), pl.BlockSpec((1, D), lambda i: (0, 1)), wp, wp],
        out_specs=[row(D)] * 4 + [row(512)] * 2 + [vec] * 2,
        compiler_params=_params(1))(dh16, w_out, oa, ob16, proj, proj, b_gate, b_gate, wpa, wpb)


def _comb_bwd(doa, os, ws, deps=()):
    tm = 512

    def body(d_ref, o0, o1, o2, w0, w1, w2, cc_ref):
        prod = d_ref[...] * (w0[...] * o0[...] + w1[...] * o1[...] + w2[...] * o2[...])
        for h in range(4):
            sl = slice(h * HD, (h + 1) * HD)
            cc_ref[:, sl] = jnp.broadcast_to(jnp.sum(prod[:, sl], axis=-1, keepdims=True), (tm, HD))

    spec = pl.BlockSpec((tm, 512), lambda i: (i, 0))
    return pl.pallas_call(
        _after(body, deps), name="comb_bwd", out_shape=SDS((T, 512), F32), grid=(T // tm,),
        in_specs=[DEP_SPEC] * len(deps) + [spec] * 7, out_specs=spec,
        compiler_params=_params(1))(*deps, doa, *os, *ws)


def _attn_a_bwd(qkn, proj, doa, lse, w, cc, g):
    dil = DILS[g]
    m = T // dil
    nb = m // 128

    def body(q_ref, k_ref, v_ref, d_ref, l_ref, w_ref, c_ref, dqk_ref, dv_ref, kp, vp, dkp, dvp):
        _fill_padded(kp, k_ref[...], m)
        _fill_padded(vp, v_ref[...], m)
        dkp[...] = jnp.zeros_like(dkp)
        dvp[...] = jnp.zeros_like(dvp)

        def block(b, carry):
            q0 = pl.multiple_of(b * 128, 128)
            rows = pl.ds(q0, 128)
            win = pl.ds(q0, 256)
            qb, kw, vw = q_ref[rows, :], kp[win, :], vp[win, :]
            s = _dot_nt(qb, kw) * SCALE
            s = jnp.where(_band_mask(q0, m), s, NEG)
            wp = _wide(w_ref[rows, :], 2) * jnp.exp(s - _wide(l_ref[rows, :], 2))
            dob = d_ref[rows, :].astype(BF16)
            ds = (wp * (_dot_nt(dob, vw) - _wide(c_ref[rows, :], 2))).astype(BF16)
            dqk_ref[0, rows, :] = _dot(ds, kw) * SCALE
            dkp[win, :] += _dot_tn(ds, qb) * SCALE
            dvp[win, :] += _dot_tn(wp.astype(BF16), dob)
            return carry

        lax.fori_loop(0, nb, block, 0)
        dqk_ref[1] = dkp[64:64 + m, :]
        dv_ref[...] = dvp[64:64 + m, :]

    blk = pl.BlockSpec((m, HD), lambda h, r: (0, r * 4 + h))
    view = lambda a: a.reshape(m, dil * 512)
    dqk, dv = pl.pallas_call(
        body, name=f"attn_a_bwd_{g}", out_shape=[SDS((2, m, dil * 512), F32), SDS((m, dil * 512), F32)], grid=(4, dil),
        in_specs=[blk] * 7,
        out_specs=[pl.BlockSpec((2, m, HD), lambda h, r: (0, 0, r * 4 + h)), blk],
        scratch_shapes=[pltpu.VMEM((m + 128, HD), BF16), pltpu.VMEM((m + 128, HD), BF16),
                        pltpu.VMEM((m + 128, HD), F32), pltpu.VMEM((m + 128, HD), F32)],
        compiler_params=_params(2))(*_group_views(qkn, proj, g), view(doa), view(lse), view(w), view(cc))
    return dqk.reshape(2, T, 512), dv.reshape(T, 512)


def _attn_b_bwd(qkn, proj, dob, ob, lse, bias):
    def body(q_ref, k_ref, v_ref, d_ref, o_ref, l_ref, bias_ref, dqk_ref, dv_ref, drpb_ref, vb, dk_acc, dv_acc, a_acc):
        vb[...] = v_ref[...].astype(BF16)
        dk_acc[...] = jnp.zeros_like(dk_acc)
        dv_acc[...] = jnp.zeros_like(dv_acc)
        a_acc[...] = jnp.zeros_like(a_acc)

        def row(r, carry):
            start, off = _nbr_window(r)
            rows = pl.ds(pl.multiple_of(r * GRID_W, GRID_W), GRID_W)
            win = pl.ds(pl.multiple_of(start * GRID_W, GRID_W), 512)
            qr, kw, vw = q_ref[rows, :], k_ref[win, :], vb[win, :]
            s = _dot_nt(qr, kw) * SCALE + bias_ref[off]
            p = jnp.exp(s - _wide(l_ref[rows, :], 4))
            dov = d_ref[rows, :]
            delta = jnp.sum(dov * o_ref[rows, :], axis=-1, keepdims=True)
            do16 = dov.astype(BF16)
            ds = p * (_dot_nt(do16, vw) - delta)
            a_acc[off] += ds
            ds16 = ds.astype(BF16)
            dqk_ref[0, rows, :] = _dot(ds16, kw) * SCALE
            dk_acc[win, :] += _dot_tn(ds16, qr) * SCALE
            dv_acc[win, :] += _dot_tn(p.astype(BF16), do16)
            return carry

        lax.fori_loop(0, T // GRID_W, row, 0)
        dqk_ref[1] = dk_acc[...]
        dv_ref[...] = dv_acc[...]

        qc = lax.broadcasted_iota(jnp.int32, (GRID_W, 512), 0)
        lane = lax.broadcasted_iota(jnp.int32, (GRID_W, 512), 1)
        dc = jnp.clip((lane & (GRID_W - 1)) - qc, -(WIN_C - 1), WIN_C - 1) + (WIN_C - 1)
        jrow = lax.broadcasted_iota(jnp.int32, (1, 512), 1) >> 6
        dlane = lax.broadcasted_iota(jnp.int32, (1, HD), 1)
        drpb_ref[...] = jnp.zeros_like(drpb_ref)

        def per_dc(e, carry):
            sel = dc == e
            out = jnp.zeros((1, HD), F32)
            for off in range(8):
                col = jnp.sum(jnp.where(sel, a_acc[off], 0.0), axis=0, keepdims=True)
                for j in range(8):
                    part = jnp.sum(jnp.where(jrow == j, col, 0.0), axis=-1, keepdims=True)
                    out = out + jnp.where(dlane == off + j, part, 0.0)
            drpb_ref[pl.ds(e, 1), :] = out
            return carry

        lax.fori_loop(0, 31, per_dc, 0)

    blk = pl.BlockSpec((T, HD), lambda h: (0, h))
    return pl.pallas_call(
        body, name="attn_b_bwd",
        out_shape=[SDS((2, T, 512), F32), SDS((T, 512), F32), SDS((4, 32, HD), F32)], grid=(4,),
        in_specs=[pl.BlockSpec((T, HD), lambda h: (0, NHA + h)),
                  pl.BlockSpec((T, HD), lambda h: (0, NH + NHA + h)),
                  pl.BlockSpec((T, HD), lambda h: (0, 2 * NH + NHA + h)), blk, blk, blk,
                  pl.BlockSpec((None, 8, GRID_W, 512), lambda h: (h, 0, 0, 0))],
        out_specs=[pl.BlockSpec((2, T, HD), lambda h: (0, 0, h)), blk,
                   pl.BlockSpec((None, 32, HD), lambda h: (h, 0, 0))],
        scratch_shapes=[pltpu.VMEM((T, HD), BF16), pltpu.VMEM((T, HD), F32), pltpu.VMEM((T, HD), F32),
                        pltpu.VMEM((8, GRID_W, 512), F32)],
        compiler_params=_params(1))(qkn, qkn, proj, dob, ob, lse, bias)


def _qk_bwd(proj, nw, cos, sin, dqk_groups, dqk_b):
    tm = 256

    def body(p_ref, w_ref, cos_ref, sin_ref, d0, d1, d2, d3, o_ref, dn_ref):
        @pl.when(pl.program_id(1) == 0)
        def _():
            dn_ref[...] = jnp.zeros_like(dn_ref)

        cv, sv = cos_ref[...], sin_ref[...]
        srcs = (d0, d1, d2, d3)
        dna = jnp.zeros((1, HD), F32)
        dnb = jnp.zeros((1, HD), F32)
        for h in range(NH):
            sl = slice(h * HD, (h + 1) * HD)
            dz = srcs[h // 4][:, (h % 4) * HD:(h % 4 + 1) * HD]
            if h < NHA:
                dz = dz * cv + pltpu.roll(dz * sv, 64, 1)
            dx, dg = _norm_bwd(p_ref[:, sl], dz, w_ref[:, sl])
            o_ref[:, sl] = dx.astype(BF16)
            if h < NHA:
                dna += dg
            else:
                dnb += dg
        dn_ref[0:1, :] += dna
        dn_ref[1:2, :] += dnb

    dspec = pl.BlockSpec((None, tm, 512), lambda j, i: (j, i, 0))
    return pl.pallas_call(
        body, name="qk_bwd", out_shape=[SDS((T, 2 * D), BF16), SDS((2, 8, HD), F32)], grid=(2, T // tm),
        in_specs=[pl.BlockSpec((tm, D), lambda j, i: (i, j)),
                  pl.BlockSpec((None, 1, D), lambda j, i: (j, 0, 0)),
                  pl.BlockSpec((tm, HD), lambda j, i: (i, 0)),
                  pl.BlockSpec((tm, HD), lambda j, i: (i, 0)), dspec, dspec, dspec, dspec],
        out_specs=[pl.BlockSpec((tm, D), lambda j, i: (i, j)), pl.BlockSpec((None, 8, HD), lambda j, i: (j, 0, 0))],
        compiler_params=_params(2))(proj, nw, cos, sin, *dqk_groups, dqk_b)


def _in_proj_bwd(dproj, w_in, x, dh1, g, deps=()):
    tm, tk = 512, 1280
    per = (DIN // NSH) // tk
    nk = DIN // tk

    def body(dp_ref, w_ref, x_ref, dh_ref, g_ref, dx_ref, dg_ref, acc):
        i, k = pl.program_id(0), pl.program_id(1)

        @pl.when(k == 0)
        def _():
            acc[...] = jnp.zeros_like(acc)

        @pl.when((k == 0) & (i == 0))
        def _():
            dg_ref[...] = jnp.zeros_like(dg_ref)

        acc[...] += _dot_nt(dp_ref[...], w_ref[...])

        @pl.when(k == nk - 1)
        def _():
            dx, dg = _norm_bwd(x_ref[...], acc[...], g_ref[...])
            dx_ref[...] = dh_ref[...] + dx
            dg_ref[...] += dg

    row = pl.BlockSpec((tm, D), lambda i, k: (i, 0))
    vec = pl.BlockSpec((1, D), lambda i, k: (0, 0))
    return pl.pallas_call(
        _after(body, deps), name="in_proj_bwd", out_shape=[SDS((T, D), F32), SDS((1, D), F32)], grid=(T // tm, nk),
        in_specs=[DEP_SPEC] * len(deps) + [
            pl.BlockSpec((tm, tk), lambda i, k: (i, k)),
            pl.BlockSpec((None, D, tk), lambda i, k: (k // per, 0, k % per)), row, row, vec],
        out_specs=[row, vec], scratch_shapes=[pltpu.VMEM((tm, D), F32)],
        compiler_params=_params(2))(*deps, dproj, w_in, x, dh1, g)


def _grad_w(name, a, g, shard_rows, rows, cols, tr, tc, square=False):
    ni, nj = rows // tr, cols // tc
    if shard_rows:
        a_map, g_map = (lambda s, i, j: (0, s * ni + i)), (lambda s, i, j: (0, j))
    else:
        a_map, g_map = (lambda s, i, j: (0, i)), (lambda s, i, j: (0, s * nj + j))

    def body(a_ref, g_ref, o_ref):
        av = a_ref[...]
        if square:
            av = (av * av).astype(BF16)
        o_ref[...] = _dot_tn(av, g_ref[...]).astype(BF16)

    return pl.pallas_call(
        body, name=name, out_shape=SDS((NSH, rows, cols), BF16), grid=(NSH, ni, nj),
        in_specs=[pl.BlockSpec((T, tr), a_map), pl.BlockSpec((T, tc), g_map)],
        out_specs=pl.BlockSpec((None, tr, tc), lambda s, i, j: (s, i, j)), compiler_params=_params(3))(a, g)


def _adamw(w, g, m, v):
    m = B1 * m + (1.0 - B1) * g
    v = B2 * v + (1.0 - B2) * (g * g)
    m_hat = m / (1.0 - B1 ** STEP)
    v_hat = v / (1.0 - B2 ** STEP)
    delta = -LR * (m_hat / (jnp.sqrt(v_hat) + AEPS) + WD * w)
    return delta, m, v


def _sum_halves(name, mine, theirs):
    _, rows, cols = mine.shape
    tr = _row_tile(rows, cols, 1 << 17)

    def body(a_ref, b_ref, o_ref):
        o_ref[...] = (a_ref[...].astype(F32) + b_ref[...].astype(F32)).astype(BF16)

    spec = pl.BlockSpec((NSH, tr, cols), lambda i: (0, i, 0))
    return pl.pallas_call(
        body, name=name, out_shape=SDS(mine.shape, BF16), grid=(rows // tr,),
        in_specs=[spec, spec], out_specs=spec, compiler_params=_params(1))(mine, theirs)


def _sum_landed(name, me, part, landed):
    _, rows, cols = part.shape
    tr = _row_tile(rows, cols, 1 << 18)

    def body(me_ref, p_ref, l_ref, o_ref):
        o_ref[...] = ((p_ref[...].astype(F32) + l_ref[0].astype(F32)) + l_ref[1].astype(F32)) + l_ref[2].astype(F32)

    return pl.pallas_call(
        body, name=name, out_shape=SDS((rows, cols), F32),
        grid_spec=pltpu.PrefetchScalarGridSpec(
            num_scalar_prefetch=1, grid=(rows // tr,),
            in_specs=[pl.BlockSpec((None, tr, cols), lambda i, me_ref: (me_ref[0], i, 0)),
                      pl.BlockSpec((3, tr, cols), lambda i, me_ref: (0, i, 0))],
            out_specs=pl.BlockSpec((tr, cols), lambda i, me_ref: (i, 0))),
        compiler_params=_params(1))(me, part, landed)


def _adam_shard(name, g, w, m, v):
    rows, cols = w.shape
    tr = _row_tile(rows, cols, 1 << 18)

    def body(g_ref, w_ref, m_ref, v_ref, d_ref, nm_ref, nv_ref):
        d_ref[...], nm_ref[...], nv_ref[...] = _adamw(w_ref[...], g_ref[...], m_ref[...], v_ref[...])

    spec = pl.BlockSpec((tr, cols), lambda i: (i, 0))
    return pl.pallas_call(
        body, name=name, out_shape=[SDS((rows, cols), F32)] * 3, grid=(rows // tr,),
        in_specs=[spec] * 4, out_specs=[spec] * 3, compiler_params=_params(1))(g, w, m, v)


def _adam_small(gathered, w, m, v):
    def body(g_ref, w_ref, m_ref, v_ref, go_ref, d_ref, nm_ref, nv_ref):
        g = g_ref[0:SMALL_ROWS, :]
        for dev in range(1, 8):
            g = g + g_ref[dev * SMALL_ROWS:(dev + 1) * SMALL_ROWS, :]
        go_ref[...] = g
        d_ref[...], nm_ref[...], nv_ref[...] = _adamw(w_ref[...], g, m_ref[...], v_ref[...])

    return pl.pallas_call(body, name="adam_small", out_shape=[SDS((SMALL_ROWS, HD), F32)] * 4)(gathered, w, m, v)


SMALL = (("norm_mix", (1, D)), ("b_gate", (1, 2 * D)), ("q_norm_a", (1, HD)), ("k_norm_a", (1, HD)),
         ("q_norm_b", (1, HD)), ("k_norm_b", (1, HD)), ("rpb_b", (1, 4, 15, 31)), ("norm_ffn", (1, D)))


def _pack_small(vals):
    pieces = []
    for (name, shape), val in zip(SMALL, vals):
        flat = val.reshape(-1)
        pad = (-flat.shape[0]) % HD
        pieces.append(jnp.pad(flat, (0, pad)).reshape(-1, HD))
    packed = jnp.concatenate(pieces, axis=0)
    return jnp.pad(packed, ((0, SMALL_ROWS - packed.shape[0]), (0, 0)))


def _unpack_small(packed):
    out, row = [], 0
    for name, shape in SMALL:
        size = int(np.prod(shape))
        nrows = -(-size // HD)
        out.append(packed[row:row + nrows].reshape(-1)[:size].reshape(shape))
        row += nrows
    return out


def kernel(x, norm_mix, w_in, b_gate, q_norm_a, k_norm_a, q_norm_b, k_norm_b, rpb_b, w_proj_a, w_proj_b, w_out, norm_ffn, w_up, w_down, loss_target, m_norm_mix, m_w_in, m_b_gate, m_q_norm_a, m_k_norm_a, m_q_norm_b, m_k_norm_b, m_rpb_b, m_w_proj_a, m_w_proj_b, m_w_out, m_norm_ffn, m_w_up, m_w_down, v_norm_mix, v_w_in, v_b_gate, v_q_norm_a, v_k_norm_a, v_q_norm_b, v_k_norm_b, v_rpb_b, v_w_proj_a, v_w_proj_b, v_w_out, v_norm_ffn, v_w_up, v_w_down):
    big_names = ("w_in", "w_proj_a", "w_proj_b", "w_out", "w_up", "w_down")
    big_w = [a[0] for a in (w_in, w_proj_a, w_proj_b, w_out, w_up, w_down)]
    big_m = [a[0] for a in (m_w_in, m_w_proj_a, m_w_proj_b, m_w_out, m_w_up, m_w_down)]
    big_v = [a[0] for a in (v_w_in, v_w_proj_a, v_w_proj_b, v_w_out, v_w_up, v_w_down)]
    x2, target = x[0], loss_target[0]

    shards = [_cast_bf16(w, "cast_" + n) for n, w in zip(big_names, big_w)]
    me = (2 * lax.axis_index("x") + lax.axis_index("y")).astype(jnp.int32).reshape(1)
    groups = ((0,), (1, 2, 3), (4,), (5,))
    started = [_gather_start(f"gather_start_{j}", [shards[i] for i in grp]) for j, grp in enumerate(groups)]

    def gathered(j, after):
        send, recv, srcs, lands, _ = started[j]
        srcs, lands = _gather_wait(f"gather_wait_{j}", send, recv, srcs, lands, after)
        return _gather_finish(f"gather_finish_{j}", srcs, lands)

    def reduce_begin(j, grads):
        mine, theirs = _reduce_exchange(f"reduce_exchange_{j}", grads)
        parts = [_sum_halves(f"sum_halves_{j}_{i}", a, b) for i, (a, b) in enumerate(zip(mine, theirs))]
        send, recv, parts, lands, token = _reduce_start(f"reduce_start_{j}", parts)
        return (send, recv, parts, lands), token

    big_out = {}

    def reduce_end(j, state, after):
        send, recv, parts, lands = state
        parts, lands = _reduce_wait(f"reduce_wait_{j}", send, recv, parts, lands, after)
        sums = [_sum_landed(f"sum_landed_{j}_{i}", me, p, l) for i, (p, l) in enumerate(zip(parts, lands))]
        for idx, g in zip(groups[j], _reduce_share(f"reduce_share_{j}", sums)):
            big_out[idx] = (g, *_adam_shard("adam_" + big_names[idx], g, big_w[idx], big_m[idx], big_v[idx]))
        return big_out[groups[j][-1]][1]

    (win_f,) = gathered(0, ())
    proj, xn = _norm_in_proj(x2, norm_mix, win_f)
    cos, sin = _rope_tables()
    nw = jnp.stack([jnp.concatenate([jnp.tile(q_norm_a, (1, NHA)), jnp.tile(q_norm_b, (1, NH - NHA))], axis=1),
                    jnp.concatenate([jnp.tile(k_norm_a, (1, NHA)), jnp.tile(k_norm_b, (1, NH - NHA))], axis=1)])
    qkn = _qk_prep(proj, nw, cos, sin)
    wpa_f, wpb_f, wout_f = gathered(1, (qkn,))
    wout_f = wout_f.reshape(D, D)
    fwd_a = [_attn_a_fwd(qkn, proj, g) for g in range(3)]
    os, ls = [f[0] for f in fwd_a], [f[1] for f in fwd_a]
    ob, lse_b, bias = _attn_b_fwd(qkn, proj, rpb_b.reshape(-1))
    oa, w0, w1, w2 = _comb_fwd(os, ls)
    ws = [w0, w1, w2]
    mixed, ob16 = _mix_fwd(oa, ob, proj, b_gate, wpa_f, wpb_f)
    h1, hn = _out_proj_fwd(mixed, wout_f, x2, norm_ffn)
    (wup_f,) = gathered(2, (h1,))
    u = _ffn_up(hn, wup_f)
    (wdown_f,) = gathered(3, (u,))
    wdown_f = wdown_f.reshape(DFF, D)
    dy, dy16, loss_parts = _ffn_down_loss(u, wdown_f, h1, target)
    loss = lax.psum(jnp.sum(loss_parts[:, 0, 0]), ("x", "y", "c"))

    g_down = _grad_w("grad_w_down", u, dy16, True, DFF // NSH, D, 512, 1024, square=True)
    red_down, token = reduce_begin(3, [g_down])
    du = _ffn_down_bwd(dy16, wdown_f, u, deps=(token,))
    g_up = _grad_w("grad_w_up", hn, du, False, D, DFF // NSH, 1024, 1024)
    red_up, token = reduce_begin(2, [g_up])
    dh1, dh16, d_norm_ffn = _ffn_up_bwd(du, wup_f, h1, dy, norm_ffn, deps=(token,))
    dya, dyb, dga, dgb, doa, dob, dba, dbb = _mix_bwd(dh16, wout_f, oa, ob16, proj, b_gate, wpa_f, wpb_f)
    g_out = _grad_w("grad_w_out", mixed, dh16, True, D // NSH, D, 512, 1024)
    g_pa = _grad_w("grad_w_proj_a", oa, dya, False, 512, 512, 512, 512)
    g_pb = _grad_w("grad_w_proj_b", ob16, dyb, False, 512, 512, 512, 512)
    red_mid, token = reduce_begin(1, [g_pa, g_pb, g_out])
    cc = _comb_bwd(doa, os, ws, deps=(token,))
    bwd_a = [_attn_a_bwd(qkn, proj, doa, ls[g], ws[g], cc, g) for g in range(3)]
    dqk_b, dv_b, drpb_t = _attn_b_bwd(qkn, proj, dob, ob, lse_b, bias)
    dqk_pre, dn = _qk_bwd(proj, nw, cos, sin, [b[0] for b in bwd_a], dqk_b)
    dv16 = jnp.concatenate([b[1] for b in bwd_a] + [dv_b], axis=1).astype(BF16)
    dproj = jnp.concatenate([dqk_pre, dv16, dga, dgb], axis=1)
    g_in = _grad_w("grad_w_in", xn, dproj, False, D, DIN // NSH, 1024, 1280)
    red_in, token = reduce_begin(0, [g_in])
    grad_x, d_norm_mix = _in_proj_bwd(dproj, win_f, x2, dh1, norm_mix, deps=(token,))

    done = reduce_end(3, red_down, (grad_x,))
    done = reduce_end(2, red_up, (done,))
    done = reduce_end(1, red_mid, (done,))
    reduce_end(0, red_in, (done,))

    d_rpb = jnp.transpose(drpb_t[:, :31, :15], (0, 2, 1))
    small_g = [d_norm_mix, jnp.concatenate([dba, dbb], axis=1), dn[0, 0], dn[1, 0], dn[0, 1], dn[1, 1], d_rpb, d_norm_ffn]
    gathered = _allgather_small(_pack_small(small_g))
    small_w = (norm_mix, b_gate, q_norm_a, k_norm_a, q_norm_b, k_norm_b, rpb_b, norm_ffn)
    small_m = (m_norm_mix, m_b_gate, m_q_norm_a, m_k_norm_a, m_q_norm_b, m_k_norm_b, m_rpb_b, m_norm_ffn)
    small_v = (v_norm_mix, v_b_gate, v_q_norm_a, v_k_norm_a, v_q_norm_b, v_k_norm_b, v_rpb_b, v_norm_ffn)
    small_out = [_unpack_small(p) for p in
                 _adam_small(gathered, _pack_small(small_w), _pack_small(small_m), _pack_small(small_v))]

    order = ("norm_mix", "w_in", "b_gate", "q_norm_a", "k_norm_a", "q_norm_b", "k_norm_b", "rpb_b",
             "w_proj_a", "w_proj_b", "w_out", "norm_ffn", "w_up", "w_down")
    small_idx = {name: i for i, (name, _) in enumerate(SMALL)}
    outs = []
    for kind in range(4):
        for name in order:
            if name in small_idx:
                outs.append(small_out[kind][small_idx[name]])
            else:
                outs.append(big_out[big_names.index(name)][kind][None])
    return (loss, grad_x[None], *outs)
```

```python
import functools

import numpy as np
import jax
import jax.numpy as jnp
from jax import lax
from jax.experimental import pallas as pl
from jax.experimental.pallas import tpu as pltpu

F32, BF16 = jnp.float32, jnp.bfloat16
SDS = jax.ShapeDtypeStruct
MESH = pl.DeviceIdType.MESH

T = 2048
D = 2048
HD = 128
NH, NHA = 16, 12
DIN = 10240
DFF = 8192
NSH = 4
DILS = (1, 4, 16)
EPS = 1e-6
NEG = -1e30
SCALE = HD ** -0.5
GRID_W, WIN_R, WIN_C = 64, 8, 16
NRPB = 15 * 31
VMEM_LIMIT = 56 * 1024 * 1024
B1, B2, LR, AEPS, WD, STEP = 0.9, 0.999, 0.001, 1e-08, 0.01, 10
SMALL_ROWS = 88


def _dot(a, b):
    return jnp.dot(a, b, preferred_element_type=F32)


def _dot_nt(a, b):
    return lax.dot_general(a, b, (((1,), (1,)), ((), ())), preferred_element_type=F32)


def _dot_tn(a, b):
    return lax.dot_general(a, b, (((0,), (0,)), ((), ())), preferred_element_type=F32)


def _params(n):
    return pltpu.CompilerParams(dimension_semantics=("arbitrary",) * n, vmem_limit_bytes=VMEM_LIMIT)


def _resident(shape, index_map):
    return pl.BlockSpec(shape, index_map, pipeline_mode=pl.Buffered(1))


def _sigmoid(z):
    return 1.0 / (1.0 + jnp.exp(-z))


def _wide(v, n):
    return jnp.concatenate([v] * n, axis=1)


def _row_tile(rows, cols, elems):
    tr = 16
    while tr * 2 <= rows and tr * 2 * cols <= elems:
        tr *= 2
    return tr


def _place():
    x, y, c = lax.axis_index("x"), lax.axis_index("y"), lax.axis_index("c")
    peers = [(1 - x, y), (x, 1 - y), (1 - x, 1 - y)]
    return x, y, c, peers


def _cast_bf16(w, name):
    rows, cols = w.shape
    tr = min(rows, 256)

    def body(w_ref, o_ref):
        o_ref[...] = w_ref[...].astype(BF16)

    return pl.pallas_call(
        body, name=name, out_shape=SDS((rows, cols), BF16), grid=(rows // tr,),
        in_specs=[pl.BlockSpec((tr, cols), lambda i: (i, 0))],
        out_specs=pl.BlockSpec((tr, cols), lambda i: (i, 0)), compiler_params=_params(1))(w)


ANY_SPEC = pl.BlockSpec(memory_space=pl.ANY)
HBM_SPEC = pl.BlockSpec(memory_space=pltpu.HBM)
SEM_SPEC = pl.BlockSpec(memory_space=pltpu.SEMAPHORE)
DEP_SPEC = pl.BlockSpec((8, 128), lambda *_: (0, 0))
EFFECT = pltpu.SideEffectType.DATAFLOW_SIDE_EFFECTING


def _after(body, deps):
    n = len(deps)
    return (lambda *refs: body(*refs[n:])) if n else body


def _split_start(name, srcs, lands, n_copies, issue):
    n, m = len(srcs), len(lands)

    def body(*refs):
        issue(refs[:n], refs[n:n + m], refs[n + m], refs[n + m + 1])
        refs[-1][...] = jnp.zeros((8, 128), F32)

    arrays = list(srcs) + list(lands)
    outs = pl.pallas_call(
        body, name=name,
        out_shape=(pltpu.SemaphoreType.DMA((n_copies,)), pltpu.SemaphoreType.DMA((n_copies,)),
                   *[pltpu.HBM(a.shape, a.dtype) for a in arrays], SDS((8, 128), F32)),
        in_specs=[HBM_SPEC] * (n + m),
        out_specs=(SEM_SPEC, SEM_SPEC, *[HBM_SPEC] * (n + m), pl.BlockSpec(memory_space=pltpu.VMEM)),
        input_output_aliases={i: 2 + i for i in range(n + m)},
        compiler_params=pltpu.CompilerParams(has_side_effects=EFFECT),
    )(*[pltpu.with_memory_space_constraint(a, pltpu.HBM) for a in arrays])
    return outs[0], outs[1], list(outs[2:2 + n]), list(outs[2 + n:2 + n + m]), outs[-1]


def _split_wait(name, send_sems, recv_sems, srcs, lands, after, wait):
    n, m = len(srcs), len(lands)

    def body(*refs):
        wait(refs[:n], refs[n:n + m], refs[n + m], refs[n + m + 1])

    arrays = list(srcs) + list(lands)
    outs = pl.pallas_call(
        body, name=name, out_shape=[pltpu.HBM(a.shape, a.dtype) for a in arrays],
        in_specs=[HBM_SPEC] * (n + m) + [SEM_SPEC, SEM_SPEC] + [ANY_SPEC] * len(after),
        out_specs=[HBM_SPEC] * (n + m), input_output_aliases={i: i for i in range(n + m)},
        compiler_params=pltpu.CompilerParams(has_side_effects=EFFECT),
    )(*arrays, send_sems, recv_sems, *after)
    return list(outs[:n]), list(outs[n:])


def _gather_start(name, shards):
    lands = [lax.empty((NSH,) + s.shape, s.dtype) for s in shards]

    def issue(srcs, dsts, send_sems, recv_sems):
        x, y, c, peers = _place()
        for i in range(len(shards)):
            for k, (px, py) in enumerate(peers):
                pltpu.make_async_remote_copy(
                    src_ref=srcs[i].at[c], dst_ref=dsts[i].at[2 * x + y, c], send_sem=send_sems.at[3 * i + k],
                    recv_sem=recv_sems.at[3 * i + k], device_id=(px, py, c), device_id_type=MESH).start()

    return _split_start(name, shards, lands, 3 * len(shards), issue)


def _gather_wait(name, send_sems, recv_sems, shards, lands, after):
    def wait(srcs, dsts, send_sems, recv_sems):
        x, y, c, peers = _place()
        for i in range(len(shards)):
            for k, (px, py) in enumerate(peers):
                cp = pltpu.make_async_remote_copy(
                    src_ref=srcs[i].at[c], dst_ref=dsts[i].at[2 * px + py, c], send_sem=send_sems.at[3 * i + k],
                    recv_sem=recv_sems.at[3 * i + k], device_id=(px, py, c), device_id_type=MESH)
                cp.wait_send()
                cp.wait_recv()

    return _split_wait(name, send_sems, recv_sems, shards, lands, after, wait)


def _gather_finish(name, shards, fulls):
    n = len(shards)

    def body(*refs):
        srcs, fin, fout = refs[:n], refs[n:2 * n], refs[2 * n:3 * n]
        send_sems, recv_sems, local_sems = refs[3 * n:]
        x, y, c, peers = _place()
        local = [pltpu.make_async_copy(srcs[i], fout[i].at[2 * x + y], local_sems.at[i]) for i in range(n)]
        for cp in local:
            cp.start()

        def copy(i, k, half):
            px, py = peers[k]
            return pltpu.make_async_remote_copy(
                src_ref=fin[i].at[2 * px + py, half], dst_ref=fout[i].at[2 * px + py, half],
                send_sem=send_sems.at[3 * i + k], recv_sem=recv_sems.at[3 * i + k],
                device_id=(x, y, 1 - c), device_id_type=MESH)

        sends = [copy(i, k, c) for i in range(n) for k in range(3)]
        for cp in sends:
            cp.start()
        for i in range(n):
            for k in range(3):
                copy(i, k, 1 - c).wait_recv()
        for cp in sends:
            cp.wait_send()
        for cp in local:
            cp.wait()

    return pl.pallas_call(
        body, name=name, out_shape=[SDS(f.shape, f.dtype) for f in fulls],
        in_specs=[ANY_SPEC] * (2 * n), out_specs=[ANY_SPEC] * n, input_output_aliases={n + i: i for i in range(n)},
        scratch_shapes=[pltpu.SemaphoreType.DMA((3 * n,)), pltpu.SemaphoreType.DMA((3 * n,)),
                        pltpu.SemaphoreType.DMA((n,))])(*shards, *fulls)


def _reduce_exchange(name, grads):
    n = len(grads)

    def body(*refs):
        ins, mine, theirs = refs[:n], refs[n:2 * n], refs[2 * n:3 * n]
        send_sems, recv_sems, local_sems = refs[3 * n:]
        x, y, c, _ = _place()
        copies = []
        for i in range(n):
            local = pltpu.make_async_copy(ins[i].at[:, c], mine[i], local_sems.at[i])
            local.start()
            cp = pltpu.make_async_remote_copy(
                src_ref=ins[i].at[:, 1 - c], dst_ref=theirs[i], send_sem=send_sems.at[i],
                recv_sem=recv_sems.at[i], device_id=(x, y, 1 - c), device_id_type=MESH)
            cp.start()
            copies.append((local, cp))
        for local, cp in copies:
            cp.wait_recv()
            cp.wait_send()
            local.wait()

    halves = [SDS((NSH,) + g.shape[2:], g.dtype) for g in grads]
    outs = pl.pallas_call(
        body, name=name, out_shape=halves + halves, in_specs=[ANY_SPEC] * n, out_specs=[ANY_SPEC] * (2 * n),
        scratch_shapes=[pltpu.SemaphoreType.DMA((n,)), pltpu.SemaphoreType.DMA((n,)),
                        pltpu.SemaphoreType.DMA((n,))])(*grads)
    return list(outs[:n]), list(outs[n:])


def _reduce_start(name, parts):
    lands = [lax.empty((3,) + p.shape[1:], p.dtype) for p in parts]

    def issue(srcs, dsts, send_sems, recv_sems):
        x, y, c, peers = _place()
        for i in range(len(parts)):
            for k, (px, py) in enumerate(peers):
                pltpu.make_async_remote_copy(
                    src_ref=srcs[i].at[2 * px + py], dst_ref=dsts[i].at[k], send_sem=send_sems.at[3 * i + k],
                    recv_sem=recv_sems.at[3 * i + k], device_id=(px, py, c), device_id_type=MESH).start()

    return _split_start(name, parts, lands, 3 * len(parts), issue)


def _reduce_wait(name, send_sems, recv_sems, parts, lands, after):
    def wait(srcs, dsts, send_sems, recv_sems):
        x, y, c, peers = _place()
        for i in range(len(parts)):
            for k, (px, py) in enumerate(peers):
                cp = pltpu.make_async_remote_copy(
                    src_ref=srcs[i].at[2 * px + py], dst_ref=dsts[i].at[k], send_sem=send_sems.at[3 * i + k],
                    recv_sem=recv_sems.at[3 * i + k], device_id=(px, py, c), device_id_type=MESH)
                cp.wait_send()
                cp.wait_recv()

    return _split_wait(name, send_sems, recv_sems, parts, lands, after, wait)


def _reduce_share(name, sums):
    n = len(sums)

    def body(*refs):
        ins, outs = refs[:n], refs[n:2 * n]
        send_sems, recv_sems, local_sems = refs[2 * n:]
        x, y, c, _ = _place()
        copies = []
        for i in range(n):
            local = pltpu.make_async_copy(ins[i], outs[i].at[c], local_sems.at[i])
            local.start()
            cp = pltpu.make_async_remote_copy(
                src_ref=ins[i], dst_ref=outs[i].at[c], send_sem=send_sems.at[i], recv_sem=recv_sems.at[i],
                device_id=(x, y, 1 - c), device_id_type=MESH)
            cp.start()
            copies.append((local, cp))
        for i, (local, cp) in enumerate(copies):
            pltpu.make_async_remote_copy(
                src_ref=ins[i], dst_ref=outs[i].at[1 - c], send_sem=send_sems.at[i],
                recv_sem=recv_sems.at[i], device_id=(x, y, 1 - c), device_id_type=MESH).wait_recv()
            cp.wait_send()
            local.wait()

    return pl.pallas_call(
        body, name=name, out_shape=[SDS((2,) + s.shape, s.dtype) for s in sums],
        in_specs=[ANY_SPEC] * n, out_specs=[ANY_SPEC] * n,
        scratch_shapes=[pltpu.SemaphoreType.DMA((n,)), pltpu.SemaphoreType.DMA((n,)),
                        pltpu.SemaphoreType.DMA((n,))])(*sums)


def _allgather_small(v):
    m_per, n = v.shape

    def body(x_ref, out_ref, send_sems, recv_sems, local_sem):
        x, y, c = lax.axis_index("x"), lax.axis_index("y"), lax.axis_index("c")
        me, sibling = (x, y, c), (x, y, 1 - c)
        chips = [(1 - x, y), (x, 1 - y), (1 - x, 1 - y)]

        def rows(px, py, pc):
            return out_ref.at[pl.ds((4 * px + 2 * py + pc) * m_per, m_per), :]

        def copy(k, block, to, src=None):
            return pltpu.make_async_remote_copy(
                src_ref=rows(*block) if src is None else src, dst_ref=rows(*block),
                send_sem=send_sems.at[k], recv_sem=recv_sems.at[k], device_id=to, device_id_type=MESH)

        mine = pltpu.make_async_copy(x_ref, rows(*me), local_sem)
        mine.start()
        first = [copy(0, me, sibling, src=x_ref)]
        first += [copy(1 + j, me, (*chip, c), src=x_ref) for j, chip in enumerate(chips)]
        for cp in first:
            cp.start()
        passed = [copy(4 + j, (*chip, c), sibling) for j, chip in enumerate(chips)]
        for j, chip in enumerate(chips):
            copy(1 + j, (*chip, c), me).wait_recv()
            passed[j].start()
        copy(0, sibling, me).wait_recv()
        for j, chip in enumerate(chips):
            copy(4 + j, (*chip, 1 - c), me).wait_recv()
        for cp in first + passed:
            cp.wait_send()
        mine.wait()

    return pl.pallas_call(
        body, name="allgather_small", out_shape=SDS((8 * m_per, n), v.dtype),
        in_specs=[pl.BlockSpec(memory_space=pltpu.VMEM)], out_specs=pl.BlockSpec(memory_space=pltpu.VMEM),
        scratch_shapes=[pltpu.SemaphoreType.DMA((7,)), pltpu.SemaphoreType.DMA((7,)), pltpu.SemaphoreType.DMA])(v)


def _norm_in_proj(x, g, w_full):
    tm, tn = 512, 512
    per = (DIN // NSH) // tn

    def body(x_ref, g_ref, w_ref, proj_ref, xn_ref):
        @pl.when(pl.program_id(1) == 0)
        def _():
            xv = x_ref[...]
            r = lax.rsqrt(jnp.mean(xv * xv, axis=-1, keepdims=True) + EPS)
            xn_ref[...] = (xv * r * g_ref[...]).astype(BF16)

        proj_ref[...] = _dot(xn_ref[...], w_ref[...])

    return pl.pallas_call(
        body, name="norm_in_proj", out_shape=[SDS((T, DIN), F32), SDS((T, D), BF16)],
        grid=(T // tm, DIN // tn),
        in_specs=[pl.BlockSpec((tm, D), lambda i, j: (i, 0)),
                  pl.BlockSpec((1, D), lambda i, j: (0, 0)),
                  pl.BlockSpec((None, D, tn), lambda i, j: (j // per, 0, j % per))],
        out_specs=[pl.BlockSpec((tm, tn), lambda i, j: (i, j)),
                   pl.BlockSpec((tm, D), lambda i, j: (i, 0))],
        compiler_params=_params(2))(x, g, w_full)


def _rope_tables():
    pos = np.arange(T, dtype=np.float32)
    inv = (10000.0 ** (-np.arange(0, HD, 2, dtype=np.float32) / HD)).astype(np.float32)
    ang = (pos[:, None] * inv[None, :]).astype(np.float32)
    cos, sin = np.cos(ang).astype(np.float32), np.sin(ang).astype(np.float32)
    return (jnp.asarray(np.concatenate([cos, cos], axis=1)), jnp.asarray(np.concatenate([-sin, sin], axis=1)))


def _qk_prep(proj, nw, cos, sin):
    tm = 256

    def body(p_ref, w_ref, cos_ref, sin_ref, o_ref):
        cv, sv = cos_ref[...], sin_ref[...]
        for h in range(NH):
            sl = slice(h * HD, (h + 1) * HD)
            xv = p_ref[:, sl]
            r = lax.rsqrt(jnp.mean(xv * xv, axis=-1, keepdims=True) + EPS)
            z = xv * r * w_ref[:, sl]
            if h < NHA:
                z = z * cv + pltpu.roll(z, 64, 1) * sv
            o_ref[:, sl] = z.astype(BF16)

    return pl.pallas_call(
        body, name="qk_prep", out_shape=SDS((T, 2 * D), BF16), grid=(T // tm, 2),
        in_specs=[pl.BlockSpec((tm, D), lambda i, j: (i, j)),
                  pl.BlockSpec((None, 1, D), lambda i, j: (j, 0, 0)),
                  pl.BlockSpec((tm, HD), lambda i, j: (i, 0)),
                  pl.BlockSpec((tm, HD), lambda i, j: (i, 0))],
        out_specs=pl.BlockSpec((tm, D), lambda i, j: (i, j)),
        compiler_params=_params(2))(proj, nw, cos, sin)


def _band_mask(q0, m):
    ii = lax.broadcasted_iota(jnp.int32, (128, 256), 0)
    jj = lax.broadcasted_iota(jnp.int32, (128, 256), 1)
    rel = jj - ii
    kpos = jj + (q0 - 64)
    return (rel >= 0) & (rel <= 128) & (kpos >= 0) & (kpos < m)


def _fill_padded(dst, src, m):
    zeros = jnp.zeros((64, HD), dst.dtype)
    dst[0:64, :] = zeros
    dst[64 + m:128 + m, :] = zeros
    dst[64:64 + m, :] = src.astype(dst.dtype)


def _group_views(qkn, proj, g):
    m = T // DILS[g]
    cols = (qkn[:, g * 512:(g + 1) * 512], qkn[:, D + g * 512:D + (g + 1) * 512],
            proj[:, 2 * D + g * 512:2 * D + (g + 1) * 512])
    return [a.reshape(m, DILS[g] * 512) for a in cols]


def _attn_a_fwd(qkn, proj, g):
    dil = DILS[g]
    m = T // dil
    nb = m // 128

    def body(q_ref, k_ref, v_ref, o_ref, l_ref, kp, vp):
        _fill_padded(kp, k_ref[...], m)
        _fill_padded(vp, v_ref[...], m)

        def block(b, carry):
            q0 = pl.multiple_of(b * 128, 128)
            kw, vw = kp[pl.ds(q0, 256), :], vp[pl.ds(q0, 256), :]
            s = _dot_nt(q_ref[pl.ds(q0, 128), :], kw) * SCALE
            s = jnp.where(_band_mask(q0, m), s, NEG)
            mx = jnp.max(s, axis=-1, keepdims=True)
            e = jnp.exp(s - mx)
            den = jnp.sum(e, axis=-1, keepdims=True)
            o_ref[pl.ds(q0, 128), :] = _dot((e / den).astype(BF16), vw)
            l_ref[pl.ds(q0, 128), :] = jnp.broadcast_to(mx + jnp.log(den), (128, HD))
            return carry

        lax.fori_loop(0, nb, block, 0)

    blk = pl.BlockSpec((m, HD), lambda h, r: (0, r * 4 + h))
    o, lse = pl.pallas_call(
        body, name=f"attn_a_fwd_{g}", out_shape=[SDS((m, dil * 512), F32)] * 2, grid=(4, dil),
        in_specs=[blk] * 3, out_specs=[blk] * 2,
        scratch_shapes=[pltpu.VMEM((m + 128, HD), BF16), pltpu.VMEM((m + 128, HD), BF16)],
        compiler_params=_params(2))(*_group_views(qkn, proj, g))
    return o.reshape(T, 512), lse.reshape(T, 512)


def _nbr_window(r):
    start = jnp.clip(r - WIN_R // 2, 0, T // GRID_W - WIN_R)
    return start, start - r + (WIN_R - 1)


def _attn_b_fwd(qkn, proj, rpb_flat):
    def body(rpb_ref, q_ref, k_ref, v_ref, o_ref, l_ref, bias_ref, vb):
        h = pl.program_id(0)
        qc = lax.broadcasted_iota(jnp.int32, (GRID_W, 512), 0)
        lane = lax.broadcasted_iota(jnp.int32, (GRID_W, 512), 1)
        kc = lane & (GRID_W - 1)
        dc = jnp.clip(kc - qc, -(WIN_C - 1), WIN_C - 1) + (WIN_C - 1)
        cs = jnp.clip(qc - WIN_C // 2, 0, GRID_W - WIN_C)
        colmask = (kc >= cs) & (kc < cs + WIN_C)
        jrow = lax.broadcasted_iota(jnp.int32, (1, 512), 1) >> 6
        for off in range(8):
            bias_ref[off] = jnp.zeros((GRID_W, 512), F32)
        for e in range(31):
            sel = dc == e
            for off in range(8):
                v = jnp.zeros((1, 512), F32)
                for j in range(8):
                    v = jnp.where(jrow == j, rpb_ref[h * NRPB + (off + j) * 31 + e], v)
                bias_ref[off] = jnp.where(sel, v, bias_ref[off])
        for off in range(8):
            bias_ref[off] = jnp.where(colmask, bias_ref[off], NEG)
        vb[...] = v_ref[...].astype(BF16)

        def row(r, carry):
            start, off = _nbr_window(r)
            q0 = pl.multiple_of(r * GRID_W, GRID_W)
            k0 = pl.multiple_of(start * GRID_W, GRID_W)
            s = _dot_nt(q_ref[pl.ds(q0, GRID_W), :], k_ref[pl.ds(k0, 512), :]) * SCALE + bias_ref[off]
            mx = jnp.max(s, axis=-1, keepdims=True)
            e = jnp.exp(s - mx)
            den = jnp.sum(e, axis=-1, keepdims=True)
            o_ref[pl.ds(q0, GRID_W), :] = _dot((e / den).astype(BF16), vb[pl.ds(k0, 512), :])
            l_ref[pl.ds(q0, GRID_W), :] = jnp.broadcast_to(mx + jnp.log(den), (GRID_W, HD))
            return carry

        lax.fori_loop(0, T // GRID_W, row, 0)

    return pl.pallas_call(
        body, name="attn_b_fwd",
        out_shape=[SDS((T, 512), F32), SDS((T, 512), F32), SDS((4, 8, GRID_W, 512), F32)], grid=(4,),
        in_specs=[pl.BlockSpec(memory_space=pltpu.SMEM),
                  pl.BlockSpec((T, HD), lambda h: (0, NHA + h)),
                  pl.BlockSpec((T, HD), lambda h: (0, NH + NHA + h)),
                  pl.BlockSpec((T, HD), lambda h: (0, 2 * NH + NHA + h))],
        out_specs=[pl.BlockSpec((T, HD), lambda h: (0, h)), pl.BlockSpec((T, HD), lambda h: (0, h)),
                   pl.BlockSpec((None, 8, GRID_W, 512), lambda h: (h, 0, 0, 0))],
        scratch_shapes=[pltpu.VMEM((T, HD), BF16)],
        compiler_params=_params(1))(rpb_flat, qkn, qkn, proj)


def _comb_fwd(os, ls):
    tm = 512

    def body(o0, o1, o2, l0, l1, l2, oa_ref, w0, w1, w2):
        lv = [l0[...], l1[...], l2[...]]
        mx = jnp.maximum(jnp.maximum(lv[0], lv[1]), lv[2])
        ev = [jnp.exp(l - mx) for l in lv]
        den = ev[0] + ev[1] + ev[2]
        wv = [e / den for e in ev]
        oa_ref[...] = (wv[0] * o0[...] + wv[1] * o1[...] + wv[2] * o2[...]).astype(BF16)
        w0[...], w1[...], w2[...] = wv

    spec = pl.BlockSpec((tm, 512), lambda i: (i, 0))
    return pl.pallas_call(
        body, name="comb_fwd", out_shape=[SDS((T, 512), BF16)] + [SDS((T, 512), F32)] * 3, grid=(T // tm,),
        in_specs=[spec] * 6, out_specs=[spec] * 4, compiler_params=_params(1))(*os, *ls)


def _mix_fwd(oa, ob, proj, b_gate, wpa, wpb):
    tm = 256

    def body(oa_ref, ob_ref, ga_ref, gb_ref, ba_ref, bb_ref, wpa_ref, wpb_ref, mixed_ref, ob16_ref):
        oav = oa_ref[...]
        obv = ob_ref[...].astype(BF16)
        ob16_ref[...] = obv
        for s in range(NSH):
            sl = slice(s * 512, (s + 1) * 512)
            ga = _sigmoid(ga_ref[:, sl] + ba_ref[:, sl])
            gb = _sigmoid(gb_ref[:, sl] + bb_ref[:, sl])
            mixed_ref[:, sl] = (ga * _dot(oav, wpa_ref[s]) + gb * _dot(obv, wpb_ref[s])).astype(BF16)

    row = lambda w: pl.BlockSpec((tm, w), lambda i: (i, 0))
    return pl.pallas_call(
        body, name="mix_fwd", out_shape=[SDS((T, D), BF16), SDS((T, 512), BF16)], grid=(T // tm,),
        in_specs=[row(512), row(512),
                  pl.BlockSpec((tm, D), lambda i: (i, 3)), pl.BlockSpec((tm, D), lambda i: (i, 4)),
                  pl.BlockSpec((1, D), lambda i: (0, 0)), pl.BlockSpec((1, D), lambda i: (0, 1)),
                  _resident((NSH, 512, 512), lambda i: (0, 0, 0)), _resident((NSH, 512, 512), lambda i: (0, 0, 0))],
        out_specs=[row(D), row(512)], compiler_params=_params(1))(oa, ob, proj, proj, b_gate, b_gate, wpa, wpb)


def _out_proj_fwd(mixed, w_out, x, g):
    tm = 256

    def body(m_ref, w_ref, x_ref, g_ref, h1_ref, hn_ref):
        h1 = x_ref[...] + _dot(m_ref[...], w_ref[...])
        h1_ref[...] = h1
        r = lax.rsqrt(jnp.mean(h1 * h1, axis=-1, keepdims=True) + EPS)
        hn_ref[...] = (h1 * r * g_ref[...]).astype(BF16)

    row = pl.BlockSpec((tm, D), lambda i: (i, 0))
    return pl.pallas_call(
        body, name="out_proj_fwd", out_shape=[SDS((T, D), F32), SDS((T, D), BF16)], grid=(T // tm,),
        in_specs=[row, _resident((D, D), lambda i: (0, 0)), row, pl.BlockSpec((1, D), lambda i: (0, 0))],
        out_specs=[row, row], compiler_params=_params(1))(mixed, w_out, x, g)


def _ffn_up(hn, w_up):
    tm, tn = 1024, 512
    per = (DFF // NSH) // tn

    def body(h_ref, w_ref, u_ref):
        u_ref[...] = jnp.maximum(_dot(h_ref[...], w_ref[...]), 0.0)

    return pl.pallas_call(
        body, name="ffn_up", out_shape=SDS((T, DFF), F32), grid=(T // tm, DFF // tn),
        in_specs=[pl.BlockSpec((tm, D), lambda i, j: (i, 0)),
                  pl.BlockSpec((None, D, tn), lambda i, j: (j // per, 0, j % per))],
        out_specs=pl.BlockSpec((tm, tn), lambda i, j: (i, j)), compiler_params=_params(2))(hn, w_up)


def _ffn_down_loss(u, w_down, h1, target):
    tm, tk = 512, 512
    nk = DFF // tk

    def body(u_ref, w_ref, h1_ref, t_ref, dy_ref, dy16_ref, loss_ref, acc):
        k = pl.program_id(1)

        @pl.when(k == 0)
        def _():
            acc[...] = jnp.zeros_like(acc)

        uv = u_ref[...]
        acc[...] += _dot((uv * uv).astype(BF16), w_ref[...])

        @pl.when(k == nk - 1)
        def _():
            err = acc[...] + h1_ref[...] - t_ref[...]
            dy = err * (1.0 / D)
            dy_ref[...] = dy
            dy16_ref[...] = dy.astype(BF16)
            part = 0.5 * jnp.sum(jnp.mean(err * err, axis=-1, keepdims=True), axis=0, keepdims=True)
            loss_ref[...] = jnp.broadcast_to(part, (8, 128))

    row = pl.BlockSpec((tm, D), lambda i, k: (i, 0))
    return pl.pallas_call(
        body, name="ffn_down_loss",
        out_shape=[SDS((T, D), F32), SDS((T, D), BF16), SDS((T // tm, 8, 128), F32)], grid=(T // tm, nk),
        in_specs=[pl.BlockSpec((tm, tk), lambda i, k: (i, k)), pl.BlockSpec((tk, D), lambda i, k: (k, 0)), row, row],
        out_specs=[row, row, pl.BlockSpec((None, 8, 128), lambda i, k: (i, 0, 0))],
        scratch_shapes=[pltpu.VMEM((tm, D), F32)], compiler_params=_params(2))(u, w_down, h1, target)


def _ffn_down_bwd(dy16, w_down, u, deps=()):
    tm, tn = 1024, 512

    def body(dy_ref, w_ref, u_ref, du_ref):
        uv = u_ref[...]
        du_ref[...] = jnp.where(uv > 0.0, 2.0 * uv * _dot_nt(dy_ref[...], w_ref[...]), 0.0).astype(BF16)

    return pl.pallas_call(
        _after(body, deps), name="ffn_down_bwd", out_shape=SDS((T, DFF), BF16), grid=(T // tm, DFF // tn),
        in_specs=[DEP_SPEC] * len(deps) + [
            pl.BlockSpec((tm, D), lambda i, j: (i, 0)), pl.BlockSpec((tn, D), lambda i, j: (j, 0)),
            pl.BlockSpec((tm, tn), lambda i, j: (i, j))],
        out_specs=pl.BlockSpec((tm, tn), lambda i, j: (i, j)), compiler_params=_params(2))(*deps, dy16, w_down, u)


def _norm_bwd(xv, dz_in, g):
    r = lax.rsqrt(jnp.mean(xv * xv, axis=-1, keepdims=True) + EPS)
    dg = jnp.sum(xv * r * dz_in, axis=0, keepdims=True)
    dz = dz_in * g
    dx = r * dz - xv * (r * r * r) * jnp.mean(xv * dz, axis=-1, keepdims=True)
    return dx, dg


def _ffn_up_bwd(du, w_up, h1, dy, g, deps=()):
    tm, tk = 512, 1024
    per = (DFF // NSH) // tk
    nk = DFF // tk

    def body(du_ref, w_ref, h1_ref, dy_ref, g_ref, dh1_ref, dh16_ref, dg_ref, acc):
        i, k = pl.program_id(0), pl.program_id(1)

        @pl.when(k == 0)
        def _():
            acc[...] = jnp.zeros_like(acc)

        @pl.when((k == 0) & (i == 0))
        def _():
            dg_ref[...] = jnp.zeros_like(dg_ref)

        acc[...] += _dot_nt(du_ref[...], w_ref[...])

        @pl.when(k == nk - 1)
        def _():
            dx, dg = _norm_bwd(h1_ref[...], acc[...], g_ref[...])
            dh1 = dy_ref[...] + dx
            dh1_ref[...] = dh1
            dh16_ref[...] = dh1.astype(BF16)
            dg_ref[...] += dg

    row = pl.BlockSpec((tm, D), lambda i, k: (i, 0))
    vec = pl.BlockSpec((1, D), lambda i, k: (0, 0))
    return pl.pallas_call(
        _after(body, deps), name="ffn_up_bwd", out_shape=[SDS((T, D), F32), SDS((T, D), BF16), SDS((1, D), F32)],
        grid=(T // tm, nk),
        in_specs=[DEP_SPEC] * len(deps) + [
            pl.BlockSpec((tm, tk), lambda i, k: (i, k)),
            pl.BlockSpec((None, D, tk), lambda i, k: (k // per, 0, k % per)), row, row, vec],
        out_specs=[row, row, vec], scratch_shapes=[pltpu.VMEM((tm, D), F32)],
        compiler_params=_params(2))(*deps, du, w_up, h1, dy, g)


def _mix_bwd(dh16, w_out, oa, ob16, proj, b_gate, wpa, wpb):
    tm = 128

    def body(dh_ref, wo_ref, oa_ref, ob_ref, ga_ref, gb_ref, ba_ref, bb_ref, wpa_ref, wpb_ref,
             dya_ref, dyb_ref, dga_ref, dgb_ref, doa_ref, dob_ref, dba_ref, dbb_ref):
        @pl.when(pl.program_id(0) == 0)
        def _():
            dba_ref[...] = jnp.zeros_like(dba_ref)
            dbb_ref[...] = jnp.zeros_like(dbb_ref)

        oav, obv = oa_ref[...], ob_ref[...]
        doa = jnp.zeros((tm, 512), F32)
        dob = jnp.zeros((tm, 512), F32)
        for s in range(NSH):
            sl = slice(s * 512, (s + 1) * 512)
            dm = _dot_nt(dh_ref[...], wo_ref[sl, :])
            ga = _sigmoid(ga_ref[:, sl] + ba_ref[:, sl])
            gb = _sigmoid(gb_ref[:, sl] + bb_ref[:, sl])
            dya = (dm * ga).astype(BF16)
            dyb = (dm * gb).astype(BF16)
            dza = dm * _dot(oav, wpa_ref[s]) * ga * (1.0 - ga)
            dzb = dm * _dot(obv, wpb_ref[s]) * gb * (1.0 - gb)
            dya_ref[:, sl], dyb_ref[:, sl] = dya, dyb
            dga_ref[:, sl], dgb_ref[:, sl] = dza.astype(BF16), dzb.astype(BF16)
            dba_ref[:, sl] += jnp.sum(dza, axis=0, keepdims=True)
            dbb_ref[:, sl] += jnp.sum(dzb, axis=0, keepdims=True)
            doa += _dot_nt(dya, wpa_ref[s])
            dob += _dot_nt(dyb, wpb_ref[s])
        doa_ref[...], dob_ref[...] = doa, dob

    row = lambda w: pl.BlockSpec((tm, w), lambda i: (i, 0))
    vec = pl.BlockSpec((1, D), lambda i: (0, 0))
    wp = _resident((NSH, 512, 512), lambda i: (0, 0, 0))
    return pl.pallas_call(
        body, name="mix_bwd",
        out_shape=[SDS((T, D), BF16)] * 4 + [SDS((T, 512), F32)] * 2 + [SDS((1, D), F32)] * 2, grid=(T // tm,),
        in_specs=[row(D), _resident((D, D), lambda i: (0, 0)), row(512), row(512),
                  pl.BlockSpec((tm, D), lambda i: (i, 3)), pl.BlockSpec((tm, D), lambda i: (i, 4)),
                  pl.BlockSpec((1, D), lambda i: (0, 0)), pl.BlockSpec((1, D), lambda i: (0, 1)), wp, wp],
        out_specs=[row(D)] * 4 + [row(512)] * 2 + [vec] * 2,
        compiler_params=_params(1))(dh16, w_out, oa, ob16, proj, proj, b_gate, b_gate, wpa, wpb)


def _comb_bwd(doa, os, ws, deps=()):
    tm = 512

    def body(d_ref, o0, o1, o2, w0, w1, w2, cc_ref):
        prod = d_ref[...] * (w0[...] * o0[...] + w1[...] * o1[...] + w2[...] * o2[...])
        for h in range(4):
            sl = slice(h * HD, (h + 1) * HD)
            cc_ref[:, sl] = jnp.broadcast_to(jnp.sum(prod[:, sl], axis=-1, keepdims=True), (tm, HD))

    spec = pl.BlockSpec((tm, 512), lambda i: (i, 0))
    return pl.pallas_call(
        _after(body, deps), name="comb_bwd", out_shape=SDS((T, 512), F32), grid=(T // tm,),
        in_specs=[DEP_SPEC] * len(deps) + [spec] * 7, out_specs=spec,
        compiler_params=_params(1))(*deps, doa, *os, *ws)


def _attn_a_bwd(qkn, proj, doa, lse, w, cc, g):
    dil = DILS[g]
    m = T // dil
    nb = m // 128

    def body(q_ref, k_ref, v_ref, d_ref, l_ref, w_ref, c_ref, dqk_ref, dv_ref, kp, vp, dkp, dvp):
        _fill_padded(kp, k_ref[...], m)
        _fill_padded(vp, v_ref[...], m)
        dkp[...] = jnp.zeros_like(dkp)
        dvp[...] = jnp.zeros_like(dvp)

        def block(b, carry):
            q0 = pl.multiple_of(b * 128, 128)
            rows = pl.ds(q0, 128)
            win = pl.ds(q0, 256)
            qb, kw, vw = q_ref[rows, :], kp[win, :], vp[win, :]
            s = _dot_nt(qb, kw) * SCALE
            s = jnp.where(_band_mask(q0, m), s, NEG)
            wp = _wide(w_ref[rows, :], 2) * jnp.exp(s - _wide(l_ref[rows, :], 2))
            dob = d_ref[rows, :].astype(BF16)
            ds = (wp * (_dot_nt(dob, vw) - _wide(c_ref[rows, :], 2))).astype(BF16)
            dqk_ref[0, rows, :] = _dot(ds, kw) * SCALE
            dkp[win, :] += _dot_tn(ds, qb) * SCALE
            dvp[win, :] += _dot_tn(wp.astype(BF16), dob)
            return carry

        lax.fori_loop(0, nb, block, 0)
        dqk_ref[1] = dkp[64:64 + m, :]
        dv_ref[...] = dvp[64:64 + m, :]

    blk = pl.BlockSpec((m, HD), lambda h, r: (0, r * 4 + h))
    view = lambda a: a.reshape(m, dil * 512)
    dqk, dv = pl.pallas_call(
        body, name=f"attn_a_bwd_{g}", out_shape=[SDS((2, m, dil * 512), F32), SDS((m, dil * 512), F32)], grid=(4, dil),
        in_specs=[blk] * 7,
        out_specs=[pl.BlockSpec((2, m, HD), lambda h, r: (0, 0, r * 4 + h)), blk],
        scratch_shapes=[pltpu.VMEM((m + 128, HD), BF16), pltpu.VMEM((m + 128, HD), BF16),
                        pltpu.VMEM((m + 128, HD), F32), pltpu.VMEM((m + 128, HD), F32)],
        compiler_params=_params(2))(*_group_views(qkn, proj, g), view(doa), view(lse), view(w), view(cc))
    return dqk.reshape(2, T, 512), dv.reshape(T, 512)


def _attn_b_bwd(qkn, proj, dob, ob, lse, bias):
    def body(q_ref, k_ref, v_ref, d_ref, o_ref, l_ref, bias_ref, dqk_ref, dv_ref, drpb_ref, vb, dk_acc, dv_acc, a_acc):
        vb[...] = v_ref[...].astype(BF16)
        dk_acc[...] = jnp.zeros_like(dk_acc)
        dv_acc[...] = jnp.zeros_like(dv_acc)
        a_acc[...] = jnp.zeros_like(a_acc)

        def row(r, carry):
            start, off = _nbr_window(r)
            rows = pl.ds(pl.multiple_of(r * GRID_W, GRID_W), GRID_W)
            win = pl.ds(pl.multiple_of(start * GRID_W, GRID_W), 512)
            qr, kw, vw = q_ref[rows, :], k_ref[win, :], vb[win, :]
            s = _dot_nt(qr, kw) * SCALE + bias_ref[off]
            p = jnp.exp(s - _wide(l_ref[rows, :], 4))
            dov = d_ref[rows, :]
            delta = jnp.sum(dov * o_ref[rows, :], axis=-1, keepdims=True)
            do16 = dov.astype(BF16)
            ds = p * (_dot_nt(do16, vw) - delta)
            a_acc[off] += ds
            ds16 = ds.astype(BF16)
            dqk_ref[0, rows, :] = _dot(ds16, kw) * SCALE
            dk_acc[win, :] += _dot_tn(ds16, qr) * SCALE
            dv_acc[win, :] += _dot_tn(p.astype(BF16), do16)
            return carry

        lax.fori_loop(0, T // GRID_W, row, 0)
        dqk_ref[1] = dk_acc[...]
        dv_ref[...] = dv_acc[...]

        qc = lax.broadcasted_iota(jnp.int32, (GRID_W, 512), 0)
        lane = lax.broadcasted_iota(jnp.int32, (GRID_W, 512), 1)
        dc = jnp.clip((lane & (GRID_W - 1)) - qc, -(WIN_C - 1), WIN_C - 1) + (WIN_C - 1)
        jrow = lax.broadcasted_iota(jnp.int32, (1, 512), 1) >> 6
        dlane = lax.broadcasted_iota(jnp.int32, (1, HD), 1)
        drpb_ref[...] = jnp.zeros_like(drpb_ref)

        def per_dc(e, carry):
            sel = dc == e
            out = jnp.zeros((1, HD), F32)
            for off in range(8):
                col = jnp.sum(jnp.where(sel, a_acc[off], 0.0), axis=0, keepdims=True)
                for j in range(8):
                    part = jnp.sum(jnp.where(jrow == j, col, 0.0), axis=-1, keepdims=True)
                    out = out + jnp.where(dlane == off + j, part, 0.0)
            drpb_ref[pl.ds(e, 1), :] = out
            return carry

        lax.fori_loop(0, 31, per_dc, 0)

    blk = pl.BlockSpec((T, HD), lambda h: (0, h))
    return pl.pallas_call(
        body, name="attn_b_bwd",
        out_shape=[SDS((2, T, 512), F32), SDS((T, 512), F32), SDS((4, 32, HD), F32)], grid=(4,),
        in_specs=[pl.BlockSpec((T, HD), lambda h: (0, NHA + h)),
                  pl.BlockSpec((T, HD), lambda h: (0, NH + NHA + h)),
                  pl.BlockSpec((T, HD), lambda h: (0, 2 * NH + NHA + h)), blk, blk, blk,
                  pl.BlockSpec((None, 8, GRID_W, 512), lambda h: (h, 0, 0, 0))],
        out_specs=[pl.BlockSpec((2, T, HD), lambda h: (0, 0, h)), blk,
                   pl.BlockSpec((None, 32, HD), lambda h: (h, 0, 0))],
        scratch_shapes=[pltpu.VMEM((T, HD), BF16), pltpu.VMEM((T, HD), F32), pltpu.VMEM((T, HD), F32),
                        pltpu.VMEM((8, GRID_W, 512), F32)],
        compiler_params=_params(1))(qkn, qkn, proj, dob, ob, lse, bias)


def _qk_bwd(proj, nw, cos, sin, dqk_groups, dqk_b):
    tm = 256

    def body(p_ref, w_ref, cos_ref, sin_ref, d0, d1, d2, d3, o_ref, dn_ref):
        @pl.when(pl.program_id(1) == 0)
        def _():
            dn_ref[...] = jnp.zeros_like(dn_ref)

        cv, sv = cos_ref[...], sin_ref[...]
        srcs = (d0, d1, d2, d3)
        dna = jnp.zeros((1, HD), F32)
        dnb = jnp.zeros((1, HD), F32)
        for h in range(NH):
            sl = slice(h * HD, (h + 1) * HD)
            dz = srcs[h // 4][:, (h % 4) * HD:(h % 4 + 1) * HD]
            if h < NHA:
                dz = dz * cv + pltpu.roll(dz * sv, 64, 1)
            dx, dg = _norm_bwd(p_ref[:, sl], dz, w_ref[:, sl])
            o_ref[:, sl] = dx.astype(BF16)
            if h < NHA:
                dna += dg
            else:
                dnb += dg
        dn_ref[0:1, :] += dna
        dn_ref[1:2, :] += dnb

    dspec = pl.BlockSpec((None, tm, 512), lambda j, i: (j, i, 0))
    return pl.pallas_call(
        body, name="qk_bwd", out_shape=[SDS((T, 2 * D), BF16), SDS((2, 8, HD), F32)], grid=(2, T // tm),
        in_specs=[pl.BlockSpec((tm, D), lambda j, i: (i, j)),
                  pl.BlockSpec((None, 1, D), lambda j, i: (j, 0, 0)),
                  pl.BlockSpec((tm, HD), lambda j, i: (i, 0)),
                  pl.BlockSpec((tm, HD), lambda j, i: (i, 0)), dspec, dspec, dspec, dspec],
        out_specs=[pl.BlockSpec((tm, D), lambda j, i: (i, j)), pl.BlockSpec((None, 8, HD), lambda j, i: (j, 0, 0))],
        compiler_params=_params(2))(proj, nw, cos, sin, *dqk_groups, dqk_b)


def _in_proj_bwd(dproj, w_in, x, dh1, g, deps=()):
    tm, tk = 512, 1280
    per = (DIN // NSH) // tk
    nk = DIN // tk

    def body(dp_ref, w_ref, x_ref, dh_ref, g_ref, dx_ref, dg_ref, acc):
        i, k = pl.program_id(0), pl.program_id(1)

        @pl.when(k == 0)
        def _():
            acc[...] = jnp.zeros_like(acc)

        @pl.when((k == 0) & (i == 0))
        def _():
            dg_ref[...] = jnp.zeros_like(dg_ref)

        acc[...] += _dot_nt(dp_ref[...], w_ref[...])

        @pl.when(k == nk - 1)
        def _():
            dx, dg = _norm_bwd(x_ref[...], acc[...], g_ref[...])
            dx_ref[...] = dh_ref[...] + dx
            dg_ref[...] += dg

    row = pl.BlockSpec((tm, D), lambda i, k: (i, 0))
    vec = pl.BlockSpec((1, D), lambda i, k: (0, 0))
    return pl.pallas_call(
        _after(body, deps), name="in_proj_bwd", out_shape=[SDS((T, D), F32), SDS((1, D), F32)], grid=(T // tm, nk),
        in_specs=[DEP_SPEC] * len(deps) + [
            pl.BlockSpec((tm, tk), lambda i, k: (i, k)),
            pl.BlockSpec((None, D, tk), lambda i, k: (k // per, 0, k % per)), row, row, vec],
        out_specs=[row, vec], scratch_shapes=[pltpu.VMEM((tm, D), F32)],
        compiler_params=_params(2))(*deps, dproj, w_in, x, dh1, g)


def _grad_w(name, a, g, shard_rows, rows, cols, tr, tc, square=False):
    ni, nj = rows // tr, cols // tc
    if shard_rows:
        a_map, g_map = (lambda s, i, j: (0, s * ni + i)), (lambda s, i, j: (0, j))
    else:
        a_map, g_map = (lambda s, i, j: (0, i)), (lambda s, i, j: (0, s * nj + j))

    def body(a_ref, g_ref, o_ref):
        av = a_ref[...]
        if square:
            av = (av * av).astype(BF16)
        o_ref[...] = _dot_tn(av, g_ref[...]).astype(BF16)

    return pl.pallas_call(
        body, name=name, out_shape=SDS((NSH, rows, cols), BF16), grid=(NSH, ni, nj),
        in_specs=[pl.BlockSpec((T, tr), a_map), pl.BlockSpec((T, tc), g_map)],
        out_specs=pl.BlockSpec((None, tr, tc), lambda s, i, j: (s, i, j)), compiler_params=_params(3))(a, g)


def _adamw(w, g, m, v):
    m = B1 * m + (1.0 - B1) * g
    v = B2 * v + (1.0 - B2) * (g * g)
    m_hat = m / (1.0 - B1 ** STEP)
    v_hat = v / (1.0 - B2 ** STEP)
    delta = -LR * (m_hat / (jnp.sqrt(v_hat) + AEPS) + WD * w)
    return delta, m, v


def _sum_halves(name, mine, theirs):
    _, rows, cols = mine.shape
    tr = _row_tile(rows, cols, 1 << 17)

    def body(a_ref, b_ref, o_ref):
        o_ref[...] = (a_ref[...].astype(F32) + b_ref[...].astype(F32)).astype(BF16)

    spec = pl.BlockSpec((NSH, tr, cols), lambda i: (0, i, 0))
    return pl.pallas_call(
        body, name=name, out_shape=SDS(mine.shape, BF16), grid=(rows // tr,),
        in_specs=[spec, spec], out_specs=spec, compiler_params=_params(1))(mine, theirs)


def _sum_landed(name, me, part, landed):
    _, rows, cols = part.shape
    tr = _row_tile(rows, cols, 1 << 18)

    def body(me_ref, p_ref, l_ref, o_ref):
        o_ref[...] = ((p_ref[...].astype(F32) + l_ref[0].astype(F32)) + l_ref[1].astype(F32)) + l_ref[2].astype(F32)

    return pl.pallas_call(
        body, name=name, out_shape=SDS((rows, cols), F32),
        grid_spec=pltpu.PrefetchScalarGridSpec(
            num_scalar_prefetch=1, grid=(rows // tr,),
            in_specs=[pl.BlockSpec((None, tr, cols), lambda i, me_ref: (me_ref[0], i, 0)),
                      pl.BlockSpec((3, tr, cols), lambda i, me_ref: (0, i, 0))],
            out_specs=pl.BlockSpec((tr, cols), lambda i, me_ref: (i, 0))),
        compiler_params=_params(1))(me, part, landed)


def _adam_shard(name, g, w, m, v):
    rows, cols = w.shape
    tr = _row_tile(rows, cols, 1 << 18)

    def body(g_ref, w_ref, m_ref, v_ref, d_ref, nm_ref, nv_ref):
        d_ref[...], nm_ref[...], nv_ref[...] = _adamw(w_ref[...], g_ref[...], m_ref[...], v_ref[...])

    spec = pl.BlockSpec((tr, cols), lambda i: (i, 0))
    return pl.pallas_call(
        body, name=name, out_shape=[SDS((rows, cols), F32)] * 3, grid=(rows // tr,),
        in_specs=[spec] * 4, out_specs=[spec] * 3, compiler_params=_params(1))(g, w, m, v)


def _adam_small(gathered, w, m, v):
    def body(g_ref, w_ref, m_ref, v_ref, go_ref, d_ref, nm_ref, nv_ref):
        g = g_ref[0:SMALL_ROWS, :]
        for dev in range(1, 8):
            g = g + g_ref[dev * SMALL_ROWS:(dev + 1) * SMALL_ROWS, :]
        go_ref[...] = g
        d_ref[...], nm_ref[...], nv_ref[...] = _adamw(w_ref[...], g, m_ref[...], v_ref[...])

    return pl.pallas_call(body, name="adam_small", out_shape=[SDS((SMALL_ROWS, HD), F32)] * 4)(gathered, w, m, v)


SMALL = (("norm_mix", (1, D)), ("b_gate", (1, 2 * D)), ("q_norm_a", (1, HD)), ("k_norm_a", (1, HD)),
         ("q_norm_b", (1, HD)), ("k_norm_b", (1, HD)), ("rpb_b", (1, 4, 15, 31)), ("norm_ffn", (1, D)))


def _pack_small(vals):
    pieces = []
    for (name, shape), val in zip(SMALL, vals):
        flat = val.reshape(-1)
        pad = (-flat.shape[0]) % HD
        pieces.append(jnp.pad(flat, (0, pad)).reshape(-1, HD))
    packed = jnp.concatenate(pieces, axis=0)
    return jnp.pad(packed, ((0, SMALL_ROWS - packed.shape[0]), (0, 0)))


def _unpack_small(packed):
    out, row = [], 0
    for name, shape in SMALL:
        size = int(np.prod(shape))
        nrows = -(-size // HD)
        out.append(packed[row:row + nrows].reshape(-1)[:size].reshape(shape))
        row += nrows
    return out


def kernel(x, norm_mix, w_in, b_gate, q_norm_a, k_norm_a, q_norm_b, k_norm_b, rpb_b, w_proj_a, w_proj_b, w_out, norm_ffn, w_up, w_down, loss_target, m_norm_mix, m_w_in, m_b_gate, m_q_norm_a, m_k_norm_a, m_q_norm_b, m_k_norm_b, m_rpb_b, m_w_proj_a, m_w_proj_b, m_w_out, m_norm_ffn, m_w_up, m_w_down, v_norm_mix, v_w_in, v_b_gate, v_q_norm_a, v_k_norm_a, v_q_norm_b, v_k_norm_b, v_rpb_b, v_w_proj_a, v_w_proj_b, v_w_out, v_norm_ffn, v_w_up, v_w_down):
    big_names = ("w_in", "w_proj_a", "w_proj_b", "w_out", "w_up", "w_down")
    big_w = [a[0] for a in (w_in, w_proj_a, w_proj_b, w_out, w_up, w_down)]
    big_m = [a[0] for a in (m_w_in, m_w_proj_a, m_w_proj_b, m_w_out, m_w_up, m_w_down)]
    big_v = [a[0] for a in (v_w_in, v_w_proj_a, v_w_proj_b, v_w_out, v_w_up, v_w_down)]
    x2, target = x[0], loss_target[0]

    shards = [_cast_bf16(w, "cast_" + n) for n, w in zip(big_names, big_w)]
    me = (2 * lax.axis_index("x") + lax.axis_index("y")).astype(jnp.int32).reshape(1)
    groups = ((0,), (1, 2, 3), (4,), (5,))
    shards = [s.reshape(2, s.shape[0] // 2, s.shape[1]) for s in shards]
    started = [_gather_start(f"gather_start_{j}", [shards[i] for i in grp]) for j, grp in enumerate(groups)]

    def gathered(j, after):
        send, recv, srcs, lands, _ = started[j]
        srcs, lands = _gather_wait(f"gather_wait_{j}", send, recv, srcs, lands, after)
        fulls = _gather_finish(f"gather_finish_{j}", srcs, lands)
        return [f.reshape(NSH, 2 * f.shape[2], f.shape[3]) for f in fulls]

    def reduce_begin(j, grads):
        grads = [g.reshape(NSH, 2, g.shape[1] // 2, g.shape[2]) for g in grads]
        mine, theirs = _reduce_exchange(f"reduce_exchange_{j}", grads)
        parts = [_sum_halves(f"sum_halves_{j}_{i}", a, b) for i, (a, b) in enumerate(zip(mine, theirs))]
        send, recv, parts, lands, token = _reduce_start(f"reduce_start_{j}", parts)
        return (send, recv, parts, lands), token

    big_out = {}

    def reduce_end(j, state, after):
        send, recv, parts, lands = state
        parts, lands = _reduce_wait(f"reduce_wait_{j}", send, recv, parts, lands, after)
        sums = [_sum_landed(f"sum_landed_{j}_{i}", me, p, l) for i, (p, l) in enumerate(zip(parts, lands))]
        for idx, g in zip(groups[j], _reduce_share(f"reduce_share_{j}", sums)):
            g = g.reshape(big_w[idx].shape)
            big_out[idx] =(g, *_adam_shard("adam_" + big_names[idx], g, big_w[idx], big_m[idx], big_v[idx]))
        return big_out[groups[j][-1]][1]

    (win_f,) = gathered(0, ())
    proj, xn = _norm_in_proj(x2, norm_mix, win_f)
    cos, sin = _rope_tables()
    nw = jnp.stack([jnp.concatenate([jnp.tile(q_norm_a, (1, NHA)), jnp.tile(q_norm_b, (1, NH - NHA))], axis=1),
                    jnp.concatenate([jnp.tile(k_norm_a, (1, NHA)), jnp.tile(k_norm_b, (1, NH - NHA))], axis=1)])
    qkn = _qk_prep(proj, nw, cos, sin)
    wpa_f, wpb_f, wout_f = gathered(1, (qkn,))
    wout_f = wout_f.reshape(D, D)
    fwd_a = [_attn_a_fwd(qkn, proj, g) for g in range(3)]
    os, ls = [f[0] for f in fwd_a], [f[1] for f in fwd_a]
    ob, lse_b, bias = _attn_b_fwd(qkn, proj, rpb_b.reshape(-1))
    oa, w0, w1, w2 = _comb_fwd(os, ls)
    ws = [w0, w1, w2]
    mixed, ob16 = _mix_fwd(oa, ob, proj, b_gate, wpa_f, wpb_f)
    h1, hn = _out_proj_fwd(mixed, wout_f, x2, norm_ffn)
    (wup_f,) = gathered(2, (h1,))
    u = _ffn_up(hn, wup_f)
    (wdown_f,) = gathered(3, (u,))
    wdown_f = wdown_f.reshape(DFF, D)
    dy, dy16, loss_parts = _ffn_down_loss(u, wdown_f, h1, target)
    loss = lax.psum(jnp.sum(loss_parts[:, 0, 0]), ("x", "y", "c"))

    g_down = _grad_w("grad_w_down", u, dy16, True, DFF // NSH, D, 512, 1024, square=True)
    red_down, token = reduce_begin(3, [g_down])
    du = _ffn_down_bwd(dy16, wdown_f, u, deps=(token,))
    g_up = _grad_w("grad_w_up", hn, du, False, D, DFF // NSH, 1024, 1024)
    red_up, token = reduce_begin(2, [g_up])
    dh1, dh16, d_norm_ffn = _ffn_up_bwd(du, wup_f, h1, dy, norm_ffn, deps=(token,))
    dya, dyb, dga, dgb, doa, dob, dba, dbb = _mix_bwd(dh16, wout_f, oa, ob16, proj, b_gate, wpa_f, wpb_f)
    g_out = _grad_w("grad_w_out", mixed, dh16, True, D // NSH, D, 512, 1024)
    g_pa = _grad_w("grad_w_proj_a", oa, dya, False, 512, 512, 512, 512)
    g_pb = _grad_w("grad_w_proj_b", ob16, dyb, False, 512, 512, 512, 512)
    red_mid, token = reduce_begin(1, [g_pa, g_pb, g_out])
    cc = _comb_bwd(doa, os, ws, deps=(token,))
    bwd_a = [_attn_a_bwd(qkn, proj, doa, ls[g], ws[g], cc, g) for g in range(3)]
    dqk_b, dv_b, drpb_t = _attn_b_bwd(qkn, proj, dob, ob, lse_b, bias)
    dqk_pre, dn = _qk_bwd(proj, nw, cos, sin, [b[0] for b in bwd_a], dqk_b)
    dv16 = jnp.concatenate([b[1] for b in bwd_a] + [dv_b], axis=1).astype(BF16)
    dproj = jnp.concatenate([dqk_pre, dv16, dga, dgb], axis=1)
    g_in = _grad_w("grad_w_in", xn, dproj, False, D, DIN // NSH, 1024, 1280)
    red_in, token = reduce_begin(0, [g_in])
    grad_x, d_norm_mix = _in_proj_bwd(dproj, win_f, x2, dh1, norm_mix, deps=(token,))

    done = reduce_end(3, red_down, (grad_x,))
    done = reduce_end(2, red_up, (done,))
    done = reduce_end(1, red_mid, (done,))
    reduce_end(0, red_in, (done,))

    d_rpb = jnp.transpose(drpb_t[:, :31, :15], (0, 2, 1))
    small_g = [d_norm_mix, jnp.concatenate([dba, dbb], axis=1), dn[0, 0], dn[1, 0], dn[0, 1], dn[1, 1], d_rpb, d_norm_ffn]
    gathered = _allgather_small(_pack_small(small_g))
    small_w = (norm_mix, b_gate, q_norm_a, k_norm_a, q_norm_b, k_norm_b, rpb_b, norm_ffn)
    small_m = (m_norm_mix, m_b_gate, m_q_norm_a, m_k_norm_a, m_q_norm_b, m_k_norm_b, m_rpb_b, m_norm_ffn)
    small_v = (v_norm_mix, v_b_gate, v_q_norm_a, v_k_norm_a, v_q_norm_b, v_k_norm_b, v_rpb_b, v_norm_ffn)
    small_out = [_unpack_small(p) for p in
                 _adam_small(gathered, _pack_small(small_w), _pack_small(small_m), _pack_small(small_v))]

    order = ("norm_mix", "w_in", "b_gate", "q_norm_a", "k_norm_a", "q_norm_b", "k_norm_b", "rpb_b",
             "w_proj_a", "w_proj_b", "w_out", "norm_ffn", "w_up", "w_down")
    small_idx = {name: i for i, (name, _) in enumerate(SMALL)}
    outs = []
    for kind in range(4):
        for name in order:
            if name in small_idx:
                outs.append(small_out[kind][small_idx[name]])
            else:
                outs.append(big_out[big_names.index(name)][kind][None])
    return (loss, grad_x[None], *outs)
```

```python
import functools

import numpy as np
import jax
import jax.numpy as jnp
from jax import lax
from jax.experimental import pallas as pl
from jax.experimental.pallas import tpu as pltpu

F32, BF16 = jnp.float32, jnp.bfloat16
SDS = jax.ShapeDtypeStruct
MESH = pl.DeviceIdType.MESH

T = 2048
D = 2048
HD = 128
NH, NHA = 16, 12
DIN = 10240
DFF = 8192
NSH = 4
DILS = (1, 4, 16)
EPS = 1e-6
NEG = -1e30
SCALE = HD ** -0.5
GRID_W, WIN_R, WIN_C = 64, 8, 16
NRPB = 15 * 31
VMEM_LIMIT = 56 * 1024 * 1024
B1, B2, LR, AEPS, WD, STEP = 0.9, 0.999, 0.001, 1e-08, 0.01, 10
SMALL_ROWS = 88


def _dot(a, b):
    return jnp.dot(a, b, preferred_element_type=F32)


def _dot_nt(a, b):
    return lax.dot_general(a, b, (((1,), (1,)), ((), ())), preferred_element_type=F32)


def _dot_tn(a, b):
    return lax.dot_general(a, b, (((0,), (0,)), ((), ())), preferred_element_type=F32)


def _params(n):
    return pltpu.CompilerParams(dimension_semantics=("arbitrary",) * n, vmem_limit_bytes=VMEM_LIMIT)


def _resident(shape, index_map):
    return pl.BlockSpec(shape, index_map, pipeline_mode=pl.Buffered(1))


def _sigmoid(z):
    return 1.0 / (1.0 + jnp.exp(-z))


def _wide(v, n):
    return jnp.concatenate([v] * n, axis=1)


def _row_tile(rows, cols, elems):
    tr = 16
    while tr * 2 <= rows and tr * 2 * cols <= elems:
        tr *= 2
    return tr


def _place():
    x, y, c = lax.axis_index("x"), lax.axis_index("y"), lax.axis_index("c")
    peers = [(1 - x, y), (x, 1 - y), (1 - x, 1 - y)]
    return x, y, c, peers


def _cast_into_place(w, name, place):
    rows, cols = w.shape
    hr = rows // 2
    tr = min(hr, 256)
    per = hr // tr

    def body(place_ref, w_ref, o_ref):
        o_ref[...] = w_ref[...].astype(BF16)

    return pl.pallas_call(
        body, name=name, out_shape=SDS((NSH, 2, hr, cols), BF16),
        grid_spec=pltpu.PrefetchScalarGridSpec(
            num_scalar_prefetch=1, grid=(2, per),
            in_specs=[pl.BlockSpec((tr, cols), lambda h, i, p: (h * per + i, 0))],
            out_specs=pl.BlockSpec((None, None, tr, cols), lambda h, i, p: (p[0], h, i, 0))),
        compiler_params=_params(2))(place, w)


ANY_SPEC = pl.BlockSpec(memory_space=pl.ANY)
HBM_SPEC = pl.BlockSpec(memory_space=pltpu.HBM)
SEM_SPEC = pl.BlockSpec(memory_space=pltpu.SEMAPHORE)
DEP_SPEC = pl.BlockSpec((8, 128), lambda *_: (0, 0))
EFFECT = pltpu.SideEffectType.DATAFLOW_SIDE_EFFECTING


def _after(body, deps):
    n = len(deps)
    return (lambda *refs: body(*refs[n:])) if n else body


def _split_start(name, srcs, lands, n_copies, issue):
    n, m = len(srcs), len(lands)

    def body(*refs):
        issue(refs[:n], refs[n:n + m], refs[n + m], refs[n + m + 1])
        refs[-1][...] = jnp.zeros((8, 128), F32)

    arrays = list(srcs) + list(lands)
    outs = pl.pallas_call(
        body, name=name,
        out_shape=(pltpu.SemaphoreType.DMA((n_copies,)), pltpu.SemaphoreType.DMA((n_copies,)),
                   *[pltpu.HBM(a.shape, a.dtype) for a in arrays], SDS((8, 128), F32)),
        in_specs=[HBM_SPEC] * (n + m),
        out_specs=(SEM_SPEC, SEM_SPEC, *[HBM_SPEC] * (n + m), pl.BlockSpec(memory_space=pltpu.VMEM)),
        input_output_aliases={i: 2 + i for i in range(n + m)},
        compiler_params=pltpu.CompilerParams(has_side_effects=EFFECT),
    )(*[pltpu.with_memory_space_constraint(a, pltpu.HBM) for a in arrays])
    return outs[0], outs[1], list(outs[2:2 + n]), list(outs[2 + n:2 + n + m]), outs[-1]


def _split_wait(name, send_sems, recv_sems, srcs, lands, after, wait):
    n, m = len(srcs), len(lands)

    def body(*refs):
        wait(refs[:n], refs[n:n + m], refs[n + m], refs[n + m + 1])

    arrays = list(srcs) + list(lands)
    outs = pl.pallas_call(
        body, name=name, out_shape=[pltpu.HBM(a.shape, a.dtype) for a in arrays],
        in_specs=[HBM_SPEC] * (n + m) + [SEM_SPEC, SEM_SPEC] + [ANY_SPEC] * len(after),
        out_specs=[HBM_SPEC] * (n + m), input_output_aliases={i: i for i in range(n + m)},
        compiler_params=pltpu.CompilerParams(has_side_effects=EFFECT),
    )(*arrays, send_sems, recv_sems, *after)
    return list(outs[:n]), list(outs[n:])


def _gather_start(name, fulls):
    def issue(srcs, dsts, send_sems, recv_sems):
        x, y, c, peers = _place()
        for i in range(len(fulls)):
            mine = dsts[i].at[2 * x + y, c]
            for k, (px, py) in enumerate(peers):
                pltpu.make_async_remote_copy(
                    src_ref=mine, dst_ref=mine, send_sem=send_sems.at[3 * i + k],
                    recv_sem=recv_sems.at[3 * i + k], device_id=(px, py, c), device_id_type=MESH).start()

    return _split_start(name, [], fulls, 3 * len(fulls), issue)


def _gather_wait(name, send_sems, recv_sems, fulls, after):
    def wait(srcs, dsts, send_sems, recv_sems):
        x, y, c, peers = _place()
        for i in range(len(fulls)):
            for k, (px, py) in enumerate(peers):
                cp = pltpu.make_async_remote_copy(
                    src_ref=dsts[i].at[2 * x + y, c], dst_ref=dsts[i].at[2 * px + py, c],
                    send_sem=send_sems.at[3 * i + k], recv_sem=recv_sems.at[3 * i + k],
                    device_id=(px, py, c), device_id_type=MESH)
                cp.wait_send()
                cp.wait_recv()

    return _split_wait(name, send_sems, recv_sems, [], fulls, after, wait)[1]


def _gather_finish(name, fulls):
    n = len(fulls)

    def body(*refs):
        fin, fout = refs[:n], refs[n:2 * n]
        send_sems, recv_sems = refs[2 * n:]
        x, y, c, peers = _place()

        def copy(i, k, half):
            px, py = peers[k]
            return pltpu.make_async_remote_copy(
                src_ref=fin[i].at[2 * px + py, half], dst_ref=fout[i].at[2 * px + py, half],
                send_sem=send_sems.at[3 * i + k], recv_sem=recv_sems.at[3 * i + k],
                device_id=(x, y, 1 - c), device_id_type=MESH)

        sends = [copy(i, k, c) for i in range(n) for k in range(3)]
        for cp in sends:
            cp.start()
        for i in range(n):
            for k in range(3):
                copy(i, k, 1 - c).wait_recv()
        for cp in sends:
            cp.wait_send()

    return pl.pallas_call(
        body, name=name, out_shape=[SDS(f.shape, f.dtype) for f in fulls],
        in_specs=[ANY_SPEC] * n, out_specs=[ANY_SPEC] * n, input_output_aliases={i: i for i in range(n)},
        scratch_shapes=[pltpu.SemaphoreType.DMA((3 * n,)), pltpu.SemaphoreType.DMA((3 * n,))])(*fulls)


def _reduce_exchange(name, grads):
    n = len(grads)

    def body(*refs):
        ins, theirs = refs[:n], refs[n:2 * n]
        send_sems, recv_sems = refs[2 * n:]
        x, y, c, _ = _place()
        copies = []
        for i in range(n):
            cp = pltpu.make_async_remote_copy(
                src_ref=ins[i].at[:, 1 - c], dst_ref=theirs[i], send_sem=send_sems.at[i],
                recv_sem=recv_sems.at[i], device_id=(x, y, 1 - c), device_id_type=MESH)
            cp.start()
            copies.append(cp)
        for cp in copies:
            cp.wait_recv()
            cp.wait_send()

    return pl.pallas_call(
        body, name=name, out_shape=[SDS((NSH,) + g.shape[2:], g.dtype) for g in grads],
        in_specs=[ANY_SPEC] * n, out_specs=[ANY_SPEC] * n,
        scratch_shapes=[pltpu.SemaphoreType.DMA((n,)), pltpu.SemaphoreType.DMA((n,))])(*grads)


def _reduce_start(name, parts):
    lands = [lax.empty((3,) + p.shape[1:], p.dtype) for p in parts]

    def issue(srcs, dsts, send_sems, recv_sems):
        x, y, c, peers = _place()
        for i in range(len(parts)):
            for k, (px, py) in enumerate(peers):
                pltpu.make_async_remote_copy(
                    src_ref=srcs[i].at[2 * px + py], dst_ref=dsts[i].at[k], send_sem=send_sems.at[3 * i + k],
                    recv_sem=recv_sems.at[3 * i + k], device_id=(px, py, c), device_id_type=MESH).start()

    return _split_start(name, parts, lands, 3 * len(parts), issue)


def _reduce_wait(name, send_sems, recv_sems, parts, lands, after):
    def wait(srcs, dsts, send_sems, recv_sems):
        x, y, c, peers = _place()
        for i in range(len(parts)):
            for k, (px, py) in enumerate(peers):
                cp = pltpu.make_async_remote_copy(
                    src_ref=srcs[i].at[2 * px + py], dst_ref=dsts[i].at[k], send_sem=send_sems.at[3 * i + k],
                    recv_sem=recv_sems.at[3 * i + k], device_id=(px, py, c), device_id_type=MESH)
                cp.wait_send()
                cp.wait_recv()

    return _split_wait(name, send_sems, recv_sems, parts, lands, after, wait)


def _reduce_share(name, sums):
    n = len(sums)

    def body(*refs):
        ins, outs = refs[:n], refs[n:2 * n]
        send_sems, recv_sems = refs[2 * n:]
        x, y, c, _ = _place()
        copies = []
        for i in range(n):
            cp = pltpu.make_async_remote_copy(
                src_ref=ins[i].at[c], dst_ref=outs[i].at[c], send_sem=send_sems.at[i], recv_sem=recv_sems.at[i],
                device_id=(x, y, 1 - c), device_id_type=MESH)
            cp.start()
            copies.append(cp)
        for i, cp in enumerate(copies):
            pltpu.make_async_remote_copy(
                src_ref=ins[i].at[c], dst_ref=outs[i].at[1 - c], send_sem=send_sems.at[i],
                recv_sem=recv_sems.at[i], device_id=(x, y, 1 - c), device_id_type=MESH).wait_recv()
            cp.wait_send()

    return pl.pallas_call(
        body, name=name, out_shape=[SDS(s.shape, s.dtype) for s in sums],
        in_specs=[ANY_SPEC] * n, out_specs=[ANY_SPEC] * n, input_output_aliases={i: i for i in range(n)},
        scratch_shapes=[pltpu.SemaphoreType.DMA((n,)), pltpu.SemaphoreType.DMA((n,))])(*sums)


def _allgather_small(v):
    m_per, n = v.shape

    def body(x_ref, out_ref, send_sems, recv_sems, local_sem):
        x, y, c = lax.axis_index("x"), lax.axis_index("y"), lax.axis_index("c")
        me, sibling = (x, y, c), (x, y, 1 - c)
        chips = [(1 - x, y), (x, 1 - y), (1 - x, 1 - y)]

        def rows(px, py, pc):
            return out_ref.at[pl.ds((4 * px + 2 * py + pc) * m_per, m_per), :]

        def copy(k, block, to, src=None):
            return pltpu.make_async_remote_copy(
                src_ref=rows(*block) if src is None else src, dst_ref=rows(*block),
                send_sem=send_sems.at[k], recv_sem=recv_sems.at[k], device_id=to, device_id_type=MESH)

        mine = pltpu.make_async_copy(x_ref, rows(*me), local_sem)
        mine.start()
        first = [copy(0, me, sibling, src=x_ref)]
        first += [copy(1 + j, me, (*chip, c), src=x_ref) for j, chip in enumerate(chips)]
        for cp in first:
            cp.start()
        passed = [copy(4 + j, (*chip, c), sibling) for j, chip in enumerate(chips)]
        for j, chip in enumerate(chips):
            copy(1 + j, (*chip, c), me).wait_recv()
            passed[j].start()
        copy(0, sibling, me).wait_recv()
        for j, chip in enumerate(chips):
            copy(4 + j, (*chip, 1 - c), me).wait_recv()
        for cp in first + passed:
            cp.wait_send()
        mine.wait()

    return pl.pallas_call(
        body, name="allgather_small", out_shape=SDS((8 * m_per, n), v.dtype),
        in_specs=[pl.BlockSpec(memory_space=pltpu.VMEM)], out_specs=pl.BlockSpec(memory_space=pltpu.VMEM),
        scratch_shapes=[pltpu.SemaphoreType.DMA((7,)), pltpu.SemaphoreType.DMA((7,)), pltpu.SemaphoreType.DMA])(v)


def _norm_in_proj(x, g, w_full):
    tm, tn = 512, 512
    per = (DIN // NSH) // tn

    def body(x_ref, g_ref, w_ref, proj_ref, xn_ref):
        @pl.when(pl.program_id(1) == 0)
        def _():
            xv = x_ref[...]
            r = lax.rsqrt(jnp.mean(xv * xv, axis=-1, keepdims=True) + EPS)
            xn_ref[...] = (xv * r * g_ref[...]).astype(BF16)

        proj_ref[...] = _dot(xn_ref[...], w_ref[...])

    return pl.pallas_call(
        body, name="norm_in_proj", out_shape=[SDS((T, DIN), F32), SDS((T, D), BF16)],
        grid=(T // tm, DIN // tn),
        in_specs=[pl.BlockSpec((tm, D), lambda i, j: (i, 0)),
                  pl.BlockSpec((1, D), lambda i, j: (0, 0)),
                  pl.BlockSpec((None, D, tn), lambda i, j: (j // per, 0, j % per))],
        out_specs=[pl.BlockSpec((tm, tn), lambda i, j: (i, j)),
                   pl.BlockSpec((tm, D), lambda i, j: (i, 0))],
        compiler_params=_params(2))(x, g, w_full)


def _rope_tables():
    pos = np.arange(T, dtype=np.float32)
    inv = (10000.0 ** (-np.arange(0, HD, 2, dtype=np.float32) / HD)).astype(np.float32)
    ang = (pos[:, None] * inv[None, :]).astype(np.float32)
    cos, sin = np.cos(ang).astype(np.float32), np.sin(ang).astype(np.float32)
    return (jnp.asarray(np.concatenate([cos, cos], axis=1)), jnp.asarray(np.concatenate([-sin, sin], axis=1)))


def _qk_prep(proj, nw, cos, sin):
    tm = 256

    def body(p_ref, w_ref, cos_ref, sin_ref, o_ref):
        cv, sv = cos_ref[...], sin_ref[...]
        for h in range(NH):
            sl = slice(h * HD, (h + 1) * HD)
            xv = p_ref[:, sl]
            r = lax.rsqrt(jnp.mean(xv * xv, axis=-1, keepdims=True) + EPS)
            z = xv * r * w_ref[:, sl]
            if h < NHA:
                z = z * cv + pltpu.roll(z, 64, 1) * sv
            o_ref[:, sl] = z.astype(BF16)

    return pl.pallas_call(
        body, name="qk_prep", out_shape=SDS((T, 2 * D), BF16), grid=(T // tm, 2),
        in_specs=[pl.BlockSpec((tm, D), lambda i, j: (i, j)),
                  pl.BlockSpec((None, 1, D), lambda i, j: (j, 0, 0)),
                  pl.BlockSpec((tm, HD), lambda i, j: (i, 0)),
                  pl.BlockSpec((tm, HD), lambda i, j: (i, 0))],
        out_specs=pl.BlockSpec((tm, D), lambda i, j: (i, j)),
        compiler_params=_params(2))(proj, nw, cos, sin)


def _band_mask(q0, m):
    ii = lax.broadcasted_iota(jnp.int32, (128, 256), 0)
    jj = lax.broadcasted_iota(jnp.int32, (128, 256), 1)
    rel = jj - ii
    kpos = jj + (q0 - 64)
    return (rel >= 0) & (rel <= 128) & (kpos >= 0) & (kpos < m)


def _fill_padded(dst, src, m):
    zeros = jnp.zeros((64, HD), dst.dtype)
    dst[0:64, :] = zeros
    dst[64 + m:128 + m, :] = zeros
    dst[64:64 + m, :] = src.astype(dst.dtype)


def _group_views(qkn, proj, g):
    m = T // DILS[g]
    cols = (qkn[:, g * 512:(g + 1) * 512], qkn[:, D + g * 512:D + (g + 1) * 512],
            proj[:, 2 * D + g * 512:2 * D + (g + 1) * 512])
    return [a.reshape(m, DILS[g] * 512) for a in cols]


def _attn_a_fwd(qkn, proj, g):
    dil = DILS[g]
    m = T // dil
    nb = m // 128

    def body(q_ref, k_ref, v_ref, o_ref, l_ref, kp, vp):
        _fill_padded(kp, k_ref[...], m)
        _fill_padded(vp, v_ref[...], m)

        def block(b, carry):
            q0 = pl.multiple_of(b * 128, 128)
            kw, vw = kp[pl.ds(q0, 256), :], vp[pl.ds(q0, 256), :]
            s = _dot_nt(q_ref[pl.ds(q0, 128), :], kw) * SCALE
            s = jnp.where(_band_mask(q0, m), s, NEG)
            mx = jnp.max(s, axis=-1, keepdims=True)
            e = jnp.exp(s - mx)
            den = jnp.sum(e, axis=-1, keepdims=True)
            o_ref[pl.ds(q0, 128), :] = _dot((e / den).astype(BF16), vw)
            l_ref[pl.ds(q0, 128), :] = jnp.broadcast_to(mx + jnp.log(den), (128, HD))
            return carry

        lax.fori_loop(0, nb, block, 0)

    blk = pl.BlockSpec((m, HD), lambda h, r: (0, r * 4 + h))
    o, lse = pl.pallas_call(
        body, name=f"attn_a_fwd_{g}", out_shape=[SDS((m, dil * 512), F32)] * 2, grid=(4, dil),
        in_specs=[blk] * 3, out_specs=[blk] * 2,
        scratch_shapes=[pltpu.VMEM((m + 128, HD), BF16), pltpu.VMEM((m + 128, HD), BF16)],
        compiler_params=_params(2))(*_group_views(qkn, proj, g))
    return o.reshape(T, 512), lse.reshape(T, 512)


def _nbr_window(r):
    start = jnp.clip(r - WIN_R // 2, 0, T // GRID_W - WIN_R)
    return start, start - r + (WIN_R - 1)


def _attn_b_fwd(qkn, proj, rpb_flat):
    def body(rpb_ref, q_ref, k_ref, v_ref, o_ref, l_ref, bias_ref, vb):
        h = pl.program_id(0)
        qc = lax.broadcasted_iota(jnp.int32, (GRID_W, 512), 0)
        lane = lax.broadcasted_iota(jnp.int32, (GRID_W, 512), 1)
        kc = lane & (GRID_W - 1)
        dc = jnp.clip(kc - qc, -(WIN_C - 1), WIN_C - 1) + (WIN_C - 1)
        cs = jnp.clip(qc - WIN_C // 2, 0, GRID_W - WIN_C)
        colmask = (kc >= cs) & (kc < cs + WIN_C)
        jrow = lax.broadcasted_iota(jnp.int32, (1, 512), 1) >> 6
        for off in range(8):
            bias_ref[off] = jnp.zeros((GRID_W, 512), F32)
        for e in range(31):
            sel = dc == e
            for off in range(8):
                v = jnp.zeros((1, 512), F32)
                for j in range(8):
                    v = jnp.where(jrow == j, rpb_ref[h * NRPB + (off + j) * 31 + e], v)
                bias_ref[off] = jnp.where(sel, v, bias_ref[off])
        for off in range(8):
            bias_ref[off] = jnp.where(colmask, bias_ref[off], NEG)
        vb[...] = v_ref[...].astype(BF16)

        def row(r, carry):
            start, off = _nbr_window(r)
            q0 = pl.multiple_of(r * GRID_W, GRID_W)
            k0 = pl.multiple_of(start * GRID_W, GRID_W)
            s = _dot_nt(q_ref[pl.ds(q0, GRID_W), :], k_ref[pl.ds(k0, 512), :]) * SCALE + bias_ref[off]
            mx = jnp.max(s, axis=-1, keepdims=True)
            e = jnp.exp(s - mx)
            den = jnp.sum(e, axis=-1, keepdims=True)
            o_ref[pl.ds(q0, GRID_W), :] = _dot((e / den).astype(BF16), vb[pl.ds(k0, 512), :])
            l_ref[pl.ds(q0, GRID_W), :] = jnp.broadcast_to(mx + jnp.log(den), (GRID_W, HD))
            return carry

        lax.fori_loop(0, T // GRID_W, row, 0)

    return pl.pallas_call(
        body, name="attn_b_fwd",
        out_shape=[SDS((T, 512), F32), SDS((T, 512), F32), SDS((4, 8, GRID_W, 512), F32)], grid=(4,),
        in_specs=[pl.BlockSpec(memory_space=pltpu.SMEM),
                  pl.BlockSpec((T, HD), lambda h: (0, NHA + h)),
                  pl.BlockSpec((T, HD), lambda h: (0, NH + NHA + h)),
                  pl.BlockSpec((T, HD), lambda h: (0, 2 * NH + NHA + h))],
        out_specs=[pl.BlockSpec((T, HD), lambda h: (0, h)), pl.BlockSpec((T, HD), lambda h: (0, h)),
                   pl.BlockSpec((None, 8, GRID_W, 512), lambda h: (h, 0, 0, 0))],
        scratch_shapes=[pltpu.VMEM((T, HD), BF16)],
        compiler_params=_params(1))(rpb_flat, qkn, qkn, proj)


def _comb_fwd(os, ls):
    tm = 512

    def body(o0, o1, o2, l0, l1, l2, oa_ref, w0, w1, w2):
        lv = [l0[...], l1[...], l2[...]]
        mx = jnp.maximum(jnp.maximum(lv[0], lv[1]), lv[2])
        ev = [jnp.exp(l - mx) for l in lv]
        den = ev[0] + ev[1] + ev[2]
        wv = [e / den for e in ev]
        oa_ref[...] = (wv[0] * o0[...] + wv[1] * o1[...] + wv[2] * o2[...]).astype(BF16)
        w0[...], w1[...], w2[...] = wv

    spec = pl.BlockSpec((tm, 512), lambda i: (i, 0))
    return pl.pallas_call(
        body, name="comb_fwd", out_shape=[SDS((T, 512), BF16)] + [SDS((T, 512), F32)] * 3, grid=(T // tm,),
        in_specs=[spec] * 6, out_specs=[spec] * 4, compiler_params=_params(1))(*os, *ls)


def _mix_fwd(oa, ob, proj, b_gate, wpa, wpb):
    tm = 256

    def body(oa_ref, ob_ref, ga_ref, gb_ref, ba_ref, bb_ref, wpa_ref, wpb_ref, mixed_ref, ob16_ref):
        oav = oa_ref[...]
        obv = ob_ref[...].astype(BF16)
        ob16_ref[...] = obv
        for s in range(NSH):
            sl = slice(s * 512, (s + 1) * 512)
            ga = _sigmoid(ga_ref[:, sl] + ba_ref[:, sl])
            gb = _sigmoid(gb_ref[:, sl] + bb_ref[:, sl])
            mixed_ref[:, sl] = (ga * _dot(oav, wpa_ref[s]) + gb * _dot(obv, wpb_ref[s])).astype(BF16)

    row = lambda w: pl.BlockSpec((tm, w), lambda i: (i, 0))
    return pl.pallas_call(
        body, name="mix_fwd", out_shape=[SDS((T, D), BF16), SDS((T, 512), BF16)], grid=(T // tm,),
        in_specs=[row(512), row(512),
                  pl.BlockSpec((tm, D), lambda i: (i, 3)), pl.BlockSpec((tm, D), lambda i: (i, 4)),
                  pl.BlockSpec((1, D), lambda i: (0, 0)), pl.BlockSpec((1, D), lambda i: (0, 1)),
                  _resident((NSH, 512, 512), lambda i: (0, 0, 0)), _resident((NSH, 512, 512), lambda i: (0, 0, 0))],
        out_specs=[row(D), row(512)], compiler_params=_params(1))(oa, ob, proj, proj, b_gate, b_gate, wpa, wpb)


def _out_proj_fwd(mixed, w_out, x, g):
    tm = 256

    def body(m_ref, w_ref, x_ref, g_ref, h1_ref, hn_ref):
        h1 = x_ref[...] + _dot(m_ref[...], w_ref[...])
        h1_ref[...] = h1
        r = lax.rsqrt(jnp.mean(h1 * h1, axis=-1, keepdims=True) + EPS)
        hn_ref[...] = (h1 * r * g_ref[...]).astype(BF16)

    row = pl.BlockSpec((tm, D), lambda i: (i, 0))
    return pl.pallas_call(
        body, name="out_proj_fwd", out_shape=[SDS((T, D), F32), SDS((T, D), BF16)], grid=(T // tm,),
        in_specs=[row, _resident((D, D), lambda i: (0, 0)), row, pl.BlockSpec((1, D), lambda i: (0, 0))],
        out_specs=[row, row], compiler_params=_params(1))(mixed, w_out, x, g)


def _ffn_up(hn, w_up):
    tm, tn = 1024, 512
    per = (DFF // NSH) // tn

    def body(h_ref, w_ref, u_ref):
        u_ref[...] = jnp.maximum(_dot(h_ref[...], w_ref[...]), 0.0)

    return pl.pallas_call(
        body, name="ffn_up", out_shape=SDS((T, DFF), F32), grid=(T // tm, DFF // tn),
        in_specs=[pl.BlockSpec((tm, D), lambda i, j: (i, 0)),
                  pl.BlockSpec((None, D, tn), lambda i, j: (j // per, 0, j % per))],
        out_specs=pl.BlockSpec((tm, tn), lambda i, j: (i, j)), compiler_params=_params(2))(hn, w_up)


def _ffn_down_loss(u, w_down, h1, target):
    tm, tk = 512, 512
    nk = DFF // tk

    def body(u_ref, w_ref, h1_ref, t_ref, dy_ref, dy16_ref, loss_ref, acc):
        k = pl.program_id(1)

        @pl.when(k == 0)
        def _():
            acc[...] = jnp.zeros_like(acc)

        uv = u_ref[...]
        acc[...] += _dot((uv * uv).astype(BF16), w_ref[...])

        @pl.when(k == nk - 1)
        def _():
            err = acc[...] + h1_ref[...] - t_ref[...]
            dy = err * (1.0 / D)
            dy_ref[...] = dy
            dy16_ref[...] = dy.astype(BF16)
            part = 0.5 * jnp.sum(jnp.mean(err * err, axis=-1, keepdims=True), axis=0, keepdims=True)
            loss_ref[...] = jnp.broadcast_to(part, (8, 128))

    row = pl.BlockSpec((tm, D), lambda i, k: (i, 0))
    return pl.pallas_call(
        body, name="ffn_down_loss",
        out_shape=[SDS((T, D), F32), SDS((T, D), BF16), SDS((T // tm, 8, 128), F32)], grid=(T // tm, nk),
        in_specs=[pl.BlockSpec((tm, tk), lambda i, k: (i, k)), pl.BlockSpec((tk, D), lambda i, k: (k, 0)), row, row],
        out_specs=[row, row, pl.BlockSpec((None, 8, 128), lambda i, k: (i, 0, 0))],
        scratch_shapes=[pltpu.VMEM((tm, D), F32)], compiler_params=_params(2))(u, w_down, h1, target)


def _ffn_down_bwd(dy16, w_down, u, deps=()):
    tm, tn = 1024, 512

    def body(dy_ref, w_ref, u_ref, du_ref):
        uv = u_ref[...]
        du_ref[...] = jnp.where(uv > 0.0, 2.0 * uv * _dot_nt(dy_ref[...], w_ref[...]), 0.0).astype(BF16)

    return pl.pallas_call(
        _after(body, deps), name="ffn_down_bwd", out_shape=SDS((T, DFF), BF16), grid=(T // tm, DFF // tn),
        in_specs=[DEP_SPEC] * len(deps) + [
            pl.BlockSpec((tm, D), lambda i, j: (i, 0)), pl.BlockSpec((tn, D), lambda i, j: (j, 0)),
            pl.BlockSpec((tm, tn), lambda i, j: (i, j))],
        out_specs=pl.BlockSpec((tm, tn), lambda i, j: (i, j)), compiler_params=_params(2))(*deps, dy16, w_down, u)


def _norm_bwd(xv, dz_in, g):
    r = lax.rsqrt(jnp.mean(xv * xv, axis=-1, keepdims=True) + EPS)
    dg = jnp.sum(xv * r * dz_in, axis=0, keepdims=True)
    dz = dz_in * g
    dx = r * dz - xv * (r * r * r) * jnp.mean(xv * dz, axis=-1, keepdims=True)
    return dx, dg


def _ffn_up_bwd(du, w_up, h1, dy, g, deps=()):
    tm, tk = 512, 1024
    per = (DFF // NSH) // tk
    nk = DFF // tk

    def body(du_ref, w_ref, h1_ref, dy_ref, g_ref, dh1_ref, dh16_ref, dg_ref, acc):
        i, k = pl.program_id(0), pl.program_id(1)

        @pl.when(k == 0)
        def _():
            acc[...] = jnp.zeros_like(acc)

        @pl.when((k == 0) & (i == 0))
        def _():
            dg_ref[...] = jnp.zeros_like(dg_ref)

        acc[...] += _dot_nt(du_ref[...], w_ref[...])

        @pl.when(k == nk - 1)
        def _():
            dx, dg = _norm_bwd(h1_ref[...], acc[...], g_ref[...])
            dh1 = dy_ref[...] + dx
            dh1_ref[...] = dh1
            dh16_ref[...] = dh1.astype(BF16)
            dg_ref[...] += dg

    row = pl.BlockSpec((tm, D), lambda i, k: (i, 0))
    vec = pl.BlockSpec((1, D), lambda i, k: (0, 0))
    return pl.pallas_call(
        _after(body, deps), name="ffn_up_bwd", out_shape=[SDS((T, D), F32), SDS((T, D), BF16), SDS((1, D), F32)],
        grid=(T // tm, nk),
        in_specs=[DEP_SPEC] * len(deps) + [
            pl.BlockSpec((tm, tk), lambda i, k: (i, k)),
            pl.BlockSpec((None, D, tk), lambda i, k: (k // per, 0, k % per)), row, row, vec],
        out_specs=[row, row, vec], scratch_shapes=[pltpu.VMEM((tm, D), F32)],
        compiler_params=_params(2))(*deps, du, w_up, h1, dy, g)


def _mix_bwd(dh16, w_out, oa, ob16, proj, b_gate, wpa, wpb):
    tm = 128

    def body(dh_ref, wo_ref, oa_ref, ob_ref, ga_ref, gb_ref, ba_ref, bb_ref, wpa_ref, wpb_ref,
             dya_ref, dyb_ref, dga_ref, dgb_ref, doa_ref, dob_ref, dba_ref, dbb_ref):
        @pl.when(pl.program_id(0) == 0)
        def _():
            dba_ref[...] = jnp.zeros_like(dba_ref)
            dbb_ref[...] = jnp.zeros_like(dbb_ref)

        oav, obv = oa_ref[...], ob_ref[...]
        doa = jnp.zeros((tm, 512), F32)
        dob = jnp.zeros((tm, 512), F32)
        for s in range(NSH):
            sl = slice(s * 512, (s + 1) * 512)
            dm = _dot_nt(dh_ref[...], wo_ref[sl, :])
            ga = _sigmoid(ga_ref[:, sl] + ba_ref[:, sl])
            gb = _sigmoid(gb_ref[:, sl] + bb_ref[:, sl])
            dya = (dm * ga).astype(BF16)
            dyb = (dm * gb).astype(BF16)
            dza = dm * _dot(oav, wpa_ref[s]) * ga * (1.0 - ga)
            dzb = dm * _dot(obv, wpb_ref[s]) * gb * (1.0 - gb)
            dya_ref[:, sl], dyb_ref[:, sl] = dya, dyb
            dga_ref[:, sl], dgb_ref[:, sl] = dza.astype(BF16), dzb.astype(BF16)
            dba_ref[:, sl] += jnp.sum(dza, axis=0, keepdims=True)
            dbb_ref[:, sl] += jnp.sum(dzb, axis=0, keepdims=True)
            doa += _dot_nt(dya, wpa_ref[s])
            dob += _dot_nt(dyb, wpb_ref[s])
        doa_ref[...], dob_ref[...] = doa, dob

    row = lambda w: pl.BlockSpec((tm, w), lambda i: (i, 0))
    vec = pl.BlockSpec((1, D), lambda i: (0, 0))
    wp = _resident((NSH, 512, 512), lambda i: (0, 0, 0))
    return pl.pallas_call(
        body, name="mix_bwd",
        out_shape=[SDS((T, D), BF16)] * 4 + [SDS((T, 512), F32)] * 2 + [SDS((1, D), F32)] * 2, grid=(T // tm,),
        in_specs=[row(D), _resident((D, D), lambda i: (0, 0)), row(512), row(512),
                  pl.BlockSpec((tm, D), lambda i: (i, 3)), pl.BlockSpec((tm, D), lambda i: (i, 4)),
                  pl.BlockSpec((1, D), lambda i: (0, 0)), pl.BlockSpec((1, D), lambda i: (0, 1)), wp, wp],
        out_specs=[row(D)] * 4 + [row(512)] * 2 + [vec] * 2,
        compiler_params=_params(1))(dh16, w_out, oa, ob16, proj, proj, b_gate, b_gate, wpa, wpb)


def _comb_bwd(doa, os, ws, deps=()):
    tm = 512

    def body(d_ref, o0, o1, o2, w0, w1, w2, cc_ref):
        prod = d_ref[...] * (w0[...] * o0[...] + w1[...] * o1[...] + w2[...] * o2[...])
        for h in range(4):
            sl = slice(h * HD, (h + 1) * HD)
            cc_ref[:, sl] = jnp.broadcast_to(jnp.sum(prod[:, sl], axis=-1, keepdims=True), (tm, HD))

    spec = pl.BlockSpec((tm, 512), lambda i: (i, 0))
    return pl.pallas_call(
        _after(body, deps), name="comb_bwd", out_shape=SDS((T, 512), F32), grid=(T // tm,),
        in_specs=[DEP_SPEC] * len(deps) + [spec] * 7, out_specs=spec,
        compiler_params=_params(1))(*deps, doa, *os, *ws)


def _attn_a_bwd(qkn, proj, doa, lse, w, cc, g):
    dil = DILS[g]
    m = T // dil
    nb = m // 128

    def body(q_ref, k_ref, v_ref, d_ref, l_ref, w_ref, c_ref, dqk_ref, dv_ref, kp, vp, dkp, dvp):
        _fill_padded(kp, k_ref[...], m)
        _fill_padded(vp, v_ref[...], m)
        dkp[...] = jnp.zeros_like(dkp)
        dvp[...] = jnp.zeros_like(dvp)

        def block(b, carry):
            q0 = pl.multiple_of(b * 128, 128)
            rows = pl.ds(q0, 128)
            win = pl.ds(q0, 256)
            qb, kw, vw = q_ref[rows, :], kp[win, :], vp[win, :]
            s = _dot_nt(qb, kw) * SCALE
            s = jnp.where(_band_mask(q0, m), s, NEG)
            wp = _wide(w_ref[rows, :], 2) * jnp.exp(s - _wide(l_ref[rows, :], 2))
            dob = d_ref[rows, :].astype(BF16)
            ds = (wp * (_dot_nt(dob, vw) - _wide(c_ref[rows, :], 2))).astype(BF16)
            dqk_ref[0, rows, :] = _dot(ds, kw) * SCALE
            dkp[win, :] += _dot_tn(ds, qb) * SCALE
            dvp[win, :] += _dot_tn(wp.astype(BF16), dob)
            return carry

        lax.fori_loop(0, nb, block, 0)
        dqk_ref[1] = dkp[64:64 + m, :]
        dv_ref[...] = dvp[64:64 + m, :]

    blk = pl.BlockSpec((m, HD), lambda h, r: (0, r * 4 + h))
    view = lambda a: a.reshape(m, dil * 512)
    dqk, dv = pl.pallas_call(
        body, name=f"attn_a_bwd_{g}", out_shape=[SDS((2, m, dil * 512), F32), SDS((m, dil * 512), F32)], grid=(4, dil),
        in_specs=[blk] * 7,
        out_specs=[pl.BlockSpec((2, m, HD), lambda h, r: (0, 0, r * 4 + h)), blk],
        scratch_shapes=[pltpu.VMEM((m + 128, HD), BF16), pltpu.VMEM((m + 128, HD), BF16),
                        pltpu.VMEM((m + 128, HD), F32), pltpu.VMEM((m + 128, HD), F32)],
        compiler_params=_params(2))(*_group_views(qkn, proj, g), view(doa), view(lse), view(w), view(cc))
    return dqk.reshape(2, T, 512), dv.reshape(T, 512)


def _attn_b_bwd(qkn, proj, dob, ob, lse, bias):
    def body(q_ref, k_ref, v_ref, d_ref, o_ref, l_ref, bias_ref, dqk_ref, dv_ref, drpb_ref, vb, dk_acc, dv_acc, a_acc):
        vb[...] = v_ref[...].astype(BF16)
        dk_acc[...] = jnp.zeros_like(dk_acc)
        dv_acc[...] = jnp.zeros_like(dv_acc)
        a_acc[...] = jnp.zeros_like(a_acc)

        def row(r, carry):
            start, off = _nbr_window(r)
            rows = pl.ds(pl.multiple_of(r * GRID_W, GRID_W), GRID_W)
            win = pl.ds(pl.multiple_of(start * GRID_W, GRID_W), 512)
            qr, kw, vw = q_ref[rows, :], k_ref[win, :], vb[win, :]
            s = _dot_nt(qr, kw) * SCALE + bias_ref[off]
            p = jnp.exp(s - _wide(l_ref[rows, :], 4))
            dov = d_ref[rows, :]
            delta = jnp.sum(dov * o_ref[rows, :], axis=-1, keepdims=True)
            do16 = dov.astype(BF16)
            ds = p * (_dot_nt(do16, vw) - delta)
            a_acc[off] += ds
            ds16 = ds.astype(BF16)
            dqk_ref[0, rows, :] = _dot(ds16, kw) * SCALE
            dk_acc[win, :] += _dot_tn(ds16, qr) * SCALE
            dv_acc[win, :] += _dot_tn(p.astype(BF16), do16)
            return carry

        lax.fori_loop(0, T // GRID_W, row, 0)
        dqk_ref[1] = dk_acc[...]
        dv_ref[...] = dv_acc[...]

        qc = lax.broadcasted_iota(jnp.int32, (GRID_W, 512), 0)
        lane = lax.broadcasted_iota(jnp.int32, (GRID_W, 512), 1)
        dc = jnp.clip((lane & (GRID_W - 1)) - qc, -(WIN_C - 1), WIN_C - 1) + (WIN_C - 1)
        jrow = lax.broadcasted_iota(jnp.int32, (1, 512), 1) >> 6
        dlane = lax.broadcasted_iota(jnp.int32, (1, HD), 1)
        drpb_ref[...] = jnp.zeros_like(drpb_ref)

        def per_dc(e, carry):
            sel = dc == e
            out = jnp.zeros((1, HD), F32)
            for off in range(8):
                col = jnp.sum(jnp.where(sel, a_acc[off], 0.0), axis=0, keepdims=True)
                for j in range(8):
                    part = jnp.sum(jnp.where(jrow == j, col, 0.0), axis=-1, keepdims=True)
                    out = out + jnp.where(dlane == off + j, part, 0.0)
            drpb_ref[pl.ds(e, 1), :] = out
            return carry

        lax.fori_loop(0, 31, per_dc, 0)

    blk = pl.BlockSpec((T, HD), lambda h: (0, h))
    return pl.pallas_call(
        body, name="attn_b_bwd",
        out_shape=[SDS((2, T, 512), F32), SDS((T, 512), F32), SDS((4, 32, HD), F32)], grid=(4,),
        in_specs=[pl.BlockSpec((T, HD), lambda h: (0, NHA + h)),
                  pl.BlockSpec((T, HD), lambda h: (0, NH + NHA + h)),
                  pl.BlockSpec((T, HD), lambda h: (0, 2 * NH + NHA + h)), blk, blk, blk,
                  pl.BlockSpec((None, 8, GRID_W, 512), lambda h: (h, 0, 0, 0))],
        out_specs=[pl.BlockSpec((2, T, HD), lambda h: (0, 0, h)), blk,
                   pl.BlockSpec((None, 32, HD), lambda h: (h, 0, 0))],
        scratch_shapes=[pltpu.VMEM((T, HD), BF16), pltpu.VMEM((T, HD), F32), pltpu.VMEM((T, HD), F32),
                        pltpu.VMEM((8, GRID_W, 512), F32)],
        compiler_params=_params(1))(qkn, qkn, proj, dob, ob, lse, bias)


def _qk_bwd(proj, nw, cos, sin, dqk_groups, dqk_b):
    tm = 256

    def body(p_ref, w_ref, cos_ref, sin_ref, d0, d1, d2, d3, o_ref, dn_ref):
        @pl.when(pl.program_id(1) == 0)
        def _():
            dn_ref[...] = jnp.zeros_like(dn_ref)

        cv, sv = cos_ref[...], sin_ref[...]
        srcs = (d0, d1, d2, d3)
        dna = jnp.zeros((1, HD), F32)
        dnb = jnp.zeros((1, HD), F32)
        for h in range(NH):
            sl = slice(h * HD, (h + 1) * HD)
            dz = srcs[h // 4][:, (h % 4) * HD:(h % 4 + 1) * HD]
            if h < NHA:
                dz = dz * cv + pltpu.roll(dz * sv, 64, 1)
            dx, dg = _norm_bwd(p_ref[:, sl], dz, w_ref[:, sl])
            o_ref[:, sl] = dx.astype(BF16)
            if h < NHA:
                dna += dg
            else:
                dnb += dg
        dn_ref[0:1, :] += dna
        dn_ref[1:2, :] += dnb

    dspec = pl.BlockSpec((None, tm, 512), lambda j, i: (j, i, 0))
    return pl.pallas_call(
        body, name="qk_bwd", out_shape=[SDS((T, 2 * D), BF16), SDS((2, 8, HD), F32)], grid=(2, T // tm),
        in_specs=[pl.BlockSpec((tm, D), lambda j, i: (i, j)),
                  pl.BlockSpec((None, 1, D), lambda j, i: (j, 0, 0)),
                  pl.BlockSpec((tm, HD), lambda j, i: (i, 0)),
                  pl.BlockSpec((tm, HD), lambda j, i: (i, 0)), dspec, dspec, dspec, dspec],
        out_specs=[pl.BlockSpec((tm, D), lambda j, i: (i, j)), pl.BlockSpec((None, 8, HD), lambda j, i: (j, 0, 0))],
        compiler_params=_params(2))(proj, nw, cos, sin, *dqk_groups, dqk_b)


def _in_proj_bwd(dproj, w_in, x, dh1, g, deps=()):
    tm, tk = 512, 1280
    per = (DIN // NSH) // tk
    nk = DIN // tk

    def body(dp_ref, w_ref, x_ref, dh_ref, g_ref, dx_ref, dg_ref, acc):
        i, k = pl.program_id(0), pl.program_id(1)

        @pl.when(k == 0)
        def _():
            acc[...] = jnp.zeros_like(acc)

        @pl.when((k == 0) & (i == 0))
        def _():
            dg_ref[...] = jnp.zeros_like(dg_ref)

        acc[...] += _dot_nt(dp_ref[...], w_ref[...])

        @pl.when(k == nk - 1)
        def _():
            dx, dg = _norm_bwd(x_ref[...], acc[...], g_ref[...])
            dx_ref[...] = dh_ref[...] + dx
            dg_ref[...] += dg

    row = pl.BlockSpec((tm, D), lambda i, k: (i, 0))
    vec = pl.BlockSpec((1, D), lambda i, k: (0, 0))
    return pl.pallas_call(
        _after(body, deps), name="in_proj_bwd", out_shape=[SDS((T, D), F32), SDS((1, D), F32)], grid=(T // tm, nk),
        in_specs=[DEP_SPEC] * len(deps) + [
            pl.BlockSpec((tm, tk), lambda i, k: (i, k)),
            pl.BlockSpec((None, D, tk), lambda i, k: (k // per, 0, k % per)), row, row, vec],
        out_specs=[row, vec], scratch_shapes=[pltpu.VMEM((tm, D), F32)],
        compiler_params=_params(2))(*deps, dproj, w_in, x, dh1, g)


def _grad_w(name, a, g, shard_rows, rows, cols, tr, tc, square=False):
    ni, nj = rows // tr, cols // tc
    if shard_rows:
        a_map, g_map = (lambda s, i, j: (0, s * ni + i)), (lambda s, i, j: (0, j))
    else:
        a_map, g_map = (lambda s, i, j: (0, i)), (lambda s, i, j: (0, s * nj + j))

    def body(a_ref, g_ref, o_ref):
        av = a_ref[...]
        if square:
            av = (av * av).astype(BF16)
        o_ref[...] = _dot_tn(av, g_ref[...]).astype(BF16)

    return pl.pallas_call(
        body, name=name, out_shape=SDS((NSH, rows, cols), BF16), grid=(NSH, ni, nj),
        in_specs=[pl.BlockSpec((T, tr), a_map), pl.BlockSpec((T, tc), g_map)],
        out_specs=pl.BlockSpec((None, tr, tc), lambda s, i, j: (s, i, j)), compiler_params=_params(3))(a, g)


def _adamw(w, g, m, v):
    m = B1 * m + (1.0 - B1) * g
    v = B2 * v + (1.0 - B2) * (g * g)
    m_hat = m / (1.0 - B1 ** STEP)
    v_hat = v / (1.0 - B2 ** STEP)
    delta = -LR * (m_hat / (jnp.sqrt(v_hat) + AEPS) + WD * w)
    return delta, m, v


def _sum_halves(name, place, grads, theirs):
    _, rows, cols = theirs.shape
    tr = _row_tile(rows, cols, 1 << 17)

    def body(place_ref, a_ref, b_ref, o_ref):
        o_ref[...] = (a_ref[...].astype(F32) + b_ref[...].astype(F32)).astype(BF16)

    spec = pl.BlockSpec((NSH, tr, cols), lambda i, p: (0, i, 0))
    return pl.pallas_call(
        body, name=name, out_shape=SDS(theirs.shape, BF16),
        grid_spec=pltpu.PrefetchScalarGridSpec(
            num_scalar_prefetch=1, grid=(rows // tr,),
            in_specs=[pl.BlockSpec((NSH, None, tr, cols), lambda i, p: (0, p[1], i, 0)), spec], out_specs=spec),
        compiler_params=_params(1))(place, grads, theirs)


def _sum_landed(name, place, part, landed):
    _, rows, cols = part.shape
    tr = _row_tile(rows, cols, 1 << 18)

    def body(place_ref, p_ref, l_ref, o_ref):
        o_ref[...] = ((p_ref[...].astype(F32) + l_ref[0].astype(F32)) + l_ref[1].astype(F32)) + l_ref[2].astype(F32)

    return pl.pallas_call(
        body, name=name, out_shape=SDS((2, rows, cols), F32),
        grid_spec=pltpu.PrefetchScalarGridSpec(
            num_scalar_prefetch=1, grid=(rows // tr,),
            in_specs=[pl.BlockSpec((None, tr, cols), lambda i, p: (p[0], i, 0)),
                      pl.BlockSpec((3, tr, cols), lambda i, p: (0, i, 0))],
            out_specs=pl.BlockSpec((None, tr, cols), lambda i, p: (p[1], i, 0))),
        compiler_params=_params(1))(place, part, landed)


def _adam_shard(name, g, w, m, v):
    rows, cols = w.shape
    tr = _row_tile(rows, cols, 1 << 18)

    def body(g_ref, w_ref, m_ref, v_ref, d_ref, nm_ref, nv_ref):
        d_ref[...], nm_ref[...], nv_ref[...] = _adamw(w_ref[...], g_ref[...], m_ref[...], v_ref[...])

    spec = pl.BlockSpec((tr, cols), lambda i: (i, 0))
    return pl.pallas_call(
        body, name=name, out_shape=[SDS((rows, cols), F32)] * 3, grid=(rows // tr,),
        in_specs=[spec] * 4, out_specs=[spec] * 3, compiler_params=_params(1))(g, w, m, v)


def _adam_small(gathered, w, m, v):
    def body(g_ref, w_ref, m_ref, v_ref, go_ref, d_ref, nm_ref, nv_ref):
        g = g_ref[0:SMALL_ROWS, :]
        for dev in range(1, 8):
            g = g + g_ref[dev * SMALL_ROWS:(dev + 1) * SMALL_ROWS, :]
        go_ref[...] = g
        d_ref[...], nm_ref[...], nv_ref[...] = _adamw(w_ref[...], g, m_ref[...], v_ref[...])

    return pl.pallas_call(body, name="adam_small", out_shape=[SDS((SMALL_ROWS, HD), F32)] * 4)(gathered, w, m, v)


SMALL = (("norm_mix", (1, D)), ("b_gate", (1, 2 * D)), ("q_norm_a", (1, HD)), ("k_norm_a", (1, HD)),
         ("q_norm_b", (1, HD)), ("k_norm_b", (1, HD)), ("rpb_b", (1, 4, 15, 31)), ("norm_ffn", (1, D)))


def _pack_small(vals):
    pieces = []
    for (name, shape), val in zip(SMALL, vals):
        flat = val.reshape(-1)
        pad = (-flat.shape[0]) % HD
        pieces.append(jnp.pad(flat, (0, pad)).reshape(-1, HD))
    packed = jnp.concatenate(pieces, axis=0)
    return jnp.pad(packed, ((0, SMALL_ROWS - packed.shape[0]), (0, 0)))


def _unpack_small(packed):
    out, row = [], 0
    for name, shape in SMALL:
        size = int(np.prod(shape))
        nrows = -(-size // HD)
        out.append(packed[row:row + nrows].reshape(-1)[:size].reshape(shape))
        row += nrows
    return out


def kernel(x, norm_mix, w_in, b_gate, q_norm_a, k_norm_a, q_norm_b, k_norm_b, rpb_b, w_proj_a, w_proj_b, w_out, norm_ffn, w_up, w_down, loss_target, m_norm_mix, m_w_in, m_b_gate, m_q_norm_a, m_k_norm_a, m_q_norm_b, m_k_norm_b, m_rpb_b, m_w_proj_a, m_w_proj_b, m_w_out, m_norm_ffn, m_w_up, m_w_down, v_norm_mix, v_w_in, v_b_gate, v_q_norm_a, v_k_norm_a, v_q_norm_b, v_k_norm_b, v_rpb_b, v_w_proj_a, v_w_proj_b, v_w_out, v_norm_ffn, v_w_up, v_w_down):
    big_names = ("w_in", "w_proj_a", "w_proj_b", "w_out", "w_up", "w_down")
    big_w = [a[0] for a in (w_in, w_proj_a, w_proj_b, w_out, w_up, w_down)]
    big_m = [a[0] for a in (m_w_in, m_w_proj_a, m_w_proj_b, m_w_out, m_w_up, m_w_down)]
    big_v = [a[0] for a in (v_w_in, v_w_proj_a, v_w_proj_b, v_w_out, v_w_up, v_w_down)]
    x2, target = x[0], loss_target[0]

    place = jnp.stack([2 * lax.axis_index("x") + lax.axis_index("y"), lax.axis_index("c")]).astype(jnp.int32)
    placed = [_cast_into_place(w, "cast_" + n, place) for n, w in zip(big_names, big_w)]
    groups = ((0,), (1, 2, 3), (4,), (5,))
    started = [_gather_start(f"gather_start_{j}", [placed[i] for i in grp]) for j, grp in enumerate(groups)]

    def gathered(j, after):
        send, recv, _, fulls, _ = started[j]
        fulls = _gather_wait(f"gather_wait_{j}", send, recv, fulls, after)
        fulls = _gather_finish(f"gather_finish_{j}", fulls)
        return [f.reshape(NSH, 2 * f.shape[2], f.shape[3]) for f in fulls]

    def reduce_begin(j, grads):
        grads = [g.reshape(NSH, 2, g.shape[1] // 2, g.shape[2]) for g in grads]
        theirs = _reduce_exchange(f"reduce_exchange_{j}", grads)
        parts = [_sum_halves(f"sum_halves_{j}_{i}", place, a, b) for i, (a, b) in enumerate(zip(grads, theirs))]
        send, recv, parts, lands, token = _reduce_start(f"reduce_start_{j}", parts)
        return (send, recv, parts, lands), token

    big_out = {}

    def reduce_end(j, state, after):
        send, recv, parts, lands = state
        parts, lands = _reduce_wait(f"reduce_wait_{j}", send, recv, parts, lands, after)
        sums = [_sum_landed(f"sum_landed_{j}_{i}", place, p, l) for i, (p, l) in enumerate(zip(parts, lands))]
        for idx, g in zip(groups[j], _reduce_share(f"reduce_share_{j}", sums)):
            g = g.reshape(big_w[idx].shape)
            big_out[idx] =(g, *_adam_shard("adam_" + big_names[idx], g, big_w[idx], big_m[idx], big_v[idx]))
        return big_out[groups[j][-1]][1]

    (win_f,) = gathered(0, ())
    proj, xn = _norm_in_proj(x2, norm_mix, win_f)
    cos, sin = _rope_tables()
    nw = jnp.stack([jnp.concatenate([jnp.tile(q_norm_a, (1, NHA)), jnp.tile(q_norm_b, (1, NH - NHA))], axis=1),
                    jnp.concatenate([jnp.tile(k_norm_a, (1, NHA)), jnp.tile(k_norm_b, (1, NH - NHA))], axis=1)])
    qkn = _qk_prep(proj, nw, cos, sin)
    wpa_f, wpb_f, wout_f = gathered(1, (qkn,))
    wout_f = wout_f.reshape(D, D)
    fwd_a = [_attn_a_fwd(qkn, proj, g) for g in range(3)]
    os, ls = [f[0] for f in fwd_a], [f[1] for f in fwd_a]
    ob, lse_b, bias = _attn_b_fwd(qkn, proj, rpb_b.reshape(-1))
    oa, w0, w1, w2 = _comb_fwd(os, ls)
    ws = [w0, w1, w2]
    mixed, ob16 = _mix_fwd(oa, ob, proj, b_gate, wpa_f, wpb_f)
    h1, hn = _out_proj_fwd(mixed, wout_f, x2, norm_ffn)
    (wup_f,) = gathered(2, (h1,))
    u = _ffn_up(hn, wup_f)
    (wdown_f,) = gathered(3, (u,))
    wdown_f = wdown_f.reshape(DFF, D)
    dy, dy16, loss_parts = _ffn_down_loss(u, wdown_f, h1, target)
    loss = lax.psum(jnp.sum(loss_parts[:, 0, 0]), ("x", "y", "c"))

    g_down = _grad_w("grad_w_down", u, dy16, True, DFF // NSH, D, 512, 1024, square=True)
    red_down, token = reduce_begin(3, [g_down])
    du = _ffn_down_bwd(dy16, wdown_f, u, deps=(token,))
    g_up = _grad_w("grad_w_up", hn, du, False, D, DFF // NSH, 1024, 1024)
    red_up, token = reduce_begin(2, [g_up])
    dh1, dh16, d_norm_ffn = _ffn_up_bwd(du, wup_f, h1, dy, norm_ffn, deps=(token,))
    dya, dyb, dga, dgb, doa, dob, dba, dbb = _mix_bwd(dh16, wout_f, oa, ob16, proj, b_gate, wpa_f, wpb_f)
    g_out = _grad_w("grad_w_out", mixed, dh16, True, D // NSH, D, 512, 1024)
    g_pa = _grad_w("grad_w_proj_a", oa, dya, False, 512, 512, 512, 512)
    g_pb = _grad_w("grad_w_proj_b", ob16, dyb, False, 512, 512, 512, 512)
    red_mid, token = reduce_begin(1, [g_pa, g_pb, g_out])
    cc = _comb_bwd(doa, os, ws, deps=(token,))
    bwd_a = [_attn_a_bwd(qkn, proj, doa, ls[g], ws[g], cc, g) for g in range(3)]
    dqk_b, dv_b, drpb_t = _attn_b_bwd(qkn, proj, dob, ob, lse_b, bias)
    dqk_pre, dn = _qk_bwd(proj, nw, cos, sin, [b[0] for b in bwd_a], dqk_b)
    dv16 = jnp.concatenate([b[1] for b in bwd_a] + [dv_b], axis=1).astype(BF16)
    dproj = jnp.concatenate([dqk_pre, dv16, dga, dgb], axis=1)
    g_in = _grad_w("grad_w_in", xn, dproj, False, D, DIN // NSH, 1024, 1280)
    red_in, token = reduce_begin(0, [g_in])
    grad_x, d_norm_mix = _in_proj_bwd(dproj, win_f, x2, dh1, norm_mix, deps=(token,))

    done = reduce_end(3, red_down, (grad_x,))
    done = reduce_end(2, red_up, (done,))
    done = reduce_end(1, red_mid, (done,))
    reduce_end(0, red_in, (done,))

    d_rpb = jnp.transpose(drpb_t[:, :31, :15], (0, 2, 1))
    small_g = [d_norm_mix, jnp.concatenate([dba, dbb], axis=1), dn[0, 0], dn[1, 0], dn[0, 1], dn[1, 1], d_rpb, d_norm_ffn]
    gathered = _allgather_small(_pack_small(small_g))
    small_w = (norm_mix, b_gate, q_norm_a, k_norm_a, q_norm_b, k_norm_b, rpb_b, norm_ffn)
    small_m = (m_norm_mix, m_b_gate, m_q_norm_a, m_k_norm_a, m_q_norm_b, m_k_norm_b, m_rpb_b, m_norm_ffn)
    small_v = (v_norm_mix, v_b_gate, v_q_norm_a, v_k_norm_a, v_q_norm_b, v_k_norm_b, v_rpb_b, v_norm_ffn)
    small_out = [_unpack_small(p) for p in
                 _adam_small(gathered, _pack_small(small_w), _pack_small(small_m), _pack_small(small_v))]

    order = ("norm_mix", "w_in", "b_gate", "q_norm_a", "k_norm_a", "q_norm_b", "k_norm_b", "rpb_b",
             "w_proj_a", "w_proj_b", "w_out", "norm_ffn", "w_up", "w_down")
    small_idx = {name: i for i, (name, _) in enumerate(SMALL)}
    outs = []
    for kind in range(4):
        for name in order:
            if name in small_idx:
                outs.append(small_out[kind][small_idx[name]])
            else:
                outs.append(big_out[big_names.index(name)][kind][None])
    return (loss, grad_x[None], *outs)
```

```python
import functools

import numpy as np
import jax
import jax.numpy as jnp
from jax import lax
from jax.experimental import pallas as pl
from jax.experimental.pallas import tpu as pltpu

F32, BF16 = jnp.float32, jnp.bfloat16
SDS = jax.ShapeDtypeStruct
MESH = pl.DeviceIdType.MESH

T = 2048
D = 2048
HD = 128
NH, NHA = 16, 12
DIN = 10240
DFF = 8192
NSH = 4
DILS = (1, 4, 16)
EPS = 1e-6
NEG = -1e30
SCALE = HD ** -0.5
GRID_W, WIN_R, WIN_C = 64, 8, 16
NRPB = 15 * 31
VMEM_LIMIT = 56 * 1024 * 1024
B1, B2, LR, AEPS, WD, STEP = 0.9, 0.999, 0.001, 1e-08, 0.01, 10
SMALL_ROWS = 88


def _dot(a, b):
    return jnp.dot(a, b, preferred_element_type=F32)


def _dot_nt(a, b):
    return lax.dot_general(a, b, (((1,), (1,)), ((), ())), preferred_element_type=F32)


def _dot_tn(a, b):
    return lax.dot_general(a, b, (((0,), (0,)), ((), ())), preferred_element_type=F32)


def _params(n):
    return pltpu.CompilerParams(dimension_semantics=("arbitrary",) * n, vmem_limit_bytes=VMEM_LIMIT)


def _resident(shape, index_map):
    return pl.BlockSpec(shape, index_map, pipeline_mode=pl.Buffered(1))


def _sigmoid(z):
    return 1.0 / (1.0 + jnp.exp(-z))


def _wide(v, n):
    return jnp.concatenate([v] * n, axis=1)


def _row_tile(rows, cols, elems):
    tr = 16
    while tr * 2 <= rows and tr * 2 * cols <= elems:
        tr *= 2
    return tr


def _place():
    x, y, c = lax.axis_index("x"), lax.axis_index("y"), lax.axis_index("c")
    peers = [(1 - x, y), (x, 1 - y), (1 - x, 1 - y)]
    return x, y, c, peers


def _cast_into_place(w, name, place):
    rows, cols = w.shape
    hr = rows // 2
    tr = min(hr, 256)
    per = hr // tr

    def body(place_ref, w_ref, o_ref):
        o_ref[...] = w_ref[...].astype(BF16)

    return pl.pallas_call(
        body, name=name, out_shape=SDS((NSH, 2, hr, cols), BF16),
        grid_spec=pltpu.PrefetchScalarGridSpec(
            num_scalar_prefetch=1, grid=(2, per),
            in_specs=[pl.BlockSpec((tr, cols), lambda h, i, p: (h * per + i, 0))],
            out_specs=pl.BlockSpec((None, None, tr, cols), lambda h, i, p: (p[0], h, i, 0))),
        compiler_params=_params(2))(place, w)


ANY_SPEC = pl.BlockSpec(memory_space=pl.ANY)
HBM_SPEC = pl.BlockSpec(memory_space=pltpu.HBM)
SEM_SPEC = pl.BlockSpec(memory_space=pltpu.SEMAPHORE)
DEP_SPEC = pl.BlockSpec((8, 128), lambda *_: (0, 0))
EFFECT = pltpu.SideEffectType.DATAFLOW_SIDE_EFFECTING


def _after(body, deps):
    n = len(deps)
    return (lambda *refs: body(*refs[n:])) if n else body


def _split_start(name, srcs, lands, n_copies, issue):
    n, m = len(srcs), len(lands)

    def body(*refs):
        issue(refs[:n], refs[n:n + m], refs[n + m], refs[n + m + 1])
        refs[-1][...] = jnp.zeros((8, 128), F32)

    arrays = list(srcs) + list(lands)
    outs = pl.pallas_call(
        body, name=name,
        out_shape=(pltpu.SemaphoreType.DMA((n_copies,)), pltpu.SemaphoreType.DMA((n_copies,)),
                   *[pltpu.HBM(a.shape, a.dtype) for a in arrays], SDS((8, 128), F32)),
        in_specs=[HBM_SPEC] * (n + m),
        out_specs=(SEM_SPEC, SEM_SPEC, *[HBM_SPEC] * (n + m), pl.BlockSpec(memory_space=pltpu.VMEM)),
        input_output_aliases={i: 2 + i for i in range(n + m)},
        compiler_params=pltpu.CompilerParams(has_side_effects=EFFECT),
    )(*[pltpu.with_memory_space_constraint(a, pltpu.HBM) for a in arrays])
    return outs[0], outs[1], list(outs[2:2 + n]), list(outs[2 + n:2 + n + m]), outs[-1]


def _split_wait(name, send_sems, recv_sems, srcs, lands, after, wait):
    n, m = len(srcs), len(lands)

    def body(*refs):
        wait(refs[:n], refs[n:n + m], refs[n + m], refs[n + m + 1])

    arrays = list(srcs) + list(lands)
    outs = pl.pallas_call(
        body, name=name, out_shape=[pltpu.HBM(a.shape, a.dtype) for a in arrays],
        in_specs=[HBM_SPEC] * (n + m) + [SEM_SPEC, SEM_SPEC] + [ANY_SPEC] * len(after),
        out_specs=[HBM_SPEC] * (n + m), input_output_aliases={i: i for i in range(n + m)},
        compiler_params=pltpu.CompilerParams(has_side_effects=EFFECT),
    )(*arrays, send_sems, recv_sems, *after)
    return list(outs[:n]), list(outs[n:])


def _gather_start(name, fulls):
    def issue(srcs, dsts, send_sems, recv_sems):
        x, y, c, peers = _place()
        for i in range(len(fulls)):
            mine = dsts[i].at[2 * x + y, c]
            for k, (px, py) in enumerate(peers):
                pltpu.make_async_remote_copy(
                    src_ref=mine, dst_ref=mine, send_sem=send_sems.at[3 * i + k],
                    recv_sem=recv_sems.at[3 * i + k], device_id=(px, py, c), device_id_type=MESH).start()

    return _split_start(name, [], fulls, 3 * len(fulls), issue)


def _gather_wait(name, send_sems, recv_sems, fulls, after):
    def wait(srcs, dsts, send_sems, recv_sems):
        x, y, c, peers = _place()
        for i in range(len(fulls)):
            for k, (px, py) in enumerate(peers):
                cp = pltpu.make_async_remote_copy(
                    src_ref=dsts[i].at[2 * x + y, c], dst_ref=dsts[i].at[2 * px + py, c],
                    send_sem=send_sems.at[3 * i + k], recv_sem=recv_sems.at[3 * i + k],
                    device_id=(px, py, c), device_id_type=MESH)
                cp.wait_send()
                cp.wait_recv()

    return _split_wait(name, send_sems, recv_sems, [], fulls, after, wait)[1]


def _gather_finish(name, fulls):
    n = len(fulls)

    def body(*refs):
        fin, fout = refs[:n], refs[n:2 * n]
        send_sems, recv_sems = refs[2 * n:]
        x, y, c, peers = _place()

        def copy(i, k, half):
            px, py = peers[k]
            return pltpu.make_async_remote_copy(
                src_ref=fin[i].at[2 * px + py, half], dst_ref=fout[i].at[2 * px + py, half],
                send_sem=send_sems.at[3 * i + k], recv_sem=recv_sems.at[3 * i + k],
                device_id=(x, y, 1 - c), device_id_type=MESH)

        sends = [copy(i, k, c) for i in range(n) for k in range(3)]
        for cp in sends:
            cp.start()
        for i in range(n):
            for k in range(3):
                copy(i, k, 1 - c).wait_recv()
        for cp in sends:
            cp.wait_send()

    return pl.pallas_call(
        body, name=name, out_shape=[SDS(f.shape, f.dtype) for f in fulls],
        in_specs=[ANY_SPEC] * n, out_specs=[ANY_SPEC] * n, input_output_aliases={i: i for i in range(n)},
        scratch_shapes=[pltpu.SemaphoreType.DMA((3 * n,)), pltpu.SemaphoreType.DMA((3 * n,))])(*fulls)


def _reduce_exchange(name, grads):
    n = len(grads)

    def body(*refs):
        ins, theirs = refs[:n], refs[n:2 * n]
        send_sems, recv_sems = refs[2 * n:]
        x, y, c, _ = _place()
        copies = []
        for i in range(n):
            cp = pltpu.make_async_remote_copy(
                src_ref=ins[i].at[:, 1 - c], dst_ref=theirs[i], send_sem=send_sems.at[i],
                recv_sem=recv_sems.at[i], device_id=(x, y, 1 - c), device_id_type=MESH)
            cp.start()
            copies.append(cp)
        for cp in copies:
            cp.wait_recv()
            cp.wait_send()

    return pl.pallas_call(
        body, name=name, out_shape=[SDS((NSH,) + g.shape[2:], g.dtype) for g in grads],
        in_specs=[ANY_SPEC] * n, out_specs=[ANY_SPEC] * n,
        scratch_shapes=[pltpu.SemaphoreType.DMA((n,)), pltpu.SemaphoreType.DMA((n,))])(*grads)


def _reduce_start(name, parts):
    lands = [lax.empty((3,) + p.shape[1:], p.dtype) for p in parts]

    def issue(srcs, dsts, send_sems, recv_sems):
        x, y, c, peers = _place()
        for i in range(len(parts)):
            for k, (px, py) in enumerate(peers):
                pltpu.make_async_remote_copy(
                    src_ref=srcs[i].at[2 * px + py], dst_ref=dsts[i].at[k], send_sem=send_sems.at[3 * i + k],
                    recv_sem=recv_sems.at[3 * i + k], device_id=(px, py, c), device_id_type=MESH).start()

    return _split_start(name, parts, lands, 3 * len(parts), issue)


def _reduce_wait(name, send_sems, recv_sems, parts, lands, after):
    def wait(srcs, dsts, send_sems, recv_sems):
        x, y, c, peers = _place()
        for i in range(len(parts)):
            for k, (px, py) in enumerate(peers):
                cp = pltpu.make_async_remote_copy(
                    src_ref=srcs[i].at[2 * px + py], dst_ref=dsts[i].at[k], send_sem=send_sems.at[3 * i + k],
                    recv_sem=recv_sems.at[3 * i + k], device_id=(px, py, c), device_id_type=MESH)
                cp.wait_send()
                cp.wait_recv()

    return _split_wait(name, send_sems, recv_sems, parts, lands, after, wait)


def _reduce_share(name, sums):
    n = len(sums)

    def body(*refs):
        ins, outs = refs[:n], refs[n:2 * n]
        send_sems, recv_sems = refs[2 * n:]
        x, y, c, _ = _place()
        copies = []
        for i in range(n):
            cp = pltpu.make_async_remote_copy(
                src_ref=ins[i].at[c], dst_ref=outs[i].at[c], send_sem=send_sems.at[i], recv_sem=recv_sems.at[i],
                device_id=(x, y, 1 - c), device_id_type=MESH)
            cp.start()
            copies.append(cp)
        for i, cp in enumerate(copies):
            pltpu.make_async_remote_copy(
                src_ref=ins[i].at[c], dst_ref=outs[i].at[1 - c], send_sem=send_sems.at[i],
                recv_sem=recv_sems.at[i], device_id=(x, y, 1 - c), device_id_type=MESH).wait_recv()
            cp.wait_send()

    return pl.pallas_call(
        body, name=name, out_shape=[SDS(s.shape, s.dtype) for s in sums],
        in_specs=[ANY_SPEC] * n, out_specs=[ANY_SPEC] * n, input_output_aliases={i: i for i in range(n)},
        scratch_shapes=[pltpu.SemaphoreType.DMA((n,)), pltpu.SemaphoreType.DMA((n,))])(*sums)


def _allgather_small(v, after):
    m_per, n = v.shape

    def body(x_ref, after_ref, out_ref, send_sems, recv_sems, local_sem):
        x, y, c = lax.axis_index("x"), lax.axis_index("y"), lax.axis_index("c")
        me, sibling = (x, y, c), (x, y, 1 - c)
        chips = [(1 - x, y), (x, 1 - y), (1 - x, 1 - y)]

        def rows(px, py, pc):
            return out_ref.at[pl.ds((4 * px + 2 * py + pc) * m_per, m_per), :]

        def copy(k, block, to, src=None):
            return pltpu.make_async_remote_copy(
                src_ref=rows(*block) if src is None else src, dst_ref=rows(*block),
                send_sem=send_sems.at[k], recv_sem=recv_sems.at[k], device_id=to, device_id_type=MESH)

        mine = pltpu.make_async_copy(x_ref, rows(*me), local_sem)
        mine.start()
        first = [copy(0, me, sibling, src=x_ref)]
        first += [copy(1 + j, me, (*chip, c), src=x_ref) for j, chip in enumerate(chips)]
        for cp in first:
            cp.start()
        passed = [copy(4 + j, (*chip, c), sibling) for j, chip in enumerate(chips)]
        for j, chip in enumerate(chips):
            copy(1 + j, (*chip, c), me).wait_recv()
            passed[j].start()
        copy(0, sibling, me).wait_recv()
        for j, chip in enumerate(chips):
            copy(4 + j, (*chip, 1 - c), me).wait_recv()
        for cp in first + passed:
            cp.wait_send()
        mine.wait()

    return pl.pallas_call(
        body, name="allgather_small", out_shape=SDS((8 * m_per, n), v.dtype),
        in_specs=[pl.BlockSpec(memory_space=pltpu.VMEM), ANY_SPEC], out_specs=pl.BlockSpec(memory_space=pltpu.VMEM),
        scratch_shapes=[pltpu.SemaphoreType.DMA((7,)), pltpu.SemaphoreType.DMA((7,)), pltpu.SemaphoreType.DMA])(v, after)


def _norm_in_proj(x, g, w_full):
    tm, tn = 512, 512
    per = (DIN // NSH) // tn

    def body(x_ref, g_ref, w_ref, proj_ref, xn_ref):
        @pl.when(pl.program_id(1) == 0)
        def _():
            xv = x_ref[...]
            r = lax.rsqrt(jnp.mean(xv * xv, axis=-1, keepdims=True) + EPS)
            xn_ref[...] = (xv * r * g_ref[...]).astype(BF16)

        proj_ref[...] = _dot(xn_ref[...], w_ref[...])

    return pl.pallas_call(
        body, name="norm_in_proj", out_shape=[SDS((T, DIN), F32), SDS((T, D), BF16)],
        grid=(T // tm, DIN // tn),
        in_specs=[pl.BlockSpec((tm, D), lambda i, j: (i, 0)),
                  pl.BlockSpec((1, D), lambda i, j: (0, 0)),
                  pl.BlockSpec((None, D, tn), lambda i, j: (j // per, 0, j % per))],
        out_specs=[pl.BlockSpec((tm, tn), lambda i, j: (i, j)),
                   pl.BlockSpec((tm, D), lambda i, j: (i, 0))],
        compiler_params=_params(2))(x, g, w_full)


def _rope_tables():
    pos = np.arange(T, dtype=np.float32)
    inv = (10000.0 ** (-np.arange(0, HD, 2, dtype=np.float32) / HD)).astype(np.float32)
    ang = (pos[:, None] * inv[None, :]).astype(np.float32)
    cos, sin = np.cos(ang).astype(np.float32), np.sin(ang).astype(np.float32)
    return (jnp.asarray(np.concatenate([cos, cos], axis=1)), jnp.asarray(np.concatenate([-sin, sin], axis=1)))


def _qk_prep(proj, nw, cos, sin):
    tm = 256

    def body(p_ref, w_ref, cos_ref, sin_ref, o_ref):
        cv, sv = cos_ref[...], sin_ref[...]
        for h in range(NH):
            sl = slice(h * HD, (h + 1) * HD)
            xv = p_ref[:, sl]
            r = lax.rsqrt(jnp.mean(xv * xv, axis=-1, keepdims=True) + EPS)
            z = xv * r * w_ref[:, sl]
            if h < NHA:
                z = z * cv + pltpu.roll(z, 64, 1) * sv
            o_ref[:, sl] = z.astype(BF16)

    return pl.pallas_call(
        body, name="qk_prep", out_shape=SDS((T, 2 * D), BF16), grid=(T // tm, 2),
        in_specs=[pl.BlockSpec((tm, D), lambda i, j: (i, j)),
                  pl.BlockSpec((None, 1, D), lambda i, j: (j, 0, 0)),
                  pl.BlockSpec((tm, HD), lambda i, j: (i, 0)),
                  pl.BlockSpec((tm, HD), lambda i, j: (i, 0))],
        out_specs=pl.BlockSpec((tm, D), lambda i, j: (i, j)),
        compiler_params=_params(2))(proj, nw, cos, sin)


def _band_mask(q0, m):
    ii = lax.broadcasted_iota(jnp.int32, (128, 256), 0)
    jj = lax.broadcasted_iota(jnp.int32, (128, 256), 1)
    rel = jj - ii
    kpos = jj + (q0 - 64)
    return (rel >= 0) & (rel <= 128) & (kpos >= 0) & (kpos < m)


def _fill_padded(dst, src, m):
    zeros = jnp.zeros((64, HD), dst.dtype)
    dst[0:64, :] = zeros
    dst[64 + m:128 + m, :] = zeros
    dst[64:64 + m, :] = src.astype(dst.dtype)


def _group_views(qkn, proj, g):
    m = T // DILS[g]
    cols = (qkn[:, g * 512:(g + 1) * 512], qkn[:, D + g * 512:D + (g + 1) * 512],
            proj[:, 2 * D + g * 512:2 * D + (g + 1) * 512])
    return [a.reshape(m, DILS[g] * 512) for a in cols]


def _attn_a_fwd(qkn, proj, g):
    dil = DILS[g]
    m = T // dil
    nb = m // 128

    def body(q_ref, k_ref, v_ref, o_ref, l_ref, kp, vp):
        _fill_padded(kp, k_ref[...], m)
        _fill_padded(vp, v_ref[...], m)

        def block(b, carry):
            q0 = pl.multiple_of(b * 128, 128)
            kw, vw = kp[pl.ds(q0, 256), :], vp[pl.ds(q0, 256), :]
            s = _dot_nt(q_ref[pl.ds(q0, 128), :], kw) * SCALE
            s = jnp.where(_band_mask(q0, m), s, NEG)
            mx = jnp.max(s, axis=-1, keepdims=True)
            e = jnp.exp(s - mx)
            den = jnp.sum(e, axis=-1, keepdims=True)
            o_ref[pl.ds(q0, 128), :] = _dot((e / den).astype(BF16), vw)
            l_ref[pl.ds(q0, 128), :] = jnp.broadcast_to(mx + jnp.log(den), (128, HD))
            return carry

        lax.fori_loop(0, nb, block, 0)

    blk = pl.BlockSpec((m, HD), lambda h, r: (0, r * 4 + h))
    o, lse = pl.pallas_call(
        body, name=f"attn_a_fwd_{g}", out_shape=[SDS((m, dil * 512), F32)] * 2, grid=(4, dil),
        in_specs=[blk] * 3, out_specs=[blk] * 2,
        scratch_shapes=[pltpu.VMEM((m + 128, HD), BF16), pltpu.VMEM((m + 128, HD), BF16)],
        compiler_params=_params(2))(*_group_views(qkn, proj, g))
    return o.reshape(T, 512), lse.reshape(T, 512)


def _nbr_window(r):
    start = jnp.clip(r - WIN_R // 2, 0, T // GRID_W - WIN_R)
    return start, start - r + (WIN_R - 1)


def _attn_b_fwd(qkn, proj, rpb_flat):
    def body(rpb_ref, q_ref, k_ref, v_ref, o_ref, l_ref, bias_ref, vb):
        h = pl.program_id(0)
        qc = lax.broadcasted_iota(jnp.int32, (GRID_W, 512), 0)
        lane = lax.broadcasted_iota(jnp.int32, (GRID_W, 512), 1)
        kc = lane & (GRID_W - 1)
        dc = jnp.clip(kc - qc, -(WIN_C - 1), WIN_C - 1) + (WIN_C - 1)
        cs = jnp.clip(qc - WIN_C // 2, 0, GRID_W - WIN_C)
        colmask = (kc >= cs) & (kc < cs + WIN_C)
        jrow = lax.broadcasted_iota(jnp.int32, (1, 512), 1) >> 6
        for off in range(8):
            bias_ref[off] = jnp.zeros((GRID_W, 512), F32)
        for e in range(31):
            sel = dc == e
            for off in range(8):
                v = jnp.zeros((1, 512), F32)
                for j in range(8):
                    v = jnp.where(jrow == j, rpb_ref[h * NRPB + (off + j) * 31 + e], v)
                bias_ref[off] = jnp.where(sel, v, bias_ref[off])
        for off in range(8):
            bias_ref[off] = jnp.where(colmask, bias_ref[off], NEG)
        vb[...] = v_ref[...].astype(BF16)

        def row(r, carry):
            start, off = _nbr_window(r)
            q0 = pl.multiple_of(r * GRID_W, GRID_W)
            k0 = pl.multiple_of(start * GRID_W, GRID_W)
            s = _dot_nt(q_ref[pl.ds(q0, GRID_W), :], k_ref[pl.ds(k0, 512), :]) * SCALE + bias_ref[off]
            mx = jnp.max(s, axis=-1, keepdims=True)
            e = jnp.exp(s - mx)
            den = jnp.sum(e, axis=-1, keepdims=True)
            o_ref[pl.ds(q0, GRID_W), :] = _dot((e / den).astype(BF16), vb[pl.ds(k0, 512), :])
            l_ref[pl.ds(q0, GRID_W), :] = jnp.broadcast_to(mx + jnp.log(den), (GRID_W, HD))
            return carry

        lax.fori_loop(0, T // GRID_W, row, 0)

    return pl.pallas_call(
        body, name="attn_b_fwd",
        out_shape=[SDS((T, 512), F32), SDS((T, 512), F32), SDS((4, 8, GRID_W, 512), F32)], grid=(4,),
        in_specs=[pl.BlockSpec(memory_space=pltpu.SMEM),
                  pl.BlockSpec((T, HD), lambda h: (0, NHA + h)),
                  pl.BlockSpec((T, HD), lambda h: (0, NH + NHA + h)),
                  pl.BlockSpec((T, HD), lambda h: (0, 2 * NH + NHA + h))],
        out_specs=[pl.BlockSpec((T, HD), lambda h: (0, h)), pl.BlockSpec((T, HD), lambda h: (0, h)),
                   pl.BlockSpec((None, 8, GRID_W, 512), lambda h: (h, 0, 0, 0))],
        scratch_shapes=[pltpu.VMEM((T, HD), BF16)],
        compiler_params=_params(1))(rpb_flat, qkn, qkn, proj)


def _comb_fwd(os, ls):
    tm = 512

    def body(o0, o1, o2, l0, l1, l2, oa_ref, w0, w1, w2):
        lv = [l0[...], l1[...], l2[...]]
        mx = jnp.maximum(jnp.maximum(lv[0], lv[1]), lv[2])
        ev = [jnp.exp(l - mx) for l in lv]
        den = ev[0] + ev[1] + ev[2]
        wv = [e / den for e in ev]
        oa_ref[...] = (wv[0] * o0[...] + wv[1] * o1[...] + wv[2] * o2[...]).astype(BF16)
        w0[...], w1[...], w2[...] = wv

    spec = pl.BlockSpec((tm, 512), lambda i: (i, 0))
    return pl.pallas_call(
        body, name="comb_fwd", out_shape=[SDS((T, 512), BF16)] + [SDS((T, 512), F32)] * 3, grid=(T // tm,),
        in_specs=[spec] * 6, out_specs=[spec] * 4, compiler_params=_params(1))(*os, *ls)


def _mix_fwd(oa, ob, proj, b_gate, wpa, wpb):
    tm = 256

    def body(oa_ref, ob_ref, ga_ref, gb_ref, ba_ref, bb_ref, wpa_ref, wpb_ref, mixed_ref, ob16_ref):
        oav = oa_ref[...]
        obv = ob_ref[...].astype(BF16)
        ob16_ref[...] = obv
        for s in range(NSH):
            sl = slice(s * 512, (s + 1) * 512)
            ga = _sigmoid(ga_ref[:, sl] + ba_ref[:, sl])
            gb = _sigmoid(gb_ref[:, sl] + bb_ref[:, sl])
            mixed_ref[:, sl] = (ga * _dot(oav, wpa_ref[s]) + gb * _dot(obv, wpb_ref[s])).astype(BF16)

    row = lambda w: pl.BlockSpec((tm, w), lambda i: (i, 0))
    return pl.pallas_call(
        body, name="mix_fwd", out_shape=[SDS((T, D), BF16), SDS((T, 512), BF16)], grid=(T // tm,),
        in_specs=[row(512), row(512),
                  pl.BlockSpec((tm, D), lambda i: (i, 3)), pl.BlockSpec((tm, D), lambda i: (i, 4)),
                  pl.BlockSpec((1, D), lambda i: (0, 0)), pl.BlockSpec((1, D), lambda i: (0, 1)),
                  _resident((NSH, 512, 512), lambda i: (0, 0, 0)), _resident((NSH, 512, 512), lambda i: (0, 0, 0))],
        out_specs=[row(D), row(512)], compiler_params=_params(1))(oa, ob, proj, proj, b_gate, b_gate, wpa, wpb)


def _out_proj_fwd(mixed, w_out, x, g):
    tm = 256

    def body(m_ref, w_ref, x_ref, g_ref, h1_ref, hn_ref):
        h1 = x_ref[...] + _dot(m_ref[...], w_ref[...])
        h1_ref[...] = h1
        r = lax.rsqrt(jnp.mean(h1 * h1, axis=-1, keepdims=True) + EPS)
        hn_ref[...] = (h1 * r * g_ref[...]).astype(BF16)

    row = pl.BlockSpec((tm, D), lambda i: (i, 0))
    return pl.pallas_call(
        body, name="out_proj_fwd", out_shape=[SDS((T, D), F32), SDS((T, D), BF16)], grid=(T // tm,),
        in_specs=[row, _resident((D, D), lambda i: (0, 0)), row, pl.BlockSpec((1, D), lambda i: (0, 0))],
        out_specs=[row, row], compiler_params=_params(1))(mixed, w_out, x, g)


def _ffn_up(hn, w_up):
    tm, tn = 1024, 512
    per = (DFF // NSH) // tn

    def body(h_ref, w_ref, u_ref):
        u_ref[...] = jnp.maximum(_dot(h_ref[...], w_ref[...]), 0.0)

    return pl.pallas_call(
        body, name="ffn_up", out_shape=SDS((T, DFF), F32), grid=(T // tm, DFF // tn),
        in_specs=[pl.BlockSpec((tm, D), lambda i, j: (i, 0)),
                  pl.BlockSpec((None, D, tn), lambda i, j: (j // per, 0, j % per))],
        out_specs=pl.BlockSpec((tm, tn), lambda i, j: (i, j)), compiler_params=_params(2))(hn, w_up)


def _ffn_down_loss(u, w_down, h1, target):
    tm, tk = 512, 512
    nk = DFF // tk

    def body(u_ref, w_ref, h1_ref, t_ref, dy_ref, dy16_ref, loss_ref, acc):
        k = pl.program_id(1)

        @pl.when(k == 0)
        def _():
            acc[...] = jnp.zeros_like(acc)

        uv = u_ref[...]
        acc[...] += _dot((uv * uv).astype(BF16), w_ref[...])

        @pl.when(k == nk - 1)
        def _():
            err = acc[...] + h1_ref[...] - t_ref[...]
            dy = err * (1.0 / D)
            dy_ref[...] = dy
            dy16_ref[...] = dy.astype(BF16)
            part = 0.5 * jnp.sum(jnp.mean(err * err, axis=-1, keepdims=True), axis=0, keepdims=True)
            loss_ref[...] = jnp.broadcast_to(part, (8, 128))

    row = pl.BlockSpec((tm, D), lambda i, k: (i, 0))
    return pl.pallas_call(
        body, name="ffn_down_loss",
        out_shape=[SDS((T, D), F32), SDS((T, D), BF16), SDS((T // tm, 8, 128), F32)], grid=(T // tm, nk),
        in_specs=[pl.BlockSpec((tm, tk), lambda i, k: (i, k)), pl.BlockSpec((tk, D), lambda i, k: (k, 0)), row, row],
        out_specs=[row, row, pl.BlockSpec((None, 8, 128), lambda i, k: (i, 0, 0))],
        scratch_shapes=[pltpu.VMEM((tm, D), F32)], compiler_params=_params(2))(u, w_down, h1, target)


def _ffn_down_bwd(dy16, w_down, u, deps=()):
    tm, tn = 1024, 512

    def body(dy_ref, w_ref, u_ref, du_ref):
        uv = u_ref[...]
        du_ref[...] = jnp.where(uv > 0.0, 2.0 * uv * _dot_nt(dy_ref[...], w_ref[...]), 0.0).astype(BF16)

    return pl.pallas_call(
        _after(body, deps), name="ffn_down_bwd", out_shape=SDS((T, DFF), BF16), grid=(T // tm, DFF // tn),
        in_specs=[DEP_SPEC] * len(deps) + [
            pl.BlockSpec((tm, D), lambda i, j: (i, 0)), pl.BlockSpec((tn, D), lambda i, j: (j, 0)),
            pl.BlockSpec((tm, tn), lambda i, j: (i, j))],
        out_specs=pl.BlockSpec((tm, tn), lambda i, j: (i, j)), compiler_params=_params(2))(*deps, dy16, w_down, u)


def _norm_bwd(xv, dz_in, g):
    r = lax.rsqrt(jnp.mean(xv * xv, axis=-1, keepdims=True) + EPS)
    dg = jnp.sum(xv * r * dz_in, axis=0, keepdims=True)
    dz = dz_in * g
    dx = r * dz - xv * (r * r * r) * jnp.mean(xv * dz, axis=-1, keepdims=True)
    return dx, dg


def _ffn_up_bwd(du, w_up, h1, dy, g, deps=()):
    tm, tk = 512, 1024
    per = (DFF // NSH) // tk
    nk = DFF // tk

    def body(du_ref, w_ref, h1_ref, dy_ref, g_ref, dh1_ref, dh16_ref, dg_ref, acc):
        i, k = pl.program_id(0), pl.program_id(1)

        @pl.when(k == 0)
        def _():
            acc[...] = jnp.zeros_like(acc)

        @pl.when((k == 0) & (i == 0))
        def _():
            dg_ref[...] = jnp.zeros_like(dg_ref)

        acc[...] += _dot_nt(du_ref[...], w_ref[...])

        @pl.when(k == nk - 1)
        def _():
            dx, dg = _norm_bwd(h1_ref[...], acc[...], g_ref[...])
            dh1 = dy_ref[...] + dx
            dh1_ref[...] = dh1
            dh16_ref[...] = dh1.astype(BF16)
            dg_ref[...] += dg

    row = pl.BlockSpec((tm, D), lambda i, k: (i, 0))
    vec = pl.BlockSpec((1, D), lambda i, k: (0, 0))
    return pl.pallas_call(
        _after(body, deps), name="ffn_up_bwd", out_shape=[SDS((T, D), F32), SDS((T, D), BF16), SDS((1, D), F32)],
        grid=(T // tm, nk),
        in_specs=[DEP_SPEC] * len(deps) + [
            pl.BlockSpec((tm, tk), lambda i, k: (i, k)),
            pl.BlockSpec((None, D, tk), lambda i, k: (k // per, 0, k % per)), row, row, vec],
        out_specs=[row, row, vec], scratch_shapes=[pltpu.VMEM((tm, D), F32)],
        compiler_params=_params(2))(*deps, du, w_up, h1, dy, g)


def _mix_bwd(dh16, w_out, oa, ob16, proj, b_gate, wpa, wpb):
    tm = 128

    def body(dh_ref, wo_ref, oa_ref, ob_ref, ga_ref, gb_ref, ba_ref, bb_ref, wpa_ref, wpb_ref,
             dya_ref, dyb_ref, dga_ref, dgb_ref, doa_ref, dob_ref, dba_ref, dbb_ref):
        @pl.when(pl.program_id(0) == 0)
        def _():
            dba_ref[...] = jnp.zeros_like(dba_ref)
            dbb_ref[...] = jnp.zeros_like(dbb_ref)

        oav, obv = oa_ref[...], ob_ref[...]
        doa = jnp.zeros((tm, 512), F32)
        dob = jnp.zeros((tm, 512), F32)
        for s in range(NSH):
            sl = slice(s * 512, (s + 1) * 512)
            dm = _dot_nt(dh_ref[...], wo_ref[sl, :])
            ga = _sigmoid(ga_ref[:, sl] + ba_ref[:, sl])
            gb = _sigmoid(gb_ref[:, sl] + bb_ref[:, sl])
            dya = (dm * ga).astype(BF16)
            dyb = (dm * gb).astype(BF16)
            dza = dm * _dot(oav, wpa_ref[s]) * ga * (1.0 - ga)
            dzb = dm * _dot(obv, wpb_ref[s]) * gb * (1.0 - gb)
            dya_ref[:, sl], dyb_ref[:, sl] = dya, dyb
            dga_ref[:, sl], dgb_ref[:, sl] = dza.astype(BF16), dzb.astype(BF16)
            dba_ref[:, sl] += jnp.sum(dza, axis=0, keepdims=True)
            dbb_ref[:, sl] += jnp.sum(dzb, axis=0, keepdims=True)
            doa += _dot_nt(dya, wpa_ref[s])
            dob += _dot_nt(dyb, wpb_ref[s])
        doa_ref[...], dob_ref[...] = doa, dob

    row = lambda w: pl.BlockSpec((tm, w), lambda i: (i, 0))
    vec = pl.BlockSpec((1, D), lambda i: (0, 0))
    wp = _resident((NSH, 512, 512), lambda i: (0, 0, 0))
    return pl.pallas_call(
        body, name="mix_bwd",
        out_shape=[SDS((T, D), BF16)] * 4 + [SDS((T, 512), F32)] * 2 + [SDS((1, D), F32)] * 2, grid=(T // tm,),
        in_specs=[row(D), _resident((D, D), lambda i: (0, 0)), row(512), row(512),
                  pl.BlockSpec((tm, D), lambda i: (i, 3)), pl.BlockSpec((tm, D), lambda i: (i, 4)),
                  pl.BlockSpec((1, D), lambda i: (0, 0)), pl.BlockSpec((1, D), lambda i: (0, 1)), wp, wp],
        out_specs=[row(D)] * 4 + [row(512)] * 2 + [vec] * 2,
        compiler_params=_params(1))(dh16, w_out, oa, ob16, proj, proj, b_gate, b_gate, wpa, wpb)


def _comb_bwd(doa, os, ws, deps=()):
    tm = 512

    def body(d_ref, o0, o1, o2, w0, w1, w2, cc_ref):
        prod = d_ref[...] * (w0[...] * o0[...] + w1[...] * o1[...] + w2[...] * o2[...])
        for h in range(4):
            sl = slice(h * HD, (h + 1) * HD)
            cc_ref[:, sl] = jnp.broadcast_to(jnp.sum(prod[:, sl], axis=-1, keepdims=True), (tm, HD))

    spec = pl.BlockSpec((tm, 512), lambda i: (i, 0))
    return pl.pallas_call(
        _after(body, deps), name="comb_bwd", out_shape=SDS((T, 512), F32), grid=(T // tm,),
        in_specs=[DEP_SPEC] * len(deps) + [spec] * 7, out_specs=spec,
        compiler_params=_params(1))(*deps, doa, *os, *ws)


def _attn_a_bwd(qkn, proj, doa, lse, w, cc, g):
    dil = DILS[g]
    m = T // dil
    nb = m // 128

    def body(q_ref, k_ref, v_ref, d_ref, l_ref, w_ref, c_ref, dqk_ref, dv_ref, kp, vp, dkp, dvp):
        _fill_padded(kp, k_ref[...], m)
        _fill_padded(vp, v_ref[...], m)
        dkp[...] = jnp.zeros_like(dkp)
        dvp[...] = jnp.zeros_like(dvp)

        def block(b, carry):
            q0 = pl.multiple_of(b * 128, 128)
            rows = pl.ds(q0, 128)
            win = pl.ds(q0, 256)
            qb, kw, vw = q_ref[rows, :], kp[win, :], vp[win, :]
            s = _dot_nt(qb, kw) * SCALE
            s = jnp.where(_band_mask(q0, m), s, NEG)
            wp = _wide(w_ref[rows, :], 2) * jnp.exp(s - _wide(l_ref[rows, :], 2))
            dob = d_ref[rows, :].astype(BF16)
            ds = (wp * (_dot_nt(dob, vw) - _wide(c_ref[rows, :], 2))).astype(BF16)
            dqk_ref[0, rows, :] = _dot(ds, kw) * SCALE
            dkp[win, :] += _dot_tn(ds, qb) * SCALE
            dvp[win, :] += _dot_tn(wp.astype(BF16), dob)
            return carry

        lax.fori_loop(0, nb, block, 0)
        dqk_ref[1] = dkp[64:64 + m, :]
        dv_ref[...] = dvp[64:64 + m, :]

    blk = pl.BlockSpec((m, HD), lambda h, r: (0, r * 4 + h))
    view = lambda a: a.reshape(m, dil * 512)
    dqk, dv = pl.pallas_call(
        body, name=f"attn_a_bwd_{g}", out_shape=[SDS((2, m, dil * 512), F32), SDS((m, dil * 512), F32)], grid=(4, dil),
        in_specs=[blk] * 7,
        out_specs=[pl.BlockSpec((2, m, HD), lambda h, r: (0, 0, r * 4 + h)), blk],
        scratch_shapes=[pltpu.VMEM((m + 128, HD), BF16), pltpu.VMEM((m + 128, HD), BF16),
                        pltpu.VMEM((m + 128, HD), F32), pltpu.VMEM((m + 128, HD), F32)],
        compiler_params=_params(2))(*_group_views(qkn, proj, g), view(doa), view(lse), view(w), view(cc))
    return dqk.reshape(2, T, 512), dv.reshape(T, 512)


def _attn_b_bwd(qkn, proj, dob, ob, lse, bias):
    def body(q_ref, k_ref, v_ref, d_ref, o_ref, l_ref, bias_ref, dqk_ref, dv_ref, drpb_ref, vb, dk_acc, dv_acc, a_acc):
        vb[...] = v_ref[...].astype(BF16)
        dk_acc[...] = jnp.zeros_like(dk_acc)
        dv_acc[...] = jnp.zeros_like(dv_acc)
        a_acc[...] = jnp.zeros_like(a_acc)

        def row(r, carry):
            start, off = _nbr_window(r)
            rows = pl.ds(pl.multiple_of(r * GRID_W, GRID_W), GRID_W)
            win = pl.ds(pl.multiple_of(start * GRID_W, GRID_W), 512)
            qr, kw, vw = q_ref[rows, :], k_ref[win, :], vb[win, :]
            s = _dot_nt(qr, kw) * SCALE + bias_ref[off]
            p = jnp.exp(s - _wide(l_ref[rows, :], 4))
            dov = d_ref[rows, :]
            delta = jnp.sum(dov * o_ref[rows, :], axis=-1, keepdims=True)
            do16 = dov.astype(BF16)
            ds = p * (_dot_nt(do16, vw) - delta)
            a_acc[off] += ds
            ds16 = ds.astype(BF16)
            dqk_ref[0, rows, :] = _dot(ds16, kw) * SCALE
            dk_acc[win, :] += _dot_tn(ds16, qr) * SCALE
            dv_acc[win, :] += _dot_tn(p.astype(BF16), do16)
            return carry

        lax.fori_loop(0, T // GRID_W, row, 0)
        dqk_ref[1] = dk_acc[...]
        dv_ref[...] = dv_acc[...]

        qc = lax.broadcasted_iota(jnp.int32, (GRID_W, 512), 0)
        lane = lax.broadcasted_iota(jnp.int32, (GRID_W, 512), 1)
        dc = jnp.clip((lane & (GRID_W - 1)) - qc, -(WIN_C - 1), WIN_C - 1) + (WIN_C - 1)
        jrow = lax.broadcasted_iota(jnp.int32, (1, 512), 1) >> 6
        dlane = lax.broadcasted_iota(jnp.int32, (1, HD), 1)
        drpb_ref[...] = jnp.zeros_like(drpb_ref)

        def per_dc(e, carry):
            sel = dc == e
            out = jnp.zeros((1, HD), F32)
            for off in range(8):
                col = jnp.sum(jnp.where(sel, a_acc[off], 0.0), axis=0, keepdims=True)
                for j in range(8):
                    part = jnp.sum(jnp.where(jrow == j, col, 0.0), axis=-1, keepdims=True)
                    out = out + jnp.where(dlane == off + j, part, 0.0)
            drpb_ref[pl.ds(e, 1), :] = out
            return carry

        lax.fori_loop(0, 31, per_dc, 0)

    blk = pl.BlockSpec((T, HD), lambda h: (0, h))
    return pl.pallas_call(
        body, name="attn_b_bwd",
        out_shape=[SDS((2, T, 512), F32), SDS((T, 512), F32), SDS((4, 32, HD), F32)], grid=(4,),
        in_specs=[pl.BlockSpec((T, HD), lambda h: (0, NHA + h)),
                  pl.BlockSpec((T, HD), lambda h: (0, NH + NHA + h)),
                  pl.BlockSpec((T, HD), lambda h: (0, 2 * NH + NHA + h)), blk, blk, blk,
                  pl.BlockSpec((None, 8, GRID_W, 512), lambda h: (h, 0, 0, 0))],
        out_specs=[pl.BlockSpec((2, T, HD), lambda h: (0, 0, h)), blk,
                   pl.BlockSpec((None, 32, HD), lambda h: (h, 0, 0))],
        scratch_shapes=[pltpu.VMEM((T, HD), BF16), pltpu.VMEM((T, HD), F32), pltpu.VMEM((T, HD), F32),
                        pltpu.VMEM((8, GRID_W, 512), F32)],
        compiler_params=_params(1))(qkn, qkn, proj, dob, ob, lse, bias)


def _qk_bwd(proj, nw, cos, sin, dqk_groups, dqk_b):
    tm = 256

    def body(p_ref, w_ref, cos_ref, sin_ref, d0, d1, d2, d3, o_ref, dn_ref):
        @pl.when(pl.program_id(1) == 0)
        def _():
            dn_ref[...] = jnp.zeros_like(dn_ref)

        cv, sv = cos_ref[...], sin_ref[...]
        srcs = (d0, d1, d2, d3)
        dna = jnp.zeros((1, HD), F32)
        dnb = jnp.zeros((1, HD), F32)
        for h in range(NH):
            sl = slice(h * HD, (h + 1) * HD)
            dz = srcs[h // 4][:, (h % 4) * HD:(h % 4 + 1) * HD]
            if h < NHA:
                dz = dz * cv + pltpu.roll(dz * sv, 64, 1)
            dx, dg = _norm_bwd(p_ref[:, sl], dz, w_ref[:, sl])
            o_ref[:, sl] = dx.astype(BF16)
            if h < NHA:
                dna += dg
            else:
                dnb += dg
        dn_ref[0:1, :] += dna
        dn_ref[1:2, :] += dnb

    dspec = pl.BlockSpec((None, tm, 512), lambda j, i: (j, i, 0))
    return pl.pallas_call(
        body, name="qk_bwd", out_shape=[SDS((T, 2 * D), BF16), SDS((2, 8, HD), F32)], grid=(2, T // tm),
        in_specs=[pl.BlockSpec((tm, D), lambda j, i: (i, j)),
                  pl.BlockSpec((None, 1, D), lambda j, i: (j, 0, 0)),
                  pl.BlockSpec((tm, HD), lambda j, i: (i, 0)),
                  pl.BlockSpec((tm, HD), lambda j, i: (i, 0)), dspec, dspec, dspec, dspec],
        out_specs=[pl.BlockSpec((tm, D), lambda j, i: (i, j)), pl.BlockSpec((None, 8, HD), lambda j, i: (j, 0, 0))],
        compiler_params=_params(2))(proj, nw, cos, sin, *dqk_groups, dqk_b)


def _in_proj_bwd(dproj, w_in, x, dh1, g, deps=()):
    tm, tk = 512, 1280
    per = (DIN // NSH) // tk
    nk = DIN // tk

    def body(dp_ref, w_ref, x_ref, dh_ref, g_ref, dx_ref, dg_ref, acc):
        i, k = pl.program_id(0), pl.program_id(1)

        @pl.when(k == 0)
        def _():
            acc[...] = jnp.zeros_like(acc)

        @pl.when((k == 0) & (i == 0))
        def _():
            dg_ref[...] = jnp.zeros_like(dg_ref)

        acc[...] += _dot_nt(dp_ref[...], w_ref[...])

        @pl.when(k == nk - 1)
        def _():
            dx, dg = _norm_bwd(x_ref[...], acc[...], g_ref[...])
            dx_ref[...] = dh_ref[...] + dx
            dg_ref[...] += dg

    row = pl.BlockSpec((tm, D), lambda i, k: (i, 0))
    vec = pl.BlockSpec((1, D), lambda i, k: (0, 0))
    return pl.pallas_call(
        _after(body, deps), name="in_proj_bwd", out_shape=[SDS((T, D), F32), SDS((1, D), F32)], grid=(T // tm, nk),
        in_specs=[DEP_SPEC] * len(deps) + [
            pl.BlockSpec((tm, tk), lambda i, k: (i, k)),
            pl.BlockSpec((None, D, tk), lambda i, k: (k // per, 0, k % per)), row, row, vec],
        out_specs=[row, vec], scratch_shapes=[pltpu.VMEM((tm, D), F32)],
        compiler_params=_params(2))(*deps, dproj, w_in, x, dh1, g)


def _grad_w(name, a, g, shard_rows, rows, cols, tr, tc, square=False):
    ni, nj = rows // tr, cols // tc
    if shard_rows:
        a_map, g_map = (lambda s, i, j: (0, s * ni + i)), (lambda s, i, j: (0, j))
    else:
        a_map, g_map = (lambda s, i, j: (0, i)), (lambda s, i, j: (0, s * nj + j))

    def body(a_ref, g_ref, o_ref):
        av = a_ref[...]
        if square:
            av = (av * av).astype(BF16)
        o_ref[...] = _dot_tn(av, g_ref[...]).astype(BF16)

    return pl.pallas_call(
        body, name=name, out_shape=SDS((NSH, rows, cols), BF16), grid=(NSH, ni, nj),
        in_specs=[pl.BlockSpec((T, tr), a_map), pl.BlockSpec((T, tc), g_map)],
        out_specs=pl.BlockSpec((None, tr, tc), lambda s, i, j: (s, i, j)), compiler_params=_params(3))(a, g)


def _adamw(w, g, m, v):
    m = B1 * m + (1.0 - B1) * g
    v = B2 * v + (1.0 - B2) * (g * g)
    m_hat = m / (1.0 - B1 ** STEP)
    v_hat = v / (1.0 - B2 ** STEP)
    delta = -LR * (m_hat / (jnp.sqrt(v_hat) + AEPS) + WD * w)
    return delta, m, v


def _sum_halves(name, place, grads, theirs):
    _, rows, cols = theirs.shape
    tr = _row_tile(rows, cols, 1 << 17)

    def body(place_ref, a_ref, b_ref, o_ref):
        o_ref[...] = (a_ref[...].astype(F32) + b_ref[...].astype(F32)).astype(BF16)

    spec = pl.BlockSpec((NSH, tr, cols), lambda i, p: (0, i, 0))
    return pl.pallas_call(
        body, name=name, out_shape=SDS(theirs.shape, BF16),
        grid_spec=pltpu.PrefetchScalarGridSpec(
            num_scalar_prefetch=1, grid=(rows // tr,),
            in_specs=[pl.BlockSpec((NSH, None, tr, cols), lambda i, p: (0, p[1], i, 0)), spec], out_specs=spec),
        compiler_params=_params(1))(place, grads, theirs)


def _sum_landed(name, place, part, landed):
    _, rows, cols = part.shape
    tr = _row_tile(rows, cols, 1 << 18)

    def body(place_ref, p_ref, l_ref, o_ref):
        o_ref[...] = ((p_ref[...].astype(F32) + l_ref[0].astype(F32)) + l_ref[1].astype(F32)) + l_ref[2].astype(F32)

    return pl.pallas_call(
        body, name=name, out_shape=SDS((2, rows, cols), F32),
        grid_spec=pltpu.PrefetchScalarGridSpec(
            num_scalar_prefetch=1, grid=(rows // tr,),
            in_specs=[pl.BlockSpec((None, tr, cols), lambda i, p: (p[0], i, 0)),
                      pl.BlockSpec((3, tr, cols), lambda i, p: (0, i, 0))],
            out_specs=pl.BlockSpec((None, tr, cols), lambda i, p: (p[1], i, 0))),
        compiler_params=_params(1))(place, part, landed)


def _adam_shard(name, g, w, m, v):
    rows, cols = w.shape
    tr = _row_tile(rows, cols, 1 << 18)

    def body(g_ref, w_ref, m_ref, v_ref, d_ref, nm_ref, nv_ref):
        d_ref[...], nm_ref[...], nv_ref[...] = _adamw(w_ref[...], g_ref[...], m_ref[...], v_ref[...])

    spec = pl.BlockSpec((tr, cols), lambda i: (i, 0))
    return pl.pallas_call(
        body, name=name, out_shape=[SDS((rows, cols), F32)] * 3, grid=(rows // tr,),
        in_specs=[spec] * 4, out_specs=[spec] * 3, compiler_params=_params(1))(g, w, m, v)


def _adam_small(gathered, w, m, v):
    def body(g_ref, w_ref, m_ref, v_ref, go_ref, d_ref, nm_ref, nv_ref):
        g = g_ref[0:SMALL_ROWS, :]
        for dev in range(1, 8):
            g = g + g_ref[dev * SMALL_ROWS:(dev + 1) * SMALL_ROWS, :]
        go_ref[...] = g
        d_ref[...], nm_ref[...], nv_ref[...] = _adamw(w_ref[...], g, m_ref[...], v_ref[...])

    return pl.pallas_call(body, name="adam_small", out_shape=[SDS((SMALL_ROWS, HD), F32)] * 4)(gathered, w, m, v)


SMALL = (("norm_mix", (1, D)), ("b_gate", (1, 2 * D)), ("q_norm_a", (1, HD)), ("k_norm_a", (1, HD)),
         ("q_norm_b", (1, HD)), ("k_norm_b", (1, HD)), ("rpb_b", (1, 4, 15, 31)), ("norm_ffn", (1, D)))


def _pack_small(vals):
    pieces = []
    for (name, shape), val in zip(SMALL, vals):
        flat = val.reshape(-1)
        pad = (-flat.shape[0]) % HD
        pieces.append(jnp.pad(flat, (0, pad)).reshape(-1, HD))
    packed = jnp.concatenate(pieces, axis=0)
    return jnp.pad(packed, ((0, SMALL_ROWS - packed.shape[0]), (0, 0)))


def _unpack_small(packed):
    out, row = [], 0
    for name, shape in SMALL:
        size = int(np.prod(shape))
        nrows = -(-size // HD)
        out.append(packed[row:row + nrows].reshape(-1)[:size].reshape(shape))
        row += nrows
    return out


def kernel(x, norm_mix, w_in, b_gate, q_norm_a, k_norm_a, q_norm_b, k_norm_b, rpb_b, w_proj_a, w_proj_b, w_out, norm_ffn, w_up, w_down, loss_target, m_norm_mix, m_w_in, m_b_gate, m_q_norm_a, m_k_norm_a, m_q_norm_b, m_k_norm_b, m_rpb_b, m_w_proj_a, m_w_proj_b, m_w_out, m_norm_ffn, m_w_up, m_w_down, v_norm_mix, v_w_in, v_b_gate, v_q_norm_a, v_k_norm_a, v_q_norm_b, v_k_norm_b, v_rpb_b, v_w_proj_a, v_w_proj_b, v_w_out, v_norm_ffn, v_w_up, v_w_down):
    big_names = ("w_in", "w_proj_a", "w_proj_b", "w_out", "w_up", "w_down")
    big_w = [a[0] for a in (w_in, w_proj_a, w_proj_b, w_out, w_up, w_down)]
    big_m = [a[0] for a in (m_w_in, m_w_proj_a, m_w_proj_b, m_w_out, m_w_up, m_w_down)]
    big_v = [a[0] for a in (v_w_in, v_w_proj_a, v_w_proj_b, v_w_out, v_w_up, v_w_down)]
    x2, target = x[0], loss_target[0]

    place = jnp.stack([2 * lax.axis_index("x") + lax.axis_index("y"), lax.axis_index("c")]).astype(jnp.int32)
    placed = [_cast_into_place(w, "cast_" + n, place) for n, w in zip(big_names, big_w)]
    groups = ((0,), (1, 2, 3), (4,), (5,))
    started = [_gather_start(f"gather_start_{j}", [placed[i] for i in grp]) for j, grp in enumerate(groups)]

    def gathered(j, after):
        send, recv, _, fulls, _ = started[j]
        fulls = _gather_wait(f"gather_wait_{j}", send, recv, fulls, after)
        fulls = _gather_finish(f"gather_finish_{j}", fulls)
        return [f.reshape(NSH, 2 * f.shape[2], f.shape[3]) for f in fulls]

    def reduce_begin(j, grads):
        grads = [g.reshape(NSH, 2, g.shape[1] // 2, g.shape[2]) for g in grads]
        theirs = _reduce_exchange(f"reduce_exchange_{j}", grads)
        parts = [_sum_halves(f"sum_halves_{j}_{i}", place, a, b) for i, (a, b) in enumerate(zip(grads, theirs))]
        send, recv, parts, lands, token = _reduce_start(f"reduce_start_{j}", parts)
        return (send, recv, parts, lands), token

    big_out = {}

    def reduce_end(j, state, after):
        send, recv, parts, lands = state
        parts, lands = _reduce_wait(f"reduce_wait_{j}", send, recv, parts, lands, after)
        sums = [_sum_landed(f"sum_landed_{j}_{i}", place, p, l) for i, (p, l) in enumerate(zip(parts, lands))]
        for idx, g in zip(groups[j], _reduce_share(f"reduce_share_{j}", sums)):
            g = g.reshape(big_w[idx].shape)
            big_out[idx] =(g, *_adam_shard("adam_" + big_names[idx], g, big_w[idx], big_m[idx], big_v[idx]))
        return big_out[groups[j][-1]][1]

    (win_f,) = gathered(0, tuple(s[4] for s in started[1:]))
    proj, xn = _norm_in_proj(x2, norm_mix, win_f)
    cos, sin = _rope_tables()
    nw = jnp.stack([jnp.concatenate([jnp.tile(q_norm_a, (1, NHA)), jnp.tile(q_norm_b, (1, NH - NHA))], axis=1),
                    jnp.concatenate([jnp.tile(k_norm_a, (1, NHA)), jnp.tile(k_norm_b, (1, NH - NHA))], axis=1)])
    qkn = _qk_prep(proj, nw, cos, sin)
    wpa_f, wpb_f, wout_f = gathered(1, (qkn,))
    wout_f = wout_f.reshape(D, D)
    fwd_a = [_attn_a_fwd(qkn, proj, g) for g in range(3)]
    os, ls = [f[0] for f in fwd_a], [f[1] for f in fwd_a]
    ob, lse_b, bias = _attn_b_fwd(qkn, proj, rpb_b.reshape(-1))
    oa, w0, w1, w2 = _comb_fwd(os, ls)
    ws = [w0, w1, w2]
    mixed, ob16 = _mix_fwd(oa, ob, proj, b_gate, wpa_f, wpb_f)
    h1, hn = _out_proj_fwd(mixed, wout_f, x2, norm_ffn)
    (wup_f,) = gathered(2, (h1,))
    u = _ffn_up(hn, wup_f)
    (wdown_f,) = gathered(3, (u,))
    wdown_f = wdown_f.reshape(DFF, D)
    dy, dy16, loss_parts = _ffn_down_loss(u, wdown_f, h1, target)
    loss = lax.psum(jnp.sum(loss_parts[:, 0, 0]), ("x", "y", "c"))

    g_down = _grad_w("grad_w_down", u, dy16, True, DFF // NSH, D, 512, 1024, square=True)
    red_down, token = reduce_begin(3, [g_down])
    du = _ffn_down_bwd(dy16, wdown_f, u, deps=(token,))
    g_up = _grad_w("grad_w_up", hn, du, False, D, DFF // NSH, 1024, 1024)
    red_up, token = reduce_begin(2, [g_up])
    dh1, dh16, d_norm_ffn = _ffn_up_bwd(du, wup_f, h1, dy, norm_ffn, deps=(token,))
    dya, dyb, dga, dgb, doa, dob, dba, dbb = _mix_bwd(dh16, wout_f, oa, ob16, proj, b_gate, wpa_f, wpb_f)
    g_out = _grad_w("grad_w_out", mixed, dh16, True, D // NSH, D, 512, 1024)
    g_pa = _grad_w("grad_w_proj_a", oa, dya, False, 512, 512, 512, 512)
    g_pb = _grad_w("grad_w_proj_b", ob16, dyb, False, 512, 512, 512, 512)
    red_mid, token = reduce_begin(1, [g_pa, g_pb, g_out])
    cc = _comb_bwd(doa, os, ws, deps=(token,))
    bwd_a = [_attn_a_bwd(qkn, proj, doa, ls[g], ws[g], cc, g) for g in range(3)]
    dqk_b, dv_b, drpb_t = _attn_b_bwd(qkn, proj, dob, ob, lse_b, bias)
    dqk_pre, dn = _qk_bwd(proj, nw, cos, sin, [b[0] for b in bwd_a], dqk_b)
    dv16 = jnp.concatenate([b[1] for b in bwd_a] + [dv_b], axis=1).astype(BF16)
    dproj = jnp.concatenate([dqk_pre, dv16, dga, dgb], axis=1)
    g_in = _grad_w("grad_w_in", xn, dproj, False, D, DIN // NSH, 1024, 1280)
    red_in, token = reduce_begin(0, [g_in])
    grad_x, d_norm_mix = _in_proj_bwd(dproj, win_f, x2, dh1, norm_mix, deps=(token,))

    done = reduce_end(3, red_down, (grad_x,))
    done = reduce_end(2, red_up, (done,))
    done = reduce_end(1, red_mid, (done,))
    done = reduce_end(0, red_in, (done,))

    d_rpb = jnp.transpose(drpb_t[:, :31, :15], (0, 2, 1))
    small_g = [d_norm_mix, jnp.concatenate([dba, dbb], axis=1), dn[0, 0], dn[1, 0], dn[0, 1], dn[1, 1], d_rpb, d_norm_ffn]
    gathered_small = _allgather_small(_pack_small(small_g), done)
    small_w = (norm_mix, b_gate, q_norm_a, k_norm_a, q_norm_b, k_norm_b, rpb_b, norm_ffn)
    small_m = (m_norm_mix, m_b_gate, m_q_norm_a, m_k_norm_a, m_q_norm_b, m_k_norm_b, m_rpb_b, m_norm_ffn)
    small_v = (v_norm_mix, v_b_gate, v_q_norm_a, v_k_norm_a, v_q_norm_b, v_k_norm_b, v_rpb_b, v_norm_ffn)
    small_out = [_unpack_small(p) for p in
                 _adam_small(gathered_small, _pack_small(small_w), _pack_small(small_m), _pack_small(small_v))]

    order = ("norm_mix", "w_in", "b_gate", "q_norm_a", "k_norm_a", "q_norm_b", "k_norm_b", "rpb_b",
             "w_proj_a", "w_proj_b", "w_out", "norm_ffn", "w_up", "w_down")
    small_idx = {name: i for i, (name, _) in enumerate(SMALL)}
    outs = []
    for kind in range(4):
        for name in order:
            if name in small_idx:
                outs.append(small_out[kind][small_idx[name]])
            else:
                outs.append(big_out[big_names.index(name)][kind][None])
    return (loss, grad_x[None], *outs)
```

```python
import functools

import numpy as np
import jax
import jax.numpy as jnp
from jax import lax
from jax.experimental import pallas as pl
from jax.experimental.pallas import tpu as pltpu

F32, BF16 = jnp.float32, jnp.bfloat16
SDS = jax.ShapeDtypeStruct
MESH = pl.DeviceIdType.MESH

T = 2048
D = 2048
HD = 128
NH, NHA = 16, 12
DIN = 10240
DFF = 8192
NSH = 4
DILS = (1, 4, 16)
EPS = 1e-6
NEG = -1e30
SCALE = HD ** -0.5
GRID_W, WIN_R, WIN_C = 64, 8, 16
VMEM_LIMIT = 56 * 1024 * 1024
B1, B2, LR, AEPS, WD, STEP = 0.9, 0.999, 0.001, 1e-08, 0.01, 10
SMALL_ROWS = 88


def _dot(a, b):
    return jnp.dot(a, b, preferred_element_type=F32)


def _dot_nt(a, b):
    return lax.dot_general(a, b, (((1,), (1,)), ((), ())), preferred_element_type=F32)


def _dot_tn(a, b):
    return lax.dot_general(a, b, (((0,), (0,)), ((), ())), preferred_element_type=F32)


def _params(n):
    return pltpu.CompilerParams(dimension_semantics=("arbitrary",) * n, vmem_limit_bytes=VMEM_LIMIT)


def _resident(shape, index_map):
    return pl.BlockSpec(shape, index_map, pipeline_mode=pl.Buffered(1))


def _sigmoid(z):
    return 1.0 / (1.0 + jnp.exp(-z))


def _wide(v, n):
    return jnp.concatenate([v] * n, axis=1)


def _row_tile(rows, cols, elems):
    tr = 16
    while tr * 2 <= rows and tr * 2 * cols <= elems:
        tr *= 2
    return tr


def _place():
    x, y, c = lax.axis_index("x"), lax.axis_index("y"), lax.axis_index("c")
    peers = [(1 - x, y), (x, 1 - y), (1 - x, 1 - y)]
    return x, y, c, peers


def _cast_into_place(w, name, place, deps=()):
    rows, cols = w.shape
    hr = rows // 2
    tr = min(hr, 256)
    per = hr // tr

    def body(*refs):
        w_ref, o_ref = refs[-2:]
        o_ref[...] = w_ref[...].astype(BF16)

    return pl.pallas_call(
        body, name=name, out_shape=SDS((NSH, 2, hr, cols), BF16),
        grid_spec=pltpu.PrefetchScalarGridSpec(
            num_scalar_prefetch=1, grid=(2, per),
            in_specs=[DEP_SPEC] * len(deps) + [pl.BlockSpec((tr, cols), lambda h, i, p: (h * per + i, 0))],
            out_specs=pl.BlockSpec((None, None, tr, cols), lambda h, i, p: (p[0], h, i, 0))),
        compiler_params=_params(2))(place, *deps, w)


ANY_SPEC = pl.BlockSpec(memory_space=pl.ANY)
HBM_SPEC = pl.BlockSpec(memory_space=pltpu.HBM)
SEM_SPEC = pl.BlockSpec(memory_space=pltpu.SEMAPHORE)
DEP_SPEC = pl.BlockSpec((8, 128), lambda *_: (0, 0))
EFFECT = pltpu.SideEffectType.DATAFLOW_SIDE_EFFECTING


def _after(body, deps):
    n = len(deps)
    return (lambda *refs: body(*refs[n:])) if n else body


def _split_start(name, srcs, lands, n_copies, issue):
    n, m = len(srcs), len(lands)

    def body(*refs):
        issue(refs[:n], refs[n:n + m], refs[n + m], refs[n + m + 1])
        refs[-1][...] = jnp.zeros((8, 128), F32)

    arrays = list(srcs) + list(lands)
    outs = pl.pallas_call(
        body, name=name,
        out_shape=(pltpu.SemaphoreType.DMA((n_copies,)), pltpu.SemaphoreType.DMA((n_copies,)),
                   *[pltpu.HBM(a.shape, a.dtype) for a in arrays], SDS((8, 128), F32)),
        in_specs=[HBM_SPEC] * (n + m),
        out_specs=(SEM_SPEC, SEM_SPEC, *[HBM_SPEC] * (n + m), pl.BlockSpec(memory_space=pltpu.VMEM)),
        input_output_aliases={i: 2 + i for i in range(n + m)},
        compiler_params=pltpu.CompilerParams(has_side_effects=EFFECT),
    )(*[pltpu.with_memory_space_constraint(a, pltpu.HBM) for a in arrays])
    return outs[0], outs[1], list(outs[2:2 + n]), list(outs[2 + n:2 + n + m]), outs[-1]


def _split_wait(name, send_sems, recv_sems, srcs, lands, after, wait):
    n, m = len(srcs), len(lands)

    def body(*refs):
        wait(refs[:n], refs[n:n + m], refs[n + m], refs[n + m + 1])

    arrays = list(srcs) + list(lands)
    outs = pl.pallas_call(
        body, name=name, out_shape=[pltpu.HBM(a.shape, a.dtype) for a in arrays],
        in_specs=[HBM_SPEC] * (n + m) + [SEM_SPEC, SEM_SPEC] + [ANY_SPEC] * len(after),
        out_specs=[HBM_SPEC] * (n + m), input_output_aliases={i: i for i in range(n + m)},
        compiler_params=pltpu.CompilerParams(has_side_effects=EFFECT),
    )(*arrays, send_sems, recv_sems, *after)
    return list(outs[:n]), list(outs[n:])


def _gather_start(name, fulls):
    def issue(srcs, dsts, send_sems, recv_sems):
        x, y, c, peers = _place()
        for i in range(len(fulls)):
            mine = dsts[i].at[2 * x + y, c]
            for k, (px, py) in enumerate(peers):
                pltpu.make_async_remote_copy(
                    src_ref=mine, dst_ref=mine, send_sem=send_sems.at[3 * i + k],
                    recv_sem=recv_sems.at[3 * i + k], device_id=(px, py, c), device_id_type=MESH).start()

    return _split_start(name, [], fulls, 3 * len(fulls), issue)


def _gather_wait(name, send_sems, recv_sems, fulls, after):
    def wait(srcs, dsts, send_sems, recv_sems):
        x, y, c, peers = _place()
        for i in range(len(fulls)):
            for k, (px, py) in enumerate(peers):
                cp = pltpu.make_async_remote_copy(
                    src_ref=dsts[i].at[2 * x + y, c], dst_ref=dsts[i].at[2 * px + py, c],
                    send_sem=send_sems.at[3 * i + k], recv_sem=recv_sems.at[3 * i + k],
                    device_id=(px, py, c), device_id_type=MESH)
                cp.wait_send()
                cp.wait_recv()

    return _split_wait(name, send_sems, recv_sems, [], fulls, after, wait)[1]


def _gather_finish(name, fulls):
    n = len(fulls)

    def body(*refs):
        fin, fout = refs[:n], refs[n:2 * n]
        send_sems, recv_sems = refs[2 * n:]
        x, y, c, peers = _place()

        def copy(i, k, half):
            px, py = peers[k]
            return pltpu.make_async_remote_copy(
                src_ref=fin[i].at[2 * px + py, half], dst_ref=fout[i].at[2 * px + py, half],
                send_sem=send_sems.at[3 * i + k], recv_sem=recv_sems.at[3 * i + k],
                device_id=(x, y, 1 - c), device_id_type=MESH)

        sends = [copy(i, k, c) for i in range(n) for k in range(3)]
        for cp in sends:
            cp.start()
        for i in range(n):
            for k in range(3):
                copy(i, k, 1 - c).wait_recv()
        for cp in sends:
            cp.wait_send()

    return pl.pallas_call(
        body, name=name, out_shape=[SDS(f.shape, f.dtype) for f in fulls],
        in_specs=[ANY_SPEC] * n, out_specs=[ANY_SPEC] * n, input_output_aliases={i: i for i in range(n)},
        scratch_shapes=[pltpu.SemaphoreType.DMA((3 * n,)), pltpu.SemaphoreType.DMA((3 * n,))])(*fulls)


def _reduce_exchange(name, grads):
    n = len(grads)

    def body(*refs):
        ins, theirs = refs[:n], refs[n:2 * n]
        send_sems, recv_sems = refs[2 * n:]
        x, y, c, _ = _place()
        copies = []
        for i in range(n):
            cp = pltpu.make_async_remote_copy(
                src_ref=ins[i].at[:, 1 - c], dst_ref=theirs[i], send_sem=send_sems.at[i],
                recv_sem=recv_sems.at[i], device_id=(x, y, 1 - c), device_id_type=MESH)
            cp.start()
            copies.append(cp)
        for cp in copies:
            cp.wait_recv()
            cp.wait_send()

    return pl.pallas_call(
        body, name=name, out_shape=[SDS((NSH,) + g.shape[2:], g.dtype) for g in grads],
        in_specs=[ANY_SPEC] * n, out_specs=[ANY_SPEC] * n,
        scratch_shapes=[pltpu.SemaphoreType.DMA((n,)), pltpu.SemaphoreType.DMA((n,))])(*grads)


def _reduce_start(name, parts):
    lands = [lax.empty((3,) + p.shape[1:], p.dtype) for p in parts]

    def issue(srcs, dsts, send_sems, recv_sems):
        x, y, c, peers = _place()
        for i in range(len(parts)):
            for k, (px, py) in enumerate(peers):
                pltpu.make_async_remote_copy(
                    src_ref=srcs[i].at[2 * px + py], dst_ref=dsts[i].at[k], send_sem=send_sems.at[3 * i + k],
                    recv_sem=recv_sems.at[3 * i + k], device_id=(px, py, c), device_id_type=MESH).start()

    return _split_start(name, parts, lands, 3 * len(parts), issue)


def _reduce_wait(name, send_sems, recv_sems, parts, lands, after):
    def wait(srcs, dsts, send_sems, recv_sems):
        x, y, c, peers = _place()
        for i in range(len(parts)):
            for k, (px, py) in enumerate(peers):
                cp = pltpu.make_async_remote_copy(
                    src_ref=srcs[i].at[2 * px + py], dst_ref=dsts[i].at[k], send_sem=send_sems.at[3 * i + k],
                    recv_sem=recv_sems.at[3 * i + k], device_id=(px, py, c), device_id_type=MESH)
                cp.wait_send()
                cp.wait_recv()

    return _split_wait(name, send_sems, recv_sems, parts, lands, after, wait)


def _reduce_share(name, sums):
    n = len(sums)

    def body(*refs):
        ins, outs = refs[:n], refs[n:2 * n]
        send_sems, recv_sems = refs[2 * n:]
        x, y, c, _ = _place()
        copies = []
        for i in range(n):
            cp = pltpu.make_async_remote_copy(
                src_ref=ins[i].at[c], dst_ref=outs[i].at[c], send_sem=send_sems.at[i], recv_sem=recv_sems.at[i],
                device_id=(x, y, 1 - c), device_id_type=MESH)
            cp.start()
            copies.append(cp)
        for i, cp in enumerate(copies):
            pltpu.make_async_remote_copy(
                src_ref=ins[i].at[c], dst_ref=outs[i].at[1 - c], send_sem=send_sems.at[i],
                recv_sem=recv_sems.at[i], device_id=(x, y, 1 - c), device_id_type=MESH).wait_recv()
            cp.wait_send()

    return pl.pallas_call(
        body, name=name, out_shape=[SDS(s.shape, s.dtype) for s in sums],
        in_specs=[ANY_SPEC] * n, out_specs=[ANY_SPEC] * n, input_output_aliases={i: i for i in range(n)},
        scratch_shapes=[pltpu.SemaphoreType.DMA((n,)), pltpu.SemaphoreType.DMA((n,))])(*sums)


def _allgather_small(v, after):
    m_per, n = v.shape

    def body(x_ref, after_ref, out_ref, send_sems, recv_sems, local_sem):
        x, y, c = lax.axis_index("x"), lax.axis_index("y"), lax.axis_index("c")
        me, sibling = (x, y, c), (x, y, 1 - c)
        chips = [(1 - x, y), (x, 1 - y), (1 - x, 1 - y)]

        def rows(px, py, pc):
            return out_ref.at[pl.ds((4 * px + 2 * py + pc) * m_per, m_per), :]

        def copy(k, block, to, src=None):
            return pltpu.make_async_remote_copy(
                src_ref=rows(*block) if src is None else src, dst_ref=rows(*block),
                send_sem=send_sems.at[k], recv_sem=recv_sems.at[k], device_id=to, device_id_type=MESH)

        mine = pltpu.make_async_copy(x_ref, rows(*me), local_sem)
        mine.start()
        first = [copy(0, me, sibling, src=x_ref)]
        first += [copy(1 + j, me, (*chip, c), src=x_ref) for j, chip in enumerate(chips)]
        for cp in first:
            cp.start()
        passed = [copy(4 + j, (*chip, c), sibling) for j, chip in enumerate(chips)]
        for j, chip in enumerate(chips):
            copy(1 + j, (*chip, c), me).wait_recv()
            passed[j].start()
        copy(0, sibling, me).wait_recv()
        for j, chip in enumerate(chips):
            copy(4 + j, (*chip, 1 - c), me).wait_recv()
        for cp in first + passed:
            cp.wait_send()
        mine.wait()

    return pl.pallas_call(
        body, name="allgather_small", out_shape=SDS((8 * m_per, n), v.dtype),
        in_specs=[pl.BlockSpec(memory_space=pltpu.VMEM), ANY_SPEC], out_specs=pl.BlockSpec(memory_space=pltpu.VMEM),
        scratch_shapes=[pltpu.SemaphoreType.DMA((7,)), pltpu.SemaphoreType.DMA((7,)), pltpu.SemaphoreType.DMA])(v, after)


def _norm_in_proj(x, g, w_full):
    tm, tn = 1024, 512
    per = (DIN // NSH) // tn

    def body(x_ref, g_ref, w_ref, proj_ref, xn_ref):
        @pl.when(pl.program_id(1) == 0)
        def _():
            xv = x_ref[...]
            r = lax.rsqrt(jnp.mean(xv * xv, axis=-1, keepdims=True) + EPS)
            xn_ref[...] = (xv * r * g_ref[...]).astype(BF16)

        proj_ref[...] = _dot(xn_ref[...], w_ref[...])

    return pl.pallas_call(
        body, name="norm_in_proj", out_shape=[SDS((T, DIN), F32), SDS((T, D), BF16)],
        grid=(T // tm, DIN // tn),
        in_specs=[pl.BlockSpec((tm, D), lambda i, j: (i, 0)),
                  pl.BlockSpec((1, D), lambda i, j: (0, 0)),
                  pl.BlockSpec((None, D, tn), lambda i, j: (j // per, 0, j % per))],
        out_specs=[pl.BlockSpec((tm, tn), lambda i, j: (i, j)),
                   pl.BlockSpec((tm, D), lambda i, j: (i, 0))],
        compiler_params=_params(2))(x, g, w_full)


def _rope_tables():
    pos = np.arange(T, dtype=np.float32)
    inv = (10000.0 ** (-np.arange(0, HD, 2, dtype=np.float32) / HD)).astype(np.float32)
    ang = (pos[:, None] * inv[None, :]).astype(np.float32)
    cos, sin = np.cos(ang).astype(np.float32), np.sin(ang).astype(np.float32)
    return (jnp.asarray(np.concatenate([cos, cos], axis=1)), jnp.asarray(np.concatenate([-sin, sin], axis=1)))


def _qk_prep(proj, nw, cos, sin):
    tm = 256

    def body(p_ref, w_ref, cos_ref, sin_ref, o_ref):
        cv, sv = cos_ref[...], sin_ref[...]
        for h in range(NH):
            sl = slice(h * HD, (h + 1) * HD)
            xv = p_ref[:, sl]
            r = lax.rsqrt(jnp.mean(xv * xv, axis=-1, keepdims=True) + EPS)
            z = xv * r * w_ref[:, sl]
            if h < NHA:
                z = z * cv + pltpu.roll(z, 64, 1) * sv
            o_ref[:, sl] = z.astype(BF16)

    return pl.pallas_call(
        body, name="qk_prep", out_shape=SDS((T, 2 * D), BF16), grid=(T // tm, 2),
        in_specs=[pl.BlockSpec((tm, D), lambda i, j: (i, j)),
                  pl.BlockSpec((None, 1, D), lambda i, j: (j, 0, 0)),
                  pl.BlockSpec((tm, HD), lambda i, j: (i, 0)),
                  pl.BlockSpec((tm, HD), lambda i, j: (i, 0))],
        out_specs=pl.BlockSpec((tm, D), lambda i, j: (i, j)),
        compiler_params=_params(2))(proj, nw, cos, sin)


def _band_mask(q0, m):
    ii = lax.broadcasted_iota(jnp.int32, (128, 256), 0)
    jj = lax.broadcasted_iota(jnp.int32, (128, 256), 1)
    rel = jj - ii
    kpos = jj + (q0 - 64)
    return (rel >= 0) & (rel <= 128) & (kpos >= 0) & (kpos < m)


def _fill_padded(dst, src, m):
    zeros = jnp.zeros((64, HD), dst.dtype)
    dst[0:64, :] = zeros
    dst[64 + m:128 + m, :] = zeros
    dst[64:64 + m, :] = src.astype(dst.dtype)


def _group_views(qkn, proj, g):
    m = T // DILS[g]
    cols = (qkn[:, g * 512:(g + 1) * 512], qkn[:, D + g * 512:D + (g + 1) * 512],
            proj[:, 2 * D + g * 512:2 * D + (g + 1) * 512])
    return [a.reshape(m, DILS[g] * 512) for a in cols]


def _heads_per_step(m):
    return 4 if m <= 512 else 1


def _attn_a_fwd(qkn, proj, g):
    dil = DILS[g]
    m = T // dil
    nb = m // 128
    hp = _heads_per_step(m)

    def body(q_ref, k_ref, v_ref, o_ref, l_ref, kp, vp):
        for hh in range(hp):
            sl = slice(hh * HD, (hh + 1) * HD)
            _fill_padded(kp, k_ref[:, sl], m)
            _fill_padded(vp, v_ref[:, sl], m)

            def block(b, carry):
                q0 = pl.multiple_of(b * 128, 128)
                kw, vw = kp[pl.ds(q0, 256), :], vp[pl.ds(q0, 256), :]
                s = _dot_nt(q_ref[pl.ds(q0, 128), sl], kw) * SCALE
                s = jnp.where(_band_mask(q0, m), s, NEG)
                mx = jnp.max(s, axis=-1, keepdims=True)
                e = jnp.exp(s - mx)
                den = jnp.sum(e, axis=-1, keepdims=True)
                o_ref[pl.ds(q0, 128), sl] = _dot((e / den).astype(BF16), vw)
                l_ref[pl.ds(q0, 128), sl] = jnp.broadcast_to(mx + jnp.log(den), (128, HD))
                return carry

            lax.fori_loop(0, nb, block, 0)

    blk = pl.BlockSpec((m, hp * HD), lambda h, r: (0, r * (4 // hp) + h))
    o, lse = pl.pallas_call(
        body, name=f"attn_a_fwd_{g}", out_shape=[SDS((m, dil * 512), F32)] * 2, grid=(4 // hp, dil),
        in_specs=[blk] * 3, out_specs=[blk] * 2,
        scratch_shapes=[pltpu.VMEM((m + 128, HD), BF16), pltpu.VMEM((m + 128, HD), BF16)],
        compiler_params=_params(2))(*_group_views(qkn, proj, g))
    return o.reshape(T, 512), lse.reshape(T, 512)


def _nbr_window(r):
    start = jnp.clip(r - WIN_R // 2, 0, T // GRID_W - WIN_R)
    return start, start - r + (WIN_R - 1)


def _rpb_rows(rpb):
    zeros = jnp.zeros((4, 14, 33), F32)
    a, b = rpb[:, :14], rpb[:, 1:15]
    rows = jnp.concatenate([a[:, :, 15:31], zeros, b, zeros, a[:, :, 0:15]], axis=2)
    return jnp.pad(rows, ((0, 0), (0, 2), (0, 0)))


def _attn_b_fwd(qkn, proj, rpb_rows):
    def body(r_ref, q_ref, k_ref, v_ref, o_ref, l_ref, bias_ref, vb, pair):
        qc = lax.broadcasted_iota(jnp.int32, (GRID_W, 512), 0)
        kc = lax.broadcasted_iota(jnp.int32, (GRID_W, 512), 1) & (GRID_W - 1)
        cs = jnp.clip(qc - WIN_C // 2, 0, GRID_W - WIN_C)
        colmask = (kc >= cs) & (kc < cs + WIN_C)
        for d in range(14):
            pair[d] = pltpu.roll(jnp.broadcast_to(r_ref[d:d + 1, :], (GRID_W, HD)), 0, 1, stride=1, stride_axis=0)
        for off in range(8):
            rows = jnp.concatenate([pair[off + 2 * jj] for jj in range(4)], axis=1)
            bias_ref[off] = jnp.where(colmask, rows, NEG)
        vb[...] = v_ref[...].astype(BF16)

        def row(r, carry):
            start, off = _nbr_window(r)
            q0 = pl.multiple_of(r * GRID_W, GRID_W)
            k0 = pl.multiple_of(start * GRID_W, GRID_W)
            s = _dot_nt(q_ref[pl.ds(q0, GRID_W), :], k_ref[pl.ds(k0, 512), :]) * SCALE + bias_ref[off]
            mx = jnp.max(s, axis=-1, keepdims=True)
            e = jnp.exp(s - mx)
            den = jnp.sum(e, axis=-1, keepdims=True)
            o_ref[pl.ds(q0, GRID_W), :] = _dot((e / den).astype(BF16), vb[pl.ds(k0, 512), :])
            l_ref[pl.ds(q0, GRID_W), :] = jnp.broadcast_to(mx + jnp.log(den), (GRID_W, HD))
            return carry

        lax.fori_loop(0, T // GRID_W, row, 0)

    return pl.pallas_call(
        body, name="attn_b_fwd",
        out_shape=[SDS((T, 512), F32), SDS((T, 512), F32), SDS((4, 8, GRID_W, 512), F32)], grid=(4,),
        in_specs=[pl.BlockSpec((None, 16, HD), lambda h: (h, 0, 0)),
                  pl.BlockSpec((T, HD), lambda h: (0, NHA + h)),
                  pl.BlockSpec((T, HD), lambda h: (0, NH + NHA + h)),
                  pl.BlockSpec((T, HD), lambda h: (0, 2 * NH + NHA + h))],
        out_specs=[pl.BlockSpec((T, HD), lambda h: (0, h)), pl.BlockSpec((T, HD), lambda h: (0, h)),
                   pl.BlockSpec((None, 8, GRID_W, 512), lambda h: (h, 0, 0, 0))],
        scratch_shapes=[pltpu.VMEM((T, HD), BF16), pltpu.VMEM((14, GRID_W, HD), F32)],
        compiler_params=_params(1))(rpb_rows, qkn, qkn, proj)


def _comb_fwd(os, ls):
    tm = 512

    def body(o0, o1, o2, l0, l1, l2, oa_ref, w0, w1, w2):
        lv = [l0[...], l1[...], l2[...]]
        mx = jnp.maximum(jnp.maximum(lv[0], lv[1]), lv[2])
        ev = [jnp.exp(l - mx) for l in lv]
        den = ev[0] + ev[1] + ev[2]
        wv = [e / den for e in ev]
        oa_ref[...] = (wv[0] * o0[...] + wv[1] * o1[...] + wv[2] * o2[...]).astype(BF16)
        w0[...], w1[...], w2[...] = wv

    spec = pl.BlockSpec((tm, 512), lambda i: (i, 0))
    return pl.pallas_call(
        body, name="comb_fwd", out_shape=[SDS((T, 512), BF16)] + [SDS((T, 512), F32)] * 3, grid=(T // tm,),
        in_specs=[spec] * 6, out_specs=[spec] * 4, compiler_params=_params(1))(*os, *ls)


def _mix_fwd(oa, ob, proj, b_gate, wpa, wpb):
    tm = 256

    def body(oa_ref, ob_ref, ga_ref, gb_ref, ba_ref, bb_ref, wpa_ref, wpb_ref, mixed_ref, ob16_ref):
        oav = oa_ref[...]
        obv = ob_ref[...].astype(BF16)
        ob16_ref[...] = obv
        for s in range(NSH):
            sl = slice(s * 512, (s + 1) * 512)
            ga = _sigmoid(ga_ref[:, sl] + ba_ref[:, sl])
            gb = _sigmoid(gb_ref[:, sl] + bb_ref[:, sl])
            mixed_ref[:, sl] = (ga * _dot(oav, wpa_ref[s]) + gb * _dot(obv, wpb_ref[s])).astype(BF16)

    row = lambda w: pl.BlockSpec((tm, w), lambda i: (i, 0))
    return pl.pallas_call(
        body, name="mix_fwd", out_shape=[SDS((T, D), BF16), SDS((T, 512), BF16)], grid=(T // tm,),
        in_specs=[row(512), row(512),
                  pl.BlockSpec((tm, D), lambda i: (i, 3)), pl.BlockSpec((tm, D), lambda i: (i, 4)),
                  pl.BlockSpec((1, D), lambda i: (0, 0)), pl.BlockSpec((1, D), lambda i: (0, 1)),
                  _resident((NSH, 512, 512), lambda i: (0, 0, 0)), _resident((NSH, 512, 512), lambda i: (0, 0, 0))],
        out_specs=[row(D), row(512)], compiler_params=_params(1))(oa, ob, proj, proj, b_gate, b_gate, wpa, wpb)


def _out_proj_fwd(mixed, w_out, x, g):
    tm = 256

    def body(m_ref, w_ref, x_ref, g_ref, h1_ref, hn_ref):
        h1 = x_ref[...] + _dot(m_ref[...], w_ref[...])
        h1_ref[...] = h1
        r = lax.rsqrt(jnp.mean(h1 * h1, axis=-1, keepdims=True) + EPS)
        hn_ref[...] = (h1 * r * g_ref[...]).astype(BF16)

    row = pl.BlockSpec((tm, D), lambda i: (i, 0))
    return pl.pallas_call(
        body, name="out_proj_fwd", out_shape=[SDS((T, D), F32), SDS((T, D), BF16)], grid=(T // tm,),
        in_specs=[row, _resident((D, D), lambda i: (0, 0)), row, pl.BlockSpec((1, D), lambda i: (0, 0))],
        out_specs=[row, row], compiler_params=_params(1))(mixed, w_out, x, g)


def _ffn_up(hn, w_up):
    tm, tn = 1024, 512
    per = (DFF // NSH) // tn

    def body(h_ref, w_ref, a_ref, u_ref):
        uv = jnp.maximum(_dot(h_ref[...], w_ref[...]), 0.0)
        a_ref[...] = (uv * uv).astype(BF16)
        u_ref[...] = uv.astype(BF16)

    out = pl.BlockSpec((tm, tn), lambda i, j: (i, j))
    return pl.pallas_call(
        body, name="ffn_up", out_shape=[SDS((T, DFF), BF16)] * 2, grid=(T // tm, DFF // tn),
        in_specs=[pl.BlockSpec((tm, D), lambda i, j: (i, 0)),
                  pl.BlockSpec((None, D, tn), lambda i, j: (j // per, 0, j % per))],
        out_specs=[out, out], compiler_params=_params(2))(hn, w_up)


def _ffn_down_loss(u, w_down, h1, target):
    tm, tk = 512, 512
    nk = DFF // tk

    def body(u_ref, w_ref, h1_ref, t_ref, dy_ref, dy16_ref, loss_ref, acc):
        k = pl.program_id(1)

        @pl.when(k == 0)
        def _():
            acc[...] = jnp.zeros_like(acc)

        acc[...] += _dot(u_ref[...], w_ref[...])

        @pl.when(k == nk - 1)
        def _():
            err = acc[...] + h1_ref[...] - t_ref[...]
            dy = err * (1.0 / D)
            dy_ref[...] = dy
            dy16_ref[...] = dy.astype(BF16)
            part = 0.5 * jnp.sum(jnp.mean(err * err, axis=-1, keepdims=True), axis=0, keepdims=True)
            loss_ref[...] = jnp.broadcast_to(part, (8, 128))

    row = pl.BlockSpec((tm, D), lambda i, k: (i, 0))
    return pl.pallas_call(
        body, name="ffn_down_loss",
        out_shape=[SDS((T, D), F32), SDS((T, D), BF16), SDS((T // tm, 8, 128), F32)], grid=(T // tm, nk),
        in_specs=[pl.BlockSpec((tm, tk), lambda i, k: (i, k)), pl.BlockSpec((tk, D), lambda i, k: (k, 0)), row, row],
        out_specs=[row, row, pl.BlockSpec((None, 8, 128), lambda i, k: (i, 0, 0))],
        scratch_shapes=[pltpu.VMEM((tm, D), F32)], compiler_params=_params(2))(u, w_down, h1, target)


def _ffn_down_bwd(dy16, w_down, u, deps=()):
    tm, tn = 1024, 512

    def body(dy_ref, w_ref, u_ref, du_ref):
        uv = u_ref[...].astype(F32)
        du_ref[...] = jnp.where(uv > 0.0, 2.0 * uv * _dot_nt(dy_ref[...], w_ref[...]), 0.0).astype(BF16)

    return pl.pallas_call(
        _after(body, deps), name="ffn_down_bwd", out_shape=SDS((T, DFF), BF16), grid=(T // tm, DFF // tn),
        in_specs=[DEP_SPEC] * len(deps) + [
            pl.BlockSpec((tm, D), lambda i, j: (i, 0)), pl.BlockSpec((tn, D), lambda i, j: (j, 0)),
            pl.BlockSpec((tm, tn), lambda i, j: (i, j))],
        out_specs=pl.BlockSpec((tm, tn), lambda i, j: (i, j)), compiler_params=_params(2))(*deps, dy16, w_down, u)


def _norm_bwd(xv, dz_in, g):
    r = lax.rsqrt(jnp.mean(xv * xv, axis=-1, keepdims=True) + EPS)
    dg = jnp.sum(xv * r * dz_in, axis=0, keepdims=True)
    dz = dz_in * g
    dx = r * dz - xv * (r * r * r) * jnp.mean(xv * dz, axis=-1, keepdims=True)
    return dx, dg


def _ffn_up_bwd(du, w_up, h1, dy, g, deps=()):
    tm, tk = 512, 1024
    per = (DFF // NSH) // tk
    nk = DFF // tk

    def body(du_ref, w_ref, h1_ref, dy_ref, g_ref, dh1_ref, dh16_ref, dg_ref, acc):
        i, k = pl.program_id(0), pl.program_id(1)

        @pl.when(k == 0)
        def _():
            acc[...] = jnp.zeros_like(acc)

        @pl.when((k == 0) & (i == 0))
        def _():
            dg_ref[...] = jnp.zeros_like(dg_ref)

        acc[...] += _dot_nt(du_ref[...], w_ref[...])

        @pl.when(k == nk - 1)
        def _():
            dx, dg = _norm_bwd(h1_ref[...], acc[...], g_ref[...])
            dh1 = dy_ref[...] + dx
            dh1_ref[...] = dh1
            dh16_ref[...] = dh1.astype(BF16)
            dg_ref[...] += dg

    row = pl.BlockSpec((tm, D), lambda i, k: (i, 0))
    vec = pl.BlockSpec((1, D), lambda i, k: (0, 0))
    return pl.pallas_call(
        _after(body, deps), name="ffn_up_bwd", out_shape=[SDS((T, D), F32), SDS((T, D), BF16), SDS((1, D), F32)],
        grid=(T // tm, nk),
        in_specs=[DEP_SPEC] * len(deps) + [
            pl.BlockSpec((tm, tk), lambda i, k: (i, k)),
            pl.BlockSpec((None, D, tk), lambda i, k: (k // per, 0, k % per)), row, row, vec],
        out_specs=[row, row, vec], scratch_shapes=[pltpu.VMEM((tm, D), F32)],
        compiler_params=_params(2))(*deps, du, w_up, h1, dy, g)


def _mix_bwd(dh16, w_out, oa, ob16, proj, b_gate, wpa, wpb):
    tm = 128

    def body(dh_ref, wo_ref, oa_ref, ob_ref, ga_ref, gb_ref, ba_ref, bb_ref, wpa_ref, wpb_ref,
             dya_ref, dyb_ref, dga_ref, dgb_ref, doa_ref, dob_ref, dba_ref, dbb_ref):
        @pl.when(pl.program_id(0) == 0)
        def _():
            dba_ref[...] = jnp.zeros_like(dba_ref)
            dbb_ref[...] = jnp.zeros_like(dbb_ref)

        oav, obv = oa_ref[...], ob_ref[...]
        doa = jnp.zeros((tm, 512), F32)
        dob = jnp.zeros((tm, 512), F32)
        for s in range(NSH):
            sl = slice(s * 512, (s + 1) * 512)
            dm = _dot_nt(dh_ref[...], wo_ref[sl, :])
            ga = _sigmoid(ga_ref[:, sl] + ba_ref[:, sl])
            gb = _sigmoid(gb_ref[:, sl] + bb_ref[:, sl])
            dya = (dm * ga).astype(BF16)
            dyb = (dm * gb).astype(BF16)
            dza = dm * _dot(oav, wpa_ref[s]) * ga * (1.0 - ga)
            dzb = dm * _dot(obv, wpb_ref[s]) * gb * (1.0 - gb)
            dya_ref[:, sl], dyb_ref[:, sl] = dya, dyb
            dga_ref[:, sl], dgb_ref[:, sl] = dza.astype(BF16), dzb.astype(BF16)
            dba_ref[:, sl] += jnp.sum(dza, axis=0, keepdims=True)
            dbb_ref[:, sl] += jnp.sum(dzb, axis=0, keepdims=True)
            doa += _dot_nt(dya, wpa_ref[s])
            dob += _dot_nt(dyb, wpb_ref[s])
        doa_ref[...], dob_ref[...] = doa, dob

    row = lambda w: pl.BlockSpec((tm, w), lambda i: (i, 0))
    vec = pl.BlockSpec((1, D), lambda i: (0, 0))
    wp = _resident((NSH, 512, 512), lambda i: (0, 0, 0))
    return pl.pallas_call(
        body, name="mix_bwd",
        out_shape=[SDS((T, D), BF16)] * 4 + [SDS((T, 512), F32)] * 2 + [SDS((1, D), F32)] * 2, grid=(T // tm,),
        in_specs=[row(D), _resident((D, D), lambda i: (0, 0)), row(512), row(512),
                  pl.BlockSpec((tm, D), lambda i: (i, 3)), pl.BlockSpec((tm, D), lambda i: (i, 4)),
                  pl.BlockSpec((1, D), lambda i: (0, 0)), pl.BlockSpec((1, D), lambda i: (0, 1)), wp, wp],
        out_specs=[row(D)] * 4 + [row(512)] * 2 + [vec] * 2,
        compiler_params=_params(1))(dh16, w_out, oa, ob16, proj, proj, b_gate, b_gate, wpa, wpb)


def _comb_bwd(doa, os, ws, deps=()):
    tm = 512

    def body(d_ref, o0, o1, o2, w0, w1, w2, cc_ref):
        prod = d_ref[...] * (w0[...] * o0[...] + w1[...] * o1[...] + w2[...] * o2[...])
        for h in range(4):
            sl = slice(h * HD, (h + 1) * HD)
            cc_ref[:, sl] = jnp.broadcast_to(jnp.sum(prod[:, sl], axis=-1, keepdims=True), (tm, HD))

    spec = pl.BlockSpec((tm, 512), lambda i: (i, 0))
    return pl.pallas_call(
        _after(body, deps), name="comb_bwd", out_shape=SDS((T, 512), F32), grid=(T // tm,),
        in_specs=[DEP_SPEC] * len(deps) + [spec] * 7, out_specs=spec,
        compiler_params=_params(1))(*deps, doa, *os, *ws)


def _attn_a_bwd(qkn, proj, doa, lse, w, cc, g):
    dil = DILS[g]
    m = T // dil
    nb = m // 128
    hp = _heads_per_step(m)

    def body(q_ref, k_ref, v_ref, d_ref, l_ref, w_ref, c_ref, dqk_ref, dv_ref, kp, vp, dkp, dvp):
        for hh in range(hp):
            sl = slice(hh * HD, (hh + 1) * HD)
            _fill_padded(kp, k_ref[:, sl], m)
            _fill_padded(vp, v_ref[:, sl], m)
            dkp[...] = jnp.zeros_like(dkp)
            dvp[...] = jnp.zeros_like(dvp)

            def block(b, carry):
                q0 = pl.multiple_of(b * 128, 128)
                rows = pl.ds(q0, 128)
                win = pl.ds(q0, 256)
                qb, kw, vw = q_ref[rows, sl], kp[win, :], vp[win, :]
                s = _dot_nt(qb, kw) * SCALE
                s = jnp.where(_band_mask(q0, m), s, NEG)
                wp = _wide(w_ref[rows, sl], 2) * jnp.exp(s - _wide(l_ref[rows, sl], 2))
                dob = d_ref[rows, sl].astype(BF16)
                ds = (wp * (_dot_nt(dob, vw) - _wide(c_ref[rows, sl], 2))).astype(BF16)
                dqk_ref[0, rows, sl] = _dot(ds, kw) * SCALE
                dkp[win, :] += _dot_tn(ds, qb) * SCALE
                dvp[win, :] += _dot_tn(wp.astype(BF16), dob)
                return carry

            lax.fori_loop(0, nb, block, 0)
            dqk_ref[1, :, sl] = dkp[64:64 + m, :]
            dv_ref[:, sl] = dvp[64:64 + m, :]

    blk = pl.BlockSpec((m, hp * HD), lambda h, r: (0, r * (4 // hp) + h))
    view = lambda a: a.reshape(m, dil * 512)
    dqk, dv = pl.pallas_call(
        body, name=f"attn_a_bwd_{g}", out_shape=[SDS((2, m, dil * 512), F32), SDS((m, dil * 512), F32)],
        grid=(4 // hp, dil), in_specs=[blk] * 7,
        out_specs=[pl.BlockSpec((2, m, hp * HD), lambda h, r: (0, 0, r * (4 // hp) + h)), blk],
        scratch_shapes=[pltpu.VMEM((m + 128, HD), BF16), pltpu.VMEM((m + 128, HD), BF16),
                        pltpu.VMEM((m + 128, HD), F32), pltpu.VMEM((m + 128, HD), F32)],
        compiler_params=_params(2))(*_group_views(qkn, proj, g), view(doa), view(lse), view(w), view(cc))
    return dqk.reshape(2, T, 512), dv.reshape(T, 512)


def _attn_b_bwd(qkn, proj, dob, ob, lse, bias):
    def body(q_ref, k_ref, v_ref, d_ref, o_ref, l_ref, bias_ref, dqk_ref, dv_ref, drpb_ref, vb, dk_acc, dv_acc, a_acc):
        vb[...] = v_ref[...].astype(BF16)
        dk_acc[...] = jnp.zeros_like(dk_acc)
        dv_acc[...] = jnp.zeros_like(dv_acc)
        a_acc[...] = jnp.zeros_like(a_acc)

        def row(r, carry):
            start, off = _nbr_window(r)
            rows = pl.ds(pl.multiple_of(r * GRID_W, GRID_W), GRID_W)
            win = pl.ds(pl.multiple_of(start * GRID_W, GRID_W), 512)
            qr, kw, vw = q_ref[rows, :], k_ref[win, :], vb[win, :]
            s = _dot_nt(qr, kw) * SCALE + bias_ref[off]
            p = jnp.exp(s - _wide(l_ref[rows, :], 4))
            dov = d_ref[rows, :]
            delta = jnp.sum(dov * o_ref[rows, :], axis=-1, keepdims=True)
            do16 = dov.astype(BF16)
            ds = p * (_dot_nt(do16, vw) - delta)
            a_acc[off] += ds
            ds16 = ds.astype(BF16)
            dqk_ref[0, rows, :] = _dot(ds16, kw) * SCALE
            dk_acc[win, :] += _dot_tn(ds16, qr) * SCALE
            dv_acc[win, :] += _dot_tn(p.astype(BF16), do16)
            return carry

        lax.fori_loop(0, T // GRID_W, row, 0)
        dqk_ref[1] = dk_acc[...]
        dv_ref[...] = dv_acc[...]

        lane = lax.broadcasted_iota(jnp.int32, (16, HD), 1)
        rowi = lax.broadcasted_iota(jnp.int32, (16, HD), 0)
        low = (lane >= GRID_W - WIN_C) & (lane < GRID_W + WIN_C - 1)
        high = (lane >= HD - WIN_C) | (lane < WIN_C - 1)
        flip = (lax.broadcasted_iota(jnp.int32, (GRID_W, GRID_W), 0)
                + lax.broadcasted_iota(jnp.int32, (GRID_W, GRID_W), 1) == GRID_W - 1).astype(BF16)
        out = jnp.zeros((16, HD), F32)
        for d in range(14):
            acc = None
            for off in range(8):
                if 0 <= d - off <= 6 and (d - off) % 2 == 0:
                    jj = (d - off) // 2
                    piece = a_acc[off, :, jj * HD:(jj + 1) * HD]
                    acc = piece if acc is None else acc + piece
            hi = acc.astype(BF16)
            lo = (acc - hi.astype(F32)).astype(BF16)
            rev = _dot(flip, hi) + _dot(flip, lo)
            v = jnp.sum(pltpu.roll(rev, 0, 1, stride=1, stride_axis=0), axis=0, keepdims=True)
            v = jnp.broadcast_to(v, (16, HD))
            out = out + jnp.where((rowi == d) & low, v, 0.0)
            out = out + jnp.where(rowi == d + 1, pltpu.roll(jnp.where(high, v, 0.0), GRID_W, 1), 0.0)
        drpb_ref[...] = out

    blk = pl.BlockSpec((T, HD), lambda h: (0, h))
    return pl.pallas_call(
        body, name="attn_b_bwd",
        out_shape=[SDS((2, T, 512), F32), SDS((T, 512), F32), SDS((4, 16, HD), F32)], grid=(4,),
        in_specs=[pl.BlockSpec((T, HD), lambda h: (0, NHA + h)),
                  pl.BlockSpec((T, HD), lambda h: (0, NH + NHA + h)),
                  pl.BlockSpec((T, HD), lambda h: (0, 2 * NH + NHA + h)), blk, blk, blk,
                  pl.BlockSpec((None, 8, GRID_W, 512), lambda h: (h, 0, 0, 0))],
        out_specs=[pl.BlockSpec((2, T, HD), lambda h: (0, 0, h)), blk,
                   pl.BlockSpec((None, 16, HD), lambda h: (h, 0, 0))],
        scratch_shapes=[pltpu.VMEM((T, HD), BF16), pltpu.VMEM((T, HD), F32), pltpu.VMEM((T, HD), F32),
                        pltpu.VMEM((8, GRID_W, 512), F32)],
        compiler_params=_params(1))(qkn, qkn, proj, dob, ob, lse, bias)


def _qk_bwd(proj, nw, cos, sin, dqk_groups, dqk_b):
    tm = 256

    def body(p_ref, w_ref, cos_ref, sin_ref, d0, d1, d2, d3, o_ref, dn_ref):
        @pl.when(pl.program_id(1) == 0)
        def _():
            dn_ref[...] = jnp.zeros_like(dn_ref)

        cv, sv = cos_ref[...], sin_ref[...]
        srcs = (d0, d1, d2, d3)
        dna = jnp.zeros((1, HD), F32)
        dnb = jnp.zeros((1, HD), F32)
        for h in range(NH):
            sl = slice(h * HD, (h + 1) * HD)
            dz = srcs[h // 4][:, (h % 4) * HD:(h % 4 + 1) * HD]
            if h < NHA:
                dz = dz * cv + pltpu.roll(dz * sv, 64, 1)
            dx, dg = _norm_bwd(p_ref[:, sl], dz, w_ref[:, sl])
            o_ref[:, sl] = dx.astype(BF16)
            if h < NHA:
                dna += dg
            else:
                dnb += dg
        dn_ref[0:1, :] += dna
        dn_ref[1:2, :] += dnb

    dspec = pl.BlockSpec((None, tm, 512), lambda j, i: (j, i, 0))
    return pl.pallas_call(
        body, name="qk_bwd", out_shape=[SDS((T, 2 * D), BF16), SDS((2, 8, HD), F32)], grid=(2, T // tm),
        in_specs=[pl.BlockSpec((tm, D), lambda j, i: (i, j)),
                  pl.BlockSpec((None, 1, D), lambda j, i: (j, 0, 0)),
                  pl.BlockSpec((tm, HD), lambda j, i: (i, 0)),
                  pl.BlockSpec((tm, HD), lambda j, i: (i, 0)), dspec, dspec, dspec, dspec],
        out_specs=[pl.BlockSpec((tm, D), lambda j, i: (i, j)), pl.BlockSpec((None, 8, HD), lambda j, i: (j, 0, 0))],
        compiler_params=_params(2))(proj, nw, cos, sin, *dqk_groups, dqk_b)


def _in_proj_bwd(dproj, w_in, x, dh1, g, deps=()):
    tm, tk = 512, 1280
    per = (DIN // NSH) // tk
    nk = DIN // tk

    def body(dp_ref, w_ref, x_ref, dh_ref, g_ref, dx_ref, dg_ref, acc):
        i, k = pl.program_id(0), pl.program_id(1)

        @pl.when(k == 0)
        def _():
            acc[...] = jnp.zeros_like(acc)

        @pl.when((k == 0) & (i == 0))
        def _():
            dg_ref[...] = jnp.zeros_like(dg_ref)

        acc[...] += _dot_nt(dp_ref[...], w_ref[...])

        @pl.when(k == nk - 1)
        def _():
            dx, dg = _norm_bwd(x_ref[...], acc[...], g_ref[...])
            dx_ref[...] = dh_ref[...] + dx
            dg_ref[...] += dg

    row = pl.BlockSpec((tm, D), lambda i, k: (i, 0))
    vec = pl.BlockSpec((1, D), lambda i, k: (0, 0))
    return pl.pallas_call(
        _after(body, deps), name="in_proj_bwd", out_shape=[SDS((T, D), F32), SDS((1, D), F32)], grid=(T // tm, nk),
        in_specs=[DEP_SPEC] * len(deps) + [
            pl.BlockSpec((tm, tk), lambda i, k: (i, k)),
            pl.BlockSpec((None, D, tk), lambda i, k: (k // per, 0, k % per)), row, row, vec],
        out_specs=[row, vec], scratch_shapes=[pltpu.VMEM((tm, D), F32)],
        compiler_params=_params(2))(*deps, dproj, w_in, x, dh1, g)


def _grad_w(name, a, g, shard_rows, rows, cols, tr, tc):
    ni, nj = rows // tr, cols // tc
    if shard_rows:
        a_map, g_map = (lambda s, i, j: (0, s * ni + i)), (lambda s, i, j: (0, j))
    else:
        a_map, g_map = (lambda s, i, j: (0, i)), (lambda s, i, j: (0, s * nj + j))

    def body(a_ref, g_ref, o_ref):
        o_ref[...] = _dot_tn(a_ref[...], g_ref[...]).astype(BF16)

    return pl.pallas_call(
        body, name=name, out_shape=SDS((NSH, rows, cols), BF16), grid=(NSH, ni, nj),
        in_specs=[pl.BlockSpec((T, tr), a_map), pl.BlockSpec((T, tc), g_map)],
        out_specs=pl.BlockSpec((None, tr, tc), lambda s, i, j: (s, i, j)), compiler_params=_params(3))(a, g)


def _adamw(w, g, m, v):
    m = B1 * m + (1.0 - B1) * g
    v = B2 * v + (1.0 - B2) * (g * g)
    m_hat = m / (1.0 - B1 ** STEP)
    v_hat = v / (1.0 - B2 ** STEP)
    delta = -LR * (m_hat / (jnp.sqrt(v_hat) + AEPS) + WD * w)
    return delta, m, v


def _sum_halves(name, place, grads, theirs):
    _, rows, cols = theirs.shape
    tr = _row_tile(rows, cols, 1 << 17)

    def body(place_ref, a_ref, b_ref, o_ref):
        o_ref[...] = (a_ref[...].astype(F32) + b_ref[...].astype(F32)).astype(BF16)

    spec = pl.BlockSpec((NSH, tr, cols), lambda i, p: (0, i, 0))
    return pl.pallas_call(
        body, name=name, out_shape=SDS(theirs.shape, BF16),
        grid_spec=pltpu.PrefetchScalarGridSpec(
            num_scalar_prefetch=1, grid=(rows // tr,),
            in_specs=[pl.BlockSpec((NSH, None, tr, cols), lambda i, p: (0, p[1], i, 0)), spec], out_specs=spec),
        compiler_params=_params(1))(place, grads, theirs)


def _sum_landed(name, place, part, landed):
    _, rows, cols = part.shape
    tr = _row_tile(rows, cols, 1 << 18)

    def body(place_ref, p_ref, l_ref, o_ref):
        o_ref[...] = ((p_ref[...].astype(F32) + l_ref[0].astype(F32)) + l_ref[1].astype(F32)) + l_ref[2].astype(F32)

    return pl.pallas_call(
        body, name=name, out_shape=SDS((2, rows, cols), F32),
        grid_spec=pltpu.PrefetchScalarGridSpec(
            num_scalar_prefetch=1, grid=(rows // tr,),
            in_specs=[pl.BlockSpec((None, tr, cols), lambda i, p: (p[0], i, 0)),
                      pl.BlockSpec((3, tr, cols), lambda i, p: (0, i, 0))],
            out_specs=pl.BlockSpec((None, tr, cols), lambda i, p: (p[1], i, 0))),
        compiler_params=_params(1))(place, part, landed)


def _adam_shard(name, g, w, m, v):
    rows, cols = w.shape
    tr = _row_tile(rows, cols, 1 << 18)

    def body(g_ref, w_ref, m_ref, v_ref, d_ref, nm_ref, nv_ref):
        d_ref[...], nm_ref[...], nv_ref[...] = _adamw(w_ref[...], g_ref[...], m_ref[...], v_ref[...])

    spec = pl.BlockSpec((tr, cols), lambda i: (i, 0))
    return pl.pallas_call(
        body, name=name, out_shape=[SDS((rows, cols), F32)] * 3, grid=(rows // tr,),
        in_specs=[spec] * 4, out_specs=[spec] * 3, compiler_params=_params(1))(g, w, m, v)


def _adam_small(gathered, w, m, v):
    def body(g_ref, w_ref, m_ref, v_ref, go_ref, d_ref, nm_ref, nv_ref):
        g = g_ref[0:SMALL_ROWS, :]
        for dev in range(1, 8):
            g = g + g_ref[dev * SMALL_ROWS:(dev + 1) * SMALL_ROWS, :]
        go_ref[...] = g
        d_ref[...], nm_ref[...], nv_ref[...] = _adamw(w_ref[...], g, m_ref[...], v_ref[...])

    return pl.pallas_call(body, name="adam_small", out_shape=[SDS((SMALL_ROWS, HD), F32)] * 4)(gathered, w, m, v)


SMALL = (("norm_mix", (1, D)), ("b_gate", (1, 2 * D)), ("q_norm_a", (1, HD)), ("k_norm_a", (1, HD)),
         ("q_norm_b", (1, HD)), ("k_norm_b", (1, HD)), ("rpb_b", (1, 4, 15, 31)), ("norm_ffn", (1, D)))


def _pack_small(vals):
    pieces = []
    for (name, shape), val in zip(SMALL, vals):
        flat = val.reshape(-1)
        pad = (-flat.shape[0]) % HD
        pieces.append(jnp.pad(flat, (0, pad)).reshape(-1, HD))
    packed = jnp.concatenate(pieces, axis=0)
    return jnp.pad(packed, ((0, SMALL_ROWS - packed.shape[0]), (0, 0)))


def _unpack_small(packed):
    out, row = [], 0
    for name, shape in SMALL:
        size = int(np.prod(shape))
        nrows = -(-size // HD)
        out.append(packed[row:row + nrows].reshape(-1)[:size].reshape(shape))
        row += nrows
    return out


def kernel(x, norm_mix, w_in, b_gate, q_norm_a, k_norm_a, q_norm_b, k_norm_b, rpb_b, w_proj_a, w_proj_b, w_out, norm_ffn, w_up, w_down, loss_target, m_norm_mix, m_w_in, m_b_gate, m_q_norm_a, m_k_norm_a, m_q_norm_b, m_k_norm_b, m_rpb_b, m_w_proj_a, m_w_proj_b, m_w_out, m_norm_ffn, m_w_up, m_w_down, v_norm_mix, v_w_in, v_b_gate, v_q_norm_a, v_k_norm_a, v_q_norm_b, v_k_norm_b, v_rpb_b, v_w_proj_a, v_w_proj_b, v_w_out, v_norm_ffn, v_w_up, v_w_down):
    big_names = ("w_in", "w_proj_a", "w_proj_b", "w_out", "w_up", "w_down")
    big_w = [a[0] for a in (w_in, w_proj_a, w_proj_b, w_out, w_up, w_down)]
    big_m = [a[0] for a in (m_w_in, m_w_proj_a, m_w_proj_b, m_w_out, m_w_up, m_w_down)]
    big_v = [a[0] for a in (v_w_in, v_w_proj_a, v_w_proj_b, v_w_out, v_w_up, v_w_down)]
    x2, target = x[0], loss_target[0]

    place = jnp.stack([2 * lax.axis_index("x") + lax.axis_index("y"), lax.axis_index("c")]).astype(jnp.int32)
    groups = ((0,), (1, 2, 3), (4,), (5,))
    started = []
    for j, grp in enumerate(groups):
        deps = (started[0][4],) if j else ()
        placed = [_cast_into_place(big_w[i], "cast_" + big_names[i], place, deps) for i in grp]
        started.append(_gather_start(f"gather_start_{j}", placed))

    def gathered(j, after):
        send, recv, _, fulls, _ = started[j]
        fulls = _gather_wait(f"gather_wait_{j}", send, recv, fulls, after)
        fulls = _gather_finish(f"gather_finish_{j}", fulls)
        return [f.reshape(NSH, 2 * f.shape[2], f.shape[3]) for f in fulls]

    def reduce_begin(j, grads):
        grads = [g.reshape(NSH, 2, g.shape[1] // 2, g.shape[2]) for g in grads]
        theirs = _reduce_exchange(f"reduce_exchange_{j}", grads)
        parts = [_sum_halves(f"sum_halves_{j}_{i}", place, a, b) for i, (a, b) in enumerate(zip(grads, theirs))]
        send, recv, parts, lands, token = _reduce_start(f"reduce_start_{j}", parts)
        return (send, recv, parts, lands), token

    big_out = {}

    def reduce_end(j, state, after):
        send, recv, parts, lands = state
        parts, lands = _reduce_wait(f"reduce_wait_{j}", send, recv, parts, lands, after)
        sums = [_sum_landed(f"sum_landed_{j}_{i}", place, p, l) for i, (p, l) in enumerate(zip(parts, lands))]
        for idx, g in zip(groups[j], _reduce_share(f"reduce_share_{j}", sums)):
            g = g.reshape(big_w[idx].shape)
            big_out[idx] =(g, *_adam_shard("adam_" + big_names[idx], g, big_w[idx], big_m[idx], big_v[idx]))
        return big_out[groups[j][-1]][1]

    (win_f,) = gathered(0, tuple(s[4] for s in started[1:]))
    proj, xn = _norm_in_proj(x2, norm_mix, win_f)
    cos, sin = _rope_tables()
    nw = jnp.stack([jnp.concatenate([jnp.tile(q_norm_a, (1, NHA)), jnp.tile(q_norm_b, (1, NH - NHA))], axis=1),
                    jnp.concatenate([jnp.tile(k_norm_a, (1, NHA)), jnp.tile(k_norm_b, (1, NH - NHA))], axis=1)])
    qkn = _qk_prep(proj, nw, cos, sin)
    wpa_f, wpb_f, wout_f = gathered(1, (qkn,))
    wout_f = wout_f.reshape(D, D)
    fwd_a = [_attn_a_fwd(qkn, proj, g) for g in range(3)]
    os, ls = [f[0] for f in fwd_a], [f[1] for f in fwd_a]
    ob, lse_b, bias = _attn_b_fwd(qkn, proj, _rpb_rows(rpb_b[0]))
    oa, w0, w1, w2 = _comb_fwd(os, ls)
    ws = [w0, w1, w2]
    mixed, ob16 = _mix_fwd(oa, ob, proj, b_gate, wpa_f, wpb_f)
    h1, hn = _out_proj_fwd(mixed, wout_f, x2, norm_ffn)
    (wup_f,) = gathered(2, (h1,))
    usq, u = _ffn_up(hn, wup_f)
    (wdown_f,) = gathered(3, (u,))
    wdown_f = wdown_f.reshape(DFF, D)
    dy, dy16, loss_parts = _ffn_down_loss(usq, wdown_f, h1, target)
    loss = lax.psum(jnp.sum(loss_parts[:, 0, 0]), ("x", "y", "c"))

    g_down = _grad_w("grad_w_down", usq, dy16, True, DFF // NSH, D, 1024, 1024)
    red_down, token = reduce_begin(3, [g_down])
    du = _ffn_down_bwd(dy16, wdown_f, u, deps=(token,))
    g_up = _grad_w("grad_w_up", hn, du, False, D, DFF // NSH, 1024, 1024)
    red_up, token = reduce_begin(2, [g_up])
    dh1, dh16, d_norm_ffn = _ffn_up_bwd(du, wup_f, h1, dy, norm_ffn, deps=(token,))
    dya, dyb, dga, dgb, doa, dob, dba, dbb = _mix_bwd(dh16, wout_f, oa, ob16, proj, b_gate, wpa_f, wpb_f)
    g_out = _grad_w("grad_w_out", mixed, dh16, True, D // NSH, D, 512, 1024)
    g_pa = _grad_w("grad_w_proj_a", oa, dya, False, 512, 512, 512, 512)
    g_pb = _grad_w("grad_w_proj_b", ob16, dyb, False, 512, 512, 512, 512)
    red_mid, token = reduce_begin(1, [g_pa, g_pb, g_out])
    cc = _comb_bwd(doa, os, ws, deps=(token,))
    bwd_a = [_attn_a_bwd(qkn, proj, doa, ls[g], ws[g], cc, g) for g in range(3)]
    dqk_b, dv_b, drpb_t = _attn_b_bwd(qkn, proj, dob, ob, lse_b, bias)
    dqk_pre, dn = _qk_bwd(proj, nw, cos, sin, [b[0] for b in bwd_a], dqk_b)
    dv16 = jnp.concatenate([b[1] for b in bwd_a] + [dv_b], axis=1).astype(BF16)
    dproj = jnp.concatenate([dqk_pre, dv16, dga, dgb], axis=1)
    g_in = _grad_w("grad_w_in", xn, dproj, False, D, DIN // NSH, 1024, 1280)
    red_in, token = reduce_begin(0, [g_in])
    grad_x, d_norm_mix = _in_proj_bwd(dproj, win_f, x2, dh1, norm_mix, deps=(token,))

    done = reduce_end(3, red_down, (grad_x,))
    done = reduce_end(2, red_up, (done,))
    done = reduce_end(1, red_mid, (done,))
    done = reduce_end(0, red_in, (done,))

    d_rpb = drpb_t[:, :15, GRID_W - WIN_C:GRID_W + WIN_C - 1]
    small_g = [d_norm_mix, jnp.concatenate([dba, dbb], axis=1), dn[0, 0], dn[1, 0], dn[0, 1], dn[1, 1], d_rpb, d_norm_ffn]
    gathered_small = _allgather_small(_pack_small(small_g), done)
    small_w = (norm_mix, b_gate, q_norm_a, k_norm_a, q_norm_b, k_norm_b, rpb_b, norm_ffn)
    small_m = (m_norm_mix, m_b_gate, m_q_norm_a, m_k_norm_a, m_q_norm_b, m_k_norm_b, m_rpb_b, m_norm_ffn)
    small_v = (v_norm_mix, v_b_gate, v_q_norm_a, v_k_norm_a, v_q_norm_b, v_k_norm_b, v_rpb_b, v_norm_ffn)
    small_out = [_unpack_small(p) for p in
                 _adam_small(gathered_small, _pack_small(small_w), _pack_small(small_m), _pack_small(small_v))]

    order = ("norm_mix", "w_in", "b_gate", "q_norm_a", "k_norm_a", "q_norm_b", "k_norm_b", "rpb_b",
             "w_proj_a", "w_proj_b", "w_out", "norm_ffn", "w_up", "w_down")
    small_idx = {name: i for i, (name, _) in enumerate(SMALL)}
    outs = []
    for kind in range(4):
        for name in order:
            if name in small_idx:
                outs.append(small_out[kind][small_idx[name]])
            else:
                outs.append(big_out[big_names.index(name)][kind][None])
    return (loss, grad_x[None], *outs)
```

```python
import functools

import numpy as np
import jax
import jax.numpy as jnp
from jax import lax
from jax.experimental import pallas as pl
from jax.experimental.pallas import tpu as pltpu

F32, BF16 = jnp.float32, jnp.bfloat16
SDS = jax.ShapeDtypeStruct
MESH = pl.DeviceIdType.MESH

T = 2048
D = 2048
HD = 128
NH, NHA = 16, 12
DIN = 10240
DFF = 8192
NSH = 4
DILS = (1, 4, 16)
EPS = 1e-6
NEG = -1e30
SCALE = HD ** -0.5
GRID_W, WIN_R, WIN_C = 64, 8, 16
VMEM_LIMIT = 56 * 1024 * 1024
B1, B2, LR, AEPS, WD, STEP = 0.9, 0.999, 0.001, 1e-08, 0.01, 10
SMALL_ROWS = 88


def _dot(a, b):
    return jnp.dot(a, b, preferred_element_type=F32)


def _dot_nt(a, b):
    return lax.dot_general(a, b, (((1,), (1,)), ((), ())), preferred_element_type=F32)


def _dot_tn(a, b):
    return lax.dot_general(a, b, (((0,), (0,)), ((), ())), preferred_element_type=F32)


def _params(n):
    return pltpu.CompilerParams(dimension_semantics=("arbitrary",) * n, vmem_limit_bytes=VMEM_LIMIT)


def _resident(shape, index_map):
    return pl.BlockSpec(shape, index_map, pipeline_mode=pl.Buffered(1))


def _sigmoid(z):
    return 1.0 / (1.0 + jnp.exp(-z))


def _wide(v, n):
    return jnp.concatenate([v] * n, axis=1)


def _row_tile(rows, cols, elems):
    tr = 16
    while tr * 2 <= rows and tr * 2 * cols <= elems:
        tr *= 2
    return tr


def _place():
    x, y, c = lax.axis_index("x"), lax.axis_index("y"), lax.axis_index("c")
    peers = [(1 - x, y), (x, 1 - y), (1 - x, 1 - y)]
    return x, y, c, peers


def _cast_into_place(w, name, place, deps=()):
    rows, cols = w.shape
    hr = rows // 2
    tr = min(hr, 256)
    per = hr // tr

    def body(*refs):
        w_ref, o_ref = refs[-2:]
        o_ref[...] = w_ref[...].astype(BF16)

    return pl.pallas_call(
        body, name=name, out_shape=SDS((NSH, 2, hr, cols), BF16),
        grid_spec=pltpu.PrefetchScalarGridSpec(
            num_scalar_prefetch=1, grid=(2, per),
            in_specs=[DEP_SPEC] * len(deps) + [pl.BlockSpec((tr, cols), lambda h, i, p: (h * per + i, 0))],
            out_specs=pl.BlockSpec((None, None, tr, cols), lambda h, i, p: (p[0], h, i, 0))),
        compiler_params=_params(2))(place, *deps, w)


ANY_SPEC = pl.BlockSpec(memory_space=pl.ANY)
HBM_SPEC = pl.BlockSpec(memory_space=pltpu.HBM)
SEM_SPEC = pl.BlockSpec(memory_space=pltpu.SEMAPHORE)
DEP_SPEC = pl.BlockSpec((8, 128), lambda *_: (0, 0))
EFFECT = pltpu.SideEffectType.DATAFLOW_SIDE_EFFECTING


def _after(body, deps):
    n = len(deps)
    return (lambda *refs: body(*refs[n:])) if n else body


def _split_start(name, srcs, lands, n_copies, issue):
    n, m = len(srcs), len(lands)

    def body(*refs):
        issue(refs[:n], refs[n:n + m], refs[n + m], refs[n + m + 1])
        refs[-1][...] = jnp.zeros((8, 128), F32)

    arrays = list(srcs) + list(lands)
    outs = pl.pallas_call(
        body, name=name,
        out_shape=(pltpu.SemaphoreType.DMA((n_copies,)), pltpu.SemaphoreType.DMA((n_copies,)),
                   *[pltpu.HBM(a.shape, a.dtype) for a in arrays], SDS((8, 128), F32)),
        in_specs=[HBM_SPEC] * (n + m),
        out_specs=(SEM_SPEC, SEM_SPEC, *[HBM_SPEC] * (n + m), pl.BlockSpec(memory_space=pltpu.VMEM)),
        input_output_aliases={i: 2 + i for i in range(n + m)},
        compiler_params=pltpu.CompilerParams(has_side_effects=EFFECT),
    )(*[pltpu.with_memory_space_constraint(a, pltpu.HBM) for a in arrays])
    return outs[0], outs[1], list(outs[2:2 + n]), list(outs[2 + n:2 + n + m]), outs[-1]


def _split_wait(name, send_sems, recv_sems, srcs, lands, after, wait):
    n, m = len(srcs), len(lands)

    def body(*refs):
        wait(refs[:n], refs[n:n + m], refs[n + m], refs[n + m + 1])

    arrays = list(srcs) + list(lands)
    outs = pl.pallas_call(
        body, name=name, out_shape=[pltpu.HBM(a.shape, a.dtype) for a in arrays],
        in_specs=[HBM_SPEC] * (n + m) + [SEM_SPEC, SEM_SPEC] + [ANY_SPEC] * len(after),
        out_specs=[HBM_SPEC] * (n + m), input_output_aliases={i: i for i in range(n + m)},
        compiler_params=pltpu.CompilerParams(has_side_effects=EFFECT),
    )(*arrays, send_sems, recv_sems, *after)
    return list(outs[:n]), list(outs[n:])


def _gather_start(name, fulls):
    def issue(srcs, dsts, send_sems, recv_sems):
        x, y, c, peers = _place()
        for i in range(len(fulls)):
            mine = dsts[i].at[2 * x + y, c]
            for k, (px, py) in enumerate(peers):
                pltpu.make_async_remote_copy(
                    src_ref=mine, dst_ref=mine, send_sem=send_sems.at[3 * i + k],
                    recv_sem=recv_sems.at[3 * i + k], device_id=(px, py, c), device_id_type=MESH).start()

    return _split_start(name, [], fulls, 3 * len(fulls), issue)


def _gather_wait(name, send_sems, recv_sems, fulls, after):
    def wait(srcs, dsts, send_sems, recv_sems):
        x, y, c, peers = _place()
        for i in range(len(fulls)):
            for k, (px, py) in enumerate(peers):
                cp = pltpu.make_async_remote_copy(
                    src_ref=dsts[i].at[2 * x + y, c], dst_ref=dsts[i].at[2 * px + py, c],
                    send_sem=send_sems.at[3 * i + k], recv_sem=recv_sems.at[3 * i + k],
                    device_id=(px, py, c), device_id_type=MESH)
                cp.wait_send()
                cp.wait_recv()

    return _split_wait(name, send_sems, recv_sems, [], fulls, after, wait)[1]


def _gather_finish(name, fulls):
    n = len(fulls)

    def body(*refs):
        fin, fout = refs[:n], refs[n:2 * n]
        send_sems, recv_sems = refs[2 * n:]
        x, y, c, peers = _place()

        def copy(i, k, half):
            px, py = peers[k]
            return pltpu.make_async_remote_copy(
                src_ref=fin[i].at[2 * px + py, half], dst_ref=fout[i].at[2 * px + py, half],
                send_sem=send_sems.at[3 * i + k], recv_sem=recv_sems.at[3 * i + k],
                device_id=(x, y, 1 - c), device_id_type=MESH)

        sends = [copy(i, k, c) for i in range(n) for k in range(3)]
        for cp in sends:
            cp.start()
        for i in range(n):
            for k in range(3):
                copy(i, k, 1 - c).wait_recv()
        for cp in sends:
            cp.wait_send()

    return pl.pallas_call(
        body, name=name, out_shape=[SDS(f.shape, f.dtype) for f in fulls],
        in_specs=[ANY_SPEC] * n, out_specs=[ANY_SPEC] * n, input_output_aliases={i: i for i in range(n)},
        scratch_shapes=[pltpu.SemaphoreType.DMA((3 * n,)), pltpu.SemaphoreType.DMA((3 * n,))])(*fulls)


def _reduce_exchange(name, grads):
    n = len(grads)

    def body(*refs):
        ins, theirs = refs[:n], refs[n:2 * n]
        send_sems, recv_sems = refs[2 * n:]
        x, y, c, _ = _place()
        copies = []
        for i in range(n):
            cp = pltpu.make_async_remote_copy(
                src_ref=ins[i].at[:, 1 - c], dst_ref=theirs[i], send_sem=send_sems.at[i],
                recv_sem=recv_sems.at[i], device_id=(x, y, 1 - c), device_id_type=MESH)
            cp.start()
            copies.append(cp)
        for cp in copies:
            cp.wait_recv()
            cp.wait_send()

    return pl.pallas_call(
        body, name=name, out_shape=[SDS((NSH,) + g.shape[2:], g.dtype) for g in grads],
        in_specs=[ANY_SPEC] * n, out_specs=[ANY_SPEC] * n,
        scratch_shapes=[pltpu.SemaphoreType.DMA((n,)), pltpu.SemaphoreType.DMA((n,))])(*grads)


def _reduce_start(name, parts):
    lands = [lax.empty((3,) + p.shape[1:], p.dtype) for p in parts]

    def issue(srcs, dsts, send_sems, recv_sems):
        x, y, c, peers = _place()
        for i in range(len(parts)):
            for k, (px, py) in enumerate(peers):
                pltpu.make_async_remote_copy(
                    src_ref=srcs[i].at[2 * px + py], dst_ref=dsts[i].at[k], send_sem=send_sems.at[3 * i + k],
                    recv_sem=recv_sems.at[3 * i + k], device_id=(px, py, c), device_id_type=MESH).start()

    return _split_start(name, parts, lands, 3 * len(parts), issue)


def _reduce_wait(name, send_sems, recv_sems, parts, lands, after):
    def wait(srcs, dsts, send_sems, recv_sems):
        x, y, c, peers = _place()
        for i in range(len(parts)):
            for k, (px, py) in enumerate(peers):
                cp = pltpu.make_async_remote_copy(
                    src_ref=srcs[i].at[2 * px + py], dst_ref=dsts[i].at[k], send_sem=send_sems.at[3 * i + k],
                    recv_sem=recv_sems.at[3 * i + k], device_id=(px, py, c), device_id_type=MESH)
                cp.wait_send()
                cp.wait_recv()

    return _split_wait(name, send_sems, recv_sems, parts, lands, after, wait)


def _reduce_share(name, sums):
    n = len(sums)

    def body(*refs):
        ins, outs = refs[:n], refs[n:2 * n]
        send_sems, recv_sems = refs[2 * n:]
        x, y, c, _ = _place()
        copies = []
        for i in range(n):
            cp = pltpu.make_async_remote_copy(
                src_ref=ins[i].at[c], dst_ref=outs[i].at[c], send_sem=send_sems.at[i], recv_sem=recv_sems.at[i],
                device_id=(x, y, 1 - c), device_id_type=MESH)
            cp.start()
            copies.append(cp)
        for i, cp in enumerate(copies):
            pltpu.make_async_remote_copy(
                src_ref=ins[i].at[c], dst_ref=outs[i].at[1 - c], send_sem=send_sems.at[i],
                recv_sem=recv_sems.at[i], device_id=(x, y, 1 - c), device_id_type=MESH).wait_recv()
            cp.wait_send()

    return pl.pallas_call(
        body, name=name, out_shape=[SDS(s.shape, s.dtype) for s in sums],
        in_specs=[ANY_SPEC] * n, out_specs=[ANY_SPEC] * n, input_output_aliases={i: i for i in range(n)},
        scratch_shapes=[pltpu.SemaphoreType.DMA((n,)), pltpu.SemaphoreType.DMA((n,))])(*sums)


def _allgather_small(v, after):
    m_per, n = v.shape

    def body(x_ref, after_ref, out_ref, send_sems, recv_sems, local_sem):
        x, y, c = lax.axis_index("x"), lax.axis_index("y"), lax.axis_index("c")
        me, sibling = (x, y, c), (x, y, 1 - c)
        chips = [(1 - x, y), (x, 1 - y), (1 - x, 1 - y)]

        def rows(px, py, pc):
            return out_ref.at[pl.ds((4 * px + 2 * py + pc) * m_per, m_per), :]

        def copy(k, block, to, src=None):
            return pltpu.make_async_remote_copy(
                src_ref=rows(*block) if src is None else src, dst_ref=rows(*block),
                send_sem=send_sems.at[k], recv_sem=recv_sems.at[k], device_id=to, device_id_type=MESH)

        mine = pltpu.make_async_copy(x_ref, rows(*me), local_sem)
        mine.start()
        first = [copy(0, me, sibling, src=x_ref)]
        first += [copy(1 + j, me, (*chip, c), src=x_ref) for j, chip in enumerate(chips)]
        for cp in first:
            cp.start()
        passed = [copy(4 + j, (*chip, c), sibling) for j, chip in enumerate(chips)]
        for j, chip in enumerate(chips):
            copy(1 + j, (*chip, c), me).wait_recv()
            passed[j].start()
        copy(0, sibling, me).wait_recv()
        for j, chip in enumerate(chips):
            copy(4 + j, (*chip, 1 - c), me).wait_recv()
        for cp in first + passed:
            cp.wait_send()
        mine.wait()

    return pl.pallas_call(
        body, name="allgather_small", out_shape=SDS((8 * m_per, n), v.dtype),
        in_specs=[pl.BlockSpec(memory_space=pltpu.VMEM), ANY_SPEC], out_specs=pl.BlockSpec(memory_space=pltpu.VMEM),
        scratch_shapes=[pltpu.SemaphoreType.DMA((7,)), pltpu.SemaphoreType.DMA((7,)), pltpu.SemaphoreType.DMA])(v, after)


def _norm_in_proj(x, g, w_full):
    tn, chunk = 512, 256
    per = (DIN // NSH) // tn

    def body(x_ref, g_ref, w_ref, proj_ref, xn_ref):
        @pl.when(pl.program_id(0) == 0)
        def _():
            def norm(r, carry):
                rows = pl.ds(pl.multiple_of(r * chunk, chunk), chunk)
                xv = x_ref[rows, :]
                rs = lax.rsqrt(jnp.mean(xv * xv, axis=-1, keepdims=True) + EPS)
                xn_ref[rows, :] = (xv * rs * g_ref[...]).astype(BF16)
                return carry

            lax.fori_loop(0, T // chunk, norm, 0)

        proj_ref[...] = _dot(xn_ref[...], w_ref[...])

    return pl.pallas_call(
        body, name="norm_in_proj", out_shape=[SDS((T, DIN), F32), SDS((T, D), BF16)], grid=(DIN // tn,),
        in_specs=[_resident((T, D), lambda j: (0, 0)),
                  pl.BlockSpec((1, D), lambda j: (0, 0)),
                  pl.BlockSpec((None, D, tn), lambda j: (j // per, 0, j % per))],
        out_specs=[pl.BlockSpec((T, tn), lambda j: (0, j)),
                   pl.BlockSpec((T, D), lambda j: (0, 0))],
        compiler_params=_params(1))(x, g, w_full)


def _rope_tables():
    pos = np.arange(T, dtype=np.float32)
    inv = (10000.0 ** (-np.arange(0, HD, 2, dtype=np.float32) / HD)).astype(np.float32)
    ang = (pos[:, None] * inv[None, :]).astype(np.float32)
    cos, sin = np.cos(ang).astype(np.float32), np.sin(ang).astype(np.float32)
    return (jnp.asarray(np.concatenate([cos, cos], axis=1)), jnp.asarray(np.concatenate([-sin, sin], axis=1)))


def _qk_prep(proj, nw, cos, sin):
    tm = 256

    def body(p_ref, w_ref, cos_ref, sin_ref, o_ref):
        cv, sv = cos_ref[...], sin_ref[...]
        for h in range(NH):
            sl = slice(h * HD, (h + 1) * HD)
            xv = p_ref[:, sl]
            r = lax.rsqrt(jnp.mean(xv * xv, axis=-1, keepdims=True) + EPS)
            z = xv * r * w_ref[:, sl]
            if h < NHA:
                z = z * cv + pltpu.roll(z, 64, 1) * sv
            o_ref[:, sl] = z.astype(BF16)

    return pl.pallas_call(
        body, name="qk_prep", out_shape=SDS((T, 2 * D), BF16), grid=(T // tm, 2),
        in_specs=[pl.BlockSpec((tm, D), lambda i, j: (i, j)),
                  pl.BlockSpec((None, 1, D), lambda i, j: (j, 0, 0)),
                  pl.BlockSpec((tm, HD), lambda i, j: (i, 0)),
                  pl.BlockSpec((tm, HD), lambda i, j: (i, 0))],
        out_specs=pl.BlockSpec((tm, D), lambda i, j: (i, j)),
        compiler_params=_params(2))(proj, nw, cos, sin)


def _band_mask(q0, m):
    ii = lax.broadcasted_iota(jnp.int32, (128, 256), 0)
    jj = lax.broadcasted_iota(jnp.int32, (128, 256), 1)
    rel = jj - ii
    kpos = jj + (q0 - 64)
    return (rel >= 0) & (rel <= 128) & (kpos >= 0) & (kpos < m)


def _fill_padded(dst, src, m):
    zeros = jnp.zeros((64, HD), dst.dtype)
    dst[0:64, :] = zeros
    dst[64 + m:128 + m, :] = zeros
    dst[64:64 + m, :] = src.astype(dst.dtype)


def _group_views(qkn, proj, g):
    m = T // DILS[g]
    cols = (qkn[:, g * 512:(g + 1) * 512], qkn[:, D + g * 512:D + (g + 1) * 512],
            proj[:, 2 * D + g * 512:2 * D + (g + 1) * 512])
    return [a.reshape(m, DILS[g] * 512) for a in cols]


def _heads_per_step(m):
    return 4 if m <= 512 else 1


def _attn_a_fwd(qkn, proj, g):
    dil = DILS[g]
    m = T // dil
    nb = m // 128
    hp = _heads_per_step(m)

    def body(q_ref, k_ref, v_ref, o_ref, l_ref, kp, vp):
        for hh in range(hp):
            sl = slice(hh * HD, (hh + 1) * HD)
            _fill_padded(kp, k_ref[:, sl], m)
            _fill_padded(vp, v_ref[:, sl], m)

            def block(b, carry):
                q0 = pl.multiple_of(b * 128, 128)
                kw, vw = kp[pl.ds(q0, 256), :], vp[pl.ds(q0, 256), :]
                s = _dot_nt(q_ref[pl.ds(q0, 128), sl], kw) * SCALE
                s = jnp.where(_band_mask(q0, m), s, NEG)
                mx = jnp.max(s, axis=-1, keepdims=True)
                e = jnp.exp(s - mx)
                den = jnp.sum(e, axis=-1, keepdims=True)
                o_ref[pl.ds(q0, 128), sl] = _dot((e / den).astype(BF16), vw)
                l_ref[pl.ds(q0, 128), sl] = jnp.broadcast_to(mx + jnp.log(den), (128, HD))
                return carry

            lax.fori_loop(0, nb, block, 0)

    blk = pl.BlockSpec((m, hp * HD), lambda h, r: (0, r * (4 // hp) + h))
    o, lse = pl.pallas_call(
        body, name=f"attn_a_fwd_{g}", out_shape=[SDS((m, dil * 512), F32)] * 2, grid=(4 // hp, dil),
        in_specs=[blk] * 3, out_specs=[blk] * 2,
        scratch_shapes=[pltpu.VMEM((m + 128, HD), BF16), pltpu.VMEM((m + 128, HD), BF16)],
        compiler_params=_params(2))(*_group_views(qkn, proj, g))
    return o.reshape(T, 512), lse.reshape(T, 512)


def _nbr_window(r):
    start = jnp.clip(r - WIN_R // 2, 0, T // GRID_W - WIN_R)
    return start, start - r + (WIN_R - 1)


def _rpb_rows(rpb):
    zeros = jnp.zeros((4, 14, 33), F32)
    a, b = rpb[:, :14], rpb[:, 1:15]
    rows = jnp.concatenate([a[:, :, 15:31], zeros, b, zeros, a[:, :, 0:15]], axis=2)
    return jnp.pad(rows, ((0, 0), (0, 2), (0, 0)))


def _attn_b_fwd(qkn, proj, rpb_rows):
    def body(r_ref, q_ref, k_ref, v_ref, o_ref, l_ref, bias_ref, vb, pair):
        qc = lax.broadcasted_iota(jnp.int32, (GRID_W, 512), 0)
        kc = lax.broadcasted_iota(jnp.int32, (GRID_W, 512), 1) & (GRID_W - 1)
        cs = jnp.clip(qc - WIN_C // 2, 0, GRID_W - WIN_C)
        colmask = (kc >= cs) & (kc < cs + WIN_C)
        for d in range(14):
            pair[d] = pltpu.roll(jnp.broadcast_to(r_ref[d:d + 1, :], (GRID_W, HD)), 0, 1, stride=1, stride_axis=0)
        for off in range(8):
            rows = jnp.concatenate([pair[off + 2 * jj] for jj in range(4)], axis=1)
            bias_ref[off] = jnp.where(colmask, rows, NEG)
        vb[...] = v_ref[...].astype(BF16)

        def row(r, carry):
            start, off = _nbr_window(r)
            q0 = pl.multiple_of(r * GRID_W, GRID_W)
            k0 = pl.multiple_of(start * GRID_W, GRID_W)
            s = _dot_nt(q_ref[pl.ds(q0, GRID_W), :], k_ref[pl.ds(k0, 512), :]) * SCALE + bias_ref[off]
            mx = jnp.max(s, axis=-1, keepdims=True)
            e = jnp.exp(s - mx)
            den = jnp.sum(e, axis=-1, keepdims=True)
            o_ref[pl.ds(q0, GRID_W), :] = _dot((e / den).astype(BF16), vb[pl.ds(k0, 512), :])
            l_ref[pl.ds(q0, GRID_W), :] = jnp.broadcast_to(mx + jnp.log(den), (GRID_W, HD))
            return carry

        lax.fori_loop(0, T // GRID_W, row, 0)

    return pl.pallas_call(
        body, name="attn_b_fwd",
        out_shape=[SDS((T, 512), F32), SDS((T, 512), F32), SDS((4, 8, GRID_W, 512), F32)], grid=(4,),
        in_specs=[pl.BlockSpec((None, 16, HD), lambda h: (h, 0, 0)),
                  pl.BlockSpec((T, HD), lambda h: (0, NHA + h)),
                  pl.BlockSpec((T, HD), lambda h: (0, NH + NHA + h)),
                  pl.BlockSpec((T, HD), lambda h: (0, 2 * NH + NHA + h))],
        out_specs=[pl.BlockSpec((T, HD), lambda h: (0, h)), pl.BlockSpec((T, HD), lambda h: (0, h)),
                   pl.BlockSpec((None, 8, GRID_W, 512), lambda h: (h, 0, 0, 0))],
        scratch_shapes=[pltpu.VMEM((T, HD), BF16), pltpu.VMEM((14, GRID_W, HD), F32)],
        compiler_params=_params(1))(rpb_rows, qkn, qkn, proj)


def _comb_fwd(os, ls):
    tm = 512

    def body(o0, o1, o2, l0, l1, l2, oa_ref, w0, w1, w2):
        lv = [l0[...], l1[...], l2[...]]
        mx = jnp.maximum(jnp.maximum(lv[0], lv[1]), lv[2])
        ev = [jnp.exp(l - mx) for l in lv]
        den = ev[0] + ev[1] + ev[2]
        wv = [e / den for e in ev]
        oa_ref[...] = (wv[0] * o0[...] + wv[1] * o1[...] + wv[2] * o2[...]).astype(BF16)
        w0[...], w1[...], w2[...] = wv

    spec = pl.BlockSpec((tm, 512), lambda i: (i, 0))
    return pl.pallas_call(
        body, name="comb_fwd", out_shape=[SDS((T, 512), BF16)] + [SDS((T, 512), F32)] * 3, grid=(T // tm,),
        in_specs=[spec] * 6, out_specs=[spec] * 4, compiler_params=_params(1))(*os, *ls)


def _mix_fwd(oa, ob, proj, b_gate, wpa, wpb):
    tm = 512

    def body(oa_ref, ob_ref, ga_ref, gb_ref, ba_ref, bb_ref, wpa_ref, wpb_ref, mixed_ref, ob16_ref):
        oav = oa_ref[...]
        obv = ob_ref[...].astype(BF16)
        ob16_ref[...] = obv
        for s in range(NSH):
            sl = slice(s * 512, (s + 1) * 512)
            ga = _sigmoid(ga_ref[:, sl] + ba_ref[:, sl])
            gb = _sigmoid(gb_ref[:, sl] + bb_ref[:, sl])
            mixed_ref[:, sl] = (ga * _dot(oav, wpa_ref[s]) + gb * _dot(obv, wpb_ref[s])).astype(BF16)

    row = lambda w: pl.BlockSpec((tm, w), lambda i: (i, 0))
    return pl.pallas_call(
        body, name="mix_fwd", out_shape=[SDS((T, D), BF16), SDS((T, 512), BF16)], grid=(T // tm,),
        in_specs=[row(512), row(512),
                  pl.BlockSpec((tm, D), lambda i: (i, 3)), pl.BlockSpec((tm, D), lambda i: (i, 4)),
                  pl.BlockSpec((1, D), lambda i: (0, 0)), pl.BlockSpec((1, D), lambda i: (0, 1)),
                  _resident((NSH, 512, 512), lambda i: (0, 0, 0)), _resident((NSH, 512, 512), lambda i: (0, 0, 0))],
        out_specs=[row(D), row(512)], compiler_params=_params(1))(oa, ob, proj, proj, b_gate, b_gate, wpa, wpb)


def _out_proj_fwd(mixed, w_out, x, g):
    tm = 512

    def body(m_ref, w_ref, x_ref, g_ref, h1_ref, hn_ref):
        h1 = x_ref[...] + _dot(m_ref[...], w_ref[...])
        h1_ref[...] = h1
        r = lax.rsqrt(jnp.mean(h1 * h1, axis=-1, keepdims=True) + EPS)
        hn_ref[...] = (h1 * r * g_ref[...]).astype(BF16)

    row = pl.BlockSpec((tm, D), lambda i: (i, 0))
    return pl.pallas_call(
        body, name="out_proj_fwd", out_shape=[SDS((T, D), F32), SDS((T, D), BF16)], grid=(T // tm,),
        in_specs=[row, _resident((D, D), lambda i: (0, 0)), row, pl.BlockSpec((1, D), lambda i: (0, 0))],
        out_specs=[row, row], compiler_params=_params(1))(mixed, w_out, x, g)


def _ffn_up(hn, w_up):
    tm, tn = T, 512
    per = (DFF // NSH) // tn

    def body(h_ref, w_ref, a_ref, u_ref):
        uv = jnp.maximum(_dot(h_ref[...], w_ref[...]), 0.0)
        a_ref[...] = (uv * uv).astype(BF16)
        u_ref[...] = uv.astype(BF16)

    out = pl.BlockSpec((tm, tn), lambda i, j: (i, j))
    return pl.pallas_call(
        body, name="ffn_up", out_shape=[SDS((T, DFF), BF16)] * 2, grid=(T // tm, DFF // tn),
        in_specs=[pl.BlockSpec((tm, D), lambda i, j: (i, 0)),
                  pl.BlockSpec((None, D, tn), lambda i, j: (j // per, 0, j % per))],
        out_specs=[out, out], compiler_params=_params(2))(hn, w_up)


def _ffn_down_loss(u, w_down, h1, target):
    tm, tk = 512, 512
    nk = DFF // tk

    def body(u_ref, w_ref, h1_ref, t_ref, dy_ref, dy16_ref, loss_ref, acc):
        k = pl.program_id(1)

        @pl.when(k == 0)
        def _():
            acc[...] = jnp.zeros_like(acc)

        acc[...] += _dot(u_ref[...], w_ref[...])

        @pl.when(k == nk - 1)
        def _():
            err = acc[...] + h1_ref[...] - t_ref[...]
            dy = err * (1.0 / D)
            dy_ref[...] = dy
            dy16_ref[...] = dy.astype(BF16)
            part = 0.5 * jnp.sum(jnp.mean(err * err, axis=-1, keepdims=True), axis=0, keepdims=True)
            loss_ref[...] = jnp.broadcast_to(part, (8, 128))

    row = pl.BlockSpec((tm, D), lambda i, k: (i, 0))
    return pl.pallas_call(
        body, name="ffn_down_loss",
        out_shape=[SDS((T, D), F32), SDS((T, D), BF16), SDS((T // tm, 8, 128), F32)], grid=(T // tm, nk),
        in_specs=[pl.BlockSpec((tm, tk), lambda i, k: (i, k)), pl.BlockSpec((tk, D), lambda i, k: (k, 0)), row, row],
        out_specs=[row, row, pl.BlockSpec((None, 8, 128), lambda i, k: (i, 0, 0))],
        scratch_shapes=[pltpu.VMEM((tm, D), F32)], compiler_params=_params(2))(u, w_down, h1, target)


def _ffn_down_bwd(dy16, w_down, u, deps=()):
    tm, tn = T, 512

    def body(dy_ref, w_ref, u_ref, du_ref):
        uv = u_ref[...].astype(F32)
        du_ref[...] = jnp.where(uv > 0.0, 2.0 * uv * _dot_nt(dy_ref[...], w_ref[...]), 0.0).astype(BF16)

    return pl.pallas_call(
        _after(body, deps), name="ffn_down_bwd", out_shape=SDS((T, DFF), BF16), grid=(T // tm, DFF // tn),
        in_specs=[DEP_SPEC] * len(deps) + [
            pl.BlockSpec((tm, D), lambda i, j: (i, 0)), pl.BlockSpec((tn, D), lambda i, j: (j, 0)),
            pl.BlockSpec((tm, tn), lambda i, j: (i, j))],
        out_specs=pl.BlockSpec((tm, tn), lambda i, j: (i, j)), compiler_params=_params(2))(*deps, dy16, w_down, u)


def _norm_bwd(xv, dz_in, g):
    r = lax.rsqrt(jnp.mean(xv * xv, axis=-1, keepdims=True) + EPS)
    dg = jnp.sum(xv * r * dz_in, axis=0, keepdims=True)
    dz = dz_in * g
    dx = r * dz - xv * (r * r * r) * jnp.mean(xv * dz, axis=-1, keepdims=True)
    return dx, dg


def _ffn_up_bwd(du, w_up, h1, dy, g, deps=()):
    tm, tk = 512, 1024
    per = (DFF // NSH) // tk
    nk = DFF // tk

    def body(du_ref, w_ref, h1_ref, dy_ref, g_ref, dh1_ref, dh16_ref, dg_ref, acc):
        i, k = pl.program_id(0), pl.program_id(1)

        @pl.when(k == 0)
        def _():
            acc[...] = jnp.zeros_like(acc)

        @pl.when((k == 0) & (i == 0))
        def _():
            dg_ref[...] = jnp.zeros_like(dg_ref)

        acc[...] += _dot_nt(du_ref[...], w_ref[...])

        @pl.when(k == nk - 1)
        def _():
            dx, dg = _norm_bwd(h1_ref[...], acc[...], g_ref[...])
            dh1 = dy_ref[...] + dx
            dh1_ref[...] = dh1
            dh16_ref[...] = dh1.astype(BF16)
            dg_ref[...] += dg

    row = pl.BlockSpec((tm, D), lambda i, k: (i, 0))
    vec = pl.BlockSpec((1, D), lambda i, k: (0, 0))
    return pl.pallas_call(
        _after(body, deps), name="ffn_up_bwd", out_shape=[SDS((T, D), F32), SDS((T, D), BF16), SDS((1, D), F32)],
        grid=(T // tm, nk),
        in_specs=[DEP_SPEC] * len(deps) + [
            pl.BlockSpec((tm, tk), lambda i, k: (i, k)),
            pl.BlockSpec((None, D, tk), lambda i, k: (k // per, 0, k % per)), row, row, vec],
        out_specs=[row, row, vec], scratch_shapes=[pltpu.VMEM((tm, D), F32)],
        compiler_params=_params(2))(*deps, du, w_up, h1, dy, g)


def _mix_bwd(dh16, w_out, oa, ob16, proj, b_gate, wpa, wpb):
    tm = 256

    def body(dh_ref, wo_ref, oa_ref, ob_ref, ga_ref, gb_ref, ba_ref, bb_ref, wpa_ref, wpb_ref,
             dya_ref, dyb_ref, dga_ref, dgb_ref, doa_ref, dob_ref, dba_ref, dbb_ref):
        @pl.when(pl.program_id(0) == 0)
        def _():
            dba_ref[...] = jnp.zeros_like(dba_ref)
            dbb_ref[...] = jnp.zeros_like(dbb_ref)

        oav, obv = oa_ref[...], ob_ref[...]
        doa = jnp.zeros((tm, 512), F32)
        dob = jnp.zeros((tm, 512), F32)
        for s in range(NSH):
            sl = slice(s * 512, (s + 1) * 512)
            dm = _dot_nt(dh_ref[...], wo_ref[sl, :])
            ga = _sigmoid(ga_ref[:, sl] + ba_ref[:, sl])
            gb = _sigmoid(gb_ref[:, sl] + bb_ref[:, sl])
            dya = (dm * ga).astype(BF16)
            dyb = (dm * gb).astype(BF16)
            dza = dm * _dot(oav, wpa_ref[s]) * ga * (1.0 - ga)
            dzb = dm * _dot(obv, wpb_ref[s]) * gb * (1.0 - gb)
            dya_ref[:, sl], dyb_ref[:, sl] = dya, dyb
            dga_ref[:, sl], dgb_ref[:, sl] = dza.astype(BF16), dzb.astype(BF16)
            dba_ref[:, sl] += jnp.sum(dza, axis=0, keepdims=True)
            dbb_ref[:, sl] += jnp.sum(dzb, axis=0, keepdims=True)
            doa += _dot_nt(dya, wpa_ref[s])
            dob += _dot_nt(dyb, wpb_ref[s])
        doa_ref[...], dob_ref[...] = doa, dob

    row = lambda w: pl.BlockSpec((tm, w), lambda i: (i, 0))
    vec = pl.BlockSpec((1, D), lambda i: (0, 0))
    wp = _resident((NSH, 512, 512), lambda i: (0, 0, 0))
    return pl.pallas_call(
        body, name="mix_bwd",
        out_shape=[SDS((T, D), BF16)] * 4 + [SDS((T, 512), F32)] * 2 + [SDS((1, D), F32)] * 2, grid=(T // tm,),
        in_specs=[row(D), _resident((D, D), lambda i: (0, 0)), row(512), row(512),
                  pl.BlockSpec((tm, D), lambda i: (i, 3)), pl.BlockSpec((tm, D), lambda i: (i, 4)),
                  pl.BlockSpec((1, D), lambda i: (0, 0)), pl.BlockSpec((1, D), lambda i: (0, 1)), wp, wp],
        out_specs=[row(D)] * 4 + [row(512)] * 2 + [vec] * 2,
        compiler_params=_params(1))(dh16, w_out, oa, ob16, proj, proj, b_gate, b_gate, wpa, wpb)


def _comb_bwd(doa, os, ws, deps=()):
    tm = 512

    def body(d_ref, o0, o1, o2, w0, w1, w2, cc_ref):
        prod = d_ref[...] * (w0[...] * o0[...] + w1[...] * o1[...] + w2[...] * o2[...])
        for h in range(4):
            sl = slice(h * HD, (h + 1) * HD)
            cc_ref[:, sl] = jnp.broadcast_to(jnp.sum(prod[:, sl], axis=-1, keepdims=True), (tm, HD))

    spec = pl.BlockSpec((tm, 512), lambda i: (i, 0))
    return pl.pallas_call(
        _after(body, deps), name="comb_bwd", out_shape=SDS((T, 512), F32), grid=(T // tm,),
        in_specs=[DEP_SPEC] * len(deps) + [spec] * 7, out_specs=spec,
        compiler_params=_params(1))(*deps, doa, *os, *ws)


def _attn_a_bwd(qkn, proj, doa, lse, w, cc, g):
    dil = DILS[g]
    m = T // dil
    nb = m // 128
    hp = _heads_per_step(m)

    def body(q_ref, k_ref, v_ref, d_ref, l_ref, w_ref, c_ref, dqk_ref, dv_ref, kp, vp, dkp, dvp):
        for hh in range(hp):
            sl = slice(hh * HD, (hh + 1) * HD)
            _fill_padded(kp, k_ref[:, sl], m)
            _fill_padded(vp, v_ref[:, sl], m)
            dkp[...] = jnp.zeros_like(dkp)
            dvp[...] = jnp.zeros_like(dvp)

            def block(b, carry):
                q0 = pl.multiple_of(b * 128, 128)
                rows = pl.ds(q0, 128)
                win = pl.ds(q0, 256)
                qb, kw, vw = q_ref[rows, sl], kp[win, :], vp[win, :]
                s = _dot_nt(qb, kw) * SCALE
                s = jnp.where(_band_mask(q0, m), s, NEG)
                wp = _wide(w_ref[rows, sl], 2) * jnp.exp(s - _wide(l_ref[rows, sl], 2))
                dob = d_ref[rows, sl].astype(BF16)
                ds = (wp * (_dot_nt(dob, vw) - _wide(c_ref[rows, sl], 2))).astype(BF16)
                dqk_ref[0, rows, sl] = _dot(ds, kw) * SCALE
                dkp[win, :] += _dot_tn(ds, qb) * SCALE
                dvp[win, :] += _dot_tn(wp.astype(BF16), dob)
                return carry

            lax.fori_loop(0, nb, block, 0)
            dqk_ref[1, :, sl] = dkp[64:64 + m, :]
            dv_ref[:, sl] = dvp[64:64 + m, :]

    blk = pl.BlockSpec((m, hp * HD), lambda h, r: (0, r * (4 // hp) + h))
    view = lambda a: a.reshape(m, dil * 512)
    dqk, dv = pl.pallas_call(
        body, name=f"attn_a_bwd_{g}", out_shape=[SDS((2, m, dil * 512), F32), SDS((m, dil * 512), F32)],
        grid=(4 // hp, dil), in_specs=[blk] * 7,
        out_specs=[pl.BlockSpec((2, m, hp * HD), lambda h, r: (0, 0, r * (4 // hp) + h)), blk],
        scratch_shapes=[pltpu.VMEM((m + 128, HD), BF16), pltpu.VMEM((m + 128, HD), BF16),
                        pltpu.VMEM((m + 128, HD), F32), pltpu.VMEM((m + 128, HD), F32)],
        compiler_params=_params(2))(*_group_views(qkn, proj, g), view(doa), view(lse), view(w), view(cc))
    return dqk.reshape(2, T, 512), dv.reshape(T, 512)


def _attn_b_bwd(qkn, proj, dob, ob, lse, bias):
    def body(q_ref, k_ref, v_ref, d_ref, o_ref, l_ref, bias_ref, dqk_ref, dv_ref, drpb_ref, vb, dk_acc, dv_acc, a_acc):
        vb[...] = v_ref[...].astype(BF16)
        dk_acc[...] = jnp.zeros_like(dk_acc)
        dv_acc[...] = jnp.zeros_like(dv_acc)
        a_acc[...] = jnp.zeros_like(a_acc)

        def row(r, carry):
            start, off = _nbr_window(r)
            rows = pl.ds(pl.multiple_of(r * GRID_W, GRID_W), GRID_W)
            win = pl.ds(pl.multiple_of(start * GRID_W, GRID_W), 512)
            qr, kw, vw = q_ref[rows, :], k_ref[win, :], vb[win, :]
            s = _dot_nt(qr, kw) * SCALE + bias_ref[off]
            p = jnp.exp(s - _wide(l_ref[rows, :], 4))
            dov = d_ref[rows, :]
            delta = jnp.sum(dov * o_ref[rows, :], axis=-1, keepdims=True)
            do16 = dov.astype(BF16)
            ds = p * (_dot_nt(do16, vw) - delta)
            a_acc[off] += ds
            ds16 = ds.astype(BF16)
            dqk_ref[0, rows, :] = _dot(ds16, kw) * SCALE
            dk_acc[win, :] += _dot_tn(ds16, qr) * SCALE
            dv_acc[win, :] += _dot_tn(p.astype(BF16), do16)
            return carry

        lax.fori_loop(0, T // GRID_W, row, 0)
        dqk_ref[1] = dk_acc[...]
        dv_ref[...] = dv_acc[...]

        lane = lax.broadcasted_iota(jnp.int32, (16, HD), 1)
        rowi = lax.broadcasted_iota(jnp.int32, (16, HD), 0)
        low = (lane >= GRID_W - WIN_C) & (lane < GRID_W + WIN_C - 1)
        high = (lane >= HD - WIN_C) | (lane < WIN_C - 1)
        flip = (lax.broadcasted_iota(jnp.int32, (GRID_W, GRID_W), 0)
                + lax.broadcasted_iota(jnp.int32, (GRID_W, GRID_W), 1) == GRID_W - 1).astype(BF16)
        out = jnp.zeros((16, HD), F32)
        for d in range(14):
            acc = None
            for off in range(8):
                if 0 <= d - off <= 6 and (d - off) % 2 == 0:
                    jj = (d - off) // 2
                    piece = a_acc[off, :, jj * HD:(jj + 1) * HD]
                    acc = piece if acc is None else acc + piece
            hi = acc.astype(BF16)
            lo = (acc - hi.astype(F32)).astype(BF16)
            rev = _dot(flip, hi) + _dot(flip, lo)
            v = jnp.sum(pltpu.roll(rev, 0, 1, stride=1, stride_axis=0), axis=0, keepdims=True)
            v = jnp.broadcast_to(v, (16, HD))
            out = out + jnp.where((rowi == d) & low, v, 0.0)
            out = out + jnp.where(rowi == d + 1, pltpu.roll(jnp.where(high, v, 0.0), GRID_W, 1), 0.0)
        drpb_ref[...] = out

    blk = pl.BlockSpec((T, HD), lambda h: (0, h))
    return pl.pallas_call(
        body, name="attn_b_bwd",
        out_shape=[SDS((2, T, 512), F32), SDS((T, 512), F32), SDS((4, 16, HD), F32)], grid=(4,),
        in_specs=[pl.BlockSpec((T, HD), lambda h: (0, NHA + h)),
                  pl.BlockSpec((T, HD), lambda h: (0, NH + NHA + h)),
                  pl.BlockSpec((T, HD), lambda h: (0, 2 * NH + NHA + h)), blk, blk, blk,
                  pl.BlockSpec((None, 8, GRID_W, 512), lambda h: (h, 0, 0, 0))],
        out_specs=[pl.BlockSpec((2, T, HD), lambda h: (0, 0, h)), blk,
                   pl.BlockSpec((None, 16, HD), lambda h: (h, 0, 0))],
        scratch_shapes=[pltpu.VMEM((T, HD), BF16), pltpu.VMEM((T, HD), F32), pltpu.VMEM((T, HD), F32),
                        pltpu.VMEM((8, GRID_W, 512), F32)],
        compiler_params=_params(1))(qkn, qkn, proj, dob, ob, lse, bias)


def _qk_bwd(proj, nw, cos, sin, dqk_groups, dqk_b):
    tm = 256

    def body(p_ref, w_ref, cos_ref, sin_ref, d0, d1, d2, d3, o_ref, dn_ref):
        @pl.when(pl.program_id(1) == 0)
        def _():
            dn_ref[...] = jnp.zeros_like(dn_ref)

        cv, sv = cos_ref[...], sin_ref[...]
        srcs = (d0, d1, d2, d3)
        dna = jnp.zeros((1, HD), F32)
        dnb = jnp.zeros((1, HD), F32)
        for h in range(NH):
            sl = slice(h * HD, (h + 1) * HD)
            dz = srcs[h // 4][:, (h % 4) * HD:(h % 4 + 1) * HD]
            if h < NHA:
                dz = dz * cv + pltpu.roll(dz * sv, 64, 1)
            dx, dg = _norm_bwd(p_ref[:, sl], dz, w_ref[:, sl])
            o_ref[:, sl] = dx.astype(BF16)
            if h < NHA:
                dna += dg
            else:
                dnb += dg
        dn_ref[0:1, :] += dna
        dn_ref[1:2, :] += dnb

    dspec = pl.BlockSpec((None, tm, 512), lambda j, i: (j, i, 0))
    return pl.pallas_call(
        body, name="qk_bwd", out_shape=[SDS((T, 2 * D), BF16), SDS((2, 8, HD), F32)], grid=(2, T // tm),
        in_specs=[pl.BlockSpec((tm, D), lambda j, i: (i, j)),
                  pl.BlockSpec((None, 1, D), lambda j, i: (j, 0, 0)),
                  pl.BlockSpec((tm, HD), lambda j, i: (i, 0)),
                  pl.BlockSpec((tm, HD), lambda j, i: (i, 0)), dspec, dspec, dspec, dspec],
        out_specs=[pl.BlockSpec((tm, D), lambda j, i: (i, j)), pl.BlockSpec((None, 8, HD), lambda j, i: (j, 0, 0))],
        compiler_params=_params(2))(proj, nw, cos, sin, *dqk_groups, dqk_b)


def _in_proj_bwd(dproj, w_in, x, dh1, g, deps=()):
    tm, tk = 512, 1280
    per = (DIN // NSH) // tk
    nk = DIN // tk

    def body(dp_ref, w_ref, x_ref, dh_ref, g_ref, dx_ref, dg_ref, acc):
        i, k = pl.program_id(0), pl.program_id(1)

        @pl.when(k == 0)
        def _():
            acc[...] = jnp.zeros_like(acc)

        @pl.when((k == 0) & (i == 0))
        def _():
            dg_ref[...] = jnp.zeros_like(dg_ref)

        acc[...] += _dot_nt(dp_ref[...], w_ref[...])

        @pl.when(k == nk - 1)
        def _():
            dx, dg = _norm_bwd(x_ref[...], acc[...], g_ref[...])
            dx_ref[...] = dh_ref[...] + dx
            dg_ref[...] += dg

    row = pl.BlockSpec((tm, D), lambda i, k: (i, 0))
    vec = pl.BlockSpec((1, D), lambda i, k: (0, 0))
    return pl.pallas_call(
        _after(body, deps), name="in_proj_bwd", out_shape=[SDS((T, D), F32), SDS((1, D), F32)], grid=(T // tm, nk),
        in_specs=[DEP_SPEC] * len(deps) + [
            pl.BlockSpec((tm, tk), lambda i, k: (i, k)),
            pl.BlockSpec((None, D, tk), lambda i, k: (k // per, 0, k % per)), row, row, vec],
        out_specs=[row, vec], scratch_shapes=[pltpu.VMEM((tm, D), F32)],
        compiler_params=_params(2))(*deps, dproj, w_in, x, dh1, g)


def _grad_w(name, a, g, shard_rows, rows, cols, tr, tc):
    ni, nj = rows // tr, cols // tc
    if shard_rows:
        a_map, g_map = (lambda s, i, j: (0, s * ni + i)), (lambda s, i, j: (0, j))
    else:
        a_map, g_map = (lambda s, i, j: (0, i)), (lambda s, i, j: (0, s * nj + j))

    def body(a_ref, g_ref, o_ref):
        o_ref[...] = _dot_tn(a_ref[...], g_ref[...]).astype(BF16)

    return pl.pallas_call(
        body, name=name, out_shape=SDS((NSH, rows, cols), BF16), grid=(NSH, ni, nj),
        in_specs=[pl.BlockSpec((T, tr), a_map), pl.BlockSpec((T, tc), g_map)],
        out_specs=pl.BlockSpec((None, tr, tc), lambda s, i, j: (s, i, j)), compiler_params=_params(3))(a, g)


def _adamw(w, g, m, v):
    m = B1 * m + (1.0 - B1) * g
    v = B2 * v + (1.0 - B2) * (g * g)
    m_hat = m / (1.0 - B1 ** STEP)
    v_hat = v / (1.0 - B2 ** STEP)
    delta = -LR * (m_hat / (jnp.sqrt(v_hat) + AEPS) + WD * w)
    return delta, m, v


def _sum_halves(name, place, grads, theirs):
    _, rows, cols = theirs.shape
    tr = _row_tile(rows, cols, 1 << 17)

    def body(place_ref, a_ref, b_ref, o_ref):
        o_ref[...] = (a_ref[...].astype(F32) + b_ref[...].astype(F32)).astype(BF16)

    spec = pl.BlockSpec((NSH, tr, cols), lambda i, p: (0, i, 0))
    return pl.pallas_call(
        body, name=name, out_shape=SDS(theirs.shape, BF16),
        grid_spec=pltpu.PrefetchScalarGridSpec(
            num_scalar_prefetch=1, grid=(rows // tr,),
            in_specs=[pl.BlockSpec((NSH, None, tr, cols), lambda i, p: (0, p[1], i, 0)), spec], out_specs=spec),
        compiler_params=_params(1))(place, grads, theirs)


def _sum_landed(name, place, part, landed):
    _, rows, cols = part.shape
    tr = _row_tile(rows, cols, 1 << 18)

    def body(place_ref, p_ref, l_ref, o_ref):
        o_ref[...] = ((p_ref[...].astype(F32) + l_ref[0].astype(F32)) + l_ref[1].astype(F32)) + l_ref[2].astype(F32)

    return pl.pallas_call(
        body, name=name, out_shape=SDS((2, rows, cols), F32),
        grid_spec=pltpu.PrefetchScalarGridSpec(
            num_scalar_prefetch=1, grid=(rows // tr,),
            in_specs=[pl.BlockSpec((None, tr, cols), lambda i, p: (p[0], i, 0)),
                      pl.BlockSpec((3, tr, cols), lambda i, p: (0, i, 0))],
            out_specs=pl.BlockSpec((None, tr, cols), lambda i, p: (p[1], i, 0))),
        compiler_params=_params(1))(place, part, landed)


def _adam_shard(name, g, w, m, v):
    rows, cols = w.shape
    tr = _row_tile(rows, cols, 1 << 18)

    def body(g_ref, w_ref, m_ref, v_ref, d_ref, nm_ref, nv_ref):
        d_ref[...], nm_ref[...], nv_ref[...] = _adamw(w_ref[...], g_ref[...], m_ref[...], v_ref[...])

    spec = pl.BlockSpec((tr, cols), lambda i: (i, 0))
    return pl.pallas_call(
        body, name=name, out_shape=[SDS((rows, cols), F32)] * 3, grid=(rows // tr,),
        in_specs=[spec] * 4, out_specs=[spec] * 3, compiler_params=_params(1))(g, w, m, v)


def _adam_small(gathered, w, m, v):
    def body(g_ref, w_ref, m_ref, v_ref, go_ref, d_ref, nm_ref, nv_ref):
        g = g_ref[0:SMALL_ROWS, :]
        for dev in range(1, 8):
            g = g + g_ref[dev * SMALL_ROWS:(dev + 1) * SMALL_ROWS, :]
        go_ref[...] = g
        d_ref[...], nm_ref[...], nv_ref[...] = _adamw(w_ref[...], g, m_ref[...], v_ref[...])

    return pl.pallas_call(body, name="adam_small", out_shape=[SDS((SMALL_ROWS, HD), F32)] * 4)(gathered, w, m, v)


SMALL = (("norm_mix", (1, D)), ("b_gate", (1, 2 * D)), ("q_norm_a", (1, HD)), ("k_norm_a", (1, HD)),
         ("q_norm_b", (1, HD)), ("k_norm_b", (1, HD)), ("rpb_b", (1, 4, 15, 31)), ("norm_ffn", (1, D)))


def _pack_small(vals):
    pieces = []
    for (name, shape), val in zip(SMALL, vals):
        flat = val.reshape(-1)
        pad = (-flat.shape[0]) % HD
        pieces.append(jnp.pad(flat, (0, pad)).reshape(-1, HD))
    packed = jnp.concatenate(pieces, axis=0)
    return jnp.pad(packed, ((0, SMALL_ROWS - packed.shape[0]), (0, 0)))


def _unpack_small(packed):
    out, row = [], 0
    for name, shape in SMALL:
        size = int(np.prod(shape))
        nrows = -(-size // HD)
        out.append(packed[row:row + nrows].reshape(-1)[:size].reshape(shape))
        row += nrows
    return out


def kernel(x, norm_mix, w_in, b_gate, q_norm_a, k_norm_a, q_norm_b, k_norm_b, rpb_b, w_proj_a, w_proj_b, w_out, norm_ffn, w_up, w_down, loss_target, m_norm_mix, m_w_in, m_b_gate, m_q_norm_a, m_k_norm_a, m_q_norm_b, m_k_norm_b, m_rpb_b, m_w_proj_a, m_w_proj_b, m_w_out, m_norm_ffn, m_w_up, m_w_down, v_norm_mix, v_w_in, v_b_gate, v_q_norm_a, v_k_norm_a, v_q_norm_b, v_k_norm_b, v_rpb_b, v_w_proj_a, v_w_proj_b, v_w_out, v_norm_ffn, v_w_up, v_w_down):
    big_names = ("w_in", "w_proj_a", "w_proj_b", "w_out", "w_up", "w_down")
    big_w = [a[0] for a in (w_in, w_proj_a, w_proj_b, w_out, w_up, w_down)]
    big_m = [a[0] for a in (m_w_in, m_w_proj_a, m_w_proj_b, m_w_out, m_w_up, m_w_down)]
    big_v = [a[0] for a in (v_w_in, v_w_proj_a, v_w_proj_b, v_w_out, v_w_up, v_w_down)]
    x2, target = x[0], loss_target[0]

    place = jnp.stack([2 * lax.axis_index("x") + lax.axis_index("y"), lax.axis_index("c")]).astype(jnp.int32)
    groups = ((0,), (1, 2, 3), (4,), (5,))
    started = []
    for j, grp in enumerate(groups):
        deps = (started[0][4],) if j else ()
        placed = [_cast_into_place(big_w[i], "cast_" + big_names[i], place, deps) for i in grp]
        started.append(_gather_start(f"gather_start_{j}", placed))

    def gathered(j, after):
        send, recv, _, fulls, _ = started[j]
        fulls = _gather_wait(f"gather_wait_{j}", send, recv, fulls, after)
        fulls = _gather_finish(f"gather_finish_{j}", fulls)
        return [f.reshape(NSH, 2 * f.shape[2], f.shape[3]) for f in fulls]

    def reduce_begin(j, grads):
        grads = [g.reshape(NSH, 2, g.shape[1] // 2, g.shape[2]) for g in grads]
        theirs = _reduce_exchange(f"reduce_exchange_{j}", grads)
        parts = [_sum_halves(f"sum_halves_{j}_{i}", place, a, b) for i, (a, b) in enumerate(zip(grads, theirs))]
        send, recv, parts, lands, token = _reduce_start(f"reduce_start_{j}", parts)
        return (send, recv, parts, lands), token

    big_out = {}

    def reduce_end(j, state, after):
        send, recv, parts, lands = state
        parts, lands = _reduce_wait(f"reduce_wait_{j}", send, recv, parts, lands, after)
        sums = [_sum_landed(f"sum_landed_{j}_{i}", place, p, l) for i, (p, l) in enumerate(zip(parts, lands))]
        for idx, g in zip(groups[j], _reduce_share(f"reduce_share_{j}", sums)):
            g = g.reshape(big_w[idx].shape)
            big_out[idx] =(g, *_adam_shard("adam_" + big_names[idx], g, big_w[idx], big_m[idx], big_v[idx]))
        return big_out[groups[j][-1]][1]

    (win_f,) = gathered(0, tuple(s[4] for s in started[1:]))
    proj, xn = _norm_in_proj(x2, norm_mix, win_f)
    cos, sin = _rope_tables()
    nw = jnp.stack([jnp.concatenate([jnp.tile(q_norm_a, (1, NHA)), jnp.tile(q_norm_b, (1, NH - NHA))], axis=1),
                    jnp.concatenate([jnp.tile(k_norm_a, (1, NHA)), jnp.tile(k_norm_b, (1, NH - NHA))], axis=1)])
    qkn = _qk_prep(proj, nw, cos, sin)
    wpa_f, wpb_f, wout_f = gathered(1, (qkn,))
    wout_f = wout_f.reshape(D, D)
    fwd_a = [_attn_a_fwd(qkn, proj, g) for g in range(3)]
    os, ls = [f[0] for f in fwd_a], [f[1] for f in fwd_a]
    ob, lse_b, bias = _attn_b_fwd(qkn, proj, _rpb_rows(rpb_b[0]))
    oa, w0, w1, w2 = _comb_fwd(os, ls)
    ws = [w0, w1, w2]
    mixed, ob16 = _mix_fwd(oa, ob, proj, b_gate, wpa_f, wpb_f)
    h1, hn = _out_proj_fwd(mixed, wout_f, x2, norm_ffn)
    (wup_f,) = gathered(2, (h1,))
    usq, u = _ffn_up(hn, wup_f)
    (wdown_f,) = gathered(3, (u,))
    wdown_f = wdown_f.reshape(DFF, D)
    dy, dy16, loss_parts = _ffn_down_loss(usq, wdown_f, h1, target)
    loss = lax.psum(jnp.sum(loss_parts[:, 0, 0]), ("x", "y", "c"))

    g_down = _grad_w("grad_w_down", usq, dy16, True, DFF // NSH, D, 1024, 1024)
    red_down, token = reduce_begin(3, [g_down])
    du = _ffn_down_bwd(dy16, wdown_f, u, deps=(token,))
    g_up = _grad_w("grad_w_up", hn, du, False, D, DFF // NSH, 1024, 1024)
    red_up, token = reduce_begin(2, [g_up])
    dh1, dh16, d_norm_ffn = _ffn_up_bwd(du, wup_f, h1, dy, norm_ffn, deps=(token,))
    dya, dyb, dga, dgb, doa, dob, dba, dbb = _mix_bwd(dh16, wout_f, oa, ob16, proj, b_gate, wpa_f, wpb_f)
    g_out = _grad_w("grad_w_out", mixed, dh16, True, D // NSH, D, 512, 1024)
    g_pa = _grad_w("grad_w_proj_a", oa, dya, False, 512, 512, 512, 512)
    g_pb = _grad_w("grad_w_proj_b", ob16, dyb, False, 512, 512, 512, 512)
    red_mid, token = reduce_begin(1, [g_pa, g_pb, g_out])
    cc = _comb_bwd(doa, os, ws, deps=(token,))
    bwd_a = [_attn_a_bwd(qkn, proj, doa, ls[g], ws[g], cc, g) for g in range(3)]
    dqk_b, dv_b, drpb_t = _attn_b_bwd(qkn, proj, dob, ob, lse_b, bias)
    dqk_pre, dn = _qk_bwd(proj, nw, cos, sin, [b[0] for b in bwd_a], dqk_b)
    dv16 = jnp.concatenate([b[1] for b in bwd_a] + [dv_b], axis=1).astype(BF16)
    dproj = jnp.concatenate([dqk_pre, dv16, dga, dgb], axis=1)
    g_in = _grad_w("grad_w_in", xn, dproj, False, D, DIN // NSH, 1024, 1280)
    red_in, token = reduce_begin(0, [g_in])
    grad_x, d_norm_mix = _in_proj_bwd(dproj, win_f, x2, dh1, norm_mix, deps=(token,))

    done = reduce_end(3, red_down, (grad_x,))
    done = reduce_end(2, red_up, (done,))
    done = reduce_end(1, red_mid, (done,))
    done = reduce_end(0, red_in, (done,))

    d_rpb = drpb_t[:, :15, GRID_W - WIN_C:GRID_W + WIN_C - 1]
    small_g = [d_norm_mix, jnp.concatenate([dba, dbb], axis=1), dn[0, 0], dn[1, 0], dn[0, 1], dn[1, 1], d_rpb, d_norm_ffn]
    gathered_small = _allgather_small(_pack_small(small_g), done)
    small_w = (norm_mix, b_gate, q_norm_a, k_norm_a, q_norm_b, k_norm_b, rpb_b, norm_ffn)
    small_m = (m_norm_mix, m_b_gate, m_q_norm_a, m_k_norm_a, m_q_norm_b, m_k_norm_b, m_rpb_b, m_norm_ffn)
    small_v = (v_norm_mix, v_b_gate, v_q_norm_a, v_k_norm_a, v_q_norm_b, v_k_norm_b, v_rpb_b, v_norm_ffn)
    small_out = [_unpack_small(p) for p in
                 _adam_small(gathered_small, _pack_small(small_w), _pack_small(small_m), _pack_small(small_v))]

    order = ("norm_mix", "w_in", "b_gate", "q_norm_a", "k_norm_a", "q_norm_b", "k_norm_b", "rpb_b",
             "w_proj_a", "w_proj_b", "w_out", "norm_ffn", "w_up", "w_down")
    small_idx = {name: i for i, (name, _) in enumerate(SMALL)}
    outs = []
    for kind in range(4):
        for name in order:
            if name in small_idx:
                outs.append(small_out[kind][small_idx[name]])
            else:
                outs.append(big_out[big_names.index(name)][kind][None])
    return (loss, grad_x[None], *outs)
```

```python
import functools

import numpy as np
import jax
import jax.numpy as jnp
from jax import lax
from jax.experimental import pallas as pl
from jax.experimental.pallas import tpu as pltpu

F32, BF16 = jnp.float32, jnp.bfloat16
SDS = jax.ShapeDtypeStruct
MESH = pl.DeviceIdType.MESH

T = 2048
D = 2048
HD = 128
NH, NHA = 16, 12
DIN = 10240
DFF = 8192
NSH = 4
DILS = (1, 4, 16)
EPS = 1e-6
NEG = -1e30
SCALE = HD ** -0.5
GRID_W, WIN_R, WIN_C = 64, 8, 16
VMEM_LIMIT = 56 * 1024 * 1024
B1, B2, LR, AEPS, WD, STEP = 0.9, 0.999, 0.001, 1e-08, 0.01, 10
SMALL_ROWS = 88


def _dot(a, b):
    return jnp.dot(a, b, preferred_element_type=F32)


def _dot_nt(a, b):
    return lax.dot_general(a, b, (((1,), (1,)), ((), ())), preferred_element_type=F32)


def _dot_tn(a, b):
    return lax.dot_general(a, b, (((0,), (0,)), ((), ())), preferred_element_type=F32)


def _params(n):
    return pltpu.CompilerParams(dimension_semantics=("arbitrary",) * n, vmem_limit_bytes=VMEM_LIMIT)


def _resident(shape, index_map):
    return pl.BlockSpec(shape, index_map, pipeline_mode=pl.Buffered(1))


def _sigmoid(z):
    return 1.0 / (1.0 + jnp.exp(-z))


def _wide(v, n):
    return jnp.concatenate([v] * n, axis=1)


def _row_tile(rows, cols, elems):
    tr = 16
    while tr * 2 <= rows and tr * 2 * cols <= elems:
        tr *= 2
    return tr


def _place():
    x, y, c = lax.axis_index("x"), lax.axis_index("y"), lax.axis_index("c")
    peers = [(1 - x, y), (x, 1 - y), (1 - x, 1 - y)]
    return x, y, c, peers


def _cast_into_place(w, name, place, deps=()):
    rows, cols = w.shape
    hr = rows // 2
    tr = min(hr, 256)
    per = hr // tr

    def body(*refs):
        w_ref, o_ref = refs[-2:]
        o_ref[...] = w_ref[...].astype(BF16)

    return pl.pallas_call(
        body, name=name, out_shape=SDS((NSH, 2, hr, cols), BF16),
        grid_spec=pltpu.PrefetchScalarGridSpec(
            num_scalar_prefetch=1, grid=(2, per),
            in_specs=[DEP_SPEC] * len(deps) + [pl.BlockSpec((tr, cols), lambda h, i, p: (h * per + i, 0))],
            out_specs=pl.BlockSpec((None, None, tr, cols), lambda h, i, p: (p[0], h, i, 0))),
        compiler_params=_params(2))(place, *deps, w)


ANY_SPEC = pl.BlockSpec(memory_space=pl.ANY)
HBM_SPEC = pl.BlockSpec(memory_space=pltpu.HBM)
SEM_SPEC = pl.BlockSpec(memory_space=pltpu.SEMAPHORE)
DEP_SPEC = pl.BlockSpec((8, 128), lambda *_: (0, 0))
EFFECT = pltpu.SideEffectType.DATAFLOW_SIDE_EFFECTING


def _after(body, deps):
    n = len(deps)
    return (lambda *refs: body(*refs[n:])) if n else body


def _split_start(name, srcs, lands, n_copies, issue):
    n, m = len(srcs), len(lands)

    def body(*refs):
        issue(refs[:n], refs[n:n + m], refs[n + m], refs[n + m + 1])
        refs[-1][...] = jnp.zeros((8, 128), F32)

    arrays = list(srcs) + list(lands)
    outs = pl.pallas_call(
        body, name=name,
        out_shape=(pltpu.SemaphoreType.DMA((n_copies,)), pltpu.SemaphoreType.DMA((n_copies,)),
                   *[pltpu.HBM(a.shape, a.dtype) for a in arrays], SDS((8, 128), F32)),
        in_specs=[HBM_SPEC] * (n + m),
        out_specs=(SEM_SPEC, SEM_SPEC, *[HBM_SPEC] * (n + m), pl.BlockSpec(memory_space=pltpu.VMEM)),
        input_output_aliases={i: 2 + i for i in range(n + m)},
        compiler_params=pltpu.CompilerParams(has_side_effects=EFFECT),
    )(*[pltpu.with_memory_space_constraint(a, pltpu.HBM) for a in arrays])
    return outs[0], outs[1], list(outs[2:2 + n]), list(outs[2 + n:2 + n + m]), outs[-1]


def _split_wait(name, send_sems, recv_sems, srcs, lands, after, wait):
    n, m = len(srcs), len(lands)

    def body(*refs):
        wait(refs[:n], refs[n:n + m], refs[n + m], refs[n + m + 1])

    arrays = list(srcs) + list(lands)
    outs = pl.pallas_call(
        body, name=name, out_shape=[pltpu.HBM(a.shape, a.dtype) for a in arrays],
        in_specs=[HBM_SPEC] * (n + m) + [SEM_SPEC, SEM_SPEC] + [ANY_SPEC] * len(after),
        out_specs=[HBM_SPEC] * (n + m), input_output_aliases={i: i for i in range(n + m)},
        compiler_params=pltpu.CompilerParams(has_side_effects=EFFECT),
    )(*arrays, send_sems, recv_sems, *after)
    return list(outs[:n]), list(outs[n:])


def _gather_start(name, fulls):
    def issue(srcs, dsts, send_sems, recv_sems):
        x, y, c, peers = _place()
        for i in range(len(fulls)):
            mine = dsts[i].at[2 * x + y, c]
            for k, (px, py) in enumerate(peers):
                pltpu.make_async_remote_copy(
                    src_ref=mine, dst_ref=mine, send_sem=send_sems.at[3 * i + k],
                    recv_sem=recv_sems.at[3 * i + k], device_id=(px, py, c), device_id_type=MESH).start()

    return _split_start(name, [], fulls, 3 * len(fulls), issue)


def _gather_wait(name, send_sems, recv_sems, fulls, after):
    def wait(srcs, dsts, send_sems, recv_sems):
        x, y, c, peers = _place()
        for i in range(len(fulls)):
            for k, (px, py) in enumerate(peers):
                cp = pltpu.make_async_remote_copy(
                    src_ref=dsts[i].at[2 * x + y, c], dst_ref=dsts[i].at[2 * px + py, c],
                    send_sem=send_sems.at[3 * i + k], recv_sem=recv_sems.at[3 * i + k],
                    device_id=(px, py, c), device_id_type=MESH)
                cp.wait_send()
                cp.wait_recv()

    return _split_wait(name, send_sems, recv_sems, [], fulls, after, wait)[1]


def _gather_finish(name, fulls):
    n = len(fulls)

    def body(*refs):
        fin, fout = refs[:n], refs[n:2 * n]
        send_sems, recv_sems = refs[2 * n:]
        x, y, c, peers = _place()

        def copy(i, k, half):
            px, py = peers[k]
            return pltpu.make_async_remote_copy(
                src_ref=fin[i].at[2 * px + py, half], dst_ref=fout[i].at[2 * px + py, half],
                send_sem=send_sems.at[3 * i + k], recv_sem=recv_sems.at[3 * i + k],
                device_id=(x, y, 1 - c), device_id_type=MESH)

        sends = [copy(i, k, c) for i in range(n) for k in range(3)]
        for cp in sends:
            cp.start()
        for i in range(n):
            for k in range(3):
                copy(i, k, 1 - c).wait_recv()
        for cp in sends:
            cp.wait_send()

    return pl.pallas_call(
        body, name=name, out_shape=[SDS(f.shape, f.dtype) for f in fulls],
        in_specs=[ANY_SPEC] * n, out_specs=[ANY_SPEC] * n, input_output_aliases={i: i for i in range(n)},
        scratch_shapes=[pltpu.SemaphoreType.DMA((3 * n,)), pltpu.SemaphoreType.DMA((3 * n,))])(*fulls)


def _reduce_exchange(name, grads):
    n = len(grads)

    def body(*refs):
        ins, theirs = refs[:n], refs[n:2 * n]
        send_sems, recv_sems = refs[2 * n:]
        x, y, c, _ = _place()
        copies = []
        for i in range(n):
            cp = pltpu.make_async_remote_copy(
                src_ref=ins[i].at[:, 1 - c], dst_ref=theirs[i], send_sem=send_sems.at[i],
                recv_sem=recv_sems.at[i], device_id=(x, y, 1 - c), device_id_type=MESH)
            cp.start()
            copies.append(cp)
        for cp in copies:
            cp.wait_recv()
            cp.wait_send()

    return pl.pallas_call(
        body, name=name, out_shape=[SDS((NSH,) + g.shape[2:], g.dtype) for g in grads],
        in_specs=[ANY_SPEC] * n, out_specs=[ANY_SPEC] * n,
        scratch_shapes=[pltpu.SemaphoreType.DMA((n,)), pltpu.SemaphoreType.DMA((n,))])(*grads)


def _reduce_start(name, parts):
    lands = [lax.empty((3,) + p.shape[1:], p.dtype) for p in parts]

    def issue(srcs, dsts, send_sems, recv_sems):
        x, y, c, peers = _place()
        for i in range(len(parts)):
            for k, (px, py) in enumerate(peers):
                pltpu.make_async_remote_copy(
                    src_ref=srcs[i].at[2 * px + py], dst_ref=dsts[i].at[k], send_sem=send_sems.at[3 * i + k],
                    recv_sem=recv_sems.at[3 * i + k], device_id=(px, py, c), device_id_type=MESH).start()

    return _split_start(name, parts, lands, 3 * len(parts), issue)


def _reduce_wait(name, send_sems, recv_sems, parts, lands, after):
    def wait(srcs, dsts, send_sems, recv_sems):
        x, y, c, peers = _place()
        for i in range(len(parts)):
            for k, (px, py) in enumerate(peers):
                cp = pltpu.make_async_remote_copy(
                    src_ref=srcs[i].at[2 * px + py], dst_ref=dsts[i].at[k], send_sem=send_sems.at[3 * i + k],
                    recv_sem=recv_sems.at[3 * i + k], device_id=(px, py, c), device_id_type=MESH)
                cp.wait_send()
                cp.wait_recv()

    return _split_wait(name, send_sems, recv_sems, parts, lands, after, wait)


def _reduce_share(name, sums):
    n = len(sums)

    def body(*refs):
        ins, outs = refs[:n], refs[n:2 * n]
        send_sems, recv_sems = refs[2 * n:]
        x, y, c, _ = _place()
        copies = []
        for i in range(n):
            cp = pltpu.make_async_remote_copy(
                src_ref=ins[i].at[c], dst_ref=outs[i].at[c], send_sem=send_sems.at[i], recv_sem=recv_sems.at[i],
                device_id=(x, y, 1 - c), device_id_type=MESH)
            cp.start()
            copies.append(cp)
        for i, cp in enumerate(copies):
            pltpu.make_async_remote_copy(
                src_ref=ins[i].at[c], dst_ref=outs[i].at[1 - c], send_sem=send_sems.at[i],
                recv_sem=recv_sems.at[i], device_id=(x, y, 1 - c), device_id_type=MESH).wait_recv()
            cp.wait_send()

    return pl.pallas_call(
        body, name=name, out_shape=[SDS(s.shape, s.dtype) for s in sums],
        in_specs=[ANY_SPEC] * n, out_specs=[ANY_SPEC] * n, input_output_aliases={i: i for i in range(n)},
        scratch_shapes=[pltpu.SemaphoreType.DMA((n,)), pltpu.SemaphoreType.DMA((n,))])(*sums)


def _allgather_small(v, after):
    m_per, n = v.shape

    def body(x_ref, after_ref, out_ref, send_sems, recv_sems, local_sem):
        x, y, c = lax.axis_index("x"), lax.axis_index("y"), lax.axis_index("c")
        me, sibling = (x, y, c), (x, y, 1 - c)
        chips = [(1 - x, y), (x, 1 - y), (1 - x, 1 - y)]

        def rows(px, py, pc):
            return out_ref.at[pl.ds((4 * px + 2 * py + pc) * m_per, m_per), :]

        def copy(k, block, to, src=None):
            return pltpu.make_async_remote_copy(
                src_ref=rows(*block) if src is None else src, dst_ref=rows(*block),
                send_sem=send_sems.at[k], recv_sem=recv_sems.at[k], device_id=to, device_id_type=MESH)

        mine = pltpu.make_async_copy(x_ref, rows(*me), local_sem)
        mine.start()
        first = [copy(0, me, sibling, src=x_ref)]
        first += [copy(1 + j, me, (*chip, c), src=x_ref) for j, chip in enumerate(chips)]
        for cp in first:
            cp.start()
        passed = [copy(4 + j, (*chip, c), sibling) for j, chip in enumerate(chips)]
        for j, chip in enumerate(chips):
            copy(1 + j, (*chip, c), me).wait_recv()
            passed[j].start()
        copy(0, sibling, me).wait_recv()
        for j, chip in enumerate(chips):
            copy(4 + j, (*chip, 1 - c), me).wait_recv()
        for cp in first + passed:
            cp.wait_send()
        mine.wait()

    return pl.pallas_call(
        body, name="allgather_small", out_shape=SDS((8 * m_per, n), v.dtype),
        in_specs=[pl.BlockSpec(memory_space=pltpu.VMEM), ANY_SPEC], out_specs=pl.BlockSpec(memory_space=pltpu.VMEM),
        scratch_shapes=[pltpu.SemaphoreType.DMA((7,)), pltpu.SemaphoreType.DMA((7,)), pltpu.SemaphoreType.DMA])(v, after)


def _norm_in_proj(x, g, w_full):
    tn, chunk = 512, 256
    per = (DIN // NSH) // tn

    def body(x_ref, g_ref, w_ref, proj_ref, xn_ref):
        @pl.when(pl.program_id(0) == 0)
        def _():
            def norm(r, carry):
                rows = pl.ds(pl.multiple_of(r * chunk, chunk), chunk)
                xv = x_ref[rows, :]
                rs = lax.rsqrt(jnp.mean(xv * xv, axis=-1, keepdims=True) + EPS)
                xn_ref[rows, :] = (xv * rs * g_ref[...]).astype(BF16)
                return carry

            lax.fori_loop(0, T // chunk, norm, 0)

        proj_ref[...] = _dot(xn_ref[...], w_ref[...])

    return pl.pallas_call(
        body, name="norm_in_proj", out_shape=[SDS((T, DIN), F32), SDS((T, D), BF16)], grid=(DIN // tn,),
        in_specs=[_resident((T, D), lambda j: (0, 0)),
                  pl.BlockSpec((1, D), lambda j: (0, 0)),
                  pl.BlockSpec((None, D, tn), lambda j: (j // per, 0, j % per))],
        out_specs=[pl.BlockSpec((T, tn), lambda j: (0, j)),
                   pl.BlockSpec((T, D), lambda j: (0, 0))],
        compiler_params=_params(1))(x, g, w_full)


def _rope_tables():
    pos = np.arange(T, dtype=np.float32)
    inv = (10000.0 ** (-np.arange(0, HD, 2, dtype=np.float32) / HD)).astype(np.float32)
    ang = (pos[:, None] * inv[None, :]).astype(np.float32)
    cos, sin = np.cos(ang).astype(np.float32), np.sin(ang).astype(np.float32)
    return (jnp.asarray(np.concatenate([cos, cos], axis=1)), jnp.asarray(np.concatenate([-sin, sin], axis=1)))


def _qk_prep(proj, nw, cos, sin):
    tm = 256

    def body(p_ref, w_ref, cos_ref, sin_ref, o_ref):
        cv, sv = cos_ref[...], sin_ref[...]
        for h in range(NH):
            sl = slice(h * HD, (h + 1) * HD)
            xv = p_ref[:, sl]
            r = lax.rsqrt(jnp.mean(xv * xv, axis=-1, keepdims=True) + EPS)
            z = xv * r * w_ref[:, sl]
            if h < NHA:
                z = z * cv + pltpu.roll(z, 64, 1) * sv
            o_ref[:, sl] = z.astype(BF16)

    return pl.pallas_call(
        body, name="qk_prep", out_shape=SDS((T, 2 * D), BF16), grid=(T // tm, 2),
        in_specs=[pl.BlockSpec((tm, D), lambda i, j: (i, j)),
                  pl.BlockSpec((None, 1, D), lambda i, j: (j, 0, 0)),
                  pl.BlockSpec((tm, HD), lambda i, j: (i, 0)),
                  pl.BlockSpec((tm, HD), lambda i, j: (i, 0))],
        out_specs=pl.BlockSpec((tm, D), lambda i, j: (i, j)),
        compiler_params=_params(2))(proj, nw, cos, sin)


def _band_mask(q0, m):
    ii = lax.broadcasted_iota(jnp.int32, (128, 256), 0)
    jj = lax.broadcasted_iota(jnp.int32, (128, 256), 1)
    rel = jj - ii
    kpos = jj + (q0 - 64)
    return (rel >= 0) & (rel <= 128) & (kpos >= 0) & (kpos < m)


def _fill_padded(dst, src, m):
    zeros = jnp.zeros((64, HD), dst.dtype)
    dst[0:64, :] = zeros
    dst[64 + m:128 + m, :] = zeros
    dst[64:64 + m, :] = src.astype(dst.dtype)


def _group_views(qkn, proj, g):
    m = T // DILS[g]
    cols = (qkn[:, g * 512:(g + 1) * 512], qkn[:, D + g * 512:D + (g + 1) * 512],
            proj[:, 2 * D + g * 512:2 * D + (g + 1) * 512])
    return [a.reshape(m, DILS[g] * 512) for a in cols]


def _heads_per_step(m):
    return 4 if m <= 512 else 1


def _attn_a_fwd(qkn, proj, g):
    dil = DILS[g]
    m = T // dil
    nb = m // 128
    hp = _heads_per_step(m)

    def body(q_ref, k_ref, v_ref, o_ref, l_ref, kp, vp):
        for hh in range(hp):
            sl = slice(hh * HD, (hh + 1) * HD)
            _fill_padded(kp, k_ref[:, sl], m)
            _fill_padded(vp, v_ref[:, sl], m)

            def block(b, carry):
                q0 = pl.multiple_of(b * 128, 128)
                kw, vw = kp[pl.ds(q0, 256), :], vp[pl.ds(q0, 256), :]
                s = _dot_nt(q_ref[pl.ds(q0, 128), sl], kw) * SCALE
                s = jnp.where(_band_mask(q0, m), s, NEG)
                mx = jnp.max(s, axis=-1, keepdims=True)
                e = jnp.exp(s - mx)
                den = jnp.sum(e, axis=-1, keepdims=True)
                o_ref[pl.ds(q0, 128), sl] = _dot((e / den).astype(BF16), vw)
                l_ref[pl.ds(q0, 128), sl] = jnp.broadcast_to(mx + jnp.log(den), (128, HD))
                return carry

            lax.fori_loop(0, nb, block, 0, unroll=min(nb, 2))

    blk = pl.BlockSpec((m, hp * HD), lambda h, r: (0, r * (4 // hp) + h))
    o, lse = pl.pallas_call(
        body, name=f"attn_a_fwd_{g}", out_shape=[SDS((m, dil * 512), F32)] * 2, grid=(4 // hp, dil),
        in_specs=[blk] * 3, out_specs=[blk] * 2,
        scratch_shapes=[pltpu.VMEM((m + 128, HD), BF16), pltpu.VMEM((m + 128, HD), BF16)],
        compiler_params=_params(2))(*_group_views(qkn, proj, g))
    return o.reshape(T, 512), lse.reshape(T, 512)


def _nbr_window(r):
    start = jnp.clip(r - WIN_R // 2, 0, T // GRID_W - WIN_R)
    return start, start - r + (WIN_R - 1)


def _rpb_rows(rpb):
    zeros = jnp.zeros((4, 14, 33), F32)
    a, b = rpb[:, :14], rpb[:, 1:15]
    rows = jnp.concatenate([a[:, :, 15:31], zeros, b, zeros, a[:, :, 0:15]], axis=2)
    return jnp.pad(rows, ((0, 0), (0, 2), (0, 0)))


def _attn_b_fwd(qkn, proj, rpb_rows):
    def body(r_ref, q_ref, k_ref, v_ref, o_ref, l_ref, bias_ref, vb, pair):
        qc = lax.broadcasted_iota(jnp.int32, (GRID_W, 512), 0)
        kc = lax.broadcasted_iota(jnp.int32, (GRID_W, 512), 1) & (GRID_W - 1)
        cs = jnp.clip(qc - WIN_C // 2, 0, GRID_W - WIN_C)
        colmask = (kc >= cs) & (kc < cs + WIN_C)
        for d in range(14):
            pair[d] = pltpu.roll(jnp.broadcast_to(r_ref[d:d + 1, :], (GRID_W, HD)), 0, 1, stride=1, stride_axis=0)
        for off in range(8):
            rows = jnp.concatenate([pair[off + 2 * jj] for jj in range(4)], axis=1)
            bias_ref[off] = jnp.where(colmask, rows, NEG)
        vb[...] = v_ref[...].astype(BF16)

        def row(r, carry):
            start, off = _nbr_window(r)
            q0 = pl.multiple_of(r * GRID_W, GRID_W)
            k0 = pl.multiple_of(start * GRID_W, GRID_W)
            s = _dot_nt(q_ref[pl.ds(q0, GRID_W), :], k_ref[pl.ds(k0, 512), :]) * SCALE + bias_ref[off]
            mx = jnp.max(s, axis=-1, keepdims=True)
            e = jnp.exp(s - mx)
            den = jnp.sum(e, axis=-1, keepdims=True)
            o_ref[pl.ds(q0, GRID_W), :] = _dot((e / den).astype(BF16), vb[pl.ds(k0, 512), :])
            l_ref[pl.ds(q0, GRID_W), :] = jnp.broadcast_to(mx + jnp.log(den), (GRID_W, HD))
            return carry

        lax.fori_loop(0, T // GRID_W, row, 0, unroll=2)

    return pl.pallas_call(
        body, name="attn_b_fwd",
        out_shape=[SDS((T, 512), F32), SDS((T, 512), F32), SDS((4, 8, GRID_W, 512), F32)], grid=(4,),
        in_specs=[pl.BlockSpec((None, 16, HD), lambda h: (h, 0, 0)),
                  pl.BlockSpec((T, HD), lambda h: (0, NHA + h)),
                  pl.BlockSpec((T, HD), lambda h: (0, NH + NHA + h)),
                  pl.BlockSpec((T, HD), lambda h: (0, 2 * NH + NHA + h))],
        out_specs=[pl.BlockSpec((T, HD), lambda h: (0, h)), pl.BlockSpec((T, HD), lambda h: (0, h)),
                   pl.BlockSpec((None, 8, GRID_W, 512), lambda h: (h, 0, 0, 0))],
        scratch_shapes=[pltpu.VMEM((T, HD), BF16), pltpu.VMEM((14, GRID_W, HD), F32)],
        compiler_params=_params(1))(rpb_rows, qkn, qkn, proj)


def _comb_fwd(os, ls):
    tm = 512

    def body(o0, o1, o2, l0, l1, l2, oa_ref, w0, w1, w2):
        lv = [l0[...], l1[...], l2[...]]
        mx = jnp.maximum(jnp.maximum(lv[0], lv[1]), lv[2])
        ev = [jnp.exp(l - mx) for l in lv]
        den = ev[0] + ev[1] + ev[2]
        wv = [e / den for e in ev]
        oa_ref[...] = (wv[0] * o0[...] + wv[1] * o1[...] + wv[2] * o2[...]).astype(BF16)
        w0[...], w1[...], w2[...] = wv

    spec = pl.BlockSpec((tm, 512), lambda i: (i, 0))
    return pl.pallas_call(
        body, name="comb_fwd", out_shape=[SDS((T, 512), BF16)] + [SDS((T, 512), F32)] * 3, grid=(T // tm,),
        in_specs=[spec] * 6, out_specs=[spec] * 4, compiler_params=_params(1))(*os, *ls)


def _mix_fwd(oa, ob, proj, b_gate, wpa, wpb):
    tm = 512

    def body(oa_ref, ob_ref, ga_ref, gb_ref, ba_ref, bb_ref, wpa_ref, wpb_ref, mixed_ref, ob16_ref):
        oav = oa_ref[...]
        obv = ob_ref[...].astype(BF16)
        ob16_ref[...] = obv
        for s in range(NSH):
            sl = slice(s * 512, (s + 1) * 512)
            ga = _sigmoid(ga_ref[:, sl] + ba_ref[:, sl])
            gb = _sigmoid(gb_ref[:, sl] + bb_ref[:, sl])
            mixed_ref[:, sl] = (ga * _dot(oav, wpa_ref[s]) + gb * _dot(obv, wpb_ref[s])).astype(BF16)

    row = lambda w: pl.BlockSpec((tm, w), lambda i: (i, 0))
    return pl.pallas_call(
        body, name="mix_fwd", out_shape=[SDS((T, D), BF16), SDS((T, 512), BF16)], grid=(T // tm,),
        in_specs=[row(512), row(512),
                  pl.BlockSpec((tm, D), lambda i: (i, 3)), pl.BlockSpec((tm, D), lambda i: (i, 4)),
                  pl.BlockSpec((1, D), lambda i: (0, 0)), pl.BlockSpec((1, D), lambda i: (0, 1)),
                  _resident((NSH, 512, 512), lambda i: (0, 0, 0)), _resident((NSH, 512, 512), lambda i: (0, 0, 0))],
        out_specs=[row(D), row(512)], compiler_params=_params(1))(oa, ob, proj, proj, b_gate, b_gate, wpa, wpb)


def _out_proj_fwd(mixed, w_out, x, g):
    tm = 512

    def body(m_ref, w_ref, x_ref, g_ref, h1_ref, hn_ref):
        h1 = x_ref[...] + _dot(m_ref[...], w_ref[...])
        h1_ref[...] = h1
        r = lax.rsqrt(jnp.mean(h1 * h1, axis=-1, keepdims=True) + EPS)
        hn_ref[...] = (h1 * r * g_ref[...]).astype(BF16)

    row = pl.BlockSpec((tm, D), lambda i: (i, 0))
    return pl.pallas_call(
        body, name="out_proj_fwd", out_shape=[SDS((T, D), F32), SDS((T, D), BF16)], grid=(T // tm,),
        in_specs=[row, _resident((D, D), lambda i: (0, 0)), row, pl.BlockSpec((1, D), lambda i: (0, 0))],
        out_specs=[row, row], compiler_params=_params(1))(mixed, w_out, x, g)


def _ffn_up(hn, w_up):
    tm, tn = T, 512
    per = (DFF // NSH) // tn

    def body(h_ref, w_ref, a_ref, u_ref):
        uv = jnp.maximum(_dot(h_ref[...], w_ref[...]), 0.0)
        a_ref[...] = (uv * uv).astype(BF16)
        u_ref[...] = uv.astype(BF16)

    out = pl.BlockSpec((tm, tn), lambda i, j: (i, j))
    return pl.pallas_call(
        body, name="ffn_up", out_shape=[SDS((T, DFF), BF16)] * 2, grid=(T // tm, DFF // tn),
        in_specs=[pl.BlockSpec((tm, D), lambda i, j: (i, 0)),
                  pl.BlockSpec((None, D, tn), lambda i, j: (j // per, 0, j % per))],
        out_specs=[out, out], compiler_params=_params(2))(hn, w_up)


def _ffn_down_loss(u, w_down, h1, target):
    tm, tk = 512, 2048
    nk = DFF // tk

    def body(u_ref, w_ref, h1_ref, t_ref, dy_ref, dy16_ref, loss_ref, acc):
        k = pl.program_id(1)

        @pl.when(k == 0)
        def _():
            acc[...] = jnp.zeros_like(acc)

        acc[...] += _dot(u_ref[...], w_ref[...])

        @pl.when(k == nk - 1)
        def _():
            def chunk(r, sq):
                rows = pl.ds(pl.multiple_of(r * 16, 16), 16)
                err = acc[rows, :] + h1_ref[rows, :] - t_ref[rows, :]
                dy = err * (1.0 / D)
                dy_ref[rows, :] = dy
                dy16_ref[rows, :] = dy.astype(BF16)
                return sq + err * err

            sq = lax.fori_loop(0, tm // 16, chunk, jnp.zeros((16, D), F32), unroll=2)
            part = 0.5 * jnp.sum(jnp.mean(sq, axis=-1, keepdims=True), axis=0, keepdims=True)
            loss_ref[...] = jnp.broadcast_to(part, (8, 128))

    row = pl.BlockSpec((tm, D), lambda i, k: (i, 0))
    once = _resident((tm, D), lambda i, k: (i, 0))
    return pl.pallas_call(
        body, name="ffn_down_loss",
        out_shape=[SDS((T, D), F32), SDS((T, D), BF16), SDS((T // tm, 8, 128), F32)], grid=(T // tm, nk),
        in_specs=[pl.BlockSpec((tm, tk), lambda i, k: (i, k)), pl.BlockSpec((tk, D), lambda i, k: (k, 0)), once, once],
        out_specs=[row, row, pl.BlockSpec((None, 8, 128), lambda i, k: (i, 0, 0))],
        scratch_shapes=[pltpu.VMEM((tm, D), F32)], compiler_params=_params(2))(u, w_down, h1, target)


def _ffn_down_bwd(dy16, w_down, u, deps=()):
    tm, tn = T, 512

    def body(dy_ref, w_ref, u_ref, du_ref):
        uv = u_ref[...].astype(F32)
        du_ref[...] = jnp.where(uv > 0.0, 2.0 * uv * _dot_nt(dy_ref[...], w_ref[...]), 0.0).astype(BF16)

    return pl.pallas_call(
        _after(body, deps), name="ffn_down_bwd", out_shape=SDS((T, DFF), BF16), grid=(T // tm, DFF // tn),
        in_specs=[DEP_SPEC] * len(deps) + [
            pl.BlockSpec((tm, D), lambda i, j: (i, 0)), pl.BlockSpec((tn, D), lambda i, j: (j, 0)),
            pl.BlockSpec((tm, tn), lambda i, j: (i, j))],
        out_specs=pl.BlockSpec((tm, tn), lambda i, j: (i, j)), compiler_params=_params(2))(*deps, dy16, w_down, u)


def _norm_bwd(xv, dz_in, g):
    r = lax.rsqrt(jnp.mean(xv * xv, axis=-1, keepdims=True) + EPS)
    dg = jnp.sum(xv * r * dz_in, axis=0, keepdims=True)
    dz = dz_in * g
    dx = r * dz - xv * (r * r * r) * jnp.mean(xv * dz, axis=-1, keepdims=True)
    return dx, dg


def _norm_bwd_rows(x_ref, dz_ref, g_ref, n_rows, chunk, emit):
    gv = g_ref[...]

    def step(r, dgp):
        rows = pl.ds(pl.multiple_of(r * chunk, chunk), chunk)
        xv, dzi = x_ref[rows, :], dz_ref[rows, :]
        rs = lax.rsqrt(jnp.mean(xv * xv, axis=-1, keepdims=True) + EPS)
        dz = dzi * gv
        emit(rows, rs * dz - xv * (rs * rs * rs) * jnp.mean(xv * dz, axis=-1, keepdims=True))
        return dgp + xv * rs * dzi

    dgp = lax.fori_loop(0, n_rows // chunk, step, jnp.zeros((chunk, D), F32), unroll=2)
    return jnp.sum(dgp, axis=0, keepdims=True)


def _ffn_up_bwd(du, w_up, h1, dy, g, deps=()):
    tm, tk = 512, 2048
    per = (DFF // NSH) // tk
    nk = DFF // tk

    def body(du_ref, w_ref, h1_ref, dy_ref, g_ref, dh1_ref, dh16_ref, dg_ref, acc):
        i, k = pl.program_id(0), pl.program_id(1)

        @pl.when(k == 0)
        def _():
            acc[...] = jnp.zeros_like(acc)

        @pl.when((k == 0) & (i == 0))
        def _():
            dg_ref[...] = jnp.zeros_like(dg_ref)

        acc[...] += _dot_nt(du_ref[...], w_ref[...])

        @pl.when(k == nk - 1)
        def _():
            def emit(rows, dx):
                dh1 = dy_ref[rows, :] + dx
                dh1_ref[rows, :] = dh1
                dh16_ref[rows, :] = dh1.astype(BF16)

            dg_ref[...] += _norm_bwd_rows(h1_ref, acc, g_ref, tm, 16, emit)

    row = pl.BlockSpec((tm, D), lambda i, k: (i, 0))
    once = _resident((tm, D), lambda i, k: (i, 0))
    vec = pl.BlockSpec((1, D), lambda i, k: (0, 0))
    return pl.pallas_call(
        _after(body, deps), name="ffn_up_bwd", out_shape=[SDS((T, D), F32), SDS((T, D), BF16), SDS((1, D), F32)],
        grid=(T // tm, nk),
        in_specs=[DEP_SPEC] * len(deps) + [
            pl.BlockSpec((tm, tk), lambda i, k: (i, k)),
            pl.BlockSpec((None, D, tk), lambda i, k: (k // per, 0, k % per)), once, once, vec],
        out_specs=[row, row, vec], scratch_shapes=[pltpu.VMEM((tm, D), F32)],
        compiler_params=_params(2))(*deps, du, w_up, h1, dy, g)


def _mix_bwd(dh16, w_out, oa, ob16, proj, b_gate, wpa, wpb):
    tm = 256

    def body(dh_ref, wo_ref, oa_ref, ob_ref, ga_ref, gb_ref, ba_ref, bb_ref, wpa_ref, wpb_ref,
             dya_ref, dyb_ref, dga_ref, dgb_ref, doa_ref, dob_ref, dba_ref, dbb_ref):
        @pl.when(pl.program_id(0) == 0)
        def _():
            dba_ref[...] = jnp.zeros_like(dba_ref)
            dbb_ref[...] = jnp.zeros_like(dbb_ref)

        oav, obv = oa_ref[...], ob_ref[...]
        doa = jnp.zeros((tm, 512), F32)
        dob = jnp.zeros((tm, 512), F32)
        for s in range(NSH):
            sl = slice(s * 512, (s + 1) * 512)
            dm = _dot_nt(dh_ref[...], wo_ref[sl, :])
            ga = _sigmoid(ga_ref[:, sl] + ba_ref[:, sl])
            gb = _sigmoid(gb_ref[:, sl] + bb_ref[:, sl])
            dya = (dm * ga).astype(BF16)
            dyb = (dm * gb).astype(BF16)
            dza = dm * _dot(oav, wpa_ref[s]) * ga * (1.0 - ga)
            dzb = dm * _dot(obv, wpb_ref[s]) * gb * (1.0 - gb)
            dya_ref[:, sl], dyb_ref[:, sl] = dya, dyb
            dga_ref[:, sl], dgb_ref[:, sl] = dza.astype(BF16), dzb.astype(BF16)
            dba_ref[:, sl] += jnp.sum(dza, axis=0, keepdims=True)
            dbb_ref[:, sl] += jnp.sum(dzb, axis=0, keepdims=True)
            doa += _dot_nt(dya, wpa_ref[s])
            dob += _dot_nt(dyb, wpb_ref[s])
        doa_ref[...], dob_ref[...] = doa, dob

    row = lambda w: pl.BlockSpec((tm, w), lambda i: (i, 0))
    vec = pl.BlockSpec((1, D), lambda i: (0, 0))
    wp = _resident((NSH, 512, 512), lambda i: (0, 0, 0))
    return pl.pallas_call(
        body, name="mix_bwd",
        out_shape=[SDS((T, D), BF16)] * 4 + [SDS((T, 512), F32)] * 2 + [SDS((1, D), F32)] * 2, grid=(T // tm,),
        in_specs=[row(D), _resident((D, D), lambda i: (0, 0)), row(512), row(512),
                  pl.BlockSpec((tm, D), lambda i: (i, 3)), pl.BlockSpec((tm, D), lambda i: (i, 4)),
                  pl.BlockSpec((1, D), lambda i: (0, 0)), pl.BlockSpec((1, D), lambda i: (0, 1)), wp, wp],
        out_specs=[row(D)] * 4 + [row(512)] * 2 + [vec] * 2,
        compiler_params=_params(1))(dh16, w_out, oa, ob16, proj, proj, b_gate, b_gate, wpa, wpb)


def _comb_bwd(doa, os, ws, deps=()):
    tm = 512

    def body(d_ref, o0, o1, o2, w0, w1, w2, cc_ref):
        prod = d_ref[...] * (w0[...] * o0[...] + w1[...] * o1[...] + w2[...] * o2[...])
        for h in range(4):
            sl = slice(h * HD, (h + 1) * HD)
            cc_ref[:, sl] = jnp.broadcast_to(jnp.sum(prod[:, sl], axis=-1, keepdims=True), (tm, HD))

    spec = pl.BlockSpec((tm, 512), lambda i: (i, 0))
    return pl.pallas_call(
        _after(body, deps), name="comb_bwd", out_shape=SDS((T, 512), F32), grid=(T // tm,),
        in_specs=[DEP_SPEC] * len(deps) + [spec] * 7, out_specs=spec,
        compiler_params=_params(1))(*deps, doa, *os, *ws)


def _attn_a_bwd(qkn, proj, doa, lse, w, cc, g):
    dil = DILS[g]
    m = T // dil
    nb = m // 128
    hp = _heads_per_step(m)

    def body(q_ref, k_ref, v_ref, d_ref, l_ref, w_ref, c_ref, dqk_ref, dv_ref, kp, vp, dkp, dvp):
        for hh in range(hp):
            sl = slice(hh * HD, (hh + 1) * HD)
            _fill_padded(kp, k_ref[:, sl], m)
            _fill_padded(vp, v_ref[:, sl], m)
            dkp[...] = jnp.zeros_like(dkp)
            dvp[...] = jnp.zeros_like(dvp)

            def block(b, carry):
                q0 = pl.multiple_of(b * 128, 128)
                rows = pl.ds(q0, 128)
                win = pl.ds(q0, 256)
                qb, kw, vw = q_ref[rows, sl], kp[win, :], vp[win, :]
                s = _dot_nt(qb, kw) * SCALE
                s = jnp.where(_band_mask(q0, m), s, NEG)
                wp = _wide(w_ref[rows, sl], 2) * jnp.exp(s - _wide(l_ref[rows, sl], 2))
                dob = d_ref[rows, sl].astype(BF16)
                ds = (wp * (_dot_nt(dob, vw) - _wide(c_ref[rows, sl], 2))).astype(BF16)
                dqk_ref[0, rows, sl] = _dot(ds, kw) * SCALE
                dkp[win, :] += _dot_tn(ds, qb) * SCALE
                dvp[win, :] += _dot_tn(wp.astype(BF16), dob)
                return carry

            lax.fori_loop(0, nb, block, 0, unroll=min(nb, 2))
            dqk_ref[1, :, sl] = dkp[64:64 + m, :]
            dv_ref[:, sl] = dvp[64:64 + m, :]

    blk = pl.BlockSpec((m, hp * HD), lambda h, r: (0, r * (4 // hp) + h))
    view = lambda a: a.reshape(m, dil * 512)
    dqk, dv = pl.pallas_call(
        body, name=f"attn_a_bwd_{g}", out_shape=[SDS((2, m, dil * 512), F32), SDS((m, dil * 512), F32)],
        grid=(4 // hp, dil), in_specs=[blk] * 7,
        out_specs=[pl.BlockSpec((2, m, hp * HD), lambda h, r: (0, 0, r * (4 // hp) + h)), blk],
        scratch_shapes=[pltpu.VMEM((m + 128, HD), BF16), pltpu.VMEM((m + 128, HD), BF16),
                        pltpu.VMEM((m + 128, HD), F32), pltpu.VMEM((m + 128, HD), F32)],
        compiler_params=_params(2))(*_group_views(qkn, proj, g), view(doa), view(lse), view(w), view(cc))
    return dqk.reshape(2, T, 512), dv.reshape(T, 512)


def _attn_b_bwd(qkn, proj, dob, ob, lse, bias):
    def body(q_ref, k_ref, v_ref, d_ref, o_ref, l_ref, bias_ref, dqk_ref, dv_ref, drpb_ref, vb, dk_acc, dv_acc, a_acc):
        vb[...] = v_ref[...].astype(BF16)
        dk_acc[...] = jnp.zeros_like(dk_acc)
        dv_acc[...] = jnp.zeros_like(dv_acc)
        a_acc[...] = jnp.zeros_like(a_acc)

        def row(r, carry):
            start, off = _nbr_window(r)
            rows = pl.ds(pl.multiple_of(r * GRID_W, GRID_W), GRID_W)
            win = pl.ds(pl.multiple_of(start * GRID_W, GRID_W), 512)
            qr, kw, vw = q_ref[rows, :], k_ref[win, :], vb[win, :]
            s = _dot_nt(qr, kw) * SCALE + bias_ref[off]
            p = jnp.exp(s - _wide(l_ref[rows, :], 4))
            dov = d_ref[rows, :]
            delta = jnp.sum(dov * o_ref[rows, :], axis=-1, keepdims=True)
            do16 = dov.astype(BF16)
            ds = p * (_dot_nt(do16, vw) - delta)
            a_acc[off] += ds
            ds16 = ds.astype(BF16)
            dqk_ref[0, rows, :] = _dot(ds16, kw) * SCALE
            dk_acc[win, :] += _dot_tn(ds16, qr) * SCALE
            dv_acc[win, :] += _dot_tn(p.astype(BF16), do16)
            return carry

        lax.fori_loop(0, T // GRID_W, row, 0, unroll=2)
        dqk_ref[1] = dk_acc[...]
        dv_ref[...] = dv_acc[...]

        lane = lax.broadcasted_iota(jnp.int32, (16, HD), 1)
        rowi = lax.broadcasted_iota(jnp.int32, (16, HD), 0)
        low = (lane >= GRID_W - WIN_C) & (lane < GRID_W + WIN_C - 1)
        high = (lane >= HD - WIN_C) | (lane < WIN_C - 1)
        flip = (lax.broadcasted_iota(jnp.int32, (GRID_W, GRID_W), 0)
                + lax.broadcasted_iota(jnp.int32, (GRID_W, GRID_W), 1) == GRID_W - 1).astype(BF16)
        out = jnp.zeros((16, HD), F32)
        for d in range(14):
            acc = None
            for off in range(8):
                if 0 <= d - off <= 6 and (d - off) % 2 == 0:
                    jj = (d - off) // 2
                    piece = a_acc[off, :, jj * HD:(jj + 1) * HD]
                    acc = piece if acc is None else acc + piece
            hi = acc.astype(BF16)
            lo = (acc - hi.astype(F32)).astype(BF16)
            rev = _dot(flip, hi) + _dot(flip, lo)
            v = jnp.sum(pltpu.roll(rev, 0, 1, stride=1, stride_axis=0), axis=0, keepdims=True)
            v = jnp.broadcast_to(v, (16, HD))
            out = out + jnp.where((rowi == d) & low, v, 0.0)
            out = out + jnp.where(rowi == d + 1, pltpu.roll(jnp.where(high, v, 0.0), GRID_W, 1), 0.0)
        drpb_ref[...] = out

    blk = pl.BlockSpec((T, HD), lambda h: (0, h))
    return pl.pallas_call(
        body, name="attn_b_bwd",
        out_shape=[SDS((2, T, 512), F32), SDS((T, 512), F32), SDS((4, 16, HD), F32)], grid=(4,),
        in_specs=[pl.BlockSpec((T, HD), lambda h: (0, NHA + h)),
                  pl.BlockSpec((T, HD), lambda h: (0, NH + NHA + h)),
                  pl.BlockSpec((T, HD), lambda h: (0, 2 * NH + NHA + h)), blk, blk, blk,
                  pl.BlockSpec((None, 8, GRID_W, 512), lambda h: (h, 0, 0, 0))],
        out_specs=[pl.BlockSpec((2, T, HD), lambda h: (0, 0, h)), blk,
                   pl.BlockSpec((None, 16, HD), lambda h: (h, 0, 0))],
        scratch_shapes=[pltpu.VMEM((T, HD), BF16), pltpu.VMEM((T, HD), F32), pltpu.VMEM((T, HD), F32),
                        pltpu.VMEM((8, GRID_W, 512), F32)],
        compiler_params=_params(1))(qkn, qkn, proj, dob, ob, lse, bias)


def _qk_bwd(proj, nw, cos, sin, dqk_groups, dqk_b):
    tm = 256

    def body(p_ref, w_ref, cos_ref, sin_ref, d0, d1, d2, d3, o_ref, dn_ref):
        @pl.when(pl.program_id(1) == 0)
        def _():
            dn_ref[...] = jnp.zeros_like(dn_ref)

        cv, sv = cos_ref[...], sin_ref[...]
        srcs = (d0, d1, d2, d3)
        dna = jnp.zeros((1, HD), F32)
        dnb = jnp.zeros((1, HD), F32)
        for h in range(NH):
            sl = slice(h * HD, (h + 1) * HD)
            dz = srcs[h // 4][:, (h % 4) * HD:(h % 4 + 1) * HD]
            if h < NHA:
                dz = dz * cv + pltpu.roll(dz * sv, 64, 1)
            dx, dg = _norm_bwd(p_ref[:, sl], dz, w_ref[:, sl])
            o_ref[:, sl] = dx.astype(BF16)
            if h < NHA:
                dna += dg
            else:
                dnb += dg
        dn_ref[0:1, :] += dna
        dn_ref[1:2, :] += dnb

    dspec = pl.BlockSpec((None, tm, 512), lambda j, i: (j, i, 0))
    return pl.pallas_call(
        body, name="qk_bwd", out_shape=[SDS((T, 2 * D), BF16), SDS((2, 8, HD), F32)], grid=(2, T // tm),
        in_specs=[pl.BlockSpec((tm, D), lambda j, i: (i, j)),
                  pl.BlockSpec((None, 1, D), lambda j, i: (j, 0, 0)),
                  pl.BlockSpec((tm, HD), lambda j, i: (i, 0)),
                  pl.BlockSpec((tm, HD), lambda j, i: (i, 0)), dspec, dspec, dspec, dspec],
        out_specs=[pl.BlockSpec((tm, D), lambda j, i: (i, j)), pl.BlockSpec((None, 8, HD), lambda j, i: (j, 0, 0))],
        compiler_params=_params(2))(proj, nw, cos, sin, *dqk_groups, dqk_b)


def _in_proj_bwd(dproj, w_in, x, dh1, g, deps=()):
    tm, tk = 512, DIN // NSH
    nk = DIN // tk

    def body(dp_ref, w_ref, x_ref, dh_ref, g_ref, dx_ref, dg_ref, acc):
        i, k = pl.program_id(0), pl.program_id(1)

        @pl.when(k == 0)
        def _():
            acc[...] = jnp.zeros_like(acc)

        @pl.when((k == 0) & (i == 0))
        def _():
            dg_ref[...] = jnp.zeros_like(dg_ref)

        acc[...] += _dot_nt(dp_ref[...], w_ref[...])

        @pl.when(k == nk - 1)
        def _():
            def emit(rows, dx):
                dx_ref[rows, :] = dh_ref[rows, :] + dx

            dg_ref[...] += _norm_bwd_rows(x_ref, acc, g_ref, tm, 8, emit)

    row = pl.BlockSpec((tm, D), lambda i, k: (i, 0))
    once = _resident((tm, D), lambda i, k: (i, 0))
    vec = pl.BlockSpec((1, D), lambda i, k: (0, 0))
    return pl.pallas_call(
        _after(body, deps), name="in_proj_bwd", out_shape=[SDS((T, D), F32), SDS((1, D), F32)], grid=(T // tm, nk),
        in_specs=[DEP_SPEC] * len(deps) + [
            pl.BlockSpec((tm, tk), lambda i, k: (i, k)),
            pl.BlockSpec((None, D, tk), lambda i, k: (k, 0, 0)), once, once, vec],
        out_specs=[row, vec], scratch_shapes=[pltpu.VMEM((tm, D), F32)],
        compiler_params=_params(2))(*deps, dproj, w_in, x, dh1, g)


def _grad_w(name, a, g, shard_rows, rows, cols, tr, tc):
    ni, nj = rows // tr, cols // tc
    if shard_rows:
        a_map, g_map = (lambda s, i, j: (0, s * ni + i)), (lambda s, i, j: (0, j))
    else:
        a_map, g_map = (lambda s, i, j: (0, i)), (lambda s, i, j: (0, s * nj + j))

    def body(a_ref, g_ref, o_ref):
        o_ref[...] = _dot_tn(a_ref[...], g_ref[...]).astype(BF16)

    return pl.pallas_call(
        body, name=name, out_shape=SDS((NSH, rows, cols), BF16), grid=(NSH, ni, nj),
        in_specs=[pl.BlockSpec((T, tr), a_map), pl.BlockSpec((T, tc), g_map)],
        out_specs=pl.BlockSpec((None, tr, tc), lambda s, i, j: (s, i, j)), compiler_params=_params(3))(a, g)


def _adamw(w, g, m, v):
    m = B1 * m + (1.0 - B1) * g
    v = B2 * v + (1.0 - B2) * (g * g)
    m_hat = m / (1.0 - B1 ** STEP)
    v_hat = v / (1.0 - B2 ** STEP)
    delta = -LR * (m_hat / (jnp.sqrt(v_hat) + AEPS) + WD * w)
    return delta, m, v


def _sum_halves(name, place, grads, theirs):
    _, rows, cols = theirs.shape
    tr = _row_tile(rows, cols, 1 << 17)

    def body(place_ref, a_ref, b_ref, o_ref):
        o_ref[...] = (a_ref[...].astype(F32) + b_ref[...].astype(F32)).astype(BF16)

    spec = pl.BlockSpec((NSH, tr, cols), lambda i, p: (0, i, 0))
    return pl.pallas_call(
        body, name=name, out_shape=SDS(theirs.shape, BF16),
        grid_spec=pltpu.PrefetchScalarGridSpec(
            num_scalar_prefetch=1, grid=(rows // tr,),
            in_specs=[pl.BlockSpec((NSH, None, tr, cols), lambda i, p: (0, p[1], i, 0)), spec], out_specs=spec),
        compiler_params=_params(1))(place, grads, theirs)


def _sum_landed(name, place, part, landed):
    _, rows, cols = part.shape
    tr = _row_tile(rows, cols, 1 << 18)

    def body(place_ref, p_ref, l_ref, o_ref):
        o_ref[...] = ((p_ref[...].astype(F32) + l_ref[0].astype(F32)) + l_ref[1].astype(F32)) + l_ref[2].astype(F32)

    return pl.pallas_call(
        body, name=name, out_shape=SDS((2, rows, cols), F32),
        grid_spec=pltpu.PrefetchScalarGridSpec(
            num_scalar_prefetch=1, grid=(rows // tr,),
            in_specs=[pl.BlockSpec((None, tr, cols), lambda i, p: (p[0], i, 0)),
                      pl.BlockSpec((3, tr, cols), lambda i, p: (0, i, 0))],
            out_specs=pl.BlockSpec((None, tr, cols), lambda i, p: (p[1], i, 0))),
        compiler_params=_params(1))(place, part, landed)


def _adam_shard(name, g, w, m, v):
    rows, cols = w.shape
    tr = _row_tile(rows, cols, 1 << 18)

    def body(g_ref, w_ref, m_ref, v_ref, d_ref, nm_ref, nv_ref):
        d_ref[...], nm_ref[...], nv_ref[...] = _adamw(w_ref[...], g_ref[...], m_ref[...], v_ref[...])

    spec = pl.BlockSpec((tr, cols), lambda i: (i, 0))
    return pl.pallas_call(
        body, name=name, out_shape=[SDS((rows, cols), F32)] * 3, grid=(rows // tr,),
        in_specs=[spec] * 4, out_specs=[spec] * 3, compiler_params=_params(1))(g, w, m, v)


def _adam_small(gathered, w, m, v):
    def body(g_ref, w_ref, m_ref, v_ref, go_ref, d_ref, nm_ref, nv_ref):
        g = g_ref[0:SMALL_ROWS, :]
        for dev in range(1, 8):
            g = g + g_ref[dev * SMALL_ROWS:(dev + 1) * SMALL_ROWS, :]
        go_ref[...] = g
        d_ref[...], nm_ref[...], nv_ref[...] = _adamw(w_ref[...], g, m_ref[...], v_ref[...])

    return pl.pallas_call(body, name="adam_small", out_shape=[SDS((SMALL_ROWS, HD), F32)] * 4)(gathered, w, m, v)


SMALL = (("norm_mix", (1, D)), ("b_gate", (1, 2 * D)), ("q_norm_a", (1, HD)), ("k_norm_a", (1, HD)),
         ("q_norm_b", (1, HD)), ("k_norm_b", (1, HD)), ("rpb_b", (1, 4, 15, 31)), ("norm_ffn", (1, D)))


def _pack_small(vals):
    pieces = []
    for (name, shape), val in zip(SMALL, vals):
        flat = val.reshape(-1)
        pad = (-flat.shape[0]) % HD
        pieces.append(jnp.pad(flat, (0, pad)).reshape(-1, HD))
    packed = jnp.concatenate(pieces, axis=0)
    return jnp.pad(packed, ((0, SMALL_ROWS - packed.shape[0]), (0, 0)))


def _unpack_small(packed):
    out, row = [], 0
    for name, shape in SMALL:
        size = int(np.prod(shape))
        nrows = -(-size // HD)
        out.append(packed[row:row + nrows].reshape(-1)[:size].reshape(shape))
        row += nrows
    return out


def kernel(x, norm_mix, w_in, b_gate, q_norm_a, k_norm_a, q_norm_b, k_norm_b, rpb_b, w_proj_a, w_proj_b, w_out, norm_ffn, w_up, w_down, loss_target, m_norm_mix, m_w_in, m_b_gate, m_q_norm_a, m_k_norm_a, m_q_norm_b, m_k_norm_b, m_rpb_b, m_w_proj_a, m_w_proj_b, m_w_out, m_norm_ffn, m_w_up, m_w_down, v_norm_mix, v_w_in, v_b_gate, v_q_norm_a, v_k_norm_a, v_q_norm_b, v_k_norm_b, v_rpb_b, v_w_proj_a, v_w_proj_b, v_w_out, v_norm_ffn, v_w_up, v_w_down):
    big_names = ("w_in", "w_proj_a", "w_proj_b", "w_out", "w_up", "w_down")
    big_w = [a[0] for a in (w_in, w_proj_a, w_proj_b, w_out, w_up, w_down)]
    big_m = [a[0] for a in (m_w_in, m_w_proj_a, m_w_proj_b, m_w_out, m_w_up, m_w_down)]
    big_v = [a[0] for a in (v_w_in, v_w_proj_a, v_w_proj_b, v_w_out, v_w_up, v_w_down)]
    x2, target = x[0], loss_target[0]

    place = jnp.stack([2 * lax.axis_index("x") + lax.axis_index("y"), lax.axis_index("c")]).astype(jnp.int32)
    groups = ((0,), (1, 2, 3), (4,), (5,))
    started = []
    for j, grp in enumerate(groups):
        deps = (started[0][4],) if j else ()
        placed = [_cast_into_place(big_w[i], "cast_" + big_names[i], place, deps) for i in grp]
        started.append(_gather_start(f"gather_start_{j}", placed))

    def gathered(j, after):
        send, recv, _, fulls, _ = started[j]
        fulls = _gather_wait(f"gather_wait_{j}", send, recv, fulls, after)
        fulls = _gather_finish(f"gather_finish_{j}", fulls)
        return [f.reshape(NSH, 2 * f.shape[2], f.shape[3]) for f in fulls]

    def reduce_begin(j, grads):
        grads = [g.reshape(NSH, 2, g.shape[1] // 2, g.shape[2]) for g in grads]
        theirs = _reduce_exchange(f"reduce_exchange_{j}", grads)
        parts = [_sum_halves(f"sum_halves_{j}_{i}", place, a, b) for i, (a, b) in enumerate(zip(grads, theirs))]
        send, recv, parts, lands, token = _reduce_start(f"reduce_start_{j}", parts)
        return (send, recv, parts, lands), token

    big_out = {}

    def reduce_end(j, state, after):
        send, recv, parts, lands = state
        parts, lands = _reduce_wait(f"reduce_wait_{j}", send, recv, parts, lands, after)
        sums = [_sum_landed(f"sum_landed_{j}_{i}", place, p, l) for i, (p, l) in enumerate(zip(parts, lands))]
        for idx, g in zip(groups[j], _reduce_share(f"reduce_share_{j}", sums)):
            g = g.reshape(big_w[idx].shape)
            big_out[idx] =(g, *_adam_shard("adam_" + big_names[idx], g, big_w[idx], big_m[idx], big_v[idx]))
        return big_out[groups[j][-1]][1]

    (win_f,) = gathered(0, tuple(s[4] for s in started[1:]))
    proj, xn = _norm_in_proj(x2, norm_mix, win_f)
    cos, sin = _rope_tables()
    nw = jnp.stack([jnp.concatenate([jnp.tile(q_norm_a, (1, NHA)), jnp.tile(q_norm_b, (1, NH - NHA))], axis=1),
                    jnp.concatenate([jnp.tile(k_norm_a, (1, NHA)), jnp.tile(k_norm_b, (1, NH - NHA))], axis=1)])
    qkn = _qk_prep(proj, nw, cos, sin)
    wpa_f, wpb_f, wout_f = gathered(1, (qkn,))
    wout_f = wout_f.reshape(D, D)
    fwd_a = [_attn_a_fwd(qkn, proj, g) for g in range(3)]
    os, ls = [f[0] for f in fwd_a], [f[1] for f in fwd_a]
    ob, lse_b, bias = _attn_b_fwd(qkn, proj, _rpb_rows(rpb_b[0]))
    oa, w0, w1, w2 = _comb_fwd(os, ls)
    ws = [w0, w1, w2]
    mixed, ob16 = _mix_fwd(oa, ob, proj, b_gate, wpa_f, wpb_f)
    h1, hn = _out_proj_fwd(mixed, wout_f, x2, norm_ffn)
    (wup_f,) = gathered(2, (h1,))
    usq, u = _ffn_up(hn, wup_f)
    (wdown_f,) = gathered(3, (u,))
    wdown_f = wdown_f.reshape(DFF, D)
    dy, dy16, loss_parts = _ffn_down_loss(usq, wdown_f, h1, target)
    loss = lax.psum(jnp.sum(loss_parts[:, 0, 0]), ("x", "y", "c"))

    g_down = _grad_w("grad_w_down", usq, dy16, True, DFF // NSH, D, 1024, 1024)
    red_down, token = reduce_begin(3, [g_down])
    du = _ffn_down_bwd(dy16, wdown_f, u, deps=(token,))
    g_up = _grad_w("grad_w_up", hn, du, False, D, DFF // NSH, 1024, 1024)
    red_up, token = reduce_begin(2, [g_up])
    dh1, dh16, d_norm_ffn = _ffn_up_bwd(du, wup_f, h1, dy, norm_ffn, deps=(token,))
    dya, dyb, dga, dgb, doa, dob, dba, dbb = _mix_bwd(dh16, wout_f, oa, ob16, proj, b_gate, wpa_f, wpb_f)
    g_out = _grad_w("grad_w_out", mixed, dh16, True, D // NSH, D, 512, 1024)
    g_pa = _grad_w("grad_w_proj_a", oa, dya, False, 512, 512, 512, 512)
    g_pb = _grad_w("grad_w_proj_b", ob16, dyb, False, 512, 512, 512, 512)
    red_mid, token = reduce_begin(1, [g_pa, g_pb, g_out])
    cc = _comb_bwd(doa, os, ws, deps=(token,))
    bwd_a = [_attn_a_bwd(qkn, proj, doa, ls[g], ws[g], cc, g) for g in range(3)]
    dqk_b, dv_b, drpb_t = _attn_b_bwd(qkn, proj, dob, ob, lse_b, bias)
    dqk_pre, dn = _qk_bwd(proj, nw, cos, sin, [b[0] for b in bwd_a], dqk_b)
    dv16 = jnp.concatenate([b[1] for b in bwd_a] + [dv_b], axis=1).astype(BF16)
    dproj = jnp.concatenate([dqk_pre, dv16, dga, dgb], axis=1)
    g_in = _grad_w("grad_w_in", xn, dproj, False, D, DIN // NSH, 1024, 1280)
    red_in, token = reduce_begin(0, [g_in])
    grad_x, d_norm_mix = _in_proj_bwd(dproj, win_f, x2, dh1, norm_mix, deps=(token,))

    done = reduce_end(3, red_down, (grad_x,))
    done = reduce_end(2, red_up, (done,))
    done = reduce_end(1, red_mid, (done,))
    done = reduce_end(0, red_in, (done,))

    d_rpb = drpb_t[:, :15, GRID_W - WIN_C:GRID_W + WIN_C - 1]
    small_g = [d_norm_mix, jnp.concatenate([dba, dbb], axis=1), dn[0, 0], dn[1, 0], dn[0, 1], dn[1, 1], d_rpb, d_norm_ffn]
    gathered_small = _allgather_small(_pack_small(small_g), done)
    small_w = (norm_mix, b_gate, q_norm_a, k_norm_a, q_norm_b, k_norm_b, rpb_b, norm_ffn)
    small_m = (m_norm_mix, m_b_gate, m_q_norm_a, m_k_norm_a, m_q_norm_b, m_k_norm_b, m_rpb_b, m_norm_ffn)
    small_v = (v_norm_mix, v_b_gate, v_q_norm_a, v_k_norm_a, v_q_norm_b, v_k_norm_b, v_rpb_b, v_norm_ffn)
    small_out = [_unpack_small(p) for p in
                 _adam_small(gathered_small, _pack_small(small_w), _pack_small(small_m), _pack_small(small_v))]

    order = ("norm_mix", "w_in", "b_gate", "q_norm_a", "k_norm_a", "q_norm_b", "k_norm_b", "rpb_b",
             "w_proj_a", "w_proj_b", "w_out", "norm_ffn", "w_up", "w_down")
    small_idx = {name: i for i, (name, _) in enumerate(SMALL)}
    outs = []
    for kind in range(4):
        for name in order:
            if name in small_idx:
                outs.append(small_out[kind][small_idx[name]])
            else:
                outs.append(big_out[big_names.index(name)][kind][None])
    return (loss, grad_x[None], *outs)
```

```python
import functools

import numpy as np
import jax
import jax.numpy as jnp
from jax import lax
from jax.experimental import pallas as pl
from jax.experimental.pallas import tpu as pltpu

F32, BF16 = jnp.float32, jnp.bfloat16
SDS = jax.ShapeDtypeStruct
MESH = pl.DeviceIdType.MESH

T = 2048
D = 2048
HD = 128
NH, NHA = 16, 12
DIN = 10240
DFF = 8192
NSH = 4
DILS = (1, 4, 16)
EPS = 1e-6
NEG = -1e30
SCALE = HD ** -0.5
GRID_W, WIN_R, WIN_C = 64, 8, 16
VMEM_LIMIT = 56 * 1024 * 1024
B1, B2, LR, AEPS, WD, STEP = 0.9, 0.999, 0.001, 1e-08, 0.01, 10
SMALL_ROWS = 88


def _dot(a, b):
    return jnp.dot(a, b, preferred_element_type=F32)


def _dot_nt(a, b):
    return lax.dot_general(a, b, (((1,), (1,)), ((), ())), preferred_element_type=F32)


def _dot_tn(a, b):
    return lax.dot_general(a, b, (((0,), (0,)), ((), ())), preferred_element_type=F32)


def _params(n):
    return pltpu.CompilerParams(dimension_semantics=("arbitrary",) * n, vmem_limit_bytes=VMEM_LIMIT)


def _resident(shape, index_map):
    return pl.BlockSpec(shape, index_map, pipeline_mode=pl.Buffered(1))


def _sigmoid(z):
    return 1.0 / (1.0 + jnp.exp(-z))


def _wide(v, n):
    return jnp.concatenate([v] * n, axis=1)


def _row_tile(rows, cols, elems):
    tr = 16
    while tr * 2 <= rows and tr * 2 * cols <= elems:
        tr *= 2
    return tr


def _place():
    x, y, c = lax.axis_index("x"), lax.axis_index("y"), lax.axis_index("c")
    peers = [(1 - x, y), (x, 1 - y), (1 - x, 1 - y)]
    return x, y, c, peers


def _cast_into_place(w, name, place, deps=()):
    rows, cols = w.shape
    hr = rows // 2
    tr = min(hr, 256)
    per = hr // tr

    def body(*refs):
        w_ref, o_ref = refs[-2:]
        o_ref[...] = w_ref[...].astype(BF16)

    return pl.pallas_call(
        body, name=name, out_shape=SDS((NSH, 2, hr, cols), BF16),
        grid_spec=pltpu.PrefetchScalarGridSpec(
            num_scalar_prefetch=1, grid=(2, per),
            in_specs=[DEP_SPEC] * len(deps) + [pl.BlockSpec((tr, cols), lambda h, i, p: (h * per + i, 0))],
            out_specs=pl.BlockSpec((None, None, tr, cols), lambda h, i, p: (p[0], h, i, 0))),
        compiler_params=_params(2))(place, *deps, w)


ANY_SPEC = pl.BlockSpec(memory_space=pl.ANY)
HBM_SPEC = pl.BlockSpec(memory_space=pltpu.HBM)
SEM_SPEC = pl.BlockSpec(memory_space=pltpu.SEMAPHORE)
DEP_SPEC = pl.BlockSpec((8, 128), lambda *_: (0, 0))
EFFECT = pltpu.SideEffectType.DATAFLOW_SIDE_EFFECTING


def _after(body, deps):
    n = len(deps)
    return (lambda *refs: body(*refs[n:])) if n else body


def _split_start(name, srcs, lands, n_copies, issue):
    n, m = len(srcs), len(lands)

    def body(*refs):
        issue(refs[:n], refs[n:n + m], refs[n + m], refs[n + m + 1])
        refs[-1][...] = jnp.zeros((8, 128), F32)

    arrays = list(srcs) + list(lands)
    outs = pl.pallas_call(
        body, name=name,
        out_shape=(pltpu.SemaphoreType.DMA((n_copies,)), pltpu.SemaphoreType.DMA((n_copies,)),
                   *[pltpu.HBM(a.shape, a.dtype) for a in arrays], SDS((8, 128), F32)),
        in_specs=[HBM_SPEC] * (n + m),
        out_specs=(SEM_SPEC, SEM_SPEC, *[HBM_SPEC] * (n + m), pl.BlockSpec(memory_space=pltpu.VMEM)),
        input_output_aliases={i: 2 + i for i in range(n + m)},
        compiler_params=pltpu.CompilerParams(has_side_effects=EFFECT),
    )(*[pltpu.with_memory_space_constraint(a, pltpu.HBM) for a in arrays])
    return outs[0], outs[1], list(outs[2:2 + n]), list(outs[2 + n:2 + n + m]), outs[-1]


def _split_wait(name, send_sems, recv_sems, srcs, lands, after, wait):
    n, m = len(srcs), len(lands)

    def body(*refs):
        wait(refs[:n], refs[n:n + m], refs[n + m], refs[n + m + 1])

    arrays = list(srcs) + list(lands)
    outs = pl.pallas_call(
        body, name=name, out_shape=[pltpu.HBM(a.shape, a.dtype) for a in arrays],
        in_specs=[HBM_SPEC] * (n + m) + [SEM_SPEC, SEM_SPEC] + [ANY_SPEC] * len(after),
        out_specs=[HBM_SPEC] * (n + m), input_output_aliases={i: i for i in range(n + m)},
        compiler_params=pltpu.CompilerParams(has_side_effects=EFFECT),
    )(*arrays, send_sems, recv_sems, *after)
    return list(outs[:n]), list(outs[n:])


def _gather_start(name, fulls):
    def issue(srcs, dsts, send_sems, recv_sems):
        x, y, c, peers = _place()
        for i in range(len(fulls)):
            mine = dsts[i].at[2 * x + y, c]
            for k, (px, py) in enumerate(peers):
                pltpu.make_async_remote_copy(
                    src_ref=mine, dst_ref=mine, send_sem=send_sems.at[3 * i + k],
                    recv_sem=recv_sems.at[3 * i + k], device_id=(px, py, c), device_id_type=MESH).start()

    return _split_start(name, [], fulls, 3 * len(fulls), issue)


def _gather_wait(name, send_sems, recv_sems, fulls, after):
    def wait(srcs, dsts, send_sems, recv_sems):
        x, y, c, peers = _place()
        for i in range(len(fulls)):
            for k, (px, py) in enumerate(peers):
                cp = pltpu.make_async_remote_copy(
                    src_ref=dsts[i].at[2 * x + y, c], dst_ref=dsts[i].at[2 * px + py, c],
                    send_sem=send_sems.at[3 * i + k], recv_sem=recv_sems.at[3 * i + k],
                    device_id=(px, py, c), device_id_type=MESH)
                cp.wait_send()
                cp.wait_recv()

    return _split_wait(name, send_sems, recv_sems, [], fulls, after, wait)[1]


def _gather_finish(name, fulls):
    n = len(fulls)

    def body(*refs):
        fin, fout = refs[:n], refs[n:2 * n]
        send_sems, recv_sems = refs[2 * n:]
        x, y, c, peers = _place()

        def copy(i, k, half):
            px, py = peers[k]
            return pltpu.make_async_remote_copy(
                src_ref=fin[i].at[2 * px + py, half], dst_ref=fout[i].at[2 * px + py, half],
                send_sem=send_sems.at[3 * i + k], recv_sem=recv_sems.at[3 * i + k],
                device_id=(x, y, 1 - c), device_id_type=MESH)

        sends = [copy(i, k, c) for i in range(n) for k in range(3)]
        for cp in sends:
            cp.start()
        for i in range(n):
            for k in range(3):
                copy(i, k, 1 - c).wait_recv()
        for cp in sends:
            cp.wait_send()

    return pl.pallas_call(
        body, name=name, out_shape=[SDS(f.shape, f.dtype) for f in fulls],
        in_specs=[ANY_SPEC] * n, out_specs=[ANY_SPEC] * n, input_output_aliases={i: i for i in range(n)},
        scratch_shapes=[pltpu.SemaphoreType.DMA((3 * n,)), pltpu.SemaphoreType.DMA((3 * n,))])(*fulls)


def _reduce_exchange(name, grads):
    n = len(grads)

    def body(*refs):
        ins, theirs = refs[:n], refs[n:2 * n]
        send_sems, recv_sems = refs[2 * n:]
        x, y, c, _ = _place()
        copies = []
        for i in range(n):
            cp = pltpu.make_async_remote_copy(
                src_ref=ins[i].at[:, 1 - c], dst_ref=theirs[i], send_sem=send_sems.at[i],
                recv_sem=recv_sems.at[i], device_id=(x, y, 1 - c), device_id_type=MESH)
            cp.start()
            copies.append(cp)
        for cp in copies:
            cp.wait_recv()
            cp.wait_send()

    return pl.pallas_call(
        body, name=name, out_shape=[SDS((NSH,) + g.shape[2:], g.dtype) for g in grads],
        in_specs=[ANY_SPEC] * n, out_specs=[ANY_SPEC] * n,
        scratch_shapes=[pltpu.SemaphoreType.DMA((n,)), pltpu.SemaphoreType.DMA((n,))])(*grads)


def _reduce_start(name, parts):
    lands = [lax.empty((3,) + p.shape[1:], p.dtype) for p in parts]

    def issue(srcs, dsts, send_sems, recv_sems):
        x, y, c, peers = _place()
        for i in range(len(parts)):
            for k, (px, py) in enumerate(peers):
                pltpu.make_async_remote_copy(
                    src_ref=srcs[i].at[2 * px + py], dst_ref=dsts[i].at[k], send_sem=send_sems.at[3 * i + k],
                    recv_sem=recv_sems.at[3 * i + k], device_id=(px, py, c), device_id_type=MESH).start()

    return _split_start(name, parts, lands, 3 * len(parts), issue)


def _reduce_wait(name, send_sems, recv_sems, parts, lands, after):
    def wait(srcs, dsts, send_sems, recv_sems):
        x, y, c, peers = _place()
        for i in range(len(parts)):
            for k, (px, py) in enumerate(peers):
                cp = pltpu.make_async_remote_copy(
                    src_ref=srcs[i].at[2 * px + py], dst_ref=dsts[i].at[k], send_sem=send_sems.at[3 * i + k],
                    recv_sem=recv_sems.at[3 * i + k], device_id=(px, py, c), device_id_type=MESH)
                cp.wait_send()
                cp.wait_recv()

    return _split_wait(name, send_sems, recv_sems, parts, lands, after, wait)


def _sibling_copy(src, dst, send_sems, recv_sems, k):
    x, y, c, _ = _place()
    return pltpu.make_async_remote_copy(src_ref=src, dst_ref=dst, send_sem=send_sems.at[k], recv_sem=recv_sems.at[k],
                                        device_id=(x, y, 1 - c), device_id_type=MESH)


def _forward_start(name, fulls):
    def issue(srcs, dsts, send_sems, recv_sems):
        x, y, c, peers = _place()
        for i in range(len(fulls)):
            for k, (px, py) in enumerate(peers):
                part = dsts[i].at[2 * px + py, c]
                _sibling_copy(part, part, send_sems, recv_sems, 3 * i + k).start()

    return _split_start(name, [], fulls, 3 * len(fulls), issue)


def _forward_wait(name, send_sems, recv_sems, fulls, after):
    def wait(srcs, dsts, send_sems, recv_sems):
        x, y, c, peers = _place()
        for i in range(len(fulls)):
            for k, (px, py) in enumerate(peers):
                cp = _sibling_copy(dsts[i].at[2 * px + py, c], dsts[i].at[2 * px + py, 1 - c], send_sems, recv_sems, 3 * i + k)
                cp.wait_send()
                cp.wait_recv()

    return _split_wait(name, send_sems, recv_sems, [], fulls, after, wait)[1]


def _exchange_start(name, grads):
    lands = [lax.empty((NSH,) + g.shape[2:], g.dtype) for g in grads]

    def issue(srcs, dsts, send_sems, recv_sems):
        c = lax.axis_index("c")
        for i in range(len(grads)):
            _sibling_copy(srcs[i].at[:, 1 - c], dsts[i], send_sems, recv_sems, i).start()

    return _split_start(name, grads, lands, len(grads), issue)


def _exchange_wait(name, send_sems, recv_sems, grads, lands, after):
    def wait(srcs, dsts, send_sems, recv_sems):
        c = lax.axis_index("c")
        for i in range(len(grads)):
            cp = _sibling_copy(srcs[i].at[:, 1 - c], dsts[i], send_sems, recv_sems, i)
            cp.wait_send()
            cp.wait_recv()

    return _split_wait(name, send_sems, recv_sems, grads, lands, after, wait)


def _share_start(name, sums):
    def issue(srcs, dsts, send_sems, recv_sems):
        c = lax.axis_index("c")
        for i in range(len(sums)):
            _sibling_copy(dsts[i].at[c], dsts[i].at[c], send_sems, recv_sems, i).start()

    return _split_start(name, [], sums, len(sums), issue)


def _share_wait(name, send_sems, recv_sems, sums, after):
    def wait(srcs, dsts, send_sems, recv_sems):
        c = lax.axis_index("c")
        for i in range(len(sums)):
            cp = _sibling_copy(dsts[i].at[c], dsts[i].at[1 - c], send_sems, recv_sems, i)
            cp.wait_send()
            cp.wait_recv()

    return _split_wait(name, send_sems, recv_sems, [], sums, after, wait)[1]


def _allgather_small(v, after):
    m_per, n = v.shape

    def body(x_ref, after_ref, out_ref, send_sems, recv_sems, local_sem):
        x, y, c = lax.axis_index("x"), lax.axis_index("y"), lax.axis_index("c")
        me, sibling = (x, y, c), (x, y, 1 - c)
        chips = [(1 - x, y), (x, 1 - y), (1 - x, 1 - y)]

        def rows(px, py, pc):
            return out_ref.at[pl.ds((4 * px + 2 * py + pc) * m_per, m_per), :]

        def copy(k, block, to, src=None):
            return pltpu.make_async_remote_copy(
                src_ref=rows(*block) if src is None else src, dst_ref=rows(*block),
                send_sem=send_sems.at[k], recv_sem=recv_sems.at[k], device_id=to, device_id_type=MESH)

        mine = pltpu.make_async_copy(x_ref, rows(*me), local_sem)
        mine.start()
        first = [copy(0, me, sibling, src=x_ref)]
        first += [copy(1 + j, me, (*chip, c), src=x_ref) for j, chip in enumerate(chips)]
        for cp in first:
            cp.start()
        passed = [copy(4 + j, (*chip, c), sibling) for j, chip in enumerate(chips)]
        for j, chip in enumerate(chips):
            copy(1 + j, (*chip, c), me).wait_recv()
            passed[j].start()
        copy(0, sibling, me).wait_recv()
        for j, chip in enumerate(chips):
            copy(4 + j, (*chip, 1 - c), me).wait_recv()
        for cp in first + passed:
            cp.wait_send()
        mine.wait()

    return pl.pallas_call(
        body, name="allgather_small", out_shape=SDS((8 * m_per, n), v.dtype),
        in_specs=[pl.BlockSpec(memory_space=pltpu.VMEM), ANY_SPEC], out_specs=pl.BlockSpec(memory_space=pltpu.VMEM),
        scratch_shapes=[pltpu.SemaphoreType.DMA((7,)), pltpu.SemaphoreType.DMA((7,)), pltpu.SemaphoreType.DMA])(v, after)


def _norm_in_proj(x, g, w_full):
    tn, chunk = 512, 256
    per = (DIN // NSH) // tn

    def body(x_ref, g_ref, w_ref, proj_ref, xn_ref):
        @pl.when(pl.program_id(0) == 0)
        def _():
            def norm(r, carry):
                rows = pl.ds(pl.multiple_of(r * chunk, chunk), chunk)
                xv = x_ref[rows, :]
                rs = lax.rsqrt(jnp.mean(xv * xv, axis=-1, keepdims=True) + EPS)
                xn_ref[rows, :] = (xv * rs * g_ref[...]).astype(BF16)
                return carry

            lax.fori_loop(0, T // chunk, norm, 0)

        proj_ref[...] = _dot(xn_ref[...], w_ref[...])

    return pl.pallas_call(
        body, name="norm_in_proj", out_shape=[SDS((T, DIN), F32), SDS((T, D), BF16)], grid=(DIN // tn,),
        in_specs=[_resident((T, D), lambda j: (0, 0)),
                  pl.BlockSpec((1, D), lambda j: (0, 0)),
                  pl.BlockSpec((None, D, tn), lambda j: (j // per, 0, j % per))],
        out_specs=[pl.BlockSpec((T, tn), lambda j: (0, j)),
                   pl.BlockSpec((T, D), lambda j: (0, 0))],
        compiler_params=_params(1))(x, g, w_full)


def _rope_tables():
    pos = np.arange(T, dtype=np.float32)
    inv = (10000.0 ** (-np.arange(0, HD, 2, dtype=np.float32) / HD)).astype(np.float32)
    ang = (pos[:, None] * inv[None, :]).astype(np.float32)
    cos, sin = np.cos(ang).astype(np.float32), np.sin(ang).astype(np.float32)
    return (jnp.asarray(np.concatenate([cos, cos], axis=1)), jnp.asarray(np.concatenate([-sin, sin], axis=1)))


def _qk_prep(proj, nw, cos, sin):
    tm = 256

    def body(p_ref, w_ref, cos_ref, sin_ref, o_ref):
        cv, sv = cos_ref[...], sin_ref[...]
        for h in range(NH):
            sl = slice(h * HD, (h + 1) * HD)
            xv = p_ref[:, sl]
            r = lax.rsqrt(jnp.mean(xv * xv, axis=-1, keepdims=True) + EPS)
            z = xv * r * w_ref[:, sl]
            if h < NHA:
                z = z * cv + pltpu.roll(z, 64, 1) * sv
            o_ref[:, sl] = z.astype(BF16)

    return pl.pallas_call(
        body, name="qk_prep", out_shape=SDS((T, 2 * D), BF16), grid=(T // tm, 2),
        in_specs=[pl.BlockSpec((tm, D), lambda i, j: (i, j)),
                  pl.BlockSpec((None, 1, D), lambda i, j: (j, 0, 0)),
                  pl.BlockSpec((tm, HD), lambda i, j: (i, 0)),
                  pl.BlockSpec((tm, HD), lambda i, j: (i, 0))],
        out_specs=pl.BlockSpec((tm, D), lambda i, j: (i, j)),
        compiler_params=_params(2))(proj, nw, cos, sin)


def _band_mask(q0, m):
    ii = lax.broadcasted_iota(jnp.int32, (128, 256), 0)
    jj = lax.broadcasted_iota(jnp.int32, (128, 256), 1)
    rel = jj - ii
    kpos = jj + (q0 - 64)
    return (rel >= 0) & (rel <= 128) & (kpos >= 0) & (kpos < m)


def _fill_padded(dst, src, m):
    zeros = jnp.zeros((64, HD), dst.dtype)
    dst[0:64, :] = zeros
    dst[64 + m:128 + m, :] = zeros
    dst[64:64 + m, :] = src.astype(dst.dtype)


def _group_views(qkn, proj, g):
    m = T // DILS[g]
    cols = (qkn[:, g * 512:(g + 1) * 512], qkn[:, D + g * 512:D + (g + 1) * 512],
            proj[:, 2 * D + g * 512:2 * D + (g + 1) * 512])
    return [a.reshape(m, DILS[g] * 512) for a in cols]


def _heads_per_step(m):
    return 4 if m <= 512 else 1


def _attn_a_fwd(qkn, proj, g):
    dil = DILS[g]
    m = T // dil
    nb = m // 128
    hp = _heads_per_step(m)

    def body(q_ref, k_ref, v_ref, o_ref, l_ref, kp, vp):
        for hh in range(hp):
            sl = slice(hh * HD, (hh + 1) * HD)
            _fill_padded(kp, k_ref[:, sl], m)
            _fill_padded(vp, v_ref[:, sl], m)

            def block(b, carry):
                q0 = pl.multiple_of(b * 128, 128)
                kw, vw = kp[pl.ds(q0, 256), :], vp[pl.ds(q0, 256), :]
                s = _dot_nt(q_ref[pl.ds(q0, 128), sl], kw) * SCALE
                s = jnp.where(_band_mask(q0, m), s, NEG)
                mx = jnp.max(s, axis=-1, keepdims=True)
                e = jnp.exp(s - mx)
                den = jnp.sum(e, axis=-1, keepdims=True)
                o_ref[pl.ds(q0, 128), sl] = _dot((e / den).astype(BF16), vw)
                l_ref[pl.ds(q0, 128), sl] = jnp.broadcast_to(mx + jnp.log(den), (128, HD))
                return carry

            lax.fori_loop(0, nb, block, 0, unroll=min(nb, 2))

    blk = pl.BlockSpec((m, hp * HD), lambda h, r: (0, r * (4 // hp) + h))
    o, lse = pl.pallas_call(
        body, name=f"attn_a_fwd_{g}", out_shape=[SDS((m, dil * 512), F32)] * 2, grid=(4 // hp, dil),
        in_specs=[blk] * 3, out_specs=[blk] * 2,
        scratch_shapes=[pltpu.VMEM((m + 128, HD), BF16), pltpu.VMEM((m + 128, HD), BF16)],
        compiler_params=_params(2))(*_group_views(qkn, proj, g))
    return o.reshape(T, 512), lse.reshape(T, 512)


def _nbr_window(r):
    start = jnp.clip(r - WIN_R // 2, 0, T // GRID_W - WIN_R)
    return start, start - r + (WIN_R - 1)


def _rpb_rows(rpb):
    zeros = jnp.zeros((4, 14, 33), F32)
    a, b = rpb[:, :14], rpb[:, 1:15]
    rows = jnp.concatenate([a[:, :, 15:31], zeros, b, zeros, a[:, :, 0:15]], axis=2)
    return jnp.pad(rows, ((0, 0), (0, 2), (0, 0)))


def _attn_b_fwd(qkn, proj, rpb_rows):
    def body(r_ref, q_ref, k_ref, v_ref, o_ref, l_ref, bias_ref, vb, pair):
        qc = lax.broadcasted_iota(jnp.int32, (GRID_W, 512), 0)
        kc = lax.broadcasted_iota(jnp.int32, (GRID_W, 512), 1) & (GRID_W - 1)
        cs = jnp.clip(qc - WIN_C // 2, 0, GRID_W - WIN_C)
        colmask = (kc >= cs) & (kc < cs + WIN_C)
        for d in range(14):
            pair[d] = pltpu.roll(jnp.broadcast_to(r_ref[d:d + 1, :], (GRID_W, HD)), 0, 1, stride=1, stride_axis=0)
        for off in range(8):
            rows = jnp.concatenate([pair[off + 2 * jj] for jj in range(4)], axis=1)
            bias_ref[off] = jnp.where(colmask, rows, NEG)
        vb[...] = v_ref[...].astype(BF16)

        def row(r, carry):
            start, off = _nbr_window(r)
            q0 = pl.multiple_of(r * GRID_W, GRID_W)
            k0 = pl.multiple_of(start * GRID_W, GRID_W)
            s = _dot_nt(q_ref[pl.ds(q0, GRID_W), :], k_ref[pl.ds(k0, 512), :]) * SCALE + bias_ref[off]
            mx = jnp.max(s, axis=-1, keepdims=True)
            e = jnp.exp(s - mx)
            den = jnp.sum(e, axis=-1, keepdims=True)
            o_ref[pl.ds(q0, GRID_W), :] = _dot((e / den).astype(BF16), vb[pl.ds(k0, 512), :])
            l_ref[pl.ds(q0, GRID_W), :] = jnp.broadcast_to(mx + jnp.log(den), (GRID_W, HD))
            return carry

        lax.fori_loop(0, T // GRID_W, row, 0, unroll=2)

    return pl.pallas_call(
        body, name="attn_b_fwd",
        out_shape=[SDS((T, 512), F32), SDS((T, 512), F32), SDS((4, 8, GRID_W, 512), F32)], grid=(4,),
        in_specs=[pl.BlockSpec((None, 16, HD), lambda h: (h, 0, 0)),
                  pl.BlockSpec((T, HD), lambda h: (0, NHA + h)),
                  pl.BlockSpec((T, HD), lambda h: (0, NH + NHA + h)),
                  pl.BlockSpec((T, HD), lambda h: (0, 2 * NH + NHA + h))],
        out_specs=[pl.BlockSpec((T, HD), lambda h: (0, h)), pl.BlockSpec((T, HD), lambda h: (0, h)),
                   pl.BlockSpec((None, 8, GRID_W, 512), lambda h: (h, 0, 0, 0))],
        scratch_shapes=[pltpu.VMEM((T, HD), BF16), pltpu.VMEM((14, GRID_W, HD), F32)],
        compiler_params=_params(1))(rpb_rows, qkn, qkn, proj)


def _comb_fwd(os, ls):
    tm = 512

    def body(o0, o1, o2, l0, l1, l2, oa_ref, w0, w1, w2):
        lv = [l0[...], l1[...], l2[...]]
        mx = jnp.maximum(jnp.maximum(lv[0], lv[1]), lv[2])
        ev = [jnp.exp(l - mx) for l in lv]
        den = ev[0] + ev[1] + ev[2]
        wv = [e / den for e in ev]
        oa_ref[...] = (wv[0] * o0[...] + wv[1] * o1[...] + wv[2] * o2[...]).astype(BF16)
        w0[...], w1[...], w2[...] = wv

    spec = pl.BlockSpec((tm, 512), lambda i: (i, 0))
    return pl.pallas_call(
        body, name="comb_fwd", out_shape=[SDS((T, 512), BF16)] + [SDS((T, 512), F32)] * 3, grid=(T // tm,),
        in_specs=[spec] * 6, out_specs=[spec] * 4, compiler_params=_params(1))(*os, *ls)


def _mix_fwd(oa, ob, proj, b_gate, wpa, wpb):
    tm = 512

    def body(oa_ref, ob_ref, ga_ref, gb_ref, ba_ref, bb_ref, wpa_ref, wpb_ref, mixed_ref, ob16_ref):
        oav = oa_ref[...]
        obv = ob_ref[...].astype(BF16)
        ob16_ref[...] = obv
        for s in range(NSH):
            sl = slice(s * 512, (s + 1) * 512)
            ga = _sigmoid(ga_ref[:, sl] + ba_ref[:, sl])
            gb = _sigmoid(gb_ref[:, sl] + bb_ref[:, sl])
            mixed_ref[:, sl] = (ga * _dot(oav, wpa_ref[s]) + gb * _dot(obv, wpb_ref[s])).astype(BF16)

    row = lambda w: pl.BlockSpec((tm, w), lambda i: (i, 0))
    return pl.pallas_call(
        body, name="mix_fwd", out_shape=[SDS((T, D), BF16), SDS((T, 512), BF16)], grid=(T // tm,),
        in_specs=[row(512), row(512),
                  pl.BlockSpec((tm, D), lambda i: (i, 3)), pl.BlockSpec((tm, D), lambda i: (i, 4)),
                  pl.BlockSpec((1, D), lambda i: (0, 0)), pl.BlockSpec((1, D), lambda i: (0, 1)),
                  _resident((NSH, 512, 512), lambda i: (0, 0, 0)), _resident((NSH, 512, 512), lambda i: (0, 0, 0))],
        out_specs=[row(D), row(512)], compiler_params=_params(1))(oa, ob, proj, proj, b_gate, b_gate, wpa, wpb)


def _out_proj_fwd(mixed, w_out, x, g):
    tm = 512

    def body(m_ref, w_ref, x_ref, g_ref, h1_ref, hn_ref):
        h1 = x_ref[...] + _dot(m_ref[...], w_ref[...])
        h1_ref[...] = h1
        r = lax.rsqrt(jnp.mean(h1 * h1, axis=-1, keepdims=True) + EPS)
        hn_ref[...] = (h1 * r * g_ref[...]).astype(BF16)

    row = pl.BlockSpec((tm, D), lambda i: (i, 0))
    return pl.pallas_call(
        body, name="out_proj_fwd", out_shape=[SDS((T, D), F32), SDS((T, D), BF16)], grid=(T // tm,),
        in_specs=[row, _resident((D, D), lambda i: (0, 0)), row, pl.BlockSpec((1, D), lambda i: (0, 0))],
        out_specs=[row, row], compiler_params=_params(1))(mixed, w_out, x, g)


def _ffn_up(hn, w_up):
    tm, tn = T, 512
    per = (DFF // NSH) // tn

    def body(h_ref, w_ref, a_ref, u_ref):
        uv = jnp.maximum(_dot(h_ref[...], w_ref[...]), 0.0)
        a_ref[...] = (uv * uv).astype(BF16)
        u_ref[...] = uv.astype(BF16)

    out = pl.BlockSpec((tm, tn), lambda i, j: (i, j))
    return pl.pallas_call(
        body, name="ffn_up", out_shape=[SDS((T, DFF), BF16)] * 2, grid=(T // tm, DFF // tn),
        in_specs=[pl.BlockSpec((tm, D), lambda i, j: (i, 0)),
                  pl.BlockSpec((None, D, tn), lambda i, j: (j // per, 0, j % per))],
        out_specs=[out, out], compiler_params=_params(2))(hn, w_up)


def _ffn_down_loss(u, w_down, h1, target):
    tm, tk = 512, 2048
    nk = DFF // tk

    def body(u_ref, w_ref, h1_ref, t_ref, dy_ref, dy16_ref, loss_ref, acc):
        k = pl.program_id(1)

        @pl.when(k == 0)
        def _():
            acc[...] = jnp.zeros_like(acc)

        acc[...] += _dot(u_ref[...], w_ref[...])

        @pl.when(k == nk - 1)
        def _():
            def chunk(r, sq):
                rows = pl.ds(pl.multiple_of(r * 16, 16), 16)
                err = acc[rows, :] + h1_ref[rows, :] - t_ref[rows, :]
                dy = err * (1.0 / D)
                dy_ref[rows, :] = dy
                dy16_ref[rows, :] = dy.astype(BF16)
                return sq + err * err

            sq = lax.fori_loop(0, tm // 16, chunk, jnp.zeros((16, D), F32), unroll=2)
            part = 0.5 * jnp.sum(jnp.mean(sq, axis=-1, keepdims=True), axis=0, keepdims=True)
            loss_ref[...] = jnp.broadcast_to(part, (8, 128))

    row = pl.BlockSpec((tm, D), lambda i, k: (i, 0))
    once = _resident((tm, D), lambda i, k: (i, 0))
    return pl.pallas_call(
        body, name="ffn_down_loss",
        out_shape=[SDS((T, D), F32), SDS((T, D), BF16), SDS((T // tm, 8, 128), F32)], grid=(T // tm, nk),
        in_specs=[pl.BlockSpec((tm, tk), lambda i, k: (i, k)), pl.BlockSpec((tk, D), lambda i, k: (k, 0)), once, once],
        out_specs=[row, row, pl.BlockSpec((None, 8, 128), lambda i, k: (i, 0, 0))],
        scratch_shapes=[pltpu.VMEM((tm, D), F32)], compiler_params=_params(2))(u, w_down, h1, target)


def _ffn_down_bwd(dy16, w_down, u, deps=()):
    tm, tn = T, 512

    def body(dy_ref, w_ref, u_ref, du_ref):
        uv = u_ref[...].astype(F32)
        du_ref[...] = jnp.where(uv > 0.0, 2.0 * uv * _dot_nt(dy_ref[...], w_ref[...]), 0.0).astype(BF16)

    return pl.pallas_call(
        _after(body, deps), name="ffn_down_bwd", out_shape=SDS((T, DFF), BF16), grid=(T // tm, DFF // tn),
        in_specs=[DEP_SPEC] * len(deps) + [
            pl.BlockSpec((tm, D), lambda i, j: (i, 0)), pl.BlockSpec((tn, D), lambda i, j: (j, 0)),
            pl.BlockSpec((tm, tn), lambda i, j: (i, j))],
        out_specs=pl.BlockSpec((tm, tn), lambda i, j: (i, j)), compiler_params=_params(2))(*deps, dy16, w_down, u)


def _norm_bwd(xv, dz_in, g):
    r = lax.rsqrt(jnp.mean(xv * xv, axis=-1, keepdims=True) + EPS)
    dg = jnp.sum(xv * r * dz_in, axis=0, keepdims=True)
    dz = dz_in * g
    dx = r * dz - xv * (r * r * r) * jnp.mean(xv * dz, axis=-1, keepdims=True)
    return dx, dg


def _ffn_up_bwd(du, w_up, h1, dy, g, deps=()):
    tm, tk = 512, 1024
    per = (DFF // NSH) // tk
    nk = DFF // tk

    def body(du_ref, w_ref, h1_ref, dy_ref, g_ref, dh1_ref, dh16_ref, dg_ref, acc):
        i, k = pl.program_id(0), pl.program_id(1)

        @pl.when(k == 0)
        def _():
            acc[...] = jnp.zeros_like(acc)

        @pl.when((k == 0) & (i == 0))
        def _():
            dg_ref[...] = jnp.zeros_like(dg_ref)

        acc[...] += _dot_nt(du_ref[...], w_ref[...])

        @pl.when(k == nk - 1)
        def _():
            dx, dg = _norm_bwd(h1_ref[...], acc[...], g_ref[...])
            dh1 = dy_ref[...] + dx
            dh1_ref[...] = dh1
            dh16_ref[...] = dh1.astype(BF16)
            dg_ref[...] += dg

    row = pl.BlockSpec((tm, D), lambda i, k: (i, 0))
    vec = pl.BlockSpec((1, D), lambda i, k: (0, 0))
    return pl.pallas_call(
        _after(body, deps), name="ffn_up_bwd", out_shape=[SDS((T, D), F32), SDS((T, D), BF16), SDS((1, D), F32)],
        grid=(T // tm, nk),
        in_specs=[DEP_SPEC] * len(deps) + [
            pl.BlockSpec((tm, tk), lambda i, k: (i, k)),
            pl.BlockSpec((None, D, tk), lambda i, k: (k // per, 0, k % per)), row, row, vec],
        out_specs=[row, row, vec], scratch_shapes=[pltpu.VMEM((tm, D), F32)],
        compiler_params=_params(2))(*deps, du, w_up, h1, dy, g)


def _mix_bwd(dh16, w_out, oa, ob16, proj, b_gate, wpa, wpb):
    tm = 256

    def body(dh_ref, wo_ref, oa_ref, ob_ref, ga_ref, gb_ref, ba_ref, bb_ref, wpa_ref, wpb_ref,
             dya_ref, dyb_ref, dga_ref, dgb_ref, doa_ref, dob_ref, dba_ref, dbb_ref):
        @pl.when(pl.program_id(0) == 0)
        def _():
            dba_ref[...] = jnp.zeros_like(dba_ref)
            dbb_ref[...] = jnp.zeros_like(dbb_ref)

        oav, obv = oa_ref[...], ob_ref[...]
        doa = jnp.zeros((tm, 512), F32)
        dob = jnp.zeros((tm, 512), F32)
        for s in range(NSH):
            sl = slice(s * 512, (s + 1) * 512)
            dm = _dot_nt(dh_ref[...], wo_ref[sl, :])
            ga = _sigmoid(ga_ref[:, sl] + ba_ref[:, sl])
            gb = _sigmoid(gb_ref[:, sl] + bb_ref[:, sl])
            dya = (dm * ga).astype(BF16)
            dyb = (dm * gb).astype(BF16)
            dza = dm * _dot(oav, wpa_ref[s]) * ga * (1.0 - ga)
            dzb = dm * _dot(obv, wpb_ref[s]) * gb * (1.0 - gb)
            dya_ref[:, sl], dyb_ref[:, sl] = dya, dyb
            dga_ref[:, sl], dgb_ref[:, sl] = dza.astype(BF16), dzb.astype(BF16)
            dba_ref[:, sl] += jnp.sum(dza, axis=0, keepdims=True)
            dbb_ref[:, sl] += jnp.sum(dzb, axis=0, keepdims=True)
            doa += _dot_nt(dya, wpa_ref[s])
            dob += _dot_nt(dyb, wpb_ref[s])
        doa_ref[...], dob_ref[...] = doa, dob

    row = lambda w: pl.BlockSpec((tm, w), lambda i: (i, 0))
    vec = pl.BlockSpec((1, D), lambda i: (0, 0))
    wp = _resident((NSH, 512, 512), lambda i: (0, 0, 0))
    return pl.pallas_call(
        body, name="mix_bwd",
        out_shape=[SDS((T, D), BF16)] * 4 + [SDS((T, 512), F32)] * 2 + [SDS((1, D), F32)] * 2, grid=(T // tm,),
        in_specs=[row(D), _resident((D, D), lambda i: (0, 0)), row(512), row(512),
                  pl.BlockSpec((tm, D), lambda i: (i, 3)), pl.BlockSpec((tm, D), lambda i: (i, 4)),
                  pl.BlockSpec((1, D), lambda i: (0, 0)), pl.BlockSpec((1, D), lambda i: (0, 1)), wp, wp],
        out_specs=[row(D)] * 4 + [row(512)] * 2 + [vec] * 2,
        compiler_params=_params(1))(dh16, w_out, oa, ob16, proj, proj, b_gate, b_gate, wpa, wpb)


def _comb_bwd(doa, os, ws, deps=()):
    tm = 512

    def body(d_ref, o0, o1, o2, w0, w1, w2, cc_ref):
        prod = d_ref[...] * (w0[...] * o0[...] + w1[...] * o1[...] + w2[...] * o2[...])
        for h in range(4):
            sl = slice(h * HD, (h + 1) * HD)
            cc_ref[:, sl] = jnp.broadcast_to(jnp.sum(prod[:, sl], axis=-1, keepdims=True), (tm, HD))

    spec = pl.BlockSpec((tm, 512), lambda i: (i, 0))
    return pl.pallas_call(
        _after(body, deps), name="comb_bwd", out_shape=SDS((T, 512), F32), grid=(T // tm,),
        in_specs=[DEP_SPEC] * len(deps) + [spec] * 7, out_specs=spec,
        compiler_params=_params(1))(*deps, doa, *os, *ws)


def _attn_a_bwd(qkn, proj, doa, lse, w, cc, g):
    dil = DILS[g]
    m = T // dil
    nb = m // 128
    hp = _heads_per_step(m)

    def body(q_ref, k_ref, v_ref, d_ref, l_ref, w_ref, c_ref, dqk_ref, dv_ref, kp, vp, dkp, dvp):
        for hh in range(hp):
            sl = slice(hh * HD, (hh + 1) * HD)
            _fill_padded(kp, k_ref[:, sl], m)
            _fill_padded(vp, v_ref[:, sl], m)
            dkp[...] = jnp.zeros_like(dkp)
            dvp[...] = jnp.zeros_like(dvp)

            def block(b, carry):
                q0 = pl.multiple_of(b * 128, 128)
                rows = pl.ds(q0, 128)
                win = pl.ds(q0, 256)
                qb, kw, vw = q_ref[rows, sl], kp[win, :], vp[win, :]
                s = _dot_nt(qb, kw) * SCALE
                s = jnp.where(_band_mask(q0, m), s, NEG)
                wp = _wide(w_ref[rows, sl], 2) * jnp.exp(s - _wide(l_ref[rows, sl], 2))
                dob = d_ref[rows, sl].astype(BF16)
                ds = (wp * (_dot_nt(dob, vw) - _wide(c_ref[rows, sl], 2))).astype(BF16)
                dqk_ref[0, rows, sl] = _dot(ds, kw) * SCALE
                dkp[win, :] += _dot_tn(ds, qb) * SCALE
                dvp[win, :] += _dot_tn(wp.astype(BF16), dob)
                return carry

            lax.fori_loop(0, nb, block, 0, unroll=min(nb, 2))
            dqk_ref[1, :, sl] = dkp[64:64 + m, :]
            dv_ref[:, sl] = dvp[64:64 + m, :]

    blk = pl.BlockSpec((m, hp * HD), lambda h, r: (0, r * (4 // hp) + h))
    view = lambda a: a.reshape(m, dil * 512)
    dqk, dv = pl.pallas_call(
        body, name=f"attn_a_bwd_{g}", out_shape=[SDS((2, m, dil * 512), F32), SDS((m, dil * 512), F32)],
        grid=(4 // hp, dil), in_specs=[blk] * 7,
        out_specs=[pl.BlockSpec((2, m, hp * HD), lambda h, r: (0, 0, r * (4 // hp) + h)), blk],
        scratch_shapes=[pltpu.VMEM((m + 128, HD), BF16), pltpu.VMEM((m + 128, HD), BF16),
                        pltpu.VMEM((m + 128, HD), F32), pltpu.VMEM((m + 128, HD), F32)],
        compiler_params=_params(2))(*_group_views(qkn, proj, g), view(doa), view(lse), view(w), view(cc))
    return dqk.reshape(2, T, 512), dv.reshape(T, 512)


def _attn_b_bwd(qkn, proj, dob, ob, lse, bias, deps=()):
    def body(q_ref, k_ref, v_ref, d_ref, o_ref, l_ref, bias_ref, dqk_ref, dv_ref, drpb_ref, vb, dk_acc, dv_acc, a_acc):
        vb[...] = v_ref[...].astype(BF16)
        dk_acc[...] = jnp.zeros_like(dk_acc)
        dv_acc[...] = jnp.zeros_like(dv_acc)
        a_acc[...] = jnp.zeros_like(a_acc)

        def row(r, carry):
            start, off = _nbr_window(r)
            rows = pl.ds(pl.multiple_of(r * GRID_W, GRID_W), GRID_W)
            win = pl.ds(pl.multiple_of(start * GRID_W, GRID_W), 512)
            qr, kw, vw = q_ref[rows, :], k_ref[win, :], vb[win, :]
            s = _dot_nt(qr, kw) * SCALE + bias_ref[off]
            p = jnp.exp(s - _wide(l_ref[rows, :], 4))
            dov = d_ref[rows, :]
            delta = jnp.sum(dov * o_ref[rows, :], axis=-1, keepdims=True)
            do16 = dov.astype(BF16)
            ds = p * (_dot_nt(do16, vw) - delta)
            a_acc[off] += ds
            ds16 = ds.astype(BF16)
            dqk_ref[0, rows, :] = _dot(ds16, kw) * SCALE
            dk_acc[win, :] += _dot_tn(ds16, qr) * SCALE
            dv_acc[win, :] += _dot_tn(p.astype(BF16), do16)
            return carry

        lax.fori_loop(0, T // GRID_W, row, 0, unroll=2)
        dqk_ref[1] = dk_acc[...]
        dv_ref[...] = dv_acc[...]

        lane = lax.broadcasted_iota(jnp.int32, (16, HD), 1)
        rowi = lax.broadcasted_iota(jnp.int32, (16, HD), 0)
        low = (lane >= GRID_W - WIN_C) & (lane < GRID_W + WIN_C - 1)
        high = (lane >= HD - WIN_C) | (lane < WIN_C - 1)
        flip = (lax.broadcasted_iota(jnp.int32, (GRID_W, GRID_W), 0)
                + lax.broadcasted_iota(jnp.int32, (GRID_W, GRID_W), 1) == GRID_W - 1).astype(BF16)
        out = jnp.zeros((16, HD), F32)
        for d in range(14):
            acc = None
            for off in range(8):
                if 0 <= d - off <= 6 and (d - off) % 2 == 0:
                    jj = (d - off) // 2
                    piece = a_acc[off, :, jj * HD:(jj + 1) * HD]
                    acc = piece if acc is None else acc + piece
            hi = acc.astype(BF16)
            lo = (acc - hi.astype(F32)).astype(BF16)
            rev = _dot(flip, hi) + _dot(flip, lo)
            v = jnp.sum(pltpu.roll(rev, 0, 1, stride=1, stride_axis=0), axis=0, keepdims=True)
            v = jnp.broadcast_to(v, (16, HD))
            out = out + jnp.where((rowi == d) & low, v, 0.0)
            out = out + jnp.where(rowi == d + 1, pltpu.roll(jnp.where(high, v, 0.0), GRID_W, 1), 0.0)
        drpb_ref[...] = out

    blk = pl.BlockSpec((T, HD), lambda h: (0, h))
    return pl.pallas_call(
        _after(body, deps), name="attn_b_bwd",
        out_shape=[SDS((2, T, 512), F32), SDS((T, 512), F32), SDS((4, 16, HD), F32)], grid=(4,),
        in_specs=[DEP_SPEC] * len(deps) + [
            pl.BlockSpec((T, HD), lambda h: (0, NHA + h)),
            pl.BlockSpec((T, HD), lambda h: (0, NH + NHA + h)),
            pl.BlockSpec((T, HD), lambda h: (0, 2 * NH + NHA + h)), blk, blk, blk,
            pl.BlockSpec((None, 8, GRID_W, 512), lambda h: (h, 0, 0, 0))],
        out_specs=[pl.BlockSpec((2, T, HD), lambda h: (0, 0, h)), blk,
                   pl.BlockSpec((None, 16, HD), lambda h: (h, 0, 0))],
        scratch_shapes=[pltpu.VMEM((T, HD), BF16), pltpu.VMEM((T, HD), F32), pltpu.VMEM((T, HD), F32),
                        pltpu.VMEM((8, GRID_W, 512), F32)],
        compiler_params=_params(1))(*deps, qkn, qkn, proj, dob, ob, lse, bias)


def _qk_bwd(proj, nw, cos, sin, dqk_groups, dqk_b):
    tm = 256

    def body(p_ref, w_ref, cos_ref, sin_ref, d0, d1, d2, d3, o_ref, dn_ref):
        @pl.when(pl.program_id(1) == 0)
        def _():
            dn_ref[...] = jnp.zeros_like(dn_ref)

        cv, sv = cos_ref[...], sin_ref[...]
        srcs = (d0, d1, d2, d3)
        dna = jnp.zeros((1, HD), F32)
        dnb = jnp.zeros((1, HD), F32)
        for h in range(NH):
            sl = slice(h * HD, (h + 1) * HD)
            dz = srcs[h // 4][:, (h % 4) * HD:(h % 4 + 1) * HD]
            if h < NHA:
                dz = dz * cv + pltpu.roll(dz * sv, 64, 1)
            dx, dg = _norm_bwd(p_ref[:, sl], dz, w_ref[:, sl])
            o_ref[:, sl] = dx.astype(BF16)
            if h < NHA:
                dna += dg
            else:
                dnb += dg
        dn_ref[0:1, :] += dna
        dn_ref[1:2, :] += dnb

    dspec = pl.BlockSpec((None, tm, 512), lambda j, i: (j, i, 0))
    return pl.pallas_call(
        body, name="qk_bwd", out_shape=[SDS((T, 2 * D), BF16), SDS((2, 8, HD), F32)], grid=(2, T // tm),
        in_specs=[pl.BlockSpec((tm, D), lambda j, i: (i, j)),
                  pl.BlockSpec((None, 1, D), lambda j, i: (j, 0, 0)),
                  pl.BlockSpec((tm, HD), lambda j, i: (i, 0)),
                  pl.BlockSpec((tm, HD), lambda j, i: (i, 0)), dspec, dspec, dspec, dspec],
        out_specs=[pl.BlockSpec((tm, D), lambda j, i: (i, j)), pl.BlockSpec((None, 8, HD), lambda j, i: (j, 0, 0))],
        compiler_params=_params(2))(proj, nw, cos, sin, *dqk_groups, dqk_b)


def _in_proj_bwd(dproj, w_in, x, dh1, g, deps=()):
    tm, tk = 512, 1280
    per = (DIN // NSH) // tk
    nk = DIN // tk

    def body(dp_ref, w_ref, x_ref, dh_ref, g_ref, dx_ref, dg_ref, acc):
        i, k = pl.program_id(0), pl.program_id(1)

        @pl.when(k == 0)
        def _():
            acc[...] = jnp.zeros_like(acc)

        @pl.when((k == 0) & (i == 0))
        def _():
            dg_ref[...] = jnp.zeros_like(dg_ref)

        acc[...] += _dot_nt(dp_ref[...], w_ref[...])

        @pl.when(k == nk - 1)
        def _():
            dx, dg = _norm_bwd(x_ref[...], acc[...], g_ref[...])
            dx_ref[...] = dh_ref[...] + dx
            dg_ref[...] += dg

    row = pl.BlockSpec((tm, D), lambda i, k: (i, 0))
    vec = pl.BlockSpec((1, D), lambda i, k: (0, 0))
    return pl.pallas_call(
        _after(body, deps), name="in_proj_bwd", out_shape=[SDS((T, D), F32), SDS((1, D), F32)], grid=(T // tm, nk),
        in_specs=[DEP_SPEC] * len(deps) + [
            pl.BlockSpec((tm, tk), lambda i, k: (i, k)),
            pl.BlockSpec((None, D, tk), lambda i, k: (k // per, 0, k % per)), row, row, vec],
        out_specs=[row, vec], scratch_shapes=[pltpu.VMEM((tm, D), F32)],
        compiler_params=_params(2))(*deps, dproj, w_in, x, dh1, g)


def _grad_w(name, a, g, shard_rows, rows, cols, tr, tc):
    ni, nj = rows // tr, cols // tc
    if shard_rows:
        a_map, g_map = (lambda s, i, j: (0, s * ni + i)), (lambda s, i, j: (0, j))
    else:
        a_map, g_map = (lambda s, i, j: (0, i)), (lambda s, i, j: (0, s * nj + j))

    def body(a_ref, g_ref, o_ref):
        o_ref[...] = _dot_tn(a_ref[...], g_ref[...]).astype(BF16)

    return pl.pallas_call(
        body, name=name, out_shape=SDS((NSH, rows, cols), BF16), grid=(NSH, ni, nj),
        in_specs=[pl.BlockSpec((T, tr), a_map), pl.BlockSpec((T, tc), g_map)],
        out_specs=pl.BlockSpec((None, tr, tc), lambda s, i, j: (s, i, j)), compiler_params=_params(3))(a, g)


def _adamw(w, g, m, v):
    m = B1 * m + (1.0 - B1) * g
    v = B2 * v + (1.0 - B2) * (g * g)
    m_hat = m / (1.0 - B1 ** STEP)
    v_hat = v / (1.0 - B2 ** STEP)
    delta = -LR * (m_hat / (jnp.sqrt(v_hat) + AEPS) + WD * w)
    return delta, m, v


def _sum_halves(name, place, grads, theirs):
    _, rows, cols = theirs.shape
    tr = _row_tile(rows, cols, 1 << 17)

    def body(place_ref, a_ref, b_ref, o_ref):
        o_ref[...] = (a_ref[...].astype(F32) + b_ref[...].astype(F32)).astype(BF16)

    spec = pl.BlockSpec((NSH, tr, cols), lambda i, p: (0, i, 0))
    return pl.pallas_call(
        body, name=name, out_shape=SDS(theirs.shape, BF16),
        grid_spec=pltpu.PrefetchScalarGridSpec(
            num_scalar_prefetch=1, grid=(rows // tr,),
            in_specs=[pl.BlockSpec((NSH, None, tr, cols), lambda i, p: (0, p[1], i, 0)), spec], out_specs=spec),
        compiler_params=_params(1))(place, grads, theirs)


def _sum_landed(name, place, part, landed):
    _, rows, cols = part.shape
    tr = _row_tile(rows, cols, 1 << 18)

    def body(place_ref, p_ref, l_ref, o_ref):
        o_ref[...] = ((p_ref[...].astype(F32) + l_ref[0].astype(F32)) + l_ref[1].astype(F32)) + l_ref[2].astype(F32)

    return pl.pallas_call(
        body, name=name, out_shape=SDS((2, rows, cols), F32),
        grid_spec=pltpu.PrefetchScalarGridSpec(
            num_scalar_prefetch=1, grid=(rows // tr,),
            in_specs=[pl.BlockSpec((None, tr, cols), lambda i, p: (p[0], i, 0)),
                      pl.BlockSpec((3, tr, cols), lambda i, p: (0, i, 0))],
            out_specs=pl.BlockSpec((None, tr, cols), lambda i, p: (p[1], i, 0))),
        compiler_params=_params(1))(place, part, landed)


def _adam_shard(name, g, w, m, v):
    rows, cols = w.shape
    tr = _row_tile(rows, cols, 1 << 18)

    def body(g_ref, w_ref, m_ref, v_ref, d_ref, nm_ref, nv_ref):
        d_ref[...], nm_ref[...], nv_ref[...] = _adamw(w_ref[...], g_ref[...], m_ref[...], v_ref[...])

    spec = pl.BlockSpec((tr, cols), lambda i: (i, 0))
    return pl.pallas_call(
        body, name=name, out_shape=[SDS((rows, cols), F32)] * 3, grid=(rows // tr,),
        in_specs=[spec] * 4, out_specs=[spec] * 3, compiler_params=_params(1))(g, w, m, v)


def _adam_small(gathered, w, m, v):
    def body(g_ref, w_ref, m_ref, v_ref, go_ref, d_ref, nm_ref, nv_ref):
        g = g_ref[0:SMALL_ROWS, :]
        for dev in range(1, 8):
            g = g + g_ref[dev * SMALL_ROWS:(dev + 1) * SMALL_ROWS, :]
        go_ref[...] = g
        d_ref[...], nm_ref[...], nv_ref[...] = _adamw(w_ref[...], g, m_ref[...], v_ref[...])

    return pl.pallas_call(body, name="adam_small", out_shape=[SDS((SMALL_ROWS, HD), F32)] * 4)(gathered, w, m, v)


SMALL = (("norm_mix", (1, D)), ("b_gate", (1, 2 * D)), ("q_norm_a", (1, HD)), ("k_norm_a", (1, HD)),
         ("q_norm_b", (1, HD)), ("k_norm_b", (1, HD)), ("rpb_b", (1, 4, 15, 31)), ("norm_ffn", (1, D)))


def _pack_small(vals):
    pieces = []
    for (name, shape), val in zip(SMALL, vals):
        flat = val.reshape(-1)
        pad = (-flat.shape[0]) % HD
        pieces.append(jnp.pad(flat, (0, pad)).reshape(-1, HD))
    packed = jnp.concatenate(pieces, axis=0)
    return jnp.pad(packed, ((0, SMALL_ROWS - packed.shape[0]), (0, 0)))


def _unpack_small(packed):
    out, row = [], 0
    for name, shape in SMALL:
        size = int(np.prod(shape))
        nrows = -(-size // HD)
        out.append(packed[row:row + nrows].reshape(-1)[:size].reshape(shape))
        row += nrows
    return out


def kernel(x, norm_mix, w_in, b_gate, q_norm_a, k_norm_a, q_norm_b, k_norm_b, rpb_b, w_proj_a, w_proj_b, w_out, norm_ffn, w_up, w_down, loss_target, m_norm_mix, m_w_in, m_b_gate, m_q_norm_a, m_k_norm_a, m_q_norm_b, m_k_norm_b, m_rpb_b, m_w_proj_a, m_w_proj_b, m_w_out, m_norm_ffn, m_w_up, m_w_down, v_norm_mix, v_w_in, v_b_gate, v_q_norm_a, v_k_norm_a, v_q_norm_b, v_k_norm_b, v_rpb_b, v_w_proj_a, v_w_proj_b, v_w_out, v_norm_ffn, v_w_up, v_w_down):
    big_names = ("w_in", "w_proj_a", "w_proj_b", "w_out", "w_up", "w_down")
    big_w = [a[0] for a in (w_in, w_proj_a, w_proj_b, w_out, w_up, w_down)]
    big_m = [a[0] for a in (m_w_in, m_w_proj_a, m_w_proj_b, m_w_out, m_w_up, m_w_down)]
    big_v = [a[0] for a in (v_w_in, v_w_proj_a, v_w_proj_b, v_w_out, v_w_up, v_w_down)]
    x2, target = x[0], loss_target[0]

    place = jnp.stack([2 * lax.axis_index("x") + lax.axis_index("y"), lax.axis_index("c")]).astype(jnp.int32)
    groups = ((0,), (1, 2, 3), (4,), (5,))
    started = []
    for j, grp in enumerate(groups):
        deps = (started[0][4],) if j else ()
        placed = [_cast_into_place(big_w[i], "cast_" + big_names[i], place, deps) for i in grp]
        started.append(_gather_start(f"gather_start_{j}", placed))

    def whole(fulls):
        return [f.reshape(NSH, 2 * f.shape[2], f.shape[3]) for f in fulls]

    def gathered(j, after):
        send, recv, _, fulls, _ = started[j]
        fulls = _gather_wait(f"gather_wait_{j}", send, recv, fulls, after)
        return whole(_gather_finish(f"gather_finish_{j}", fulls))

    def forward_begin(j, after):
        send, recv, _, fulls, _ = started[j]
        fulls = _gather_wait(f"gather_wait_{j}", send, recv, fulls, after)
        send, recv, _, fulls, token = _forward_start(f"forward_start_{j}", fulls)
        return (send, recv, fulls), token

    def forward_end(j, state, after):
        return whole(_forward_wait(f"forward_wait_{j}", *state, after))

    def as_halves(grads):
        return [g.reshape(NSH, 2, g.shape[1] // 2, g.shape[2]) for g in grads]

    def reduce_start(j, grads, theirs):
        parts = [_sum_halves(f"sum_halves_{j}_{i}", place, a, b) for i, (a, b) in enumerate(zip(grads, theirs))]
        send, recv, parts, lands, token = _reduce_start(f"reduce_start_{j}", parts)
        return (send, recv, parts, lands), token

    def reduce_begin(j, grads):
        grads = as_halves(grads)
        return reduce_start(j, grads, _reduce_exchange(f"reduce_exchange_{j}", grads))

    def exchange_begin(j, grads):
        send, recv, grads, lands, token = _exchange_start(f"exchange_start_{j}", as_halves(grads))
        return (send, recv, grads, lands), token

    def exchange_end(j, state, after):
        return reduce_start(j, *_exchange_wait(f"exchange_wait_{j}", *state, after))

    big_out = {}

    def share_begin(j, state, after):
        send, recv, parts, lands = state
        parts, lands = _reduce_wait(f"reduce_wait_{j}", send, recv, parts, lands, after)
        sums = [_sum_landed(f"sum_landed_{j}_{i}", place, p, l) for i, (p, l) in enumerate(zip(parts, lands))]
        send, recv, _, sums, token = _share_start(f"share_start_{j}", sums)
        return (send, recv, sums), token

    def share_end(j, state, after):
        for idx, g in zip(groups[j], _share_wait(f"share_wait_{j}", *state, after)):
            g = g.reshape(big_w[idx].shape)
            big_out[idx] = (g, *_adam_shard("adam_" + big_names[idx], g, big_w[idx], big_m[idx], big_v[idx]))
        return big_out[groups[j][-1]][1]

    (win_f,) = gathered(0, tuple(s[4] for s in started[1:]))
    proj, xn = _norm_in_proj(x2, norm_mix, win_f)
    cos, sin = _rope_tables()
    nw = jnp.stack([jnp.concatenate([jnp.tile(q_norm_a, (1, NHA)), jnp.tile(q_norm_b, (1, NH - NHA))], axis=1),
                    jnp.concatenate([jnp.tile(k_norm_a, (1, NHA)), jnp.tile(k_norm_b, (1, NH - NHA))], axis=1)])
    qkn = _qk_prep(proj, nw, cos, sin)
    fw1, token = forward_begin(1, (qkn,))
    fwd_a = [_attn_a_fwd(qkn, proj, g) for g in range(3)]
    os, ls = [f[0] for f in fwd_a], [f[1] for f in fwd_a]
    fw2, token = forward_begin(2, (os[2], token))
    ob, lse_b, bias = _attn_b_fwd(qkn, proj, _rpb_rows(rpb_b[0]))
    oa, w0, w1, w2 = _comb_fwd(os, ls)
    ws = [w0, w1, w2]
    wpa_f, wpb_f, wout_f = forward_end(1, fw1, (oa, token))
    wout_f = wout_f.reshape(D, D)
    mixed, ob16 = _mix_fwd(oa, ob, proj, b_gate, wpa_f, wpb_f)
    h1, hn = _out_proj_fwd(mixed, wout_f, x2, norm_ffn)
    fw3, token = forward_begin(3, (h1,))
    (wup_f,) = forward_end(2, fw2, (hn, token))
    usq, u = _ffn_up(hn, wup_f)
    (wdown_f,) = forward_end(3, fw3, (u,))
    wdown_f = wdown_f.reshape(DFF, D)
    dy, dy16, loss_parts = _ffn_down_loss(usq, wdown_f, h1, target)
    loss = lax.psum(jnp.sum(loss_parts[:, 0, 0]), ("x", "y", "c"))

    g_down = _grad_w("grad_w_down", usq, dy16, True, DFF // NSH, D, 1024, 1024)
    ex_down, token = exchange_begin(3, [g_down])
    du = _ffn_down_bwd(dy16, wdown_f, u, deps=(token,))
    g_up = _grad_w("grad_w_up", hn, du, False, D, DFF // NSH, 1024, 1024)
    red_down, token = exchange_end(3, ex_down, (g_up,))
    ex_up, token_up = exchange_begin(2, [g_up])
    dh1, dh16, d_norm_ffn = _ffn_up_bwd(du, wup_f, h1, dy, norm_ffn, deps=(token, token_up))
    dya, dyb, dga, dgb, doa, dob, dba, dbb = _mix_bwd(dh16, wout_f, oa, ob16, proj, b_gate, wpa_f, wpb_f)
    g_out = _grad_w("grad_w_out", mixed, dh16, True, D // NSH, D, 512, 1024)
    g_pa = _grad_w("grad_w_proj_a", oa, dya, False, 512, 512, 512, 512)
    g_pb = _grad_w("grad_w_proj_b", ob16, dyb, False, 512, 512, 512, 512)
    red_up, token = exchange_end(2, ex_up, (g_out,))
    ex_mid, token_mid = exchange_begin(1, [g_pa, g_pb, g_out])
    cc = _comb_bwd(doa, os, ws, deps=(token, token_mid))
    bwd_a = [_attn_a_bwd(qkn, proj, doa, ls[g], ws[g], cc, g) for g in range(3)]
    red_mid, token = exchange_end(1, ex_mid, (bwd_a[2][1],))
    dqk_b, dv_b, drpb_t = _attn_b_bwd(qkn, proj, dob, ob, lse_b, bias, deps=(token,))
    dqk_pre, dn = _qk_bwd(proj, nw, cos, sin, [b[0] for b in bwd_a], dqk_b)
    dv16 = jnp.concatenate([b[1] for b in bwd_a] + [dv_b], axis=1).astype(BF16)
    dproj = jnp.concatenate([dqk_pre, dv16, dga, dgb], axis=1)
    g_in = _grad_w("grad_w_in", xn, dproj, False, D, DIN // NSH, 1024, 1280)
    red_in, token = reduce_begin(0, [g_in])
    grad_x, d_norm_mix = _in_proj_bwd(dproj, win_f, x2, dh1, norm_mix, deps=(token,))

    sh_down, token = share_begin(3, red_down, (grad_x,))
    sh_up, token = share_begin(2, red_up, (token,))
    done = share_end(3, sh_down, (token,))
    sh_mid, token = share_begin(1, red_mid, (done,))
    done = share_end(2, sh_up, (token,))
    sh_in, token = share_begin(0, red_in, (done,))
    done = share_end(1, sh_mid, (token,))
    done = share_end(0, sh_in, (done,))

    d_rpb = drpb_t[:, :15, GRID_W - WIN_C:GRID_W + WIN_C - 1]
    small_g = [d_norm_mix, jnp.concatenate([dba, dbb], axis=1), dn[0, 0], dn[1, 0], dn[0, 1], dn[1, 1], d_rpb, d_norm_ffn]
    gathered_small = _allgather_small(_pack_small(small_g), done)
    small_w = (norm_mix, b_gate, q_norm_a, k_norm_a, q_norm_b, k_norm_b, rpb_b, norm_ffn)
    small_m = (m_norm_mix, m_b_gate, m_q_norm_a, m_k_norm_a, m_q_norm_b, m_k_norm_b, m_rpb_b, m_norm_ffn)
    small_v = (v_norm_mix, v_b_gate, v_q_norm_a, v_k_norm_a, v_q_norm_b, v_k_norm_b, v_rpb_b, v_norm_ffn)
    small_out = [_unpack_small(p) for p in
                 _adam_small(gathered_small, _pack_small(small_w), _pack_small(small_m), _pack_small(small_v))]

    order = ("norm_mix", "w_in", "b_gate", "q_norm_a", "k_norm_a", "q_norm_b", "k_norm_b", "rpb_b",
             "w_proj_a", "w_proj_b", "w_out", "norm_ffn", "w_up", "w_down")
    small_idx = {name: i for i, (name, _) in enumerate(SMALL)}
    outs = []
    for kind in range(4):
        for name in order:
            if name in small_idx:
                outs.append(small_out[kind][small_idx[name]])
            else:
                outs.append(big_out[big_names.index(name)][kind][None])
    return (loss, grad_x[None], *outs)
```

```python
import functools

import numpy as np
import jax
import jax.numpy as jnp
from jax import lax
from jax.experimental import pallas as pl
from jax.experimental.pallas import tpu as pltpu

F32, BF16 = jnp.float32, jnp.bfloat16
SDS = jax.ShapeDtypeStruct
MESH = pl.DeviceIdType.MESH

T = 2048
D = 2048
HD = 128
NH, NHA = 16, 12
DIN = 10240
DFF = 8192
NSH = 4
DILS = (1, 4, 16)
EPS = 1e-6
NEG = -1e30
SCALE = HD ** -0.5
GRID_W, WIN_R, WIN_C = 64, 8, 16
VMEM_LIMIT = 56 * 1024 * 1024
B1, B2, LR, AEPS, WD, STEP = 0.9, 0.999, 0.001, 1e-08, 0.01, 10
SMALL_ROWS = 88


def _dot(a, b):
    return jnp.dot(a, b, preferred_element_type=F32)


def _dot_nt(a, b):
    return lax.dot_general(a, b, (((1,), (1,)), ((), ())), preferred_element_type=F32)


def _dot_tn(a, b):
    return lax.dot_general(a, b, (((0,), (0,)), ((), ())), preferred_element_type=F32)


def _params(n):
    return pltpu.CompilerParams(dimension_semantics=("arbitrary",) * n, vmem_limit_bytes=VMEM_LIMIT)


def _resident(shape, index_map):
    return pl.BlockSpec(shape, index_map, pipeline_mode=pl.Buffered(1))


def _sigmoid(z):
    return 1.0 / (1.0 + jnp.exp(-z))


def _wide(v, n):
    return jnp.concatenate([v] * n, axis=1)


def _row_tile(rows, cols, elems):
    tr = 16
    while tr * 2 <= rows and tr * 2 * cols <= elems:
        tr *= 2
    return tr


def _place():
    x, y, c = lax.axis_index("x"), lax.axis_index("y"), lax.axis_index("c")
    peers = [(1 - x, y), (x, 1 - y), (1 - x, 1 - y)]
    return x, y, c, peers


def _cast_into_place(w, name, place, deps=()):
    rows, cols = w.shape
    hr = rows // 2
    tr = min(hr, 256)
    per = hr // tr

    def body(*refs):
        w_ref, o_ref = refs[-2:]
        o_ref[...] = w_ref[...].astype(BF16)

    return pl.pallas_call(
        body, name=name, out_shape=SDS((NSH, 2, hr, cols), BF16),
        grid_spec=pltpu.PrefetchScalarGridSpec(
            num_scalar_prefetch=1, grid=(2, per),
            in_specs=[DEP_SPEC] * len(deps) + [pl.BlockSpec((tr, cols), lambda h, i, p: (h * per + i, 0))],
            out_specs=pl.BlockSpec((None, None, tr, cols), lambda h, i, p: (p[0], h, i, 0))),
        compiler_params=_params(2))(place, *deps, w)


ANY_SPEC = pl.BlockSpec(memory_space=pl.ANY)
HBM_SPEC = pl.BlockSpec(memory_space=pltpu.HBM)
SEM_SPEC = pl.BlockSpec(memory_space=pltpu.SEMAPHORE)
DEP_SPEC = pl.BlockSpec((8, 128), lambda *_: (0, 0))
EFFECT = pltpu.SideEffectType.DATAFLOW_SIDE_EFFECTING


def _after(body, deps):
    n = len(deps)
    return (lambda *refs: body(*refs[n:])) if n else body


def _split_start(name, srcs, lands, n_copies, issue):
    n, m = len(srcs), len(lands)

    def body(*refs):
        issue(refs[:n], refs[n:n + m], refs[n + m], refs[n + m + 1])
        refs[-1][...] = jnp.zeros((8, 128), F32)

    arrays = list(srcs) + list(lands)
    outs = pl.pallas_call(
        body, name=name,
        out_shape=(pltpu.SemaphoreType.DMA((n_copies,)), pltpu.SemaphoreType.DMA((n_copies,)),
                   *[pltpu.HBM(a.shape, a.dtype) for a in arrays], SDS((8, 128), F32)),
        in_specs=[HBM_SPEC] * (n + m),
        out_specs=(SEM_SPEC, SEM_SPEC, *[HBM_SPEC] * (n + m), pl.BlockSpec(memory_space=pltpu.VMEM)),
        input_output_aliases={i: 2 + i for i in range(n + m)},
        compiler_params=pltpu.CompilerParams(has_side_effects=EFFECT),
    )(*[pltpu.with_memory_space_constraint(a, pltpu.HBM) for a in arrays])
    return outs[0], outs[1], list(outs[2:2 + n]), list(outs[2 + n:2 + n + m]), outs[-1]


def _split_wait(name, send_sems, recv_sems, srcs, lands, after, wait):
    n, m = len(srcs), len(lands)

    def body(*refs):
        wait(refs[:n], refs[n:n + m], refs[n + m], refs[n + m + 1])

    arrays = list(srcs) + list(lands)
    outs = pl.pallas_call(
        body, name=name, out_shape=[pltpu.HBM(a.shape, a.dtype) for a in arrays],
        in_specs=[HBM_SPEC] * (n + m) + [SEM_SPEC, SEM_SPEC] + [ANY_SPEC] * len(after),
        out_specs=[HBM_SPEC] * (n + m), input_output_aliases={i: i for i in range(n + m)},
        compiler_params=pltpu.CompilerParams(has_side_effects=EFFECT),
    )(*arrays, send_sems, recv_sems, *after)
    return list(outs[:n]), list(outs[n:])


def _gather_start(name, fulls):
    def issue(srcs, dsts, send_sems, recv_sems):
        x, y, c, peers = _place()
        for i in range(len(fulls)):
            mine = dsts[i].at[2 * x + y, c]
            for k, (px, py) in enumerate(peers):
                pltpu.make_async_remote_copy(
                    src_ref=mine, dst_ref=mine, send_sem=send_sems.at[3 * i + k],
                    recv_sem=recv_sems.at[3 * i + k], device_id=(px, py, c), device_id_type=MESH).start()

    return _split_start(name, [], fulls, 3 * len(fulls), issue)


def _gather_wait(name, send_sems, recv_sems, fulls, after):
    def wait(srcs, dsts, send_sems, recv_sems):
        x, y, c, peers = _place()
        for i in range(len(fulls)):
            for k, (px, py) in enumerate(peers):
                cp = pltpu.make_async_remote_copy(
                    src_ref=dsts[i].at[2 * x + y, c], dst_ref=dsts[i].at[2 * px + py, c],
                    send_sem=send_sems.at[3 * i + k], recv_sem=recv_sems.at[3 * i + k],
                    device_id=(px, py, c), device_id_type=MESH)
                cp.wait_send()
                cp.wait_recv()

    return _split_wait(name, send_sems, recv_sems, [], fulls, after, wait)[1]


def _gather_finish(name, fulls):
    n = len(fulls)

    def body(*refs):
        fin, fout = refs[:n], refs[n:2 * n]
        send_sems, recv_sems = refs[2 * n:]
        x, y, c, peers = _place()

        def copy(i, k, half):
            px, py = peers[k]
            return pltpu.make_async_remote_copy(
                src_ref=fin[i].at[2 * px + py, half], dst_ref=fout[i].at[2 * px + py, half],
                send_sem=send_sems.at[3 * i + k], recv_sem=recv_sems.at[3 * i + k],
                device_id=(x, y, 1 - c), device_id_type=MESH)

        sends = [copy(i, k, c) for i in range(n) for k in range(3)]
        for cp in sends:
            cp.start()
        for i in range(n):
            for k in range(3):
                copy(i, k, 1 - c).wait_recv()
        for cp in sends:
            cp.wait_send()

    return pl.pallas_call(
        body, name=name, out_shape=[SDS(f.shape, f.dtype) for f in fulls],
        in_specs=[ANY_SPEC] * n, out_specs=[ANY_SPEC] * n, input_output_aliases={i: i for i in range(n)},
        scratch_shapes=[pltpu.SemaphoreType.DMA((3 * n,)), pltpu.SemaphoreType.DMA((3 * n,))])(*fulls)


def _reduce_exchange(name, grads):
    n = len(grads)

    def body(*refs):
        ins, theirs = refs[:n], refs[n:2 * n]
        send_sems, recv_sems = refs[2 * n:]
        x, y, c, _ = _place()
        copies = []
        for i in range(n):
            cp = pltpu.make_async_remote_copy(
                src_ref=ins[i].at[:, 1 - c], dst_ref=theirs[i], send_sem=send_sems.at[i],
                recv_sem=recv_sems.at[i], device_id=(x, y, 1 - c), device_id_type=MESH)
            cp.start()
            copies.append(cp)
        for cp in copies:
            cp.wait_recv()
            cp.wait_send()

    return pl.pallas_call(
        body, name=name, out_shape=[SDS((NSH,) + g.shape[2:], g.dtype) for g in grads],
        in_specs=[ANY_SPEC] * n, out_specs=[ANY_SPEC] * n,
        scratch_shapes=[pltpu.SemaphoreType.DMA((n,)), pltpu.SemaphoreType.DMA((n,))])(*grads)


def _reduce_start(name, parts):
    lands = [lax.empty((3,) + p.shape[1:], p.dtype) for p in parts]

    def issue(srcs, dsts, send_sems, recv_sems):
        x, y, c, peers = _place()
        for i in range(len(parts)):
            for k, (px, py) in enumerate(peers):
                pltpu.make_async_remote_copy(
                    src_ref=srcs[i].at[2 * px + py], dst_ref=dsts[i].at[k], send_sem=send_sems.at[3 * i + k],
                    recv_sem=recv_sems.at[3 * i + k], device_id=(px, py, c), device_id_type=MESH).start()

    return _split_start(name, parts, lands, 3 * len(parts), issue)


def _reduce_wait(name, send_sems, recv_sems, parts, lands, after):
    def wait(srcs, dsts, send_sems, recv_sems):
        x, y, c, peers = _place()
        for i in range(len(parts)):
            for k, (px, py) in enumerate(peers):
                cp = pltpu.make_async_remote_copy(
                    src_ref=srcs[i].at[2 * px + py], dst_ref=dsts[i].at[k], send_sem=send_sems.at[3 * i + k],
                    recv_sem=recv_sems.at[3 * i + k], device_id=(px, py, c), device_id_type=MESH)
                cp.wait_send()
                cp.wait_recv()

    return _split_wait(name, send_sems, recv_sems, parts, lands, after, wait)


def _sibling_copy(src, dst, send_sems, recv_sems, k):
    x, y, c, _ = _place()
    return pltpu.make_async_remote_copy(src_ref=src, dst_ref=dst, send_sem=send_sems.at[k], recv_sem=recv_sems.at[k],
                                        device_id=(x, y, 1 - c), device_id_type=MESH)


def _forward_start(name, fulls):
    def issue(srcs, dsts, send_sems, recv_sems):
        x, y, c, peers = _place()
        for i in range(len(fulls)):
            for k, (px, py) in enumerate(peers):
                part = dsts[i].at[2 * px + py, c]
                _sibling_copy(part, part, send_sems, recv_sems, 3 * i + k).start()

    return _split_start(name, [], fulls, 3 * len(fulls), issue)


def _forward_wait(name, send_sems, recv_sems, fulls, after):
    def wait(srcs, dsts, send_sems, recv_sems):
        x, y, c, peers = _place()
        for i in range(len(fulls)):
            for k, (px, py) in enumerate(peers):
                cp = _sibling_copy(dsts[i].at[2 * px + py, c], dsts[i].at[2 * px + py, 1 - c], send_sems, recv_sems, 3 * i + k)
                cp.wait_send()
                cp.wait_recv()

    return _split_wait(name, send_sems, recv_sems, [], fulls, after, wait)[1]


def _exchange_start(name, grads):
    lands = [lax.empty((NSH,) + g.shape[2:], g.dtype) for g in grads]

    def issue(srcs, dsts, send_sems, recv_sems):
        c = lax.axis_index("c")
        for i in range(len(grads)):
            _sibling_copy(srcs[i].at[:, 1 - c], dsts[i], send_sems, recv_sems, i).start()

    return _split_start(name, grads, lands, len(grads), issue)


def _exchange_wait(name, send_sems, recv_sems, grads, lands, after):
    def wait(srcs, dsts, send_sems, recv_sems):
        c = lax.axis_index("c")
        for i in range(len(grads)):
            cp = _sibling_copy(srcs[i].at[:, 1 - c], dsts[i], send_sems, recv_sems, i)
            cp.wait_send()
            cp.wait_recv()

    return _split_wait(name, send_sems, recv_sems, grads, lands, after, wait)


def _share_start(name, sums):
    def issue(srcs, dsts, send_sems, recv_sems):
        c = lax.axis_index("c")
        for i in range(len(sums)):
            _sibling_copy(dsts[i].at[c], dsts[i].at[c], send_sems, recv_sems, i).start()

    return _split_start(name, [], sums, len(sums), issue)


def _share_wait(name, send_sems, recv_sems, sums, after):
    def wait(srcs, dsts, send_sems, recv_sems):
        c = lax.axis_index("c")
        for i in range(len(sums)):
            cp = _sibling_copy(dsts[i].at[c], dsts[i].at[1 - c], send_sems, recv_sems, i)
            cp.wait_send()
            cp.wait_recv()

    return _split_wait(name, send_sems, recv_sems, [], sums, after, wait)[1]


def _allgather_small(v, after):
    m_per, n = v.shape

    def body(x_ref, after_ref, out_ref, send_sems, recv_sems, local_sem):
        x, y, c = lax.axis_index("x"), lax.axis_index("y"), lax.axis_index("c")
        me, sibling = (x, y, c), (x, y, 1 - c)
        chips = [(1 - x, y), (x, 1 - y), (1 - x, 1 - y)]

        def rows(px, py, pc):
            return out_ref.at[pl.ds((4 * px + 2 * py + pc) * m_per, m_per), :]

        def copy(k, block, to, src=None):
            return pltpu.make_async_remote_copy(
                src_ref=rows(*block) if src is None else src, dst_ref=rows(*block),
                send_sem=send_sems.at[k], recv_sem=recv_sems.at[k], device_id=to, device_id_type=MESH)

        mine = pltpu.make_async_copy(x_ref, rows(*me), local_sem)
        mine.start()
        first = [copy(0, me, sibling, src=x_ref)]
        first += [copy(1 + j, me, (*chip, c), src=x_ref) for j, chip in enumerate(chips)]
        for cp in first:
            cp.start()
        passed = [copy(4 + j, (*chip, c), sibling) for j, chip in enumerate(chips)]
        for j, chip in enumerate(chips):
            copy(1 + j, (*chip, c), me).wait_recv()
            passed[j].start()
        copy(0, sibling, me).wait_recv()
        for j, chip in enumerate(chips):
            copy(4 + j, (*chip, 1 - c), me).wait_recv()
        for cp in first + passed:
            cp.wait_send()
        mine.wait()

    return pl.pallas_call(
        body, name="allgather_small", out_shape=SDS((8 * m_per, n), v.dtype),
        in_specs=[pl.BlockSpec(memory_space=pltpu.VMEM), ANY_SPEC], out_specs=pl.BlockSpec(memory_space=pltpu.VMEM),
        scratch_shapes=[pltpu.SemaphoreType.DMA((7,)), pltpu.SemaphoreType.DMA((7,)), pltpu.SemaphoreType.DMA])(v, after)


def _norm_in_proj(x, g, w_full):
    tn, chunk = 512, 256
    per = (DIN // NSH) // tn

    def body(x_ref, g_ref, w_ref, proj_ref, xn_ref):
        @pl.when(pl.program_id(0) == 0)
        def _():
            def norm(r, carry):
                rows = pl.ds(pl.multiple_of(r * chunk, chunk), chunk)
                xv = x_ref[rows, :]
                rs = lax.rsqrt(jnp.mean(xv * xv, axis=-1, keepdims=True) + EPS)
                xn_ref[rows, :] = (xv * rs * g_ref[...]).astype(BF16)
                return carry

            lax.fori_loop(0, T // chunk, norm, 0)

        proj_ref[...] = _dot(xn_ref[...], w_ref[...])

    return pl.pallas_call(
        body, name="norm_in_proj", out_shape=[SDS((T, DIN), F32), SDS((T, D), BF16)], grid=(DIN // tn,),
        in_specs=[_resident((T, D), lambda j: (0, 0)),
                  pl.BlockSpec((1, D), lambda j: (0, 0)),
                  pl.BlockSpec((None, D, tn), lambda j: (j // per, 0, j % per))],
        out_specs=[pl.BlockSpec((T, tn), lambda j: (0, j)),
                   pl.BlockSpec((T, D), lambda j: (0, 0))],
        compiler_params=_params(1))(x, g, w_full)


def _rope_tables():
    pos = np.arange(T, dtype=np.float32)
    inv = (10000.0 ** (-np.arange(0, HD, 2, dtype=np.float32) / HD)).astype(np.float32)
    ang = (pos[:, None] * inv[None, :]).astype(np.float32)
    cos, sin = np.cos(ang).astype(np.float32), np.sin(ang).astype(np.float32)
    return (jnp.asarray(np.concatenate([cos, cos], axis=1)), jnp.asarray(np.concatenate([-sin, sin], axis=1)))


def _qk_prep(proj, nw, cos, sin):
    tm = 256

    def body(p_ref, w_ref, cos_ref, sin_ref, o_ref):
        cv, sv = cos_ref[...], sin_ref[...]
        for h in range(NH):
            sl = slice(h * HD, (h + 1) * HD)
            xv = p_ref[:, sl]
            r = lax.rsqrt(jnp.mean(xv * xv, axis=-1, keepdims=True) + EPS)
            z = xv * r * w_ref[:, sl]
            if h < NHA:
                z = z * cv + pltpu.roll(z, 64, 1) * sv
            o_ref[:, sl] = z.astype(BF16)

    return pl.pallas_call(
        body, name="qk_prep", out_shape=SDS((T, 2 * D), BF16), grid=(T // tm, 2),
        in_specs=[pl.BlockSpec((tm, D), lambda i, j: (i, j)),
                  pl.BlockSpec((None, 1, D), lambda i, j: (j, 0, 0)),
                  pl.BlockSpec((tm, HD), lambda i, j: (i, 0)),
                  pl.BlockSpec((tm, HD), lambda i, j: (i, 0))],
        out_specs=pl.BlockSpec((tm, D), lambda i, j: (i, j)),
        compiler_params=_params(2))(proj, nw, cos, sin)


def _band_mask(q0, m):
    ii = lax.broadcasted_iota(jnp.int32, (128, 256), 0)
    jj = lax.broadcasted_iota(jnp.int32, (128, 256), 1)
    rel = jj - ii
    kpos = jj + (q0 - 64)
    return (rel >= 0) & (rel <= 128) & (kpos >= 0) & (kpos < m)


def _fill_padded(dst, src, m):
    zeros = jnp.zeros((64, HD), dst.dtype)
    dst[0:64, :] = zeros
    dst[64 + m:128 + m, :] = zeros
    dst[64:64 + m, :] = src.astype(dst.dtype)


def _group_views(qkn, proj, g):
    m = T // DILS[g]
    cols = (qkn[:, g * 512:(g + 1) * 512], qkn[:, D + g * 512:D + (g + 1) * 512],
            proj[:, 2 * D + g * 512:2 * D + (g + 1) * 512])
    return [a.reshape(m, DILS[g] * 512) for a in cols]


def _heads_per_step(m):
    return 4 if m <= 512 else 1


def _attn_a_fwd(qkn, proj, g):
    dil = DILS[g]
    m = T // dil
    nb = m // 128
    hp = _heads_per_step(m)

    def body(q_ref, k_ref, v_ref, o_ref, l_ref, kp, vp):
        for hh in range(hp):
            sl = slice(hh * HD, (hh + 1) * HD)
            _fill_padded(kp, k_ref[:, sl], m)
            _fill_padded(vp, v_ref[:, sl], m)

            def block(b, carry):
                q0 = pl.multiple_of(b * 128, 128)
                kw, vw = kp[pl.ds(q0, 256), :], vp[pl.ds(q0, 256), :]
                s = _dot_nt(q_ref[pl.ds(q0, 128), sl], kw) * SCALE
                s = jnp.where(_band_mask(q0, m), s, NEG)
                mx = jnp.max(s, axis=-1, keepdims=True)
                e = jnp.exp(s - mx)
                den = jnp.sum(e, axis=-1, keepdims=True)
                o_ref[pl.ds(q0, 128), sl] = _dot((e / den).astype(BF16), vw)
                l_ref[pl.ds(q0, 128), sl] = jnp.broadcast_to(mx + jnp.log(den), (128, HD))
                return carry

            lax.fori_loop(0, nb, block, 0, unroll=min(nb, 2))

    blk = pl.BlockSpec((m, hp * HD), lambda h, r: (0, r * (4 // hp) + h))
    o, lse = pl.pallas_call(
        body, name=f"attn_a_fwd_{g}", out_shape=[SDS((m, dil * 512), F32)] * 2, grid=(4 // hp, dil),
        in_specs=[blk] * 3, out_specs=[blk] * 2,
        scratch_shapes=[pltpu.VMEM((m + 128, HD), BF16), pltpu.VMEM((m + 128, HD), BF16)],
        compiler_params=_params(2))(*_group_views(qkn, proj, g))
    return o.reshape(T, 512), lse.reshape(T, 512)


def _nbr_window(r):
    start = jnp.clip(r - WIN_R // 2, 0, T // GRID_W - WIN_R)
    return start, start - r + (WIN_R - 1)


def _rpb_rows(rpb):
    zeros = jnp.zeros((4, 14, 33), F32)
    a, b = rpb[:, :14], rpb[:, 1:15]
    rows = jnp.concatenate([a[:, :, 15:31], zeros, b, zeros, a[:, :, 0:15]], axis=2)
    return jnp.pad(rows, ((0, 0), (0, 2), (0, 0)))


def _attn_b_fwd(qkn, proj, rpb_rows):
    def body(r_ref, q_ref, k_ref, v_ref, o_ref, l_ref, bias_ref, vb, pair):
        qc = lax.broadcasted_iota(jnp.int32, (GRID_W, 512), 0)
        kc = lax.broadcasted_iota(jnp.int32, (GRID_W, 512), 1) & (GRID_W - 1)
        cs = jnp.clip(qc - WIN_C // 2, 0, GRID_W - WIN_C)
        colmask = (kc >= cs) & (kc < cs + WIN_C)
        for d in range(14):
            pair[d] = pltpu.roll(jnp.broadcast_to(r_ref[d:d + 1, :], (GRID_W, HD)), 0, 1, stride=1, stride_axis=0)
        for off in range(8):
            rows = jnp.concatenate([pair[off + 2 * jj] for jj in range(4)], axis=1)
            bias_ref[off] = jnp.where(colmask, rows, NEG)
        vb[...] = v_ref[...].astype(BF16)

        def row(r, carry):
            start, off = _nbr_window(r)
            q0 = pl.multiple_of(r * GRID_W, GRID_W)
            k0 = pl.multiple_of(start * GRID_W, GRID_W)
            s = _dot_nt(q_ref[pl.ds(q0, GRID_W), :], k_ref[pl.ds(k0, 512), :]) * SCALE + bias_ref[off]
            mx = jnp.max(s, axis=-1, keepdims=True)
            e = jnp.exp(s - mx)
            den = jnp.sum(e, axis=-1, keepdims=True)
            o_ref[pl.ds(q0, GRID_W), :] = _dot((e / den).astype(BF16), vb[pl.ds(k0, 512), :])
            l_ref[pl.ds(q0, GRID_W), :] = jnp.broadcast_to(mx + jnp.log(den), (GRID_W, HD))
            return carry

        lax.fori_loop(0, T // GRID_W, row, 0, unroll=2)

    return pl.pallas_call(
        body, name="attn_b_fwd",
        out_shape=[SDS((T, 512), F32), SDS((T, 512), F32), SDS((4, 8, GRID_W, 512), F32)], grid=(4,),
        in_specs=[pl.BlockSpec((None, 16, HD), lambda h: (h, 0, 0)),
                  pl.BlockSpec((T, HD), lambda h: (0, NHA + h)),
                  pl.BlockSpec((T, HD), lambda h: (0, NH + NHA + h)),
                  pl.BlockSpec((T, HD), lambda h: (0, 2 * NH + NHA + h))],
        out_specs=[pl.BlockSpec((T, HD), lambda h: (0, h)), pl.BlockSpec((T, HD), lambda h: (0, h)),
                   pl.BlockSpec((None, 8, GRID_W, 512), lambda h: (h, 0, 0, 0))],
        scratch_shapes=[pltpu.VMEM((T, HD), BF16), pltpu.VMEM((14, GRID_W, HD), F32)],
        compiler_params=_params(1))(rpb_rows, qkn, qkn, proj)


def _comb_fwd(os, ls):
    tm = 512

    def body(o0, o1, o2, l0, l1, l2, oa_ref, w0, w1, w2):
        lv = [l0[...], l1[...], l2[...]]
        mx = jnp.maximum(jnp.maximum(lv[0], lv[1]), lv[2])
        ev = [jnp.exp(l - mx) for l in lv]
        den = ev[0] + ev[1] + ev[2]
        wv = [e / den for e in ev]
        oa_ref[...] = (wv[0] * o0[...] + wv[1] * o1[...] + wv[2] * o2[...]).astype(BF16)
        w0[...], w1[...], w2[...] = wv

    spec = pl.BlockSpec((tm, 512), lambda i: (i, 0))
    return pl.pallas_call(
        body, name="comb_fwd", out_shape=[SDS((T, 512), BF16)] + [SDS((T, 512), F32)] * 3, grid=(T // tm,),
        in_specs=[spec] * 6, out_specs=[spec] * 4, compiler_params=_params(1))(*os, *ls)


def _mix_fwd(oa, ob, proj, b_gate, wpa, wpb):
    tm = 512

    def body(oa_ref, ob_ref, ga_ref, gb_ref, ba_ref, bb_ref, wpa_ref, wpb_ref, mixed_ref, ob16_ref):
        oav = oa_ref[...]
        obv = ob_ref[...].astype(BF16)
        ob16_ref[...] = obv
        for s in range(NSH):
            sl = slice(s * 512, (s + 1) * 512)
            ga = _sigmoid(ga_ref[:, sl] + ba_ref[:, sl])
            gb = _sigmoid(gb_ref[:, sl] + bb_ref[:, sl])
            mixed_ref[:, sl] = (ga * _dot(oav, wpa_ref[s]) + gb * _dot(obv, wpb_ref[s])).astype(BF16)

    row = lambda w: pl.BlockSpec((tm, w), lambda i: (i, 0))
    return pl.pallas_call(
        body, name="mix_fwd", out_shape=[SDS((T, D), BF16), SDS((T, 512), BF16)], grid=(T // tm,),
        in_specs=[row(512), row(512),
                  pl.BlockSpec((tm, D), lambda i: (i, 3)), pl.BlockSpec((tm, D), lambda i: (i, 4)),
                  pl.BlockSpec((1, D), lambda i: (0, 0)), pl.BlockSpec((1, D), lambda i: (0, 1)),
                  _resident((NSH, 512, 512), lambda i: (0, 0, 0)), _resident((NSH, 512, 512), lambda i: (0, 0, 0))],
        out_specs=[row(D), row(512)], compiler_params=_params(1))(oa, ob, proj, proj, b_gate, b_gate, wpa, wpb)


def _out_proj_fwd(mixed, w_out, x, g):
    tm = 512

    def body(m_ref, w_ref, x_ref, g_ref, h1_ref, hn_ref):
        h1 = x_ref[...] + _dot(m_ref[...], w_ref[...])
        h1_ref[...] = h1
        r = lax.rsqrt(jnp.mean(h1 * h1, axis=-1, keepdims=True) + EPS)
        hn_ref[...] = (h1 * r * g_ref[...]).astype(BF16)

    row = pl.BlockSpec((tm, D), lambda i: (i, 0))
    return pl.pallas_call(
        body, name="out_proj_fwd", out_shape=[SDS((T, D), F32), SDS((T, D), BF16)], grid=(T // tm,),
        in_specs=[row, _resident((D, D), lambda i: (0, 0)), row, pl.BlockSpec((1, D), lambda i: (0, 0))],
        out_specs=[row, row], compiler_params=_params(1))(mixed, w_out, x, g)


def _ffn_up(hn, w_up):
    tm, tn = T, 512
    per = (DFF // NSH) // tn

    def body(h_ref, w_ref, a_ref, u_ref):
        uv = jnp.maximum(_dot(h_ref[...], w_ref[...]), 0.0)
        a_ref[...] = (uv * uv).astype(BF16)
        u_ref[...] = uv.astype(BF16)

    out = pl.BlockSpec((tm, tn), lambda i, j: (i, j))
    return pl.pallas_call(
        body, name="ffn_up", out_shape=[SDS((T, DFF), BF16)] * 2, grid=(T // tm, DFF // tn),
        in_specs=[pl.BlockSpec((tm, D), lambda i, j: (i, 0)),
                  pl.BlockSpec((None, D, tn), lambda i, j: (j // per, 0, j % per))],
        out_specs=[out, out], compiler_params=_params(2))(hn, w_up)


def _ffn_down_loss(u, w_down, h1, target):
    tm, tk = 512, 2048
    nk = DFF // tk

    def body(u_ref, w_ref, h1_ref, t_ref, dy_ref, dy16_ref, loss_ref, acc):
        k = pl.program_id(1)

        @pl.when(k == 0)
        def _():
            acc[...] = jnp.zeros_like(acc)

        acc[...] += _dot(u_ref[...], w_ref[...])

        @pl.when(k == nk - 1)
        def _():
            def chunk(r, sq):
                rows = pl.ds(pl.multiple_of(r * 16, 16), 16)
                err = acc[rows, :] + h1_ref[rows, :] - t_ref[rows, :]
                dy = err * (1.0 / D)
                dy_ref[rows, :] = dy
                dy16_ref[rows, :] = dy.astype(BF16)
                return sq + err * err

            sq = lax.fori_loop(0, tm // 16, chunk, jnp.zeros((16, D), F32), unroll=2)
            part = 0.5 * jnp.sum(jnp.mean(sq, axis=-1, keepdims=True), axis=0, keepdims=True)
            loss_ref[...] = jnp.broadcast_to(part, (8, 128))

    row = pl.BlockSpec((tm, D), lambda i, k: (i, 0))
    once = _resident((tm, D), lambda i, k: (i, 0))
    return pl.pallas_call(
        body, name="ffn_down_loss",
        out_shape=[SDS((T, D), F32), SDS((T, D), BF16), SDS((T // tm, 8, 128), F32)], grid=(T // tm, nk),
        in_specs=[pl.BlockSpec((tm, tk), lambda i, k: (i, k)), pl.BlockSpec((tk, D), lambda i, k: (k, 0)), once, once],
        out_specs=[row, row, pl.BlockSpec((None, 8, 128), lambda i, k: (i, 0, 0))],
        scratch_shapes=[pltpu.VMEM((tm, D), F32)], compiler_params=_params(2))(u, w_down, h1, target)


def _ffn_down_bwd(dy16, w_down, u, deps=()):
    tm, tn = T, 512

    def body(dy_ref, w_ref, u_ref, du_ref):
        uv = u_ref[...].astype(F32)
        du_ref[...] = jnp.where(uv > 0.0, 2.0 * uv * _dot_nt(dy_ref[...], w_ref[...]), 0.0).astype(BF16)

    return pl.pallas_call(
        _after(body, deps), name="ffn_down_bwd", out_shape=SDS((T, DFF), BF16), grid=(T // tm, DFF // tn),
        in_specs=[DEP_SPEC] * len(deps) + [
            pl.BlockSpec((tm, D), lambda i, j: (i, 0)), pl.BlockSpec((tn, D), lambda i, j: (j, 0)),
            pl.BlockSpec((tm, tn), lambda i, j: (i, j))],
        out_specs=pl.BlockSpec((tm, tn), lambda i, j: (i, j)), compiler_params=_params(2))(*deps, dy16, w_down, u)


def _norm_bwd(xv, dz_in, g):
    r = lax.rsqrt(jnp.mean(xv * xv, axis=-1, keepdims=True) + EPS)
    dg = jnp.sum(xv * r * dz_in, axis=0, keepdims=True)
    dz = dz_in * g
    dx = r * dz - xv * (r * r * r) * jnp.mean(xv * dz, axis=-1, keepdims=True)
    return dx, dg


def _ffn_up_bwd(du, w_up, h1, dy, g, deps=()):
    tm, tk = 512, 1024
    per = (DFF // NSH) // tk
    nk = DFF // tk

    def body(du_ref, w_ref, h1_ref, dy_ref, g_ref, dh1_ref, dh16_ref, dg_ref, acc):
        i, k = pl.program_id(0), pl.program_id(1)

        @pl.when(k == 0)
        def _():
            acc[...] = jnp.zeros_like(acc)

        @pl.when((k == 0) & (i == 0))
        def _():
            dg_ref[...] = jnp.zeros_like(dg_ref)

        acc[...] += _dot_nt(du_ref[...], w_ref[...])

        @pl.when(k == nk - 1)
        def _():
            dx, dg = _norm_bwd(h1_ref[...], acc[...], g_ref[...])
            dh1 = dy_ref[...] + dx
            dh1_ref[...] = dh1
            dh16_ref[...] = dh1.astype(BF16)
            dg_ref[...] += dg

    row = pl.BlockSpec((tm, D), lambda i, k: (i, 0))
    vec = pl.BlockSpec((1, D), lambda i, k: (0, 0))
    return pl.pallas_call(
        _after(body, deps), name="ffn_up_bwd", out_shape=[SDS((T, D), F32), SDS((T, D), BF16), SDS((1, D), F32)],
        grid=(T // tm, nk),
        in_specs=[DEP_SPEC] * len(deps) + [
            pl.BlockSpec((tm, tk), lambda i, k: (i, k)),
            pl.BlockSpec((None, D, tk), lambda i, k: (k // per, 0, k % per)), row, row, vec],
        out_specs=[row, row, vec], scratch_shapes=[pltpu.VMEM((tm, D), F32)],
        compiler_params=_params(2))(*deps, du, w_up, h1, dy, g)


def _mix_bwd(dh16, w_out, oa, ob16, proj, b_gate, wpa, wpb):
    tm = 256

    def body(dh_ref, wo_ref, oa_ref, ob_ref, ga_ref, gb_ref, ba_ref, bb_ref, wpa_ref, wpb_ref,
             dya_ref, dyb_ref, dga_ref, dgb_ref, doa_ref, dob_ref, dba_ref, dbb_ref):
        @pl.when(pl.program_id(0) == 0)
        def _():
            dba_ref[...] = jnp.zeros_like(dba_ref)
            dbb_ref[...] = jnp.zeros_like(dbb_ref)

        oav, obv = oa_ref[...], ob_ref[...]
        doa = jnp.zeros((tm, 512), F32)
        dob = jnp.zeros((tm, 512), F32)
        for s in range(NSH):
            sl = slice(s * 512, (s + 1) * 512)
            dm = _dot_nt(dh_ref[...], wo_ref[sl, :])
            ga = _sigmoid(ga_ref[:, sl] + ba_ref[:, sl])
            gb = _sigmoid(gb_ref[:, sl] + bb_ref[:, sl])
            dya = (dm * ga).astype(BF16)
            dyb = (dm * gb).astype(BF16)
            dza = dm * _dot(oav, wpa_ref[s]) * ga * (1.0 - ga)
            dzb = dm * _dot(obv, wpb_ref[s]) * gb * (1.0 - gb)
            dya_ref[:, sl], dyb_ref[:, sl] = dya, dyb
            dga_ref[:, sl], dgb_ref[:, sl] = dza.astype(BF16), dzb.astype(BF16)
            dba_ref[:, sl] += jnp.sum(dza, axis=0, keepdims=True)
            dbb_ref[:, sl] += jnp.sum(dzb, axis=0, keepdims=True)
            doa += _dot_nt(dya, wpa_ref[s])
            dob += _dot_nt(dyb, wpb_ref[s])
        doa_ref[...], dob_ref[...] = doa, dob

    row = lambda w: pl.BlockSpec((tm, w), lambda i: (i, 0))
    vec = pl.BlockSpec((1, D), lambda i: (0, 0))
    wp = _resident((NSH, 512, 512), lambda i: (0, 0, 0))
    return pl.pallas_call(
        body, name="mix_bwd",
        out_shape=[SDS((T, D), BF16)] * 4 + [SDS((T, 512), F32)] * 2 + [SDS((1, D), F32)] * 2, grid=(T // tm,),
        in_specs=[row(D), _resident((D, D), lambda i: (0, 0)), row(512), row(512),
                  pl.BlockSpec((tm, D), lambda i: (i, 3)), pl.BlockSpec((tm, D), lambda i: (i, 4)),
                  pl.BlockSpec((1, D), lambda i: (0, 0)), pl.BlockSpec((1, D), lambda i: (0, 1)), wp, wp],
        out_specs=[row(D)] * 4 + [row(512)] * 2 + [vec] * 2,
        compiler_params=_params(1))(dh16, w_out, oa, ob16, proj, proj, b_gate, b_gate, wpa, wpb)


def _comb_bwd(doa, os, ws, deps=()):
    tm = 512

    def body(d_ref, o0, o1, o2, w0, w1, w2, cc_ref):
        prod = d_ref[...] * (w0[...] * o0[...] + w1[...] * o1[...] + w2[...] * o2[...])
        for h in range(4):
            sl = slice(h * HD, (h + 1) * HD)
            cc_ref[:, sl] = jnp.broadcast_to(jnp.sum(prod[:, sl], axis=-1, keepdims=True), (tm, HD))

    spec = pl.BlockSpec((tm, 512), lambda i: (i, 0))
    return pl.pallas_call(
        _after(body, deps), name="comb_bwd", out_shape=SDS((T, 512), F32), grid=(T // tm,),
        in_specs=[DEP_SPEC] * len(deps) + [spec] * 7, out_specs=spec,
        compiler_params=_params(1))(*deps, doa, *os, *ws)


def _attn_a_bwd(qkn, proj, doa, lse, w, cc, g):
    dil = DILS[g]
    m = T // dil
    nb = m // 128
    hp = _heads_per_step(m)

    def body(q_ref, k_ref, v_ref, d_ref, l_ref, w_ref, c_ref, dqk_ref, dv_ref, kp, vp, dkp, dvp):
        for hh in range(hp):
            sl = slice(hh * HD, (hh + 1) * HD)
            _fill_padded(kp, k_ref[:, sl], m)
            _fill_padded(vp, v_ref[:, sl], m)
            dkp[...] = jnp.zeros_like(dkp)
            dvp[...] = jnp.zeros_like(dvp)

            def block(b, carry):
                q0 = pl.multiple_of(b * 128, 128)
                rows = pl.ds(q0, 128)
                win = pl.ds(q0, 256)
                qb, kw, vw = q_ref[rows, sl], kp[win, :], vp[win, :]
                s = _dot_nt(qb, kw) * SCALE
                s = jnp.where(_band_mask(q0, m), s, NEG)
                wp = _wide(w_ref[rows, sl], 2) * jnp.exp(s - _wide(l_ref[rows, sl], 2))
                dob = d_ref[rows, sl].astype(BF16)
                ds = (wp * (_dot_nt(dob, vw) - _wide(c_ref[rows, sl], 2))).astype(BF16)
                dqk_ref[0, rows, sl] = _dot(ds, kw) * SCALE
                dkp[win, :] += _dot_tn(ds, qb) * SCALE
                dvp[win, :] += _dot_tn(wp.astype(BF16), dob)
                return carry

            lax.fori_loop(0, nb, block, 0, unroll=min(nb, 2))
            dqk_ref[1, :, sl] = dkp[64:64 + m, :]
            dv_ref[:, sl] = dvp[64:64 + m, :]

    blk = pl.BlockSpec((m, hp * HD), lambda h, r: (0, r * (4 // hp) + h))
    view = lambda a: a.reshape(m, dil * 512)
    dqk, dv = pl.pallas_call(
        body, name=f"attn_a_bwd_{g}", out_shape=[SDS((2, m, dil * 512), F32), SDS((m, dil * 512), F32)],
        grid=(4 // hp, dil), in_specs=[blk] * 7,
        out_specs=[pl.BlockSpec((2, m, hp * HD), lambda h, r: (0, 0, r * (4 // hp) + h)), blk],
        scratch_shapes=[pltpu.VMEM((m + 128, HD), BF16), pltpu.VMEM((m + 128, HD), BF16),
                        pltpu.VMEM((m + 128, HD), F32), pltpu.VMEM((m + 128, HD), F32)],
        compiler_params=_params(2))(*_group_views(qkn, proj, g), view(doa), view(lse), view(w), view(cc))
    return dqk.reshape(2, T, 512), dv.reshape(T, 512)


def _attn_b_bwd(qkn, proj, dob, ob, lse, bias, deps=()):
    def body(q_ref, k_ref, v_ref, d_ref, o_ref, l_ref, bias_ref, dqk_ref, dv_ref, drpb_ref, vb, dk_acc, dv_acc, a_acc):
        vb[...] = v_ref[...].astype(BF16)
        dk_acc[...] = jnp.zeros_like(dk_acc)
        dv_acc[...] = jnp.zeros_like(dv_acc)
        a_acc[...] = jnp.zeros_like(a_acc)

        def row(r, carry):
            start, off = _nbr_window(r)
            rows = pl.ds(pl.multiple_of(r * GRID_W, GRID_W), GRID_W)
            win = pl.ds(pl.multiple_of(start * GRID_W, GRID_W), 512)
            qr, kw, vw = q_ref[rows, :], k_ref[win, :], vb[win, :]
            s = _dot_nt(qr, kw) * SCALE + bias_ref[off]
            p = jnp.exp(s - _wide(l_ref[rows, :], 4))
            dov = d_ref[rows, :]
            delta = jnp.sum(dov * o_ref[rows, :], axis=-1, keepdims=True)
            do16 = dov.astype(BF16)
            ds = p * (_dot_nt(do16, vw) - delta)
            a_acc[off] += ds
            ds16 = ds.astype(BF16)
            dqk_ref[0, rows, :] = _dot(ds16, kw) * SCALE
            dk_acc[win, :] += _dot_tn(ds16, qr) * SCALE
            dv_acc[win, :] += _dot_tn(p.astype(BF16), do16)
            return carry

        lax.fori_loop(0, T // GRID_W, row, 0, unroll=2)
        dqk_ref[1] = dk_acc[...]
        dv_ref[...] = dv_acc[...]

        lane = lax.broadcasted_iota(jnp.int32, (16, HD), 1)
        rowi = lax.broadcasted_iota(jnp.int32, (16, HD), 0)
        low = (lane >= GRID_W - WIN_C) & (lane < GRID_W + WIN_C - 1)
        high = (lane >= HD - WIN_C) | (lane < WIN_C - 1)
        flip = (lax.broadcasted_iota(jnp.int32, (GRID_W, GRID_W), 0)
                + lax.broadcasted_iota(jnp.int32, (GRID_W, GRID_W), 1) == GRID_W - 1).astype(BF16)
        out = jnp.zeros((16, HD), F32)
        for d in range(14):
            acc = None
            for off in range(8):
                if 0 <= d - off <= 6 and (d - off) % 2 == 0:
                    jj = (d - off) // 2
                    piece = a_acc[off, :, jj * HD:(jj + 1) * HD]
                    acc = piece if acc is None else acc + piece
            hi = acc.astype(BF16)
            lo = (acc - hi.astype(F32)).astype(BF16)
            rev = _dot(flip, hi) + _dot(flip, lo)
            v = jnp.sum(pltpu.roll(rev, 0, 1, stride=1, stride_axis=0), axis=0, keepdims=True)
            v = jnp.broadcast_to(v, (16, HD))
            out = out + jnp.where((rowi == d) & low, v, 0.0)
            out = out + jnp.where(rowi == d + 1, pltpu.roll(jnp.where(high, v, 0.0), GRID_W, 1), 0.0)
        drpb_ref[...] = out

    blk = pl.BlockSpec((T, HD), lambda h: (0, h))
    return pl.pallas_call(
        _after(body, deps), name="attn_b_bwd",
        out_shape=[SDS((2, T, 512), F32), SDS((T, 512), F32), SDS((4, 16, HD), F32)], grid=(4,),
        in_specs=[DEP_SPEC] * len(deps) + [
            pl.BlockSpec((T, HD), lambda h: (0, NHA + h)),
            pl.BlockSpec((T, HD), lambda h: (0, NH + NHA + h)),
            pl.BlockSpec((T, HD), lambda h: (0, 2 * NH + NHA + h)), blk, blk, blk,
            pl.BlockSpec((None, 8, GRID_W, 512), lambda h: (h, 0, 0, 0))],
        out_specs=[pl.BlockSpec((2, T, HD), lambda h: (0, 0, h)), blk,
                   pl.BlockSpec((None, 16, HD), lambda h: (h, 0, 0))],
        scratch_shapes=[pltpu.VMEM((T, HD), BF16), pltpu.VMEM((T, HD), F32), pltpu.VMEM((T, HD), F32),
                        pltpu.VMEM((8, GRID_W, 512), F32)],
        compiler_params=_params(1))(*deps, qkn, qkn, proj, dob, ob, lse, bias)


def _qk_bwd(proj, nw, cos, sin, dqk_groups, dqk_b, dvs, dga, dgb):
    tm = 256

    def body(p_ref, w_ref, cos_ref, sin_ref, d0, d1, d2, d3, v0, v1, v2, v3, ga_ref, gb_ref, o_ref, dn_ref):
        j, i = pl.program_id(0), pl.program_id(1)

        @pl.when((j < 2) & (i == 0))
        def _():
            dn_ref[...] = jnp.zeros_like(dn_ref)

        @pl.when(j < 2)
        def _():
            cv, sv = cos_ref[...], sin_ref[...]
            srcs = (d0, d1, d2, d3)
            dna = jnp.zeros((1, HD), F32)
            dnb = jnp.zeros((1, HD), F32)
            for h in range(NH):
                sl = slice(h * HD, (h + 1) * HD)
                dz = srcs[h // 4][:, (h % 4) * HD:(h % 4 + 1) * HD]
                if h < NHA:
                    dz = dz * cv + pltpu.roll(dz * sv, 64, 1)
                dx, dg = _norm_bwd(p_ref[:, sl], dz, w_ref[:, sl])
                o_ref[:, sl] = dx.astype(BF16)
                if h < NHA:
                    dna += dg
                else:
                    dnb += dg
            dn_ref[0:1, :] += dna
            dn_ref[1:2, :] += dnb

        @pl.when(j == 2)
        def _():
            for s, v_ref in enumerate((v0, v1, v2, v3)):
                o_ref[:, s * 512:(s + 1) * 512] = v_ref[...].astype(BF16)

        @pl.when(j == 3)
        def _():
            o_ref[...] = ga_ref[...]

        @pl.when(j == 4)
        def _():
            o_ref[...] = gb_ref[...]

    def rows(used):
        return lambda j, i: (jnp.where(used(j), i, 0), 0)

    qk = lambda j: j < 2
    dspec = pl.BlockSpec((None, tm, 512), lambda j, i: (jnp.minimum(j, 1), jnp.where(j < 2, i, 0), 0))
    vspec = pl.BlockSpec((tm, 512), rows(lambda j: j == 2))
    return pl.pallas_call(
        body, name="qk_bwd", out_shape=[SDS((T, DIN), BF16), SDS((2, 8, HD), F32)], grid=(5, T // tm),
        in_specs=[pl.BlockSpec((tm, D), lambda j, i: (jnp.where(j < 2, i, 0), jnp.minimum(j, 1))),
                  pl.BlockSpec((None, 1, D), lambda j, i: (jnp.minimum(j, 1), 0, 0)),
                  pl.BlockSpec((tm, HD), rows(qk)), pl.BlockSpec((tm, HD), rows(qk)),
                  dspec, dspec, dspec, dspec, vspec, vspec, vspec, vspec,
                  pl.BlockSpec((tm, D), rows(lambda j: j == 3)), pl.BlockSpec((tm, D), rows(lambda j: j == 4))],
        out_specs=[pl.BlockSpec((tm, D), lambda j, i: (i, j)),
                   pl.BlockSpec((None, 8, HD), lambda j, i: (jnp.minimum(j, 1), 0, 0))],
        compiler_params=_params(2))(proj, nw, cos, sin, *dqk_groups, dqk_b, *dvs, dga, dgb)


def _in_proj_bwd(dproj, w_in, x, dh1, g, deps=()):
    tm, tk = 512, 1280
    per = (DIN // NSH) // tk
    nk = DIN // tk

    def body(dp_ref, w_ref, x_ref, dh_ref, g_ref, dx_ref, dg_ref, acc):
        i, k = pl.program_id(0), pl.program_id(1)

        @pl.when(k == 0)
        def _():
            acc[...] = jnp.zeros_like(acc)

        @pl.when((k == 0) & (i == 0))
        def _():
            dg_ref[...] = jnp.zeros_like(dg_ref)

        acc[...] += _dot_nt(dp_ref[...], w_ref[...])

        @pl.when(k == nk - 1)
        def _():
            dx, dg = _norm_bwd(x_ref[...], acc[...], g_ref[...])
            dx_ref[...] = dh_ref[...] + dx
            dg_ref[...] += dg

    row = pl.BlockSpec((tm, D), lambda i, k: (i, 0))
    vec = pl.BlockSpec((1, D), lambda i, k: (0, 0))
    return pl.pallas_call(
        _after(body, deps), name="in_proj_bwd", out_shape=[SDS((T, D), F32), SDS((1, D), F32)], grid=(T // tm, nk),
        in_specs=[DEP_SPEC] * len(deps) + [
            pl.BlockSpec((tm, tk), lambda i, k: (i, k)),
            pl.BlockSpec((None, D, tk), lambda i, k: (k // per, 0, k % per)), row, row, vec],
        out_specs=[row, vec], scratch_shapes=[pltpu.VMEM((tm, D), F32)],
        compiler_params=_params(2))(*deps, dproj, w_in, x, dh1, g)


def _grad_w(name, a, g, shard_rows, rows, cols, tr, tc):
    ni, nj = rows // tr, cols // tc
    if shard_rows:
        a_map, g_map = (lambda s, i, j: (0, s * ni + i)), (lambda s, i, j: (0, j))
    else:
        a_map, g_map = (lambda s, i, j: (0, i)), (lambda s, i, j: (0, s * nj + j))

    def body(a_ref, g_ref, o_ref):
        o_ref[...] = _dot_tn(a_ref[...], g_ref[...]).astype(BF16)

    return pl.pallas_call(
        body, name=name, out_shape=SDS((NSH, rows, cols), BF16), grid=(NSH, ni, nj),
        in_specs=[pl.BlockSpec((T, tr), a_map), pl.BlockSpec((T, tc), g_map)],
        out_specs=pl.BlockSpec((None, tr, tc), lambda s, i, j: (s, i, j)), compiler_params=_params(3))(a, g)


def _adamw(w, g, m, v):
    m = B1 * m + (1.0 - B1) * g
    v = B2 * v + (1.0 - B2) * (g * g)
    m_hat = m / (1.0 - B1 ** STEP)
    v_hat = v / (1.0 - B2 ** STEP)
    delta = -LR * (m_hat / (jnp.sqrt(v_hat) + AEPS) + WD * w)
    return delta, m, v


def _sum_halves(name, place, grads, theirs):
    _, rows, cols = theirs.shape
    tr = _row_tile(rows, cols, 1 << 19)

    def body(place_ref, a_ref, b_ref, o_ref):
        o_ref[...] = (a_ref[...].astype(F32) + b_ref[...].astype(F32)).astype(BF16)

    spec = pl.BlockSpec((NSH, tr, cols), lambda i, p: (0, i, 0))
    return pl.pallas_call(
        body, name=name, out_shape=SDS(theirs.shape, BF16),
        grid_spec=pltpu.PrefetchScalarGridSpec(
            num_scalar_prefetch=1, grid=(rows // tr,),
            in_specs=[pl.BlockSpec((NSH, None, tr, cols), lambda i, p: (0, p[1], i, 0)), spec], out_specs=spec),
        compiler_params=_params(1))(place, grads, theirs)


def _sum_landed(name, place, part, landed):
    _, rows, cols = part.shape
    tr = _row_tile(rows, cols, 1 << 19)

    def body(place_ref, p_ref, l_ref, o_ref):
        o_ref[...] = ((p_ref[...].astype(F32) + l_ref[0].astype(F32)) + l_ref[1].astype(F32)) + l_ref[2].astype(F32)

    return pl.pallas_call(
        body, name=name, out_shape=SDS((2, rows, cols), F32),
        grid_spec=pltpu.PrefetchScalarGridSpec(
            num_scalar_prefetch=1, grid=(rows // tr,),
            in_specs=[pl.BlockSpec((None, tr, cols), lambda i, p: (p[0], i, 0)),
                      pl.BlockSpec((3, tr, cols), lambda i, p: (0, i, 0))],
            out_specs=pl.BlockSpec((None, tr, cols), lambda i, p: (p[1], i, 0))),
        compiler_params=_params(1))(place, part, landed)


def _adam_shard(name, g, w, m, v):
    rows, cols = w.shape
    tr = _row_tile(rows, cols, 1 << 19)

    def body(g_ref, w_ref, m_ref, v_ref, d_ref, nm_ref, nv_ref):
        d_ref[...], nm_ref[...], nv_ref[...] = _adamw(w_ref[...], g_ref[...], m_ref[...], v_ref[...])

    spec = pl.BlockSpec((tr, cols), lambda i: (i, 0))
    return pl.pallas_call(
        body, name=name, out_shape=[SDS((rows, cols), F32)] * 3, grid=(rows // tr,),
        in_specs=[spec] * 4, out_specs=[spec] * 3, compiler_params=_params(1))(g, w, m, v)


def _adam_small(gathered, w, m, v):
    def body(g_ref, w_ref, m_ref, v_ref, go_ref, d_ref, nm_ref, nv_ref):
        g = g_ref[0:SMALL_ROWS, :]
        for dev in range(1, 8):
            g = g + g_ref[dev * SMALL_ROWS:(dev + 1) * SMALL_ROWS, :]
        go_ref[...] = g
        d_ref[...], nm_ref[...], nv_ref[...] = _adamw(w_ref[...], g, m_ref[...], v_ref[...])

    return pl.pallas_call(body, name="adam_small", out_shape=[SDS((SMALL_ROWS, HD), F32)] * 4)(gathered, w, m, v)


SMALL = (("norm_mix", (1, D)), ("b_gate", (1, 2 * D)), ("q_norm_a", (1, HD)), ("k_norm_a", (1, HD)),
         ("q_norm_b", (1, HD)), ("k_norm_b", (1, HD)), ("rpb_b", (1, 4, 15, 31)), ("norm_ffn", (1, D)))


def _pack_small(vals):
    pieces = []
    for (name, shape), val in zip(SMALL, vals):
        flat = val.reshape(-1)
        pad = (-flat.shape[0]) % HD
        pieces.append(jnp.pad(flat, (0, pad)).reshape(-1, HD))
    packed = jnp.concatenate(pieces, axis=0)
    return jnp.pad(packed, ((0, SMALL_ROWS - packed.shape[0]), (0, 0)))


def _unpack_small(packed):
    out, row = [], 0
    for name, shape in SMALL:
        size = int(np.prod(shape))
        nrows = -(-size // HD)
        out.append(packed[row:row + nrows].reshape(-1)[:size].reshape(shape))
        row += nrows
    return out


def kernel(x, norm_mix, w_in, b_gate, q_norm_a, k_norm_a, q_norm_b, k_norm_b, rpb_b, w_proj_a, w_proj_b, w_out, norm_ffn, w_up, w_down, loss_target, m_norm_mix, m_w_in, m_b_gate, m_q_norm_a, m_k_norm_a, m_q_norm_b, m_k_norm_b, m_rpb_b, m_w_proj_a, m_w_proj_b, m_w_out, m_norm_ffn, m_w_up, m_w_down, v_norm_mix, v_w_in, v_b_gate, v_q_norm_a, v_k_norm_a, v_q_norm_b, v_k_norm_b, v_rpb_b, v_w_proj_a, v_w_proj_b, v_w_out, v_norm_ffn, v_w_up, v_w_down):
    big_names = ("w_in", "w_proj_a", "w_proj_b", "w_out", "w_up", "w_down")
    big_w = [a[0] for a in (w_in, w_proj_a, w_proj_b, w_out, w_up, w_down)]
    big_m = [a[0] for a in (m_w_in, m_w_proj_a, m_w_proj_b, m_w_out, m_w_up, m_w_down)]
    big_v = [a[0] for a in (v_w_in, v_w_proj_a, v_w_proj_b, v_w_out, v_w_up, v_w_down)]
    x2, target = x[0], loss_target[0]

    place = jnp.stack([2 * lax.axis_index("x") + lax.axis_index("y"), lax.axis_index("c")]).astype(jnp.int32)
    groups = ((0,), (1, 2, 3), (4,), (5,))
    started = []
    for j, grp in enumerate(groups):
        deps = (started[0][4],) if j else ()
        placed = [_cast_into_place(big_w[i], "cast_" + big_names[i], place, deps) for i in grp]
        started.append(_gather_start(f"gather_start_{j}", placed))

    def whole(fulls):
        return [f.reshape(NSH, 2 * f.shape[2], f.shape[3]) for f in fulls]

    def gathered(j, after):
        send, recv, _, fulls, _ = started[j]
        fulls = _gather_wait(f"gather_wait_{j}", send, recv, fulls, after)
        return whole(_gather_finish(f"gather_finish_{j}", fulls))

    def forward_begin(j, after):
        send, recv, _, fulls, _ = started[j]
        fulls = _gather_wait(f"gather_wait_{j}", send, recv, fulls, after)
        send, recv, _, fulls, token = _forward_start(f"forward_start_{j}", fulls)
        return (send, recv, fulls), token

    def forward_end(j, state, after):
        return whole(_forward_wait(f"forward_wait_{j}", *state, after))

    def as_halves(grads):
        return [g.reshape(NSH, 2, g.shape[1] // 2, g.shape[2]) for g in grads]

    def reduce_start(j, grads, theirs):
        parts = [_sum_halves(f"sum_halves_{j}_{i}", place, a, b) for i, (a, b) in enumerate(zip(grads, theirs))]
        send, recv, parts, lands, token = _reduce_start(f"reduce_start_{j}", parts)
        return (send, recv, parts, lands), token

    def reduce_begin(j, grads):
        grads = as_halves(grads)
        return reduce_start(j, grads, _reduce_exchange(f"reduce_exchange_{j}", grads))

    def exchange_begin(j, grads):
        send, recv, grads, lands, token = _exchange_start(f"exchange_start_{j}", as_halves(grads))
        return (send, recv, grads, lands), token

    def exchange_end(j, state, after):
        return reduce_start(j, *_exchange_wait(f"exchange_wait_{j}", *state, after))

    big_out = {}

    def share_begin(j, state, after):
        send, recv, parts, lands = state
        parts, lands = _reduce_wait(f"reduce_wait_{j}", send, recv, parts, lands, after)
        sums = [_sum_landed(f"sum_landed_{j}_{i}", place, p, l) for i, (p, l) in enumerate(zip(parts, lands))]
        send, recv, _, sums, token = _share_start(f"share_start_{j}", sums)
        return (send, recv, sums), token

    def share_end(j, state, after):
        for idx, g in zip(groups[j], _share_wait(f"share_wait_{j}", *state, after)):
            g = g.reshape(big_w[idx].shape)
            big_out[idx] = (g, *_adam_shard("adam_" + big_names[idx], g, big_w[idx], big_m[idx], big_v[idx]))
        return big_out[groups[j][-1]][1]

    (win_f,) = gathered(0, tuple(s[4] for s in started[1:]))
    proj, xn = _norm_in_proj(x2, norm_mix, win_f)
    cos, sin = _rope_tables()
    nw = jnp.stack([jnp.concatenate([jnp.tile(q_norm_a, (1, NHA)), jnp.tile(q_norm_b, (1, NH - NHA))], axis=1),
                    jnp.concatenate([jnp.tile(k_norm_a, (1, NHA)), jnp.tile(k_norm_b, (1, NH - NHA))], axis=1)])
    qkn = _qk_prep(proj, nw, cos, sin)
    fw1, token = forward_begin(1, (qkn,))
    fwd_a = [_attn_a_fwd(qkn, proj, g) for g in range(3)]
    os, ls = [f[0] for f in fwd_a], [f[1] for f in fwd_a]
    fw2, token = forward_begin(2, (os[2], token))
    ob, lse_b, bias = _attn_b_fwd(qkn, proj, _rpb_rows(rpb_b[0]))
    oa, w0, w1, w2 = _comb_fwd(os, ls)
    ws = [w0, w1, w2]
    wpa_f, wpb_f, wout_f = forward_end(1, fw1, (oa, token))
    wout_f = wout_f.reshape(D, D)
    mixed, ob16 = _mix_fwd(oa, ob, proj, b_gate, wpa_f, wpb_f)
    h1, hn = _out_proj_fwd(mixed, wout_f, x2, norm_ffn)
    fw3, token = forward_begin(3, (h1,))
    (wup_f,) = forward_end(2, fw2, (hn, token))
    usq, u = _ffn_up(hn, wup_f)
    (wdown_f,) = forward_end(3, fw3, (u,))
    wdown_f = wdown_f.reshape(DFF, D)
    dy, dy16, loss_parts = _ffn_down_loss(usq, wdown_f, h1, target)
    loss = lax.psum(jnp.sum(loss_parts[:, 0, 0]), ("x", "y", "c"))

    g_down = _grad_w("grad_w_down", usq, dy16, True, DFF // NSH, D, 1024, 1024)
    ex_down, token = exchange_begin(3, [g_down])
    du = _ffn_down_bwd(dy16, wdown_f, u, deps=(token,))
    g_up = _grad_w("grad_w_up", hn, du, False, D, DFF // NSH, 1024, 1024)
    red_down, token = exchange_end(3, ex_down, (g_up,))
    ex_up, token_up = exchange_begin(2, [g_up])
    dh1, dh16, d_norm_ffn = _ffn_up_bwd(du, wup_f, h1, dy, norm_ffn, deps=(token, token_up))
    dya, dyb, dga, dgb, doa, dob, dba, dbb = _mix_bwd(dh16, wout_f, oa, ob16, proj, b_gate, wpa_f, wpb_f)
    g_out = _grad_w("grad_w_out", mixed, dh16, True, D // NSH, D, 512, 1024)
    g_pa = _grad_w("grad_w_proj_a", oa, dya, False, 512, 512, 512, 512)
    g_pb = _grad_w("grad_w_proj_b", ob16, dyb, False, 512, 512, 512, 512)
    red_up, token = exchange_end(2, ex_up, (g_out,))
    ex_mid, token_mid = exchange_begin(1, [g_pa, g_pb, g_out])
    cc = _comb_bwd(doa, os, ws, deps=(token, token_mid))
    bwd_a = [_attn_a_bwd(qkn, proj, doa, ls[g], ws[g], cc, g) for g in range(3)]
    red_mid, token = exchange_end(1, ex_mid, (bwd_a[2][1],))
    dqk_b, dv_b, drpb_t = _attn_b_bwd(qkn, proj, dob, ob, lse_b, bias, deps=(token,))
    dproj, dn = _qk_bwd(proj, nw, cos, sin, [b[0] for b in bwd_a], dqk_b, [b[1] for b in bwd_a] + [dv_b], dga, dgb)
    g_in = _grad_w("grad_w_in", xn, dproj, False, D, DIN // NSH, 1024, 1280)
    red_in, token = reduce_begin(0, [g_in])
    grad_x, d_norm_mix = _in_proj_bwd(dproj, win_f, x2, dh1, norm_mix, deps=(token,))

    sh_down, token = share_begin(3, red_down, (grad_x,))
    sh_up, token = share_begin(2, red_up, (token,))
    done = share_end(3, sh_down, (token,))
    sh_mid, token = share_begin(1, red_mid, (done,))
    done = share_end(2, sh_up, (token,))
    sh_in, token = share_begin(0, red_in, (done,))
    done = share_end(1, sh_mid, (token,))
    done = share_end(0, sh_in, (done,))

    d_rpb = drpb_t[:, :15, GRID_W - WIN_C:GRID_W + WIN_C - 1]
    small_g = [d_norm_mix, jnp.concatenate([dba, dbb], axis=1), dn[0, 0], dn[1, 0], dn[0, 1], dn[1, 1], d_rpb, d_norm_ffn]
    gathered_small = _allgather_small(_pack_small(small_g), done)
    small_w = (norm_mix, b_gate, q_norm_a, k_norm_a, q_norm_b, k_norm_b, rpb_b, norm_ffn)
    small_m = (m_norm_mix, m_b_gate, m_q_norm_a, m_k_norm_a, m_q_norm_b, m_k_norm_b, m_rpb_b, m_norm_ffn)
    small_v = (v_norm_mix, v_b_gate, v_q_norm_a, v_k_norm_a, v_q_norm_b, v_k_norm_b, v_rpb_b, v_norm_ffn)
    small_out = [_unpack_small(p) for p in
                 _adam_small(gathered_small, _pack_small(small_w), _pack_small(small_m), _pack_small(small_v))]

    order = ("norm_mix", "w_in", "b_gate", "q_norm_a", "k_norm_a", "q_norm_b", "k_norm_b", "rpb_b",
             "w_proj_a", "w_proj_b", "w_out", "norm_ffn", "w_up", "w_down")
    small_idx = {name: i for i, (name, _) in enumerate(SMALL)}
    outs = []
    for kind in range(4):
        for name in order:
            if name in small_idx:
                outs.append(small_out[kind][small_idx[name]])
            else:
                outs.append(big_out[big_names.index(name)][kind][None])
    return (loss, grad_x[None], *outs)
```

```python
import functools

import numpy as np
import jax
import jax.numpy as jnp
from jax import lax
from jax.experimental import pallas as pl
from jax.experimental.pallas import tpu as pltpu

F32, BF16 = jnp.float32, jnp.bfloat16
SDS = jax.ShapeDtypeStruct
MESH = pl.DeviceIdType.MESH

T = 2048
D = 2048
HD = 128
NH, NHA = 16, 12
DIN = 10240
DFF = 8192
NSH = 4
DILS = (1, 4, 16)
EPS = 1e-6
NEG = -1e30
SCALE = HD ** -0.5
GRID_W, WIN_R, WIN_C = 64, 8, 16
VMEM_LIMIT = 56 * 1024 * 1024
B1, B2, LR, AEPS, WD, STEP = 0.9, 0.999, 0.001, 1e-08, 0.01, 10
SMALL_ROWS = 88


def _dot(a, b):
    return jnp.dot(a, b, preferred_element_type=F32)


def _dot_nt(a, b):
    return lax.dot_general(a, b, (((1,), (1,)), ((), ())), preferred_element_type=F32)


def _dot_tn(a, b):
    return lax.dot_general(a, b, (((0,), (0,)), ((), ())), preferred_element_type=F32)


def _params(n):
    return pltpu.CompilerParams(dimension_semantics=("arbitrary",) * n, vmem_limit_bytes=VMEM_LIMIT)


def _resident(shape, index_map):
    return pl.BlockSpec(shape, index_map, pipeline_mode=pl.Buffered(1))


def _sigmoid(z):
    return 1.0 / (1.0 + jnp.exp(-z))


def _wide(v, n):
    return jnp.concatenate([v] * n, axis=1)


def _row_tile(rows, cols, elems):
    tr = 16
    while tr * 2 <= rows and tr * 2 * cols <= elems:
        tr *= 2
    return tr


def _place():
    x, y, c = lax.axis_index("x"), lax.axis_index("y"), lax.axis_index("c")
    peers = [(1 - x, y), (x, 1 - y), (1 - x, 1 - y)]
    return x, y, c, peers


def _cast_into_place(w, name, place, deps=()):
    rows, cols = w.shape
    hr = rows // 2
    tr = min(hr, 256)
    per = hr // tr

    def body(*refs):
        w_ref, o_ref = refs[-2:]
        o_ref[...] = w_ref[...].astype(BF16)

    return pl.pallas_call(
        body, name=name, out_shape=SDS((NSH, 2, hr, cols), BF16),
        grid_spec=pltpu.PrefetchScalarGridSpec(
            num_scalar_prefetch=1, grid=(2, per),
            in_specs=[DEP_SPEC] * len(deps) + [pl.BlockSpec((tr, cols), lambda h, i, p: (h * per + i, 0))],
            out_specs=pl.BlockSpec((None, None, tr, cols), lambda h, i, p: (p[0], h, i, 0))),
        compiler_params=_params(2))(place, *deps, w)


ANY_SPEC = pl.BlockSpec(memory_space=pl.ANY)
HBM_SPEC = pl.BlockSpec(memory_space=pltpu.HBM)
SEM_SPEC = pl.BlockSpec(memory_space=pltpu.SEMAPHORE)
DEP_SPEC = pl.BlockSpec((8, 128), lambda *_: (0, 0))
EFFECT = pltpu.SideEffectType.DATAFLOW_SIDE_EFFECTING


def _after(body, deps):
    n = len(deps)
    return (lambda *refs: body(*refs[n:])) if n else body


def _split_start(name, srcs, lands, n_copies, issue):
    n, m = len(srcs), len(lands)

    def body(*refs):
        issue(refs[:n], refs[n:n + m], refs[n + m], refs[n + m + 1])
        refs[-1][...] = jnp.zeros((8, 128), F32)

    arrays = list(srcs) + list(lands)
    outs = pl.pallas_call(
        body, name=name,
        out_shape=(pltpu.SemaphoreType.DMA((n_copies,)), pltpu.SemaphoreType.DMA((n_copies,)),
                   *[pltpu.HBM(a.shape, a.dtype) for a in arrays], SDS((8, 128), F32)),
        in_specs=[HBM_SPEC] * (n + m),
        out_specs=(SEM_SPEC, SEM_SPEC, *[HBM_SPEC] * (n + m), pl.BlockSpec(memory_space=pltpu.VMEM)),
        input_output_aliases={i: 2 + i for i in range(n + m)},
        compiler_params=pltpu.CompilerParams(has_side_effects=EFFECT),
    )(*[pltpu.with_memory_space_constraint(a, pltpu.HBM) for a in arrays])
    return outs[0], outs[1], list(outs[2:2 + n]), list(outs[2 + n:2 + n + m]), outs[-1]


def _split_wait(name, send_sems, recv_sems, srcs, lands, after, wait):
    n, m = len(srcs), len(lands)

    def body(*refs):
        wait(refs[:n], refs[n:n + m], refs[n + m], refs[n + m + 1])

    arrays = list(srcs) + list(lands)
    outs = pl.pallas_call(
        body, name=name, out_shape=[pltpu.HBM(a.shape, a.dtype) for a in arrays],
        in_specs=[HBM_SPEC] * (n + m) + [SEM_SPEC, SEM_SPEC] + [ANY_SPEC] * len(after),
        out_specs=[HBM_SPEC] * (n + m), input_output_aliases={i: i for i in range(n + m)},
        compiler_params=pltpu.CompilerParams(has_side_effects=EFFECT),
    )(*arrays, send_sems, recv_sems, *after)
    return list(outs[:n]), list(outs[n:])


def _gather_start(name, fulls):
    def issue(srcs, dsts, send_sems, recv_sems):
        x, y, c, peers = _place()
        for i in range(len(fulls)):
            mine = dsts[i].at[2 * x + y, c]
            for k, (px, py) in enumerate(peers):
                pltpu.make_async_remote_copy(
                    src_ref=mine, dst_ref=mine, send_sem=send_sems.at[3 * i + k],
                    recv_sem=recv_sems.at[3 * i + k], device_id=(px, py, c), device_id_type=MESH).start()

    return _split_start(name, [], fulls, 3 * len(fulls), issue)


def _gather_wait(name, send_sems, recv_sems, fulls, after):
    def wait(srcs, dsts, send_sems, recv_sems):
        x, y, c, peers = _place()
        for i in range(len(fulls)):
            for k, (px, py) in enumerate(peers):
                cp = pltpu.make_async_remote_copy(
                    src_ref=dsts[i].at[2 * x + y, c], dst_ref=dsts[i].at[2 * px + py, c],
                    send_sem=send_sems.at[3 * i + k], recv_sem=recv_sems.at[3 * i + k],
                    device_id=(px, py, c), device_id_type=MESH)
                cp.wait_send()
                cp.wait_recv()

    return _split_wait(name, send_sems, recv_sems, [], fulls, after, wait)[1]


def _gather_finish(name, fulls):
    n = len(fulls)

    def body(*refs):
        fin, fout = refs[:n], refs[n:2 * n]
        send_sems, recv_sems = refs[2 * n:]
        x, y, c, peers = _place()

        def copy(i, k, half):
            px, py = peers[k]
            return pltpu.make_async_remote_copy(
                src_ref=fin[i].at[2 * px + py, half], dst_ref=fout[i].at[2 * px + py, half],
                send_sem=send_sems.at[3 * i + k], recv_sem=recv_sems.at[3 * i + k],
                device_id=(x, y, 1 - c), device_id_type=MESH)

        sends = [copy(i, k, c) for i in range(n) for k in range(3)]
        for cp in sends:
            cp.start()
        for i in range(n):
            for k in range(3):
                copy(i, k, 1 - c).wait_recv()
        for cp in sends:
            cp.wait_send()

    return pl.pallas_call(
        body, name=name, out_shape=[SDS(f.shape, f.dtype) for f in fulls],
        in_specs=[ANY_SPEC] * n, out_specs=[ANY_SPEC] * n, input_output_aliases={i: i for i in range(n)},
        scratch_shapes=[pltpu.SemaphoreType.DMA((3 * n,)), pltpu.SemaphoreType.DMA((3 * n,))])(*fulls)


def _reduce_exchange(name, grads):
    n = len(grads)

    def body(*refs):
        ins, theirs = refs[:n], refs[n:2 * n]
        send_sems, recv_sems = refs[2 * n:]
        x, y, c, _ = _place()
        copies = []
        for i in range(n):
            cp = pltpu.make_async_remote_copy(
                src_ref=ins[i].at[:, 1 - c], dst_ref=theirs[i], send_sem=send_sems.at[i],
                recv_sem=recv_sems.at[i], device_id=(x, y, 1 - c), device_id_type=MESH)
            cp.start()
            copies.append(cp)
        for cp in copies:
            cp.wait_recv()
            cp.wait_send()

    return pl.pallas_call(
        body, name=name, out_shape=[SDS((NSH,) + g.shape[2:], g.dtype) for g in grads],
        in_specs=[ANY_SPEC] * n, out_specs=[ANY_SPEC] * n,
        scratch_shapes=[pltpu.SemaphoreType.DMA((n,)), pltpu.SemaphoreType.DMA((n,))])(*grads)


def _reduce_start(name, parts):
    lands = [lax.empty((3,) + p.shape[1:], p.dtype) for p in parts]

    def issue(srcs, dsts, send_sems, recv_sems):
        x, y, c, peers = _place()
        for i in range(len(parts)):
            for k, (px, py) in enumerate(peers):
                pltpu.make_async_remote_copy(
                    src_ref=srcs[i].at[2 * px + py], dst_ref=dsts[i].at[k], send_sem=send_sems.at[3 * i + k],
                    recv_sem=recv_sems.at[3 * i + k], device_id=(px, py, c), device_id_type=MESH).start()

    return _split_start(name, parts, lands, 3 * len(parts), issue)


def _reduce_wait(name, send_sems, recv_sems, parts, lands, after):
    def wait(srcs, dsts, send_sems, recv_sems):
        x, y, c, peers = _place()
        for i in range(len(parts)):
            for k, (px, py) in enumerate(peers):
                cp = pltpu.make_async_remote_copy(
                    src_ref=srcs[i].at[2 * px + py], dst_ref=dsts[i].at[k], send_sem=send_sems.at[3 * i + k],
                    recv_sem=recv_sems.at[3 * i + k], device_id=(px, py, c), device_id_type=MESH)
                cp.wait_send()
                cp.wait_recv()

    return _split_wait(name, send_sems, recv_sems, parts, lands, after, wait)


def _sibling_copy(src, dst, send_sems, recv_sems, k):
    x, y, c, _ = _place()
    return pltpu.make_async_remote_copy(src_ref=src, dst_ref=dst, send_sem=send_sems.at[k], recv_sem=recv_sems.at[k],
                                        device_id=(x, y, 1 - c), device_id_type=MESH)


def _forward_start(name, fulls):
    def issue(srcs, dsts, send_sems, recv_sems):
        x, y, c, peers = _place()
        for i in range(len(fulls)):
            for k, (px, py) in enumerate(peers):
                part = dsts[i].at[2 * px + py, c]
                _sibling_copy(part, part, send_sems, recv_sems, 3 * i + k).start()

    return _split_start(name, [], fulls, 3 * len(fulls), issue)


def _forward_wait(name, send_sems, recv_sems, fulls, after):
    def wait(srcs, dsts, send_sems, recv_sems):
        x, y, c, peers = _place()
        for i in range(len(fulls)):
            for k, (px, py) in enumerate(peers):
                cp = _sibling_copy(dsts[i].at[2 * px + py, c], dsts[i].at[2 * px + py, 1 - c], send_sems, recv_sems, 3 * i + k)
                cp.wait_send()
                cp.wait_recv()

    return _split_wait(name, send_sems, recv_sems, [], fulls, after, wait)[1]


def _exchange_start(name, grads):
    lands = [lax.empty((NSH,) + g.shape[2:], g.dtype) for g in grads]

    def issue(srcs, dsts, send_sems, recv_sems):
        c = lax.axis_index("c")
        for i in range(len(grads)):
            _sibling_copy(srcs[i].at[:, 1 - c], dsts[i], send_sems, recv_sems, i).start()

    return _split_start(name, grads, lands, len(grads), issue)


def _exchange_wait(name, send_sems, recv_sems, grads, lands, after):
    def wait(srcs, dsts, send_sems, recv_sems):
        c = lax.axis_index("c")
        for i in range(len(grads)):
            cp = _sibling_copy(srcs[i].at[:, 1 - c], dsts[i], send_sems, recv_sems, i)
            cp.wait_send()
            cp.wait_recv()

    return _split_wait(name, send_sems, recv_sems, grads, lands, after, wait)


def _share_start(name, sums):
    def issue(srcs, dsts, send_sems, recv_sems):
        c = lax.axis_index("c")
        for i in range(len(sums)):
            _sibling_copy(dsts[i].at[c], dsts[i].at[c], send_sems, recv_sems, i).start()

    return _split_start(name, [], sums, len(sums), issue)


def _share_wait(name, send_sems, recv_sems, sums, after):
    def wait(srcs, dsts, send_sems, recv_sems):
        c = lax.axis_index("c")
        for i in range(len(sums)):
            cp = _sibling_copy(dsts[i].at[c], dsts[i].at[1 - c], send_sems, recv_sems, i)
            cp.wait_send()
            cp.wait_recv()

    return _split_wait(name, send_sems, recv_sems, [], sums, after, wait)[1]


def _allgather_small(v, after):
    m_per, n = v.shape

    def body(x_ref, after_ref, out_ref, send_sems, recv_sems, local_sem):
        x, y, c = lax.axis_index("x"), lax.axis_index("y"), lax.axis_index("c")
        me, sibling = (x, y, c), (x, y, 1 - c)
        chips = [(1 - x, y), (x, 1 - y), (1 - x, 1 - y)]

        def rows(px, py, pc):
            return out_ref.at[pl.ds((4 * px + 2 * py + pc) * m_per, m_per), :]

        def copy(k, block, to, src=None):
            return pltpu.make_async_remote_copy(
                src_ref=rows(*block) if src is None else src, dst_ref=rows(*block),
                send_sem=send_sems.at[k], recv_sem=recv_sems.at[k], device_id=to, device_id_type=MESH)

        mine = pltpu.make_async_copy(x_ref, rows(*me), local_sem)
        mine.start()
        first = [copy(0, me, sibling, src=x_ref)]
        first += [copy(1 + j, me, (*chip, c), src=x_ref) for j, chip in enumerate(chips)]
        for cp in first:
            cp.start()
        passed = [copy(4 + j, (*chip, c), sibling) for j, chip in enumerate(chips)]
        for j, chip in enumerate(chips):
            copy(1 + j, (*chip, c), me).wait_recv()
            passed[j].start()
        copy(0, sibling, me).wait_recv()
        for j, chip in enumerate(chips):
            copy(4 + j, (*chip, 1 - c), me).wait_recv()
        for cp in first + passed:
            cp.wait_send()
        mine.wait()

    return pl.pallas_call(
        body, name="allgather_small", out_shape=SDS((8 * m_per, n), v.dtype),
        in_specs=[pl.BlockSpec(memory_space=pltpu.VMEM), ANY_SPEC], out_specs=pl.BlockSpec(memory_space=pltpu.VMEM),
        scratch_shapes=[pltpu.SemaphoreType.DMA((7,)), pltpu.SemaphoreType.DMA((7,)), pltpu.SemaphoreType.DMA])(v, after)


def _norm_in_proj(x, g, w_full):
    tn, chunk = 512, 256
    per = (DIN // NSH) // tn

    def body(x_ref, g_ref, w_ref, proj_ref, xn_ref):
        @pl.when(pl.program_id(0) == 0)
        def _():
            def norm(r, carry):
                rows = pl.ds(pl.multiple_of(r * chunk, chunk), chunk)
                xv = x_ref[rows, :]
                rs = lax.rsqrt(jnp.mean(xv * xv, axis=-1, keepdims=True) + EPS)
                xn_ref[rows, :] = (xv * rs * g_ref[...]).astype(BF16)
                return carry

            lax.fori_loop(0, T // chunk, norm, 0)

        proj_ref[...] = _dot(xn_ref[...], w_ref[...])

    return pl.pallas_call(
        body, name="norm_in_proj", out_shape=[SDS((T, DIN), F32), SDS((T, D), BF16)], grid=(DIN // tn,),
        in_specs=[_resident((T, D), lambda j: (0, 0)),
                  pl.BlockSpec((1, D), lambda j: (0, 0)),
                  pl.BlockSpec((None, D, tn), lambda j: (j // per, 0, j % per))],
        out_specs=[pl.BlockSpec((T, tn), lambda j: (0, j)),
                   pl.BlockSpec((T, D), lambda j: (0, 0))],
        compiler_params=_params(1))(x, g, w_full)


def _rope_tables():
    pos = np.arange(T, dtype=np.float32)
    inv = (10000.0 ** (-np.arange(0, HD, 2, dtype=np.float32) / HD)).astype(np.float32)
    ang = (pos[:, None] * inv[None, :]).astype(np.float32)
    cos, sin = np.cos(ang).astype(np.float32), np.sin(ang).astype(np.float32)
    return (jnp.asarray(np.concatenate([cos, cos], axis=1)), jnp.asarray(np.concatenate([-sin, sin], axis=1)))


def _qk_prep(proj, nw, cos, sin):
    tm = 256

    def body(p_ref, w_ref, cos_ref, sin_ref, o_ref):
        cv, sv = cos_ref[...], sin_ref[...]
        for h in range(NH):
            sl = slice(h * HD, (h + 1) * HD)
            xv = p_ref[:, sl]
            r = lax.rsqrt(jnp.mean(xv * xv, axis=-1, keepdims=True) + EPS)
            z = xv * r * w_ref[:, sl]
            if h < NHA:
                z = z * cv + pltpu.roll(z, 64, 1) * sv
            o_ref[:, sl] = z.astype(BF16)

    return pl.pallas_call(
        body, name="qk_prep", out_shape=SDS((T, 2 * D), BF16), grid=(T // tm, 2),
        in_specs=[pl.BlockSpec((tm, D), lambda i, j: (i, j)),
                  pl.BlockSpec((None, 1, D), lambda i, j: (j, 0, 0)),
                  pl.BlockSpec((tm, HD), lambda i, j: (i, 0)),
                  pl.BlockSpec((tm, HD), lambda i, j: (i, 0))],
        out_specs=pl.BlockSpec((tm, D), lambda i, j: (i, j)),
        compiler_params=_params(2))(proj, nw, cos, sin)


def _band_mask(q0, m):
    ii = lax.broadcasted_iota(jnp.int32, (128, 256), 0)
    jj = lax.broadcasted_iota(jnp.int32, (128, 256), 1)
    rel = jj - ii
    kpos = jj + (q0 - 64)
    return (rel >= 0) & (rel <= 128) & (kpos >= 0) & (kpos < m)


def _fill_padded(dst, src, m):
    zeros = jnp.zeros((64, HD), dst.dtype)
    dst[0:64, :] = zeros
    dst[64 + m:128 + m, :] = zeros
    dst[64:64 + m, :] = src.astype(dst.dtype)


def _group_views(qkn, proj, g):
    m = T // DILS[g]
    cols = (qkn[:, g * 512:(g + 1) * 512], qkn[:, D + g * 512:D + (g + 1) * 512],
            proj[:, 2 * D + g * 512:2 * D + (g + 1) * 512])
    return [a.reshape(m, DILS[g] * 512) for a in cols]


def _heads_per_step(m):
    return 4 if m <= 512 else 1


def _attn_a_fwd(qkn, proj, g):
    dil = DILS[g]
    m = T // dil
    nb = m // 128
    hp = _heads_per_step(m)

    def body(q_ref, k_ref, v_ref, o_ref, l_ref, kp, vp):
        for hh in range(hp):
            sl = slice(hh * HD, (hh + 1) * HD)
            _fill_padded(kp, k_ref[:, sl], m)
            _fill_padded(vp, v_ref[:, sl], m)

            def block(b, carry):
                q0 = pl.multiple_of(b * 128, 128)
                kw, vw = kp[pl.ds(q0, 256), :], vp[pl.ds(q0, 256), :]
                s = _dot_nt(q_ref[pl.ds(q0, 128), sl], kw) * SCALE
                s = jnp.where(_band_mask(q0, m), s, NEG)
                mx = jnp.max(s, axis=-1, keepdims=True)
                e = jnp.exp(s - mx)
                den = jnp.sum(e, axis=-1, keepdims=True)
                o_ref[pl.ds(q0, 128), sl] = _dot((e / den).astype(BF16), vw)
                l_ref[pl.ds(q0, 128), sl] = jnp.broadcast_to(mx + jnp.log(den), (128, HD))
                return carry

            lax.fori_loop(0, nb, block, 0, unroll=min(nb, 2))

    blk = pl.BlockSpec((m, hp * HD), lambda h, r: (0, r * (4 // hp) + h))
    o, lse = pl.pallas_call(
        body, name=f"attn_a_fwd_{g}", out_shape=[SDS((m, dil * 512), F32)] * 2, grid=(4 // hp, dil),
        in_specs=[blk] * 3, out_specs=[blk] * 2,
        scratch_shapes=[pltpu.VMEM((m + 128, HD), BF16), pltpu.VMEM((m + 128, HD), BF16)],
        compiler_params=_params(2))(*_group_views(qkn, proj, g))
    return o.reshape(T, 512), lse.reshape(T, 512)


def _nbr_window(r):
    start = jnp.clip(r - WIN_R // 2, 0, T // GRID_W - WIN_R)
    return start, start - r + (WIN_R - 1)


def _rpb_rows(rpb):
    zeros = jnp.zeros((4, 14, 33), F32)
    a, b = rpb[:, :14], rpb[:, 1:15]
    rows = jnp.concatenate([a[:, :, 15:31], zeros, b, zeros, a[:, :, 0:15]], axis=2)
    return jnp.pad(rows, ((0, 0), (0, 2), (0, 0)))


def _attn_b_fwd(qkn, proj, rpb_rows):
    def body(r_ref, q_ref, k_ref, v_ref, o_ref, l_ref, bias_ref, vb, pair):
        qc = lax.broadcasted_iota(jnp.int32, (GRID_W, 512), 0)
        kc = lax.broadcasted_iota(jnp.int32, (GRID_W, 512), 1) & (GRID_W - 1)
        cs = jnp.clip(qc - WIN_C // 2, 0, GRID_W - WIN_C)
        colmask = (kc >= cs) & (kc < cs + WIN_C)
        for d in range(14):
            pair[d] = pltpu.roll(jnp.broadcast_to(r_ref[d:d + 1, :], (GRID_W, HD)), 0, 1, stride=1, stride_axis=0)
        for off in range(8):
            rows = jnp.concatenate([pair[off + 2 * jj] for jj in range(4)], axis=1)
            bias_ref[off] = jnp.where(colmask, rows, NEG)
        vb[...] = v_ref[...].astype(BF16)

        def row(r, carry):
            start, off = _nbr_window(r)
            q0 = pl.multiple_of(r * GRID_W, GRID_W)
            k0 = pl.multiple_of(start * GRID_W, GRID_W)
            s = _dot_nt(q_ref[pl.ds(q0, GRID_W), :], k_ref[pl.ds(k0, 512), :]) * SCALE + bias_ref[off]
            mx = jnp.max(s, axis=-1, keepdims=True)
            e = jnp.exp(s - mx)
            den = jnp.sum(e, axis=-1, keepdims=True)
            o_ref[pl.ds(q0, GRID_W), :] = _dot((e / den).astype(BF16), vb[pl.ds(k0, 512), :])
            l_ref[pl.ds(q0, GRID_W), :] = jnp.broadcast_to(mx + jnp.log(den), (GRID_W, HD))
            return carry

        lax.fori_loop(0, T // GRID_W, row, 0, unroll=2)

    return pl.pallas_call(
        body, name="attn_b_fwd",
        out_shape=[SDS((T, 512), F32), SDS((T, 512), F32), SDS((4, 8, GRID_W, 512), F32)], grid=(4,),
        in_specs=[pl.BlockSpec((None, 16, HD), lambda h: (h, 0, 0)),
                  pl.BlockSpec((T, HD), lambda h: (0, NHA + h)),
                  pl.BlockSpec((T, HD), lambda h: (0, NH + NHA + h)),
                  pl.BlockSpec((T, HD), lambda h: (0, 2 * NH + NHA + h))],
        out_specs=[pl.BlockSpec((T, HD), lambda h: (0, h)), pl.BlockSpec((T, HD), lambda h: (0, h)),
                   pl.BlockSpec((None, 8, GRID_W, 512), lambda h: (h, 0, 0, 0))],
        scratch_shapes=[pltpu.VMEM((T, HD), BF16), pltpu.VMEM((14, GRID_W, HD), F32)],
        compiler_params=_params(1))(rpb_rows, qkn, qkn, proj)


def _comb_fwd(os, ls):
    tm = 512

    def body(o0, o1, o2, l0, l1, l2, oa_ref, w0, w1, w2):
        lv = [l0[...], l1[...], l2[...]]
        mx = jnp.maximum(jnp.maximum(lv[0], lv[1]), lv[2])
        ev = [jnp.exp(l - mx) for l in lv]
        den = ev[0] + ev[1] + ev[2]
        wv = [e / den for e in ev]
        oa_ref[...] = (wv[0] * o0[...] + wv[1] * o1[...] + wv[2] * o2[...]).astype(BF16)
        w0[...], w1[...], w2[...] = wv

    spec = pl.BlockSpec((tm, 512), lambda i: (i, 0))
    return pl.pallas_call(
        body, name="comb_fwd", out_shape=[SDS((T, 512), BF16)] + [SDS((T, 512), F32)] * 3, grid=(T // tm,),
        in_specs=[spec] * 6, out_specs=[spec] * 4, compiler_params=_params(1))(*os, *ls)


def _mix_fwd(oa, ob, proj, b_gate, wpa, wpb):
    tm = 512

    def body(oa_ref, ob_ref, ga_ref, gb_ref, ba_ref, bb_ref, wpa_ref, wpb_ref, mixed_ref, ob16_ref):
        oav = oa_ref[...]
        obv = ob_ref[...].astype(BF16)
        ob16_ref[...] = obv
        for s in range(NSH):
            sl = slice(s * 512, (s + 1) * 512)
            ga = _sigmoid(ga_ref[:, sl] + ba_ref[:, sl])
            gb = _sigmoid(gb_ref[:, sl] + bb_ref[:, sl])
            mixed_ref[:, sl] = (ga * _dot(oav, wpa_ref[s]) + gb * _dot(obv, wpb_ref[s])).astype(BF16)

    row = lambda w: pl.BlockSpec((tm, w), lambda i: (i, 0))
    return pl.pallas_call(
        body, name="mix_fwd", out_shape=[SDS((T, D), BF16), SDS((T, 512), BF16)], grid=(T // tm,),
        in_specs=[row(512), row(512),
                  pl.BlockSpec((tm, D), lambda i: (i, 3)), pl.BlockSpec((tm, D), lambda i: (i, 4)),
                  pl.BlockSpec((1, D), lambda i: (0, 0)), pl.BlockSpec((1, D), lambda i: (0, 1)),
                  _resident((NSH, 512, 512), lambda i: (0, 0, 0)), _resident((NSH, 512, 512), lambda i: (0, 0, 0))],
        out_specs=[row(D), row(512)], compiler_params=_params(1))(oa, ob, proj, proj, b_gate, b_gate, wpa, wpb)


def _out_proj_fwd(mixed, w_out, x, g):
    tm = 512

    def body(m_ref, w_ref, x_ref, g_ref, h1_ref, hn_ref):
        h1 = x_ref[...] + _dot(m_ref[...], w_ref[...])
        h1_ref[...] = h1
        r = lax.rsqrt(jnp.mean(h1 * h1, axis=-1, keepdims=True) + EPS)
        hn_ref[...] = (h1 * r * g_ref[...]).astype(BF16)

    row = pl.BlockSpec((tm, D), lambda i: (i, 0))
    return pl.pallas_call(
        body, name="out_proj_fwd", out_shape=[SDS((T, D), F32), SDS((T, D), BF16)], grid=(T // tm,),
        in_specs=[row, _resident((D, D), lambda i: (0, 0)), row, pl.BlockSpec((1, D), lambda i: (0, 0))],
        out_specs=[row, row], compiler_params=_params(1))(mixed, w_out, x, g)


def _ffn_up(hn, w_up):
    tm, tn = T, 512
    per = (DFF // NSH) // tn

    def body(h_ref, w_ref, a_ref, u_ref):
        uv = jnp.maximum(_dot(h_ref[...], w_ref[...]), 0.0)
        a_ref[...] = (uv * uv).astype(BF16)
        u_ref[...] = uv.astype(BF16)

    out = pl.BlockSpec((tm, tn), lambda i, j: (i, j))
    return pl.pallas_call(
        body, name="ffn_up", out_shape=[SDS((T, DFF), BF16)] * 2, grid=(T // tm, DFF // tn),
        in_specs=[pl.BlockSpec((tm, D), lambda i, j: (i, 0)),
                  pl.BlockSpec((None, D, tn), lambda i, j: (j // per, 0, j % per))],
        out_specs=[out, out], compiler_params=_params(2))(hn, w_up)


def _ffn_down_loss(u, w_down, h1, target):
    tm, tk = 512, 2048
    nk = DFF // tk

    def body(u_ref, w_ref, h1_ref, t_ref, dy_ref, dy16_ref, loss_ref, acc):
        k = pl.program_id(1)

        @pl.when(k == 0)
        def _():
            acc[...] = jnp.zeros_like(acc)

        acc[...] += _dot(u_ref[...], w_ref[...])

        @pl.when(k == nk - 1)
        def _():
            def chunk(r, sq):
                rows = pl.ds(pl.multiple_of(r * 16, 16), 16)
                err = acc[rows, :] + h1_ref[rows, :] - t_ref[rows, :]
                dy = err * (1.0 / D)
                dy_ref[rows, :] = dy
                dy16_ref[rows, :] = dy.astype(BF16)
                return sq + err * err

            sq = lax.fori_loop(0, tm // 16, chunk, jnp.zeros((16, D), F32), unroll=2)
            part = 0.5 * jnp.sum(jnp.mean(sq, axis=-1, keepdims=True), axis=0, keepdims=True)
            loss_ref[...] = jnp.broadcast_to(part, (8, 128))

    row = pl.BlockSpec((tm, D), lambda i, k: (i, 0))
    once = _resident((tm, D), lambda i, k: (i, 0))
    return pl.pallas_call(
        body, name="ffn_down_loss",
        out_shape=[SDS((T, D), F32), SDS((T, D), BF16), SDS((T // tm, 8, 128), F32)], grid=(T // tm, nk),
        in_specs=[pl.BlockSpec((tm, tk), lambda i, k: (i, k)), pl.BlockSpec((tk, D), lambda i, k: (k, 0)), once, once],
        out_specs=[row, row, pl.BlockSpec((None, 8, 128), lambda i, k: (i, 0, 0))],
        scratch_shapes=[pltpu.VMEM((tm, D), F32)], compiler_params=_params(2))(u, w_down, h1, target)


def _ffn_down_bwd(dy16, w_down, u, deps=()):
    tm, tn = T, 512

    def body(dy_ref, w_ref, u_ref, du_ref):
        uv = u_ref[...].astype(F32)
        du_ref[...] = jnp.where(uv > 0.0, 2.0 * uv * _dot_nt(dy_ref[...], w_ref[...]), 0.0).astype(BF16)

    return pl.pallas_call(
        _after(body, deps), name="ffn_down_bwd", out_shape=SDS((T, DFF), BF16), grid=(T // tm, DFF // tn),
        in_specs=[DEP_SPEC] * len(deps) + [
            pl.BlockSpec((tm, D), lambda i, j: (i, 0)), pl.BlockSpec((tn, D), lambda i, j: (j, 0)),
            pl.BlockSpec((tm, tn), lambda i, j: (i, j))],
        out_specs=pl.BlockSpec((tm, tn), lambda i, j: (i, j)), compiler_params=_params(2))(*deps, dy16, w_down, u)


def _norm_bwd(xv, dz_in, g):
    r = lax.rsqrt(jnp.mean(xv * xv, axis=-1, keepdims=True) + EPS)
    dg = jnp.sum(xv * r * dz_in, axis=0, keepdims=True)
    dz = dz_in * g
    dx = r * dz - xv * (r * r * r) * jnp.mean(xv * dz, axis=-1, keepdims=True)
    return dx, dg


def _ffn_up_bwd(du, w_up, h1, dy, g, deps=()):
    tm, tk = 512, 1024
    per = (DFF // NSH) // tk
    nk = DFF // tk

    def body(du_ref, w_ref, h1_ref, dy_ref, g_ref, dh1_ref, dh16_ref, dg_ref, acc):
        i, k = pl.program_id(0), pl.program_id(1)

        @pl.when(k == 0)
        def _():
            acc[...] = jnp.zeros_like(acc)

        @pl.when((k == 0) & (i == 0))
        def _():
            dg_ref[...] = jnp.zeros_like(dg_ref)

        acc[...] += _dot_nt(du_ref[...], w_ref[...])

        @pl.when(k == nk - 1)
        def _():
            dx, dg = _norm_bwd(h1_ref[...], acc[...], g_ref[...])
            dh1 = dy_ref[...] + dx
            dh1_ref[...] = dh1
            dh16_ref[...] = dh1.astype(BF16)
            dg_ref[...] += dg

    row = pl.BlockSpec((tm, D), lambda i, k: (i, 0))
    vec = pl.BlockSpec((1, D), lambda i, k: (0, 0))
    return pl.pallas_call(
        _after(body, deps), name="ffn_up_bwd", out_shape=[SDS((T, D), F32), SDS((T, D), BF16), SDS((1, D), F32)],
        grid=(T // tm, nk),
        in_specs=[DEP_SPEC] * len(deps) + [
            pl.BlockSpec((tm, tk), lambda i, k: (i, k)),
            pl.BlockSpec((None, D, tk), lambda i, k: (k // per, 0, k % per)), row, row, vec],
        out_specs=[row, row, vec], scratch_shapes=[pltpu.VMEM((tm, D), F32)],
        compiler_params=_params(2))(*deps, du, w_up, h1, dy, g)


def _mix_bwd(dh16, w_out, oa, ob16, proj, b_gate, wpa, wpb):
    tm = 256

    def body(dh_ref, wo_ref, oa_ref, ob_ref, ga_ref, gb_ref, ba_ref, bb_ref, wpa_ref, wpb_ref,
             dya_ref, dyb_ref, dga_ref, dgb_ref, doa_ref, dob_ref, dba_ref, dbb_ref):
        @pl.when(pl.program_id(0) == 0)
        def _():
            dba_ref[...] = jnp.zeros_like(dba_ref)
            dbb_ref[...] = jnp.zeros_like(dbb_ref)

        oav, obv = oa_ref[...], ob_ref[...]
        doa = jnp.zeros((tm, 512), F32)
        dob = jnp.zeros((tm, 512), F32)
        for s in range(NSH):
            sl = slice(s * 512, (s + 1) * 512)
            dm = _dot_nt(dh_ref[...], wo_ref[sl, :])
            ga = _sigmoid(ga_ref[:, sl] + ba_ref[:, sl])
            gb = _sigmoid(gb_ref[:, sl] + bb_ref[:, sl])
            dya = (dm * ga).astype(BF16)
            dyb = (dm * gb).astype(BF16)
            dza = dm * _dot(oav, wpa_ref[s]) * ga * (1.0 - ga)
            dzb = dm * _dot(obv, wpb_ref[s]) * gb * (1.0 - gb)
            dya_ref[:, sl], dyb_ref[:, sl] = dya, dyb
            dga_ref[:, sl], dgb_ref[:, sl] = dza.astype(BF16), dzb.astype(BF16)
            dba_ref[:, sl] += jnp.sum(dza, axis=0, keepdims=True)
            dbb_ref[:, sl] += jnp.sum(dzb, axis=0, keepdims=True)
            doa += _dot_nt(dya, wpa_ref[s])
            dob += _dot_nt(dyb, wpb_ref[s])
        doa_ref[...], dob_ref[...] = doa, dob

    row = lambda w: pl.BlockSpec((tm, w), lambda i: (i, 0))
    vec = pl.BlockSpec((1, D), lambda i: (0, 0))
    wp = _resident((NSH, 512, 512), lambda i: (0, 0, 0))
    return pl.pallas_call(
        body, name="mix_bwd",
        out_shape=[SDS((T, D), BF16)] * 4 + [SDS((T, 512), F32)] * 2 + [SDS((1, D), F32)] * 2, grid=(T // tm,),
        in_specs=[row(D), _resident((D, D), lambda i: (0, 0)), row(512), row(512),
                  pl.BlockSpec((tm, D), lambda i: (i, 3)), pl.BlockSpec((tm, D), lambda i: (i, 4)),
                  pl.BlockSpec((1, D), lambda i: (0, 0)), pl.BlockSpec((1, D), lambda i: (0, 1)), wp, wp],
        out_specs=[row(D)] * 4 + [row(512)] * 2 + [vec] * 2,
        compiler_params=_params(1))(dh16, w_out, oa, ob16, proj, proj, b_gate, b_gate, wpa, wpb)


def _comb_bwd(doa, os, ws, deps=()):
    tm = 512

    def body(d_ref, o0, o1, o2, w0, w1, w2, cc_ref):
        prod = d_ref[...] * (w0[...] * o0[...] + w1[...] * o1[...] + w2[...] * o2[...])
        for h in range(4):
            sl = slice(h * HD, (h + 1) * HD)
            cc_ref[:, sl] = jnp.broadcast_to(jnp.sum(prod[:, sl], axis=-1, keepdims=True), (tm, HD))

    spec = pl.BlockSpec((tm, 512), lambda i: (i, 0))
    return pl.pallas_call(
        _after(body, deps), name="comb_bwd", out_shape=SDS((T, 512), F32), grid=(T // tm,),
        in_specs=[DEP_SPEC] * len(deps) + [spec] * 7, out_specs=spec,
        compiler_params=_params(1))(*deps, doa, *os, *ws)


def _attn_a_bwd(qkn, proj, doa, lse, w, cc, g):
    dil = DILS[g]
    m = T // dil
    nb = m // 128
    hp = _heads_per_step(m)

    def body(q_ref, k_ref, v_ref, d_ref, l_ref, w_ref, c_ref, dqk_ref, dv_ref, kp, vp, dkp, dvp):
        for hh in range(hp):
            sl = slice(hh * HD, (hh + 1) * HD)
            _fill_padded(kp, k_ref[:, sl], m)
            _fill_padded(vp, v_ref[:, sl], m)
            dkp[...] = jnp.zeros_like(dkp)
            dvp[...] = jnp.zeros_like(dvp)

            def block(b, carry):
                q0 = pl.multiple_of(b * 128, 128)
                rows = pl.ds(q0, 128)
                win = pl.ds(q0, 256)
                qb, kw, vw = q_ref[rows, sl], kp[win, :], vp[win, :]
                s = _dot_nt(qb, kw) * SCALE
                s = jnp.where(_band_mask(q0, m), s, NEG)
                wp = _wide(w_ref[rows, sl], 2) * jnp.exp(s - _wide(l_ref[rows, sl], 2))
                dob = d_ref[rows, sl].astype(BF16)
                ds = (wp * (_dot_nt(dob, vw) - _wide(c_ref[rows, sl], 2))).astype(BF16)
                dqk_ref[0, rows, sl] = _dot(ds, kw) * SCALE
                dkp[win, :] += _dot_tn(ds, qb) * SCALE
                dvp[win, :] += _dot_tn(wp.astype(BF16), dob)
                return carry

            lax.fori_loop(0, nb, block, 0, unroll=min(nb, 2))
            dqk_ref[1, :, sl] = dkp[64:64 + m, :]
            dv_ref[:, sl] = dvp[64:64 + m, :]

    blk = pl.BlockSpec((m, hp * HD), lambda h, r: (0, r * (4 // hp) + h))
    view = lambda a: a.reshape(m, dil * 512)
    dqk, dv = pl.pallas_call(
        body, name=f"attn_a_bwd_{g}", out_shape=[SDS((2, m, dil * 512), F32), SDS((m, dil * 512), F32)],
        grid=(4 // hp, dil), in_specs=[blk] * 7,
        out_specs=[pl.BlockSpec((2, m, hp * HD), lambda h, r: (0, 0, r * (4 // hp) + h)), blk],
        scratch_shapes=[pltpu.VMEM((m + 128, HD), BF16), pltpu.VMEM((m + 128, HD), BF16),
                        pltpu.VMEM((m + 128, HD), F32), pltpu.VMEM((m + 128, HD), F32)],
        compiler_params=_params(2))(*_group_views(qkn, proj, g), view(doa), view(lse), view(w), view(cc))
    return dqk.reshape(2, T, 512), dv.reshape(T, 512)


def _attn_b_bwd(qkn, proj, dob, ob, lse, bias, deps=()):
    def body(q_ref, k_ref, v_ref, d_ref, o_ref, l_ref, bias_ref, dqk_ref, dv_ref, drpb_ref, vb, dk_acc, dv_acc, a_acc):
        vb[...] = v_ref[...].astype(BF16)
        dk_acc[...] = jnp.zeros_like(dk_acc)
        dv_acc[...] = jnp.zeros_like(dv_acc)
        a_acc[...] = jnp.zeros_like(a_acc)

        def row(r, carry):
            start, off = _nbr_window(r)
            rows = pl.ds(pl.multiple_of(r * GRID_W, GRID_W), GRID_W)
            win = pl.ds(pl.multiple_of(start * GRID_W, GRID_W), 512)
            qr, kw, vw = q_ref[rows, :], k_ref[win, :], vb[win, :]
            s = _dot_nt(qr, kw) * SCALE + bias_ref[off]
            p = jnp.exp(s - _wide(l_ref[rows, :], 4))
            dov = d_ref[rows, :]
            delta = jnp.sum(dov * o_ref[rows, :], axis=-1, keepdims=True)
            do16 = dov.astype(BF16)
            ds = p * (_dot_nt(do16, vw) - delta)
            a_acc[off] += ds
            ds16 = ds.astype(BF16)
            dqk_ref[0, rows, :] = _dot(ds16, kw) * SCALE
            dk_acc[win, :] += _dot_tn(ds16, qr) * SCALE
            dv_acc[win, :] += _dot_tn(p.astype(BF16), do16)
            return carry

        lax.fori_loop(0, T // GRID_W, row, 0, unroll=2)
        dqk_ref[1] = dk_acc[...]
        dv_ref[...] = dv_acc[...]

        lane = lax.broadcasted_iota(jnp.int32, (16, HD), 1)
        rowi = lax.broadcasted_iota(jnp.int32, (16, HD), 0)
        low = (lane >= GRID_W - WIN_C) & (lane < GRID_W + WIN_C - 1)
        high = (lane >= HD - WIN_C) | (lane < WIN_C - 1)
        flip = (lax.broadcasted_iota(jnp.int32, (GRID_W, GRID_W), 0)
                + lax.broadcasted_iota(jnp.int32, (GRID_W, GRID_W), 1) == GRID_W - 1).astype(BF16)
        out = jnp.zeros((16, HD), F32)
        for d in range(14):
            acc = None
            for off in range(8):
                if 0 <= d - off <= 6 and (d - off) % 2 == 0:
                    jj = (d - off) // 2
                    piece = a_acc[off, :, jj * HD:(jj + 1) * HD]
                    acc = piece if acc is None else acc + piece
            hi = acc.astype(BF16)
            lo = (acc - hi.astype(F32)).astype(BF16)
            rev = _dot(flip, hi) + _dot(flip, lo)
            v = jnp.sum(pltpu.roll(rev, 0, 1, stride=1, stride_axis=0), axis=0, keepdims=True)
            v = jnp.broadcast_to(v, (16, HD))
            out = out + jnp.where((rowi == d) & low, v, 0.0)
            out = out + jnp.where(rowi == d + 1, pltpu.roll(jnp.where(high, v, 0.0), GRID_W, 1), 0.0)
        drpb_ref[...] = out

    blk = pl.BlockSpec((T, HD), lambda h: (0, h))
    return pl.pallas_call(
        _after(body, deps), name="attn_b_bwd",
        out_shape=[SDS((2, T, 512), F32), SDS((T, 512), F32), SDS((4, 16, HD), F32)], grid=(4,),
        in_specs=[DEP_SPEC] * len(deps) + [
            pl.BlockSpec((T, HD), lambda h: (0, NHA + h)),
            pl.BlockSpec((T, HD), lambda h: (0, NH + NHA + h)),
            pl.BlockSpec((T, HD), lambda h: (0, 2 * NH + NHA + h)), blk, blk, blk,
            pl.BlockSpec((None, 8, GRID_W, 512), lambda h: (h, 0, 0, 0))],
        out_specs=[pl.BlockSpec((2, T, HD), lambda h: (0, 0, h)), blk,
                   pl.BlockSpec((None, 16, HD), lambda h: (h, 0, 0))],
        scratch_shapes=[pltpu.VMEM((T, HD), BF16), pltpu.VMEM((T, HD), F32), pltpu.VMEM((T, HD), F32),
                        pltpu.VMEM((8, GRID_W, 512), F32)],
        compiler_params=_params(1))(*deps, qkn, qkn, proj, dob, ob, lse, bias)


def _qk_bwd(proj, nw, cos, sin, dqk_groups, dqk_b, dvs, dga, dgb):
    tm = 256

    def body(p_ref, w_ref, cos_ref, sin_ref, d0, d1, d2, d3, v0, v1, v2, v3, ga_ref, gb_ref, o_ref, dn_ref):
        j, i = pl.program_id(0), pl.program_id(1)

        @pl.when((j < 2) & (i == 0))
        def _():
            dn_ref[...] = jnp.zeros_like(dn_ref)

        @pl.when(j < 2)
        def _():
            cv, sv = cos_ref[...], sin_ref[...]
            srcs = (d0, d1, d2, d3)
            dna = jnp.zeros((1, HD), F32)
            dnb = jnp.zeros((1, HD), F32)
            for h in range(NH):
                sl = slice(h * HD, (h + 1) * HD)
                dz = srcs[h // 4][:, (h % 4) * HD:(h % 4 + 1) * HD]
                if h < NHA:
                    dz = dz * cv + pltpu.roll(dz * sv, 64, 1)
                dx, dg = _norm_bwd(p_ref[:, sl], dz, w_ref[:, sl])
                o_ref[:, sl] = dx.astype(BF16)
                if h < NHA:
                    dna += dg
                else:
                    dnb += dg
            dn_ref[0:1, :] += dna
            dn_ref[1:2, :] += dnb

        @pl.when(j == 2)
        def _():
            for s, v_ref in enumerate((v0, v1, v2, v3)):
                o_ref[:, s * 512:(s + 1) * 512] = v_ref[...].astype(BF16)

        @pl.when(j == 3)
        def _():
            o_ref[...] = ga_ref[...]

        @pl.when(j == 4)
        def _():
            o_ref[...] = gb_ref[...]

    def rows(used):
        return lambda j, i: (jnp.where(used(j), i, 0), 0)

    qk = lambda j: j < 2
    dspec = pl.BlockSpec((None, tm, 512), lambda j, i: (jnp.minimum(j, 1), jnp.where(j < 2, i, 0), 0))
    vspec = pl.BlockSpec((tm, 512), rows(lambda j: j == 2))
    return pl.pallas_call(
        body, name="qk_bwd", out_shape=[SDS((T, DIN), BF16), SDS((2, 8, HD), F32)], grid=(5, T // tm),
        in_specs=[pl.BlockSpec((tm, D), lambda j, i: (jnp.where(j < 2, i, 0), jnp.minimum(j, 1))),
                  pl.BlockSpec((None, 1, D), lambda j, i: (jnp.minimum(j, 1), 0, 0)),
                  pl.BlockSpec((tm, HD), rows(qk)), pl.BlockSpec((tm, HD), rows(qk)),
                  dspec, dspec, dspec, dspec, vspec, vspec, vspec, vspec,
                  pl.BlockSpec((tm, D), rows(lambda j: j == 3)), pl.BlockSpec((tm, D), rows(lambda j: j == 4))],
        out_specs=[pl.BlockSpec((tm, D), lambda j, i: (i, j)),
                   pl.BlockSpec((None, 8, HD), lambda j, i: (jnp.minimum(j, 1), 0, 0))],
        compiler_params=_params(2))(proj, nw, cos, sin, *dqk_groups, dqk_b, *dvs, dga, dgb)


def _in_proj_bwd(dproj, w_in, x, dh1, g, deps=()):
    tm, tk = 512, 1280
    per = (DIN // NSH) // tk
    nk = DIN // tk

    def body(dp_ref, w_ref, x_ref, dh_ref, g_ref, dx_ref, dg_ref, acc):
        i, k = pl.program_id(0), pl.program_id(1)

        @pl.when(k == 0)
        def _():
            acc[...] = jnp.zeros_like(acc)

        @pl.when((k == 0) & (i == 0))
        def _():
            dg_ref[...] = jnp.zeros_like(dg_ref)

        acc[...] += _dot_nt(dp_ref[...], w_ref[...])

        @pl.when(k == nk - 1)
        def _():
            dx, dg = _norm_bwd(x_ref[...], acc[...], g_ref[...])
            dx_ref[...] = dh_ref[...] + dx
            dg_ref[...] += dg

    row = pl.BlockSpec((tm, D), lambda i, k: (i, 0))
    vec = pl.BlockSpec((1, D), lambda i, k: (0, 0))
    return pl.pallas_call(
        _after(body, deps), name="in_proj_bwd", out_shape=[SDS((T, D), F32), SDS((1, D), F32)], grid=(T // tm, nk),
        in_specs=[DEP_SPEC] * len(deps) + [
            pl.BlockSpec((tm, tk), lambda i, k: (i, k)),
            pl.BlockSpec((None, D, tk), lambda i, k: (k // per, 0, k % per)), row, row, vec],
        out_specs=[row, vec], scratch_shapes=[pltpu.VMEM((tm, D), F32)],
        compiler_params=_params(2))(*deps, dproj, w_in, x, dh1, g)


def _grad_w(name, a, g, shard_rows, rows, cols, tr, tc):
    ni, nj = rows // tr, cols // tc
    if shard_rows:
        a_map, g_map = (lambda s, i, j: (0, s * ni + i)), (lambda s, i, j: (0, j))
    else:
        a_map, g_map = (lambda s, i, j: (0, i)), (lambda s, i, j: (0, s * nj + j))

    def body(a_ref, g_ref, o_ref):
        o_ref[...] = _dot_tn(a_ref[...], g_ref[...]).astype(BF16)

    return pl.pallas_call(
        body, name=name, out_shape=SDS((NSH, rows, cols), BF16), grid=(NSH, ni, nj),
        in_specs=[pl.BlockSpec((T, tr), a_map), pl.BlockSpec((T, tc), g_map)],
        out_specs=pl.BlockSpec((None, tr, tc), lambda s, i, j: (s, i, j)), compiler_params=_params(3))(a, g)


def _adamw(w, g, m, v):
    m = B1 * m + (1.0 - B1) * g
    v = B2 * v + (1.0 - B2) * (g * g)
    m_hat = m / (1.0 - B1 ** STEP)
    v_hat = v / (1.0 - B2 ** STEP)
    delta = -LR * (m_hat / (jnp.sqrt(v_hat) + AEPS) + WD * w)
    return delta, m, v


def _sum_halves(name, place, grads, theirs):
    _, rows, cols = theirs.shape
    tr = _row_tile(rows, cols, 1 << 20)

    def body(place_ref, a_ref, b_ref, o_ref):
        o_ref[...] = (a_ref[...].astype(F32) + b_ref[...].astype(F32)).astype(BF16)

    spec = pl.BlockSpec((None, tr, cols), lambda s, i, p: (s, i, 0))
    return pl.pallas_call(
        body, name=name, out_shape=SDS(theirs.shape, BF16),
        grid_spec=pltpu.PrefetchScalarGridSpec(
            num_scalar_prefetch=1, grid=(NSH, rows // tr),
            in_specs=[pl.BlockSpec((None, None, tr, cols), lambda s, i, p: (s, p[1], i, 0)), spec], out_specs=spec),
        compiler_params=_params(2))(place, grads, theirs)


def _sum_landed(name, place, part, landed):
    _, rows, cols = part.shape
    tr = _row_tile(rows, cols, 1 << 20)

    def body(place_ref, p_ref, l_ref, o_ref):
        o_ref[...] = ((p_ref[...].astype(F32) + l_ref[0].astype(F32)) + l_ref[1].astype(F32)) + l_ref[2].astype(F32)

    return pl.pallas_call(
        body, name=name, out_shape=SDS((2, rows, cols), F32),
        grid_spec=pltpu.PrefetchScalarGridSpec(
            num_scalar_prefetch=1, grid=(rows // tr,),
            in_specs=[pl.BlockSpec((None, tr, cols), lambda i, p: (p[0], i, 0)),
                      pl.BlockSpec((3, tr, cols), lambda i, p: (0, i, 0))],
            out_specs=pl.BlockSpec((None, tr, cols), lambda i, p: (p[1], i, 0))),
        compiler_params=_params(1))(place, part, landed)


def _adam_shard(name, g, w, m, v):
    rows, cols = w.shape
    tr = _row_tile(rows, cols, 1 << 19)

    def body(g_ref, w_ref, m_ref, v_ref, go_ref, d_ref, nm_ref, nv_ref):
        g = g_ref[...]
        go_ref[...] = g
        d_ref[...], nm_ref[...], nv_ref[...] = _adamw(w_ref[...], g, m_ref[...], v_ref[...])

    spec = pl.BlockSpec((tr, cols), lambda i: (i, 0))
    return pl.pallas_call(
        body, name=name, out_shape=[SDS((rows, cols), F32)] * 4, grid=(rows // tr,),
        in_specs=[spec] * 4, out_specs=[spec] * 4, compiler_params=_params(1))(g, w, m, v)


def _adam_small(gathered, w, m, v):
    def body(g_ref, w_ref, m_ref, v_ref, go_ref, d_ref, nm_ref, nv_ref):
        g = g_ref[0:SMALL_ROWS, :]
        for dev in range(1, 8):
            g = g + g_ref[dev * SMALL_ROWS:(dev + 1) * SMALL_ROWS, :]
        go_ref[...] = g
        d_ref[...], nm_ref[...], nv_ref[...] = _adamw(w_ref[...], g, m_ref[...], v_ref[...])

    return pl.pallas_call(body, name="adam_small", out_shape=[SDS((SMALL_ROWS, HD), F32)] * 4)(gathered, w, m, v)


SMALL = (("norm_mix", (1, D)), ("b_gate", (1, 2 * D)), ("q_norm_a", (1, HD)), ("k_norm_a", (1, HD)),
         ("q_norm_b", (1, HD)), ("k_norm_b", (1, HD)), ("rpb_b", (1, 4, 15, 31)), ("norm_ffn", (1, D)))


def _pack_small(vals):
    pieces = []
    for (name, shape), val in zip(SMALL, vals):
        flat = val.reshape(-1)
        pad = (-flat.shape[0]) % HD
        pieces.append(jnp.pad(flat, (0, pad)).reshape(-1, HD))
    packed = jnp.concatenate(pieces, axis=0)
    return jnp.pad(packed, ((0, SMALL_ROWS - packed.shape[0]), (0, 0)))


def _unpack_small(packed):
    out, row = [], 0
    for name, shape in SMALL:
        size = int(np.prod(shape))
        nrows = -(-size // HD)
        out.append(packed[row:row + nrows].reshape(-1)[:size].reshape(shape))
        row += nrows
    return out


def kernel(x, norm_mix, w_in, b_gate, q_norm_a, k_norm_a, q_norm_b, k_norm_b, rpb_b, w_proj_a, w_proj_b, w_out, norm_ffn, w_up, w_down, loss_target, m_norm_mix, m_w_in, m_b_gate, m_q_norm_a, m_k_norm_a, m_q_norm_b, m_k_norm_b, m_rpb_b, m_w_proj_a, m_w_proj_b, m_w_out, m_norm_ffn, m_w_up, m_w_down, v_norm_mix, v_w_in, v_b_gate, v_q_norm_a, v_k_norm_a, v_q_norm_b, v_k_norm_b, v_rpb_b, v_w_proj_a, v_w_proj_b, v_w_out, v_norm_ffn, v_w_up, v_w_down):
    big_names = ("w_in", "w_proj_a", "w_proj_b", "w_out", "w_up", "w_down")
    big_w = [a[0] for a in (w_in, w_proj_a, w_proj_b, w_out, w_up, w_down)]
    big_m = [a[0] for a in (m_w_in, m_w_proj_a, m_w_proj_b, m_w_out, m_w_up, m_w_down)]
    big_v = [a[0] for a in (v_w_in, v_w_proj_a, v_w_proj_b, v_w_out, v_w_up, v_w_down)]
    x2, target = x[0], loss_target[0]

    place = jnp.stack([2 * lax.axis_index("x") + lax.axis_index("y"), lax.axis_index("c")]).astype(jnp.int32)
    groups = ((0,), (1, 2, 3), (4,), (5,))
    started = []
    for j, grp in enumerate(groups):
        deps = (started[0][4],) if j else ()
        placed = [_cast_into_place(big_w[i], "cast_" + big_names[i], place, deps) for i in grp]
        started.append(_gather_start(f"gather_start_{j}", placed))

    def whole(fulls):
        return [f.reshape(NSH, 2 * f.shape[2], f.shape[3]) for f in fulls]

    def gathered(j, after):
        send, recv, _, fulls, _ = started[j]
        fulls = _gather_wait(f"gather_wait_{j}", send, recv, fulls, after)
        return whole(_gather_finish(f"gather_finish_{j}", fulls))

    def forward_begin(j, after):
        send, recv, _, fulls, _ = started[j]
        fulls = _gather_wait(f"gather_wait_{j}", send, recv, fulls, after)
        send, recv, _, fulls, token = _forward_start(f"forward_start_{j}", fulls)
        return (send, recv, fulls), token

    def forward_end(j, state, after):
        return whole(_forward_wait(f"forward_wait_{j}", *state, after))

    def as_halves(grads):
        return [g.reshape(NSH, 2, g.shape[1] // 2, g.shape[2]) for g in grads]

    def reduce_start(j, grads, theirs):
        parts = [_sum_halves(f"sum_halves_{j}_{i}", place, a, b) for i, (a, b) in enumerate(zip(grads, theirs))]
        send, recv, parts, lands, token = _reduce_start(f"reduce_start_{j}", parts)
        return (send, recv, parts, lands), token

    def reduce_begin(j, grads):
        grads = as_halves(grads)
        return reduce_start(j, grads, _reduce_exchange(f"reduce_exchange_{j}", grads))

    def exchange_begin(j, grads):
        send, recv, grads, lands, token = _exchange_start(f"exchange_start_{j}", as_halves(grads))
        return (send, recv, grads, lands), token

    def exchange_end(j, state, after):
        return reduce_start(j, *_exchange_wait(f"exchange_wait_{j}", *state, after))

    big_out = {}

    def share_begin(j, state, after):
        send, recv, parts, lands = state
        parts, lands = _reduce_wait(f"reduce_wait_{j}", send, recv, parts, lands, after)
        sums = [_sum_landed(f"sum_landed_{j}_{i}", place, p, l) for i, (p, l) in enumerate(zip(parts, lands))]
        send, recv, _, sums, token = _share_start(f"share_start_{j}", sums)
        return (send, recv, sums), token

    def share_end(j, state, after):
        for idx, g in zip(groups[j], _share_wait(f"share_wait_{j}", *state, after)):
            g = g.reshape(big_w[idx].shape)
            big_out[idx] = _adam_shard("adam_" + big_names[idx], g, big_w[idx], big_m[idx], big_v[idx])
        return big_out[groups[j][-1]][1]

    (win_f,) = gathered(0, tuple(s[4] for s in started[1:]))
    proj, xn = _norm_in_proj(x2, norm_mix, win_f)
    cos, sin = _rope_tables()
    nw = jnp.stack([jnp.concatenate([jnp.tile(q_norm_a, (1, NHA)), jnp.tile(q_norm_b, (1, NH - NHA))], axis=1),
                    jnp.concatenate([jnp.tile(k_norm_a, (1, NHA)), jnp.tile(k_norm_b, (1, NH - NHA))], axis=1)])
    qkn = _qk_prep(proj, nw, cos, sin)
    fw1, token = forward_begin(1, (qkn,))
    fwd_a = [_attn_a_fwd(qkn, proj, g) for g in range(3)]
    os, ls = [f[0] for f in fwd_a], [f[1] for f in fwd_a]
    fw2, token = forward_begin(2, (os[2], token))
    ob, lse_b, bias = _attn_b_fwd(qkn, proj, _rpb_rows(rpb_b[0]))
    oa, w0, w1, w2 = _comb_fwd(os, ls)
    ws = [w0, w1, w2]
    wpa_f, wpb_f, wout_f = forward_end(1, fw1, (oa, token))
    wout_f = wout_f.reshape(D, D)
    mixed, ob16 = _mix_fwd(oa, ob, proj, b_gate, wpa_f, wpb_f)
    h1, hn = _out_proj_fwd(mixed, wout_f, x2, norm_ffn)
    fw3, token = forward_begin(3, (h1,))
    (wup_f,) = forward_end(2, fw2, (hn, token))
    usq, u = _ffn_up(hn, wup_f)
    (wdown_f,) = forward_end(3, fw3, (u,))
    wdown_f = wdown_f.reshape(DFF, D)
    dy, dy16, loss_parts = _ffn_down_loss(usq, wdown_f, h1, target)
    loss = lax.psum(jnp.sum(loss_parts[:, 0, 0]), ("x", "y", "c"))

    g_down = _grad_w("grad_w_down", usq, dy16, True, DFF // NSH, D, 1024, 1024)
    ex_down, token = exchange_begin(3, [g_down])
    du = _ffn_down_bwd(dy16, wdown_f, u, deps=(token,))
    g_up = _grad_w("grad_w_up", hn, du, False, D, DFF // NSH, 1024, 1024)
    red_down, token = exchange_end(3, ex_down, (g_up,))
    ex_up, token_up = exchange_begin(2, [g_up])
    dh1, dh16, d_norm_ffn = _ffn_up_bwd(du, wup_f, h1, dy, norm_ffn, deps=(token, token_up))
    dya, dyb, dga, dgb, doa, dob, dba, dbb = _mix_bwd(dh16, wout_f, oa, ob16, proj, b_gate, wpa_f, wpb_f)
    g_out = _grad_w("grad_w_out", mixed, dh16, True, D // NSH, D, 512, 1024)
    g_pa = _grad_w("grad_w_proj_a", oa, dya, False, 512, 512, 512, 512)
    g_pb = _grad_w("grad_w_proj_b", ob16, dyb, False, 512, 512, 512, 512)
    red_up, token = exchange_end(2, ex_up, (g_out,))
    ex_mid, token_mid = exchange_begin(1, [g_pa, g_pb, g_out])
    cc = _comb_bwd(doa, os, ws, deps=(token, token_mid))
    bwd_a = [_attn_a_bwd(qkn, proj, doa, ls[g], ws[g], cc, g) for g in range(3)]
    red_mid, token = exchange_end(1, ex_mid, (bwd_a[2][1],))
    dqk_b, dv_b, drpb_t = _attn_b_bwd(qkn, proj, dob, ob, lse_b, bias, deps=(token,))
    dproj, dn = _qk_bwd(proj, nw, cos, sin, [b[0] for b in bwd_a], dqk_b, [b[1] for b in bwd_a] + [dv_b], dga, dgb)
    g_in = _grad_w("grad_w_in", xn, dproj, False, D, DIN // NSH, 1024, 1280)
    red_in, token = reduce_begin(0, [g_in])
    grad_x, d_norm_mix = _in_proj_bwd(dproj, win_f, x2, dh1, norm_mix, deps=(token,))

    sh_down, token = share_begin(3, red_down, (grad_x,))
    sh_up, token = share_begin(2, red_up, (token,))
    done = share_end(3, sh_down, (token,))
    sh_mid, token = share_begin(1, red_mid, (done,))
    done = share_end(2, sh_up, (token,))
    sh_in, token = share_begin(0, red_in, (done,))
    done = share_end(1, sh_mid, (token,))
    done = share_end(0, sh_in, (done,))

    d_rpb = drpb_t[:, :15, GRID_W - WIN_C:GRID_W + WIN_C - 1]
    small_g = [d_norm_mix, jnp.concatenate([dba, dbb], axis=1), dn[0, 0], dn[1, 0], dn[0, 1], dn[1, 1], d_rpb, d_norm_ffn]
    gathered_small = _allgather_small(_pack_small(small_g), done)
    small_w = (norm_mix, b_gate, q_norm_a, k_norm_a, q_norm_b, k_norm_b, rpb_b, norm_ffn)
    small_m = (m_norm_mix, m_b_gate, m_q_norm_a, m_k_norm_a, m_q_norm_b, m_k_norm_b, m_rpb_b, m_norm_ffn)
    small_v = (v_norm_mix, v_b_gate, v_q_norm_a, v_k_norm_a, v_q_norm_b, v_k_norm_b, v_rpb_b, v_norm_ffn)
    small_out = [_unpack_small(p) for p in
                 _adam_small(gathered_small, _pack_small(small_w), _pack_small(small_m), _pack_small(small_v))]

    order = ("norm_mix", "w_in", "b_gate", "q_norm_a", "k_norm_a", "q_norm_b", "k_norm_b", "rpb_b",
             "w_proj_a", "w_proj_b", "w_out", "norm_ffn", "w_up", "w_down")
    small_idx = {name: i for i, (name, _) in enumerate(SMALL)}
    outs = []
    for kind in range(4):
        for name in order:
            if name in small_idx:
                outs.append(small_out[kind][small_idx[name]])
            else:
                outs.append(big_out[big_names.index(name)][kind][None])
    return (loss, grad_x[None], *outs)
```

```python
import functools

import numpy as np
import jax
import jax.numpy as jnp
from jax import lax
from jax.experimental import pallas as pl
from jax.experimental.pallas import tpu as pltpu

F32, BF16 = jnp.float32, jnp.bfloat16
SDS = jax.ShapeDtypeStruct
MESH = pl.DeviceIdType.MESH

T = 2048
D = 2048
HD = 128
NH, NHA = 16, 12
DIN = 10240
DFF = 8192
NSH = 4
DILS = (1, 4, 16)
EPS = 1e-6
NEG = -1e30
SCALE = HD ** -0.5
GRID_W, WIN_R, WIN_C = 64, 8, 16
VMEM_LIMIT = 56 * 1024 * 1024
B1, B2, LR, AEPS, WD, STEP = 0.9, 0.999, 0.001, 1e-08, 0.01, 10
SMALL_ROWS = 88


def _dot(a, b):
    return jnp.dot(a, b, preferred_element_type=F32)


def _dot_nt(a, b):
    return lax.dot_general(a, b, (((1,), (1,)), ((), ())), preferred_element_type=F32)


def _dot_tn(a, b):
    return lax.dot_general(a, b, (((0,), (0,)), ((), ())), preferred_element_type=F32)


def _params(n):
    return pltpu.CompilerParams(dimension_semantics=("arbitrary",) * n, vmem_limit_bytes=VMEM_LIMIT)


def _resident(shape, index_map):
    return pl.BlockSpec(shape, index_map, pipeline_mode=pl.Buffered(1))


def _sigmoid(z):
    return 1.0 / (1.0 + jnp.exp(-z))


def _wide(v, n):
    return jnp.concatenate([v] * n, axis=1)


def _row_tile(rows, cols, elems):
    tr = 16
    while tr * 2 <= rows and tr * 2 * cols <= elems:
        tr *= 2
    return tr


def _place():
    x, y, c = lax.axis_index("x"), lax.axis_index("y"), lax.axis_index("c")
    peers = [(1 - x, y), (x, 1 - y), (1 - x, 1 - y)]
    return x, y, c, peers


def _cast_into_place(w, name, place, deps=()):
    rows, cols = w.shape
    hr = rows // 2
    tr = min(hr, 256)
    per = hr // tr

    def body(*refs):
        w_ref, o_ref = refs[-2:]
        o_ref[...] = w_ref[...].astype(BF16)

    return pl.pallas_call(
        body, name=name, out_shape=SDS((NSH, 2, hr, cols), BF16),
        grid_spec=pltpu.PrefetchScalarGridSpec(
            num_scalar_prefetch=1, grid=(2, per),
            in_specs=[DEP_SPEC] * len(deps) + [pl.BlockSpec((tr, cols), lambda h, i, p: (h * per + i, 0))],
            out_specs=pl.BlockSpec((None, None, tr, cols), lambda h, i, p: (p[0], h, i, 0))),
        compiler_params=_params(2))(place, *deps, w)


ANY_SPEC = pl.BlockSpec(memory_space=pl.ANY)
HBM_SPEC = pl.BlockSpec(memory_space=pltpu.HBM)
SEM_SPEC = pl.BlockSpec(memory_space=pltpu.SEMAPHORE)
DEP_SPEC = pl.BlockSpec((8, 128), lambda *_: (0, 0))
EFFECT = pltpu.SideEffectType.DATAFLOW_SIDE_EFFECTING


def _after(body, deps):
    n = len(deps)
    return (lambda *refs: body(*refs[n:])) if n else body


SIBLING_BARRIER = 1


def _split_start(name, srcs, lands, n_copies, issue, sibling_only=False):
    n, m = len(srcs), len(lands)

    def body(*refs):
        if sibling_only:
            x, y, c, _ = _place()
            barrier = pltpu.get_barrier_semaphore()
            pl.semaphore_signal(barrier, inc=1, device_id=(x, y, 1 - c), device_id_type=MESH)
            pl.semaphore_wait(barrier, 1)
        issue(refs[:n], refs[n:n + m], refs[n + m], refs[n + m + 1])
        refs[-1][...] = jnp.zeros((8, 128), F32)

    arrays = list(srcs) + list(lands)
    outs = pl.pallas_call(
        body, name=name,
        out_shape=(pltpu.SemaphoreType.DMA((n_copies,)), pltpu.SemaphoreType.DMA((n_copies,)),
                   *[pltpu.HBM(a.shape, a.dtype) for a in arrays], SDS((8, 128), F32)),
        in_specs=[HBM_SPEC] * (n + m),
        out_specs=(SEM_SPEC, SEM_SPEC, *[HBM_SPEC] * (n + m), pl.BlockSpec(memory_space=pltpu.VMEM)),
        input_output_aliases={i: 2 + i for i in range(n + m)},
        compiler_params=pltpu.CompilerParams(has_side_effects=EFFECT,
                                             collective_id=SIBLING_BARRIER if sibling_only else None),
    )(*[pltpu.with_memory_space_constraint(a, pltpu.HBM) for a in arrays])
    return outs[0], outs[1], list(outs[2:2 + n]), list(outs[2 + n:2 + n + m]), outs[-1]


def _split_wait(name, send_sems, recv_sems, srcs, lands, after, wait):
    n, m = len(srcs), len(lands)

    def body(*refs):
        wait(refs[:n], refs[n:n + m], refs[n + m], refs[n + m + 1])

    arrays = list(srcs) + list(lands)
    outs = pl.pallas_call(
        body, name=name, out_shape=[pltpu.HBM(a.shape, a.dtype) for a in arrays],
        in_specs=[HBM_SPEC] * (n + m) + [SEM_SPEC, SEM_SPEC] + [ANY_SPEC] * len(after),
        out_specs=[HBM_SPEC] * (n + m), input_output_aliases={i: i for i in range(n + m)},
        compiler_params=pltpu.CompilerParams(has_side_effects=EFFECT),
    )(*arrays, send_sems, recv_sems, *after)
    return list(outs[:n]), list(outs[n:])


def _gather_start(name, fulls):
    def issue(srcs, dsts, send_sems, recv_sems):
        x, y, c, peers = _place()
        for i in range(len(fulls)):
            mine = dsts[i].at[2 * x + y, c]
            for k, (px, py) in enumerate(peers):
                pltpu.make_async_remote_copy(
                    src_ref=mine, dst_ref=mine, send_sem=send_sems.at[3 * i + k],
                    recv_sem=recv_sems.at[3 * i + k], device_id=(px, py, c), device_id_type=MESH).start()

    return _split_start(name, [], fulls, 3 * len(fulls), issue)


def _gather_wait(name, send_sems, recv_sems, fulls, after):
    def wait(srcs, dsts, send_sems, recv_sems):
        x, y, c, peers = _place()
        for i in range(len(fulls)):
            for k, (px, py) in enumerate(peers):
                cp = pltpu.make_async_remote_copy(
                    src_ref=dsts[i].at[2 * x + y, c], dst_ref=dsts[i].at[2 * px + py, c],
                    send_sem=send_sems.at[3 * i + k], recv_sem=recv_sems.at[3 * i + k],
                    device_id=(px, py, c), device_id_type=MESH)
                cp.wait_send()
                cp.wait_recv()

    return _split_wait(name, send_sems, recv_sems, [], fulls, after, wait)[1]


def _gather_finish(name, fulls):
    n = len(fulls)

    def body(*refs):
        fin, fout = refs[:n], refs[n:2 * n]
        send_sems, recv_sems = refs[2 * n:]
        x, y, c, peers = _place()

        def copy(i, k, half):
            px, py = peers[k]
            return pltpu.make_async_remote_copy(
                src_ref=fin[i].at[2 * px + py, half], dst_ref=fout[i].at[2 * px + py, half],
                send_sem=send_sems.at[3 * i + k], recv_sem=recv_sems.at[3 * i + k],
                device_id=(x, y, 1 - c), device_id_type=MESH)

        sends = [copy(i, k, c) for i in range(n) for k in range(3)]
        for cp in sends:
            cp.start()
        for i in range(n):
            for k in range(3):
                copy(i, k, 1 - c).wait_recv()
        for cp in sends:
            cp.wait_send()

    return pl.pallas_call(
        body, name=name, out_shape=[SDS(f.shape, f.dtype) for f in fulls],
        in_specs=[ANY_SPEC] * n, out_specs=[ANY_SPEC] * n, input_output_aliases={i: i for i in range(n)},
        scratch_shapes=[pltpu.SemaphoreType.DMA((3 * n,)), pltpu.SemaphoreType.DMA((3 * n,))])(*fulls)


def _reduce_exchange(name, grads):
    n = len(grads)

    def body(*refs):
        ins, theirs = refs[:n], refs[n:2 * n]
        send_sems, recv_sems = refs[2 * n:]
        x, y, c, _ = _place()
        copies = []
        for i in range(n):
            cp = pltpu.make_async_remote_copy(
                src_ref=ins[i].at[:, 1 - c], dst_ref=theirs[i], send_sem=send_sems.at[i],
                recv_sem=recv_sems.at[i], device_id=(x, y, 1 - c), device_id_type=MESH)
            cp.start()
            copies.append(cp)
        for cp in copies:
            cp.wait_recv()
            cp.wait_send()

    return pl.pallas_call(
        body, name=name, out_shape=[SDS((NSH,) + g.shape[2:], g.dtype) for g in grads],
        in_specs=[ANY_SPEC] * n, out_specs=[ANY_SPEC] * n,
        scratch_shapes=[pltpu.SemaphoreType.DMA((n,)), pltpu.SemaphoreType.DMA((n,))])(*grads)


def _reduce_start(name, parts):
    lands = [lax.empty((3,) + p.shape[1:], p.dtype) for p in parts]

    def issue(srcs, dsts, send_sems, recv_sems):
        x, y, c, peers = _place()
        for i in range(len(parts)):
            for k, (px, py) in enumerate(peers):
                pltpu.make_async_remote_copy(
                    src_ref=srcs[i].at[2 * px + py], dst_ref=dsts[i].at[k], send_sem=send_sems.at[3 * i + k],
                    recv_sem=recv_sems.at[3 * i + k], device_id=(px, py, c), device_id_type=MESH).start()

    return _split_start(name, parts, lands, 3 * len(parts), issue)


def _reduce_wait(name, send_sems, recv_sems, parts, lands, after):
    def wait(srcs, dsts, send_sems, recv_sems):
        x, y, c, peers = _place()
        for i in range(len(parts)):
            for k, (px, py) in enumerate(peers):
                cp = pltpu.make_async_remote_copy(
                    src_ref=srcs[i].at[2 * px + py], dst_ref=dsts[i].at[k], send_sem=send_sems.at[3 * i + k],
                    recv_sem=recv_sems.at[3 * i + k], device_id=(px, py, c), device_id_type=MESH)
                cp.wait_send()
                cp.wait_recv()

    return _split_wait(name, send_sems, recv_sems, parts, lands, after, wait)


def _sibling_copy(src, dst, send_sems, recv_sems, k):
    x, y, c, _ = _place()
    return pltpu.make_async_remote_copy(src_ref=src, dst_ref=dst, send_sem=send_sems.at[k], recv_sem=recv_sems.at[k],
                                        device_id=(x, y, 1 - c), device_id_type=MESH)


def _forward_start(name, fulls):
    def issue(srcs, dsts, send_sems, recv_sems):
        x, y, c, peers = _place()
        for i in range(len(fulls)):
            for k, (px, py) in enumerate(peers):
                part = dsts[i].at[2 * px + py, c]
                _sibling_copy(part, part, send_sems, recv_sems, 3 * i + k).start()

    return _split_start(name, [], fulls, 3 * len(fulls), issue, sibling_only=True)


def _forward_wait(name, send_sems, recv_sems, fulls, after):
    def wait(srcs, dsts, send_sems, recv_sems):
        x, y, c, peers = _place()
        for i in range(len(fulls)):
            for k, (px, py) in enumerate(peers):
                cp = _sibling_copy(dsts[i].at[2 * px + py, c], dsts[i].at[2 * px + py, 1 - c], send_sems, recv_sems, 3 * i + k)
                cp.wait_send()
                cp.wait_recv()

    return _split_wait(name, send_sems, recv_sems, [], fulls, after, wait)[1]


def _exchange_start(name, grads):
    lands = [lax.empty((NSH,) + g.shape[2:], g.dtype) for g in grads]

    def issue(srcs, dsts, send_sems, recv_sems):
        c = lax.axis_index("c")
        for i in range(len(grads)):
            _sibling_copy(srcs[i].at[:, 1 - c], dsts[i], send_sems, recv_sems, i).start()

    return _split_start(name, grads, lands, len(grads), issue, sibling_only=True)


def _exchange_wait(name, send_sems, recv_sems, grads, lands, after):
    def wait(srcs, dsts, send_sems, recv_sems):
        c = lax.axis_index("c")
        for i in range(len(grads)):
            cp = _sibling_copy(srcs[i].at[:, 1 - c], dsts[i], send_sems, recv_sems, i)
            cp.wait_send()
            cp.wait_recv()

    return _split_wait(name, send_sems, recv_sems, grads, lands, after, wait)


def _share_start(name, sums):
    def issue(srcs, dsts, send_sems, recv_sems):
        c = lax.axis_index("c")
        for i in range(len(sums)):
            _sibling_copy(dsts[i].at[c], dsts[i].at[c], send_sems, recv_sems, i).start()

    return _split_start(name, [], sums, len(sums), issue, sibling_only=True)


def _share_wait(name, send_sems, recv_sems, sums, after):
    def wait(srcs, dsts, send_sems, recv_sems):
        c = lax.axis_index("c")
        for i in range(len(sums)):
            cp = _sibling_copy(dsts[i].at[c], dsts[i].at[1 - c], send_sems, recv_sems, i)
            cp.wait_send()
            cp.wait_recv()

    return _split_wait(name, send_sems, recv_sems, [], sums, after, wait)[1]


def _allgather_small(v, after):
    m_per, n = v.shape

    def body(x_ref, after_ref, out_ref, send_sems, recv_sems, local_sem):
        x, y, c = lax.axis_index("x"), lax.axis_index("y"), lax.axis_index("c")
        me, sibling = (x, y, c), (x, y, 1 - c)
        chips = [(1 - x, y), (x, 1 - y), (1 - x, 1 - y)]

        def rows(px, py, pc):
            return out_ref.at[pl.ds((4 * px + 2 * py + pc) * m_per, m_per), :]

        def copy(k, block, to, src=None):
            return pltpu.make_async_remote_copy(
                src_ref=rows(*block) if src is None else src, dst_ref=rows(*block),
                send_sem=send_sems.at[k], recv_sem=recv_sems.at[k], device_id=to, device_id_type=MESH)

        mine = pltpu.make_async_copy(x_ref, rows(*me), local_sem)
        mine.start()
        first = [copy(0, me, sibling, src=x_ref)]
        first += [copy(1 + j, me, (*chip, c), src=x_ref) for j, chip in enumerate(chips)]
        for cp in first:
            cp.start()
        passed = [copy(4 + j, (*chip, c), sibling) for j, chip in enumerate(chips)]
        for j, chip in enumerate(chips):
            copy(1 + j, (*chip, c), me).wait_recv()
            passed[j].start()
        copy(0, sibling, me).wait_recv()
        for j, chip in enumerate(chips):
            copy(4 + j, (*chip, 1 - c), me).wait_recv()
        for cp in first + passed:
            cp.wait_send()
        mine.wait()

    return pl.pallas_call(
        body, name="allgather_small", out_shape=SDS((8 * m_per, n), v.dtype),
        in_specs=[pl.BlockSpec(memory_space=pltpu.VMEM), ANY_SPEC], out_specs=pl.BlockSpec(memory_space=pltpu.VMEM),
        scratch_shapes=[pltpu.SemaphoreType.DMA((7,)), pltpu.SemaphoreType.DMA((7,)), pltpu.SemaphoreType.DMA])(v, after)


def _norm_in_proj_own(x, g, w_full, place):
    tn, chunk = 512, 256
    per = (DIN // NSH) // tn

    def body(place_ref, x_ref, g_ref, w_ref, proj_ref, xn_ref):
        @pl.when(pl.program_id(0) == 0)
        def _():
            def norm(r, carry):
                rows = pl.ds(pl.multiple_of(r * chunk, chunk), chunk)
                xv = x_ref[rows, :]
                rs = lax.rsqrt(jnp.mean(xv * xv, axis=-1, keepdims=True) + EPS)
                xn_ref[rows, :] = (xv * rs * g_ref[...]).astype(BF16)
                return carry

            lax.fori_loop(0, T // chunk, norm, 0)

        proj_ref[...] = _dot(xn_ref[...], w_ref[...])

    return pl.pallas_call(
        body, name="norm_in_proj_own", out_shape=[SDS((T, DIN), F32), SDS((T, D), BF16)],
        grid_spec=pltpu.PrefetchScalarGridSpec(
            num_scalar_prefetch=1, grid=(per,),
            in_specs=[_resident((T, D), lambda j, p: (0, 0)),
                      pl.BlockSpec((1, D), lambda j, p: (0, 0)),
                      pl.BlockSpec((None, D, tn), lambda j, p: (p[0], 0, j))],
            out_specs=[pl.BlockSpec((T, tn), lambda j, p: (0, p[0] * per + j)),
                       pl.BlockSpec((T, D), lambda j, p: (0, 0))]),
        compiler_params=_params(1))(place, x, g, w_full)


def _in_proj_rest(xn, w_full, proj, place):
    tn = 512
    per = (DIN // NSH) // tn

    def body(place_ref, xn_ref, w_ref, proj_in, proj_ref):
        proj_ref[...] = _dot(xn_ref[...], w_ref[...])

    shard = lambda j, p: (p[0] + 1 + j // per) % NSH
    return pl.pallas_call(
        body, name="in_proj_rest", out_shape=SDS((T, DIN), F32),
        grid_spec=pltpu.PrefetchScalarGridSpec(
            num_scalar_prefetch=1, grid=((NSH - 1) * per,),
            in_specs=[_resident((T, D), lambda j, p: (0, 0)),
                      pl.BlockSpec((None, D, tn), lambda j, p: (shard(j, p), 0, j % per)), ANY_SPEC],
            out_specs=pl.BlockSpec((T, tn), lambda j, p: (0, shard(j, p) * per + j % per))),
        input_output_aliases={3: 0}, compiler_params=_params(1))(place, xn, w_full, proj)


def _rope_tables():
    pos = np.arange(T, dtype=np.float32)
    inv = (10000.0 ** (-np.arange(0, HD, 2, dtype=np.float32) / HD)).astype(np.float32)
    ang = (pos[:, None] * inv[None, :]).astype(np.float32)
    cos, sin = np.cos(ang).astype(np.float32), np.sin(ang).astype(np.float32)
    return (jnp.asarray(np.concatenate([cos, cos], axis=1)), jnp.asarray(np.concatenate([-sin, sin], axis=1)))


def _qk_prep(proj, nw, cos, sin):
    tm = 256

    def body(p_ref, w_ref, cos_ref, sin_ref, o_ref):
        cv, sv = cos_ref[...], sin_ref[...]
        for h in range(NH):
            sl = slice(h * HD, (h + 1) * HD)
            xv = p_ref[:, sl]
            r = lax.rsqrt(jnp.mean(xv * xv, axis=-1, keepdims=True) + EPS)
            z = xv * r * w_ref[:, sl]
            if h < NHA:
                z = z * cv + pltpu.roll(z, 64, 1) * sv
            o_ref[:, sl] = z.astype(BF16)

    return pl.pallas_call(
        body, name="qk_prep", out_shape=SDS((T, 2 * D), BF16), grid=(T // tm, 2),
        in_specs=[pl.BlockSpec((tm, D), lambda i, j: (i, j)),
                  pl.BlockSpec((None, 1, D), lambda i, j: (j, 0, 0)),
                  pl.BlockSpec((tm, HD), lambda i, j: (i, 0)),
                  pl.BlockSpec((tm, HD), lambda i, j: (i, 0))],
        out_specs=pl.BlockSpec((tm, D), lambda i, j: (i, j)),
        compiler_params=_params(2))(proj, nw, cos, sin)


def _band_mask(q0, m):
    ii = lax.broadcasted_iota(jnp.int32, (128, 256), 0)
    jj = lax.broadcasted_iota(jnp.int32, (128, 256), 1)
    rel = jj - ii
    kpos = jj + (q0 - 64)
    return (rel >= 0) & (rel <= 128) & (kpos >= 0) & (kpos < m)


def _fill_padded(dst, src, m):
    zeros = jnp.zeros((64, HD), dst.dtype)
    dst[0:64, :] = zeros
    dst[64 + m:128 + m, :] = zeros
    dst[64:64 + m, :] = src.astype(dst.dtype)


def _group_views(qkn, proj, g):
    m = T // DILS[g]
    cols = (qkn[:, g * 512:(g + 1) * 512], qkn[:, D + g * 512:D + (g + 1) * 512],
            proj[:, 2 * D + g * 512:2 * D + (g + 1) * 512])
    return [a.reshape(m, DILS[g] * 512) for a in cols]


def _heads_per_step(m):
    return 4 if m <= 512 else 1


def _attn_a_fwd(qkn, proj, g):
    dil = DILS[g]
    m = T // dil
    nb = m // 128
    hp = _heads_per_step(m)

    def body(q_ref, k_ref, v_ref, o_ref, l_ref, kp, vp):
        for hh in range(hp):
            sl = slice(hh * HD, (hh + 1) * HD)
            _fill_padded(kp, k_ref[:, sl], m)
            _fill_padded(vp, v_ref[:, sl], m)

            def block(b, carry):
                q0 = pl.multiple_of(b * 128, 128)
                kw, vw = kp[pl.ds(q0, 256), :], vp[pl.ds(q0, 256), :]
                s = _dot_nt(q_ref[pl.ds(q0, 128), sl], kw) * SCALE
                s = jnp.where(_band_mask(q0, m), s, NEG)
                mx = jnp.max(s, axis=-1, keepdims=True)
                e = jnp.exp(s - mx)
                den = jnp.sum(e, axis=-1, keepdims=True)
                o_ref[pl.ds(q0, 128), sl] = _dot((e / den).astype(BF16), vw)
                l_ref[pl.ds(q0, 128), sl] = jnp.broadcast_to(mx + jnp.log(den), (128, HD))
                return carry

            lax.fori_loop(0, nb, block, 0, unroll=min(nb, 2))

    blk = pl.BlockSpec((m, hp * HD), lambda h, r: (0, r * (4 // hp) + h))
    o, lse = pl.pallas_call(
        body, name=f"attn_a_fwd_{g}", out_shape=[SDS((m, dil * 512), F32)] * 2, grid=(4 // hp, dil),
        in_specs=[blk] * 3, out_specs=[blk] * 2,
        scratch_shapes=[pltpu.VMEM((m + 128, HD), BF16), pltpu.VMEM((m + 128, HD), BF16)],
        compiler_params=_params(2))(*_group_views(qkn, proj, g))
    return o.reshape(T, 512), lse.reshape(T, 512)


def _nbr_window(r):
    start = jnp.clip(r - WIN_R // 2, 0, T // GRID_W - WIN_R)
    return start, start - r + (WIN_R - 1)


def _rpb_rows(rpb):
    zeros = jnp.zeros((4, 14, 33), F32)
    a, b = rpb[:, :14], rpb[:, 1:15]
    rows = jnp.concatenate([a[:, :, 15:31], zeros, b, zeros, a[:, :, 0:15]], axis=2)
    return jnp.pad(rows, ((0, 0), (0, 2), (0, 0)))


def _attn_b_fwd(qkn, proj, rpb_rows):
    def body(r_ref, q_ref, k_ref, v_ref, o_ref, l_ref, bias_ref, vb, pair):
        qc = lax.broadcasted_iota(jnp.int32, (GRID_W, 512), 0)
        kc = lax.broadcasted_iota(jnp.int32, (GRID_W, 512), 1) & (GRID_W - 1)
        cs = jnp.clip(qc - WIN_C // 2, 0, GRID_W - WIN_C)
        colmask = (kc >= cs) & (kc < cs + WIN_C)
        for d in range(14):
            pair[d] = pltpu.roll(jnp.broadcast_to(r_ref[d:d + 1, :], (GRID_W, HD)), 0, 1, stride=1, stride_axis=0)
        for off in range(8):
            rows = jnp.concatenate([pair[off + 2 * jj] for jj in range(4)], axis=1)
            bias_ref[off] = jnp.where(colmask, rows, NEG)
        vb[...] = v_ref[...].astype(BF16)

        def row(r, carry):
            start, off = _nbr_window(r)
            q0 = pl.multiple_of(r * GRID_W, GRID_W)
            k0 = pl.multiple_of(start * GRID_W, GRID_W)
            s = _dot_nt(q_ref[pl.ds(q0, GRID_W), :], k_ref[pl.ds(k0, 512), :]) * SCALE + bias_ref[off]
            mx = jnp.max(s, axis=-1, keepdims=True)
            e = jnp.exp(s - mx)
            den = jnp.sum(e, axis=-1, keepdims=True)
            o_ref[pl.ds(q0, GRID_W), :] = _dot((e / den).astype(BF16), vb[pl.ds(k0, 512), :])
            l_ref[pl.ds(q0, GRID_W), :] = jnp.broadcast_to(mx + jnp.log(den), (GRID_W, HD))
            return carry

        lax.fori_loop(0, T // GRID_W, row, 0, unroll=2)

    return pl.pallas_call(
        body, name="attn_b_fwd",
        out_shape=[SDS((T, 512), F32), SDS((T, 512), F32), SDS((4, 8, GRID_W, 512), F32)], grid=(4,),
        in_specs=[pl.BlockSpec((None, 16, HD), lambda h: (h, 0, 0)),
                  pl.BlockSpec((T, HD), lambda h: (0, NHA + h)),
                  pl.BlockSpec((T, HD), lambda h: (0, NH + NHA + h)),
                  pl.BlockSpec((T, HD), lambda h: (0, 2 * NH + NHA + h))],
        out_specs=[pl.BlockSpec((T, HD), lambda h: (0, h)), pl.BlockSpec((T, HD), lambda h: (0, h)),
                   pl.BlockSpec((None, 8, GRID_W, 512), lambda h: (h, 0, 0, 0))],
        scratch_shapes=[pltpu.VMEM((T, HD), BF16), pltpu.VMEM((14, GRID_W, HD), F32)],
        compiler_params=_params(1))(rpb_rows, qkn, qkn, proj)


def _comb_fwd(os, ls):
    tm = 512

    def body(o0, o1, o2, l0, l1, l2, oa_ref, w0, w1, w2):
        lv = [l0[...], l1[...], l2[...]]
        mx = jnp.maximum(jnp.maximum(lv[0], lv[1]), lv[2])
        ev = [jnp.exp(l - mx) for l in lv]
        den = ev[0] + ev[1] + ev[2]
        wv = [e / den for e in ev]
        oa_ref[...] = (wv[0] * o0[...] + wv[1] * o1[...] + wv[2] * o2[...]).astype(BF16)
        w0[...], w1[...], w2[...] = wv

    spec = pl.BlockSpec((tm, 512), lambda i: (i, 0))
    return pl.pallas_call(
        body, name="comb_fwd", out_shape=[SDS((T, 512), BF16)] + [SDS((T, 512), F32)] * 3, grid=(T // tm,),
        in_specs=[spec] * 6, out_specs=[spec] * 4, compiler_params=_params(1))(*os, *ls)


def _mix_fwd(oa, ob, proj, b_gate, wpa, wpb):
    tm = 512

    def body(oa_ref, ob_ref, ga_ref, gb_ref, ba_ref, bb_ref, wpa_ref, wpb_ref, mixed_ref, ob16_ref):
        oav = oa_ref[...]
        obv = ob_ref[...].astype(BF16)
        ob16_ref[...] = obv
        for s in range(NSH):
            sl = slice(s * 512, (s + 1) * 512)
            ga = _sigmoid(ga_ref[:, sl] + ba_ref[:, sl])
            gb = _sigmoid(gb_ref[:, sl] + bb_ref[:, sl])
            mixed_ref[:, sl] = (ga * _dot(oav, wpa_ref[s]) + gb * _dot(obv, wpb_ref[s])).astype(BF16)

    row = lambda w: pl.BlockSpec((tm, w), lambda i: (i, 0))
    return pl.pallas_call(
        body, name="mix_fwd", out_shape=[SDS((T, D), BF16), SDS((T, 512), BF16)], grid=(T // tm,),
        in_specs=[row(512), row(512),
                  pl.BlockSpec((tm, D), lambda i: (i, 3)), pl.BlockSpec((tm, D), lambda i: (i, 4)),
                  pl.BlockSpec((1, D), lambda i: (0, 0)), pl.BlockSpec((1, D), lambda i: (0, 1)),
                  _resident((NSH, 512, 512), lambda i: (0, 0, 0)), _resident((NSH, 512, 512), lambda i: (0, 0, 0))],
        out_specs=[row(D), row(512)], compiler_params=_params(1))(oa, ob, proj, proj, b_gate, b_gate, wpa, wpb)


def _out_proj_fwd(mixed, w_out, x, g):
    tm = 512

    def body(m_ref, w_ref, x_ref, g_ref, h1_ref, hn_ref):
        h1 = x_ref[...] + _dot(m_ref[...], w_ref[...])
        h1_ref[...] = h1
        r = lax.rsqrt(jnp.mean(h1 * h1, axis=-1, keepdims=True) + EPS)
        hn_ref[...] = (h1 * r * g_ref[...]).astype(BF16)

    row = pl.BlockSpec((tm, D), lambda i: (i, 0))
    return pl.pallas_call(
        body, name="out_proj_fwd", out_shape=[SDS((T, D), F32), SDS((T, D), BF16)], grid=(T // tm,),
        in_specs=[row, _resident((D, D), lambda i: (0, 0)), row, pl.BlockSpec((1, D), lambda i: (0, 0))],
        out_specs=[row, row], compiler_params=_params(1))(mixed, w_out, x, g)


def _ffn_up(hn, w_up):
    tm, tn = T, 512
    per = (DFF // NSH) // tn

    def body(h_ref, w_ref, a_ref, u_ref):
        uv = jnp.maximum(_dot(h_ref[...], w_ref[...]), 0.0)
        a_ref[...] = (uv * uv).astype(BF16)
        u_ref[...] = uv.astype(BF16)

    out = pl.BlockSpec((tm, tn), lambda i, j: (i, j))
    return pl.pallas_call(
        body, name="ffn_up", out_shape=[SDS((T, DFF), BF16)] * 2, grid=(T // tm, DFF // tn),
        in_specs=[pl.BlockSpec((tm, D), lambda i, j: (i, 0)),
                  pl.BlockSpec((None, D, tn), lambda i, j: (j // per, 0, j % per))],
        out_specs=[out, out], compiler_params=_params(2))(hn, w_up)


def _ffn_down_loss(u, w_down, h1, target):
    tm, tk = 512, 2048
    nk = DFF // tk

    def body(u_ref, w_ref, h1_ref, t_ref, dy_ref, dy16_ref, loss_ref, acc):
        k = pl.program_id(1)

        @pl.when(k == 0)
        def _():
            acc[...] = jnp.zeros_like(acc)

        acc[...] += _dot(u_ref[...], w_ref[...])

        @pl.when(k == nk - 1)
        def _():
            def chunk(r, sq):
                rows = pl.ds(pl.multiple_of(r * 16, 16), 16)
                err = acc[rows, :] + h1_ref[rows, :] - t_ref[rows, :]
                dy = err * (1.0 / D)
                dy_ref[rows, :] = dy
                dy16_ref[rows, :] = dy.astype(BF16)
                return sq + err * err

            sq = lax.fori_loop(0, tm // 16, chunk, jnp.zeros((16, D), F32), unroll=2)
            part = 0.5 * jnp.sum(jnp.mean(sq, axis=-1, keepdims=True), axis=0, keepdims=True)
            loss_ref[...] = jnp.broadcast_to(part, (8, 128))

    row = pl.BlockSpec((tm, D), lambda i, k: (i, 0))
    once = _resident((tm, D), lambda i, k: (i, 0))
    return pl.pallas_call(
        body, name="ffn_down_loss",
        out_shape=[SDS((T, D), F32), SDS((T, D), BF16), SDS((T // tm, 8, 128), F32)], grid=(T // tm, nk),
        in_specs=[pl.BlockSpec((tm, tk), lambda i, k: (i, k)), pl.BlockSpec((tk, D), lambda i, k: (k, 0)), once, once],
        out_specs=[row, row, pl.BlockSpec((None, 8, 128), lambda i, k: (i, 0, 0))],
        scratch_shapes=[pltpu.VMEM((tm, D), F32)], compiler_params=_params(2))(u, w_down, h1, target)


def _ffn_down_bwd(dy16, w_down, u, deps=()):
    tm, tn = T, 512

    def body(dy_ref, w_ref, u_ref, du_ref):
        uv = u_ref[...].astype(F32)
        du_ref[...] = jnp.where(uv > 0.0, 2.0 * uv * _dot_nt(dy_ref[...], w_ref[...]), 0.0).astype(BF16)

    return pl.pallas_call(
        _after(body, deps), name="ffn_down_bwd", out_shape=SDS((T, DFF), BF16), grid=(T // tm, DFF // tn),
        in_specs=[DEP_SPEC] * len(deps) + [
            pl.BlockSpec((tm, D), lambda i, j: (i, 0)), pl.BlockSpec((tn, D), lambda i, j: (j, 0)),
            pl.BlockSpec((tm, tn), lambda i, j: (i, j))],
        out_specs=pl.BlockSpec((tm, tn), lambda i, j: (i, j)), compiler_params=_params(2))(*deps, dy16, w_down, u)


def _norm_bwd(xv, dz_in, g):
    r = lax.rsqrt(jnp.mean(xv * xv, axis=-1, keepdims=True) + EPS)
    dg = jnp.sum(xv * r * dz_in, axis=0, keepdims=True)
    dz = dz_in * g
    dx = r * dz - xv * (r * r * r) * jnp.mean(xv * dz, axis=-1, keepdims=True)
    return dx, dg


def _ffn_up_bwd(du, w_up, h1, dy, g, deps=()):
    tm, tk = 512, 1024
    per = (DFF // NSH) // tk
    nk = DFF // tk

    def body(du_ref, w_ref, h1_ref, dy_ref, g_ref, dh1_ref, dh16_ref, dg_ref, acc):
        i, k = pl.program_id(0), pl.program_id(1)

        @pl.when(k == 0)
        def _():
            acc[...] = jnp.zeros_like(acc)

        @pl.when((k == 0) & (i == 0))
        def _():
            dg_ref[...] = jnp.zeros_like(dg_ref)

        acc[...] += _dot_nt(du_ref[...], w_ref[...])

        @pl.when(k == nk - 1)
        def _():
            dx, dg = _norm_bwd(h1_ref[...], acc[...], g_ref[...])
            dh1 = dy_ref[...] + dx
            dh1_ref[...] = dh1
            dh16_ref[...] = dh1.astype(BF16)
            dg_ref[...] += dg

    row = pl.BlockSpec((tm, D), lambda i, k: (i, 0))
    vec = pl.BlockSpec((1, D), lambda i, k: (0, 0))
    return pl.pallas_call(
        _after(body, deps), name="ffn_up_bwd", out_shape=[SDS((T, D), F32), SDS((T, D), BF16), SDS((1, D), F32)],
        grid=(T // tm, nk),
        in_specs=[DEP_SPEC] * len(deps) + [
            pl.BlockSpec((tm, tk), lambda i, k: (i, k)),
            pl.BlockSpec((None, D, tk), lambda i, k: (k // per, 0, k % per)), row, row, vec],
        out_specs=[row, row, vec], scratch_shapes=[pltpu.VMEM((tm, D), F32)],
        compiler_params=_params(2))(*deps, du, w_up, h1, dy, g)


def _mix_bwd(dh16, w_out, oa, ob16, proj, b_gate, wpa, wpb):
    tm = 256

    def body(dh_ref, wo_ref, oa_ref, ob_ref, ga_ref, gb_ref, ba_ref, bb_ref, wpa_ref, wpb_ref,
             dya_ref, dyb_ref, dga_ref, dgb_ref, doa_ref, dob_ref, dba_ref, dbb_ref):
        @pl.when(pl.program_id(0) == 0)
        def _():
            dba_ref[...] = jnp.zeros_like(dba_ref)
            dbb_ref[...] = jnp.zeros_like(dbb_ref)

        oav, obv = oa_ref[...], ob_ref[...]
        doa = jnp.zeros((tm, 512), F32)
        dob = jnp.zeros((tm, 512), F32)
        for s in range(NSH):
            sl = slice(s * 512, (s + 1) * 512)
            dm = _dot_nt(dh_ref[...], wo_ref[sl, :])
            ga = _sigmoid(ga_ref[:, sl] + ba_ref[:, sl])
            gb = _sigmoid(gb_ref[:, sl] + bb_ref[:, sl])
            dya = (dm * ga).astype(BF16)
            dyb = (dm * gb).astype(BF16)
            dza = dm * _dot(oav, wpa_ref[s]) * ga * (1.0 - ga)
            dzb = dm * _dot(obv, wpb_ref[s]) * gb * (1.0 - gb)
            dya_ref[:, sl], dyb_ref[:, sl] = dya, dyb
            dga_ref[:, sl], dgb_ref[:, sl] = dza.astype(BF16), dzb.astype(BF16)
            dba_ref[:, sl] += jnp.sum(dza, axis=0, keepdims=True)
            dbb_ref[:, sl] += jnp.sum(dzb, axis=0, keepdims=True)
            doa += _dot_nt(dya, wpa_ref[s])
            dob += _dot_nt(dyb, wpb_ref[s])
        doa_ref[...], dob_ref[...] = doa, dob

    row = lambda w: pl.BlockSpec((tm, w), lambda i: (i, 0))
    vec = pl.BlockSpec((1, D), lambda i: (0, 0))
    wp = _resident((NSH, 512, 512), lambda i: (0, 0, 0))
    return pl.pallas_call(
        body, name="mix_bwd",
        out_shape=[SDS((T, D), BF16)] * 4 + [SDS((T, 512), F32)] * 2 + [SDS((1, D), F32)] * 2, grid=(T // tm,),
        in_specs=[row(D), _resident((D, D), lambda i: (0, 0)), row(512), row(512),
                  pl.BlockSpec((tm, D), lambda i: (i, 3)), pl.BlockSpec((tm, D), lambda i: (i, 4)),
                  pl.BlockSpec((1, D), lambda i: (0, 0)), pl.BlockSpec((1, D), lambda i: (0, 1)), wp, wp],
        out_specs=[row(D)] * 4 + [row(512)] * 2 + [vec] * 2,
        compiler_params=_params(1))(dh16, w_out, oa, ob16, proj, proj, b_gate, b_gate, wpa, wpb)


def _comb_bwd(doa, os, ws, deps=()):
    tm = 512

    def body(d_ref, o0, o1, o2, w0, w1, w2, cc_ref):
        prod = d_ref[...] * (w0[...] * o0[...] + w1[...] * o1[...] + w2[...] * o2[...])
        for h in range(4):
            sl = slice(h * HD, (h + 1) * HD)
            cc_ref[:, sl] = jnp.broadcast_to(jnp.sum(prod[:, sl], axis=-1, keepdims=True), (tm, HD))

    spec = pl.BlockSpec((tm, 512), lambda i: (i, 0))
    return pl.pallas_call(
        _after(body, deps), name="comb_bwd", out_shape=SDS((T, 512), F32), grid=(T // tm,),
        in_specs=[DEP_SPEC] * len(deps) + [spec] * 7, out_specs=spec,
        compiler_params=_params(1))(*deps, doa, *os, *ws)


def _attn_a_bwd(qkn, proj, doa, lse, w, cc, g):
    dil = DILS[g]
    m = T // dil
    nb = m // 128
    hp = _heads_per_step(m)

    def body(q_ref, k_ref, v_ref, d_ref, l_ref, w_ref, c_ref, dqk_ref, dv_ref, kp, vp, dkp, dvp):
        for hh in range(hp):
            sl = slice(hh * HD, (hh + 1) * HD)
            _fill_padded(kp, k_ref[:, sl], m)
            _fill_padded(vp, v_ref[:, sl], m)
            dkp[...] = jnp.zeros_like(dkp)
            dvp[...] = jnp.zeros_like(dvp)

            def block(b, carry):
                q0 = pl.multiple_of(b * 128, 128)
                rows = pl.ds(q0, 128)
                win = pl.ds(q0, 256)
                qb, kw, vw = q_ref[rows, sl], kp[win, :], vp[win, :]
                s = _dot_nt(qb, kw) * SCALE
                s = jnp.where(_band_mask(q0, m), s, NEG)
                wp = _wide(w_ref[rows, sl], 2) * jnp.exp(s - _wide(l_ref[rows, sl], 2))
                dob = d_ref[rows, sl].astype(BF16)
                ds = (wp * (_dot_nt(dob, vw) - _wide(c_ref[rows, sl], 2))).astype(BF16)
                dqk_ref[0, rows, sl] = _dot(ds, kw) * SCALE
                dkp[win, :] += _dot_tn(ds, qb) * SCALE
                dvp[win, :] += _dot_tn(wp.astype(BF16), dob)
                return carry

            lax.fori_loop(0, nb, block, 0, unroll=min(nb, 2))
            dqk_ref[1, :, sl] = dkp[64:64 + m, :]
            dv_ref[:, sl] = dvp[64:64 + m, :]

    blk = pl.BlockSpec((m, hp * HD), lambda h, r: (0, r * (4 // hp) + h))
    view = lambda a: a.reshape(m, dil * 512)
    dqk, dv = pl.pallas_call(
        body, name=f"attn_a_bwd_{g}", out_shape=[SDS((2, m, dil * 512), F32), SDS((m, dil * 512), F32)],
        grid=(4 // hp, dil), in_specs=[blk] * 7,
        out_specs=[pl.BlockSpec((2, m, hp * HD), lambda h, r: (0, 0, r * (4 // hp) + h)), blk],
        scratch_shapes=[pltpu.VMEM((m + 128, HD), BF16), pltpu.VMEM((m + 128, HD), BF16),
                        pltpu.VMEM((m + 128, HD), F32), pltpu.VMEM((m + 128, HD), F32)],
        compiler_params=_params(2))(*_group_views(qkn, proj, g), view(doa), view(lse), view(w), view(cc))
    return dqk.reshape(2, T, 512), dv.reshape(T, 512)


def _attn_b_bwd(qkn, proj, dob, ob, lse, bias, deps=()):
    def body(q_ref, k_ref, v_ref, d_ref, o_ref, l_ref, bias_ref, dqk_ref, dv_ref, drpb_ref, vb, dk_acc, dv_acc, a_acc):
        vb[...] = v_ref[...].astype(BF16)
        dk_acc[...] = jnp.zeros_like(dk_acc)
        dv_acc[...] = jnp.zeros_like(dv_acc)
        a_acc[...] = jnp.zeros_like(a_acc)

        def row(r, carry):
            start, off = _nbr_window(r)
            rows = pl.ds(pl.multiple_of(r * GRID_W, GRID_W), GRID_W)
            win = pl.ds(pl.multiple_of(start * GRID_W, GRID_W), 512)
            qr, kw, vw = q_ref[rows, :], k_ref[win, :], vb[win, :]
            s = _dot_nt(qr, kw) * SCALE + bias_ref[off]
            p = jnp.exp(s - _wide(l_ref[rows, :], 4))
            dov = d_ref[rows, :]
            delta = jnp.sum(dov * o_ref[rows, :], axis=-1, keepdims=True)
            do16 = dov.astype(BF16)
            ds = p * (_dot_nt(do16, vw) - delta)
            a_acc[off] += ds
            ds16 = ds.astype(BF16)
            dqk_ref[0, rows, :] = _dot(ds16, kw) * SCALE
            dk_acc[win, :] += _dot_tn(ds16, qr) * SCALE
            dv_acc[win, :] += _dot_tn(p.astype(BF16), do16)
            return carry

        lax.fori_loop(0, T // GRID_W, row, 0, unroll=2)
        dqk_ref[1] = dk_acc[...]
        dv_ref[...] = dv_acc[...]

        lane = lax.broadcasted_iota(jnp.int32, (16, HD), 1)
        rowi = lax.broadcasted_iota(jnp.int32, (16, HD), 0)
        low = (lane >= GRID_W - WIN_C) & (lane < GRID_W + WIN_C - 1)
        high = (lane >= HD - WIN_C) | (lane < WIN_C - 1)
        flip = (lax.broadcasted_iota(jnp.int32, (GRID_W, GRID_W), 0)
                + lax.broadcasted_iota(jnp.int32, (GRID_W, GRID_W), 1) == GRID_W - 1).astype(BF16)
        out = jnp.zeros((16, HD), F32)
        for d in range(14):
            acc = None
            for off in range(8):
                if 0 <= d - off <= 6 and (d - off) % 2 == 0:
                    jj = (d - off) // 2
                    piece = a_acc[off, :, jj * HD:(jj + 1) * HD]
                    acc = piece if acc is None else acc + piece
            hi = acc.astype(BF16)
            lo = (acc - hi.astype(F32)).astype(BF16)
            rev = _dot(flip, hi) + _dot(flip, lo)
            v = jnp.sum(pltpu.roll(rev, 0, 1, stride=1, stride_axis=0), axis=0, keepdims=True)
            v = jnp.broadcast_to(v, (16, HD))
            out = out + jnp.where((rowi == d) & low, v, 0.0)
            out = out + jnp.where(rowi == d + 1, pltpu.roll(jnp.where(high, v, 0.0), GRID_W, 1), 0.0)
        drpb_ref[...] = out

    blk = pl.BlockSpec((T, HD), lambda h: (0, h))
    return pl.pallas_call(
        _after(body, deps), name="attn_b_bwd",
        out_shape=[SDS((2, T, 512), F32), SDS((T, 512), F32), SDS((4, 16, HD), F32)], grid=(4,),
        in_specs=[DEP_SPEC] * len(deps) + [
            pl.BlockSpec((T, HD), lambda h: (0, NHA + h)),
            pl.BlockSpec((T, HD), lambda h: (0, NH + NHA + h)),
            pl.BlockSpec((T, HD), lambda h: (0, 2 * NH + NHA + h)), blk, blk, blk,
            pl.BlockSpec((None, 8, GRID_W, 512), lambda h: (h, 0, 0, 0))],
        out_specs=[pl.BlockSpec((2, T, HD), lambda h: (0, 0, h)), blk,
                   pl.BlockSpec((None, 16, HD), lambda h: (h, 0, 0))],
        scratch_shapes=[pltpu.VMEM((T, HD), BF16), pltpu.VMEM((T, HD), F32), pltpu.VMEM((T, HD), F32),
                        pltpu.VMEM((8, GRID_W, 512), F32)],
        compiler_params=_params(1))(*deps, qkn, qkn, proj, dob, ob, lse, bias)


def _qk_bwd(proj, nw, cos, sin, dqk_groups, dqk_b, dvs, dga, dgb):
    tm = 256

    def body(p_ref, w_ref, cos_ref, sin_ref, d0, d1, d2, d3, v0, v1, v2, v3, ga_ref, gb_ref, o_ref, dn_ref):
        j, i = pl.program_id(0), pl.program_id(1)

        @pl.when((j < 2) & (i == 0))
        def _():
            dn_ref[...] = jnp.zeros_like(dn_ref)

        @pl.when(j < 2)
        def _():
            cv, sv = cos_ref[...], sin_ref[...]
            srcs = (d0, d1, d2, d3)
            dna = jnp.zeros((1, HD), F32)
            dnb = jnp.zeros((1, HD), F32)
            for h in range(NH):
                sl = slice(h * HD, (h + 1) * HD)
                dz = srcs[h // 4][:, (h % 4) * HD:(h % 4 + 1) * HD]
                if h < NHA:
                    dz = dz * cv + pltpu.roll(dz * sv, 64, 1)
                dx, dg = _norm_bwd(p_ref[:, sl], dz, w_ref[:, sl])
                o_ref[:, sl] = dx.astype(BF16)
                if h < NHA:
                    dna += dg
                else:
                    dnb += dg
            dn_ref[0:1, :] += dna
            dn_ref[1:2, :] += dnb

        @pl.when(j == 2)
        def _():
            for s, v_ref in enumerate((v0, v1, v2, v3)):
                o_ref[:, s * 512:(s + 1) * 512] = v_ref[...].astype(BF16)

        @pl.when(j == 3)
        def _():
            o_ref[...] = ga_ref[...]

        @pl.when(j == 4)
        def _():
            o_ref[...] = gb_ref[...]

    def rows(used):
        return lambda j, i: (jnp.where(used(j), i, 0), 0)

    qk = lambda j: j < 2
    dspec = pl.BlockSpec((None, tm, 512), lambda j, i: (jnp.minimum(j, 1), jnp.where(j < 2, i, 0), 0))
    vspec = pl.BlockSpec((tm, 512), rows(lambda j: j == 2))
    return pl.pallas_call(
        body, name="qk_bwd", out_shape=[SDS((T, DIN), BF16), SDS((2, 8, HD), F32)], grid=(5, T // tm),
        in_specs=[pl.BlockSpec((tm, D), lambda j, i: (jnp.where(j < 2, i, 0), jnp.minimum(j, 1))),
                  pl.BlockSpec((None, 1, D), lambda j, i: (jnp.minimum(j, 1), 0, 0)),
                  pl.BlockSpec((tm, HD), rows(qk)), pl.BlockSpec((tm, HD), rows(qk)),
                  dspec, dspec, dspec, dspec, vspec, vspec, vspec, vspec,
                  pl.BlockSpec((tm, D), rows(lambda j: j == 3)), pl.BlockSpec((tm, D), rows(lambda j: j == 4))],
        out_specs=[pl.BlockSpec((tm, D), lambda j, i: (i, j)),
                   pl.BlockSpec((None, 8, HD), lambda j, i: (jnp.minimum(j, 1), 0, 0))],
        compiler_params=_params(2))(proj, nw, cos, sin, *dqk_groups, dqk_b, *dvs, dga, dgb)


def _in_proj_bwd(dproj, w_in, x, dh1, g, deps=()):
    tm, tk = 512, 1280
    per = (DIN // NSH) // tk
    nk = DIN // tk

    def body(dp_ref, w_ref, x_ref, dh_ref, g_ref, dx_ref, dg_ref, acc):
        i, k = pl.program_id(0), pl.program_id(1)

        @pl.when(k == 0)
        def _():
            acc[...] = jnp.zeros_like(acc)

        @pl.when((k == 0) & (i == 0))
        def _():
            dg_ref[...] = jnp.zeros_like(dg_ref)

        acc[...] += _dot_nt(dp_ref[...], w_ref[...])

        @pl.when(k == nk - 1)
        def _():
            dx, dg = _norm_bwd(x_ref[...], acc[...], g_ref[...])
            dx_ref[...] = dh_ref[...] + dx
            dg_ref[...] += dg

    row = pl.BlockSpec((tm, D), lambda i, k: (i, 0))
    vec = pl.BlockSpec((1, D), lambda i, k: (0, 0))
    return pl.pallas_call(
        _after(body, deps), name="in_proj_bwd", out_shape=[SDS((T, D), F32), SDS((1, D), F32)], grid=(T // tm, nk),
        in_specs=[DEP_SPEC] * len(deps) + [
            pl.BlockSpec((tm, tk), lambda i, k: (i, k)),
            pl.BlockSpec((None, D, tk), lambda i, k: (k // per, 0, k % per)), row, row, vec],
        out_specs=[row, vec], scratch_shapes=[pltpu.VMEM((tm, D), F32)],
        compiler_params=_params(2))(*deps, dproj, w_in, x, dh1, g)


def _grad_w(name, a, g, shard_rows, rows, cols, tr, tc):
    ni, nj = rows // tr, cols // tc
    if shard_rows:
        a_map, g_map = (lambda s, i, j: (0, s * ni + i)), (lambda s, i, j: (0, j))
    else:
        a_map, g_map = (lambda s, i, j: (0, i)), (lambda s, i, j: (0, s * nj + j))

    def body(a_ref, g_ref, o_ref):
        o_ref[...] = _dot_tn(a_ref[...], g_ref[...]).astype(BF16)

    return pl.pallas_call(
        body, name=name, out_shape=SDS((NSH, rows, cols), BF16), grid=(NSH, ni, nj),
        in_specs=[pl.BlockSpec((T, tr), a_map), pl.BlockSpec((T, tc), g_map)],
        out_specs=pl.BlockSpec((None, tr, tc), lambda s, i, j: (s, i, j)), compiler_params=_params(3))(a, g)


def _adamw(w, g, m, v):
    m = B1 * m + (1.0 - B1) * g
    v = B2 * v + (1.0 - B2) * (g * g)
    m_hat = m / (1.0 - B1 ** STEP)
    v_hat = v / (1.0 - B2 ** STEP)
    delta = -LR * (m_hat / (jnp.sqrt(v_hat) + AEPS) + WD * w)
    return delta, m, v


def _sum_halves(name, place, grads, theirs):
    _, rows, cols = theirs.shape
    tr = _row_tile(rows, cols, 1 << 20)

    def body(place_ref, a_ref, b_ref, o_ref):
        o_ref[...] = (a_ref[...].astype(F32) + b_ref[...].astype(F32)).astype(BF16)

    spec = pl.BlockSpec((None, tr, cols), lambda s, i, p: (s, i, 0))
    return pl.pallas_call(
        body, name=name, out_shape=SDS(theirs.shape, BF16),
        grid_spec=pltpu.PrefetchScalarGridSpec(
            num_scalar_prefetch=1, grid=(NSH, rows // tr),
            in_specs=[pl.BlockSpec((None, None, tr, cols), lambda s, i, p: (s, p[1], i, 0)), spec], out_specs=spec),
        compiler_params=_params(2))(place, grads, theirs)


def _sum_landed(name, place, part, landed):
    _, rows, cols = part.shape
    tr = _row_tile(rows, cols, 1 << 20)

    def body(place_ref, p_ref, l_ref, o_ref):
        o_ref[...] = ((p_ref[...].astype(F32) + l_ref[0].astype(F32)) + l_ref[1].astype(F32)) + l_ref[2].astype(F32)

    return pl.pallas_call(
        body, name=name, out_shape=SDS((2, rows, cols), F32),
        grid_spec=pltpu.PrefetchScalarGridSpec(
            num_scalar_prefetch=1, grid=(rows // tr,),
            in_specs=[pl.BlockSpec((None, tr, cols), lambda i, p: (p[0], i, 0)),
                      pl.BlockSpec((3, tr, cols), lambda i, p: (0, i, 0))],
            out_specs=pl.BlockSpec((None, tr, cols), lambda i, p: (p[1], i, 0))),
        compiler_params=_params(1))(place, part, landed)


def _adam_shard(name, g, w, m, v):
    rows, cols = w.shape
    tr = _row_tile(rows, cols, 1 << 19)

    def body(g_ref, w_ref, m_ref, v_ref, go_ref, d_ref, nm_ref, nv_ref):
        g = g_ref[...]
        go_ref[...] = g
        d_ref[...], nm_ref[...], nv_ref[...] = _adamw(w_ref[...], g, m_ref[...], v_ref[...])

    spec = pl.BlockSpec((tr, cols), lambda i: (i, 0))
    return pl.pallas_call(
        body, name=name, out_shape=[SDS((rows, cols), F32)] * 4, grid=(rows // tr,),
        in_specs=[spec] * 4, out_specs=[spec] * 4, compiler_params=_params(1))(g, w, m, v)


def _adam_small(gathered, w, m, v):
    def body(g_ref, w_ref, m_ref, v_ref, go_ref, d_ref, nm_ref, nv_ref):
        g = g_ref[0:SMALL_ROWS, :]
        for dev in range(1, 8):
            g = g + g_ref[dev * SMALL_ROWS:(dev + 1) * SMALL_ROWS, :]
        go_ref[...] = g
        d_ref[...], nm_ref[...], nv_ref[...] = _adamw(w_ref[...], g, m_ref[...], v_ref[...])

    return pl.pallas_call(body, name="adam_small", out_shape=[SDS((SMALL_ROWS, HD), F32)] * 4)(gathered, w, m, v)


SMALL = (("norm_mix", (1, D)), ("b_gate", (1, 2 * D)), ("q_norm_a", (1, HD)), ("k_norm_a", (1, HD)),
         ("q_norm_b", (1, HD)), ("k_norm_b", (1, HD)), ("rpb_b", (1, 4, 15, 31)), ("norm_ffn", (1, D)))


def _pack_small(vals):
    pieces = []
    for (name, shape), val in zip(SMALL, vals):
        flat = val.reshape(-1)
        pad = (-flat.shape[0]) % HD
        pieces.append(jnp.pad(flat, (0, pad)).reshape(-1, HD))
    packed = jnp.concatenate(pieces, axis=0)
    return jnp.pad(packed, ((0, SMALL_ROWS - packed.shape[0]), (0, 0)))


def _unpack_small(packed):
    out, row = [], 0
    for name, shape in SMALL:
        size = int(np.prod(shape))
        nrows = -(-size // HD)
        out.append(packed[row:row + nrows].reshape(-1)[:size].reshape(shape))
        row += nrows
    return out


def kernel(x, norm_mix, w_in, b_gate, q_norm_a, k_norm_a, q_norm_b, k_norm_b, rpb_b, w_proj_a, w_proj_b, w_out, norm_ffn, w_up, w_down, loss_target, m_norm_mix, m_w_in, m_b_gate, m_q_norm_a, m_k_norm_a, m_q_norm_b, m_k_norm_b, m_rpb_b, m_w_proj_a, m_w_proj_b, m_w_out, m_norm_ffn, m_w_up, m_w_down, v_norm_mix, v_w_in, v_b_gate, v_q_norm_a, v_k_norm_a, v_q_norm_b, v_k_norm_b, v_rpb_b, v_w_proj_a, v_w_proj_b, v_w_out, v_norm_ffn, v_w_up, v_w_down):
    big_names = ("w_in", "w_proj_a", "w_proj_b", "w_out", "w_up", "w_down")
    big_w = [a[0] for a in (w_in, w_proj_a, w_proj_b, w_out, w_up, w_down)]
    big_m = [a[0] for a in (m_w_in, m_w_proj_a, m_w_proj_b, m_w_out, m_w_up, m_w_down)]
    big_v = [a[0] for a in (v_w_in, v_w_proj_a, v_w_proj_b, v_w_out, v_w_up, v_w_down)]
    x2, target = x[0], loss_target[0]

    place = jnp.stack([2 * lax.axis_index("x") + lax.axis_index("y"), lax.axis_index("c")]).astype(jnp.int32)
    groups = ((0,), (1, 2, 3), (4,), (5,))
    started = []
    for j, grp in enumerate(groups):
        deps = (started[0][4],) if j else ()
        placed = [_cast_into_place(big_w[i], "cast_" + big_names[i], place, deps) for i in grp]
        started.append(_gather_start(f"gather_start_{j}", placed))

    def whole(fulls):
        return [f.reshape(NSH, 2 * f.shape[2], f.shape[3]) for f in fulls]

    def gathered(j, after):
        send, recv, _, fulls, _ = started[j]
        fulls = _gather_wait(f"gather_wait_{j}", send, recv, fulls, after)
        return whole(_gather_finish(f"gather_finish_{j}", fulls))

    def forward_begin(j, after):
        send, recv, _, fulls, _ = started[j]
        fulls = _gather_wait(f"gather_wait_{j}", send, recv, fulls, after)
        send, recv, _, fulls, token = _forward_start(f"forward_start_{j}", fulls)
        return (send, recv, fulls), token

    def forward_end(j, state, after):
        return whole(_forward_wait(f"forward_wait_{j}", *state, after))

    def as_halves(grads):
        return [g.reshape(NSH, 2, g.shape[1] // 2, g.shape[2]) for g in grads]

    def reduce_start(j, grads, theirs):
        parts = [_sum_halves(f"sum_halves_{j}_{i}", place, a, b) for i, (a, b) in enumerate(zip(grads, theirs))]
        send, recv, parts, lands, token = _reduce_start(f"reduce_start_{j}", parts)
        return (send, recv, parts, lands), token

    def reduce_begin(j, grads):
        grads = as_halves(grads)
        return reduce_start(j, grads, _reduce_exchange(f"reduce_exchange_{j}", grads))

    def exchange_begin(j, grads):
        send, recv, grads, lands, token = _exchange_start(f"exchange_start_{j}", as_halves(grads))
        return (send, recv, grads, lands), token

    def exchange_end(j, state, after):
        return reduce_start(j, *_exchange_wait(f"exchange_wait_{j}", *state, after))

    big_out = {}

    def share_begin(j, state, after):
        send, recv, parts, lands = state
        parts, lands = _reduce_wait(f"reduce_wait_{j}", send, recv, parts, lands, after)
        sums = [_sum_landed(f"sum_landed_{j}_{i}", place, p, l) for i, (p, l) in enumerate(zip(parts, lands))]
        send, recv, _, sums, token = _share_start(f"share_start_{j}", sums)
        return (send, recv, sums), token

    def share_end(j, state, after):
        for idx, g in zip(groups[j], _share_wait(f"share_wait_{j}", *state, after)):
            g = g.reshape(big_w[idx].shape)
            big_out[idx] = _adam_shard("adam_" + big_names[idx], g, big_w[idx], big_m[idx], big_v[idx])
        return big_out[groups[j][-1]][1]

    proj, xn = _norm_in_proj_own(x2, norm_mix, whole(started[0][3])[0], place)
    (win_f,) = gathered(0, (proj, *[s[4] for s in started[1:]]))
    proj = _in_proj_rest(xn, win_f, proj, place)
    cos, sin = _rope_tables()
    nw = jnp.stack([jnp.concatenate([jnp.tile(q_norm_a, (1, NHA)), jnp.tile(q_norm_b, (1, NH - NHA))], axis=1),
                    jnp.concatenate([jnp.tile(k_norm_a, (1, NHA)), jnp.tile(k_norm_b, (1, NH - NHA))], axis=1)])
    qkn = _qk_prep(proj, nw, cos, sin)
    fw1, token = forward_begin(1, (qkn,))
    fwd_a = [_attn_a_fwd(qkn, proj, g) for g in range(3)]
    os, ls = [f[0] for f in fwd_a], [f[1] for f in fwd_a]
    fw2, token = forward_begin(2, (os[2], token))
    ob, lse_b, bias = _attn_b_fwd(qkn, proj, _rpb_rows(rpb_b[0]))
    oa, w0, w1, w2 = _comb_fwd(os, ls)
    ws = [w0, w1, w2]
    wpa_f, wpb_f, wout_f = forward_end(1, fw1, (oa, token))
    wout_f = wout_f.reshape(D, D)
    mixed, ob16 = _mix_fwd(oa, ob, proj, b_gate, wpa_f, wpb_f)
    h1, hn = _out_proj_fwd(mixed, wout_f, x2, norm_ffn)
    fw3, token = forward_begin(3, (h1,))
    (wup_f,) = forward_end(2, fw2, (hn, token))
    usq, u = _ffn_up(hn, wup_f)
    (wdown_f,) = forward_end(3, fw3, (u,))
    wdown_f = wdown_f.reshape(DFF, D)
    dy, dy16, loss_parts = _ffn_down_loss(usq, wdown_f, h1, target)
    loss = lax.psum(jnp.sum(loss_parts[:, 0, 0]), ("x", "y", "c"))

    g_down = _grad_w("grad_w_down", usq, dy16, True, DFF // NSH, D, 1024, 1024)
    ex_down, token = exchange_begin(3, [g_down])
    du = _ffn_down_bwd(dy16, wdown_f, u, deps=(token,))
    g_up = _grad_w("grad_w_up", hn, du, False, D, DFF // NSH, 1024, 1024)
    red_down, token = exchange_end(3, ex_down, (g_up,))
    ex_up, token_up = exchange_begin(2, [g_up])
    dh1, dh16, d_norm_ffn = _ffn_up_bwd(du, wup_f, h1, dy, norm_ffn, deps=(token, token_up))
    dya, dyb, dga, dgb, doa, dob, dba, dbb = _mix_bwd(dh16, wout_f, oa, ob16, proj, b_gate, wpa_f, wpb_f)
    g_out = _grad_w("grad_w_out", mixed, dh16, True, D // NSH, D, 512, 1024)
    g_pa = _grad_w("grad_w_proj_a", oa, dya, False, 512, 512, 512, 512)
    g_pb = _grad_w("grad_w_proj_b", ob16, dyb, False, 512, 512, 512, 512)
    red_up, token = exchange_end(2, ex_up, (g_out,))
    ex_mid, token_mid = exchange_begin(1, [g_pa, g_pb, g_out])
    cc = _comb_bwd(doa, os, ws, deps=(token, token_mid))
    bwd_a = [_attn_a_bwd(qkn, proj, doa, ls[g], ws[g], cc, g) for g in range(3)]
    red_mid, token = exchange_end(1, ex_mid, (bwd_a[2][1],))
    dqk_b, dv_b, drpb_t = _attn_b_bwd(qkn, proj, dob, ob, lse_b, bias, deps=(token,))
    dproj, dn = _qk_bwd(proj, nw, cos, sin, [b[0] for b in bwd_a], dqk_b, [b[1] for b in bwd_a] + [dv_b], dga, dgb)
    g_in = _grad_w("grad_w_in", xn, dproj, False, D, DIN // NSH, 1024, 1280)
    red_in, token = reduce_begin(0, [g_in])
    grad_x, d_norm_mix = _in_proj_bwd(dproj, win_f, x2, dh1, norm_mix, deps=(token,))

    sh_down, token = share_begin(3, red_down, (grad_x,))
    sh_up, token = share_begin(2, red_up, (token,))
    done = share_end(3, sh_down, (token,))
    sh_mid, token = share_begin(1, red_mid, (done,))
    done = share_end(2, sh_up, (token,))
    sh_in, token = share_begin(0, red_in, (done,))
    done = share_end(1, sh_mid, (token,))
    done = share_end(0, sh_in, (done,))

    d_rpb = drpb_t[:, :15, GRID_W - WIN_C:GRID_W + WIN_C - 1]
    small_g = [d_norm_mix, jnp.concatenate([dba, dbb], axis=1), dn[0, 0], dn[1, 0], dn[0, 1], dn[1, 1], d_rpb, d_norm_ffn]
    gathered_small = _allgather_small(_pack_small(small_g), done)
    small_w = (norm_mix, b_gate, q_norm_a, k_norm_a, q_norm_b, k_norm_b, rpb_b, norm_ffn)
    small_m = (m_norm_mix, m_b_gate, m_q_norm_a, m_k_norm_a, m_q_norm_b, m_k_norm_b, m_rpb_b, m_norm_ffn)
    small_v = (v_norm_mix, v_b_gate, v_q_norm_a, v_k_norm_a, v_q_norm_b, v_k_norm_b, v_rpb_b, v_norm_ffn)
    small_out = [_unpack_small(p) for p in
                 _adam_small(gathered_small, _pack_small(small_w), _pack_small(small_m), _pack_small(small_v))]

    order = ("norm_mix", "w_in", "b_gate", "q_norm_a", "k_norm_a", "q_norm_b", "k_norm_b", "rpb_b",
             "w_proj_a", "w_proj_b", "w_out", "norm_ffn", "w_up", "w_down")
    small_idx = {name: i for i, (name, _) in enumerate(SMALL)}
    outs = []
    for kind in range(4):
        for name in order:
            if name in small_idx:
                outs.append(small_out[kind][small_idx[name]])
            else:
                outs.append(big_out[big_names.index(name)][kind][None])
    return (loss, grad_x[None], *outs)
```

```python
import functools

import numpy as np
import jax
import jax.numpy as jnp
from jax import lax
from jax.experimental import pallas as pl
from jax.experimental.pallas import tpu as pltpu

F32, BF16 = jnp.float32, jnp.bfloat16
SDS = jax.ShapeDtypeStruct
MESH = pl.DeviceIdType.MESH

T = 2048
D = 2048
HD = 128
NH, NHA = 16, 12
DIN = 10240
DFF = 8192
NSH = 4
DILS = (1, 4, 16)
EPS = 1e-6
NEG = -1e30
SCALE = HD ** -0.5
GRID_W, WIN_R, WIN_C = 64, 8, 16
VMEM_LIMIT = 56 * 1024 * 1024
B1, B2, LR, AEPS, WD, STEP = 0.9, 0.999, 0.001, 1e-08, 0.01, 10
SMALL_ROWS = 88


def _dot(a, b):
    return jnp.dot(a, b, preferred_element_type=F32)


def _dot_nt(a, b):
    return lax.dot_general(a, b, (((1,), (1,)), ((), ())), preferred_element_type=F32)


def _dot_tn(a, b):
    return lax.dot_general(a, b, (((0,), (0,)), ((), ())), preferred_element_type=F32)


def _params(n):
    return pltpu.CompilerParams(dimension_semantics=("arbitrary",) * n, vmem_limit_bytes=VMEM_LIMIT)


def _resident(shape, index_map):
    return pl.BlockSpec(shape, index_map, pipeline_mode=pl.Buffered(1))


def _sigmoid(z):
    return 1.0 / (1.0 + jnp.exp(-z))


def _wide(v, n):
    return jnp.concatenate([v] * n, axis=1)


def _row_tile(rows, cols, elems):
    tr = 16
    while tr * 2 <= rows and tr * 2 * cols <= elems:
        tr *= 2
    return tr


def _place():
    x, y, c = lax.axis_index("x"), lax.axis_index("y"), lax.axis_index("c")
    peers = [(1 - x, y), (x, 1 - y), (1 - x, 1 - y)]
    return x, y, c, peers


def _cast_into_place(w, name, place, deps=()):
    rows, cols = w.shape
    hr = rows // 2
    tr = min(hr, 256)
    per = hr // tr

    def body(*refs):
        w_ref, o_ref = refs[-2:]
        o_ref[...] = w_ref[...].astype(BF16)

    return pl.pallas_call(
        body, name=name, out_shape=SDS((NSH, 2, hr, cols), BF16),
        grid_spec=pltpu.PrefetchScalarGridSpec(
            num_scalar_prefetch=1, grid=(2, per),
            in_specs=[DEP_SPEC] * len(deps) + [pl.BlockSpec((tr, cols), lambda h, i, p: (h * per + i, 0))],
            out_specs=pl.BlockSpec((None, None, tr, cols), lambda h, i, p: (p[0], h, i, 0))),
        compiler_params=_params(2))(place, *deps, w)


ANY_SPEC = pl.BlockSpec(memory_space=pl.ANY)
HBM_SPEC = pl.BlockSpec(memory_space=pltpu.HBM)
SEM_SPEC = pl.BlockSpec(memory_space=pltpu.SEMAPHORE)
DEP_SPEC = pl.BlockSpec((8, 128), lambda *_: (0, 0))
EFFECT = pltpu.SideEffectType.DATAFLOW_SIDE_EFFECTING


def _after(body, deps):
    n = len(deps)
    return (lambda *refs: body(*refs[n:])) if n else body


SIBLING_BARRIER = 1


def _split_start(name, srcs, lands, n_copies, issue, sibling_only=False):
    n, m = len(srcs), len(lands)

    def body(*refs):
        if sibling_only:
            x, y, c, _ = _place()
            barrier = pltpu.get_barrier_semaphore()
            pl.semaphore_signal(barrier, inc=1, device_id=(x, y, 1 - c), device_id_type=MESH)
            pl.semaphore_wait(barrier, 1)
        issue(refs[:n], refs[n:n + m], refs[n + m], refs[n + m + 1])
        refs[-1][...] = jnp.zeros((8, 128), F32)

    arrays = list(srcs) + list(lands)
    outs = pl.pallas_call(
        body, name=name,
        out_shape=(pltpu.SemaphoreType.DMA((n_copies,)), pltpu.SemaphoreType.DMA((n_copies,)),
                   *[pltpu.HBM(a.shape, a.dtype) for a in arrays], SDS((8, 128), F32)),
        in_specs=[HBM_SPEC] * (n + m),
        out_specs=(SEM_SPEC, SEM_SPEC, *[HBM_SPEC] * (n + m), pl.BlockSpec(memory_space=pltpu.VMEM)),
        input_output_aliases={i: 2 + i for i in range(n + m)},
        compiler_params=pltpu.CompilerParams(has_side_effects=EFFECT,
                                             collective_id=SIBLING_BARRIER if sibling_only else None),
    )(*[pltpu.with_memory_space_constraint(a, pltpu.HBM) for a in arrays])
    return outs[0], outs[1], list(outs[2:2 + n]), list(outs[2 + n:2 + n + m]), outs[-1]


def _split_wait(name, send_sems, recv_sems, srcs, lands, after, wait):
    n, m = len(srcs), len(lands)

    def body(*refs):
        wait(refs[:n], refs[n:n + m], refs[n + m], refs[n + m + 1])

    arrays = list(srcs) + list(lands)
    outs = pl.pallas_call(
        body, name=name, out_shape=[pltpu.HBM(a.shape, a.dtype) for a in arrays],
        in_specs=[HBM_SPEC] * (n + m) + [SEM_SPEC, SEM_SPEC] + [ANY_SPEC] * len(after),
        out_specs=[HBM_SPEC] * (n + m), input_output_aliases={i: i for i in range(n + m)},
        compiler_params=pltpu.CompilerParams(has_side_effects=EFFECT),
    )(*arrays, send_sems, recv_sems, *after)
    return list(outs[:n]), list(outs[n:])


def _gather_start(name, fulls):
    def issue(srcs, dsts, send_sems, recv_sems):
        x, y, c, peers = _place()
        for i in range(len(fulls)):
            mine = dsts[i].at[2 * x + y, c]
            for k, (px, py) in enumerate(peers):
                pltpu.make_async_remote_copy(
                    src_ref=mine, dst_ref=mine, send_sem=send_sems.at[3 * i + k],
                    recv_sem=recv_sems.at[3 * i + k], device_id=(px, py, c), device_id_type=MESH).start()

    return _split_start(name, [], fulls, 3 * len(fulls), issue)


def _gather_wait(name, send_sems, recv_sems, fulls, after, ks=(0, 1, 2)):
    def wait(srcs, dsts, send_sems, recv_sems):
        x, y, c, peers = _place()
        for i in range(len(fulls)):
            for k in ks:
                px, py = peers[k]
                cp = pltpu.make_async_remote_copy(
                    src_ref=dsts[i].at[2 * x + y, c], dst_ref=dsts[i].at[2 * px + py, c],
                    send_sem=send_sems.at[3 * i + k], recv_sem=recv_sems.at[3 * i + k],
                    device_id=(px, py, c), device_id_type=MESH)
                cp.wait_send()
                cp.wait_recv()

    return _split_wait(name, send_sems, recv_sems, [], fulls, after, wait)[1]


def _gather_finish(name, fulls, ks=(0, 1, 2)):
    n = len(fulls)

    def body(*refs):
        fin, fout = refs[:n], refs[n:2 * n]
        send_sems, recv_sems = refs[2 * n:]
        x, y, c, peers = _place()

        def copy(i, k, half):
            px, py = peers[k]
            return pltpu.make_async_remote_copy(
                src_ref=fin[i].at[2 * px + py, half], dst_ref=fout[i].at[2 * px + py, half],
                send_sem=send_sems.at[3 * i + k], recv_sem=recv_sems.at[3 * i + k],
                device_id=(x, y, 1 - c), device_id_type=MESH)

        sends = [copy(i, k, c) for i in range(n) for k in ks]
        for cp in sends:
            cp.start()
        for i in range(n):
            for k in ks:
                copy(i, k, 1 - c).wait_recv()
        for cp in sends:
            cp.wait_send()

    return pl.pallas_call(
        body, name=name, out_shape=[SDS(f.shape, f.dtype) for f in fulls],
        in_specs=[ANY_SPEC] * n, out_specs=[ANY_SPEC] * n, input_output_aliases={i: i for i in range(n)},
        scratch_shapes=[pltpu.SemaphoreType.DMA((3 * n,)), pltpu.SemaphoreType.DMA((3 * n,))])(*fulls)


def _reduce_exchange(name, grads):
    n = len(grads)

    def body(*refs):
        ins, theirs = refs[:n], refs[n:2 * n]
        send_sems, recv_sems = refs[2 * n:]
        x, y, c, _ = _place()
        copies = []
        for i in range(n):
            cp = pltpu.make_async_remote_copy(
                src_ref=ins[i].at[:, 1 - c], dst_ref=theirs[i], send_sem=send_sems.at[i],
                recv_sem=recv_sems.at[i], device_id=(x, y, 1 - c), device_id_type=MESH)
            cp.start()
            copies.append(cp)
        for cp in copies:
            cp.wait_recv()
            cp.wait_send()

    return pl.pallas_call(
        body, name=name, out_shape=[SDS((NSH,) + g.shape[2:], g.dtype) for g in grads],
        in_specs=[ANY_SPEC] * n, out_specs=[ANY_SPEC] * n,
        scratch_shapes=[pltpu.SemaphoreType.DMA((n,)), pltpu.SemaphoreType.DMA((n,))])(*grads)


def _reduce_start(name, parts):
    lands = [lax.empty((3,) + p.shape[1:], p.dtype) for p in parts]

    def issue(srcs, dsts, send_sems, recv_sems):
        x, y, c, peers = _place()
        for i in range(len(parts)):
            for k, (px, py) in enumerate(peers):
                pltpu.make_async_remote_copy(
                    src_ref=srcs[i].at[2 * px + py], dst_ref=dsts[i].at[k], send_sem=send_sems.at[3 * i + k],
                    recv_sem=recv_sems.at[3 * i + k], device_id=(px, py, c), device_id_type=MESH).start()

    return _split_start(name, parts, lands, 3 * len(parts), issue)


def _reduce_wait(name, send_sems, recv_sems, parts, lands, after):
    def wait(srcs, dsts, send_sems, recv_sems):
        x, y, c, peers = _place()
        for i in range(len(parts)):
            for k, (px, py) in enumerate(peers):
                cp = pltpu.make_async_remote_copy(
                    src_ref=srcs[i].at[2 * px + py], dst_ref=dsts[i].at[k], send_sem=send_sems.at[3 * i + k],
                    recv_sem=recv_sems.at[3 * i + k], device_id=(px, py, c), device_id_type=MESH)
                cp.wait_send()
                cp.wait_recv()

    return _split_wait(name, send_sems, recv_sems, parts, lands, after, wait)


def _sibling_copy(src, dst, send_sems, recv_sems, k):
    x, y, c, _ = _place()
    return pltpu.make_async_remote_copy(src_ref=src, dst_ref=dst, send_sem=send_sems.at[k], recv_sem=recv_sems.at[k],
                                        device_id=(x, y, 1 - c), device_id_type=MESH)


def _forward_start(name, fulls):
    def issue(srcs, dsts, send_sems, recv_sems):
        x, y, c, peers = _place()
        for i in range(len(fulls)):
            for k, (px, py) in enumerate(peers):
                part = dsts[i].at[2 * px + py, c]
                _sibling_copy(part, part, send_sems, recv_sems, 3 * i + k).start()

    return _split_start(name, [], fulls, 3 * len(fulls), issue, sibling_only=True)


def _forward_wait(name, send_sems, recv_sems, fulls, after):
    def wait(srcs, dsts, send_sems, recv_sems):
        x, y, c, peers = _place()
        for i in range(len(fulls)):
            for k, (px, py) in enumerate(peers):
                cp = _sibling_copy(dsts[i].at[2 * px + py, c], dsts[i].at[2 * px + py, 1 - c], send_sems, recv_sems, 3 * i + k)
                cp.wait_send()
                cp.wait_recv()

    return _split_wait(name, send_sems, recv_sems, [], fulls, after, wait)[1]


def _exchange_start(name, grads):
    lands = [lax.empty((NSH,) + g.shape[2:], g.dtype) for g in grads]

    def issue(srcs, dsts, send_sems, recv_sems):
        c = lax.axis_index("c")
        for i in range(len(grads)):
            _sibling_copy(srcs[i].at[:, 1 - c], dsts[i], send_sems, recv_sems, i).start()

    return _split_start(name, grads, lands, len(grads), issue, sibling_only=True)


def _exchange_wait(name, send_sems, recv_sems, grads, lands, after):
    def wait(srcs, dsts, send_sems, recv_sems):
        c = lax.axis_index("c")
        for i in range(len(grads)):
            cp = _sibling_copy(srcs[i].at[:, 1 - c], dsts[i], send_sems, recv_sems, i)
            cp.wait_send()
            cp.wait_recv()

    return _split_wait(name, send_sems, recv_sems, grads, lands, after, wait)


def _share_start(name, sums):
    def issue(srcs, dsts, send_sems, recv_sems):
        c = lax.axis_index("c")
        for i in range(len(sums)):
            _sibling_copy(dsts[i].at[c], dsts[i].at[c], send_sems, recv_sems, i).start()

    return _split_start(name, [], sums, len(sums), issue, sibling_only=True)


def _share_wait(name, send_sems, recv_sems, sums, after):
    def wait(srcs, dsts, send_sems, recv_sems):
        c = lax.axis_index("c")
        for i in range(len(sums)):
            cp = _sibling_copy(dsts[i].at[c], dsts[i].at[1 - c], send_sems, recv_sems, i)
            cp.wait_send()
            cp.wait_recv()

    return _split_wait(name, send_sems, recv_sems, [], sums, after, wait)[1]


def _allgather_small(v, after):
    m_per, n = v.shape

    def body(x_ref, after_ref, out_ref, send_sems, recv_sems, local_sem):
        x, y, c = lax.axis_index("x"), lax.axis_index("y"), lax.axis_index("c")
        me, sibling = (x, y, c), (x, y, 1 - c)
        chips = [(1 - x, y), (x, 1 - y), (1 - x, 1 - y)]

        def rows(px, py, pc):
            return out_ref.at[pl.ds((4 * px + 2 * py + pc) * m_per, m_per), :]

        def copy(k, block, to, src=None):
            return pltpu.make_async_remote_copy(
                src_ref=rows(*block) if src is None else src, dst_ref=rows(*block),
                send_sem=send_sems.at[k], recv_sem=recv_sems.at[k], device_id=to, device_id_type=MESH)

        mine = pltpu.make_async_copy(x_ref, rows(*me), local_sem)
        mine.start()
        first = [copy(0, me, sibling, src=x_ref)]
        first += [copy(1 + j, me, (*chip, c), src=x_ref) for j, chip in enumerate(chips)]
        for cp in first:
            cp.start()
        passed = [copy(4 + j, (*chip, c), sibling) for j, chip in enumerate(chips)]
        for j, chip in enumerate(chips):
            copy(1 + j, (*chip, c), me).wait_recv()
            passed[j].start()
        copy(0, sibling, me).wait_recv()
        for j, chip in enumerate(chips):
            copy(4 + j, (*chip, 1 - c), me).wait_recv()
        for cp in first + passed:
            cp.wait_send()
        mine.wait()

    return pl.pallas_call(
        body, name="allgather_small", out_shape=SDS((8 * m_per, n), v.dtype),
        in_specs=[pl.BlockSpec(memory_space=pltpu.VMEM), ANY_SPEC], out_specs=pl.BlockSpec(memory_space=pltpu.VMEM),
        scratch_shapes=[pltpu.SemaphoreType.DMA((7,)), pltpu.SemaphoreType.DMA((7,)), pltpu.SemaphoreType.DMA])(v, after)


def _norm_in_proj_own(x, g, w_full, place):
    tn, chunk = 512, 256
    per = (DIN // NSH) // tn

    def body(place_ref, x_ref, g_ref, w_ref, proj_ref, xn_ref):
        @pl.when(pl.program_id(0) == 0)
        def _():
            def norm(r, carry):
                rows = pl.ds(pl.multiple_of(r * chunk, chunk), chunk)
                xv = x_ref[rows, :]
                rs = lax.rsqrt(jnp.mean(xv * xv, axis=-1, keepdims=True) + EPS)
                xn_ref[rows, :] = (xv * rs * g_ref[...]).astype(BF16)
                return carry

            lax.fori_loop(0, T // chunk, norm, 0)

        proj_ref[...] = _dot(xn_ref[...], w_ref[...])

    return pl.pallas_call(
        body, name="norm_in_proj_own", out_shape=[SDS((T, DIN), F32), SDS((T, D), BF16)],
        grid_spec=pltpu.PrefetchScalarGridSpec(
            num_scalar_prefetch=1, grid=(per,),
            in_specs=[_resident((T, D), lambda j, p: (0, 0)),
                      pl.BlockSpec((1, D), lambda j, p: (0, 0)),
                      pl.BlockSpec((None, D, tn), lambda j, p: (p[0], 0, j))],
            out_specs=[pl.BlockSpec((T, tn), lambda j, p: (0, p[0] * per + j)),
                       pl.BlockSpec((T, D), lambda j, p: (0, 0))]),
        compiler_params=_params(1))(place, x, g, w_full)


def _in_proj_rest(name, xn, w_full, proj, place, flips):
    tn = 512
    per = (DIN // NSH) // tn

    def body(place_ref, xn_ref, w_ref, proj_in, proj_ref):
        proj_ref[...] = _dot(xn_ref[...], w_ref[...])

    def shard(j, p):
        flip = flips[0]
        for n, f in enumerate(flips[1:]):
            flip = jnp.where(j // per == n + 1, f, flip)
        return p[0] ^ flip

    return pl.pallas_call(
        body, name=name, out_shape=SDS((T, DIN), F32),
        grid_spec=pltpu.PrefetchScalarGridSpec(
            num_scalar_prefetch=1, grid=(len(flips) * per,),
            in_specs=[_resident((T, D), lambda j, p: (0, 0)),
                      pl.BlockSpec((None, D, tn), lambda j, p: (shard(j, p), 0, j % per)), ANY_SPEC],
            out_specs=pl.BlockSpec((T, tn), lambda j, p: (0, shard(j, p) * per + j % per))),
        input_output_aliases={3: 0}, compiler_params=_params(1))(place, xn, w_full, proj)


def _rope_tables():
    pos = np.arange(T, dtype=np.float32)
    inv = (10000.0 ** (-np.arange(0, HD, 2, dtype=np.float32) / HD)).astype(np.float32)
    ang = (pos[:, None] * inv[None, :]).astype(np.float32)
    cos, sin = np.cos(ang).astype(np.float32), np.sin(ang).astype(np.float32)
    return (jnp.asarray(np.concatenate([cos, cos], axis=1)), jnp.asarray(np.concatenate([-sin, sin], axis=1)))


def _qk_prep(proj, nw, cos, sin):
    tm = 256

    def body(p_ref, w_ref, cos_ref, sin_ref, o_ref):
        cv, sv = cos_ref[...], sin_ref[...]
        for h in range(NH):
            sl = slice(h * HD, (h + 1) * HD)
            xv = p_ref[:, sl]
            r = lax.rsqrt(jnp.mean(xv * xv, axis=-1, keepdims=True) + EPS)
            z = xv * r * w_ref[:, sl]
            if h < NHA:
                z = z * cv + pltpu.roll(z, 64, 1) * sv
            o_ref[:, sl] = z.astype(BF16)

    return pl.pallas_call(
        body, name="qk_prep", out_shape=SDS((T, 2 * D), BF16), grid=(T // tm, 2),
        in_specs=[pl.BlockSpec((tm, D), lambda i, j: (i, j)),
                  pl.BlockSpec((None, 1, D), lambda i, j: (j, 0, 0)),
                  pl.BlockSpec((tm, HD), lambda i, j: (i, 0)),
                  pl.BlockSpec((tm, HD), lambda i, j: (i, 0))],
        out_specs=pl.BlockSpec((tm, D), lambda i, j: (i, j)),
        compiler_params=_params(2))(proj, nw, cos, sin)


def _band_mask(q0, m):
    ii = lax.broadcasted_iota(jnp.int32, (128, 256), 0)
    jj = lax.broadcasted_iota(jnp.int32, (128, 256), 1)
    rel = jj - ii
    kpos = jj + (q0 - 64)
    return (rel >= 0) & (rel <= 128) & (kpos >= 0) & (kpos < m)


def _fill_padded(dst, src, m):
    zeros = jnp.zeros((64, HD), dst.dtype)
    dst[0:64, :] = zeros
    dst[64 + m:128 + m, :] = zeros
    dst[64:64 + m, :] = src.astype(dst.dtype)


def _group_views(qkn, proj, g):
    m = T // DILS[g]
    cols = (qkn[:, g * 512:(g + 1) * 512], qkn[:, D + g * 512:D + (g + 1) * 512],
            proj[:, 2 * D + g * 512:2 * D + (g + 1) * 512])
    return [a.reshape(m, DILS[g] * 512) for a in cols]


def _heads_per_step(m):
    return 4 if m <= 512 else 1


def _attn_a_fwd(qkn, proj, g):
    dil = DILS[g]
    m = T // dil
    nb = m // 128
    hp = _heads_per_step(m)

    def body(q_ref, k_ref, v_ref, o_ref, l_ref, kp, vp):
        for hh in range(hp):
            sl = slice(hh * HD, (hh + 1) * HD)
            _fill_padded(kp, k_ref[:, sl], m)
            _fill_padded(vp, v_ref[:, sl], m)

            def block(b, carry):
                q0 = pl.multiple_of(b * 128, 128)
                kw, vw = kp[pl.ds(q0, 256), :], vp[pl.ds(q0, 256), :]
                s = _dot_nt(q_ref[pl.ds(q0, 128), sl], kw) * SCALE
                s = jnp.where(_band_mask(q0, m), s, NEG)
                mx = jnp.max(s, axis=-1, keepdims=True)
                e = jnp.exp(s - mx)
                den = jnp.sum(e, axis=-1, keepdims=True)
                o_ref[pl.ds(q0, 128), sl] = _dot((e / den).astype(BF16), vw)
                l_ref[pl.ds(q0, 128), sl] = jnp.broadcast_to(mx + jnp.log(den), (128, HD))
                return carry

            lax.fori_loop(0, nb, block, 0, unroll=min(nb, 2))

    blk = pl.BlockSpec((m, hp * HD), lambda h, r: (0, r * (4 // hp) + h))
    o, lse = pl.pallas_call(
        body, name=f"attn_a_fwd_{g}", out_shape=[SDS((m, dil * 512), F32)] * 2, grid=(4 // hp, dil),
        in_specs=[blk] * 3, out_specs=[blk] * 2,
        scratch_shapes=[pltpu.VMEM((m + 128, HD), BF16), pltpu.VMEM((m + 128, HD), BF16)],
        compiler_params=_params(2))(*_group_views(qkn, proj, g))
    return o.reshape(T, 512), lse.reshape(T, 512)


def _nbr_window(r):
    start = jnp.clip(r - WIN_R // 2, 0, T // GRID_W - WIN_R)
    return start, start - r + (WIN_R - 1)


def _rpb_rows(rpb):
    zeros = jnp.zeros((4, 14, 33), F32)
    a, b = rpb[:, :14], rpb[:, 1:15]
    rows = jnp.concatenate([a[:, :, 15:31], zeros, b, zeros, a[:, :, 0:15]], axis=2)
    return jnp.pad(rows, ((0, 0), (0, 2), (0, 0)))


def _attn_b_fwd(qkn, proj, rpb_rows):
    def body(r_ref, q_ref, k_ref, v_ref, o_ref, l_ref, bias_ref, vb, pair):
        qc = lax.broadcasted_iota(jnp.int32, (GRID_W, 512), 0)
        kc = lax.broadcasted_iota(jnp.int32, (GRID_W, 512), 1) & (GRID_W - 1)
        cs = jnp.clip(qc - WIN_C // 2, 0, GRID_W - WIN_C)
        colmask = (kc >= cs) & (kc < cs + WIN_C)
        for d in range(14):
            pair[d] = pltpu.roll(jnp.broadcast_to(r_ref[d:d + 1, :], (GRID_W, HD)), 0, 1, stride=1, stride_axis=0)
        for off in range(8):
            rows = jnp.concatenate([pair[off + 2 * jj] for jj in range(4)], axis=1)
            bias_ref[off] = jnp.where(colmask, rows, NEG)
        vb[...] = v_ref[...].astype(BF16)

        def row(r, carry):
            start, off = _nbr_window(r)
            q0 = pl.multiple_of(r * GRID_W, GRID_W)
            k0 = pl.multiple_of(start * GRID_W, GRID_W)
            s = _dot_nt(q_ref[pl.ds(q0, GRID_W), :], k_ref[pl.ds(k0, 512), :]) * SCALE + bias_ref[off]
            mx = jnp.max(s, axis=-1, keepdims=True)
            e = jnp.exp(s - mx)
            den = jnp.sum(e, axis=-1, keepdims=True)
            o_ref[pl.ds(q0, GRID_W), :] = _dot((e / den).astype(BF16), vb[pl.ds(k0, 512), :])
            l_ref[pl.ds(q0, GRID_W), :] = jnp.broadcast_to(mx + jnp.log(den), (GRID_W, HD))
            return carry

        lax.fori_loop(0, T // GRID_W, row, 0, unroll=2)

    return pl.pallas_call(
        body, name="attn_b_fwd",
        out_shape=[SDS((T, 512), F32), SDS((T, 512), F32), SDS((4, 8, GRID_W, 512), F32)], grid=(4,),
        in_specs=[pl.BlockSpec((None, 16, HD), lambda h: (h, 0, 0)),
                  pl.BlockSpec((T, HD), lambda h: (0, NHA + h)),
                  pl.BlockSpec((T, HD), lambda h: (0, NH + NHA + h)),
                  pl.BlockSpec((T, HD), lambda h: (0, 2 * NH + NHA + h))],
        out_specs=[pl.BlockSpec((T, HD), lambda h: (0, h)), pl.BlockSpec((T, HD), lambda h: (0, h)),
                   pl.BlockSpec((None, 8, GRID_W, 512), lambda h: (h, 0, 0, 0))],
        scratch_shapes=[pltpu.VMEM((T, HD), BF16), pltpu.VMEM((14, GRID_W, HD), F32)],
        compiler_params=_params(1))(rpb_rows, qkn, qkn, proj)


def _comb_fwd(os, ls):
    tm = 512

    def body(o0, o1, o2, l0, l1, l2, oa_ref, w0, w1, w2):
        lv = [l0[...], l1[...], l2[...]]
        mx = jnp.maximum(jnp.maximum(lv[0], lv[1]), lv[2])
        ev = [jnp.exp(l - mx) for l in lv]
        den = ev[0] + ev[1] + ev[2]
        wv = [e / den for e in ev]
        oa_ref[...] = (wv[0] * o0[...] + wv[1] * o1[...] + wv[2] * o2[...]).astype(BF16)
        w0[...], w1[...], w2[...] = wv

    spec = pl.BlockSpec((tm, 512), lambda i: (i, 0))
    return pl.pallas_call(
        body, name="comb_fwd", out_shape=[SDS((T, 512), BF16)] + [SDS((T, 512), F32)] * 3, grid=(T // tm,),
        in_specs=[spec] * 6, out_specs=[spec] * 4, compiler_params=_params(1))(*os, *ls)


def _mix_fwd(oa, ob, proj, b_gate, wpa, wpb):
    tm = 512

    def body(oa_ref, ob_ref, ga_ref, gb_ref, ba_ref, bb_ref, wpa_ref, wpb_ref, mixed_ref, ob16_ref):
        oav = oa_ref[...]
        obv = ob_ref[...].astype(BF16)
        ob16_ref[...] = obv
        for s in range(NSH):
            sl = slice(s * 512, (s + 1) * 512)
            ga = _sigmoid(ga_ref[:, sl] + ba_ref[:, sl])
            gb = _sigmoid(gb_ref[:, sl] + bb_ref[:, sl])
            mixed_ref[:, sl] = (ga * _dot(oav, wpa_ref[s]) + gb * _dot(obv, wpb_ref[s])).astype(BF16)

    row = lambda w: pl.BlockSpec((tm, w), lambda i: (i, 0))
    return pl.pallas_call(
        body, name="mix_fwd", out_shape=[SDS((T, D), BF16), SDS((T, 512), BF16)], grid=(T // tm,),
        in_specs=[row(512), row(512),
                  pl.BlockSpec((tm, D), lambda i: (i, 3)), pl.BlockSpec((tm, D), lambda i: (i, 4)),
                  pl.BlockSpec((1, D), lambda i: (0, 0)), pl.BlockSpec((1, D), lambda i: (0, 1)),
                  _resident((NSH, 512, 512), lambda i: (0, 0, 0)), _resident((NSH, 512, 512), lambda i: (0, 0, 0))],
        out_specs=[row(D), row(512)], compiler_params=_params(1))(oa, ob, proj, proj, b_gate, b_gate, wpa, wpb)


def _out_proj_fwd(mixed, w_out, x, g):
    tm = 512

    def body(m_ref, w_ref, x_ref, g_ref, h1_ref, hn_ref):
        h1 = x_ref[...] + _dot(m_ref[...], w_ref[...])
        h1_ref[...] = h1
        r = lax.rsqrt(jnp.mean(h1 * h1, axis=-1, keepdims=True) + EPS)
        hn_ref[...] = (h1 * r * g_ref[...]).astype(BF16)

    row = pl.BlockSpec((tm, D), lambda i: (i, 0))
    return pl.pallas_call(
        body, name="out_proj_fwd", out_shape=[SDS((T, D), F32), SDS((T, D), BF16)], grid=(T // tm,),
        in_specs=[row, _resident((D, D), lambda i: (0, 0)), row, pl.BlockSpec((1, D), lambda i: (0, 0))],
        out_specs=[row, row], compiler_params=_params(1))(mixed, w_out, x, g)


def _ffn_up(hn, w_up):
    tm, tn = T, 512
    per = (DFF // NSH) // tn

    def body(h_ref, w_ref, a_ref, u_ref):
        uv = jnp.maximum(_dot(h_ref[...], w_ref[...]), 0.0)
        a_ref[...] = (uv * uv).astype(BF16)
        u_ref[...] = uv.astype(BF16)

    out = pl.BlockSpec((tm, tn), lambda i, j: (i, j))
    return pl.pallas_call(
        body, name="ffn_up", out_shape=[SDS((T, DFF), BF16)] * 2, grid=(T // tm, DFF // tn),
        in_specs=[pl.BlockSpec((tm, D), lambda i, j: (i, 0)),
                  pl.BlockSpec((None, D, tn), lambda i, j: (j // per, 0, j % per))],
        out_specs=[out, out], compiler_params=_params(2))(hn, w_up)


def _ffn_down_loss(u, w_down, h1, target):
    tm, tk = 512, 2048
    nk = DFF // tk

    def body(u_ref, w_ref, h1_ref, t_ref, dy_ref, dy16_ref, loss_ref, acc):
        k = pl.program_id(1)

        @pl.when(k == 0)
        def _():
            acc[...] = jnp.zeros_like(acc)

        acc[...] += _dot(u_ref[...], w_ref[...])

        @pl.when(k == nk - 1)
        def _():
            def chunk(r, sq):
                rows = pl.ds(pl.multiple_of(r * 16, 16), 16)
                err = acc[rows, :] + h1_ref[rows, :] - t_ref[rows, :]
                dy = err * (1.0 / D)
                dy_ref[rows, :] = dy
                dy16_ref[rows, :] = dy.astype(BF16)
                return sq + err * err

            sq = lax.fori_loop(0, tm // 16, chunk, jnp.zeros((16, D), F32), unroll=2)
            part = 0.5 * jnp.sum(jnp.mean(sq, axis=-1, keepdims=True), axis=0, keepdims=True)
            loss_ref[...] = jnp.broadcast_to(part, (8, 128))

    row = pl.BlockSpec((tm, D), lambda i, k: (i, 0))
    once = _resident((tm, D), lambda i, k: (i, 0))
    return pl.pallas_call(
        body, name="ffn_down_loss",
        out_shape=[SDS((T, D), F32), SDS((T, D), BF16), SDS((T // tm, 8, 128), F32)], grid=(T // tm, nk),
        in_specs=[pl.BlockSpec((tm, tk), lambda i, k: (i, k)), pl.BlockSpec((tk, D), lambda i, k: (k, 0)), once, once],
        out_specs=[row, row, pl.BlockSpec((None, 8, 128), lambda i, k: (i, 0, 0))],
        scratch_shapes=[pltpu.VMEM((tm, D), F32)], compiler_params=_params(2))(u, w_down, h1, target)


def _ffn_down_bwd(dy16, w_down, u, deps=()):
    tm, tn = T, 512

    def body(dy_ref, w_ref, u_ref, du_ref):
        uv = u_ref[...].astype(F32)
        du_ref[...] = jnp.where(uv > 0.0, 2.0 * uv * _dot_nt(dy_ref[...], w_ref[...]), 0.0).astype(BF16)

    return pl.pallas_call(
        _after(body, deps), name="ffn_down_bwd", out_shape=SDS((T, DFF), BF16), grid=(T // tm, DFF // tn),
        in_specs=[DEP_SPEC] * len(deps) + [
            pl.BlockSpec((tm, D), lambda i, j: (i, 0)), pl.BlockSpec((tn, D), lambda i, j: (j, 0)),
            pl.BlockSpec((tm, tn), lambda i, j: (i, j))],
        out_specs=pl.BlockSpec((tm, tn), lambda i, j: (i, j)), compiler_params=_params(2))(*deps, dy16, w_down, u)


def _norm_bwd(xv, dz_in, g):
    r = lax.rsqrt(jnp.mean(xv * xv, axis=-1, keepdims=True) + EPS)
    dg = jnp.sum(xv * r * dz_in, axis=0, keepdims=True)
    dz = dz_in * g
    dx = r * dz - xv * (r * r * r) * jnp.mean(xv * dz, axis=-1, keepdims=True)
    return dx, dg


def _ffn_up_bwd(du, w_up, h1, dy, g, deps=()):
    tm, tk = 512, 1024
    per = (DFF // NSH) // tk
    nk = DFF // tk

    def body(du_ref, w_ref, h1_ref, dy_ref, g_ref, dh1_ref, dh16_ref, dg_ref, acc):
        i, k = pl.program_id(0), pl.program_id(1)

        @pl.when(k == 0)
        def _():
            acc[...] = jnp.zeros_like(acc)

        @pl.when((k == 0) & (i == 0))
        def _():
            dg_ref[...] = jnp.zeros_like(dg_ref)

        acc[...] += _dot_nt(du_ref[...], w_ref[...])

        @pl.when(k == nk - 1)
        def _():
            dx, dg = _norm_bwd(h1_ref[...], acc[...], g_ref[...])
            dh1 = dy_ref[...] + dx
            dh1_ref[...] = dh1
            dh16_ref[...] = dh1.astype(BF16)
            dg_ref[...] += dg

    row = pl.BlockSpec((tm, D), lambda i, k: (i, 0))
    vec = pl.BlockSpec((1, D), lambda i, k: (0, 0))
    return pl.pallas_call(
        _after(body, deps), name="ffn_up_bwd", out_shape=[SDS((T, D), F32), SDS((T, D), BF16), SDS((1, D), F32)],
        grid=(T // tm, nk),
        in_specs=[DEP_SPEC] * len(deps) + [
            pl.BlockSpec((tm, tk), lambda i, k: (i, k)),
            pl.BlockSpec((None, D, tk), lambda i, k: (k // per, 0, k % per)), row, row, vec],
        out_specs=[row, row, vec], scratch_shapes=[pltpu.VMEM((tm, D), F32)],
        compiler_params=_params(2))(*deps, du, w_up, h1, dy, g)


def _mix_bwd(dh16, w_out, oa, ob16, proj, b_gate, wpa, wpb):
    tm = 256

    def body(dh_ref, wo_ref, oa_ref, ob_ref, ga_ref, gb_ref, ba_ref, bb_ref, wpa_ref, wpb_ref,
             dya_ref, dyb_ref, dga_ref, dgb_ref, doa_ref, dob_ref, dba_ref, dbb_ref):
        @pl.when(pl.program_id(0) == 0)
        def _():
            dba_ref[...] = jnp.zeros_like(dba_ref)
            dbb_ref[...] = jnp.zeros_like(dbb_ref)

        oav, obv = oa_ref[...], ob_ref[...]
        doa = jnp.zeros((tm, 512), F32)
        dob = jnp.zeros((tm, 512), F32)
        for s in range(NSH):
            sl = slice(s * 512, (s + 1) * 512)
            dm = _dot_nt(dh_ref[...], wo_ref[sl, :])
            ga = _sigmoid(ga_ref[:, sl] + ba_ref[:, sl])
            gb = _sigmoid(gb_ref[:, sl] + bb_ref[:, sl])
            dya = (dm * ga).astype(BF16)
            dyb = (dm * gb).astype(BF16)
            dza = dm * _dot(oav, wpa_ref[s]) * ga * (1.0 - ga)
            dzb = dm * _dot(obv, wpb_ref[s]) * gb * (1.0 - gb)
            dya_ref[:, sl], dyb_ref[:, sl] = dya, dyb
            dga_ref[:, sl], dgb_ref[:, sl] = dza.astype(BF16), dzb.astype(BF16)
            dba_ref[:, sl] += jnp.sum(dza, axis=0, keepdims=True)
            dbb_ref[:, sl] += jnp.sum(dzb, axis=0, keepdims=True)
            doa += _dot_nt(dya, wpa_ref[s])
            dob += _dot_nt(dyb, wpb_ref[s])
        doa_ref[...], dob_ref[...] = doa, dob

    row = lambda w: pl.BlockSpec((tm, w), lambda i: (i, 0))
    vec = pl.BlockSpec((1, D), lambda i: (0, 0))
    wp = _resident((NSH, 512, 512), lambda i: (0, 0, 0))
    return pl.pallas_call(
        body, name="mix_bwd",
        out_shape=[SDS((T, D), BF16)] * 4 + [SDS((T, 512), F32)] * 2 + [SDS((1, D), F32)] * 2, grid=(T // tm,),
        in_specs=[row(D), _resident((D, D), lambda i: (0, 0)), row(512), row(512),
                  pl.BlockSpec((tm, D), lambda i: (i, 3)), pl.BlockSpec((tm, D), lambda i: (i, 4)),
                  pl.BlockSpec((1, D), lambda i: (0, 0)), pl.BlockSpec((1, D), lambda i: (0, 1)), wp, wp],
        out_specs=[row(D)] * 4 + [row(512)] * 2 + [vec] * 2,
        compiler_params=_params(1))(dh16, w_out, oa, ob16, proj, proj, b_gate, b_gate, wpa, wpb)


def _comb_bwd(doa, os, ws, deps=()):
    tm = 512

    def body(d_ref, o0, o1, o2, w0, w1, w2, cc_ref):
        prod = d_ref[...] * (w0[...] * o0[...] + w1[...] * o1[...] + w2[...] * o2[...])
        for h in range(4):
            sl = slice(h * HD, (h + 1) * HD)
            cc_ref[:, sl] = jnp.broadcast_to(jnp.sum(prod[:, sl], axis=-1, keepdims=True), (tm, HD))

    spec = pl.BlockSpec((tm, 512), lambda i: (i, 0))
    return pl.pallas_call(
        _after(body, deps), name="comb_bwd", out_shape=SDS((T, 512), F32), grid=(T // tm,),
        in_specs=[DEP_SPEC] * len(deps) + [spec] * 7, out_specs=spec,
        compiler_params=_params(1))(*deps, doa, *os, *ws)


def _attn_a_bwd(qkn, proj, doa, lse, w, cc, g):
    dil = DILS[g]
    m = T // dil
    nb = m // 128
    hp = _heads_per_step(m)

    def body(q_ref, k_ref, v_ref, d_ref, l_ref, w_ref, c_ref, dqk_ref, dv_ref, kp, vp, dkp, dvp):
        for hh in range(hp):
            sl = slice(hh * HD, (hh + 1) * HD)
            _fill_padded(kp, k_ref[:, sl], m)
            _fill_padded(vp, v_ref[:, sl], m)
            dkp[...] = jnp.zeros_like(dkp)
            dvp[...] = jnp.zeros_like(dvp)

            def block(b, carry):
                q0 = pl.multiple_of(b * 128, 128)
                rows = pl.ds(q0, 128)
                win = pl.ds(q0, 256)
                qb, kw, vw = q_ref[rows, sl], kp[win, :], vp[win, :]
                s = _dot_nt(qb, kw) * SCALE
                s = jnp.where(_band_mask(q0, m), s, NEG)
                wp = _wide(w_ref[rows, sl], 2) * jnp.exp(s - _wide(l_ref[rows, sl], 2))
                dob = d_ref[rows, sl].astype(BF16)
                ds = (wp * (_dot_nt(dob, vw) - _wide(c_ref[rows, sl], 2))).astype(BF16)
                dqk_ref[0, rows, sl] = _dot(ds, kw) * SCALE
                dkp[win, :] += _dot_tn(ds, qb) * SCALE
                dvp[win, :] += _dot_tn(wp.astype(BF16), dob)
                return carry

            lax.fori_loop(0, nb, block, 0, unroll=min(nb, 2))
            dqk_ref[1, :, sl] = dkp[64:64 + m, :]
            dv_ref[:, sl] = dvp[64:64 + m, :]

    blk = pl.BlockSpec((m, hp * HD), lambda h, r: (0, r * (4 // hp) + h))
    view = lambda a: a.reshape(m, dil * 512)
    dqk, dv = pl.pallas_call(
        body, name=f"attn_a_bwd_{g}", out_shape=[SDS((2, m, dil * 512), F32), SDS((m, dil * 512), F32)],
        grid=(4 // hp, dil), in_specs=[blk] * 7,
        out_specs=[pl.BlockSpec((2, m, hp * HD), lambda h, r: (0, 0, r * (4 // hp) + h)), blk],
        scratch_shapes=[pltpu.VMEM((m + 128, HD), BF16), pltpu.VMEM((m + 128, HD), BF16),
                        pltpu.VMEM((m + 128, HD), F32), pltpu.VMEM((m + 128, HD), F32)],
        compiler_params=_params(2))(*_group_views(qkn, proj, g), view(doa), view(lse), view(w), view(cc))
    return dqk.reshape(2, T, 512), dv.reshape(T, 512)


def _attn_b_bwd(qkn, proj, dob, ob, lse, bias, deps=()):
    def body(q_ref, k_ref, v_ref, d_ref, o_ref, l_ref, bias_ref, dqk_ref, dv_ref, drpb_ref, vb, dk_acc, dv_acc, a_acc):
        vb[...] = v_ref[...].astype(BF16)
        dk_acc[...] = jnp.zeros_like(dk_acc)
        dv_acc[...] = jnp.zeros_like(dv_acc)
        a_acc[...] = jnp.zeros_like(a_acc)

        def row(r, carry):
            start, off = _nbr_window(r)
            rows = pl.ds(pl.multiple_of(r * GRID_W, GRID_W), GRID_W)
            win = pl.ds(pl.multiple_of(start * GRID_W, GRID_W), 512)
            qr, kw, vw = q_ref[rows, :], k_ref[win, :], vb[win, :]
            s = _dot_nt(qr, kw) * SCALE + bias_ref[off]
            p = jnp.exp(s - _wide(l_ref[rows, :], 4))
            dov = d_ref[rows, :]
            delta = jnp.sum(dov * o_ref[rows, :], axis=-1, keepdims=True)
            do16 = dov.astype(BF16)
            ds = p * (_dot_nt(do16, vw) - delta)
            a_acc[off] += ds
            ds16 = ds.astype(BF16)
            dqk_ref[0, rows, :] = _dot(ds16, kw) * SCALE
            dk_acc[win, :] += _dot_tn(ds16, qr) * SCALE
            dv_acc[win, :] += _dot_tn(p.astype(BF16), do16)
            return carry

        lax.fori_loop(0, T // GRID_W, row, 0, unroll=2)
        dqk_ref[1] = dk_acc[...]
        dv_ref[...] = dv_acc[...]

        lane = lax.broadcasted_iota(jnp.int32, (16, HD), 1)
        rowi = lax.broadcasted_iota(jnp.int32, (16, HD), 0)
        low = (lane >= GRID_W - WIN_C) & (lane < GRID_W + WIN_C - 1)
        high = (lane >= HD - WIN_C) | (lane < WIN_C - 1)
        flip = (lax.broadcasted_iota(jnp.int32, (GRID_W, GRID_W), 0)
                + lax.broadcasted_iota(jnp.int32, (GRID_W, GRID_W), 1) == GRID_W - 1).astype(BF16)
        out = jnp.zeros((16, HD), F32)
        for d in range(14):
            acc = None
            for off in range(8):
                if 0 <= d - off <= 6 and (d - off) % 2 == 0:
                    jj = (d - off) // 2
                    piece = a_acc[off, :, jj * HD:(jj + 1) * HD]
                    acc = piece if acc is None else acc + piece
            hi = acc.astype(BF16)
            lo = (acc - hi.astype(F32)).astype(BF16)
            rev = _dot(flip, hi) + _dot(flip, lo)
            v = jnp.sum(pltpu.roll(rev, 0, 1, stride=1, stride_axis=0), axis=0, keepdims=True)
            v = jnp.broadcast_to(v, (16, HD))
            out = out + jnp.where((rowi == d) & low, v, 0.0)
            out = out + jnp.where(rowi == d + 1, pltpu.roll(jnp.where(high, v, 0.0), GRID_W, 1), 0.0)
        drpb_ref[...] = out

    blk = pl.BlockSpec((T, HD), lambda h: (0, h))
    return pl.pallas_call(
        _after(body, deps), name="attn_b_bwd",
        out_shape=[SDS((2, T, 512), F32), SDS((T, 512), F32), SDS((4, 16, HD), F32)], grid=(4,),
        in_specs=[DEP_SPEC] * len(deps) + [
            pl.BlockSpec((T, HD), lambda h: (0, NHA + h)),
            pl.BlockSpec((T, HD), lambda h: (0, NH + NHA + h)),
            pl.BlockSpec((T, HD), lambda h: (0, 2 * NH + NHA + h)), blk, blk, blk,
            pl.BlockSpec((None, 8, GRID_W, 512), lambda h: (h, 0, 0, 0))],
        out_specs=[pl.BlockSpec((2, T, HD), lambda h: (0, 0, h)), blk,
                   pl.BlockSpec((None, 16, HD), lambda h: (h, 0, 0))],
        scratch_shapes=[pltpu.VMEM((T, HD), BF16), pltpu.VMEM((T, HD), F32), pltpu.VMEM((T, HD), F32),
                        pltpu.VMEM((8, GRID_W, 512), F32)],
        compiler_params=_params(1))(*deps, qkn, qkn, proj, dob, ob, lse, bias)


def _qk_bwd(proj, nw, cos, sin, dqk_groups, dqk_b, dvs, dga, dgb):
    tm = 256

    def body(p_ref, w_ref, cos_ref, sin_ref, d0, d1, d2, d3, v0, v1, v2, v3, ga_ref, gb_ref, o_ref, dn_ref):
        j, i = pl.program_id(0), pl.program_id(1)

        @pl.when((j < 2) & (i == 0))
        def _():
            dn_ref[...] = jnp.zeros_like(dn_ref)

        @pl.when(j < 2)
        def _():
            cv, sv = cos_ref[...], sin_ref[...]
            srcs = (d0, d1, d2, d3)
            dna = jnp.zeros((1, HD), F32)
            dnb = jnp.zeros((1, HD), F32)
            for h in range(NH):
                sl = slice(h * HD, (h + 1) * HD)
                dz = srcs[h // 4][:, (h % 4) * HD:(h % 4 + 1) * HD]
                if h < NHA:
                    dz = dz * cv + pltpu.roll(dz * sv, 64, 1)
                dx, dg = _norm_bwd(p_ref[:, sl], dz, w_ref[:, sl])
                o_ref[:, sl] = dx.astype(BF16)
                if h < NHA:
                    dna += dg
                else:
                    dnb += dg
            dn_ref[0:1, :] += dna
            dn_ref[1:2, :] += dnb

        @pl.when(j == 2)
        def _():
            for s, v_ref in enumerate((v0, v1, v2, v3)):
                o_ref[:, s * 512:(s + 1) * 512] = v_ref[...].astype(BF16)

        @pl.when(j == 3)
        def _():
            o_ref[...] = ga_ref[...]

        @pl.when(j == 4)
        def _():
            o_ref[...] = gb_ref[...]

    def rows(used):
        return lambda j, i: (jnp.where(used(j), i, 0), 0)

    qk = lambda j: j < 2
    dspec = pl.BlockSpec((None, tm, 512), lambda j, i: (jnp.minimum(j, 1), jnp.where(j < 2, i, 0), 0))
    vspec = pl.BlockSpec((tm, 512), rows(lambda j: j == 2))
    return pl.pallas_call(
        body, name="qk_bwd", out_shape=[SDS((T, DIN), BF16), SDS((2, 8, HD), F32)], grid=(5, T // tm),
        in_specs=[pl.BlockSpec((tm, D), lambda j, i: (jnp.where(j < 2, i, 0), jnp.minimum(j, 1))),
                  pl.BlockSpec((None, 1, D), lambda j, i: (jnp.minimum(j, 1), 0, 0)),
                  pl.BlockSpec((tm, HD), rows(qk)), pl.BlockSpec((tm, HD), rows(qk)),
                  dspec, dspec, dspec, dspec, vspec, vspec, vspec, vspec,
                  pl.BlockSpec((tm, D), rows(lambda j: j == 3)), pl.BlockSpec((tm, D), rows(lambda j: j == 4))],
        out_specs=[pl.BlockSpec((tm, D), lambda j, i: (i, j)),
                   pl.BlockSpec((None, 8, HD), lambda j, i: (jnp.minimum(j, 1), 0, 0))],
        compiler_params=_params(2))(proj, nw, cos, sin, *dqk_groups, dqk_b, *dvs, dga, dgb)


def _in_proj_bwd(dproj, w_in, x, dh1, g, deps=()):
    tm, tk = 512, 1280
    per = (DIN // NSH) // tk
    nk = DIN // tk

    def body(dp_ref, w_ref, x_ref, dh_ref, g_ref, dx_ref, dg_ref, acc):
        i, k = pl.program_id(0), pl.program_id(1)

        @pl.when(k == 0)
        def _():
            acc[...] = jnp.zeros_like(acc)

        @pl.when((k == 0) & (i == 0))
        def _():
            dg_ref[...] = jnp.zeros_like(dg_ref)

        acc[...] += _dot_nt(dp_ref[...], w_ref[...])

        @pl.when(k == nk - 1)
        def _():
            dx, dg = _norm_bwd(x_ref[...], acc[...], g_ref[...])
            dx_ref[...] = dh_ref[...] + dx
            dg_ref[...] += dg

    row = pl.BlockSpec((tm, D), lambda i, k: (i, 0))
    vec = pl.BlockSpec((1, D), lambda i, k: (0, 0))
    return pl.pallas_call(
        _after(body, deps), name="in_proj_bwd", out_shape=[SDS((T, D), F32), SDS((1, D), F32)], grid=(T // tm, nk),
        in_specs=[DEP_SPEC] * len(deps) + [
            pl.BlockSpec((tm, tk), lambda i, k: (i, k)),
            pl.BlockSpec((None, D, tk), lambda i, k: (k // per, 0, k % per)), row, row, vec],
        out_specs=[row, vec], scratch_shapes=[pltpu.VMEM((tm, D), F32)],
        compiler_params=_params(2))(*deps, dproj, w_in, x, dh1, g)


def _grad_w(name, a, g, shard_rows, rows, cols, tr, tc):
    ni, nj = rows // tr, cols // tc
    if shard_rows:
        a_map, g_map = (lambda s, i, j: (0, s * ni + i)), (lambda s, i, j: (0, j))
    else:
        a_map, g_map = (lambda s, i, j: (0, i)), (lambda s, i, j: (0, s * nj + j))

    def body(a_ref, g_ref, o_ref):
        o_ref[...] = _dot_tn(a_ref[...], g_ref[...]).astype(BF16)

    return pl.pallas_call(
        body, name=name, out_shape=SDS((NSH, rows, cols), BF16), grid=(NSH, ni, nj),
        in_specs=[pl.BlockSpec((T, tr), a_map), pl.BlockSpec((T, tc), g_map)],
        out_specs=pl.BlockSpec((None, tr, tc), lambda s, i, j: (s, i, j)), compiler_params=_params(3))(a, g)


def _adamw(w, g, m, v):
    m = B1 * m + (1.0 - B1) * g
    v = B2 * v + (1.0 - B2) * (g * g)
    m_hat = m / (1.0 - B1 ** STEP)
    v_hat = v / (1.0 - B2 ** STEP)
    delta = -LR * (m_hat / (jnp.sqrt(v_hat) + AEPS) + WD * w)
    return delta, m, v


def _sum_halves(name, place, grads, theirs):
    _, rows, cols = theirs.shape
    tr = _row_tile(rows, cols, 1 << 20)

    def body(place_ref, a_ref, b_ref, o_ref):
        o_ref[...] = (a_ref[...].astype(F32) + b_ref[...].astype(F32)).astype(BF16)

    spec = pl.BlockSpec((None, tr, cols), lambda s, i, p: (s, i, 0))
    return pl.pallas_call(
        body, name=name, out_shape=SDS(theirs.shape, BF16),
        grid_spec=pltpu.PrefetchScalarGridSpec(
            num_scalar_prefetch=1, grid=(NSH, rows // tr),
            in_specs=[pl.BlockSpec((None, None, tr, cols), lambda s, i, p: (s, p[1], i, 0)), spec], out_specs=spec),
        compiler_params=_params(2))(place, grads, theirs)


def _sum_landed(name, place, part, landed):
    _, rows, cols = part.shape
    tr = _row_tile(rows, cols, 1 << 20)

    def body(place_ref, p_ref, l_ref, o_ref):
        o_ref[...] = ((p_ref[...].astype(F32) + l_ref[0].astype(F32)) + l_ref[1].astype(F32)) + l_ref[2].astype(F32)

    return pl.pallas_call(
        body, name=name, out_shape=SDS((2, rows, cols), F32),
        grid_spec=pltpu.PrefetchScalarGridSpec(
            num_scalar_prefetch=1, grid=(rows // tr,),
            in_specs=[pl.BlockSpec((None, tr, cols), lambda i, p: (p[0], i, 0)),
                      pl.BlockSpec((3, tr, cols), lambda i, p: (0, i, 0))],
            out_specs=pl.BlockSpec((None, tr, cols), lambda i, p: (p[1], i, 0))),
        compiler_params=_params(1))(place, part, landed)


def _adam_shard(name, g, w, m, v):
    rows, cols = w.shape
    tr = _row_tile(rows, cols, 1 << 19)

    def body(g_ref, w_ref, m_ref, v_ref, go_ref, d_ref, nm_ref, nv_ref):
        g = g_ref[...]
        go_ref[...] = g
        d_ref[...], nm_ref[...], nv_ref[...] = _adamw(w_ref[...], g, m_ref[...], v_ref[...])

    spec = pl.BlockSpec((tr, cols), lambda i: (i, 0))
    return pl.pallas_call(
        body, name=name, out_shape=[SDS((rows, cols), F32)] * 4, grid=(rows // tr,),
        in_specs=[spec] * 4, out_specs=[spec] * 4, compiler_params=_params(1))(g, w, m, v)


def _adam_small(gathered, w, m, v):
    def body(g_ref, w_ref, m_ref, v_ref, go_ref, d_ref, nm_ref, nv_ref):
        g = g_ref[0:SMALL_ROWS, :]
        for dev in range(1, 8):
            g = g + g_ref[dev * SMALL_ROWS:(dev + 1) * SMALL_ROWS, :]
        go_ref[...] = g
        d_ref[...], nm_ref[...], nv_ref[...] = _adamw(w_ref[...], g, m_ref[...], v_ref[...])

    return pl.pallas_call(body, name="adam_small", out_shape=[SDS((SMALL_ROWS, HD), F32)] * 4)(gathered, w, m, v)


SMALL = (("norm_mix", (1, D)), ("b_gate", (1, 2 * D)), ("q_norm_a", (1, HD)), ("k_norm_a", (1, HD)),
         ("q_norm_b", (1, HD)), ("k_norm_b", (1, HD)), ("rpb_b", (1, 4, 15, 31)), ("norm_ffn", (1, D)))


def _pack_small(vals):
    pieces = []
    for (name, shape), val in zip(SMALL, vals):
        flat = val.reshape(-1)
        pad = (-flat.shape[0]) % HD
        pieces.append(jnp.pad(flat, (0, pad)).reshape(-1, HD))
    packed = jnp.concatenate(pieces, axis=0)
    return jnp.pad(packed, ((0, SMALL_ROWS - packed.shape[0]), (0, 0)))


def _unpack_small(packed):
    out, row = [], 0
    for name, shape in SMALL:
        size = int(np.prod(shape))
        nrows = -(-size // HD)
        out.append(packed[row:row + nrows].reshape(-1)[:size].reshape(shape))
        row += nrows
    return out


def kernel(x, norm_mix, w_in, b_gate, q_norm_a, k_norm_a, q_norm_b, k_norm_b, rpb_b, w_proj_a, w_proj_b, w_out, norm_ffn, w_up, w_down, loss_target, m_norm_mix, m_w_in, m_b_gate, m_q_norm_a, m_k_norm_a, m_q_norm_b, m_k_norm_b, m_rpb_b, m_w_proj_a, m_w_proj_b, m_w_out, m_norm_ffn, m_w_up, m_w_down, v_norm_mix, v_w_in, v_b_gate, v_q_norm_a, v_k_norm_a, v_q_norm_b, v_k_norm_b, v_rpb_b, v_w_proj_a, v_w_proj_b, v_w_out, v_norm_ffn, v_w_up, v_w_down):
    big_names = ("w_in", "w_proj_a", "w_proj_b", "w_out", "w_up", "w_down")
    big_w = [a[0] for a in (w_in, w_proj_a, w_proj_b, w_out, w_up, w_down)]
    big_m = [a[0] for a in (m_w_in, m_w_proj_a, m_w_proj_b, m_w_out, m_w_up, m_w_down)]
    big_v = [a[0] for a in (v_w_in, v_w_proj_a, v_w_proj_b, v_w_out, v_w_up, v_w_down)]
    x2, target = x[0], loss_target[0]

    place = jnp.stack([2 * lax.axis_index("x") + lax.axis_index("y"), lax.axis_index("c")]).astype(jnp.int32)
    groups = ((0,), (1, 2, 3), (4,), (5,))
    started = []
    for j, grp in enumerate(groups):
        deps = (started[0][4],) if j else ()
        placed = [_cast_into_place(big_w[i], "cast_" + big_names[i], place, deps) for i in grp]
        started.append(_gather_start(f"gather_start_{j}", placed))

    def whole(fulls):
        return [f.reshape(NSH, 2 * f.shape[2], f.shape[3]) for f in fulls]

    def forward_begin(j, after):
        send, recv, _, fulls, _ = started[j]
        fulls = _gather_wait(f"gather_wait_{j}", send, recv, fulls, after)
        send, recv, _, fulls, token = _forward_start(f"forward_start_{j}", fulls)
        return (send, recv, fulls), token

    def forward_end(j, state, after):
        return whole(_forward_wait(f"forward_wait_{j}", *state, after))

    def as_halves(grads):
        return [g.reshape(NSH, 2, g.shape[1] // 2, g.shape[2]) for g in grads]

    def reduce_start(j, grads, theirs):
        parts = [_sum_halves(f"sum_halves_{j}_{i}", place, a, b) for i, (a, b) in enumerate(zip(grads, theirs))]
        send, recv, parts, lands, token = _reduce_start(f"reduce_start_{j}", parts)
        return (send, recv, parts, lands), token

    def reduce_begin(j, grads):
        grads = as_halves(grads)
        return reduce_start(j, grads, _reduce_exchange(f"reduce_exchange_{j}", grads))

    def exchange_begin(j, grads):
        send, recv, grads, lands, token = _exchange_start(f"exchange_start_{j}", as_halves(grads))
        return (send, recv, grads, lands), token

    def exchange_end(j, state, after):
        return reduce_start(j, *_exchange_wait(f"exchange_wait_{j}", *state, after))

    big_out = {}

    def share_begin(j, state, after):
        send, recv, parts, lands = state
        parts, lands = _reduce_wait(f"reduce_wait_{j}", send, recv, parts, lands, after)
        sums = [_sum_landed(f"sum_landed_{j}_{i}", place, p, l) for i, (p, l) in enumerate(zip(parts, lands))]
        send, recv, _, sums, token = _share_start(f"share_start_{j}", sums)
        return (send, recv, sums), token

    def share_end(j, state, after):
        for idx, g in zip(groups[j], _share_wait(f"share_wait_{j}", *state, after)):
            g = g.reshape(big_w[idx].shape)
            big_out[idx] = _adam_shard("adam_" + big_names[idx], g, big_w[idx], big_m[idx], big_v[idx])
        return big_out[groups[j][-1]][1]

    proj, xn = _norm_in_proj_own(x2, norm_mix, whole(started[0][3])[0], place)
    send, recv, _, win, _ = started[0]
    win = _gather_wait("gather_wait_0", send, recv, win, (proj, *[s[4] for s in started[1:]]), ks=(0, 1))
    win = _gather_finish("gather_finish_0", win, ks=(0, 1))
    proj = _in_proj_rest("in_proj_near", xn, whole(win)[0], proj, place, (2, 1))
    win = _gather_wait("gather_wait_0_far", send, recv, win, (proj,), ks=(2,))
    win = _gather_finish("gather_finish_0_far", win, ks=(2,))
    (win_f,) = whole(win)
    proj = _in_proj_rest("in_proj_far", xn, win_f, proj, place, (3,))
    cos, sin = _rope_tables()
    nw = jnp.stack([jnp.concatenate([jnp.tile(q_norm_a, (1, NHA)), jnp.tile(q_norm_b, (1, NH - NHA))], axis=1),
                    jnp.concatenate([jnp.tile(k_norm_a, (1, NHA)), jnp.tile(k_norm_b, (1, NH - NHA))], axis=1)])
    qkn = _qk_prep(proj, nw, cos, sin)
    fw1, token = forward_begin(1, (qkn,))
    fwd_a = [_attn_a_fwd(qkn, proj, g) for g in range(3)]
    os, ls = [f[0] for f in fwd_a], [f[1] for f in fwd_a]
    fw2, token = forward_begin(2, (os[2], token))
    ob, lse_b, bias = _attn_b_fwd(qkn, proj, _rpb_rows(rpb_b[0]))
    oa, w0, w1, w2 = _comb_fwd(os, ls)
    ws = [w0, w1, w2]
    wpa_f, wpb_f, wout_f = forward_end(1, fw1, (oa, token))
    wout_f = wout_f.reshape(D, D)
    mixed, ob16 = _mix_fwd(oa, ob, proj, b_gate, wpa_f, wpb_f)
    h1, hn = _out_proj_fwd(mixed, wout_f, x2, norm_ffn)
    fw3, token = forward_begin(3, (h1,))
    (wup_f,) = forward_end(2, fw2, (hn, token))
    usq, u = _ffn_up(hn, wup_f)
    (wdown_f,) = forward_end(3, fw3, (u,))
    wdown_f = wdown_f.reshape(DFF, D)
    dy, dy16, loss_parts = _ffn_down_loss(usq, wdown_f, h1, target)
    loss = lax.psum(jnp.sum(loss_parts[:, 0, 0]), ("x", "y", "c"))

    g_down = _grad_w("grad_w_down", usq, dy16, True, DFF // NSH, D, 1024, 1024)
    ex_down, token = exchange_begin(3, [g_down])
    du = _ffn_down_bwd(dy16, wdown_f, u, deps=(token,))
    g_up = _grad_w("grad_w_up", hn, du, False, D, DFF // NSH, 1024, 1024)
    red_down, token = exchange_end(3, ex_down, (g_up,))
    ex_up, token_up = exchange_begin(2, [g_up])
    dh1, dh16, d_norm_ffn = _ffn_up_bwd(du, wup_f, h1, dy, norm_ffn, deps=(token, token_up))
    dya, dyb, dga, dgb, doa, dob, dba, dbb = _mix_bwd(dh16, wout_f, oa, ob16, proj, b_gate, wpa_f, wpb_f)
    g_out = _grad_w("grad_w_out", mixed, dh16, True, D // NSH, D, 512, 1024)
    g_pa = _grad_w("grad_w_proj_a", oa, dya, False, 512, 512, 512, 512)
    g_pb = _grad_w("grad_w_proj_b", ob16, dyb, False, 512, 512, 512, 512)
    red_up, token = exchange_end(2, ex_up, (g_out,))
    ex_mid, token_mid = exchange_begin(1, [g_pa, g_pb, g_out])
    cc = _comb_bwd(doa, os, ws, deps=(token, token_mid))
    bwd_a = [_attn_a_bwd(qkn, proj, doa, ls[g], ws[g], cc, g) for g in range(3)]
    red_mid, token = exchange_end(1, ex_mid, (bwd_a[2][1],))
    dqk_b, dv_b, drpb_t = _attn_b_bwd(qkn, proj, dob, ob, lse_b, bias, deps=(token,))
    dproj, dn = _qk_bwd(proj, nw, cos, sin, [b[0] for b in bwd_a], dqk_b, [b[1] for b in bwd_a] + [dv_b], dga, dgb)
    g_in = _grad_w("grad_w_in", xn, dproj, False, D, DIN // NSH, 1024, 1280)
    red_in, token = reduce_begin(0, [g_in])
    grad_x, d_norm_mix = _in_proj_bwd(dproj, win_f, x2, dh1, norm_mix, deps=(token,))

    sh_down, token = share_begin(3, red_down, (grad_x,))
    sh_up, token = share_begin(2, red_up, (token,))
    done = share_end(3, sh_down, (token,))
    sh_mid, token = share_begin(1, red_mid, (done,))
    done = share_end(2, sh_up, (token,))
    sh_in, token = share_begin(0, red_in, (done,))
    done = share_end(1, sh_mid, (token,))
    done = share_end(0, sh_in, (done,))

    d_rpb = drpb_t[:, :15, GRID_W - WIN_C:GRID_W + WIN_C - 1]
    small_g = [d_norm_mix, jnp.concatenate([dba, dbb], axis=1), dn[0, 0], dn[1, 0], dn[0, 1], dn[1, 1], d_rpb, d_norm_ffn]
    gathered_small = _allgather_small(_pack_small(small_g), done)
    small_w = (norm_mix, b_gate, q_norm_a, k_norm_a, q_norm_b, k_norm_b, rpb_b, norm_ffn)
    small_m = (m_norm_mix, m_b_gate, m_q_norm_a, m_k_norm_a, m_q_norm_b, m_k_norm_b, m_rpb_b, m_norm_ffn)
    small_v = (v_norm_mix, v_b_gate, v_q_norm_a, v_k_norm_a, v_q_norm_b, v_k_norm_b, v_rpb_b, v_norm_ffn)
    small_out = [_unpack_small(p) for p in
                 _adam_small(gathered_small, _pack_small(small_w), _pack_small(small_m), _pack_small(small_v))]

    order = ("norm_mix", "w_in", "b_gate", "q_norm_a", "k_norm_a", "q_norm_b", "k_norm_b", "rpb_b",
             "w_proj_a", "w_proj_b", "w_out", "norm_ffn", "w_up", "w_down")
    small_idx = {name: i for i, (name, _) in enumerate(SMALL)}
    outs = []
    for kind in range(4):
        for name in order:
            if name in small_idx:
                outs.append(small_out[kind][small_idx[name]])
            else:
                outs.append(big_out[big_names.index(name)][kind][None])
    return (loss, grad_x[None], *outs)
```

```python
import functools

import numpy as np
import jax
import jax.numpy as jnp
from jax import lax
from jax.experimental import pallas as pl
from jax.experimental.pallas import tpu as pltpu

F32, BF16 = jnp.float32, jnp.bfloat16
SDS = jax.ShapeDtypeStruct
MESH = pl.DeviceIdType.MESH

T = 2048
D = 2048
HD = 128
NH, NHA = 16, 12
DIN = 10240
DFF = 8192
NSH = 4
DILS = (1, 4, 16)
EPS = 1e-6
NEG = -1e30
SCALE = HD ** -0.5
GRID_W, WIN_R, WIN_C = 64, 8, 16
VMEM_LIMIT = 56 * 1024 * 1024
B1, B2, LR, AEPS, WD, STEP = 0.9, 0.999, 0.001, 1e-08, 0.01, 10
SMALL_ROWS = 88


def _dot(a, b):
    return jnp.dot(a, b, preferred_element_type=F32)


def _dot_nt(a, b):
    return lax.dot_general(a, b, (((1,), (1,)), ((), ())), preferred_element_type=F32)


def _dot_tn(a, b):
    return lax.dot_general(a, b, (((0,), (0,)), ((), ())), preferred_element_type=F32)


def _params(n):
    return pltpu.CompilerParams(dimension_semantics=("arbitrary",) * n, vmem_limit_bytes=VMEM_LIMIT)


def _resident(shape, index_map):
    return pl.BlockSpec(shape, index_map, pipeline_mode=pl.Buffered(1))


def _sigmoid(z):
    return 1.0 / (1.0 + jnp.exp(-z))


def _wide(v, n):
    return jnp.concatenate([v] * n, axis=1)


def _row_tile(rows, cols, elems):
    tr = 16
    while tr * 2 <= rows and tr * 2 * cols <= elems:
        tr *= 2
    return tr


def _place():
    x, y, c = lax.axis_index("x"), lax.axis_index("y"), lax.axis_index("c")
    peers = [(1 - x, y), (x, 1 - y), (1 - x, 1 - y)]
    return x, y, c, peers


def _cast_into_place(w, name, place, deps=()):
    rows, cols = w.shape
    hr = rows // 2
    tr = min(hr, 256)
    per = hr // tr

    def body(*refs):
        w_ref, o_ref = refs[-2:]
        o_ref[...] = w_ref[...].astype(BF16)

    return pl.pallas_call(
        body, name=name, out_shape=SDS((NSH, 2, hr, cols), BF16),
        grid_spec=pltpu.PrefetchScalarGridSpec(
            num_scalar_prefetch=1, grid=(2, per),
            in_specs=[DEP_SPEC] * len(deps) + [pl.BlockSpec((tr, cols), lambda h, i, p: (h * per + i, 0))],
            out_specs=pl.BlockSpec((None, None, tr, cols), lambda h, i, p: (p[0], h, i, 0))),
        compiler_params=_params(2))(place, *deps, w)


ANY_SPEC = pl.BlockSpec(memory_space=pl.ANY)
HBM_SPEC = pl.BlockSpec(memory_space=pltpu.HBM)
SEM_SPEC = pl.BlockSpec(memory_space=pltpu.SEMAPHORE)
DEP_SPEC = pl.BlockSpec((8, 128), lambda *_: (0, 0))
EFFECT = pltpu.SideEffectType.DATAFLOW_SIDE_EFFECTING


def _after(body, deps):
    n = len(deps)
    return (lambda *refs: body(*refs[n:])) if n else body


SIBLING_BARRIER = 1


def _split_start(name, srcs, lands, n_copies, issue, sibling_only=False, after=()):
    n, m, d = len(srcs), len(lands), len(after)

    def body(*refs):
        if sibling_only:
            x, y, c, _ = _place()
            barrier = pltpu.get_barrier_semaphore()
            pl.semaphore_signal(barrier, inc=1, device_id=(x, y, 1 - c), device_id_type=MESH)
            pl.semaphore_wait(barrier, 1)
        issue(refs[:n], refs[n:n + m], refs[n + m + d], refs[n + m + d + 1])
        refs[-1][...] = jnp.zeros((8, 128), F32)

    arrays = list(srcs) + list(lands)
    outs = pl.pallas_call(
        body, name=name,
        out_shape=(pltpu.SemaphoreType.DMA((n_copies,)), pltpu.SemaphoreType.DMA((n_copies,)),
                   *[pltpu.HBM(a.shape, a.dtype) for a in arrays], SDS((8, 128), F32)),
        in_specs=[HBM_SPEC] * (n + m) + [ANY_SPEC] * d,
        out_specs=(SEM_SPEC, SEM_SPEC, *[HBM_SPEC] * (n + m), pl.BlockSpec(memory_space=pltpu.VMEM)),
        input_output_aliases={i: 2 + i for i in range(n + m)},
        compiler_params=pltpu.CompilerParams(has_side_effects=EFFECT,
                                             collective_id=SIBLING_BARRIER if sibling_only else None),
    )(*[pltpu.with_memory_space_constraint(a, pltpu.HBM) for a in arrays], *after)
    return outs[0], outs[1], list(outs[2:2 + n]), list(outs[2 + n:2 + n + m]), outs[-1]


def _split_wait(name, send_sems, recv_sems, srcs, lands, after, wait):
    n, m = len(srcs), len(lands)

    def body(*refs):
        wait(refs[:n], refs[n:n + m], refs[n + m], refs[n + m + 1])

    arrays = list(srcs) + list(lands)
    outs = pl.pallas_call(
        body, name=name, out_shape=[pltpu.HBM(a.shape, a.dtype) for a in arrays],
        in_specs=[HBM_SPEC] * (n + m) + [SEM_SPEC, SEM_SPEC] + [ANY_SPEC] * len(after),
        out_specs=[HBM_SPEC] * (n + m), input_output_aliases={i: i for i in range(n + m)},
        compiler_params=pltpu.CompilerParams(has_side_effects=EFFECT),
    )(*arrays, send_sems, recv_sems, *after)
    return list(outs[:n]), list(outs[n:])


def _gather_start(name, fulls, ks=(0, 1, 2), after=()):
    def issue(srcs, dsts, send_sems, recv_sems):
        x, y, c, peers = _place()
        for i in range(len(fulls)):
            mine = dsts[i].at[2 * x + y, c]
            for k in ks:
                px, py = peers[k]
                pltpu.make_async_remote_copy(
                    src_ref=mine, dst_ref=mine, send_sem=send_sems.at[3 * i + k],
                    recv_sem=recv_sems.at[3 * i + k], device_id=(px, py, c), device_id_type=MESH).start()

    return _split_start(name, [], fulls, 3 * len(fulls), issue, after=after)


def _gather_wait(name, send_sems, recv_sems, fulls, after, ks=(0, 1, 2)):
    def wait(srcs, dsts, send_sems, recv_sems):
        x, y, c, peers = _place()
        for i in range(len(fulls)):
            for k in ks:
                px, py = peers[k]
                cp = pltpu.make_async_remote_copy(
                    src_ref=dsts[i].at[2 * x + y, c], dst_ref=dsts[i].at[2 * px + py, c],
                    send_sem=send_sems.at[3 * i + k], recv_sem=recv_sems.at[3 * i + k],
                    device_id=(px, py, c), device_id_type=MESH)
                cp.wait_send()
                cp.wait_recv()

    return _split_wait(name, send_sems, recv_sems, [], fulls, after, wait)[1]


def _gather_finish(name, fulls, ks=(0, 1, 2)):
    n = len(fulls)

    def body(*refs):
        fin, fout = refs[:n], refs[n:2 * n]
        send_sems, recv_sems = refs[2 * n:]
        x, y, c, peers = _place()

        def copy(i, k, half):
            px, py = peers[k]
            return pltpu.make_async_remote_copy(
                src_ref=fin[i].at[2 * px + py, half], dst_ref=fout[i].at[2 * px + py, half],
                send_sem=send_sems.at[3 * i + k], recv_sem=recv_sems.at[3 * i + k],
                device_id=(x, y, 1 - c), device_id_type=MESH)

        sends = [copy(i, k, c) for i in range(n) for k in ks]
        for cp in sends:
            cp.start()
        for i in range(n):
            for k in ks:
                copy(i, k, 1 - c).wait_recv()
        for cp in sends:
            cp.wait_send()

    return pl.pallas_call(
        body, name=name, out_shape=[SDS(f.shape, f.dtype) for f in fulls],
        in_specs=[ANY_SPEC] * n, out_specs=[ANY_SPEC] * n, input_output_aliases={i: i for i in range(n)},
        scratch_shapes=[pltpu.SemaphoreType.DMA((3 * n,)), pltpu.SemaphoreType.DMA((3 * n,))])(*fulls)


def _reduce_exchange(name, grads):
    n = len(grads)

    def body(*refs):
        ins, theirs = refs[:n], refs[n:2 * n]
        send_sems, recv_sems = refs[2 * n:]
        x, y, c, _ = _place()
        copies = []
        for i in range(n):
            cp = pltpu.make_async_remote_copy(
                src_ref=ins[i].at[:, 1 - c], dst_ref=theirs[i], send_sem=send_sems.at[i],
                recv_sem=recv_sems.at[i], device_id=(x, y, 1 - c), device_id_type=MESH)
            cp.start()
            copies.append(cp)
        for cp in copies:
            cp.wait_recv()
            cp.wait_send()

    return pl.pallas_call(
        body, name=name, out_shape=[SDS((NSH,) + g.shape[2:], g.dtype) for g in grads],
        in_specs=[ANY_SPEC] * n, out_specs=[ANY_SPEC] * n,
        scratch_shapes=[pltpu.SemaphoreType.DMA((n,)), pltpu.SemaphoreType.DMA((n,))])(*grads)


def _reduce_start(name, parts):
    lands = [lax.empty((3,) + p.shape[1:], p.dtype) for p in parts]

    def issue(srcs, dsts, send_sems, recv_sems):
        x, y, c, peers = _place()
        for i in range(len(parts)):
            for k, (px, py) in enumerate(peers):
                pltpu.make_async_remote_copy(
                    src_ref=srcs[i].at[2 * px + py], dst_ref=dsts[i].at[k], send_sem=send_sems.at[3 * i + k],
                    recv_sem=recv_sems.at[3 * i + k], device_id=(px, py, c), device_id_type=MESH).start()

    return _split_start(name, parts, lands, 3 * len(parts), issue)


def _reduce_wait(name, send_sems, recv_sems, parts, lands, after):
    def wait(srcs, dsts, send_sems, recv_sems):
        x, y, c, peers = _place()
        for i in range(len(parts)):
            for k, (px, py) in enumerate(peers):
                cp = pltpu.make_async_remote_copy(
                    src_ref=srcs[i].at[2 * px + py], dst_ref=dsts[i].at[k], send_sem=send_sems.at[3 * i + k],
                    recv_sem=recv_sems.at[3 * i + k], device_id=(px, py, c), device_id_type=MESH)
                cp.wait_send()
                cp.wait_recv()

    return _split_wait(name, send_sems, recv_sems, parts, lands, after, wait)


def _sibling_copy(src, dst, send_sems, recv_sems, k):
    x, y, c, _ = _place()
    return pltpu.make_async_remote_copy(src_ref=src, dst_ref=dst, send_sem=send_sems.at[k], recv_sem=recv_sems.at[k],
                                        device_id=(x, y, 1 - c), device_id_type=MESH)


def _forward_start(name, fulls):
    def issue(srcs, dsts, send_sems, recv_sems):
        x, y, c, peers = _place()
        for i in range(len(fulls)):
            for k, (px, py) in enumerate(peers):
                part = dsts[i].at[2 * px + py, c]
                _sibling_copy(part, part, send_sems, recv_sems, 3 * i + k).start()

    return _split_start(name, [], fulls, 3 * len(fulls), issue, sibling_only=True)


def _forward_wait(name, send_sems, recv_sems, fulls, after):
    def wait(srcs, dsts, send_sems, recv_sems):
        x, y, c, peers = _place()
        for i in range(len(fulls)):
            for k, (px, py) in enumerate(peers):
                cp = _sibling_copy(dsts[i].at[2 * px + py, c], dsts[i].at[2 * px + py, 1 - c], send_sems, recv_sems, 3 * i + k)
                cp.wait_send()
                cp.wait_recv()

    return _split_wait(name, send_sems, recv_sems, [], fulls, after, wait)[1]


def _exchange_start(name, grads):
    lands = [lax.empty((NSH,) + g.shape[2:], g.dtype) for g in grads]

    def issue(srcs, dsts, send_sems, recv_sems):
        c = lax.axis_index("c")
        for i in range(len(grads)):
            _sibling_copy(srcs[i].at[:, 1 - c], dsts[i], send_sems, recv_sems, i).start()

    return _split_start(name, grads, lands, len(grads), issue, sibling_only=True)


def _exchange_wait(name, send_sems, recv_sems, grads, lands, after):
    def wait(srcs, dsts, send_sems, recv_sems):
        c = lax.axis_index("c")
        for i in range(len(grads)):
            cp = _sibling_copy(srcs[i].at[:, 1 - c], dsts[i], send_sems, recv_sems, i)
            cp.wait_send()
            cp.wait_recv()

    return _split_wait(name, send_sems, recv_sems, grads, lands, after, wait)


def _share_start(name, sums):
    def issue(srcs, dsts, send_sems, recv_sems):
        c = lax.axis_index("c")
        for i in range(len(sums)):
            _sibling_copy(dsts[i].at[c], dsts[i].at[c], send_sems, recv_sems, i).start()

    return _split_start(name, [], sums, len(sums), issue, sibling_only=True)


def _share_wait(name, send_sems, recv_sems, sums, after):
    def wait(srcs, dsts, send_sems, recv_sems):
        c = lax.axis_index("c")
        for i in range(len(sums)):
            cp = _sibling_copy(dsts[i].at[c], dsts[i].at[1 - c], send_sems, recv_sems, i)
            cp.wait_send()
            cp.wait_recv()

    return _split_wait(name, send_sems, recv_sems, [], sums, after, wait)[1]


def _allgather_small(v, after):
    m_per, n = v.shape

    def body(x_ref, after_ref, out_ref, send_sems, recv_sems, local_sem):
        x, y, c = lax.axis_index("x"), lax.axis_index("y"), lax.axis_index("c")
        me, sibling = (x, y, c), (x, y, 1 - c)
        chips = [(1 - x, y), (x, 1 - y), (1 - x, 1 - y)]

        def rows(px, py, pc):
            return out_ref.at[pl.ds((4 * px + 2 * py + pc) * m_per, m_per), :]

        def copy(k, block, to, src=None):
            return pltpu.make_async_remote_copy(
                src_ref=rows(*block) if src is None else src, dst_ref=rows(*block),
                send_sem=send_sems.at[k], recv_sem=recv_sems.at[k], device_id=to, device_id_type=MESH)

        mine = pltpu.make_async_copy(x_ref, rows(*me), local_sem)
        mine.start()
        first = [copy(0, me, sibling, src=x_ref)]
        first += [copy(1 + j, me, (*chip, c), src=x_ref) for j, chip in enumerate(chips)]
        for cp in first:
            cp.start()
        passed = [copy(4 + j, (*chip, c), sibling) for j, chip in enumerate(chips)]
        for j, chip in enumerate(chips):
            copy(1 + j, (*chip, c), me).wait_recv()
            passed[j].start()
        copy(0, sibling, me).wait_recv()
        for j, chip in enumerate(chips):
            copy(4 + j, (*chip, 1 - c), me).wait_recv()
        for cp in first + passed:
            cp.wait_send()
        mine.wait()

    return pl.pallas_call(
        body, name="allgather_small", out_shape=SDS((8 * m_per, n), v.dtype),
        in_specs=[pl.BlockSpec(memory_space=pltpu.VMEM), ANY_SPEC], out_specs=pl.BlockSpec(memory_space=pltpu.VMEM),
        scratch_shapes=[pltpu.SemaphoreType.DMA((7,)), pltpu.SemaphoreType.DMA((7,)), pltpu.SemaphoreType.DMA])(v, after)


def _norm_in_proj_own(x, g, w_full, place):
    tn, chunk = 512, 256
    per = (DIN // NSH) // tn

    def body(place_ref, x_ref, g_ref, w_ref, proj_ref, xn_ref):
        @pl.when(pl.program_id(0) == 0)
        def _():
            def norm(r, carry):
                rows = pl.ds(pl.multiple_of(r * chunk, chunk), chunk)
                xv = x_ref[rows, :]
                rs = lax.rsqrt(jnp.mean(xv * xv, axis=-1, keepdims=True) + EPS)
                xn_ref[rows, :] = (xv * rs * g_ref[...]).astype(BF16)
                return carry

            lax.fori_loop(0, T // chunk, norm, 0)

        proj_ref[...] = _dot(xn_ref[...], w_ref[...])

    return pl.pallas_call(
        body, name="norm_in_proj_own", out_shape=[SDS((T, DIN), F32), SDS((T, D), BF16)],
        grid_spec=pltpu.PrefetchScalarGridSpec(
            num_scalar_prefetch=1, grid=(per,),
            in_specs=[_resident((T, D), lambda j, p: (0, 0)),
                      pl.BlockSpec((1, D), lambda j, p: (0, 0)),
                      pl.BlockSpec((None, D, tn), lambda j, p: (p[0], 0, j))],
            out_specs=[pl.BlockSpec((T, tn), lambda j, p: (0, p[0] * per + j)),
                       pl.BlockSpec((T, D), lambda j, p: (0, 0))]),
        compiler_params=_params(1))(place, x, g, w_full)


def _in_proj_rest(name, xn, w_full, proj, place, flips):
    tn = 512
    per = (DIN // NSH) // tn

    def body(place_ref, xn_ref, w_ref, proj_in, proj_ref):
        proj_ref[...] = _dot(xn_ref[...], w_ref[...])

    def shard(j, p):
        flip = flips[0]
        for n, f in enumerate(flips[1:]):
            flip = jnp.where(j // per == n + 1, f, flip)
        return p[0] ^ flip

    return pl.pallas_call(
        body, name=name, out_shape=SDS((T, DIN), F32),
        grid_spec=pltpu.PrefetchScalarGridSpec(
            num_scalar_prefetch=1, grid=(len(flips) * per,),
            in_specs=[_resident((T, D), lambda j, p: (0, 0)),
                      pl.BlockSpec((None, D, tn), lambda j, p: (shard(j, p), 0, j % per)), ANY_SPEC],
            out_specs=pl.BlockSpec((T, tn), lambda j, p: (0, shard(j, p) * per + j % per))),
        input_output_aliases={3: 0}, compiler_params=_params(1))(place, xn, w_full, proj)


def _rope_tables():
    pos = np.arange(T, dtype=np.float32)
    inv = (10000.0 ** (-np.arange(0, HD, 2, dtype=np.float32) / HD)).astype(np.float32)
    ang = (pos[:, None] * inv[None, :]).astype(np.float32)
    cos, sin = np.cos(ang).astype(np.float32), np.sin(ang).astype(np.float32)
    return (jnp.asarray(np.concatenate([cos, cos], axis=1)), jnp.asarray(np.concatenate([-sin, sin], axis=1)))


def _qk_prep(proj, nw, cos, sin):
    tm = 256

    def body(p_ref, w_ref, cos_ref, sin_ref, o_ref):
        cv, sv = cos_ref[...], sin_ref[...]
        for h in range(NH):
            sl = slice(h * HD, (h + 1) * HD)
            xv = p_ref[:, sl]
            r = lax.rsqrt(jnp.mean(xv * xv, axis=-1, keepdims=True) + EPS)
            z = xv * r * w_ref[:, sl]
            if h < NHA:
                z = z * cv + pltpu.roll(z, 64, 1) * sv
            o_ref[:, sl] = z.astype(BF16)

    return pl.pallas_call(
        body, name="qk_prep", out_shape=SDS((T, 2 * D), BF16), grid=(T // tm, 2),
        in_specs=[pl.BlockSpec((tm, D), lambda i, j: (i, j)),
                  pl.BlockSpec((None, 1, D), lambda i, j: (j, 0, 0)),
                  pl.BlockSpec((tm, HD), lambda i, j: (i, 0)),
                  pl.BlockSpec((tm, HD), lambda i, j: (i, 0))],
        out_specs=pl.BlockSpec((tm, D), lambda i, j: (i, j)),
        compiler_params=_params(2))(proj, nw, cos, sin)


def _band_mask(q0, m):
    ii = lax.broadcasted_iota(jnp.int32, (128, 256), 0)
    jj = lax.broadcasted_iota(jnp.int32, (128, 256), 1)
    rel = jj - ii
    kpos = jj + (q0 - 64)
    return (rel >= 0) & (rel <= 128) & (kpos >= 0) & (kpos < m)


def _fill_padded(dst, src, m):
    zeros = jnp.zeros((64, HD), dst.dtype)
    dst[0:64, :] = zeros
    dst[64 + m:128 + m, :] = zeros
    dst[64:64 + m, :] = src.astype(dst.dtype)


def _group_views(qkn, proj, g):
    m = T // DILS[g]
    cols = (qkn[:, g * 512:(g + 1) * 512], qkn[:, D + g * 512:D + (g + 1) * 512],
            proj[:, 2 * D + g * 512:2 * D + (g + 1) * 512])
    return [a.reshape(m, DILS[g] * 512) for a in cols]


def _heads_per_step(m):
    return 4 if m <= 512 else 1


def _attn_a_fwd(qkn, proj, g):
    dil = DILS[g]
    m = T // dil
    nb = m // 128
    hp = _heads_per_step(m)

    def body(q_ref, k_ref, v_ref, o_ref, l_ref, kp, vp):
        for hh in range(hp):
            sl = slice(hh * HD, (hh + 1) * HD)
            _fill_padded(kp, k_ref[:, sl], m)
            _fill_padded(vp, v_ref[:, sl], m)

            def block(b, carry):
                q0 = pl.multiple_of(b * 128, 128)
                kw, vw = kp[pl.ds(q0, 256), :], vp[pl.ds(q0, 256), :]
                s = _dot_nt(q_ref[pl.ds(q0, 128), sl], kw) * SCALE
                s = jnp.where(_band_mask(q0, m), s, NEG)
                mx = jnp.max(s, axis=-1, keepdims=True)
                e = jnp.exp(s - mx)
                den = jnp.sum(e, axis=-1, keepdims=True)
                o_ref[pl.ds(q0, 128), sl] = _dot((e / den).astype(BF16), vw)
                l_ref[pl.ds(q0, 128), sl] = jnp.broadcast_to(mx + jnp.log(den), (128, HD))
                return carry

            lax.fori_loop(0, nb, block, 0, unroll=min(nb, 2))

    blk = pl.BlockSpec((m, hp * HD), lambda h, r: (0, r * (4 // hp) + h))
    o, lse = pl.pallas_call(
        body, name=f"attn_a_fwd_{g}", out_shape=[SDS((m, dil * 512), F32)] * 2, grid=(4 // hp, dil),
        in_specs=[blk] * 3, out_specs=[blk] * 2,
        scratch_shapes=[pltpu.VMEM((m + 128, HD), BF16), pltpu.VMEM((m + 128, HD), BF16)],
        compiler_params=_params(2))(*_group_views(qkn, proj, g))
    return o.reshape(T, 512), lse.reshape(T, 512)


def _nbr_window(r):
    start = jnp.clip(r - WIN_R // 2, 0, T // GRID_W - WIN_R)
    return start, start - r + (WIN_R - 1)


def _rpb_rows(rpb):
    zeros = jnp.zeros((4, 14, 33), F32)
    a, b = rpb[:, :14], rpb[:, 1:15]
    rows = jnp.concatenate([a[:, :, 15:31], zeros, b, zeros, a[:, :, 0:15]], axis=2)
    return jnp.pad(rows, ((0, 0), (0, 2), (0, 0)))


def _attn_b_fwd(qkn, proj, rpb_rows):
    def body(r_ref, q_ref, k_ref, v_ref, o_ref, l_ref, bias_ref, vb, pair):
        qc = lax.broadcasted_iota(jnp.int32, (GRID_W, 512), 0)
        kc = lax.broadcasted_iota(jnp.int32, (GRID_W, 512), 1) & (GRID_W - 1)
        cs = jnp.clip(qc - WIN_C // 2, 0, GRID_W - WIN_C)
        colmask = (kc >= cs) & (kc < cs + WIN_C)
        for d in range(14):
            pair[d] = pltpu.roll(jnp.broadcast_to(r_ref[d:d + 1, :], (GRID_W, HD)), 0, 1, stride=1, stride_axis=0)
        for off in range(8):
            rows = jnp.concatenate([pair[off + 2 * jj] for jj in range(4)], axis=1)
            bias_ref[off] = jnp.where(colmask, rows, NEG)
        vb[...] = v_ref[...].astype(BF16)

        def row(r, carry):
            start, off = _nbr_window(r)
            q0 = pl.multiple_of(r * GRID_W, GRID_W)
            k0 = pl.multiple_of(start * GRID_W, GRID_W)
            s = _dot_nt(q_ref[pl.ds(q0, GRID_W), :], k_ref[pl.ds(k0, 512), :]) * SCALE + bias_ref[off]
            mx = jnp.max(s, axis=-1, keepdims=True)
            e = jnp.exp(s - mx)
            den = jnp.sum(e, axis=-1, keepdims=True)
            o_ref[pl.ds(q0, GRID_W), :] = _dot((e / den).astype(BF16), vb[pl.ds(k0, 512), :])
            l_ref[pl.ds(q0, GRID_W), :] = jnp.broadcast_to(mx + jnp.log(den), (GRID_W, HD))
            return carry

        lax.fori_loop(0, T // GRID_W, row, 0, unroll=2)

    return pl.pallas_call(
        body, name="attn_b_fwd",
        out_shape=[SDS((T, 512), F32), SDS((T, 512), F32), SDS((4, 8, GRID_W, 512), F32)], grid=(4,),
        in_specs=[pl.BlockSpec((None, 16, HD), lambda h: (h, 0, 0)),
                  pl.BlockSpec((T, HD), lambda h: (0, NHA + h)),
                  pl.BlockSpec((T, HD), lambda h: (0, NH + NHA + h)),
                  pl.BlockSpec((T, HD), lambda h: (0, 2 * NH + NHA + h))],
        out_specs=[pl.BlockSpec((T, HD), lambda h: (0, h)), pl.BlockSpec((T, HD), lambda h: (0, h)),
                   pl.BlockSpec((None, 8, GRID_W, 512), lambda h: (h, 0, 0, 0))],
        scratch_shapes=[pltpu.VMEM((T, HD), BF16), pltpu.VMEM((14, GRID_W, HD), F32)],
        compiler_params=_params(1))(rpb_rows, qkn, qkn, proj)


def _comb_fwd(os, ls):
    tm = 512

    def body(o0, o1, o2, l0, l1, l2, oa_ref, w0, w1, w2):
        lv = [l0[...], l1[...], l2[...]]
        mx = jnp.maximum(jnp.maximum(lv[0], lv[1]), lv[2])
        ev = [jnp.exp(l - mx) for l in lv]
        den = ev[0] + ev[1] + ev[2]
        wv = [e / den for e in ev]
        oa_ref[...] = (wv[0] * o0[...] + wv[1] * o1[...] + wv[2] * o2[...]).astype(BF16)
        w0[...], w1[...], w2[...] = wv

    spec = pl.BlockSpec((tm, 512), lambda i: (i, 0))
    return pl.pallas_call(
        body, name="comb_fwd", out_shape=[SDS((T, 512), BF16)] + [SDS((T, 512), F32)] * 3, grid=(T // tm,),
        in_specs=[spec] * 6, out_specs=[spec] * 4, compiler_params=_params(1))(*os, *ls)


def _mix_fwd(oa, ob, proj, b_gate, wpa, wpb):
    tm = 512

    def body(oa_ref, ob_ref, ga_ref, gb_ref, ba_ref, bb_ref, wpa_ref, wpb_ref, mixed_ref, ob16_ref):
        oav = oa_ref[...]
        obv = ob_ref[...].astype(BF16)
        ob16_ref[...] = obv
        for s in range(NSH):
            sl = slice(s * 512, (s + 1) * 512)
            ga = _sigmoid(ga_ref[:, sl] + ba_ref[:, sl])
            gb = _sigmoid(gb_ref[:, sl] + bb_ref[:, sl])
            mixed_ref[:, sl] = (ga * _dot(oav, wpa_ref[s]) + gb * _dot(obv, wpb_ref[s])).astype(BF16)

    row = lambda w: pl.BlockSpec((tm, w), lambda i: (i, 0))
    return pl.pallas_call(
        body, name="mix_fwd", out_shape=[SDS((T, D), BF16), SDS((T, 512), BF16)], grid=(T // tm,),
        in_specs=[row(512), row(512),
                  pl.BlockSpec((tm, D), lambda i: (i, 3)), pl.BlockSpec((tm, D), lambda i: (i, 4)),
                  pl.BlockSpec((1, D), lambda i: (0, 0)), pl.BlockSpec((1, D), lambda i: (0, 1)),
                  _resident((NSH, 512, 512), lambda i: (0, 0, 0)), _resident((NSH, 512, 512), lambda i: (0, 0, 0))],
        out_specs=[row(D), row(512)], compiler_params=_params(1))(oa, ob, proj, proj, b_gate, b_gate, wpa, wpb)


def _out_proj_fwd(mixed, w_out, x, g):
    tm = 512

    def body(m_ref, w_ref, x_ref, g_ref, h1_ref, hn_ref):
        h1 = x_ref[...] + _dot(m_ref[...], w_ref[...])
        h1_ref[...] = h1
        r = lax.rsqrt(jnp.mean(h1 * h1, axis=-1, keepdims=True) + EPS)
        hn_ref[...] = (h1 * r * g_ref[...]).astype(BF16)

    row = pl.BlockSpec((tm, D), lambda i: (i, 0))
    return pl.pallas_call(
        body, name="out_proj_fwd", out_shape=[SDS((T, D), F32), SDS((T, D), BF16)], grid=(T // tm,),
        in_specs=[row, _resident((D, D), lambda i: (0, 0)), row, pl.BlockSpec((1, D), lambda i: (0, 0))],
        out_specs=[row, row], compiler_params=_params(1))(mixed, w_out, x, g)


def _ffn_up(hn, w_up):
    tm, tn = T, 512
    per = (DFF // NSH) // tn

    def body(h_ref, w_ref, a_ref, u_ref):
        uv = jnp.maximum(_dot(h_ref[...], w_ref[...]), 0.0)
        a_ref[...] = (uv * uv).astype(BF16)
        u_ref[...] = uv.astype(BF16)

    out = pl.BlockSpec((tm, tn), lambda i, j: (i, j))
    return pl.pallas_call(
        body, name="ffn_up", out_shape=[SDS((T, DFF), BF16)] * 2, grid=(T // tm, DFF // tn),
        in_specs=[pl.BlockSpec((tm, D), lambda i, j: (i, 0)),
                  pl.BlockSpec((None, D, tn), lambda i, j: (j // per, 0, j % per))],
        out_specs=[out, out], compiler_params=_params(2))(hn, w_up)


def _ffn_down_loss(u, w_down, h1, target):
    tm, tk = 512, 2048
    nk = DFF // tk

    def body(u_ref, w_ref, h1_ref, t_ref, dy_ref, dy16_ref, loss_ref, acc):
        k = pl.program_id(1)

        @pl.when(k == 0)
        def _():
            acc[...] = jnp.zeros_like(acc)

        acc[...] += _dot(u_ref[...], w_ref[...])

        @pl.when(k == nk - 1)
        def _():
            def chunk(r, sq):
                rows = pl.ds(pl.multiple_of(r * 16, 16), 16)
                err = acc[rows, :] + h1_ref[rows, :] - t_ref[rows, :]
                dy = err * (1.0 / D)
                dy_ref[rows, :] = dy
                dy16_ref[rows, :] = dy.astype(BF16)
                return sq + err * err

            sq = lax.fori_loop(0, tm // 16, chunk, jnp.zeros((16, D), F32), unroll=2)
            part = 0.5 * jnp.sum(jnp.mean(sq, axis=-1, keepdims=True), axis=0, keepdims=True)
            loss_ref[...] = jnp.broadcast_to(part, (8, 128))

    row = pl.BlockSpec((tm, D), lambda i, k: (i, 0))
    once = _resident((tm, D), lambda i, k: (i, 0))
    return pl.pallas_call(
        body, name="ffn_down_loss",
        out_shape=[SDS((T, D), F32), SDS((T, D), BF16), SDS((T // tm, 8, 128), F32)], grid=(T // tm, nk),
        in_specs=[pl.BlockSpec((tm, tk), lambda i, k: (i, k)), pl.BlockSpec((tk, D), lambda i, k: (k, 0)), once, once],
        out_specs=[row, row, pl.BlockSpec((None, 8, 128), lambda i, k: (i, 0, 0))],
        scratch_shapes=[pltpu.VMEM((tm, D), F32)], compiler_params=_params(2))(u, w_down, h1, target)


def _ffn_down_bwd(dy16, w_down, u, deps=()):
    tm, tn = T, 512

    def body(dy_ref, w_ref, u_ref, du_ref):
        uv = u_ref[...].astype(F32)
        du_ref[...] = jnp.where(uv > 0.0, 2.0 * uv * _dot_nt(dy_ref[...], w_ref[...]), 0.0).astype(BF16)

    return pl.pallas_call(
        _after(body, deps), name="ffn_down_bwd", out_shape=SDS((T, DFF), BF16), grid=(T // tm, DFF // tn),
        in_specs=[DEP_SPEC] * len(deps) + [
            pl.BlockSpec((tm, D), lambda i, j: (i, 0)), pl.BlockSpec((tn, D), lambda i, j: (j, 0)),
            pl.BlockSpec((tm, tn), lambda i, j: (i, j))],
        out_specs=pl.BlockSpec((tm, tn), lambda i, j: (i, j)), compiler_params=_params(2))(*deps, dy16, w_down, u)


def _norm_bwd(xv, dz_in, g):
    r = lax.rsqrt(jnp.mean(xv * xv, axis=-1, keepdims=True) + EPS)
    dg = jnp.sum(xv * r * dz_in, axis=0, keepdims=True)
    dz = dz_in * g
    dx = r * dz - xv * (r * r * r) * jnp.mean(xv * dz, axis=-1, keepdims=True)
    return dx, dg


def _ffn_up_bwd(du, w_up, h1, dy, g, deps=()):
    tm, tk = 512, 1024
    per = (DFF // NSH) // tk
    nk = DFF // tk

    def body(du_ref, w_ref, h1_ref, dy_ref, g_ref, dh1_ref, dh16_ref, dg_ref, acc):
        i, k = pl.program_id(0), pl.program_id(1)

        @pl.when(k == 0)
        def _():
            acc[...] = jnp.zeros_like(acc)

        @pl.when((k == 0) & (i == 0))
        def _():
            dg_ref[...] = jnp.zeros_like(dg_ref)

        acc[...] += _dot_nt(du_ref[...], w_ref[...])

        @pl.when(k == nk - 1)
        def _():
            dx, dg = _norm_bwd(h1_ref[...], acc[...], g_ref[...])
            dh1 = dy_ref[...] + dx
            dh1_ref[...] = dh1
            dh16_ref[...] = dh1.astype(BF16)
            dg_ref[...] += dg

    row = pl.BlockSpec((tm, D), lambda i, k: (i, 0))
    vec = pl.BlockSpec((1, D), lambda i, k: (0, 0))
    return pl.pallas_call(
        _after(body, deps), name="ffn_up_bwd", out_shape=[SDS((T, D), F32), SDS((T, D), BF16), SDS((1, D), F32)],
        grid=(T // tm, nk),
        in_specs=[DEP_SPEC] * len(deps) + [
            pl.BlockSpec((tm, tk), lambda i, k: (i, k)),
            pl.BlockSpec((None, D, tk), lambda i, k: (k // per, 0, k % per)), row, row, vec],
        out_specs=[row, row, vec], scratch_shapes=[pltpu.VMEM((tm, D), F32)],
        compiler_params=_params(2))(*deps, du, w_up, h1, dy, g)


def _mix_bwd(dh16, w_out, oa, ob16, proj, b_gate, wpa, wpb):
    tm = 256

    def body(dh_ref, wo_ref, oa_ref, ob_ref, ga_ref, gb_ref, ba_ref, bb_ref, wpa_ref, wpb_ref,
             dya_ref, dyb_ref, dga_ref, dgb_ref, doa_ref, dob_ref, dba_ref, dbb_ref):
        @pl.when(pl.program_id(0) == 0)
        def _():
            dba_ref[...] = jnp.zeros_like(dba_ref)
            dbb_ref[...] = jnp.zeros_like(dbb_ref)

        oav, obv = oa_ref[...], ob_ref[...]
        doa = jnp.zeros((tm, 512), F32)
        dob = jnp.zeros((tm, 512), F32)
        for s in range(NSH):
            sl = slice(s * 512, (s + 1) * 512)
            dm = _dot_nt(dh_ref[...], wo_ref[sl, :])
            ga = _sigmoid(ga_ref[:, sl] + ba_ref[:, sl])
            gb = _sigmoid(gb_ref[:, sl] + bb_ref[:, sl])
            dya = (dm * ga).astype(BF16)
            dyb = (dm * gb).astype(BF16)
            dza = dm * _dot(oav, wpa_ref[s]) * ga * (1.0 - ga)
            dzb = dm * _dot(obv, wpb_ref[s]) * gb * (1.0 - gb)
            dya_ref[:, sl], dyb_ref[:, sl] = dya, dyb
            dga_ref[:, sl], dgb_ref[:, sl] = dza.astype(BF16), dzb.astype(BF16)
            dba_ref[:, sl] += jnp.sum(dza, axis=0, keepdims=True)
            dbb_ref[:, sl] += jnp.sum(dzb, axis=0, keepdims=True)
            doa += _dot_nt(dya, wpa_ref[s])
            dob += _dot_nt(dyb, wpb_ref[s])
        doa_ref[...], dob_ref[...] = doa, dob

    row = lambda w: pl.BlockSpec((tm, w), lambda i: (i, 0))
    vec = pl.BlockSpec((1, D), lambda i: (0, 0))
    wp = _resident((NSH, 512, 512), lambda i: (0, 0, 0))
    return pl.pallas_call(
        body, name="mix_bwd",
        out_shape=[SDS((T, D), BF16)] * 4 + [SDS((T, 512), F32)] * 2 + [SDS((1, D), F32)] * 2, grid=(T // tm,),
        in_specs=[row(D), _resident((D, D), lambda i: (0, 0)), row(512), row(512),
                  pl.BlockSpec((tm, D), lambda i: (i, 3)), pl.BlockSpec((tm, D), lambda i: (i, 4)),
                  pl.BlockSpec((1, D), lambda i: (0, 0)), pl.BlockSpec((1, D), lambda i: (0, 1)), wp, wp],
        out_specs=[row(D)] * 4 + [row(512)] * 2 + [vec] * 2,
        compiler_params=_params(1))(dh16, w_out, oa, ob16, proj, proj, b_gate, b_gate, wpa, wpb)


def _comb_bwd(doa, os, ws, deps=()):
    tm = 512

    def body(d_ref, o0, o1, o2, w0, w1, w2, cc_ref):
        prod = d_ref[...] * (w0[...] * o0[...] + w1[...] * o1[...] + w2[...] * o2[...])
        for h in range(4):
            sl = slice(h * HD, (h + 1) * HD)
            cc_ref[:, sl] = jnp.broadcast_to(jnp.sum(prod[:, sl], axis=-1, keepdims=True), (tm, HD))

    spec = pl.BlockSpec((tm, 512), lambda i: (i, 0))
    return pl.pallas_call(
        _after(body, deps), name="comb_bwd", out_shape=SDS((T, 512), F32), grid=(T // tm,),
        in_specs=[DEP_SPEC] * len(deps) + [spec] * 7, out_specs=spec,
        compiler_params=_params(1))(*deps, doa, *os, *ws)


def _attn_a_bwd(qkn, proj, doa, lse, w, cc, g):
    dil = DILS[g]
    m = T // dil
    nb = m // 128
    hp = _heads_per_step(m)

    def body(q_ref, k_ref, v_ref, d_ref, l_ref, w_ref, c_ref, dqk_ref, dv_ref, kp, vp, dkp, dvp):
        for hh in range(hp):
            sl = slice(hh * HD, (hh + 1) * HD)
            _fill_padded(kp, k_ref[:, sl], m)
            _fill_padded(vp, v_ref[:, sl], m)
            dkp[...] = jnp.zeros_like(dkp)
            dvp[...] = jnp.zeros_like(dvp)

            def block(b, carry):
                q0 = pl.multiple_of(b * 128, 128)
                rows = pl.ds(q0, 128)
                win = pl.ds(q0, 256)
                qb, kw, vw = q_ref[rows, sl], kp[win, :], vp[win, :]
                s = _dot_nt(qb, kw) * SCALE
                s = jnp.where(_band_mask(q0, m), s, NEG)
                wp = _wide(w_ref[rows, sl], 2) * jnp.exp(s - _wide(l_ref[rows, sl], 2))
                dob = d_ref[rows, sl].astype(BF16)
                ds = (wp * (_dot_nt(dob, vw) - _wide(c_ref[rows, sl], 2))).astype(BF16)
                dqk_ref[0, rows, sl] = _dot(ds, kw) * SCALE
                dkp[win, :] += _dot_tn(ds, qb) * SCALE
                dvp[win, :] += _dot_tn(wp.astype(BF16), dob)
                return carry

            lax.fori_loop(0, nb, block, 0, unroll=min(nb, 2))
            dqk_ref[1, :, sl] = dkp[64:64 + m, :]
            dv_ref[:, sl] = dvp[64:64 + m, :]

    blk = pl.BlockSpec((m, hp * HD), lambda h, r: (0, r * (4 // hp) + h))
    view = lambda a: a.reshape(m, dil * 512)
    dqk, dv = pl.pallas_call(
        body, name=f"attn_a_bwd_{g}", out_shape=[SDS((2, m, dil * 512), F32), SDS((m, dil * 512), F32)],
        grid=(4 // hp, dil), in_specs=[blk] * 7,
        out_specs=[pl.BlockSpec((2, m, hp * HD), lambda h, r: (0, 0, r * (4 // hp) + h)), blk],
        scratch_shapes=[pltpu.VMEM((m + 128, HD), BF16), pltpu.VMEM((m + 128, HD), BF16),
                        pltpu.VMEM((m + 128, HD), F32), pltpu.VMEM((m + 128, HD), F32)],
        compiler_params=_params(2))(*_group_views(qkn, proj, g), view(doa), view(lse), view(w), view(cc))
    return dqk.reshape(2, T, 512), dv.reshape(T, 512)


def _attn_b_bwd(qkn, proj, dob, ob, lse, bias, deps=()):
    def body(q_ref, k_ref, v_ref, d_ref, o_ref, l_ref, bias_ref, dqk_ref, dv_ref, drpb_ref, vb, dk_acc, dv_acc, a_acc):
        vb[...] = v_ref[...].astype(BF16)
        dk_acc[...] = jnp.zeros_like(dk_acc)
        dv_acc[...] = jnp.zeros_like(dv_acc)
        a_acc[...] = jnp.zeros_like(a_acc)

        def row(r, carry):
            start, off = _nbr_window(r)
            rows = pl.ds(pl.multiple_of(r * GRID_W, GRID_W), GRID_W)
            win = pl.ds(pl.multiple_of(start * GRID_W, GRID_W), 512)
            qr, kw, vw = q_ref[rows, :], k_ref[win, :], vb[win, :]
            s = _dot_nt(qr, kw) * SCALE + bias_ref[off]
            p = jnp.exp(s - _wide(l_ref[rows, :], 4))
            dov = d_ref[rows, :]
            delta = jnp.sum(dov * o_ref[rows, :], axis=-1, keepdims=True)
            do16 = dov.astype(BF16)
            ds = p * (_dot_nt(do16, vw) - delta)
            a_acc[off] += ds
            ds16 = ds.astype(BF16)
            dqk_ref[0, rows, :] = _dot(ds16, kw) * SCALE
            dk_acc[win, :] += _dot_tn(ds16, qr) * SCALE
            dv_acc[win, :] += _dot_tn(p.astype(BF16), do16)
            return carry

        lax.fori_loop(0, T // GRID_W, row, 0, unroll=2)
        dqk_ref[1] = dk_acc[...]
        dv_ref[...] = dv_acc[...]

        lane = lax.broadcasted_iota(jnp.int32, (16, HD), 1)
        rowi = lax.broadcasted_iota(jnp.int32, (16, HD), 0)
        low = (lane >= GRID_W - WIN_C) & (lane < GRID_W + WIN_C - 1)
        high = (lane >= HD - WIN_C) | (lane < WIN_C - 1)
        flip = (lax.broadcasted_iota(jnp.int32, (GRID_W, GRID_W), 0)
                + lax.broadcasted_iota(jnp.int32, (GRID_W, GRID_W), 1) == GRID_W - 1).astype(BF16)
        out = jnp.zeros((16, HD), F32)
        for d in range(14):
            acc = None
            for off in range(8):
                if 0 <= d - off <= 6 and (d - off) % 2 == 0:
                    jj = (d - off) // 2
                    piece = a_acc[off, :, jj * HD:(jj + 1) * HD]
                    acc = piece if acc is None else acc + piece
            hi = acc.astype(BF16)
            lo = (acc - hi.astype(F32)).astype(BF16)
            rev = _dot(flip, hi) + _dot(flip, lo)
            v = jnp.sum(pltpu.roll(rev, 0, 1, stride=1, stride_axis=0), axis=0, keepdims=True)
            v = jnp.broadcast_to(v, (16, HD))
            out = out + jnp.where((rowi == d) & low, v, 0.0)
            out = out + jnp.where(rowi == d + 1, pltpu.roll(jnp.where(high, v, 0.0), GRID_W, 1), 0.0)
        drpb_ref[...] = out

    blk = pl.BlockSpec((T, HD), lambda h: (0, h))
    return pl.pallas_call(
        _after(body, deps), name="attn_b_bwd",
        out_shape=[SDS((2, T, 512), F32), SDS((T, 512), F32), SDS((4, 16, HD), F32)], grid=(4,),
        in_specs=[DEP_SPEC] * len(deps) + [
            pl.BlockSpec((T, HD), lambda h: (0, NHA + h)),
            pl.BlockSpec((T, HD), lambda h: (0, NH + NHA + h)),
            pl.BlockSpec((T, HD), lambda h: (0, 2 * NH + NHA + h)), blk, blk, blk,
            pl.BlockSpec((None, 8, GRID_W, 512), lambda h: (h, 0, 0, 0))],
        out_specs=[pl.BlockSpec((2, T, HD), lambda h: (0, 0, h)), blk,
                   pl.BlockSpec((None, 16, HD), lambda h: (h, 0, 0))],
        scratch_shapes=[pltpu.VMEM((T, HD), BF16), pltpu.VMEM((T, HD), F32), pltpu.VMEM((T, HD), F32),
                        pltpu.VMEM((8, GRID_W, 512), F32)],
        compiler_params=_params(1))(*deps, qkn, qkn, proj, dob, ob, lse, bias)


def _qk_bwd(proj, nw, cos, sin, dqk_groups, dqk_b, dvs, dga, dgb):
    tm = 256

    def body(p_ref, w_ref, cos_ref, sin_ref, d0, d1, d2, d3, v0, v1, v2, v3, ga_ref, gb_ref, o_ref, dn_ref):
        j, i = pl.program_id(0), pl.program_id(1)

        @pl.when((j < 2) & (i == 0))
        def _():
            dn_ref[...] = jnp.zeros_like(dn_ref)

        @pl.when(j < 2)
        def _():
            cv, sv = cos_ref[...], sin_ref[...]
            srcs = (d0, d1, d2, d3)
            dna = jnp.zeros((1, HD), F32)
            dnb = jnp.zeros((1, HD), F32)
            for h in range(NH):
                sl = slice(h * HD, (h + 1) * HD)
                dz = srcs[h // 4][:, (h % 4) * HD:(h % 4 + 1) * HD]
                if h < NHA:
                    dz = dz * cv + pltpu.roll(dz * sv, 64, 1)
                dx, dg = _norm_bwd(p_ref[:, sl], dz, w_ref[:, sl])
                o_ref[:, sl] = dx.astype(BF16)
                if h < NHA:
                    dna += dg
                else:
                    dnb += dg
            dn_ref[0:1, :] += dna
            dn_ref[1:2, :] += dnb

        @pl.when(j == 2)
        def _():
            for s, v_ref in enumerate((v0, v1, v2, v3)):
                o_ref[:, s * 512:(s + 1) * 512] = v_ref[...].astype(BF16)

        @pl.when(j == 3)
        def _():
            o_ref[...] = ga_ref[...]

        @pl.when(j == 4)
        def _():
            o_ref[...] = gb_ref[...]

    def rows(used):
        return lambda j, i: (jnp.where(used(j), i, 0), 0)

    qk = lambda j: j < 2
    dspec = pl.BlockSpec((None, tm, 512), lambda j, i: (jnp.minimum(j, 1), jnp.where(j < 2, i, 0), 0))
    vspec = pl.BlockSpec((tm, 512), rows(lambda j: j == 2))
    return pl.pallas_call(
        body, name="qk_bwd", out_shape=[SDS((T, DIN), BF16), SDS((2, 8, HD), F32)], grid=(5, T // tm),
        in_specs=[pl.BlockSpec((tm, D), lambda j, i: (jnp.where(j < 2, i, 0), jnp.minimum(j, 1))),
                  pl.BlockSpec((None, 1, D), lambda j, i: (jnp.minimum(j, 1), 0, 0)),
                  pl.BlockSpec((tm, HD), rows(qk)), pl.BlockSpec((tm, HD), rows(qk)),
                  dspec, dspec, dspec, dspec, vspec, vspec, vspec, vspec,
                  pl.BlockSpec((tm, D), rows(lambda j: j == 3)), pl.BlockSpec((tm, D), rows(lambda j: j == 4))],
        out_specs=[pl.BlockSpec((tm, D), lambda j, i: (i, j)),
                   pl.BlockSpec((None, 8, HD), lambda j, i: (jnp.minimum(j, 1), 0, 0))],
        compiler_params=_params(2))(proj, nw, cos, sin, *dqk_groups, dqk_b, *dvs, dga, dgb)


def _in_proj_bwd(dproj, w_in, x, dh1, g, deps=()):
    tm, tk = 512, 1280
    per = (DIN // NSH) // tk
    nk = DIN // tk

    def body(dp_ref, w_ref, x_ref, dh_ref, g_ref, dx_ref, dg_ref, acc):
        i, k = pl.program_id(0), pl.program_id(1)

        @pl.when(k == 0)
        def _():
            acc[...] = jnp.zeros_like(acc)

        @pl.when((k == 0) & (i == 0))
        def _():
            dg_ref[...] = jnp.zeros_like(dg_ref)

        acc[...] += _dot_nt(dp_ref[...], w_ref[...])

        @pl.when(k == nk - 1)
        def _():
            dx, dg = _norm_bwd(x_ref[...], acc[...], g_ref[...])
            dx_ref[...] = dh_ref[...] + dx
            dg_ref[...] += dg

    row = pl.BlockSpec((tm, D), lambda i, k: (i, 0))
    vec = pl.BlockSpec((1, D), lambda i, k: (0, 0))
    return pl.pallas_call(
        _after(body, deps), name="in_proj_bwd", out_shape=[SDS((T, D), F32), SDS((1, D), F32)], grid=(T // tm, nk),
        in_specs=[DEP_SPEC] * len(deps) + [
            pl.BlockSpec((tm, tk), lambda i, k: (i, k)),
            pl.BlockSpec((None, D, tk), lambda i, k: (k // per, 0, k % per)), row, row, vec],
        out_specs=[row, vec], scratch_shapes=[pltpu.VMEM((tm, D), F32)],
        compiler_params=_params(2))(*deps, dproj, w_in, x, dh1, g)


def _grad_w(name, a, g, shard_rows, rows, cols, tr, tc):
    ni, nj = rows // tr, cols // tc
    if shard_rows:
        a_map, g_map = (lambda s, i, j: (0, s * ni + i)), (lambda s, i, j: (0, j))
    else:
        a_map, g_map = (lambda s, i, j: (0, i)), (lambda s, i, j: (0, s * nj + j))

    def body(a_ref, g_ref, o_ref):
        o_ref[...] = _dot_tn(a_ref[...], g_ref[...]).astype(BF16)

    return pl.pallas_call(
        body, name=name, out_shape=SDS((NSH, rows, cols), BF16), grid=(NSH, ni, nj),
        in_specs=[pl.BlockSpec((T, tr), a_map), pl.BlockSpec((T, tc), g_map)],
        out_specs=pl.BlockSpec((None, tr, tc), lambda s, i, j: (s, i, j)), compiler_params=_params(3))(a, g)


def _adamw(w, g, m, v):
    m = B1 * m + (1.0 - B1) * g
    v = B2 * v + (1.0 - B2) * (g * g)
    m_hat = m / (1.0 - B1 ** STEP)
    v_hat = v / (1.0 - B2 ** STEP)
    delta = -LR * (m_hat / (jnp.sqrt(v_hat) + AEPS) + WD * w)
    return delta, m, v


def _sum_halves(name, place, grads, theirs):
    _, rows, cols = theirs.shape
    tr = _row_tile(rows, cols, 1 << 20)

    def body(place_ref, a_ref, b_ref, o_ref):
        o_ref[...] = (a_ref[...].astype(F32) + b_ref[...].astype(F32)).astype(BF16)

    spec = pl.BlockSpec((None, tr, cols), lambda s, i, p: (s, i, 0))
    return pl.pallas_call(
        body, name=name, out_shape=SDS(theirs.shape, BF16),
        grid_spec=pltpu.PrefetchScalarGridSpec(
            num_scalar_prefetch=1, grid=(NSH, rows // tr),
            in_specs=[pl.BlockSpec((None, None, tr, cols), lambda s, i, p: (s, p[1], i, 0)), spec], out_specs=spec),
        compiler_params=_params(2))(place, grads, theirs)


def _sum_landed(name, place, part, landed):
    _, rows, cols = part.shape
    tr = _row_tile(rows, cols, 1 << 20)

    def body(place_ref, p_ref, l_ref, o_ref):
        o_ref[...] = ((p_ref[...].astype(F32) + l_ref[0].astype(F32)) + l_ref[1].astype(F32)) + l_ref[2].astype(F32)

    return pl.pallas_call(
        body, name=name, out_shape=SDS((2, rows, cols), F32),
        grid_spec=pltpu.PrefetchScalarGridSpec(
            num_scalar_prefetch=1, grid=(rows // tr,),
            in_specs=[pl.BlockSpec((None, tr, cols), lambda i, p: (p[0], i, 0)),
                      pl.BlockSpec((3, tr, cols), lambda i, p: (0, i, 0))],
            out_specs=pl.BlockSpec((None, tr, cols), lambda i, p: (p[1], i, 0))),
        compiler_params=_params(1))(place, part, landed)


def _adam_shard(name, g, w, m, v):
    rows, cols = w.shape
    tr = _row_tile(rows, cols, 1 << 19)

    def body(g_ref, w_ref, m_ref, v_ref, go_ref, d_ref, nm_ref, nv_ref):
        g = g_ref[...]
        go_ref[...] = g
        d_ref[...], nm_ref[...], nv_ref[...] = _adamw(w_ref[...], g, m_ref[...], v_ref[...])

    spec = pl.BlockSpec((tr, cols), lambda i: (i, 0))
    return pl.pallas_call(
        body, name=name, out_shape=[SDS((rows, cols), F32)] * 4, grid=(rows // tr,),
        in_specs=[spec] * 4, out_specs=[spec] * 4, compiler_params=_params(1))(g, w, m, v)


def _adam_small(gathered, w, m, v):
    def body(g_ref, w_ref, m_ref, v_ref, go_ref, d_ref, nm_ref, nv_ref):
        g = g_ref[0:SMALL_ROWS, :]
        for dev in range(1, 8):
            g = g + g_ref[dev * SMALL_ROWS:(dev + 1) * SMALL_ROWS, :]
        go_ref[...] = g
        d_ref[...], nm_ref[...], nv_ref[...] = _adamw(w_ref[...], g, m_ref[...], v_ref[...])

    return pl.pallas_call(body, name="adam_small", out_shape=[SDS((SMALL_ROWS, HD), F32)] * 4)(gathered, w, m, v)


SMALL = (("norm_mix", (1, D)), ("b_gate", (1, 2 * D)), ("q_norm_a", (1, HD)), ("k_norm_a", (1, HD)),
         ("q_norm_b", (1, HD)), ("k_norm_b", (1, HD)), ("rpb_b", (1, 4, 15, 31)), ("norm_ffn", (1, D)))


def _pack_small(vals):
    pieces = []
    for (name, shape), val in zip(SMALL, vals):
        flat = val.reshape(-1)
        pad = (-flat.shape[0]) % HD
        pieces.append(jnp.pad(flat, (0, pad)).reshape(-1, HD))
    packed = jnp.concatenate(pieces, axis=0)
    return jnp.pad(packed, ((0, SMALL_ROWS - packed.shape[0]), (0, 0)))


def _unpack_small(packed):
    out, row = [], 0
    for name, shape in SMALL:
        size = int(np.prod(shape))
        nrows = -(-size // HD)
        out.append(packed[row:row + nrows].reshape(-1)[:size].reshape(shape))
        row += nrows
    return out


def kernel(x, norm_mix, w_in, b_gate, q_norm_a, k_norm_a, q_norm_b, k_norm_b, rpb_b, w_proj_a, w_proj_b, w_out, norm_ffn, w_up, w_down, loss_target, m_norm_mix, m_w_in, m_b_gate, m_q_norm_a, m_k_norm_a, m_q_norm_b, m_k_norm_b, m_rpb_b, m_w_proj_a, m_w_proj_b, m_w_out, m_norm_ffn, m_w_up, m_w_down, v_norm_mix, v_w_in, v_b_gate, v_q_norm_a, v_k_norm_a, v_q_norm_b, v_k_norm_b, v_rpb_b, v_w_proj_a, v_w_proj_b, v_w_out, v_norm_ffn, v_w_up, v_w_down):
    big_names = ("w_in", "w_proj_a", "w_proj_b", "w_out", "w_up", "w_down")
    big_w = [a[0] for a in (w_in, w_proj_a, w_proj_b, w_out, w_up, w_down)]
    big_m = [a[0] for a in (m_w_in, m_w_proj_a, m_w_proj_b, m_w_out, m_w_up, m_w_down)]
    big_v = [a[0] for a in (v_w_in, v_w_proj_a, v_w_proj_b, v_w_out, v_w_up, v_w_down)]
    x2, target = x[0], loss_target[0]

    place = jnp.stack([2 * lax.axis_index("x") + lax.axis_index("y"), lax.axis_index("c")]).astype(jnp.int32)
    groups = ((0,), (1, 2, 3), (4,), (5,))
    near = _gather_start("gather_start_0", [_cast_into_place(big_w[0], "cast_w_in", place)], ks=(0, 1))
    placed = [None] + [[_cast_into_place(big_w[i], "cast_" + big_names[i], place, (near[4],)) for i in grp]
                       for grp in groups[1:]]

    def whole(fulls):
        return [f.reshape(NSH, 2 * f.shape[2], f.shape[3]) for f in fulls]

    def forward_begin(j, after):
        send, recv, _, fulls, _ = started[j]
        fulls = _gather_wait(f"gather_wait_{j}", send, recv, fulls, after)
        send, recv, _, fulls, token = _forward_start(f"forward_start_{j}", fulls)
        return (send, recv, fulls), token

    def forward_end(j, state, after):
        return whole(_forward_wait(f"forward_wait_{j}", *state, after))

    def as_halves(grads):
        return [g.reshape(NSH, 2, g.shape[1] // 2, g.shape[2]) for g in grads]

    def reduce_start(j, grads, theirs):
        parts = [_sum_halves(f"sum_halves_{j}_{i}", place, a, b) for i, (a, b) in enumerate(zip(grads, theirs))]
        send, recv, parts, lands, token = _reduce_start(f"reduce_start_{j}", parts)
        return (send, recv, parts, lands), token

    def reduce_begin(j, grads):
        grads = as_halves(grads)
        return reduce_start(j, grads, _reduce_exchange(f"reduce_exchange_{j}", grads))

    def exchange_begin(j, grads):
        send, recv, grads, lands, token = _exchange_start(f"exchange_start_{j}", as_halves(grads))
        return (send, recv, grads, lands), token

    def exchange_end(j, state, after):
        return reduce_start(j, *_exchange_wait(f"exchange_wait_{j}", *state, after))

    big_out = {}

    def share_begin(j, state, after):
        send, recv, parts, lands = state
        parts, lands = _reduce_wait(f"reduce_wait_{j}", send, recv, parts, lands, after)
        sums = [_sum_landed(f"sum_landed_{j}_{i}", place, p, l) for i, (p, l) in enumerate(zip(parts, lands))]
        send, recv, _, sums, token = _share_start(f"share_start_{j}", sums)
        return (send, recv, sums), token

    def share_end(j, state, after):
        for idx, g in zip(groups[j], _share_wait(f"share_wait_{j}", *state, after)):
            g = g.reshape(big_w[idx].shape)
            big_out[idx] = _adam_shard("adam_" + big_names[idx], g, big_w[idx], big_m[idx], big_v[idx])
        return big_out[groups[j][-1]][1]

    proj, xn = _norm_in_proj_own(x2, norm_mix, whole(near[3])[0], place)
    cast = [a for grp in placed[1:] for a in grp]
    win = _gather_wait("gather_wait_0", near[0], near[1], near[3], (proj, *cast), ks=(0, 1))
    far = _gather_start("gather_start_0_far", win, ks=(2,))
    win = _gather_finish("gather_finish_0", far[3], ks=(0, 1))
    proj = _in_proj_rest("in_proj_near", xn, whole(win)[0], proj, place, (2, 1))
    win = _gather_wait("gather_wait_0_far", far[0], far[1], win, (proj,), ks=(2,))
    started = [far] + [_gather_start(f"gather_start_{j}", placed[j], after=tuple(win)) for j in (1, 2, 3)]
    win = _gather_finish("gather_finish_0_far", win, ks=(2,))
    (win_f,) = whole(win)
    proj = _in_proj_rest("in_proj_far", xn, win_f, proj, place, (3,))
    cos, sin = _rope_tables()
    nw = jnp.stack([jnp.concatenate([jnp.tile(q_norm_a, (1, NHA)), jnp.tile(q_norm_b, (1, NH - NHA))], axis=1),
                    jnp.concatenate([jnp.tile(k_norm_a, (1, NHA)), jnp.tile(k_norm_b, (1, NH - NHA))], axis=1)])
    qkn = _qk_prep(proj, nw, cos, sin)
    fw1, token = forward_begin(1, (qkn,))
    fwd_a = [_attn_a_fwd(qkn, proj, g) for g in range(3)]
    os, ls = [f[0] for f in fwd_a], [f[1] for f in fwd_a]
    fw2, token = forward_begin(2, (os[2], token))
    ob, lse_b, bias = _attn_b_fwd(qkn, proj, _rpb_rows(rpb_b[0]))
    oa, w0, w1, w2 = _comb_fwd(os, ls)
    ws = [w0, w1, w2]
    wpa_f, wpb_f, wout_f = forward_end(1, fw1, (oa, token))
    wout_f = wout_f.reshape(D, D)
    mixed, ob16 = _mix_fwd(oa, ob, proj, b_gate, wpa_f, wpb_f)
    h1, hn = _out_proj_fwd(mixed, wout_f, x2, norm_ffn)
    fw3, token = forward_begin(3, (h1,))
    (wup_f,) = forward_end(2, fw2, (hn, token))
    usq, u = _ffn_up(hn, wup_f)
    (wdown_f,) = forward_end(3, fw3, (u,))
    wdown_f = wdown_f.reshape(DFF, D)
    dy, dy16, loss_parts = _ffn_down_loss(usq, wdown_f, h1, target)
    loss = lax.psum(jnp.sum(loss_parts[:, 0, 0]), ("x", "y", "c"))

    g_down = _grad_w("grad_w_down", usq, dy16, True, DFF // NSH, D, 1024, 1024)
    ex_down, token = exchange_begin(3, [g_down])
    du = _ffn_down_bwd(dy16, wdown_f, u, deps=(token,))
    g_up = _grad_w("grad_w_up", hn, du, False, D, DFF // NSH, 1024, 1024)
    red_down, token = exchange_end(3, ex_down, (g_up,))
    ex_up, token_up = exchange_begin(2, [g_up])
    dh1, dh16, d_norm_ffn = _ffn_up_bwd(du, wup_f, h1, dy, norm_ffn, deps=(token, token_up))
    dya, dyb, dga, dgb, doa, dob, dba, dbb = _mix_bwd(dh16, wout_f, oa, ob16, proj, b_gate, wpa_f, wpb_f)
    g_out = _grad_w("grad_w_out", mixed, dh16, True, D // NSH, D, 512, 1024)
    g_pa = _grad_w("grad_w_proj_a", oa, dya, False, 512, 512, 512, 512)
    g_pb = _grad_w("grad_w_proj_b", ob16, dyb, False, 512, 512, 512, 512)
    red_up, token = exchange_end(2, ex_up, (g_out,))
    ex_mid, token_mid = exchange_begin(1, [g_pa, g_pb, g_out])
    cc = _comb_bwd(doa, os, ws, deps=(token, token_mid))
    bwd_a = [_attn_a_bwd(qkn, proj, doa, ls[g], ws[g], cc, g) for g in range(3)]
    red_mid, token = exchange_end(1, ex_mid, (bwd_a[2][1],))
    dqk_b, dv_b, drpb_t = _attn_b_bwd(qkn, proj, dob, ob, lse_b, bias, deps=(token,))
    dproj, dn = _qk_bwd(proj, nw, cos, sin, [b[0] for b in bwd_a], dqk_b, [b[1] for b in bwd_a] + [dv_b], dga, dgb)
    g_in = _grad_w("grad_w_in", xn, dproj, False, D, DIN // NSH, 1024, 1280)
    red_in, token = reduce_begin(0, [g_in])
    grad_x, d_norm_mix = _in_proj_bwd(dproj, win_f, x2, dh1, norm_mix, deps=(token,))

    sh_down, token = share_begin(3, red_down, (grad_x,))
    sh_up, token = share_begin(2, red_up, (token,))
    done = share_end(3, sh_down, (token,))
    sh_mid, token = share_begin(1, red_mid, (done,))
    done = share_end(2, sh_up, (token,))
    sh_in, token = share_begin(0, red_in, (done,))
    done = share_end(1, sh_mid, (token,))
    done = share_end(0, sh_in, (done,))

    d_rpb = drpb_t[:, :15, GRID_W - WIN_C:GRID_W + WIN_C - 1]
    small_g = [d_norm_mix, jnp.concatenate([dba, dbb], axis=1), dn[0, 0], dn[1, 0], dn[0, 1], dn[1, 1], d_rpb, d_norm_ffn]
    gathered_small = _allgather_small(_pack_small(small_g), done)
    small_w = (norm_mix, b_gate, q_norm_a, k_norm_a, q_norm_b, k_norm_b, rpb_b, norm_ffn)
    small_m = (m_norm_mix, m_b_gate, m_q_norm_a, m_k_norm_a, m_q_norm_b, m_k_norm_b, m_rpb_b, m_norm_ffn)
    small_v = (v_norm_mix, v_b_gate, v_q_norm_a, v_k_norm_a, v_q_norm_b, v_k_norm_b, v_rpb_b, v_norm_ffn)
    small_out = [_unpack_small(p) for p in
                 _adam_small(gathered_small, _pack_small(small_w), _pack_small(small_m), _pack_small(small_v))]

    order = ("norm_mix", "w_in", "b_gate", "q_norm_a", "k_norm_a", "q_norm_b", "k_norm_b", "rpb_b",
             "w_proj_a", "w_proj_b", "w_out", "norm_ffn", "w_up", "w_down")
    small_idx = {name: i for i, (name, _) in enumerate(SMALL)}
    outs = []
    for kind in range(4):
        for name in order:
            if name in small_idx:
                outs.append(small_out[kind][small_idx[name]])
            else:
                outs.append(big_out[big_names.index(name)][kind][None])
    return (loss, grad_x[None], *outs)
```

```python
import functools

import numpy as np
import jax
import jax.numpy as jnp
from jax import lax
from jax.experimental import pallas as pl
from jax.experimental.pallas import tpu as pltpu

F32, BF16 = jnp.float32, jnp.bfloat16
SDS = jax.ShapeDtypeStruct
MESH = pl.DeviceIdType.MESH

T = 2048
D = 2048
HD = 128
NH, NHA = 16, 12
DIN = 10240
DFF = 8192
NSH = 4
DILS = (1, 4, 16)
EPS = 1e-6
NEG = -1e30
SCALE = HD ** -0.5
GRID_W, WIN_R, WIN_C = 64, 8, 16
VMEM_LIMIT = 56 * 1024 * 1024
B1, B2, LR, AEPS, WD, STEP = 0.9, 0.999, 0.001, 1e-08, 0.01, 10
SMALL_ROWS = 88


def _dot(a, b):
    return jnp.dot(a, b, preferred_element_type=F32)


def _dot_nt(a, b):
    return lax.dot_general(a, b, (((1,), (1,)), ((), ())), preferred_element_type=F32)


def _dot_tn(a, b):
    return lax.dot_general(a, b, (((0,), (0,)), ((), ())), preferred_element_type=F32)


def _params(n):
    return pltpu.CompilerParams(dimension_semantics=("arbitrary",) * n, vmem_limit_bytes=VMEM_LIMIT)


def _resident(shape, index_map):
    return pl.BlockSpec(shape, index_map, pipeline_mode=pl.Buffered(1))


def _sigmoid(z):
    return 1.0 / (1.0 + jnp.exp(-z))


def _wide(v, n):
    return jnp.concatenate([v] * n, axis=1)


def _row_tile(rows, cols, elems):
    tr = 16
    while tr * 2 <= rows and tr * 2 * cols <= elems:
        tr *= 2
    return tr


def _place():
    x, y, c = lax.axis_index("x"), lax.axis_index("y"), lax.axis_index("c")
    peers = [(1 - x, y), (x, 1 - y), (1 - x, 1 - y)]
    return x, y, c, peers


def _cast_into_place(w, name, place, deps=()):
    rows, cols = w.shape
    hr = rows // 2
    tr = min(hr, 256)
    per = hr // tr

    def body(*refs):
        w_ref, o_ref = refs[-2:]
        o_ref[...] = w_ref[...].astype(BF16)

    return pl.pallas_call(
        body, name=name, out_shape=SDS((NSH, 2, hr, cols), BF16),
        grid_spec=pltpu.PrefetchScalarGridSpec(
            num_scalar_prefetch=1, grid=(2, per),
            in_specs=[DEP_SPEC] * len(deps) + [pl.BlockSpec((tr, cols), lambda h, i, p: (h * per + i, 0))],
            out_specs=pl.BlockSpec((None, None, tr, cols), lambda h, i, p: (p[0], h, i, 0))),
        compiler_params=_params(2))(place, *deps, w)


ANY_SPEC = pl.BlockSpec(memory_space=pl.ANY)
HBM_SPEC = pl.BlockSpec(memory_space=pltpu.HBM)
SEM_SPEC = pl.BlockSpec(memory_space=pltpu.SEMAPHORE)
DEP_SPEC = pl.BlockSpec((8, 128), lambda *_: (0, 0))
EFFECT = pltpu.SideEffectType.DATAFLOW_SIDE_EFFECTING


def _after(body, deps):
    n = len(deps)
    return (lambda *refs: body(*refs[n:])) if n else body


SIBLING_BARRIER = 1


def _split_start(name, srcs, lands, n_copies, issue, sibling_only=False, after=()):
    n, m, d = len(srcs), len(lands), len(after)

    def body(*refs):
        if sibling_only:
            x, y, c, _ = _place()
            barrier = pltpu.get_barrier_semaphore()
            pl.semaphore_signal(barrier, inc=1, device_id=(x, y, 1 - c), device_id_type=MESH)
            pl.semaphore_wait(barrier, 1)
        issue(refs[:n], refs[n:n + m], refs[n + m + d], refs[n + m + d + 1])
        refs[-1][...] = jnp.zeros((8, 128), F32)

    arrays = list(srcs) + list(lands)
    outs = pl.pallas_call(
        body, name=name,
        out_shape=(pltpu.SemaphoreType.DMA((n_copies,)), pltpu.SemaphoreType.DMA((n_copies,)),
                   *[pltpu.HBM(a.shape, a.dtype) for a in arrays], SDS((8, 128), F32)),
        in_specs=[HBM_SPEC] * (n + m) + [ANY_SPEC] * d,
        out_specs=(SEM_SPEC, SEM_SPEC, *[HBM_SPEC] * (n + m), pl.BlockSpec(memory_space=pltpu.VMEM)),
        input_output_aliases={i: 2 + i for i in range(n + m)},
        compiler_params=pltpu.CompilerParams(has_side_effects=EFFECT,
                                             collective_id=SIBLING_BARRIER if sibling_only else None),
    )(*[pltpu.with_memory_space_constraint(a, pltpu.HBM) for a in arrays], *after)
    return outs[0], outs[1], list(outs[2:2 + n]), list(outs[2 + n:2 + n + m]), outs[-1]


def _split_wait(name, send_sems, recv_sems, srcs, lands, after, wait):
    n, m = len(srcs), len(lands)

    def body(*refs):
        wait(refs[:n], refs[n:n + m], refs[n + m], refs[n + m + 1])

    arrays = list(srcs) + list(lands)
    outs = pl.pallas_call(
        body, name=name, out_shape=[pltpu.HBM(a.shape, a.dtype) for a in arrays],
        in_specs=[HBM_SPEC] * (n + m) + [SEM_SPEC, SEM_SPEC] + [ANY_SPEC] * len(after),
        out_specs=[HBM_SPEC] * (n + m), input_output_aliases={i: i for i in range(n + m)},
        compiler_params=pltpu.CompilerParams(has_side_effects=EFFECT),
    )(*arrays, send_sems, recv_sems, *after)
    return list(outs[:n]), list(outs[n:])


def _gather_start(name, fulls, ks=(0, 1, 2), after=()):
    def issue(srcs, dsts, send_sems, recv_sems):
        x, y, c, peers = _place()
        for i in range(len(fulls)):
            mine = dsts[i].at[2 * x + y, c]
            for k in ks:
                px, py = peers[k]
                pltpu.make_async_remote_copy(
                    src_ref=mine, dst_ref=mine, send_sem=send_sems.at[3 * i + k],
                    recv_sem=recv_sems.at[3 * i + k], device_id=(px, py, c), device_id_type=MESH).start()

    return _split_start(name, [], fulls, 3 * len(fulls), issue, after=after)


def _gather_wait(name, send_sems, recv_sems, fulls, after, ks=(0, 1, 2)):
    def wait(srcs, dsts, send_sems, recv_sems):
        x, y, c, peers = _place()
        for i in range(len(fulls)):
            for k in ks:
                px, py = peers[k]
                cp = pltpu.make_async_remote_copy(
                    src_ref=dsts[i].at[2 * x + y, c], dst_ref=dsts[i].at[2 * px + py, c],
                    send_sem=send_sems.at[3 * i + k], recv_sem=recv_sems.at[3 * i + k],
                    device_id=(px, py, c), device_id_type=MESH)
                cp.wait_send()
                cp.wait_recv()

    return _split_wait(name, send_sems, recv_sems, [], fulls, after, wait)[1]


def _gather_finish(name, fulls, ks=(0, 1, 2)):
    n = len(fulls)

    def body(*refs):
        fin, fout = refs[:n], refs[n:2 * n]
        send_sems, recv_sems = refs[2 * n:]
        x, y, c, peers = _place()

        def copy(i, k, half):
            px, py = peers[k]
            return pltpu.make_async_remote_copy(
                src_ref=fin[i].at[2 * px + py, half], dst_ref=fout[i].at[2 * px + py, half],
                send_sem=send_sems.at[3 * i + k], recv_sem=recv_sems.at[3 * i + k],
                device_id=(x, y, 1 - c), device_id_type=MESH)

        sends = [copy(i, k, c) for i in range(n) for k in ks]
        for cp in sends:
            cp.start()
        for i in range(n):
            for k in ks:
                copy(i, k, 1 - c).wait_recv()
        for cp in sends:
            cp.wait_send()

    return pl.pallas_call(
        body, name=name, out_shape=[SDS(f.shape, f.dtype) for f in fulls],
        in_specs=[ANY_SPEC] * n, out_specs=[ANY_SPEC] * n, input_output_aliases={i: i for i in range(n)},
        scratch_shapes=[pltpu.SemaphoreType.DMA((3 * n,)), pltpu.SemaphoreType.DMA((3 * n,))])(*fulls)


def _reduce_exchange(name, grads):
    n = len(grads)

    def body(*refs):
        ins, theirs = refs[:n], refs[n:2 * n]
        send_sems, recv_sems = refs[2 * n:]
        x, y, c, _ = _place()
        copies = []
        for i in range(n):
            cp = pltpu.make_async_remote_copy(
                src_ref=ins[i].at[:, 1 - c], dst_ref=theirs[i], send_sem=send_sems.at[i],
                recv_sem=recv_sems.at[i], device_id=(x, y, 1 - c), device_id_type=MESH)
            cp.start()
            copies.append(cp)
        for cp in copies:
            cp.wait_recv()
            cp.wait_send()

    return pl.pallas_call(
        body, name=name, out_shape=[SDS((NSH,) + g.shape[2:], g.dtype) for g in grads],
        in_specs=[ANY_SPEC] * n, out_specs=[ANY_SPEC] * n,
        scratch_shapes=[pltpu.SemaphoreType.DMA((n,)), pltpu.SemaphoreType.DMA((n,))])(*grads)


def _reduce_start(name, parts):
    lands = [lax.empty((3,) + p.shape[1:], p.dtype) for p in parts]

    def issue(srcs, dsts, send_sems, recv_sems):
        x, y, c, peers = _place()
        for i in range(len(parts)):
            for k, (px, py) in enumerate(peers):
                pltpu.make_async_remote_copy(
                    src_ref=srcs[i].at[2 * px + py], dst_ref=dsts[i].at[k], send_sem=send_sems.at[3 * i + k],
                    recv_sem=recv_sems.at[3 * i + k], device_id=(px, py, c), device_id_type=MESH).start()

    return _split_start(name, parts, lands, 3 * len(parts), issue)


def _reduce_wait(name, send_sems, recv_sems, parts, lands, after):
    def wait(srcs, dsts, send_sems, recv_sems):
        x, y, c, peers = _place()
        for i in range(len(parts)):
            for k, (px, py) in enumerate(peers):
                cp = pltpu.make_async_remote_copy(
                    src_ref=srcs[i].at[2 * px + py], dst_ref=dsts[i].at[k], send_sem=send_sems.at[3 * i + k],
                    recv_sem=recv_sems.at[3 * i + k], device_id=(px, py, c), device_id_type=MESH)
                cp.wait_send()
                cp.wait_recv()

    return _split_wait(name, send_sems, recv_sems, parts, lands, after, wait)


def _sibling_copy(src, dst, send_sems, recv_sems, k):
    x, y, c, _ = _place()
    return pltpu.make_async_remote_copy(src_ref=src, dst_ref=dst, send_sem=send_sems.at[k], recv_sem=recv_sems.at[k],
                                        device_id=(x, y, 1 - c), device_id_type=MESH)


def _forward_start(name, fulls):
    def issue(srcs, dsts, send_sems, recv_sems):
        x, y, c, peers = _place()
        for i in range(len(fulls)):
            for k, (px, py) in enumerate(peers):
                part = dsts[i].at[2 * px + py, c]
                _sibling_copy(part, part, send_sems, recv_sems, 3 * i + k).start()

    return _split_start(name, [], fulls, 3 * len(fulls), issue, sibling_only=True)


def _forward_wait(name, send_sems, recv_sems, fulls, after):
    def wait(srcs, dsts, send_sems, recv_sems):
        x, y, c, peers = _place()
        for i in range(len(fulls)):
            for k, (px, py) in enumerate(peers):
                cp = _sibling_copy(dsts[i].at[2 * px + py, c], dsts[i].at[2 * px + py, 1 - c], send_sems, recv_sems, 3 * i + k)
                cp.wait_send()
                cp.wait_recv()

    return _split_wait(name, send_sems, recv_sems, [], fulls, after, wait)[1]


def _exchange_start(name, grads):
    lands = [lax.empty((NSH,) + g.shape[2:], g.dtype) for g in grads]

    def issue(srcs, dsts, send_sems, recv_sems):
        c = lax.axis_index("c")
        for i in range(len(grads)):
            _sibling_copy(srcs[i].at[:, 1 - c], dsts[i], send_sems, recv_sems, i).start()

    return _split_start(name, grads, lands, len(grads), issue, sibling_only=True)


def _exchange_wait(name, send_sems, recv_sems, grads, lands, after):
    def wait(srcs, dsts, send_sems, recv_sems):
        c = lax.axis_index("c")
        for i in range(len(grads)):
            cp = _sibling_copy(srcs[i].at[:, 1 - c], dsts[i], send_sems, recv_sems, i)
            cp.wait_send()
            cp.wait_recv()

    return _split_wait(name, send_sems, recv_sems, grads, lands, after, wait)


def _share_start(name, sums):
    def issue(srcs, dsts, send_sems, recv_sems):
        c = lax.axis_index("c")
        for i in range(len(sums)):
            _sibling_copy(dsts[i].at[c], dsts[i].at[c], send_sems, recv_sems, i).start()

    return _split_start(name, [], sums, len(sums), issue, sibling_only=True)


def _share_wait(name, send_sems, recv_sems, sums, after):
    def wait(srcs, dsts, send_sems, recv_sems):
        c = lax.axis_index("c")
        for i in range(len(sums)):
            cp = _sibling_copy(dsts[i].at[c], dsts[i].at[1 - c], send_sems, recv_sems, i)
            cp.wait_send()
            cp.wait_recv()

    return _split_wait(name, send_sems, recv_sems, [], sums, after, wait)[1]


def _allgather_small(v, after):
    m_per, n = v.shape

    def body(x_ref, after_ref, out_ref, send_sems, recv_sems, local_sem):
        x, y, c = lax.axis_index("x"), lax.axis_index("y"), lax.axis_index("c")
        me, sibling = (x, y, c), (x, y, 1 - c)
        chips = [(1 - x, y), (x, 1 - y), (1 - x, 1 - y)]

        def rows(px, py, pc):
            return out_ref.at[pl.ds((4 * px + 2 * py + pc) * m_per, m_per), :]

        def copy(k, block, to, src=None):
            return pltpu.make_async_remote_copy(
                src_ref=rows(*block) if src is None else src, dst_ref=rows(*block),
                send_sem=send_sems.at[k], recv_sem=recv_sems.at[k], device_id=to, device_id_type=MESH)

        mine = pltpu.make_async_copy(x_ref, rows(*me), local_sem)
        mine.start()
        first = [copy(0, me, sibling, src=x_ref)]
        first += [copy(1 + j, me, (*chip, c), src=x_ref) for j, chip in enumerate(chips)]
        for cp in first:
            cp.start()
        passed = [copy(4 + j, (*chip, c), sibling) for j, chip in enumerate(chips)]
        for j, chip in enumerate(chips):
            copy(1 + j, (*chip, c), me).wait_recv()
            passed[j].start()
        copy(0, sibling, me).wait_recv()
        for j, chip in enumerate(chips):
            copy(4 + j, (*chip, 1 - c), me).wait_recv()
        for cp in first + passed:
            cp.wait_send()
        mine.wait()

    return pl.pallas_call(
        body, name="allgather_small", out_shape=SDS((8 * m_per, n), v.dtype),
        in_specs=[pl.BlockSpec(memory_space=pltpu.VMEM), ANY_SPEC], out_specs=pl.BlockSpec(memory_space=pltpu.VMEM),
        scratch_shapes=[pltpu.SemaphoreType.DMA((7,)), pltpu.SemaphoreType.DMA((7,)), pltpu.SemaphoreType.DMA])(v, after)


def _norm_in_proj_own(x, g, w_full, place):
    tn, chunk = 512, 256
    per = (DIN // NSH) // tn

    def body(place_ref, x_ref, g_ref, w_ref, proj_ref, xn_ref):
        @pl.when(pl.program_id(0) == 0)
        def _():
            def norm(r, carry):
                rows = pl.ds(pl.multiple_of(r * chunk, chunk), chunk)
                xv = x_ref[rows, :]
                rs = lax.rsqrt(jnp.mean(xv * xv, axis=-1, keepdims=True) + EPS)
                xn_ref[rows, :] = (xv * rs * g_ref[...]).astype(BF16)
                return carry

            lax.fori_loop(0, T // chunk, norm, 0)

        proj_ref[...] = _dot(xn_ref[...], w_ref[...])

    return pl.pallas_call(
        body, name="norm_in_proj_own", out_shape=[SDS((T, DIN), F32), SDS((T, D), BF16)],
        grid_spec=pltpu.PrefetchScalarGridSpec(
            num_scalar_prefetch=1, grid=(per,),
            in_specs=[_resident((T, D), lambda j, p: (0, 0)),
                      pl.BlockSpec((1, D), lambda j, p: (0, 0)),
                      pl.BlockSpec((None, D, tn), lambda j, p: (p[0], 0, j))],
            out_specs=[pl.BlockSpec((T, tn), lambda j, p: (0, p[0] * per + j)),
                       pl.BlockSpec((T, D), lambda j, p: (0, 0))]),
        compiler_params=_params(1))(place, x, g, w_full)


def _in_proj_rest(name, xn, w_full, proj, place, flips):
    tn = 512
    per = (DIN // NSH) // tn

    def body(place_ref, xn_ref, w_ref, proj_in, proj_ref):
        proj_ref[...] = _dot(xn_ref[...], w_ref[...])

    def shard(j, p):
        flip = flips[0]
        for n, f in enumerate(flips[1:]):
            flip = jnp.where(j // per == n + 1, f, flip)
        return p[0] ^ flip

    return pl.pallas_call(
        body, name=name, out_shape=SDS((T, DIN), F32),
        grid_spec=pltpu.PrefetchScalarGridSpec(
            num_scalar_prefetch=1, grid=(len(flips) * per,),
            in_specs=[_resident((T, D), lambda j, p: (0, 0)),
                      pl.BlockSpec((None, D, tn), lambda j, p: (shard(j, p), 0, j % per)), ANY_SPEC],
            out_specs=pl.BlockSpec((T, tn), lambda j, p: (0, shard(j, p) * per + j % per))),
        input_output_aliases={3: 0}, compiler_params=_params(1))(place, xn, w_full, proj)


def _rope_tables():
    pos = np.arange(T, dtype=np.float32)
    inv = (10000.0 ** (-np.arange(0, HD, 2, dtype=np.float32) / HD)).astype(np.float32)
    ang = (pos[:, None] * inv[None, :]).astype(np.float32)
    cos, sin = np.cos(ang).astype(np.float32), np.sin(ang).astype(np.float32)
    return (jnp.asarray(np.concatenate([cos, cos], axis=1)), jnp.asarray(np.concatenate([-sin, sin], axis=1)))


def _qk_prep(proj, nw, cos, sin):
    tm = 256

    def body(p_ref, w_ref, cos_ref, sin_ref, o_ref):
        cv, sv = cos_ref[...], sin_ref[...]
        for h in range(NH):
            sl = slice(h * HD, (h + 1) * HD)
            xv = p_ref[:, sl]
            r = lax.rsqrt(jnp.mean(xv * xv, axis=-1, keepdims=True) + EPS)
            z = xv * r * w_ref[:, sl]
            if h < NHA:
                z = z * cv + pltpu.roll(z, 64, 1) * sv
            o_ref[:, sl] = z.astype(BF16)

    return pl.pallas_call(
        body, name="qk_prep", out_shape=SDS((T, 2 * D), BF16), grid=(T // tm, 2),
        in_specs=[pl.BlockSpec((tm, D), lambda i, j: (i, j)),
                  pl.BlockSpec((None, 1, D), lambda i, j: (j, 0, 0)),
                  pl.BlockSpec((tm, HD), lambda i, j: (i, 0)),
                  pl.BlockSpec((tm, HD), lambda i, j: (i, 0))],
        out_specs=pl.BlockSpec((tm, D), lambda i, j: (i, j)),
        compiler_params=_params(2))(proj, nw, cos, sin)


def _band_mask(q0, m):
    ii = lax.broadcasted_iota(jnp.int32, (128, 256), 0)
    jj = lax.broadcasted_iota(jnp.int32, (128, 256), 1)
    rel = jj - ii
    kpos = jj + (q0 - 64)
    return (rel >= 0) & (rel <= 128) & (kpos >= 0) & (kpos < m)


def _fill_padded(dst, src, m):
    zeros = jnp.zeros((64, HD), dst.dtype)
    dst[0:64, :] = zeros
    dst[64 + m:128 + m, :] = zeros
    dst[64:64 + m, :] = src.astype(dst.dtype)


def _residue_rows(r, m, dil):
    return pl.ds(r, m, stride=dil) if dil > 1 else slice(None)


def _head_blocks(g):
    col = lambda base: pl.BlockSpec((T, HD), lambda h: (0, base + g * 4 + h))
    return col(0), col(NH), col(2 * NH), pl.BlockSpec((T, HD), lambda h: (0, h))


def _attn_a_fwd(qkn, proj, g):
    dil = DILS[g]
    m = T // dil
    nb = m // 128

    def body(q_ref, k_ref, v_ref, o_ref, l_ref, qf, kf, qp, kp, vp, ob, lb):
        qf[...] = q_ref[...].astype(F32)
        kf[...] = k_ref[...].astype(F32)
        for r in range(dil):
            rows = _residue_rows(r, m, dil)
            qp[...] = qf[rows, :].astype(BF16)
            _fill_padded(kp, kf[rows, :], m)
            _fill_padded(vp, v_ref[rows, :], m)

            def block(b, carry):
                q0 = pl.multiple_of(b * 128, 128)
                kw, vw = kp[pl.ds(q0, 256), :], vp[pl.ds(q0, 256), :]
                s = _dot_nt(qp[pl.ds(q0, 128), :], kw) * SCALE
                s = jnp.where(_band_mask(q0, m), s, NEG)
                mx = jnp.max(s, axis=-1, keepdims=True)
                e = jnp.exp(s - mx)
                den = jnp.sum(e, axis=-1, keepdims=True)
                ob[pl.ds(q0, 128), :] = _dot((e / den).astype(BF16), vw)
                lb[pl.ds(q0, 128), :] = jnp.broadcast_to(mx + jnp.log(den), (128, HD))
                return carry

            lax.fori_loop(0, nb, block, 0, unroll=min(nb, 2))
            o_ref[rows, :] = ob[...]
            l_ref[rows, :] = lb[...]

    q_blk, k_blk, v_blk, out_blk = _head_blocks(g)
    return pl.pallas_call(
        body, name=f"attn_a_fwd_{g}", out_shape=[SDS((T, 512), F32)] * 2, grid=(4,),
        in_specs=[q_blk, k_blk, v_blk], out_specs=[out_blk] * 2,
        scratch_shapes=[pltpu.VMEM((T, HD), F32), pltpu.VMEM((T, HD), F32), pltpu.VMEM((m, HD), BF16),
                        pltpu.VMEM((m + 128, HD), BF16), pltpu.VMEM((m + 128, HD), BF16),
                        pltpu.VMEM((m, HD), F32), pltpu.VMEM((m, HD), F32)],
        compiler_params=_params(1))(qkn, qkn, proj)


def _nbr_window(r):
    start = jnp.clip(r - WIN_R // 2, 0, T // GRID_W - WIN_R)
    return start, start - r + (WIN_R - 1)


def _rpb_rows(rpb):
    zeros = jnp.zeros((4, 14, 33), F32)
    a, b = rpb[:, :14], rpb[:, 1:15]
    rows = jnp.concatenate([a[:, :, 15:31], zeros, b, zeros, a[:, :, 0:15]], axis=2)
    return jnp.pad(rows, ((0, 0), (0, 2), (0, 0)))


def _attn_b_fwd(qkn, proj, rpb_rows):
    def body(r_ref, q_ref, k_ref, v_ref, o_ref, l_ref, bias_ref, vb, pair):
        qc = lax.broadcasted_iota(jnp.int32, (GRID_W, 512), 0)
        kc = lax.broadcasted_iota(jnp.int32, (GRID_W, 512), 1) & (GRID_W - 1)
        cs = jnp.clip(qc - WIN_C // 2, 0, GRID_W - WIN_C)
        colmask = (kc >= cs) & (kc < cs + WIN_C)
        for d in range(14):
            pair[d] = pltpu.roll(jnp.broadcast_to(r_ref[d:d + 1, :], (GRID_W, HD)), 0, 1, stride=1, stride_axis=0)
        for off in range(8):
            rows = jnp.concatenate([pair[off + 2 * jj] for jj in range(4)], axis=1)
            bias_ref[off] = jnp.where(colmask, rows, NEG)
        vb[...] = v_ref[...].astype(BF16)

        def row(r, carry):
            start, off = _nbr_window(r)
            q0 = pl.multiple_of(r * GRID_W, GRID_W)
            k0 = pl.multiple_of(start * GRID_W, GRID_W)
            s = _dot_nt(q_ref[pl.ds(q0, GRID_W), :], k_ref[pl.ds(k0, 512), :]) * SCALE + bias_ref[off]
            mx = jnp.max(s, axis=-1, keepdims=True)
            e = jnp.exp(s - mx)
            den = jnp.sum(e, axis=-1, keepdims=True)
            o_ref[pl.ds(q0, GRID_W), :] = _dot((e / den).astype(BF16), vb[pl.ds(k0, 512), :])
            l_ref[pl.ds(q0, GRID_W), :] = jnp.broadcast_to(mx + jnp.log(den), (GRID_W, HD))
            return carry

        lax.fori_loop(0, T // GRID_W, row, 0, unroll=2)

    return pl.pallas_call(
        body, name="attn_b_fwd",
        out_shape=[SDS((T, 512), F32), SDS((T, 512), F32), SDS((4, 8, GRID_W, 512), F32)], grid=(4,),
        in_specs=[pl.BlockSpec((None, 16, HD), lambda h: (h, 0, 0)),
                  pl.BlockSpec((T, HD), lambda h: (0, NHA + h)),
                  pl.BlockSpec((T, HD), lambda h: (0, NH + NHA + h)),
                  pl.BlockSpec((T, HD), lambda h: (0, 2 * NH + NHA + h))],
        out_specs=[pl.BlockSpec((T, HD), lambda h: (0, h)), pl.BlockSpec((T, HD), lambda h: (0, h)),
                   pl.BlockSpec((None, 8, GRID_W, 512), lambda h: (h, 0, 0, 0))],
        scratch_shapes=[pltpu.VMEM((T, HD), BF16), pltpu.VMEM((14, GRID_W, HD), F32)],
        compiler_params=_params(1))(rpb_rows, qkn, qkn, proj)


def _comb_fwd(os, ls):
    tm = 512

    def body(o0, o1, o2, l0, l1, l2, oa_ref, w0, w1, w2):
        lv = [l0[...], l1[...], l2[...]]
        mx = jnp.maximum(jnp.maximum(lv[0], lv[1]), lv[2])
        ev = [jnp.exp(l - mx) for l in lv]
        den = ev[0] + ev[1] + ev[2]
        wv = [e / den for e in ev]
        oa_ref[...] = (wv[0] * o0[...] + wv[1] * o1[...] + wv[2] * o2[...]).astype(BF16)
        w0[...], w1[...], w2[...] = wv

    spec = pl.BlockSpec((tm, 512), lambda i: (i, 0))
    return pl.pallas_call(
        body, name="comb_fwd", out_shape=[SDS((T, 512), BF16)] + [SDS((T, 512), F32)] * 3, grid=(T // tm,),
        in_specs=[spec] * 6, out_specs=[spec] * 4, compiler_params=_params(1))(*os, *ls)


def _mix_fwd(oa, ob, proj, b_gate, wpa, wpb):
    tm = 512

    def body(oa_ref, ob_ref, ga_ref, gb_ref, ba_ref, bb_ref, wpa_ref, wpb_ref, mixed_ref, ob16_ref):
        oav = oa_ref[...]
        obv = ob_ref[...].astype(BF16)
        ob16_ref[...] = obv
        for s in range(NSH):
            sl = slice(s * 512, (s + 1) * 512)
            ga = _sigmoid(ga_ref[:, sl] + ba_ref[:, sl])
            gb = _sigmoid(gb_ref[:, sl] + bb_ref[:, sl])
            mixed_ref[:, sl] = (ga * _dot(oav, wpa_ref[s]) + gb * _dot(obv, wpb_ref[s])).astype(BF16)

    row = lambda w: pl.BlockSpec((tm, w), lambda i: (i, 0))
    return pl.pallas_call(
        body, name="mix_fwd", out_shape=[SDS((T, D), BF16), SDS((T, 512), BF16)], grid=(T // tm,),
        in_specs=[row(512), row(512),
                  pl.BlockSpec((tm, D), lambda i: (i, 3)), pl.BlockSpec((tm, D), lambda i: (i, 4)),
                  pl.BlockSpec((1, D), lambda i: (0, 0)), pl.BlockSpec((1, D), lambda i: (0, 1)),
                  _resident((NSH, 512, 512), lambda i: (0, 0, 0)), _resident((NSH, 512, 512), lambda i: (0, 0, 0))],
        out_specs=[row(D), row(512)], compiler_params=_params(1))(oa, ob, proj, proj, b_gate, b_gate, wpa, wpb)


def _out_proj_fwd(mixed, w_out, x, g):
    tm = 512

    def body(m_ref, w_ref, x_ref, g_ref, h1_ref, hn_ref):
        h1 = x_ref[...] + _dot(m_ref[...], w_ref[...])
        h1_ref[...] = h1
        r = lax.rsqrt(jnp.mean(h1 * h1, axis=-1, keepdims=True) + EPS)
        hn_ref[...] = (h1 * r * g_ref[...]).astype(BF16)

    row = pl.BlockSpec((tm, D), lambda i: (i, 0))
    return pl.pallas_call(
        body, name="out_proj_fwd", out_shape=[SDS((T, D), F32), SDS((T, D), BF16)], grid=(T // tm,),
        in_specs=[row, _resident((D, D), lambda i: (0, 0)), row, pl.BlockSpec((1, D), lambda i: (0, 0))],
        out_specs=[row, row], compiler_params=_params(1))(mixed, w_out, x, g)


def _ffn_up(hn, w_up):
    tm, tn = T, 512
    per = (DFF // NSH) // tn

    def body(h_ref, w_ref, a_ref, u_ref):
        uv = jnp.maximum(_dot(h_ref[...], w_ref[...]), 0.0)
        a_ref[...] = (uv * uv).astype(BF16)
        u_ref[...] = uv.astype(BF16)

    out = pl.BlockSpec((tm, tn), lambda i, j: (i, j))
    return pl.pallas_call(
        body, name="ffn_up", out_shape=[SDS((T, DFF), BF16)] * 2, grid=(T // tm, DFF // tn),
        in_specs=[pl.BlockSpec((tm, D), lambda i, j: (i, 0)),
                  pl.BlockSpec((None, D, tn), lambda i, j: (j // per, 0, j % per))],
        out_specs=[out, out], compiler_params=_params(2))(hn, w_up)


def _ffn_down_loss(u, w_down, h1, target):
    tm, tk = 512, 2048
    nk = DFF // tk

    def body(u_ref, w_ref, h1_ref, t_ref, dy_ref, dy16_ref, loss_ref, acc):
        k = pl.program_id(1)

        @pl.when(k == 0)
        def _():
            acc[...] = jnp.zeros_like(acc)

        acc[...] += _dot(u_ref[...], w_ref[...])

        @pl.when(k == nk - 1)
        def _():
            def chunk(r, sq):
                rows = pl.ds(pl.multiple_of(r * 16, 16), 16)
                err = acc[rows, :] + h1_ref[rows, :] - t_ref[rows, :]
                dy = err * (1.0 / D)
                dy_ref[rows, :] = dy
                dy16_ref[rows, :] = dy.astype(BF16)
                return sq + err * err

            sq = lax.fori_loop(0, tm // 16, chunk, jnp.zeros((16, D), F32), unroll=2)
            part = 0.5 * jnp.sum(jnp.mean(sq, axis=-1, keepdims=True), axis=0, keepdims=True)
            loss_ref[...] = jnp.broadcast_to(part, (8, 128))

    row = pl.BlockSpec((tm, D), lambda i, k: (i, 0))
    once = _resident((tm, D), lambda i, k: (i, 0))
    return pl.pallas_call(
        body, name="ffn_down_loss",
        out_shape=[SDS((T, D), F32), SDS((T, D), BF16), SDS((T // tm, 8, 128), F32)], grid=(T // tm, nk),
        in_specs=[pl.BlockSpec((tm, tk), lambda i, k: (i, k)), pl.BlockSpec((tk, D), lambda i, k: (k, 0)), once, once],
        out_specs=[row, row, pl.BlockSpec((None, 8, 128), lambda i, k: (i, 0, 0))],
        scratch_shapes=[pltpu.VMEM((tm, D), F32)], compiler_params=_params(2))(u, w_down, h1, target)


def _ffn_down_bwd(dy16, w_down, u, deps=()):
    tm, tn = T, 512

    def body(dy_ref, w_ref, u_ref, du_ref):
        uv = u_ref[...].astype(F32)
        du_ref[...] = jnp.where(uv > 0.0, 2.0 * uv * _dot_nt(dy_ref[...], w_ref[...]), 0.0).astype(BF16)

    return pl.pallas_call(
        _after(body, deps), name="ffn_down_bwd", out_shape=SDS((T, DFF), BF16), grid=(T // tm, DFF // tn),
        in_specs=[DEP_SPEC] * len(deps) + [
            pl.BlockSpec((tm, D), lambda i, j: (i, 0)), pl.BlockSpec((tn, D), lambda i, j: (j, 0)),
            pl.BlockSpec((tm, tn), lambda i, j: (i, j))],
        out_specs=pl.BlockSpec((tm, tn), lambda i, j: (i, j)), compiler_params=_params(2))(*deps, dy16, w_down, u)


def _norm_bwd(xv, dz_in, g):
    r = lax.rsqrt(jnp.mean(xv * xv, axis=-1, keepdims=True) + EPS)
    dg = jnp.sum(xv * r * dz_in, axis=0, keepdims=True)
    dz = dz_in * g
    dx = r * dz - xv * (r * r * r) * jnp.mean(xv * dz, axis=-1, keepdims=True)
    return dx, dg


def _ffn_up_bwd(du, w_up, h1, dy, g, deps=()):
    tm, tk = 512, 1024
    per = (DFF // NSH) // tk
    nk = DFF // tk

    def body(du_ref, w_ref, h1_ref, dy_ref, g_ref, dh1_ref, dh16_ref, dg_ref, acc):
        i, k = pl.program_id(0), pl.program_id(1)

        @pl.when(k == 0)
        def _():
            acc[...] = jnp.zeros_like(acc)

        @pl.when((k == 0) & (i == 0))
        def _():
            dg_ref[...] = jnp.zeros_like(dg_ref)

        acc[...] += _dot_nt(du_ref[...], w_ref[...])

        @pl.when(k == nk - 1)
        def _():
            dx, dg = _norm_bwd(h1_ref[...], acc[...], g_ref[...])
            dh1 = dy_ref[...] + dx
            dh1_ref[...] = dh1
            dh16_ref[...] = dh1.astype(BF16)
            dg_ref[...] += dg

    row = pl.BlockSpec((tm, D), lambda i, k: (i, 0))
    vec = pl.BlockSpec((1, D), lambda i, k: (0, 0))
    return pl.pallas_call(
        _after(body, deps), name="ffn_up_bwd", out_shape=[SDS((T, D), F32), SDS((T, D), BF16), SDS((1, D), F32)],
        grid=(T // tm, nk),
        in_specs=[DEP_SPEC] * len(deps) + [
            pl.BlockSpec((tm, tk), lambda i, k: (i, k)),
            pl.BlockSpec((None, D, tk), lambda i, k: (k // per, 0, k % per)), row, row, vec],
        out_specs=[row, row, vec], scratch_shapes=[pltpu.VMEM((tm, D), F32)],
        compiler_params=_params(2))(*deps, du, w_up, h1, dy, g)


def _mix_bwd(dh16, w_out, oa, ob16, proj, b_gate, wpa, wpb):
    tm = 256

    def body(dh_ref, wo_ref, oa_ref, ob_ref, ga_ref, gb_ref, ba_ref, bb_ref, wpa_ref, wpb_ref,
             dya_ref, dyb_ref, dga_ref, dgb_ref, doa_ref, dob_ref, dba_ref, dbb_ref):
        @pl.when(pl.program_id(0) == 0)
        def _():
            dba_ref[...] = jnp.zeros_like(dba_ref)
            dbb_ref[...] = jnp.zeros_like(dbb_ref)

        oav, obv = oa_ref[...], ob_ref[...]
        doa = jnp.zeros((tm, 512), F32)
        dob = jnp.zeros((tm, 512), F32)
        for s in range(NSH):
            sl = slice(s * 512, (s + 1) * 512)
            dm = _dot_nt(dh_ref[...], wo_ref[sl, :])
            ga = _sigmoid(ga_ref[:, sl] + ba_ref[:, sl])
            gb = _sigmoid(gb_ref[:, sl] + bb_ref[:, sl])
            dya = (dm * ga).astype(BF16)
            dyb = (dm * gb).astype(BF16)
            dza = dm * _dot(oav, wpa_ref[s]) * ga * (1.0 - ga)
            dzb = dm * _dot(obv, wpb_ref[s]) * gb * (1.0 - gb)
            dya_ref[:, sl], dyb_ref[:, sl] = dya, dyb
            dga_ref[:, sl], dgb_ref[:, sl] = dza.astype(BF16), dzb.astype(BF16)
            dba_ref[:, sl] += jnp.sum(dza, axis=0, keepdims=True)
            dbb_ref[:, sl] += jnp.sum(dzb, axis=0, keepdims=True)
            doa += _dot_nt(dya, wpa_ref[s])
            dob += _dot_nt(dyb, wpb_ref[s])
        doa_ref[...], dob_ref[...] = doa, dob

    row = lambda w: pl.BlockSpec((tm, w), lambda i: (i, 0))
    vec = pl.BlockSpec((1, D), lambda i: (0, 0))
    wp = _resident((NSH, 512, 512), lambda i: (0, 0, 0))
    return pl.pallas_call(
        body, name="mix_bwd",
        out_shape=[SDS((T, D), BF16)] * 4 + [SDS((T, 512), F32)] * 2 + [SDS((1, D), F32)] * 2, grid=(T // tm,),
        in_specs=[row(D), _resident((D, D), lambda i: (0, 0)), row(512), row(512),
                  pl.BlockSpec((tm, D), lambda i: (i, 3)), pl.BlockSpec((tm, D), lambda i: (i, 4)),
                  pl.BlockSpec((1, D), lambda i: (0, 0)), pl.BlockSpec((1, D), lambda i: (0, 1)), wp, wp],
        out_specs=[row(D)] * 4 + [row(512)] * 2 + [vec] * 2,
        compiler_params=_params(1))(dh16, w_out, oa, ob16, proj, proj, b_gate, b_gate, wpa, wpb)


def _comb_bwd(doa, os, ws, deps=()):
    tm = 512

    def body(d_ref, o0, o1, o2, w0, w1, w2, cc_ref):
        prod = d_ref[...] * (w0[...] * o0[...] + w1[...] * o1[...] + w2[...] * o2[...])
        for h in range(4):
            sl = slice(h * HD, (h + 1) * HD)
            cc_ref[:, sl] = jnp.broadcast_to(jnp.sum(prod[:, sl], axis=-1, keepdims=True), (tm, HD))

    spec = pl.BlockSpec((tm, 512), lambda i: (i, 0))
    return pl.pallas_call(
        _after(body, deps), name="comb_bwd", out_shape=SDS((T, 512), F32), grid=(T // tm,),
        in_specs=[DEP_SPEC] * len(deps) + [spec] * 7, out_specs=spec,
        compiler_params=_params(1))(*deps, doa, *os, *ws)


def _attn_a_bwd(qkn, proj, doa, lse, w, cc, g):
    dil = DILS[g]
    m = T // dil
    nb = m // 128

    def body(q_ref, k_ref, v_ref, d_ref, l_ref, w_ref, c_ref, dqk_ref, dv_ref,
             qf, kf, qp, kp, vp, dp, lp, wsub, cp, dqb, dkp, dvp):
        qf[...] = q_ref[...].astype(F32)
        kf[...] = k_ref[...].astype(F32)
        for r in range(dil):
            sub = _residue_rows(r, m, dil)
            qp[...] = qf[sub, :].astype(BF16)
            _fill_padded(kp, kf[sub, :], m)
            _fill_padded(vp, v_ref[sub, :], m)
            dp[...] = d_ref[sub, :].astype(BF16)
            lp[...], wsub[...], cp[...] = l_ref[sub, :], w_ref[sub, :], c_ref[sub, :]
            dkp[...] = jnp.zeros_like(dkp)
            dvp[...] = jnp.zeros_like(dvp)

            def block(b, carry):
                q0 = pl.multiple_of(b * 128, 128)
                rows = pl.ds(q0, 128)
                win = pl.ds(q0, 256)
                qb, kw, vw = qp[rows, :], kp[win, :], vp[win, :]
                s = _dot_nt(qb, kw) * SCALE
                s = jnp.where(_band_mask(q0, m), s, NEG)
                wp = _wide(wsub[rows, :], 2) * jnp.exp(s - _wide(lp[rows, :], 2))
                dob = dp[rows, :]
                ds = (wp * (_dot_nt(dob, vw) - _wide(cp[rows, :], 2))).astype(BF16)
                dqb[rows, :] = _dot(ds, kw) * SCALE
                dkp[win, :] += _dot_tn(ds, qb) * SCALE
                dvp[win, :] += _dot_tn(wp.astype(BF16), dob)
                return carry

            lax.fori_loop(0, nb, block, 0, unroll=min(nb, 2))
            dqk_ref.at[0][sub, :] = dqb[...]
            dqk_ref.at[1][sub, :] = dkp[64:64 + m, :]
            dv_ref[sub, :] = dvp[64:64 + m, :]

    q_blk, k_blk, v_blk, blk = _head_blocks(g)
    sub16 = pltpu.VMEM((m, HD), BF16)
    sub32 = pltpu.VMEM((m, HD), F32)
    return pl.pallas_call(
        body, name=f"attn_a_bwd_{g}", out_shape=[SDS((2, T, 512), F32), SDS((T, 512), F32)], grid=(4,),
        in_specs=[q_blk, k_blk, v_blk, blk, blk, blk, blk],
        out_specs=[pl.BlockSpec((2, T, HD), lambda h: (0, 0, h)), blk],
        scratch_shapes=[pltpu.VMEM((T, HD), F32), pltpu.VMEM((T, HD), F32), sub16,
                        pltpu.VMEM((m + 128, HD), BF16), pltpu.VMEM((m + 128, HD), BF16), sub16,
                        sub32, sub32, sub32, sub32,
                        pltpu.VMEM((m + 128, HD), F32), pltpu.VMEM((m + 128, HD), F32)],
        compiler_params=_params(1))(qkn, qkn, proj, doa, lse, w, cc)


def _attn_b_bwd(qkn, proj, dob, ob, lse, bias, deps=()):
    def body(q_ref, k_ref, v_ref, d_ref, o_ref, l_ref, bias_ref, dqk_ref, dv_ref, drpb_ref, vb, dk_acc, dv_acc, a_acc):
        vb[...] = v_ref[...].astype(BF16)
        dk_acc[...] = jnp.zeros_like(dk_acc)
        dv_acc[...] = jnp.zeros_like(dv_acc)
        a_acc[...] = jnp.zeros_like(a_acc)

        def row(r, carry):
            start, off = _nbr_window(r)
            rows = pl.ds(pl.multiple_of(r * GRID_W, GRID_W), GRID_W)
            win = pl.ds(pl.multiple_of(start * GRID_W, GRID_W), 512)
            qr, kw, vw = q_ref[rows, :], k_ref[win, :], vb[win, :]
            s = _dot_nt(qr, kw) * SCALE + bias_ref[off]
            p = jnp.exp(s - _wide(l_ref[rows, :], 4))
            dov = d_ref[rows, :]
            delta = jnp.sum(dov * o_ref[rows, :], axis=-1, keepdims=True)
            do16 = dov.astype(BF16)
            ds = p * (_dot_nt(do16, vw) - delta)
            a_acc[off] += ds
            ds16 = ds.astype(BF16)
            dqk_ref[0, rows, :] = _dot(ds16, kw) * SCALE
            dk_acc[win, :] += _dot_tn(ds16, qr) * SCALE
            dv_acc[win, :] += _dot_tn(p.astype(BF16), do16)
            return carry

        lax.fori_loop(0, T // GRID_W, row, 0, unroll=2)
        dqk_ref[1] = dk_acc[...]
        dv_ref[...] = dv_acc[...]

        lane = lax.broadcasted_iota(jnp.int32, (16, HD), 1)
        rowi = lax.broadcasted_iota(jnp.int32, (16, HD), 0)
        low = (lane >= GRID_W - WIN_C) & (lane < GRID_W + WIN_C - 1)
        high = (lane >= HD - WIN_C) | (lane < WIN_C - 1)
        flip = (lax.broadcasted_iota(jnp.int32, (GRID_W, GRID_W), 0)
                + lax.broadcasted_iota(jnp.int32, (GRID_W, GRID_W), 1) == GRID_W - 1).astype(BF16)
        out = jnp.zeros((16, HD), F32)
        for d in range(14):
            acc = None
            for off in range(8):
                if 0 <= d - off <= 6 and (d - off) % 2 == 0:
                    jj = (d - off) // 2
                    piece = a_acc[off, :, jj * HD:(jj + 1) * HD]
                    acc = piece if acc is None else acc + piece
            hi = acc.astype(BF16)
            lo = (acc - hi.astype(F32)).astype(BF16)
            rev = _dot(flip, hi) + _dot(flip, lo)
            v = jnp.sum(pltpu.roll(rev, 0, 1, stride=1, stride_axis=0), axis=0, keepdims=True)
            v = jnp.broadcast_to(v, (16, HD))
            out = out + jnp.where((rowi == d) & low, v, 0.0)
            out = out + jnp.where(rowi == d + 1, pltpu.roll(jnp.where(high, v, 0.0), GRID_W, 1), 0.0)
        drpb_ref[...] = out

    blk = pl.BlockSpec((T, HD), lambda h: (0, h))
    return pl.pallas_call(
        _after(body, deps), name="attn_b_bwd",
        out_shape=[SDS((2, T, 512), F32), SDS((T, 512), F32), SDS((4, 16, HD), F32)], grid=(4,),
        in_specs=[DEP_SPEC] * len(deps) + [
            pl.BlockSpec((T, HD), lambda h: (0, NHA + h)),
            pl.BlockSpec((T, HD), lambda h: (0, NH + NHA + h)),
            pl.BlockSpec((T, HD), lambda h: (0, 2 * NH + NHA + h)), blk, blk, blk,
            pl.BlockSpec((None, 8, GRID_W, 512), lambda h: (h, 0, 0, 0))],
        out_specs=[pl.BlockSpec((2, T, HD), lambda h: (0, 0, h)), blk,
                   pl.BlockSpec((None, 16, HD), lambda h: (h, 0, 0))],
        scratch_shapes=[pltpu.VMEM((T, HD), BF16), pltpu.VMEM((T, HD), F32), pltpu.VMEM((T, HD), F32),
                        pltpu.VMEM((8, GRID_W, 512), F32)],
        compiler_params=_params(1))(*deps, qkn, qkn, proj, dob, ob, lse, bias)


def _qk_bwd(proj, nw, cos, sin, dqk_groups, dqk_b, dvs, dga, dgb):
    tm = 256

    def body(p_ref, w_ref, cos_ref, sin_ref, d0, d1, d2, d3, v0, v1, v2, v3, ga_ref, gb_ref, o_ref, dn_ref):
        j, i = pl.program_id(0), pl.program_id(1)

        @pl.when((j < 2) & (i == 0))
        def _():
            dn_ref[...] = jnp.zeros_like(dn_ref)

        @pl.when(j < 2)
        def _():
            cv, sv = cos_ref[...], sin_ref[...]
            srcs = (d0, d1, d2, d3)
            dna = jnp.zeros((1, HD), F32)
            dnb = jnp.zeros((1, HD), F32)
            for h in range(NH):
                sl = slice(h * HD, (h + 1) * HD)
                dz = srcs[h // 4][:, (h % 4) * HD:(h % 4 + 1) * HD]
                if h < NHA:
                    dz = dz * cv + pltpu.roll(dz * sv, 64, 1)
                dx, dg = _norm_bwd(p_ref[:, sl], dz, w_ref[:, sl])
                o_ref[:, sl] = dx.astype(BF16)
                if h < NHA:
                    dna += dg
                else:
                    dnb += dg
            dn_ref[0:1, :] += dna
            dn_ref[1:2, :] += dnb

        @pl.when(j == 2)
        def _():
            for s, v_ref in enumerate((v0, v1, v2, v3)):
                o_ref[:, s * 512:(s + 1) * 512] = v_ref[...].astype(BF16)

        @pl.when(j == 3)
        def _():
            o_ref[...] = ga_ref[...]

        @pl.when(j == 4)
        def _():
            o_ref[...] = gb_ref[...]

    def rows(used):
        return lambda j, i: (jnp.where(used(j), i, 0), 0)

    qk = lambda j: j < 2
    dspec = pl.BlockSpec((None, tm, 512), lambda j, i: (jnp.minimum(j, 1), jnp.where(j < 2, i, 0), 0))
    vspec = pl.BlockSpec((tm, 512), rows(lambda j: j == 2))
    return pl.pallas_call(
        body, name="qk_bwd", out_shape=[SDS((T, DIN), BF16), SDS((2, 8, HD), F32)], grid=(5, T // tm),
        in_specs=[pl.BlockSpec((tm, D), lambda j, i: (jnp.where(j < 2, i, 0), jnp.minimum(j, 1))),
                  pl.BlockSpec((None, 1, D), lambda j, i: (jnp.minimum(j, 1), 0, 0)),
                  pl.BlockSpec((tm, HD), rows(qk)), pl.BlockSpec((tm, HD), rows(qk)),
                  dspec, dspec, dspec, dspec, vspec, vspec, vspec, vspec,
                  pl.BlockSpec((tm, D), rows(lambda j: j == 3)), pl.BlockSpec((tm, D), rows(lambda j: j == 4))],
        out_specs=[pl.BlockSpec((tm, D), lambda j, i: (i, j)),
                   pl.BlockSpec((None, 8, HD), lambda j, i: (jnp.minimum(j, 1), 0, 0))],
        compiler_params=_params(2))(proj, nw, cos, sin, *dqk_groups, dqk_b, *dvs, dga, dgb)


def _in_proj_bwd(dproj, w_in, x, dh1, g, deps=()):
    tm, tk = 512, 1280
    per = (DIN // NSH) // tk
    nk = DIN // tk

    def body(dp_ref, w_ref, x_ref, dh_ref, g_ref, dx_ref, dg_ref, acc):
        i, k = pl.program_id(0), pl.program_id(1)

        @pl.when(k == 0)
        def _():
            acc[...] = jnp.zeros_like(acc)

        @pl.when((k == 0) & (i == 0))
        def _():
            dg_ref[...] = jnp.zeros_like(dg_ref)

        acc[...] += _dot_nt(dp_ref[...], w_ref[...])

        @pl.when(k == nk - 1)
        def _():
            dx, dg = _norm_bwd(x_ref[...], acc[...], g_ref[...])
            dx_ref[...] = dh_ref[...] + dx
            dg_ref[...] += dg

    row = pl.BlockSpec((tm, D), lambda i, k: (i, 0))
    vec = pl.BlockSpec((1, D), lambda i, k: (0, 0))
    return pl.pallas_call(
        _after(body, deps), name="in_proj_bwd", out_shape=[SDS((T, D), F32), SDS((1, D), F32)], grid=(T // tm, nk),
        in_specs=[DEP_SPEC] * len(deps) + [
            pl.BlockSpec((tm, tk), lambda i, k: (i, k)),
            pl.BlockSpec((None, D, tk), lambda i, k: (k // per, 0, k % per)), row, row, vec],
        out_specs=[row, vec], scratch_shapes=[pltpu.VMEM((tm, D), F32)],
        compiler_params=_params(2))(*deps, dproj, w_in, x, dh1, g)


def _grad_w(name, a, g, shard_rows, rows, cols, tr, tc):
    ni, nj = rows // tr, cols // tc
    if shard_rows:
        a_map, g_map = (lambda s, i, j: (0, s * ni + i)), (lambda s, i, j: (0, j))
    else:
        a_map, g_map = (lambda s, i, j: (0, i)), (lambda s, i, j: (0, s * nj + j))

    def body(a_ref, g_ref, o_ref):
        o_ref[...] = _dot_tn(a_ref[...], g_ref[...]).astype(BF16)

    return pl.pallas_call(
        body, name=name, out_shape=SDS((NSH, rows, cols), BF16), grid=(NSH, ni, nj),
        in_specs=[pl.BlockSpec((T, tr), a_map), pl.BlockSpec((T, tc), g_map)],
        out_specs=pl.BlockSpec((None, tr, tc), lambda s, i, j: (s, i, j)), compiler_params=_params(3))(a, g)


def _adamw(w, g, m, v):
    m = B1 * m + (1.0 - B1) * g
    v = B2 * v + (1.0 - B2) * (g * g)
    m_hat = m / (1.0 - B1 ** STEP)
    v_hat = v / (1.0 - B2 ** STEP)
    delta = -LR * (m_hat / (jnp.sqrt(v_hat) + AEPS) + WD * w)
    return delta, m, v


def _sum_halves(name, place, grads, theirs):
    _, rows, cols = theirs.shape
    tr = _row_tile(rows, cols, 1 << 20)

    def body(place_ref, a_ref, b_ref, o_ref):
        o_ref[...] = (a_ref[...].astype(F32) + b_ref[...].astype(F32)).astype(BF16)

    spec = pl.BlockSpec((None, tr, cols), lambda s, i, p: (s, i, 0))
    return pl.pallas_call(
        body, name=name, out_shape=SDS(theirs.shape, BF16),
        grid_spec=pltpu.PrefetchScalarGridSpec(
            num_scalar_prefetch=1, grid=(NSH, rows // tr),
            in_specs=[pl.BlockSpec((None, None, tr, cols), lambda s, i, p: (s, p[1], i, 0)), spec], out_specs=spec),
        compiler_params=_params(2))(place, grads, theirs)


def _sum_landed(name, place, part, landed):
    _, rows, cols = part.shape
    tr = _row_tile(rows, cols, 1 << 20)

    def body(place_ref, p_ref, l_ref, o_ref):
        o_ref[...] = ((p_ref[...].astype(F32) + l_ref[0].astype(F32)) + l_ref[1].astype(F32)) + l_ref[2].astype(F32)

    return pl.pallas_call(
        body, name=name, out_shape=SDS((2, rows, cols), F32),
        grid_spec=pltpu.PrefetchScalarGridSpec(
            num_scalar_prefetch=1, grid=(rows // tr,),
            in_specs=[pl.BlockSpec((None, tr, cols), lambda i, p: (p[0], i, 0)),
                      pl.BlockSpec((3, tr, cols), lambda i, p: (0, i, 0))],
            out_specs=pl.BlockSpec((None, tr, cols), lambda i, p: (p[1], i, 0))),
        compiler_params=_params(1))(place, part, landed)


def _adam_shard(name, g, w, m, v):
    rows, cols = w.shape
    tr = _row_tile(rows, cols, 1 << 19)

    def body(g_ref, w_ref, m_ref, v_ref, go_ref, d_ref, nm_ref, nv_ref):
        g = g_ref[...]
        go_ref[...] = g
        d_ref[...], nm_ref[...], nv_ref[...] = _adamw(w_ref[...], g, m_ref[...], v_ref[...])

    spec = pl.BlockSpec((tr, cols), lambda i: (i, 0))
    return pl.pallas_call(
        body, name=name, out_shape=[SDS((rows, cols), F32)] * 4, grid=(rows // tr,),
        in_specs=[spec] * 4, out_specs=[spec] * 4, compiler_params=_params(1))(g, w, m, v)


def _adam_small(gathered, w, m, v):
    def body(g_ref, w_ref, m_ref, v_ref, go_ref, d_ref, nm_ref, nv_ref):
        g = g_ref[0:SMALL_ROWS, :]
        for dev in range(1, 8):
            g = g + g_ref[dev * SMALL_ROWS:(dev + 1) * SMALL_ROWS, :]
        go_ref[...] = g
        d_ref[...], nm_ref[...], nv_ref[...] = _adamw(w_ref[...], g, m_ref[...], v_ref[...])

    return pl.pallas_call(body, name="adam_small", out_shape=[SDS((SMALL_ROWS, HD), F32)] * 4)(gathered, w, m, v)


SMALL = (("norm_mix", (1, D)), ("b_gate", (1, 2 * D)), ("q_norm_a", (1, HD)), ("k_norm_a", (1, HD)),
         ("q_norm_b", (1, HD)), ("k_norm_b", (1, HD)), ("rpb_b", (1, 4, 15, 31)), ("norm_ffn", (1, D)))


def _pack_small(vals):
    pieces = []
    for (name, shape), val in zip(SMALL, vals):
        flat = val.reshape(-1)
        pad = (-flat.shape[0]) % HD
        pieces.append(jnp.pad(flat, (0, pad)).reshape(-1, HD))
    packed = jnp.concatenate(pieces, axis=0)
    return jnp.pad(packed, ((0, SMALL_ROWS - packed.shape[0]), (0, 0)))


def _unpack_small(packed):
    out, row = [], 0
    for name, shape in SMALL:
        size = int(np.prod(shape))
        nrows = -(-size // HD)
        out.append(packed[row:row + nrows].reshape(-1)[:size].reshape(shape))
        row += nrows
    return out


def kernel(x, norm_mix, w_in, b_gate, q_norm_a, k_norm_a, q_norm_b, k_norm_b, rpb_b, w_proj_a, w_proj_b, w_out, norm_ffn, w_up, w_down, loss_target, m_norm_mix, m_w_in, m_b_gate, m_q_norm_a, m_k_norm_a, m_q_norm_b, m_k_norm_b, m_rpb_b, m_w_proj_a, m_w_proj_b, m_w_out, m_norm_ffn, m_w_up, m_w_down, v_norm_mix, v_w_in, v_b_gate, v_q_norm_a, v_k_norm_a, v_q_norm_b, v_k_norm_b, v_rpb_b, v_w_proj_a, v_w_proj_b, v_w_out, v_norm_ffn, v_w_up, v_w_down):
    big_names = ("w_in", "w_proj_a", "w_proj_b", "w_out", "w_up", "w_down")
    big_w = [a[0] for a in (w_in, w_proj_a, w_proj_b, w_out, w_up, w_down)]
    big_m = [a[0] for a in (m_w_in, m_w_proj_a, m_w_proj_b, m_w_out, m_w_up, m_w_down)]
    big_v = [a[0] for a in (v_w_in, v_w_proj_a, v_w_proj_b, v_w_out, v_w_up, v_w_down)]
    x2, target = x[0], loss_target[0]

    place = jnp.stack([2 * lax.axis_index("x") + lax.axis_index("y"), lax.axis_index("c")]).astype(jnp.int32)
    groups = ((0,), (1, 2, 3), (4,), (5,))
    started = []
    for j, grp in enumerate(groups):
        deps = (started[0][4],) if j else ()
        placed = [_cast_into_place(big_w[i], "cast_" + big_names[i], place, deps) for i in grp]
        started.append(_gather_start(f"gather_start_{j}", placed))

    def whole(fulls):
        return [f.reshape(NSH, 2 * f.shape[2], f.shape[3]) for f in fulls]

    def forward_begin(j, after):
        send, recv, _, fulls, _ = started[j]
        fulls = _gather_wait(f"gather_wait_{j}", send, recv, fulls, after)
        send, recv, _, fulls, token = _forward_start(f"forward_start_{j}", fulls)
        return (send, recv, fulls), token

    def forward_end(j, state, after):
        return whole(_forward_wait(f"forward_wait_{j}", *state, after))

    def as_halves(grads):
        return [g.reshape(NSH, 2, g.shape[1] // 2, g.shape[2]) for g in grads]

    def reduce_start(j, grads, theirs):
        parts = [_sum_halves(f"sum_halves_{j}_{i}", place, a, b) for i, (a, b) in enumerate(zip(grads, theirs))]
        send, recv, parts, lands, token = _reduce_start(f"reduce_start_{j}", parts)
        return (send, recv, parts, lands), token

    def reduce_begin(j, grads):
        grads = as_halves(grads)
        return reduce_start(j, grads, _reduce_exchange(f"reduce_exchange_{j}", grads))

    def exchange_begin(j, grads):
        send, recv, grads, lands, token = _exchange_start(f"exchange_start_{j}", as_halves(grads))
        return (send, recv, grads, lands), token

    def exchange_end(j, state, after):
        return reduce_start(j, *_exchange_wait(f"exchange_wait_{j}", *state, after))

    big_out = {}

    def share_begin(j, state, after):
        send, recv, parts, lands = state
        parts, lands = _reduce_wait(f"reduce_wait_{j}", send, recv, parts, lands, after)
        sums = [_sum_landed(f"sum_landed_{j}_{i}", place, p, l) for i, (p, l) in enumerate(zip(parts, lands))]
        send, recv, _, sums, token = _share_start(f"share_start_{j}", sums)
        return (send, recv, sums), token

    def share_end(j, state, after):
        for idx, g in zip(groups[j], _share_wait(f"share_wait_{j}", *state, after)):
            g = g.reshape(big_w[idx].shape)
            big_out[idx] = _adam_shard("adam_" + big_names[idx], g, big_w[idx], big_m[idx], big_v[idx])
        return big_out[groups[j][-1]][1]

    proj, xn = _norm_in_proj_own(x2, norm_mix, whole(started[0][3])[0], place)
    send, recv, _, win, _ = started[0]
    win = _gather_wait("gather_wait_0", send, recv, win, (proj, *[s[4] for s in started[1:]]))
    (win_f,) = whole(_gather_finish("gather_finish_0", win))
    proj = _in_proj_rest("in_proj_rest", xn, win_f, proj, place, (2, 1, 3))
    cos, sin = _rope_tables()
    nw = jnp.stack([jnp.concatenate([jnp.tile(q_norm_a, (1, NHA)), jnp.tile(q_norm_b, (1, NH - NHA))], axis=1),
                    jnp.concatenate([jnp.tile(k_norm_a, (1, NHA)), jnp.tile(k_norm_b, (1, NH - NHA))], axis=1)])
    qkn = _qk_prep(proj, nw, cos, sin)
    fw1, token = forward_begin(1, (qkn,))
    fwd_a = [_attn_a_fwd(qkn, proj, g) for g in range(3)]
    os, ls = [f[0] for f in fwd_a], [f[1] for f in fwd_a]
    fw2, token = forward_begin(2, (os[2], token))
    ob, lse_b, bias = _attn_b_fwd(qkn, proj, _rpb_rows(rpb_b[0]))
    oa, w0, w1, w2 = _comb_fwd(os, ls)
    ws = [w0, w1, w2]
    wpa_f, wpb_f, wout_f = forward_end(1, fw1, (oa, token))
    wout_f = wout_f.reshape(D, D)
    mixed, ob16 = _mix_fwd(oa, ob, proj, b_gate, wpa_f, wpb_f)
    h1, hn = _out_proj_fwd(mixed, wout_f, x2, norm_ffn)
    fw3, token = forward_begin(3, (h1,))
    (wup_f,) = forward_end(2, fw2, (hn, token))
    usq, u = _ffn_up(hn, wup_f)
    (wdown_f,) = forward_end(3, fw3, (u,))
    wdown_f = wdown_f.reshape(DFF, D)
    dy, dy16, loss_parts = _ffn_down_loss(usq, wdown_f, h1, target)
    loss = lax.psum(jnp.sum(loss_parts[:, 0, 0]), ("x", "y", "c"))

    g_down = _grad_w("grad_w_down", usq, dy16, True, DFF // NSH, D, 1024, 1024)
    ex_down, token = exchange_begin(3, [g_down])
    du = _ffn_down_bwd(dy16, wdown_f, u, deps=(token,))
    g_up = _grad_w("grad_w_up", hn, du, False, D, DFF // NSH, 1024, 1024)
    red_down, token = exchange_end(3, ex_down, (g_up,))
    ex_up, token_up = exchange_begin(2, [g_up])
    dh1, dh16, d_norm_ffn = _ffn_up_bwd(du, wup_f, h1, dy, norm_ffn, deps=(token, token_up))
    dya, dyb, dga, dgb, doa, dob, dba, dbb = _mix_bwd(dh16, wout_f, oa, ob16, proj, b_gate, wpa_f, wpb_f)
    g_out = _grad_w("grad_w_out", mixed, dh16, True, D // NSH, D, 512, 1024)
    g_pa = _grad_w("grad_w_proj_a", oa, dya, False, 512, 512, 512, 512)
    g_pb = _grad_w("grad_w_proj_b", ob16, dyb, False, 512, 512, 512, 512)
    red_up, token = exchange_end(2, ex_up, (g_out,))
    ex_mid, token_mid = exchange_begin(1, [g_pa, g_pb, g_out])
    cc = _comb_bwd(doa, os, ws, deps=(token, token_mid))
    bwd_a = [_attn_a_bwd(qkn, proj, doa, ls[g], ws[g], cc, g) for g in range(3)]
    red_mid, token = exchange_end(1, ex_mid, (bwd_a[2][1],))
    dqk_b, dv_b, drpb_t = _attn_b_bwd(qkn, proj, dob, ob, lse_b, bias, deps=(token,))
    dproj, dn = _qk_bwd(proj, nw, cos, sin, [b[0] for b in bwd_a], dqk_b, [b[1] for b in bwd_a] + [dv_b], dga, dgb)
    g_in = _grad_w("grad_w_in", xn, dproj, False, D, DIN // NSH, 1024, 1280)
    red_in, token = reduce_begin(0, [g_in])
    grad_x, d_norm_mix = _in_proj_bwd(dproj, win_f, x2, dh1, norm_mix, deps=(token,))

    sh_down, token = share_begin(3, red_down, (grad_x,))
    sh_up, token = share_begin(2, red_up, (token,))
    done = share_end(3, sh_down, (token,))
    sh_mid, token = share_begin(1, red_mid, (done,))
    done = share_end(2, sh_up, (token,))
    sh_in, token = share_begin(0, red_in, (done,))
    done = share_end(1, sh_mid, (token,))
    done = share_end(0, sh_in, (done,))

    d_rpb = drpb_t[:, :15, GRID_W - WIN_C:GRID_W + WIN_C - 1]
    small_g = [d_norm_mix, jnp.concatenate([dba, dbb], axis=1), dn[0, 0], dn[1, 0], dn[0, 1], dn[1, 1], d_rpb, d_norm_ffn]
    gathered_small = _allgather_small(_pack_small(small_g), done)
    small_w = (norm_mix, b_gate, q_norm_a, k_norm_a, q_norm_b, k_norm_b, rpb_b, norm_ffn)
    small_m = (m_norm_mix, m_b_gate, m_q_norm_a, m_k_norm_a, m_q_norm_b, m_k_norm_b, m_rpb_b, m_norm_ffn)
    small_v = (v_norm_mix, v_b_gate, v_q_norm_a, v_k_norm_a, v_q_norm_b, v_k_norm_b, v_rpb_b, v_norm_ffn)
    small_out = [_unpack_small(p) for p in
                 _adam_small(gathered_small, _pack_small(small_w), _pack_small(small_m), _pack_small(small_v))]

    order = ("norm_mix", "w_in", "b_gate", "q_norm_a", "k_norm_a", "q_norm_b", "k_norm_b", "rpb_b",
             "w_proj_a", "w_proj_b", "w_out", "norm_ffn", "w_up", "w_down")
    small_idx = {name: i for i, (name, _) in enumerate(SMALL)}
    outs = []
    for kind in range(4):
        for name in order:
            if name in small_idx:
                outs.append(small_out[kind][small_idx[name]])
            else:
                outs.append(big_out[big_names.index(name)][kind][None])
    return (loss, grad_x[None], *outs)
```

```python
import functools

import numpy as np
import jax
import jax.numpy as jnp
from jax import lax
from jax.experimental import pallas as pl
from jax.experimental.pallas import tpu as pltpu

F32, BF16 = jnp.float32, jnp.bfloat16
SDS = jax.ShapeDtypeStruct
MESH = pl.DeviceIdType.MESH

T = 2048
D = 2048
HD = 128
NH, NHA = 16, 12
DIN = 10240
DFF = 8192
NSH = 4
DILS = (1, 4, 16)
EPS = 1e-6
NEG = -1e30
SCALE = HD ** -0.5
GRID_W, WIN_R, WIN_C = 64, 8, 16
VMEM_LIMIT = 56 * 1024 * 1024
B1, B2, LR, AEPS, WD, STEP = 0.9, 0.999, 0.001, 1e-08, 0.01, 10
SMALL_ROWS = 88


def _dot(a, b):
    return jnp.dot(a, b, preferred_element_type=F32)


def _dot_nt(a, b):
    return lax.dot_general(a, b, (((1,), (1,)), ((), ())), preferred_element_type=F32)


def _dot_tn(a, b):
    return lax.dot_general(a, b, (((0,), (0,)), ((), ())), preferred_element_type=F32)


def _params(n):
    return pltpu.CompilerParams(dimension_semantics=("arbitrary",) * n, vmem_limit_bytes=VMEM_LIMIT)


def _resident(shape, index_map):
    return pl.BlockSpec(shape, index_map, pipeline_mode=pl.Buffered(1))


def _sigmoid(z):
    return 1.0 / (1.0 + jnp.exp(-z))


def _wide(v, n):
    return jnp.concatenate([v] * n, axis=1)


def _row_tile(rows, cols, elems):
    tr = 16
    while tr * 2 <= rows and tr * 2 * cols <= elems:
        tr *= 2
    return tr


def _place():
    x, y, c = lax.axis_index("x"), lax.axis_index("y"), lax.axis_index("c")
    peers = [(1 - x, y), (x, 1 - y), (1 - x, 1 - y)]
    return x, y, c, peers


def _cast_into_place(w, name, place, deps=()):
    rows, cols = w.shape
    hr = rows // 2
    tr = min(hr, 256)
    per = hr // tr

    def body(*refs):
        w_ref, o_ref = refs[-2:]
        o_ref[...] = w_ref[...].astype(BF16)

    return pl.pallas_call(
        body, name=name, out_shape=SDS((NSH, 2, hr, cols), BF16),
        grid_spec=pltpu.PrefetchScalarGridSpec(
            num_scalar_prefetch=1, grid=(2, per),
            in_specs=[DEP_SPEC] * len(deps) + [pl.BlockSpec((tr, cols), lambda h, i, p: (h * per + i, 0))],
            out_specs=pl.BlockSpec((None, None, tr, cols), lambda h, i, p: (p[0], h, i, 0))),
        compiler_params=_params(2))(place, *deps, w)


ANY_SPEC = pl.BlockSpec(memory_space=pl.ANY)
HBM_SPEC = pl.BlockSpec(memory_space=pltpu.HBM)
SEM_SPEC = pl.BlockSpec(memory_space=pltpu.SEMAPHORE)
DEP_SPEC = pl.BlockSpec((8, 128), lambda *_: (0, 0))
EFFECT = pltpu.SideEffectType.DATAFLOW_SIDE_EFFECTING


def _after(body, deps):
    n = len(deps)
    return (lambda *refs: body(*refs[n:])) if n else body


SIBLING_BARRIER = 1


def _split_start(name, srcs, lands, n_copies, issue, sibling_only=False, after=()):
    n, m, d = len(srcs), len(lands), len(after)

    def body(*refs):
        if sibling_only:
            x, y, c, _ = _place()
            barrier = pltpu.get_barrier_semaphore()
            pl.semaphore_signal(barrier, inc=1, device_id=(x, y, 1 - c), device_id_type=MESH)
            pl.semaphore_wait(barrier, 1)
        issue(refs[:n], refs[n:n + m], refs[n + m + d], refs[n + m + d + 1])
        refs[-1][...] = jnp.zeros((8, 128), F32)

    arrays = list(srcs) + list(lands)
    outs = pl.pallas_call(
        body, name=name,
        out_shape=(pltpu.SemaphoreType.DMA((n_copies,)), pltpu.SemaphoreType.DMA((n_copies,)),
                   *[pltpu.HBM(a.shape, a.dtype) for a in arrays], SDS((8, 128), F32)),
        in_specs=[HBM_SPEC] * (n + m) + [ANY_SPEC] * d,
        out_specs=(SEM_SPEC, SEM_SPEC, *[HBM_SPEC] * (n + m), pl.BlockSpec(memory_space=pltpu.VMEM)),
        input_output_aliases={i: 2 + i for i in range(n + m)},
        compiler_params=pltpu.CompilerParams(has_side_effects=EFFECT,
                                             collective_id=SIBLING_BARRIER if sibling_only else None),
    )(*[pltpu.with_memory_space_constraint(a, pltpu.HBM) for a in arrays], *after)
    return outs[0], outs[1], list(outs[2:2 + n]), list(outs[2 + n:2 + n + m]), outs[-1]


def _split_wait(name, send_sems, recv_sems, srcs, lands, after, wait):
    n, m = len(srcs), len(lands)

    def body(*refs):
        wait(refs[:n], refs[n:n + m], refs[n + m], refs[n + m + 1])

    arrays = list(srcs) + list(lands)
    outs = pl.pallas_call(
        body, name=name, out_shape=[pltpu.HBM(a.shape, a.dtype) for a in arrays],
        in_specs=[HBM_SPEC] * (n + m) + [SEM_SPEC, SEM_SPEC] + [ANY_SPEC] * len(after),
        out_specs=[HBM_SPEC] * (n + m), input_output_aliases={i: i for i in range(n + m)},
        compiler_params=pltpu.CompilerParams(has_side_effects=EFFECT),
    )(*arrays, send_sems, recv_sems, *after)
    return list(outs[:n]), list(outs[n:])


def _gather_start(name, fulls, ks=(0, 1, 2), after=()):
    def issue(srcs, dsts, send_sems, recv_sems):
        x, y, c, peers = _place()
        for i in range(len(fulls)):
            mine = dsts[i].at[2 * x + y, c]
            for k in ks:
                px, py = peers[k]
                pltpu.make_async_remote_copy(
                    src_ref=mine, dst_ref=mine, send_sem=send_sems.at[3 * i + k],
                    recv_sem=recv_sems.at[3 * i + k], device_id=(px, py, c), device_id_type=MESH).start()

    return _split_start(name, [], fulls, 3 * len(fulls), issue, after=after)


def _gather_wait(name, send_sems, recv_sems, fulls, after, ks=(0, 1, 2)):
    def wait(srcs, dsts, send_sems, recv_sems):
        x, y, c, peers = _place()
        for i in range(len(fulls)):
            for k in ks:
                px, py = peers[k]
                cp = pltpu.make_async_remote_copy(
                    src_ref=dsts[i].at[2 * x + y, c], dst_ref=dsts[i].at[2 * px + py, c],
                    send_sem=send_sems.at[3 * i + k], recv_sem=recv_sems.at[3 * i + k],
                    device_id=(px, py, c), device_id_type=MESH)
                cp.wait_send()
                cp.wait_recv()

    return _split_wait(name, send_sems, recv_sems, [], fulls, after, wait)[1]


def _gather_finish(name, fulls, ks=(0, 1, 2)):
    n = len(fulls)

    def body(*refs):
        fin, fout = refs[:n], refs[n:2 * n]
        send_sems, recv_sems = refs[2 * n:]
        x, y, c, peers = _place()

        def copy(i, k, half):
            px, py = peers[k]
            return pltpu.make_async_remote_copy(
                src_ref=fin[i].at[2 * px + py, half], dst_ref=fout[i].at[2 * px + py, half],
                send_sem=send_sems.at[3 * i + k], recv_sem=recv_sems.at[3 * i + k],
                device_id=(x, y, 1 - c), device_id_type=MESH)

        sends = [copy(i, k, c) for i in range(n) for k in ks]
        for cp in sends:
            cp.start()
        for i in range(n):
            for k in ks:
                copy(i, k, 1 - c).wait_recv()
        for cp in sends:
            cp.wait_send()

    return pl.pallas_call(
        body, name=name, out_shape=[SDS(f.shape, f.dtype) for f in fulls],
        in_specs=[ANY_SPEC] * n, out_specs=[ANY_SPEC] * n, input_output_aliases={i: i for i in range(n)},
        scratch_shapes=[pltpu.SemaphoreType.DMA((3 * n,)), pltpu.SemaphoreType.DMA((3 * n,))])(*fulls)


def _reduce_start(name, parts):
    lands = [lax.empty((3,) + p.shape[1:], p.dtype) for p in parts]

    def issue(srcs, dsts, send_sems, recv_sems):
        x, y, c, peers = _place()
        for i in range(len(parts)):
            for k, (px, py) in enumerate(peers):
                pltpu.make_async_remote_copy(
                    src_ref=srcs[i].at[2 * px + py], dst_ref=dsts[i].at[k], send_sem=send_sems.at[3 * i + k],
                    recv_sem=recv_sems.at[3 * i + k], device_id=(px, py, c), device_id_type=MESH).start()

    return _split_start(name, parts, lands, 3 * len(parts), issue)


def _reduce_wait(name, send_sems, recv_sems, parts, lands, after):
    def wait(srcs, dsts, send_sems, recv_sems):
        x, y, c, peers = _place()
        for i in range(len(parts)):
            for k, (px, py) in enumerate(peers):
                cp = pltpu.make_async_remote_copy(
                    src_ref=srcs[i].at[2 * px + py], dst_ref=dsts[i].at[k], send_sem=send_sems.at[3 * i + k],
                    recv_sem=recv_sems.at[3 * i + k], device_id=(px, py, c), device_id_type=MESH)
                cp.wait_send()
                cp.wait_recv()

    return _split_wait(name, send_sems, recv_sems, parts, lands, after, wait)


def _sibling_copy(src, dst, send_sems, recv_sems, k):
    x, y, c, _ = _place()
    return pltpu.make_async_remote_copy(src_ref=src, dst_ref=dst, send_sem=send_sems.at[k], recv_sem=recv_sems.at[k],
                                        device_id=(x, y, 1 - c), device_id_type=MESH)


def _forward_start(name, fulls):
    def issue(srcs, dsts, send_sems, recv_sems):
        x, y, c, peers = _place()
        for i in range(len(fulls)):
            for k, (px, py) in enumerate(peers):
                part = dsts[i].at[2 * px + py, c]
                _sibling_copy(part, part, send_sems, recv_sems, 3 * i + k).start()

    return _split_start(name, [], fulls, 3 * len(fulls), issue, sibling_only=True)


def _forward_wait(name, send_sems, recv_sems, fulls, after):
    def wait(srcs, dsts, send_sems, recv_sems):
        x, y, c, peers = _place()
        for i in range(len(fulls)):
            for k, (px, py) in enumerate(peers):
                cp = _sibling_copy(dsts[i].at[2 * px + py, c], dsts[i].at[2 * px + py, 1 - c], send_sems, recv_sems, 3 * i + k)
                cp.wait_send()
                cp.wait_recv()

    return _split_wait(name, send_sems, recv_sems, [], fulls, after, wait)[1]


def _exchange_start(name, grads, sliced=True):
    lands = [lax.empty((NSH,) + g.shape[-2:], g.dtype) for g in grads]

    def issue(srcs, dsts, send_sems, recv_sems):
        c = lax.axis_index("c")
        for i in range(len(grads)):
            src = srcs[i].at[:, 1 - c] if sliced else srcs[i]
            _sibling_copy(src, dsts[i], send_sems, recv_sems, i).start()

    return _split_start(name, grads, lands, len(grads), issue, sibling_only=True)


def _exchange_wait(name, send_sems, recv_sems, grads, lands, after, sliced=True):
    def wait(srcs, dsts, send_sems, recv_sems):
        c = lax.axis_index("c")
        for i in range(len(grads)):
            cp = _sibling_copy(srcs[i].at[:, 1 - c] if sliced else srcs[i], dsts[i], send_sems, recv_sems, i)
            cp.wait_send()
            cp.wait_recv()

    return _split_wait(name, send_sems, recv_sems, grads, lands, after, wait)


def _share_start(name, sums):
    def issue(srcs, dsts, send_sems, recv_sems):
        c = lax.axis_index("c")
        for i in range(len(sums)):
            _sibling_copy(dsts[i].at[c], dsts[i].at[c], send_sems, recv_sems, i).start()

    return _split_start(name, [], sums, len(sums), issue, sibling_only=True)


def _share_wait(name, send_sems, recv_sems, sums, after):
    def wait(srcs, dsts, send_sems, recv_sems):
        c = lax.axis_index("c")
        for i in range(len(sums)):
            cp = _sibling_copy(dsts[i].at[c], dsts[i].at[1 - c], send_sems, recv_sems, i)
            cp.wait_send()
            cp.wait_recv()

    return _split_wait(name, send_sems, recv_sems, [], sums, after, wait)[1]


def _allgather_small(v, after):
    m_per, n = v.shape

    def body(x_ref, after_ref, out_ref, send_sems, recv_sems, local_sem):
        x, y, c = lax.axis_index("x"), lax.axis_index("y"), lax.axis_index("c")
        me, sibling = (x, y, c), (x, y, 1 - c)
        chips = [(1 - x, y), (x, 1 - y), (1 - x, 1 - y)]

        def rows(px, py, pc):
            return out_ref.at[pl.ds((4 * px + 2 * py + pc) * m_per, m_per), :]

        def copy(k, block, to, src=None):
            return pltpu.make_async_remote_copy(
                src_ref=rows(*block) if src is None else src, dst_ref=rows(*block),
                send_sem=send_sems.at[k], recv_sem=recv_sems.at[k], device_id=to, device_id_type=MESH)

        mine = pltpu.make_async_copy(x_ref, rows(*me), local_sem)
        mine.start()
        first = [copy(0, me, sibling, src=x_ref)]
        first += [copy(1 + j, me, (*chip, c), src=x_ref) for j, chip in enumerate(chips)]
        for cp in first:
            cp.start()
        passed = [copy(4 + j, (*chip, c), sibling) for j, chip in enumerate(chips)]
        for j, chip in enumerate(chips):
            copy(1 + j, (*chip, c), me).wait_recv()
            passed[j].start()
        copy(0, sibling, me).wait_recv()
        for j, chip in enumerate(chips):
            copy(4 + j, (*chip, 1 - c), me).wait_recv()
        for cp in first + passed:
            cp.wait_send()
        mine.wait()

    return pl.pallas_call(
        body, name="allgather_small", out_shape=SDS((8 * m_per, n), v.dtype),
        in_specs=[pl.BlockSpec(memory_space=pltpu.VMEM), ANY_SPEC], out_specs=pl.BlockSpec(memory_space=pltpu.VMEM),
        scratch_shapes=[pltpu.SemaphoreType.DMA((7,)), pltpu.SemaphoreType.DMA((7,)), pltpu.SemaphoreType.DMA])(v, after)


def _norm_in_proj_own(x, g, w_full, place):
    tn, chunk = 512, 256
    per = (DIN // NSH) // tn

    def body(place_ref, x_ref, g_ref, w_ref, proj_ref, xn_ref):
        @pl.when(pl.program_id(0) == 0)
        def _():
            def norm(r, carry):
                rows = pl.ds(pl.multiple_of(r * chunk, chunk), chunk)
                xv = x_ref[rows, :]
                rs = lax.rsqrt(jnp.mean(xv * xv, axis=-1, keepdims=True) + EPS)
                xn_ref[rows, :] = (xv * rs * g_ref[...]).astype(BF16)
                return carry

            lax.fori_loop(0, T // chunk, norm, 0)

        proj_ref[...] = _dot(xn_ref[...], w_ref[...])

    return pl.pallas_call(
        body, name="norm_in_proj_own", out_shape=[SDS((T, DIN), F32), SDS((T, D), BF16)],
        grid_spec=pltpu.PrefetchScalarGridSpec(
            num_scalar_prefetch=1, grid=(per,),
            in_specs=[_resident((T, D), lambda j, p: (0, 0)),
                      pl.BlockSpec((1, D), lambda j, p: (0, 0)),
                      pl.BlockSpec((None, D, tn), lambda j, p: (p[0], 0, j))],
            out_specs=[pl.BlockSpec((T, tn), lambda j, p: (0, p[0] * per + j)),
                       pl.BlockSpec((T, D), lambda j, p: (0, 0))]),
        compiler_params=_params(1))(place, x, g, w_full)


def _in_proj_rest(name, xn, w_full, proj, place, flips):
    tn = 512
    per = (DIN // NSH) // tn

    def body(place_ref, xn_ref, w_ref, proj_in, proj_ref):
        proj_ref[...] = _dot(xn_ref[...], w_ref[...])

    def shard(j, p):
        flip = flips[0]
        for n, f in enumerate(flips[1:]):
            flip = jnp.where(j // per == n + 1, f, flip)
        return p[0] ^ flip

    return pl.pallas_call(
        body, name=name, out_shape=SDS((T, DIN), F32),
        grid_spec=pltpu.PrefetchScalarGridSpec(
            num_scalar_prefetch=1, grid=(len(flips) * per,),
            in_specs=[_resident((T, D), lambda j, p: (0, 0)),
                      pl.BlockSpec((None, D, tn), lambda j, p: (shard(j, p), 0, j % per)), ANY_SPEC],
            out_specs=pl.BlockSpec((T, tn), lambda j, p: (0, shard(j, p) * per + j % per))),
        input_output_aliases={3: 0}, compiler_params=_params(1))(place, xn, w_full, proj)


def _rope_tables():
    pos = np.arange(T, dtype=np.float32)
    inv = (10000.0 ** (-np.arange(0, HD, 2, dtype=np.float32) / HD)).astype(np.float32)
    ang = (pos[:, None] * inv[None, :]).astype(np.float32)
    cos, sin = np.cos(ang).astype(np.float32), np.sin(ang).astype(np.float32)
    return (jnp.asarray(np.concatenate([cos, cos], axis=1)), jnp.asarray(np.concatenate([-sin, sin], axis=1)))


def _qk_prep(proj, nw, cos, sin):
    tm = 256

    def body(p_ref, w_ref, cos_ref, sin_ref, o_ref):
        cv, sv = cos_ref[...], sin_ref[...]
        for h in range(NH):
            sl = slice(h * HD, (h + 1) * HD)
            xv = p_ref[:, sl]
            r = lax.rsqrt(jnp.mean(xv * xv, axis=-1, keepdims=True) + EPS)
            z = xv * r * w_ref[:, sl]
            if h < NHA:
                z = z * cv + pltpu.roll(z, 64, 1) * sv
            o_ref[:, sl] = z.astype(BF16)

    return pl.pallas_call(
        body, name="qk_prep", out_shape=SDS((T, 2 * D), BF16), grid=(T // tm, 2),
        in_specs=[pl.BlockSpec((tm, D), lambda i, j: (i, j)),
                  pl.BlockSpec((None, 1, D), lambda i, j: (j, 0, 0)),
                  pl.BlockSpec((tm, HD), lambda i, j: (i, 0)),
                  pl.BlockSpec((tm, HD), lambda i, j: (i, 0))],
        out_specs=pl.BlockSpec((tm, D), lambda i, j: (i, j)),
        compiler_params=_params(2))(proj, nw, cos, sin)


def _band_mask(q0, m):
    ii = lax.broadcasted_iota(jnp.int32, (128, 256), 0)
    jj = lax.broadcasted_iota(jnp.int32, (128, 256), 1)
    rel = jj - ii
    kpos = jj + (q0 - 64)
    return (rel >= 0) & (rel <= 128) & (kpos >= 0) & (kpos < m)


def _fill_padded(dst, src, m):
    zeros = jnp.zeros((64, HD), dst.dtype)
    dst[0:64, :] = zeros
    dst[64 + m:128 + m, :] = zeros
    dst[64:64 + m, :] = src.astype(dst.dtype)


def _residue_rows(r, m, dil):
    return pl.ds(r, m, stride=dil) if dil > 1 else slice(None)


def _head_blocks(g):
    col = lambda base: pl.BlockSpec((T, HD), lambda h: (0, base + g * 4 + h))
    return col(0), col(NH), col(2 * NH), pl.BlockSpec((T, HD), lambda h: (0, h))


def _attn_a_fwd(qkn, proj, g):
    dil = DILS[g]
    m = T // dil
    nb = m // 128

    def body(q_ref, k_ref, v_ref, o_ref, l_ref, qf, kf, qp, kp, vp, ob, lb):
        qf[...] = q_ref[...].astype(F32)
        kf[...] = k_ref[...].astype(F32)
        for r in range(dil):
            rows = _residue_rows(r, m, dil)
            qp[...] = qf[rows, :].astype(BF16)
            _fill_padded(kp, kf[rows, :], m)
            _fill_padded(vp, v_ref[rows, :], m)

            def block(b, carry):
                q0 = pl.multiple_of(b * 128, 128)
                kw, vw = kp[pl.ds(q0, 256), :], vp[pl.ds(q0, 256), :]
                s = _dot_nt(qp[pl.ds(q0, 128), :], kw) * SCALE
                s = jnp.where(_band_mask(q0, m), s, NEG)
                mx = jnp.max(s, axis=-1, keepdims=True)
                e = jnp.exp(s - mx)
                den = jnp.sum(e, axis=-1, keepdims=True)
                ob[pl.ds(q0, 128), :] = _dot((e / den).astype(BF16), vw)
                lb[pl.ds(q0, 128), :] = jnp.broadcast_to(mx + jnp.log(den), (128, HD))
                return carry

            lax.fori_loop(0, nb, block, 0, unroll=min(nb, 2))
            o_ref[rows, :] = ob[...]
            l_ref[rows, :] = lb[...]

    q_blk, k_blk, v_blk, out_blk = _head_blocks(g)
    return pl.pallas_call(
        body, name=f"attn_a_fwd_{g}", out_shape=[SDS((T, 512), F32)] * 2, grid=(4,),
        in_specs=[q_blk, k_blk, v_blk], out_specs=[out_blk] * 2,
        scratch_shapes=[pltpu.VMEM((T, HD), F32), pltpu.VMEM((T, HD), F32), pltpu.VMEM((m, HD), BF16),
                        pltpu.VMEM((m + 128, HD), BF16), pltpu.VMEM((m + 128, HD), BF16),
                        pltpu.VMEM((m, HD), F32), pltpu.VMEM((m, HD), F32)],
        compiler_params=_params(1))(qkn, qkn, proj)


def _nbr_window(r):
    start = jnp.clip(r - WIN_R // 2, 0, T // GRID_W - WIN_R)
    return start, start - r + (WIN_R - 1)


def _rpb_rows(rpb):
    zeros = jnp.zeros((4, 14, 33), F32)
    a, b = rpb[:, :14], rpb[:, 1:15]
    rows = jnp.concatenate([a[:, :, 15:31], zeros, b, zeros, a[:, :, 0:15]], axis=2)
    return jnp.pad(rows, ((0, 0), (0, 2), (0, 0)))


def _attn_b_fwd(qkn, proj, rpb_rows):
    def body(r_ref, q_ref, k_ref, v_ref, o_ref, l_ref, bias_ref, vb, pair):
        qc = lax.broadcasted_iota(jnp.int32, (GRID_W, 512), 0)
        kc = lax.broadcasted_iota(jnp.int32, (GRID_W, 512), 1) & (GRID_W - 1)
        cs = jnp.clip(qc - WIN_C // 2, 0, GRID_W - WIN_C)
        colmask = (kc >= cs) & (kc < cs + WIN_C)
        for d in range(14):
            pair[d] = pltpu.roll(jnp.broadcast_to(r_ref[d:d + 1, :], (GRID_W, HD)), 0, 1, stride=1, stride_axis=0)
        for off in range(8):
            rows = jnp.concatenate([pair[off + 2 * jj] for jj in range(4)], axis=1)
            bias_ref[off] = jnp.where(colmask, rows, NEG)
        vb[...] = v_ref[...].astype(BF16)

        def row(r, carry):
            start, off = _nbr_window(r)
            q0 = pl.multiple_of(r * GRID_W, GRID_W)
            k0 = pl.multiple_of(start * GRID_W, GRID_W)
            s = _dot_nt(q_ref[pl.ds(q0, GRID_W), :], k_ref[pl.ds(k0, 512), :]) * SCALE + bias_ref[off]
            mx = jnp.max(s, axis=-1, keepdims=True)
            e = jnp.exp(s - mx)
            den = jnp.sum(e, axis=-1, keepdims=True)
            o_ref[pl.ds(q0, GRID_W), :] = _dot((e / den).astype(BF16), vb[pl.ds(k0, 512), :])
            l_ref[pl.ds(q0, GRID_W), :] = jnp.broadcast_to(mx + jnp.log(den), (GRID_W, HD))
            return carry

        lax.fori_loop(0, T // GRID_W, row, 0, unroll=2)

    return pl.pallas_call(
        body, name="attn_b_fwd",
        out_shape=[SDS((T, 512), F32), SDS((T, 512), F32), SDS((4, 8, GRID_W, 512), F32)], grid=(4,),
        in_specs=[pl.BlockSpec((None, 16, HD), lambda h: (h, 0, 0)),
                  pl.BlockSpec((T, HD), lambda h: (0, NHA + h)),
                  pl.BlockSpec((T, HD), lambda h: (0, NH + NHA + h)),
                  pl.BlockSpec((T, HD), lambda h: (0, 2 * NH + NHA + h))],
        out_specs=[pl.BlockSpec((T, HD), lambda h: (0, h)), pl.BlockSpec((T, HD), lambda h: (0, h)),
                   pl.BlockSpec((None, 8, GRID_W, 512), lambda h: (h, 0, 0, 0))],
        scratch_shapes=[pltpu.VMEM((T, HD), BF16), pltpu.VMEM((14, GRID_W, HD), F32)],
        compiler_params=_params(1))(rpb_rows, qkn, qkn, proj)


def _comb_fwd(os, ls):
    tm = 512

    def body(o0, o1, o2, l0, l1, l2, oa_ref, w0, w1, w2):
        lv = [l0[...], l1[...], l2[...]]
        mx = jnp.maximum(jnp.maximum(lv[0], lv[1]), lv[2])
        ev = [jnp.exp(l - mx) for l in lv]
        den = ev[0] + ev[1] + ev[2]
        wv = [e / den for e in ev]
        oa_ref[...] = (wv[0] * o0[...] + wv[1] * o1[...] + wv[2] * o2[...]).astype(BF16)
        w0[...], w1[...], w2[...] = wv

    spec = pl.BlockSpec((tm, 512), lambda i: (i, 0))
    return pl.pallas_call(
        body, name="comb_fwd", out_shape=[SDS((T, 512), BF16)] + [SDS((T, 512), F32)] * 3, grid=(T // tm,),
        in_specs=[spec] * 6, out_specs=[spec] * 4, compiler_params=_params(1))(*os, *ls)


def _mix_fwd(oa, ob, proj, b_gate, wpa, wpb):
    tm = 512

    def body(oa_ref, ob_ref, ga_ref, gb_ref, ba_ref, bb_ref, wpa_ref, wpb_ref, mixed_ref, ob16_ref):
        oav = oa_ref[...]
        obv = ob_ref[...].astype(BF16)
        ob16_ref[...] = obv
        for s in range(NSH):
            sl = slice(s * 512, (s + 1) * 512)
            ga = _sigmoid(ga_ref[:, sl] + ba_ref[:, sl])
            gb = _sigmoid(gb_ref[:, sl] + bb_ref[:, sl])
            mixed_ref[:, sl] = (ga * _dot(oav, wpa_ref[s]) + gb * _dot(obv, wpb_ref[s])).astype(BF16)

    row = lambda w: pl.BlockSpec((tm, w), lambda i: (i, 0))
    return pl.pallas_call(
        body, name="mix_fwd", out_shape=[SDS((T, D), BF16), SDS((T, 512), BF16)], grid=(T // tm,),
        in_specs=[row(512), row(512),
                  pl.BlockSpec((tm, D), lambda i: (i, 3)), pl.BlockSpec((tm, D), lambda i: (i, 4)),
                  pl.BlockSpec((1, D), lambda i: (0, 0)), pl.BlockSpec((1, D), lambda i: (0, 1)),
                  _resident((NSH, 512, 512), lambda i: (0, 0, 0)), _resident((NSH, 512, 512), lambda i: (0, 0, 0))],
        out_specs=[row(D), row(512)], compiler_params=_params(1))(oa, ob, proj, proj, b_gate, b_gate, wpa, wpb)


def _out_proj_fwd(mixed, w_out, x, g):
    tm = 512

    def body(m_ref, w_ref, x_ref, g_ref, h1_ref, hn_ref):
        h1 = x_ref[...] + _dot(m_ref[...], w_ref[...])
        h1_ref[...] = h1
        r = lax.rsqrt(jnp.mean(h1 * h1, axis=-1, keepdims=True) + EPS)
        hn_ref[...] = (h1 * r * g_ref[...]).astype(BF16)

    row = pl.BlockSpec((tm, D), lambda i: (i, 0))
    return pl.pallas_call(
        body, name="out_proj_fwd", out_shape=[SDS((T, D), F32), SDS((T, D), BF16)], grid=(T // tm,),
        in_specs=[row, _resident((D, D), lambda i: (0, 0)), row, pl.BlockSpec((1, D), lambda i: (0, 0))],
        out_specs=[row, row], compiler_params=_params(1))(mixed, w_out, x, g)


def _ffn_up(hn, w_up):
    tm, tn = T, 512
    per = (DFF // NSH) // tn

    def body(h_ref, w_ref, a_ref, u_ref):
        uv = jnp.maximum(_dot(h_ref[...], w_ref[...]), 0.0)
        a_ref[...] = (uv * uv).astype(BF16)
        u_ref[...] = uv.astype(BF16)

    out = pl.BlockSpec((tm, tn), lambda i, j: (i, j))
    return pl.pallas_call(
        body, name="ffn_up", out_shape=[SDS((T, DFF), BF16)] * 2, grid=(T // tm, DFF // tn),
        in_specs=[pl.BlockSpec((tm, D), lambda i, j: (i, 0)),
                  pl.BlockSpec((None, D, tn), lambda i, j: (j // per, 0, j % per))],
        out_specs=[out, out], compiler_params=_params(2))(hn, w_up)


def _ffn_down_loss(u, w_down, h1, target):
    tm, tk = 512, 2048
    nk = DFF // tk

    def body(u_ref, w_ref, h1_ref, t_ref, dy_ref, dy16_ref, loss_ref, acc):
        k = pl.program_id(1)

        @pl.when(k == 0)
        def _():
            acc[...] = jnp.zeros_like(acc)

        acc[...] += _dot(u_ref[...], w_ref[...])

        @pl.when(k == nk - 1)
        def _():
            def chunk(r, sq):
                rows = pl.ds(pl.multiple_of(r * 16, 16), 16)
                err = acc[rows, :] + h1_ref[rows, :] - t_ref[rows, :]
                dy = err * (1.0 / D)
                dy_ref[rows, :] = dy
                dy16_ref[rows, :] = dy.astype(BF16)
                return sq + err * err

            sq = lax.fori_loop(0, tm // 16, chunk, jnp.zeros((16, D), F32), unroll=2)
            part = 0.5 * jnp.sum(jnp.mean(sq, axis=-1, keepdims=True), axis=0, keepdims=True)
            loss_ref[...] = jnp.broadcast_to(part, (8, 128))

    row = pl.BlockSpec((tm, D), lambda i, k: (i, 0))
    once = _resident((tm, D), lambda i, k: (i, 0))
    return pl.pallas_call(
        body, name="ffn_down_loss",
        out_shape=[SDS((T, D), F32), SDS((T, D), BF16), SDS((T // tm, 8, 128), F32)], grid=(T // tm, nk),
        in_specs=[pl.BlockSpec((tm, tk), lambda i, k: (i, k)), pl.BlockSpec((tk, D), lambda i, k: (k, 0)), once, once],
        out_specs=[row, row, pl.BlockSpec((None, 8, 128), lambda i, k: (i, 0, 0))],
        scratch_shapes=[pltpu.VMEM((tm, D), F32)], compiler_params=_params(2))(u, w_down, h1, target)


def _ffn_down_bwd(dy16, w_down, u, deps=()):
    tm, tn = T, 512

    def body(dy_ref, w_ref, u_ref, du_ref):
        uv = u_ref[...].astype(F32)
        du_ref[...] = jnp.where(uv > 0.0, 2.0 * uv * _dot_nt(dy_ref[...], w_ref[...]), 0.0).astype(BF16)

    return pl.pallas_call(
        _after(body, deps), name="ffn_down_bwd", out_shape=SDS((T, DFF), BF16), grid=(T // tm, DFF // tn),
        in_specs=[DEP_SPEC] * len(deps) + [
            pl.BlockSpec((tm, D), lambda i, j: (i, 0)), pl.BlockSpec((tn, D), lambda i, j: (j, 0)),
            pl.BlockSpec((tm, tn), lambda i, j: (i, j))],
        out_specs=pl.BlockSpec((tm, tn), lambda i, j: (i, j)), compiler_params=_params(2))(*deps, dy16, w_down, u)


def _norm_bwd(xv, dz_in, g):
    r = lax.rsqrt(jnp.mean(xv * xv, axis=-1, keepdims=True) + EPS)
    dg = jnp.sum(xv * r * dz_in, axis=0, keepdims=True)
    dz = dz_in * g
    dx = r * dz - xv * (r * r * r) * jnp.mean(xv * dz, axis=-1, keepdims=True)
    return dx, dg


def _ffn_up_bwd(du, w_up, h1, dy, g, deps=()):
    tm, tk = 512, 1024
    per = (DFF // NSH) // tk
    nk = DFF // tk

    def body(du_ref, w_ref, h1_ref, dy_ref, g_ref, dh1_ref, dh16_ref, dg_ref, acc):
        i, k = pl.program_id(0), pl.program_id(1)

        @pl.when(k == 0)
        def _():
            acc[...] = jnp.zeros_like(acc)

        @pl.when((k == 0) & (i == 0))
        def _():
            dg_ref[...] = jnp.zeros_like(dg_ref)

        acc[...] += _dot_nt(du_ref[...], w_ref[...])

        @pl.when(k == nk - 1)
        def _():
            dx, dg = _norm_bwd(h1_ref[...], acc[...], g_ref[...])
            dh1 = dy_ref[...] + dx
            dh1_ref[...] = dh1
            dh16_ref[...] = dh1.astype(BF16)
            dg_ref[...] += dg

    row = pl.BlockSpec((tm, D), lambda i, k: (i, 0))
    vec = pl.BlockSpec((1, D), lambda i, k: (0, 0))
    return pl.pallas_call(
        _after(body, deps), name="ffn_up_bwd", out_shape=[SDS((T, D), F32), SDS((T, D), BF16), SDS((1, D), F32)],
        grid=(T // tm, nk),
        in_specs=[DEP_SPEC] * len(deps) + [
            pl.BlockSpec((tm, tk), lambda i, k: (i, k)),
            pl.BlockSpec((None, D, tk), lambda i, k: (k // per, 0, k % per)), row, row, vec],
        out_specs=[row, row, vec], scratch_shapes=[pltpu.VMEM((tm, D), F32)],
        compiler_params=_params(2))(*deps, du, w_up, h1, dy, g)


def _mix_bwd(dh16, w_out, oa, ob16, proj, b_gate, wpa, wpb):
    tm = 256

    def body(dh_ref, wo_ref, oa_ref, ob_ref, ga_ref, gb_ref, ba_ref, bb_ref, wpa_ref, wpb_ref,
             dya_ref, dyb_ref, dga_ref, dgb_ref, doa_ref, dob_ref, dba_ref, dbb_ref):
        @pl.when(pl.program_id(0) == 0)
        def _():
            dba_ref[...] = jnp.zeros_like(dba_ref)
            dbb_ref[...] = jnp.zeros_like(dbb_ref)

        oav, obv = oa_ref[...], ob_ref[...]
        doa = jnp.zeros((tm, 512), F32)
        dob = jnp.zeros((tm, 512), F32)
        for s in range(NSH):
            sl = slice(s * 512, (s + 1) * 512)
            dm = _dot_nt(dh_ref[...], wo_ref[sl, :])
            ga = _sigmoid(ga_ref[:, sl] + ba_ref[:, sl])
            gb = _sigmoid(gb_ref[:, sl] + bb_ref[:, sl])
            dya = (dm * ga).astype(BF16)
            dyb = (dm * gb).astype(BF16)
            dza = dm * _dot(oav, wpa_ref[s]) * ga * (1.0 - ga)
            dzb = dm * _dot(obv, wpb_ref[s]) * gb * (1.0 - gb)
            dya_ref[:, sl], dyb_ref[:, sl] = dya, dyb
            dga_ref[:, sl], dgb_ref[:, sl] = dza.astype(BF16), dzb.astype(BF16)
            dba_ref[:, sl] += jnp.sum(dza, axis=0, keepdims=True)
            dbb_ref[:, sl] += jnp.sum(dzb, axis=0, keepdims=True)
            doa += _dot_nt(dya, wpa_ref[s])
            dob += _dot_nt(dyb, wpb_ref[s])
        doa_ref[...], dob_ref[...] = doa, dob

    row = lambda w: pl.BlockSpec((tm, w), lambda i: (i, 0))
    vec = pl.BlockSpec((1, D), lambda i: (0, 0))
    wp = _resident((NSH, 512, 512), lambda i: (0, 0, 0))
    return pl.pallas_call(
        body, name="mix_bwd",
        out_shape=[SDS((T, D), BF16)] * 4 + [SDS((T, 512), F32)] * 2 + [SDS((1, D), F32)] * 2, grid=(T // tm,),
        in_specs=[row(D), _resident((D, D), lambda i: (0, 0)), row(512), row(512),
                  pl.BlockSpec((tm, D), lambda i: (i, 3)), pl.BlockSpec((tm, D), lambda i: (i, 4)),
                  pl.BlockSpec((1, D), lambda i: (0, 0)), pl.BlockSpec((1, D), lambda i: (0, 1)), wp, wp],
        out_specs=[row(D)] * 4 + [row(512)] * 2 + [vec] * 2,
        compiler_params=_params(1))(dh16, w_out, oa, ob16, proj, proj, b_gate, b_gate, wpa, wpb)


def _comb_bwd(doa, os, ws, deps=()):
    tm = 512

    def body(d_ref, o0, o1, o2, w0, w1, w2, cc_ref):
        prod = d_ref[...] * (w0[...] * o0[...] + w1[...] * o1[...] + w2[...] * o2[...])
        for h in range(4):
            sl = slice(h * HD, (h + 1) * HD)
            cc_ref[:, sl] = jnp.broadcast_to(jnp.sum(prod[:, sl], axis=-1, keepdims=True), (tm, HD))

    spec = pl.BlockSpec((tm, 512), lambda i: (i, 0))
    return pl.pallas_call(
        _after(body, deps), name="comb_bwd", out_shape=SDS((T, 512), F32), grid=(T // tm,),
        in_specs=[DEP_SPEC] * len(deps) + [spec] * 7, out_specs=spec,
        compiler_params=_params(1))(*deps, doa, *os, *ws)


def _attn_a_bwd(qkn, proj, doa, lse, w, cc, g):
    dil = DILS[g]
    m = T // dil
    nb = m // 128

    def body(q_ref, k_ref, v_ref, d_ref, l_ref, w_ref, c_ref, dqk_ref, dv_ref,
             qf, kf, qp, kp, vp, dp, lp, wsub, cp, dqb, dkp, dvp):
        qf[...] = q_ref[...].astype(F32)
        kf[...] = k_ref[...].astype(F32)
        for r in range(dil):
            sub = _residue_rows(r, m, dil)
            qp[...] = qf[sub, :].astype(BF16)
            _fill_padded(kp, kf[sub, :], m)
            _fill_padded(vp, v_ref[sub, :], m)
            dp[...] = d_ref[sub, :].astype(BF16)
            lp[...], wsub[...], cp[...] = l_ref[sub, :], w_ref[sub, :], c_ref[sub, :]
            dkp[...] = jnp.zeros_like(dkp)
            dvp[...] = jnp.zeros_like(dvp)

            def block(b, carry):
                q0 = pl.multiple_of(b * 128, 128)
                rows = pl.ds(q0, 128)
                win = pl.ds(q0, 256)
                qb, kw, vw = qp[rows, :], kp[win, :], vp[win, :]
                s = _dot_nt(qb, kw) * SCALE
                s = jnp.where(_band_mask(q0, m), s, NEG)
                wp = _wide(wsub[rows, :], 2) * jnp.exp(s - _wide(lp[rows, :], 2))
                dob = dp[rows, :]
                ds = (wp * (_dot_nt(dob, vw) - _wide(cp[rows, :], 2))).astype(BF16)
                dqb[rows, :] = _dot(ds, kw) * SCALE
                dkp[win, :] += _dot_tn(ds, qb) * SCALE
                dvp[win, :] += _dot_tn(wp.astype(BF16), dob)
                return carry

            lax.fori_loop(0, nb, block, 0, unroll=min(nb, 2))
            dqk_ref.at[0][sub, :] = dqb[...]
            dqk_ref.at[1][sub, :] = dkp[64:64 + m, :]
            dv_ref[sub, :] = dvp[64:64 + m, :]

    q_blk, k_blk, v_blk, blk = _head_blocks(g)
    sub16 = pltpu.VMEM((m, HD), BF16)
    sub32 = pltpu.VMEM((m, HD), F32)
    return pl.pallas_call(
        body, name=f"attn_a_bwd_{g}", out_shape=[SDS((2, T, 512), F32), SDS((T, 512), F32)], grid=(4,),
        in_specs=[q_blk, k_blk, v_blk, blk, blk, blk, blk],
        out_specs=[pl.BlockSpec((2, T, HD), lambda h: (0, 0, h)), blk],
        scratch_shapes=[pltpu.VMEM((T, HD), F32), pltpu.VMEM((T, HD), F32), sub16,
                        pltpu.VMEM((m + 128, HD), BF16), pltpu.VMEM((m + 128, HD), BF16), sub16,
                        sub32, sub32, sub32, sub32,
                        pltpu.VMEM((m + 128, HD), F32), pltpu.VMEM((m + 128, HD), F32)],
        compiler_params=_params(1))(qkn, qkn, proj, doa, lse, w, cc)


def _attn_b_bwd(qkn, proj, dob, ob, lse, bias, deps=()):
    def body(q_ref, k_ref, v_ref, d_ref, o_ref, l_ref, bias_ref, dqk_ref, dv_ref, drpb_ref, vb, dk_acc, dv_acc, a_acc):
        vb[...] = v_ref[...].astype(BF16)
        dk_acc[...] = jnp.zeros_like(dk_acc)
        dv_acc[...] = jnp.zeros_like(dv_acc)
        a_acc[...] = jnp.zeros_like(a_acc)

        def row(r, carry):
            start, off = _nbr_window(r)
            rows = pl.ds(pl.multiple_of(r * GRID_W, GRID_W), GRID_W)
            win = pl.ds(pl.multiple_of(start * GRID_W, GRID_W), 512)
            qr, kw, vw = q_ref[rows, :], k_ref[win, :], vb[win, :]
            s = _dot_nt(qr, kw) * SCALE + bias_ref[off]
            p = jnp.exp(s - _wide(l_ref[rows, :], 4))
            dov = d_ref[rows, :]
            delta = jnp.sum(dov * o_ref[rows, :], axis=-1, keepdims=True)
            do16 = dov.astype(BF16)
            ds = p * (_dot_nt(do16, vw) - delta)
            a_acc[off] += ds
            ds16 = ds.astype(BF16)
            dqk_ref[0, rows, :] = _dot(ds16, kw) * SCALE
            dk_acc[win, :] += _dot_tn(ds16, qr) * SCALE
            dv_acc[win, :] += _dot_tn(p.astype(BF16), do16)
            return carry

        lax.fori_loop(0, T // GRID_W, row, 0, unroll=2)
        dqk_ref[1] = dk_acc[...]
        dv_ref[...] = dv_acc[...]

        lane = lax.broadcasted_iota(jnp.int32, (16, HD), 1)
        rowi = lax.broadcasted_iota(jnp.int32, (16, HD), 0)
        low = (lane >= GRID_W - WIN_C) & (lane < GRID_W + WIN_C - 1)
        high = (lane >= HD - WIN_C) | (lane < WIN_C - 1)
        flip = (lax.broadcasted_iota(jnp.int32, (GRID_W, GRID_W), 0)
                + lax.broadcasted_iota(jnp.int32, (GRID_W, GRID_W), 1) == GRID_W - 1).astype(BF16)
        out = jnp.zeros((16, HD), F32)
        for d in range(14):
            acc = None
            for off in range(8):
                if 0 <= d - off <= 6 and (d - off) % 2 == 0:
                    jj = (d - off) // 2
                    piece = a_acc[off, :, jj * HD:(jj + 1) * HD]
                    acc = piece if acc is None else acc + piece
            hi = acc.astype(BF16)
            lo = (acc - hi.astype(F32)).astype(BF16)
            rev = _dot(flip, hi) + _dot(flip, lo)
            v = jnp.sum(pltpu.roll(rev, 0, 1, stride=1, stride_axis=0), axis=0, keepdims=True)
            v = jnp.broadcast_to(v, (16, HD))
            out = out + jnp.where((rowi == d) & low, v, 0.0)
            out = out + jnp.where(rowi == d + 1, pltpu.roll(jnp.where(high, v, 0.0), GRID_W, 1), 0.0)
        drpb_ref[...] = out

    blk = pl.BlockSpec((T, HD), lambda h: (0, h))
    return pl.pallas_call(
        _after(body, deps), name="attn_b_bwd",
        out_shape=[SDS((2, T, 512), F32), SDS((T, 512), F32), SDS((4, 16, HD), F32)], grid=(4,),
        in_specs=[DEP_SPEC] * len(deps) + [
            pl.BlockSpec((T, HD), lambda h: (0, NHA + h)),
            pl.BlockSpec((T, HD), lambda h: (0, NH + NHA + h)),
            pl.BlockSpec((T, HD), lambda h: (0, 2 * NH + NHA + h)), blk, blk, blk,
            pl.BlockSpec((None, 8, GRID_W, 512), lambda h: (h, 0, 0, 0))],
        out_specs=[pl.BlockSpec((2, T, HD), lambda h: (0, 0, h)), blk,
                   pl.BlockSpec((None, 16, HD), lambda h: (h, 0, 0))],
        scratch_shapes=[pltpu.VMEM((T, HD), BF16), pltpu.VMEM((T, HD), F32), pltpu.VMEM((T, HD), F32),
                        pltpu.VMEM((8, GRID_W, 512), F32)],
        compiler_params=_params(1))(*deps, qkn, qkn, proj, dob, ob, lse, bias)


def _qk_bwd(proj, nw, cos, sin, dqk_groups, dqk_b, dvs, dga, dgb):
    tm = 256

    def body(p_ref, w_ref, cos_ref, sin_ref, d0, d1, d2, d3, v0, v1, v2, v3, ga_ref, gb_ref, o_ref, dn_ref):
        j, i = pl.program_id(0), pl.program_id(1)

        @pl.when((j < 2) & (i == 0))
        def _():
            dn_ref[...] = jnp.zeros_like(dn_ref)

        @pl.when(j < 2)
        def _():
            cv, sv = cos_ref[...], sin_ref[...]
            srcs = (d0, d1, d2, d3)
            dna = jnp.zeros((1, HD), F32)
            dnb = jnp.zeros((1, HD), F32)
            for h in range(NH):
                sl = slice(h * HD, (h + 1) * HD)
                dz = srcs[h // 4][:, (h % 4) * HD:(h % 4 + 1) * HD]
                if h < NHA:
                    dz = dz * cv + pltpu.roll(dz * sv, 64, 1)
                dx, dg = _norm_bwd(p_ref[:, sl], dz, w_ref[:, sl])
                o_ref[:, sl] = dx.astype(BF16)
                if h < NHA:
                    dna += dg
                else:
                    dnb += dg
            dn_ref[0:1, :] += dna
            dn_ref[1:2, :] += dnb

        @pl.when(j == 2)
        def _():
            for s, v_ref in enumerate((v0, v1, v2, v3)):
                o_ref[:, s * 512:(s + 1) * 512] = v_ref[...].astype(BF16)

        @pl.when(j == 3)
        def _():
            o_ref[...] = ga_ref[...]

        @pl.when(j == 4)
        def _():
            o_ref[...] = gb_ref[...]

    def rows(used):
        return lambda j, i: (jnp.where(used(j), i, 0), 0)

    qk = lambda j: j < 2
    dspec = pl.BlockSpec((None, tm, 512), lambda j, i: (jnp.minimum(j, 1), jnp.where(j < 2, i, 0), 0))
    vspec = pl.BlockSpec((tm, 512), rows(lambda j: j == 2))
    return pl.pallas_call(
        body, name="qk_bwd", out_shape=[SDS((T, DIN), BF16), SDS((2, 8, HD), F32)], grid=(5, T // tm),
        in_specs=[pl.BlockSpec((tm, D), lambda j, i: (jnp.where(j < 2, i, 0), jnp.minimum(j, 1))),
                  pl.BlockSpec((None, 1, D), lambda j, i: (jnp.minimum(j, 1), 0, 0)),
                  pl.BlockSpec((tm, HD), rows(qk)), pl.BlockSpec((tm, HD), rows(qk)),
                  dspec, dspec, dspec, dspec, vspec, vspec, vspec, vspec,
                  pl.BlockSpec((tm, D), rows(lambda j: j == 3)), pl.BlockSpec((tm, D), rows(lambda j: j == 4))],
        out_specs=[pl.BlockSpec((tm, D), lambda j, i: (i, j)),
                   pl.BlockSpec((None, 8, HD), lambda j, i: (jnp.minimum(j, 1), 0, 0))],
        compiler_params=_params(2))(proj, nw, cos, sin, *dqk_groups, dqk_b, *dvs, dga, dgb)


def _in_proj_bwd(dproj, w_in, x, dh1, g, deps=()):
    tm, tk = 512, 1280
    per = (DIN // NSH) // tk
    nk = DIN // tk

    def body(dp_ref, w_ref, x_ref, dh_ref, g_ref, dx_ref, dg_ref, acc):
        i, k = pl.program_id(0), pl.program_id(1)

        @pl.when(k == 0)
        def _():
            acc[...] = jnp.zeros_like(acc)

        @pl.when((k == 0) & (i == 0))
        def _():
            dg_ref[...] = jnp.zeros_like(dg_ref)

        acc[...] += _dot_nt(dp_ref[...], w_ref[...])

        @pl.when(k == nk - 1)
        def _():
            dx, dg = _norm_bwd(x_ref[...], acc[...], g_ref[...])
            dx_ref[...] = dh_ref[...] + dx
            dg_ref[...] += dg

    row = pl.BlockSpec((tm, D), lambda i, k: (i, 0))
    vec = pl.BlockSpec((1, D), lambda i, k: (0, 0))
    return pl.pallas_call(
        _after(body, deps), name="in_proj_bwd", out_shape=[SDS((T, D), F32), SDS((1, D), F32)], grid=(T // tm, nk),
        in_specs=[DEP_SPEC] * len(deps) + [
            pl.BlockSpec((tm, tk), lambda i, k: (i, k)),
            pl.BlockSpec((None, D, tk), lambda i, k: (k // per, 0, k % per)), row, row, vec],
        out_specs=[row, vec], scratch_shapes=[pltpu.VMEM((tm, D), F32)],
        compiler_params=_params(2))(*deps, dproj, w_in, x, dh1, g)


def _grad_w(name, a, g, shard_rows, rows, cols, tr, tc):
    ni, nj = rows // tr, cols // tc
    if shard_rows:
        a_map, g_map = (lambda s, i, j: (0, s * ni + i)), (lambda s, i, j: (0, j))
    else:
        a_map, g_map = (lambda s, i, j: (0, i)), (lambda s, i, j: (0, s * nj + j))

    def body(a_ref, g_ref, o_ref):
        o_ref[...] = _dot_tn(a_ref[...], g_ref[...]).astype(BF16)

    return pl.pallas_call(
        body, name=name, out_shape=SDS((NSH, rows, cols), BF16), grid=(NSH, ni, nj),
        in_specs=[pl.BlockSpec((T, tr), a_map), pl.BlockSpec((T, tc), g_map)],
        out_specs=pl.BlockSpec((None, tr, tc), lambda s, i, j: (s, i, j)), compiler_params=_params(3))(a, g)


def _grad_w_in_half(name, xn, dproj, place, for_sibling, deps=()):
    tr, tc = D // 2, 1280
    nj = (DIN // NSH) // tc

    def body(*refs):
        a_ref, g_ref, o_ref = refs[-3:]
        o_ref[...] = _dot_tn(a_ref[...], g_ref[...]).astype(BF16)

    half = (lambda p: 1 - p[1]) if for_sibling else (lambda p: p[1])
    return pl.pallas_call(
        body, name=name, out_shape=SDS((NSH, tr, DIN // NSH), BF16),
        grid_spec=pltpu.PrefetchScalarGridSpec(
            num_scalar_prefetch=1, grid=(NSH, nj),
            in_specs=[DEP_SPEC] * len(deps) + [pl.BlockSpec((T, tr), lambda s, j, p: (0, half(p))),
                                               pl.BlockSpec((T, tc), lambda s, j, p: (0, s * nj + j))],
            out_specs=pl.BlockSpec((None, tr, tc), lambda s, j, p: (s, 0, j))),
        compiler_params=_params(2))(place, *deps, xn, dproj)


def _adamw(w, g, m, v):
    m = B1 * m + (1.0 - B1) * g
    v = B2 * v + (1.0 - B2) * (g * g)
    m_hat = m / (1.0 - B1 ** STEP)
    v_hat = v / (1.0 - B2 ** STEP)
    delta = -LR * (m_hat / (jnp.sqrt(v_hat) + AEPS) + WD * w)
    return delta, m, v


def _sum_halves(name, place, grads, theirs):
    _, rows, cols = theirs.shape
    tr = _row_tile(rows, cols, 1 << 20)

    def body(place_ref, a_ref, b_ref, o_ref):
        o_ref[...] = (a_ref[...].astype(F32) + b_ref[...].astype(F32)).astype(BF16)

    spec = pl.BlockSpec((None, tr, cols), lambda s, i, p: (s, i, 0))
    mine = spec if grads.ndim == 3 else pl.BlockSpec((None, None, tr, cols), lambda s, i, p: (s, p[1], i, 0))
    return pl.pallas_call(
        body, name=name, out_shape=SDS(theirs.shape, BF16),
        grid_spec=pltpu.PrefetchScalarGridSpec(
            num_scalar_prefetch=1, grid=(NSH, rows // tr), in_specs=[mine, spec], out_specs=spec),
        compiler_params=_params(2))(place, grads, theirs)


def _sum_landed(name, place, part, landed):
    _, rows, cols = part.shape
    tr = _row_tile(rows, cols, 1 << 20)

    def body(place_ref, p_ref, l_ref, o_ref):
        o_ref[...] = ((p_ref[...].astype(F32) + l_ref[0].astype(F32)) + l_ref[1].astype(F32)) + l_ref[2].astype(F32)

    return pl.pallas_call(
        body, name=name, out_shape=SDS((2, rows, cols), F32),
        grid_spec=pltpu.PrefetchScalarGridSpec(
            num_scalar_prefetch=1, grid=(rows // tr,),
            in_specs=[pl.BlockSpec((None, tr, cols), lambda i, p: (p[0], i, 0)),
                      pl.BlockSpec((3, tr, cols), lambda i, p: (0, i, 0))],
            out_specs=pl.BlockSpec((None, tr, cols), lambda i, p: (p[1], i, 0))),
        compiler_params=_params(1))(place, part, landed)


def _adam_shard(name, g, w, m, v):
    rows, cols = w.shape
    tr = _row_tile(rows, cols, 1 << 19)

    def body(g_ref, w_ref, m_ref, v_ref, go_ref, d_ref, nm_ref, nv_ref):
        g = g_ref[...]
        go_ref[...] = g
        d_ref[...], nm_ref[...], nv_ref[...] = _adamw(w_ref[...], g, m_ref[...], v_ref[...])

    spec = pl.BlockSpec((tr, cols), lambda i: (i, 0))
    return pl.pallas_call(
        body, name=name, out_shape=[SDS((rows, cols), F32)] * 4, grid=(rows // tr,),
        in_specs=[spec] * 4, out_specs=[spec] * 4, compiler_params=_params(1))(g, w, m, v)


def _adam_small(gathered, w, m, v):
    def body(g_ref, w_ref, m_ref, v_ref, go_ref, d_ref, nm_ref, nv_ref):
        g = g_ref[0:SMALL_ROWS, :]
        for dev in range(1, 8):
            g = g + g_ref[dev * SMALL_ROWS:(dev + 1) * SMALL_ROWS, :]
        go_ref[...] = g
        d_ref[...], nm_ref[...], nv_ref[...] = _adamw(w_ref[...], g, m_ref[...], v_ref[...])

    return pl.pallas_call(body, name="adam_small", out_shape=[SDS((SMALL_ROWS, HD), F32)] * 4)(gathered, w, m, v)


SMALL = (("norm_mix", (1, D)), ("b_gate", (1, 2 * D)), ("q_norm_a", (1, HD)), ("k_norm_a", (1, HD)),
         ("q_norm_b", (1, HD)), ("k_norm_b", (1, HD)), ("rpb_b", (1, 4, 15, 31)), ("norm_ffn", (1, D)))


def _pack_small(vals):
    pieces = []
    for (name, shape), val in zip(SMALL, vals):
        flat = val.reshape(-1)
        pad = (-flat.shape[0]) % HD
        pieces.append(jnp.pad(flat, (0, pad)).reshape(-1, HD))
    packed = jnp.concatenate(pieces, axis=0)
    return jnp.pad(packed, ((0, SMALL_ROWS - packed.shape[0]), (0, 0)))


def _unpack_small(packed):
    out, row = [], 0
    for name, shape in SMALL:
        size = int(np.prod(shape))
        nrows = -(-size // HD)
        out.append(packed[row:row + nrows].reshape(-1)[:size].reshape(shape))
        row += nrows
    return out


def kernel(x, norm_mix, w_in, b_gate, q_norm_a, k_norm_a, q_norm_b, k_norm_b, rpb_b, w_proj_a, w_proj_b, w_out, norm_ffn, w_up, w_down, loss_target, m_norm_mix, m_w_in, m_b_gate, m_q_norm_a, m_k_norm_a, m_q_norm_b, m_k_norm_b, m_rpb_b, m_w_proj_a, m_w_proj_b, m_w_out, m_norm_ffn, m_w_up, m_w_down, v_norm_mix, v_w_in, v_b_gate, v_q_norm_a, v_k_norm_a, v_q_norm_b, v_k_norm_b, v_rpb_b, v_w_proj_a, v_w_proj_b, v_w_out, v_norm_ffn, v_w_up, v_w_down):
    big_names = ("w_in", "w_proj_a", "w_proj_b", "w_out", "w_up", "w_down")
    big_w = [a[0] for a in (w_in, w_proj_a, w_proj_b, w_out, w_up, w_down)]
    big_m = [a[0] for a in (m_w_in, m_w_proj_a, m_w_proj_b, m_w_out, m_w_up, m_w_down)]
    big_v = [a[0] for a in (v_w_in, v_w_proj_a, v_w_proj_b, v_w_out, v_w_up, v_w_down)]
    x2, target = x[0], loss_target[0]

    place = jnp.stack([2 * lax.axis_index("x") + lax.axis_index("y"), lax.axis_index("c")]).astype(jnp.int32)
    groups = ((0,), (1, 2, 3), (4,), (5,))
    started = []
    for j, grp in enumerate(groups):
        deps = (started[0][4],) if j else ()
        placed = [_cast_into_place(big_w[i], "cast_" + big_names[i], place, deps) for i in grp]
        started.append(_gather_start(f"gather_start_{j}", placed))

    def whole(fulls):
        return [f.reshape(NSH, 2 * f.shape[2], f.shape[3]) for f in fulls]

    def forward_begin(j, after):
        send, recv, _, fulls, _ = started[j]
        fulls = _gather_wait(f"gather_wait_{j}", send, recv, fulls, after)
        send, recv, _, fulls, token = _forward_start(f"forward_start_{j}", fulls)
        return (send, recv, fulls), token

    def forward_end(j, state, after):
        return whole(_forward_wait(f"forward_wait_{j}", *state, after))

    def as_halves(grads):
        return [g.reshape(NSH, 2, g.shape[1] // 2, g.shape[2]) for g in grads]

    def reduce_start(j, grads, theirs):
        parts = [_sum_halves(f"sum_halves_{j}_{i}", place, a, b) for i, (a, b) in enumerate(zip(grads, theirs))]
        send, recv, parts, lands, token = _reduce_start(f"reduce_start_{j}", parts)
        return (send, recv, parts, lands), token

    def exchange_begin(j, grads):
        send, recv, grads, lands, token = _exchange_start(f"exchange_start_{j}", as_halves(grads))
        return (send, recv, grads, lands), token

    def exchange_end(j, state, after):
        return reduce_start(j, *_exchange_wait(f"exchange_wait_{j}", *state, after))

    big_out = {}

    def share_begin(j, state, after):
        send, recv, parts, lands = state
        parts, lands = _reduce_wait(f"reduce_wait_{j}", send, recv, parts, lands, after)
        sums = [_sum_landed(f"sum_landed_{j}_{i}", place, p, l) for i, (p, l) in enumerate(zip(parts, lands))]
        send, recv, _, sums, token = _share_start(f"share_start_{j}", sums)
        return (send, recv, sums), token

    def share_end(j, state, after):
        for idx, g in zip(groups[j], _share_wait(f"share_wait_{j}", *state, after)):
            g = g.reshape(big_w[idx].shape)
            big_out[idx] = _adam_shard("adam_" + big_names[idx], g, big_w[idx], big_m[idx], big_v[idx])
        return big_out[groups[j][-1]][1]

    proj, xn = _norm_in_proj_own(x2, norm_mix, whole(started[0][3])[0], place)
    send, recv, _, win, _ = started[0]
    win = _gather_wait("gather_wait_0", send, recv, win, (proj, *[s[4] for s in started[1:]]))
    (win_f,) = whole(_gather_finish("gather_finish_0", win))
    proj = _in_proj_rest("in_proj_rest", xn, win_f, proj, place, (2, 1, 3))
    cos, sin = _rope_tables()
    nw = jnp.stack([jnp.concatenate([jnp.tile(q_norm_a, (1, NHA)), jnp.tile(q_norm_b, (1, NH - NHA))], axis=1),
                    jnp.concatenate([jnp.tile(k_norm_a, (1, NHA)), jnp.tile(k_norm_b, (1, NH - NHA))], axis=1)])
    qkn = _qk_prep(proj, nw, cos, sin)
    fw1, token = forward_begin(1, (qkn,))
    fwd_a = [_attn_a_fwd(qkn, proj, g) for g in range(3)]
    os, ls = [f[0] for f in fwd_a], [f[1] for f in fwd_a]
    fw2, token = forward_begin(2, (os[2], token))
    ob, lse_b, bias = _attn_b_fwd(qkn, proj, _rpb_rows(rpb_b[0]))
    oa, w0, w1, w2 = _comb_fwd(os, ls)
    ws = [w0, w1, w2]
    wpa_f, wpb_f, wout_f = forward_end(1, fw1, (oa, token))
    wout_f = wout_f.reshape(D, D)
    mixed, ob16 = _mix_fwd(oa, ob, proj, b_gate, wpa_f, wpb_f)
    h1, hn = _out_proj_fwd(mixed, wout_f, x2, norm_ffn)
    fw3, token = forward_begin(3, (h1,))
    (wup_f,) = forward_end(2, fw2, (hn, token))
    usq, u = _ffn_up(hn, wup_f)
    (wdown_f,) = forward_end(3, fw3, (u,))
    wdown_f = wdown_f.reshape(DFF, D)
    dy, dy16, loss_parts = _ffn_down_loss(usq, wdown_f, h1, target)
    loss = lax.psum(jnp.sum(loss_parts[:, 0, 0]), ("x", "y", "c"))

    g_down = _grad_w("grad_w_down", usq, dy16, True, DFF // NSH, D, 1024, 1024)
    ex_down, token = exchange_begin(3, [g_down])
    du = _ffn_down_bwd(dy16, wdown_f, u, deps=(token,))
    g_up = _grad_w("grad_w_up", hn, du, False, D, DFF // NSH, 1024, 1024)
    red_down, token = exchange_end(3, ex_down, (g_up,))
    ex_up, token_up = exchange_begin(2, [g_up])
    dh1, dh16, d_norm_ffn = _ffn_up_bwd(du, wup_f, h1, dy, norm_ffn, deps=(token, token_up))
    dya, dyb, dga, dgb, doa, dob, dba, dbb = _mix_bwd(dh16, wout_f, oa, ob16, proj, b_gate, wpa_f, wpb_f)
    g_out = _grad_w("grad_w_out", mixed, dh16, True, D // NSH, D, 512, 1024)
    g_pa = _grad_w("grad_w_proj_a", oa, dya, False, 512, 512, 512, 512)
    g_pb = _grad_w("grad_w_proj_b", ob16, dyb, False, 512, 512, 512, 512)
    red_up, token = exchange_end(2, ex_up, (g_out,))
    ex_mid, token_mid = exchange_begin(1, [g_pa, g_pb, g_out])
    cc = _comb_bwd(doa, os, ws, deps=(token, token_mid))
    bwd_a = [_attn_a_bwd(qkn, proj, doa, ls[g], ws[g], cc, g) for g in range(3)]
    red_mid, token = exchange_end(1, ex_mid, (bwd_a[2][1],))
    dqk_b, dv_b, drpb_t = _attn_b_bwd(qkn, proj, dob, ob, lse_b, bias, deps=(token,))
    dproj, dn = _qk_bwd(proj, nw, cos, sin, [b[0] for b in bwd_a], dqk_b, [b[1] for b in bwd_a] + [dv_b], dga, dgb)
    g_in_theirs = _grad_w_in_half("grad_w_in_for_sibling", xn, dproj, place, True)
    send, recv, g_in_theirs, lands, token = _exchange_start("exchange_start_0", [g_in_theirs], sliced=False)
    g_in_mine = _grad_w_in_half("grad_w_in_own", xn, dproj, place, False, deps=(token,))
    _, theirs = _exchange_wait("exchange_wait_0", send, recv, g_in_theirs, lands, (g_in_mine,), sliced=False)
    red_in, token = reduce_start(0, [g_in_mine], theirs)
    grad_x, d_norm_mix = _in_proj_bwd(dproj, win_f, x2, dh1, norm_mix, deps=(token,))

    sh_down, token = share_begin(3, red_down, (grad_x,))
    sh_up, token = share_begin(2, red_up, (token,))
    done = share_end(3, sh_down, (token,))
    sh_mid, token = share_begin(1, red_mid, (done,))
    done = share_end(2, sh_up, (token,))
    sh_in, token = share_begin(0, red_in, (done,))
    done = share_end(1, sh_mid, (token,))
    done = share_end(0, sh_in, (done,))

    d_rpb = drpb_t[:, :15, GRID_W - WIN_C:GRID_W + WIN_C - 1]
    small_g = [d_norm_mix, jnp.concatenate([dba, dbb], axis=1), dn[0, 0], dn[1, 0], dn[0, 1], dn[1, 1], d_rpb, d_norm_ffn]
    gathered_small = _allgather_small(_pack_small(small_g), done)
    small_w = (norm_mix, b_gate, q_norm_a, k_norm_a, q_norm_b, k_norm_b, rpb_b, norm_ffn)
    small_m = (m_norm_mix, m_b_gate, m_q_norm_a, m_k_norm_a, m_q_norm_b, m_k_norm_b, m_rpb_b, m_norm_ffn)
    small_v = (v_norm_mix, v_b_gate, v_q_norm_a, v_k_norm_a, v_q_norm_b, v_k_norm_b, v_rpb_b, v_norm_ffn)
    small_out = [_unpack_small(p) for p in
                 _adam_small(gathered_small, _pack_small(small_w), _pack_small(small_m), _pack_small(small_v))]

    order = ("norm_mix", "w_in", "b_gate", "q_norm_a", "k_norm_a", "q_norm_b", "k_norm_b", "rpb_b",
             "w_proj_a", "w_proj_b", "w_out", "norm_ffn", "w_up", "w_down")
    small_idx = {name: i for i, (name, _) in enumerate(SMALL)}
    outs = []
    for kind in range(4):
        for name in order:
            if name in small_idx:
                outs.append(small_out[kind][small_idx[name]])
            else:
                outs.append(big_out[big_names.index(name)][kind][None])
    return (loss, grad_x[None], *outs)
```

```python
import functools

import numpy as np
import jax
import jax.numpy as jnp
from jax import lax
from jax.experimental import pallas as pl
from jax.experimental.pallas import tpu as pltpu

F32, BF16 = jnp.float32, jnp.bfloat16
SDS = jax.ShapeDtypeStruct
MESH = pl.DeviceIdType.MESH

T = 2048
D = 2048
HD = 128
NH, NHA = 16, 12
DIN = 10240
DFF = 8192
NSH = 4
DILS = (1, 4, 16)
EPS = 1e-6
NEG = -1e30
SCALE = HD ** -0.5
GRID_W, WIN_R, WIN_C = 64, 8, 16
VMEM_LIMIT = 56 * 1024 * 1024
B1, B2, LR, AEPS, WD, STEP = 0.9, 0.999, 0.001, 1e-08, 0.01, 10
SMALL_ROWS = 88


def _dot(a, b):
    return jnp.dot(a, b, preferred_element_type=F32)


def _dot_nt(a, b):
    return lax.dot_general(a, b, (((1,), (1,)), ((), ())), preferred_element_type=F32)


def _dot_tn(a, b):
    return lax.dot_general(a, b, (((0,), (0,)), ((), ())), preferred_element_type=F32)


def _params(n):
    return pltpu.CompilerParams(dimension_semantics=("arbitrary",) * n, vmem_limit_bytes=VMEM_LIMIT)


def _resident(shape, index_map):
    return pl.BlockSpec(shape, index_map, pipeline_mode=pl.Buffered(1))


def _sigmoid(z):
    return 1.0 / (1.0 + jnp.exp(-z))


def _wide(v, n):
    return jnp.concatenate([v] * n, axis=1)


def _row_tile(rows, cols, elems):
    tr = 16
    while tr * 2 <= rows and tr * 2 * cols <= elems:
        tr *= 2
    return tr


def _place():
    x, y, c = lax.axis_index("x"), lax.axis_index("y"), lax.axis_index("c")
    peers = [(1 - x, y), (x, 1 - y), (1 - x, 1 - y)]
    return x, y, c, peers


def _cast_into_place(w, name, place, deps=()):
    rows, cols = w.shape
    hr = rows // 2
    tr = min(hr, 256)
    per = hr // tr

    def body(*refs):
        w_ref, o_ref = refs[-2:]
        o_ref[...] = w_ref[...].astype(BF16)

    return pl.pallas_call(
        body, name=name, out_shape=SDS((NSH, 2, hr, cols), BF16),
        grid_spec=pltpu.PrefetchScalarGridSpec(
            num_scalar_prefetch=1, grid=(2, per),
            in_specs=[DEP_SPEC] * len(deps) + [pl.BlockSpec((tr, cols), lambda h, i, p: (h * per + i, 0))],
            out_specs=pl.BlockSpec((None, None, tr, cols), lambda h, i, p: (p[0], h, i, 0))),
        compiler_params=_params(2))(place, *deps, w)


ANY_SPEC = pl.BlockSpec(memory_space=pl.ANY)
HBM_SPEC = pl.BlockSpec(memory_space=pltpu.HBM)
SEM_SPEC = pl.BlockSpec(memory_space=pltpu.SEMAPHORE)
DEP_SPEC = pl.BlockSpec((8, 128), lambda *_: (0, 0))
EFFECT = pltpu.SideEffectType.DATAFLOW_SIDE_EFFECTING


def _after(body, deps):
    n = len(deps)
    return (lambda *refs: body(*refs[n:])) if n else body


SIBLING_BARRIER = 1


def _split_start(name, srcs, lands, n_copies, issue, sibling_only=False, after=()):
    n, m, d = len(srcs), len(lands), len(after)

    def body(*refs):
        if sibling_only:
            x, y, c, _ = _place()
            barrier = pltpu.get_barrier_semaphore()
            pl.semaphore_signal(barrier, inc=1, device_id=(x, y, 1 - c), device_id_type=MESH)
            pl.semaphore_wait(barrier, 1)
        issue(refs[:n], refs[n:n + m], refs[n + m + d], refs[n + m + d + 1])
        refs[-1][...] = jnp.zeros((8, 128), F32)

    arrays = list(srcs) + list(lands)
    outs = pl.pallas_call(
        body, name=name,
        out_shape=(pltpu.SemaphoreType.DMA((n_copies,)), pltpu.SemaphoreType.DMA((n_copies,)),
                   *[pltpu.HBM(a.shape, a.dtype) for a in arrays], SDS((8, 128), F32)),
        in_specs=[HBM_SPEC] * (n + m) + [ANY_SPEC] * d,
        out_specs=(SEM_SPEC, SEM_SPEC, *[HBM_SPEC] * (n + m), pl.BlockSpec(memory_space=pltpu.VMEM)),
        input_output_aliases={i: 2 + i for i in range(n + m)},
        compiler_params=pltpu.CompilerParams(has_side_effects=EFFECT,
                                             collective_id=SIBLING_BARRIER if sibling_only else None),
    )(*[pltpu.with_memory_space_constraint(a, pltpu.HBM) for a in arrays], *after)
    return outs[0], outs[1], list(outs[2:2 + n]), list(outs[2 + n:2 + n + m]), outs[-1]


def _split_wait(name, send_sems, recv_sems, srcs, lands, after, wait):
    n, m = len(srcs), len(lands)

    def body(*refs):
        wait(refs[:n], refs[n:n + m], refs[n + m], refs[n + m + 1])

    arrays = list(srcs) + list(lands)
    outs = pl.pallas_call(
        body, name=name, out_shape=[pltpu.HBM(a.shape, a.dtype) for a in arrays],
        in_specs=[HBM_SPEC] * (n + m) + [SEM_SPEC, SEM_SPEC] + [ANY_SPEC] * len(after),
        out_specs=[HBM_SPEC] * (n + m), input_output_aliases={i: i for i in range(n + m)},
        compiler_params=pltpu.CompilerParams(has_side_effects=EFFECT),
    )(*arrays, send_sems, recv_sems, *after)
    return list(outs[:n]), list(outs[n:])


def _gather_start(name, fulls, ks=(0, 1, 2), after=()):
    def issue(srcs, dsts, send_sems, recv_sems):
        x, y, c, peers = _place()
        for i in range(len(fulls)):
            mine = dsts[i].at[2 * x + y, c]
            for k in ks:
                px, py = peers[k]
                pltpu.make_async_remote_copy(
                    src_ref=mine, dst_ref=mine, send_sem=send_sems.at[3 * i + k],
                    recv_sem=recv_sems.at[3 * i + k], device_id=(px, py, c), device_id_type=MESH).start()

    return _split_start(name, [], fulls, 3 * len(fulls), issue, after=after)


def _gather_wait(name, send_sems, recv_sems, fulls, after, ks=(0, 1, 2)):
    def wait(srcs, dsts, send_sems, recv_sems):
        x, y, c, peers = _place()
        for i in range(len(fulls)):
            for k in ks:
                px, py = peers[k]
                cp = pltpu.make_async_remote_copy(
                    src_ref=dsts[i].at[2 * x + y, c], dst_ref=dsts[i].at[2 * px + py, c],
                    send_sem=send_sems.at[3 * i + k], recv_sem=recv_sems.at[3 * i + k],
                    device_id=(px, py, c), device_id_type=MESH)
                cp.wait_send()
                cp.wait_recv()

    return _split_wait(name, send_sems, recv_sems, [], fulls, after, wait)[1]


def _gather_finish(name, fulls, ks=(0, 1, 2)):
    n = len(fulls)

    def body(*refs):
        fin, fout = refs[:n], refs[n:2 * n]
        send_sems, recv_sems = refs[2 * n:]
        x, y, c, peers = _place()

        def copy(i, k, half):
            px, py = peers[k]
            return pltpu.make_async_remote_copy(
                src_ref=fin[i].at[2 * px + py, half], dst_ref=fout[i].at[2 * px + py, half],
                send_sem=send_sems.at[3 * i + k], recv_sem=recv_sems.at[3 * i + k],
                device_id=(x, y, 1 - c), device_id_type=MESH)

        sends = [copy(i, k, c) for i in range(n) for k in ks]
        for cp in sends:
            cp.start()
        for i in range(n):
            for k in ks:
                copy(i, k, 1 - c).wait_recv()
        for cp in sends:
            cp.wait_send()

    return pl.pallas_call(
        body, name=name, out_shape=[SDS(f.shape, f.dtype) for f in fulls],
        in_specs=[ANY_SPEC] * n, out_specs=[ANY_SPEC] * n, input_output_aliases={i: i for i in range(n)},
        scratch_shapes=[pltpu.SemaphoreType.DMA((3 * n,)), pltpu.SemaphoreType.DMA((3 * n,))])(*fulls)


def _reduce_start(name, parts):
    lands = [lax.empty((3,) + p.shape[1:], p.dtype) for p in parts]

    def issue(srcs, dsts, send_sems, recv_sems):
        x, y, c, peers = _place()
        for i in range(len(parts)):
            for k, (px, py) in enumerate(peers):
                pltpu.make_async_remote_copy(
                    src_ref=srcs[i].at[2 * px + py], dst_ref=dsts[i].at[k], send_sem=send_sems.at[3 * i + k],
                    recv_sem=recv_sems.at[3 * i + k], device_id=(px, py, c), device_id_type=MESH).start()

    return _split_start(name, parts, lands, 3 * len(parts), issue)


def _reduce_wait(name, send_sems, recv_sems, parts, lands, after):
    def wait(srcs, dsts, send_sems, recv_sems):
        x, y, c, peers = _place()
        for i in range(len(parts)):
            for k, (px, py) in enumerate(peers):
                cp = pltpu.make_async_remote_copy(
                    src_ref=srcs[i].at[2 * px + py], dst_ref=dsts[i].at[k], send_sem=send_sems.at[3 * i + k],
                    recv_sem=recv_sems.at[3 * i + k], device_id=(px, py, c), device_id_type=MESH)
                cp.wait_send()
                cp.wait_recv()

    return _split_wait(name, send_sems, recv_sems, parts, lands, after, wait)


def _sibling_copy(src, dst, send_sems, recv_sems, k):
    x, y, c, _ = _place()
    return pltpu.make_async_remote_copy(src_ref=src, dst_ref=dst, send_sem=send_sems.at[k], recv_sem=recv_sems.at[k],
                                        device_id=(x, y, 1 - c), device_id_type=MESH)


def _forward_start(name, fulls):
    def issue(srcs, dsts, send_sems, recv_sems):
        x, y, c, peers = _place()
        for i in range(len(fulls)):
            for k, (px, py) in enumerate(peers):
                part = dsts[i].at[2 * px + py, c]
                _sibling_copy(part, part, send_sems, recv_sems, 3 * i + k).start()

    return _split_start(name, [], fulls, 3 * len(fulls), issue, sibling_only=True)


def _forward_wait(name, send_sems, recv_sems, fulls, after):
    def wait(srcs, dsts, send_sems, recv_sems):
        x, y, c, peers = _place()
        for i in range(len(fulls)):
            for k, (px, py) in enumerate(peers):
                cp = _sibling_copy(dsts[i].at[2 * px + py, c], dsts[i].at[2 * px + py, 1 - c], send_sems, recv_sems, 3 * i + k)
                cp.wait_send()
                cp.wait_recv()

    return _split_wait(name, send_sems, recv_sems, [], fulls, after, wait)[1]


def _exchange_start(name, grads, sliced=True):
    lands = [lax.empty((NSH,) + g.shape[-2:], g.dtype) for g in grads]

    def issue(srcs, dsts, send_sems, recv_sems):
        c = lax.axis_index("c")
        for i in range(len(grads)):
            src = srcs[i].at[:, 1 - c] if sliced else srcs[i]
            _sibling_copy(src, dsts[i], send_sems, recv_sems, i).start()

    return _split_start(name, grads, lands, len(grads), issue, sibling_only=True)


def _exchange_wait(name, send_sems, recv_sems, grads, lands, after, sliced=True):
    def wait(srcs, dsts, send_sems, recv_sems):
        c = lax.axis_index("c")
        for i in range(len(grads)):
            cp = _sibling_copy(srcs[i].at[:, 1 - c] if sliced else srcs[i], dsts[i], send_sems, recv_sems, i)
            cp.wait_send()
            cp.wait_recv()

    return _split_wait(name, send_sems, recv_sems, grads, lands, after, wait)


def _share_start(name, sums):
    def issue(srcs, dsts, send_sems, recv_sems):
        c = lax.axis_index("c")
        for i in range(len(sums)):
            _sibling_copy(dsts[i].at[c], dsts[i].at[c], send_sems, recv_sems, i).start()

    return _split_start(name, [], sums, len(sums), issue, sibling_only=True)


def _share_wait(name, send_sems, recv_sems, sums, after):
    def wait(srcs, dsts, send_sems, recv_sems):
        c = lax.axis_index("c")
        for i in range(len(sums)):
            cp = _sibling_copy(dsts[i].at[c], dsts[i].at[1 - c], send_sems, recv_sems, i)
            cp.wait_send()
            cp.wait_recv()

    return _split_wait(name, send_sems, recv_sems, [], sums, after, wait)[1]


def _allgather_small(v, after):
    m_per, n = v.shape

    def body(x_ref, after_ref, out_ref, send_sems, recv_sems, local_sem):
        x, y, c = lax.axis_index("x"), lax.axis_index("y"), lax.axis_index("c")
        me, sibling = (x, y, c), (x, y, 1 - c)
        chips = [(1 - x, y), (x, 1 - y), (1 - x, 1 - y)]

        def rows(px, py, pc):
            return out_ref.at[pl.ds((4 * px + 2 * py + pc) * m_per, m_per), :]

        def copy(k, block, to, src=None):
            return pltpu.make_async_remote_copy(
                src_ref=rows(*block) if src is None else src, dst_ref=rows(*block),
                send_sem=send_sems.at[k], recv_sem=recv_sems.at[k], device_id=to, device_id_type=MESH)

        mine = pltpu.make_async_copy(x_ref, rows(*me), local_sem)
        mine.start()
        first = [copy(0, me, sibling, src=x_ref)]
        first += [copy(1 + j, me, (*chip, c), src=x_ref) for j, chip in enumerate(chips)]
        for cp in first:
            cp.start()
        passed = [copy(4 + j, (*chip, c), sibling) for j, chip in enumerate(chips)]
        for j, chip in enumerate(chips):
            copy(1 + j, (*chip, c), me).wait_recv()
            passed[j].start()
        copy(0, sibling, me).wait_recv()
        for j, chip in enumerate(chips):
            copy(4 + j, (*chip, 1 - c), me).wait_recv()
        for cp in first + passed:
            cp.wait_send()
        mine.wait()

    return pl.pallas_call(
        body, name="allgather_small", out_shape=SDS((8 * m_per, n), v.dtype),
        in_specs=[pl.BlockSpec(memory_space=pltpu.VMEM), ANY_SPEC], out_specs=pl.BlockSpec(memory_space=pltpu.VMEM),
        scratch_shapes=[pltpu.SemaphoreType.DMA((7,)), pltpu.SemaphoreType.DMA((7,)), pltpu.SemaphoreType.DMA])(v, after)


def _norm_in_proj_own(x, g, w_full, place):
    tn, chunk = 512, 256
    per = (DIN // NSH) // tn

    def body(place_ref, x_ref, g_ref, w_ref, proj_ref, xn_ref):
        @pl.when(pl.program_id(0) == 0)
        def _():
            def norm(r, carry):
                rows = pl.ds(pl.multiple_of(r * chunk, chunk), chunk)
                xv = x_ref[rows, :]
                rs = lax.rsqrt(jnp.mean(xv * xv, axis=-1, keepdims=True) + EPS)
                xn_ref[rows, :] = (xv * rs * g_ref[...]).astype(BF16)
                return carry

            lax.fori_loop(0, T // chunk, norm, 0)

        proj_ref[...] = _dot(xn_ref[...], w_ref[...])

    return pl.pallas_call(
        body, name="norm_in_proj_own", out_shape=[SDS((T, DIN), F32), SDS((T, D), BF16)],
        grid_spec=pltpu.PrefetchScalarGridSpec(
            num_scalar_prefetch=1, grid=(per,),
            in_specs=[_resident((T, D), lambda j, p: (0, 0)),
                      pl.BlockSpec((1, D), lambda j, p: (0, 0)),
                      pl.BlockSpec((None, D, tn), lambda j, p: (p[0], 0, j))],
            out_specs=[pl.BlockSpec((T, tn), lambda j, p: (0, p[0] * per + j)),
                       pl.BlockSpec((T, D), lambda j, p: (0, 0))]),
        compiler_params=_params(1))(place, x, g, w_full)


def _in_proj_rest(name, xn, w_full, proj, place, flips):
    tn = 512
    per = (DIN // NSH) // tn

    def body(place_ref, xn_ref, w_ref, proj_in, proj_ref):
        proj_ref[...] = _dot(xn_ref[...], w_ref[...])

    def shard(j, p):
        flip = flips[0]
        for n, f in enumerate(flips[1:]):
            flip = jnp.where(j // per == n + 1, f, flip)
        return p[0] ^ flip

    return pl.pallas_call(
        body, name=name, out_shape=SDS((T, DIN), F32),
        grid_spec=pltpu.PrefetchScalarGridSpec(
            num_scalar_prefetch=1, grid=(len(flips) * per,),
            in_specs=[_resident((T, D), lambda j, p: (0, 0)),
                      pl.BlockSpec((None, D, tn), lambda j, p: (shard(j, p), 0, j % per)), ANY_SPEC],
            out_specs=pl.BlockSpec((T, tn), lambda j, p: (0, shard(j, p) * per + j % per))),
        input_output_aliases={3: 0}, compiler_params=_params(1))(place, xn, w_full, proj)


def _rope_tables():
    pos = np.arange(T, dtype=np.float32)
    inv = (10000.0 ** (-np.arange(0, HD, 2, dtype=np.float32) / HD)).astype(np.float32)
    ang = (pos[:, None] * inv[None, :]).astype(np.float32)
    cos, sin = np.cos(ang).astype(np.float32), np.sin(ang).astype(np.float32)
    return (jnp.asarray(np.concatenate([cos, cos], axis=1)), jnp.asarray(np.concatenate([-sin, sin], axis=1)))


def _qk_prep(proj, nw, cos, sin):
    tm = 256

    def body(p_ref, w_ref, cos_ref, sin_ref, o_ref):
        cv, sv = cos_ref[...], sin_ref[...]
        for h in range(NH):
            sl = slice(h * HD, (h + 1) * HD)
            xv = p_ref[:, sl]
            r = lax.rsqrt(jnp.mean(xv * xv, axis=-1, keepdims=True) + EPS)
            z = xv * r * w_ref[:, sl]
            if h < NHA:
                z = z * cv + pltpu.roll(z, 64, 1) * sv
            o_ref[:, sl] = z.astype(BF16)

    return pl.pallas_call(
        body, name="qk_prep", out_shape=SDS((T, 2 * D), BF16), grid=(T // tm, 2),
        in_specs=[pl.BlockSpec((tm, D), lambda i, j: (i, j)),
                  pl.BlockSpec((None, 1, D), lambda i, j: (j, 0, 0)),
                  pl.BlockSpec((tm, HD), lambda i, j: (i, 0)),
                  pl.BlockSpec((tm, HD), lambda i, j: (i, 0))],
        out_specs=pl.BlockSpec((tm, D), lambda i, j: (i, j)),
        compiler_params=_params(2))(proj, nw, cos, sin)


def _band_mask(q0, m):
    ii = lax.broadcasted_iota(jnp.int32, (128, 256), 0)
    jj = lax.broadcasted_iota(jnp.int32, (128, 256), 1)
    rel = jj - ii
    kpos = jj + (q0 - 64)
    return (rel >= 0) & (rel <= 128) & (kpos >= 0) & (kpos < m)


def _fill_padded(dst, src, m):
    zeros = jnp.zeros((64, HD), dst.dtype)
    dst[0:64, :] = zeros
    dst[64 + m:128 + m, :] = zeros
    dst[64:64 + m, :] = src.astype(dst.dtype)


def _residue_rows(r, m, dil):
    return pl.ds(r, m, stride=dil) if dil > 1 else slice(None)


def _head_blocks(g):
    col = lambda base: pl.BlockSpec((T, HD), lambda h: (0, base + g * 4 + h))
    return col(0), col(NH), col(2 * NH), pl.BlockSpec((T, HD), lambda h: (0, h))


def _attn_a_fwd(qkn, proj, g):
    dil = DILS[g]
    m = T // dil
    nb = m // 128

    def body(q_ref, k_ref, v_ref, o_ref, l_ref, qf, kf, qp, kp, vp, ob, lb):
        qf[...] = q_ref[...].astype(F32)
        kf[...] = k_ref[...].astype(F32)
        for r in range(dil):
            rows = _residue_rows(r, m, dil)
            qp[...] = qf[rows, :].astype(BF16)
            _fill_padded(kp, kf[rows, :], m)
            _fill_padded(vp, v_ref[rows, :], m)

            def block(b, carry):
                q0 = pl.multiple_of(b * 128, 128)
                kw, vw = kp[pl.ds(q0, 256), :], vp[pl.ds(q0, 256), :]
                s = _dot_nt(qp[pl.ds(q0, 128), :], kw) * SCALE
                s = jnp.where(_band_mask(q0, m), s, NEG)
                mx = jnp.max(s, axis=-1, keepdims=True)
                e = jnp.exp(s - mx)
                den = jnp.sum(e, axis=-1, keepdims=True)
                ob[pl.ds(q0, 128), :] = _dot((e / den).astype(BF16), vw)
                lb[pl.ds(q0, 128), :] = jnp.broadcast_to(mx + jnp.log(den), (128, HD))
                return carry

            lax.fori_loop(0, nb, block, 0, unroll=min(nb, 2))
            o_ref[rows, :] = ob[...]
            l_ref[rows, :] = lb[...]

    q_blk, k_blk, v_blk, out_blk = _head_blocks(g)
    return pl.pallas_call(
        body, name=f"attn_a_fwd_{g}", out_shape=[SDS((T, 512), F32)] * 2, grid=(4,),
        in_specs=[q_blk, k_blk, v_blk], out_specs=[out_blk] * 2,
        scratch_shapes=[pltpu.VMEM((T, HD), F32), pltpu.VMEM((T, HD), F32), pltpu.VMEM((m, HD), BF16),
                        pltpu.VMEM((m + 128, HD), BF16), pltpu.VMEM((m + 128, HD), BF16),
                        pltpu.VMEM((m, HD), F32), pltpu.VMEM((m, HD), F32)],
        compiler_params=_params(1))(qkn, qkn, proj)


def _nbr_window(r):
    start = jnp.clip(r - WIN_R // 2, 0, T // GRID_W - WIN_R)
    return start, start - r + (WIN_R - 1)


def _rpb_rows(rpb):
    zeros = jnp.zeros((4, 14, 33), F32)
    a, b = rpb[:, :14], rpb[:, 1:15]
    rows = jnp.concatenate([a[:, :, 15:31], zeros, b, zeros, a[:, :, 0:15]], axis=2)
    return jnp.pad(rows, ((0, 0), (0, 2), (0, 0)))


def _attn_b_fwd(qkn, proj, rpb_rows):
    def body(r_ref, q_ref, k_ref, v_ref, o_ref, l_ref, bias_ref, vb, pair):
        qc = lax.broadcasted_iota(jnp.int32, (GRID_W, 512), 0)
        kc = lax.broadcasted_iota(jnp.int32, (GRID_W, 512), 1) & (GRID_W - 1)
        cs = jnp.clip(qc - WIN_C // 2, 0, GRID_W - WIN_C)
        colmask = (kc >= cs) & (kc < cs + WIN_C)
        for d in range(14):
            pair[d] = pltpu.roll(jnp.broadcast_to(r_ref[d:d + 1, :], (GRID_W, HD)), 0, 1, stride=1, stride_axis=0)
        for off in range(8):
            rows = jnp.concatenate([pair[off + 2 * jj] for jj in range(4)], axis=1)
            bias_ref[off] = jnp.where(colmask, rows, NEG)
        vb[...] = v_ref[...].astype(BF16)

        def row(r, carry):
            start, off = _nbr_window(r)
            q0 = pl.multiple_of(r * GRID_W, GRID_W)
            k0 = pl.multiple_of(start * GRID_W, GRID_W)
            s = _dot_nt(q_ref[pl.ds(q0, GRID_W), :], k_ref[pl.ds(k0, 512), :]) * SCALE + bias_ref[off]
            mx = jnp.max(s, axis=-1, keepdims=True)
            e = jnp.exp(s - mx)
            den = jnp.sum(e, axis=-1, keepdims=True)
            o_ref[pl.ds(q0, GRID_W), :] = _dot((e / den).astype(BF16), vb[pl.ds(k0, 512), :])
            l_ref[pl.ds(q0, GRID_W), :] = jnp.broadcast_to(mx + jnp.log(den), (GRID_W, HD))
            return carry

        lax.fori_loop(0, T // GRID_W, row, 0, unroll=2)

    return pl.pallas_call(
        body, name="attn_b_fwd",
        out_shape=[SDS((T, 512), F32), SDS((T, 512), F32), SDS((4, 8, GRID_W, 512), F32)], grid=(4,),
        in_specs=[pl.BlockSpec((None, 16, HD), lambda h: (h, 0, 0)),
                  pl.BlockSpec((T, HD), lambda h: (0, NHA + h)),
                  pl.BlockSpec((T, HD), lambda h: (0, NH + NHA + h)),
                  pl.BlockSpec((T, HD), lambda h: (0, 2 * NH + NHA + h))],
        out_specs=[pl.BlockSpec((T, HD), lambda h: (0, h)), pl.BlockSpec((T, HD), lambda h: (0, h)),
                   pl.BlockSpec((None, 8, GRID_W, 512), lambda h: (h, 0, 0, 0))],
        scratch_shapes=[pltpu.VMEM((T, HD), BF16), pltpu.VMEM((14, GRID_W, HD), F32)],
        compiler_params=_params(1))(rpb_rows, qkn, qkn, proj)


def _comb_fwd(os, ls):
    tm = 512

    def body(o0, o1, o2, l0, l1, l2, oa_ref, w0, w1, w2):
        lv = [l0[...], l1[...], l2[...]]
        mx = jnp.maximum(jnp.maximum(lv[0], lv[1]), lv[2])
        ev = [jnp.exp(l - mx) for l in lv]
        den = ev[0] + ev[1] + ev[2]
        wv = [e / den for e in ev]
        oa_ref[...] = (wv[0] * o0[...] + wv[1] * o1[...] + wv[2] * o2[...]).astype(BF16)
        w0[...], w1[...], w2[...] = wv

    spec = pl.BlockSpec((tm, 512), lambda i: (i, 0))
    return pl.pallas_call(
        body, name="comb_fwd", out_shape=[SDS((T, 512), BF16)] + [SDS((T, 512), F32)] * 3, grid=(T // tm,),
        in_specs=[spec] * 6, out_specs=[spec] * 4, compiler_params=_params(1))(*os, *ls)


def _mix_fwd(oa, ob, proj, b_gate, wpa, wpb):
    tm = 512

    def body(oa_ref, ob_ref, ga_ref, gb_ref, ba_ref, bb_ref, wpa_ref, wpb_ref, mixed_ref, ob16_ref):
        oav = oa_ref[...]
        obv = ob_ref[...].astype(BF16)
        ob16_ref[...] = obv
        for s in range(NSH):
            sl = slice(s * 512, (s + 1) * 512)
            ga = _sigmoid(ga_ref[:, sl] + ba_ref[:, sl])
            gb = _sigmoid(gb_ref[:, sl] + bb_ref[:, sl])
            mixed_ref[:, sl] = (ga * _dot(oav, wpa_ref[s]) + gb * _dot(obv, wpb_ref[s])).astype(BF16)

    row = lambda w: pl.BlockSpec((tm, w), lambda i: (i, 0))
    return pl.pallas_call(
        body, name="mix_fwd", out_shape=[SDS((T, D), BF16), SDS((T, 512), BF16)], grid=(T // tm,),
        in_specs=[row(512), row(512),
                  pl.BlockSpec((tm, D), lambda i: (i, 3)), pl.BlockSpec((tm, D), lambda i: (i, 4)),
                  pl.BlockSpec((1, D), lambda i: (0, 0)), pl.BlockSpec((1, D), lambda i: (0, 1)),
                  _resident((NSH, 512, 512), lambda i: (0, 0, 0)), _resident((NSH, 512, 512), lambda i: (0, 0, 0))],
        out_specs=[row(D), row(512)], compiler_params=_params(1))(oa, ob, proj, proj, b_gate, b_gate, wpa, wpb)


def _out_proj_fwd(mixed, w_out, x, g):
    tm = 512

    def body(m_ref, w_ref, x_ref, g_ref, h1_ref, hn_ref):
        h1 = x_ref[...] + _dot(m_ref[...], w_ref[...])
        h1_ref[...] = h1
        r = lax.rsqrt(jnp.mean(h1 * h1, axis=-1, keepdims=True) + EPS)
        hn_ref[...] = (h1 * r * g_ref[...]).astype(BF16)

    row = pl.BlockSpec((tm, D), lambda i: (i, 0))
    return pl.pallas_call(
        body, name="out_proj_fwd", out_shape=[SDS((T, D), F32), SDS((T, D), BF16)], grid=(T // tm,),
        in_specs=[row, _resident((D, D), lambda i: (0, 0)), row, pl.BlockSpec((1, D), lambda i: (0, 0))],
        out_specs=[row, row], compiler_params=_params(1))(mixed, w_out, x, g)


def _ffn_up(hn, w_up):
    tm, tn = T, 512
    per = (DFF // NSH) // tn

    def body(h_ref, w_ref, a_ref, u_ref):
        uv = jnp.maximum(_dot(h_ref[...], w_ref[...]), 0.0)
        a_ref[...] = (uv * uv).astype(BF16)
        u_ref[...] = uv.astype(BF16)

    out = pl.BlockSpec((tm, tn), lambda i, j: (i, j))
    return pl.pallas_call(
        body, name="ffn_up", out_shape=[SDS((T, DFF), BF16)] * 2, grid=(T // tm, DFF // tn),
        in_specs=[pl.BlockSpec((tm, D), lambda i, j: (i, 0)),
                  pl.BlockSpec((None, D, tn), lambda i, j: (j // per, 0, j % per))],
        out_specs=[out, out], compiler_params=_params(2))(hn, w_up)


def _ffn_down_loss(u, w_down, h1, target):
    tm, tk = 512, 2048
    nk = DFF // tk

    def body(u_ref, w_ref, h1_ref, t_ref, dy_ref, dy16_ref, loss_ref, acc):
        k = pl.program_id(1)

        @pl.when(k == 0)
        def _():
            acc[...] = jnp.zeros_like(acc)

        acc[...] += _dot(u_ref[...], w_ref[...])

        @pl.when(k == nk - 1)
        def _():
            def chunk(r, sq):
                rows = pl.ds(pl.multiple_of(r * 16, 16), 16)
                err = acc[rows, :] + h1_ref[rows, :] - t_ref[rows, :]
                dy = err * (1.0 / D)
                dy_ref[rows, :] = dy
                dy16_ref[rows, :] = dy.astype(BF16)
                return sq + err * err

            sq = lax.fori_loop(0, tm // 16, chunk, jnp.zeros((16, D), F32), unroll=2)
            part = 0.5 * jnp.sum(jnp.mean(sq, axis=-1, keepdims=True), axis=0, keepdims=True)
            loss_ref[...] = jnp.broadcast_to(part, (8, 128))

    row = pl.BlockSpec((tm, D), lambda i, k: (i, 0))
    once = _resident((tm, D), lambda i, k: (i, 0))
    return pl.pallas_call(
        body, name="ffn_down_loss",
        out_shape=[SDS((T, D), F32), SDS((T, D), BF16), SDS((T // tm, 8, 128), F32)], grid=(T // tm, nk),
        in_specs=[pl.BlockSpec((tm, tk), lambda i, k: (i, k)), pl.BlockSpec((tk, D), lambda i, k: (k, 0)), once, once],
        out_specs=[row, row, pl.BlockSpec((None, 8, 128), lambda i, k: (i, 0, 0))],
        scratch_shapes=[pltpu.VMEM((tm, D), F32)], compiler_params=_params(2))(u, w_down, h1, target)


def _ffn_down_bwd(dy16, w_down, u, deps=()):
    tm, tn = T, 512

    def body(dy_ref, w_ref, u_ref, du_ref):
        uv = u_ref[...].astype(F32)
        du_ref[...] = jnp.where(uv > 0.0, 2.0 * uv * _dot_nt(dy_ref[...], w_ref[...]), 0.0).astype(BF16)

    return pl.pallas_call(
        _after(body, deps), name="ffn_down_bwd", out_shape=SDS((T, DFF), BF16), grid=(T // tm, DFF // tn),
        in_specs=[DEP_SPEC] * len(deps) + [
            pl.BlockSpec((tm, D), lambda i, j: (i, 0)), pl.BlockSpec((tn, D), lambda i, j: (j, 0)),
            pl.BlockSpec((tm, tn), lambda i, j: (i, j))],
        out_specs=pl.BlockSpec((tm, tn), lambda i, j: (i, j)), compiler_params=_params(2))(*deps, dy16, w_down, u)


def _norm_bwd(xv, dz_in, g):
    r = lax.rsqrt(jnp.mean(xv * xv, axis=-1, keepdims=True) + EPS)
    dg = jnp.sum(xv * r * dz_in, axis=0, keepdims=True)
    dz = dz_in * g
    dx = r * dz - xv * (r * r * r) * jnp.mean(xv * dz, axis=-1, keepdims=True)
    return dx, dg


def _ffn_up_bwd(du, w_up, h1, dy, g, deps=()):
    tm, tk = 512, 1024
    per = (DFF // NSH) // tk
    nk = DFF // tk

    def body(du_ref, w_ref, h1_ref, dy_ref, g_ref, dh1_ref, dh16_ref, dg_ref, acc):
        i, k = pl.program_id(0), pl.program_id(1)

        @pl.when(k == 0)
        def _():
            acc[...] = jnp.zeros_like(acc)

        @pl.when((k == 0) & (i == 0))
        def _():
            dg_ref[...] = jnp.zeros_like(dg_ref)

        acc[...] += _dot_nt(du_ref[...], w_ref[...])

        @pl.when(k == nk - 1)
        def _():
            dx, dg = _norm_bwd(h1_ref[...], acc[...], g_ref[...])
            dh1 = dy_ref[...] + dx
            dh1_ref[...] = dh1
            dh16_ref[...] = dh1.astype(BF16)
            dg_ref[...] += dg

    row = pl.BlockSpec((tm, D), lambda i, k: (i, 0))
    vec = pl.BlockSpec((1, D), lambda i, k: (0, 0))
    return pl.pallas_call(
        _after(body, deps), name="ffn_up_bwd", out_shape=[SDS((T, D), F32), SDS((T, D), BF16), SDS((1, D), F32)],
        grid=(T // tm, nk),
        in_specs=[DEP_SPEC] * len(deps) + [
            pl.BlockSpec((tm, tk), lambda i, k: (i, k)),
            pl.BlockSpec((None, D, tk), lambda i, k: (k // per, 0, k % per)), row, row, vec],
        out_specs=[row, row, vec], scratch_shapes=[pltpu.VMEM((tm, D), F32)],
        compiler_params=_params(2))(*deps, du, w_up, h1, dy, g)


def _mix_bwd(dh16, w_out, oa, ob16, proj, b_gate, wpa, wpb):
    tm = 256

    def body(dh_ref, wo_ref, oa_ref, ob_ref, ga_ref, gb_ref, ba_ref, bb_ref, wpa_ref, wpb_ref,
             dya_ref, dyb_ref, dga_ref, dgb_ref, doa_ref, dob_ref, dba_ref, dbb_ref):
        @pl.when(pl.program_id(0) == 0)
        def _():
            dba_ref[...] = jnp.zeros_like(dba_ref)
            dbb_ref[...] = jnp.zeros_like(dbb_ref)

        oav, obv = oa_ref[...], ob_ref[...]
        doa = jnp.zeros((tm, 512), F32)
        dob = jnp.zeros((tm, 512), F32)
        for s in range(NSH):
            sl = slice(s * 512, (s + 1) * 512)
            dm = _dot_nt(dh_ref[...], wo_ref[sl, :])
            ga = _sigmoid(ga_ref[:, sl] + ba_ref[:, sl])
            gb = _sigmoid(gb_ref[:, sl] + bb_ref[:, sl])
            dya = (dm * ga).astype(BF16)
            dyb = (dm * gb).astype(BF16)
            dza = dm * _dot(oav, wpa_ref[s]) * ga * (1.0 - ga)
            dzb = dm * _dot(obv, wpb_ref[s]) * gb * (1.0 - gb)
            dya_ref[:, sl], dyb_ref[:, sl] = dya, dyb
            dga_ref[:, sl], dgb_ref[:, sl] = dza.astype(BF16), dzb.astype(BF16)
            dba_ref[:, sl] += jnp.sum(dza, axis=0, keepdims=True)
            dbb_ref[:, sl] += jnp.sum(dzb, axis=0, keepdims=True)
            doa += _dot_nt(dya, wpa_ref[s])
            dob += _dot_nt(dyb, wpb_ref[s])
        doa_ref[...], dob_ref[...] = doa, dob

    row = lambda w: pl.BlockSpec((tm, w), lambda i: (i, 0))
    vec = pl.BlockSpec((1, D), lambda i: (0, 0))
    wp = _resident((NSH, 512, 512), lambda i: (0, 0, 0))
    return pl.pallas_call(
        body, name="mix_bwd",
        out_shape=[SDS((T, D), BF16)] * 4 + [SDS((T, 512), F32)] * 2 + [SDS((1, D), F32)] * 2, grid=(T // tm,),
        in_specs=[row(D), _resident((D, D), lambda i: (0, 0)), row(512), row(512),
                  pl.BlockSpec((tm, D), lambda i: (i, 3)), pl.BlockSpec((tm, D), lambda i: (i, 4)),
                  pl.BlockSpec((1, D), lambda i: (0, 0)), pl.BlockSpec((1, D), lambda i: (0, 1)), wp, wp],
        out_specs=[row(D)] * 4 + [row(512)] * 2 + [vec] * 2,
        compiler_params=_params(1))(dh16, w_out, oa, ob16, proj, proj, b_gate, b_gate, wpa, wpb)


def _comb_bwd(doa, os, ws, deps=()):
    tm = 512

    def body(d_ref, o0, o1, o2, w0, w1, w2, cc_ref):
        prod = d_ref[...] * (w0[...] * o0[...] + w1[...] * o1[...] + w2[...] * o2[...])
        for h in range(4):
            sl = slice(h * HD, (h + 1) * HD)
            cc_ref[:, sl] = jnp.broadcast_to(jnp.sum(prod[:, sl], axis=-1, keepdims=True), (tm, HD))

    spec = pl.BlockSpec((tm, 512), lambda i: (i, 0))
    return pl.pallas_call(
        _after(body, deps), name="comb_bwd", out_shape=SDS((T, 512), F32), grid=(T // tm,),
        in_specs=[DEP_SPEC] * len(deps) + [spec] * 7, out_specs=spec,
        compiler_params=_params(1))(*deps, doa, *os, *ws)


def _attn_a_bwd(qkn, proj, doa, lse, w, cc, g):
    dil = DILS[g]
    m = T // dil
    nb = m // 128

    def body(q_ref, k_ref, v_ref, d_ref, l_ref, w_ref, c_ref, dqk_ref, dv_ref,
             qf, kf, qp, kp, vp, dp, lp, wsub, cp, dqb, dkp, dvp):
        qf[...] = q_ref[...].astype(F32)
        kf[...] = k_ref[...].astype(F32)
        for r in range(dil):
            sub = _residue_rows(r, m, dil)
            qp[...] = qf[sub, :].astype(BF16)
            _fill_padded(kp, kf[sub, :], m)
            _fill_padded(vp, v_ref[sub, :], m)
            dp[...] = d_ref[sub, :].astype(BF16)
            lp[...], wsub[...], cp[...] = l_ref[sub, :], w_ref[sub, :], c_ref[sub, :]
            dkp[...] = jnp.zeros_like(dkp)
            dvp[...] = jnp.zeros_like(dvp)

            def block(b, carry):
                q0 = pl.multiple_of(b * 128, 128)
                rows = pl.ds(q0, 128)
                win = pl.ds(q0, 256)
                qb, kw, vw = qp[rows, :], kp[win, :], vp[win, :]
                s = _dot_nt(qb, kw) * SCALE
                s = jnp.where(_band_mask(q0, m), s, NEG)
                wp = _wide(wsub[rows, :], 2) * jnp.exp(s - _wide(lp[rows, :], 2))
                dob = dp[rows, :]
                ds = (wp * (_dot_nt(dob, vw) - _wide(cp[rows, :], 2))).astype(BF16)
                dqb[rows, :] = _dot(ds, kw) * SCALE
                dkp[win, :] += _dot_tn(ds, qb) * SCALE
                dvp[win, :] += _dot_tn(wp.astype(BF16), dob)
                return carry

            lax.fori_loop(0, nb, block, 0, unroll=min(nb, 8))
            dqk_ref.at[0][sub, :] = dqb[...]
            dqk_ref.at[1][sub, :] = dkp[64:64 + m, :]
            dv_ref[sub, :] = dvp[64:64 + m, :]

    q_blk, k_blk, v_blk, blk = _head_blocks(g)
    sub16 = pltpu.VMEM((m, HD), BF16)
    sub32 = pltpu.VMEM((m, HD), F32)
    return pl.pallas_call(
        body, name=f"attn_a_bwd_{g}", out_shape=[SDS((2, T, 512), F32), SDS((T, 512), F32)], grid=(4,),
        in_specs=[q_blk, k_blk, v_blk, blk, blk, blk, blk],
        out_specs=[pl.BlockSpec((2, T, HD), lambda h: (0, 0, h)), blk],
        scratch_shapes=[pltpu.VMEM((T, HD), F32), pltpu.VMEM((T, HD), F32), sub16,
                        pltpu.VMEM((m + 128, HD), BF16), pltpu.VMEM((m + 128, HD), BF16), sub16,
                        sub32, sub32, sub32, sub32,
                        pltpu.VMEM((m + 128, HD), F32), pltpu.VMEM((m + 128, HD), F32)],
        compiler_params=_params(1))(qkn, qkn, proj, doa, lse, w, cc)


def _attn_b_bwd(qkn, proj, dob, ob, lse, bias, deps=()):
    def body(q_ref, k_ref, v_ref, d_ref, o_ref, l_ref, bias_ref, dqk_ref, dv_ref, drpb_ref, vb, dk_acc, dv_acc, a_acc):
        vb[...] = v_ref[...].astype(BF16)
        dk_acc[...] = jnp.zeros_like(dk_acc)
        dv_acc[...] = jnp.zeros_like(dv_acc)
        a_acc[...] = jnp.zeros_like(a_acc)

        def row(r, carry):
            start, off = _nbr_window(r)
            rows = pl.ds(pl.multiple_of(r * GRID_W, GRID_W), GRID_W)
            win = pl.ds(pl.multiple_of(start * GRID_W, GRID_W), 512)
            qr, kw, vw = q_ref[rows, :], k_ref[win, :], vb[win, :]
            s = _dot_nt(qr, kw) * SCALE + bias_ref[off]
            p = jnp.exp(s - _wide(l_ref[rows, :], 4))
            dov = d_ref[rows, :]
            delta = jnp.sum(dov * o_ref[rows, :], axis=-1, keepdims=True)
            do16 = dov.astype(BF16)
            ds = p * (_dot_nt(do16, vw) - delta)
            a_acc[off] += ds
            ds16 = ds.astype(BF16)
            dqk_ref[0, rows, :] = _dot(ds16, kw) * SCALE
            dk_acc[win, :] += _dot_tn(ds16, qr) * SCALE
            dv_acc[win, :] += _dot_tn(p.astype(BF16), do16)
            return carry

        lax.fori_loop(0, T // GRID_W, row, 0, unroll=8)
        dqk_ref[1] = dk_acc[...]
        dv_ref[...] = dv_acc[...]

        lane = lax.broadcasted_iota(jnp.int32, (16, HD), 1)
        rowi = lax.broadcasted_iota(jnp.int32, (16, HD), 0)
        low = (lane >= GRID_W - WIN_C) & (lane < GRID_W + WIN_C - 1)
        high = (lane >= HD - WIN_C) | (lane < WIN_C - 1)
        flip = (lax.broadcasted_iota(jnp.int32, (GRID_W, GRID_W), 0)
                + lax.broadcasted_iota(jnp.int32, (GRID_W, GRID_W), 1) == GRID_W - 1).astype(BF16)
        out = jnp.zeros((16, HD), F32)
        for d in range(14):
            acc = None
            for off in range(8):
                if 0 <= d - off <= 6 and (d - off) % 2 == 0:
                    jj = (d - off) // 2
                    piece = a_acc[off, :, jj * HD:(jj + 1) * HD]
                    acc = piece if acc is None else acc + piece
            hi = acc.astype(BF16)
            lo = (acc - hi.astype(F32)).astype(BF16)
            rev = _dot(flip, hi) + _dot(flip, lo)
            v = jnp.sum(pltpu.roll(rev, 0, 1, stride=1, stride_axis=0), axis=0, keepdims=True)
            v = jnp.broadcast_to(v, (16, HD))
            out = out + jnp.where((rowi == d) & low, v, 0.0)
            out = out + jnp.where(rowi == d + 1, pltpu.roll(jnp.where(high, v, 0.0), GRID_W, 1), 0.0)
        drpb_ref[...] = out

    blk = pl.BlockSpec((T, HD), lambda h: (0, h))
    return pl.pallas_call(
        _after(body, deps), name="attn_b_bwd",
        out_shape=[SDS((2, T, 512), F32), SDS((T, 512), F32), SDS((4, 16, HD), F32)], grid=(4,),
        in_specs=[DEP_SPEC] * len(deps) + [
            pl.BlockSpec((T, HD), lambda h: (0, NHA + h)),
            pl.BlockSpec((T, HD), lambda h: (0, NH + NHA + h)),
            pl.BlockSpec((T, HD), lambda h: (0, 2 * NH + NHA + h)), blk, blk, blk,
            pl.BlockSpec((None, 8, GRID_W, 512), lambda h: (h, 0, 0, 0))],
        out_specs=[pl.BlockSpec((2, T, HD), lambda h: (0, 0, h)), blk,
                   pl.BlockSpec((None, 16, HD), lambda h: (h, 0, 0))],
        scratch_shapes=[pltpu.VMEM((T, HD), BF16), pltpu.VMEM((T, HD), F32), pltpu.VMEM((T, HD), F32),
                        pltpu.VMEM((8, GRID_W, 512), F32)],
        compiler_params=_params(1))(*deps, qkn, qkn, proj, dob, ob, lse, bias)


def _qk_bwd(proj, nw, cos, sin, dqk_groups, dqk_b, dvs, dga, dgb):
    tm = 512

    def body(p_ref, w_ref, cos_ref, sin_ref, d0, d1, d2, d3, v0, v1, v2, v3, ga_ref, gb_ref, o_ref, dn_ref):
        j, i = pl.program_id(0), pl.program_id(1)

        @pl.when((j < 2) & (i == 0))
        def _():
            dn_ref[...] = jnp.zeros_like(dn_ref)

        @pl.when(j < 2)
        def _():
            cv, sv = cos_ref[...], sin_ref[...]
            srcs = (d0, d1, d2, d3)
            dna = jnp.zeros((1, HD), F32)
            dnb = jnp.zeros((1, HD), F32)
            for h in range(NH):
                sl = slice(h * HD, (h + 1) * HD)
                dz = srcs[h // 4][:, (h % 4) * HD:(h % 4 + 1) * HD]
                if h < NHA:
                    dz = dz * cv + pltpu.roll(dz * sv, 64, 1)
                dx, dg = _norm_bwd(p_ref[:, sl], dz, w_ref[:, sl])
                o_ref[:, sl] = dx.astype(BF16)
                if h < NHA:
                    dna += dg
                else:
                    dnb += dg
            dn_ref[0:1, :] += dna
            dn_ref[1:2, :] += dnb

        @pl.when(j == 2)
        def _():
            for s, v_ref in enumerate((v0, v1, v2, v3)):
                o_ref[:, s * 512:(s + 1) * 512] = v_ref[...].astype(BF16)

        @pl.when(j == 3)
        def _():
            o_ref[...] = ga_ref[...]

        @pl.when(j == 4)
        def _():
            o_ref[...] = gb_ref[...]

    def rows(used):
        return lambda j, i: (jnp.where(used(j), i, 0), 0)

    qk = lambda j: j < 2
    dspec = pl.BlockSpec((None, tm, 512), lambda j, i: (jnp.minimum(j, 1), jnp.where(j < 2, i, 0), 0))
    vspec = pl.BlockSpec((tm, 512), rows(lambda j: j == 2))
    return pl.pallas_call(
        body, name="qk_bwd", out_shape=[SDS((T, DIN), BF16), SDS((2, 8, HD), F32)], grid=(5, T // tm),
        in_specs=[pl.BlockSpec((tm, D), lambda j, i: (jnp.where(j < 2, i, 0), jnp.minimum(j, 1))),
                  pl.BlockSpec((None, 1, D), lambda j, i: (jnp.minimum(j, 1), 0, 0)),
                  pl.BlockSpec((tm, HD), rows(qk)), pl.BlockSpec((tm, HD), rows(qk)),
                  dspec, dspec, dspec, dspec, vspec, vspec, vspec, vspec,
                  pl.BlockSpec((tm, D), rows(lambda j: j == 3)), pl.BlockSpec((tm, D), rows(lambda j: j == 4))],
        out_specs=[pl.BlockSpec((tm, D), lambda j, i: (i, j)),
                   pl.BlockSpec((None, 8, HD), lambda j, i: (jnp.minimum(j, 1), 0, 0))],
        compiler_params=_params(2))(proj, nw, cos, sin, *dqk_groups, dqk_b, *dvs, dga, dgb)


def _in_proj_bwd(dproj, w_in, x, dh1, g, deps=()):
    tm, tk = 512, 1280
    per = (DIN // NSH) // tk
    nk = DIN // tk

    def body(dp_ref, w_ref, x_ref, dh_ref, g_ref, dx_ref, dg_ref, acc):
        i, k = pl.program_id(0), pl.program_id(1)

        @pl.when(k == 0)
        def _():
            acc[...] = jnp.zeros_like(acc)

        @pl.when((k == 0) & (i == 0))
        def _():
            dg_ref[...] = jnp.zeros_like(dg_ref)

        acc[...] += _dot_nt(dp_ref[...], w_ref[...])

        @pl.when(k == nk - 1)
        def _():
            dx, dg = _norm_bwd(x_ref[...], acc[...], g_ref[...])
            dx_ref[...] = dh_ref[...] + dx
            dg_ref[...] += dg

    row = pl.BlockSpec((tm, D), lambda i, k: (i, 0))
    vec = pl.BlockSpec((1, D), lambda i, k: (0, 0))
    return pl.pallas_call(
        _after(body, deps), name="in_proj_bwd", out_shape=[SDS((T, D), F32), SDS((1, D), F32)], grid=(T // tm, nk),
        in_specs=[DEP_SPEC] * len(deps) + [
            pl.BlockSpec((tm, tk), lambda i, k: (i, k)),
            pl.BlockSpec((None, D, tk), lambda i, k: (k // per, 0, k % per)), row, row, vec],
        out_specs=[row, vec], scratch_shapes=[pltpu.VMEM((tm, D), F32)],
        compiler_params=_params(2))(*deps, dproj, w_in, x, dh1, g)


def _grad_w(name, a, g, shard_rows, rows, cols, tr, tc):
    ni, nj = rows // tr, cols // tc
    if shard_rows:
        a_map, g_map = (lambda s, i, j: (0, s * ni + i)), (lambda s, i, j: (0, j))
    else:
        a_map, g_map = (lambda s, i, j: (0, i)), (lambda s, i, j: (0, s * nj + j))

    def body(a_ref, g_ref, o_ref):
        o_ref[...] = _dot_tn(a_ref[...], g_ref[...]).astype(BF16)

    return pl.pallas_call(
        body, name=name, out_shape=SDS((NSH, rows, cols), BF16), grid=(NSH, ni, nj),
        in_specs=[pl.BlockSpec((T, tr), a_map), pl.BlockSpec((T, tc), g_map)],
        out_specs=pl.BlockSpec((None, tr, tc), lambda s, i, j: (s, i, j)), compiler_params=_params(3))(a, g)


def _grad_w_in_half(name, xn, dproj, place, for_sibling, deps=()):
    tr, tc = D // 2, 1280
    nj = (DIN // NSH) // tc

    def body(*refs):
        a_ref, g_ref, o_ref = refs[-3:]
        o_ref[...] = _dot_tn(a_ref[...], g_ref[...]).astype(BF16)

    half = (lambda p: 1 - p[1]) if for_sibling else (lambda p: p[1])
    return pl.pallas_call(
        body, name=name, out_shape=SDS((NSH, tr, DIN // NSH), BF16),
        grid_spec=pltpu.PrefetchScalarGridSpec(
            num_scalar_prefetch=1, grid=(NSH, nj),
            in_specs=[DEP_SPEC] * len(deps) + [pl.BlockSpec((T, tr), lambda s, j, p: (0, half(p))),
                                               pl.BlockSpec((T, tc), lambda s, j, p: (0, s * nj + j))],
            out_specs=pl.BlockSpec((None, tr, tc), lambda s, j, p: (s, 0, j))),
        compiler_params=_params(2))(place, *deps, xn, dproj)


def _adamw(w, g, m, v):
    m = B1 * m + (1.0 - B1) * g
    v = B2 * v + (1.0 - B2) * (g * g)
    m_hat = m / (1.0 - B1 ** STEP)
    v_hat = v / (1.0 - B2 ** STEP)
    delta = -LR * (m_hat / (jnp.sqrt(v_hat) + AEPS) + WD * w)
    return delta, m, v


def _sum_halves(name, place, grads, theirs):
    _, rows, cols = theirs.shape
    tr = _row_tile(rows, cols, 1 << 20)

    def body(place_ref, a_ref, b_ref, o_ref):
        o_ref[...] = (a_ref[...].astype(F32) + b_ref[...].astype(F32)).astype(BF16)

    spec = pl.BlockSpec((None, tr, cols), lambda s, i, p: (s, i, 0))
    mine = spec if grads.ndim == 3 else pl.BlockSpec((None, None, tr, cols), lambda s, i, p: (s, p[1], i, 0))
    return pl.pallas_call(
        body, name=name, out_shape=SDS(theirs.shape, BF16),
        grid_spec=pltpu.PrefetchScalarGridSpec(
            num_scalar_prefetch=1, grid=(NSH, rows // tr), in_specs=[mine, spec], out_specs=spec),
        compiler_params=_params(2))(place, grads, theirs)


def _sum_landed(name, place, part, landed):
    _, rows, cols = part.shape
    tr = _row_tile(rows, cols, 1 << 20)

    def body(place_ref, p_ref, l_ref, o_ref):
        o_ref[...] = ((p_ref[...].astype(F32) + l_ref[0].astype(F32)) + l_ref[1].astype(F32)) + l_ref[2].astype(F32)

    return pl.pallas_call(
        body, name=name, out_shape=SDS((2, rows, cols), F32),
        grid_spec=pltpu.PrefetchScalarGridSpec(
            num_scalar_prefetch=1, grid=(rows // tr,),
            in_specs=[pl.BlockSpec((None, tr, cols), lambda i, p: (p[0], i, 0)),
                      pl.BlockSpec((3, tr, cols), lambda i, p: (0, i, 0))],
            out_specs=pl.BlockSpec((None, tr, cols), lambda i, p: (p[1], i, 0))),
        compiler_params=_params(1))(place, part, landed)


def _adam_shard(name, g, w, m, v):
    rows, cols = w.shape
    tr = _row_tile(rows, cols, 1 << 19)

    def body(g_ref, w_ref, m_ref, v_ref, go_ref, d_ref, nm_ref, nv_ref):
        g = g_ref[...]
        go_ref[...] = g
        d_ref[...], nm_ref[...], nv_ref[...] = _adamw(w_ref[...], g, m_ref[...], v_ref[...])

    spec = pl.BlockSpec((tr, cols), lambda i: (i, 0))
    return pl.pallas_call(
        body, name=name, out_shape=[SDS((rows, cols), F32)] * 4, grid=(rows // tr,),
        in_specs=[spec] * 4, out_specs=[spec] * 4, compiler_params=_params(1))(g, w, m, v)


def _adam_small(gathered, w, m, v):
    def body(g_ref, w_ref, m_ref, v_ref, go_ref, d_ref, nm_ref, nv_ref):
        g = g_ref[0:SMALL_ROWS, :]
        for dev in range(1, 8):
            g = g + g_ref[dev * SMALL_ROWS:(dev + 1) * SMALL_ROWS, :]
        go_ref[...] = g
        d_ref[...], nm_ref[...], nv_ref[...] = _adamw(w_ref[...], g, m_ref[...], v_ref[...])

    return pl.pallas_call(body, name="adam_small", out_shape=[SDS((SMALL_ROWS, HD), F32)] * 4)(gathered, w, m, v)


SMALL = (("norm_mix", (1, D)), ("b_gate", (1, 2 * D)), ("q_norm_a", (1, HD)), ("k_norm_a", (1, HD)),
         ("q_norm_b", (1, HD)), ("k_norm_b", (1, HD)), ("rpb_b", (1, 4, 15, 31)), ("norm_ffn", (1, D)))


def _pack_small(vals):
    pieces = []
    for (name, shape), val in zip(SMALL, vals):
        flat = val.reshape(-1)
        pad = (-flat.shape[0]) % HD
        pieces.append(jnp.pad(flat, (0, pad)).reshape(-1, HD))
    packed = jnp.concatenate(pieces, axis=0)
    return jnp.pad(packed, ((0, SMALL_ROWS - packed.shape[0]), (0, 0)))


def _unpack_small(packed):
    out, row = [], 0
    for name, shape in SMALL:
        size = int(np.prod(shape))
        nrows = -(-size // HD)
        out.append(packed[row:row + nrows].reshape(-1)[:size].reshape(shape))
        row += nrows
    return out


def kernel(x, norm_mix, w_in, b_gate, q_norm_a, k_norm_a, q_norm_b, k_norm_b, rpb_b, w_proj_a, w_proj_b, w_out, norm_ffn, w_up, w_down, loss_target, m_norm_mix, m_w_in, m_b_gate, m_q_norm_a, m_k_norm_a, m_q_norm_b, m_k_norm_b, m_rpb_b, m_w_proj_a, m_w_proj_b, m_w_out, m_norm_ffn, m_w_up, m_w_down, v_norm_mix, v_w_in, v_b_gate, v_q_norm_a, v_k_norm_a, v_q_norm_b, v_k_norm_b, v_rpb_b, v_w_proj_a, v_w_proj_b, v_w_out, v_norm_ffn, v_w_up, v_w_down):
    big_names = ("w_in", "w_proj_a", "w_proj_b", "w_out", "w_up", "w_down")
    big_w = [a[0] for a in (w_in, w_proj_a, w_proj_b, w_out, w_up, w_down)]
    big_m = [a[0] for a in (m_w_in, m_w_proj_a, m_w_proj_b, m_w_out, m_w_up, m_w_down)]
    big_v = [a[0] for a in (v_w_in, v_w_proj_a, v_w_proj_b, v_w_out, v_w_up, v_w_down)]
    x2, target = x[0], loss_target[0]

    place = jnp.stack([2 * lax.axis_index("x") + lax.axis_index("y"), lax.axis_index("c")]).astype(jnp.int32)
    groups = ((0,), (1, 2, 3), (4,), (5,))
    started = []
    for j, grp in enumerate(groups):
        deps = (started[0][4],) if j else ()
        placed = [_cast_into_place(big_w[i], "cast_" + big_names[i], place, deps) for i in grp]
        started.append(_gather_start(f"gather_start_{j}", placed))

    def whole(fulls):
        return [f.reshape(NSH, 2 * f.shape[2], f.shape[3]) for f in fulls]

    def forward_begin(j, after):
        send, recv, _, fulls, _ = started[j]
        fulls = _gather_wait(f"gather_wait_{j}", send, recv, fulls, after)
        send, recv, _, fulls, token = _forward_start(f"forward_start_{j}", fulls)
        return (send, recv, fulls), token

    def forward_end(j, state, after):
        return whole(_forward_wait(f"forward_wait_{j}", *state, after))

    def as_halves(grads):
        return [g.reshape(NSH, 2, g.shape[1] // 2, g.shape[2]) for g in grads]

    def reduce_start(j, grads, theirs):
        parts = [_sum_halves(f"sum_halves_{j}_{i}", place, a, b) for i, (a, b) in enumerate(zip(grads, theirs))]
        send, recv, parts, lands, token = _reduce_start(f"reduce_start_{j}", parts)
        return (send, recv, parts, lands), token

    def exchange_begin(j, grads):
        send, recv, grads, lands, token = _exchange_start(f"exchange_start_{j}", as_halves(grads))
        return (send, recv, grads, lands), token

    def exchange_end(j, state, after):
        return reduce_start(j, *_exchange_wait(f"exchange_wait_{j}", *state, after))

    big_out = {}

    def share_begin(j, state, after):
        send, recv, parts, lands = state
        parts, lands = _reduce_wait(f"reduce_wait_{j}", send, recv, parts, lands, after)
        sums = [_sum_landed(f"sum_landed_{j}_{i}", place, p, l) for i, (p, l) in enumerate(zip(parts, lands))]
        send, recv, _, sums, token = _share_start(f"share_start_{j}", sums)
        return (send, recv, sums), token

    def share_end(j, state, after):
        for idx, g in zip(groups[j], _share_wait(f"share_wait_{j}", *state, after)):
            g = g.reshape(big_w[idx].shape)
            big_out[idx] = _adam_shard("adam_" + big_names[idx], g, big_w[idx], big_m[idx], big_v[idx])
        return big_out[groups[j][-1]][1]

    proj, xn = _norm_in_proj_own(x2, norm_mix, whole(started[0][3])[0], place)
    send, recv, _, win, _ = started[0]
    win = _gather_wait("gather_wait_0", send, recv, win, (proj, *[s[4] for s in started[1:]]))
    (win_f,) = whole(_gather_finish("gather_finish_0", win))
    proj = _in_proj_rest("in_proj_rest", xn, win_f, proj, place, (2, 1, 3))
    cos, sin = _rope_tables()
    nw = jnp.stack([jnp.concatenate([jnp.tile(q_norm_a, (1, NHA)), jnp.tile(q_norm_b, (1, NH - NHA))], axis=1),
                    jnp.concatenate([jnp.tile(k_norm_a, (1, NHA)), jnp.tile(k_norm_b, (1, NH - NHA))], axis=1)])
    qkn = _qk_prep(proj, nw, cos, sin)
    fw1, token = forward_begin(1, (qkn,))
    fwd_a = [_attn_a_fwd(qkn, proj, g) for g in range(3)]
    os, ls = [f[0] for f in fwd_a], [f[1] for f in fwd_a]
    fw2, token = forward_begin(2, (os[2], token))
    ob, lse_b, bias = _attn_b_fwd(qkn, proj, _rpb_rows(rpb_b[0]))
    oa, w0, w1, w2 = _comb_fwd(os, ls)
    ws = [w0, w1, w2]
    wpa_f, wpb_f, wout_f = forward_end(1, fw1, (oa, token))
    wout_f = wout_f.reshape(D, D)
    mixed, ob16 = _mix_fwd(oa, ob, proj, b_gate, wpa_f, wpb_f)
    h1, hn = _out_proj_fwd(mixed, wout_f, x2, norm_ffn)
    fw3, token = forward_begin(3, (h1,))
    (wup_f,) = forward_end(2, fw2, (hn, token))
    usq, u = _ffn_up(hn, wup_f)
    (wdown_f,) = forward_end(3, fw3, (u,))
    wdown_f = wdown_f.reshape(DFF, D)
    dy, dy16, loss_parts = _ffn_down_loss(usq, wdown_f, h1, target)
    loss = lax.psum(jnp.sum(loss_parts[:, 0, 0]), ("x", "y", "c"))

    g_down = _grad_w("grad_w_down", usq, dy16, True, DFF // NSH, D, 1024, 1024)
    ex_down, token = exchange_begin(3, [g_down])
    du = _ffn_down_bwd(dy16, wdown_f, u, deps=(token,))
    g_up = _grad_w("grad_w_up", hn, du, False, D, DFF // NSH, 1024, 1024)
    red_down, token = exchange_end(3, ex_down, (g_up,))
    ex_up, token_up = exchange_begin(2, [g_up])
    dh1, dh16, d_norm_ffn = _ffn_up_bwd(du, wup_f, h1, dy, norm_ffn, deps=(token, token_up))
    dya, dyb, dga, dgb, doa, dob, dba, dbb = _mix_bwd(dh16, wout_f, oa, ob16, proj, b_gate, wpa_f, wpb_f)
    g_out = _grad_w("grad_w_out", mixed, dh16, True, D // NSH, D, 512, 1024)
    g_pa = _grad_w("grad_w_proj_a", oa, dya, False, 512, 512, 512, 512)
    g_pb = _grad_w("grad_w_proj_b", ob16, dyb, False, 512, 512, 512, 512)
    red_up, token = exchange_end(2, ex_up, (g_out,))
    ex_mid, token_mid = exchange_begin(1, [g_pa, g_pb, g_out])
    cc = _comb_bwd(doa, os, ws, deps=(token, token_mid))
    bwd_a = [_attn_a_bwd(qkn, proj, doa, ls[g], ws[g], cc, g) for g in range(3)]
    red_mid, token = exchange_end(1, ex_mid, (bwd_a[2][1],))
    dqk_b, dv_b, drpb_t = _attn_b_bwd(qkn, proj, dob, ob, lse_b, bias, deps=(token,))
    dproj, dn = _qk_bwd(proj, nw, cos, sin, [b[0] for b in bwd_a], dqk_b, [b[1] for b in bwd_a] + [dv_b], dga, dgb)
    g_in_theirs = _grad_w_in_half("grad_w_in_for_sibling", xn, dproj, place, True)
    send, recv, g_in_theirs, lands, token = _exchange_start("exchange_start_0", [g_in_theirs], sliced=False)
    g_in_mine = _grad_w_in_half("grad_w_in_own", xn, dproj, place, False, deps=(token,))
    _, theirs = _exchange_wait("exchange_wait_0", send, recv, g_in_theirs, lands, (g_in_mine,), sliced=False)
    red_in, token = reduce_start(0, [g_in_mine], theirs)
    grad_x, d_norm_mix = _in_proj_bwd(dproj, win_f, x2, dh1, norm_mix, deps=(token,))

    sh_down, token = share_begin(3, red_down, (grad_x,))
    sh_up, token = share_begin(2, red_up, (token,))
    done = share_end(3, sh_down, (token,))
    sh_mid, token = share_begin(1, red_mid, (done,))
    done = share_end(2, sh_up, (token,))
    sh_in, token = share_begin(0, red_in, (done,))
    done = share_end(1, sh_mid, (token,))
    done = share_end(0, sh_in, (done,))

    d_rpb = drpb_t[:, :15, GRID_W - WIN_C:GRID_W + WIN_C - 1]
    small_g = [d_norm_mix, jnp.concatenate([dba, dbb], axis=1), dn[0, 0], dn[1, 0], dn[0, 1], dn[1, 1], d_rpb, d_norm_ffn]
    gathered_small = _allgather_small(_pack_small(small_g), done)
    small_w = (norm_mix, b_gate, q_norm_a, k_norm_a, q_norm_b, k_norm_b, rpb_b, norm_ffn)
    small_m = (m_norm_mix, m_b_gate, m_q_norm_a, m_k_norm_a, m_q_norm_b, m_k_norm_b, m_rpb_b, m_norm_ffn)
    small_v = (v_norm_mix, v_b_gate, v_q_norm_a, v_k_norm_a, v_q_norm_b, v_k_norm_b, v_rpb_b, v_norm_ffn)
    small_out = [_unpack_small(p) for p in
                 _adam_small(gathered_small, _pack_small(small_w), _pack_small(small_m), _pack_small(small_v))]

    order = ("norm_mix", "w_in", "b_gate", "q_norm_a", "k_norm_a", "q_norm_b", "k_norm_b", "rpb_b",
             "w_proj_a", "w_proj_b", "w_out", "norm_ffn", "w_up", "w_down")
    small_idx = {name: i for i, (name, _) in enumerate(SMALL)}
    outs = []
    for kind in range(4):
        for name in order:
            if name in small_idx:
                outs.append(small_out[kind][small_idx[name]])
            else:
                outs.append(big_out[big_names.index(name)][kind][None])
    return (loss, grad_x[None], *outs)
```

```python
import functools

import numpy as np
import jax
import jax.numpy as jnp
from jax import lax
from jax.experimental import pallas as pl
from jax.experimental.pallas import tpu as pltpu

F32, BF16 = jnp.float32, jnp.bfloat16
SDS = jax.ShapeDtypeStruct
MESH = pl.DeviceIdType.MESH

T = 2048
D = 2048
HD = 128
NH, NHA = 16, 12
DIN = 10240
DFF = 8192
NSH = 4
DILS = (1, 4, 16)
EPS = 1e-6
NEG = -1e30
SCALE = HD ** -0.5
GRID_W, WIN_R, WIN_C = 64, 8, 16
VMEM_LIMIT = 56 * 1024 * 1024
B1, B2, LR, AEPS, WD, STEP = 0.9, 0.999, 0.001, 1e-08, 0.01, 10
SMALL_ROWS = 88


def _dot(a, b):
    return jnp.dot(a, b, preferred_element_type=F32)


def _dot_nt(a, b):
    return lax.dot_general(a, b, (((1,), (1,)), ((), ())), preferred_element_type=F32)


def _dot_tn(a, b):
    return lax.dot_general(a, b, (((0,), (0,)), ((), ())), preferred_element_type=F32)


def _params(n):
    return pltpu.CompilerParams(dimension_semantics=("arbitrary",) * n, vmem_limit_bytes=VMEM_LIMIT)


def _resident(shape, index_map):
    return pl.BlockSpec(shape, index_map, pipeline_mode=pl.Buffered(1))


def _sigmoid(z):
    return 1.0 / (1.0 + jnp.exp(-z))


def _wide(v, n):
    return jnp.concatenate([v] * n, axis=1)


def _row_tile(rows, cols, elems):
    tr = 16
    while tr * 2 <= rows and tr * 2 * cols <= elems:
        tr *= 2
    return tr


def _place():
    x, y, c = lax.axis_index("x"), lax.axis_index("y"), lax.axis_index("c")
    peers = [(1 - x, y), (x, 1 - y), (1 - x, 1 - y)]
    return x, y, c, peers


def _cast_into_place(w, name, place, deps=()):
    rows, cols = w.shape
    hr = rows // 2
    tr = min(hr, 256)
    per = hr // tr

    def body(*refs):
        w_ref, o_ref = refs[-2:]
        o_ref[...] = w_ref[...].astype(BF16)

    return pl.pallas_call(
        body, name=name, out_shape=SDS((NSH, 2, hr, cols), BF16),
        grid_spec=pltpu.PrefetchScalarGridSpec(
            num_scalar_prefetch=1, grid=(2, per),
            in_specs=[DEP_SPEC] * len(deps) + [pl.BlockSpec((tr, cols), lambda h, i, p: (h * per + i, 0))],
            out_specs=pl.BlockSpec((None, None, tr, cols), lambda h, i, p: (p[0], h, i, 0))),
        compiler_params=_params(2))(place, *deps, w)


ANY_SPEC = pl.BlockSpec(memory_space=pl.ANY)
HBM_SPEC = pl.BlockSpec(memory_space=pltpu.HBM)
SEM_SPEC = pl.BlockSpec(memory_space=pltpu.SEMAPHORE)
DEP_SPEC = pl.BlockSpec((8, 128), lambda *_: (0, 0))
EFFECT = pltpu.SideEffectType.DATAFLOW_SIDE_EFFECTING


def _after(body, deps):
    n = len(deps)
    return (lambda *refs: body(*refs[n:])) if n else body


SIBLING_BARRIER = 1


def _split_start(name, srcs, lands, n_copies, issue, sibling_only=False, after=()):
    n, m, d = len(srcs), len(lands), len(after)

    def body(*refs):
        if sibling_only:
            x, y, c, _ = _place()
            barrier = pltpu.get_barrier_semaphore()
            pl.semaphore_signal(barrier, inc=1, device_id=(x, y, 1 - c), device_id_type=MESH)
            pl.semaphore_wait(barrier, 1)
        issue(refs[:n], refs[n:n + m], refs[n + m + d], refs[n + m + d + 1])
        refs[-1][...] = jnp.zeros((8, 128), F32)

    arrays = list(srcs) + list(lands)
    outs = pl.pallas_call(
        body, name=name,
        out_shape=(pltpu.SemaphoreType.DMA((n_copies,)), pltpu.SemaphoreType.DMA((n_copies,)),
                   *[pltpu.HBM(a.shape, a.dtype) for a in arrays], SDS((8, 128), F32)),
        in_specs=[HBM_SPEC] * (n + m) + [ANY_SPEC] * d,
        out_specs=(SEM_SPEC, SEM_SPEC, *[HBM_SPEC] * (n + m), pl.BlockSpec(memory_space=pltpu.VMEM)),
        input_output_aliases={i: 2 + i for i in range(n + m)},
        compiler_params=pltpu.CompilerParams(has_side_effects=EFFECT,
                                             collective_id=SIBLING_BARRIER if sibling_only else None),
    )(*[pltpu.with_memory_space_constraint(a, pltpu.HBM) for a in arrays], *after)
    return outs[0], outs[1], list(outs[2:2 + n]), list(outs[2 + n:2 + n + m]), outs[-1]


def _split_wait(name, send_sems, recv_sems, srcs, lands, after, wait):
    n, m = len(srcs), len(lands)

    def body(*refs):
        wait(refs[:n], refs[n:n + m], refs[n + m], refs[n + m + 1])

    arrays = list(srcs) + list(lands)
    outs = pl.pallas_call(
        body, name=name, out_shape=[pltpu.HBM(a.shape, a.dtype) for a in arrays],
        in_specs=[HBM_SPEC] * (n + m) + [SEM_SPEC, SEM_SPEC] + [ANY_SPEC] * len(after),
        out_specs=[HBM_SPEC] * (n + m), input_output_aliases={i: i for i in range(n + m)},
        compiler_params=pltpu.CompilerParams(has_side_effects=EFFECT),
    )(*arrays, send_sems, recv_sems, *after)
    return list(outs[:n]), list(outs[n:])


def _gather_start(name, fulls, ks=(0, 1, 2), after=()):
    def issue(srcs, dsts, send_sems, recv_sems):
        x, y, c, peers = _place()
        for i in range(len(fulls)):
            mine = dsts[i].at[2 * x + y, c]
            for k in ks:
                px, py = peers[k]
                pltpu.make_async_remote_copy(
                    src_ref=mine, dst_ref=mine, send_sem=send_sems.at[3 * i + k],
                    recv_sem=recv_sems.at[3 * i + k], device_id=(px, py, c), device_id_type=MESH).start()

    return _split_start(name, [], fulls, 3 * len(fulls), issue, after=after)


def _gather_wait(name, send_sems, recv_sems, fulls, after, ks=(0, 1, 2)):
    def wait(srcs, dsts, send_sems, recv_sems):
        x, y, c, peers = _place()
        for i in range(len(fulls)):
            for k in ks:
                px, py = peers[k]
                cp = pltpu.make_async_remote_copy(
                    src_ref=dsts[i].at[2 * x + y, c], dst_ref=dsts[i].at[2 * px + py, c],
                    send_sem=send_sems.at[3 * i + k], recv_sem=recv_sems.at[3 * i + k],
                    device_id=(px, py, c), device_id_type=MESH)
                cp.wait_send()
                cp.wait_recv()

    return _split_wait(name, send_sems, recv_sems, [], fulls, after, wait)[1]


def _gather_finish(name, fulls, ks=(0, 1, 2)):
    n = len(fulls)

    def body(*refs):
        fin, fout = refs[:n], refs[n:2 * n]
        send_sems, recv_sems = refs[2 * n:]
        x, y, c, peers = _place()

        def copy(i, k, half):
            px, py = peers[k]
            return pltpu.make_async_remote_copy(
                src_ref=fin[i].at[2 * px + py, half], dst_ref=fout[i].at[2 * px + py, half],
                send_sem=send_sems.at[3 * i + k], recv_sem=recv_sems.at[3 * i + k],
                device_id=(x, y, 1 - c), device_id_type=MESH)

        sends = [copy(i, k, c) for i in range(n) for k in ks]
        for cp in sends:
            cp.start()
        for i in range(n):
            for k in ks:
                copy(i, k, 1 - c).wait_recv()
        for cp in sends:
            cp.wait_send()

    return pl.pallas_call(
        body, name=name, out_shape=[SDS(f.shape, f.dtype) for f in fulls],
        in_specs=[ANY_SPEC] * n, out_specs=[ANY_SPEC] * n, input_output_aliases={i: i for i in range(n)},
        scratch_shapes=[pltpu.SemaphoreType.DMA((3 * n,)), pltpu.SemaphoreType.DMA((3 * n,))])(*fulls)


def _reduce_start(name, parts):
    lands = [lax.empty((3,) + p.shape[1:], p.dtype) for p in parts]

    def issue(srcs, dsts, send_sems, recv_sems):
        x, y, c, peers = _place()
        for i in range(len(parts)):
            for k, (px, py) in enumerate(peers):
                pltpu.make_async_remote_copy(
                    src_ref=srcs[i].at[2 * px + py], dst_ref=dsts[i].at[k], send_sem=send_sems.at[3 * i + k],
                    recv_sem=recv_sems.at[3 * i + k], device_id=(px, py, c), device_id_type=MESH).start()

    return _split_start(name, parts, lands, 3 * len(parts), issue)


def _reduce_wait(name, send_sems, recv_sems, parts, lands, after):
    def wait(srcs, dsts, send_sems, recv_sems):
        x, y, c, peers = _place()
        for i in range(len(parts)):
            for k, (px, py) in enumerate(peers):
                cp = pltpu.make_async_remote_copy(
                    src_ref=srcs[i].at[2 * px + py], dst_ref=dsts[i].at[k], send_sem=send_sems.at[3 * i + k],
                    recv_sem=recv_sems.at[3 * i + k], device_id=(px, py, c), device_id_type=MESH)
                cp.wait_send()
                cp.wait_recv()

    return _split_wait(name, send_sems, recv_sems, parts, lands, after, wait)


def _sibling_copy(src, dst, send_sems, recv_sems, k):
    x, y, c, _ = _place()
    return pltpu.make_async_remote_copy(src_ref=src, dst_ref=dst, send_sem=send_sems.at[k], recv_sem=recv_sems.at[k],
                                        device_id=(x, y, 1 - c), device_id_type=MESH)


def _forward_start(name, fulls):
    def issue(srcs, dsts, send_sems, recv_sems):
        x, y, c, peers = _place()
        for i in range(len(fulls)):
            for k, (px, py) in enumerate(peers):
                part = dsts[i].at[2 * px + py, c]
                _sibling_copy(part, part, send_sems, recv_sems, 3 * i + k).start()

    return _split_start(name, [], fulls, 3 * len(fulls), issue, sibling_only=True)


def _forward_wait(name, send_sems, recv_sems, fulls, after):
    def wait(srcs, dsts, send_sems, recv_sems):
        x, y, c, peers = _place()
        for i in range(len(fulls)):
            for k, (px, py) in enumerate(peers):
                cp = _sibling_copy(dsts[i].at[2 * px + py, c], dsts[i].at[2 * px + py, 1 - c], send_sems, recv_sems, 3 * i + k)
                cp.wait_send()
                cp.wait_recv()

    return _split_wait(name, send_sems, recv_sems, [], fulls, after, wait)[1]


def _exchange_start(name, grads, sliced=True):
    lands = [lax.empty((NSH,) + g.shape[-2:], g.dtype) for g in grads]

    def issue(srcs, dsts, send_sems, recv_sems):
        c = lax.axis_index("c")
        for i in range(len(grads)):
            src = srcs[i].at[:, 1 - c] if sliced else srcs[i]
            _sibling_copy(src, dsts[i], send_sems, recv_sems, i).start()

    return _split_start(name, grads, lands, len(grads), issue, sibling_only=True)


def _exchange_wait(name, send_sems, recv_sems, grads, lands, after, sliced=True):
    def wait(srcs, dsts, send_sems, recv_sems):
        c = lax.axis_index("c")
        for i in range(len(grads)):
            cp = _sibling_copy(srcs[i].at[:, 1 - c] if sliced else srcs[i], dsts[i], send_sems, recv_sems, i)
            cp.wait_send()
            cp.wait_recv()

    return _split_wait(name, send_sems, recv_sems, grads, lands, after, wait)


def _share_start(name, sums):
    def issue(srcs, dsts, send_sems, recv_sems):
        c = lax.axis_index("c")
        for i in range(len(sums)):
            _sibling_copy(dsts[i].at[c], dsts[i].at[c], send_sems, recv_sems, i).start()

    return _split_start(name, [], sums, len(sums), issue, sibling_only=True)


def _share_wait(name, send_sems, recv_sems, sums, after):
    def wait(srcs, dsts, send_sems, recv_sems):
        c = lax.axis_index("c")
        for i in range(len(sums)):
            cp = _sibling_copy(dsts[i].at[c], dsts[i].at[1 - c], send_sems, recv_sems, i)
            cp.wait_send()
            cp.wait_recv()

    return _split_wait(name, send_sems, recv_sems, [], sums, after, wait)[1]


def _allgather_small(v, after):
    m_per, n = v.shape

    def body(x_ref, after_ref, out_ref, send_sems, recv_sems, local_sem):
        x, y, c = lax.axis_index("x"), lax.axis_index("y"), lax.axis_index("c")
        me, sibling = (x, y, c), (x, y, 1 - c)
        chips = [(1 - x, y), (x, 1 - y), (1 - x, 1 - y)]

        def rows(px, py, pc):
            return out_ref.at[pl.ds((4 * px + 2 * py + pc) * m_per, m_per), :]

        def copy(k, block, to, src=None):
            return pltpu.make_async_remote_copy(
                src_ref=rows(*block) if src is None else src, dst_ref=rows(*block),
                send_sem=send_sems.at[k], recv_sem=recv_sems.at[k], device_id=to, device_id_type=MESH)

        mine = pltpu.make_async_copy(x_ref, rows(*me), local_sem)
        mine.start()
        first = [copy(0, me, sibling, src=x_ref)]
        first += [copy(1 + j, me, (*chip, c), src=x_ref) for j, chip in enumerate(chips)]
        for cp in first:
            cp.start()
        passed = [copy(4 + j, (*chip, c), sibling) for j, chip in enumerate(chips)]
        for j, chip in enumerate(chips):
            copy(1 + j, (*chip, c), me).wait_recv()
            passed[j].start()
        copy(0, sibling, me).wait_recv()
        for j, chip in enumerate(chips):
            copy(4 + j, (*chip, 1 - c), me).wait_recv()
        for cp in first + passed:
            cp.wait_send()
        mine.wait()

    return pl.pallas_call(
        body, name="allgather_small", out_shape=SDS((8 * m_per, n), v.dtype),
        in_specs=[pl.BlockSpec(memory_space=pltpu.VMEM), ANY_SPEC], out_specs=pl.BlockSpec(memory_space=pltpu.VMEM),
        scratch_shapes=[pltpu.SemaphoreType.DMA((7,)), pltpu.SemaphoreType.DMA((7,)), pltpu.SemaphoreType.DMA])(v, after)


def _norm_in_proj_own(x, g, w_full, place):
    tn, chunk = 512, 256
    per = (DIN // NSH) // tn

    def body(place_ref, x_ref, g_ref, w_ref, proj_ref, xn_ref):
        @pl.when(pl.program_id(0) == 0)
        def _():
            def norm(r, carry):
                rows = pl.ds(pl.multiple_of(r * chunk, chunk), chunk)
                xv = x_ref[rows, :]
                rs = lax.rsqrt(jnp.mean(xv * xv, axis=-1, keepdims=True) + EPS)
                xn_ref[rows, :] = (xv * rs * g_ref[...]).astype(BF16)
                return carry

            lax.fori_loop(0, T // chunk, norm, 0)

        proj_ref[...] = _dot(xn_ref[...], w_ref[...])

    return pl.pallas_call(
        body, name="norm_in_proj_own", out_shape=[SDS((T, DIN), F32), SDS((T, D), BF16)],
        grid_spec=pltpu.PrefetchScalarGridSpec(
            num_scalar_prefetch=1, grid=(per,),
            in_specs=[_resident((T, D), lambda j, p: (0, 0)),
                      pl.BlockSpec((1, D), lambda j, p: (0, 0)),
                      pl.BlockSpec((None, D, tn), lambda j, p: (p[0], 0, j))],
            out_specs=[pl.BlockSpec((T, tn), lambda j, p: (0, p[0] * per + j)),
                       pl.BlockSpec((T, D), lambda j, p: (0, 0))]),
        compiler_params=_params(1))(place, x, g, w_full)


def _in_proj_rest(name, xn, w_full, proj, place, flips):
    tn = 512
    per = (DIN // NSH) // tn

    def body(place_ref, xn_ref, w_ref, proj_in, proj_ref):
        proj_ref[...] = _dot(xn_ref[...], w_ref[...])

    def shard(j, p):
        flip = flips[0]
        for n, f in enumerate(flips[1:]):
            flip = jnp.where(j // per == n + 1, f, flip)
        return p[0] ^ flip

    return pl.pallas_call(
        body, name=name, out_shape=SDS((T, DIN), F32),
        grid_spec=pltpu.PrefetchScalarGridSpec(
            num_scalar_prefetch=1, grid=(len(flips) * per,),
            in_specs=[_resident((T, D), lambda j, p: (0, 0)),
                      pl.BlockSpec((None, D, tn), lambda j, p: (shard(j, p), 0, j % per)), ANY_SPEC],
            out_specs=pl.BlockSpec((T, tn), lambda j, p: (0, shard(j, p) * per + j % per))),
        input_output_aliases={3: 0}, compiler_params=_params(1))(place, xn, w_full, proj)


def _rope_tables():
    pos = np.arange(T, dtype=np.float32)
    inv = (10000.0 ** (-np.arange(0, HD, 2, dtype=np.float32) / HD)).astype(np.float32)
    ang = (pos[:, None] * inv[None, :]).astype(np.float32)
    cos, sin = np.cos(ang).astype(np.float32), np.sin(ang).astype(np.float32)
    return (jnp.asarray(np.concatenate([cos, cos], axis=1)), jnp.asarray(np.concatenate([-sin, sin], axis=1)))


def _qk_prep(proj, nw, cos, sin):
    tm = 256

    def body(p_ref, w_ref, cos_ref, sin_ref, o_ref):
        cv, sv = cos_ref[...], sin_ref[...]
        for h in range(NH):
            sl = slice(h * HD, (h + 1) * HD)
            xv = p_ref[:, sl]
            r = lax.rsqrt(jnp.mean(xv * xv, axis=-1, keepdims=True) + EPS)
            z = xv * r * w_ref[:, sl]
            if h < NHA:
                z = z * cv + pltpu.roll(z, 64, 1) * sv
            o_ref[:, sl] = z.astype(BF16)

    return pl.pallas_call(
        body, name="qk_prep", out_shape=SDS((T, 2 * D), BF16), grid=(T // tm, 2),
        in_specs=[pl.BlockSpec((tm, D), lambda i, j: (i, j)),
                  pl.BlockSpec((None, 1, D), lambda i, j: (j, 0, 0)),
                  pl.BlockSpec((tm, HD), lambda i, j: (i, 0)),
                  pl.BlockSpec((tm, HD), lambda i, j: (i, 0))],
        out_specs=pl.BlockSpec((tm, D), lambda i, j: (i, j)),
        compiler_params=_params(2))(proj, nw, cos, sin)


def _band_mask(q0, m):
    ii = lax.broadcasted_iota(jnp.int32, (128, 256), 0)
    jj = lax.broadcasted_iota(jnp.int32, (128, 256), 1)
    rel = jj - ii
    kpos = jj + (q0 - 64)
    return (rel >= 0) & (rel <= 128) & (kpos >= 0) & (kpos < m)


def _fill_padded(dst, src, m):
    zeros = jnp.zeros((64, HD), dst.dtype)
    dst[0:64, :] = zeros
    dst[64 + m:128 + m, :] = zeros
    dst[64:64 + m, :] = src.astype(dst.dtype)


def _residue_rows(r, m, dil):
    return pl.ds(r, m, stride=dil) if dil > 1 else slice(None)


def _head_blocks(g):
    col = lambda base: pl.BlockSpec((T, HD), lambda h: (0, base + g * 4 + h))
    return col(0), col(NH), col(2 * NH), pl.BlockSpec((T, HD), lambda h: (0, h))


def _attn_a_fwd(qkn, proj, g):
    dil = DILS[g]
    m = T // dil
    nb = m // 128

    def body(q_ref, k_ref, v_ref, o_ref, l_ref, qf, kf, qp, kp, vp, ob, lb):
        qf[...] = q_ref[...].astype(F32)
        kf[...] = k_ref[...].astype(F32)
        for r in range(dil):
            rows = _residue_rows(r, m, dil)
            qp[...] = qf[rows, :].astype(BF16)
            _fill_padded(kp, kf[rows, :], m)
            _fill_padded(vp, v_ref[rows, :], m)

            def block(b, carry):
                q0 = pl.multiple_of(b * 128, 128)
                kw, vw = kp[pl.ds(q0, 256), :], vp[pl.ds(q0, 256), :]
                s = _dot_nt(qp[pl.ds(q0, 128), :], kw) * SCALE
                s = jnp.where(_band_mask(q0, m), s, NEG)
                mx = jnp.max(s, axis=-1, keepdims=True)
                e = jnp.exp(s - mx)
                den = jnp.sum(e, axis=-1, keepdims=True)
                ob[pl.ds(q0, 128), :] = _dot((e / den).astype(BF16), vw)
                lb[pl.ds(q0, 128), :] = jnp.broadcast_to(mx + jnp.log(den), (128, HD))
                return carry

            lax.fori_loop(0, nb, block, 0, unroll=min(nb, 2))
            o_ref[rows, :] = ob[...]
            l_ref[rows, :] = lb[...]

    q_blk, k_blk, v_blk, out_blk = _head_blocks(g)
    return pl.pallas_call(
        body, name=f"attn_a_fwd_{g}", out_shape=[SDS((T, 512), F32)] * 2, grid=(4,),
        in_specs=[q_blk, k_blk, v_blk], out_specs=[out_blk] * 2,
        scratch_shapes=[pltpu.VMEM((T, HD), F32), pltpu.VMEM((T, HD), F32), pltpu.VMEM((m, HD), BF16),
                        pltpu.VMEM((m + 128, HD), BF16), pltpu.VMEM((m + 128, HD), BF16),
                        pltpu.VMEM((m, HD), F32), pltpu.VMEM((m, HD), F32)],
        compiler_params=_params(1))(qkn, qkn, proj)


def _nbr_window(r):
    start = jnp.clip(r - WIN_R // 2, 0, T // GRID_W - WIN_R)
    return start, start - r + (WIN_R - 1)


def _rpb_rows(rpb):
    zeros = jnp.zeros((4, 14, 33), F32)
    a, b = rpb[:, :14], rpb[:, 1:15]
    rows = jnp.concatenate([a[:, :, 15:31], zeros, b, zeros, a[:, :, 0:15]], axis=2)
    return jnp.pad(rows, ((0, 0), (0, 2), (0, 0)))


def _attn_b_fwd(qkn, proj, rpb_rows):
    def body(r_ref, q_ref, k_ref, v_ref, o_ref, l_ref, bias_ref, vb, pair):
        qc = lax.broadcasted_iota(jnp.int32, (GRID_W, 512), 0)
        kc = lax.broadcasted_iota(jnp.int32, (GRID_W, 512), 1) & (GRID_W - 1)
        cs = jnp.clip(qc - WIN_C // 2, 0, GRID_W - WIN_C)
        colmask = (kc >= cs) & (kc < cs + WIN_C)
        for d in range(14):
            pair[d] = pltpu.roll(jnp.broadcast_to(r_ref[d:d + 1, :], (GRID_W, HD)), 0, 1, stride=1, stride_axis=0)
        for off in range(8):
            rows = jnp.concatenate([pair[off + 2 * jj] for jj in range(4)], axis=1)
            bias_ref[off] = jnp.where(colmask, rows, NEG)
        vb[...] = v_ref[...].astype(BF16)

        def row(r, carry):
            start, off = _nbr_window(r)
            q0 = pl.multiple_of(r * GRID_W, GRID_W)
            k0 = pl.multiple_of(start * GRID_W, GRID_W)
            s = _dot_nt(q_ref[pl.ds(q0, GRID_W), :], k_ref[pl.ds(k0, 512), :]) * SCALE + bias_ref[off]
            mx = jnp.max(s, axis=-1, keepdims=True)
            e = jnp.exp(s - mx)
            den = jnp.sum(e, axis=-1, keepdims=True)
            o_ref[pl.ds(q0, GRID_W), :] = _dot((e / den).astype(BF16), vb[pl.ds(k0, 512), :])
            l_ref[pl.ds(q0, GRID_W), :] = jnp.broadcast_to(mx + jnp.log(den), (GRID_W, HD))
            return carry

        lax.fori_loop(0, T // GRID_W, row, 0, unroll=2)

    return pl.pallas_call(
        body, name="attn_b_fwd",
        out_shape=[SDS((T, 512), F32), SDS((T, 512), F32), SDS((4, 8, GRID_W, 512), F32)], grid=(4,),
        in_specs=[pl.BlockSpec((None, 16, HD), lambda h: (h, 0, 0)),
                  pl.BlockSpec((T, HD), lambda h: (0, NHA + h)),
                  pl.BlockSpec((T, HD), lambda h: (0, NH + NHA + h)),
                  pl.BlockSpec((T, HD), lambda h: (0, 2 * NH + NHA + h))],
        out_specs=[pl.BlockSpec((T, HD), lambda h: (0, h)), pl.BlockSpec((T, HD), lambda h: (0, h)),
                   pl.BlockSpec((None, 8, GRID_W, 512), lambda h: (h, 0, 0, 0))],
        scratch_shapes=[pltpu.VMEM((T, HD), BF16), pltpu.VMEM((14, GRID_W, HD), F32)],
        compiler_params=_params(1))(rpb_rows, qkn, qkn, proj)


def _comb_fwd(os, ls):
    tm = 512

    def body(o0, o1, o2, l0, l1, l2, oa_ref, w0, w1, w2):
        lv = [l0[...], l1[...], l2[...]]
        mx = jnp.maximum(jnp.maximum(lv[0], lv[1]), lv[2])
        ev = [jnp.exp(l - mx) for l in lv]
        den = ev[0] + ev[1] + ev[2]
        wv = [e / den for e in ev]
        oa_ref[...] = (wv[0] * o0[...] + wv[1] * o1[...] + wv[2] * o2[...]).astype(BF16)
        w0[...], w1[...], w2[...] = wv

    spec = pl.BlockSpec((tm, 512), lambda i: (i, 0))
    return pl.pallas_call(
        body, name="comb_fwd", out_shape=[SDS((T, 512), BF16)] + [SDS((T, 512), F32)] * 3, grid=(T // tm,),
        in_specs=[spec] * 6, out_specs=[spec] * 4, compiler_params=_params(1))(*os, *ls)


def _mix_fwd(oa, ob, proj, b_gate, wpa, wpb):
    tm = 512

    def body(oa_ref, ob_ref, ga_ref, gb_ref, ba_ref, bb_ref, wpa_ref, wpb_ref, mixed_ref, ob16_ref):
        oav = oa_ref[...]
        obv = ob_ref[...].astype(BF16)
        ob16_ref[...] = obv
        for s in range(NSH):
            sl = slice(s * 512, (s + 1) * 512)
            ga = _sigmoid(ga_ref[:, sl] + ba_ref[:, sl])
            gb = _sigmoid(gb_ref[:, sl] + bb_ref[:, sl])
            mixed_ref[:, sl] = (ga * _dot(oav, wpa_ref[s]) + gb * _dot(obv, wpb_ref[s])).astype(BF16)

    row = lambda w: pl.BlockSpec((tm, w), lambda i: (i, 0))
    return pl.pallas_call(
        body, name="mix_fwd", out_shape=[SDS((T, D), BF16), SDS((T, 512), BF16)], grid=(T // tm,),
        in_specs=[row(512), row(512),
                  pl.BlockSpec((tm, D), lambda i: (i, 3)), pl.BlockSpec((tm, D), lambda i: (i, 4)),
                  pl.BlockSpec((1, D), lambda i: (0, 0)), pl.BlockSpec((1, D), lambda i: (0, 1)),
                  _resident((NSH, 512, 512), lambda i: (0, 0, 0)), _resident((NSH, 512, 512), lambda i: (0, 0, 0))],
        out_specs=[row(D), row(512)], compiler_params=_params(1))(oa, ob, proj, proj, b_gate, b_gate, wpa, wpb)


def _out_proj_fwd(mixed, w_out, x, g):
    tm = 512

    def body(m_ref, w_ref, x_ref, g_ref, h1_ref, hn_ref):
        h1 = x_ref[...] + _dot(m_ref[...], w_ref[...])
        h1_ref[...] = h1
        r = lax.rsqrt(jnp.mean(h1 * h1, axis=-1, keepdims=True) + EPS)
        hn_ref[...] = (h1 * r * g_ref[...]).astype(BF16)

    row = pl.BlockSpec((tm, D), lambda i: (i, 0))
    return pl.pallas_call(
        body, name="out_proj_fwd", out_shape=[SDS((T, D), F32), SDS((T, D), BF16)], grid=(T // tm,),
        in_specs=[row, _resident((D, D), lambda i: (0, 0)), row, pl.BlockSpec((1, D), lambda i: (0, 0))],
        out_specs=[row, row], compiler_params=_params(1))(mixed, w_out, x, g)


def _ffn_up(hn, w_up):
    tm, tn = T, 512
    per = (DFF // NSH) // tn

    def body(h_ref, w_ref, a_ref, u_ref):
        uv = jnp.maximum(_dot(h_ref[...], w_ref[...]), 0.0)
        a_ref[...] = (uv * uv).astype(BF16)
        u_ref[...] = uv.astype(BF16)

    out = pl.BlockSpec((tm, tn), lambda i, j: (i, j))
    return pl.pallas_call(
        body, name="ffn_up", out_shape=[SDS((T, DFF), BF16)] * 2, grid=(T // tm, DFF // tn),
        in_specs=[pl.BlockSpec((tm, D), lambda i, j: (i, 0)),
                  pl.BlockSpec((None, D, tn), lambda i, j: (j // per, 0, j % per))],
        out_specs=[out, out], compiler_params=_params(2))(hn, w_up)


def _ffn_down_own(u, w_down, place):
    tm, tk = 512, DFF // NSH

    def body(place_ref, u_ref, w_ref, o_ref):
        o_ref[...] = _dot(u_ref[...], w_ref[...])

    return pl.pallas_call(
        body, name="ffn_down_own", out_shape=SDS((T, D), F32),
        grid_spec=pltpu.PrefetchScalarGridSpec(
            num_scalar_prefetch=1, grid=(T // tm,),
            in_specs=[pl.BlockSpec((tm, tk), lambda i, p: (i, p[0])), pl.BlockSpec((tk, D), lambda i, p: (p[0], 0))],
            out_specs=pl.BlockSpec((tm, D), lambda i, p: (i, 0))),
        compiler_params=_params(1))(place, u, w_down)


def _ffn_down_loss(u, w_down, h1, target, own, place):
    tm, tk = 512, DFF // NSH
    nk = NSH - 1

    def body(place_ref, u_ref, w_ref, h1_ref, t_ref, own_ref, dy_ref, dy16_ref, loss_ref, acc):
        k = pl.program_id(1)

        @pl.when(k == 0)
        def _():
            acc[...] = own_ref[...]

        acc[...] += _dot(u_ref[...], w_ref[...])

        @pl.when(k == nk - 1)
        def _():
            def chunk(r, sq):
                rows = pl.ds(pl.multiple_of(r * 16, 16), 16)
                err = acc[rows, :] + h1_ref[rows, :] - t_ref[rows, :]
                dy = err * (1.0 / D)
                dy_ref[rows, :] = dy
                dy16_ref[rows, :] = dy.astype(BF16)
                return sq + err * err

            sq = lax.fori_loop(0, tm // 16, chunk, jnp.zeros((16, D), F32), unroll=2)
            part = 0.5 * jnp.sum(jnp.mean(sq, axis=-1, keepdims=True), axis=0, keepdims=True)
            loss_ref[...] = jnp.broadcast_to(part, (8, 128))

    row = pl.BlockSpec((tm, D), lambda i, k, p: (i, 0))
    once = _resident((tm, D), lambda i, k, p: (i, 0))
    shard = lambda k, p: p[0] ^ (k + 1)
    return pl.pallas_call(
        body, name="ffn_down_loss",
        out_shape=[SDS((T, D), F32), SDS((T, D), BF16), SDS((T // tm, 8, 128), F32)],
        grid_spec=pltpu.PrefetchScalarGridSpec(
            num_scalar_prefetch=1, grid=(T // tm, nk),
            in_specs=[pl.BlockSpec((tm, tk), lambda i, k, p: (i, shard(k, p))),
                      pl.BlockSpec((tk, D), lambda i, k, p: (shard(k, p), 0)), once, once, once],
            out_specs=[row, row, pl.BlockSpec((None, 8, 128), lambda i, k, p: (i, 0, 0))],
            scratch_shapes=[pltpu.VMEM((tm, D), F32)]),
        compiler_params=_params(2))(place, u, w_down, h1, target, own)


def _ffn_down_bwd(dy16, w_down, u, deps=()):
    tm, tn = T, 512

    def body(dy_ref, w_ref, u_ref, du_ref):
        uv = u_ref[...].astype(F32)
        du_ref[...] = jnp.where(uv > 0.0, 2.0 * uv * _dot_nt(dy_ref[...], w_ref[...]), 0.0).astype(BF16)

    return pl.pallas_call(
        _after(body, deps), name="ffn_down_bwd", out_shape=SDS((T, DFF), BF16), grid=(T // tm, DFF // tn),
        in_specs=[DEP_SPEC] * len(deps) + [
            pl.BlockSpec((tm, D), lambda i, j: (i, 0)), pl.BlockSpec((tn, D), lambda i, j: (j, 0)),
            pl.BlockSpec((tm, tn), lambda i, j: (i, j))],
        out_specs=pl.BlockSpec((tm, tn), lambda i, j: (i, j)), compiler_params=_params(2))(*deps, dy16, w_down, u)


def _norm_bwd(xv, dz_in, g):
    r = lax.rsqrt(jnp.mean(xv * xv, axis=-1, keepdims=True) + EPS)
    dg = jnp.sum(xv * r * dz_in, axis=0, keepdims=True)
    dz = dz_in * g
    dx = r * dz - xv * (r * r * r) * jnp.mean(xv * dz, axis=-1, keepdims=True)
    return dx, dg


def _ffn_up_bwd(du, w_up, h1, dy, g, deps=()):
    tm, tk = 512, 1024
    per = (DFF // NSH) // tk
    nk = DFF // tk

    def body(du_ref, w_ref, h1_ref, dy_ref, g_ref, dh1_ref, dh16_ref, dg_ref, acc):
        i, k = pl.program_id(0), pl.program_id(1)

        @pl.when(k == 0)
        def _():
            acc[...] = jnp.zeros_like(acc)

        @pl.when((k == 0) & (i == 0))
        def _():
            dg_ref[...] = jnp.zeros_like(dg_ref)

        acc[...] += _dot_nt(du_ref[...], w_ref[...])

        @pl.when(k == nk - 1)
        def _():
            dx, dg = _norm_bwd(h1_ref[...], acc[...], g_ref[...])
            dh1 = dy_ref[...] + dx
            dh1_ref[...] = dh1
            dh16_ref[...] = dh1.astype(BF16)
            dg_ref[...] += dg

    row = pl.BlockSpec((tm, D), lambda i, k: (i, 0))
    vec = pl.BlockSpec((1, D), lambda i, k: (0, 0))
    return pl.pallas_call(
        _after(body, deps), name="ffn_up_bwd", out_shape=[SDS((T, D), F32), SDS((T, D), BF16), SDS((1, D), F32)],
        grid=(T // tm, nk),
        in_specs=[DEP_SPEC] * len(deps) + [
            pl.BlockSpec((tm, tk), lambda i, k: (i, k)),
            pl.BlockSpec((None, D, tk), lambda i, k: (k // per, 0, k % per)), row, row, vec],
        out_specs=[row, row, vec], scratch_shapes=[pltpu.VMEM((tm, D), F32)],
        compiler_params=_params(2))(*deps, du, w_up, h1, dy, g)


def _mix_bwd(dh16, w_out, oa, ob16, proj, b_gate, wpa, wpb):
    tm = 256

    def body(dh_ref, wo_ref, oa_ref, ob_ref, ga_ref, gb_ref, ba_ref, bb_ref, wpa_ref, wpb_ref,
             dya_ref, dyb_ref, dga_ref, dgb_ref, doa_ref, dob_ref, dba_ref, dbb_ref):
        @pl.when(pl.program_id(0) == 0)
        def _():
            dba_ref[...] = jnp.zeros_like(dba_ref)
            dbb_ref[...] = jnp.zeros_like(dbb_ref)

        oav, obv = oa_ref[...], ob_ref[...]
        doa = jnp.zeros((tm, 512), F32)
        dob = jnp.zeros((tm, 512), F32)
        for s in range(NSH):
            sl = slice(s * 512, (s + 1) * 512)
            dm = _dot_nt(dh_ref[...], wo_ref[sl, :])
            ga = _sigmoid(ga_ref[:, sl] + ba_ref[:, sl])
            gb = _sigmoid(gb_ref[:, sl] + bb_ref[:, sl])
            dya = (dm * ga).astype(BF16)
            dyb = (dm * gb).astype(BF16)
            dza = dm * _dot(oav, wpa_ref[s]) * ga * (1.0 - ga)
            dzb = dm * _dot(obv, wpb_ref[s]) * gb * (1.0 - gb)
            dya_ref[:, sl], dyb_ref[:, sl] = dya, dyb
            dga_ref[:, sl], dgb_ref[:, sl] = dza.astype(BF16), dzb.astype(BF16)
            dba_ref[:, sl] += jnp.sum(dza, axis=0, keepdims=True)
            dbb_ref[:, sl] += jnp.sum(dzb, axis=0, keepdims=True)
            doa += _dot_nt(dya, wpa_ref[s])
            dob += _dot_nt(dyb, wpb_ref[s])
        doa_ref[...], dob_ref[...] = doa, dob

    row = lambda w: pl.BlockSpec((tm, w), lambda i: (i, 0))
    vec = pl.BlockSpec((1, D), lambda i: (0, 0))
    wp = _resident((NSH, 512, 512), lambda i: (0, 0, 0))
    return pl.pallas_call(
        body, name="mix_bwd",
        out_shape=[SDS((T, D), BF16)] * 4 + [SDS((T, 512), F32)] * 2 + [SDS((1, D), F32)] * 2, grid=(T // tm,),
        in_specs=[row(D), _resident((D, D), lambda i: (0, 0)), row(512), row(512),
                  pl.BlockSpec((tm, D), lambda i: (i, 3)), pl.BlockSpec((tm, D), lambda i: (i, 4)),
                  pl.BlockSpec((1, D), lambda i: (0, 0)), pl.BlockSpec((1, D), lambda i: (0, 1)), wp, wp],
        out_specs=[row(D)] * 4 + [row(512)] * 2 + [vec] * 2,
        compiler_params=_params(1))(dh16, w_out, oa, ob16, proj, proj, b_gate, b_gate, wpa, wpb)


def _comb_bwd(doa, os, ws, deps=()):
    tm = 512

    def body(d_ref, o0, o1, o2, w0, w1, w2, cc_ref):
        prod = d_ref[...] * (w0[...] * o0[...] + w1[...] * o1[...] + w2[...] * o2[...])
        for h in range(4):
            sl = slice(h * HD, (h + 1) * HD)
            cc_ref[:, sl] = jnp.broadcast_to(jnp.sum(prod[:, sl], axis=-1, keepdims=True), (tm, HD))

    spec = pl.BlockSpec((tm, 512), lambda i: (i, 0))
    return pl.pallas_call(
        _after(body, deps), name="comb_bwd", out_shape=SDS((T, 512), F32), grid=(T // tm,),
        in_specs=[DEP_SPEC] * len(deps) + [spec] * 7, out_specs=spec,
        compiler_params=_params(1))(*deps, doa, *os, *ws)


def _attn_a_bwd(qkn, proj, doa, lse, w, cc, g):
    dil = DILS[g]
    m = T // dil
    nb = m // 128

    def body(q_ref, k_ref, v_ref, d_ref, l_ref, w_ref, c_ref, dqk_ref, dv_ref,
             qf, kf, qp, kp, vp, dp, lp, wsub, cp, dqb, dkp, dvp):
        qf[...] = q_ref[...].astype(F32)
        kf[...] = k_ref[...].astype(F32)
        for r in range(dil):
            sub = _residue_rows(r, m, dil)
            qp[...] = qf[sub, :].astype(BF16)
            _fill_padded(kp, kf[sub, :], m)
            _fill_padded(vp, v_ref[sub, :], m)
            dp[...] = d_ref[sub, :].astype(BF16)
            lp[...], wsub[...], cp[...] = l_ref[sub, :], w_ref[sub, :], c_ref[sub, :]
            dkp[...] = jnp.zeros_like(dkp)
            dvp[...] = jnp.zeros_like(dvp)

            def block(b, carry):
                q0 = pl.multiple_of(b * 128, 128)
                rows = pl.ds(q0, 128)
                win = pl.ds(q0, 256)
                qb, kw, vw = qp[rows, :], kp[win, :], vp[win, :]
                s = _dot_nt(qb, kw) * SCALE
                s = jnp.where(_band_mask(q0, m), s, NEG)
                wp = _wide(wsub[rows, :], 2) * jnp.exp(s - _wide(lp[rows, :], 2))
                dob = dp[rows, :]
                ds = (wp * (_dot_nt(dob, vw) - _wide(cp[rows, :], 2))).astype(BF16)
                dqb[rows, :] = _dot(ds, kw) * SCALE
                dkp[win, :] += _dot_tn(ds, qb) * SCALE
                dvp[win, :] += _dot_tn(wp.astype(BF16), dob)
                return carry

            lax.fori_loop(0, nb, block, 0, unroll=min(nb, 8))
            dqk_ref.at[0][sub, :] = dqb[...]
            dqk_ref.at[1][sub, :] = dkp[64:64 + m, :]
            dv_ref[sub, :] = dvp[64:64 + m, :]

    q_blk, k_blk, v_blk, blk = _head_blocks(g)
    sub16 = pltpu.VMEM((m, HD), BF16)
    sub32 = pltpu.VMEM((m, HD), F32)
    return pl.pallas_call(
        body, name=f"attn_a_bwd_{g}", out_shape=[SDS((2, T, 512), F32), SDS((T, 512), F32)], grid=(4,),
        in_specs=[q_blk, k_blk, v_blk, blk, blk, blk, blk],
        out_specs=[pl.BlockSpec((2, T, HD), lambda h: (0, 0, h)), blk],
        scratch_shapes=[pltpu.VMEM((T, HD), F32), pltpu.VMEM((T, HD), F32), sub16,
                        pltpu.VMEM((m + 128, HD), BF16), pltpu.VMEM((m + 128, HD), BF16), sub16,
                        sub32, sub32, sub32, sub32,
                        pltpu.VMEM((m + 128, HD), F32), pltpu.VMEM((m + 128, HD), F32)],
        compiler_params=_params(1))(qkn, qkn, proj, doa, lse, w, cc)


def _attn_b_bwd(qkn, proj, dob, ob, lse, bias, deps=()):
    def body(q_ref, k_ref, v_ref, d_ref, o_ref, l_ref, bias_ref, dqk_ref, dv_ref, drpb_ref, vb, dk_acc, dv_acc, a_acc):
        vb[...] = v_ref[...].astype(BF16)
        dk_acc[...] = jnp.zeros_like(dk_acc)
        dv_acc[...] = jnp.zeros_like(dv_acc)
        a_acc[...] = jnp.zeros_like(a_acc)

        def row(r, carry):
            start, off = _nbr_window(r)
            rows = pl.ds(pl.multiple_of(r * GRID_W, GRID_W), GRID_W)
            win = pl.ds(pl.multiple_of(start * GRID_W, GRID_W), 512)
            qr, kw, vw = q_ref[rows, :], k_ref[win, :], vb[win, :]
            s = _dot_nt(qr, kw) * SCALE + bias_ref[off]
            p = jnp.exp(s - _wide(l_ref[rows, :], 4))
            dov = d_ref[rows, :]
            delta = jnp.sum(dov * o_ref[rows, :], axis=-1, keepdims=True)
            do16 = dov.astype(BF16)
            ds = p * (_dot_nt(do16, vw) - delta)
            a_acc[off] += ds
            ds16 = ds.astype(BF16)
            dqk_ref[0, rows, :] = _dot(ds16, kw) * SCALE
            dk_acc[win, :] += _dot_tn(ds16, qr) * SCALE
            dv_acc[win, :] += _dot_tn(p.astype(BF16), do16)
            return carry

        lax.fori_loop(0, T // GRID_W, row, 0, unroll=8)
        dqk_ref[1] = dk_acc[...]
        dv_ref[...] = dv_acc[...]

        lane = lax.broadcasted_iota(jnp.int32, (16, HD), 1)
        rowi = lax.broadcasted_iota(jnp.int32, (16, HD), 0)
        low = (lane >= GRID_W - WIN_C) & (lane < GRID_W + WIN_C - 1)
        high = (lane >= HD - WIN_C) | (lane < WIN_C - 1)
        flip = (lax.broadcasted_iota(jnp.int32, (GRID_W, GRID_W), 0)
                + lax.broadcasted_iota(jnp.int32, (GRID_W, GRID_W), 1) == GRID_W - 1).astype(BF16)
        out = jnp.zeros((16, HD), F32)
        for d in range(14):
            acc = None
            for off in range(8):
                if 0 <= d - off <= 6 and (d - off) % 2 == 0:
                    jj = (d - off) // 2
                    piece = a_acc[off, :, jj * HD:(jj + 1) * HD]
                    acc = piece if acc is None else acc + piece
            hi = acc.astype(BF16)
            lo = (acc - hi.astype(F32)).astype(BF16)
            rev = _dot(flip, hi) + _dot(flip, lo)
            v = jnp.sum(pltpu.roll(rev, 0, 1, stride=1, stride_axis=0), axis=0, keepdims=True)
            v = jnp.broadcast_to(v, (16, HD))
            out = out + jnp.where((rowi == d) & low, v, 0.0)
            out = out + jnp.where(rowi == d + 1, pltpu.roll(jnp.where(high, v, 0.0), GRID_W, 1), 0.0)
        drpb_ref[...] = out

    blk = pl.BlockSpec((T, HD), lambda h: (0, h))
    return pl.pallas_call(
        _after(body, deps), name="attn_b_bwd",
        out_shape=[SDS((2, T, 512), F32), SDS((T, 512), F32), SDS((4, 16, HD), F32)], grid=(4,),
        in_specs=[DEP_SPEC] * len(deps) + [
            pl.BlockSpec((T, HD), lambda h: (0, NHA + h)),
            pl.BlockSpec((T, HD), lambda h: (0, NH + NHA + h)),
            pl.BlockSpec((T, HD), lambda h: (0, 2 * NH + NHA + h)), blk, blk, blk,
            pl.BlockSpec((None, 8, GRID_W, 512), lambda h: (h, 0, 0, 0))],
        out_specs=[pl.BlockSpec((2, T, HD), lambda h: (0, 0, h)), blk,
                   pl.BlockSpec((None, 16, HD), lambda h: (h, 0, 0))],
        scratch_shapes=[pltpu.VMEM((T, HD), BF16), pltpu.VMEM((T, HD), F32), pltpu.VMEM((T, HD), F32),
                        pltpu.VMEM((8, GRID_W, 512), F32)],
        compiler_params=_params(1))(*deps, qkn, qkn, proj, dob, ob, lse, bias)


def _qk_bwd(proj, nw, cos, sin, dqk_groups, dqk_b, dvs, dga, dgb):
    tm = 512

    def body(p_ref, w_ref, cos_ref, sin_ref, d0, d1, d2, d3, v0, v1, v2, v3, ga_ref, gb_ref, o_ref, dn_ref):
        j, i = pl.program_id(0), pl.program_id(1)

        @pl.when((j < 2) & (i == 0))
        def _():
            dn_ref[...] = jnp.zeros_like(dn_ref)

        @pl.when(j < 2)
        def _():
            cv, sv = cos_ref[...], sin_ref[...]
            srcs = (d0, d1, d2, d3)
            dna = jnp.zeros((1, HD), F32)
            dnb = jnp.zeros((1, HD), F32)
            for h in range(NH):
                sl = slice(h * HD, (h + 1) * HD)
                dz = srcs[h // 4][:, (h % 4) * HD:(h % 4 + 1) * HD]
                if h < NHA:
                    dz = dz * cv + pltpu.roll(dz * sv, 64, 1)
                dx, dg = _norm_bwd(p_ref[:, sl], dz, w_ref[:, sl])
                o_ref[:, sl] = dx.astype(BF16)
                if h < NHA:
                    dna += dg
                else:
                    dnb += dg
            dn_ref[0:1, :] += dna
            dn_ref[1:2, :] += dnb

        @pl.when(j == 2)
        def _():
            for s, v_ref in enumerate((v0, v1, v2, v3)):
                o_ref[:, s * 512:(s + 1) * 512] = v_ref[...].astype(BF16)

        @pl.when(j == 3)
        def _():
            o_ref[...] = ga_ref[...]

        @pl.when(j == 4)
        def _():
            o_ref[...] = gb_ref[...]

    def rows(used):
        return lambda j, i: (jnp.where(used(j), i, 0), 0)

    qk = lambda j: j < 2
    dspec = pl.BlockSpec((None, tm, 512), lambda j, i: (jnp.minimum(j, 1), jnp.where(j < 2, i, 0), 0))
    vspec = pl.BlockSpec((tm, 512), rows(lambda j: j == 2))
    return pl.pallas_call(
        body, name="qk_bwd", out_shape=[SDS((T, DIN), BF16), SDS((2, 8, HD), F32)], grid=(5, T // tm),
        in_specs=[pl.BlockSpec((tm, D), lambda j, i: (jnp.where(j < 2, i, 0), jnp.minimum(j, 1))),
                  pl.BlockSpec((None, 1, D), lambda j, i: (jnp.minimum(j, 1), 0, 0)),
                  pl.BlockSpec((tm, HD), rows(qk)), pl.BlockSpec((tm, HD), rows(qk)),
                  dspec, dspec, dspec, dspec, vspec, vspec, vspec, vspec,
                  pl.BlockSpec((tm, D), rows(lambda j: j == 3)), pl.BlockSpec((tm, D), rows(lambda j: j == 4))],
        out_specs=[pl.BlockSpec((tm, D), lambda j, i: (i, j)),
                   pl.BlockSpec((None, 8, HD), lambda j, i: (jnp.minimum(j, 1), 0, 0))],
        compiler_params=_params(2))(proj, nw, cos, sin, *dqk_groups, dqk_b, *dvs, dga, dgb)


def _in_proj_bwd(dproj, w_in, x, dh1, g, deps=()):
    tm, tk = 512, 1280
    per = (DIN // NSH) // tk
    nk = DIN // tk

    def body(dp_ref, w_ref, x_ref, dh_ref, g_ref, dx_ref, dg_ref, acc):
        i, k = pl.program_id(0), pl.program_id(1)

        @pl.when(k == 0)
        def _():
            acc[...] = jnp.zeros_like(acc)

        @pl.when((k == 0) & (i == 0))
        def _():
            dg_ref[...] = jnp.zeros_like(dg_ref)

        acc[...] += _dot_nt(dp_ref[...], w_ref[...])

        @pl.when(k == nk - 1)
        def _():
            dx, dg = _norm_bwd(x_ref[...], acc[...], g_ref[...])
            dx_ref[...] = dh_ref[...] + dx
            dg_ref[...] += dg

    row = pl.BlockSpec((tm, D), lambda i, k: (i, 0))
    vec = pl.BlockSpec((1, D), lambda i, k: (0, 0))
    return pl.pallas_call(
        _after(body, deps), name="in_proj_bwd", out_shape=[SDS((T, D), F32), SDS((1, D), F32)], grid=(T // tm, nk),
        in_specs=[DEP_SPEC] * len(deps) + [
            pl.BlockSpec((tm, tk), lambda i, k: (i, k)),
            pl.BlockSpec((None, D, tk), lambda i, k: (k // per, 0, k % per)), row, row, vec],
        out_specs=[row, vec], scratch_shapes=[pltpu.VMEM((tm, D), F32)],
        compiler_params=_params(2))(*deps, dproj, w_in, x, dh1, g)


def _grad_w(name, a, g, shard_rows, rows, cols, tr, tc):
    ni, nj = rows // tr, cols // tc
    if shard_rows:
        a_map, g_map = (lambda s, i, j: (0, s * ni + i)), (lambda s, i, j: (0, j))
    else:
        a_map, g_map = (lambda s, i, j: (0, i)), (lambda s, i, j: (0, s * nj + j))

    def body(a_ref, g_ref, o_ref):
        o_ref[...] = _dot_tn(a_ref[...], g_ref[...]).astype(BF16)

    return pl.pallas_call(
        body, name=name, out_shape=SDS((NSH, rows, cols), BF16), grid=(NSH, ni, nj),
        in_specs=[pl.BlockSpec((T, tr), a_map), pl.BlockSpec((T, tc), g_map)],
        out_specs=pl.BlockSpec((None, tr, tc), lambda s, i, j: (s, i, j)), compiler_params=_params(3))(a, g)


def _grad_w_in_half(name, xn, dproj, place, for_sibling, deps=()):
    tr, tc = D // 2, 1280
    nj = (DIN // NSH) // tc

    def body(*refs):
        a_ref, g_ref, o_ref = refs[-3:]
        o_ref[...] = _dot_tn(a_ref[...], g_ref[...]).astype(BF16)

    half = (lambda p: 1 - p[1]) if for_sibling else (lambda p: p[1])
    return pl.pallas_call(
        body, name=name, out_shape=SDS((NSH, tr, DIN // NSH), BF16),
        grid_spec=pltpu.PrefetchScalarGridSpec(
            num_scalar_prefetch=1, grid=(NSH, nj),
            in_specs=[DEP_SPEC] * len(deps) + [pl.BlockSpec((T, tr), lambda s, j, p: (0, half(p))),
                                               pl.BlockSpec((T, tc), lambda s, j, p: (0, s * nj + j))],
            out_specs=pl.BlockSpec((None, tr, tc), lambda s, j, p: (s, 0, j))),
        compiler_params=_params(2))(place, *deps, xn, dproj)


def _adamw(w, g, m, v):
    m = B1 * m + (1.0 - B1) * g
    v = B2 * v + (1.0 - B2) * (g * g)
    m_hat = m / (1.0 - B1 ** STEP)
    v_hat = v / (1.0 - B2 ** STEP)
    delta = -LR * (m_hat / (jnp.sqrt(v_hat) + AEPS) + WD * w)
    return delta, m, v


def _sum_halves(name, place, grads, theirs):
    _, rows, cols = theirs.shape
    tr = _row_tile(rows, cols, 1 << 20)

    def body(place_ref, a_ref, b_ref, o_ref):
        o_ref[...] = (a_ref[...].astype(F32) + b_ref[...].astype(F32)).astype(BF16)

    spec = pl.BlockSpec((None, tr, cols), lambda s, i, p: (s, i, 0))
    mine = spec if grads.ndim == 3 else pl.BlockSpec((None, None, tr, cols), lambda s, i, p: (s, p[1], i, 0))
    return pl.pallas_call(
        body, name=name, out_shape=SDS(theirs.shape, BF16),
        grid_spec=pltpu.PrefetchScalarGridSpec(
            num_scalar_prefetch=1, grid=(NSH, rows // tr), in_specs=[mine, spec], out_specs=spec),
        compiler_params=_params(2))(place, grads, theirs)


def _sum_landed(name, place, part, landed):
    _, rows, cols = part.shape
    tr = _row_tile(rows, cols, 1 << 20)

    def body(place_ref, p_ref, l_ref, o_ref):
        o_ref[...] = ((p_ref[...].astype(F32) + l_ref[0].astype(F32)) + l_ref[1].astype(F32)) + l_ref[2].astype(F32)

    return pl.pallas_call(
        body, name=name, out_shape=SDS((2, rows, cols), F32),
        grid_spec=pltpu.PrefetchScalarGridSpec(
            num_scalar_prefetch=1, grid=(rows // tr,),
            in_specs=[pl.BlockSpec((None, tr, cols), lambda i, p: (p[0], i, 0)),
                      pl.BlockSpec((3, tr, cols), lambda i, p: (0, i, 0))],
            out_specs=pl.BlockSpec((None, tr, cols), lambda i, p: (p[1], i, 0))),
        compiler_params=_params(1))(place, part, landed)


def _adam_shard(name, g, w, m, v):
    rows, cols = w.shape
    tr = _row_tile(rows, cols, 1 << 19)

    def body(g_ref, w_ref, m_ref, v_ref, go_ref, d_ref, nm_ref, nv_ref):
        g = g_ref[...]
        go_ref[...] = g
        d_ref[...], nm_ref[...], nv_ref[...] = _adamw(w_ref[...], g, m_ref[...], v_ref[...])

    spec = pl.BlockSpec((tr, cols), lambda i: (i, 0))
    return pl.pallas_call(
        body, name=name, out_shape=[SDS((rows, cols), F32)] * 4, grid=(rows // tr,),
        in_specs=[spec] * 4, out_specs=[spec] * 4, compiler_params=_params(1))(g, w, m, v)


def _adam_small(gathered, w, m, v):
    def body(g_ref, w_ref, m_ref, v_ref, go_ref, d_ref, nm_ref, nv_ref):
        g = g_ref[0:SMALL_ROWS, :]
        for dev in range(1, 8):
            g = g + g_ref[dev * SMALL_ROWS:(dev + 1) * SMALL_ROWS, :]
        go_ref[...] = g
        d_ref[...], nm_ref[...], nv_ref[...] = _adamw(w_ref[...], g, m_ref[...], v_ref[...])

    return pl.pallas_call(body, name="adam_small", out_shape=[SDS((SMALL_ROWS, HD), F32)] * 4)(gathered, w, m, v)


SMALL = (("norm_mix", (1, D)), ("b_gate", (1, 2 * D)), ("q_norm_a", (1, HD)), ("k_norm_a", (1, HD)),
         ("q_norm_b", (1, HD)), ("k_norm_b", (1, HD)), ("rpb_b", (1, 4, 15, 31)), ("norm_ffn", (1, D)))


def _pack_small(vals):
    pieces = []
    for (name, shape), val in zip(SMALL, vals):
        flat = val.reshape(-1)
        pad = (-flat.shape[0]) % HD
        pieces.append(jnp.pad(flat, (0, pad)).reshape(-1, HD))
    packed = jnp.concatenate(pieces, axis=0)
    return jnp.pad(packed, ((0, SMALL_ROWS - packed.shape[0]), (0, 0)))


def _unpack_small(packed):
    out, row = [], 0
    for name, shape in SMALL:
        size = int(np.prod(shape))
        nrows = -(-size // HD)
        out.append(packed[row:row + nrows].reshape(-1)[:size].reshape(shape))
        row += nrows
    return out


def kernel(x, norm_mix, w_in, b_gate, q_norm_a, k_norm_a, q_norm_b, k_norm_b, rpb_b, w_proj_a, w_proj_b, w_out, norm_ffn, w_up, w_down, loss_target, m_norm_mix, m_w_in, m_b_gate, m_q_norm_a, m_k_norm_a, m_q_norm_b, m_k_norm_b, m_rpb_b, m_w_proj_a, m_w_proj_b, m_w_out, m_norm_ffn, m_w_up, m_w_down, v_norm_mix, v_w_in, v_b_gate, v_q_norm_a, v_k_norm_a, v_q_norm_b, v_k_norm_b, v_rpb_b, v_w_proj_a, v_w_proj_b, v_w_out, v_norm_ffn, v_w_up, v_w_down):
    big_names = ("w_in", "w_proj_a", "w_proj_b", "w_out", "w_up", "w_down")
    big_w = [a[0] for a in (w_in, w_proj_a, w_proj_b, w_out, w_up, w_down)]
    big_m = [a[0] for a in (m_w_in, m_w_proj_a, m_w_proj_b, m_w_out, m_w_up, m_w_down)]
    big_v = [a[0] for a in (v_w_in, v_w_proj_a, v_w_proj_b, v_w_out, v_w_up, v_w_down)]
    x2, target = x[0], loss_target[0]

    place = jnp.stack([2 * lax.axis_index("x") + lax.axis_index("y"), lax.axis_index("c")]).astype(jnp.int32)
    groups = ((0,), (1, 2, 3), (4,), (5,))
    started = []
    for j, grp in enumerate(groups):
        deps = (started[0][4],) if j else ()
        placed = [_cast_into_place(big_w[i], "cast_" + big_names[i], place, deps) for i in grp]
        started.append(_gather_start(f"gather_start_{j}", placed))

    def whole(fulls):
        return [f.reshape(NSH, 2 * f.shape[2], f.shape[3]) for f in fulls]

    def forward_begin(j, after):
        send, recv, _, fulls, _ = started[j]
        fulls = _gather_wait(f"gather_wait_{j}", send, recv, fulls, after)
        send, recv, _, fulls, token = _forward_start(f"forward_start_{j}", fulls)
        return (send, recv, fulls), token

    def forward_end(j, state, after):
        return whole(_forward_wait(f"forward_wait_{j}", *state, after))

    def as_halves(grads):
        return [g.reshape(NSH, 2, g.shape[1] // 2, g.shape[2]) for g in grads]

    def reduce_start(j, grads, theirs):
        parts = [_sum_halves(f"sum_halves_{j}_{i}", place, a, b) for i, (a, b) in enumerate(zip(grads, theirs))]
        send, recv, parts, lands, token = _reduce_start(f"reduce_start_{j}", parts)
        return (send, recv, parts, lands), token

    def exchange_begin(j, grads):
        send, recv, grads, lands, token = _exchange_start(f"exchange_start_{j}", as_halves(grads))
        return (send, recv, grads, lands), token

    def exchange_end(j, state, after):
        return reduce_start(j, *_exchange_wait(f"exchange_wait_{j}", *state, after))

    big_out = {}

    def share_begin(j, state, after):
        send, recv, parts, lands = state
        parts, lands = _reduce_wait(f"reduce_wait_{j}", send, recv, parts, lands, after)
        sums = [_sum_landed(f"sum_landed_{j}_{i}", place, p, l) for i, (p, l) in enumerate(zip(parts, lands))]
        send, recv, _, sums, token = _share_start(f"share_start_{j}", sums)
        return (send, recv, sums), token

    def share_end(j, state, after):
        for idx, g in zip(groups[j], _share_wait(f"share_wait_{j}", *state, after)):
            g = g.reshape(big_w[idx].shape)
            big_out[idx] = _adam_shard("adam_" + big_names[idx], g, big_w[idx], big_m[idx], big_v[idx])
        return big_out[groups[j][-1]][1]

    proj, xn = _norm_in_proj_own(x2, norm_mix, whole(started[0][3])[0], place)
    send, recv, _, win, _ = started[0]
    win = _gather_wait("gather_wait_0", send, recv, win, (proj, *[s[4] for s in started[1:]]))
    (win_f,) = whole(_gather_finish("gather_finish_0", win))
    proj = _in_proj_rest("in_proj_rest", xn, win_f, proj, place, (2, 1, 3))
    cos, sin = _rope_tables()
    nw = jnp.stack([jnp.concatenate([jnp.tile(q_norm_a, (1, NHA)), jnp.tile(q_norm_b, (1, NH - NHA))], axis=1),
                    jnp.concatenate([jnp.tile(k_norm_a, (1, NHA)), jnp.tile(k_norm_b, (1, NH - NHA))], axis=1)])
    qkn = _qk_prep(proj, nw, cos, sin)
    fw1, token = forward_begin(1, (qkn,))
    fwd_a = [_attn_a_fwd(qkn, proj, g) for g in range(3)]
    os, ls = [f[0] for f in fwd_a], [f[1] for f in fwd_a]
    fw2, token = forward_begin(2, (os[2], token))
    ob, lse_b, bias = _attn_b_fwd(qkn, proj, _rpb_rows(rpb_b[0]))
    oa, w0, w1, w2 = _comb_fwd(os, ls)
    ws = [w0, w1, w2]
    wpa_f, wpb_f, wout_f = forward_end(1, fw1, (oa, token))
    wout_f = wout_f.reshape(D, D)
    mixed, ob16 = _mix_fwd(oa, ob, proj, b_gate, wpa_f, wpb_f)
    h1, hn = _out_proj_fwd(mixed, wout_f, x2, norm_ffn)
    (wup_f,) = forward_end(2, fw2, (hn,))
    usq, u = _ffn_up(hn, wup_f)
    fw3, token = forward_begin(3, (u,))
    own = _ffn_down_own(usq, whole(fw3[2])[0].reshape(DFF, D), place)
    (wdown_f,) = forward_end(3, fw3, (own, token))
    wdown_f = wdown_f.reshape(DFF, D)
    dy, dy16, loss_parts = _ffn_down_loss(usq, wdown_f, h1, target, own, place)
    loss = lax.psum(jnp.sum(loss_parts[:, 0, 0]), ("x", "y", "c"))

    g_down = _grad_w("grad_w_down", usq, dy16, True, DFF // NSH, D, 1024, 1024)
    ex_down, token = exchange_begin(3, [g_down])
    du = _ffn_down_bwd(dy16, wdown_f, u, deps=(token,))
    g_up = _grad_w("grad_w_up", hn, du, False, D, DFF // NSH, 1024, 1024)
    red_down, token = exchange_end(3, ex_down, (g_up,))
    ex_up, token_up = exchange_begin(2, [g_up])
    dh1, dh16, d_norm_ffn = _ffn_up_bwd(du, wup_f, h1, dy, norm_ffn, deps=(token, token_up))
    dya, dyb, dga, dgb, doa, dob, dba, dbb = _mix_bwd(dh16, wout_f, oa, ob16, proj, b_gate, wpa_f, wpb_f)
    g_out = _grad_w("grad_w_out", mixed, dh16, True, D // NSH, D, 512, 1024)
    g_pa = _grad_w("grad_w_proj_a", oa, dya, False, 512, 512, 512, 512)
    g_pb = _grad_w("grad_w_proj_b", ob16, dyb, False, 512, 512, 512, 512)
    red_up, token = exchange_end(2, ex_up, (g_out,))
    ex_mid, token_mid = exchange_begin(1, [g_pa, g_pb, g_out])
    cc = _comb_bwd(doa, os, ws, deps=(token, token_mid))
    bwd_a = [_attn_a_bwd(qkn, proj, doa, ls[g], ws[g], cc, g) for g in range(3)]
    red_mid, token = exchange_end(1, ex_mid, (bwd_a[2][1],))
    dqk_b, dv_b, drpb_t = _attn_b_bwd(qkn, proj, dob, ob, lse_b, bias, deps=(token,))
    dproj, dn = _qk_bwd(proj, nw, cos, sin, [b[0] for b in bwd_a], dqk_b, [b[1] for b in bwd_a] + [dv_b], dga, dgb)
    g_in_theirs = _grad_w_in_half("grad_w_in_for_sibling", xn, dproj, place, True)
    send, recv, g_in_theirs, lands, token = _exchange_start("exchange_start_0", [g_in_theirs], sliced=False)
    g_in_mine = _grad_w_in_half("grad_w_in_own", xn, dproj, place, False, deps=(token,))
    _, theirs = _exchange_wait("exchange_wait_0", send, recv, g_in_theirs, lands, (g_in_mine,), sliced=False)
    red_in, token = reduce_start(0, [g_in_mine], theirs)
    grad_x, d_norm_mix = _in_proj_bwd(dproj, win_f, x2, dh1, norm_mix, deps=(token,))

    sh_down, token = share_begin(3, red_down, (grad_x,))
    sh_up, token = share_begin(2, red_up, (token,))
    done = share_end(3, sh_down, (token,))
    sh_mid, token = share_begin(1, red_mid, (done,))
    done = share_end(2, sh_up, (token,))
    sh_in, token = share_begin(0, red_in, (done,))
    done = share_end(1, sh_mid, (token,))
    done = share_end(0, sh_in, (done,))

    d_rpb = drpb_t[:, :15, GRID_W - WIN_C:GRID_W + WIN_C - 1]
    small_g = [d_norm_mix, jnp.concatenate([dba, dbb], axis=1), dn[0, 0], dn[1, 0], dn[0, 1], dn[1, 1], d_rpb, d_norm_ffn]
    gathered_small = _allgather_small(_pack_small(small_g), done)
    small_w = (norm_mix, b_gate, q_norm_a, k_norm_a, q_norm_b, k_norm_b, rpb_b, norm_ffn)
    small_m = (m_norm_mix, m_b_gate, m_q_norm_a, m_k_norm_a, m_q_norm_b, m_k_norm_b, m_rpb_b, m_norm_ffn)
    small_v = (v_norm_mix, v_b_gate, v_q_norm_a, v_k_norm_a, v_q_norm_b, v_k_norm_b, v_rpb_b, v_norm_ffn)
    small_out = [_unpack_small(p) for p in
                 _adam_small(gathered_small, _pack_small(small_w), _pack_small(small_m), _pack_small(small_v))]

    order = ("norm_mix", "w_in", "b_gate", "q_norm_a", "k_norm_a", "q_norm_b", "k_norm_b", "rpb_b",
             "w_proj_a", "w_proj_b", "w_out", "norm_ffn", "w_up", "w_down")
    small_idx = {name: i for i, (name, _) in enumerate(SMALL)}
    outs = []
    for kind in range(4):
        for name in order:
            if name in small_idx:
                outs.append(small_out[kind][small_idx[name]])
            else:
                outs.append(big_out[big_names.index(name)][kind][None])
    return (loss, grad_x[None], *outs)
```

```python
import functools

import numpy as np
import jax
import jax.numpy as jnp
from jax import lax
from jax.experimental import pallas as pl
from jax.experimental.pallas import tpu as pltpu

F32, BF16 = jnp.float32, jnp.bfloat16
SDS = jax.ShapeDtypeStruct
MESH = pl.DeviceIdType.MESH

T = 2048
D = 2048
HD = 128
NH, NHA = 16, 12
DIN = 10240
DFF = 8192
NSH = 4
DILS = (1, 4, 16)
EPS = 1e-6
NEG = -1e30
SCALE = HD ** -0.5
GRID_W, WIN_R, WIN_C = 64, 8, 16
VMEM_LIMIT = 56 * 1024 * 1024
B1, B2, LR, AEPS, WD, STEP = 0.9, 0.999, 0.001, 1e-08, 0.01, 10
SMALL_ROWS = 88


def _dot(a, b):
    return jnp.dot(a, b, preferred_element_type=F32)


def _dot_nt(a, b):
    return lax.dot_general(a, b, (((1,), (1,)), ((), ())), preferred_element_type=F32)


def _dot_tn(a, b):
    return lax.dot_general(a, b, (((0,), (0,)), ((), ())), preferred_element_type=F32)


def _params(n):
    return pltpu.CompilerParams(dimension_semantics=("arbitrary",) * n, vmem_limit_bytes=VMEM_LIMIT)


def _resident(shape, index_map):
    return pl.BlockSpec(shape, index_map, pipeline_mode=pl.Buffered(1))


def _sigmoid(z):
    return 1.0 / (1.0 + jnp.exp(-z))


def _wide(v, n):
    return jnp.concatenate([v] * n, axis=1)


def _row_tile(rows, cols, elems):
    tr = 16
    while tr * 2 <= rows and tr * 2 * cols <= elems:
        tr *= 2
    return tr


def _place():
    x, y, c = lax.axis_index("x"), lax.axis_index("y"), lax.axis_index("c")
    peers = [(1 - x, y), (x, 1 - y), (1 - x, 1 - y)]
    return x, y, c, peers


def _cast_into_place(w, name, place, deps=()):
    rows, cols = w.shape
    hr = rows // 2
    tr = min(hr, 256)
    per = hr // tr

    def body(*refs):
        w_ref, o_ref = refs[-2:]
        o_ref[...] = w_ref[...].astype(BF16)

    return pl.pallas_call(
        body, name=name, out_shape=SDS((NSH, 2, hr, cols), BF16),
        grid_spec=pltpu.PrefetchScalarGridSpec(
            num_scalar_prefetch=1, grid=(2, per),
            in_specs=[DEP_SPEC] * len(deps) + [pl.BlockSpec((tr, cols), lambda h, i, p: (h * per + i, 0))],
            out_specs=pl.BlockSpec((None, None, tr, cols), lambda h, i, p: (p[0], h, i, 0))),
        compiler_params=_params(2))(place, *deps, w)


ANY_SPEC = pl.BlockSpec(memory_space=pl.ANY)
HBM_SPEC = pl.BlockSpec(memory_space=pltpu.HBM)
SEM_SPEC = pl.BlockSpec(memory_space=pltpu.SEMAPHORE)
DEP_SPEC = pl.BlockSpec((8, 128), lambda *_: (0, 0))
EFFECT = pltpu.SideEffectType.DATAFLOW_SIDE_EFFECTING


def _after(body, deps):
    n = len(deps)
    return (lambda *refs: body(*refs[n:])) if n else body


SIBLING_BARRIER = 1


def _split_start(name, srcs, lands, n_copies, issue, sibling_only=False, after=()):
    n, m, d = len(srcs), len(lands), len(after)

    def body(*refs):
        if sibling_only:
            x, y, c, _ = _place()
            barrier = pltpu.get_barrier_semaphore()
            pl.semaphore_signal(barrier, inc=1, device_id=(x, y, 1 - c), device_id_type=MESH)
            pl.semaphore_wait(barrier, 1)
        issue(refs[:n], refs[n:n + m], refs[n + m + d], refs[n + m + d + 1])
        refs[-1][...] = jnp.zeros((8, 128), F32)

    arrays = list(srcs) + list(lands)
    outs = pl.pallas_call(
        body, name=name,
        out_shape=(pltpu.SemaphoreType.DMA((n_copies,)), pltpu.SemaphoreType.DMA((n_copies,)),
                   *[pltpu.HBM(a.shape, a.dtype) for a in arrays], SDS((8, 128), F32)),
        in_specs=[HBM_SPEC] * (n + m) + [ANY_SPEC] * d,
        out_specs=(SEM_SPEC, SEM_SPEC, *[HBM_SPEC] * (n + m), pl.BlockSpec(memory_space=pltpu.VMEM)),
        input_output_aliases={i: 2 + i for i in range(n + m)},
        compiler_params=pltpu.CompilerParams(has_side_effects=EFFECT,
                                             collective_id=SIBLING_BARRIER if sibling_only else None),
    )(*[pltpu.with_memory_space_constraint(a, pltpu.HBM) for a in arrays], *after)
    return outs[0], outs[1], list(outs[2:2 + n]), list(outs[2 + n:2 + n + m]), outs[-1]


def _split_wait(name, send_sems, recv_sems, srcs, lands, after, wait):
    n, m = len(srcs), len(lands)

    def body(*refs):
        wait(refs[:n], refs[n:n + m], refs[n + m], refs[n + m + 1])

    arrays = list(srcs) + list(lands)
    outs = pl.pallas_call(
        body, name=name, out_shape=[pltpu.HBM(a.shape, a.dtype) for a in arrays],
        in_specs=[HBM_SPEC] * (n + m) + [SEM_SPEC, SEM_SPEC] + [ANY_SPEC] * len(after),
        out_specs=[HBM_SPEC] * (n + m), input_output_aliases={i: i for i in range(n + m)},
        compiler_params=pltpu.CompilerParams(has_side_effects=EFFECT),
    )(*arrays, send_sems, recv_sems, *after)
    return list(outs[:n]), list(outs[n:])


def _gather_start(name, fulls, ks=(0, 1, 2), after=()):
    def issue(srcs, dsts, send_sems, recv_sems):
        x, y, c, peers = _place()
        for i in range(len(fulls)):
            mine = dsts[i].at[2 * x + y, c]
            for k in ks:
                px, py = peers[k]
                pltpu.make_async_remote_copy(
                    src_ref=mine, dst_ref=mine, send_sem=send_sems.at[3 * i + k],
                    recv_sem=recv_sems.at[3 * i + k], device_id=(px, py, c), device_id_type=MESH).start()

    return _split_start(name, [], fulls, 3 * len(fulls), issue, after=after)


def _gather_wait(name, send_sems, recv_sems, fulls, after, ks=(0, 1, 2)):
    def wait(srcs, dsts, send_sems, recv_sems):
        x, y, c, peers = _place()
        for i in range(len(fulls)):
            for k in ks:
                px, py = peers[k]
                cp = pltpu.make_async_remote_copy(
                    src_ref=dsts[i].at[2 * x + y, c], dst_ref=dsts[i].at[2 * px + py, c],
                    send_sem=send_sems.at[3 * i + k], recv_sem=recv_sems.at[3 * i + k],
                    device_id=(px, py, c), device_id_type=MESH)
                cp.wait_send()
                cp.wait_recv()

    return _split_wait(name, send_sems, recv_sems, [], fulls, after, wait)[1]


def _gather_finish(name, fulls, ks=(0, 1, 2)):
    n = len(fulls)

    def body(*refs):
        fin, fout = refs[:n], refs[n:2 * n]
        send_sems, recv_sems = refs[2 * n:]
        x, y, c, peers = _place()

        def copy(i, k, half):
            px, py = peers[k]
            return pltpu.make_async_remote_copy(
                src_ref=fin[i].at[2 * px + py, half], dst_ref=fout[i].at[2 * px + py, half],
                send_sem=send_sems.at[3 * i + k], recv_sem=recv_sems.at[3 * i + k],
                device_id=(x, y, 1 - c), device_id_type=MESH)

        sends = [copy(i, k, c) for i in range(n) for k in ks]
        for cp in sends:
            cp.start()
        for i in range(n):
            for k in ks:
                copy(i, k, 1 - c).wait_recv()
        for cp in sends:
            cp.wait_send()

    return pl.pallas_call(
        body, name=name, out_shape=[SDS(f.shape, f.dtype) for f in fulls],
        in_specs=[ANY_SPEC] * n, out_specs=[ANY_SPEC] * n, input_output_aliases={i: i for i in range(n)},
        scratch_shapes=[pltpu.SemaphoreType.DMA((3 * n,)), pltpu.SemaphoreType.DMA((3 * n,))])(*fulls)


def _reduce_start(name, parts):
    lands = [lax.empty((3,) + p.shape[1:], p.dtype) for p in parts]

    def issue(srcs, dsts, send_sems, recv_sems):
        x, y, c, peers = _place()
        for i in range(len(parts)):
            for k, (px, py) in enumerate(peers):
                pltpu.make_async_remote_copy(
                    src_ref=srcs[i].at[2 * px + py], dst_ref=dsts[i].at[k], send_sem=send_sems.at[3 * i + k],
                    recv_sem=recv_sems.at[3 * i + k], device_id=(px, py, c), device_id_type=MESH).start()

    return _split_start(name, parts, lands, 3 * len(parts), issue)


def _reduce_wait(name, send_sems, recv_sems, parts, lands, after):
    def wait(srcs, dsts, send_sems, recv_sems):
        x, y, c, peers = _place()
        for i in range(len(parts)):
            for k, (px, py) in enumerate(peers):
                cp = pltpu.make_async_remote_copy(
                    src_ref=srcs[i].at[2 * px + py], dst_ref=dsts[i].at[k], send_sem=send_sems.at[3 * i + k],
                    recv_sem=recv_sems.at[3 * i + k], device_id=(px, py, c), device_id_type=MESH)
                cp.wait_send()
                cp.wait_recv()

    return _split_wait(name, send_sems, recv_sems, parts, lands, after, wait)


def _sibling_copy(src, dst, send_sems, recv_sems, k):
    x, y, c, _ = _place()
    return pltpu.make_async_remote_copy(src_ref=src, dst_ref=dst, send_sem=send_sems.at[k], recv_sem=recv_sems.at[k],
                                        device_id=(x, y, 1 - c), device_id_type=MESH)


def _forward_start(name, fulls):
    def issue(srcs, dsts, send_sems, recv_sems):
        x, y, c, peers = _place()
        for i in range(len(fulls)):
            for k, (px, py) in enumerate(peers):
                part = dsts[i].at[2 * px + py, c]
                _sibling_copy(part, part, send_sems, recv_sems, 3 * i + k).start()

    return _split_start(name, [], fulls, 3 * len(fulls), issue, sibling_only=True)


def _forward_wait(name, send_sems, recv_sems, fulls, after):
    def wait(srcs, dsts, send_sems, recv_sems):
        x, y, c, peers = _place()
        for i in range(len(fulls)):
            for k, (px, py) in enumerate(peers):
                cp = _sibling_copy(dsts[i].at[2 * px + py, c], dsts[i].at[2 * px + py, 1 - c], send_sems, recv_sems, 3 * i + k)
                cp.wait_send()
                cp.wait_recv()

    return _split_wait(name, send_sems, recv_sems, [], fulls, after, wait)[1]


def _exchange_start(name, grads, sliced=True):
    lands = [lax.empty((NSH,) + g.shape[-2:], g.dtype) for g in grads]

    def issue(srcs, dsts, send_sems, recv_sems):
        c = lax.axis_index("c")
        for i in range(len(grads)):
            src = srcs[i].at[:, 1 - c] if sliced else srcs[i]
            _sibling_copy(src, dsts[i], send_sems, recv_sems, i).start()

    return _split_start(name, grads, lands, len(grads), issue, sibling_only=True)


def _exchange_wait(name, send_sems, recv_sems, grads, lands, after, sliced=True):
    def wait(srcs, dsts, send_sems, recv_sems):
        c = lax.axis_index("c")
        for i in range(len(grads)):
            cp = _sibling_copy(srcs[i].at[:, 1 - c] if sliced else srcs[i], dsts[i], send_sems, recv_sems, i)
            cp.wait_send()
            cp.wait_recv()

    return _split_wait(name, send_sems, recv_sems, grads, lands, after, wait)


def _share_start(name, sums):
    def issue(srcs, dsts, send_sems, recv_sems):
        c = lax.axis_index("c")
        for i in range(len(sums)):
            _sibling_copy(dsts[i].at[c], dsts[i].at[c], send_sems, recv_sems, i).start()

    return _split_start(name, [], sums, len(sums), issue, sibling_only=True)


def _share_wait(name, send_sems, recv_sems, sums, after):
    def wait(srcs, dsts, send_sems, recv_sems):
        c = lax.axis_index("c")
        for i in range(len(sums)):
            cp = _sibling_copy(dsts[i].at[c], dsts[i].at[1 - c], send_sems, recv_sems, i)
            cp.wait_send()
            cp.wait_recv()

    return _split_wait(name, send_sems, recv_sems, [], sums, after, wait)[1]


def _allgather_small(v, after):
    m_per, n = v.shape

    def body(x_ref, after_ref, out_ref, send_sems, recv_sems, local_sem):
        x, y, c = lax.axis_index("x"), lax.axis_index("y"), lax.axis_index("c")
        me, sibling = (x, y, c), (x, y, 1 - c)
        chips = [(1 - x, y), (x, 1 - y), (1 - x, 1 - y)]

        def rows(px, py, pc):
            return out_ref.at[pl.ds((4 * px + 2 * py + pc) * m_per, m_per), :]

        def copy(k, block, to, src=None):
            return pltpu.make_async_remote_copy(
                src_ref=rows(*block) if src is None else src, dst_ref=rows(*block),
                send_sem=send_sems.at[k], recv_sem=recv_sems.at[k], device_id=to, device_id_type=MESH)

        mine = pltpu.make_async_copy(x_ref, rows(*me), local_sem)
        mine.start()
        first = [copy(0, me, sibling, src=x_ref)]
        first += [copy(1 + j, me, (*chip, c), src=x_ref) for j, chip in enumerate(chips)]
        for cp in first:
            cp.start()
        passed = [copy(4 + j, (*chip, c), sibling) for j, chip in enumerate(chips)]
        for j, chip in enumerate(chips):
            copy(1 + j, (*chip, c), me).wait_recv()
            passed[j].start()
        copy(0, sibling, me).wait_recv()
        for j, chip in enumerate(chips):
            copy(4 + j, (*chip, 1 - c), me).wait_recv()
        for cp in first + passed:
            cp.wait_send()
        mine.wait()

    return pl.pallas_call(
        body, name="allgather_small", out_shape=SDS((8 * m_per, n), v.dtype),
        in_specs=[pl.BlockSpec(memory_space=pltpu.VMEM), ANY_SPEC], out_specs=pl.BlockSpec(memory_space=pltpu.VMEM),
        scratch_shapes=[pltpu.SemaphoreType.DMA((7,)), pltpu.SemaphoreType.DMA((7,)), pltpu.SemaphoreType.DMA])(v, after)


def _norm_in_proj_own(x, g, w_full, place):
    tn, chunk = 512, 256
    per = (DIN // NSH) // tn

    def body(place_ref, x_ref, g_ref, w_ref, proj_ref, xn_ref):
        @pl.when(pl.program_id(0) == 0)
        def _():
            def norm(r, carry):
                rows = pl.ds(pl.multiple_of(r * chunk, chunk), chunk)
                xv = x_ref[rows, :]
                rs = lax.rsqrt(jnp.mean(xv * xv, axis=-1, keepdims=True) + EPS)
                xn_ref[rows, :] = (xv * rs * g_ref[...]).astype(BF16)
                return carry

            lax.fori_loop(0, T // chunk, norm, 0)

        proj_ref[...] = _dot(xn_ref[...], w_ref[...])

    return pl.pallas_call(
        body, name="norm_in_proj_own", out_shape=[SDS((T, DIN), F32), SDS((T, D), BF16)],
        grid_spec=pltpu.PrefetchScalarGridSpec(
            num_scalar_prefetch=1, grid=(per,),
            in_specs=[_resident((T, D), lambda j, p: (0, 0)),
                      pl.BlockSpec((1, D), lambda j, p: (0, 0)),
                      pl.BlockSpec((None, D, tn), lambda j, p: (p[0], 0, j))],
            out_specs=[pl.BlockSpec((T, tn), lambda j, p: (0, p[0] * per + j)),
                       pl.BlockSpec((T, D), lambda j, p: (0, 0))]),
        compiler_params=_params(1))(place, x, g, w_full)


def _in_proj_rest(name, xn, w_full, proj, place, flips):
    tn = 512
    per = (DIN // NSH) // tn

    def body(place_ref, xn_ref, w_ref, proj_in, proj_ref):
        proj_ref[...] = _dot(xn_ref[...], w_ref[...])

    def shard(j, p):
        flip = flips[0]
        for n, f in enumerate(flips[1:]):
            flip = jnp.where(j // per == n + 1, f, flip)
        return p[0] ^ flip

    return pl.pallas_call(
        body, name=name, out_shape=SDS((T, DIN), F32),
        grid_spec=pltpu.PrefetchScalarGridSpec(
            num_scalar_prefetch=1, grid=(len(flips) * per,),
            in_specs=[_resident((T, D), lambda j, p: (0, 0)),
                      pl.BlockSpec((None, D, tn), lambda j, p: (shard(j, p), 0, j % per)), ANY_SPEC],
            out_specs=pl.BlockSpec((T, tn), lambda j, p: (0, shard(j, p) * per + j % per))),
        input_output_aliases={3: 0}, compiler_params=_params(1))(place, xn, w_full, proj)


def _rope_tables():
    pos = np.arange(T, dtype=np.float32)
    inv = (10000.0 ** (-np.arange(0, HD, 2, dtype=np.float32) / HD)).astype(np.float32)
    ang = (pos[:, None] * inv[None, :]).astype(np.float32)
    cos, sin = np.cos(ang).astype(np.float32), np.sin(ang).astype(np.float32)
    return (jnp.asarray(np.concatenate([cos, cos], axis=1)), jnp.asarray(np.concatenate([-sin, sin], axis=1)))


def _qk_prep(proj, nw, cos, sin):
    tm = 256

    def body(p_ref, w_ref, cos_ref, sin_ref, o_ref):
        cv, sv = cos_ref[...], sin_ref[...]
        for h in range(NH):
            sl = slice(h * HD, (h + 1) * HD)
            xv = p_ref[:, sl]
            r = lax.rsqrt(jnp.mean(xv * xv, axis=-1, keepdims=True) + EPS)
            z = xv * r * w_ref[:, sl]
            if h < NHA:
                z = z * cv + pltpu.roll(z, 64, 1) * sv
            o_ref[:, sl] = z.astype(BF16)

    return pl.pallas_call(
        body, name="qk_prep", out_shape=SDS((T, 2 * D), BF16), grid=(T // tm, 2),
        in_specs=[pl.BlockSpec((tm, D), lambda i, j: (i, j)),
                  pl.BlockSpec((None, 1, D), lambda i, j: (j, 0, 0)),
                  pl.BlockSpec((tm, HD), lambda i, j: (i, 0)),
                  pl.BlockSpec((tm, HD), lambda i, j: (i, 0))],
        out_specs=pl.BlockSpec((tm, D), lambda i, j: (i, j)),
        compiler_params=_params(2))(proj, nw, cos, sin)


def _band_mask(q0, m):
    ii = lax.broadcasted_iota(jnp.int32, (128, 256), 0)
    jj = lax.broadcasted_iota(jnp.int32, (128, 256), 1)
    rel = jj - ii
    kpos = jj + (q0 - 64)
    return (rel >= 0) & (rel <= 128) & (kpos >= 0) & (kpos < m)


def _fill_padded(dst, src, m):
    zeros = jnp.zeros((64, HD), dst.dtype)
    dst[0:64, :] = zeros
    dst[64 + m:128 + m, :] = zeros
    dst[64:64 + m, :] = src.astype(dst.dtype)


def _residue_rows(r, m, dil):
    return pl.ds(r, m, stride=dil) if dil > 1 else slice(None)


def _head_blocks(g):
    col = lambda base: pl.BlockSpec((T, HD), lambda h: (0, base + g * 4 + h))
    return col(0), col(NH), col(2 * NH), pl.BlockSpec((T, HD), lambda h: (0, h))


def _attn_a_fwd(qkn, proj, g):
    dil = DILS[g]
    m = T // dil
    nb = m // 128

    def body(q_ref, k_ref, v_ref, o_ref, l_ref, qf, kf, qp, kp, vp, ob, lb):
        qf[...] = q_ref[...].astype(F32)
        kf[...] = k_ref[...].astype(F32)
        for r in range(dil):
            rows = _residue_rows(r, m, dil)
            qp[...] = qf[rows, :].astype(BF16)
            _fill_padded(kp, kf[rows, :], m)
            _fill_padded(vp, v_ref[rows, :], m)

            def block(b, carry):
                q0 = pl.multiple_of(b * 128, 128)
                kw, vw = kp[pl.ds(q0, 256), :], vp[pl.ds(q0, 256), :]
                s = _dot_nt(qp[pl.ds(q0, 128), :], kw) * SCALE
                s = jnp.where(_band_mask(q0, m), s, NEG)
                mx = jnp.max(s, axis=-1, keepdims=True)
                e = jnp.exp(s - mx)
                den = jnp.sum(e, axis=-1, keepdims=True)
                ob[pl.ds(q0, 128), :] = _dot((e / den).astype(BF16), vw)
                lb[pl.ds(q0, 128), :] = jnp.broadcast_to(mx + jnp.log(den), (128, HD))
                return carry

            lax.fori_loop(0, nb, block, 0, unroll=min(nb, 8))
            o_ref[rows, :] = ob[...]
            l_ref[rows, :] = lb[...]

    q_blk, k_blk, v_blk, out_blk = _head_blocks(g)
    return pl.pallas_call(
        body, name=f"attn_a_fwd_{g}", out_shape=[SDS((T, 512), F32)] * 2, grid=(4,),
        in_specs=[q_blk, k_blk, v_blk], out_specs=[out_blk] * 2,
        scratch_shapes=[pltpu.VMEM((T, HD), F32), pltpu.VMEM((T, HD), F32), pltpu.VMEM((m, HD), BF16),
                        pltpu.VMEM((m + 128, HD), BF16), pltpu.VMEM((m + 128, HD), BF16),
                        pltpu.VMEM((m, HD), F32), pltpu.VMEM((m, HD), F32)],
        compiler_params=_params(1))(qkn, qkn, proj)


def _nbr_window(r):
    start = jnp.clip(r - WIN_R // 2, 0, T // GRID_W - WIN_R)
    return start, start - r + (WIN_R - 1)


def _rpb_rows(rpb):
    zeros = jnp.zeros((4, 14, 33), F32)
    a, b = rpb[:, :14], rpb[:, 1:15]
    rows = jnp.concatenate([a[:, :, 15:31], zeros, b, zeros, a[:, :, 0:15]], axis=2)
    return jnp.pad(rows, ((0, 0), (0, 2), (0, 0)))


def _attn_b_fwd(qkn, proj, rpb_rows):
    def body(r_ref, q_ref, k_ref, v_ref, o_ref, l_ref, bias_ref, vb, pair):
        qc = lax.broadcasted_iota(jnp.int32, (GRID_W, 512), 0)
        kc = lax.broadcasted_iota(jnp.int32, (GRID_W, 512), 1) & (GRID_W - 1)
        cs = jnp.clip(qc - WIN_C // 2, 0, GRID_W - WIN_C)
        colmask = (kc >= cs) & (kc < cs + WIN_C)
        for d in range(14):
            pair[d] = pltpu.roll(jnp.broadcast_to(r_ref[d:d + 1, :], (GRID_W, HD)), 0, 1, stride=1, stride_axis=0)
        for off in range(8):
            rows = jnp.concatenate([pair[off + 2 * jj] for jj in range(4)], axis=1)
            bias_ref[off] = jnp.where(colmask, rows, NEG)
        vb[...] = v_ref[...].astype(BF16)

        def row(r, carry):
            start, off = _nbr_window(r)
            q0 = pl.multiple_of(r * GRID_W, GRID_W)
            k0 = pl.multiple_of(start * GRID_W, GRID_W)
            s = _dot_nt(q_ref[pl.ds(q0, GRID_W), :], k_ref[pl.ds(k0, 512), :]) * SCALE + bias_ref[off]
            mx = jnp.max(s, axis=-1, keepdims=True)
            e = jnp.exp(s - mx)
            den = jnp.sum(e, axis=-1, keepdims=True)
            o_ref[pl.ds(q0, GRID_W), :] = _dot((e / den).astype(BF16), vb[pl.ds(k0, 512), :])
            l_ref[pl.ds(q0, GRID_W), :] = jnp.broadcast_to(mx + jnp.log(den), (GRID_W, HD))
            return carry

        lax.fori_loop(0, T // GRID_W, row, 0, unroll=8)

    return pl.pallas_call(
        body, name="attn_b_fwd",
        out_shape=[SDS((T, 512), F32), SDS((T, 512), F32), SDS((4, 8, GRID_W, 512), F32)], grid=(4,),
        in_specs=[pl.BlockSpec((None, 16, HD), lambda h: (h, 0, 0)),
                  pl.BlockSpec((T, HD), lambda h: (0, NHA + h)),
                  pl.BlockSpec((T, HD), lambda h: (0, NH + NHA + h)),
                  pl.BlockSpec((T, HD), lambda h: (0, 2 * NH + NHA + h))],
        out_specs=[pl.BlockSpec((T, HD), lambda h: (0, h)), pl.BlockSpec((T, HD), lambda h: (0, h)),
                   pl.BlockSpec((None, 8, GRID_W, 512), lambda h: (h, 0, 0, 0))],
        scratch_shapes=[pltpu.VMEM((T, HD), BF16), pltpu.VMEM((14, GRID_W, HD), F32)],
        compiler_params=_params(1))(rpb_rows, qkn, qkn, proj)


def _comb_fwd(os, ls):
    tm = 512

    def body(o0, o1, o2, l0, l1, l2, oa_ref, w0, w1, w2):
        lv = [l0[...], l1[...], l2[...]]
        mx = jnp.maximum(jnp.maximum(lv[0], lv[1]), lv[2])
        ev = [jnp.exp(l - mx) for l in lv]
        den = ev[0] + ev[1] + ev[2]
        wv = [e / den for e in ev]
        oa_ref[...] = (wv[0] * o0[...] + wv[1] * o1[...] + wv[2] * o2[...]).astype(BF16)
        w0[...], w1[...], w2[...] = wv

    spec = pl.BlockSpec((tm, 512), lambda i: (i, 0))
    return pl.pallas_call(
        body, name="comb_fwd", out_shape=[SDS((T, 512), BF16)] + [SDS((T, 512), F32)] * 3, grid=(T // tm,),
        in_specs=[spec] * 6, out_specs=[spec] * 4, compiler_params=_params(1))(*os, *ls)


def _mix_fwd(oa, ob, proj, b_gate, wpa, wpb):
    tm = 512

    def body(oa_ref, ob_ref, ga_ref, gb_ref, ba_ref, bb_ref, wpa_ref, wpb_ref, mixed_ref, ob16_ref):
        oav = oa_ref[...]
        obv = ob_ref[...].astype(BF16)
        ob16_ref[...] = obv
        for s in range(NSH):
            sl = slice(s * 512, (s + 1) * 512)
            ga = _sigmoid(ga_ref[:, sl] + ba_ref[:, sl])
            gb = _sigmoid(gb_ref[:, sl] + bb_ref[:, sl])
            mixed_ref[:, sl] = (ga * _dot(oav, wpa_ref[s]) + gb * _dot(obv, wpb_ref[s])).astype(BF16)

    row = lambda w: pl.BlockSpec((tm, w), lambda i: (i, 0))
    return pl.pallas_call(
        body, name="mix_fwd", out_shape=[SDS((T, D), BF16), SDS((T, 512), BF16)], grid=(T // tm,),
        in_specs=[row(512), row(512),
                  pl.BlockSpec((tm, D), lambda i: (i, 3)), pl.BlockSpec((tm, D), lambda i: (i, 4)),
                  pl.BlockSpec((1, D), lambda i: (0, 0)), pl.BlockSpec((1, D), lambda i: (0, 1)),
                  _resident((NSH, 512, 512), lambda i: (0, 0, 0)), _resident((NSH, 512, 512), lambda i: (0, 0, 0))],
        out_specs=[row(D), row(512)], compiler_params=_params(1))(oa, ob, proj, proj, b_gate, b_gate, wpa, wpb)


def _out_proj_fwd(mixed, w_out, x, g):
    tm = 512

    def body(m_ref, w_ref, x_ref, g_ref, h1_ref, hn_ref):
        h1 = x_ref[...] + _dot(m_ref[...], w_ref[...])
        h1_ref[...] = h1
        r = lax.rsqrt(jnp.mean(h1 * h1, axis=-1, keepdims=True) + EPS)
        hn_ref[...] = (h1 * r * g_ref[...]).astype(BF16)

    row = pl.BlockSpec((tm, D), lambda i: (i, 0))
    return pl.pallas_call(
        body, name="out_proj_fwd", out_shape=[SDS((T, D), F32), SDS((T, D), BF16)], grid=(T // tm,),
        in_specs=[row, _resident((D, D), lambda i: (0, 0)), row, pl.BlockSpec((1, D), lambda i: (0, 0))],
        out_specs=[row, row], compiler_params=_params(1))(mixed, w_out, x, g)


def _ffn_up(hn, w_up):
    tm, tn = T, 512
    per = (DFF // NSH) // tn

    def body(h_ref, w_ref, a_ref, u_ref):
        uv = jnp.maximum(_dot(h_ref[...], w_ref[...]), 0.0)
        a_ref[...] = (uv * uv).astype(BF16)
        u_ref[...] = uv.astype(BF16)

    out = pl.BlockSpec((tm, tn), lambda i, j: (i, j))
    return pl.pallas_call(
        body, name="ffn_up", out_shape=[SDS((T, DFF), BF16)] * 2, grid=(T // tm, DFF // tn),
        in_specs=[pl.BlockSpec((tm, D), lambda i, j: (i, 0)),
                  pl.BlockSpec((None, D, tn), lambda i, j: (j // per, 0, j % per))],
        out_specs=[out, out], compiler_params=_params(2))(hn, w_up)


def _ffn_down_own(u, w_down, place):
    tm, tk = 512, DFF // NSH

    def body(place_ref, u_ref, w_ref, o_ref):
        o_ref[...] = _dot(u_ref[...], w_ref[...])

    return pl.pallas_call(
        body, name="ffn_down_own", out_shape=SDS((T, D), F32),
        grid_spec=pltpu.PrefetchScalarGridSpec(
            num_scalar_prefetch=1, grid=(T // tm,),
            in_specs=[pl.BlockSpec((tm, tk), lambda i, p: (i, p[0])), pl.BlockSpec((tk, D), lambda i, p: (p[0], 0))],
            out_specs=pl.BlockSpec((tm, D), lambda i, p: (i, 0))),
        compiler_params=_params(1))(place, u, w_down)


def _ffn_down_loss(u, w_down, h1, target, own, place):
    tm, tk = 512, DFF // NSH
    nk = NSH - 1

    def body(place_ref, u_ref, w_ref, h1_ref, t_ref, own_ref, dy_ref, dy16_ref, loss_ref, acc):
        k = pl.program_id(1)

        @pl.when(k == 0)
        def _():
            acc[...] = own_ref[...]

        acc[...] += _dot(u_ref[...], w_ref[...])

        @pl.when(k == nk - 1)
        def _():
            def chunk(r, sq):
                rows = pl.ds(pl.multiple_of(r * 16, 16), 16)
                err = acc[rows, :] + h1_ref[rows, :] - t_ref[rows, :]
                dy = err * (1.0 / D)
                dy_ref[rows, :] = dy
                dy16_ref[rows, :] = dy.astype(BF16)
                return sq + err * err

            sq = lax.fori_loop(0, tm // 16, chunk, jnp.zeros((16, D), F32), unroll=2)
            part = 0.5 * jnp.sum(jnp.mean(sq, axis=-1, keepdims=True), axis=0, keepdims=True)
            loss_ref[...] = jnp.broadcast_to(part, (8, 128))

    row = pl.BlockSpec((tm, D), lambda i, k, p: (i, 0))
    once = _resident((tm, D), lambda i, k, p: (i, 0))
    shard = lambda k, p: p[0] ^ (k + 1)
    return pl.pallas_call(
        body, name="ffn_down_loss",
        out_shape=[SDS((T, D), F32), SDS((T, D), BF16), SDS((T // tm, 8, 128), F32)],
        grid_spec=pltpu.PrefetchScalarGridSpec(
            num_scalar_prefetch=1, grid=(T // tm, nk),
            in_specs=[pl.BlockSpec((tm, tk), lambda i, k, p: (i, shard(k, p))),
                      pl.BlockSpec((tk, D), lambda i, k, p: (shard(k, p), 0)), once, once, once],
            out_specs=[row, row, pl.BlockSpec((None, 8, 128), lambda i, k, p: (i, 0, 0))],
            scratch_shapes=[pltpu.VMEM((tm, D), F32)]),
        compiler_params=_params(2))(place, u, w_down, h1, target, own)


def _ffn_down_bwd(dy16, w_down, u, deps=()):
    tm, tn = T, 512

    def body(dy_ref, w_ref, u_ref, du_ref):
        uv = u_ref[...].astype(F32)
        du_ref[...] = jnp.where(uv > 0.0, 2.0 * uv * _dot_nt(dy_ref[...], w_ref[...]), 0.0).astype(BF16)

    return pl.pallas_call(
        _after(body, deps), name="ffn_down_bwd", out_shape=SDS((T, DFF), BF16), grid=(T // tm, DFF // tn),
        in_specs=[DEP_SPEC] * len(deps) + [
            pl.BlockSpec((tm, D), lambda i, j: (i, 0)), pl.BlockSpec((tn, D), lambda i, j: (j, 0)),
            pl.BlockSpec((tm, tn), lambda i, j: (i, j))],
        out_specs=pl.BlockSpec((tm, tn), lambda i, j: (i, j)), compiler_params=_params(2))(*deps, dy16, w_down, u)


def _norm_bwd(xv, dz_in, g):
    r = lax.rsqrt(jnp.mean(xv * xv, axis=-1, keepdims=True) + EPS)
    dg = jnp.sum(xv * r * dz_in, axis=0, keepdims=True)
    dz = dz_in * g
    dx = r * dz - xv * (r * r * r) * jnp.mean(xv * dz, axis=-1, keepdims=True)
    return dx, dg


def _ffn_up_bwd(du, w_up, h1, dy, g, deps=()):
    tm, tk = 512, 1024
    per = (DFF // NSH) // tk
    nk = DFF // tk

    def body(du_ref, w_ref, h1_ref, dy_ref, g_ref, dh1_ref, dh16_ref, dg_ref, acc):
        i, k = pl.program_id(0), pl.program_id(1)

        @pl.when(k == 0)
        def _():
            acc[...] = jnp.zeros_like(acc)

        @pl.when((k == 0) & (i == 0))
        def _():
            dg_ref[...] = jnp.zeros_like(dg_ref)

        acc[...] += _dot_nt(du_ref[...], w_ref[...])

        @pl.when(k == nk - 1)
        def _():
            dx, dg = _norm_bwd(h1_ref[...], acc[...], g_ref[...])
            dh1 = dy_ref[...] + dx
            dh1_ref[...] = dh1
            dh16_ref[...] = dh1.astype(BF16)
            dg_ref[...] += dg

    row = pl.BlockSpec((tm, D), lambda i, k: (i, 0))
    vec = pl.BlockSpec((1, D), lambda i, k: (0, 0))
    return pl.pallas_call(
        _after(body, deps), name="ffn_up_bwd", out_shape=[SDS((T, D), F32), SDS((T, D), BF16), SDS((1, D), F32)],
        grid=(T // tm, nk),
        in_specs=[DEP_SPEC] * len(deps) + [
            pl.BlockSpec((tm, tk), lambda i, k: (i, k)),
            pl.BlockSpec((None, D, tk), lambda i, k: (k // per, 0, k % per)), row, row, vec],
        out_specs=[row, row, vec], scratch_shapes=[pltpu.VMEM((tm, D), F32)],
        compiler_params=_params(2))(*deps, du, w_up, h1, dy, g)


def _mix_bwd(dh16, w_out, oa, ob16, proj, b_gate, wpa, wpb):
    tm = 256

    def body(dh_ref, wo_ref, oa_ref, ob_ref, ga_ref, gb_ref, ba_ref, bb_ref, wpa_ref, wpb_ref,
             dya_ref, dyb_ref, dga_ref, dgb_ref, doa_ref, dob_ref, dba_ref, dbb_ref):
        @pl.when(pl.program_id(0) == 0)
        def _():
            dba_ref[...] = jnp.zeros_like(dba_ref)
            dbb_ref[...] = jnp.zeros_like(dbb_ref)

        oav, obv = oa_ref[...], ob_ref[...]
        doa = jnp.zeros((tm, 512), F32)
        dob = jnp.zeros((tm, 512), F32)
        for s in range(NSH):
            sl = slice(s * 512, (s + 1) * 512)
            dm = _dot_nt(dh_ref[...], wo_ref[sl, :])
            ga = _sigmoid(ga_ref[:, sl] + ba_ref[:, sl])
            gb = _sigmoid(gb_ref[:, sl] + bb_ref[:, sl])
            dya = (dm * ga).astype(BF16)
            dyb = (dm * gb).astype(BF16)
            dza = dm * _dot(oav, wpa_ref[s]) * ga * (1.0 - ga)
            dzb = dm * _dot(obv, wpb_ref[s]) * gb * (1.0 - gb)
            dya_ref[:, sl], dyb_ref[:, sl] = dya, dyb
            dga_ref[:, sl], dgb_ref[:, sl] = dza.astype(BF16), dzb.astype(BF16)
            dba_ref[:, sl] += jnp.sum(dza, axis=0, keepdims=True)
            dbb_ref[:, sl] += jnp.sum(dzb, axis=0, keepdims=True)
            doa += _dot_nt(dya, wpa_ref[s])
            dob += _dot_nt(dyb, wpb_ref[s])
        doa_ref[...], dob_ref[...] = doa, dob

    row = lambda w: pl.BlockSpec((tm, w), lambda i: (i, 0))
    vec = pl.BlockSpec((1, D), lambda i: (0, 0))
    wp = _resident((NSH, 512, 512), lambda i: (0, 0, 0))
    return pl.pallas_call(
        body, name="mix_bwd",
        out_shape=[SDS((T, D), BF16)] * 4 + [SDS((T, 512), F32)] * 2 + [SDS((1, D), F32)] * 2, grid=(T // tm,),
        in_specs=[row(D), _resident((D, D), lambda i: (0, 0)), row(512), row(512),
                  pl.BlockSpec((tm, D), lambda i: (i, 3)), pl.BlockSpec((tm, D), lambda i: (i, 4)),
                  pl.BlockSpec((1, D), lambda i: (0, 0)), pl.BlockSpec((1, D), lambda i: (0, 1)), wp, wp],
        out_specs=[row(D)] * 4 + [row(512)] * 2 + [vec] * 2,
        compiler_params=_params(1))(dh16, w_out, oa, ob16, proj, proj, b_gate, b_gate, wpa, wpb)


def _comb_bwd(doa, os, ws, deps=()):
    tm = 512

    def body(d_ref, o0, o1, o2, w0, w1, w2, cc_ref):
        prod = d_ref[...] * (w0[...] * o0[...] + w1[...] * o1[...] + w2[...] * o2[...])
        for h in range(4):
            sl = slice(h * HD, (h + 1) * HD)
            cc_ref[:, sl] = jnp.broadcast_to(jnp.sum(prod[:, sl], axis=-1, keepdims=True), (tm, HD))

    spec = pl.BlockSpec((tm, 512), lambda i: (i, 0))
    return pl.pallas_call(
        _after(body, deps), name="comb_bwd", out_shape=SDS((T, 512), F32), grid=(T // tm,),
        in_specs=[DEP_SPEC] * len(deps) + [spec] * 7, out_specs=spec,
        compiler_params=_params(1))(*deps, doa, *os, *ws)


def _attn_a_bwd(qkn, proj, doa, lse, w, cc, g):
    dil = DILS[g]
    m = T // dil
    nb = m // 128

    def body(q_ref, k_ref, v_ref, d_ref, l_ref, w_ref, c_ref, dqk_ref, dv_ref,
             qf, kf, qp, kp, vp, dp, lp, wsub, cp, dqb, dkp, dvp):
        qf[...] = q_ref[...].astype(F32)
        kf[...] = k_ref[...].astype(F32)
        for r in range(dil):
            sub = _residue_rows(r, m, dil)
            qp[...] = qf[sub, :].astype(BF16)
            _fill_padded(kp, kf[sub, :], m)
            _fill_padded(vp, v_ref[sub, :], m)
            dp[...] = d_ref[sub, :].astype(BF16)
            lp[...], wsub[...], cp[...] = l_ref[sub, :], w_ref[sub, :], c_ref[sub, :]
            dkp[...] = jnp.zeros_like(dkp)
            dvp[...] = jnp.zeros_like(dvp)

            def block(b, carry):
                q0 = pl.multiple_of(b * 128, 128)
                rows = pl.ds(q0, 128)
                win = pl.ds(q0, 256)
                qb, kw, vw = qp[rows, :], kp[win, :], vp[win, :]
                s = _dot_nt(qb, kw) * SCALE
                s = jnp.where(_band_mask(q0, m), s, NEG)
                wp = _wide(wsub[rows, :], 2) * jnp.exp(s - _wide(lp[rows, :], 2))
                dob = dp[rows, :]
                ds = (wp * (_dot_nt(dob, vw) - _wide(cp[rows, :], 2))).astype(BF16)
                dqb[rows, :] = _dot(ds, kw) * SCALE
                dkp[win, :] += _dot_tn(ds, qb) * SCALE
                dvp[win, :] += _dot_tn(wp.astype(BF16), dob)
                return carry

            lax.fori_loop(0, nb, block, 0, unroll=min(nb, 8))
            dqk_ref.at[0][sub, :] = dqb[...]
            dqk_ref.at[1][sub, :] = dkp[64:64 + m, :]
            dv_ref[sub, :] = dvp[64:64 + m, :]

    q_blk, k_blk, v_blk, blk = _head_blocks(g)
    sub16 = pltpu.VMEM((m, HD), BF16)
    sub32 = pltpu.VMEM((m, HD), F32)
    return pl.pallas_call(
        body, name=f"attn_a_bwd_{g}", out_shape=[SDS((2, T, 512), F32), SDS((T, 512), F32)], grid=(4,),
        in_specs=[q_blk, k_blk, v_blk, blk, blk, blk, blk],
        out_specs=[pl.BlockSpec((2, T, HD), lambda h: (0, 0, h)), blk],
        scratch_shapes=[pltpu.VMEM((T, HD), F32), pltpu.VMEM((T, HD), F32), sub16,
                        pltpu.VMEM((m + 128, HD), BF16), pltpu.VMEM((m + 128, HD), BF16), sub16,
                        sub32, sub32, sub32, sub32,
                        pltpu.VMEM((m + 128, HD), F32), pltpu.VMEM((m + 128, HD), F32)],
        compiler_params=_params(1))(qkn, qkn, proj, doa, lse, w, cc)


def _attn_b_bwd(qkn, proj, dob, ob, lse, bias, deps=()):
    def body(q_ref, k_ref, v_ref, d_ref, o_ref, l_ref, bias_ref, dqk_ref, dv_ref, drpb_ref, vb, dk_acc, dv_acc, a_acc):
        vb[...] = v_ref[...].astype(BF16)
        dk_acc[...] = jnp.zeros_like(dk_acc)
        dv_acc[...] = jnp.zeros_like(dv_acc)
        a_acc[...] = jnp.zeros_like(a_acc)

        def row(r, carry):
            start, off = _nbr_window(r)
            rows = pl.ds(pl.multiple_of(r * GRID_W, GRID_W), GRID_W)
            win = pl.ds(pl.multiple_of(start * GRID_W, GRID_W), 512)
            qr, kw, vw = q_ref[rows, :], k_ref[win, :], vb[win, :]
            s = _dot_nt(qr, kw) * SCALE + bias_ref[off]
            p = jnp.exp(s - _wide(l_ref[rows, :], 4))
            dov = d_ref[rows, :]
            delta = jnp.sum(dov * o_ref[rows, :], axis=-1, keepdims=True)
            do16 = dov.astype(BF16)
            ds = p * (_dot_nt(do16, vw) - delta)
            a_acc[off] += ds
            ds16 = ds.astype(BF16)
            dqk_ref[0, rows, :] = _dot(ds16, kw) * SCALE
            dk_acc[win, :] += _dot_tn(ds16, qr) * SCALE
            dv_acc[win, :] += _dot_tn(p.astype(BF16), do16)
            return carry

        lax.fori_loop(0, T // GRID_W, row, 0, unroll=8)
        dqk_ref[1] = dk_acc[...]
        dv_ref[...] = dv_acc[...]

        lane = lax.broadcasted_iota(jnp.int32, (16, HD), 1)
        rowi = lax.broadcasted_iota(jnp.int32, (16, HD), 0)
        low = (lane >= GRID_W - WIN_C) & (lane < GRID_W + WIN_C - 1)
        high = (lane >= HD - WIN_C) | (lane < WIN_C - 1)
        flip = (lax.broadcasted_iota(jnp.int32, (GRID_W, GRID_W), 0)
                + lax.broadcasted_iota(jnp.int32, (GRID_W, GRID_W), 1) == GRID_W - 1).astype(BF16)
        out = jnp.zeros((16, HD), F32)
        for d in range(14):
            acc = None
            for off in range(8):
                if 0 <= d - off <= 6 and (d - off) % 2 == 0:
                    jj = (d - off) // 2
                    piece = a_acc[off, :, jj * HD:(jj + 1) * HD]
                    acc = piece if acc is None else acc + piece
            hi = acc.astype(BF16)
            lo = (acc - hi.astype(F32)).astype(BF16)
            rev = _dot(flip, hi) + _dot(flip, lo)
            v = jnp.sum(pltpu.roll(rev, 0, 1, stride=1, stride_axis=0), axis=0, keepdims=True)
            v = jnp.broadcast_to(v, (16, HD))
            out = out + jnp.where((rowi == d) & low, v, 0.0)
            out = out + jnp.where(rowi == d + 1, pltpu.roll(jnp.where(high, v, 0.0), GRID_W, 1), 0.0)
        drpb_ref[...] = out

    blk = pl.BlockSpec((T, HD), lambda h: (0, h))
    return pl.pallas_call(
        _after(body, deps), name="attn_b_bwd",
        out_shape=[SDS((2, T, 512), F32), SDS((T, 512), F32), SDS((4, 16, HD), F32)], grid=(4,),
        in_specs=[DEP_SPEC] * len(deps) + [
            pl.BlockSpec((T, HD), lambda h: (0, NHA + h)),
            pl.BlockSpec((T, HD), lambda h: (0, NH + NHA + h)),
            pl.BlockSpec((T, HD), lambda h: (0, 2 * NH + NHA + h)), blk, blk, blk,
            pl.BlockSpec((None, 8, GRID_W, 512), lambda h: (h, 0, 0, 0))],
        out_specs=[pl.BlockSpec((2, T, HD), lambda h: (0, 0, h)), blk,
                   pl.BlockSpec((None, 16, HD), lambda h: (h, 0, 0))],
        scratch_shapes=[pltpu.VMEM((T, HD), BF16), pltpu.VMEM((T, HD), F32), pltpu.VMEM((T, HD), F32),
                        pltpu.VMEM((8, GRID_W, 512), F32)],
        compiler_params=_params(1))(*deps, qkn, qkn, proj, dob, ob, lse, bias)


def _qk_bwd(proj, nw, cos, sin, dqk_groups, dqk_b, dvs, dga, dgb):
    tm = 512

    def body(p_ref, w_ref, cos_ref, sin_ref, d0, d1, d2, d3, v0, v1, v2, v3, ga_ref, gb_ref, o_ref, dn_ref):
        j, i = pl.program_id(0), pl.program_id(1)

        @pl.when((j < 2) & (i == 0))
        def _():
            dn_ref[...] = jnp.zeros_like(dn_ref)

        @pl.when(j < 2)
        def _():
            cv, sv = cos_ref[...], sin_ref[...]
            srcs = (d0, d1, d2, d3)
            dna = jnp.zeros((1, HD), F32)
            dnb = jnp.zeros((1, HD), F32)
            for h in range(NH):
                sl = slice(h * HD, (h + 1) * HD)
                dz = srcs[h // 4][:, (h % 4) * HD:(h % 4 + 1) * HD]
                if h < NHA:
                    dz = dz * cv + pltpu.roll(dz * sv, 64, 1)
                dx, dg = _norm_bwd(p_ref[:, sl], dz, w_ref[:, sl])
                o_ref[:, sl] = dx.astype(BF16)
                if h < NHA:
                    dna += dg
                else:
                    dnb += dg
            dn_ref[0:1, :] += dna
            dn_ref[1:2, :] += dnb

        @pl.when(j == 2)
        def _():
            for s, v_ref in enumerate((v0, v1, v2, v3)):
                o_ref[:, s * 512:(s + 1) * 512] = v_ref[...].astype(BF16)

        @pl.when(j == 3)
        def _():
            o_ref[...] = ga_ref[...]

        @pl.when(j == 4)
        def _():
            o_ref[...] = gb_ref[...]

    def rows(used):
        return lambda j, i: (jnp.where(used(j), i, 0), 0)

    qk = lambda j: j < 2
    dspec = pl.BlockSpec((None, tm, 512), lambda j, i: (jnp.minimum(j, 1), jnp.where(j < 2, i, 0), 0))
    vspec = pl.BlockSpec((tm, 512), rows(lambda j: j == 2))
    return pl.pallas_call(
        body, name="qk_bwd", out_shape=[SDS((T, DIN), BF16), SDS((2, 8, HD), F32)], grid=(5, T // tm),
        in_specs=[pl.BlockSpec((tm, D), lambda j, i: (jnp.where(j < 2, i, 0), jnp.minimum(j, 1))),
                  pl.BlockSpec((None, 1, D), lambda j, i: (jnp.minimum(j, 1), 0, 0)),
                  pl.BlockSpec((tm, HD), rows(qk)), pl.BlockSpec((tm, HD), rows(qk)),
                  dspec, dspec, dspec, dspec, vspec, vspec, vspec, vspec,
                  pl.BlockSpec((tm, D), rows(lambda j: j == 3)), pl.BlockSpec((tm, D), rows(lambda j: j == 4))],
        out_specs=[pl.BlockSpec((tm, D), lambda j, i: (i, j)),
                   pl.BlockSpec((None, 8, HD), lambda j, i: (jnp.minimum(j, 1), 0, 0))],
        compiler_params=_params(2))(proj, nw, cos, sin, *dqk_groups, dqk_b, *dvs, dga, dgb)


def _in_proj_bwd(dproj, w_in, x, dh1, g, deps=()):
    tm, tk = 512, 1280
    per = (DIN // NSH) // tk
    nk = DIN // tk

    def body(dp_ref, w_ref, x_ref, dh_ref, g_ref, dx_ref, dg_ref, acc):
        i, k = pl.program_id(0), pl.program_id(1)

        @pl.when(k == 0)
        def _():
            acc[...] = jnp.zeros_like(acc)

        @pl.when((k == 0) & (i == 0))
        def _():
            dg_ref[...] = jnp.zeros_like(dg_ref)

        acc[...] += _dot_nt(dp_ref[...], w_ref[...])

        @pl.when(k == nk - 1)
        def _():
            dx, dg = _norm_bwd(x_ref[...], acc[...], g_ref[...])
            dx_ref[...] = dh_ref[...] + dx
            dg_ref[...] += dg

    row = pl.BlockSpec((tm, D), lambda i, k: (i, 0))
    vec = pl.BlockSpec((1, D), lambda i, k: (0, 0))
    return pl.pallas_call(
        _after(body, deps), name="in_proj_bwd", out_shape=[SDS((T, D), F32), SDS((1, D), F32)], grid=(T // tm, nk),
        in_specs=[DEP_SPEC] * len(deps) + [
            pl.BlockSpec((tm, tk), lambda i, k: (i, k)),
            pl.BlockSpec((None, D, tk), lambda i, k: (k // per, 0, k % per)), row, row, vec],
        out_specs=[row, vec], scratch_shapes=[pltpu.VMEM((tm, D), F32)],
        compiler_params=_params(2))(*deps, dproj, w_in, x, dh1, g)


def _grad_w(name, a, g, shard_rows, rows, cols, tr, tc):
    ni, nj = rows // tr, cols // tc
    if shard_rows:
        a_map, g_map = (lambda s, i, j: (0, s * ni + i)), (lambda s, i, j: (0, j))
    else:
        a_map, g_map = (lambda s, i, j: (0, i)), (lambda s, i, j: (0, s * nj + j))

    def body(a_ref, g_ref, o_ref):
        o_ref[...] = _dot_tn(a_ref[...], g_ref[...]).astype(BF16)

    return pl.pallas_call(
        body, name=name, out_shape=SDS((NSH, rows, cols), BF16), grid=(NSH, ni, nj),
        in_specs=[pl.BlockSpec((T, tr), a_map), pl.BlockSpec((T, tc), g_map)],
        out_specs=pl.BlockSpec((None, tr, tc), lambda s, i, j: (s, i, j)), compiler_params=_params(3))(a, g)


def _grad_w_in_half(name, xn, dproj, place, for_sibling, deps=()):
    tr, tc = D // 2, 1280
    nj = (DIN // NSH) // tc

    def body(*refs):
        a_ref, g_ref, o_ref = refs[-3:]
        o_ref[...] = _dot_tn(a_ref[...], g_ref[...]).astype(BF16)

    half = (lambda p: 1 - p[1]) if for_sibling else (lambda p: p[1])
    return pl.pallas_call(
        body, name=name, out_shape=SDS((NSH, tr, DIN // NSH), BF16),
        grid_spec=pltpu.PrefetchScalarGridSpec(
            num_scalar_prefetch=1, grid=(NSH, nj),
            in_specs=[DEP_SPEC] * len(deps) + [pl.BlockSpec((T, tr), lambda s, j, p: (0, half(p))),
                                               pl.BlockSpec((T, tc), lambda s, j, p: (0, s * nj + j))],
            out_specs=pl.BlockSpec((None, tr, tc), lambda s, j, p: (s, 0, j))),
        compiler_params=_params(2))(place, *deps, xn, dproj)


def _adamw(w, g, m, v):
    m = B1 * m + (1.0 - B1) * g
    v = B2 * v + (1.0 - B2) * (g * g)
    m_hat = m / (1.0 - B1 ** STEP)
    v_hat = v / (1.0 - B2 ** STEP)
    delta = -LR * (m_hat / (jnp.sqrt(v_hat) + AEPS) + WD * w)
    return delta, m, v


def _sum_halves(name, place, grads, theirs):
    _, rows, cols = theirs.shape
    tr = _row_tile(rows, cols, 1 << 20)

    def body(place_ref, a_ref, b_ref, o_ref):
        o_ref[...] = (a_ref[...].astype(F32) + b_ref[...].astype(F32)).astype(BF16)

    spec = pl.BlockSpec((None, tr, cols), lambda s, i, p: (s, i, 0))
    mine = spec if grads.ndim == 3 else pl.BlockSpec((None, None, tr, cols), lambda s, i, p: (s, p[1], i, 0))
    return pl.pallas_call(
        body, name=name, out_shape=SDS(theirs.shape, BF16),
        grid_spec=pltpu.PrefetchScalarGridSpec(
            num_scalar_prefetch=1, grid=(NSH, rows // tr), in_specs=[mine, spec], out_specs=spec),
        compiler_params=_params(2))(place, grads, theirs)


def _sum_landed(name, place, part, landed):
    _, rows, cols = part.shape
    tr = _row_tile(rows, cols, 1 << 20)

    def body(place_ref, p_ref, l_ref, o_ref):
        o_ref[...] = ((p_ref[...].astype(F32) + l_ref[0].astype(F32)) + l_ref[1].astype(F32)) + l_ref[2].astype(F32)

    return pl.pallas_call(
        body, name=name, out_shape=SDS((2, rows, cols), F32),
        grid_spec=pltpu.PrefetchScalarGridSpec(
            num_scalar_prefetch=1, grid=(rows // tr,),
            in_specs=[pl.BlockSpec((None, tr, cols), lambda i, p: (p[0], i, 0)),
                      pl.BlockSpec((3, tr, cols), lambda i, p: (0, i, 0))],
            out_specs=pl.BlockSpec((None, tr, cols), lambda i, p: (p[1], i, 0))),
        compiler_params=_params(1))(place, part, landed)


def _adam_shard(name, g, w, m, v):
    rows, cols = w.shape
    tr = _row_tile(rows, cols, 1 << 19)

    def body(g_ref, w_ref, m_ref, v_ref, go_ref, d_ref, nm_ref, nv_ref):
        g = g_ref[...]
        go_ref[...] = g
        d_ref[...], nm_ref[...], nv_ref[...] = _adamw(w_ref[...], g, m_ref[...], v_ref[...])

    spec = pl.BlockSpec((tr, cols), lambda i: (i, 0))
    return pl.pallas_call(
        body, name=name, out_shape=[SDS((rows, cols), F32)] * 4, grid=(rows // tr,),
        in_specs=[spec] * 4, out_specs=[spec] * 4, compiler_params=_params(1))(g, w, m, v)


def _adam_small(gathered, w, m, v):
    def body(g_ref, w_ref, m_ref, v_ref, go_ref, d_ref, nm_ref, nv_ref):
        g = g_ref[0:SMALL_ROWS, :]
        for dev in range(1, 8):
            g = g + g_ref[dev * SMALL_ROWS:(dev + 1) * SMALL_ROWS, :]
        go_ref[...] = g
        d_ref[...], nm_ref[...], nv_ref[...] = _adamw(w_ref[...], g, m_ref[...], v_ref[...])

    return pl.pallas_call(body, name="adam_small", out_shape=[SDS((SMALL_ROWS, HD), F32)] * 4)(gathered, w, m, v)


SMALL = (("norm_mix", (1, D)), ("b_gate", (1, 2 * D)), ("q_norm_a", (1, HD)), ("k_norm_a", (1, HD)),
         ("q_norm_b", (1, HD)), ("k_norm_b", (1, HD)), ("rpb_b", (1, 4, 15, 31)), ("norm_ffn", (1, D)))


def _pack_small(vals):
    pieces = []
    for (name, shape), val in zip(SMALL, vals):
        flat = val.reshape(-1)
        pad = (-flat.shape[0]) % HD
        pieces.append(jnp.pad(flat, (0, pad)).reshape(-1, HD))
    packed = jnp.concatenate(pieces, axis=0)
    return jnp.pad(packed, ((0, SMALL_ROWS - packed.shape[0]), (0, 0)))


def _unpack_small(packed):
    out, row = [], 0
    for name, shape in SMALL:
        size = int(np.prod(shape))
        nrows = -(-size // HD)
        out.append(packed[row:row + nrows].reshape(-1)[:size].reshape(shape))
        row += nrows
    return out


def kernel(x, norm_mix, w_in, b_gate, q_norm_a, k_norm_a, q_norm_b, k_norm_b, rpb_b, w_proj_a, w_proj_b, w_out, norm_ffn, w_up, w_down, loss_target, m_norm_mix, m_w_in, m_b_gate, m_q_norm_a, m_k_norm_a, m_q_norm_b, m_k_norm_b, m_rpb_b, m_w_proj_a, m_w_proj_b, m_w_out, m_norm_ffn, m_w_up, m_w_down, v_norm_mix, v_w_in, v_b_gate, v_q_norm_a, v_k_norm_a, v_q_norm_b, v_k_norm_b, v_rpb_b, v_w_proj_a, v_w_proj_b, v_w_out, v_norm_ffn, v_w_up, v_w_down):
    big_names = ("w_in", "w_proj_a", "w_proj_b", "w_out", "w_up", "w_down")
    big_w = [a[0] for a in (w_in, w_proj_a, w_proj_b, w_out, w_up, w_down)]
    big_m = [a[0] for a in (m_w_in, m_w_proj_a, m_w_proj_b, m_w_out, m_w_up, m_w_down)]
    big_v = [a[0] for a in (v_w_in, v_w_proj_a, v_w_proj_b, v_w_out, v_w_up, v_w_down)]
    x2, target = x[0], loss_target[0]

    place = jnp.stack([2 * lax.axis_index("x") + lax.axis_index("y"), lax.axis_index("c")]).astype(jnp.int32)
    groups = ((0,), (1, 2, 3), (4,), (5,))
    started = []
    for j, grp in enumerate(groups):
        deps = (started[0][4],) if j else ()
        placed = [_cast_into_place(big_w[i], "cast_" + big_names[i], place, deps) for i in grp]
        started.append(_gather_start(f"gather_start_{j}", placed))

    def whole(fulls):
        return [f.reshape(NSH, 2 * f.shape[2], f.shape[3]) for f in fulls]

    def forward_begin(j, after):
        send, recv, _, fulls, _ = started[j]
        fulls = _gather_wait(f"gather_wait_{j}", send, recv, fulls, after)
        send, recv, _, fulls, token = _forward_start(f"forward_start_{j}", fulls)
        return (send, recv, fulls), token

    def forward_end(j, state, after):
        return whole(_forward_wait(f"forward_wait_{j}", *state, after))

    def as_halves(grads):
        return [g.reshape(NSH, 2, g.shape[1] // 2, g.shape[2]) for g in grads]

    def reduce_start(j, grads, theirs):
        parts = [_sum_halves(f"sum_halves_{j}_{i}", place, a, b) for i, (a, b) in enumerate(zip(grads, theirs))]
        send, recv, parts, lands, token = _reduce_start(f"reduce_start_{j}", parts)
        return (send, recv, parts, lands), token

    def exchange_begin(j, grads):
        send, recv, grads, lands, token = _exchange_start(f"exchange_start_{j}", as_halves(grads))
        return (send, recv, grads, lands), token

    def exchange_end(j, state, after):
        return reduce_start(j, *_exchange_wait(f"exchange_wait_{j}", *state, after))

    big_out = {}

    def share_begin(j, state, after):
        send, recv, parts, lands = state
        parts, lands = _reduce_wait(f"reduce_wait_{j}", send, recv, parts, lands, after)
        sums = [_sum_landed(f"sum_landed_{j}_{i}", place, p, l) for i, (p, l) in enumerate(zip(parts, lands))]
        send, recv, _, sums, token = _share_start(f"share_start_{j}", sums)
        return (send, recv, sums), token

    def share_end(j, state, after):
        for idx, g in zip(groups[j], _share_wait(f"share_wait_{j}", *state, after)):
            g = g.reshape(big_w[idx].shape)
            big_out[idx] = _adam_shard("adam_" + big_names[idx], g, big_w[idx], big_m[idx], big_v[idx])
        return big_out[groups[j][-1]][1]

    proj, xn = _norm_in_proj_own(x2, norm_mix, whole(started[0][3])[0], place)
    send, recv, _, win, _ = started[0]
    win = _gather_wait("gather_wait_0", send, recv, win, (proj, *[s[4] for s in started[1:]]))
    (win_f,) = whole(_gather_finish("gather_finish_0", win))
    proj = _in_proj_rest("in_proj_rest", xn, win_f, proj, place, (2, 1, 3))
    cos, sin = _rope_tables()
    nw = jnp.stack([jnp.concatenate([jnp.tile(q_norm_a, (1, NHA)), jnp.tile(q_norm_b, (1, NH - NHA))], axis=1),
                    jnp.concatenate([jnp.tile(k_norm_a, (1, NHA)), jnp.tile(k_norm_b, (1, NH - NHA))], axis=1)])
    qkn = _qk_prep(proj, nw, cos, sin)
    fw1, token = forward_begin(1, (qkn,))
    fwd_a = [_attn_a_fwd(qkn, proj, g) for g in range(3)]
    os, ls = [f[0] for f in fwd_a], [f[1] for f in fwd_a]
    fw2, token = forward_begin(2, (os[2], token))
    ob, lse_b, bias = _attn_b_fwd(qkn, proj, _rpb_rows(rpb_b[0]))
    oa, w0, w1, w2 = _comb_fwd(os, ls)
    ws = [w0, w1, w2]
    wpa_f, wpb_f, wout_f = forward_end(1, fw1, (oa, token))
    wout_f = wout_f.reshape(D, D)
    mixed, ob16 = _mix_fwd(oa, ob, proj, b_gate, wpa_f, wpb_f)
    h1, hn = _out_proj_fwd(mixed, wout_f, x2, norm_ffn)
    (wup_f,) = forward_end(2, fw2, (hn,))
    usq, u = _ffn_up(hn, wup_f)
    fw3, token = forward_begin(3, (u,))
    own = _ffn_down_own(usq, whole(fw3[2])[0].reshape(DFF, D), place)
    (wdown_f,) = forward_end(3, fw3, (own, token))
    wdown_f = wdown_f.reshape(DFF, D)
    dy, dy16, loss_parts = _ffn_down_loss(usq, wdown_f, h1, target, own, place)
    loss = lax.psum(jnp.sum(loss_parts[:, 0, 0]), ("x", "y", "c"))

    g_down = _grad_w("grad_w_down", usq, dy16, True, DFF // NSH, D, 1024, 1024)
    ex_down, token = exchange_begin(3, [g_down])
    du = _ffn_down_bwd(dy16, wdown_f, u, deps=(token,))
    g_up = _grad_w("grad_w_up", hn, du, False, D, DFF // NSH, 1024, 1024)
    red_down, token = exchange_end(3, ex_down, (g_up,))
    ex_up, token_up = exchange_begin(2, [g_up])
    dh1, dh16, d_norm_ffn = _ffn_up_bwd(du, wup_f, h1, dy, norm_ffn, deps=(token, token_up))
    dya, dyb, dga, dgb, doa, dob, dba, dbb = _mix_bwd(dh16, wout_f, oa, ob16, proj, b_gate, wpa_f, wpb_f)
    g_out = _grad_w("grad_w_out", mixed, dh16, True, D // NSH, D, 512, 1024)
    g_pa = _grad_w("grad_w_proj_a", oa, dya, False, 512, 512, 512, 512)
    g_pb = _grad_w("grad_w_proj_b", ob16, dyb, False, 512, 512, 512, 512)
    red_up, token = exchange_end(2, ex_up, (g_out,))
    ex_mid, token_mid = exchange_begin(1, [g_pa, g_pb, g_out])
    cc = _comb_bwd(doa, os, ws, deps=(token, token_mid))
    bwd_a = [_attn_a_bwd(qkn, proj, doa, ls[g], ws[g], cc, g) for g in range(3)]
    red_mid, token = exchange_end(1, ex_mid, (bwd_a[2][1],))
    dqk_b, dv_b, drpb_t = _attn_b_bwd(qkn, proj, dob, ob, lse_b, bias, deps=(token,))
    dproj, dn = _qk_bwd(proj, nw, cos, sin, [b[0] for b in bwd_a], dqk_b, [b[1] for b in bwd_a] + [dv_b], dga, dgb)
    g_in_theirs = _grad_w_in_half("grad_w_in_for_sibling", xn, dproj, place, True)
    send, recv, g_in_theirs, lands, token = _exchange_start("exchange_start_0", [g_in_theirs], sliced=False)
    g_in_mine = _grad_w_in_half("grad_w_in_own", xn, dproj, place, False, deps=(token,))
    _, theirs = _exchange_wait("exchange_wait_0", send, recv, g_in_theirs, lands, (g_in_mine,), sliced=False)
    red_in, token = reduce_start(0, [g_in_mine], theirs)
    grad_x, d_norm_mix = _in_proj_bwd(dproj, win_f, x2, dh1, norm_mix, deps=(token,))

    sh_down, token = share_begin(3, red_down, (grad_x,))
    sh_up, token = share_begin(2, red_up, (token,))
    done = share_end(3, sh_down, (token,))
    sh_mid, token = share_begin(1, red_mid, (done,))
    done = share_end(2, sh_up, (token,))
    sh_in, token = share_begin(0, red_in, (done,))
    done = share_end(1, sh_mid, (token,))
    done = share_end(0, sh_in, (done,))

    d_rpb = drpb_t[:, :15, GRID_W - WIN_C:GRID_W + WIN_C - 1]
    small_g = [d_norm_mix, jnp.concatenate([dba, dbb], axis=1), dn[0, 0], dn[1, 0], dn[0, 1], dn[1, 1], d_rpb, d_norm_ffn]
    gathered_small = _allgather_small(_pack_small(small_g), done)
    small_w = (norm_mix, b_gate, q_norm_a, k_norm_a, q_norm_b, k_norm_b, rpb_b, norm_ffn)
    small_m = (m_norm_mix, m_b_gate, m_q_norm_a, m_k_norm_a, m_q_norm_b, m_k_norm_b, m_rpb_b, m_norm_ffn)
    small_v = (v_norm_mix, v_b_gate, v_q_norm_a, v_k_norm_a, v_q_norm_b, v_k_norm_b, v_rpb_b, v_norm_ffn)
    small_out = [_unpack_small(p) for p in
                 _adam_small(gathered_small, _pack_small(small_w), _pack_small(small_m), _pack_small(small_v))]

    order = ("norm_mix", "w_in", "b_gate", "q_norm_a", "k_norm_a", "q_norm_b", "k_norm_b", "rpb_b",
             "w_proj_a", "w_proj_b", "w_out", "norm_ffn", "w_up", "w_down")
    small_idx = {name: i for i, (name, _) in enumerate(SMALL)}
    outs = []
    for kind in range(4):
        for name in order:
            if name in small_idx:
                outs.append(small_out[kind][small_idx[name]])
            else:
                outs.append(big_out[big_names.index(name)][kind][None])
    return (loss, grad_x[None], *outs)
```

```python
import functools

import numpy as np
import jax
import jax.numpy as jnp
from jax import lax
from jax.experimental import pallas as pl
from jax.experimental.pallas import tpu as pltpu

F32, BF16 = jnp.float32, jnp.bfloat16
SDS = jax.ShapeDtypeStruct
MESH = pl.DeviceIdType.MESH

T = 2048
D = 2048
HD = 128
NH, NHA = 16, 12
DIN = 10240
DFF = 8192
NSH = 4
DILS = (1, 4, 16)
EPS = 1e-6
NEG = -1e30
SCALE = HD ** -0.5
GRID_W, WIN_R, WIN_C = 64, 8, 16
VMEM_LIMIT = 56 * 1024 * 1024
B1, B2, LR, AEPS, WD, STEP = 0.9, 0.999, 0.001, 1e-08, 0.01, 10
SMALL_ROWS = 88


def _dot(a, b):
    return jnp.dot(a, b, preferred_element_type=F32)


def _dot_nt(a, b):
    return lax.dot_general(a, b, (((1,), (1,)), ((), ())), preferred_element_type=F32)


def _dot_tn(a, b):
    return lax.dot_general(a, b, (((0,), (0,)), ((), ())), preferred_element_type=F32)


def _params(n):
    return pltpu.CompilerParams(dimension_semantics=("arbitrary",) * n, vmem_limit_bytes=VMEM_LIMIT)


def _resident(shape, index_map):
    return pl.BlockSpec(shape, index_map, pipeline_mode=pl.Buffered(1))


def _sigmoid(z):
    return 1.0 / (1.0 + jnp.exp(-z))


def _wide(v, n):
    return jnp.concatenate([v] * n, axis=1)


def _row_tile(rows, cols, elems):
    tr = 16
    while tr * 2 <= rows and tr * 2 * cols <= elems:
        tr *= 2
    return tr


def _place():
    x, y, c = lax.axis_index("x"), lax.axis_index("y"), lax.axis_index("c")
    peers = [(1 - x, y), (x, 1 - y), (1 - x, 1 - y)]
    return x, y, c, peers


def _cast_into_place(w, name, place, deps=()):
    rows, cols = w.shape
    hr = rows // 2
    tr = min(hr, 256)
    per = hr // tr

    def body(*refs):
        w_ref, o_ref = refs[-2:]
        o_ref[...] = w_ref[...].astype(BF16)

    return pl.pallas_call(
        body, name=name, out_shape=SDS((NSH, 2, hr, cols), BF16),
        grid_spec=pltpu.PrefetchScalarGridSpec(
            num_scalar_prefetch=1, grid=(2, per),
            in_specs=[DEP_SPEC] * len(deps) + [pl.BlockSpec((tr, cols), lambda h, i, p: (h * per + i, 0))],
            out_specs=pl.BlockSpec((None, None, tr, cols), lambda h, i, p: (p[0], h, i, 0))),
        compiler_params=_params(2))(place, *deps, w)


ANY_SPEC = pl.BlockSpec(memory_space=pl.ANY)
HBM_SPEC = pl.BlockSpec(memory_space=pltpu.HBM)
SEM_SPEC = pl.BlockSpec(memory_space=pltpu.SEMAPHORE)
DEP_SPEC = pl.BlockSpec((8, 128), lambda *_: (0, 0))
EFFECT = pltpu.SideEffectType.DATAFLOW_SIDE_EFFECTING


def _after(body, deps):
    n = len(deps)
    return (lambda *refs: body(*refs[n:])) if n else body


SIBLING_BARRIER = 1


def _split_start(name, srcs, lands, n_copies, issue, sibling_only=False, after=()):
    n, m, d = len(srcs), len(lands), len(after)

    def body(*refs):
        if sibling_only:
            x, y, c, _ = _place()
            barrier = pltpu.get_barrier_semaphore()
            pl.semaphore_signal(barrier, inc=1, device_id=(x, y, 1 - c), device_id_type=MESH)
            pl.semaphore_wait(barrier, 1)
        issue(refs[:n], refs[n:n + m], refs[n + m + d], refs[n + m + d + 1])
        refs[-1][...] = jnp.zeros((8, 128), F32)

    arrays = list(srcs) + list(lands)
    outs = pl.pallas_call(
        body, name=name,
        out_shape=(pltpu.SemaphoreType.DMA((n_copies,)), pltpu.SemaphoreType.DMA((n_copies,)),
                   *[pltpu.HBM(a.shape, a.dtype) for a in arrays], SDS((8, 128), F32)),
        in_specs=[HBM_SPEC] * (n + m) + [ANY_SPEC] * d,
        out_specs=(SEM_SPEC, SEM_SPEC, *[HBM_SPEC] * (n + m), pl.BlockSpec(memory_space=pltpu.VMEM)),
        input_output_aliases={i: 2 + i for i in range(n + m)},
        compiler_params=pltpu.CompilerParams(has_side_effects=EFFECT,
                                             collective_id=SIBLING_BARRIER if sibling_only else None),
    )(*[pltpu.with_memory_space_constraint(a, pltpu.HBM) for a in arrays], *after)
    return outs[0], outs[1], list(outs[2:2 + n]), list(outs[2 + n:2 + n + m]), outs[-1]


def _split_wait(name, send_sems, recv_sems, srcs, lands, after, wait):
    n, m = len(srcs), len(lands)

    def body(*refs):
        wait(refs[:n], refs[n:n + m], refs[n + m], refs[n + m + 1])

    arrays = list(srcs) + list(lands)
    outs = pl.pallas_call(
        body, name=name, out_shape=[pltpu.HBM(a.shape, a.dtype) for a in arrays],
        in_specs=[HBM_SPEC] * (n + m) + [SEM_SPEC, SEM_SPEC] + [ANY_SPEC] * len(after),
        out_specs=[HBM_SPEC] * (n + m), input_output_aliases={i: i for i in range(n + m)},
        compiler_params=pltpu.CompilerParams(has_side_effects=EFFECT),
    )(*arrays, send_sems, recv_sems, *after)
    return list(outs[:n]), list(outs[n:])


def _gather_start(name, fulls, ks=(0, 1, 2), after=()):
    def issue(srcs, dsts, send_sems, recv_sems):
        x, y, c, peers = _place()
        for i in range(len(fulls)):
            mine = dsts[i].at[2 * x + y, c]
            for k in ks:
                px, py = peers[k]
                pltpu.make_async_remote_copy(
                    src_ref=mine, dst_ref=mine, send_sem=send_sems.at[3 * i + k],
                    recv_sem=recv_sems.at[3 * i + k], device_id=(px, py, c), device_id_type=MESH).start()

    return _split_start(name, [], fulls, 3 * len(fulls), issue, after=after)


def _gather_wait(name, send_sems, recv_sems, fulls, after, ks=(0, 1, 2)):
    def wait(srcs, dsts, send_sems, recv_sems):
        x, y, c, peers = _place()
        for i in range(len(fulls)):
            for k in ks:
                px, py = peers[k]
                cp = pltpu.make_async_remote_copy(
                    src_ref=dsts[i].at[2 * x + y, c], dst_ref=dsts[i].at[2 * px + py, c],
                    send_sem=send_sems.at[3 * i + k], recv_sem=recv_sems.at[3 * i + k],
                    device_id=(px, py, c), device_id_type=MESH)
                cp.wait_send()
                cp.wait_recv()

    return _split_wait(name, send_sems, recv_sems, [], fulls, after, wait)[1]


def _gather_finish(name, fulls, ks=(0, 1, 2)):
    n = len(fulls)

    def body(*refs):
        fin, fout = refs[:n], refs[n:2 * n]
        send_sems, recv_sems = refs[2 * n:]
        x, y, c, peers = _place()

        def copy(i, k, half):
            px, py = peers[k]
            return pltpu.make_async_remote_copy(
                src_ref=fin[i].at[2 * px + py, half], dst_ref=fout[i].at[2 * px + py, half],
                send_sem=send_sems.at[3 * i + k], recv_sem=recv_sems.at[3 * i + k],
                device_id=(x, y, 1 - c), device_id_type=MESH)

        sends = [copy(i, k, c) for i in range(n) for k in ks]
        for cp in sends:
            cp.start()
        for i in range(n):
            for k in ks:
                copy(i, k, 1 - c).wait_recv()
        for cp in sends:
            cp.wait_send()

    return pl.pallas_call(
        body, name=name, out_shape=[SDS(f.shape, f.dtype) for f in fulls],
        in_specs=[ANY_SPEC] * n, out_specs=[ANY_SPEC] * n, input_output_aliases={i: i for i in range(n)},
        scratch_shapes=[pltpu.SemaphoreType.DMA((3 * n,)), pltpu.SemaphoreType.DMA((3 * n,))])(*fulls)


def _reduce_start(name, parts):
    lands = [lax.empty((3,) + p.shape[1:], p.dtype) for p in parts]

    def issue(srcs, dsts, send_sems, recv_sems):
        x, y, c, peers = _place()
        for i in range(len(parts)):
            for k, (px, py) in enumerate(peers):
                pltpu.make_async_remote_copy(
                    src_ref=srcs[i].at[2 * px + py], dst_ref=dsts[i].at[k], send_sem=send_sems.at[3 * i + k],
                    recv_sem=recv_sems.at[3 * i + k], device_id=(px, py, c), device_id_type=MESH).start()

    return _split_start(name, parts, lands, 3 * len(parts), issue)


def _reduce_wait(name, send_sems, recv_sems, parts, lands, after):
    def wait(srcs, dsts, send_sems, recv_sems):
        x, y, c, peers = _place()
        for i in range(len(parts)):
            for k, (px, py) in enumerate(peers):
                cp = pltpu.make_async_remote_copy(
                    src_ref=srcs[i].at[2 * px + py], dst_ref=dsts[i].at[k], send_sem=send_sems.at[3 * i + k],
                    recv_sem=recv_sems.at[3 * i + k], device_id=(px, py, c), device_id_type=MESH)
                cp.wait_send()
                cp.wait_recv()

    return _split_wait(name, send_sems, recv_sems, parts, lands, after, wait)


def _sibling_copy(src, dst, send_sems, recv_sems, k):
    x, y, c, _ = _place()
    return pltpu.make_async_remote_copy(src_ref=src, dst_ref=dst, send_sem=send_sems.at[k], recv_sem=recv_sems.at[k],
                                        device_id=(x, y, 1 - c), device_id_type=MESH)


def _forward_start(name, fulls):
    def issue(srcs, dsts, send_sems, recv_sems):
        x, y, c, peers = _place()
        for i in range(len(fulls)):
            for k, (px, py) in enumerate(peers):
                part = dsts[i].at[2 * px + py, c]
                _sibling_copy(part, part, send_sems, recv_sems, 3 * i + k).start()

    return _split_start(name, [], fulls, 3 * len(fulls), issue, sibling_only=True)


def _forward_wait(name, send_sems, recv_sems, fulls, after):
    def wait(srcs, dsts, send_sems, recv_sems):
        x, y, c, peers = _place()
        for i in range(len(fulls)):
            for k, (px, py) in enumerate(peers):
                cp = _sibling_copy(dsts[i].at[2 * px + py, c], dsts[i].at[2 * px + py, 1 - c], send_sems, recv_sems, 3 * i + k)
                cp.wait_send()
                cp.wait_recv()

    return _split_wait(name, send_sems, recv_sems, [], fulls, after, wait)[1]


def _exchange_start(name, grads, sliced=True):
    lands = [lax.empty((NSH,) + g.shape[-2:], g.dtype) for g in grads]

    def issue(srcs, dsts, send_sems, recv_sems):
        c = lax.axis_index("c")
        for i in range(len(grads)):
            src = srcs[i].at[:, 1 - c] if sliced else srcs[i]
            _sibling_copy(src, dsts[i], send_sems, recv_sems, i).start()

    return _split_start(name, grads, lands, len(grads), issue, sibling_only=True)


def _exchange_wait(name, send_sems, recv_sems, grads, lands, after, sliced=True):
    def wait(srcs, dsts, send_sems, recv_sems):
        c = lax.axis_index("c")
        for i in range(len(grads)):
            cp = _sibling_copy(srcs[i].at[:, 1 - c] if sliced else srcs[i], dsts[i], send_sems, recv_sems, i)
            cp.wait_send()
            cp.wait_recv()

    return _split_wait(name, send_sems, recv_sems, grads, lands, after, wait)


def _share_start(name, sums):
    def issue(srcs, dsts, send_sems, recv_sems):
        c = lax.axis_index("c")
        for i in range(len(sums)):
            _sibling_copy(dsts[i].at[c], dsts[i].at[c], send_sems, recv_sems, i).start()

    return _split_start(name, [], sums, len(sums), issue, sibling_only=True)


def _share_wait(name, send_sems, recv_sems, sums, after):
    def wait(srcs, dsts, send_sems, recv_sems):
        c = lax.axis_index("c")
        for i in range(len(sums)):
            cp = _sibling_copy(dsts[i].at[c], dsts[i].at[1 - c], send_sems, recv_sems, i)
            cp.wait_send()
            cp.wait_recv()

    return _split_wait(name, send_sems, recv_sems, [], sums, after, wait)[1]


def _allgather_small(v, after):
    m_per, n = v.shape

    def body(x_ref, after_ref, out_ref, send_sems, recv_sems, local_sem):
        x, y, c = lax.axis_index("x"), lax.axis_index("y"), lax.axis_index("c")
        me, sibling = (x, y, c), (x, y, 1 - c)
        chips = [(1 - x, y), (x, 1 - y), (1 - x, 1 - y)]

        def rows(px, py, pc):
            return out_ref.at[pl.ds((4 * px + 2 * py + pc) * m_per, m_per), :]

        def copy(k, block, to, src=None):
            return pltpu.make_async_remote_copy(
                src_ref=rows(*block) if src is None else src, dst_ref=rows(*block),
                send_sem=send_sems.at[k], recv_sem=recv_sems.at[k], device_id=to, device_id_type=MESH)

        mine = pltpu.make_async_copy(x_ref, rows(*me), local_sem)
        mine.start()
        first = [copy(0, me, sibling, src=x_ref)]
        first += [copy(1 + j, me, (*chip, c), src=x_ref) for j, chip in enumerate(chips)]
        for cp in first:
            cp.start()
        passed = [copy(4 + j, (*chip, c), sibling) for j, chip in enumerate(chips)]
        for j, chip in enumerate(chips):
            copy(1 + j, (*chip, c), me).wait_recv()
            passed[j].start()
        copy(0, sibling, me).wait_recv()
        for j, chip in enumerate(chips):
            copy(4 + j, (*chip, 1 - c), me).wait_recv()
        for cp in first + passed:
            cp.wait_send()
        mine.wait()

    return pl.pallas_call(
        body, name="allgather_small", out_shape=SDS((8 * m_per, n), v.dtype),
        in_specs=[pl.BlockSpec(memory_space=pltpu.VMEM), ANY_SPEC], out_specs=pl.BlockSpec(memory_space=pltpu.VMEM),
        scratch_shapes=[pltpu.SemaphoreType.DMA((7,)), pltpu.SemaphoreType.DMA((7,)), pltpu.SemaphoreType.DMA])(v, after)


def _norm_in_proj_own(x, g, w_full, place):
    tn, chunk = 512, 256
    per = (DIN // NSH) // tn

    def body(place_ref, x_ref, g_ref, w_ref, proj_ref, xn_ref):
        @pl.when(pl.program_id(0) == 0)
        def _():
            def norm(r, carry):
                rows = pl.ds(pl.multiple_of(r * chunk, chunk), chunk)
                xv = x_ref[rows, :]
                rs = lax.rsqrt(jnp.mean(xv * xv, axis=-1, keepdims=True) + EPS)
                xn_ref[rows, :] = (xv * rs * g_ref[...]).astype(BF16)
                return carry

            lax.fori_loop(0, T // chunk, norm, 0)

        proj_ref[...] = _dot(xn_ref[...], w_ref[...])

    return pl.pallas_call(
        body, name="norm_in_proj_own", out_shape=[SDS((T, DIN), F32), SDS((T, D), BF16)],
        grid_spec=pltpu.PrefetchScalarGridSpec(
            num_scalar_prefetch=1, grid=(per,),
            in_specs=[_resident((T, D), lambda j, p: (0, 0)),
                      pl.BlockSpec((1, D), lambda j, p: (0, 0)),
                      pl.BlockSpec((None, D, tn), lambda j, p: (p[0], 0, j))],
            out_specs=[pl.BlockSpec((T, tn), lambda j, p: (0, p[0] * per + j)),
                       pl.BlockSpec((T, D), lambda j, p: (0, 0))]),
        compiler_params=_params(1))(place, x, g, w_full)


def _in_proj_rest(name, xn, w_full, proj, place, flips):
    tn = 512
    per = (DIN // NSH) // tn

    def body(place_ref, xn_ref, w_ref, proj_in, proj_ref):
        proj_ref[...] = _dot(xn_ref[...], w_ref[...])

    def shard(j, p):
        flip = flips[0]
        for n, f in enumerate(flips[1:]):
            flip = jnp.where(j // per == n + 1, f, flip)
        return p[0] ^ flip

    return pl.pallas_call(
        body, name=name, out_shape=SDS((T, DIN), F32),
        grid_spec=pltpu.PrefetchScalarGridSpec(
            num_scalar_prefetch=1, grid=(len(flips) * per,),
            in_specs=[_resident((T, D), lambda j, p: (0, 0)),
                      pl.BlockSpec((None, D, tn), lambda j, p: (shard(j, p), 0, j % per)), ANY_SPEC],
            out_specs=pl.BlockSpec((T, tn), lambda j, p: (0, shard(j, p) * per + j % per))),
        input_output_aliases={3: 0}, compiler_params=_params(1))(place, xn, w_full, proj)


def _rope_tables():
    pos = np.arange(T, dtype=np.float32)
    inv = (10000.0 ** (-np.arange(0, HD, 2, dtype=np.float32) / HD)).astype(np.float32)
    ang = (pos[:, None] * inv[None, :]).astype(np.float32)
    cos, sin = np.cos(ang).astype(np.float32), np.sin(ang).astype(np.float32)
    return (jnp.asarray(np.concatenate([cos, cos], axis=1)), jnp.asarray(np.concatenate([-sin, sin], axis=1)))


def _qk_prep(proj, nw, cos, sin):
    tm = 256

    def body(p_ref, w_ref, cos_ref, sin_ref, o_ref):
        cv, sv = cos_ref[...], sin_ref[...]
        for h in range(NH):
            sl = slice(h * HD, (h + 1) * HD)
            xv = p_ref[:, sl]
            r = lax.rsqrt(jnp.mean(xv * xv, axis=-1, keepdims=True) + EPS)
            z = xv * r * w_ref[:, sl]
            if h < NHA:
                z = z * cv + pltpu.roll(z, 64, 1) * sv
            o_ref[:, sl] = z.astype(BF16)

    return pl.pallas_call(
        body, name="qk_prep", out_shape=SDS((T, 2 * D), BF16), grid=(T // tm, 2),
        in_specs=[pl.BlockSpec((tm, D), lambda i, j: (i, j)),
                  pl.BlockSpec((None, 1, D), lambda i, j: (j, 0, 0)),
                  pl.BlockSpec((tm, HD), lambda i, j: (i, 0)),
                  pl.BlockSpec((tm, HD), lambda i, j: (i, 0))],
        out_specs=pl.BlockSpec((tm, D), lambda i, j: (i, j)),
        compiler_params=_params(2))(proj, nw, cos, sin)


def _band_mask(q0, m):
    ii = lax.broadcasted_iota(jnp.int32, (128, 256), 0)
    jj = lax.broadcasted_iota(jnp.int32, (128, 256), 1)
    rel = jj - ii
    kpos = jj + (q0 - 64)
    return (rel >= 0) & (rel <= 128) & (kpos >= 0) & (kpos < m)


def _fill_padded(dst, src, m):
    zeros = jnp.zeros((64, HD), dst.dtype)
    dst[0:64, :] = zeros
    dst[64 + m:128 + m, :] = zeros
    dst[64:64 + m, :] = src.astype(dst.dtype)


def _residue_rows(r, m, dil):
    return pl.ds(r, m, stride=dil) if dil > 1 else slice(None)


def _head_blocks(g):
    col = lambda base: pl.BlockSpec((T, HD), lambda h: (0, base + g * 4 + h))
    return col(0), col(NH), col(2 * NH), pl.BlockSpec((T, HD), lambda h: (0, h))


def _attn_a_fwd(qkn, proj, g):
    dil = DILS[g]
    m = T // dil
    nb = m // 128

    def body(q_ref, k_ref, v_ref, o_ref, l_ref, qf, kf, qp, kp, vp, ob, lb):
        qf[...] = q_ref[...].astype(F32)
        kf[...] = k_ref[...].astype(F32)
        for r in range(dil):
            rows = _residue_rows(r, m, dil)
            qp[...] = qf[rows, :].astype(BF16)
            _fill_padded(kp, kf[rows, :], m)
            _fill_padded(vp, v_ref[rows, :], m)

            def block(b, carry):
                q0 = pl.multiple_of(b * 128, 128)
                kw, vw = kp[pl.ds(q0, 256), :], vp[pl.ds(q0, 256), :]
                s = _dot_nt(qp[pl.ds(q0, 128), :], kw) * SCALE
                s = jnp.where(_band_mask(q0, m), s, NEG)
                mx = jnp.max(s, axis=-1, keepdims=True)
                e = jnp.exp(s - mx)
                den = jnp.sum(e, axis=-1, keepdims=True)
                ob[pl.ds(q0, 128), :] = _dot((e / den).astype(BF16), vw)
                lb[pl.ds(q0, 128), :] = jnp.broadcast_to(mx + jnp.log(den), (128, HD))
                return carry

            lax.fori_loop(0, nb, block, 0, unroll=min(nb, 16))
            o_ref[rows, :] = ob[...]
            l_ref[rows, :] = lb[...]

    q_blk, k_blk, v_blk, out_blk = _head_blocks(g)
    return pl.pallas_call(
        body, name=f"attn_a_fwd_{g}", out_shape=[SDS((T, 512), F32)] * 2, grid=(4,),
        in_specs=[q_blk, k_blk, v_blk], out_specs=[out_blk] * 2,
        scratch_shapes=[pltpu.VMEM((T, HD), F32), pltpu.VMEM((T, HD), F32), pltpu.VMEM((m, HD), BF16),
                        pltpu.VMEM((m + 128, HD), BF16), pltpu.VMEM((m + 128, HD), BF16),
                        pltpu.VMEM((m, HD), F32), pltpu.VMEM((m, HD), F32)],
        compiler_params=_params(1))(qkn, qkn, proj)


def _nbr_window(r):
    start = jnp.clip(r - WIN_R // 2, 0, T // GRID_W - WIN_R)
    return start, start - r + (WIN_R - 1)


def _rpb_rows(rpb):
    zeros = jnp.zeros((4, 14, 33), F32)
    a, b = rpb[:, :14], rpb[:, 1:15]
    rows = jnp.concatenate([a[:, :, 15:31], zeros, b, zeros, a[:, :, 0:15]], axis=2)
    return jnp.pad(rows, ((0, 0), (0, 2), (0, 0)))


def _attn_b_fwd(qkn, proj, rpb_rows):
    def body(r_ref, q_ref, k_ref, v_ref, o_ref, l_ref, bias_ref, vb, pair):
        qc = lax.broadcasted_iota(jnp.int32, (GRID_W, 512), 0)
        kc = lax.broadcasted_iota(jnp.int32, (GRID_W, 512), 1) & (GRID_W - 1)
        cs = jnp.clip(qc - WIN_C // 2, 0, GRID_W - WIN_C)
        colmask = (kc >= cs) & (kc < cs + WIN_C)
        for d in range(14):
            pair[d] = pltpu.roll(jnp.broadcast_to(r_ref[d:d + 1, :], (GRID_W, HD)), 0, 1, stride=1, stride_axis=0)
        for off in range(8):
            rows = jnp.concatenate([pair[off + 2 * jj] for jj in range(4)], axis=1)
            bias_ref[off] = jnp.where(colmask, rows, NEG)
        vb[...] = v_ref[...].astype(BF16)

        def row(r, carry):
            start, off = _nbr_window(r)
            q0 = pl.multiple_of(r * GRID_W, GRID_W)
            k0 = pl.multiple_of(start * GRID_W, GRID_W)
            s = _dot_nt(q_ref[pl.ds(q0, GRID_W), :], k_ref[pl.ds(k0, 512), :]) * SCALE + bias_ref[off]
            mx = jnp.max(s, axis=-1, keepdims=True)
            e = jnp.exp(s - mx)
            den = jnp.sum(e, axis=-1, keepdims=True)
            o_ref[pl.ds(q0, GRID_W), :] = _dot((e / den).astype(BF16), vb[pl.ds(k0, 512), :])
            l_ref[pl.ds(q0, GRID_W), :] = jnp.broadcast_to(mx + jnp.log(den), (GRID_W, HD))
            return carry

        lax.fori_loop(0, T // GRID_W, row, 0, unroll=16)

    return pl.pallas_call(
        body, name="attn_b_fwd",
        out_shape=[SDS((T, 512), F32), SDS((T, 512), F32), SDS((4, 8, GRID_W, 512), F32)], grid=(4,),
        in_specs=[pl.BlockSpec((None, 16, HD), lambda h: (h, 0, 0)),
                  pl.BlockSpec((T, HD), lambda h: (0, NHA + h)),
                  pl.BlockSpec((T, HD), lambda h: (0, NH + NHA + h)),
                  pl.BlockSpec((T, HD), lambda h: (0, 2 * NH + NHA + h))],
        out_specs=[pl.BlockSpec((T, HD), lambda h: (0, h)), pl.BlockSpec((T, HD), lambda h: (0, h)),
                   pl.BlockSpec((None, 8, GRID_W, 512), lambda h: (h, 0, 0, 0))],
        scratch_shapes=[pltpu.VMEM((T, HD), BF16), pltpu.VMEM((14, GRID_W, HD), F32)],
        compiler_params=_params(1))(rpb_rows, qkn, qkn, proj)


def _comb_fwd(os, ls):
    tm = 512

    def body(o0, o1, o2, l0, l1, l2, oa_ref, w0, w1, w2):
        lv = [l0[...], l1[...], l2[...]]
        mx = jnp.maximum(jnp.maximum(lv[0], lv[1]), lv[2])
        ev = [jnp.exp(l - mx) for l in lv]
        den = ev[0] + ev[1] + ev[2]
        wv = [e / den for e in ev]
        oa_ref[...] = (wv[0] * o0[...] + wv[1] * o1[...] + wv[2] * o2[...]).astype(BF16)
        w0[...], w1[...], w2[...] = wv

    spec = pl.BlockSpec((tm, 512), lambda i: (i, 0))
    return pl.pallas_call(
        body, name="comb_fwd", out_shape=[SDS((T, 512), BF16)] + [SDS((T, 512), F32)] * 3, grid=(T // tm,),
        in_specs=[spec] * 6, out_specs=[spec] * 4, compiler_params=_params(1))(*os, *ls)


def _mix_fwd(oa, ob, proj, b_gate, wpa, wpb):
    tm = 512

    def body(oa_ref, ob_ref, ga_ref, gb_ref, ba_ref, bb_ref, wpa_ref, wpb_ref, mixed_ref, ob16_ref):
        oav = oa_ref[...]
        obv = ob_ref[...].astype(BF16)
        ob16_ref[...] = obv
        for s in range(NSH):
            sl = slice(s * 512, (s + 1) * 512)
            ga = _sigmoid(ga_ref[:, sl] + ba_ref[:, sl])
            gb = _sigmoid(gb_ref[:, sl] + bb_ref[:, sl])
            mixed_ref[:, sl] = (ga * _dot(oav, wpa_ref[s]) + gb * _dot(obv, wpb_ref[s])).astype(BF16)

    row = lambda w: pl.BlockSpec((tm, w), lambda i: (i, 0))
    return pl.pallas_call(
        body, name="mix_fwd", out_shape=[SDS((T, D), BF16), SDS((T, 512), BF16)], grid=(T // tm,),
        in_specs=[row(512), row(512),
                  pl.BlockSpec((tm, D), lambda i: (i, 3)), pl.BlockSpec((tm, D), lambda i: (i, 4)),
                  pl.BlockSpec((1, D), lambda i: (0, 0)), pl.BlockSpec((1, D), lambda i: (0, 1)),
                  _resident((NSH, 512, 512), lambda i: (0, 0, 0)), _resident((NSH, 512, 512), lambda i: (0, 0, 0))],
        out_specs=[row(D), row(512)], compiler_params=_params(1))(oa, ob, proj, proj, b_gate, b_gate, wpa, wpb)


def _out_proj_fwd(mixed, w_out, x, g):
    tm = 512

    def body(m_ref, w_ref, x_ref, g_ref, h1_ref, hn_ref):
        h1 = x_ref[...] + _dot(m_ref[...], w_ref[...])
        h1_ref[...] = h1
        r = lax.rsqrt(jnp.mean(h1 * h1, axis=-1, keepdims=True) + EPS)
        hn_ref[...] = (h1 * r * g_ref[...]).astype(BF16)

    row = pl.BlockSpec((tm, D), lambda i: (i, 0))
    return pl.pallas_call(
        body, name="out_proj_fwd", out_shape=[SDS((T, D), F32), SDS((T, D), BF16)], grid=(T // tm,),
        in_specs=[row, _resident((D, D), lambda i: (0, 0)), row, pl.BlockSpec((1, D), lambda i: (0, 0))],
        out_specs=[row, row], compiler_params=_params(1))(mixed, w_out, x, g)


def _ffn_up(hn, w_up):
    tm, tn = T, 512
    per = (DFF // NSH) // tn

    def body(h_ref, w_ref, a_ref, u_ref):
        uv = jnp.maximum(_dot(h_ref[...], w_ref[...]), 0.0)
        a_ref[...] = (uv * uv).astype(BF16)
        u_ref[...] = uv.astype(BF16)

    out = pl.BlockSpec((tm, tn), lambda i, j: (i, j))
    return pl.pallas_call(
        body, name="ffn_up", out_shape=[SDS((T, DFF), BF16)] * 2, grid=(T // tm, DFF // tn),
        in_specs=[pl.BlockSpec((tm, D), lambda i, j: (i, 0)),
                  pl.BlockSpec((None, D, tn), lambda i, j: (j // per, 0, j % per))],
        out_specs=[out, out], compiler_params=_params(2))(hn, w_up)


def _ffn_down_own(u, w_down, place):
    tm, tk = 512, DFF // NSH

    def body(place_ref, u_ref, w_ref, o_ref):
        o_ref[...] = _dot(u_ref[...], w_ref[...])

    return pl.pallas_call(
        body, name="ffn_down_own", out_shape=SDS((T, D), F32),
        grid_spec=pltpu.PrefetchScalarGridSpec(
            num_scalar_prefetch=1, grid=(T // tm,),
            in_specs=[pl.BlockSpec((tm, tk), lambda i, p: (i, p[0])), pl.BlockSpec((tk, D), lambda i, p: (p[0], 0))],
            out_specs=pl.BlockSpec((tm, D), lambda i, p: (i, 0))),
        compiler_params=_params(1))(place, u, w_down)


def _ffn_down_loss(u, w_down, h1, target, own, place):
    tm, tk = 512, DFF // NSH
    nk = NSH - 1

    def body(place_ref, u_ref, w_ref, h1_ref, t_ref, own_ref, dy_ref, dy16_ref, loss_ref, acc):
        k = pl.program_id(1)

        @pl.when(k == 0)
        def _():
            acc[...] = own_ref[...]

        acc[...] += _dot(u_ref[...], w_ref[...])

        @pl.when(k == nk - 1)
        def _():
            def chunk(r, sq):
                rows = pl.ds(pl.multiple_of(r * 16, 16), 16)
                err = acc[rows, :] + h1_ref[rows, :] - t_ref[rows, :]
                dy = err * (1.0 / D)
                dy_ref[rows, :] = dy
                dy16_ref[rows, :] = dy.astype(BF16)
                return sq + err * err

            sq = lax.fori_loop(0, tm // 16, chunk, jnp.zeros((16, D), F32), unroll=2)
            part = 0.5 * jnp.sum(jnp.mean(sq, axis=-1, keepdims=True), axis=0, keepdims=True)
            loss_ref[...] = jnp.broadcast_to(part, (8, 128))

    row = pl.BlockSpec((tm, D), lambda i, k, p: (i, 0))
    once = _resident((tm, D), lambda i, k, p: (i, 0))
    shard = lambda k, p: p[0] ^ (k + 1)
    return pl.pallas_call(
        body, name="ffn_down_loss",
        out_shape=[SDS((T, D), F32), SDS((T, D), BF16), SDS((T // tm, 8, 128), F32)],
        grid_spec=pltpu.PrefetchScalarGridSpec(
            num_scalar_prefetch=1, grid=(T // tm, nk),
            in_specs=[pl.BlockSpec((tm, tk), lambda i, k, p: (i, shard(k, p))),
                      pl.BlockSpec((tk, D), lambda i, k, p: (shard(k, p), 0)), once, once, once],
            out_specs=[row, row, pl.BlockSpec((None, 8, 128), lambda i, k, p: (i, 0, 0))],
            scratch_shapes=[pltpu.VMEM((tm, D), F32)]),
        compiler_params=_params(2))(place, u, w_down, h1, target, own)


def _ffn_down_bwd(dy16, w_down, u, deps=()):
    tm, tn = T, 512

    def body(dy_ref, w_ref, u_ref, du_ref):
        uv = u_ref[...].astype(F32)
        du_ref[...] = jnp.where(uv > 0.0, 2.0 * uv * _dot_nt(dy_ref[...], w_ref[...]), 0.0).astype(BF16)

    return pl.pallas_call(
        _after(body, deps), name="ffn_down_bwd", out_shape=SDS((T, DFF), BF16), grid=(T // tm, DFF // tn),
        in_specs=[DEP_SPEC] * len(deps) + [
            pl.BlockSpec((tm, D), lambda i, j: (i, 0)), pl.BlockSpec((tn, D), lambda i, j: (j, 0)),
            pl.BlockSpec((tm, tn), lambda i, j: (i, j))],
        out_specs=pl.BlockSpec((tm, tn), lambda i, j: (i, j)), compiler_params=_params(2))(*deps, dy16, w_down, u)


def _norm_bwd(xv, dz_in, g):
    r = lax.rsqrt(jnp.mean(xv * xv, axis=-1, keepdims=True) + EPS)
    dg = jnp.sum(xv * r * dz_in, axis=0, keepdims=True)
    dz = dz_in * g
    dx = r * dz - xv * (r * r * r) * jnp.mean(xv * dz, axis=-1, keepdims=True)
    return dx, dg


def _ffn_up_bwd(du, w_up, h1, dy, g, deps=()):
    tm, tk = 512, 1024
    per = (DFF // NSH) // tk
    nk = DFF // tk

    def body(du_ref, w_ref, h1_ref, dy_ref, g_ref, dh1_ref, dh16_ref, dg_ref, acc):
        i, k = pl.program_id(0), pl.program_id(1)

        @pl.when(k == 0)
        def _():
            acc[...] = jnp.zeros_like(acc)

        @pl.when((k == 0) & (i == 0))
        def _():
            dg_ref[...] = jnp.zeros_like(dg_ref)

        acc[...] += _dot_nt(du_ref[...], w_ref[...])

        @pl.when(k == nk - 1)
        def _():
            dx, dg = _norm_bwd(h1_ref[...], acc[...], g_ref[...])
            dh1 = dy_ref[...] + dx
            dh1_ref[...] = dh1
            dh16_ref[...] = dh1.astype(BF16)
            dg_ref[...] += dg

    row = pl.BlockSpec((tm, D), lambda i, k: (i, 0))
    vec = pl.BlockSpec((1, D), lambda i, k: (0, 0))
    return pl.pallas_call(
        _after(body, deps), name="ffn_up_bwd", out_shape=[SDS((T, D), F32), SDS((T, D), BF16), SDS((1, D), F32)],
        grid=(T // tm, nk),
        in_specs=[DEP_SPEC] * len(deps) + [
            pl.BlockSpec((tm, tk), lambda i, k: (i, k)),
            pl.BlockSpec((None, D, tk), lambda i, k: (k // per, 0, k % per)), row, row, vec],
        out_specs=[row, row, vec], scratch_shapes=[pltpu.VMEM((tm, D), F32)],
        compiler_params=_params(2))(*deps, du, w_up, h1, dy, g)


def _mix_bwd(dh16, w_out, oa, ob16, proj, b_gate, wpa, wpb):
    tm = 256

    def body(dh_ref, wo_ref, oa_ref, ob_ref, ga_ref, gb_ref, ba_ref, bb_ref, wpa_ref, wpb_ref,
             dya_ref, dyb_ref, dga_ref, dgb_ref, doa_ref, dob_ref, dba_ref, dbb_ref):
        @pl.when(pl.program_id(0) == 0)
        def _():
            dba_ref[...] = jnp.zeros_like(dba_ref)
            dbb_ref[...] = jnp.zeros_like(dbb_ref)

        oav, obv = oa_ref[...], ob_ref[...]
        doa = jnp.zeros((tm, 512), F32)
        dob = jnp.zeros((tm, 512), F32)
        for s in range(NSH):
            sl = slice(s * 512, (s + 1) * 512)
            dm = _dot_nt(dh_ref[...], wo_ref[sl, :])
            ga = _sigmoid(ga_ref[:, sl] + ba_ref[:, sl])
            gb = _sigmoid(gb_ref[:, sl] + bb_ref[:, sl])
            dya = (dm * ga).astype(BF16)
            dyb = (dm * gb).astype(BF16)
            dza = dm * _dot(oav, wpa_ref[s]) * ga * (1.0 - ga)
            dzb = dm * _dot(obv, wpb_ref[s]) * gb * (1.0 - gb)
            dya_ref[:, sl], dyb_ref[:, sl] = dya, dyb
            dga_ref[:, sl], dgb_ref[:, sl] = dza.astype(BF16), dzb.astype(BF16)
            dba_ref[:, sl] += jnp.sum(dza, axis=0, keepdims=True)
            dbb_ref[:, sl] += jnp.sum(dzb, axis=0, keepdims=True)
            doa += _dot_nt(dya, wpa_ref[s])
            dob += _dot_nt(dyb, wpb_ref[s])
        doa_ref[...], dob_ref[...] = doa, dob

    row = lambda w: pl.BlockSpec((tm, w), lambda i: (i, 0))
    vec = pl.BlockSpec((1, D), lambda i: (0, 0))
    wp = _resident((NSH, 512, 512), lambda i: (0, 0, 0))
    return pl.pallas_call(
        body, name="mix_bwd",
        out_shape=[SDS((T, D), BF16)] * 4 + [SDS((T, 512), F32)] * 2 + [SDS((1, D), F32)] * 2, grid=(T // tm,),
        in_specs=[row(D), _resident((D, D), lambda i: (0, 0)), row(512), row(512),
                  pl.BlockSpec((tm, D), lambda i: (i, 3)), pl.BlockSpec((tm, D), lambda i: (i, 4)),
                  pl.BlockSpec((1, D), lambda i: (0, 0)), pl.BlockSpec((1, D), lambda i: (0, 1)), wp, wp],
        out_specs=[row(D)] * 4 + [row(512)] * 2 + [vec] * 2,
        compiler_params=_params(1))(dh16, w_out, oa, ob16, proj, proj, b_gate, b_gate, wpa, wpb)


def _comb_bwd(doa, os, ws, deps=()):
    tm = 512

    def body(d_ref, o0, o1, o2, w0, w1, w2, cc_ref):
        prod = d_ref[...] * (w0[...] * o0[...] + w1[...] * o1[...] + w2[...] * o2[...])
        for h in range(4):
            sl = slice(h * HD, (h + 1) * HD)
            cc_ref[:, sl] = jnp.broadcast_to(jnp.sum(prod[:, sl], axis=-1, keepdims=True), (tm, HD))

    spec = pl.BlockSpec((tm, 512), lambda i: (i, 0))
    return pl.pallas_call(
        _after(body, deps), name="comb_bwd", out_shape=SDS((T, 512), F32), grid=(T // tm,),
        in_specs=[DEP_SPEC] * len(deps) + [spec] * 7, out_specs=spec,
        compiler_params=_params(1))(*deps, doa, *os, *ws)


def _attn_a_bwd(qkn, proj, doa, lse, w, cc, g):
    dil = DILS[g]
    m = T // dil
    nb = m // 128

    def body(q_ref, k_ref, v_ref, d_ref, l_ref, w_ref, c_ref, dqk_ref, dv_ref,
             qf, kf, qp, kp, vp, dp, lp, wsub, cp, dqb, dkp, dvp):
        qf[...] = q_ref[...].astype(F32)
        kf[...] = k_ref[...].astype(F32)
        for r in range(dil):
            sub = _residue_rows(r, m, dil)
            qp[...] = qf[sub, :].astype(BF16)
            _fill_padded(kp, kf[sub, :], m)
            _fill_padded(vp, v_ref[sub, :], m)
            dp[...] = d_ref[sub, :].astype(BF16)
            lp[...], wsub[...], cp[...] = l_ref[sub, :], w_ref[sub, :], c_ref[sub, :]
            dkp[...] = jnp.zeros_like(dkp)
            dvp[...] = jnp.zeros_like(dvp)

            def block(b, carry):
                q0 = pl.multiple_of(b * 128, 128)
                rows = pl.ds(q0, 128)
                win = pl.ds(q0, 256)
                qb, kw, vw = qp[rows, :], kp[win, :], vp[win, :]
                s = _dot_nt(qb, kw) * SCALE
                s = jnp.where(_band_mask(q0, m), s, NEG)
                wp = _wide(wsub[rows, :], 2) * jnp.exp(s - _wide(lp[rows, :], 2))
                dob = dp[rows, :]
                ds = (wp * (_dot_nt(dob, vw) - _wide(cp[rows, :], 2))).astype(BF16)
                dqb[rows, :] = _dot(ds, kw) * SCALE
                dkp[win, :] += _dot_tn(ds, qb) * SCALE
                dvp[win, :] += _dot_tn(wp.astype(BF16), dob)
                return carry

            lax.fori_loop(0, nb, block, 0, unroll=min(nb, 16))
            dqk_ref.at[0][sub, :] = dqb[...]
            dqk_ref.at[1][sub, :] = dkp[64:64 + m, :]
            dv_ref[sub, :] = dvp[64:64 + m, :]

    q_blk, k_blk, v_blk, blk = _head_blocks(g)
    sub16 = pltpu.VMEM((m, HD), BF16)
    sub32 = pltpu.VMEM((m, HD), F32)
    return pl.pallas_call(
        body, name=f"attn_a_bwd_{g}", out_shape=[SDS((2, T, 512), F32), SDS((T, 512), F32)], grid=(4,),
        in_specs=[q_blk, k_blk, v_blk, blk, blk, blk, blk],
        out_specs=[pl.BlockSpec((2, T, HD), lambda h: (0, 0, h)), blk],
        scratch_shapes=[pltpu.VMEM((T, HD), F32), pltpu.VMEM((T, HD), F32), sub16,
                        pltpu.VMEM((m + 128, HD), BF16), pltpu.VMEM((m + 128, HD), BF16), sub16,
                        sub32, sub32, sub32, sub32,
                        pltpu.VMEM((m + 128, HD), F32), pltpu.VMEM((m + 128, HD), F32)],
        compiler_params=_params(1))(qkn, qkn, proj, doa, lse, w, cc)


def _attn_b_bwd(qkn, proj, dob, ob, lse, bias, deps=()):
    def body(q_ref, k_ref, v_ref, d_ref, o_ref, l_ref, bias_ref, dqk_ref, dv_ref, drpb_ref, vb, dk_acc, dv_acc, a_acc):
        vb[...] = v_ref[...].astype(BF16)
        dk_acc[...] = jnp.zeros_like(dk_acc)
        dv_acc[...] = jnp.zeros_like(dv_acc)
        a_acc[...] = jnp.zeros_like(a_acc)

        def row(r, carry):
            start, off = _nbr_window(r)
            rows = pl.ds(pl.multiple_of(r * GRID_W, GRID_W), GRID_W)
            win = pl.ds(pl.multiple_of(start * GRID_W, GRID_W), 512)
            qr, kw, vw = q_ref[rows, :], k_ref[win, :], vb[win, :]
            s = _dot_nt(qr, kw) * SCALE + bias_ref[off]
            p = jnp.exp(s - _wide(l_ref[rows, :], 4))
            dov = d_ref[rows, :]
            delta = jnp.sum(dov * o_ref[rows, :], axis=-1, keepdims=True)
            do16 = dov.astype(BF16)
            ds = p * (_dot_nt(do16, vw) - delta)
            a_acc[off] += ds
            ds16 = ds.astype(BF16)
            dqk_ref[0, rows, :] = _dot(ds16, kw) * SCALE
            dk_acc[win, :] += _dot_tn(ds16, qr) * SCALE
            dv_acc[win, :] += _dot_tn(p.astype(BF16), do16)
            return carry

        lax.fori_loop(0, T // GRID_W, row, 0, unroll=16)
        dqk_ref[1] = dk_acc[...]
        dv_ref[...] = dv_acc[...]

        lane = lax.broadcasted_iota(jnp.int32, (16, HD), 1)
        rowi = lax.broadcasted_iota(jnp.int32, (16, HD), 0)
        low = (lane >= GRID_W - WIN_C) & (lane < GRID_W + WIN_C - 1)
        high = (lane >= HD - WIN_C) | (lane < WIN_C - 1)
        flip = (lax.broadcasted_iota(jnp.int32, (GRID_W, GRID_W), 0)
                + lax.broadcasted_iota(jnp.int32, (GRID_W, GRID_W), 1) == GRID_W - 1).astype(BF16)
        out = jnp.zeros((16, HD), F32)
        for d in range(14):
            acc = None
            for off in range(8):
                if 0 <= d - off <= 6 and (d - off) % 2 == 0:
                    jj = (d - off) // 2
                    piece = a_acc[off, :, jj * HD:(jj + 1) * HD]
                    acc = piece if acc is None else acc + piece
            hi = acc.astype(BF16)
            lo = (acc - hi.astype(F32)).astype(BF16)
            rev = _dot(flip, hi) + _dot(flip, lo)
            v = jnp.sum(pltpu.roll(rev, 0, 1, stride=1, stride_axis=0), axis=0, keepdims=True)
            v = jnp.broadcast_to(v, (16, HD))
            out = out + jnp.where((rowi == d) & low, v, 0.0)
            out = out + jnp.where(rowi == d + 1, pltpu.roll(jnp.where(high, v, 0.0), GRID_W, 1), 0.0)
        drpb_ref[...] = out

    blk = pl.BlockSpec((T, HD), lambda h: (0, h))
    return pl.pallas_call(
        _after(body, deps), name="attn_b_bwd",
        out_shape=[SDS((2, T, 512), F32), SDS((T, 512), F32), SDS((4, 16, HD), F32)], grid=(4,),
        in_specs=[DEP_SPEC] * len(deps) + [
            pl.BlockSpec((T, HD), lambda h: (0, NHA + h)),
            pl.BlockSpec((T, HD), lambda h: (0, NH + NHA + h)),
            pl.BlockSpec((T, HD), lambda h: (0, 2 * NH + NHA + h)), blk, blk, blk,
            pl.BlockSpec((None, 8, GRID_W, 512), lambda h: (h, 0, 0, 0))],
        out_specs=[pl.BlockSpec((2, T, HD), lambda h: (0, 0, h)), blk,
                   pl.BlockSpec((None, 16, HD), lambda h: (h, 0, 0))],
        scratch_shapes=[pltpu.VMEM((T, HD), BF16), pltpu.VMEM((T, HD), F32), pltpu.VMEM((T, HD), F32),
                        pltpu.VMEM((8, GRID_W, 512), F32)],
        compiler_params=_params(1))(*deps, qkn, qkn, proj, dob, ob, lse, bias)


def _qk_bwd(proj, nw, cos, sin, dqk_groups, dqk_b, dvs, dga, dgb):
    tm = 512

    def body(p_ref, w_ref, cos_ref, sin_ref, d0, d1, d2, d3, v0, v1, v2, v3, ga_ref, gb_ref, o_ref, dn_ref):
        j, i = pl.program_id(0), pl.program_id(1)

        @pl.when((j < 2) & (i == 0))
        def _():
            dn_ref[...] = jnp.zeros_like(dn_ref)

        @pl.when(j < 2)
        def _():
            cv, sv = cos_ref[...], sin_ref[...]
            srcs = (d0, d1, d2, d3)
            dna = jnp.zeros((1, HD), F32)
            dnb = jnp.zeros((1, HD), F32)
            for h in range(NH):
                sl = slice(h * HD, (h + 1) * HD)
                dz = srcs[h // 4][:, (h % 4) * HD:(h % 4 + 1) * HD]
                if h < NHA:
                    dz = dz * cv + pltpu.roll(dz * sv, 64, 1)
                dx, dg = _norm_bwd(p_ref[:, sl], dz, w_ref[:, sl])
                o_ref[:, sl] = dx.astype(BF16)
                if h < NHA:
                    dna += dg
                else:
                    dnb += dg
            dn_ref[0:1, :] += dna
            dn_ref[1:2, :] += dnb

        @pl.when(j == 2)
        def _():
            for s, v_ref in enumerate((v0, v1, v2, v3)):
                o_ref[:, s * 512:(s + 1) * 512] = v_ref[...].astype(BF16)

        @pl.when(j == 3)
        def _():
            o_ref[...] = ga_ref[...]

        @pl.when(j == 4)
        def _():
            o_ref[...] = gb_ref[...]

    def rows(used):
        return lambda j, i: (jnp.where(used(j), i, 0), 0)

    qk = lambda j: j < 2
    dspec = pl.BlockSpec((None, tm, 512), lambda j, i: (jnp.minimum(j, 1), jnp.where(j < 2, i, 0), 0))
    vspec = pl.BlockSpec((tm, 512), rows(lambda j: j == 2))
    return pl.pallas_call(
        body, name="qk_bwd", out_shape=[SDS((T, DIN), BF16), SDS((2, 8, HD), F32)], grid=(5, T // tm),
        in_specs=[pl.BlockSpec((tm, D), lambda j, i: (jnp.where(j < 2, i, 0), jnp.minimum(j, 1))),
                  pl.BlockSpec((None, 1, D), lambda j, i: (jnp.minimum(j, 1), 0, 0)),
                  pl.BlockSpec((tm, HD), rows(qk)), pl.BlockSpec((tm, HD), rows(qk)),
                  dspec, dspec, dspec, dspec, vspec, vspec, vspec, vspec,
                  pl.BlockSpec((tm, D), rows(lambda j: j == 3)), pl.BlockSpec((tm, D), rows(lambda j: j == 4))],
        out_specs=[pl.BlockSpec((tm, D), lambda j, i: (i, j)),
                   pl.BlockSpec((None, 8, HD), lambda j, i: (jnp.minimum(j, 1), 0, 0))],
        compiler_params=_params(2))(proj, nw, cos, sin, *dqk_groups, dqk_b, *dvs, dga, dgb)


def _in_proj_bwd(dproj, w_in, x, dh1, g, deps=()):
    tm, tk = 512, 1280
    per = (DIN // NSH) // tk
    nk = DIN // tk

    def body(dp_ref, w_ref, x_ref, dh_ref, g_ref, dx_ref, dg_ref, acc):
        i, k = pl.program_id(0), pl.program_id(1)

        @pl.when(k == 0)
        def _():
            acc[...] = jnp.zeros_like(acc)

        @pl.when((k == 0) & (i == 0))
        def _():
            dg_ref[...] = jnp.zeros_like(dg_ref)

        acc[...] += _dot_nt(dp_ref[...], w_ref[...])

        @pl.when(k == nk - 1)
        def _():
            dx, dg = _norm_bwd(x_ref[...], acc[...], g_ref[...])
            dx_ref[...] = dh_ref[...] + dx
            dg_ref[...] += dg

    row = pl.BlockSpec((tm, D), lambda i, k: (i, 0))
    vec = pl.BlockSpec((1, D), lambda i, k: (0, 0))
    return pl.pallas_call(
        _after(body, deps), name="in_proj_bwd", out_shape=[SDS((T, D), F32), SDS((1, D), F32)], grid=(T // tm, nk),
        in_specs=[DEP_SPEC] * len(deps) + [
            pl.BlockSpec((tm, tk), lambda i, k: (i, k)),
            pl.BlockSpec((None, D, tk), lambda i, k: (k // per, 0, k % per)), row, row, vec],
        out_specs=[row, vec], scratch_shapes=[pltpu.VMEM((tm, D), F32)],
        compiler_params=_params(2))(*deps, dproj, w_in, x, dh1, g)


def _grad_w(name, a, g, shard_rows, rows, cols, tr, tc):
    ni, nj = rows // tr, cols // tc
    if shard_rows:
        a_map, g_map = (lambda s, i, j: (0, s * ni + i)), (lambda s, i, j: (0, j))
    else:
        a_map, g_map = (lambda s, i, j: (0, i)), (lambda s, i, j: (0, s * nj + j))

    def body(a_ref, g_ref, o_ref):
        o_ref[...] = _dot_tn(a_ref[...], g_ref[...]).astype(BF16)

    return pl.pallas_call(
        body, name=name, out_shape=SDS((NSH, rows, cols), BF16), grid=(NSH, ni, nj),
        in_specs=[pl.BlockSpec((T, tr), a_map), pl.BlockSpec((T, tc), g_map)],
        out_specs=pl.BlockSpec((None, tr, tc), lambda s, i, j: (s, i, j)), compiler_params=_params(3))(a, g)


def _grad_w_in_half(name, xn, dproj, place, for_sibling, deps=()):
    tr, tc = D // 2, 1280
    nj = (DIN // NSH) // tc

    def body(*refs):
        a_ref, g_ref, o_ref = refs[-3:]
        o_ref[...] = _dot_tn(a_ref[...], g_ref[...]).astype(BF16)

    half = (lambda p: 1 - p[1]) if for_sibling else (lambda p: p[1])
    return pl.pallas_call(
        body, name=name, out_shape=SDS((NSH, tr, DIN // NSH), BF16),
        grid_spec=pltpu.PrefetchScalarGridSpec(
            num_scalar_prefetch=1, grid=(NSH, nj),
            in_specs=[DEP_SPEC] * len(deps) + [pl.BlockSpec((T, tr), lambda s, j, p: (0, half(p))),
                                               pl.BlockSpec((T, tc), lambda s, j, p: (0, s * nj + j))],
            out_specs=pl.BlockSpec((None, tr, tc), lambda s, j, p: (s, 0, j))),
        compiler_params=_params(2))(place, *deps, xn, dproj)


def _adamw(w, g, m, v):
    m = B1 * m + (1.0 - B1) * g
    v = B2 * v + (1.0 - B2) * (g * g)
    m_hat = m / (1.0 - B1 ** STEP)
    v_hat = v / (1.0 - B2 ** STEP)
    delta = -LR * (m_hat / (jnp.sqrt(v_hat) + AEPS) + WD * w)
    return delta, m, v


def _sum_halves(name, place, grads, theirs):
    _, rows, cols = theirs.shape
    tr = _row_tile(rows, cols, 1 << 20)

    def body(place_ref, a_ref, b_ref, o_ref):
        o_ref[...] = (a_ref[...].astype(F32) + b_ref[...].astype(F32)).astype(BF16)

    spec = pl.BlockSpec((None, tr, cols), lambda s, i, p: (s, i, 0))
    mine = spec if grads.ndim == 3 else pl.BlockSpec((None, None, tr, cols), lambda s, i, p: (s, p[1], i, 0))
    return pl.pallas_call(
        body, name=name, out_shape=SDS(theirs.shape, BF16),
        grid_spec=pltpu.PrefetchScalarGridSpec(
            num_scalar_prefetch=1, grid=(NSH, rows // tr), in_specs=[mine, spec], out_specs=spec),
        compiler_params=_params(2))(place, grads, theirs)


def _sum_landed(name, place, part, landed):
    _, rows, cols = part.shape
    tr = _row_tile(rows, cols, 1 << 20)

    def body(place_ref, p_ref, l_ref, o_ref):
        o_ref[...] = ((p_ref[...].astype(F32) + l_ref[0].astype(F32)) + l_ref[1].astype(F32)) + l_ref[2].astype(F32)

    return pl.pallas_call(
        body, name=name, out_shape=SDS((2, rows, cols), F32),
        grid_spec=pltpu.PrefetchScalarGridSpec(
            num_scalar_prefetch=1, grid=(rows // tr,),
            in_specs=[pl.BlockSpec((None, tr, cols), lambda i, p: (p[0], i, 0)),
                      pl.BlockSpec((3, tr, cols), lambda i, p: (0, i, 0))],
            out_specs=pl.BlockSpec((None, tr, cols), lambda i, p: (p[1], i, 0))),
        compiler_params=_params(1))(place, part, landed)


def _adam_shard(name, g, w, m, v):
    rows, cols = w.shape
    tr = _row_tile(rows, cols, 1 << 19)

    def body(g_ref, w_ref, m_ref, v_ref, go_ref, d_ref, nm_ref, nv_ref):
        g = g_ref[...]
        go_ref[...] = g
        d_ref[...], nm_ref[...], nv_ref[...] = _adamw(w_ref[...], g, m_ref[...], v_ref[...])

    spec = pl.BlockSpec((tr, cols), lambda i: (i, 0))
    return pl.pallas_call(
        body, name=name, out_shape=[SDS((rows, cols), F32)] * 4, grid=(rows // tr,),
        in_specs=[spec] * 4, out_specs=[spec] * 4, compiler_params=_params(1))(g, w, m, v)


def _adam_small(gathered, w, m, v):
    def body(g_ref, w_ref, m_ref, v_ref, go_ref, d_ref, nm_ref, nv_ref):
        g = g_ref[0:SMALL_ROWS, :]
        for dev in range(1, 8):
            g = g + g_ref[dev * SMALL_ROWS:(dev + 1) * SMALL_ROWS, :]
        go_ref[...] = g
        d_ref[...], nm_ref[...], nv_ref[...] = _adamw(w_ref[...], g, m_ref[...], v_ref[...])

    return pl.pallas_call(body, name="adam_small", out_shape=[SDS((SMALL_ROWS, HD), F32)] * 4)(gathered, w, m, v)


SMALL = (("norm_mix", (1, D)), ("b_gate", (1, 2 * D)), ("q_norm_a", (1, HD)), ("k_norm_a", (1, HD)),
         ("q_norm_b", (1, HD)), ("k_norm_b", (1, HD)), ("rpb_b", (1, 4, 15, 31)), ("norm_ffn", (1, D)))


def _pack_small(vals):
    pieces = []
    for (name, shape), val in zip(SMALL, vals):
        flat = val.reshape(-1)
        pad = (-flat.shape[0]) % HD
        pieces.append(jnp.pad(flat, (0, pad)).reshape(-1, HD))
    packed = jnp.concatenate(pieces, axis=0)
    return jnp.pad(packed, ((0, SMALL_ROWS - packed.shape[0]), (0, 0)))


def _unpack_small(packed):
    out, row = [], 0
    for name, shape in SMALL:
        size = int(np.prod(shape))
        nrows = -(-size // HD)
        out.append(packed[row:row + nrows].reshape(-1)[:size].reshape(shape))
        row += nrows
    return out


def kernel(x, norm_mix, w_in, b_gate, q_norm_a, k_norm_a, q_norm_b, k_norm_b, rpb_b, w_proj_a, w_proj_b, w_out, norm_ffn, w_up, w_down, loss_target, m_norm_mix, m_w_in, m_b_gate, m_q_norm_a, m_k_norm_a, m_q_norm_b, m_k_norm_b, m_rpb_b, m_w_proj_a, m_w_proj_b, m_w_out, m_norm_ffn, m_w_up, m_w_down, v_norm_mix, v_w_in, v_b_gate, v_q_norm_a, v_k_norm_a, v_q_norm_b, v_k_norm_b, v_rpb_b, v_w_proj_a, v_w_proj_b, v_w_out, v_norm_ffn, v_w_up, v_w_down):
    big_names = ("w_in", "w_proj_a", "w_proj_b", "w_out", "w_up", "w_down")
    big_w = [a[0] for a in (w_in, w_proj_a, w_proj_b, w_out, w_up, w_down)]
    big_m = [a[0] for a in (m_w_in, m_w_proj_a, m_w_proj_b, m_w_out, m_w_up, m_w_down)]
    big_v = [a[0] for a in (v_w_in, v_w_proj_a, v_w_proj_b, v_w_out, v_w_up, v_w_down)]
    x2, target = x[0], loss_target[0]

    place = jnp.stack([2 * lax.axis_index("x") + lax.axis_index("y"), lax.axis_index("c")]).astype(jnp.int32)
    groups = ((0,), (1, 2, 3), (4,), (5,))
    started = []
    for j, grp in enumerate(groups):
        deps = (started[0][4],) if j else ()
        placed = [_cast_into_place(big_w[i], "cast_" + big_names[i], place, deps) for i in grp]
        started.append(_gather_start(f"gather_start_{j}", placed))

    def whole(fulls):
        return [f.reshape(NSH, 2 * f.shape[2], f.shape[3]) for f in fulls]

    def forward_begin(j, after):
        send, recv, _, fulls, _ = started[j]
        fulls = _gather_wait(f"gather_wait_{j}", send, recv, fulls, after)
        send, recv, _, fulls, token = _forward_start(f"forward_start_{j}", fulls)
        return (send, recv, fulls), token

    def forward_end(j, state, after):
        return whole(_forward_wait(f"forward_wait_{j}", *state, after))

    def as_halves(grads):
        return [g.reshape(NSH, 2, g.shape[1] // 2, g.shape[2]) for g in grads]

    def reduce_start(j, grads, theirs):
        parts = [_sum_halves(f"sum_halves_{j}_{i}", place, a, b) for i, (a, b) in enumerate(zip(grads, theirs))]
        send, recv, parts, lands, token = _reduce_start(f"reduce_start_{j}", parts)
        return (send, recv, parts, lands), token

    def exchange_begin(j, grads):
        send, recv, grads, lands, token = _exchange_start(f"exchange_start_{j}", as_halves(grads))
        return (send, recv, grads, lands), token

    def exchange_end(j, state, after):
        return reduce_start(j, *_exchange_wait(f"exchange_wait_{j}", *state, after))

    big_out = {}

    def share_begin(j, state, after):
        send, recv, parts, lands = state
        parts, lands = _reduce_wait(f"reduce_wait_{j}", send, recv, parts, lands, after)
        sums = [_sum_landed(f"sum_landed_{j}_{i}", place, p, l) for i, (p, l) in enumerate(zip(parts, lands))]
        send, recv, _, sums, token = _share_start(f"share_start_{j}", sums)
        return (send, recv, sums), token

    def share_end(j, state, after):
        for idx, g in zip(groups[j], _share_wait(f"share_wait_{j}", *state, after)):
            g = g.reshape(big_w[idx].shape)
            big_out[idx] = _adam_shard("adam_" + big_names[idx], g, big_w[idx], big_m[idx], big_v[idx])
        return big_out[groups[j][-1]][1]

    proj, xn = _norm_in_proj_own(x2, norm_mix, whole(started[0][3])[0], place)
    send, recv, _, win, _ = started[0]
    win = _gather_wait("gather_wait_0", send, recv, win, (proj, *[s[4] for s in started[1:]]))
    (win_f,) = whole(_gather_finish("gather_finish_0", win))
    proj = _in_proj_rest("in_proj_rest", xn, win_f, proj, place, (2, 1, 3))
    cos, sin = _rope_tables()
    nw = jnp.stack([jnp.concatenate([jnp.tile(q_norm_a, (1, NHA)), jnp.tile(q_norm_b, (1, NH - NHA))], axis=1),
                    jnp.concatenate([jnp.tile(k_norm_a, (1, NHA)), jnp.tile(k_norm_b, (1, NH - NHA))], axis=1)])
    qkn = _qk_prep(proj, nw, cos, sin)
    fw1, token = forward_begin(1, (qkn,))
    fwd_a = [_attn_a_fwd(qkn, proj, g) for g in range(3)]
    os, ls = [f[0] for f in fwd_a], [f[1] for f in fwd_a]
    fw2, token = forward_begin(2, (os[2], token))
    ob, lse_b, bias = _attn_b_fwd(qkn, proj, _rpb_rows(rpb_b[0]))
    oa, w0, w1, w2 = _comb_fwd(os, ls)
    ws = [w0, w1, w2]
    wpa_f, wpb_f, wout_f = forward_end(1, fw1, (oa, token))
    wout_f = wout_f.reshape(D, D)
    mixed, ob16 = _mix_fwd(oa, ob, proj, b_gate, wpa_f, wpb_f)
    h1, hn = _out_proj_fwd(mixed, wout_f, x2, norm_ffn)
    (wup_f,) = forward_end(2, fw2, (hn,))
    usq, u = _ffn_up(hn, wup_f)
    fw3, token = forward_begin(3, (u,))
    own = _ffn_down_own(usq, whole(fw3[2])[0].reshape(DFF, D), place)
    (wdown_f,) = forward_end(3, fw3, (own, token))
    wdown_f = wdown_f.reshape(DFF, D)
    dy, dy16, loss_parts = _ffn_down_loss(usq, wdown_f, h1, target, own, place)
    loss = lax.psum(jnp.sum(loss_parts[:, 0, 0]), ("x", "y", "c"))

    g_down = _grad_w("grad_w_down", usq, dy16, True, DFF // NSH, D, 1024, 1024)
    ex_down, token = exchange_begin(3, [g_down])
    du = _ffn_down_bwd(dy16, wdown_f, u, deps=(token,))
    g_up = _grad_w("grad_w_up", hn, du, False, D, DFF // NSH, 1024, 1024)
    red_down, token = exchange_end(3, ex_down, (g_up,))
    ex_up, token_up = exchange_begin(2, [g_up])
    dh1, dh16, d_norm_ffn = _ffn_up_bwd(du, wup_f, h1, dy, norm_ffn, deps=(token, token_up))
    dya, dyb, dga, dgb, doa, dob, dba, dbb = _mix_bwd(dh16, wout_f, oa, ob16, proj, b_gate, wpa_f, wpb_f)
    g_out = _grad_w("grad_w_out", mixed, dh16, True, D // NSH, D, 512, 1024)
    g_pa = _grad_w("grad_w_proj_a", oa, dya, False, 512, 512, 512, 512)
    g_pb = _grad_w("grad_w_proj_b", ob16, dyb, False, 512, 512, 512, 512)
    red_up, token = exchange_end(2, ex_up, (g_out,))
    ex_mid, token_mid = exchange_begin(1, [g_pa, g_pb, g_out])
    cc = _comb_bwd(doa, os, ws, deps=(token, token_mid))
    bwd_a = [_attn_a_bwd(qkn, proj, doa, ls[g], ws[g], cc, g) for g in range(3)]
    red_mid, token = exchange_end(1, ex_mid, (bwd_a[2][1],))
    dqk_b, dv_b, drpb_t = _attn_b_bwd(qkn, proj, dob, ob, lse_b, bias, deps=(token,))
    dproj, dn = _qk_bwd(proj, nw, cos, sin, [b[0] for b in bwd_a], dqk_b, [b[1] for b in bwd_a] + [dv_b], dga, dgb)
    g_in_theirs = _grad_w_in_half("grad_w_in_for_sibling", xn, dproj, place, True)
    send, recv, g_in_theirs, lands, token = _exchange_start("exchange_start_0", [g_in_theirs], sliced=False)
    g_in_mine = _grad_w_in_half("grad_w_in_own", xn, dproj, place, False, deps=(token,))
    _, theirs = _exchange_wait("exchange_wait_0", send, recv, g_in_theirs, lands, (g_in_mine,), sliced=False)
    red_in, token = reduce_start(0, [g_in_mine], theirs)
    grad_x, d_norm_mix = _in_proj_bwd(dproj, win_f, x2, dh1, norm_mix, deps=(token,))

    sh_down, token = share_begin(3, red_down, (grad_x,))
    sh_up, token = share_begin(2, red_up, (token,))
    done = share_end(3, sh_down, (token,))
    sh_mid, token = share_begin(1, red_mid, (done,))
    done = share_end(2, sh_up, (token,))
    sh_in, token = share_begin(0, red_in, (done,))
    done = share_end(1, sh_mid, (token,))
    done = share_end(0, sh_in, (done,))

    d_rpb = drpb_t[:, :15, GRID_W - WIN_C:GRID_W + WIN_C - 1]
    small_g = [d_norm_mix, jnp.concatenate([dba, dbb], axis=1), dn[0, 0], dn[1, 0], dn[0, 1], dn[1, 1], d_rpb, d_norm_ffn]
    gathered_small = _allgather_small(_pack_small(small_g), done)
    small_w = (norm_mix, b_gate, q_norm_a, k_norm_a, q_norm_b, k_norm_b, rpb_b, norm_ffn)
    small_m = (m_norm_mix, m_b_gate, m_q_norm_a, m_k_norm_a, m_q_norm_b, m_k_norm_b, m_rpb_b, m_norm_ffn)
    small_v = (v_norm_mix, v_b_gate, v_q_norm_a, v_k_norm_a, v_q_norm_b, v_k_norm_b, v_rpb_b, v_norm_ffn)
    small_out = [_unpack_small(p) for p in
                 _adam_small(gathered_small, _pack_small(small_w), _pack_small(small_m), _pack_small(small_v))]

    order = ("norm_mix", "w_in", "b_gate", "q_norm_a", "k_norm_a", "q_norm_b", "k_norm_b", "rpb_b",
             "w_proj_a", "w_proj_b", "w_out", "norm_ffn", "w_up", "w_down")
    small_idx = {name: i for i, (name, _) in enumerate(SMALL)}
    outs = []
    for kind in range(4):
        for name in order:
            if name in small_idx:
                outs.append(small_out[kind][small_idx[name]])
            else:
                outs.append(big_out[big_names.index(name)][kind][None])
    return (loss, grad_x[None], *outs)
```

```python
import functools

import numpy as np
import jax
import jax.numpy as jnp
from jax import lax
from jax.experimental import pallas as pl
from jax.experimental.pallas import tpu as pltpu

F32, BF16 = jnp.float32, jnp.bfloat16
SDS = jax.ShapeDtypeStruct
MESH = pl.DeviceIdType.MESH

T = 2048
D = 2048
HD = 128
NH, NHA = 16, 12
DIN = 10240
DFF = 8192
NSH = 4
DILS = (1, 4, 16)
EPS = 1e-6
NEG = -1e30
SCALE = HD ** -0.5
GRID_W, WIN_R, WIN_C = 64, 8, 16
VMEM_LIMIT = 56 * 1024 * 1024
B1, B2, LR, AEPS, WD, STEP = 0.9, 0.999, 0.001, 1e-08, 0.01, 10
SMALL_ROWS = 88


def _dot(a, b):
    return jnp.dot(a, b, preferred_element_type=F32)


def _dot_nt(a, b):
    return lax.dot_general(a, b, (((1,), (1,)), ((), ())), preferred_element_type=F32)


def _dot_tn(a, b):
    return lax.dot_general(a, b, (((0,), (0,)), ((), ())), preferred_element_type=F32)


def _params(n):
    return pltpu.CompilerParams(dimension_semantics=("arbitrary",) * n, vmem_limit_bytes=VMEM_LIMIT)


def _resident(shape, index_map):
    return pl.BlockSpec(shape, index_map, pipeline_mode=pl.Buffered(1))


def _sigmoid(z):
    return 1.0 / (1.0 + jnp.exp(-z))


def _wide(v, n):
    return jnp.concatenate([v] * n, axis=1)


def _row_tile(rows, cols, elems):
    tr = 16
    while tr * 2 <= rows and tr * 2 * cols <= elems:
        tr *= 2
    return tr


def _place():
    x, y, c = lax.axis_index("x"), lax.axis_index("y"), lax.axis_index("c")
    peers = [(1 - x, y), (x, 1 - y), (1 - x, 1 - y)]
    return x, y, c, peers


def _cast_into_place(w, name, place, deps=()):
    rows, cols = w.shape
    hr = rows // 2
    tr = min(hr, 256)
    per = hr // tr

    def body(*refs):
        w_ref, o_ref = refs[-2:]
        o_ref[...] = w_ref[...].astype(BF16)

    return pl.pallas_call(
        body, name=name, out_shape=SDS((NSH, 2, hr, cols), BF16),
        grid_spec=pltpu.PrefetchScalarGridSpec(
            num_scalar_prefetch=1, grid=(2, per),
            in_specs=[DEP_SPEC] * len(deps) + [pl.BlockSpec((tr, cols), lambda h, i, p: (h * per + i, 0))],
            out_specs=pl.BlockSpec((None, None, tr, cols), lambda h, i, p: (p[0], h, i, 0))),
        compiler_params=_params(2))(place, *deps, w)


ANY_SPEC = pl.BlockSpec(memory_space=pl.ANY)
HBM_SPEC = pl.BlockSpec(memory_space=pltpu.HBM)
SEM_SPEC = pl.BlockSpec(memory_space=pltpu.SEMAPHORE)
DEP_SPEC = pl.BlockSpec((8, 128), lambda *_: (0, 0))
EFFECT = pltpu.SideEffectType.DATAFLOW_SIDE_EFFECTING


def _after(body, deps):
    n = len(deps)
    return (lambda *refs: body(*refs[n:])) if n else body


SIBLING_BARRIER = 1


def _split_start(name, srcs, lands, n_copies, issue, sibling_only=False, after=()):
    n, m, d = len(srcs), len(lands), len(after)

    def body(*refs):
        if sibling_only:
            x, y, c, _ = _place()
            barrier = pltpu.get_barrier_semaphore()
            pl.semaphore_signal(barrier, inc=1, device_id=(x, y, 1 - c), device_id_type=MESH)
            pl.semaphore_wait(barrier, 1)
        issue(refs[:n], refs[n:n + m], refs[n + m + d], refs[n + m + d + 1])
        refs[-1][...] = jnp.zeros((8, 128), F32)

    arrays = list(srcs) + list(lands)
    outs = pl.pallas_call(
        body, name=name,
        out_shape=(pltpu.SemaphoreType.DMA((n_copies,)), pltpu.SemaphoreType.DMA((n_copies,)),
                   *[pltpu.HBM(a.shape, a.dtype) for a in arrays], SDS((8, 128), F32)),
        in_specs=[HBM_SPEC] * (n + m) + [ANY_SPEC] * d,
        out_specs=(SEM_SPEC, SEM_SPEC, *[HBM_SPEC] * (n + m), pl.BlockSpec(memory_space=pltpu.VMEM)),
        input_output_aliases={i: 2 + i for i in range(n + m)},
        compiler_params=pltpu.CompilerParams(has_side_effects=EFFECT,
                                             collective_id=SIBLING_BARRIER if sibling_only else None),
    )(*[pltpu.with_memory_space_constraint(a, pltpu.HBM) for a in arrays], *after)
    return outs[0], outs[1], list(outs[2:2 + n]), list(outs[2 + n:2 + n + m]), outs[-1]


def _split_wait(name, send_sems, recv_sems, srcs, lands, after, wait):
    n, m = len(srcs), len(lands)

    def body(*refs):
        wait(refs[:n], refs[n:n + m], refs[n + m], refs[n + m + 1])

    arrays = list(srcs) + list(lands)
    outs = pl.pallas_call(
        body, name=name, out_shape=[pltpu.HBM(a.shape, a.dtype) for a in arrays],
        in_specs=[HBM_SPEC] * (n + m) + [SEM_SPEC, SEM_SPEC] + [ANY_SPEC] * len(after),
        out_specs=[HBM_SPEC] * (n + m), input_output_aliases={i: i for i in range(n + m)},
        compiler_params=pltpu.CompilerParams(has_side_effects=EFFECT),
    )(*arrays, send_sems, recv_sems, *after)
    return list(outs[:n]), list(outs[n:])


def _gather_start(name, fulls, ks=(0, 1, 2), after=()):
    def issue(srcs, dsts, send_sems, recv_sems):
        x, y, c, peers = _place()
        for i in range(len(fulls)):
            mine = dsts[i].at[2 * x + y, c]
            for k in ks:
                px, py = peers[k]
                pltpu.make_async_remote_copy(
                    src_ref=mine, dst_ref=mine, send_sem=send_sems.at[3 * i + k],
                    recv_sem=recv_sems.at[3 * i + k], device_id=(px, py, c), device_id_type=MESH).start()

    return _split_start(name, [], fulls, 3 * len(fulls), issue, after=after)


def _gather_wait(name, send_sems, recv_sems, fulls, after, ks=(0, 1, 2)):
    def wait(srcs, dsts, send_sems, recv_sems):
        x, y, c, peers = _place()
        for i in range(len(fulls)):
            for k in ks:
                px, py = peers[k]
                cp = pltpu.make_async_remote_copy(
                    src_ref=dsts[i].at[2 * x + y, c], dst_ref=dsts[i].at[2 * px + py, c],
                    send_sem=send_sems.at[3 * i + k], recv_sem=recv_sems.at[3 * i + k],
                    device_id=(px, py, c), device_id_type=MESH)
                cp.wait_send()
                cp.wait_recv()

    return _split_wait(name, send_sems, recv_sems, [], fulls, after, wait)[1]


def _gather_finish(name, fulls, ks=(0, 1, 2)):
    n = len(fulls)

    def body(*refs):
        fin, fout = refs[:n], refs[n:2 * n]
        send_sems, recv_sems = refs[2 * n:]
        x, y, c, peers = _place()

        def copy(i, k, half):
            px, py = peers[k]
            return pltpu.make_async_remote_copy(
                src_ref=fin[i].at[2 * px + py, half], dst_ref=fout[i].at[2 * px + py, half],
                send_sem=send_sems.at[3 * i + k], recv_sem=recv_sems.at[3 * i + k],
                device_id=(x, y, 1 - c), device_id_type=MESH)

        sends = [copy(i, k, c) for i in range(n) for k in ks]
        for cp in sends:
            cp.start()
        for i in range(n):
            for k in ks:
                copy(i, k, 1 - c).wait_recv()
        for cp in sends:
            cp.wait_send()

    return pl.pallas_call(
        body, name=name, out_shape=[SDS(f.shape, f.dtype) for f in fulls],
        in_specs=[ANY_SPEC] * n, out_specs=[ANY_SPEC] * n, input_output_aliases={i: i for i in range(n)},
        scratch_shapes=[pltpu.SemaphoreType.DMA((3 * n,)), pltpu.SemaphoreType.DMA((3 * n,))])(*fulls)


def _reduce_start(name, parts):
    lands = [lax.empty((3,) + p.shape[1:], p.dtype) for p in parts]

    def issue(srcs, dsts, send_sems, recv_sems):
        x, y, c, peers = _place()
        for i in range(len(parts)):
            for k, (px, py) in enumerate(peers):
                pltpu.make_async_remote_copy(
                    src_ref=srcs[i].at[2 * px + py], dst_ref=dsts[i].at[k], send_sem=send_sems.at[3 * i + k],
                    recv_sem=recv_sems.at[3 * i + k], device_id=(px, py, c), device_id_type=MESH).start()

    return _split_start(name, parts, lands, 3 * len(parts), issue)


def _reduce_wait(name, send_sems, recv_sems, parts, lands, after):
    def wait(srcs, dsts, send_sems, recv_sems):
        x, y, c, peers = _place()
        for i in range(len(parts)):
            for k, (px, py) in enumerate(peers):
                cp = pltpu.make_async_remote_copy(
                    src_ref=srcs[i].at[2 * px + py], dst_ref=dsts[i].at[k], send_sem=send_sems.at[3 * i + k],
                    recv_sem=recv_sems.at[3 * i + k], device_id=(px, py, c), device_id_type=MESH)
                cp.wait_send()
                cp.wait_recv()

    return _split_wait(name, send_sems, recv_sems, parts, lands, after, wait)


def _sibling_copy(src, dst, send_sems, recv_sems, k):
    x, y, c, _ = _place()
    return pltpu.make_async_remote_copy(src_ref=src, dst_ref=dst, send_sem=send_sems.at[k], recv_sem=recv_sems.at[k],
                                        device_id=(x, y, 1 - c), device_id_type=MESH)


def _forward_start(name, fulls):
    def issue(srcs, dsts, send_sems, recv_sems):
        x, y, c, peers = _place()
        for i in range(len(fulls)):
            for k, (px, py) in enumerate(peers):
                part = dsts[i].at[2 * px + py, c]
                _sibling_copy(part, part, send_sems, recv_sems, 3 * i + k).start()

    return _split_start(name, [], fulls, 3 * len(fulls), issue, sibling_only=True)


def _forward_wait(name, send_sems, recv_sems, fulls, after):
    def wait(srcs, dsts, send_sems, recv_sems):
        x, y, c, peers = _place()
        for i in range(len(fulls)):
            for k, (px, py) in enumerate(peers):
                cp = _sibling_copy(dsts[i].at[2 * px + py, c], dsts[i].at[2 * px + py, 1 - c], send_sems, recv_sems, 3 * i + k)
                cp.wait_send()
                cp.wait_recv()

    return _split_wait(name, send_sems, recv_sems, [], fulls, after, wait)[1]


def _exchange_start(name, grads, sliced=True):
    lands = [lax.empty((NSH,) + g.shape[-2:], g.dtype) for g in grads]

    def issue(srcs, dsts, send_sems, recv_sems):
        c = lax.axis_index("c")
        for i in range(len(grads)):
            src = srcs[i].at[:, 1 - c] if sliced else srcs[i]
            _sibling_copy(src, dsts[i], send_sems, recv_sems, i).start()

    return _split_start(name, grads, lands, len(grads), issue, sibling_only=True)


def _exchange_wait(name, send_sems, recv_sems, grads, lands, after, sliced=True):
    def wait(srcs, dsts, send_sems, recv_sems):
        c = lax.axis_index("c")
        for i in range(len(grads)):
            cp = _sibling_copy(srcs[i].at[:, 1 - c] if sliced else srcs[i], dsts[i], send_sems, recv_sems, i)
            cp.wait_send()
            cp.wait_recv()

    return _split_wait(name, send_sems, recv_sems, grads, lands, after, wait)


def _share_start(name, sums):
    def issue(srcs, dsts, send_sems, recv_sems):
        c = lax.axis_index("c")
        for i in range(len(sums)):
            _sibling_copy(dsts[i].at[c], dsts[i].at[c], send_sems, recv_sems, i).start()

    return _split_start(name, [], sums, len(sums), issue, sibling_only=True)


def _share_wait(name, send_sems, recv_sems, sums, after):
    def wait(srcs, dsts, send_sems, recv_sems):
        c = lax.axis_index("c")
        for i in range(len(sums)):
            cp = _sibling_copy(dsts[i].at[c], dsts[i].at[1 - c], send_sems, recv_sems, i)
            cp.wait_send()
            cp.wait_recv()

    return _split_wait(name, send_sems, recv_sems, [], sums, after, wait)[1]


def _allgather_small(v, after):
    m_per, n = v.shape

    def body(x_ref, after_ref, out_ref, send_sems, recv_sems, local_sem):
        x, y, c = lax.axis_index("x"), lax.axis_index("y"), lax.axis_index("c")
        me, sibling = (x, y, c), (x, y, 1 - c)
        chips = [(1 - x, y), (x, 1 - y), (1 - x, 1 - y)]

        def rows(px, py, pc):
            return out_ref.at[pl.ds((4 * px + 2 * py + pc) * m_per, m_per), :]

        def copy(k, block, to, src=None):
            return pltpu.make_async_remote_copy(
                src_ref=rows(*block) if src is None else src, dst_ref=rows(*block),
                send_sem=send_sems.at[k], recv_sem=recv_sems.at[k], device_id=to, device_id_type=MESH)

        mine = pltpu.make_async_copy(x_ref, rows(*me), local_sem)
        mine.start()
        first = [copy(0, me, sibling, src=x_ref)]
        first += [copy(1 + j, me, (*chip, c), src=x_ref) for j, chip in enumerate(chips)]
        for cp in first:
            cp.start()
        passed = [copy(4 + j, (*chip, c), sibling) for j, chip in enumerate(chips)]
        for j, chip in enumerate(chips):
            copy(1 + j, (*chip, c), me).wait_recv()
            passed[j].start()
        copy(0, sibling, me).wait_recv()
        for j, chip in enumerate(chips):
            copy(4 + j, (*chip, 1 - c), me).wait_recv()
        for cp in first + passed:
            cp.wait_send()
        mine.wait()

    return pl.pallas_call(
        body, name="allgather_small", out_shape=SDS((8 * m_per, n), v.dtype),
        in_specs=[pl.BlockSpec(memory_space=pltpu.VMEM), ANY_SPEC], out_specs=pl.BlockSpec(memory_space=pltpu.VMEM),
        scratch_shapes=[pltpu.SemaphoreType.DMA((7,)), pltpu.SemaphoreType.DMA((7,)), pltpu.SemaphoreType.DMA])(v, after)


def _norm_in_proj_own(x, g, w_full, place):
    tn, chunk = 512, 256
    per = (DIN // NSH) // tn

    def body(place_ref, x_ref, g_ref, w_ref, proj_ref, xn_ref):
        @pl.when(pl.program_id(0) == 0)
        def _():
            def norm(r, carry):
                rows = pl.ds(pl.multiple_of(r * chunk, chunk), chunk)
                xv = x_ref[rows, :]
                rs = lax.rsqrt(jnp.mean(xv * xv, axis=-1, keepdims=True) + EPS)
                xn_ref[rows, :] = (xv * rs * g_ref[...]).astype(BF16)
                return carry

            lax.fori_loop(0, T // chunk, norm, 0)

        proj_ref[...] = _dot(xn_ref[...], w_ref[...])

    return pl.pallas_call(
        body, name="norm_in_proj_own", out_shape=[SDS((T, DIN), F32), SDS((T, D), BF16)],
        grid_spec=pltpu.PrefetchScalarGridSpec(
            num_scalar_prefetch=1, grid=(per,),
            in_specs=[_resident((T, D), lambda j, p: (0, 0)),
                      pl.BlockSpec((1, D), lambda j, p: (0, 0)),
                      pl.BlockSpec((None, D, tn), lambda j, p: (p[0], 0, j))],
            out_specs=[pl.BlockSpec((T, tn), lambda j, p: (0, p[0] * per + j)),
                       pl.BlockSpec((T, D), lambda j, p: (0, 0))]),
        compiler_params=_params(1))(place, x, g, w_full)


def _in_proj_rest(name, xn, w_full, proj, place, flips):
    tn = 512
    per = (DIN // NSH) // tn

    def body(place_ref, xn_ref, w_ref, proj_in, proj_ref):
        proj_ref[...] = _dot(xn_ref[...], w_ref[...])

    def shard(j, p):
        flip = flips[0]
        for n, f in enumerate(flips[1:]):
            flip = jnp.where(j // per == n + 1, f, flip)
        return p[0] ^ flip

    return pl.pallas_call(
        body, name=name, out_shape=SDS((T, DIN), F32),
        grid_spec=pltpu.PrefetchScalarGridSpec(
            num_scalar_prefetch=1, grid=(len(flips) * per,),
            in_specs=[_resident((T, D), lambda j, p: (0, 0)),
                      pl.BlockSpec((None, D, tn), lambda j, p: (shard(j, p), 0, j % per)), ANY_SPEC],
            out_specs=pl.BlockSpec((T, tn), lambda j, p: (0, shard(j, p) * per + j % per))),
        input_output_aliases={3: 0}, compiler_params=_params(1))(place, xn, w_full, proj)


def _rope_tables():
    pos = np.arange(T, dtype=np.float32)
    inv = (10000.0 ** (-np.arange(0, HD, 2, dtype=np.float32) / HD)).astype(np.float32)
    ang = (pos[:, None] * inv[None, :]).astype(np.float32)
    cos, sin = np.cos(ang).astype(np.float32), np.sin(ang).astype(np.float32)
    return (jnp.asarray(np.concatenate([cos, cos], axis=1)), jnp.asarray(np.concatenate([-sin, sin], axis=1)))


def _qk_prep(proj, nw, cos, sin):
    tm = 256

    def body(p_ref, w_ref, cos_ref, sin_ref, o_ref):
        cv, sv = cos_ref[...], sin_ref[...]
        for h in range(NH):
            sl = slice(h * HD, (h + 1) * HD)
            xv = p_ref[:, sl]
            r = lax.rsqrt(jnp.mean(xv * xv, axis=-1, keepdims=True) + EPS)
            z = xv * r * w_ref[:, sl]
            if h < NHA:
                z = z * cv + pltpu.roll(z, 64, 1) * sv
            o_ref[:, sl] = z.astype(BF16)

    return pl.pallas_call(
        body, name="qk_prep", out_shape=SDS((T, 2 * D), BF16), grid=(T // tm, 2),
        in_specs=[pl.BlockSpec((tm, D), lambda i, j: (i, j)),
                  pl.BlockSpec((None, 1, D), lambda i, j: (j, 0, 0)),
                  pl.BlockSpec((tm, HD), lambda i, j: (i, 0)),
                  pl.BlockSpec((tm, HD), lambda i, j: (i, 0))],
        out_specs=pl.BlockSpec((tm, D), lambda i, j: (i, j)),
        compiler_params=_params(2))(proj, nw, cos, sin)


def _band_mask(q0, m):
    ii = lax.broadcasted_iota(jnp.int32, (128, 256), 0)
    jj = lax.broadcasted_iota(jnp.int32, (128, 256), 1)
    rel = jj - ii
    kpos = jj + (q0 - 64)
    return (rel >= 0) & (rel <= 128) & (kpos >= 0) & (kpos < m)


def _fill_padded(dst, src, m):
    zeros = jnp.zeros((64, HD), dst.dtype)
    dst[0:64, :] = zeros
    dst[64 + m:128 + m, :] = zeros
    dst[64:64 + m, :] = src.astype(dst.dtype)


def _residue_rows(r, m, dil):
    return pl.ds(r, m, stride=dil) if dil > 1 else slice(None)


def _head_blocks(g):
    col = lambda base: pl.BlockSpec((T, HD), lambda h: (0, base + g * 4 + h))
    return col(0), col(NH), col(2 * NH), pl.BlockSpec((T, HD), lambda h: (0, h))


def _attn_a_fwd(qkn, proj, g):
    dil = DILS[g]
    m = T // dil
    nb = m // 128

    def body(q_ref, k_ref, v_ref, o_ref, l_ref, qf, kf, qp, kp, vp, ob, lb):
        qf[...] = q_ref[...].astype(F32)
        kf[...] = k_ref[...].astype(F32)
        for r in range(dil):
            rows = _residue_rows(r, m, dil)
            qp[...] = qf[rows, :].astype(BF16)
            _fill_padded(kp, kf[rows, :], m)
            _fill_padded(vp, v_ref[rows, :], m)

            def block(b, carry):
                q0 = pl.multiple_of(b * 128, 128)
                kw, vw = kp[pl.ds(q0, 256), :], vp[pl.ds(q0, 256), :]
                s = _dot_nt(qp[pl.ds(q0, 128), :], kw) * SCALE
                s = jnp.where(_band_mask(q0, m), s, NEG)
                mx = jnp.max(s, axis=-1, keepdims=True)
                e = jnp.exp(s - mx)
                den = jnp.sum(e, axis=-1, keepdims=True)
                ob[pl.ds(q0, 128), :] = _dot((e / den).astype(BF16), vw)
                lb[pl.ds(q0, 128), :] = jnp.broadcast_to(mx + jnp.log(den), (128, HD))
                return carry

            lax.fori_loop(0, nb, block, 0, unroll=min(nb, 16))
            o_ref[rows, :] = ob[...]
            l_ref[rows, :] = lb[...]

    q_blk, k_blk, v_blk, out_blk = _head_blocks(g)
    return pl.pallas_call(
        body, name=f"attn_a_fwd_{g}", out_shape=[SDS((T, 512), F32)] * 2, grid=(4,),
        in_specs=[q_blk, k_blk, v_blk], out_specs=[out_blk] * 2,
        scratch_shapes=[pltpu.VMEM((T, HD), F32), pltpu.VMEM((T, HD), F32), pltpu.VMEM((m, HD), BF16),
                        pltpu.VMEM((m + 128, HD), BF16), pltpu.VMEM((m + 128, HD), BF16),
                        pltpu.VMEM((m, HD), F32), pltpu.VMEM((m, HD), F32)],
        compiler_params=_params(1))(qkn, qkn, proj)


def _nbr_window(r):
    start = jnp.clip(r - WIN_R // 2, 0, T // GRID_W - WIN_R)
    return start, start - r + (WIN_R - 1)


def _rpb_rows(rpb):
    zeros = jnp.zeros((4, 14, 33), F32)
    a, b = rpb[:, :14], rpb[:, 1:15]
    rows = jnp.concatenate([a[:, :, 15:31], zeros, b, zeros, a[:, :, 0:15]], axis=2)
    return jnp.pad(rows, ((0, 0), (0, 2), (0, 0)))


def _attn_b_fwd(qkn, proj, rpb_rows):
    def body(r_ref, q_ref, k_ref, v_ref, o_ref, l_ref, bias_ref, vb, pair):
        qc = lax.broadcasted_iota(jnp.int32, (GRID_W, 512), 0)
        kc = lax.broadcasted_iota(jnp.int32, (GRID_W, 512), 1) & (GRID_W - 1)
        cs = jnp.clip(qc - WIN_C // 2, 0, GRID_W - WIN_C)
        colmask = (kc >= cs) & (kc < cs + WIN_C)
        for d in range(14):
            pair[d] = pltpu.roll(jnp.broadcast_to(r_ref[d:d + 1, :], (GRID_W, HD)), 0, 1, stride=1, stride_axis=0)
        for off in range(8):
            rows = jnp.concatenate([pair[off + 2 * jj] for jj in range(4)], axis=1)
            bias_ref[off] = jnp.where(colmask, rows, NEG)
        vb[...] = v_ref[...].astype(BF16)

        def row(r, carry):
            start, off = _nbr_window(r)
            q0 = pl.multiple_of(r * GRID_W, GRID_W)
            k0 = pl.multiple_of(start * GRID_W, GRID_W)
            s = _dot_nt(q_ref[pl.ds(q0, GRID_W), :], k_ref[pl.ds(k0, 512), :]) * SCALE + bias_ref[off]
            mx = jnp.max(s, axis=-1, keepdims=True)
            e = jnp.exp(s - mx)
            den = jnp.sum(e, axis=-1, keepdims=True)
            o_ref[pl.ds(q0, GRID_W), :] = _dot((e / den).astype(BF16), vb[pl.ds(k0, 512), :])
            l_ref[pl.ds(q0, GRID_W), :] = jnp.broadcast_to(mx + jnp.log(den), (GRID_W, HD))
            return carry

        lax.fori_loop(0, T // GRID_W, row, 0, unroll=16)

    return pl.pallas_call(
        body, name="attn_b_fwd",
        out_shape=[SDS((T, 512), F32), SDS((T, 512), F32), SDS((4, 8, GRID_W, 512), F32)], grid=(4,),
        in_specs=[pl.BlockSpec((None, 16, HD), lambda h: (h, 0, 0)),
                  pl.BlockSpec((T, HD), lambda h: (0, NHA + h)),
                  pl.BlockSpec((T, HD), lambda h: (0, NH + NHA + h)),
                  pl.BlockSpec((T, HD), lambda h: (0, 2 * NH + NHA + h))],
        out_specs=[pl.BlockSpec((T, HD), lambda h: (0, h)), pl.BlockSpec((T, HD), lambda h: (0, h)),
                   pl.BlockSpec((None, 8, GRID_W, 512), lambda h: (h, 0, 0, 0))],
        scratch_shapes=[pltpu.VMEM((T, HD), BF16), pltpu.VMEM((14, GRID_W, HD), F32)],
        compiler_params=_params(1))(rpb_rows, qkn, qkn, proj)


def _comb_fwd(os, ls):
    tm = 512

    def body(o0, o1, o2, l0, l1, l2, oa_ref, w0, w1, w2):
        lv = [l0[...], l1[...], l2[...]]
        mx = jnp.maximum(jnp.maximum(lv[0], lv[1]), lv[2])
        ev = [jnp.exp(l - mx) for l in lv]
        den = ev[0] + ev[1] + ev[2]
        wv = [e / den for e in ev]
        oa_ref[...] = (wv[0] * o0[...] + wv[1] * o1[...] + wv[2] * o2[...]).astype(BF16)
        w0[...], w1[...], w2[...] = wv

    spec = pl.BlockSpec((tm, 512), lambda i: (i, 0))
    return pl.pallas_call(
        body, name="comb_fwd", out_shape=[SDS((T, 512), BF16)] + [SDS((T, 512), F32)] * 3, grid=(T // tm,),
        in_specs=[spec] * 6, out_specs=[spec] * 4, compiler_params=_params(1))(*os, *ls)


def _mix_fwd(oa, ob, proj, b_gate, wpa, wpb):
    tm = 512

    def body(oa_ref, ob_ref, ga_ref, gb_ref, ba_ref, bb_ref, wpa_ref, wpb_ref, mixed_ref, ob16_ref):
        oav = oa_ref[...]
        obv = ob_ref[...].astype(BF16)
        ob16_ref[...] = obv
        for s in range(NSH):
            sl = slice(s * 512, (s + 1) * 512)
            ga = _sigmoid(ga_ref[:, sl] + ba_ref[:, sl])
            gb = _sigmoid(gb_ref[:, sl] + bb_ref[:, sl])
            mixed_ref[:, sl] = (ga * _dot(oav, wpa_ref[s]) + gb * _dot(obv, wpb_ref[s])).astype(BF16)

    row = lambda w: pl.BlockSpec((tm, w), lambda i: (i, 0))
    return pl.pallas_call(
        body, name="mix_fwd", out_shape=[SDS((T, D), BF16), SDS((T, 512), BF16)], grid=(T // tm,),
        in_specs=[row(512), row(512),
                  pl.BlockSpec((tm, D), lambda i: (i, 3)), pl.BlockSpec((tm, D), lambda i: (i, 4)),
                  pl.BlockSpec((1, D), lambda i: (0, 0)), pl.BlockSpec((1, D), lambda i: (0, 1)),
                  _resident((NSH, 512, 512), lambda i: (0, 0, 0)), _resident((NSH, 512, 512), lambda i: (0, 0, 0))],
        out_specs=[row(D), row(512)], compiler_params=_params(1))(oa, ob, proj, proj, b_gate, b_gate, wpa, wpb)


def _out_proj_fwd(mixed, w_out, x, g):
    tm = 512

    def body(m_ref, w_ref, x_ref, g_ref, h1_ref, hn_ref):
        h1 = x_ref[...] + _dot(m_ref[...], w_ref[...])
        h1_ref[...] = h1
        r = lax.rsqrt(jnp.mean(h1 * h1, axis=-1, keepdims=True) + EPS)
        hn_ref[...] = (h1 * r * g_ref[...]).astype(BF16)

    row = pl.BlockSpec((tm, D), lambda i: (i, 0))
    return pl.pallas_call(
        body, name="out_proj_fwd", out_shape=[SDS((T, D), F32), SDS((T, D), BF16)], grid=(T // tm,),
        in_specs=[row, _resident((D, D), lambda i: (0, 0)), row, pl.BlockSpec((1, D), lambda i: (0, 0))],
        out_specs=[row, row], compiler_params=_params(1))(mixed, w_out, x, g)


def _ffn_up(hn, w_up):
    tm, tn = T, 512
    per = (DFF // NSH) // tn

    def body(h_ref, w_ref, a_ref, u_ref):
        uv = jnp.maximum(_dot(h_ref[...], w_ref[...]), 0.0)
        a_ref[...] = (uv * uv).astype(BF16)
        u_ref[...] = uv.astype(BF16)

    out = pl.BlockSpec((tm, tn), lambda i, j: (i, j))
    return pl.pallas_call(
        body, name="ffn_up", out_shape=[SDS((T, DFF), BF16)] * 2, grid=(T // tm, DFF // tn),
        in_specs=[pl.BlockSpec((tm, D), lambda i, j: (i, 0)),
                  pl.BlockSpec((None, D, tn), lambda i, j: (j // per, 0, j % per))],
        out_specs=[out, out], compiler_params=_params(2))(hn, w_up)


def _ffn_down_own(u, w_down, place):
    tm, tk = 512, DFF // NSH

    def body(place_ref, u_ref, w_ref, o_ref):
        o_ref[...] = _dot(u_ref[...], w_ref[...])

    return pl.pallas_call(
        body, name="ffn_down_own", out_shape=SDS((T, D), F32),
        grid_spec=pltpu.PrefetchScalarGridSpec(
            num_scalar_prefetch=1, grid=(T // tm,),
            in_specs=[pl.BlockSpec((tm, tk), lambda i, p: (i, p[0])), pl.BlockSpec((tk, D), lambda i, p: (p[0], 0))],
            out_specs=pl.BlockSpec((tm, D), lambda i, p: (i, 0))),
        compiler_params=_params(1))(place, u, w_down)


def _ffn_down_loss(u, w_down, h1, target, own, place):
    tm, tk = 512, DFF // NSH
    nk = NSH - 1

    def body(place_ref, u_ref, w_ref, h1_ref, t_ref, own_ref, dy_ref, dy16_ref, loss_ref, acc):
        k = pl.program_id(1)

        @pl.when(k == 0)
        def _():
            acc[...] = own_ref[...]

        acc[...] += _dot(u_ref[...], w_ref[...])

        @pl.when(k == nk - 1)
        def _():
            def chunk(r, sq):
                rows = pl.ds(pl.multiple_of(r * 16, 16), 16)
                err = acc[rows, :] + h1_ref[rows, :] - t_ref[rows, :]
                dy = err * (1.0 / D)
                dy_ref[rows, :] = dy
                dy16_ref[rows, :] = dy.astype(BF16)
                return sq + err * err

            sq = lax.fori_loop(0, tm // 16, chunk, jnp.zeros((16, D), F32), unroll=2)
            part = 0.5 * jnp.sum(jnp.mean(sq, axis=-1, keepdims=True), axis=0, keepdims=True)
            loss_ref[...] = jnp.broadcast_to(part, (8, 128))

    row = pl.BlockSpec((tm, D), lambda i, k, p: (i, 0))
    once = _resident((tm, D), lambda i, k, p: (i, 0))
    shard = lambda k, p: p[0] ^ (k + 1)
    return pl.pallas_call(
        body, name="ffn_down_loss",
        out_shape=[SDS((T, D), F32), SDS((T, D), BF16), SDS((T // tm, 8, 128), F32)],
        grid_spec=pltpu.PrefetchScalarGridSpec(
            num_scalar_prefetch=1, grid=(T // tm, nk),
            in_specs=[pl.BlockSpec((tm, tk), lambda i, k, p: (i, shard(k, p))),
                      pl.BlockSpec((tk, D), lambda i, k, p: (shard(k, p), 0)), once, once, once],
            out_specs=[row, row, pl.BlockSpec((None, 8, 128), lambda i, k, p: (i, 0, 0))],
            scratch_shapes=[pltpu.VMEM((tm, D), F32)]),
        compiler_params=_params(2))(place, u, w_down, h1, target, own)


def _ffn_down_bwd(dy16, w_down, u, deps=()):
    tm, tn = T, 512

    def body(dy_ref, w_ref, u_ref, du_ref):
        uv = u_ref[...].astype(F32)
        du_ref[...] = jnp.where(uv > 0.0, 2.0 * uv * _dot_nt(dy_ref[...], w_ref[...]), 0.0).astype(BF16)

    return pl.pallas_call(
        _after(body, deps), name="ffn_down_bwd", out_shape=SDS((T, DFF), BF16), grid=(T // tm, DFF // tn),
        in_specs=[DEP_SPEC] * len(deps) + [
            pl.BlockSpec((tm, D), lambda i, j: (i, 0)), pl.BlockSpec((tn, D), lambda i, j: (j, 0)),
            pl.BlockSpec((tm, tn), lambda i, j: (i, j))],
        out_specs=pl.BlockSpec((tm, tn), lambda i, j: (i, j)), compiler_params=_params(2))(*deps, dy16, w_down, u)


def _norm_bwd(xv, dz_in, g):
    r = lax.rsqrt(jnp.mean(xv * xv, axis=-1, keepdims=True) + EPS)
    dg = jnp.sum(xv * r * dz_in, axis=0, keepdims=True)
    dz = dz_in * g
    dx = r * dz - xv * (r * r * r) * jnp.mean(xv * dz, axis=-1, keepdims=True)
    return dx, dg


def _ffn_up_bwd(du, w_up, h1, dy, g, deps=()):
    tm, tk = 512, 1024
    per = (DFF // NSH) // tk
    nk = DFF // tk

    def body(du_ref, w_ref, h1_ref, dy_ref, g_ref, dh1_ref, dh16_ref, dg_ref, acc):
        i, k = pl.program_id(0), pl.program_id(1)

        @pl.when(k == 0)
        def _():
            acc[...] = jnp.zeros_like(acc)

        @pl.when((k == 0) & (i == 0))
        def _():
            dg_ref[...] = jnp.zeros_like(dg_ref)

        acc[...] += _dot_nt(du_ref[...], w_ref[...])

        @pl.when(k == nk - 1)
        def _():
            dx, dg = _norm_bwd(h1_ref[...], acc[...], g_ref[...])
            dh1 = dy_ref[...] + dx
            dh1_ref[...] = dh1
            dh16_ref[...] = dh1.astype(BF16)
            dg_ref[...] += dg

    row = pl.BlockSpec((tm, D), lambda i, k: (i, 0))
    vec = pl.BlockSpec((1, D), lambda i, k: (0, 0))
    return pl.pallas_call(
        _after(body, deps), name="ffn_up_bwd", out_shape=[SDS((T, D), F32), SDS((T, D), BF16), SDS((1, D), F32)],
        grid=(T // tm, nk),
        in_specs=[DEP_SPEC] * len(deps) + [
            pl.BlockSpec((tm, tk), lambda i, k: (i, k)),
            pl.BlockSpec((None, D, tk), lambda i, k: (k // per, 0, k % per)), row, row, vec],
        out_specs=[row, row, vec], scratch_shapes=[pltpu.VMEM((tm, D), F32)],
        compiler_params=_params(2))(*deps, du, w_up, h1, dy, g)


def _mix_bwd(dh16, w_out, oa, ob16, proj, b_gate, wpa, wpb):
    tm = 256

    def body(dh_ref, wo_ref, oa_ref, ob_ref, ga_ref, gb_ref, ba_ref, bb_ref, wpa_ref, wpb_ref,
             dya_ref, dyb_ref, dga_ref, dgb_ref, doa_ref, dob_ref, dba_ref, dbb_ref):
        @pl.when(pl.program_id(0) == 0)
        def _():
            dba_ref[...] = jnp.zeros_like(dba_ref)
            dbb_ref[...] = jnp.zeros_like(dbb_ref)

        oav, obv = oa_ref[...], ob_ref[...]
        doa = jnp.zeros((tm, 512), F32)
        dob = jnp.zeros((tm, 512), F32)
        for s in range(NSH):
            sl = slice(s * 512, (s + 1) * 512)
            dm = _dot_nt(dh_ref[...], wo_ref[sl, :])
            ga = _sigmoid(ga_ref[:, sl] + ba_ref[:, sl])
            gb = _sigmoid(gb_ref[:, sl] + bb_ref[:, sl])
            dya = (dm * ga).astype(BF16)
            dyb = (dm * gb).astype(BF16)
            dza = dm * _dot(oav, wpa_ref[s]) * ga * (1.0 - ga)
            dzb = dm * _dot(obv, wpb_ref[s]) * gb * (1.0 - gb)
            dya_ref[:, sl], dyb_ref[:, sl] = dya, dyb
            dga_ref[:, sl], dgb_ref[:, sl] = dza.astype(BF16), dzb.astype(BF16)
            dba_ref[:, sl] += jnp.sum(dza, axis=0, keepdims=True)
            dbb_ref[:, sl] += jnp.sum(dzb, axis=0, keepdims=True)
            doa += _dot_nt(dya, wpa_ref[s])
            dob += _dot_nt(dyb, wpb_ref[s])
        doa_ref[...], dob_ref[...] = doa, dob

    row = lambda w: pl.BlockSpec((tm, w), lambda i: (i, 0))
    vec = pl.BlockSpec((1, D), lambda i: (0, 0))
    wp = _resident((NSH, 512, 512), lambda i: (0, 0, 0))
    return pl.pallas_call(
        body, name="mix_bwd",
        out_shape=[SDS((T, D), BF16)] * 4 + [SDS((T, 512), F32)] * 2 + [SDS((1, D), F32)] * 2, grid=(T // tm,),
        in_specs=[row(D), _resident((D, D), lambda i: (0, 0)), row(512), row(512),
                  pl.BlockSpec((tm, D), lambda i: (i, 3)), pl.BlockSpec((tm, D), lambda i: (i, 4)),
                  pl.BlockSpec((1, D), lambda i: (0, 0)), pl.BlockSpec((1, D), lambda i: (0, 1)), wp, wp],
        out_specs=[row(D)] * 4 + [row(512)] * 2 + [vec] * 2,
        compiler_params=_params(1))(dh16, w_out, oa, ob16, proj, proj, b_gate, b_gate, wpa, wpb)


def _comb_bwd(doa, os, ws, deps=()):
    tm = 512

    def body(d_ref, o0, o1, o2, w0, w1, w2, cc_ref):
        prod = d_ref[...] * (w0[...] * o0[...] + w1[...] * o1[...] + w2[...] * o2[...])
        for h in range(4):
            sl = slice(h * HD, (h + 1) * HD)
            cc_ref[:, sl] = jnp.broadcast_to(jnp.sum(prod[:, sl], axis=-1, keepdims=True), (tm, HD))

    spec = pl.BlockSpec((tm, 512), lambda i: (i, 0))
    return pl.pallas_call(
        _after(body, deps), name="comb_bwd", out_shape=SDS((T, 512), F32), grid=(T // tm,),
        in_specs=[DEP_SPEC] * len(deps) + [spec] * 7, out_specs=spec,
        compiler_params=_params(1))(*deps, doa, *os, *ws)


def _attn_a_bwd(qkn, proj, doa, lse, w, cc, g):
    dil = DILS[g]
    m = T // dil
    nb = m // 128

    def body(q_ref, k_ref, v_ref, d_ref, l_ref, w_ref, c_ref, dqk_ref, dv_ref,
             qf, kf, qp, kp, vp, dp, lp, wsub, cp, dqb, dkp, dvp):
        qf[...] = q_ref[...].astype(F32)
        kf[...] = k_ref[...].astype(F32)
        for r in range(dil):
            sub = _residue_rows(r, m, dil)
            qp[...] = qf[sub, :].astype(BF16)
            _fill_padded(kp, kf[sub, :], m)
            _fill_padded(vp, v_ref[sub, :], m)
            dp[...] = d_ref[sub, :].astype(BF16)
            lp[...], wsub[...], cp[...] = l_ref[sub, :], w_ref[sub, :], c_ref[sub, :]
            dkp[...] = jnp.zeros_like(dkp)
            dvp[...] = jnp.zeros_like(dvp)

            def block(b, carry):
                q0 = pl.multiple_of(b * 128, 128)
                rows = pl.ds(q0, 128)
                win = pl.ds(q0, 256)
                qb, kw, vw = qp[rows, :], kp[win, :], vp[win, :]
                s = _dot_nt(qb, kw) * SCALE
                s = jnp.where(_band_mask(q0, m), s, NEG)
                wp = _wide(wsub[rows, :], 2) * jnp.exp(s - _wide(lp[rows, :], 2))
                dob = dp[rows, :]
                ds = (wp * (_dot_nt(dob, vw) - _wide(cp[rows, :], 2))).astype(BF16)
                dqb[rows, :] = _dot(ds, kw) * SCALE
                dkp[win, :] += _dot_tn(ds, qb) * SCALE
                dvp[win, :] += _dot_tn(wp.astype(BF16), dob)
                return carry

            lax.fori_loop(0, nb, block, 0, unroll=min(nb, 16))
            dqk_ref.at[0][sub, :] = dqb[...]
            dqk_ref.at[1][sub, :] = dkp[64:64 + m, :]
            dv_ref[sub, :] = dvp[64:64 + m, :]

    q_blk, k_blk, v_blk, blk = _head_blocks(g)
    sub16 = pltpu.VMEM((m, HD), BF16)
    sub32 = pltpu.VMEM((m, HD), F32)
    return pl.pallas_call(
        body, name=f"attn_a_bwd_{g}", out_shape=[SDS((2, T, 512), F32), SDS((T, 512), F32)], grid=(4,),
        in_specs=[q_blk, k_blk, v_blk, blk, blk, blk, blk],
        out_specs=[pl.BlockSpec((2, T, HD), lambda h: (0, 0, h)), blk],
        scratch_shapes=[pltpu.VMEM((T, HD), F32), pltpu.VMEM((T, HD), F32), sub16,
                        pltpu.VMEM((m + 128, HD), BF16), pltpu.VMEM((m + 128, HD), BF16), sub16,
                        sub32, sub32, sub32, sub32,
                        pltpu.VMEM((m + 128, HD), F32), pltpu.VMEM((m + 128, HD), F32)],
        compiler_params=_params(1))(qkn, qkn, proj, doa, lse, w, cc)


def _attn_b_bwd(qkn, proj, dob, ob, lse, bias, deps=()):
    def body(q_ref, k_ref, v_ref, d_ref, o_ref, l_ref, bias_ref, dqk_ref, dv_ref, drpb_ref, vb, dk_acc, dv_acc, a_acc):
        vb[...] = v_ref[...].astype(BF16)
        dk_acc[...] = jnp.zeros_like(dk_acc)
        dv_acc[...] = jnp.zeros_like(dv_acc)
        a_acc[...] = jnp.zeros_like(a_acc)

        def row(r, carry):
            start, off = _nbr_window(r)
            rows = pl.ds(pl.multiple_of(r * GRID_W, GRID_W), GRID_W)
            win = pl.ds(pl.multiple_of(start * GRID_W, GRID_W), 512)
            qr, kw, vw = q_ref[rows, :], k_ref[win, :], vb[win, :]
            s = _dot_nt(qr, kw) * SCALE + bias_ref[off]
            p = jnp.exp(s - _wide(l_ref[rows, :], 4))
            dov = d_ref[rows, :]
            delta = jnp.sum(dov * o_ref[rows, :], axis=-1, keepdims=True)
            do16 = dov.astype(BF16)
            ds = p * (_dot_nt(do16, vw) - delta)
            a_acc[off] += ds
            ds16 = ds.astype(BF16)
            dqk_ref[0, rows, :] = _dot(ds16, kw) * SCALE
            dk_acc[win, :] += _dot_tn(ds16, qr) * SCALE
            dv_acc[win, :] += _dot_tn(p.astype(BF16), do16)
            return carry

        lax.fori_loop(0, T // GRID_W, row, 0, unroll=16)
        dqk_ref[1] = dk_acc[...]
        dv_ref[...] = dv_acc[...]

        lane = lax.broadcasted_iota(jnp.int32, (16, HD), 1)
        rowi = lax.broadcasted_iota(jnp.int32, (16, HD), 0)
        low = (lane >= GRID_W - WIN_C) & (lane < GRID_W + WIN_C - 1)
        high = (lane >= HD - WIN_C) | (lane < WIN_C - 1)
        flip = (lax.broadcasted_iota(jnp.int32, (GRID_W, GRID_W), 0)
                + lax.broadcasted_iota(jnp.int32, (GRID_W, GRID_W), 1) == GRID_W - 1).astype(BF16)
        out = jnp.zeros((16, HD), F32)
        for d in range(14):
            acc = None
            for off in range(8):
                if 0 <= d - off <= 6 and (d - off) % 2 == 0:
                    jj = (d - off) // 2
                    piece = a_acc[off, :, jj * HD:(jj + 1) * HD]
                    acc = piece if acc is None else acc + piece
            hi = acc.astype(BF16)
            lo = (acc - hi.astype(F32)).astype(BF16)
            rev = _dot(flip, hi) + _dot(flip, lo)
            v = jnp.sum(pltpu.roll(rev, 0, 1, stride=1, stride_axis=0), axis=0, keepdims=True)
            v = jnp.broadcast_to(v, (16, HD))
            out = out + jnp.where((rowi == d) & low, v, 0.0)
            out = out + jnp.where(rowi == d + 1, pltpu.roll(jnp.where(high, v, 0.0), GRID_W, 1), 0.0)
        drpb_ref[...] = out

    blk = pl.BlockSpec((T, HD), lambda h: (0, h))
    return pl.pallas_call(
        _after(body, deps), name="attn_b_bwd",
        out_shape=[SDS((2, T, 512), F32), SDS((T, 512), F32), SDS((4, 16, HD), F32)], grid=(4,),
        in_specs=[DEP_SPEC] * len(deps) + [
            pl.BlockSpec((T, HD), lambda h: (0, NHA + h)),
            pl.BlockSpec((T, HD), lambda h: (0, NH + NHA + h)),
            pl.BlockSpec((T, HD), lambda h: (0, 2 * NH + NHA + h)), blk, blk, blk,
            pl.BlockSpec((None, 8, GRID_W, 512), lambda h: (h, 0, 0, 0))],
        out_specs=[pl.BlockSpec((2, T, HD), lambda h: (0, 0, h)), blk,
                   pl.BlockSpec((None, 16, HD), lambda h: (h, 0, 0))],
        scratch_shapes=[pltpu.VMEM((T, HD), BF16), pltpu.VMEM((T, HD), F32), pltpu.VMEM((T, HD), F32),
                        pltpu.VMEM((8, GRID_W, 512), F32)],
        compiler_params=_params(1))(*deps, qkn, qkn, proj, dob, ob, lse, bias)


def _qk_bwd(proj, nw, cos, sin, dqk_groups, dqk_b, dvs, dga, dgb):
    tm = 512

    def body(p_ref, w_ref, cos_ref, sin_ref, d0, d1, d2, d3, v0, v1, v2, v3, ga_ref, gb_ref, o_ref, dn_ref):
        j, i = pl.program_id(0), pl.program_id(1)

        @pl.when((j < 2) & (i == 0))
        def _():
            dn_ref[...] = jnp.zeros_like(dn_ref)

        @pl.when(j < 2)
        def _():
            cv, sv = cos_ref[...], sin_ref[...]
            srcs = (d0, d1, d2, d3)
            dna = jnp.zeros((1, HD), F32)
            dnb = jnp.zeros((1, HD), F32)
            for h in range(NH):
                sl = slice(h * HD, (h + 1) * HD)
                dz = srcs[h // 4][:, (h % 4) * HD:(h % 4 + 1) * HD]
                if h < NHA:
                    dz = dz * cv + pltpu.roll(dz * sv, 64, 1)
                dx, dg = _norm_bwd(p_ref[:, sl], dz, w_ref[:, sl])
                o_ref[:, sl] = dx.astype(BF16)
                if h < NHA:
                    dna += dg
                else:
                    dnb += dg
            dn_ref[0:1, :] += dna
            dn_ref[1:2, :] += dnb

        @pl.when(j == 2)
        def _():
            for s, v_ref in enumerate((v0, v1, v2, v3)):
                o_ref[:, s * 512:(s + 1) * 512] = v_ref[...].astype(BF16)

        @pl.when(j == 3)
        def _():
            o_ref[...] = ga_ref[...]

        @pl.when(j == 4)
        def _():
            o_ref[...] = gb_ref[...]

    def rows(used):
        return lambda j, i: (jnp.where(used(j), i, 0), 0)

    qk = lambda j: j < 2
    dspec = pl.BlockSpec((None, tm, 512), lambda j, i: (jnp.minimum(j, 1), jnp.where(j < 2, i, 0), 0))
    vspec = pl.BlockSpec((tm, 512), rows(lambda j: j == 2))
    return pl.pallas_call(
        body, name="qk_bwd", out_shape=[SDS((T, DIN), BF16), SDS((2, 8, HD), F32)], grid=(5, T // tm),
        in_specs=[pl.BlockSpec((tm, D), lambda j, i: (jnp.where(j < 2, i, 0), jnp.minimum(j, 1))),
                  pl.BlockSpec((None, 1, D), lambda j, i: (jnp.minimum(j, 1), 0, 0)),
                  pl.BlockSpec((tm, HD), rows(qk)), pl.BlockSpec((tm, HD), rows(qk)),
                  dspec, dspec, dspec, dspec, vspec, vspec, vspec, vspec,
                  pl.BlockSpec((tm, D), rows(lambda j: j == 3)), pl.BlockSpec((tm, D), rows(lambda j: j == 4))],
        out_specs=[pl.BlockSpec((tm, D), lambda j, i: (i, j)),
                   pl.BlockSpec((None, 8, HD), lambda j, i: (jnp.minimum(j, 1), 0, 0))],
        compiler_params=_params(2))(proj, nw, cos, sin, *dqk_groups, dqk_b, *dvs, dga, dgb)


def _in_proj_bwd(dproj, w_in, x, dh1, g, deps=()):
    tm, tk = 512, 1280
    per = (DIN // NSH) // tk
    nk = DIN // tk

    def body(dp_ref, w_ref, x_ref, dh_ref, g_ref, dx_ref, dg_ref, acc):
        i, k = pl.program_id(0), pl.program_id(1)

        @pl.when(k == 0)
        def _():
            acc[...] = jnp.zeros_like(acc)

        @pl.when((k == 0) & (i == 0))
        def _():
            dg_ref[...] = jnp.zeros_like(dg_ref)

        acc[...] += _dot_nt(dp_ref[...], w_ref[...])

        @pl.when(k == nk - 1)
        def _():
            dx, dg = _norm_bwd(x_ref[...], acc[...], g_ref[...])
            dx_ref[...] = dh_ref[...] + dx
            dg_ref[...] += dg

    row = pl.BlockSpec((tm, D), lambda i, k: (i, 0))
    vec = pl.BlockSpec((1, D), lambda i, k: (0, 0))
    return pl.pallas_call(
        _after(body, deps), name="in_proj_bwd", out_shape=[SDS((T, D), F32), SDS((1, D), F32)], grid=(T // tm, nk),
        in_specs=[DEP_SPEC] * len(deps) + [
            pl.BlockSpec((tm, tk), lambda i, k: (i, k)),
            pl.BlockSpec((None, D, tk), lambda i, k: (k // per, 0, k % per)), row, row, vec],
        out_specs=[row, vec], scratch_shapes=[pltpu.VMEM((tm, D), F32)],
        compiler_params=_params(2))(*deps, dproj, w_in, x, dh1, g)


def _grad_w(name, a, g, shard_rows, rows, cols, tr, tc):
    ni, nj = rows // tr, cols // tc
    if shard_rows:
        a_map, g_map = (lambda s, i, j: (0, s * ni + i)), (lambda s, i, j: (0, j))
    else:
        a_map, g_map = (lambda s, i, j: (0, i)), (lambda s, i, j: (0, s * nj + j))

    def body(a_ref, g_ref, o_ref):
        o_ref[...] = _dot_tn(a_ref[...], g_ref[...]).astype(BF16)

    return pl.pallas_call(
        body, name=name, out_shape=SDS((NSH, rows, cols), BF16), grid=(NSH, ni, nj),
        in_specs=[pl.BlockSpec((T, tr), a_map), pl.BlockSpec((T, tc), g_map)],
        out_specs=pl.BlockSpec((None, tr, tc), lambda s, i, j: (s, i, j)), compiler_params=_params(3))(a, g)


def _grad_w_in_half(name, xn, dproj, place, for_sibling, deps=()):
    tr, tc = D // 2, 1280
    nj = (DIN // NSH) // tc

    def body(*refs):
        a_ref, g_ref, o_ref = refs[-3:]
        o_ref[...] = _dot_tn(a_ref[...], g_ref[...]).astype(BF16)

    half = (lambda p: 1 - p[1]) if for_sibling else (lambda p: p[1])
    return pl.pallas_call(
        body, name=name, out_shape=SDS((NSH, tr, DIN // NSH), BF16),
        grid_spec=pltpu.PrefetchScalarGridSpec(
            num_scalar_prefetch=1, grid=(NSH, nj),
            in_specs=[DEP_SPEC] * len(deps) + [pl.BlockSpec((T, tr), lambda s, j, p: (0, half(p))),
                                               pl.BlockSpec((T, tc), lambda s, j, p: (0, s * nj + j))],
            out_specs=pl.BlockSpec((None, tr, tc), lambda s, j, p: (s, 0, j))),
        compiler_params=_params(2))(place, *deps, xn, dproj)


def _adamw(w, g, m, v):
    m = B1 * m + (1.0 - B1) * g
    v = B2 * v + (1.0 - B2) * (g * g)
    m_hat = m / (1.0 - B1 ** STEP)
    v_hat = v / (1.0 - B2 ** STEP)
    delta = -LR * (m_hat / (jnp.sqrt(v_hat) + AEPS) + WD * w)
    return delta, m, v


def _sum_halves(name, place, grads, theirs):
    _, rows, cols = theirs.shape
    tr = _row_tile(rows, cols, 1 << 20)

    def body(place_ref, a_ref, b_ref, o_ref):
        o_ref[...] = (a_ref[...].astype(F32) + b_ref[...].astype(F32)).astype(BF16)

    spec = pl.BlockSpec((None, tr, cols), lambda s, i, p: (s, i, 0))
    mine = spec if grads.ndim == 3 else pl.BlockSpec((None, None, tr, cols), lambda s, i, p: (s, p[1], i, 0))
    return pl.pallas_call(
        body, name=name, out_shape=SDS(theirs.shape, BF16),
        grid_spec=pltpu.PrefetchScalarGridSpec(
            num_scalar_prefetch=1, grid=(NSH, rows // tr), in_specs=[mine, spec], out_specs=spec),
        compiler_params=_params(2))(place, grads, theirs)


def _sum_landed(name, place, part, landed):
    _, rows, cols = part.shape
    tr = _row_tile(rows, cols, 1 << 20)

    def body(place_ref, p_ref, l_ref, o_ref):
        o_ref[...] = ((p_ref[...].astype(F32) + l_ref[0].astype(F32)) + l_ref[1].astype(F32)) + l_ref[2].astype(F32)

    return pl.pallas_call(
        body, name=name, out_shape=SDS((2, rows, cols), F32),
        grid_spec=pltpu.PrefetchScalarGridSpec(
            num_scalar_prefetch=1, grid=(rows // tr,),
            in_specs=[pl.BlockSpec((None, tr, cols), lambda i, p: (p[0], i, 0)),
                      pl.BlockSpec((3, tr, cols), lambda i, p: (0, i, 0))],
            out_specs=pl.BlockSpec((None, tr, cols), lambda i, p: (p[1], i, 0))),
        compiler_params=_params(1))(place, part, landed)


def _adam_shard(name, g, w, m, v):
    rows, cols = w.shape
    tr = _row_tile(rows, cols, 1 << 19)

    def body(g_ref, w_ref, m_ref, v_ref, go_ref, d_ref, nm_ref, nv_ref):
        g = g_ref[...]
        go_ref[...] = g
        d_ref[...], nm_ref[...], nv_ref[...] = _adamw(w_ref[...], g, m_ref[...], v_ref[...])

    spec = pl.BlockSpec((tr, cols), lambda i: (i, 0))
    return pl.pallas_call(
        body, name=name, out_shape=[SDS((rows, cols), F32)] * 4, grid=(rows // tr,),
        in_specs=[spec] * 4, out_specs=[spec] * 4, compiler_params=_params(1))(g, w, m, v)


def _adam_small(gathered, w, m, v):
    def body(g_ref, w_ref, m_ref, v_ref, go_ref, d_ref, nm_ref, nv_ref):
        g = g_ref[0:SMALL_ROWS, :]
        for dev in range(1, 8):
            g = g + g_ref[dev * SMALL_ROWS:(dev + 1) * SMALL_ROWS, :]
        go_ref[...] = g
        d_ref[...], nm_ref[...], nv_ref[...] = _adamw(w_ref[...], g, m_ref[...], v_ref[...])

    return pl.pallas_call(body, name="adam_small", out_shape=[SDS((SMALL_ROWS, HD), F32)] * 4)(gathered, w, m, v)


SMALL = (("norm_mix", (1, D)), ("b_gate", (1, 2 * D)), ("q_norm_a", (1, HD)), ("k_norm_a", (1, HD)),
         ("q_norm_b", (1, HD)), ("k_norm_b", (1, HD)), ("rpb_b", (1, 4, 15, 31)), ("norm_ffn", (1, D)))


LOSS_ROW = 83


def _pack_small(vals, scalar=None):
    pieces = []
    for (name, shape), val in zip(SMALL, vals):
        flat = val.reshape(-1)
        pad = (-flat.shape[0]) % HD
        pieces.append(jnp.pad(flat, (0, pad)).reshape(-1, HD))
    if scalar is not None:
        pieces.append(jnp.pad(scalar.reshape(1, 1), ((0, 0), (0, HD - 1))))
    packed = jnp.concatenate(pieces, axis=0)
    assert packed.shape[0] == LOSS_ROW + (scalar is not None)
    return jnp.pad(packed, ((0, SMALL_ROWS - packed.shape[0]), (0, 0)))


def _unpack_small(packed):
    out, row = [], 0
    for name, shape in SMALL:
        size = int(np.prod(shape))
        nrows = -(-size // HD)
        out.append(packed[row:row + nrows].reshape(-1)[:size].reshape(shape))
        row += nrows
    return out


def kernel(x, norm_mix, w_in, b_gate, q_norm_a, k_norm_a, q_norm_b, k_norm_b, rpb_b, w_proj_a, w_proj_b, w_out, norm_ffn, w_up, w_down, loss_target, m_norm_mix, m_w_in, m_b_gate, m_q_norm_a, m_k_norm_a, m_q_norm_b, m_k_norm_b, m_rpb_b, m_w_proj_a, m_w_proj_b, m_w_out, m_norm_ffn, m_w_up, m_w_down, v_norm_mix, v_w_in, v_b_gate, v_q_norm_a, v_k_norm_a, v_q_norm_b, v_k_norm_b, v_rpb_b, v_w_proj_a, v_w_proj_b, v_w_out, v_norm_ffn, v_w_up, v_w_down):
    big_names = ("w_in", "w_proj_a", "w_proj_b", "w_out", "w_up", "w_down")
    big_w = [a[0] for a in (w_in, w_proj_a, w_proj_b, w_out, w_up, w_down)]
    big_m = [a[0] for a in (m_w_in, m_w_proj_a, m_w_proj_b, m_w_out, m_w_up, m_w_down)]
    big_v = [a[0] for a in (v_w_in, v_w_proj_a, v_w_proj_b, v_w_out, v_w_up, v_w_down)]
    x2, target = x[0], loss_target[0]

    place = jnp.stack([2 * lax.axis_index("x") + lax.axis_index("y"), lax.axis_index("c")]).astype(jnp.int32)
    groups = ((0,), (1, 2, 3), (4,), (5,))
    started = []
    for j, grp in enumerate(groups):
        deps = (started[0][4],) if j else ()
        placed = [_cast_into_place(big_w[i], "cast_" + big_names[i], place, deps) for i in grp]
        started.append(_gather_start(f"gather_start_{j}", placed))

    def whole(fulls):
        return [f.reshape(NSH, 2 * f.shape[2], f.shape[3]) for f in fulls]

    def forward_begin(j, after):
        send, recv, _, fulls, _ = started[j]
        fulls = _gather_wait(f"gather_wait_{j}", send, recv, fulls, after)
        send, recv, _, fulls, token = _forward_start(f"forward_start_{j}", fulls)
        return (send, recv, fulls), token

    def forward_end(j, state, after):
        return whole(_forward_wait(f"forward_wait_{j}", *state, after))

    def as_halves(grads):
        return [g.reshape(NSH, 2, g.shape[1] // 2, g.shape[2]) for g in grads]

    def reduce_start(j, grads, theirs):
        parts = [_sum_halves(f"sum_halves_{j}_{i}", place, a, b) for i, (a, b) in enumerate(zip(grads, theirs))]
        send, recv, parts, lands, token = _reduce_start(f"reduce_start_{j}", parts)
        return (send, recv, parts, lands), token

    def exchange_begin(j, grads):
        send, recv, grads, lands, token = _exchange_start(f"exchange_start_{j}", as_halves(grads))
        return (send, recv, grads, lands), token

    def exchange_end(j, state, after):
        return reduce_start(j, *_exchange_wait(f"exchange_wait_{j}", *state, after))

    big_out = {}

    def share_begin(j, state, after):
        send, recv, parts, lands = state
        parts, lands = _reduce_wait(f"reduce_wait_{j}", send, recv, parts, lands, after)
        sums = [_sum_landed(f"sum_landed_{j}_{i}", place, p, l) for i, (p, l) in enumerate(zip(parts, lands))]
        send, recv, _, sums, token = _share_start(f"share_start_{j}", sums)
        return (send, recv, sums), token

    def share_end(j, state, after):
        for idx, g in zip(groups[j], _share_wait(f"share_wait_{j}", *state, after)):
            g = g.reshape(big_w[idx].shape)
            big_out[idx] = _adam_shard("adam_" + big_names[idx], g, big_w[idx], big_m[idx], big_v[idx])
        return big_out[groups[j][-1]][1]

    proj, xn = _norm_in_proj_own(x2, norm_mix, whole(started[0][3])[0], place)
    send, recv, _, win, _ = started[0]
    win = _gather_wait("gather_wait_0", send, recv, win, (proj, *[s[4] for s in started[1:]]))
    (win_f,) = whole(_gather_finish("gather_finish_0", win))
    proj = _in_proj_rest("in_proj_rest", xn, win_f, proj, place, (2, 1, 3))
    cos, sin = _rope_tables()
    nw = jnp.stack([jnp.concatenate([jnp.tile(q_norm_a, (1, NHA)), jnp.tile(q_norm_b, (1, NH - NHA))], axis=1),
                    jnp.concatenate([jnp.tile(k_norm_a, (1, NHA)), jnp.tile(k_norm_b, (1, NH - NHA))], axis=1)])
    qkn = _qk_prep(proj, nw, cos, sin)
    fw1, token = forward_begin(1, (qkn,))
    fwd_a = [_attn_a_fwd(qkn, proj, g) for g in range(3)]
    os, ls = [f[0] for f in fwd_a], [f[1] for f in fwd_a]
    fw2, token = forward_begin(2, (os[2], token))
    ob, lse_b, bias = _attn_b_fwd(qkn, proj, _rpb_rows(rpb_b[0]))
    oa, w0, w1, w2 = _comb_fwd(os, ls)
    ws = [w0, w1, w2]
    wpa_f, wpb_f, wout_f = forward_end(1, fw1, (oa, token))
    wout_f = wout_f.reshape(D, D)
    mixed, ob16 = _mix_fwd(oa, ob, proj, b_gate, wpa_f, wpb_f)
    h1, hn = _out_proj_fwd(mixed, wout_f, x2, norm_ffn)
    (wup_f,) = forward_end(2, fw2, (hn,))
    usq, u = _ffn_up(hn, wup_f)
    fw3, token = forward_begin(3, (u,))
    own = _ffn_down_own(usq, whole(fw3[2])[0].reshape(DFF, D), place)
    (wdown_f,) = forward_end(3, fw3, (own, token))
    wdown_f = wdown_f.reshape(DFF, D)
    dy, dy16, loss_parts = _ffn_down_loss(usq, wdown_f, h1, target, own, place)
    loss_part = jnp.sum(loss_parts[:, 0, 0])

    g_down = _grad_w("grad_w_down", usq, dy16, True, DFF // NSH, D, 1024, 1024)
    ex_down, token = exchange_begin(3, [g_down])
    du = _ffn_down_bwd(dy16, wdown_f, u, deps=(token,))
    g_up = _grad_w("grad_w_up", hn, du, False, D, DFF // NSH, 1024, 1024)
    red_down, token = exchange_end(3, ex_down, (g_up,))
    ex_up, token_up = exchange_begin(2, [g_up])
    dh1, dh16, d_norm_ffn = _ffn_up_bwd(du, wup_f, h1, dy, norm_ffn, deps=(token, token_up))
    dya, dyb, dga, dgb, doa, dob, dba, dbb = _mix_bwd(dh16, wout_f, oa, ob16, proj, b_gate, wpa_f, wpb_f)
    g_out = _grad_w("grad_w_out", mixed, dh16, True, D // NSH, D, 512, 1024)
    g_pa = _grad_w("grad_w_proj_a", oa, dya, False, 512, 512, 512, 512)
    g_pb = _grad_w("grad_w_proj_b", ob16, dyb, False, 512, 512, 512, 512)
    red_up, token = exchange_end(2, ex_up, (g_out,))
    ex_mid, token_mid = exchange_begin(1, [g_pa, g_pb, g_out])
    cc = _comb_bwd(doa, os, ws, deps=(token, token_mid))
    bwd_a = [_attn_a_bwd(qkn, proj, doa, ls[g], ws[g], cc, g) for g in range(3)]
    red_mid, token = exchange_end(1, ex_mid, (bwd_a[2][1],))
    dqk_b, dv_b, drpb_t = _attn_b_bwd(qkn, proj, dob, ob, lse_b, bias, deps=(token,))
    dproj, dn = _qk_bwd(proj, nw, cos, sin, [b[0] for b in bwd_a], dqk_b, [b[1] for b in bwd_a] + [dv_b], dga, dgb)
    g_in_theirs = _grad_w_in_half("grad_w_in_for_sibling", xn, dproj, place, True)
    send, recv, g_in_theirs, lands, token = _exchange_start("exchange_start_0", [g_in_theirs], sliced=False)
    g_in_mine = _grad_w_in_half("grad_w_in_own", xn, dproj, place, False, deps=(token,))
    _, theirs = _exchange_wait("exchange_wait_0", send, recv, g_in_theirs, lands, (g_in_mine,), sliced=False)
    red_in, token = reduce_start(0, [g_in_mine], theirs)
    grad_x, d_norm_mix = _in_proj_bwd(dproj, win_f, x2, dh1, norm_mix, deps=(token,))

    sh_down, token = share_begin(3, red_down, (grad_x,))
    sh_up, token = share_begin(2, red_up, (token,))
    done = share_end(3, sh_down, (token,))
    sh_mid, token = share_begin(1, red_mid, (done,))
    done = share_end(2, sh_up, (token,))
    sh_in, token = share_begin(0, red_in, (done,))
    done = share_end(1, sh_mid, (token,))
    done = share_end(0, sh_in, (done,))

    d_rpb = drpb_t[:, :15, GRID_W - WIN_C:GRID_W + WIN_C - 1]
    small_g = [d_norm_mix, jnp.concatenate([dba, dbb], axis=1), dn[0, 0], dn[1, 0], dn[0, 1], dn[1, 1], d_rpb, d_norm_ffn]
    gathered_small = _allgather_small(_pack_small(small_g, loss_part), done)
    small_w = (norm_mix, b_gate, q_norm_a, k_norm_a, q_norm_b, k_norm_b, rpb_b, norm_ffn)
    small_m = (m_norm_mix, m_b_gate, m_q_norm_a, m_k_norm_a, m_q_norm_b, m_k_norm_b, m_rpb_b, m_norm_ffn)
    small_v = (v_norm_mix, v_b_gate, v_q_norm_a, v_k_norm_a, v_q_norm_b, v_k_norm_b, v_rpb_b, v_norm_ffn)
    small_packed = _adam_small(gathered_small, _pack_small(small_w), _pack_small(small_m), _pack_small(small_v))
    small_out = [_unpack_small(p) for p in small_packed]
    loss = small_packed[0][LOSS_ROW, 0]

    order = ("norm_mix", "w_in", "b_gate", "q_norm_a", "k_norm_a", "q_norm_b", "k_norm_b", "rpb_b",
             "w_proj_a", "w_proj_b", "w_out", "norm_ffn", "w_up", "w_down")
    small_idx = {name: i for i, (name, _) in enumerate(SMALL)}
    outs = []
    for kind in range(4):
        for name in order:
            if name in small_idx:
                outs.append(small_out[kind][small_idx[name]])
            else:
                outs.append(big_out[big_names.index(name)][kind][None])
    return (loss, grad_x[None], *outs)
```

```python
import functools

import numpy as np
import jax
import jax.numpy as jnp
from jax import lax
from jax.experimental import pallas as pl
from jax.experimental.pallas import tpu as pltpu

F32, BF16 = jnp.float32, jnp.bfloat16
SDS = jax.ShapeDtypeStruct
MESH = pl.DeviceIdType.MESH

T = 2048
D = 2048
HD = 128
NH, NHA = 16, 12
DIN = 10240
DFF = 8192
NSH = 4
DILS = (1, 4, 16)
EPS = 1e-6
NEG = -1e30
SCALE = HD ** -0.5
GRID_W, WIN_R, WIN_C = 64, 8, 16
VMEM_LIMIT = 56 * 1024 * 1024
B1, B2, LR, AEPS, WD, STEP = 0.9, 0.999, 0.001, 1e-08, 0.01, 10
SMALL_ROWS = 88


def _dot(a, b):
    return jnp.dot(a, b, preferred_element_type=F32)


def _dot_nt(a, b):
    return lax.dot_general(a, b, (((1,), (1,)), ((), ())), preferred_element_type=F32)


def _dot_tn(a, b):
    return lax.dot_general(a, b, (((0,), (0,)), ((), ())), preferred_element_type=F32)


def _params(n):
    return pltpu.CompilerParams(dimension_semantics=("arbitrary",) * n, vmem_limit_bytes=VMEM_LIMIT)


def _resident(shape, index_map):
    return pl.BlockSpec(shape, index_map, pipeline_mode=pl.Buffered(1))


def _sigmoid(z):
    return 1.0 / (1.0 + jnp.exp(-z))


def _wide(v, n):
    return jnp.concatenate([v] * n, axis=1)


def _row_tile(rows, cols, elems):
    tr = 16
    while tr * 2 <= rows and tr * 2 * cols <= elems:
        tr *= 2
    return tr


def _place():
    x, y, c = lax.axis_index("x"), lax.axis_index("y"), lax.axis_index("c")
    peers = [(1 - x, y), (x, 1 - y), (1 - x, 1 - y)]
    return x, y, c, peers


def _cast_into_place(w, name, place, deps=()):
    rows, cols = w.shape
    hr = rows // 2
    tr = min(hr, 256)
    per = hr // tr

    def body(*refs):
        w_ref, o_ref = refs[-2:]
        o_ref[...] = w_ref[...].astype(BF16)

    return pl.pallas_call(
        body, name=name, out_shape=SDS((NSH, 2, hr, cols), BF16),
        grid_spec=pltpu.PrefetchScalarGridSpec(
            num_scalar_prefetch=1, grid=(2, per),
            in_specs=[DEP_SPEC] * len(deps) + [pl.BlockSpec((tr, cols), lambda h, i, p: (h * per + i, 0))],
            out_specs=pl.BlockSpec((None, None, tr, cols), lambda h, i, p: (p[0], h, i, 0))),
        compiler_params=_params(2))(place, *deps, w)


ANY_SPEC = pl.BlockSpec(memory_space=pl.ANY)
HBM_SPEC = pl.BlockSpec(memory_space=pltpu.HBM)
SEM_SPEC = pl.BlockSpec(memory_space=pltpu.SEMAPHORE)
DEP_SPEC = pl.BlockSpec((8, 128), lambda *_: (0, 0))
EFFECT = pltpu.SideEffectType.DATAFLOW_SIDE_EFFECTING


def _after(body, deps):
    n = len(deps)
    return (lambda *refs: body(*refs[n:])) if n else body


SIBLING_BARRIER = 1


def _split_start(name, srcs, lands, n_copies, issue, sibling_only=False):
    n, m = len(srcs), len(lands)

    def body(*refs):
        if sibling_only:
            x, y, c, _ = _place()
            barrier = pltpu.get_barrier_semaphore()
            pl.semaphore_signal(barrier, inc=1, device_id=(x, y, 1 - c), device_id_type=MESH)
            pl.semaphore_wait(barrier, 1)
        issue(refs[:n], refs[n:n + m], refs[n + m], refs[n + m + 1])
        refs[-1][...] = jnp.zeros((8, 128), F32)

    arrays = list(srcs) + list(lands)
    outs = pl.pallas_call(
        body, name=name,
        out_shape=(pltpu.SemaphoreType.DMA((n_copies,)), pltpu.SemaphoreType.DMA((n_copies,)),
                   *[pltpu.HBM(a.shape, a.dtype) for a in arrays], SDS((8, 128), F32)),
        in_specs=[HBM_SPEC] * (n + m),
        out_specs=(SEM_SPEC, SEM_SPEC, *[HBM_SPEC] * (n + m), pl.BlockSpec(memory_space=pltpu.VMEM)),
        input_output_aliases={i: 2 + i for i in range(n + m)},
        compiler_params=pltpu.CompilerParams(has_side_effects=EFFECT,
                                             collective_id=SIBLING_BARRIER if sibling_only else None),
    )(*[pltpu.with_memory_space_constraint(a, pltpu.HBM) for a in arrays])
    return outs[0], outs[1], list(outs[2:2 + n]), list(outs[2 + n:2 + n + m]), outs[-1]


def _split_wait(name, send_sems, recv_sems, srcs, lands, after, wait):
    n, m = len(srcs), len(lands)

    def body(*refs):
        wait(refs[:n], refs[n:n + m], refs[n + m], refs[n + m + 1])

    arrays = list(srcs) + list(lands)
    outs = pl.pallas_call(
        body, name=name, out_shape=[pltpu.HBM(a.shape, a.dtype) for a in arrays],
        in_specs=[HBM_SPEC] * (n + m) + [SEM_SPEC, SEM_SPEC] + [ANY_SPEC] * len(after),
        out_specs=[HBM_SPEC] * (n + m), input_output_aliases={i: i for i in range(n + m)},
        compiler_params=pltpu.CompilerParams(has_side_effects=EFFECT),
    )(*arrays, send_sems, recv_sems, *after)
    return list(outs[:n]), list(outs[n:])


def _gather_start(name, fulls):
    def issue(srcs, dsts, send_sems, recv_sems):
        x, y, c, peers = _place()
        for i in range(len(fulls)):
            mine = dsts[i].at[2 * x + y, c]
            for k, (px, py) in enumerate(peers):
                pltpu.make_async_remote_copy(
                    src_ref=mine, dst_ref=mine, send_sem=send_sems.at[3 * i + k],
                    recv_sem=recv_sems.at[3 * i + k], device_id=(px, py, c), device_id_type=MESH).start()

    return _split_start(name, [], fulls, 3 * len(fulls), issue)


def _gather_wait(name, send_sems, recv_sems, fulls, after):
    def wait(srcs, dsts, send_sems, recv_sems):
        x, y, c, peers = _place()
        for i in range(len(fulls)):
            for k, (px, py) in enumerate(peers):
                cp = pltpu.make_async_remote_copy(
                    src_ref=dsts[i].at[2 * x + y, c], dst_ref=dsts[i].at[2 * px + py, c],
                    send_sem=send_sems.at[3 * i + k], recv_sem=recv_sems.at[3 * i + k],
                    device_id=(px, py, c), device_id_type=MESH)
                cp.wait_send()
                cp.wait_recv()

    return _split_wait(name, send_sems, recv_sems, [], fulls, after, wait)[1]


def _gather_finish(name, fulls):
    n = len(fulls)

    def body(*refs):
        fin, fout = refs[:n], refs[n:2 * n]
        send_sems, recv_sems = refs[2 * n:]
        x, y, c, peers = _place()

        def copy(i, k, half):
            px, py = peers[k]
            return pltpu.make_async_remote_copy(
                src_ref=fin[i].at[2 * px + py, half], dst_ref=fout[i].at[2 * px + py, half],
                send_sem=send_sems.at[3 * i + k], recv_sem=recv_sems.at[3 * i + k],
                device_id=(x, y, 1 - c), device_id_type=MESH)

        sends = [copy(i, k, c) for i in range(n) for k in range(3)]
        for cp in sends:
            cp.start()
        for i in range(n):
            for k in range(3):
                copy(i, k, 1 - c).wait_recv()
        for cp in sends:
            cp.wait_send()

    return pl.pallas_call(
        body, name=name, out_shape=[SDS(f.shape, f.dtype) for f in fulls],
        in_specs=[ANY_SPEC] * n, out_specs=[ANY_SPEC] * n, input_output_aliases={i: i for i in range(n)},
        scratch_shapes=[pltpu.SemaphoreType.DMA((3 * n,)), pltpu.SemaphoreType.DMA((3 * n,))])(*fulls)


def _reduce_start(name, parts):
    lands = [lax.empty((3,) + p.shape[1:], p.dtype) for p in parts]

    def issue(srcs, dsts, send_sems, recv_sems):
        x, y, c, peers = _place()
        for i in range(len(parts)):
            for k, (px, py) in enumerate(peers):
                pltpu.make_async_remote_copy(
                    src_ref=srcs[i].at[2 * px + py], dst_ref=dsts[i].at[k], send_sem=send_sems.at[3 * i + k],
                    recv_sem=recv_sems.at[3 * i + k], device_id=(px, py, c), device_id_type=MESH).start()

    return _split_start(name, parts, lands, 3 * len(parts), issue)


def _reduce_wait(name, send_sems, recv_sems, parts, lands, after):
    def wait(srcs, dsts, send_sems, recv_sems):
        x, y, c, peers = _place()
        for i in range(len(parts)):
            for k, (px, py) in enumerate(peers):
                cp = pltpu.make_async_remote_copy(
                    src_ref=srcs[i].at[2 * px + py], dst_ref=dsts[i].at[k], send_sem=send_sems.at[3 * i + k],
                    recv_sem=recv_sems.at[3 * i + k], device_id=(px, py, c), device_id_type=MESH)
                cp.wait_send()
                cp.wait_recv()

    return _split_wait(name, send_sems, recv_sems, parts, lands, after, wait)


def _sibling_copy(src, dst, send_sems, recv_sems, k):
    x, y, c, _ = _place()
    return pltpu.make_async_remote_copy(src_ref=src, dst_ref=dst, send_sem=send_sems.at[k], recv_sem=recv_sems.at[k],
                                        device_id=(x, y, 1 - c), device_id_type=MESH)


def _forward_start(name, fulls):
    def issue(srcs, dsts, send_sems, recv_sems):
        x, y, c, peers = _place()
        for i in range(len(fulls)):
            for k, (px, py) in enumerate(peers):
                part = dsts[i].at[2 * px + py, c]
                _sibling_copy(part, part, send_sems, recv_sems, 3 * i + k).start()

    return _split_start(name, [], fulls, 3 * len(fulls), issue, sibling_only=True)


def _forward_wait(name, send_sems, recv_sems, fulls, after):
    def wait(srcs, dsts, send_sems, recv_sems):
        x, y, c, peers = _place()
        for i in range(len(fulls)):
            for k, (px, py) in enumerate(peers):
                cp = _sibling_copy(dsts[i].at[2 * px + py, c], dsts[i].at[2 * px + py, 1 - c], send_sems, recv_sems, 3 * i + k)
                cp.wait_send()
                cp.wait_recv()

    return _split_wait(name, send_sems, recv_sems, [], fulls, after, wait)[1]


def _exchange_start(name, grads, sliced=True):
    lands = [lax.empty((NSH,) + g.shape[-2:], g.dtype) for g in grads]

    def issue(srcs, dsts, send_sems, recv_sems):
        c = lax.axis_index("c")
        for i in range(len(grads)):
            src = srcs[i].at[:, 1 - c] if sliced else srcs[i]
            _sibling_copy(src, dsts[i], send_sems, recv_sems, i).start()

    return _split_start(name, grads, lands, len(grads), issue, sibling_only=True)


def _exchange_wait(name, send_sems, recv_sems, grads, lands, after, sliced=True):
    def wait(srcs, dsts, send_sems, recv_sems):
        c = lax.axis_index("c")
        for i in range(len(grads)):
            cp = _sibling_copy(srcs[i].at[:, 1 - c] if sliced else srcs[i], dsts[i], send_sems, recv_sems, i)
            cp.wait_send()
            cp.wait_recv()

    return _split_wait(name, send_sems, recv_sems, grads, lands, after, wait)


def _share_start(name, sums):
    def issue(srcs, dsts, send_sems, recv_sems):
        c = lax.axis_index("c")
        for i in range(len(sums)):
            _sibling_copy(dsts[i].at[c], dsts[i].at[c], send_sems, recv_sems, i).start()

    return _split_start(name, [], sums, len(sums), issue, sibling_only=True)


def _share_wait(name, send_sems, recv_sems, sums, after):
    def wait(srcs, dsts, send_sems, recv_sems):
        c = lax.axis_index("c")
        for i in range(len(sums)):
            cp = _sibling_copy(dsts[i].at[c], dsts[i].at[1 - c], send_sems, recv_sems, i)
            cp.wait_send()
            cp.wait_recv()

    return _split_wait(name, send_sems, recv_sems, [], sums, after, wait)[1]


def _allgather_small(v, after):
    m_per, n = v.shape

    def body(x_ref, after_ref, out_ref, send_sems, recv_sems, local_sem):
        x, y, c = lax.axis_index("x"), lax.axis_index("y"), lax.axis_index("c")
        me, sibling = (x, y, c), (x, y, 1 - c)
        chips = [(1 - x, y), (x, 1 - y), (1 - x, 1 - y)]

        def rows(px, py, pc):
            return out_ref.at[pl.ds((4 * px + 2 * py + pc) * m_per, m_per), :]

        def copy(k, block, to, src=None):
            return pltpu.make_async_remote_copy(
                src_ref=rows(*block) if src is None else src, dst_ref=rows(*block),
                send_sem=send_sems.at[k], recv_sem=recv_sems.at[k], device_id=to, device_id_type=MESH)

        mine = pltpu.make_async_copy(x_ref, rows(*me), local_sem)
        mine.start()
        first = [copy(0, me, sibling, src=x_ref)]
        first += [copy(1 + j, me, (*chip, c), src=x_ref) for j, chip in enumerate(chips)]
        for cp in first:
            cp.start()
        passed = [copy(4 + j, (*chip, c), sibling) for j, chip in enumerate(chips)]
        for j, chip in enumerate(chips):
            copy(1 + j, (*chip, c), me).wait_recv()
            passed[j].start()
        copy(0, sibling, me).wait_recv()
        for j, chip in enumerate(chips):
            copy(4 + j, (*chip, 1 - c), me).wait_recv()
        for cp in first + passed:
            cp.wait_send()
        mine.wait()

    return pl.pallas_call(
        body, name="allgather_small", out_shape=SDS((8 * m_per, n), v.dtype),
        in_specs=[pl.BlockSpec(memory_space=pltpu.VMEM), ANY_SPEC], out_specs=pl.BlockSpec(memory_space=pltpu.VMEM),
        scratch_shapes=[pltpu.SemaphoreType.DMA((7,)), pltpu.SemaphoreType.DMA((7,)), pltpu.SemaphoreType.DMA])(v, after)


def _norm_in_proj_own(x, g, w_full, place):
    tn, chunk = 512, 256
    per = (DIN // NSH) // tn

    def body(place_ref, x_ref, g_ref, w_ref, proj_ref, xn_ref):
        @pl.when(pl.program_id(0) == 0)
        def _():
            def norm(r, carry):
                rows = pl.ds(pl.multiple_of(r * chunk, chunk), chunk)
                xv = x_ref[rows, :]
                rs = lax.rsqrt(jnp.mean(xv * xv, axis=-1, keepdims=True) + EPS)
                xn_ref[rows, :] = (xv * rs * g_ref[...]).astype(BF16)
                return carry

            lax.fori_loop(0, T // chunk, norm, 0)

        proj_ref[...] = _dot(xn_ref[...], w_ref[...])

    return pl.pallas_call(
        body, name="norm_in_proj_own", out_shape=[SDS((T, DIN), F32), SDS((T, D), BF16)],
        grid_spec=pltpu.PrefetchScalarGridSpec(
            num_scalar_prefetch=1, grid=(per,),
            in_specs=[_resident((T, D), lambda j, p: (0, 0)),
                      pl.BlockSpec((1, D), lambda j, p: (0, 0)),
                      pl.BlockSpec((None, D, tn), lambda j, p: (p[0], 0, j))],
            out_specs=[pl.BlockSpec((T, tn), lambda j, p: (0, p[0] * per + j)),
                       pl.BlockSpec((T, D), lambda j, p: (0, 0))]),
        compiler_params=_params(1))(place, x, g, w_full)


def _in_proj_rest(xn, w_full, proj, place):
    tn = 512
    per = (DIN // NSH) // tn

    def body(place_ref, xn_ref, w_ref, proj_in, proj_ref):
        proj_ref[...] = _dot(xn_ref[...], w_ref[...])

    shard = lambda j, p: p[0] ^ (j // per + 1)
    return pl.pallas_call(
        body, name="in_proj_rest", out_shape=SDS((T, DIN), F32),
        grid_spec=pltpu.PrefetchScalarGridSpec(
            num_scalar_prefetch=1, grid=((NSH - 1) * per,),
            in_specs=[_resident((T, D), lambda j, p: (0, 0)),
                      pl.BlockSpec((None, D, tn), lambda j, p: (shard(j, p), 0, j % per)), ANY_SPEC],
            out_specs=pl.BlockSpec((T, tn), lambda j, p: (0, shard(j, p) * per + j % per))),
        input_output_aliases={3: 0}, compiler_params=_params(1))(place, xn, w_full, proj)


def _rope_tables():
    pos = np.arange(T, dtype=np.float32)
    inv = (10000.0 ** (-np.arange(0, HD, 2, dtype=np.float32) / HD)).astype(np.float32)
    ang = (pos[:, None] * inv[None, :]).astype(np.float32)
    cos, sin = np.cos(ang).astype(np.float32), np.sin(ang).astype(np.float32)
    return (jnp.asarray(np.concatenate([cos, cos], axis=1)), jnp.asarray(np.concatenate([-sin, sin], axis=1)))


def _qk_prep(proj, nw, cos, sin):
    tm = 256

    def body(p_ref, w_ref, cos_ref, sin_ref, o_ref):
        cv, sv = cos_ref[...], sin_ref[...]
        for h in range(NH):
            sl = slice(h * HD, (h + 1) * HD)
            xv = p_ref[:, sl]
            r = lax.rsqrt(jnp.mean(xv * xv, axis=-1, keepdims=True) + EPS)
            z = xv * r * w_ref[:, sl]
            if h < NHA:
                z = z * cv + pltpu.roll(z, 64, 1) * sv
            o_ref[:, sl] = z.astype(BF16)

    return pl.pallas_call(
        body, name="qk_prep", out_shape=SDS((T, 2 * D), BF16), grid=(T // tm, 2),
        in_specs=[pl.BlockSpec((tm, D), lambda i, j: (i, j)),
                  pl.BlockSpec((None, 1, D), lambda i, j: (j, 0, 0)),
                  pl.BlockSpec((tm, HD), lambda i, j: (i, 0)),
                  pl.BlockSpec((tm, HD), lambda i, j: (i, 0))],
        out_specs=pl.BlockSpec((tm, D), lambda i, j: (i, j)),
        compiler_params=_params(2))(proj, nw, cos, sin)


def _band_mask(q0, m):
    ii = lax.broadcasted_iota(jnp.int32, (128, 256), 0)
    jj = lax.broadcasted_iota(jnp.int32, (128, 256), 1)
    rel = jj - ii
    kpos = jj + (q0 - 64)
    return (rel >= 0) & (rel <= 128) & (kpos >= 0) & (kpos < m)


def _fill_padded(dst, src, m):
    zeros = jnp.zeros((64, HD), dst.dtype)
    dst[0:64, :] = zeros
    dst[64 + m:128 + m, :] = zeros
    dst[64:64 + m, :] = src.astype(dst.dtype)


def _residue_rows(r, m, dil):
    return pl.ds(r, m, stride=dil) if dil > 1 else slice(None)


def _head_blocks(g):
    col = lambda base: pl.BlockSpec((T, HD), lambda h: (0, base + g * 4 + h))
    return col(0), col(NH), col(2 * NH), pl.BlockSpec((T, HD), lambda h: (0, h))


def _attn_a_fwd(qkn, proj, g):
    dil = DILS[g]
    m = T // dil
    nb = m // 128

    def body(q_ref, k_ref, v_ref, o_ref, l_ref, qf, kf, qp, kp, vp, ob, lb):
        qf[...] = q_ref[...].astype(F32)
        kf[...] = k_ref[...].astype(F32)
        for r in range(dil):
            rows = _residue_rows(r, m, dil)
            qp[...] = qf[rows, :].astype(BF16)
            _fill_padded(kp, kf[rows, :], m)
            _fill_padded(vp, v_ref[rows, :], m)

            def block(b, carry):
                q0 = pl.multiple_of(b * 128, 128)
                kw, vw = kp[pl.ds(q0, 256), :], vp[pl.ds(q0, 256), :]
                s = _dot_nt(qp[pl.ds(q0, 128), :], kw) * SCALE
                s = jnp.where(_band_mask(q0, m), s, NEG)
                mx = jnp.max(s, axis=-1, keepdims=True)
                e = jnp.exp(s - mx)
                den = jnp.sum(e, axis=-1, keepdims=True)
                ob[pl.ds(q0, 128), :] = _dot((e / den).astype(BF16), vw)
                lb[pl.ds(q0, 128), :] = jnp.broadcast_to(mx + jnp.log(den), (128, HD))
                return carry

            lax.fori_loop(0, nb, block, 0, unroll=min(nb, 16))
            o_ref[rows, :] = ob[...]
            l_ref[rows, :] = lb[...]

    q_blk, k_blk, v_blk, out_blk = _head_blocks(g)
    return pl.pallas_call(
        body, name=f"attn_a_fwd_{g}", out_shape=[SDS((T, 512), F32)] * 2, grid=(4,),
        in_specs=[q_blk, k_blk, v_blk], out_specs=[out_blk] * 2,
        scratch_shapes=[pltpu.VMEM((T, HD), F32), pltpu.VMEM((T, HD), F32), pltpu.VMEM((m, HD), BF16),
                        pltpu.VMEM((m + 128, HD), BF16), pltpu.VMEM((m + 128, HD), BF16),
                        pltpu.VMEM((m, HD), F32), pltpu.VMEM((m, HD), F32)],
        compiler_params=_params(1))(qkn, qkn, proj)


def _nbr_window(r):
    start = jnp.clip(r - WIN_R // 2, 0, T // GRID_W - WIN_R)
    return start, start - r + (WIN_R - 1)


def _rpb_rows(rpb):
    zeros = jnp.zeros((4, 14, 33), F32)
    a, b = rpb[:, :14], rpb[:, 1:15]
    rows = jnp.concatenate([a[:, :, 15:31], zeros, b, zeros, a[:, :, 0:15]], axis=2)
    return jnp.pad(rows, ((0, 0), (0, 2), (0, 0)))


def _attn_b_fwd(qkn, proj, rpb_rows):
    def body(r_ref, q_ref, k_ref, v_ref, o_ref, l_ref, bias_ref, vb, pair):
        qc = lax.broadcasted_iota(jnp.int32, (GRID_W, 512), 0)
        kc = lax.broadcasted_iota(jnp.int32, (GRID_W, 512), 1) & (GRID_W - 1)
        cs = jnp.clip(qc - WIN_C // 2, 0, GRID_W - WIN_C)
        colmask = (kc >= cs) & (kc < cs + WIN_C)
        for d in range(14):
            pair[d] = pltpu.roll(jnp.broadcast_to(r_ref[d:d + 1, :], (GRID_W, HD)), 0, 1, stride=1, stride_axis=0)
        for off in range(8):
            rows = jnp.concatenate([pair[off + 2 * jj] for jj in range(4)], axis=1)
            bias_ref[off] = jnp.where(colmask, rows, NEG)
        vb[...] = v_ref[...].astype(BF16)

        def row(r, carry):
            start, off = _nbr_window(r)
            q0 = pl.multiple_of(r * GRID_W, GRID_W)
            k0 = pl.multiple_of(start * GRID_W, GRID_W)
            s = _dot_nt(q_ref[pl.ds(q0, GRID_W), :], k_ref[pl.ds(k0, 512), :]) * SCALE + bias_ref[off]
            mx = jnp.max(s, axis=-1, keepdims=True)
            e = jnp.exp(s - mx)
            den = jnp.sum(e, axis=-1, keepdims=True)
            o_ref[pl.ds(q0, GRID_W), :] = _dot((e / den).astype(BF16), vb[pl.ds(k0, 512), :])
            l_ref[pl.ds(q0, GRID_W), :] = jnp.broadcast_to(mx + jnp.log(den), (GRID_W, HD))
            return carry

        lax.fori_loop(0, T // GRID_W, row, 0, unroll=16)

    return pl.pallas_call(
        body, name="attn_b_fwd",
        out_shape=[SDS((T, 512), F32), SDS((T, 512), F32), SDS((4, 8, GRID_W, 512), F32)], grid=(4,),
        in_specs=[pl.BlockSpec((None, 16, HD), lambda h: (h, 0, 0)),
                  pl.BlockSpec((T, HD), lambda h: (0, NHA + h)),
                  pl.BlockSpec((T, HD), lambda h: (0, NH + NHA + h)),
                  pl.BlockSpec((T, HD), lambda h: (0, 2 * NH + NHA + h))],
        out_specs=[pl.BlockSpec((T, HD), lambda h: (0, h)), pl.BlockSpec((T, HD), lambda h: (0, h)),
                   pl.BlockSpec((None, 8, GRID_W, 512), lambda h: (h, 0, 0, 0))],
        scratch_shapes=[pltpu.VMEM((T, HD), BF16), pltpu.VMEM((14, GRID_W, HD), F32)],
        compiler_params=_params(1))(rpb_rows, qkn, qkn, proj)


def _comb_fwd(os, ls):
    tm = 512

    def body(o0, o1, o2, l0, l1, l2, oa_ref, w0, w1, w2):
        lv = [l0[...], l1[...], l2[...]]
        mx = jnp.maximum(jnp.maximum(lv[0], lv[1]), lv[2])
        ev = [jnp.exp(l - mx) for l in lv]
        den = ev[0] + ev[1] + ev[2]
        wv = [e / den for e in ev]
        oa_ref[...] = (wv[0] * o0[...] + wv[1] * o1[...] + wv[2] * o2[...]).astype(BF16)
        w0[...], w1[...], w2[...] = wv

    spec = pl.BlockSpec((tm, 512), lambda i: (i, 0))
    return pl.pallas_call(
        body, name="comb_fwd", out_shape=[SDS((T, 512), BF16)] + [SDS((T, 512), F32)] * 3, grid=(T // tm,),
        in_specs=[spec] * 6, out_specs=[spec] * 4, compiler_params=_params(1))(*os, *ls)


def _mix_fwd(oa, ob, proj, b_gate, wpa, wpb):
    tm = 512

    def body(oa_ref, ob_ref, ga_ref, gb_ref, ba_ref, bb_ref, wpa_ref, wpb_ref, mixed_ref, ob16_ref):
        oav = oa_ref[...]
        obv = ob_ref[...].astype(BF16)
        ob16_ref[...] = obv
        for s in range(NSH):
            sl = slice(s * 512, (s + 1) * 512)
            ga = _sigmoid(ga_ref[:, sl] + ba_ref[:, sl])
            gb = _sigmoid(gb_ref[:, sl] + bb_ref[:, sl])
            mixed_ref[:, sl] = (ga * _dot(oav, wpa_ref[s]) + gb * _dot(obv, wpb_ref[s])).astype(BF16)

    row = lambda w: pl.BlockSpec((tm, w), lambda i: (i, 0))
    return pl.pallas_call(
        body, name="mix_fwd", out_shape=[SDS((T, D), BF16), SDS((T, 512), BF16)], grid=(T // tm,),
        in_specs=[row(512), row(512),
                  pl.BlockSpec((tm, D), lambda i: (i, 3)), pl.BlockSpec((tm, D), lambda i: (i, 4)),
                  pl.BlockSpec((1, D), lambda i: (0, 0)), pl.BlockSpec((1, D), lambda i: (0, 1)),
                  _resident((NSH, 512, 512), lambda i: (0, 0, 0)), _resident((NSH, 512, 512), lambda i: (0, 0, 0))],
        out_specs=[row(D), row(512)], compiler_params=_params(1))(oa, ob, proj, proj, b_gate, b_gate, wpa, wpb)


def _out_proj_fwd(mixed, w_out, x, g):
    tm = 512

    def body(m_ref, w_ref, x_ref, g_ref, h1_ref, hn_ref):
        h1 = x_ref[...] + _dot(m_ref[...], w_ref[...])
        h1_ref[...] = h1
        r = lax.rsqrt(jnp.mean(h1 * h1, axis=-1, keepdims=True) + EPS)
        hn_ref[...] = (h1 * r * g_ref[...]).astype(BF16)

    row = pl.BlockSpec((tm, D), lambda i: (i, 0))
    return pl.pallas_call(
        body, name="out_proj_fwd", out_shape=[SDS((T, D), F32), SDS((T, D), BF16)], grid=(T // tm,),
        in_specs=[row, _resident((D, D), lambda i: (0, 0)), row, pl.BlockSpec((1, D), lambda i: (0, 0))],
        out_specs=[row, row], compiler_params=_params(1))(mixed, w_out, x, g)


def _ffn_up(hn, w_up):
    tm, tn = T, 512
    per = (DFF // NSH) // tn

    def body(h_ref, w_ref, a_ref, u_ref):
        uv = jnp.maximum(_dot(h_ref[...], w_ref[...]), 0.0)
        a_ref[...] = (uv * uv).astype(BF16)
        u_ref[...] = uv.astype(BF16)

    out = pl.BlockSpec((tm, tn), lambda i, j: (i, j))
    return pl.pallas_call(
        body, name="ffn_up", out_shape=[SDS((T, DFF), BF16)] * 2, grid=(T // tm, DFF // tn),
        in_specs=[pl.BlockSpec((tm, D), lambda i, j: (i, 0)),
                  pl.BlockSpec((None, D, tn), lambda i, j: (j // per, 0, j % per))],
        out_specs=[out, out], compiler_params=_params(2))(hn, w_up)


def _ffn_down_own(u, w_down, place):
    tm, tk = 512, DFF // NSH

    def body(place_ref, u_ref, w_ref, o_ref):
        o_ref[...] = _dot(u_ref[...], w_ref[...])

    return pl.pallas_call(
        body, name="ffn_down_own", out_shape=SDS((T, D), F32),
        grid_spec=pltpu.PrefetchScalarGridSpec(
            num_scalar_prefetch=1, grid=(T // tm,),
            in_specs=[pl.BlockSpec((tm, tk), lambda i, p: (i, p[0])), pl.BlockSpec((tk, D), lambda i, p: (p[0], 0))],
            out_specs=pl.BlockSpec((tm, D), lambda i, p: (i, 0))),
        compiler_params=_params(1))(place, u, w_down)


def _ffn_down_loss(u, w_down, h1, target, own, place):
    tm, tk = 512, DFF // NSH
    nk = NSH - 1

    def body(place_ref, u_ref, w_ref, h1_ref, t_ref, own_ref, dy_ref, dy16_ref, loss_ref, acc):
        k = pl.program_id(1)

        @pl.when(k == 0)
        def _():
            acc[...] = own_ref[...]

        acc[...] += _dot(u_ref[...], w_ref[...])

        @pl.when(k == nk - 1)
        def _():
            def chunk(r, sq):
                rows = pl.ds(pl.multiple_of(r * 16, 16), 16)
                err = acc[rows, :] + h1_ref[rows, :] - t_ref[rows, :]
                dy = err * (1.0 / D)
                dy_ref[rows, :] = dy
                dy16_ref[rows, :] = dy.astype(BF16)
                return sq + err * err

            sq = lax.fori_loop(0, tm // 16, chunk, jnp.zeros((16, D), F32), unroll=2)
            part = 0.5 * jnp.sum(jnp.mean(sq, axis=-1, keepdims=True), axis=0, keepdims=True)
            loss_ref[...] = jnp.broadcast_to(part, (8, 128))

    row = pl.BlockSpec((tm, D), lambda i, k, p: (i, 0))
    once = _resident((tm, D), lambda i, k, p: (i, 0))
    shard = lambda k, p: p[0] ^ (k + 1)
    return pl.pallas_call(
        body, name="ffn_down_loss",
        out_shape=[SDS((T, D), F32), SDS((T, D), BF16), SDS((T // tm, 8, 128), F32)],
        grid_spec=pltpu.PrefetchScalarGridSpec(
            num_scalar_prefetch=1, grid=(T // tm, nk),
            in_specs=[pl.BlockSpec((tm, tk), lambda i, k, p: (i, shard(k, p))),
                      pl.BlockSpec((tk, D), lambda i, k, p: (shard(k, p), 0)), once, once, once],
            out_specs=[row, row, pl.BlockSpec((None, 8, 128), lambda i, k, p: (i, 0, 0))],
            scratch_shapes=[pltpu.VMEM((tm, D), F32)]),
        compiler_params=_params(2))(place, u, w_down, h1, target, own)


def _ffn_down_bwd(dy16, w_down, u, deps=()):
    tm, tn = T, 512

    def body(dy_ref, w_ref, u_ref, du_ref):
        uv = u_ref[...].astype(F32)
        du_ref[...] = jnp.where(uv > 0.0, 2.0 * uv * _dot_nt(dy_ref[...], w_ref[...]), 0.0).astype(BF16)

    return pl.pallas_call(
        _after(body, deps), name="ffn_down_bwd", out_shape=SDS((T, DFF), BF16), grid=(T // tm, DFF // tn),
        in_specs=[DEP_SPEC] * len(deps) + [
            pl.BlockSpec((tm, D), lambda i, j: (i, 0)), pl.BlockSpec((tn, D), lambda i, j: (j, 0)),
            pl.BlockSpec((tm, tn), lambda i, j: (i, j))],
        out_specs=pl.BlockSpec((tm, tn), lambda i, j: (i, j)), compiler_params=_params(2))(*deps, dy16, w_down, u)


def _norm_bwd(xv, dz_in, g):
    r = lax.rsqrt(jnp.mean(xv * xv, axis=-1, keepdims=True) + EPS)
    dg = jnp.sum(xv * r * dz_in, axis=0, keepdims=True)
    dz = dz_in * g
    dx = r * dz - xv * (r * r * r) * jnp.mean(xv * dz, axis=-1, keepdims=True)
    return dx, dg


def _ffn_up_bwd(du, w_up, h1, dy, g, deps=()):
    tm, tk = 512, 1024
    per = (DFF // NSH) // tk
    nk = DFF // tk

    def body(du_ref, w_ref, h1_ref, dy_ref, g_ref, dh1_ref, dh16_ref, dg_ref, acc):
        i, k = pl.program_id(0), pl.program_id(1)

        @pl.when(k == 0)
        def _():
            acc[...] = jnp.zeros_like(acc)

        @pl.when((k == 0) & (i == 0))
        def _():
            dg_ref[...] = jnp.zeros_like(dg_ref)

        acc[...] += _dot_nt(du_ref[...], w_ref[...])

        @pl.when(k == nk - 1)
        def _():
            dx, dg = _norm_bwd(h1_ref[...], acc[...], g_ref[...])
            dh1 = dy_ref[...] + dx
            dh1_ref[...] = dh1
            dh16_ref[...] = dh1.astype(BF16)
            dg_ref[...] += dg

    row = pl.BlockSpec((tm, D), lambda i, k: (i, 0))
    vec = pl.BlockSpec((1, D), lambda i, k: (0, 0))
    return pl.pallas_call(
        _after(body, deps), name="ffn_up_bwd", out_shape=[SDS((T, D), F32), SDS((T, D), BF16), SDS((1, D), F32)],
        grid=(T // tm, nk),
        in_specs=[DEP_SPEC] * len(deps) + [
            pl.BlockSpec((tm, tk), lambda i, k: (i, k)),
            pl.BlockSpec((None, D, tk), lambda i, k: (k // per, 0, k % per)), row, row, vec],
        out_specs=[row, row, vec], scratch_shapes=[pltpu.VMEM((tm, D), F32)],
        compiler_params=_params(2))(*deps, du, w_up, h1, dy, g)


def _mix_bwd(dh16, w_out, oa, ob16, proj, b_gate, wpa, wpb):
    tm = 256

    def body(dh_ref, wo_ref, oa_ref, ob_ref, ga_ref, gb_ref, ba_ref, bb_ref, wpa_ref, wpb_ref,
             dya_ref, dyb_ref, dga_ref, dgb_ref, doa_ref, dob_ref, dba_ref, dbb_ref):
        @pl.when(pl.program_id(0) == 0)
        def _():
            dba_ref[...] = jnp.zeros_like(dba_ref)
            dbb_ref[...] = jnp.zeros_like(dbb_ref)

        oav, obv = oa_ref[...], ob_ref[...]
        doa = jnp.zeros((tm, 512), F32)
        dob = jnp.zeros((tm, 512), F32)
        for s in range(NSH):
            sl = slice(s * 512, (s + 1) * 512)
            dm = _dot_nt(dh_ref[...], wo_ref[sl, :])
            ga = _sigmoid(ga_ref[:, sl] + ba_ref[:, sl])
            gb = _sigmoid(gb_ref[:, sl] + bb_ref[:, sl])
            dya = (dm * ga).astype(BF16)
            dyb = (dm * gb).astype(BF16)
            dza = dm * _dot(oav, wpa_ref[s]) * ga * (1.0 - ga)
            dzb = dm * _dot(obv, wpb_ref[s]) * gb * (1.0 - gb)
            dya_ref[:, sl], dyb_ref[:, sl] = dya, dyb
            dga_ref[:, sl], dgb_ref[:, sl] = dza.astype(BF16), dzb.astype(BF16)
            dba_ref[:, sl] += jnp.sum(dza, axis=0, keepdims=True)
            dbb_ref[:, sl] += jnp.sum(dzb, axis=0, keepdims=True)
            doa += _dot_nt(dya, wpa_ref[s])
            dob += _dot_nt(dyb, wpb_ref[s])
        doa_ref[...], dob_ref[...] = doa, dob

    row = lambda w: pl.BlockSpec((tm, w), lambda i: (i, 0))
    vec = pl.BlockSpec((1, D), lambda i: (0, 0))
    wp = _resident((NSH, 512, 512), lambda i: (0, 0, 0))
    return pl.pallas_call(
        body, name="mix_bwd",
        out_shape=[SDS((T, D), BF16)] * 4 + [SDS((T, 512), F32)] * 2 + [SDS((1, D), F32)] * 2, grid=(T // tm,),
        in_specs=[row(D), _resident((D, D), lambda i: (0, 0)), row(512), row(512),
                  pl.BlockSpec((tm, D), lambda i: (i, 3)), pl.BlockSpec((tm, D), lambda i: (i, 4)),
                  pl.BlockSpec((1, D), lambda i: (0, 0)), pl.BlockSpec((1, D), lambda i: (0, 1)), wp, wp],
        out_specs=[row(D)] * 4 + [row(512)] * 2 + [vec] * 2,
        compiler_params=_params(1))(dh16, w_out, oa, ob16, proj, proj, b_gate, b_gate, wpa, wpb)


def _comb_bwd(doa, os, ws, deps=()):
    tm = 512

    def body(d_ref, o0, o1, o2, w0, w1, w2, cc_ref):
        prod = d_ref[...] * (w0[...] * o0[...] + w1[...] * o1[...] + w2[...] * o2[...])
        for h in range(4):
            sl = slice(h * HD, (h + 1) * HD)
            cc_ref[:, sl] = jnp.broadcast_to(jnp.sum(prod[:, sl], axis=-1, keepdims=True), (tm, HD))

    spec = pl.BlockSpec((tm, 512), lambda i: (i, 0))
    return pl.pallas_call(
        _after(body, deps), name="comb_bwd", out_shape=SDS((T, 512), F32), grid=(T // tm,),
        in_specs=[DEP_SPEC] * len(deps) + [spec] * 7, out_specs=spec,
        compiler_params=_params(1))(*deps, doa, *os, *ws)


def _attn_a_bwd(qkn, proj, doa, lse, w, cc, g):
    dil = DILS[g]
    m = T // dil
    nb = m // 128

    def body(q_ref, k_ref, v_ref, d_ref, l_ref, w_ref, c_ref, dqk_ref, dv_ref,
             qf, kf, qp, kp, vp, dp, lp, wsub, cp, dqb, dkp, dvp):
        qf[...] = q_ref[...].astype(F32)
        kf[...] = k_ref[...].astype(F32)
        for r in range(dil):
            sub = _residue_rows(r, m, dil)
            qp[...] = qf[sub, :].astype(BF16)
            _fill_padded(kp, kf[sub, :], m)
            _fill_padded(vp, v_ref[sub, :], m)
            dp[...] = d_ref[sub, :].astype(BF16)
            lp[...], wsub[...], cp[...] = l_ref[sub, :], w_ref[sub, :], c_ref[sub, :]
            dkp[...] = jnp.zeros_like(dkp)
            dvp[...] = jnp.zeros_like(dvp)

            def block(b, carry):
                q0 = pl.multiple_of(b * 128, 128)
                rows = pl.ds(q0, 128)
                win = pl.ds(q0, 256)
                qb, kw, vw = qp[rows, :], kp[win, :], vp[win, :]
                s = _dot_nt(qb, kw) * SCALE
                s = jnp.where(_band_mask(q0, m), s, NEG)
                wp = _wide(wsub[rows, :], 2) * jnp.exp(s - _wide(lp[rows, :], 2))
                dob = dp[rows, :]
                ds = (wp * (_dot_nt(dob, vw) - _wide(cp[rows, :], 2))).astype(BF16)
                dqb[rows, :] = _dot(ds, kw) * SCALE
                dkp[win, :] += _dot_tn(ds, qb) * SCALE
                dvp[win, :] += _dot_tn(wp.astype(BF16), dob)
                return carry

            lax.fori_loop(0, nb, block, 0, unroll=min(nb, 16))
            dqk_ref.at[0][sub, :] = dqb[...]
            dqk_ref.at[1][sub, :] = dkp[64:64 + m, :]
            dv_ref[sub, :] = dvp[64:64 + m, :]

    q_blk, k_blk, v_blk, blk = _head_blocks(g)
    sub16 = pltpu.VMEM((m, HD), BF16)
    sub32 = pltpu.VMEM((m, HD), F32)
    return pl.pallas_call(
        body, name=f"attn_a_bwd_{g}", out_shape=[SDS((2, T, 512), F32), SDS((T, 512), F32)], grid=(4,),
        in_specs=[q_blk, k_blk, v_blk, blk, blk, blk, blk],
        out_specs=[pl.BlockSpec((2, T, HD), lambda h: (0, 0, h)), blk],
        scratch_shapes=[pltpu.VMEM((T, HD), F32), pltpu.VMEM((T, HD), F32), sub16,
                        pltpu.VMEM((m + 128, HD), BF16), pltpu.VMEM((m + 128, HD), BF16), sub16,
                        sub32, sub32, sub32, sub32,
                        pltpu.VMEM((m + 128, HD), F32), pltpu.VMEM((m + 128, HD), F32)],
        compiler_params=_params(1))(qkn, qkn, proj, doa, lse, w, cc)


def _attn_b_bwd(qkn, proj, dob, ob, lse, bias, deps=()):
    def body(q_ref, k_ref, v_ref, d_ref, o_ref, l_ref, bias_ref, dqk_ref, dv_ref, drpb_ref, vb, dk_acc, dv_acc, a_acc):
        vb[...] = v_ref[...].astype(BF16)
        dk_acc[...] = jnp.zeros_like(dk_acc)
        dv_acc[...] = jnp.zeros_like(dv_acc)
        a_acc[...] = jnp.zeros_like(a_acc)

        def row(r, carry):
            start, off = _nbr_window(r)
            rows = pl.ds(pl.multiple_of(r * GRID_W, GRID_W), GRID_W)
            win = pl.ds(pl.multiple_of(start * GRID_W, GRID_W), 512)
            qr, kw, vw = q_ref[rows, :], k_ref[win, :], vb[win, :]
            s = _dot_nt(qr, kw) * SCALE + bias_ref[off]
            p = jnp.exp(s - _wide(l_ref[rows, :], 4))
            dov = d_ref[rows, :]
            delta = jnp.sum(dov * o_ref[rows, :], axis=-1, keepdims=True)
            do16 = dov.astype(BF16)
            ds = p * (_dot_nt(do16, vw) - delta)
            a_acc[off] += ds
            ds16 = ds.astype(BF16)
            dqk_ref[0, rows, :] = _dot(ds16, kw) * SCALE
            dk_acc[win, :] += _dot_tn(ds16, qr) * SCALE
            dv_acc[win, :] += _dot_tn(p.astype(BF16), do16)
            return carry

        lax.fori_loop(0, T // GRID_W, row, 0, unroll=16)
        dqk_ref[1] = dk_acc[...]
        dv_ref[...] = dv_acc[...]

        lane = lax.broadcasted_iota(jnp.int32, (16, HD), 1)
        rowi = lax.broadcasted_iota(jnp.int32, (16, HD), 0)
        low = (lane >= GRID_W - WIN_C) & (lane < GRID_W + WIN_C - 1)
        high = (lane >= HD - WIN_C) | (lane < WIN_C - 1)
        flip = (lax.broadcasted_iota(jnp.int32, (GRID_W, GRID_W), 0)
                + lax.broadcasted_iota(jnp.int32, (GRID_W, GRID_W), 1) == GRID_W - 1).astype(BF16)
        out = jnp.zeros((16, HD), F32)
        for d in range(14):
            acc = None
            for off in range(8):
                if 0 <= d - off <= 6 and (d - off) % 2 == 0:
                    jj = (d - off) // 2
                    piece = a_acc[off, :, jj * HD:(jj + 1) * HD]
                    acc = piece if acc is None else acc + piece
            hi = acc.astype(BF16)
            lo = (acc - hi.astype(F32)).astype(BF16)
            rev = _dot(flip, hi) + _dot(flip, lo)
            v = jnp.sum(pltpu.roll(rev, 0, 1, stride=1, stride_axis=0), axis=0, keepdims=True)
            v = jnp.broadcast_to(v, (16, HD))
            out = out + jnp.where((rowi == d) & low, v, 0.0)
            out = out + jnp.where(rowi == d + 1, pltpu.roll(jnp.where(high, v, 0.0), GRID_W, 1), 0.0)
        drpb_ref[...] = out

    blk = pl.BlockSpec((T, HD), lambda h: (0, h))
    return pl.pallas_call(
        _after(body, deps), name="attn_b_bwd",
        out_shape=[SDS((2, T, 512), F32), SDS((T, 512), F32), SDS((4, 16, HD), F32)], grid=(4,),
        in_specs=[DEP_SPEC] * len(deps) + [
            pl.BlockSpec((T, HD), lambda h: (0, NHA + h)),
            pl.BlockSpec((T, HD), lambda h: (0, NH + NHA + h)),
            pl.BlockSpec((T, HD), lambda h: (0, 2 * NH + NHA + h)), blk, blk, blk,
            pl.BlockSpec((None, 8, GRID_W, 512), lambda h: (h, 0, 0, 0))],
        out_specs=[pl.BlockSpec((2, T, HD), lambda h: (0, 0, h)), blk,
                   pl.BlockSpec((None, 16, HD), lambda h: (h, 0, 0))],
        scratch_shapes=[pltpu.VMEM((T, HD), BF16), pltpu.VMEM((T, HD), F32), pltpu.VMEM((T, HD), F32),
                        pltpu.VMEM((8, GRID_W, 512), F32)],
        compiler_params=_params(1))(*deps, qkn, qkn, proj, dob, ob, lse, bias)


def _qk_bwd(proj, nw, cos, sin, dqk_groups, dqk_b, dvs, dga, dgb):
    tm = 512

    def body(p_ref, w_ref, cos_ref, sin_ref, d0, d1, d2, d3, v0, v1, v2, v3, ga_ref, gb_ref, o_ref, dn_ref):
        j, i = pl.program_id(0), pl.program_id(1)

        @pl.when((j < 2) & (i == 0))
        def _():
            dn_ref[...] = jnp.zeros_like(dn_ref)

        @pl.when(j < 2)
        def _():
            cv, sv = cos_ref[...], sin_ref[...]
            srcs = (d0, d1, d2, d3)
            dna = jnp.zeros((1, HD), F32)
            dnb = jnp.zeros((1, HD), F32)
            for h in range(NH):
                sl = slice(h * HD, (h + 1) * HD)
                dz = srcs[h // 4][:, (h % 4) * HD:(h % 4 + 1) * HD]
                if h < NHA:
                    dz = dz * cv + pltpu.roll(dz * sv, 64, 1)
                dx, dg = _norm_bwd(p_ref[:, sl], dz, w_ref[:, sl])
                o_ref[:, sl] = dx.astype(BF16)
                if h < NHA:
                    dna += dg
                else:
                    dnb += dg
            dn_ref[0:1, :] += dna
            dn_ref[1:2, :] += dnb

        @pl.when(j == 2)
        def _():
            for s, v_ref in enumerate((v0, v1, v2, v3)):
                o_ref[:, s * 512:(s + 1) * 512] = v_ref[...].astype(BF16)

        @pl.when(j == 3)
        def _():
            o_ref[...] = ga_ref[...]

        @pl.when(j == 4)
        def _():
            o_ref[...] = gb_ref[...]

    def rows(used):
        return lambda j, i: (jnp.where(used(j), i, 0), 0)

    qk = lambda j: j < 2
    dspec = pl.BlockSpec((None, tm, 512), lambda j, i: (jnp.minimum(j, 1), jnp.where(j < 2, i, 0), 0))
    vspec = pl.BlockSpec((tm, 512), rows(lambda j: j == 2))
    return pl.pallas_call(
        body, name="qk_bwd", out_shape=[SDS((T, DIN), BF16), SDS((2, 8, HD), F32)], grid=(5, T // tm),
        in_specs=[pl.BlockSpec((tm, D), lambda j, i: (jnp.where(j < 2, i, 0), jnp.minimum(j, 1))),
                  pl.BlockSpec((None, 1, D), lambda j, i: (jnp.minimum(j, 1), 0, 0)),
                  pl.BlockSpec((tm, HD), rows(qk)), pl.BlockSpec((tm, HD), rows(qk)),
                  dspec, dspec, dspec, dspec, vspec, vspec, vspec, vspec,
                  pl.BlockSpec((tm, D), rows(lambda j: j == 3)), pl.BlockSpec((tm, D), rows(lambda j: j == 4))],
        out_specs=[pl.BlockSpec((tm, D), lambda j, i: (i, j)),
                   pl.BlockSpec((None, 8, HD), lambda j, i: (jnp.minimum(j, 1), 0, 0))],
        compiler_params=_params(2))(proj, nw, cos, sin, *dqk_groups, dqk_b, *dvs, dga, dgb)


def _in_proj_bwd(dproj, w_in, x, dh1, g, deps=()):
    tm, tk = 512, 1280
    per = (DIN // NSH) // tk
    nk = DIN // tk

    def body(dp_ref, w_ref, x_ref, dh_ref, g_ref, dx_ref, dg_ref, acc):
        i, k = pl.program_id(0), pl.program_id(1)

        @pl.when(k == 0)
        def _():
            acc[...] = jnp.zeros_like(acc)

        @pl.when((k == 0) & (i == 0))
        def _():
            dg_ref[...] = jnp.zeros_like(dg_ref)

        acc[...] += _dot_nt(dp_ref[...], w_ref[...])

        @pl.when(k == nk - 1)
        def _():
            dx, dg = _norm_bwd(x_ref[...], acc[...], g_ref[...])
            dx_ref[...] = dh_ref[...] + dx
            dg_ref[...] += dg

    row = pl.BlockSpec((tm, D), lambda i, k: (i, 0))
    vec = pl.BlockSpec((1, D), lambda i, k: (0, 0))
    return pl.pallas_call(
        _after(body, deps), name="in_proj_bwd", out_shape=[SDS((T, D), F32), SDS((1, D), F32)], grid=(T // tm, nk),
        in_specs=[DEP_SPEC] * len(deps) + [
            pl.BlockSpec((tm, tk), lambda i, k: (i, k)),
            pl.BlockSpec((None, D, tk), lambda i, k: (k // per, 0, k % per)), row, row, vec],
        out_specs=[row, vec], scratch_shapes=[pltpu.VMEM((tm, D), F32)],
        compiler_params=_params(2))(*deps, dproj, w_in, x, dh1, g)


def _grad_w(name, a, g, shard_rows, rows, cols, tr, tc):
    ni, nj = rows // tr, cols // tc
    if shard_rows:
        a_map, g_map = (lambda s, i, j: (0, s * ni + i)), (lambda s, i, j: (0, j))
    else:
        a_map, g_map = (lambda s, i, j: (0, i)), (lambda s, i, j: (0, s * nj + j))

    def body(a_ref, g_ref, o_ref):
        o_ref[...] = _dot_tn(a_ref[...], g_ref[...]).astype(BF16)

    return pl.pallas_call(
        body, name=name, out_shape=SDS((NSH, rows, cols), BF16), grid=(NSH, ni, nj),
        in_specs=[pl.BlockSpec((T, tr), a_map), pl.BlockSpec((T, tc), g_map)],
        out_specs=pl.BlockSpec((None, tr, tc), lambda s, i, j: (s, i, j)), compiler_params=_params(3))(a, g)


def _grad_w_in_half(name, xn, dproj, place, for_sibling, deps=()):
    tr, tc = D // 2, 1280
    nj = (DIN // NSH) // tc

    def body(*refs):
        a_ref, g_ref, o_ref = refs[-3:]
        o_ref[...] = _dot_tn(a_ref[...], g_ref[...]).astype(BF16)

    half = (lambda p: 1 - p[1]) if for_sibling else (lambda p: p[1])
    return pl.pallas_call(
        body, name=name, out_shape=SDS((NSH, tr, DIN // NSH), BF16),
        grid_spec=pltpu.PrefetchScalarGridSpec(
            num_scalar_prefetch=1, grid=(NSH, nj),
            in_specs=[DEP_SPEC] * len(deps) + [pl.BlockSpec((T, tr), lambda s, j, p: (0, half(p))),
                                               pl.BlockSpec((T, tc), lambda s, j, p: (0, s * nj + j))],
            out_specs=pl.BlockSpec((None, tr, tc), lambda s, j, p: (s, 0, j))),
        compiler_params=_params(2))(place, *deps, xn, dproj)


def _adamw(w, g, m, v):
    m = B1 * m + (1.0 - B1) * g
    v = B2 * v + (1.0 - B2) * (g * g)
    m_hat = m / (1.0 - B1 ** STEP)
    v_hat = v / (1.0 - B2 ** STEP)
    delta = -LR * (m_hat / (jnp.sqrt(v_hat) + AEPS) + WD * w)
    return delta, m, v


def _sum_halves(name, place, grads, theirs):
    _, rows, cols = theirs.shape
    tr = _row_tile(rows, cols, 1 << 20)

    def body(place_ref, a_ref, b_ref, o_ref):
        o_ref[...] = (a_ref[...].astype(F32) + b_ref[...].astype(F32)).astype(BF16)

    spec = pl.BlockSpec((None, tr, cols), lambda s, i, p: (s, i, 0))
    mine = spec if grads.ndim == 3 else pl.BlockSpec((None, None, tr, cols), lambda s, i, p: (s, p[1], i, 0))
    return pl.pallas_call(
        body, name=name, out_shape=SDS(theirs.shape, BF16),
        grid_spec=pltpu.PrefetchScalarGridSpec(
            num_scalar_prefetch=1, grid=(NSH, rows // tr), in_specs=[mine, spec], out_specs=spec),
        compiler_params=_params(2))(place, grads, theirs)


def _sum_landed(name, place, part, landed):
    _, rows, cols = part.shape
    tr = _row_tile(rows, cols, 1 << 20)

    def body(place_ref, p_ref, l_ref, o_ref):
        o_ref[...] = ((p_ref[...].astype(F32) + l_ref[0].astype(F32)) + l_ref[1].astype(F32)) + l_ref[2].astype(F32)

    return pl.pallas_call(
        body, name=name, out_shape=SDS((2, rows, cols), F32),
        grid_spec=pltpu.PrefetchScalarGridSpec(
            num_scalar_prefetch=1, grid=(rows // tr,),
            in_specs=[pl.BlockSpec((None, tr, cols), lambda i, p: (p[0], i, 0)),
                      pl.BlockSpec((3, tr, cols), lambda i, p: (0, i, 0))],
            out_specs=pl.BlockSpec((None, tr, cols), lambda i, p: (p[1], i, 0))),
        compiler_params=_params(1))(place, part, landed)


def _adam_shard(name, g, w, m, v):
    rows, cols = w.shape
    tr = _row_tile(rows, cols, 1 << 19)

    def body(g_ref, w_ref, m_ref, v_ref, go_ref, d_ref, nm_ref, nv_ref):
        g = g_ref[...]
        go_ref[...] = g
        d_ref[...], nm_ref[...], nv_ref[...] = _adamw(w_ref[...], g, m_ref[...], v_ref[...])

    spec = pl.BlockSpec((tr, cols), lambda i: (i, 0))
    return pl.pallas_call(
        body, name=name, out_shape=[SDS((rows, cols), F32)] * 4, grid=(rows // tr,),
        in_specs=[spec] * 4, out_specs=[spec] * 4, compiler_params=_params(1))(g, w, m, v)


def _adam_small(gathered, w, m, v):
    def body(g_ref, w_ref, m_ref, v_ref, go_ref, d_ref, nm_ref, nv_ref):
        g = g_ref[0:SMALL_ROWS, :]
        for dev in range(1, 8):
            g = g + g_ref[dev * SMALL_ROWS:(dev + 1) * SMALL_ROWS, :]
        go_ref[...] = g
        d_ref[...], nm_ref[...], nv_ref[...] = _adamw(w_ref[...], g, m_ref[...], v_ref[...])

    return pl.pallas_call(body, name="adam_small", out_shape=[SDS((SMALL_ROWS, HD), F32)] * 4)(gathered, w, m, v)


SMALL = (("norm_mix", (1, D)), ("b_gate", (1, 2 * D)), ("q_norm_a", (1, HD)), ("k_norm_a", (1, HD)),
         ("q_norm_b", (1, HD)), ("k_norm_b", (1, HD)), ("rpb_b", (1, 4, 15, 31)), ("norm_ffn", (1, D)))


def _pack_small(vals):
    pieces = []
    for (name, shape), val in zip(SMALL, vals):
        flat = val.reshape(-1)
        pad = (-flat.shape[0]) % HD
        pieces.append(jnp.pad(flat, (0, pad)).reshape(-1, HD))
    packed = jnp.concatenate(pieces, axis=0)
    return jnp.pad(packed, ((0, SMALL_ROWS - packed.shape[0]), (0, 0)))


def _unpack_small(packed):
    out, row = [], 0
    for name, shape in SMALL:
        size = int(np.prod(shape))
        nrows = -(-size // HD)
        out.append(packed[row:row + nrows].reshape(-1)[:size].reshape(shape))
        row += nrows
    return out


def kernel(x, norm_mix, w_in, b_gate, q_norm_a, k_norm_a, q_norm_b, k_norm_b, rpb_b, w_proj_a, w_proj_b, w_out, norm_ffn, w_up, w_down, loss_target, m_norm_mix, m_w_in, m_b_gate, m_q_norm_a, m_k_norm_a, m_q_norm_b, m_k_norm_b, m_rpb_b, m_w_proj_a, m_w_proj_b, m_w_out, m_norm_ffn, m_w_up, m_w_down, v_norm_mix, v_w_in, v_b_gate, v_q_norm_a, v_k_norm_a, v_q_norm_b, v_k_norm_b, v_rpb_b, v_w_proj_a, v_w_proj_b, v_w_out, v_norm_ffn, v_w_up, v_w_down):
    big_names = ("w_in", "w_proj_a", "w_proj_b", "w_out", "w_up", "w_down")
    big_w = [a[0] for a in (w_in, w_proj_a, w_proj_b, w_out, w_up, w_down)]
    big_m = [a[0] for a in (m_w_in, m_w_proj_a, m_w_proj_b, m_w_out, m_w_up, m_w_down)]
    big_v = [a[0] for a in (v_w_in, v_w_proj_a, v_w_proj_b, v_w_out, v_w_up, v_w_down)]
    x2, target = x[0], loss_target[0]

    place = jnp.stack([2 * lax.axis_index("x") + lax.axis_index("y"), lax.axis_index("c")]).astype(jnp.int32)
    groups = ((0,), (1, 2, 3), (4,), (5,))
    started = []
    for j, grp in enumerate(groups):
        deps = (started[0][4],) if j else ()
        placed = [_cast_into_place(big_w[i], "cast_" + big_names[i], place, deps) for i in grp]
        started.append(_gather_start(f"gather_start_{j}", placed))

    def whole(fulls):
        return [f.reshape(NSH, 2 * f.shape[2], f.shape[3]) for f in fulls]

    def forward_begin(j, after):
        send, recv, _, fulls, _ = started[j]
        fulls = _gather_wait(f"gather_wait_{j}", send, recv, fulls, after)
        send, recv, _, fulls, token = _forward_start(f"forward_start_{j}", fulls)
        return (send, recv, fulls), token

    def forward_end(j, state, after):
        return whole(_forward_wait(f"forward_wait_{j}", *state, after))

    def as_halves(grads):
        return [g.reshape(NSH, 2, g.shape[1] // 2, g.shape[2]) for g in grads]

    def reduce_start(j, grads, theirs):
        parts = [_sum_halves(f"sum_halves_{j}_{i}", place, a, b) for i, (a, b) in enumerate(zip(grads, theirs))]
        send, recv, parts, lands, token = _reduce_start(f"reduce_start_{j}", parts)
        return (send, recv, parts, lands), token

    def exchange_begin(j, grads):
        send, recv, grads, lands, token = _exchange_start(f"exchange_start_{j}", as_halves(grads))
        return (send, recv, grads, lands), token

    def exchange_end(j, state, after):
        return reduce_start(j, *_exchange_wait(f"exchange_wait_{j}", *state, after))

    big_out = {}

    def share_begin(j, state, after):
        send, recv, parts, lands = state
        parts, lands = _reduce_wait(f"reduce_wait_{j}", send, recv, parts, lands, after)
        sums = [_sum_landed(f"sum_landed_{j}_{i}", place, p, l) for i, (p, l) in enumerate(zip(parts, lands))]
        send, recv, _, sums, token = _share_start(f"share_start_{j}", sums)
        return (send, recv, sums), token

    def share_end(j, state, after):
        for idx, g in zip(groups[j], _share_wait(f"share_wait_{j}", *state, after)):
            g = g.reshape(big_w[idx].shape)
            big_out[idx] = _adam_shard("adam_" + big_names[idx], g, big_w[idx], big_m[idx], big_v[idx])
        return big_out[groups[j][-1]][1]

    proj, xn = _norm_in_proj_own(x2, norm_mix, whole(started[0][3])[0], place)
    send, recv, _, win, _ = started[0]
    win = _gather_wait("gather_wait_0", send, recv, win, (proj, *[s[4] for s in started[1:]]))
    (win_f,) = whole(_gather_finish("gather_finish_0", win))
    proj = _in_proj_rest(xn, win_f, proj, place)
    cos, sin = _rope_tables()
    nw = jnp.stack([jnp.concatenate([jnp.tile(q_norm_a, (1, NHA)), jnp.tile(q_norm_b, (1, NH - NHA))], axis=1),
                    jnp.concatenate([jnp.tile(k_norm_a, (1, NHA)), jnp.tile(k_norm_b, (1, NH - NHA))], axis=1)])
    qkn = _qk_prep(proj, nw, cos, sin)
    fw1, token = forward_begin(1, (qkn,))
    fwd_a = [_attn_a_fwd(qkn, proj, g) for g in range(3)]
    os, ls = [f[0] for f in fwd_a], [f[1] for f in fwd_a]
    fw2, token = forward_begin(2, (os[2], token))
    ob, lse_b, bias = _attn_b_fwd(qkn, proj, _rpb_rows(rpb_b[0]))
    oa, w0, w1, w2 = _comb_fwd(os, ls)
    ws = [w0, w1, w2]
    wpa_f, wpb_f, wout_f = forward_end(1, fw1, (oa, token))
    wout_f = wout_f.reshape(D, D)
    mixed, ob16 = _mix_fwd(oa, ob, proj, b_gate, wpa_f, wpb_f)
    h1, hn = _out_proj_fwd(mixed, wout_f, x2, norm_ffn)
    (wup_f,) = forward_end(2, fw2, (hn,))
    usq, u = _ffn_up(hn, wup_f)
    fw3, token = forward_begin(3, (u,))
    own = _ffn_down_own(usq, whole(fw3[2])[0].reshape(DFF, D), place)
    (wdown_f,) = forward_end(3, fw3, (own, token))
    wdown_f = wdown_f.reshape(DFF, D)
    dy, dy16, loss_parts = _ffn_down_loss(usq, wdown_f, h1, target, own, place)
    loss = lax.psum(jnp.sum(loss_parts[:, 0, 0]), ("x", "y", "c"))

    g_down = _grad_w("grad_w_down", usq, dy16, True, DFF // NSH, D, 1024, 1024)
    ex_down, token = exchange_begin(3, [g_down])
    du = _ffn_down_bwd(dy16, wdown_f, u, deps=(token,))
    g_up = _grad_w("grad_w_up", hn, du, False, D, DFF // NSH, 1024, 1024)
    red_down, token = exchange_end(3, ex_down, (g_up,))
    ex_up, token_up = exchange_begin(2, [g_up])
    dh1, dh16, d_norm_ffn = _ffn_up_bwd(du, wup_f, h1, dy, norm_ffn, deps=(token, token_up))
    dya, dyb, dga, dgb, doa, dob, dba, dbb = _mix_bwd(dh16, wout_f, oa, ob16, proj, b_gate, wpa_f, wpb_f)
    g_out = _grad_w("grad_w_out", mixed, dh16, True, D // NSH, D, 512, 1024)
    g_pa = _grad_w("grad_w_proj_a", oa, dya, False, 512, 512, 512, 512)
    g_pb = _grad_w("grad_w_proj_b", ob16, dyb, False, 512, 512, 512, 512)
    red_up, token = exchange_end(2, ex_up, (g_out,))
    ex_mid, token_mid = exchange_begin(1, [g_pa, g_pb, g_out])
    cc = _comb_bwd(doa, os, ws, deps=(token, token_mid))
    bwd_a = [_attn_a_bwd(qkn, proj, doa, ls[g], ws[g], cc, g) for g in range(3)]
    red_mid, token = exchange_end(1, ex_mid, (bwd_a[2][1],))
    dqk_b, dv_b, drpb_t = _attn_b_bwd(qkn, proj, dob, ob, lse_b, bias, deps=(token,))
    dproj, dn = _qk_bwd(proj, nw, cos, sin, [b[0] for b in bwd_a], dqk_b, [b[1] for b in bwd_a] + [dv_b], dga, dgb)
    g_in_theirs = _grad_w_in_half("grad_w_in_for_sibling", xn, dproj, place, True)
    send, recv, g_in_theirs, lands, token = _exchange_start("exchange_start_0", [g_in_theirs], sliced=False)
    g_in_mine = _grad_w_in_half("grad_w_in_own", xn, dproj, place, False, deps=(token,))
    _, theirs = _exchange_wait("exchange_wait_0", send, recv, g_in_theirs, lands, (g_in_mine,), sliced=False)
    red_in, token = reduce_start(0, [g_in_mine], theirs)
    grad_x, d_norm_mix = _in_proj_bwd(dproj, win_f, x2, dh1, norm_mix, deps=(token,))

    sh_down, token = share_begin(3, red_down, (grad_x,))
    sh_up, token = share_begin(2, red_up, (token,))
    done = share_end(3, sh_down, (token,))
    sh_mid, token = share_begin(1, red_mid, (done,))
    done = share_end(2, sh_up, (token,))
    sh_in, token = share_begin(0, red_in, (done,))
    done = share_end(1, sh_mid, (token,))
    done = share_end(0, sh_in, (done,))

    d_rpb = drpb_t[:, :15, GRID_W - WIN_C:GRID_W + WIN_C - 1]
    small_g = [d_norm_mix, jnp.concatenate([dba, dbb], axis=1), dn[0, 0], dn[1, 0], dn[0, 1], dn[1, 1], d_rpb, d_norm_ffn]
    gathered_small = _allgather_small(_pack_small(small_g), done)
    small_w = (norm_mix, b_gate, q_norm_a, k_norm_a, q_norm_b, k_norm_b, rpb_b, norm_ffn)
    small_m = (m_norm_mix, m_b_gate, m_q_norm_a, m_k_norm_a, m_q_norm_b, m_k_norm_b, m_rpb_b, m_norm_ffn)
    small_v = (v_norm_mix, v_b_gate, v_q_norm_a, v_k_norm_a, v_q_norm_b, v_k_norm_b, v_rpb_b, v_norm_ffn)
    small_out = [_unpack_small(p) for p in
                 _adam_small(gathered_small, _pack_small(small_w), _pack_small(small_m), _pack_small(small_v))]

    order = ("norm_mix", "w_in", "b_gate", "q_norm_a", "k_norm_a", "q_norm_b", "k_norm_b", "rpb_b",
             "w_proj_a", "w_proj_b", "w_out", "norm_ffn", "w_up", "w_down")
    small_idx = {name: i for i, (name, _) in enumerate(SMALL)}
    outs = []
    for kind in range(4):
        for name in order:
            if name in small_idx:
                outs.append(small_out[kind][small_idx[name]])
            else:
                outs.append(big_out[big_names.index(name)][kind][None])
    return (loss, grad_x[None], *outs)
```

```python
import numpy as np
import jax
import jax.numpy as jnp
from jax import lax
from jax.experimental import pallas as pl
from jax.experimental.pallas import tpu as pltpu

F32, BF16 = jnp.float32, jnp.bfloat16
SDS = jax.ShapeDtypeStruct
MESH = pl.DeviceIdType.MESH

T = 2048
D = 2048
HD = 128
NH, NHA = 16, 12
DIN = 10240
DFF = 8192
NSH = 4
DILS = (1, 4, 16)
EPS = 1e-6
NEG = -1e30
SCALE = HD ** -0.5
GRID_W, WIN_R, WIN_C = 64, 8, 16
VMEM_LIMIT = 56 * 1024 * 1024
B1, B2, LR, AEPS, WD, STEP = 0.9, 0.999, 0.001, 1e-08, 0.01, 10
SMALL_ROWS = 88


def _dot(a, b):
    return jnp.dot(a, b, preferred_element_type=F32)


def _dot_nt(a, b):
    return lax.dot_general(a, b, (((1,), (1,)), ((), ())), preferred_element_type=F32)


def _dot_tn(a, b):
    return lax.dot_general(a, b, (((0,), (0,)), ((), ())), preferred_element_type=F32)


def _params(n):
    return pltpu.CompilerParams(dimension_semantics=("arbitrary",) * n, vmem_limit_bytes=VMEM_LIMIT)


def _resident(shape, index_map):
    return pl.BlockSpec(shape, index_map, pipeline_mode=pl.Buffered(1))


def _sigmoid(z):
    return 1.0 / (1.0 + jnp.exp(-z))


def _wide(v, n):
    return jnp.concatenate([v] * n, axis=1)


def _row_tile(rows, cols, elems):
    tr = 16
    while tr * 2 <= rows and tr * 2 * cols <= elems:
        tr *= 2
    return tr


def _place():
    x, y, c = lax.axis_index("x"), lax.axis_index("y"), lax.axis_index("c")
    peers = [(1 - x, y), (x, 1 - y), (1 - x, 1 - y)]
    return x, y, c, peers


def _cast_into_place(w, name, place, deps=()):
    rows, cols = w.shape
    hr = rows // 2
    tr = min(hr, 256)
    per = hr // tr

    def body(*refs):
        w_ref, o_ref = refs[-2:]
        o_ref[...] = w_ref[...].astype(BF16)

    return pl.pallas_call(
        body, name=name, out_shape=SDS((NSH, 2, hr, cols), BF16),
        grid_spec=pltpu.PrefetchScalarGridSpec(
            num_scalar_prefetch=1, grid=(2, per),
            in_specs=[DEP_SPEC] * len(deps) + [pl.BlockSpec((tr, cols), lambda h, i, p: (h * per + i, 0))],
            out_specs=pl.BlockSpec((None, None, tr, cols), lambda h, i, p: (p[0], h, i, 0))),
        compiler_params=_params(2))(place, *deps, w)


ANY_SPEC = pl.BlockSpec(memory_space=pl.ANY)
HBM_SPEC = pl.BlockSpec(memory_space=pltpu.HBM)
SEM_SPEC = pl.BlockSpec(memory_space=pltpu.SEMAPHORE)
DEP_SPEC = pl.BlockSpec((8, 128), lambda *_: (0, 0))
EFFECT = pltpu.SideEffectType.DATAFLOW_SIDE_EFFECTING


def _after(body, deps):
    n = len(deps)
    return (lambda *refs: body(*refs[n:])) if n else body


SIBLING_BARRIER = 1


def _split_start(name, srcs, lands, n_copies, issue, sibling_only=False):
    n, m = len(srcs), len(lands)

    def body(*refs):
        if sibling_only:
            x, y, c, _ = _place()
            barrier = pltpu.get_barrier_semaphore()
            pl.semaphore_signal(barrier, inc=1, device_id=(x, y, 1 - c), device_id_type=MESH)
            pl.semaphore_wait(barrier, 1)
        issue(refs[:n], refs[n:n + m], refs[n + m], refs[n + m + 1])
        refs[-1][...] = jnp.zeros((8, 128), F32)

    arrays = list(srcs) + list(lands)
    outs = pl.pallas_call(
        body, name=name,
        out_shape=(pltpu.SemaphoreType.DMA((n_copies,)), pltpu.SemaphoreType.DMA((n_copies,)),
                   *[pltpu.HBM(a.shape, a.dtype) for a in arrays], SDS((8, 128), F32)),
        in_specs=[HBM_SPEC] * (n + m),
        out_specs=(SEM_SPEC, SEM_SPEC, *[HBM_SPEC] * (n + m), pl.BlockSpec(memory_space=pltpu.VMEM)),
        input_output_aliases={i: 2 + i for i in range(n + m)},
        compiler_params=pltpu.CompilerParams(has_side_effects=EFFECT,
                                             collective_id=SIBLING_BARRIER if sibling_only else None),
    )(*[pltpu.with_memory_space_constraint(a, pltpu.HBM) for a in arrays])
    return outs[0], outs[1], list(outs[2:2 + n]), list(outs[2 + n:2 + n + m]), outs[-1]


def _split_wait(name, send_sems, recv_sems, srcs, lands, after, wait):
    n, m = len(srcs), len(lands)

    def body(*refs):
        wait(refs[:n], refs[n:n + m], refs[n + m], refs[n + m + 1])

    arrays = list(srcs) + list(lands)
    outs = pl.pallas_call(
        body, name=name, out_shape=[pltpu.HBM(a.shape, a.dtype) for a in arrays],
        in_specs=[HBM_SPEC] * (n + m) + [SEM_SPEC, SEM_SPEC] + [ANY_SPEC] * len(after),
        out_specs=[HBM_SPEC] * (n + m), input_output_aliases={i: i for i in range(n + m)},
        compiler_params=pltpu.CompilerParams(has_side_effects=EFFECT),
    )(*arrays, send_sems, recv_sems, *after)
    return list(outs[:n]), list(outs[n:])


def _gather_start(name, fulls):
    def issue(srcs, dsts, send_sems, recv_sems):
        x, y, c, peers = _place()
        for i in range(len(fulls)):
            mine = dsts[i].at[2 * x + y, c]
            for k, (px, py) in enumerate(peers):
                pltpu.make_async_remote_copy(
                    src_ref=mine, dst_ref=mine, send_sem=send_sems.at[3 * i + k],
                    recv_sem=recv_sems.at[3 * i + k], device_id=(px, py, c), device_id_type=MESH).start()

    return _split_start(name, [], fulls, 3 * len(fulls), issue)


def _gather_wait(name, send_sems, recv_sems, fulls, after):
    def wait(srcs, dsts, send_sems, recv_sems):
        x, y, c, peers = _place()
        for i in range(len(fulls)):
            for k, (px, py) in enumerate(peers):
                cp = pltpu.make_async_remote_copy(
                    src_ref=dsts[i].at[2 * x + y, c], dst_ref=dsts[i].at[2 * px + py, c],
                    send_sem=send_sems.at[3 * i + k], recv_sem=recv_sems.at[3 * i + k],
                    device_id=(px, py, c), device_id_type=MESH)
                cp.wait_send()
                cp.wait_recv()

    return _split_wait(name, send_sems, recv_sems, [], fulls, after, wait)[1]


def _gather_finish(name, fulls):
    n = len(fulls)

    def body(*refs):
        fin, fout = refs[:n], refs[n:2 * n]
        send_sems, recv_sems = refs[2 * n:]
        x, y, c, peers = _place()

        def copy(i, k, half):
            px, py = peers[k]
            return pltpu.make_async_remote_copy(
                src_ref=fin[i].at[2 * px + py, half], dst_ref=fout[i].at[2 * px + py, half],
                send_sem=send_sems.at[3 * i + k], recv_sem=recv_sems.at[3 * i + k],
                device_id=(x, y, 1 - c), device_id_type=MESH)

        sends = [copy(i, k, c) for i in range(n) for k in range(3)]
        for cp in sends:
            cp.start()
        for i in range(n):
            for k in range(3):
                copy(i, k, 1 - c).wait_recv()
        for cp in sends:
            cp.wait_send()

    return pl.pallas_call(
        body, name=name, out_shape=[SDS(f.shape, f.dtype) for f in fulls],
        in_specs=[ANY_SPEC] * n, out_specs=[ANY_SPEC] * n, input_output_aliases={i: i for i in range(n)},
        scratch_shapes=[pltpu.SemaphoreType.DMA((3 * n,)), pltpu.SemaphoreType.DMA((3 * n,))])(*fulls)


def _reduce_start(name, parts):
    lands = [lax.empty((3,) + p.shape[1:], p.dtype) for p in parts]

    def issue(srcs, dsts, send_sems, recv_sems):
        x, y, c, peers = _place()
        for i in range(len(parts)):
            for k, (px, py) in enumerate(peers):
                pltpu.make_async_remote_copy(
                    src_ref=srcs[i].at[2 * px + py], dst_ref=dsts[i].at[k], send_sem=send_sems.at[3 * i + k],
                    recv_sem=recv_sems.at[3 * i + k], device_id=(px, py, c), device_id_type=MESH).start()

    return _split_start(name, parts, lands, 3 * len(parts), issue)


def _reduce_wait(name, send_sems, recv_sems, parts, lands, after):
    def wait(srcs, dsts, send_sems, recv_sems):
        x, y, c, peers = _place()
        for i in range(len(parts)):
            for k, (px, py) in enumerate(peers):
                cp = pltpu.make_async_remote_copy(
                    src_ref=srcs[i].at[2 * px + py], dst_ref=dsts[i].at[k], send_sem=send_sems.at[3 * i + k],
                    recv_sem=recv_sems.at[3 * i + k], device_id=(px, py, c), device_id_type=MESH)
                cp.wait_send()
                cp.wait_recv()

    return _split_wait(name, send_sems, recv_sems, parts, lands, after, wait)


def _sibling_copy(src, dst, send_sems, recv_sems, k):
    x, y, c, _ = _place()
    return pltpu.make_async_remote_copy(src_ref=src, dst_ref=dst, send_sem=send_sems.at[k], recv_sem=recv_sems.at[k],
                                        device_id=(x, y, 1 - c), device_id_type=MESH)


def _forward_start(name, fulls):
    def issue(srcs, dsts, send_sems, recv_sems):
        x, y, c, peers = _place()
        for i in range(len(fulls)):
            for k, (px, py) in enumerate(peers):
                part = dsts[i].at[2 * px + py, c]
                _sibling_copy(part, part, send_sems, recv_sems, 3 * i + k).start()

    return _split_start(name, [], fulls, 3 * len(fulls), issue, sibling_only=True)


def _forward_wait(name, send_sems, recv_sems, fulls, after):
    def wait(srcs, dsts, send_sems, recv_sems):
        x, y, c, peers = _place()
        for i in range(len(fulls)):
            for k, (px, py) in enumerate(peers):
                cp = _sibling_copy(dsts[i].at[2 * px + py, c], dsts[i].at[2 * px + py, 1 - c], send_sems, recv_sems, 3 * i + k)
                cp.wait_send()
                cp.wait_recv()

    return _split_wait(name, send_sems, recv_sems, [], fulls, after, wait)[1]


def _exchange_start(name, grads, sliced=True):
    lands = [lax.empty((NSH,) + g.shape[-2:], g.dtype) for g in grads]

    def issue(srcs, dsts, send_sems, recv_sems):
        c = lax.axis_index("c")
        for i in range(len(grads)):
            src = srcs[i].at[:, 1 - c] if sliced else srcs[i]
            _sibling_copy(src, dsts[i], send_sems, recv_sems, i).start()

    return _split_start(name, grads, lands, len(grads), issue, sibling_only=True)


def _exchange_wait(name, send_sems, recv_sems, grads, lands, after, sliced=True):
    def wait(srcs, dsts, send_sems, recv_sems):
        c = lax.axis_index("c")
        for i in range(len(grads)):
            cp = _sibling_copy(srcs[i].at[:, 1 - c] if sliced else srcs[i], dsts[i], send_sems, recv_sems, i)
            cp.wait_send()
            cp.wait_recv()

    return _split_wait(name, send_sems, recv_sems, grads, lands, after, wait)


def _share_start(name, sums):
    def issue(srcs, dsts, send_sems, recv_sems):
        c = lax.axis_index("c")
        for i in range(len(sums)):
            _sibling_copy(dsts[i].at[c], dsts[i].at[c], send_sems, recv_sems, i).start()

    return _split_start(name, [], sums, len(sums), issue, sibling_only=True)


def _share_wait(name, send_sems, recv_sems, sums, after):
    def wait(srcs, dsts, send_sems, recv_sems):
        c = lax.axis_index("c")
        for i in range(len(sums)):
            cp = _sibling_copy(dsts[i].at[c], dsts[i].at[1 - c], send_sems, recv_sems, i)
            cp.wait_send()
            cp.wait_recv()

    return _split_wait(name, send_sems, recv_sems, [], sums, after, wait)[1]


def _allgather_small(v, after):
    m_per, n = v.shape

    def body(x_ref, after_ref, out_ref, send_sems, recv_sems, local_sem):
        x, y, c = lax.axis_index("x"), lax.axis_index("y"), lax.axis_index("c")
        me, sibling = (x, y, c), (x, y, 1 - c)
        chips = [(1 - x, y), (x, 1 - y), (1 - x, 1 - y)]

        def rows(px, py, pc):
            return out_ref.at[pl.ds((4 * px + 2 * py + pc) * m_per, m_per), :]

        def copy(k, block, to, src=None):
            return pltpu.make_async_remote_copy(
                src_ref=rows(*block) if src is None else src, dst_ref=rows(*block),
                send_sem=send_sems.at[k], recv_sem=recv_sems.at[k], device_id=to, device_id_type=MESH)

        mine = pltpu.make_async_copy(x_ref, rows(*me), local_sem)
        mine.start()
        first = [copy(0, me, sibling, src=x_ref)]
        first += [copy(1 + j, me, (*chip, c), src=x_ref) for j, chip in enumerate(chips)]
        for cp in first:
            cp.start()
        passed = [copy(4 + j, (*chip, c), sibling) for j, chip in enumerate(chips)]
        for j, chip in enumerate(chips):
            copy(1 + j, (*chip, c), me).wait_recv()
            passed[j].start()
        copy(0, sibling, me).wait_recv()
        for j, chip in enumerate(chips):
            copy(4 + j, (*chip, 1 - c), me).wait_recv()
        for cp in first + passed:
            cp.wait_send()
        mine.wait()

    return pl.pallas_call(
        body, name="allgather_small", out_shape=SDS((8 * m_per, n), v.dtype),
        in_specs=[pl.BlockSpec(memory_space=pltpu.VMEM), ANY_SPEC], out_specs=pl.BlockSpec(memory_space=pltpu.VMEM),
        scratch_shapes=[pltpu.SemaphoreType.DMA((7,)), pltpu.SemaphoreType.DMA((7,)), pltpu.SemaphoreType.DMA])(v, after)


def _norm_in_proj_own(x, g, w_full, place):
    tn, chunk = 512, 256
    per = (DIN // NSH) // tn

    def body(place_ref, x_ref, g_ref, w_ref, proj_ref, xn_ref):
        @pl.when(pl.program_id(0) == 0)
        def _():
            def norm(r, carry):
                rows = pl.ds(pl.multiple_of(r * chunk, chunk), chunk)
                xv = x_ref[rows, :]
                rs = lax.rsqrt(jnp.mean(xv * xv, axis=-1, keepdims=True) + EPS)
                xn_ref[rows, :] = (xv * rs * g_ref[...]).astype(BF16)
                return carry

            lax.fori_loop(0, T // chunk, norm, 0)

        proj_ref[...] = _dot(xn_ref[...], w_ref[...])

    return pl.pallas_call(
        body, name="norm_in_proj_own", out_shape=[SDS((T, DIN), F32), SDS((T, D), BF16)],
        grid_spec=pltpu.PrefetchScalarGridSpec(
            num_scalar_prefetch=1, grid=(per,),
            in_specs=[_resident((T, D), lambda j, p: (0, 0)),
                      pl.BlockSpec((1, D), lambda j, p: (0, 0)),
                      pl.BlockSpec((None, D, tn), lambda j, p: (p[0], 0, j))],
            out_specs=[pl.BlockSpec((T, tn), lambda j, p: (0, p[0] * per + j)),
                       pl.BlockSpec((T, D), lambda j, p: (0, 0))]),
        compiler_params=_params(1))(place, x, g, w_full)


def _in_proj_rest(xn, w_full, proj, place):
    tn = 512
    per = (DIN // NSH) // tn

    def body(place_ref, xn_ref, w_ref, proj_in, proj_ref):
        proj_ref[...] = _dot(xn_ref[...], w_ref[...])

    shard = lambda j, p: p[0] ^ jnp.where(j < per, 2, jnp.where(j < 2 * per, 1, 3))
    return pl.pallas_call(
        body, name="in_proj_rest", out_shape=SDS((T, DIN), F32),
        grid_spec=pltpu.PrefetchScalarGridSpec(
            num_scalar_prefetch=1, grid=((NSH - 1) * per,),
            in_specs=[_resident((T, D), lambda j, p: (0, 0)),
                      pl.BlockSpec((None, D, tn), lambda j, p: (shard(j, p), 0, j % per)), ANY_SPEC],
            out_specs=pl.BlockSpec((T, tn), lambda j, p: (0, shard(j, p) * per + j % per))),
        input_output_aliases={3: 0}, compiler_params=_params(1))(place, xn, w_full, proj)


def _rope_tables():
    pos = np.arange(T, dtype=np.float32)
    inv = (10000.0 ** (-np.arange(0, HD, 2, dtype=np.float32) / HD)).astype(np.float32)
    ang = (pos[:, None] * inv[None, :]).astype(np.float32)
    cos, sin = np.cos(ang).astype(np.float32), np.sin(ang).astype(np.float32)
    return (jnp.asarray(np.concatenate([cos, cos], axis=1)), jnp.asarray(np.concatenate([-sin, sin], axis=1)))


def _qk_prep(proj, nw, cos, sin):
    tm = 256

    def body(p_ref, w_ref, cos_ref, sin_ref, o_ref):
        cv, sv = cos_ref[...], sin_ref[...]
        for h in range(NH):
            sl = slice(h * HD, (h + 1) * HD)
            xv = p_ref[:, sl]
            r = lax.rsqrt(jnp.mean(xv * xv, axis=-1, keepdims=True) + EPS)
            z = xv * r * w_ref[:, sl]
            if h < NHA:
                z = z * cv + pltpu.roll(z, 64, 1) * sv
            o_ref[:, sl] = z.astype(BF16)

    return pl.pallas_call(
        body, name="qk_prep", out_shape=SDS((T, 2 * D), BF16), grid=(T // tm, 2),
        in_specs=[pl.BlockSpec((tm, D), lambda i, j: (i, j)),
                  pl.BlockSpec((None, 1, D), lambda i, j: (j, 0, 0)),
                  pl.BlockSpec((tm, HD), lambda i, j: (i, 0)),
                  pl.BlockSpec((tm, HD), lambda i, j: (i, 0))],
        out_specs=pl.BlockSpec((tm, D), lambda i, j: (i, j)),
        compiler_params=_params(2))(proj, nw, cos, sin)


def _band_mask(q0, m):
    ii = lax.broadcasted_iota(jnp.int32, (128, 256), 0)
    jj = lax.broadcasted_iota(jnp.int32, (128, 256), 1)
    rel = jj - ii
    kpos = jj + (q0 - 64)
    return (rel >= 0) & (rel <= 128) & (kpos >= 0) & (kpos < m)


def _fill_padded(dst, src, m):
    zeros = jnp.zeros((64, HD), dst.dtype)
    dst[0:64, :] = zeros
    dst[64 + m:128 + m, :] = zeros
    dst[64:64 + m, :] = src.astype(dst.dtype)


def _residue_rows(r, m, dil):
    return pl.ds(r, m, stride=dil) if dil > 1 else slice(None)


def _head_blocks(g):
    col = lambda base: pl.BlockSpec((T, HD), lambda h: (0, base + g * 4 + h))
    return col(0), col(NH), col(2 * NH), pl.BlockSpec((T, HD), lambda h: (0, h))


def _attn_a_fwd(qkn, proj, g):
    dil = DILS[g]
    m = T // dil
    nb = m // 128

    def body(q_ref, k_ref, v_ref, o_ref, l_ref, qf, kf, qp, kp, vp, ob, lb):
        qf[...] = q_ref[...].astype(F32)
        kf[...] = k_ref[...].astype(F32)
        for r in range(dil):
            rows = _residue_rows(r, m, dil)
            qp[...] = qf[rows, :].astype(BF16)
            _fill_padded(kp, kf[rows, :], m)
            _fill_padded(vp, v_ref[rows, :], m)

            def block(b, carry):
                q0 = pl.multiple_of(b * 128, 128)
                kw, vw = kp[pl.ds(q0, 256), :], vp[pl.ds(q0, 256), :]
                s = _dot_nt(qp[pl.ds(q0, 128), :], kw) * SCALE
                s = jnp.where(_band_mask(q0, m), s, NEG)
                mx = jnp.max(s, axis=-1, keepdims=True)
                e = jnp.exp(s - mx)
                den = jnp.sum(e, axis=-1, keepdims=True)
                ob[pl.ds(q0, 128), :] = _dot((e / den).astype(BF16), vw)
                lb[pl.ds(q0, 128), :] = jnp.broadcast_to(mx + jnp.log(den), (128, HD))
                return carry

            lax.fori_loop(0, nb, block, 0, unroll=min(nb, 16))
            o_ref[rows, :] = ob[...]
            l_ref[rows, :] = lb[...]

    q_blk, k_blk, v_blk, out_blk = _head_blocks(g)
    return pl.pallas_call(
        body, name=f"attn_a_fwd_{g}", out_shape=[SDS((T, 512), F32)] * 2, grid=(4,),
        in_specs=[q_blk, k_blk, v_blk], out_specs=[out_blk] * 2,
        scratch_shapes=[pltpu.VMEM((T, HD), F32), pltpu.VMEM((T, HD), F32), pltpu.VMEM((m, HD), BF16),
                        pltpu.VMEM((m + 128, HD), BF16), pltpu.VMEM((m + 128, HD), BF16),
                        pltpu.VMEM((m, HD), F32), pltpu.VMEM((m, HD), F32)],
        compiler_params=_params(1))(qkn, qkn, proj)


def _nbr_window(r):
    start = jnp.clip(r - WIN_R // 2, 0, T // GRID_W - WIN_R)
    return start, start - r + (WIN_R - 1)


def _rpb_rows(rpb):
    zeros = jnp.zeros((4, 14, 33), F32)
    a, b = rpb[:, :14], rpb[:, 1:15]
    rows = jnp.concatenate([a[:, :, 15:31], zeros, b, zeros, a[:, :, 0:15]], axis=2)
    return jnp.pad(rows, ((0, 0), (0, 2), (0, 0)))


def _attn_b_fwd(qkn, proj, rpb_rows):
    def body(r_ref, q_ref, k_ref, v_ref, o_ref, l_ref, bias_ref, vb, pair):
        qc = lax.broadcasted_iota(jnp.int32, (GRID_W, 512), 0)
        kc = lax.broadcasted_iota(jnp.int32, (GRID_W, 512), 1) & (GRID_W - 1)
        cs = jnp.clip(qc - WIN_C // 2, 0, GRID_W - WIN_C)
        colmask = (kc >= cs) & (kc < cs + WIN_C)
        for d in range(14):
            pair[d] = pltpu.roll(jnp.broadcast_to(r_ref[d:d + 1, :], (GRID_W, HD)), 0, 1, stride=1, stride_axis=0)
        for off in range(8):
            rows = jnp.concatenate([pair[off + 2 * jj] for jj in range(4)], axis=1)
            bias_ref[off] = jnp.where(colmask, rows, NEG)
        vb[...] = v_ref[...].astype(BF16)

        def row(r, carry):
            start, off = _nbr_window(r)
            q0 = pl.multiple_of(r * GRID_W, GRID_W)
            k0 = pl.multiple_of(start * GRID_W, GRID_W)
            s = _dot_nt(q_ref[pl.ds(q0, GRID_W), :], k_ref[pl.ds(k0, 512), :]) * SCALE + bias_ref[off]
            mx = jnp.max(s, axis=-1, keepdims=True)
            e = jnp.exp(s - mx)
            den = jnp.sum(e, axis=-1, keepdims=True)
            o_ref[pl.ds(q0, GRID_W), :] = _dot((e / den).astype(BF16), vb[pl.ds(k0, 512), :])
            l_ref[pl.ds(q0, GRID_W), :] = jnp.broadcast_to(mx + jnp.log(den), (GRID_W, HD))
            return carry

        lax.fori_loop(0, T // GRID_W, row, 0, unroll=16)

    return pl.pallas_call(
        body, name="attn_b_fwd",
        out_shape=[SDS((T, 512), F32), SDS((T, 512), F32), SDS((4, 8, GRID_W, 512), F32)], grid=(4,),
        in_specs=[pl.BlockSpec((None, 16, HD), lambda h: (h, 0, 0)),
                  pl.BlockSpec((T, HD), lambda h: (0, NHA + h)),
                  pl.BlockSpec((T, HD), lambda h: (0, NH + NHA + h)),
                  pl.BlockSpec((T, HD), lambda h: (0, 2 * NH + NHA + h))],
        out_specs=[pl.BlockSpec((T, HD), lambda h: (0, h)), pl.BlockSpec((T, HD), lambda h: (0, h)),
                   pl.BlockSpec((None, 8, GRID_W, 512), lambda h: (h, 0, 0, 0))],
        scratch_shapes=[pltpu.VMEM((T, HD), BF16), pltpu.VMEM((14, GRID_W, HD), F32)],
        compiler_params=_params(1))(rpb_rows, qkn, qkn, proj)


def _comb_fwd(os, ls):
    tm = 512

    def body(o0, o1, o2, l0, l1, l2, oa_ref, w0, w1, w2):
        lv = [l0[...], l1[...], l2[...]]
        mx = jnp.maximum(jnp.maximum(lv[0], lv[1]), lv[2])
        ev = [jnp.exp(l - mx) for l in lv]
        den = ev[0] + ev[1] + ev[2]
        wv = [e / den for e in ev]
        oa_ref[...] = (wv[0] * o0[...] + wv[1] * o1[...] + wv[2] * o2[...]).astype(BF16)
        w0[...], w1[...], w2[...] = wv

    spec = pl.BlockSpec((tm, 512), lambda i: (i, 0))
    return pl.pallas_call(
        body, name="comb_fwd", out_shape=[SDS((T, 512), BF16)] + [SDS((T, 512), F32)] * 3, grid=(T // tm,),
        in_specs=[spec] * 6, out_specs=[spec] * 4, compiler_params=_params(1))(*os, *ls)


def _mix_fwd(oa, ob, proj, b_gate, wpa, wpb):
    tm = 512

    def body(oa_ref, ob_ref, ga_ref, gb_ref, ba_ref, bb_ref, wpa_ref, wpb_ref, mixed_ref, ob16_ref):
        oav = oa_ref[...]
        obv = ob_ref[...].astype(BF16)
        ob16_ref[...] = obv
        for s in range(NSH):
            sl = slice(s * 512, (s + 1) * 512)
            ga = _sigmoid(ga_ref[:, sl] + ba_ref[:, sl])
            gb = _sigmoid(gb_ref[:, sl] + bb_ref[:, sl])
            mixed_ref[:, sl] = (ga * _dot(oav, wpa_ref[s]) + gb * _dot(obv, wpb_ref[s])).astype(BF16)

    row = lambda w: pl.BlockSpec((tm, w), lambda i: (i, 0))
    return pl.pallas_call(
        body, name="mix_fwd", out_shape=[SDS((T, D), BF16), SDS((T, 512), BF16)], grid=(T // tm,),
        in_specs=[row(512), row(512),
                  pl.BlockSpec((tm, D), lambda i: (i, 3)), pl.BlockSpec((tm, D), lambda i: (i, 4)),
                  pl.BlockSpec((1, D), lambda i: (0, 0)), pl.BlockSpec((1, D), lambda i: (0, 1)),
                  _resident((NSH, 512, 512), lambda i: (0, 0, 0)), _resident((NSH, 512, 512), lambda i: (0, 0, 0))],
        out_specs=[row(D), row(512)], compiler_params=_params(1))(oa, ob, proj, proj, b_gate, b_gate, wpa, wpb)


def _out_proj_fwd(mixed, w_out, x, g):
    tm = 512

    def body(m_ref, w_ref, x_ref, g_ref, h1_ref, hn_ref):
        h1 = x_ref[...] + _dot(m_ref[...], w_ref[...])
        h1_ref[...] = h1
        r = lax.rsqrt(jnp.mean(h1 * h1, axis=-1, keepdims=True) + EPS)
        hn_ref[...] = (h1 * r * g_ref[...]).astype(BF16)

    row = pl.BlockSpec((tm, D), lambda i: (i, 0))
    return pl.pallas_call(
        body, name="out_proj_fwd", out_shape=[SDS((T, D), F32), SDS((T, D), BF16)], grid=(T // tm,),
        in_specs=[row, _resident((D, D), lambda i: (0, 0)), row, pl.BlockSpec((1, D), lambda i: (0, 0))],
        out_specs=[row, row], compiler_params=_params(1))(mixed, w_out, x, g)


def _ffn_up(hn, w_up):
    tm, tn = T, 512
    per = (DFF // NSH) // tn

    def body(h_ref, w_ref, a_ref, u_ref):
        uv = jnp.maximum(_dot(h_ref[...], w_ref[...]), 0.0)
        a_ref[...] = (uv * uv).astype(BF16)
        u_ref[...] = uv.astype(BF16)

    out = pl.BlockSpec((tm, tn), lambda i, j: (i, j))
    return pl.pallas_call(
        body, name="ffn_up", out_shape=[SDS((T, DFF), BF16)] * 2, grid=(T // tm, DFF // tn),
        in_specs=[pl.BlockSpec((tm, D), lambda i, j: (i, 0)),
                  pl.BlockSpec((None, D, tn), lambda i, j: (j // per, 0, j % per))],
        out_specs=[out, out], compiler_params=_params(2))(hn, w_up)


def _ffn_down_own(u, w_down, place):
    tm, tk = 512, DFF // NSH

    def body(place_ref, u_ref, w_ref, o_ref):
        o_ref[...] = _dot(u_ref[...], w_ref[...])

    return pl.pallas_call(
        body, name="ffn_down_own", out_shape=SDS((T, D), F32),
        grid_spec=pltpu.PrefetchScalarGridSpec(
            num_scalar_prefetch=1, grid=(T // tm,),
            in_specs=[pl.BlockSpec((tm, tk), lambda i, p: (i, p[0])), pl.BlockSpec((tk, D), lambda i, p: (p[0], 0))],
            out_specs=pl.BlockSpec((tm, D), lambda i, p: (i, 0))),
        compiler_params=_params(1))(place, u, w_down)


def _ffn_down_loss(u, w_down, h1, target, own, place):
    tm, tk = 512, DFF // NSH
    nk = NSH - 1

    def body(place_ref, u_ref, w_ref, h1_ref, t_ref, own_ref, dy_ref, dy16_ref, loss_ref, acc):
        k = pl.program_id(1)

        @pl.when(k == 0)
        def _():
            acc[...] = own_ref[...]

        acc[...] += _dot(u_ref[...], w_ref[...])

        @pl.when(k == nk - 1)
        def _():
            def chunk(r, sq):
                rows = pl.ds(pl.multiple_of(r * 16, 16), 16)
                err = acc[rows, :] + h1_ref[rows, :] - t_ref[rows, :]
                dy = err * (1.0 / D)
                dy_ref[rows, :] = dy
                dy16_ref[rows, :] = dy.astype(BF16)
                return sq + err * err

            sq = lax.fori_loop(0, tm // 16, chunk, jnp.zeros((16, D), F32), unroll=2)
            part = 0.5 * jnp.sum(jnp.mean(sq, axis=-1, keepdims=True), axis=0, keepdims=True)
            loss_ref[...] = jnp.broadcast_to(part, (8, 128))

    row = pl.BlockSpec((tm, D), lambda i, k, p: (i, 0))
    once = _resident((tm, D), lambda i, k, p: (i, 0))
    shard = lambda k, p: p[0] ^ (k + 1)
    return pl.pallas_call(
        body, name="ffn_down_loss",
        out_shape=[SDS((T, D), F32), SDS((T, D), BF16), SDS((T // tm, 8, 128), F32)],
        grid_spec=pltpu.PrefetchScalarGridSpec(
            num_scalar_prefetch=1, grid=(T // tm, nk),
            in_specs=[pl.BlockSpec((tm, tk), lambda i, k, p: (i, shard(k, p))),
                      pl.BlockSpec((tk, D), lambda i, k, p: (shard(k, p), 0)), once, once, once],
            out_specs=[row, row, pl.BlockSpec((None, 8, 128), lambda i, k, p: (i, 0, 0))],
            scratch_shapes=[pltpu.VMEM((tm, D), F32)]),
        compiler_params=_params(2))(place, u, w_down, h1, target, own)


def _ffn_down_bwd(dy16, w_down, u, deps=()):
    tm, tn = T, 512

    def body(dy_ref, w_ref, u_ref, du_ref):
        uv = u_ref[...].astype(F32)
        du_ref[...] = jnp.where(uv > 0.0, 2.0 * uv * _dot_nt(dy_ref[...], w_ref[...]), 0.0).astype(BF16)

    return pl.pallas_call(
        _after(body, deps), name="ffn_down_bwd", out_shape=SDS((T, DFF), BF16), grid=(T // tm, DFF // tn),
        in_specs=[DEP_SPEC] * len(deps) + [
            pl.BlockSpec((tm, D), lambda i, j: (i, 0)), pl.BlockSpec((tn, D), lambda i, j: (j, 0)),
            pl.BlockSpec((tm, tn), lambda i, j: (i, j))],
        out_specs=pl.BlockSpec((tm, tn), lambda i, j: (i, j)), compiler_params=_params(2))(*deps, dy16, w_down, u)


def _norm_bwd(xv, dz_in, g):
    r = lax.rsqrt(jnp.mean(xv * xv, axis=-1, keepdims=True) + EPS)
    dg = jnp.sum(xv * r * dz_in, axis=0, keepdims=True)
    dz = dz_in * g
    dx = r * dz - xv * (r * r * r) * jnp.mean(xv * dz, axis=-1, keepdims=True)
    return dx, dg


def _ffn_up_bwd(du, w_up, h1, dy, g, deps=()):
    tm, tk = 512, 1024
    per = (DFF // NSH) // tk
    nk = DFF // tk

    def body(du_ref, w_ref, h1_ref, dy_ref, g_ref, dh1_ref, dh16_ref, dg_ref, acc):
        i, k = pl.program_id(0), pl.program_id(1)

        @pl.when(k == 0)
        def _():
            acc[...] = jnp.zeros_like(acc)

        @pl.when((k == 0) & (i == 0))
        def _():
            dg_ref[...] = jnp.zeros_like(dg_ref)

        acc[...] += _dot_nt(du_ref[...], w_ref[...])

        @pl.when(k == nk - 1)
        def _():
            dx, dg = _norm_bwd(h1_ref[...], acc[...], g_ref[...])
            dh1 = dy_ref[...] + dx
            dh1_ref[...] = dh1
            dh16_ref[...] = dh1.astype(BF16)
            dg_ref[...] += dg

    row = pl.BlockSpec((tm, D), lambda i, k: (i, 0))
    vec = pl.BlockSpec((1, D), lambda i, k: (0, 0))
    return pl.pallas_call(
        _after(body, deps), name="ffn_up_bwd", out_shape=[SDS((T, D), F32), SDS((T, D), BF16), SDS((1, D), F32)],
        grid=(T // tm, nk),
        in_specs=[DEP_SPEC] * len(deps) + [
            pl.BlockSpec((tm, tk), lambda i, k: (i, k)),
            pl.BlockSpec((None, D, tk), lambda i, k: (k // per, 0, k % per)), row, row, vec],
        out_specs=[row, row, vec], scratch_shapes=[pltpu.VMEM((tm, D), F32)],
        compiler_params=_params(2))(*deps, du, w_up, h1, dy, g)


def _mix_bwd(dh16, w_out, oa, ob16, proj, b_gate, wpa, wpb):
    tm = 256

    def body(dh_ref, wo_ref, oa_ref, ob_ref, ga_ref, gb_ref, ba_ref, bb_ref, wpa_ref, wpb_ref,
             dya_ref, dyb_ref, dga_ref, dgb_ref, doa_ref, dob_ref, dba_ref, dbb_ref):
        @pl.when(pl.program_id(0) == 0)
        def _():
            dba_ref[...] = jnp.zeros_like(dba_ref)
            dbb_ref[...] = jnp.zeros_like(dbb_ref)

        oav, obv = oa_ref[...], ob_ref[...]
        doa = jnp.zeros((tm, 512), F32)
        dob = jnp.zeros((tm, 512), F32)
        for s in range(NSH):
            sl = slice(s * 512, (s + 1) * 512)
            dm = _dot_nt(dh_ref[...], wo_ref[sl, :])
            ga = _sigmoid(ga_ref[:, sl] + ba_ref[:, sl])
            gb = _sigmoid(gb_ref[:, sl] + bb_ref[:, sl])
            dya = (dm * ga).astype(BF16)
            dyb = (dm * gb).astype(BF16)
            dza = dm * _dot(oav, wpa_ref[s]) * ga * (1.0 - ga)
            dzb = dm * _dot(obv, wpb_ref[s]) * gb * (1.0 - gb)
            dya_ref[:, sl], dyb_ref[:, sl] = dya, dyb
            dga_ref[:, sl], dgb_ref[:, sl] = dza.astype(BF16), dzb.astype(BF16)
            dba_ref[:, sl] += jnp.sum(dza, axis=0, keepdims=True)
            dbb_ref[:, sl] += jnp.sum(dzb, axis=0, keepdims=True)
            doa += _dot_nt(dya, wpa_ref[s])
            dob += _dot_nt(dyb, wpb_ref[s])
        doa_ref[...], dob_ref[...] = doa, dob

    row = lambda w: pl.BlockSpec((tm, w), lambda i: (i, 0))
    vec = pl.BlockSpec((1, D), lambda i: (0, 0))
    wp = _resident((NSH, 512, 512), lambda i: (0, 0, 0))
    return pl.pallas_call(
        body, name="mix_bwd",
        out_shape=[SDS((T, D), BF16)] * 4 + [SDS((T, 512), F32)] * 2 + [SDS((1, D), F32)] * 2, grid=(T // tm,),
        in_specs=[row(D), _resident((D, D), lambda i: (0, 0)), row(512), row(512),
                  pl.BlockSpec((tm, D), lambda i: (i, 3)), pl.BlockSpec((tm, D), lambda i: (i, 4)),
                  pl.BlockSpec((1, D), lambda i: (0, 0)), pl.BlockSpec((1, D), lambda i: (0, 1)), wp, wp],
        out_specs=[row(D)] * 4 + [row(512)] * 2 + [vec] * 2,
        compiler_params=_params(1))(dh16, w_out, oa, ob16, proj, proj, b_gate, b_gate, wpa, wpb)


def _comb_bwd(doa, os, ws, deps=()):
    tm = 512

    def body(d_ref, o0, o1, o2, w0, w1, w2, cc_ref):
        prod = d_ref[...] * (w0[...] * o0[...] + w1[...] * o1[...] + w2[...] * o2[...])
        for h in range(4):
            sl = slice(h * HD, (h + 1) * HD)
            cc_ref[:, sl] = jnp.broadcast_to(jnp.sum(prod[:, sl], axis=-1, keepdims=True), (tm, HD))

    spec = pl.BlockSpec((tm, 512), lambda i: (i, 0))
    return pl.pallas_call(
        _after(body, deps), name="comb_bwd", out_shape=SDS((T, 512), F32), grid=(T // tm,),
        in_specs=[DEP_SPEC] * len(deps) + [spec] * 7, out_specs=spec,
        compiler_params=_params(1))(*deps, doa, *os, *ws)


def _attn_a_bwd(qkn, proj, doa, lse, w, cc, g):
    dil = DILS[g]
    m = T // dil
    nb = m // 128

    def body(q_ref, k_ref, v_ref, d_ref, l_ref, w_ref, c_ref, dqk_ref, dv_ref,
             qf, kf, qp, kp, vp, dp, lp, wsub, cp, dqb, dkp, dvp):
        qf[...] = q_ref[...].astype(F32)
        kf[...] = k_ref[...].astype(F32)
        for r in range(dil):
            sub = _residue_rows(r, m, dil)
            qp[...] = qf[sub, :].astype(BF16)
            _fill_padded(kp, kf[sub, :], m)
            _fill_padded(vp, v_ref[sub, :], m)
            dp[...] = d_ref[sub, :].astype(BF16)
            lp[...], wsub[...], cp[...] = l_ref[sub, :], w_ref[sub, :], c_ref[sub, :]
            dkp[...] = jnp.zeros_like(dkp)
            dvp[...] = jnp.zeros_like(dvp)

            def block(b, carry):
                q0 = pl.multiple_of(b * 128, 128)
                rows = pl.ds(q0, 128)
                win = pl.ds(q0, 256)
                qb, kw, vw = qp[rows, :], kp[win, :], vp[win, :]
                s = _dot_nt(qb, kw) * SCALE
                s = jnp.where(_band_mask(q0, m), s, NEG)
                wp = _wide(wsub[rows, :], 2) * jnp.exp(s - _wide(lp[rows, :], 2))
                dob = dp[rows, :]
                ds = (wp * (_dot_nt(dob, vw) - _wide(cp[rows, :], 2))).astype(BF16)
                dqb[rows, :] = _dot(ds, kw) * SCALE
                dkp[win, :] += _dot_tn(ds, qb) * SCALE
                dvp[win, :] += _dot_tn(wp.astype(BF16), dob)
                return carry

            lax.fori_loop(0, nb, block, 0, unroll=min(nb, 16))
            dqk_ref.at[0][sub, :] = dqb[...]
            dqk_ref.at[1][sub, :] = dkp[64:64 + m, :]
            dv_ref[sub, :] = dvp[64:64 + m, :]

    q_blk, k_blk, v_blk, blk = _head_blocks(g)
    sub16 = pltpu.VMEM((m, HD), BF16)
    sub32 = pltpu.VMEM((m, HD), F32)
    return pl.pallas_call(
        body, name=f"attn_a_bwd_{g}", out_shape=[SDS((2, T, 512), F32), SDS((T, 512), F32)], grid=(4,),
        in_specs=[q_blk, k_blk, v_blk, blk, blk, blk, blk],
        out_specs=[pl.BlockSpec((2, T, HD), lambda h: (0, 0, h)), blk],
        scratch_shapes=[pltpu.VMEM((T, HD), F32), pltpu.VMEM((T, HD), F32), sub16,
                        pltpu.VMEM((m + 128, HD), BF16), pltpu.VMEM((m + 128, HD), BF16), sub16,
                        sub32, sub32, sub32, sub32,
                        pltpu.VMEM((m + 128, HD), F32), pltpu.VMEM((m + 128, HD), F32)],
        compiler_params=_params(1))(qkn, qkn, proj, doa, lse, w, cc)


def _attn_b_bwd(qkn, proj, dob, ob, lse, bias, deps=()):
    def body(q_ref, k_ref, v_ref, d_ref, o_ref, l_ref, bias_ref, dqk_ref, dv_ref, drpb_ref, vb, dk_acc, dv_acc, a_acc):
        vb[...] = v_ref[...].astype(BF16)
        dk_acc[...] = jnp.zeros_like(dk_acc)
        dv_acc[...] = jnp.zeros_like(dv_acc)
        a_acc[...] = jnp.zeros_like(a_acc)

        def row(r, carry):
            start, off = _nbr_window(r)
            rows = pl.ds(pl.multiple_of(r * GRID_W, GRID_W), GRID_W)
            win = pl.ds(pl.multiple_of(start * GRID_W, GRID_W), 512)
            qr, kw, vw = q_ref[rows, :], k_ref[win, :], vb[win, :]
            s = _dot_nt(qr, kw) * SCALE + bias_ref[off]
            p = jnp.exp(s - _wide(l_ref[rows, :], 4))
            dov = d_ref[rows, :]
            delta = jnp.sum(dov * o_ref[rows, :], axis=-1, keepdims=True)
            do16 = dov.astype(BF16)
            ds = p * (_dot_nt(do16, vw) - delta)
            a_acc[off] += ds
            ds16 = ds.astype(BF16)
            dqk_ref[0, rows, :] = _dot(ds16, kw) * SCALE
            dk_acc[win, :] += _dot_tn(ds16, qr) * SCALE
            dv_acc[win, :] += _dot_tn(p.astype(BF16), do16)
            return carry

        lax.fori_loop(0, T // GRID_W, row, 0, unroll=16)
        dqk_ref[1] = dk_acc[...]
        dv_ref[...] = dv_acc[...]

        lane = lax.broadcasted_iota(jnp.int32, (16, HD), 1)
        rowi = lax.broadcasted_iota(jnp.int32, (16, HD), 0)
        low = (lane >= GRID_W - WIN_C) & (lane < GRID_W + WIN_C - 1)
        high = (lane >= HD - WIN_C) | (lane < WIN_C - 1)
        flip = (lax.broadcasted_iota(jnp.int32, (GRID_W, GRID_W), 0)
                + lax.broadcasted_iota(jnp.int32, (GRID_W, GRID_W), 1) == GRID_W - 1).astype(BF16)
        out = jnp.zeros((16, HD), F32)
        for d in range(14):
            acc = None
            for off in range(8):
                if 0 <= d - off <= 6 and (d - off) % 2 == 0:
                    jj = (d - off) // 2
                    piece = a_acc[off, :, jj * HD:(jj + 1) * HD]
                    acc = piece if acc is None else acc + piece
            hi = acc.astype(BF16)
            lo = (acc - hi.astype(F32)).astype(BF16)
            rev = _dot(flip, hi) + _dot(flip, lo)
            v = jnp.sum(pltpu.roll(rev, 0, 1, stride=1, stride_axis=0), axis=0, keepdims=True)
            v = jnp.broadcast_to(v, (16, HD))
            out = out + jnp.where((rowi == d) & low, v, 0.0)
            out = out + jnp.where(rowi == d + 1, pltpu.roll(jnp.where(high, v, 0.0), GRID_W, 1), 0.0)
        drpb_ref[...] = out

    blk = pl.BlockSpec((T, HD), lambda h: (0, h))
    return pl.pallas_call(
        _after(body, deps), name="attn_b_bwd",
        out_shape=[SDS((2, T, 512), F32), SDS((T, 512), F32), SDS((4, 16, HD), F32)], grid=(4,),
        in_specs=[DEP_SPEC] * len(deps) + [
            pl.BlockSpec((T, HD), lambda h: (0, NHA + h)),
            pl.BlockSpec((T, HD), lambda h: (0, NH + NHA + h)),
            pl.BlockSpec((T, HD), lambda h: (0, 2 * NH + NHA + h)), blk, blk, blk,
            pl.BlockSpec((None, 8, GRID_W, 512), lambda h: (h, 0, 0, 0))],
        out_specs=[pl.BlockSpec((2, T, HD), lambda h: (0, 0, h)), blk,
                   pl.BlockSpec((None, 16, HD), lambda h: (h, 0, 0))],
        scratch_shapes=[pltpu.VMEM((T, HD), BF16), pltpu.VMEM((T, HD), F32), pltpu.VMEM((T, HD), F32),
                        pltpu.VMEM((8, GRID_W, 512), F32)],
        compiler_params=_params(1))(*deps, qkn, qkn, proj, dob, ob, lse, bias)


def _qk_bwd(proj, nw, cos, sin, dqk_groups, dqk_b, dvs, dga, dgb):
    tm = 512

    def body(p_ref, w_ref, cos_ref, sin_ref, d0, d1, d2, d3, v0, v1, v2, v3, ga_ref, gb_ref, o_ref, dn_ref):
        j, i = pl.program_id(0), pl.program_id(1)

        @pl.when((j < 2) & (i == 0))
        def _():
            dn_ref[...] = jnp.zeros_like(dn_ref)

        @pl.when(j < 2)
        def _():
            cv, sv = cos_ref[...], sin_ref[...]
            srcs = (d0, d1, d2, d3)
            dna = jnp.zeros((1, HD), F32)
            dnb = jnp.zeros((1, HD), F32)
            for h in range(NH):
                sl = slice(h * HD, (h + 1) * HD)
                dz = srcs[h // 4][:, (h % 4) * HD:(h % 4 + 1) * HD]
                if h < NHA:
                    dz = dz * cv + pltpu.roll(dz * sv, 64, 1)
                dx, dg = _norm_bwd(p_ref[:, sl], dz, w_ref[:, sl])
                o_ref[:, sl] = dx.astype(BF16)
                if h < NHA:
                    dna += dg
                else:
                    dnb += dg
            dn_ref[0:1, :] += dna
            dn_ref[1:2, :] += dnb

        @pl.when(j == 2)
        def _():
            for s, v_ref in enumerate((v0, v1, v2, v3)):
                o_ref[:, s * 512:(s + 1) * 512] = v_ref[...].astype(BF16)

        @pl.when(j == 3)
        def _():
            o_ref[...] = ga_ref[...]

        @pl.when(j == 4)
        def _():
            o_ref[...] = gb_ref[...]

    def rows(used):
        return lambda j, i: (jnp.where(used(j), i, 0), 0)

    qk = lambda j: j < 2
    dspec = pl.BlockSpec((None, tm, 512), lambda j, i: (jnp.minimum(j, 1), jnp.where(j < 2, i, 0), 0))
    vspec = pl.BlockSpec((tm, 512), rows(lambda j: j == 2))
    return pl.pallas_call(
        body, name="qk_bwd", out_shape=[SDS((T, DIN), BF16), SDS((2, 8, HD), F32)], grid=(5, T // tm),
        in_specs=[pl.BlockSpec((tm, D), lambda j, i: (jnp.where(j < 2, i, 0), jnp.minimum(j, 1))),
                  pl.BlockSpec((None, 1, D), lambda j, i: (jnp.minimum(j, 1), 0, 0)),
                  pl.BlockSpec((tm, HD), rows(qk)), pl.BlockSpec((tm, HD), rows(qk)),
                  dspec, dspec, dspec, dspec, vspec, vspec, vspec, vspec,
                  pl.BlockSpec((tm, D), rows(lambda j: j == 3)), pl.BlockSpec((tm, D), rows(lambda j: j == 4))],
        out_specs=[pl.BlockSpec((tm, D), lambda j, i: (i, j)),
                   pl.BlockSpec((None, 8, HD), lambda j, i: (jnp.minimum(j, 1), 0, 0))],
        compiler_params=_params(2))(proj, nw, cos, sin, *dqk_groups, dqk_b, *dvs, dga, dgb)


def _in_proj_bwd(dproj, w_in, x, dh1, g, deps=()):
    tm, tk = 512, 1280
    per = (DIN // NSH) // tk
    nk = DIN // tk

    def body(dp_ref, w_ref, x_ref, dh_ref, g_ref, dx_ref, dg_ref, acc):
        i, k = pl.program_id(0), pl.program_id(1)

        @pl.when(k == 0)
        def _():
            acc[...] = jnp.zeros_like(acc)

        @pl.when((k == 0) & (i == 0))
        def _():
            dg_ref[...] = jnp.zeros_like(dg_ref)

        acc[...] += _dot_nt(dp_ref[...], w_ref[...])

        @pl.when(k == nk - 1)
        def _():
            dx, dg = _norm_bwd(x_ref[...], acc[...], g_ref[...])
            dx_ref[...] = dh_ref[...] + dx
            dg_ref[...] += dg

    row = pl.BlockSpec((tm, D), lambda i, k: (i, 0))
    vec = pl.BlockSpec((1, D), lambda i, k: (0, 0))
    return pl.pallas_call(
        _after(body, deps), name="in_proj_bwd", out_shape=[SDS((T, D), F32), SDS((1, D), F32)], grid=(T // tm, nk),
        in_specs=[DEP_SPEC] * len(deps) + [
            pl.BlockSpec((tm, tk), lambda i, k: (i, k)),
            pl.BlockSpec((None, D, tk), lambda i, k: (k // per, 0, k % per)), row, row, vec],
        out_specs=[row, vec], scratch_shapes=[pltpu.VMEM((tm, D), F32)],
        compiler_params=_params(2))(*deps, dproj, w_in, x, dh1, g)


def _grad_w(name, a, g, shard_rows, rows, cols, tr, tc):
    ni, nj = rows // tr, cols // tc
    if shard_rows:
        a_map, g_map = (lambda s, i, j: (0, s * ni + i)), (lambda s, i, j: (0, j))
    else:
        a_map, g_map = (lambda s, i, j: (0, i)), (lambda s, i, j: (0, s * nj + j))

    def body(a_ref, g_ref, o_ref):
        o_ref[...] = _dot_tn(a_ref[...], g_ref[...]).astype(BF16)

    return pl.pallas_call(
        body, name=name, out_shape=SDS((NSH, rows, cols), BF16), grid=(NSH, ni, nj),
        in_specs=[pl.BlockSpec((T, tr), a_map), pl.BlockSpec((T, tc), g_map)],
        out_specs=pl.BlockSpec((None, tr, tc), lambda s, i, j: (s, i, j)), compiler_params=_params(3))(a, g)


def _grad_w_in_half(name, xn, dproj, place, for_sibling, deps=()):
    tr, tc = D // 2, 1280
    nj = (DIN // NSH) // tc

    def body(*refs):
        a_ref, g_ref, o_ref = refs[-3:]
        o_ref[...] = _dot_tn(a_ref[...], g_ref[...]).astype(BF16)

    half = (lambda p: 1 - p[1]) if for_sibling else (lambda p: p[1])
    return pl.pallas_call(
        body, name=name, out_shape=SDS((NSH, tr, DIN // NSH), BF16),
        grid_spec=pltpu.PrefetchScalarGridSpec(
            num_scalar_prefetch=1, grid=(NSH, nj),
            in_specs=[DEP_SPEC] * len(deps) + [pl.BlockSpec((T, tr), lambda s, j, p: (0, half(p))),
                                               pl.BlockSpec((T, tc), lambda s, j, p: (0, s * nj + j))],
            out_specs=pl.BlockSpec((None, tr, tc), lambda s, j, p: (s, 0, j))),
        compiler_params=_params(2))(place, *deps, xn, dproj)


def _adamw(w, g, m, v):
    m = B1 * m + (1.0 - B1) * g
    v = B2 * v + (1.0 - B2) * (g * g)
    m_hat = m / (1.0 - B1 ** STEP)
    v_hat = v / (1.0 - B2 ** STEP)
    delta = -LR * (m_hat / (jnp.sqrt(v_hat) + AEPS) + WD * w)
    return delta, m, v


def _sum_halves(name, place, grads, theirs):
    _, rows, cols = theirs.shape
    tr = _row_tile(rows, cols, 1 << 20)

    def body(place_ref, a_ref, b_ref, o_ref):
        o_ref[...] = (a_ref[...].astype(F32) + b_ref[...].astype(F32)).astype(BF16)

    spec = pl.BlockSpec((None, tr, cols), lambda s, i, p: (s, i, 0))
    mine = spec if grads.ndim == 3 else pl.BlockSpec((None, None, tr, cols), lambda s, i, p: (s, p[1], i, 0))
    return pl.pallas_call(
        body, name=name, out_shape=SDS(theirs.shape, BF16),
        grid_spec=pltpu.PrefetchScalarGridSpec(
            num_scalar_prefetch=1, grid=(NSH, rows // tr), in_specs=[mine, spec], out_specs=spec),
        compiler_params=_params(2))(place, grads, theirs)


def _sum_landed(name, place, part, landed):
    _, rows, cols = part.shape
    tr = _row_tile(rows, cols, 1 << 20)

    def body(place_ref, p_ref, l_ref, o_ref):
        o_ref[...] = ((p_ref[...].astype(F32) + l_ref[0].astype(F32)) + l_ref[1].astype(F32)) + l_ref[2].astype(F32)

    return pl.pallas_call(
        body, name=name, out_shape=SDS((2, rows, cols), F32),
        grid_spec=pltpu.PrefetchScalarGridSpec(
            num_scalar_prefetch=1, grid=(rows // tr,),
            in_specs=[pl.BlockSpec((None, tr, cols), lambda i, p: (p[0], i, 0)),
                      pl.BlockSpec((3, tr, cols), lambda i, p: (0, i, 0))],
            out_specs=pl.BlockSpec((None, tr, cols), lambda i, p: (p[1], i, 0))),
        compiler_params=_params(1))(place, part, landed)


def _adam_shard(name, g, w, m, v):
    rows, cols = w.shape
    tr = _row_tile(rows, cols, 1 << 19)

    def body(g_ref, w_ref, m_ref, v_ref, go_ref, d_ref, nm_ref, nv_ref):
        g = g_ref[...]
        go_ref[...] = g
        d_ref[...], nm_ref[...], nv_ref[...] = _adamw(w_ref[...], g, m_ref[...], v_ref[...])

    spec = pl.BlockSpec((tr, cols), lambda i: (i, 0))
    return pl.pallas_call(
        body, name=name, out_shape=[SDS((rows, cols), F32)] * 4, grid=(rows // tr,),
        in_specs=[spec] * 4, out_specs=[spec] * 4, compiler_params=_params(1))(g, w, m, v)


def _adam_small(gathered, w, m, v):
    def body(g_ref, w_ref, m_ref, v_ref, go_ref, d_ref, nm_ref, nv_ref):
        g = g_ref[0:SMALL_ROWS, :]
        for dev in range(1, 8):
            g = g + g_ref[dev * SMALL_ROWS:(dev + 1) * SMALL_ROWS, :]
        go_ref[...] = g
        d_ref[...], nm_ref[...], nv_ref[...] = _adamw(w_ref[...], g, m_ref[...], v_ref[...])

    return pl.pallas_call(body, name="adam_small", out_shape=[SDS((SMALL_ROWS, HD), F32)] * 4)(gathered, w, m, v)


SMALL = (("norm_mix", (1, D)), ("b_gate", (1, 2 * D)), ("q_norm_a", (1, HD)), ("k_norm_a", (1, HD)),
         ("q_norm_b", (1, HD)), ("k_norm_b", (1, HD)), ("rpb_b", (1, 4, 15, 31)), ("norm_ffn", (1, D)))


def _pack_small(vals):
    pieces = []
    for (name, shape), val in zip(SMALL, vals):
        flat = val.reshape(-1)
        pad = (-flat.shape[0]) % HD
        pieces.append(jnp.pad(flat, (0, pad)).reshape(-1, HD))
    packed = jnp.concatenate(pieces, axis=0)
    return jnp.pad(packed, ((0, SMALL_ROWS - packed.shape[0]), (0, 0)))


def _unpack_small(packed):
    out, row = [], 0
    for name, shape in SMALL:
        size = int(np.prod(shape))
        nrows = -(-size // HD)
        out.append(packed[row:row + nrows].reshape(-1)[:size].reshape(shape))
        row += nrows
    return out


def kernel(x, norm_mix, w_in, b_gate, q_norm_a, k_norm_a, q_norm_b, k_norm_b, rpb_b, w_proj_a, w_proj_b, w_out, norm_ffn, w_up, w_down, loss_target, m_norm_mix, m_w_in, m_b_gate, m_q_norm_a, m_k_norm_a, m_q_norm_b, m_k_norm_b, m_rpb_b, m_w_proj_a, m_w_proj_b, m_w_out, m_norm_ffn, m_w_up, m_w_down, v_norm_mix, v_w_in, v_b_gate, v_q_norm_a, v_k_norm_a, v_q_norm_b, v_k_norm_b, v_rpb_b, v_w_proj_a, v_w_proj_b, v_w_out, v_norm_ffn, v_w_up, v_w_down):
    big_names = ("w_in", "w_proj_a", "w_proj_b", "w_out", "w_up", "w_down")
    big_w = [a[0] for a in (w_in, w_proj_a, w_proj_b, w_out, w_up, w_down)]
    big_m = [a[0] for a in (m_w_in, m_w_proj_a, m_w_proj_b, m_w_out, m_w_up, m_w_down)]
    big_v = [a[0] for a in (v_w_in, v_w_proj_a, v_w_proj_b, v_w_out, v_w_up, v_w_down)]
    x2, target = x[0], loss_target[0]

    place = jnp.stack([2 * lax.axis_index("x") + lax.axis_index("y"), lax.axis_index("c")]).astype(jnp.int32)
    groups = ((0,), (1, 2, 3), (4,), (5,))
    started = []
    for j, grp in enumerate(groups):
        deps = (started[0][4],) if j else ()
        placed = [_cast_into_place(big_w[i], "cast_" + big_names[i], place, deps) for i in grp]
        started.append(_gather_start(f"gather_start_{j}", placed))

    def whole(fulls):
        return [f.reshape(NSH, 2 * f.shape[2], f.shape[3]) for f in fulls]

    def forward_begin(j, after):
        send, recv, _, fulls, _ = started[j]
        fulls = _gather_wait(f"gather_wait_{j}", send, recv, fulls, after)
        send, recv, _, fulls, token = _forward_start(f"forward_start_{j}", fulls)
        return (send, recv, fulls), token

    def forward_end(j, state, after):
        return whole(_forward_wait(f"forward_wait_{j}", *state, after))

    def as_halves(grads):
        return [g.reshape(NSH, 2, g.shape[1] // 2, g.shape[2]) for g in grads]

    def reduce_start(j, grads, theirs):
        parts = [_sum_halves(f"sum_halves_{j}_{i}", place, a, b) for i, (a, b) in enumerate(zip(grads, theirs))]
        send, recv, parts, lands, token = _reduce_start(f"reduce_start_{j}", parts)
        return (send, recv, parts, lands), token

    def exchange_begin(j, grads):
        send, recv, grads, lands, token = _exchange_start(f"exchange_start_{j}", as_halves(grads))
        return (send, recv, grads, lands), token

    def exchange_end(j, state, after):
        return reduce_start(j, *_exchange_wait(f"exchange_wait_{j}", *state, after))

    big_out = {}

    def share_begin(j, state, after):
        send, recv, parts, lands = state
        parts, lands = _reduce_wait(f"reduce_wait_{j}", send, recv, parts, lands, after)
        sums = [_sum_landed(f"sum_landed_{j}_{i}", place, p, l) for i, (p, l) in enumerate(zip(parts, lands))]
        send, recv, _, sums, token = _share_start(f"share_start_{j}", sums)
        return (send, recv, sums), token

    def share_end(j, state, after):
        for idx, g in zip(groups[j], _share_wait(f"share_wait_{j}", *state, after)):
            g = g.reshape(big_w[idx].shape)
            big_out[idx] = _adam_shard("adam_" + big_names[idx], g, big_w[idx], big_m[idx], big_v[idx])
        return big_out[groups[j][-1]][1]

    proj, xn = _norm_in_proj_own(x2, norm_mix, whole(started[0][3])[0], place)
    send, recv, _, win, _ = started[0]
    win = _gather_wait("gather_wait_0", send, recv, win, (proj, *[s[4] for s in started[1:]]))
    (win_f,) = whole(_gather_finish("gather_finish_0", win))
    proj = _in_proj_rest(xn, win_f, proj, place)
    cos, sin = _rope_tables()
    nw = jnp.stack([jnp.concatenate([jnp.tile(q_norm_a, (1, NHA)), jnp.tile(q_norm_b, (1, NH - NHA))], axis=1),
                    jnp.concatenate([jnp.tile(k_norm_a, (1, NHA)), jnp.tile(k_norm_b, (1, NH - NHA))], axis=1)])
    qkn = _qk_prep(proj, nw, cos, sin)
    fw1, token = forward_begin(1, (qkn,))
    fwd_a = [_attn_a_fwd(qkn, proj, g) for g in range(3)]
    os, ls = [f[0] for f in fwd_a], [f[1] for f in fwd_a]
    fw2, token = forward_begin(2, (os[2], token))
    ob, lse_b, bias = _attn_b_fwd(qkn, proj, _rpb_rows(rpb_b[0]))
    oa, w0, w1, w2 = _comb_fwd(os, ls)
    ws = [w0, w1, w2]
    wpa_f, wpb_f, wout_f = forward_end(1, fw1, (oa, token))
    wout_f = wout_f.reshape(D, D)
    mixed, ob16 = _mix_fwd(oa, ob, proj, b_gate, wpa_f, wpb_f)
    h1, hn = _out_proj_fwd(mixed, wout_f, x2, norm_ffn)
    (wup_f,) = forward_end(2, fw2, (hn,))
    usq, u = _ffn_up(hn, wup_f)
    fw3, token = forward_begin(3, (u,))
    own = _ffn_down_own(usq, whole(fw3[2])[0].reshape(DFF, D), place)
    (wdown_f,) = forward_end(3, fw3, (own, token))
    wdown_f = wdown_f.reshape(DFF, D)
    dy, dy16, loss_parts = _ffn_down_loss(usq, wdown_f, h1, target, own, place)
    loss = lax.psum(jnp.sum(loss_parts[:, 0, 0]), ("x", "y", "c"))

    g_down = _grad_w("grad_w_down", usq, dy16, True, DFF // NSH, D, 1024, 1024)
    ex_down, token = exchange_begin(3, [g_down])
    du = _ffn_down_bwd(dy16, wdown_f, u, deps=(token,))
    g_up = _grad_w("grad_w_up", hn, du, False, D, DFF // NSH, 1024, 1024)
    red_down, token = exchange_end(3, ex_down, (g_up,))
    ex_up, token_up = exchange_begin(2, [g_up])
    dh1, dh16, d_norm_ffn = _ffn_up_bwd(du, wup_f, h1, dy, norm_ffn, deps=(token, token_up))
    dya, dyb, dga, dgb, doa, dob, dba, dbb = _mix_bwd(dh16, wout_f, oa, ob16, proj, b_gate, wpa_f, wpb_f)
    g_out = _grad_w("grad_w_out", mixed, dh16, True, D // NSH, D, 512, 1024)
    g_pa = _grad_w("grad_w_proj_a", oa, dya, False, 512, 512, 512, 512)
    g_pb = _grad_w("grad_w_proj_b", ob16, dyb, False, 512, 512, 512, 512)
    red_up, token = exchange_end(2, ex_up, (g_out,))
    ex_mid, token_mid = exchange_begin(1, [g_pa, g_pb, g_out])
    cc = _comb_bwd(doa, os, ws, deps=(token, token_mid))
    bwd_a = [_attn_a_bwd(qkn, proj, doa, ls[g], ws[g], cc, g) for g in range(3)]
    red_mid, token = exchange_end(1, ex_mid, (bwd_a[2][1],))
    dqk_b, dv_b, drpb_t = _attn_b_bwd(qkn, proj, dob, ob, lse_b, bias, deps=(token,))
    dproj, dn = _qk_bwd(proj, nw, cos, sin, [b[0] for b in bwd_a], dqk_b, [b[1] for b in bwd_a] + [dv_b], dga, dgb)
    g_in_theirs = _grad_w_in_half("grad_w_in_for_sibling", xn, dproj, place, True)
    send, recv, g_in_theirs, lands, token = _exchange_start("exchange_start_0", [g_in_theirs], sliced=False)
    g_in_mine = _grad_w_in_half("grad_w_in_own", xn, dproj, place, False, deps=(token,))
    _, theirs = _exchange_wait("exchange_wait_0", send, recv, g_in_theirs, lands, (g_in_mine,), sliced=False)
    red_in, token = reduce_start(0, [g_in_mine], theirs)
    grad_x, d_norm_mix = _in_proj_bwd(dproj, win_f, x2, dh1, norm_mix, deps=(token,))

    sh_down, token = share_begin(3, red_down, (grad_x,))
    sh_up, token = share_begin(2, red_up, (token,))
    done = share_end(3, sh_down, (token,))
    sh_mid, token = share_begin(1, red_mid, (done,))
    done = share_end(2, sh_up, (token,))
    sh_in, token = share_begin(0, red_in, (done,))
    done = share_end(1, sh_mid, (token,))
    done = share_end(0, sh_in, (done,))

    d_rpb = drpb_t[:, :15, GRID_W - WIN_C:GRID_W + WIN_C - 1]
    small_g = [d_norm_mix, jnp.concatenate([dba, dbb], axis=1), dn[0, 0], dn[1, 0], dn[0, 1], dn[1, 1], d_rpb, d_norm_ffn]
    gathered_small = _allgather_small(_pack_small(small_g), done)
    small_w = (norm_mix, b_gate, q_norm_a, k_norm_a, q_norm_b, k_norm_b, rpb_b, norm_ffn)
    small_m = (m_norm_mix, m_b_gate, m_q_norm_a, m_k_norm_a, m_q_norm_b, m_k_norm_b, m_rpb_b, m_norm_ffn)
    small_v = (v_norm_mix, v_b_gate, v_q_norm_a, v_k_norm_a, v_q_norm_b, v_k_norm_b, v_rpb_b, v_norm_ffn)
    small_out = [_unpack_small(p) for p in
                 _adam_small(gathered_small, _pack_small(small_w), _pack_small(small_m), _pack_small(small_v))]

    order = ("norm_mix", "w_in", "b_gate", "q_norm_a", "k_norm_a", "q_norm_b", "k_norm_b", "rpb_b",
             "w_proj_a", "w_proj_b", "w_out", "norm_ffn", "w_up", "w_down")
    small_idx = {name: i for i, (name, _) in enumerate(SMALL)}
    outs = []
    for kind in range(4):
        for name in order:
            if name in small_idx:
                outs.append(small_out[kind][small_idx[name]])
            else:
                outs.append(big_out[big_names.index(name)][kind][None])
    return (loss, grad_x[None], *outs)
```

```python
import numpy as np
import jax
import jax.numpy as jnp
from jax import lax
from jax.experimental import pallas as pl
from jax.experimental.pallas import tpu as pltpu

F32, BF16 = jnp.float32, jnp.bfloat16
SDS = jax.ShapeDtypeStruct
MESH = pl.DeviceIdType.MESH

T = 2048
D = 2048
HD = 128
NH, NHA = 16, 12
DIN = 10240
DFF = 8192
NSH = 4
DILS = (1, 4, 16)
EPS = 1e-6
NEG = -1e30
SCALE = HD ** -0.5
GRID_W, WIN_R, WIN_C = 64, 8, 16
VMEM_LIMIT = 56 * 1024 * 1024
B1, B2, LR, AEPS, WD, STEP = 0.9, 0.999, 0.001, 1e-08, 0.01, 10
SMALL_ROWS = 88


def _dot(a, b):
    return jnp.dot(a, b, preferred_element_type=F32)


def _dot_nt(a, b):
    return lax.dot_general(a, b, (((1,), (1,)), ((), ())), preferred_element_type=F32)


def _dot_tn(a, b):
    return lax.dot_general(a, b, (((0,), (0,)), ((), ())), preferred_element_type=F32)


def _params(n):
    return pltpu.CompilerParams(dimension_semantics=("arbitrary",) * n, vmem_limit_bytes=VMEM_LIMIT)


def _resident(shape, index_map):
    return pl.BlockSpec(shape, index_map, pipeline_mode=pl.Buffered(1))


def _sigmoid(z):
    return 1.0 / (1.0 + jnp.exp(-z))


def _wide(v, n):
    return jnp.concatenate([v] * n, axis=1)


def _row_tile(rows, cols, elems):
    tr = 16
    while tr * 2 <= rows and tr * 2 * cols <= elems:
        tr *= 2
    return tr


def _place():
    x, y, c = lax.axis_index("x"), lax.axis_index("y"), lax.axis_index("c")
    peers = [(1 - x, y), (x, 1 - y), (1 - x, 1 - y)]
    return x, y, c, peers


def _cast_into_place(w, name, place, deps=()):
    rows, cols = w.shape
    hr = rows // 2
    tr = min(hr, 256)
    per = hr // tr

    def body(*refs):
        w_ref, o_ref = refs[-2:]
        o_ref[...] = w_ref[...].astype(BF16)

    return pl.pallas_call(
        body, name=name, out_shape=SDS((NSH, 2, hr, cols), BF16),
        grid_spec=pltpu.PrefetchScalarGridSpec(
            num_scalar_prefetch=1, grid=(2, per),
            in_specs=[DEP_SPEC] * len(deps) + [pl.BlockSpec((tr, cols), lambda h, i, p: (h * per + i, 0))],
            out_specs=pl.BlockSpec((None, None, tr, cols), lambda h, i, p: (p[0], h, i, 0))),
        compiler_params=_params(2))(place, *deps, w)


ANY_SPEC = pl.BlockSpec(memory_space=pl.ANY)
HBM_SPEC = pl.BlockSpec(memory_space=pltpu.HBM)
SEM_SPEC = pl.BlockSpec(memory_space=pltpu.SEMAPHORE)
DEP_SPEC = pl.BlockSpec((8, 128), lambda *_: (0, 0))
EFFECT = pltpu.SideEffectType.DATAFLOW_SIDE_EFFECTING


def _after(body, deps):
    n = len(deps)
    return (lambda *refs: body(*refs[n:])) if n else body


SIBLING_BARRIER = 1


def _split_start(name, srcs, lands, n_copies, issue, sibling_only=False):
    n, m = len(srcs), len(lands)

    def body(*refs):
        if sibling_only:
            x, y, c, _ = _place()
            barrier = pltpu.get_barrier_semaphore()
            pl.semaphore_signal(barrier, inc=1, device_id=(x, y, 1 - c), device_id_type=MESH)
            pl.semaphore_wait(barrier, 1)
        issue(refs[:n], refs[n:n + m], refs[n + m], refs[n + m + 1])
        refs[-1][...] = jnp.zeros((8, 128), F32)

    arrays = list(srcs) + list(lands)
    outs = pl.pallas_call(
        body, name=name,
        out_shape=(pltpu.SemaphoreType.DMA((n_copies,)), pltpu.SemaphoreType.DMA((n_copies,)),
                   *[pltpu.HBM(a.shape, a.dtype) for a in arrays], SDS((8, 128), F32)),
        in_specs=[HBM_SPEC] * (n + m),
        out_specs=(SEM_SPEC, SEM_SPEC, *[HBM_SPEC] * (n + m), pl.BlockSpec(memory_space=pltpu.VMEM)),
        input_output_aliases={i: 2 + i for i in range(n + m)},
        compiler_params=pltpu.CompilerParams(has_side_effects=EFFECT,
                                             collective_id=SIBLING_BARRIER if sibling_only else None),
    )(*[pltpu.with_memory_space_constraint(a, pltpu.HBM) for a in arrays])
    return outs[0], outs[1], list(outs[2:2 + n]), list(outs[2 + n:2 + n + m]), outs[-1]


def _split_wait(name, send_sems, recv_sems, srcs, lands, after, wait):
    n, m = len(srcs), len(lands)

    def body(*refs):
        wait(refs[:n], refs[n:n + m], refs[n + m], refs[n + m + 1])

    arrays = list(srcs) + list(lands)
    outs = pl.pallas_call(
        body, name=name, out_shape=[pltpu.HBM(a.shape, a.dtype) for a in arrays],
        in_specs=[HBM_SPEC] * (n + m) + [SEM_SPEC, SEM_SPEC] + [ANY_SPEC] * len(after),
        out_specs=[HBM_SPEC] * (n + m), input_output_aliases={i: i for i in range(n + m)},
        compiler_params=pltpu.CompilerParams(has_side_effects=EFFECT),
    )(*arrays, send_sems, recv_sems, *after)
    return list(outs[:n]), list(outs[n:])


def _gather_start(name, fulls):
    def issue(srcs, dsts, send_sems, recv_sems):
        x, y, c, peers = _place()
        for i in range(len(fulls)):
            mine = dsts[i].at[2 * x + y, c]
            for k, (px, py) in enumerate(peers):
                pltpu.make_async_remote_copy(
                    src_ref=mine, dst_ref=mine, send_sem=send_sems.at[3 * i + k],
                    recv_sem=recv_sems.at[3 * i + k], device_id=(px, py, c), device_id_type=MESH).start()

    return _split_start(name, [], fulls, 3 * len(fulls), issue)


def _gather_wait(name, send_sems, recv_sems, fulls, after):
    def wait(srcs, dsts, send_sems, recv_sems):
        x, y, c, peers = _place()
        for i in range(len(fulls)):
            for k, (px, py) in enumerate(peers):
                cp = pltpu.make_async_remote_copy(
                    src_ref=dsts[i].at[2 * x + y, c], dst_ref=dsts[i].at[2 * px + py, c],
                    send_sem=send_sems.at[3 * i + k], recv_sem=recv_sems.at[3 * i + k],
                    device_id=(px, py, c), device_id_type=MESH)
                cp.wait_send()
                cp.wait_recv()

    return _split_wait(name, send_sems, recv_sems, [], fulls, after, wait)[1]


def _gather_finish(name, fulls):
    n = len(fulls)

    def body(*refs):
        fin, fout = refs[:n], refs[n:2 * n]
        send_sems, recv_sems = refs[2 * n:]
        x, y, c, peers = _place()

        def copy(i, k, half):
            px, py = peers[k]
            return pltpu.make_async_remote_copy(
                src_ref=fin[i].at[2 * px + py, half], dst_ref=fout[i].at[2 * px + py, half],
                send_sem=send_sems.at[3 * i + k], recv_sem=recv_sems.at[3 * i + k],
                device_id=(x, y, 1 - c), device_id_type=MESH)

        sends = [copy(i, k, c) for i in range(n) for k in range(3)]
        for cp in sends:
            cp.start()
        for i in range(n):
            for k in range(3):
                copy(i, k, 1 - c).wait_recv()
        for cp in sends:
            cp.wait_send()

    return pl.pallas_call(
        body, name=name, out_shape=[SDS(f.shape, f.dtype) for f in fulls],
        in_specs=[ANY_SPEC] * n, out_specs=[ANY_SPEC] * n, input_output_aliases={i: i for i in range(n)},
        scratch_shapes=[pltpu.SemaphoreType.DMA((3 * n,)), pltpu.SemaphoreType.DMA((3 * n,))])(*fulls)


def _reduce_start(name, parts):
    lands = [lax.empty((3,) + p.shape[1:], p.dtype) for p in parts]

    def issue(srcs, dsts, send_sems, recv_sems):
        x, y, c, peers = _place()
        for i in range(len(parts)):
            for k, (px, py) in enumerate(peers):
                pltpu.make_async_remote_copy(
                    src_ref=srcs[i].at[2 * px + py], dst_ref=dsts[i].at[k], send_sem=send_sems.at[3 * i + k],
                    recv_sem=recv_sems.at[3 * i + k], device_id=(px, py, c), device_id_type=MESH).start()

    return _split_start(name, parts, lands, 3 * len(parts), issue)


def _reduce_wait(name, send_sems, recv_sems, parts, lands, after):
    def wait(srcs, dsts, send_sems, recv_sems):
        x, y, c, peers = _place()
        for i in range(len(parts)):
            for k, (px, py) in enumerate(peers):
                cp = pltpu.make_async_remote_copy(
                    src_ref=srcs[i].at[2 * px + py], dst_ref=dsts[i].at[k], send_sem=send_sems.at[3 * i + k],
                    recv_sem=recv_sems.at[3 * i + k], device_id=(px, py, c), device_id_type=MESH)
                cp.wait_send()
                cp.wait_recv()

    return _split_wait(name, send_sems, recv_sems, parts, lands, after, wait)


def _sibling_copy(src, dst, send_sems, recv_sems, k):
    x, y, c, _ = _place()
    return pltpu.make_async_remote_copy(src_ref=src, dst_ref=dst, send_sem=send_sems.at[k], recv_sem=recv_sems.at[k],
                                        device_id=(x, y, 1 - c), device_id_type=MESH)


def _forward_start(name, fulls):
    def issue(srcs, dsts, send_sems, recv_sems):
        x, y, c, peers = _place()
        for i in range(len(fulls)):
            for k, (px, py) in enumerate(peers):
                part = dsts[i].at[2 * px + py, c]
                _sibling_copy(part, part, send_sems, recv_sems, 3 * i + k).start()

    return _split_start(name, [], fulls, 3 * len(fulls), issue, sibling_only=True)


def _forward_wait(name, send_sems, recv_sems, fulls, after):
    def wait(srcs, dsts, send_sems, recv_sems):
        x, y, c, peers = _place()
        for i in range(len(fulls)):
            for k, (px, py) in enumerate(peers):
                cp = _sibling_copy(dsts[i].at[2 * px + py, c], dsts[i].at[2 * px + py, 1 - c], send_sems, recv_sems, 3 * i + k)
                cp.wait_send()
                cp.wait_recv()

    return _split_wait(name, send_sems, recv_sems, [], fulls, after, wait)[1]


def _exchange_start(name, grads, sliced=True):
    lands = [lax.empty((NSH,) + g.shape[-2:], g.dtype) for g in grads]

    def issue(srcs, dsts, send_sems, recv_sems):
        c = lax.axis_index("c")
        for i in range(len(grads)):
            src = srcs[i].at[:, 1 - c] if sliced else srcs[i]
            _sibling_copy(src, dsts[i], send_sems, recv_sems, i).start()

    return _split_start(name, grads, lands, len(grads), issue, sibling_only=True)


def _exchange_wait(name, send_sems, recv_sems, grads, lands, after, sliced=True):
    def wait(srcs, dsts, send_sems, recv_sems):
        c = lax.axis_index("c")
        for i in range(len(grads)):
            cp = _sibling_copy(srcs[i].at[:, 1 - c] if sliced else srcs[i], dsts[i], send_sems, recv_sems, i)
            cp.wait_send()
            cp.wait_recv()

    return _split_wait(name, send_sems, recv_sems, grads, lands, after, wait)


def _share_start(name, sums):
    def issue(srcs, dsts, send_sems, recv_sems):
        c = lax.axis_index("c")
        for i in range(len(sums)):
            _sibling_copy(dsts[i].at[c], dsts[i].at[c], send_sems, recv_sems, i).start()

    return _split_start(name, [], sums, len(sums), issue, sibling_only=True)


def _share_wait(name, send_sems, recv_sems, sums, after):
    def wait(srcs, dsts, send_sems, recv_sems):
        c = lax.axis_index("c")
        for i in range(len(sums)):
            cp = _sibling_copy(dsts[i].at[c], dsts[i].at[1 - c], send_sems, recv_sems, i)
            cp.wait_send()
            cp.wait_recv()

    return _split_wait(name, send_sems, recv_sems, [], sums, after, wait)[1]


def _allgather_small(v, after):
    m_per, n = v.shape

    def body(x_ref, after_ref, out_ref, send_sems, recv_sems, local_sem):
        x, y, c = lax.axis_index("x"), lax.axis_index("y"), lax.axis_index("c")
        me, sibling = (x, y, c), (x, y, 1 - c)
        chips = [(1 - x, y), (x, 1 - y), (1 - x, 1 - y)]

        def rows(px, py, pc):
            return out_ref.at[pl.ds((4 * px + 2 * py + pc) * m_per, m_per), :]

        def copy(k, block, to, src=None):
            return pltpu.make_async_remote_copy(
                src_ref=rows(*block) if src is None else src, dst_ref=rows(*block),
                send_sem=send_sems.at[k], recv_sem=recv_sems.at[k], device_id=to, device_id_type=MESH)

        mine = pltpu.make_async_copy(x_ref, rows(*me), local_sem)
        mine.start()
        first = [copy(0, me, sibling, src=x_ref)]
        first += [copy(1 + j, me, (*chip, c), src=x_ref) for j, chip in enumerate(chips)]
        for cp in first:
            cp.start()
        passed = [copy(4 + j, (*chip, c), sibling) for j, chip in enumerate(chips)]
        for j, chip in enumerate(chips):
            copy(1 + j, (*chip, c), me).wait_recv()
            passed[j].start()
        copy(0, sibling, me).wait_recv()
        for j, chip in enumerate(chips):
            copy(4 + j, (*chip, 1 - c), me).wait_recv()
        for cp in first + passed:
            cp.wait_send()
        mine.wait()

    return pl.pallas_call(
        body, name="allgather_small", out_shape=SDS((8 * m_per, n), v.dtype),
        in_specs=[pl.BlockSpec(memory_space=pltpu.VMEM), ANY_SPEC], out_specs=pl.BlockSpec(memory_space=pltpu.VMEM),
        scratch_shapes=[pltpu.SemaphoreType.DMA((7,)), pltpu.SemaphoreType.DMA((7,)), pltpu.SemaphoreType.DMA])(v, after)


def _norm_in_proj_own(x, g, w_full, place):
    tn, chunk = 512, 256
    per = (DIN // NSH) // tn

    def body(place_ref, x_ref, g_ref, w_ref, proj_ref, xn_ref):
        @pl.when(pl.program_id(0) == 0)
        def _():
            def norm(r, carry):
                rows = pl.ds(pl.multiple_of(r * chunk, chunk), chunk)
                xv = x_ref[rows, :]
                rs = lax.rsqrt(jnp.mean(xv * xv, axis=-1, keepdims=True) + EPS)
                xn_ref[rows, :] = (xv * rs * g_ref[...]).astype(BF16)
                return carry

            lax.fori_loop(0, T // chunk, norm, 0)

        proj_ref[...] = _dot(xn_ref[...], w_ref[...])

    return pl.pallas_call(
        body, name="norm_in_proj_own", out_shape=[SDS((T, DIN), F32), SDS((T, D), BF16)],
        grid_spec=pltpu.PrefetchScalarGridSpec(
            num_scalar_prefetch=1, grid=(per,),
            in_specs=[_resident((T, D), lambda j, p: (0, 0)),
                      pl.BlockSpec((1, D), lambda j, p: (0, 0)),
                      pl.BlockSpec((None, D, tn), lambda j, p: (p[0], 0, j))],
            out_specs=[pl.BlockSpec((T, tn), lambda j, p: (0, p[0] * per + j)),
                       pl.BlockSpec((T, D), lambda j, p: (0, 0))]),
        compiler_params=_params(1))(place, x, g, w_full)


def _in_proj_rest(xn, w_full, proj, place):
    tn = 512
    per = (DIN // NSH) // tn

    def body(place_ref, xn_ref, w_ref, proj_in, proj_ref):
        proj_ref[...] = _dot(xn_ref[...], w_ref[...])

    shard = lambda j, p: p[0] ^ jnp.where(j < per, 2, jnp.where(j < 2 * per, 1, 3))
    return pl.pallas_call(
        body, name="in_proj_rest", out_shape=SDS((T, DIN), F32),
        grid_spec=pltpu.PrefetchScalarGridSpec(
            num_scalar_prefetch=1, grid=((NSH - 1) * per,),
            in_specs=[_resident((T, D), lambda j, p: (0, 0)),
                      pl.BlockSpec((None, D, tn), lambda j, p: (shard(j, p), 0, j % per)), ANY_SPEC],
            out_specs=pl.BlockSpec((T, tn), lambda j, p: (0, shard(j, p) * per + j % per))),
        input_output_aliases={3: 0}, compiler_params=_params(1))(place, xn, w_full, proj)


def _rope_tables():
    pos = np.arange(T, dtype=np.float32)
    inv = (10000.0 ** (-np.arange(0, HD, 2, dtype=np.float32) / HD)).astype(np.float32)
    ang = (pos[:, None] * inv[None, :]).astype(np.float32)
    cos, sin = np.cos(ang).astype(np.float32), np.sin(ang).astype(np.float32)
    return (jnp.asarray(np.concatenate([cos, cos], axis=1)), jnp.asarray(np.concatenate([-sin, sin], axis=1)))


def _qk_prep(proj, nw, cos, sin):
    tm = 256

    def body(p_ref, w_ref, cos_ref, sin_ref, o_ref):
        cv, sv = cos_ref[...], sin_ref[...]
        for h in range(NH):
            sl = slice(h * HD, (h + 1) * HD)
            xv = p_ref[:, sl]
            r = lax.rsqrt(jnp.mean(xv * xv, axis=-1, keepdims=True) + EPS)
            z = xv * r * w_ref[:, sl]
            if h < NHA:
                z = z * cv + pltpu.roll(z, 64, 1) * sv
            o_ref[:, sl] = z.astype(BF16)

    return pl.pallas_call(
        body, name="qk_prep", out_shape=SDS((T, 2 * D), BF16), grid=(T // tm, 2),
        in_specs=[pl.BlockSpec((tm, D), lambda i, j: (i, j)),
                  pl.BlockSpec((None, 1, D), lambda i, j: (j, 0, 0)),
                  pl.BlockSpec((tm, HD), lambda i, j: (i, 0)),
                  pl.BlockSpec((tm, HD), lambda i, j: (i, 0))],
        out_specs=pl.BlockSpec((tm, D), lambda i, j: (i, j)),
        compiler_params=_params(2))(proj, nw, cos, sin)


def _band_mask(q0, m):
    ii = lax.broadcasted_iota(jnp.int32, (128, 256), 0)
    jj = lax.broadcasted_iota(jnp.int32, (128, 256), 1)
    rel = jj - ii
    kpos = jj + (q0 - 64)
    return (rel >= 0) & (rel <= 128) & (kpos >= 0) & (kpos < m)


def _fill_padded(dst, src, m):
    zeros = jnp.zeros((64, HD), dst.dtype)
    dst[0:64, :] = zeros
    dst[64 + m:128 + m, :] = zeros
    dst[64:64 + m, :] = src.astype(dst.dtype)


def _residue_rows(r, m, dil):
    return pl.ds(r, m, stride=dil) if dil > 1 else slice(None)


def _head_blocks(g):
    col = lambda base: pl.BlockSpec((T, HD), lambda h: (0, base + g * 4 + h))
    return col(0), col(NH), col(2 * NH), pl.BlockSpec((T, HD), lambda h: (0, h))


def _attn_a_fwd(qkn, proj, g):
    dil = DILS[g]
    m = T // dil
    nb = m // 128

    def body(q_ref, k_ref, v_ref, o_ref, l_ref, qf, kf, qp, kp, vp, ob, lb):
        qf[...] = q_ref[...].astype(F32)
        kf[...] = k_ref[...].astype(F32)
        for r in range(dil):
            rows = _residue_rows(r, m, dil)
            qp[...] = qf[rows, :].astype(BF16)
            _fill_padded(kp, kf[rows, :], m)
            _fill_padded(vp, v_ref[rows, :], m)

            def block(b, carry):
                q0 = pl.multiple_of(b * 128, 128)
                kw, vw = kp[pl.ds(q0, 256), :], vp[pl.ds(q0, 256), :]
                s = _dot_nt(qp[pl.ds(q0, 128), :], kw) * SCALE
                s = jnp.where(_band_mask(q0, m), s, NEG)
                mx = jnp.max(s, axis=-1, keepdims=True)
                e = jnp.exp(s - mx)
                den = jnp.sum(e, axis=-1, keepdims=True)
                ob[pl.ds(q0, 128), :] = _dot((e / den).astype(BF16), vw)
                lb[pl.ds(q0, 128), :] = jnp.broadcast_to(mx + jnp.log(den), (128, HD))
                return carry

            lax.fori_loop(0, nb, block, 0, unroll=min(nb, 16))
            o_ref[rows, :] = ob[...]
            l_ref[rows, :] = lb[...]

    q_blk, k_blk, v_blk, out_blk = _head_blocks(g)
    return pl.pallas_call(
        body, name=f"attn_a_fwd_{g}", out_shape=[SDS((T, 512), F32)] * 2, grid=(4,),
        in_specs=[q_blk, k_blk, v_blk], out_specs=[out_blk] * 2,
        scratch_shapes=[pltpu.VMEM((T, HD), F32), pltpu.VMEM((T, HD), F32), pltpu.VMEM((m, HD), BF16),
                        pltpu.VMEM((m + 128, HD), BF16), pltpu.VMEM((m + 128, HD), BF16),
                        pltpu.VMEM((m, HD), F32), pltpu.VMEM((m, HD), F32)],
        compiler_params=_params(1))(qkn, qkn, proj)


def _nbr_window(r):
    start = jnp.clip(r - WIN_R // 2, 0, T // GRID_W - WIN_R)
    return start, start - r + (WIN_R - 1)


def _rpb_rows(rpb):
    zeros = jnp.zeros((4, 14, 33), F32)
    a, b = rpb[:, :14], rpb[:, 1:15]
    rows = jnp.concatenate([a[:, :, 15:31], zeros, b, zeros, a[:, :, 0:15]], axis=2)
    return jnp.pad(rows, ((0, 0), (0, 2), (0, 0)))


def _attn_b_fwd(qkn, proj, rpb_rows):
    def body(r_ref, q_ref, k_ref, v_ref, o_ref, l_ref, bias_ref, vb, pair):
        qc = lax.broadcasted_iota(jnp.int32, (GRID_W, 512), 0)
        kc = lax.broadcasted_iota(jnp.int32, (GRID_W, 512), 1) & (GRID_W - 1)
        cs = jnp.clip(qc - WIN_C // 2, 0, GRID_W - WIN_C)
        colmask = (kc >= cs) & (kc < cs + WIN_C)
        for d in range(14):
            pair[d] = pltpu.roll(jnp.broadcast_to(r_ref[d:d + 1, :], (GRID_W, HD)), 0, 1, stride=1, stride_axis=0)
        for off in range(8):
            rows = jnp.concatenate([pair[off + 2 * jj] for jj in range(4)], axis=1)
            bias_ref[off] = jnp.where(colmask, rows, NEG)
        vb[...] = v_ref[...].astype(BF16)

        def row(r, carry):
            start, off = _nbr_window(r)
            q0 = pl.multiple_of(r * GRID_W, GRID_W)
            k0 = pl.multiple_of(start * GRID_W, GRID_W)
            s = _dot_nt(q_ref[pl.ds(q0, GRID_W), :], k_ref[pl.ds(k0, 512), :]) * SCALE + bias_ref[off]
            mx = jnp.max(s, axis=-1, keepdims=True)
            e = jnp.exp(s - mx)
            den = jnp.sum(e, axis=-1, keepdims=True)
            o_ref[pl.ds(q0, GRID_W), :] = _dot((e / den).astype(BF16), vb[pl.ds(k0, 512), :])
            l_ref[pl.ds(q0, GRID_W), :] = jnp.broadcast_to(mx + jnp.log(den), (GRID_W, HD))
            return carry

        lax.fori_loop(0, T // GRID_W, row, 0, unroll=16)

    return pl.pallas_call(
        body, name="attn_b_fwd",
        out_shape=[SDS((T, 512), F32), SDS((T, 512), F32), SDS((4, 8, GRID_W, 512), F32)], grid=(4,),
        in_specs=[pl.BlockSpec((None, 16, HD), lambda h: (h, 0, 0)),
                  pl.BlockSpec((T, HD), lambda h: (0, NHA + h)),
                  pl.BlockSpec((T, HD), lambda h: (0, NH + NHA + h)),
                  pl.BlockSpec((T, HD), lambda h: (0, 2 * NH + NHA + h))],
        out_specs=[pl.BlockSpec((T, HD), lambda h: (0, h)), pl.BlockSpec((T, HD), lambda h: (0, h)),
                   pl.BlockSpec((None, 8, GRID_W, 512), lambda h: (h, 0, 0, 0))],
        scratch_shapes=[pltpu.VMEM((T, HD), BF16), pltpu.VMEM((14, GRID_W, HD), F32)],
        compiler_params=_params(1))(rpb_rows, qkn, qkn, proj)


def _comb_fwd(os, ls):
    tm = 512

    def body(o0, o1, o2, l0, l1, l2, oa_ref, w0, w1, w2):
        lv = [l0[...], l1[...], l2[...]]
        mx = jnp.maximum(jnp.maximum(lv[0], lv[1]), lv[2])
        ev = [jnp.exp(l - mx) for l in lv]
        den = ev[0] + ev[1] + ev[2]
        wv = [e / den for e in ev]
        oa_ref[...] = (wv[0] * o0[...] + wv[1] * o1[...] + wv[2] * o2[...]).astype(BF16)
        w0[...], w1[...], w2[...] = wv

    spec = pl.BlockSpec((tm, 512), lambda i: (i, 0))
    return pl.pallas_call(
        body, name="comb_fwd", out_shape=[SDS((T, 512), BF16)] + [SDS((T, 512), F32)] * 3, grid=(T // tm,),
        in_specs=[spec] * 6, out_specs=[spec] * 4, compiler_params=_params(1))(*os, *ls)


def _mix_fwd(oa, ob, proj, b_gate, wpa, wpb):
    tm = 512

    def body(oa_ref, ob_ref, ga_ref, gb_ref, ba_ref, bb_ref, wpa_ref, wpb_ref, mixed_ref, ob16_ref):
        oav = oa_ref[...]
        obv = ob_ref[...].astype(BF16)
        ob16_ref[...] = obv
        for s in range(NSH):
            sl = slice(s * 512, (s + 1) * 512)
            ga = _sigmoid(ga_ref[:, sl] + ba_ref[:, sl])
            gb = _sigmoid(gb_ref[:, sl] + bb_ref[:, sl])
            mixed_ref[:, sl] = (ga * _dot(oav, wpa_ref[s]) + gb * _dot(obv, wpb_ref[s])).astype(BF16)

    row = lambda w: pl.BlockSpec((tm, w), lambda i: (i, 0))
    return pl.pallas_call(
        body, name="mix_fwd", out_shape=[SDS((T, D), BF16), SDS((T, 512), BF16)], grid=(T // tm,),
        in_specs=[row(512), row(512),
                  pl.BlockSpec((tm, D), lambda i: (i, 3)), pl.BlockSpec((tm, D), lambda i: (i, 4)),
                  pl.BlockSpec((1, D), lambda i: (0, 0)), pl.BlockSpec((1, D), lambda i: (0, 1)),
                  _resident((NSH, 512, 512), lambda i: (0, 0, 0)), _resident((NSH, 512, 512), lambda i: (0, 0, 0))],
        out_specs=[row(D), row(512)], compiler_params=_params(1))(oa, ob, proj, proj, b_gate, b_gate, wpa, wpb)


def _out_proj_fwd(mixed, w_out, x, g):
    tm = 512

    def body(m_ref, w_ref, x_ref, g_ref, h1_ref, hn_ref):
        h1 = x_ref[...] + _dot(m_ref[...], w_ref[...])
        h1_ref[...] = h1
        r = lax.rsqrt(jnp.mean(h1 * h1, axis=-1, keepdims=True) + EPS)
        hn_ref[...] = (h1 * r * g_ref[...]).astype(BF16)

    row = pl.BlockSpec((tm, D), lambda i: (i, 0))
    return pl.pallas_call(
        body, name="out_proj_fwd", out_shape=[SDS((T, D), F32), SDS((T, D), BF16)], grid=(T // tm,),
        in_specs=[row, _resident((D, D), lambda i: (0, 0)), row, pl.BlockSpec((1, D), lambda i: (0, 0))],
        out_specs=[row, row], compiler_params=_params(1))(mixed, w_out, x, g)


def _ffn_up(hn, w_up):
    tm, tn = T, 512
    per = (DFF // NSH) // tn

    def body(h_ref, w_ref, a_ref, u_ref):
        uv = jnp.maximum(_dot(h_ref[...], w_ref[...]), 0.0)
        a_ref[...] = (uv * uv).astype(BF16)
        u_ref[...] = uv.astype(BF16)

    out = pl.BlockSpec((tm, tn), lambda i, j: (i, j))
    return pl.pallas_call(
        body, name="ffn_up", out_shape=[SDS((T, DFF), BF16)] * 2, grid=(T // tm, DFF // tn),
        in_specs=[pl.BlockSpec((tm, D), lambda i, j: (i, 0)),
                  pl.BlockSpec((None, D, tn), lambda i, j: (j // per, 0, j % per))],
        out_specs=[out, out], compiler_params=_params(2))(hn, w_up)


def _ffn_down_own(u, w_down, place):
    tm, tk = 512, DFF // NSH

    def body(place_ref, u_ref, w_ref, o_ref):
        o_ref[...] = _dot(u_ref[...], w_ref[...])

    return pl.pallas_call(
        body, name="ffn_down_own", out_shape=SDS((T, D), F32),
        grid_spec=pltpu.PrefetchScalarGridSpec(
            num_scalar_prefetch=1, grid=(T // tm,),
            in_specs=[pl.BlockSpec((tm, tk), lambda i, p: (i, p[0])), pl.BlockSpec((tk, D), lambda i, p: (p[0], 0))],
            out_specs=pl.BlockSpec((tm, D), lambda i, p: (i, 0))),
        compiler_params=_params(1))(place, u, w_down)


def _ffn_down_loss(u, w_down, h1, target, own, place):
    tm, tk = 512, DFF // NSH
    nk = NSH - 1

    def body(place_ref, u_ref, w_ref, h1_ref, t_ref, own_ref, dy_ref, dy16_ref, loss_ref, acc):
        k = pl.program_id(1)

        @pl.when(k == 0)
        def _():
            acc[...] = own_ref[...]

        acc[...] += _dot(u_ref[...], w_ref[...])

        @pl.when(k == nk - 1)
        def _():
            def chunk(r, sq):
                rows = pl.ds(pl.multiple_of(r * 16, 16), 16)
                err = acc[rows, :] + h1_ref[rows, :] - t_ref[rows, :]
                dy = err * (1.0 / D)
                dy_ref[rows, :] = dy
                dy16_ref[rows, :] = dy.astype(BF16)
                return sq + err * err

            sq = lax.fori_loop(0, tm // 16, chunk, jnp.zeros((16, D), F32), unroll=2)
            part = 0.5 * jnp.sum(jnp.mean(sq, axis=-1, keepdims=True), axis=0, keepdims=True)
            loss_ref[...] = jnp.broadcast_to(part, (8, 128))

    row = pl.BlockSpec((tm, D), lambda i, k, p: (i, 0))
    once = _resident((tm, D), lambda i, k, p: (i, 0))
    shard = lambda k, p: p[0] ^ (k + 1)
    return pl.pallas_call(
        body, name="ffn_down_loss",
        out_shape=[SDS((T, D), F32), SDS((T, D), BF16), SDS((T // tm, 8, 128), F32)],
        grid_spec=pltpu.PrefetchScalarGridSpec(
            num_scalar_prefetch=1, grid=(T // tm, nk),
            in_specs=[pl.BlockSpec((tm, tk), lambda i, k, p: (i, shard(k, p))),
                      pl.BlockSpec((tk, D), lambda i, k, p: (shard(k, p), 0)), once, once, once],
            out_specs=[row, row, pl.BlockSpec((None, 8, 128), lambda i, k, p: (i, 0, 0))],
            scratch_shapes=[pltpu.VMEM((tm, D), F32)]),
        compiler_params=_params(2))(place, u, w_down, h1, target, own)


def _ffn_down_bwd(dy16, w_down, u, deps=()):
    tm, tn = T, 512

    def body(dy_ref, w_ref, u_ref, du_ref):
        uv = u_ref[...].astype(F32)
        du_ref[...] = jnp.where(uv > 0.0, 2.0 * uv * _dot_nt(dy_ref[...], w_ref[...]), 0.0).astype(BF16)

    return pl.pallas_call(
        _after(body, deps), name="ffn_down_bwd", out_shape=SDS((T, DFF), BF16), grid=(T // tm, DFF // tn),
        in_specs=[DEP_SPEC] * len(deps) + [
            pl.BlockSpec((tm, D), lambda i, j: (i, 0)), pl.BlockSpec((tn, D), lambda i, j: (j, 0)),
            pl.BlockSpec((tm, tn), lambda i, j: (i, j))],
        out_specs=pl.BlockSpec((tm, tn), lambda i, j: (i, j)), compiler_params=_params(2))(*deps, dy16, w_down, u)


def _norm_bwd(xv, dz_in, g):
    r = lax.rsqrt(jnp.mean(xv * xv, axis=-1, keepdims=True) + EPS)
    dg = jnp.sum(xv * r * dz_in, axis=0, keepdims=True)
    dz = dz_in * g
    dx = r * dz - xv * (r * r * r) * jnp.mean(xv * dz, axis=-1, keepdims=True)
    return dx, dg


def _ffn_up_bwd(du, w_up, h1, dy, g, deps=()):
    tm, tk = 512, 1024
    per = (DFF // NSH) // tk
    nk = DFF // tk

    def body(du_ref, w_ref, h1_ref, dy_ref, g_ref, dh1_ref, dh16_ref, dg_ref, acc):
        i, k = pl.program_id(0), pl.program_id(1)

        @pl.when(k == 0)
        def _():
            acc[...] = jnp.zeros_like(acc)

        @pl.when((k == 0) & (i == 0))
        def _():
            dg_ref[...] = jnp.zeros_like(dg_ref)

        acc[...] += _dot_nt(du_ref[...], w_ref[...])

        @pl.when(k == nk - 1)
        def _():
            dx, dg = _norm_bwd(h1_ref[...], acc[...], g_ref[...])
            dh1 = dy_ref[...] + dx
            dh1_ref[...] = dh1
            dh16_ref[...] = dh1.astype(BF16)
            dg_ref[...] += dg

    row = pl.BlockSpec((tm, D), lambda i, k: (i, 0))
    vec = pl.BlockSpec((1, D), lambda i, k: (0, 0))
    return pl.pallas_call(
        _after(body, deps), name="ffn_up_bwd", out_shape=[SDS((T, D), F32), SDS((T, D), BF16), SDS((1, D), F32)],
        grid=(T // tm, nk),
        in_specs=[DEP_SPEC] * len(deps) + [
            pl.BlockSpec((tm, tk), lambda i, k: (i, k)),
            pl.BlockSpec((None, D, tk), lambda i, k: (k // per, 0, k % per)), row, row, vec],
        out_specs=[row, row, vec], scratch_shapes=[pltpu.VMEM((tm, D), F32)],
        compiler_params=_params(2))(*deps, du, w_up, h1, dy, g)


def _mix_bwd(dh16, w_out, oa, ob16, proj, b_gate, wpa, wpb):
    tm = 256

    def body(dh_ref, wo_ref, oa_ref, ob_ref, ga_ref, gb_ref, ba_ref, bb_ref, wpa_ref, wpb_ref,
             dya_ref, dyb_ref, dga_ref, dgb_ref, doa_ref, dob_ref, dba_ref, dbb_ref):
        @pl.when(pl.program_id(0) == 0)
        def _():
            dba_ref[...] = jnp.zeros_like(dba_ref)
            dbb_ref[...] = jnp.zeros_like(dbb_ref)

        oav, obv = oa_ref[...], ob_ref[...]
        doa = jnp.zeros((tm, 512), F32)
        dob = jnp.zeros((tm, 512), F32)
        for s in range(NSH):
            sl = slice(s * 512, (s + 1) * 512)
            dm = _dot_nt(dh_ref[...], wo_ref[sl, :])
            ga = _sigmoid(ga_ref[:, sl] + ba_ref[:, sl])
            gb = _sigmoid(gb_ref[:, sl] + bb_ref[:, sl])
            dya = (dm * ga).astype(BF16)
            dyb = (dm * gb).astype(BF16)
            dza = dm * _dot(oav, wpa_ref[s]) * ga * (1.0 - ga)
            dzb = dm * _dot(obv, wpb_ref[s]) * gb * (1.0 - gb)
            dya_ref[:, sl], dyb_ref[:, sl] = dya, dyb
            dga_ref[:, sl], dgb_ref[:, sl] = dza.astype(BF16), dzb.astype(BF16)
            dba_ref[:, sl] += jnp.sum(dza, axis=0, keepdims=True)
            dbb_ref[:, sl] += jnp.sum(dzb, axis=0, keepdims=True)
            doa += _dot_nt(dya, wpa_ref[s])
            dob += _dot_nt(dyb, wpb_ref[s])
        doa_ref[...], dob_ref[...] = doa, dob

    row = lambda w: pl.BlockSpec((tm, w), lambda i: (i, 0))
    vec = pl.BlockSpec((1, D), lambda i: (0, 0))
    wp = _resident((NSH, 512, 512), lambda i: (0, 0, 0))
    return pl.pallas_call(
        body, name="mix_bwd",
        out_shape=[SDS((T, D), BF16)] * 4 + [SDS((T, 512), F32)] * 2 + [SDS((1, D), F32)] * 2, grid=(T // tm,),
        in_specs=[row(D), _resident((D, D), lambda i: (0, 0)), row(512), row(512),
                  pl.BlockSpec((tm, D), lambda i: (i, 3)), pl.BlockSpec((tm, D), lambda i: (i, 4)),
                  pl.BlockSpec((1, D), lambda i: (0, 0)), pl.BlockSpec((1, D), lambda i: (0, 1)), wp, wp],
        out_specs=[row(D)] * 4 + [row(512)] * 2 + [vec] * 2,
        compiler_params=_params(1))(dh16, w_out, oa, ob16, proj, proj, b_gate, b_gate, wpa, wpb)


def _comb_bwd(doa, os, ws, deps=()):
    tm = 512

    def body(d_ref, o0, o1, o2, w0, w1, w2, cc_ref):
        prod = d_ref[...] * (w0[...] * o0[...] + w1[...] * o1[...] + w2[...] * o2[...])
        for h in range(4):
            sl = slice(h * HD, (h + 1) * HD)
            cc_ref[:, sl] = jnp.broadcast_to(jnp.sum(prod[:, sl], axis=-1, keepdims=True), (tm, HD))

    spec = pl.BlockSpec((tm, 512), lambda i: (i, 0))
    return pl.pallas_call(
        _after(body, deps), name="comb_bwd", out_shape=SDS((T, 512), F32), grid=(T // tm,),
        in_specs=[DEP_SPEC] * len(deps) + [spec] * 7, out_specs=spec,
        compiler_params=_params(1))(*deps, doa, *os, *ws)


def _attn_a_bwd(qkn, proj, doa, lse, w, cc, g):
    dil = DILS[g]
    m = T // dil
    nb = m // 128

    def body(q_ref, k_ref, v_ref, d_ref, l_ref, w_ref, c_ref, dqk_ref, dv_ref,
             qf, kf, qp, kp, vp, dp, lp, wsub, cp, dqb, dkp, dvp):
        qf[...] = q_ref[...].astype(F32)
        kf[...] = k_ref[...].astype(F32)
        for r in range(dil):
            sub = _residue_rows(r, m, dil)
            qp[...] = qf[sub, :].astype(BF16)
            _fill_padded(kp, kf[sub, :], m)
            _fill_padded(vp, v_ref[sub, :], m)
            dp[...] = d_ref[sub, :].astype(BF16)
            lp[...], wsub[...], cp[...] = l_ref[sub, :], w_ref[sub, :], c_ref[sub, :]
            dkp[...] = jnp.zeros_like(dkp)
            dvp[...] = jnp.zeros_like(dvp)

            def block(b, carry):
                q0 = pl.multiple_of(b * 128, 128)
                rows = pl.ds(q0, 128)
                win = pl.ds(q0, 256)
                qb, kw, vw = qp[rows, :], kp[win, :], vp[win, :]
                s = _dot_nt(qb, kw) * SCALE
                s = jnp.where(_band_mask(q0, m), s, NEG)
                wp = _wide(wsub[rows, :], 2) * jnp.exp(s - _wide(lp[rows, :], 2))
                dob = dp[rows, :]
                ds = (wp * (_dot_nt(dob, vw) - _wide(cp[rows, :], 2))).astype(BF16)
                dqb[rows, :] = _dot(ds, kw) * SCALE
                dkp[win, :] += _dot_tn(ds, qb) * SCALE
                dvp[win, :] += _dot_tn(wp.astype(BF16), dob)
                return carry

            lax.fori_loop(0, nb, block, 0, unroll=min(nb, 16))
            dqk_ref.at[0][sub, :] = dqb[...]
            dqk_ref.at[1][sub, :] = dkp[64:64 + m, :]
            dv_ref[sub, :] = dvp[64:64 + m, :]

    q_blk, k_blk, v_blk, blk = _head_blocks(g)
    sub16 = pltpu.VMEM((m, HD), BF16)
    sub32 = pltpu.VMEM((m, HD), F32)
    return pl.pallas_call(
        body, name=f"attn_a_bwd_{g}", out_shape=[SDS((2, T, 512), F32), SDS((T, 512), F32)], grid=(4,),
        in_specs=[q_blk, k_blk, v_blk, blk, blk, blk, blk],
        out_specs=[pl.BlockSpec((2, T, HD), lambda h: (0, 0, h)), blk],
        scratch_shapes=[pltpu.VMEM((T, HD), F32), pltpu.VMEM((T, HD), F32), sub16,
                        pltpu.VMEM((m + 128, HD), BF16), pltpu.VMEM((m + 128, HD), BF16), sub16,
                        sub32, sub32, sub32, sub32,
                        pltpu.VMEM((m + 128, HD), F32), pltpu.VMEM((m + 128, HD), F32)],
        compiler_params=_params(1))(qkn, qkn, proj, doa, lse, w, cc)


def _attn_b_bwd(qkn, proj, dob, ob, lse, bias, deps=()):
    def body(q_ref, k_ref, v_ref, d_ref, o_ref, l_ref, bias_ref, dqk_ref, dv_ref, drpb_ref, vb, dk_acc, dv_acc, a_acc):
        vb[...] = v_ref[...].astype(BF16)
        dk_acc[...] = jnp.zeros_like(dk_acc)
        dv_acc[...] = jnp.zeros_like(dv_acc)
        a_acc[...] = jnp.zeros_like(a_acc)

        def row(r, carry):
            start, off = _nbr_window(r)
            rows = pl.ds(pl.multiple_of(r * GRID_W, GRID_W), GRID_W)
            win = pl.ds(pl.multiple_of(start * GRID_W, GRID_W), 512)
            qr, kw, vw = q_ref[rows, :], k_ref[win, :], vb[win, :]
            s = _dot_nt(qr, kw) * SCALE + bias_ref[off]
            p = jnp.exp(s - _wide(l_ref[rows, :], 4))
            dov = d_ref[rows, :]
            delta = jnp.sum(dov * o_ref[rows, :], axis=-1, keepdims=True)
            do16 = dov.astype(BF16)
            ds = p * (_dot_nt(do16, vw) - delta)
            a_acc[off] += ds
            ds16 = ds.astype(BF16)
            dqk_ref[0, rows, :] = _dot(ds16, kw) * SCALE
            dk_acc[win, :] += _dot_tn(ds16, qr) * SCALE
            dv_acc[win, :] += _dot_tn(p.astype(BF16), do16)
            return carry

        lax.fori_loop(0, T // GRID_W, row, 0, unroll=16)
        dqk_ref[1] = dk_acc[...]
        dv_ref[...] = dv_acc[...]

        lane = lax.broadcasted_iota(jnp.int32, (16, HD), 1)
        rowi = lax.broadcasted_iota(jnp.int32, (16, HD), 0)
        low = (lane >= GRID_W - WIN_C) & (lane < GRID_W + WIN_C - 1)
        high = (lane >= HD - WIN_C) | (lane < WIN_C - 1)
        flip = (lax.broadcasted_iota(jnp.int32, (GRID_W, GRID_W), 0)
                + lax.broadcasted_iota(jnp.int32, (GRID_W, GRID_W), 1) == GRID_W - 1).astype(BF16)
        out = jnp.zeros((16, HD), F32)
        for d in range(14):
            acc = None
            for off in range(8):
                if 0 <= d - off <= 6 and (d - off) % 2 == 0:
                    jj = (d - off) // 2
                    piece = a_acc[off, :, jj * HD:(jj + 1) * HD]
                    acc = piece if acc is None else acc + piece
            hi = acc.astype(BF16)
            lo = (acc - hi.astype(F32)).astype(BF16)
            rev = _dot(flip, hi) + _dot(flip, lo)
            v = jnp.sum(pltpu.roll(rev, 0, 1, stride=1, stride_axis=0), axis=0, keepdims=True)
            v = jnp.broadcast_to(v, (16, HD))
            out = out + jnp.where((rowi == d) & low, v, 0.0)
            out = out + jnp.where(rowi == d + 1, pltpu.roll(jnp.where(high, v, 0.0), GRID_W, 1), 0.0)
        drpb_ref[...] = out

    blk = pl.BlockSpec((T, HD), lambda h: (0, h))
    return pl.pallas_call(
        _after(body, deps), name="attn_b_bwd",
        out_shape=[SDS((2, T, 512), F32), SDS((T, 512), F32), SDS((4, 16, HD), F32)], grid=(4,),
        in_specs=[DEP_SPEC] * len(deps) + [
            pl.BlockSpec((T, HD), lambda h: (0, NHA + h)),
            pl.BlockSpec((T, HD), lambda h: (0, NH + NHA + h)),
            pl.BlockSpec((T, HD), lambda h: (0, 2 * NH + NHA + h)), blk, blk, blk,
            pl.BlockSpec((None, 8, GRID_W, 512), lambda h: (h, 0, 0, 0))],
        out_specs=[pl.BlockSpec((2, T, HD), lambda h: (0, 0, h)), blk,
                   pl.BlockSpec((None, 16, HD), lambda h: (h, 0, 0))],
        scratch_shapes=[pltpu.VMEM((T, HD), BF16), pltpu.VMEM((T, HD), F32), pltpu.VMEM((T, HD), F32),
                        pltpu.VMEM((8, GRID_W, 512), F32)],
        compiler_params=_params(1))(*deps, qkn, qkn, proj, dob, ob, lse, bias)


def _qk_bwd(proj, nw, cos, sin, dqk_groups, dqk_b, dvs, dga, dgb):
    tm = 512

    def body(p_ref, w_ref, cos_ref, sin_ref, d0, d1, d2, d3, v0, v1, v2, v3, ga_ref, gb_ref, o_ref, dn_ref):
        j, i = pl.program_id(0), pl.program_id(1)

        @pl.when((j < 2) & (i == 0))
        def _():
            dn_ref[...] = jnp.zeros_like(dn_ref)

        @pl.when(j < 2)
        def _():
            cv, sv = cos_ref[...], sin_ref[...]
            srcs = (d0, d1, d2, d3)
            dna = jnp.zeros((1, HD), F32)
            dnb = jnp.zeros((1, HD), F32)
            for h in range(NH):
                sl = slice(h * HD, (h + 1) * HD)
                dz = srcs[h // 4][:, (h % 4) * HD:(h % 4 + 1) * HD]
                if h < NHA:
                    dz = dz * cv + pltpu.roll(dz * sv, 64, 1)
                dx, dg = _norm_bwd(p_ref[:, sl], dz, w_ref[:, sl])
                o_ref[:, sl] = dx.astype(BF16)
                if h < NHA:
                    dna += dg
                else:
                    dnb += dg
            dn_ref[0:1, :] += dna
            dn_ref[1:2, :] += dnb

        @pl.when(j == 2)
        def _():
            for s, v_ref in enumerate((v0, v1, v2, v3)):
                o_ref[:, s * 512:(s + 1) * 512] = v_ref[...].astype(BF16)

        @pl.when(j == 3)
        def _():
            o_ref[...] = ga_ref[...]

        @pl.when(j == 4)
        def _():
            o_ref[...] = gb_ref[...]

    def rows(used):
        return lambda j, i: (jnp.where(used(j), i, 0), 0)

    qk = lambda j: j < 2
    dspec = pl.BlockSpec((None, tm, 512), lambda j, i: (jnp.minimum(j, 1), jnp.where(j < 2, i, 0), 0))
    vspec = pl.BlockSpec((tm, 512), rows(lambda j: j == 2))
    return pl.pallas_call(
        body, name="qk_bwd", out_shape=[SDS((T, DIN), BF16), SDS((2, 8, HD), F32)], grid=(5, T // tm),
        in_specs=[pl.BlockSpec((tm, D), lambda j, i: (jnp.where(j < 2, i, 0), jnp.minimum(j, 1))),
                  pl.BlockSpec((None, 1, D), lambda j, i: (jnp.minimum(j, 1), 0, 0)),
                  pl.BlockSpec((tm, HD), rows(qk)), pl.BlockSpec((tm, HD), rows(qk)),
                  dspec, dspec, dspec, dspec, vspec, vspec, vspec, vspec,
                  pl.BlockSpec((tm, D), rows(lambda j: j == 3)), pl.BlockSpec((tm, D), rows(lambda j: j == 4))],
        out_specs=[pl.BlockSpec((tm, D), lambda j, i: (i, j)),
                   pl.BlockSpec((None, 8, HD), lambda j, i: (jnp.minimum(j, 1), 0, 0))],
        compiler_params=_params(2))(proj, nw, cos, sin, *dqk_groups, dqk_b, *dvs, dga, dgb)


def _in_proj_bwd(dproj, w_in, x, dh1, g, deps=()):
    tm, tk = 512, 1280
    per = (DIN // NSH) // tk
    nk = DIN // tk

    def body(dp_ref, w_ref, x_ref, dh_ref, g_ref, dx_ref, dg_ref, acc):
        i, k = pl.program_id(0), pl.program_id(1)

        @pl.when(k == 0)
        def _():
            acc[...] = jnp.zeros_like(acc)

        @pl.when((k == 0) & (i == 0))
        def _():
            dg_ref[...] = jnp.zeros_like(dg_ref)

        acc[...] += _dot_nt(dp_ref[...], w_ref[...])

        @pl.when(k == nk - 1)
        def _():
            dx, dg = _norm_bwd(x_ref[...], acc[...], g_ref[...])
            dx_ref[...] = dh_ref[...] + dx
            dg_ref[...] += dg

    row = pl.BlockSpec((tm, D), lambda i, k: (i, 0))
    vec = pl.BlockSpec((1, D), lambda i, k: (0, 0))
    return pl.pallas_call(
        _after(body, deps), name="in_proj_bwd", out_shape=[SDS((T, D), F32), SDS((1, D), F32)], grid=(T // tm, nk),
        in_specs=[DEP_SPEC] * len(deps) + [
            pl.BlockSpec((tm, tk), lambda i, k: (i, k)),
            pl.BlockSpec((None, D, tk), lambda i, k: (k // per, 0, k % per)), row, row, vec],
        out_specs=[row, vec], scratch_shapes=[pltpu.VMEM((tm, D), F32)],
        compiler_params=_params(2))(*deps, dproj, w_in, x, dh1, g)


def _grad_w(name, a, g, shard_rows, rows, cols, tr, tc):
    ni, nj = rows // tr, cols // tc
    if shard_rows:
        a_map, g_map = (lambda s, i, j: (0, s * ni + i)), (lambda s, i, j: (0, j))
    else:
        a_map, g_map = (lambda s, i, j: (0, i)), (lambda s, i, j: (0, s * nj + j))

    def body(a_ref, g_ref, o_ref):
        o_ref[...] = _dot_tn(a_ref[...], g_ref[...]).astype(BF16)

    return pl.pallas_call(
        body, name=name, out_shape=SDS((NSH, rows, cols), BF16), grid=(NSH, ni, nj),
        in_specs=[pl.BlockSpec((T, tr), a_map), pl.BlockSpec((T, tc), g_map)],
        out_specs=pl.BlockSpec((None, tr, tc), lambda s, i, j: (s, i, j)), compiler_params=_params(3))(a, g)


def _grad_w_half(name, a, g, place, shard_rows, rows, cols, tc, for_sibling, theirs=None, deps=()):
    tr = rows // 2
    nj = cols // tc
    half = (lambda p: 1 - p[1]) if for_sibling else (lambda p: p[1])
    if shard_rows:
        a_map, g_map = (lambda s, j, p: (0, 2 * s + half(p))), (lambda s, j, p: (0, j))
    else:
        a_map, g_map = (lambda s, j, p: (0, half(p))), (lambda s, j, p: (0, s * nj + j))
    out = pl.BlockSpec((None, tr, tc), lambda s, j, p: (s, 0, j))
    n_in = 2 + (theirs is not None)

    def body(*refs):
        ins, o_ref = refs[-1 - n_in:-1], refs[-1]
        acc = _dot_tn(ins[0][...], ins[1][...])
        if theirs is not None:
            acc = acc + ins[2][...].astype(F32)
        o_ref[...] = acc.astype(BF16)

    extra = () if theirs is None else (theirs,)
    return pl.pallas_call(
        body, name=name, out_shape=SDS((NSH, tr, cols), BF16),
        grid_spec=pltpu.PrefetchScalarGridSpec(
            num_scalar_prefetch=1, grid=(NSH, nj),
            in_specs=[DEP_SPEC] * len(deps) + [pl.BlockSpec((T, tr), a_map), pl.BlockSpec((T, tc), g_map)]
            + [out] * len(extra),
            out_specs=out),
        compiler_params=_params(2))(place, *deps, a, g, *extra)


def _adamw(w, g, m, v):
    m = B1 * m + (1.0 - B1) * g
    v = B2 * v + (1.0 - B2) * (g * g)
    m_hat = m / (1.0 - B1 ** STEP)
    v_hat = v / (1.0 - B2 ** STEP)
    delta = -LR * (m_hat / (jnp.sqrt(v_hat) + AEPS) + WD * w)
    return delta, m, v


def _sum_halves(name, place, grads, theirs):
    _, rows, cols = theirs.shape
    tr = _row_tile(rows, cols, 1 << 20)

    def body(place_ref, a_ref, b_ref, o_ref):
        o_ref[...] = (a_ref[...].astype(F32) + b_ref[...].astype(F32)).astype(BF16)

    spec = pl.BlockSpec((None, tr, cols), lambda s, i, p: (s, i, 0))
    mine = spec if grads.ndim == 3 else pl.BlockSpec((None, None, tr, cols), lambda s, i, p: (s, p[1], i, 0))
    return pl.pallas_call(
        body, name=name, out_shape=SDS(theirs.shape, BF16),
        grid_spec=pltpu.PrefetchScalarGridSpec(
            num_scalar_prefetch=1, grid=(NSH, rows // tr), in_specs=[mine, spec], out_specs=spec),
        compiler_params=_params(2))(place, grads, theirs)


def _sum_landed(name, place, part, landed):
    _, rows, cols = part.shape
    tr = _row_tile(rows, cols, 1 << 20)

    def body(place_ref, p_ref, l_ref, o_ref):
        o_ref[...] = ((p_ref[...].astype(F32) + l_ref[0].astype(F32)) + l_ref[1].astype(F32)) + l_ref[2].astype(F32)

    return pl.pallas_call(
        body, name=name, out_shape=SDS((2, rows, cols), F32),
        grid_spec=pltpu.PrefetchScalarGridSpec(
            num_scalar_prefetch=1, grid=(rows // tr,),
            in_specs=[pl.BlockSpec((None, tr, cols), lambda i, p: (p[0], i, 0)),
                      pl.BlockSpec((3, tr, cols), lambda i, p: (0, i, 0))],
            out_specs=pl.BlockSpec((None, tr, cols), lambda i, p: (p[1], i, 0))),
        compiler_params=_params(1))(place, part, landed)


def _adam_shard(name, g, w, m, v):
    rows, cols = w.shape
    tr = _row_tile(rows, cols, 1 << 19)

    def body(g_ref, w_ref, m_ref, v_ref, go_ref, d_ref, nm_ref, nv_ref):
        g = g_ref[...]
        go_ref[...] = g
        d_ref[...], nm_ref[...], nv_ref[...] = _adamw(w_ref[...], g, m_ref[...], v_ref[...])

    spec = pl.BlockSpec((tr, cols), lambda i: (i, 0))
    return pl.pallas_call(
        body, name=name, out_shape=[SDS((rows, cols), F32)] * 4, grid=(rows // tr,),
        in_specs=[spec] * 4, out_specs=[spec] * 4, compiler_params=_params(1))(g, w, m, v)


def _adam_small(gathered, w, m, v):
    def body(g_ref, w_ref, m_ref, v_ref, go_ref, d_ref, nm_ref, nv_ref):
        g = g_ref[0:SMALL_ROWS, :]
        for dev in range(1, 8):
            g = g + g_ref[dev * SMALL_ROWS:(dev + 1) * SMALL_ROWS, :]
        go_ref[...] = g
        d_ref[...], nm_ref[...], nv_ref[...] = _adamw(w_ref[...], g, m_ref[...], v_ref[...])

    return pl.pallas_call(body, name="adam_small", out_shape=[SDS((SMALL_ROWS, HD), F32)] * 4)(gathered, w, m, v)


SMALL = (("norm_mix", (1, D)), ("b_gate", (1, 2 * D)), ("q_norm_a", (1, HD)), ("k_norm_a", (1, HD)),
         ("q_norm_b", (1, HD)), ("k_norm_b", (1, HD)), ("rpb_b", (1, 4, 15, 31)), ("norm_ffn", (1, D)))


def _pack_small(vals):
    pieces = []
    for (name, shape), val in zip(SMALL, vals):
        flat = val.reshape(-1)
        pad = (-flat.shape[0]) % HD
        pieces.append(jnp.pad(flat, (0, pad)).reshape(-1, HD))
    packed = jnp.concatenate(pieces, axis=0)
    return jnp.pad(packed, ((0, SMALL_ROWS - packed.shape[0]), (0, 0)))


def _unpack_small(packed):
    out, row = [], 0
    for name, shape in SMALL:
        size = int(np.prod(shape))
        nrows = -(-size // HD)
        out.append(packed[row:row + nrows].reshape(-1)[:size].reshape(shape))
        row += nrows
    return out


def kernel(x, norm_mix, w_in, b_gate, q_norm_a, k_norm_a, q_norm_b, k_norm_b, rpb_b, w_proj_a, w_proj_b, w_out, norm_ffn, w_up, w_down, loss_target, m_norm_mix, m_w_in, m_b_gate, m_q_norm_a, m_k_norm_a, m_q_norm_b, m_k_norm_b, m_rpb_b, m_w_proj_a, m_w_proj_b, m_w_out, m_norm_ffn, m_w_up, m_w_down, v_norm_mix, v_w_in, v_b_gate, v_q_norm_a, v_k_norm_a, v_q_norm_b, v_k_norm_b, v_rpb_b, v_w_proj_a, v_w_proj_b, v_w_out, v_norm_ffn, v_w_up, v_w_down):
    big_names = ("w_in", "w_proj_a", "w_proj_b", "w_out", "w_up", "w_down")
    big_w = [a[0] for a in (w_in, w_proj_a, w_proj_b, w_out, w_up, w_down)]
    big_m = [a[0] for a in (m_w_in, m_w_proj_a, m_w_proj_b, m_w_out, m_w_up, m_w_down)]
    big_v = [a[0] for a in (v_w_in, v_w_proj_a, v_w_proj_b, v_w_out, v_w_up, v_w_down)]
    x2, target = x[0], loss_target[0]

    place = jnp.stack([2 * lax.axis_index("x") + lax.axis_index("y"), lax.axis_index("c")]).astype(jnp.int32)
    groups = ((0,), (1, 2, 3), (4,), (5,))
    started = []
    for j, grp in enumerate(groups):
        deps = (started[0][4],) if j else ()
        placed = [_cast_into_place(big_w[i], "cast_" + big_names[i], place, deps) for i in grp]
        started.append(_gather_start(f"gather_start_{j}", placed))

    def whole(fulls):
        return [f.reshape(NSH, 2 * f.shape[2], f.shape[3]) for f in fulls]

    def forward_begin(j, after):
        send, recv, _, fulls, _ = started[j]
        fulls = _gather_wait(f"gather_wait_{j}", send, recv, fulls, after)
        send, recv, _, fulls, token = _forward_start(f"forward_start_{j}", fulls)
        return (send, recv, fulls), token

    def forward_end(j, state, after):
        return whole(_forward_wait(f"forward_wait_{j}", *state, after))

    def as_halves(grads):
        return [g.reshape(NSH, 2, g.shape[1] // 2, g.shape[2]) for g in grads]

    def reduce_start(j, grads, theirs):
        parts = [_sum_halves(f"sum_halves_{j}_{i}", place, a, b) for i, (a, b) in enumerate(zip(grads, theirs))]
        send, recv, parts, lands, token = _reduce_start(f"reduce_start_{j}", parts)
        return (send, recv, parts, lands), token

    def exchange_begin(j, grads):
        send, recv, grads, lands, token = _exchange_start(f"exchange_start_{j}", as_halves(grads))
        return (send, recv, grads, lands), token

    def exchange_end(j, state, after):
        return reduce_start(j, *_exchange_wait(f"exchange_wait_{j}", *state, after))

    big_out = {}

    def share_begin(j, state, after):
        send, recv, parts, lands = state
        parts, lands = _reduce_wait(f"reduce_wait_{j}", send, recv, parts, lands, after)
        sums = [_sum_landed(f"sum_landed_{j}_{i}", place, p, l) for i, (p, l) in enumerate(zip(parts, lands))]
        send, recv, _, sums, token = _share_start(f"share_start_{j}", sums)
        return (send, recv, sums), token

    def share_end(j, state, after):
        for idx, g in zip(groups[j], _share_wait(f"share_wait_{j}", *state, after)):
            g = g.reshape(big_w[idx].shape)
            big_out[idx] = _adam_shard("adam_" + big_names[idx], g, big_w[idx], big_m[idx], big_v[idx])
        return big_out[groups[j][-1]][1]

    proj, xn = _norm_in_proj_own(x2, norm_mix, whole(started[0][3])[0], place)
    send, recv, _, win, _ = started[0]
    win = _gather_wait("gather_wait_0", send, recv, win, (proj, *[s[4] for s in started[1:]]))
    (win_f,) = whole(_gather_finish("gather_finish_0", win))
    proj = _in_proj_rest(xn, win_f, proj, place)
    cos, sin = _rope_tables()
    nw = jnp.stack([jnp.concatenate([jnp.tile(q_norm_a, (1, NHA)), jnp.tile(q_norm_b, (1, NH - NHA))], axis=1),
                    jnp.concatenate([jnp.tile(k_norm_a, (1, NHA)), jnp.tile(k_norm_b, (1, NH - NHA))], axis=1)])
    qkn = _qk_prep(proj, nw, cos, sin)
    fw1, token = forward_begin(1, (qkn,))
    fwd_a = [_attn_a_fwd(qkn, proj, g) for g in range(3)]
    os, ls = [f[0] for f in fwd_a], [f[1] for f in fwd_a]
    fw2, token = forward_begin(2, (os[2], token))
    ob, lse_b, bias = _attn_b_fwd(qkn, proj, _rpb_rows(rpb_b[0]))
    oa, w0, w1, w2 = _comb_fwd(os, ls)
    ws = [w0, w1, w2]
    wpa_f, wpb_f, wout_f = forward_end(1, fw1, (oa, token))
    wout_f = wout_f.reshape(D, D)
    mixed, ob16 = _mix_fwd(oa, ob, proj, b_gate, wpa_f, wpb_f)
    h1, hn = _out_proj_fwd(mixed, wout_f, x2, norm_ffn)
    (wup_f,) = forward_end(2, fw2, (hn,))
    usq, u = _ffn_up(hn, wup_f)
    fw3, token = forward_begin(3, (u,))
    own = _ffn_down_own(usq, whole(fw3[2])[0].reshape(DFF, D), place)
    (wdown_f,) = forward_end(3, fw3, (own, token))
    wdown_f = wdown_f.reshape(DFF, D)
    dy, dy16, loss_parts = _ffn_down_loss(usq, wdown_f, h1, target, own, place)
    loss = lax.psum(jnp.sum(loss_parts[:, 0, 0]), ("x", "y", "c"))

    def halves(name, a, g, shard_rows, rows, cols, j, behind, deps):
        sib = _grad_w_half(f"grad_{name}_for_sibling", a, g, place, shard_rows, rows, cols, 1024, True, deps=deps)
        send, recv, sib, lands, token = _exchange_start(f"exchange_start_{j}", [sib], sliced=False)
        out = behind(token)
        _, theirs = _exchange_wait(f"exchange_wait_{j}", send, recv, sib, lands, out[:1], sliced=False)
        part = _grad_w_half(f"grad_{name}_own", a, g, place, shard_rows, rows, cols, 1024, False, theirs=theirs[0])
        send, recv, parts, lands, token = _reduce_start(f"reduce_start_{j}", [part])
        return (send, recv, parts, lands), token, out

    red_down, token, (du,) = halves(
        "w_down", usq, dy16, True, DFF // NSH, D, 3, lambda t: (_ffn_down_bwd(dy16, wdown_f, u, deps=(t,)),), ())
    red_up, token, (dh16, dh1, d_norm_ffn) = halves(
        "w_up", hn, du, False, D, DFF // NSH, 2,
        lambda t: (lambda r: (r[1], r[0], r[2]))(_ffn_up_bwd(du, wup_f, h1, dy, norm_ffn, deps=(t,))), (token,))
    dya, dyb, dga, dgb, doa, dob, dba, dbb = _mix_bwd(dh16, wout_f, oa, ob16, proj, b_gate, wpa_f, wpb_f)
    g_out = _grad_w("grad_w_out", mixed, dh16, True, D // NSH, D, 512, 1024)
    g_pa = _grad_w("grad_w_proj_a", oa, dya, False, 512, 512, 512, 512)
    g_pb = _grad_w("grad_w_proj_b", ob16, dyb, False, 512, 512, 512, 512)
    ex_mid, token_mid = exchange_begin(1, [g_pa, g_pb, g_out])
    cc = _comb_bwd(doa, os, ws, deps=(token, token_mid))
    bwd_a = [_attn_a_bwd(qkn, proj, doa, ls[g], ws[g], cc, g) for g in range(3)]
    red_mid, token = exchange_end(1, ex_mid, (bwd_a[2][1],))
    dqk_b, dv_b, drpb_t = _attn_b_bwd(qkn, proj, dob, ob, lse_b, bias, deps=(token,))
    dproj, dn = _qk_bwd(proj, nw, cos, sin, [b[0] for b in bwd_a], dqk_b, [b[1] for b in bwd_a] + [dv_b], dga, dgb)
    g_in_theirs = _grad_w_half("grad_w_in_for_sibling", xn, dproj, place, False, D, DIN // NSH, 1280, True)
    send, recv, g_in_theirs, lands, token = _exchange_start("exchange_start_0", [g_in_theirs], sliced=False)
    g_in_mine = _grad_w_half("grad_w_in_own", xn, dproj, place, False, D, DIN // NSH, 1280, False, deps=(token,))
    _, theirs = _exchange_wait("exchange_wait_0", send, recv, g_in_theirs, lands, (g_in_mine,), sliced=False)
    red_in, token = reduce_start(0, [g_in_mine], theirs)
    grad_x, d_norm_mix = _in_proj_bwd(dproj, win_f, x2, dh1, norm_mix, deps=(token,))

    sh_down, token = share_begin(3, red_down, (grad_x,))
    sh_up, token = share_begin(2, red_up, (token,))
    done = share_end(3, sh_down, (token,))
    sh_mid, token = share_begin(1, red_mid, (done,))
    done = share_end(2, sh_up, (token,))
    sh_in, token = share_begin(0, red_in, (done,))
    done = share_end(1, sh_mid, (token,))
    done = share_end(0, sh_in, (done,))

    d_rpb = drpb_t[:, :15, GRID_W - WIN_C:GRID_W + WIN_C - 1]
    small_g = [d_norm_mix, jnp.concatenate([dba, dbb], axis=1), dn[0, 0], dn[1, 0], dn[0, 1], dn[1, 1], d_rpb, d_norm_ffn]
    gathered_small = _allgather_small(_pack_small(small_g), done)
    small_w = (norm_mix, b_gate, q_norm_a, k_norm_a, q_norm_b, k_norm_b, rpb_b, norm_ffn)
    small_m = (m_norm_mix, m_b_gate, m_q_norm_a, m_k_norm_a, m_q_norm_b, m_k_norm_b, m_rpb_b, m_norm_ffn)
    small_v = (v_norm_mix, v_b_gate, v_q_norm_a, v_k_norm_a, v_q_norm_b, v_k_norm_b, v_rpb_b, v_norm_ffn)
    small_out = [_unpack_small(p) for p in
                 _adam_small(gathered_small, _pack_small(small_w), _pack_small(small_m), _pack_small(small_v))]

    order = ("norm_mix", "w_in", "b_gate", "q_norm_a", "k_norm_a", "q_norm_b", "k_norm_b", "rpb_b",
             "w_proj_a", "w_proj_b", "w_out", "norm_ffn", "w_up", "w_down")
    small_idx = {name: i for i, (name, _) in enumerate(SMALL)}
    outs = []
    for kind in range(4):
        for name in order:
            if name in small_idx:
                outs.append(small_out[kind][small_idx[name]])
            else:
                outs.append(big_out[big_names.index(name)][kind][None])
    return (loss, grad_x[None], *outs)
```

```python
import numpy as np
import jax
import jax.numpy as jnp
from jax import lax
from jax.experimental import pallas as pl
from jax.experimental.pallas import tpu as pltpu

F32, BF16 = jnp.float32, jnp.bfloat16
SDS = jax.ShapeDtypeStruct
MESH = pl.DeviceIdType.MESH

T = 2048
D = 2048
HD = 128
NH, NHA = 16, 12
DIN = 10240
DFF = 8192
NSH = 4
DILS = (1, 4, 16)
EPS = 1e-6
NEG = -1e30
SCALE = HD ** -0.5
GRID_W, WIN_R, WIN_C = 64, 8, 16
VMEM_LIMIT = 56 * 1024 * 1024
B1, B2, LR, AEPS, WD, STEP = 0.9, 0.999, 0.001, 1e-08, 0.01, 10
SMALL_ROWS = 88


def _dot(a, b):
    return jnp.dot(a, b, preferred_element_type=F32)


def _dot_nt(a, b):
    return lax.dot_general(a, b, (((1,), (1,)), ((), ())), preferred_element_type=F32)


def _dot_tn(a, b):
    return lax.dot_general(a, b, (((0,), (0,)), ((), ())), preferred_element_type=F32)


def _params(n):
    return pltpu.CompilerParams(dimension_semantics=("arbitrary",) * n, vmem_limit_bytes=VMEM_LIMIT)


def _resident(shape, index_map):
    return pl.BlockSpec(shape, index_map, pipeline_mode=pl.Buffered(1))


def _sigmoid(z):
    return 1.0 / (1.0 + jnp.exp(-z))


def _wide(v, n):
    return jnp.concatenate([v] * n, axis=1)


def _row_tile(rows, cols, elems):
    tr = 16
    while tr * 2 <= rows and tr * 2 * cols <= elems:
        tr *= 2
    return tr


def _place():
    x, y, c = lax.axis_index("x"), lax.axis_index("y"), lax.axis_index("c")
    peers = [(1 - x, y), (x, 1 - y), (1 - x, 1 - y)]
    return x, y, c, peers


def _cast_into_place(w, name, place, deps=()):
    rows, cols = w.shape
    hr = rows // 2
    tr = min(hr, 256)
    per = hr // tr

    def body(*refs):
        w_ref, o_ref = refs[-2:]
        o_ref[...] = w_ref[...].astype(BF16)

    return pl.pallas_call(
        body, name=name, out_shape=SDS((NSH, 2, hr, cols), BF16),
        grid_spec=pltpu.PrefetchScalarGridSpec(
            num_scalar_prefetch=1, grid=(2, per),
            in_specs=[DEP_SPEC] * len(deps) + [pl.BlockSpec((tr, cols), lambda h, i, p: (h * per + i, 0))],
            out_specs=pl.BlockSpec((None, None, tr, cols), lambda h, i, p: (p[0], h, i, 0))),
        compiler_params=_params(2))(place, *deps, w)


ANY_SPEC = pl.BlockSpec(memory_space=pl.ANY)
HBM_SPEC = pl.BlockSpec(memory_space=pltpu.HBM)
SEM_SPEC = pl.BlockSpec(memory_space=pltpu.SEMAPHORE)
DEP_SPEC = pl.BlockSpec((8, 128), lambda *_: (0, 0))
EFFECT = pltpu.SideEffectType.DATAFLOW_SIDE_EFFECTING


def _after(body, deps):
    n = len(deps)
    return (lambda *refs: body(*refs[n:])) if n else body


SIBLING_BARRIER = 1


def _split_start(name, srcs, lands, n_copies, issue, sibling_only=False):
    n, m = len(srcs), len(lands)

    def body(*refs):
        if sibling_only:
            x, y, c, _ = _place()
            barrier = pltpu.get_barrier_semaphore()
            pl.semaphore_signal(barrier, inc=1, device_id=(x, y, 1 - c), device_id_type=MESH)
            pl.semaphore_wait(barrier, 1)
        issue(refs[:n], refs[n:n + m], refs[n + m], refs[n + m + 1])
        refs[-1][...] = jnp.zeros((8, 128), F32)

    arrays = list(srcs) + list(lands)
    outs = pl.pallas_call(
        body, name=name,
        out_shape=(pltpu.SemaphoreType.DMA((n_copies,)), pltpu.SemaphoreType.DMA((n_copies,)),
                   *[pltpu.HBM(a.shape, a.dtype) for a in arrays], SDS((8, 128), F32)),
        in_specs=[HBM_SPEC] * (n + m),
        out_specs=(SEM_SPEC, SEM_SPEC, *[HBM_SPEC] * (n + m), pl.BlockSpec(memory_space=pltpu.VMEM)),
        input_output_aliases={i: 2 + i for i in range(n + m)},
        compiler_params=pltpu.CompilerParams(has_side_effects=EFFECT,
                                             collective_id=SIBLING_BARRIER if sibling_only else None),
    )(*[pltpu.with_memory_space_constraint(a, pltpu.HBM) for a in arrays])
    return outs[0], outs[1], list(outs[2:2 + n]), list(outs[2 + n:2 + n + m]), outs[-1]


def _split_wait(name, send_sems, recv_sems, srcs, lands, after, wait):
    n, m = len(srcs), len(lands)

    def body(*refs):
        wait(refs[:n], refs[n:n + m], refs[n + m], refs[n + m + 1])

    arrays = list(srcs) + list(lands)
    outs = pl.pallas_call(
        body, name=name, out_shape=[pltpu.HBM(a.shape, a.dtype) for a in arrays],
        in_specs=[HBM_SPEC] * (n + m) + [SEM_SPEC, SEM_SPEC] + [ANY_SPEC] * len(after),
        out_specs=[HBM_SPEC] * (n + m), input_output_aliases={i: i for i in range(n + m)},
        compiler_params=pltpu.CompilerParams(has_side_effects=EFFECT),
    )(*arrays, send_sems, recv_sems, *after)
    return list(outs[:n]), list(outs[n:])


def _gather_start(name, fulls):
    def issue(srcs, dsts, send_sems, recv_sems):
        x, y, c, peers = _place()
        for i in range(len(fulls)):
            mine = dsts[i].at[2 * x + y, c]
            for k, (px, py) in enumerate(peers):
                pltpu.make_async_remote_copy(
                    src_ref=mine, dst_ref=mine, send_sem=send_sems.at[3 * i + k],
                    recv_sem=recv_sems.at[3 * i + k], device_id=(px, py, c), device_id_type=MESH).start()

    return _split_start(name, [], fulls, 3 * len(fulls), issue)


def _gather_wait(name, send_sems, recv_sems, fulls, after):
    def wait(srcs, dsts, send_sems, recv_sems):
        x, y, c, peers = _place()
        for i in range(len(fulls)):
            for k, (px, py) in enumerate(peers):
                cp = pltpu.make_async_remote_copy(
                    src_ref=dsts[i].at[2 * x + y, c], dst_ref=dsts[i].at[2 * px + py, c],
                    send_sem=send_sems.at[3 * i + k], recv_sem=recv_sems.at[3 * i + k],
                    device_id=(px, py, c), device_id_type=MESH)
                cp.wait_send()
                cp.wait_recv()

    return _split_wait(name, send_sems, recv_sems, [], fulls, after, wait)[1]


def _gather_finish(name, fulls):
    n = len(fulls)

    def body(*refs):
        fin, fout = refs[:n], refs[n:2 * n]
        send_sems, recv_sems = refs[2 * n:]
        x, y, c, peers = _place()

        def copy(i, k, half):
            px, py = peers[k]
            return pltpu.make_async_remote_copy(
                src_ref=fin[i].at[2 * px + py, half], dst_ref=fout[i].at[2 * px + py, half],
                send_sem=send_sems.at[3 * i + k], recv_sem=recv_sems.at[3 * i + k],
                device_id=(x, y, 1 - c), device_id_type=MESH)

        sends = [copy(i, k, c) for i in range(n) for k in range(3)]
        for cp in sends:
            cp.start()
        for i in range(n):
            for k in range(3):
                copy(i, k, 1 - c).wait_recv()
        for cp in sends:
            cp.wait_send()

    return pl.pallas_call(
        body, name=name, out_shape=[SDS(f.shape, f.dtype) for f in fulls],
        in_specs=[ANY_SPEC] * n, out_specs=[ANY_SPEC] * n, input_output_aliases={i: i for i in range(n)},
        scratch_shapes=[pltpu.SemaphoreType.DMA((3 * n,)), pltpu.SemaphoreType.DMA((3 * n,))])(*fulls)


def _reduce_start(name, parts):
    lands = [lax.empty((3,) + p.shape[1:], p.dtype) for p in parts]

    def issue(srcs, dsts, send_sems, recv_sems):
        x, y, c, peers = _place()
        for i in range(len(parts)):
            for k, (px, py) in enumerate(peers):
                pltpu.make_async_remote_copy(
                    src_ref=srcs[i].at[2 * px + py], dst_ref=dsts[i].at[k], send_sem=send_sems.at[3 * i + k],
                    recv_sem=recv_sems.at[3 * i + k], device_id=(px, py, c), device_id_type=MESH).start()

    return _split_start(name, parts, lands, 3 * len(parts), issue)


def _reduce_wait(name, send_sems, recv_sems, parts, lands, after):
    def wait(srcs, dsts, send_sems, recv_sems):
        x, y, c, peers = _place()
        for i in range(len(parts)):
            for k, (px, py) in enumerate(peers):
                cp = pltpu.make_async_remote_copy(
                    src_ref=srcs[i].at[2 * px + py], dst_ref=dsts[i].at[k], send_sem=send_sems.at[3 * i + k],
                    recv_sem=recv_sems.at[3 * i + k], device_id=(px, py, c), device_id_type=MESH)
                cp.wait_send()
                cp.wait_recv()

    return _split_wait(name, send_sems, recv_sems, parts, lands, after, wait)


def _sibling_copy(src, dst, send_sems, recv_sems, k):
    x, y, c, _ = _place()
    return pltpu.make_async_remote_copy(src_ref=src, dst_ref=dst, send_sem=send_sems.at[k], recv_sem=recv_sems.at[k],
                                        device_id=(x, y, 1 - c), device_id_type=MESH)


def _forward_start(name, fulls):
    def issue(srcs, dsts, send_sems, recv_sems):
        x, y, c, peers = _place()
        for i in range(len(fulls)):
            for k, (px, py) in enumerate(peers):
                part = dsts[i].at[2 * px + py, c]
                _sibling_copy(part, part, send_sems, recv_sems, 3 * i + k).start()

    return _split_start(name, [], fulls, 3 * len(fulls), issue, sibling_only=True)


def _forward_wait(name, send_sems, recv_sems, fulls, after):
    def wait(srcs, dsts, send_sems, recv_sems):
        x, y, c, peers = _place()
        for i in range(len(fulls)):
            for k, (px, py) in enumerate(peers):
                cp = _sibling_copy(dsts[i].at[2 * px + py, c], dsts[i].at[2 * px + py, 1 - c], send_sems, recv_sems, 3 * i + k)
                cp.wait_send()
                cp.wait_recv()

    return _split_wait(name, send_sems, recv_sems, [], fulls, after, wait)[1]


def _exchange_start(name, grads, sliced=True):
    lands = [lax.empty((NSH,) + g.shape[-2:], g.dtype) for g in grads]

    def issue(srcs, dsts, send_sems, recv_sems):
        c = lax.axis_index("c")
        for i in range(len(grads)):
            src = srcs[i].at[:, 1 - c] if sliced else srcs[i]
            _sibling_copy(src, dsts[i], send_sems, recv_sems, i).start()

    return _split_start(name, grads, lands, len(grads), issue, sibling_only=True)


def _exchange_wait(name, send_sems, recv_sems, grads, lands, after, sliced=True):
    def wait(srcs, dsts, send_sems, recv_sems):
        c = lax.axis_index("c")
        for i in range(len(grads)):
            cp = _sibling_copy(srcs[i].at[:, 1 - c] if sliced else srcs[i], dsts[i], send_sems, recv_sems, i)
            cp.wait_send()
            cp.wait_recv()

    return _split_wait(name, send_sems, recv_sems, grads, lands, after, wait)


def _share_start(name, sums):
    def issue(srcs, dsts, send_sems, recv_sems):
        c = lax.axis_index("c")
        for i in range(len(sums)):
            _sibling_copy(dsts[i].at[c], dsts[i].at[c], send_sems, recv_sems, i).start()

    return _split_start(name, [], sums, len(sums), issue, sibling_only=True)


def _share_wait(name, send_sems, recv_sems, sums, after):
    def wait(srcs, dsts, send_sems, recv_sems):
        c = lax.axis_index("c")
        for i in range(len(sums)):
            cp = _sibling_copy(dsts[i].at[c], dsts[i].at[1 - c], send_sems, recv_sems, i)
            cp.wait_send()
            cp.wait_recv()

    return _split_wait(name, send_sems, recv_sems, [], sums, after, wait)[1]


def _allgather_small(v, after):
    m_per, n = v.shape

    def body(x_ref, after_ref, out_ref, send_sems, recv_sems, local_sem):
        x, y, c = lax.axis_index("x"), lax.axis_index("y"), lax.axis_index("c")
        me, sibling = (x, y, c), (x, y, 1 - c)
        chips = [(1 - x, y), (x, 1 - y), (1 - x, 1 - y)]

        def rows(px, py, pc):
            return out_ref.at[pl.ds((4 * px + 2 * py + pc) * m_per, m_per), :]

        def copy(k, block, to, src=None):
            return pltpu.make_async_remote_copy(
                src_ref=rows(*block) if src is None else src, dst_ref=rows(*block),
                send_sem=send_sems.at[k], recv_sem=recv_sems.at[k], device_id=to, device_id_type=MESH)

        mine = pltpu.make_async_copy(x_ref, rows(*me), local_sem)
        mine.start()
        first = [copy(0, me, sibling, src=x_ref)]
        first += [copy(1 + j, me, (*chip, c), src=x_ref) for j, chip in enumerate(chips)]
        for cp in first:
            cp.start()
        passed = [copy(4 + j, (*chip, c), sibling) for j, chip in enumerate(chips)]
        for j, chip in enumerate(chips):
            copy(1 + j, (*chip, c), me).wait_recv()
            passed[j].start()
        copy(0, sibling, me).wait_recv()
        for j, chip in enumerate(chips):
            copy(4 + j, (*chip, 1 - c), me).wait_recv()
        for cp in first + passed:
            cp.wait_send()
        mine.wait()

    return pl.pallas_call(
        body, name="allgather_small", out_shape=SDS((8 * m_per, n), v.dtype),
        in_specs=[pl.BlockSpec(memory_space=pltpu.VMEM), ANY_SPEC], out_specs=pl.BlockSpec(memory_space=pltpu.VMEM),
        scratch_shapes=[pltpu.SemaphoreType.DMA((7,)), pltpu.SemaphoreType.DMA((7,)), pltpu.SemaphoreType.DMA])(v, after)


def _norm_in_proj_own(x, g, w_full, place):
    tn, chunk = 512, 256
    per = (DIN // NSH) // tn

    def body(place_ref, x_ref, g_ref, w_ref, proj_ref, xn_ref):
        @pl.when(pl.program_id(0) == 0)
        def _():
            def norm(r, carry):
                rows = pl.ds(pl.multiple_of(r * chunk, chunk), chunk)
                xv = x_ref[rows, :]
                rs = lax.rsqrt(jnp.mean(xv * xv, axis=-1, keepdims=True) + EPS)
                xn_ref[rows, :] = (xv * rs * g_ref[...]).astype(BF16)
                return carry

            lax.fori_loop(0, T // chunk, norm, 0)

        proj_ref[...] = _dot(xn_ref[...], w_ref[...])

    return pl.pallas_call(
        body, name="norm_in_proj_own", out_shape=[SDS((T, DIN), F32), SDS((T, D), BF16)],
        grid_spec=pltpu.PrefetchScalarGridSpec(
            num_scalar_prefetch=1, grid=(per,),
            in_specs=[_resident((T, D), lambda j, p: (0, 0)),
                      pl.BlockSpec((1, D), lambda j, p: (0, 0)),
                      pl.BlockSpec((None, D, tn), lambda j, p: (p[0], 0, j))],
            out_specs=[pl.BlockSpec((T, tn), lambda j, p: (0, p[0] * per + j)),
                       pl.BlockSpec((T, D), lambda j, p: (0, 0))]),
        compiler_params=_params(1))(place, x, g, w_full)


def _in_proj_rest(xn, w_full, proj, place):
    tn = 512
    per = (DIN // NSH) // tn

    def body(place_ref, xn_ref, w_ref, proj_in, proj_ref):
        proj_ref[...] = _dot(xn_ref[...], w_ref[...])

    shard = lambda j, p: p[0] ^ jnp.where(j < per, 2, jnp.where(j < 2 * per, 1, 3))
    return pl.pallas_call(
        body, name="in_proj_rest", out_shape=SDS((T, DIN), F32),
        grid_spec=pltpu.PrefetchScalarGridSpec(
            num_scalar_prefetch=1, grid=((NSH - 1) * per,),
            in_specs=[_resident((T, D), lambda j, p: (0, 0)),
                      pl.BlockSpec((None, D, tn), lambda j, p: (shard(j, p), 0, j % per)), ANY_SPEC],
            out_specs=pl.BlockSpec((T, tn), lambda j, p: (0, shard(j, p) * per + j % per))),
        input_output_aliases={3: 0}, compiler_params=_params(1))(place, xn, w_full, proj)


def _rope_tables():
    pos = np.arange(T, dtype=np.float32)
    inv = (10000.0 ** (-np.arange(0, HD, 2, dtype=np.float32) / HD)).astype(np.float32)
    ang = (pos[:, None] * inv[None, :]).astype(np.float32)
    cos, sin = np.cos(ang).astype(np.float32), np.sin(ang).astype(np.float32)
    return (jnp.asarray(np.concatenate([cos, cos], axis=1)), jnp.asarray(np.concatenate([-sin, sin], axis=1)))


def _qk_prep(proj, nw, cos, sin):
    tm = 256

    def body(p_ref, w_ref, cos_ref, sin_ref, o_ref):
        cv, sv = cos_ref[...], sin_ref[...]
        for h in range(NH):
            sl = slice(h * HD, (h + 1) * HD)
            xv = p_ref[:, sl]
            r = lax.rsqrt(jnp.mean(xv * xv, axis=-1, keepdims=True) + EPS)
            z = xv * r * w_ref[:, sl]
            if h < NHA:
                z = z * cv + pltpu.roll(z, 64, 1) * sv
            o_ref[:, sl] = z.astype(BF16)

    return pl.pallas_call(
        body, name="qk_prep", out_shape=SDS((T, 2 * D), BF16), grid=(T // tm, 2),
        in_specs=[pl.BlockSpec((tm, D), lambda i, j: (i, j)),
                  pl.BlockSpec((None, 1, D), lambda i, j: (j, 0, 0)),
                  pl.BlockSpec((tm, HD), lambda i, j: (i, 0)),
                  pl.BlockSpec((tm, HD), lambda i, j: (i, 0))],
        out_specs=pl.BlockSpec((tm, D), lambda i, j: (i, j)),
        compiler_params=_params(2))(proj, nw, cos, sin)


def _band_mask(q0, m):
    ii = lax.broadcasted_iota(jnp.int32, (128, 256), 0)
    jj = lax.broadcasted_iota(jnp.int32, (128, 256), 1)
    rel = jj - ii
    kpos = jj + (q0 - 64)
    return (rel >= 0) & (rel <= 128) & (kpos >= 0) & (kpos < m)


def _fill_padded(dst, src, m):
    zeros = jnp.zeros((64, HD), dst.dtype)
    dst[0:64, :] = zeros
    dst[64 + m:128 + m, :] = zeros
    dst[64:64 + m, :] = src.astype(dst.dtype)


def _residue_rows(r, m, dil):
    return pl.ds(r, m, stride=dil) if dil > 1 else slice(None)


def _head_blocks(g):
    col = lambda base: pl.BlockSpec((T, HD), lambda h: (0, base + g * 4 + h))
    return col(0), col(NH), col(2 * NH), pl.BlockSpec((T, HD), lambda h: (0, h))


def _attn_a_fwd(qkn, proj, g):
    dil = DILS[g]
    m = T // dil
    nb = m // 128

    def body(q_ref, k_ref, v_ref, o_ref, l_ref, qf, kf, qp, kp, vp, ob, lb):
        qf[...] = q_ref[...].astype(F32)
        kf[...] = k_ref[...].astype(F32)
        for r in range(dil):
            rows = _residue_rows(r, m, dil)
            qp[...] = qf[rows, :].astype(BF16)
            _fill_padded(kp, kf[rows, :], m)
            _fill_padded(vp, v_ref[rows, :], m)

            def block(b, carry):
                q0 = pl.multiple_of(b * 128, 128)
                kw, vw = kp[pl.ds(q0, 256), :], vp[pl.ds(q0, 256), :]
                s = _dot_nt(qp[pl.ds(q0, 128), :], kw) * SCALE
                s = jnp.where(_band_mask(q0, m), s, NEG)
                mx = jnp.max(s, axis=-1, keepdims=True)
                e = jnp.exp(s - mx)
                den = jnp.sum(e, axis=-1, keepdims=True)
                ob[pl.ds(q0, 128), :] = _dot((e / den).astype(BF16), vw)
                lb[pl.ds(q0, 128), :] = jnp.broadcast_to(mx + jnp.log(den), (128, HD))
                return carry

            lax.fori_loop(0, nb, block, 0, unroll=min(nb, 16))
            o_ref[rows, :] = ob[...]
            l_ref[rows, :] = lb[...]

    q_blk, k_blk, v_blk, out_blk = _head_blocks(g)
    return pl.pallas_call(
        body, name=f"attn_a_fwd_{g}", out_shape=[SDS((T, 512), F32)] * 2, grid=(4,),
        in_specs=[q_blk, k_blk, v_blk], out_specs=[out_blk] * 2,
        scratch_shapes=[pltpu.VMEM((T, HD), F32), pltpu.VMEM((T, HD), F32), pltpu.VMEM((m, HD), BF16),
                        pltpu.VMEM((m + 128, HD), BF16), pltpu.VMEM((m + 128, HD), BF16),
                        pltpu.VMEM((m, HD), F32), pltpu.VMEM((m, HD), F32)],
        compiler_params=_params(1))(qkn, qkn, proj)


def _nbr_window(r):
    start = jnp.clip(r - WIN_R // 2, 0, T // GRID_W - WIN_R)
    return start, start - r + (WIN_R - 1)


def _rpb_rows(rpb):
    zeros = jnp.zeros((4, 14, 33), F32)
    a, b = rpb[:, :14], rpb[:, 1:15]
    rows = jnp.concatenate([a[:, :, 15:31], zeros, b, zeros, a[:, :, 0:15]], axis=2)
    return jnp.pad(rows, ((0, 0), (0, 2), (0, 0)))


def _attn_b_fwd(qkn, proj, rpb_rows):
    def body(r_ref, q_ref, k_ref, v_ref, o_ref, l_ref, bias_ref, vb, pair):
        qc = lax.broadcasted_iota(jnp.int32, (GRID_W, 512), 0)
        kc = lax.broadcasted_iota(jnp.int32, (GRID_W, 512), 1) & (GRID_W - 1)
        cs = jnp.clip(qc - WIN_C // 2, 0, GRID_W - WIN_C)
        colmask = (kc >= cs) & (kc < cs + WIN_C)
        for d in range(14):
            pair[d] = pltpu.roll(jnp.broadcast_to(r_ref[d:d + 1, :], (GRID_W, HD)), 0, 1, stride=1, stride_axis=0)
        for off in range(8):
            rows = jnp.concatenate([pair[off + 2 * jj] for jj in range(4)], axis=1)
            bias_ref[off] = jnp.where(colmask, rows, NEG)
        vb[...] = v_ref[...].astype(BF16)

        def row(r, carry):
            start, off = _nbr_window(r)
            q0 = pl.multiple_of(r * GRID_W, GRID_W)
            k0 = pl.multiple_of(start * GRID_W, GRID_W)
            s = _dot_nt(q_ref[pl.ds(q0, GRID_W), :], k_ref[pl.ds(k0, 512), :]) * SCALE + bias_ref[off]
            mx = jnp.max(s, axis=-1, keepdims=True)
            e = jnp.exp(s - mx)
            den = jnp.sum(e, axis=-1, keepdims=True)
            o_ref[pl.ds(q0, GRID_W), :] = _dot((e / den).astype(BF16), vb[pl.ds(k0, 512), :])
            l_ref[pl.ds(q0, GRID_W), :] = jnp.broadcast_to(mx + jnp.log(den), (GRID_W, HD))
            return carry

        lax.fori_loop(0, T // GRID_W, row, 0, unroll=16)

    return pl.pallas_call(
        body, name="attn_b_fwd",
        out_shape=[SDS((T, 512), F32), SDS((T, 512), F32), SDS((4, 8, GRID_W, 512), F32)], grid=(4,),
        in_specs=[pl.BlockSpec((None, 16, HD), lambda h: (h, 0, 0)),
                  pl.BlockSpec((T, HD), lambda h: (0, NHA + h)),
                  pl.BlockSpec((T, HD), lambda h: (0, NH + NHA + h)),
                  pl.BlockSpec((T, HD), lambda h: (0, 2 * NH + NHA + h))],
        out_specs=[pl.BlockSpec((T, HD), lambda h: (0, h)), pl.BlockSpec((T, HD), lambda h: (0, h)),
                   pl.BlockSpec((None, 8, GRID_W, 512), lambda h: (h, 0, 0, 0))],
        scratch_shapes=[pltpu.VMEM((T, HD), BF16), pltpu.VMEM((14, GRID_W, HD), F32)],
        compiler_params=_params(1))(rpb_rows, qkn, qkn, proj)


def _comb_fwd(os, ls):
    tm = 512

    def body(o0, o1, o2, l0, l1, l2, oa_ref, w0, w1, w2):
        lv = [l0[...], l1[...], l2[...]]
        mx = jnp.maximum(jnp.maximum(lv[0], lv[1]), lv[2])
        ev = [jnp.exp(l - mx) for l in lv]
        den = ev[0] + ev[1] + ev[2]
        wv = [e / den for e in ev]
        oa_ref[...] = (wv[0] * o0[...] + wv[1] * o1[...] + wv[2] * o2[...]).astype(BF16)
        w0[...], w1[...], w2[...] = wv

    spec = pl.BlockSpec((tm, 512), lambda i: (i, 0))
    return pl.pallas_call(
        body, name="comb_fwd", out_shape=[SDS((T, 512), BF16)] + [SDS((T, 512), F32)] * 3, grid=(T // tm,),
        in_specs=[spec] * 6, out_specs=[spec] * 4, compiler_params=_params(1))(*os, *ls)


def _mix_fwd(oa, ob, proj, b_gate, wpa, wpb):
    tm = 512

    def body(oa_ref, ob_ref, ga_ref, gb_ref, ba_ref, bb_ref, wpa_ref, wpb_ref, mixed_ref, ob16_ref):
        oav = oa_ref[...]
        obv = ob_ref[...].astype(BF16)
        ob16_ref[...] = obv
        for s in range(NSH):
            sl = slice(s * 512, (s + 1) * 512)
            ga = _sigmoid(ga_ref[:, sl] + ba_ref[:, sl])
            gb = _sigmoid(gb_ref[:, sl] + bb_ref[:, sl])
            mixed_ref[:, sl] = (ga * _dot(oav, wpa_ref[s]) + gb * _dot(obv, wpb_ref[s])).astype(BF16)

    row = lambda w: pl.BlockSpec((tm, w), lambda i: (i, 0))
    return pl.pallas_call(
        body, name="mix_fwd", out_shape=[SDS((T, D), BF16), SDS((T, 512), BF16)], grid=(T // tm,),
        in_specs=[row(512), row(512),
                  pl.BlockSpec((tm, D), lambda i: (i, 3)), pl.BlockSpec((tm, D), lambda i: (i, 4)),
                  pl.BlockSpec((1, D), lambda i: (0, 0)), pl.BlockSpec((1, D), lambda i: (0, 1)),
                  _resident((NSH, 512, 512), lambda i: (0, 0, 0)), _resident((NSH, 512, 512), lambda i: (0, 0, 0))],
        out_specs=[row(D), row(512)], compiler_params=_params(1))(oa, ob, proj, proj, b_gate, b_gate, wpa, wpb)


def _out_proj_fwd(mixed, w_out, x, g):
    tm = 512

    def body(m_ref, w_ref, x_ref, g_ref, h1_ref, hn_ref):
        h1 = x_ref[...] + _dot(m_ref[...], w_ref[...])
        h1_ref[...] = h1
        r = lax.rsqrt(jnp.mean(h1 * h1, axis=-1, keepdims=True) + EPS)
        hn_ref[...] = (h1 * r * g_ref[...]).astype(BF16)

    row = pl.BlockSpec((tm, D), lambda i: (i, 0))
    return pl.pallas_call(
        body, name="out_proj_fwd", out_shape=[SDS((T, D), F32), SDS((T, D), BF16)], grid=(T // tm,),
        in_specs=[row, _resident((D, D), lambda i: (0, 0)), row, pl.BlockSpec((1, D), lambda i: (0, 0))],
        out_specs=[row, row], compiler_params=_params(1))(mixed, w_out, x, g)


def _ffn_up(hn, w_up):
    tm, tn = T, 512
    per = (DFF // NSH) // tn

    def body(h_ref, w_ref, a_ref, u_ref):
        uv = jnp.maximum(_dot(h_ref[...], w_ref[...]), 0.0)
        a_ref[...] = (uv * uv).astype(BF16)
        u_ref[...] = uv.astype(BF16)

    out = pl.BlockSpec((tm, tn), lambda i, j: (i, j))
    return pl.pallas_call(
        body, name="ffn_up", out_shape=[SDS((T, DFF), BF16)] * 2, grid=(T // tm, DFF // tn),
        in_specs=[pl.BlockSpec((tm, D), lambda i, j: (i, 0)),
                  pl.BlockSpec((None, D, tn), lambda i, j: (j // per, 0, j % per))],
        out_specs=[out, out], compiler_params=_params(2))(hn, w_up)


def _ffn_down_own(u, w_down, place):
    tm, tk = 512, DFF // NSH

    def body(place_ref, u_ref, w_ref, o_ref):
        o_ref[...] = _dot(u_ref[...], w_ref[...])

    return pl.pallas_call(
        body, name="ffn_down_own", out_shape=SDS((T, D), F32),
        grid_spec=pltpu.PrefetchScalarGridSpec(
            num_scalar_prefetch=1, grid=(T // tm,),
            in_specs=[pl.BlockSpec((tm, tk), lambda i, p: (i, p[0])), pl.BlockSpec((tk, D), lambda i, p: (p[0], 0))],
            out_specs=pl.BlockSpec((tm, D), lambda i, p: (i, 0))),
        compiler_params=_params(1))(place, u, w_down)


def _ffn_down_loss(u, w_down, h1, target, own, place):
    tm, tk = 512, DFF // NSH
    nk = NSH - 1

    def body(place_ref, u_ref, w_ref, h1_ref, t_ref, own_ref, dy_ref, dy16_ref, loss_ref, acc):
        k = pl.program_id(1)

        @pl.when(k == 0)
        def _():
            acc[...] = own_ref[...]

        acc[...] += _dot(u_ref[...], w_ref[...])

        @pl.when(k == nk - 1)
        def _():
            def chunk(r, sq):
                rows = pl.ds(pl.multiple_of(r * 16, 16), 16)
                err = acc[rows, :] + h1_ref[rows, :] - t_ref[rows, :]
                dy = err * (1.0 / D)
                dy_ref[rows, :] = dy
                dy16_ref[rows, :] = dy.astype(BF16)
                return sq + err * err

            sq = lax.fori_loop(0, tm // 16, chunk, jnp.zeros((16, D), F32), unroll=2)
            part = 0.5 * jnp.sum(jnp.mean(sq, axis=-1, keepdims=True), axis=0, keepdims=True)
            loss_ref[...] = jnp.broadcast_to(part, (8, 128))

    row = pl.BlockSpec((tm, D), lambda i, k, p: (i, 0))
    once = _resident((tm, D), lambda i, k, p: (i, 0))
    shard = lambda k, p: p[0] ^ (k + 1)
    return pl.pallas_call(
        body, name="ffn_down_loss",
        out_shape=[SDS((T, D), F32), SDS((T, D), BF16), SDS((T // tm, 8, 128), F32)],
        grid_spec=pltpu.PrefetchScalarGridSpec(
            num_scalar_prefetch=1, grid=(T // tm, nk),
            in_specs=[pl.BlockSpec((tm, tk), lambda i, k, p: (i, shard(k, p))),
                      pl.BlockSpec((tk, D), lambda i, k, p: (shard(k, p), 0)), once, once, once],
            out_specs=[row, row, pl.BlockSpec((None, 8, 128), lambda i, k, p: (i, 0, 0))],
            scratch_shapes=[pltpu.VMEM((tm, D), F32)]),
        compiler_params=_params(2))(place, u, w_down, h1, target, own)


def _ffn_down_bwd(dy16, w_down, u, deps=()):
    tm, tn = T, 512

    def body(dy_ref, w_ref, u_ref, du_ref):
        uv = u_ref[...].astype(F32)
        du_ref[...] = jnp.where(uv > 0.0, 2.0 * uv * _dot_nt(dy_ref[...], w_ref[...]), 0.0).astype(BF16)

    return pl.pallas_call(
        _after(body, deps), name="ffn_down_bwd", out_shape=SDS((T, DFF), BF16), grid=(T // tm, DFF // tn),
        in_specs=[DEP_SPEC] * len(deps) + [
            pl.BlockSpec((tm, D), lambda i, j: (i, 0)), pl.BlockSpec((tn, D), lambda i, j: (j, 0)),
            pl.BlockSpec((tm, tn), lambda i, j: (i, j))],
        out_specs=pl.BlockSpec((tm, tn), lambda i, j: (i, j)), compiler_params=_params(2))(*deps, dy16, w_down, u)


def _norm_bwd(xv, dz_in, g):
    r = lax.rsqrt(jnp.mean(xv * xv, axis=-1, keepdims=True) + EPS)
    dg = jnp.sum(xv * r * dz_in, axis=0, keepdims=True)
    dz = dz_in * g
    dx = r * dz - xv * (r * r * r) * jnp.mean(xv * dz, axis=-1, keepdims=True)
    return dx, dg


def _ffn_up_bwd(du, w_up, h1, dy, g, deps=()):
    tm, tk = 512, 1024
    per = (DFF // NSH) // tk
    nk = DFF // tk

    def body(du_ref, w_ref, h1_ref, dy_ref, g_ref, dh1_ref, dh16_ref, dg_ref, acc):
        i, k = pl.program_id(0), pl.program_id(1)

        @pl.when(k == 0)
        def _():
            acc[...] = jnp.zeros_like(acc)

        @pl.when((k == 0) & (i == 0))
        def _():
            dg_ref[...] = jnp.zeros_like(dg_ref)

        acc[...] += _dot_nt(du_ref[...], w_ref[...])

        @pl.when(k == nk - 1)
        def _():
            dx, dg = _norm_bwd(h1_ref[...], acc[...], g_ref[...])
            dh1 = dy_ref[...] + dx
            dh1_ref[...] = dh1
            dh16_ref[...] = dh1.astype(BF16)
            dg_ref[...] += dg

    row = pl.BlockSpec((tm, D), lambda i, k: (i, 0))
    vec = pl.BlockSpec((1, D), lambda i, k: (0, 0))
    return pl.pallas_call(
        _after(body, deps), name="ffn_up_bwd", out_shape=[SDS((T, D), F32), SDS((T, D), BF16), SDS((1, D), F32)],
        grid=(T // tm, nk),
        in_specs=[DEP_SPEC] * len(deps) + [
            pl.BlockSpec((tm, tk), lambda i, k: (i, k)),
            pl.BlockSpec((None, D, tk), lambda i, k: (k // per, 0, k % per)), row, row, vec],
        out_specs=[row, row, vec], scratch_shapes=[pltpu.VMEM((tm, D), F32)],
        compiler_params=_params(2))(*deps, du, w_up, h1, dy, g)


def _mix_bwd(dh16, w_out, oa, ob16, proj, b_gate, wpa, wpb):
    tm = 256

    def body(dh_ref, wo_ref, oa_ref, ob_ref, ga_ref, gb_ref, ba_ref, bb_ref, wpa_ref, wpb_ref,
             dya_ref, dyb_ref, dga_ref, dgb_ref, doa_ref, dob_ref, dba_ref, dbb_ref):
        @pl.when(pl.program_id(0) == 0)
        def _():
            dba_ref[...] = jnp.zeros_like(dba_ref)
            dbb_ref[...] = jnp.zeros_like(dbb_ref)

        oav, obv = oa_ref[...], ob_ref[...]
        doa = jnp.zeros((tm, 512), F32)
        dob = jnp.zeros((tm, 512), F32)
        for s in range(NSH):
            sl = slice(s * 512, (s + 1) * 512)
            dm = _dot_nt(dh_ref[...], wo_ref[sl, :])
            ga = _sigmoid(ga_ref[:, sl] + ba_ref[:, sl])
            gb = _sigmoid(gb_ref[:, sl] + bb_ref[:, sl])
            dya = (dm * ga).astype(BF16)
            dyb = (dm * gb).astype(BF16)
            dza = dm * _dot(oav, wpa_ref[s]) * ga * (1.0 - ga)
            dzb = dm * _dot(obv, wpb_ref[s]) * gb * (1.0 - gb)
            dya_ref[:, sl], dyb_ref[:, sl] = dya, dyb
            dga_ref[:, sl], dgb_ref[:, sl] = dza.astype(BF16), dzb.astype(BF16)
            dba_ref[:, sl] += jnp.sum(dza, axis=0, keepdims=True)
            dbb_ref[:, sl] += jnp.sum(dzb, axis=0, keepdims=True)
            doa += _dot_nt(dya, wpa_ref[s])
            dob += _dot_nt(dyb, wpb_ref[s])
        doa_ref[...], dob_ref[...] = doa, dob

    row = lambda w: pl.BlockSpec((tm, w), lambda i: (i, 0))
    vec = pl.BlockSpec((1, D), lambda i: (0, 0))
    wp = _resident((NSH, 512, 512), lambda i: (0, 0, 0))
    return pl.pallas_call(
        body, name="mix_bwd",
        out_shape=[SDS((T, D), BF16)] * 4 + [SDS((T, 512), F32)] * 2 + [SDS((1, D), F32)] * 2, grid=(T // tm,),
        in_specs=[row(D), _resident((D, D), lambda i: (0, 0)), row(512), row(512),
                  pl.BlockSpec((tm, D), lambda i: (i, 3)), pl.BlockSpec((tm, D), lambda i: (i, 4)),
                  pl.BlockSpec((1, D), lambda i: (0, 0)), pl.BlockSpec((1, D), lambda i: (0, 1)), wp, wp],
        out_specs=[row(D)] * 4 + [row(512)] * 2 + [vec] * 2,
        compiler_params=_params(1))(dh16, w_out, oa, ob16, proj, proj, b_gate, b_gate, wpa, wpb)


def _comb_bwd(doa, os, ws, deps=()):
    tm = 512

    def body(d_ref, o0, o1, o2, w0, w1, w2, cc_ref):
        prod = d_ref[...] * (w0[...] * o0[...] + w1[...] * o1[...] + w2[...] * o2[...])
        for h in range(4):
            sl = slice(h * HD, (h + 1) * HD)
            cc_ref[:, sl] = jnp.broadcast_to(jnp.sum(prod[:, sl], axis=-1, keepdims=True), (tm, HD))

    spec = pl.BlockSpec((tm, 512), lambda i: (i, 0))
    return pl.pallas_call(
        _after(body, deps), name="comb_bwd", out_shape=SDS((T, 512), F32), grid=(T // tm,),
        in_specs=[DEP_SPEC] * len(deps) + [spec] * 7, out_specs=spec,
        compiler_params=_params(1))(*deps, doa, *os, *ws)


def _attn_a_bwd(qkn, proj, doa, lse, w, cc, g):
    dil = DILS[g]
    m = T // dil
    nb = m // 128

    def body(q_ref, k_ref, v_ref, d_ref, l_ref, w_ref, c_ref, dqk_ref, dv_ref,
             qf, kf, qp, kp, vp, dp, lp, wsub, cp, dqb, dkp, dvp):
        qf[...] = q_ref[...].astype(F32)
        kf[...] = k_ref[...].astype(F32)
        for r in range(dil):
            sub = _residue_rows(r, m, dil)
            qp[...] = qf[sub, :].astype(BF16)
            _fill_padded(kp, kf[sub, :], m)
            _fill_padded(vp, v_ref[sub, :], m)
            dp[...] = d_ref[sub, :].astype(BF16)
            lp[...], wsub[...], cp[...] = l_ref[sub, :], w_ref[sub, :], c_ref[sub, :]
            dkp[...] = jnp.zeros_like(dkp)
            dvp[...] = jnp.zeros_like(dvp)

            def block(b, carry):
                q0 = pl.multiple_of(b * 128, 128)
                rows = pl.ds(q0, 128)
                win = pl.ds(q0, 256)
                qb, kw, vw = qp[rows, :], kp[win, :], vp[win, :]
                s = _dot_nt(qb, kw) * SCALE
                s = jnp.where(_band_mask(q0, m), s, NEG)
                wp = _wide(wsub[rows, :], 2) * jnp.exp(s - _wide(lp[rows, :], 2))
                dob = dp[rows, :]
                ds = (wp * (_dot_nt(dob, vw) - _wide(cp[rows, :], 2))).astype(BF16)
                dqb[rows, :] = _dot(ds, kw) * SCALE
                dkp[win, :] += _dot_tn(ds, qb) * SCALE
                dvp[win, :] += _dot_tn(wp.astype(BF16), dob)
                return carry

            lax.fori_loop(0, nb, block, 0, unroll=min(nb, 16))
            dqk_ref.at[0][sub, :] = dqb[...]
            dqk_ref.at[1][sub, :] = dkp[64:64 + m, :]
            dv_ref[sub, :] = dvp[64:64 + m, :]

    q_blk, k_blk, v_blk, blk = _head_blocks(g)
    sub16 = pltpu.VMEM((m, HD), BF16)
    sub32 = pltpu.VMEM((m, HD), F32)
    return pl.pallas_call(
        body, name=f"attn_a_bwd_{g}", out_shape=[SDS((2, T, 512), F32), SDS((T, 512), F32)], grid=(4,),
        in_specs=[q_blk, k_blk, v_blk, blk, blk, blk, blk],
        out_specs=[pl.BlockSpec((2, T, HD), lambda h: (0, 0, h)), blk],
        scratch_shapes=[pltpu.VMEM((T, HD), F32), pltpu.VMEM((T, HD), F32), sub16,
                        pltpu.VMEM((m + 128, HD), BF16), pltpu.VMEM((m + 128, HD), BF16), sub16,
                        sub32, sub32, sub32, sub32,
                        pltpu.VMEM((m + 128, HD), F32), pltpu.VMEM((m + 128, HD), F32)],
        compiler_params=_params(1))(qkn, qkn, proj, doa, lse, w, cc)


def _attn_b_bwd(qkn, proj, dob, ob, lse, bias, deps=()):
    def body(q_ref, k_ref, v_ref, d_ref, o_ref, l_ref, bias_ref, dqk_ref, dv_ref, drpb_ref, vb, dk_acc, dv_acc, a_acc):
        vb[...] = v_ref[...].astype(BF16)
        dk_acc[...] = jnp.zeros_like(dk_acc)
        dv_acc[...] = jnp.zeros_like(dv_acc)
        a_acc[...] = jnp.zeros_like(a_acc)

        def row(r, carry):
            start, off = _nbr_window(r)
            rows = pl.ds(pl.multiple_of(r * GRID_W, GRID_W), GRID_W)
            win = pl.ds(pl.multiple_of(start * GRID_W, GRID_W), 512)
            qr, kw, vw = q_ref[rows, :], k_ref[win, :], vb[win, :]
            s = _dot_nt(qr, kw) * SCALE + bias_ref[off]
            p = jnp.exp(s - _wide(l_ref[rows, :], 4))
            dov = d_ref[rows, :]
            delta = jnp.sum(dov * o_ref[rows, :], axis=-1, keepdims=True)
            do16 = dov.astype(BF16)
            ds = p * (_dot_nt(do16, vw) - delta)
            a_acc[off] += ds
            ds16 = ds.astype(BF16)
            dqk_ref[0, rows, :] = _dot(ds16, kw) * SCALE
            dk_acc[win, :] += _dot_tn(ds16, qr) * SCALE
            dv_acc[win, :] += _dot_tn(p.astype(BF16), do16)
            return carry

        lax.fori_loop(0, T // GRID_W, row, 0, unroll=16)
        dqk_ref[1] = dk_acc[...]
        dv_ref[...] = dv_acc[...]

        lane = lax.broadcasted_iota(jnp.int32, (16, HD), 1)
        rowi = lax.broadcasted_iota(jnp.int32, (16, HD), 0)
        low = (lane >= GRID_W - WIN_C) & (lane < GRID_W + WIN_C - 1)
        high = (lane >= HD - WIN_C) | (lane < WIN_C - 1)
        flip = (lax.broadcasted_iota(jnp.int32, (GRID_W, GRID_W), 0)
                + lax.broadcasted_iota(jnp.int32, (GRID_W, GRID_W), 1) == GRID_W - 1).astype(BF16)
        out = jnp.zeros((16, HD), F32)
        for d in range(14):
            acc = None
            for off in range(8):
                if 0 <= d - off <= 6 and (d - off) % 2 == 0:
                    jj = (d - off) // 2
                    piece = a_acc[off, :, jj * HD:(jj + 1) * HD]
                    acc = piece if acc is None else acc + piece
            hi = acc.astype(BF16)
            lo = (acc - hi.astype(F32)).astype(BF16)
            rev = _dot(flip, hi) + _dot(flip, lo)
            v = jnp.sum(pltpu.roll(rev, 0, 1, stride=1, stride_axis=0), axis=0, keepdims=True)
            v = jnp.broadcast_to(v, (16, HD))
            out = out + jnp.where((rowi == d) & low, v, 0.0)
            out = out + jnp.where(rowi == d + 1, pltpu.roll(jnp.where(high, v, 0.0), GRID_W, 1), 0.0)
        drpb_ref[...] = out

    blk = pl.BlockSpec((T, HD), lambda h: (0, h))
    return pl.pallas_call(
        _after(body, deps), name="attn_b_bwd",
        out_shape=[SDS((2, T, 512), F32), SDS((T, 512), F32), SDS((4, 16, HD), F32)], grid=(4,),
        in_specs=[DEP_SPEC] * len(deps) + [
            pl.BlockSpec((T, HD), lambda h: (0, NHA + h)),
            pl.BlockSpec((T, HD), lambda h: (0, NH + NHA + h)),
            pl.BlockSpec((T, HD), lambda h: (0, 2 * NH + NHA + h)), blk, blk, blk,
            pl.BlockSpec((None, 8, GRID_W, 512), lambda h: (h, 0, 0, 0))],
        out_specs=[pl.BlockSpec((2, T, HD), lambda h: (0, 0, h)), blk,
                   pl.BlockSpec((None, 16, HD), lambda h: (h, 0, 0))],
        scratch_shapes=[pltpu.VMEM((T, HD), BF16), pltpu.VMEM((T, HD), F32), pltpu.VMEM((T, HD), F32),
                        pltpu.VMEM((8, GRID_W, 512), F32)],
        compiler_params=_params(1))(*deps, qkn, qkn, proj, dob, ob, lse, bias)


def _qk_bwd(proj, nw, cos, sin, dqk_groups, dqk_b, dvs, dga, dgb):
    tm = 512

    def body(p_ref, w_ref, cos_ref, sin_ref, d0, d1, d2, d3, v0, v1, v2, v3, ga_ref, gb_ref, o_ref, dn_ref):
        j, i = pl.program_id(0), pl.program_id(1)

        @pl.when((j < 2) & (i == 0))
        def _():
            dn_ref[...] = jnp.zeros_like(dn_ref)

        @pl.when(j < 2)
        def _():
            cv, sv = cos_ref[...], sin_ref[...]
            srcs = (d0, d1, d2, d3)
            dna = jnp.zeros((1, HD), F32)
            dnb = jnp.zeros((1, HD), F32)
            for h in range(NH):
                sl = slice(h * HD, (h + 1) * HD)
                dz = srcs[h // 4][:, (h % 4) * HD:(h % 4 + 1) * HD]
                if h < NHA:
                    dz = dz * cv + pltpu.roll(dz * sv, 64, 1)
                dx, dg = _norm_bwd(p_ref[:, sl], dz, w_ref[:, sl])
                o_ref[:, sl] = dx.astype(BF16)
                if h < NHA:
                    dna += dg
                else:
                    dnb += dg
            dn_ref[0:1, :] += dna
            dn_ref[1:2, :] += dnb

        @pl.when(j == 2)
        def _():
            for s, v_ref in enumerate((v0, v1, v2, v3)):
                o_ref[:, s * 512:(s + 1) * 512] = v_ref[...].astype(BF16)

        @pl.when(j == 3)
        def _():
            o_ref[...] = ga_ref[...]

        @pl.when(j == 4)
        def _():
            o_ref[...] = gb_ref[...]

    def rows(used):
        return lambda j, i: (jnp.where(used(j), i, 0), 0)

    qk = lambda j: j < 2
    dspec = pl.BlockSpec((None, tm, 512), lambda j, i: (jnp.minimum(j, 1), jnp.where(j < 2, i, 0), 0))
    vspec = pl.BlockSpec((tm, 512), rows(lambda j: j == 2))
    return pl.pallas_call(
        body, name="qk_bwd", out_shape=[SDS((T, DIN), BF16), SDS((2, 8, HD), F32)], grid=(5, T // tm),
        in_specs=[pl.BlockSpec((tm, D), lambda j, i: (jnp.where(j < 2, i, 0), jnp.minimum(j, 1))),
                  pl.BlockSpec((None, 1, D), lambda j, i: (jnp.minimum(j, 1), 0, 0)),
                  pl.BlockSpec((tm, HD), rows(qk)), pl.BlockSpec((tm, HD), rows(qk)),
                  dspec, dspec, dspec, dspec, vspec, vspec, vspec, vspec,
                  pl.BlockSpec((tm, D), rows(lambda j: j == 3)), pl.BlockSpec((tm, D), rows(lambda j: j == 4))],
        out_specs=[pl.BlockSpec((tm, D), lambda j, i: (i, j)),
                   pl.BlockSpec((None, 8, HD), lambda j, i: (jnp.minimum(j, 1), 0, 0))],
        compiler_params=_params(2))(proj, nw, cos, sin, *dqk_groups, dqk_b, *dvs, dga, dgb)


def _in_proj_bwd(dproj, w_in, x, dh1, g, deps=()):
    tm, tk = 512, 1280
    per = (DIN // NSH) // tk
    nk = DIN // tk

    def body(dp_ref, w_ref, x_ref, dh_ref, g_ref, dx_ref, dg_ref, acc):
        i, k = pl.program_id(0), pl.program_id(1)

        @pl.when(k == 0)
        def _():
            acc[...] = jnp.zeros_like(acc)

        @pl.when((k == 0) & (i == 0))
        def _():
            dg_ref[...] = jnp.zeros_like(dg_ref)

        acc[...] += _dot_nt(dp_ref[...], w_ref[...])

        @pl.when(k == nk - 1)
        def _():
            dx, dg = _norm_bwd(x_ref[...], acc[...], g_ref[...])
            dx_ref[...] = dh_ref[...] + dx
            dg_ref[...] += dg

    row = pl.BlockSpec((tm, D), lambda i, k: (i, 0))
    vec = pl.BlockSpec((1, D), lambda i, k: (0, 0))
    return pl.pallas_call(
        _after(body, deps), name="in_proj_bwd", out_shape=[SDS((T, D), F32), SDS((1, D), F32)], grid=(T // tm, nk),
        in_specs=[DEP_SPEC] * len(deps) + [
            pl.BlockSpec((tm, tk), lambda i, k: (i, k)),
            pl.BlockSpec((None, D, tk), lambda i, k: (k // per, 0, k % per)), row, row, vec],
        out_specs=[row, vec], scratch_shapes=[pltpu.VMEM((tm, D), F32)],
        compiler_params=_params(2))(*deps, dproj, w_in, x, dh1, g)


def _grad_w_half(name, a, g, place, shard_rows, rows, cols, tc, for_sibling, theirs=None, deps=()):
    tr = rows // 2
    nj = cols // tc
    half = (lambda p: 1 - p[1]) if for_sibling else (lambda p: p[1])
    if shard_rows:
        a_map, g_map = (lambda s, j, p: (0, 2 * s + half(p))), (lambda s, j, p: (0, j))
    else:
        a_map, g_map = (lambda s, j, p: (0, half(p))), (lambda s, j, p: (0, s * nj + j))
    out = pl.BlockSpec((None, tr, tc), lambda s, j, p: (s, 0, j))
    n_in = 2 + (theirs is not None)

    def body(*refs):
        ins, o_ref = refs[-1 - n_in:-1], refs[-1]
        acc = _dot_tn(ins[0][...], ins[1][...])
        if theirs is not None:
            acc = acc + ins[2][...].astype(F32)
        o_ref[...] = acc.astype(BF16)

    extra = () if theirs is None else (theirs,)
    return pl.pallas_call(
        body, name=name, out_shape=SDS((NSH, tr, cols), BF16),
        grid_spec=pltpu.PrefetchScalarGridSpec(
            num_scalar_prefetch=1, grid=(NSH, nj),
            in_specs=[DEP_SPEC] * len(deps) + [pl.BlockSpec((T, tr), a_map), pl.BlockSpec((T, tc), g_map)]
            + [out] * len(extra),
            out_specs=out),
        compiler_params=_params(2))(place, *deps, a, g, *extra)


def _adamw(w, g, m, v):
    m = B1 * m + (1.0 - B1) * g
    v = B2 * v + (1.0 - B2) * (g * g)
    m_hat = m / (1.0 - B1 ** STEP)
    v_hat = v / (1.0 - B2 ** STEP)
    delta = -LR * (m_hat / (jnp.sqrt(v_hat) + AEPS) + WD * w)
    return delta, m, v


def _sum_halves(name, place, grads, theirs):
    _, rows, cols = theirs.shape
    tr = _row_tile(rows, cols, 1 << 20)

    def body(place_ref, a_ref, b_ref, o_ref):
        o_ref[...] = (a_ref[...].astype(F32) + b_ref[...].astype(F32)).astype(BF16)

    spec = pl.BlockSpec((None, tr, cols), lambda s, i, p: (s, i, 0))
    mine = spec if grads.ndim == 3 else pl.BlockSpec((None, None, tr, cols), lambda s, i, p: (s, p[1], i, 0))
    return pl.pallas_call(
        body, name=name, out_shape=SDS(theirs.shape, BF16),
        grid_spec=pltpu.PrefetchScalarGridSpec(
            num_scalar_prefetch=1, grid=(NSH, rows // tr), in_specs=[mine, spec], out_specs=spec),
        compiler_params=_params(2))(place, grads, theirs)


def _sum_landed(name, place, part, landed):
    _, rows, cols = part.shape
    tr = _row_tile(rows, cols, 1 << 20)

    def body(place_ref, p_ref, l_ref, o_ref):
        o_ref[...] = ((p_ref[...].astype(F32) + l_ref[0].astype(F32)) + l_ref[1].astype(F32)) + l_ref[2].astype(F32)

    return pl.pallas_call(
        body, name=name, out_shape=SDS((2, rows, cols), F32),
        grid_spec=pltpu.PrefetchScalarGridSpec(
            num_scalar_prefetch=1, grid=(rows // tr,),
            in_specs=[pl.BlockSpec((None, tr, cols), lambda i, p: (p[0], i, 0)),
                      pl.BlockSpec((3, tr, cols), lambda i, p: (0, i, 0))],
            out_specs=pl.BlockSpec((None, tr, cols), lambda i, p: (p[1], i, 0))),
        compiler_params=_params(1))(place, part, landed)


def _adam_shard(name, g, w, m, v):
    rows, cols = w.shape
    tr = _row_tile(rows, cols, 1 << 19)

    def body(g_ref, w_ref, m_ref, v_ref, go_ref, d_ref, nm_ref, nv_ref):
        g = g_ref[...]
        go_ref[...] = g
        d_ref[...], nm_ref[...], nv_ref[...] = _adamw(w_ref[...], g, m_ref[...], v_ref[...])

    spec = pl.BlockSpec((tr, cols), lambda i: (i, 0))
    return pl.pallas_call(
        body, name=name, out_shape=[SDS((rows, cols), F32)] * 4, grid=(rows // tr,),
        in_specs=[spec] * 4, out_specs=[spec] * 4, compiler_params=_params(1))(g, w, m, v)


def _adam_small(gathered, w, m, v):
    def body(g_ref, w_ref, m_ref, v_ref, go_ref, d_ref, nm_ref, nv_ref):
        g = g_ref[0:SMALL_ROWS, :]
        for dev in range(1, 8):
            g = g + g_ref[dev * SMALL_ROWS:(dev + 1) * SMALL_ROWS, :]
        go_ref[...] = g
        d_ref[...], nm_ref[...], nv_ref[...] = _adamw(w_ref[...], g, m_ref[...], v_ref[...])

    return pl.pallas_call(body, name="adam_small", out_shape=[SDS((SMALL_ROWS, HD), F32)] * 4)(gathered, w, m, v)


SMALL = (("norm_mix", (1, D)), ("b_gate", (1, 2 * D)), ("q_norm_a", (1, HD)), ("k_norm_a", (1, HD)),
         ("q_norm_b", (1, HD)), ("k_norm_b", (1, HD)), ("rpb_b", (1, 4, 15, 31)), ("norm_ffn", (1, D)))


def _pack_small(vals):
    pieces = []
    for (name, shape), val in zip(SMALL, vals):
        flat = val.reshape(-1)
        pad = (-flat.shape[0]) % HD
        pieces.append(jnp.pad(flat, (0, pad)).reshape(-1, HD))
    packed = jnp.concatenate(pieces, axis=0)
    return jnp.pad(packed, ((0, SMALL_ROWS - packed.shape[0]), (0, 0)))


def _unpack_small(packed):
    out, row = [], 0
    for name, shape in SMALL:
        size = int(np.prod(shape))
        nrows = -(-size // HD)
        out.append(packed[row:row + nrows].reshape(-1)[:size].reshape(shape))
        row += nrows
    return out


def kernel(x, norm_mix, w_in, b_gate, q_norm_a, k_norm_a, q_norm_b, k_norm_b, rpb_b, w_proj_a, w_proj_b, w_out, norm_ffn, w_up, w_down, loss_target, m_norm_mix, m_w_in, m_b_gate, m_q_norm_a, m_k_norm_a, m_q_norm_b, m_k_norm_b, m_rpb_b, m_w_proj_a, m_w_proj_b, m_w_out, m_norm_ffn, m_w_up, m_w_down, v_norm_mix, v_w_in, v_b_gate, v_q_norm_a, v_k_norm_a, v_q_norm_b, v_k_norm_b, v_rpb_b, v_w_proj_a, v_w_proj_b, v_w_out, v_norm_ffn, v_w_up, v_w_down):
    big_names = ("w_in", "w_proj_a", "w_proj_b", "w_out", "w_up", "w_down")
    big_w = [a[0] for a in (w_in, w_proj_a, w_proj_b, w_out, w_up, w_down)]
    big_m = [a[0] for a in (m_w_in, m_w_proj_a, m_w_proj_b, m_w_out, m_w_up, m_w_down)]
    big_v = [a[0] for a in (v_w_in, v_w_proj_a, v_w_proj_b, v_w_out, v_w_up, v_w_down)]
    x2, target = x[0], loss_target[0]

    place = jnp.stack([2 * lax.axis_index("x") + lax.axis_index("y"), lax.axis_index("c")]).astype(jnp.int32)
    groups = ((0,), (1, 2, 3), (4,), (5,))
    started = []
    for j, grp in enumerate(groups):
        deps = (started[0][4],) if j else ()
        placed = [_cast_into_place(big_w[i], "cast_" + big_names[i], place, deps) for i in grp]
        started.append(_gather_start(f"gather_start_{j}", placed))

    def whole(fulls):
        return [f.reshape(NSH, 2 * f.shape[2], f.shape[3]) for f in fulls]

    def forward_begin(j, after):
        send, recv, _, fulls, _ = started[j]
        fulls = _gather_wait(f"gather_wait_{j}", send, recv, fulls, after)
        send, recv, _, fulls, token = _forward_start(f"forward_start_{j}", fulls)
        return (send, recv, fulls), token

    def forward_end(j, state, after):
        return whole(_forward_wait(f"forward_wait_{j}", *state, after))

    def reduce_start(j, grads, theirs):
        parts = [_sum_halves(f"sum_halves_{j}_{i}", place, a, b) for i, (a, b) in enumerate(zip(grads, theirs))]
        send, recv, parts, lands, token = _reduce_start(f"reduce_start_{j}", parts)
        return (send, recv, parts, lands), token

    big_out = {}

    def share_begin(j, state, after):
        send, recv, parts, lands = state
        parts, lands = _reduce_wait(f"reduce_wait_{j}", send, recv, parts, lands, after)
        sums = [_sum_landed(f"sum_landed_{j}_{i}", place, p, l) for i, (p, l) in enumerate(zip(parts, lands))]
        send, recv, _, sums, token = _share_start(f"share_start_{j}", sums)
        return (send, recv, sums), token

    def share_end(j, state, after):
        for idx, g in zip(groups[j], _share_wait(f"share_wait_{j}", *state, after)):
            g = g.reshape(big_w[idx].shape)
            big_out[idx] = _adam_shard("adam_" + big_names[idx], g, big_w[idx], big_m[idx], big_v[idx])
        return big_out[groups[j][-1]][1]

    proj, xn = _norm_in_proj_own(x2, norm_mix, whole(started[0][3])[0], place)
    send, recv, _, win, _ = started[0]
    win = _gather_wait("gather_wait_0", send, recv, win, (proj, *[s[4] for s in started[1:]]))
    (win_f,) = whole(_gather_finish("gather_finish_0", win))
    proj = _in_proj_rest(xn, win_f, proj, place)
    cos, sin = _rope_tables()
    nw = jnp.stack([jnp.concatenate([jnp.tile(q_norm_a, (1, NHA)), jnp.tile(q_norm_b, (1, NH - NHA))], axis=1),
                    jnp.concatenate([jnp.tile(k_norm_a, (1, NHA)), jnp.tile(k_norm_b, (1, NH - NHA))], axis=1)])
    qkn = _qk_prep(proj, nw, cos, sin)
    fw1, token = forward_begin(1, (qkn,))
    fwd_a = [_attn_a_fwd(qkn, proj, g) for g in range(3)]
    os, ls = [f[0] for f in fwd_a], [f[1] for f in fwd_a]
    fw2, token = forward_begin(2, (os[2], token))
    ob, lse_b, bias = _attn_b_fwd(qkn, proj, _rpb_rows(rpb_b[0]))
    oa, w0, w1, w2 = _comb_fwd(os, ls)
    ws = [w0, w1, w2]
    wpa_f, wpb_f, wout_f = forward_end(1, fw1, (oa, token))
    wout_f = wout_f.reshape(D, D)
    mixed, ob16 = _mix_fwd(oa, ob, proj, b_gate, wpa_f, wpb_f)
    h1, hn = _out_proj_fwd(mixed, wout_f, x2, norm_ffn)
    (wup_f,) = forward_end(2, fw2, (hn,))
    usq, u = _ffn_up(hn, wup_f)
    fw3, token = forward_begin(3, (u,))
    own = _ffn_down_own(usq, whole(fw3[2])[0].reshape(DFF, D), place)
    (wdown_f,) = forward_end(3, fw3, (own, token))
    wdown_f = wdown_f.reshape(DFF, D)
    dy, dy16, loss_parts = _ffn_down_loss(usq, wdown_f, h1, target, own, place)
    loss = lax.psum(jnp.sum(loss_parts[:, 0, 0]), ("x", "y", "c"))

    def halves(name, a, g, shard_rows, rows, cols, j, behind, deps):
        sib = _grad_w_half(f"grad_{name}_for_sibling", a, g, place, shard_rows, rows, cols, 1024, True, deps=deps)
        send, recv, sib, lands, token = _exchange_start(f"exchange_start_{j}", [sib], sliced=False)
        out = behind(token)
        _, theirs = _exchange_wait(f"exchange_wait_{j}", send, recv, sib, lands, out[:1], sliced=False)
        part = _grad_w_half(f"grad_{name}_own", a, g, place, shard_rows, rows, cols, 1024, False, theirs=theirs[0])
        send, recv, parts, lands, token = _reduce_start(f"reduce_start_{j}", [part])
        return (send, recv, parts, lands), token, out

    red_down, token, (du,) = halves(
        "w_down", usq, dy16, True, DFF // NSH, D, 3, lambda t: (_ffn_down_bwd(dy16, wdown_f, u, deps=(t,)),), ())
    red_up, token, (dh16, dh1, d_norm_ffn) = halves(
        "w_up", hn, du, False, D, DFF // NSH, 2,
        lambda t: (lambda r: (r[1], r[0], r[2]))(_ffn_up_bwd(du, wup_f, h1, dy, norm_ffn, deps=(t,))), (token,))
    dya, dyb, dga, dgb, doa, dob, dba, dbb = _mix_bwd(dh16, wout_f, oa, ob16, proj, b_gate, wpa_f, wpb_f)
    mid = (("w_proj_a", oa, dya, False, 512, 512, 512), ("w_proj_b", ob16, dyb, False, 512, 512, 512),
           ("w_out", mixed, dh16, True, D // NSH, D, 1024))
    sibs = [_grad_w_half(f"grad_{n}_for_sibling", a, g, place, sr, r, c_, tc, True) for n, a, g, sr, r, c_, tc in mid]
    send, recv, sibs, lands, token_mid = _exchange_start("exchange_start_1", sibs, sliced=False)
    cc = _comb_bwd(doa, os, ws, deps=(token, token_mid))
    bwd_a = [_attn_a_bwd(qkn, proj, doa, ls[g], ws[g], cc, g) for g in range(3)]
    _, theirs = _exchange_wait("exchange_wait_1", send, recv, sibs, lands, (bwd_a[2][1],), sliced=False)
    parts = [_grad_w_half(f"grad_{n}_own", a, g, place, sr, r, c_, tc, False, theirs=t)
             for (n, a, g, sr, r, c_, tc), t in zip(mid, theirs)]
    send, recv, parts, lands, token = _reduce_start("reduce_start_1", parts)
    red_mid = (send, recv, parts, lands)
    dqk_b, dv_b, drpb_t = _attn_b_bwd(qkn, proj, dob, ob, lse_b, bias, deps=(token,))
    dproj, dn = _qk_bwd(proj, nw, cos, sin, [b[0] for b in bwd_a], dqk_b, [b[1] for b in bwd_a] + [dv_b], dga, dgb)
    g_in_theirs = _grad_w_half("grad_w_in_for_sibling", xn, dproj, place, False, D, DIN // NSH, 1280, True)
    send, recv, g_in_theirs, lands, token = _exchange_start("exchange_start_0", [g_in_theirs], sliced=False)
    g_in_mine = _grad_w_half("grad_w_in_own", xn, dproj, place, False, D, DIN // NSH, 1280, False, deps=(token,))
    _, theirs = _exchange_wait("exchange_wait_0", send, recv, g_in_theirs, lands, (g_in_mine,), sliced=False)
    red_in, token = reduce_start(0, [g_in_mine], theirs)
    grad_x, d_norm_mix = _in_proj_bwd(dproj, win_f, x2, dh1, norm_mix, deps=(token,))

    sh_down, token = share_begin(3, red_down, (grad_x,))
    sh_up, token = share_begin(2, red_up, (token,))
    done = share_end(3, sh_down, (token,))
    sh_mid, token = share_begin(1, red_mid, (done,))
    done = share_end(2, sh_up, (token,))
    sh_in, token = share_begin(0, red_in, (done,))
    done = share_end(1, sh_mid, (token,))
    done = share_end(0, sh_in, (done,))

    d_rpb = drpb_t[:, :15, GRID_W - WIN_C:GRID_W + WIN_C - 1]
    small_g = [d_norm_mix, jnp.concatenate([dba, dbb], axis=1), dn[0, 0], dn[1, 0], dn[0, 1], dn[1, 1], d_rpb, d_norm_ffn]
    gathered_small = _allgather_small(_pack_small(small_g), done)
    small_w = (norm_mix, b_gate, q_norm_a, k_norm_a, q_norm_b, k_norm_b, rpb_b, norm_ffn)
    small_m = (m_norm_mix, m_b_gate, m_q_norm_a, m_k_norm_a, m_q_norm_b, m_k_norm_b, m_rpb_b, m_norm_ffn)
    small_v = (v_norm_mix, v_b_gate, v_q_norm_a, v_k_norm_a, v_q_norm_b, v_k_norm_b, v_rpb_b, v_norm_ffn)
    small_out = [_unpack_small(p) for p in
                 _adam_small(gathered_small, _pack_small(small_w), _pack_small(small_m), _pack_small(small_v))]

    order = ("norm_mix", "w_in", "b_gate", "q_norm_a", "k_norm_a", "q_norm_b", "k_norm_b", "rpb_b",
             "w_proj_a", "w_proj_b", "w_out", "norm_ffn", "w_up", "w_down")
    small_idx = {name: i for i, (name, _) in enumerate(SMALL)}
    outs = []
    for kind in range(4):
        for name in order:
            if name in small_idx:
                outs.append(small_out[kind][small_idx[name]])
            else:
                outs.append(big_out[big_names.index(name)][kind][None])
    return (loss, grad_x[None], *outs)
```
